```python
import math
import jax, jax.numpy as jnp
from jax import lax
import numpy as np

D_MODEL = 1024
BATCH = 8
SEQ = 4096
DEPTH = 1

CHUNK = 64
Q_BLOCK = 128
SB_HEAD_DIM = 64
SB_HEADS = D_MODEL // 128
SB_WIDTH = SB_HEADS * SB_HEAD_DIM
SSM_WIDTH = D_MODEL // 2
SSM_GROUP = 16
SSM_GROUPS = SSM_WIDTH // SSM_GROUP
SSM_STATE = 64
N_MEM = 256
XA_HEADS = 4
XA_HEAD_DIM = D_MODEL // XA_HEADS
D_FF = 11 * D_MODEL // 4
CONV_WIDTH = 3
RMS_EPS = 1e-6
IN_WIDTH = 3 * SB_WIDTH + SSM_WIDTH + 2 * D_MODEL

kernel_name = "hybrid_stickbreak_s5_memxattn_convffn"


def rms_norm(x, gain):
    xf = x.astype(jnp.float32)
    y = xf * lax.rsqrt(jnp.mean(xf * xf, axis=-1, keepdims=True) + RMS_EPS)
    return (y * gain.astype(jnp.float32)).astype(x.dtype)


def stick_breaking_attention(q, k, v):
    b, s, _ = q.shape
    n_blk = s // Q_BLOCK
    qh = q.reshape(b, n_blk, Q_BLOCK, SB_HEADS, SB_HEAD_DIM).transpose(1, 0, 3, 2, 4)
    kh = k.reshape(b, s, SB_HEADS, SB_HEAD_DIM).transpose(0, 2, 1, 3)
    vh = v.reshape(b, s, SB_HEADS, SB_HEAD_DIM).transpose(0, 2, 1, 3)
    key_pos = jnp.arange(s, dtype=jnp.int32)
    scale = SB_HEAD_DIM ** -0.5

    def one_block(args):
        q_blk, blk = args
        z = jnp.einsum('bhqd,bhkd->bhqk', q_blk, kh).astype(jnp.float32) * scale
        q_pos = blk * Q_BLOCK + jnp.arange(Q_BLOCK, dtype=jnp.int32)
        causal = key_pos[None, :] < q_pos[:, None]
        log_fail = jnp.where(causal, jax.nn.log_sigmoid(-z), 0.0)
        log_remain = lax.cumsum(log_fail, axis=3, reverse=True) - log_fail
        w = jnp.where(causal, jnp.exp(jax.nn.log_sigmoid(z) + log_remain), 0.0)
        return jnp.einsum('bhqk,bhkd->bhqd', w.astype(vh.dtype), vh)

    out = lax.map(one_block, (qh, jnp.arange(n_blk, dtype=jnp.int32)))
    return out.transpose(1, 0, 3, 2, 4).reshape(b, s, SB_WIDTH)


def _linear_recurrence_combine(e_i, e_j):
    a_i, b_i = e_i
    a_j, b_j = e_j
    return a_j * a_i, a_j * b_i + b_j


def s5_branch(u, a_re, a_im, log_dt, b_re, b_im, c_re, c_im, d_skip, w_glu, b_glu):
    b, s, _ = u.shape
    f32 = jnp.float32
    ug = u.astype(f32).reshape(b, s, SSM_GROUPS, SSM_GROUP)
    lam = lax.complex(a_re.astype(f32), a_im.astype(f32))
    dt = jnp.exp(log_dt.astype(f32))[:, None]
    lam_bar = jnp.exp(lam * dt)
    b_mat = lax.complex(b_re.astype(f32), b_im.astype(f32))
    b_bar = ((lam_bar - 1.0) / lam)[:, :, None] * b_mat
    bu = jnp.einsum('gpc,bsgc->bsgp', b_bar, ug.astype(jnp.complex64))
    decay = jnp.broadcast_to(lam_bar, (1, s, SSM_GROUPS, SSM_STATE))
    _, states = lax.associative_scan(_linear_recurrence_combine, (decay, bu), axis=1)
    c_mat = lax.complex(c_re.astype(f32), c_im.astype(f32))
    y = jnp.einsum('gcp,bsgp->bsgc', c_mat, states).real
    y = y + d_skip.astype(f32).reshape(SSM_GROUPS, SSM_GROUP) * ug
    y = jax.nn.gelu(y.reshape(b, s, SSM_WIDTH)).astype(u.dtype)
    return y * jax.nn.sigmoid(y @ w_glu + b_glu)


def memory_cross_attention(h, mem_n, wq, wk, wv, wo):
    b, s, _ = h.shape
    m = mem_n.shape[1]
    q = (h @ wq).reshape(b, s, XA_HEADS, XA_HEAD_DIM)
    k = (mem_n @ wk).reshape(b, m, XA_HEADS, XA_HEAD_DIM)
    v = (mem_n @ wv).reshape(b, m, XA_HEADS, XA_HEAD_DIM)
    scores = jnp.einsum('bqhd,bmhd->bhqm', q, k).astype(jnp.float32) * (XA_HEAD_DIM ** -0.5)
    p = jax.nn.softmax(scores, axis=-1).astype(v.dtype)
    o = jnp.einsum('bhqm,bmhd->bqhd', p, v).reshape(b, s, D_MODEL)
    return o @ wo


def conv_ffn(h, w_up, conv_w, conv_b, w_down):
    up = h @ w_up
    up = lax.conv_general_dilated(
        up, conv_w[:, None, :].astype(up.dtype), window_strides=(1,),
        padding=[(CONV_WIDTH - 1, 0)], dimension_numbers=('NWC', 'WIO', 'NWC'),
        feature_group_count=2 * D_FF) + conv_b
    gate, val = jnp.split(up, 2, axis=-1)
    return (jax.nn.gelu(gate) * val) @ w_down


def _normal(key, shape, scale):
    return jax.random.normal(key, shape, jnp.float32) * scale


def _fwd_setup_inputs(seed: int = 0) -> dict:
    key = jax.random.key(seed)
    ks = jax.random.split(key, 32)
    L, D, G, P, C = DEPTH, D_MODEL, SSM_GROUPS, SSM_STATE, SSM_GROUP
    gain = lambda k: 1.0 + _normal(k, (L, D), 0.05)
    return {
        "x": _normal(ks[0], (BATCH, SEQ, D), 1.0),
        "mem": _normal(ks[1], (BATCH, N_MEM, D), 1.0),
        "norm_mix_pre": gain(ks[2]),
        "norm_mix_post": gain(ks[3]),
        "w_in": _normal(ks[4], (L, D, IN_WIDTH), D ** -0.5),
        "b_gate": _normal(ks[5], (L, 2 * D), 0.01),
        "ssm_a_re": -0.5 + _normal(ks[6], (L, G, P), 0.01),
        "ssm_a_im": jnp.pi * jnp.arange(P, dtype=jnp.float32)[None, None, :] + _normal(ks[7], (L, G, P), 0.01),
        "ssm_log_dt": jax.random.uniform(ks[8], (L, G), jnp.float32, math.log(1e-3), math.log(1e-1)),
        "ssm_b_re": _normal(ks[9], (L, G, P, C), (2 * C) ** -0.5),
        "ssm_b_im": _normal(ks[10], (L, G, P, C), (2 * C) ** -0.5),
        "ssm_c_re": _normal(ks[11], (L, G, C, P), (2 * P) ** -0.5),
        "ssm_c_im": _normal(ks[12], (L, G, C, P), (2 * P) ** -0.5),
        "ssm_d": _normal(ks[13], (L, SSM_WIDTH), 1.0),
        "ssm_w_glu": _normal(ks[14], (L, SSM_WIDTH, SSM_WIDTH), SSM_WIDTH ** -0.5),
        "ssm_b_glu": _normal(ks[15], (L, SSM_WIDTH), 0.01),
        "w_branch_attn": _normal(ks[16], (L, SB_WIDTH, D), SB_WIDTH ** -0.5),
        "w_branch_ssm": _normal(ks[17], (L, SSM_WIDTH, D), SSM_WIDTH ** -0.5),
        "w_out": _normal(ks[18], (L, D, D), D ** -0.5),
        "norm_xa_pre": gain(ks[19]),
        "norm_xa_post": gain(ks[20]),
        "norm_mem": gain(ks[21]),
        "xa_wq": _normal(ks[22], (L, D, D), D ** -0.5),
        "xa_wk": _normal(ks[23], (L, D, D), D ** -0.5),
        "xa_wv": _normal(ks[24], (L, D, D), D ** -0.5),
        "xa_wo": _normal(ks[25], (L, D, D), D ** -0.5),
        "norm_ffn_pre": gain(ks[26]),
        "norm_ffn_post": gain(ks[27]),
        "ffn_w_up": _normal(ks[28], (L, D, 2 * D_FF), D ** -0.5),
        "ffn_conv_w": _normal(ks[29], (L, CONV_WIDTH, 2 * D_FF), CONV_WIDTH ** -0.5),
        "ffn_conv_b": _normal(ks[30], (L, 2 * D_FF), 0.01),
        "ffn_w_down": _normal(ks[31], (L, D_FF, D), D_FF ** -0.5),
    }


def _fwd_reference(x, mem, norm_mix_pre, norm_mix_post, w_in, b_gate,
              ssm_a_re, ssm_a_im, ssm_log_dt, ssm_b_re, ssm_b_im, ssm_c_re, ssm_c_im,
              ssm_d, ssm_w_glu, ssm_b_glu, w_branch_attn, w_branch_ssm, w_out,
              norm_xa_pre, norm_xa_post, norm_mem, xa_wq, xa_wk, xa_wv, xa_wo,
              norm_ffn_pre, norm_ffn_post, ffn_w_up, ffn_conv_w, ffn_conv_b, ffn_w_down):
    splits = (SB_WIDTH, 2 * SB_WIDTH, 3 * SB_WIDTH, 3 * SB_WIDTH + SSM_WIDTH)
    for l in range(DEPTH):
        h = rms_norm(x, norm_mix_pre[l])
        proj = h @ w_in[l]
        q, k, v, u, gate_logits = jnp.split(proj, splits, axis=-1)
        gate_attn, gate_ssm = jnp.split(jax.nn.sigmoid(gate_logits + b_gate[l]), 2, axis=-1)
        o_attn = stick_breaking_attention(q, k, v)
        o_ssm = s5_branch(u, ssm_a_re[l], ssm_a_im[l], ssm_log_dt[l], ssm_b_re[l], ssm_b_im[l],
                          ssm_c_re[l], ssm_c_im[l], ssm_d[l], ssm_w_glu[l], ssm_b_glu[l])
        merged = gate_attn * (o_attn @ w_branch_attn[l]) + gate_ssm * (o_ssm @ w_branch_ssm[l])
        x = x + rms_norm(merged @ w_out[l], norm_mix_post[l])
        h = rms_norm(x, norm_xa_pre[l])
        mem_n = rms_norm(mem, norm_mem[l])
        xa = memory_cross_attention(h, mem_n, xa_wq[l], xa_wk[l], xa_wv[l], xa_wo[l])
        x = x + rms_norm(xa, norm_xa_post[l])
        h = rms_norm(x, norm_ffn_pre[l])
        f = conv_ffn(h, ffn_w_up[l], ffn_conv_w[l], ffn_conv_b[l], ffn_w_down[l])
        x = x + rms_norm(f, norm_ffn_post[l])
    return x


import jax as _jax
import jax.numpy as _jnp

TWIN_FORMAT = 'train_step'
FWD_PARAMS = ['x', 'mem', 'norm_mix_pre', 'norm_mix_post', 'w_in', 'b_gate', 'ssm_a_re', 'ssm_a_im', 'ssm_log_dt', 'ssm_b_re', 'ssm_b_im', 'ssm_c_re', 'ssm_c_im', 'ssm_d', 'ssm_w_glu', 'ssm_b_glu', 'w_branch_attn', 'w_branch_ssm', 'w_out', 'norm_xa_pre', 'norm_xa_post', 'norm_mem', 'xa_wq', 'xa_wk', 'xa_wv', 'xa_wo', 'norm_ffn_pre', 'norm_ffn_post', 'ffn_w_up', 'ffn_conv_w', 'ffn_conv_b', 'ffn_w_down']
TWIN_WEIGHTS = ['norm_mix_pre', 'norm_mix_post', 'w_in', 'b_gate', 'ssm_a_re', 'ssm_a_im', 'ssm_log_dt', 'ssm_b_re', 'ssm_b_im', 'ssm_c_re', 'ssm_c_im', 'ssm_d', 'ssm_w_glu', 'ssm_b_glu', 'w_branch_attn', 'w_branch_ssm', 'w_out', 'norm_xa_pre', 'norm_xa_post', 'norm_mem', 'xa_wq', 'xa_wk', 'xa_wv', 'xa_wo', 'norm_ffn_pre', 'norm_ffn_post', 'ffn_w_up', 'ffn_conv_w', 'ffn_conv_b', 'ffn_w_down']
TWIN_DIFF_INPUT = 'x'
TWIN_INPUTS = ['x', 'mem', 'norm_mix_pre', 'norm_mix_post', 'w_in', 'b_gate', 'ssm_a_re', 'ssm_a_im', 'ssm_log_dt', 'ssm_b_re', 'ssm_b_im', 'ssm_c_re', 'ssm_c_im', 'ssm_d', 'ssm_w_glu', 'ssm_b_glu', 'w_branch_attn', 'w_branch_ssm', 'w_out', 'norm_xa_pre', 'norm_xa_post', 'norm_mem', 'xa_wq', 'xa_wk', 'xa_wv', 'xa_wo', 'norm_ffn_pre', 'norm_ffn_post', 'ffn_w_up', 'ffn_conv_w', 'ffn_conv_b', 'ffn_w_down', 'loss_target', 'm_norm_mix_pre', 'm_norm_mix_post', 'm_w_in', 'm_b_gate', 'm_ssm_a_re', 'm_ssm_a_im', 'm_ssm_log_dt', 'm_ssm_b_re', 'm_ssm_b_im', 'm_ssm_c_re', 'm_ssm_c_im', 'm_ssm_d', 'm_ssm_w_glu', 'm_ssm_b_glu', 'm_w_branch_attn', 'm_w_branch_ssm', 'm_w_out', 'm_norm_xa_pre', 'm_norm_xa_post', 'm_norm_mem', 'm_xa_wq', 'm_xa_wk', 'm_xa_wv', 'm_xa_wo', 'm_norm_ffn_pre', 'm_norm_ffn_post', 'm_ffn_w_up', 'm_ffn_conv_w', 'm_ffn_conv_b', 'm_ffn_w_down', 'v_norm_mix_pre', 'v_norm_mix_post', 'v_w_in', 'v_b_gate', 'v_ssm_a_re', 'v_ssm_a_im', 'v_ssm_log_dt', 'v_ssm_b_re', 'v_ssm_b_im', 'v_ssm_c_re', 'v_ssm_c_im', 'v_ssm_d', 'v_ssm_w_glu', 'v_ssm_b_glu', 'v_w_branch_attn', 'v_w_branch_ssm', 'v_w_out', 'v_norm_xa_pre', 'v_norm_xa_post', 'v_norm_mem', 'v_xa_wq', 'v_xa_wk', 'v_xa_wv', 'v_xa_wo', 'v_norm_ffn_pre', 'v_norm_ffn_post', 'v_ffn_w_up', 'v_ffn_conv_w', 'v_ffn_conv_b', 'v_ffn_w_down']
TWIN_OUTPUTS = ['loss', 'grad_x', 'grad_norm_mix_pre', 'grad_norm_mix_post', 'grad_w_in', 'grad_b_gate', 'grad_ssm_a_re', 'grad_ssm_a_im', 'grad_ssm_log_dt', 'grad_ssm_b_re', 'grad_ssm_b_im', 'grad_ssm_c_re', 'grad_ssm_c_im', 'grad_ssm_d', 'grad_ssm_w_glu', 'grad_ssm_b_glu', 'grad_w_branch_attn', 'grad_w_branch_ssm', 'grad_w_out', 'grad_norm_xa_pre', 'grad_norm_xa_post', 'grad_norm_mem', 'grad_xa_wq', 'grad_xa_wk', 'grad_xa_wv', 'grad_xa_wo', 'grad_norm_ffn_pre', 'grad_norm_ffn_post', 'grad_ffn_w_up', 'grad_ffn_conv_w', 'grad_ffn_conv_b', 'grad_ffn_w_down', 'delta_norm_mix_pre', 'delta_norm_mix_post', 'delta_w_in', 'delta_b_gate', 'delta_ssm_a_re', 'delta_ssm_a_im', 'delta_ssm_log_dt', 'delta_ssm_b_re', 'delta_ssm_b_im', 'delta_ssm_c_re', 'delta_ssm_c_im', 'delta_ssm_d', 'delta_ssm_w_glu', 'delta_ssm_b_glu', 'delta_w_branch_attn', 'delta_w_branch_ssm', 'delta_w_out', 'delta_norm_xa_pre', 'delta_norm_xa_post', 'delta_norm_mem', 'delta_xa_wq', 'delta_xa_wk', 'delta_xa_wv', 'delta_xa_wo', 'delta_norm_ffn_pre', 'delta_norm_ffn_post', 'delta_ffn_w_up', 'delta_ffn_conv_w', 'delta_ffn_conv_b', 'delta_ffn_w_down', 'new_m_norm_mix_pre', 'new_m_norm_mix_post', 'new_m_w_in', 'new_m_b_gate', 'new_m_ssm_a_re', 'new_m_ssm_a_im', 'new_m_ssm_log_dt', 'new_m_ssm_b_re', 'new_m_ssm_b_im', 'new_m_ssm_c_re', 'new_m_ssm_c_im', 'new_m_ssm_d', 'new_m_ssm_w_glu', 'new_m_ssm_b_glu', 'new_m_w_branch_attn', 'new_m_w_branch_ssm', 'new_m_w_out', 'new_m_norm_xa_pre', 'new_m_norm_xa_post', 'new_m_norm_mem', 'new_m_xa_wq', 'new_m_xa_wk', 'new_m_xa_wv', 'new_m_xa_wo', 'new_m_norm_ffn_pre', 'new_m_norm_ffn_post', 'new_m_ffn_w_up', 'new_m_ffn_conv_w', 'new_m_ffn_conv_b', 'new_m_ffn_w_down', 'new_v_norm_mix_pre', 'new_v_norm_mix_post', 'new_v_w_in', 'new_v_b_gate', 'new_v_ssm_a_re', 'new_v_ssm_a_im', 'new_v_ssm_log_dt', 'new_v_ssm_b_re', 'new_v_ssm_b_im', 'new_v_ssm_c_re', 'new_v_ssm_c_im', 'new_v_ssm_d', 'new_v_ssm_w_glu', 'new_v_ssm_b_glu', 'new_v_w_branch_attn', 'new_v_w_branch_ssm', 'new_v_w_out', 'new_v_norm_xa_pre', 'new_v_norm_xa_post', 'new_v_norm_mem', 'new_v_xa_wq', 'new_v_xa_wk', 'new_v_xa_wv', 'new_v_xa_wo', 'new_v_norm_ffn_pre', 'new_v_norm_ffn_post', 'new_v_ffn_w_up', 'new_v_ffn_conv_w', 'new_v_ffn_conv_b', 'new_v_ffn_w_down']
TWIN_LEAF_KINDS = {'loss': 'loss', 'grad_x': 'grad_x', 'grad_norm_mix_pre': 'grad_w', 'grad_norm_mix_post': 'grad_w', 'grad_w_in': 'grad_w', 'grad_b_gate': 'grad_w', 'grad_ssm_a_re': 'grad_w', 'grad_ssm_a_im': 'grad_w', 'grad_ssm_log_dt': 'grad_w', 'grad_ssm_b_re': 'grad_w', 'grad_ssm_b_im': 'grad_w', 'grad_ssm_c_re': 'grad_w', 'grad_ssm_c_im': 'grad_w', 'grad_ssm_d': 'grad_w', 'grad_ssm_w_glu': 'grad_w', 'grad_ssm_b_glu': 'grad_w', 'grad_w_branch_attn': 'grad_w', 'grad_w_branch_ssm': 'grad_w', 'grad_w_out': 'grad_w', 'grad_norm_xa_pre': 'grad_w', 'grad_norm_xa_post': 'grad_w', 'grad_norm_mem': 'grad_w', 'grad_xa_wq': 'grad_w', 'grad_xa_wk': 'grad_w', 'grad_xa_wv': 'grad_w', 'grad_xa_wo': 'grad_w', 'grad_norm_ffn_pre': 'grad_w', 'grad_norm_ffn_post': 'grad_w', 'grad_ffn_w_up': 'grad_w', 'grad_ffn_conv_w': 'grad_w', 'grad_ffn_conv_b': 'grad_w', 'grad_ffn_w_down': 'grad_w', 'delta_norm_mix_pre': 'delta_w', 'delta_norm_mix_post': 'delta_w', 'delta_w_in': 'delta_w', 'delta_b_gate': 'delta_w', 'delta_ssm_a_re': 'delta_w', 'delta_ssm_a_im': 'delta_w', 'delta_ssm_log_dt': 'delta_w', 'delta_ssm_b_re': 'delta_w', 'delta_ssm_b_im': 'delta_w', 'delta_ssm_c_re': 'delta_w', 'delta_ssm_c_im': 'delta_w', 'delta_ssm_d': 'delta_w', 'delta_ssm_w_glu': 'delta_w', 'delta_ssm_b_glu': 'delta_w', 'delta_w_branch_attn': 'delta_w', 'delta_w_branch_ssm': 'delta_w', 'delta_w_out': 'delta_w', 'delta_norm_xa_pre': 'delta_w', 'delta_norm_xa_post': 'delta_w', 'delta_norm_mem': 'delta_w', 'delta_xa_wq': 'delta_w', 'delta_xa_wk': 'delta_w', 'delta_xa_wv': 'delta_w', 'delta_xa_wo': 'delta_w', 'delta_norm_ffn_pre': 'delta_w', 'delta_norm_ffn_post': 'delta_w', 'delta_ffn_w_up': 'delta_w', 'delta_ffn_conv_w': 'delta_w', 'delta_ffn_conv_b': 'delta_w', 'delta_ffn_w_down': 'delta_w', 'new_m_norm_mix_pre': 'new_m', 'new_m_norm_mix_post': 'new_m', 'new_m_w_in': 'new_m', 'new_m_b_gate': 'new_m', 'new_m_ssm_a_re': 'new_m', 'new_m_ssm_a_im': 'new_m', 'new_m_ssm_log_dt': 'new_m', 'new_m_ssm_b_re': 'new_m', 'new_m_ssm_b_im': 'new_m', 'new_m_ssm_c_re': 'new_m', 'new_m_ssm_c_im': 'new_m', 'new_m_ssm_d': 'new_m', 'new_m_ssm_w_glu': 'new_m', 'new_m_ssm_b_glu': 'new_m', 'new_m_w_branch_attn': 'new_m', 'new_m_w_branch_ssm': 'new_m', 'new_m_w_out': 'new_m', 'new_m_norm_xa_pre': 'new_m', 'new_m_norm_xa_post': 'new_m', 'new_m_norm_mem': 'new_m', 'new_m_xa_wq': 'new_m', 'new_m_xa_wk': 'new_m', 'new_m_xa_wv': 'new_m', 'new_m_xa_wo': 'new_m', 'new_m_norm_ffn_pre': 'new_m', 'new_m_norm_ffn_post': 'new_m', 'new_m_ffn_w_up': 'new_m', 'new_m_ffn_conv_w': 'new_m', 'new_m_ffn_conv_b': 'new_m', 'new_m_ffn_w_down': 'new_m', 'new_v_norm_mix_pre': 'new_v', 'new_v_norm_mix_post': 'new_v', 'new_v_w_in': 'new_v', 'new_v_b_gate': 'new_v', 'new_v_ssm_a_re': 'new_v', 'new_v_ssm_a_im': 'new_v', 'new_v_ssm_log_dt': 'new_v', 'new_v_ssm_b_re': 'new_v', 'new_v_ssm_b_im': 'new_v', 'new_v_ssm_c_re': 'new_v', 'new_v_ssm_c_im': 'new_v', 'new_v_ssm_d': 'new_v', 'new_v_ssm_w_glu': 'new_v', 'new_v_ssm_b_glu': 'new_v', 'new_v_w_branch_attn': 'new_v', 'new_v_w_branch_ssm': 'new_v', 'new_v_w_out': 'new_v', 'new_v_norm_xa_pre': 'new_v', 'new_v_norm_xa_post': 'new_v', 'new_v_norm_mem': 'new_v', 'new_v_xa_wq': 'new_v', 'new_v_xa_wk': 'new_v', 'new_v_xa_wv': 'new_v', 'new_v_xa_wo': 'new_v', 'new_v_norm_ffn_pre': 'new_v', 'new_v_norm_ffn_post': 'new_v', 'new_v_ffn_w_up': 'new_v', 'new_v_ffn_conv_w': 'new_v', 'new_v_ffn_conv_b': 'new_v', 'new_v_ffn_w_down': 'new_v'}


def _forward(args):
    return _fwd_reference(*[args[k] for k in FWD_PARAMS])


def _output_shape():
    out = _jax.eval_shape(lambda: _forward(_fwd_setup_inputs(0)))
    return out.shape, out.dtype

N_MICROBATCH = 1
ADAM_LR = 0.001
ADAM_B1 = 0.9
ADAM_B2 = 0.999
ADAM_EPS = 1e-08
ADAM_WD = 0.01
ADAM_STEP = 10
PER_EXAMPLE_BATCH_AXIS = {'x': 0, 'mem': 0, 'loss_target': 0}
SHARED_INPUTS = []
_WEIGHT_DTYPES = {'norm_mix_pre': _jnp.float32, 'norm_mix_post': _jnp.float32, 'w_in': _jnp.float32, 'b_gate': _jnp.float32, 'ssm_a_re': _jnp.float32, 'ssm_a_im': _jnp.float32, 'ssm_log_dt': _jnp.float32, 'ssm_b_re': _jnp.float32, 'ssm_b_im': _jnp.float32, 'ssm_c_re': _jnp.float32, 'ssm_c_im': _jnp.float32, 'ssm_d': _jnp.float32, 'ssm_w_glu': _jnp.float32, 'ssm_b_glu': _jnp.float32, 'w_branch_attn': _jnp.float32, 'w_branch_ssm': _jnp.float32, 'w_out': _jnp.float32, 'norm_xa_pre': _jnp.float32, 'norm_xa_post': _jnp.float32, 'norm_mem': _jnp.float32, 'xa_wq': _jnp.float32, 'xa_wk': _jnp.float32, 'xa_wv': _jnp.float32, 'xa_wo': _jnp.float32, 'norm_ffn_pre': _jnp.float32, 'norm_ffn_post': _jnp.float32, 'ffn_w_up': _jnp.float32, 'ffn_conv_w': _jnp.float32, 'ffn_conv_b': _jnp.float32, 'ffn_w_down': _jnp.float32}
MOMENT_SCALE = {'norm_mix_pre': 8.053579e-01, 'norm_mix_post': 3.205767e+01, 'w_in': 3.845301e-01, 'b_gate': 4.943792e-01, 'ssm_a_re': 2.378064e-02, 'ssm_a_im': 2.532344e-02, 'ssm_log_dt': 7.082164e+00, 'ssm_b_re': 1.523543e-02, 'ssm_b_im': 1.490682e-02, 'ssm_c_re': 2.758796e-02, 'ssm_c_im': 2.804470e-02, 'ssm_d': 2.971528e+00, 'ssm_w_glu': 3.788308e-01, 'ssm_b_glu': 1.294454e+00, 'w_branch_attn': 6.074037e-01, 'w_branch_ssm': 2.069296e+00, 'w_out': 1.811099e+00, 'norm_xa_pre': 8.156525e-01, 'norm_xa_post': 3.311337e+01, 'norm_mem': 2.941641e+00, 'xa_wq': 7.702871e-01, 'xa_wk': 7.730152e-01, 'xa_wv': 3.162130e+00, 'xa_wo': 3.103415e+00, 'norm_ffn_pre': 1.726329e+00, 'norm_ffn_post': 3.199151e+01, 'ffn_w_up': 7.917716e-01, 'ffn_conv_w': 9.483667e-01, 'ffn_conv_b': 2.691663e+00, 'ffn_w_down': 1.716121e+00}


def _to_microbatches(a, axis):
    t = _jnp.moveaxis(a, axis, 0)
    t = t.reshape((N_MICROBATCH, t.shape[0] // N_MICROBATCH) + t.shape[1:])
    return _jnp.moveaxis(t, 1, axis + 1)


def setup_inputs(seed: int = 0) -> dict:
    inp = _fwd_setup_inputs(seed)
    key = _jax.random.fold_in(_jax.random.key(seed), 7919)
    shape, _ = _output_shape()
    out = dict(inp)
    out["loss_target"] = _jax.random.normal(_jax.random.fold_in(key, 0), shape, _jnp.float32)
    for i, name in enumerate(TWIN_WEIGHTS):
        w = inp[name].astype(_jnp.float32)
        if MOMENT_SCALE is None:
            s = _jnp.sqrt(_jnp.mean(_jnp.square(w)) + 1e-30)
        else:
            s = MOMENT_SCALE[name]
        km, kv = _jax.random.split(_jax.random.fold_in(key, i + 1))
        out[name] = w
        out["m_" + name] = s * _jax.random.normal(km, w.shape, _jnp.float32)
        out["v_" + name] = (s * s) * _jax.random.uniform(kv, w.shape, _jnp.float32, 0.5, 1.5)
    if N_MICROBATCH > 1:
        for name, axis in PER_EXAMPLE_BATCH_AXIS.items():
            out[name] = _to_microbatches(out[name], axis)
    return {'x': out['x'], 'mem': out['mem'], 'norm_mix_pre': out['norm_mix_pre'], 'norm_mix_post': out['norm_mix_post'], 'w_in': out['w_in'], 'b_gate': out['b_gate'], 'ssm_a_re': out['ssm_a_re'], 'ssm_a_im': out['ssm_a_im'], 'ssm_log_dt': out['ssm_log_dt'], 'ssm_b_re': out['ssm_b_re'], 'ssm_b_im': out['ssm_b_im'], 'ssm_c_re': out['ssm_c_re'], 'ssm_c_im': out['ssm_c_im'], 'ssm_d': out['ssm_d'], 'ssm_w_glu': out['ssm_w_glu'], 'ssm_b_glu': out['ssm_b_glu'], 'w_branch_attn': out['w_branch_attn'], 'w_branch_ssm': out['w_branch_ssm'], 'w_out': out['w_out'], 'norm_xa_pre': out['norm_xa_pre'], 'norm_xa_post': out['norm_xa_post'], 'norm_mem': out['norm_mem'], 'xa_wq': out['xa_wq'], 'xa_wk': out['xa_wk'], 'xa_wv': out['xa_wv'], 'xa_wo': out['xa_wo'], 'norm_ffn_pre': out['norm_ffn_pre'], 'norm_ffn_post': out['norm_ffn_post'], 'ffn_w_up': out['ffn_w_up'], 'ffn_conv_w': out['ffn_conv_w'], 'ffn_conv_b': out['ffn_conv_b'], 'ffn_w_down': out['ffn_w_down'], 'loss_target': out['loss_target'], 'm_norm_mix_pre': out['m_norm_mix_pre'], 'm_norm_mix_post': out['m_norm_mix_post'], 'm_w_in': out['m_w_in'], 'm_b_gate': out['m_b_gate'], 'm_ssm_a_re': out['m_ssm_a_re'], 'm_ssm_a_im': out['m_ssm_a_im'], 'm_ssm_log_dt': out['m_ssm_log_dt'], 'm_ssm_b_re': out['m_ssm_b_re'], 'm_ssm_b_im': out['m_ssm_b_im'], 'm_ssm_c_re': out['m_ssm_c_re'], 'm_ssm_c_im': out['m_ssm_c_im'], 'm_ssm_d': out['m_ssm_d'], 'm_ssm_w_glu': out['m_ssm_w_glu'], 'm_ssm_b_glu': out['m_ssm_b_glu'], 'm_w_branch_attn': out['m_w_branch_attn'], 'm_w_branch_ssm': out['m_w_branch_ssm'], 'm_w_out': out['m_w_out'], 'm_norm_xa_pre': out['m_norm_xa_pre'], 'm_norm_xa_post': out['m_norm_xa_post'], 'm_norm_mem': out['m_norm_mem'], 'm_xa_wq': out['m_xa_wq'], 'm_xa_wk': out['m_xa_wk'], 'm_xa_wv': out['m_xa_wv'], 'm_xa_wo': out['m_xa_wo'], 'm_norm_ffn_pre': out['m_norm_ffn_pre'], 'm_norm_ffn_post': out['m_norm_ffn_post'], 'm_ffn_w_up': out['m_ffn_w_up'], 'm_ffn_conv_w': out['m_ffn_conv_w'], 'm_ffn_conv_b': out['m_ffn_conv_b'], 'm_ffn_w_down': out['m_ffn_w_down'], 'v_norm_mix_pre': out['v_norm_mix_pre'], 'v_norm_mix_post': out['v_norm_mix_post'], 'v_w_in': out['v_w_in'], 'v_b_gate': out['v_b_gate'], 'v_ssm_a_re': out['v_ssm_a_re'], 'v_ssm_a_im': out['v_ssm_a_im'], 'v_ssm_log_dt': out['v_ssm_log_dt'], 'v_ssm_b_re': out['v_ssm_b_re'], 'v_ssm_b_im': out['v_ssm_b_im'], 'v_ssm_c_re': out['v_ssm_c_re'], 'v_ssm_c_im': out['v_ssm_c_im'], 'v_ssm_d': out['v_ssm_d'], 'v_ssm_w_glu': out['v_ssm_w_glu'], 'v_ssm_b_glu': out['v_ssm_b_glu'], 'v_w_branch_attn': out['v_w_branch_attn'], 'v_w_branch_ssm': out['v_w_branch_ssm'], 'v_w_out': out['v_w_out'], 'v_norm_xa_pre': out['v_norm_xa_pre'], 'v_norm_xa_post': out['v_norm_xa_post'], 'v_norm_mem': out['v_norm_mem'], 'v_xa_wq': out['v_xa_wq'], 'v_xa_wk': out['v_xa_wk'], 'v_xa_wv': out['v_xa_wv'], 'v_xa_wo': out['v_xa_wo'], 'v_norm_ffn_pre': out['v_norm_ffn_pre'], 'v_norm_ffn_post': out['v_norm_ffn_post'], 'v_ffn_w_up': out['v_ffn_w_up'], 'v_ffn_conv_w': out['v_ffn_conv_w'], 'v_ffn_conv_b': out['v_ffn_conv_b'], 'v_ffn_w_down': out['v_ffn_w_down']}


def _loss(weights, diff, rest, loss_target):
    with _jax.named_scope("forward"):
        args = {**rest, TWIN_DIFF_INPUT: diff, **{k: w.astype(_WEIGHT_DTYPES[k]) for k, w in weights.items()}}
        y = _forward(args)
    with _jax.named_scope("loss_head"):
        err = _jnp.square(y.astype(_jnp.float32) - loss_target)
        return 0.5 * _jnp.sum(_jnp.mean(err, axis=-1)) if err.ndim else 0.5 * err


def _adamw(w, g, m, v):
    m = ADAM_B1 * m + (1.0 - ADAM_B1) * g
    v = ADAM_B2 * v + (1.0 - ADAM_B2) * _jnp.square(g)
    m_hat = m / (1.0 - ADAM_B1 ** ADAM_STEP)
    v_hat = v / (1.0 - ADAM_B2 ** ADAM_STEP)
    delta = -ADAM_LR * (m_hat / (_jnp.sqrt(v_hat) + ADAM_EPS) + ADAM_WD * w)
    return delta, m, v


def reference(x, mem, norm_mix_pre, norm_mix_post, w_in, b_gate, ssm_a_re, ssm_a_im, ssm_log_dt, ssm_b_re, ssm_b_im, ssm_c_re, ssm_c_im, ssm_d, ssm_w_glu, ssm_b_glu, w_branch_attn, w_branch_ssm, w_out, norm_xa_pre, norm_xa_post, norm_mem, xa_wq, xa_wk, xa_wv, xa_wo, norm_ffn_pre, norm_ffn_post, ffn_w_up, ffn_conv_w, ffn_conv_b, ffn_w_down, loss_target, m_norm_mix_pre, m_norm_mix_post, m_w_in, m_b_gate, m_ssm_a_re, m_ssm_a_im, m_ssm_log_dt, m_ssm_b_re, m_ssm_b_im, m_ssm_c_re, m_ssm_c_im, m_ssm_d, m_ssm_w_glu, m_ssm_b_glu, m_w_branch_attn, m_w_branch_ssm, m_w_out, m_norm_xa_pre, m_norm_xa_post, m_norm_mem, m_xa_wq, m_xa_wk, m_xa_wv, m_xa_wo, m_norm_ffn_pre, m_norm_ffn_post, m_ffn_w_up, m_ffn_conv_w, m_ffn_conv_b, m_ffn_w_down, v_norm_mix_pre, v_norm_mix_post, v_w_in, v_b_gate, v_ssm_a_re, v_ssm_a_im, v_ssm_log_dt, v_ssm_b_re, v_ssm_b_im, v_ssm_c_re, v_ssm_c_im, v_ssm_d, v_ssm_w_glu, v_ssm_b_glu, v_w_branch_attn, v_w_branch_ssm, v_w_out, v_norm_xa_pre, v_norm_xa_post, v_norm_mem, v_xa_wq, v_xa_wk, v_xa_wv, v_xa_wo, v_norm_ffn_pre, v_norm_ffn_post, v_ffn_w_up, v_ffn_conv_w, v_ffn_conv_b, v_ffn_w_down):
    given = dict(x=x, mem=mem, norm_mix_pre=norm_mix_pre, norm_mix_post=norm_mix_post, w_in=w_in, b_gate=b_gate, ssm_a_re=ssm_a_re, ssm_a_im=ssm_a_im, ssm_log_dt=ssm_log_dt, ssm_b_re=ssm_b_re, ssm_b_im=ssm_b_im, ssm_c_re=ssm_c_re, ssm_c_im=ssm_c_im, ssm_d=ssm_d, ssm_w_glu=ssm_w_glu, ssm_b_glu=ssm_b_glu, w_branch_attn=w_branch_attn, w_branch_ssm=w_branch_ssm, w_out=w_out, norm_xa_pre=norm_xa_pre, norm_xa_post=norm_xa_post, norm_mem=norm_mem, xa_wq=xa_wq, xa_wk=xa_wk, xa_wv=xa_wv, xa_wo=xa_wo, norm_ffn_pre=norm_ffn_pre, norm_ffn_post=norm_ffn_post, ffn_w_up=ffn_w_up, ffn_conv_w=ffn_conv_w, ffn_conv_b=ffn_conv_b, ffn_w_down=ffn_w_down, loss_target=loss_target, m_norm_mix_pre=m_norm_mix_pre, m_norm_mix_post=m_norm_mix_post, m_w_in=m_w_in, m_b_gate=m_b_gate, m_ssm_a_re=m_ssm_a_re, m_ssm_a_im=m_ssm_a_im, m_ssm_log_dt=m_ssm_log_dt, m_ssm_b_re=m_ssm_b_re, m_ssm_b_im=m_ssm_b_im, m_ssm_c_re=m_ssm_c_re, m_ssm_c_im=m_ssm_c_im, m_ssm_d=m_ssm_d, m_ssm_w_glu=m_ssm_w_glu, m_ssm_b_glu=m_ssm_b_glu, m_w_branch_attn=m_w_branch_attn, m_w_branch_ssm=m_w_branch_ssm, m_w_out=m_w_out, m_norm_xa_pre=m_norm_xa_pre, m_norm_xa_post=m_norm_xa_post, m_norm_mem=m_norm_mem, m_xa_wq=m_xa_wq, m_xa_wk=m_xa_wk, m_xa_wv=m_xa_wv, m_xa_wo=m_xa_wo, m_norm_ffn_pre=m_norm_ffn_pre, m_norm_ffn_post=m_norm_ffn_post, m_ffn_w_up=m_ffn_w_up, m_ffn_conv_w=m_ffn_conv_w, m_ffn_conv_b=m_ffn_conv_b, m_ffn_w_down=m_ffn_w_down, v_norm_mix_pre=v_norm_mix_pre, v_norm_mix_post=v_norm_mix_post, v_w_in=v_w_in, v_b_gate=v_b_gate, v_ssm_a_re=v_ssm_a_re, v_ssm_a_im=v_ssm_a_im, v_ssm_log_dt=v_ssm_log_dt, v_ssm_b_re=v_ssm_b_re, v_ssm_b_im=v_ssm_b_im, v_ssm_c_re=v_ssm_c_re, v_ssm_c_im=v_ssm_c_im, v_ssm_d=v_ssm_d, v_ssm_w_glu=v_ssm_w_glu, v_ssm_b_glu=v_ssm_b_glu, v_w_branch_attn=v_w_branch_attn, v_w_branch_ssm=v_w_branch_ssm, v_w_out=v_w_out, v_norm_xa_pre=v_norm_xa_pre, v_norm_xa_post=v_norm_xa_post, v_norm_mem=v_norm_mem, v_xa_wq=v_xa_wq, v_xa_wk=v_xa_wk, v_xa_wv=v_xa_wv, v_xa_wo=v_xa_wo, v_norm_ffn_pre=v_norm_ffn_pre, v_norm_ffn_post=v_norm_ffn_post, v_ffn_w_up=v_ffn_w_up, v_ffn_conv_w=v_ffn_conv_w, v_ffn_conv_b=v_ffn_conv_b, v_ffn_w_down=v_ffn_w_down)
    weights = {n: given[n] for n in TWIN_WEIGHTS}
    shared = {n: given[n] for n in SHARED_INPUTS}
    per_example = {n: given[n] for n in ['x', 'mem']}
    grad_fn = _jax.value_and_grad(_loss, argnums=(0, 1))

    def one_microbatch(ex, loss_target):
        ex = dict(ex)
        diff = ex.pop(TWIN_DIFF_INPUT)
        return grad_fn(weights, diff, {**shared, **ex}, loss_target)

    if N_MICROBATCH == 1:
        loss, (grad_w, grad_x) = one_microbatch(per_example, given["loss_target"])
    else:
        def body(carry, xs):
            loss_sum, grad_sum = carry
            l_k, (gw_k, gx_k) = one_microbatch(xs[0], xs[1])
            with _jax.named_scope("update"):
                return (loss_sum + l_k, _jax.tree.map(_jnp.add, grad_sum, gw_k)), gx_k

        init = (_jnp.zeros((), _jnp.float32), _jax.tree.map(_jnp.zeros_like, weights))
        (loss, grad_w), grad_x = _jax.lax.scan(body, init, (per_example, given["loss_target"]))
    with _jax.named_scope("update"):
        delta_w, new_m, new_v = {}, {}, {}
        for n in TWIN_WEIGHTS:
            delta_w[n], new_m[n], new_v[n] = _adamw(weights[n], grad_w[n], given["m_" + n], given["v_" + n])
    return (loss, grad_x, *[grad_w[n] for n in TWIN_WEIGHTS], *[delta_w[n] for n in TWIN_WEIGHTS],
            *[new_m[n] for n in TWIN_WEIGHTS], *[new_v[n] for n in TWIN_WEIGHTS])
```

```python
import math

import jax
import jax.numpy as jnp
from jax import lax
from jax.experimental import pallas as pl
from jax.experimental.pallas import tpu as pltpu

F32 = jnp.float32
BF16 = jnp.bfloat16

D_MODEL = 1024
SB_HEADS = 8
SB_HEAD_DIM = 64
SB_WIDTH = 512
SSM_WIDTH = 512
SSM_GROUP = 16
SSM_GROUPS = 32
SSM_STATE = 64
XA_HEADS = 4
XA_HEAD_DIM = 256
D_FF = 2816
RMS_EPS = 1e-6
IN_WIDTH = 4096
N_DEV = 8

ADAM_LR = 0.001
ADAM_B1 = 0.9
ADAM_B2 = 0.999
ADAM_EPS = 1e-08
ADAM_WD = 0.01
ADAM_STEP = 10

LANES = 128
SUBLANES = 8
VMEM_LIMIT = 48 * 1024 * 1024

_GELU_C = math.sqrt(2.0 / math.pi)


def _cparams(*sem):
    return pltpu.CompilerParams(dimension_semantics=sem, vmem_limit_bytes=VMEM_LIMIT)


def _pick(n, cands):
    for c in cands:
        if n % c == 0:
            return c
    return n


def _gelu(x):
    return 0.5 * x * (1.0 + jnp.tanh(_GELU_C * (x + 0.044715 * x * x * x)))


def _gelu_and_grad(x):
    t = jnp.tanh(_GELU_C * (x + 0.044715 * x * x * x))
    g = 0.5 * x * (1.0 + t)
    dg = 0.5 * (1.0 + t) + 0.5 * x * (1.0 - t * t) * _GELU_C * (1.0 + 3.0 * 0.044715 * x * x)
    return g, dg


def _sigmoid(x):
    return 1.0 / (1.0 + jnp.exp(-x))


def _dot(a, b, ca, cb):
    return lax.dot_general(a.astype(BF16), b.astype(BF16), (((ca,), (cb,)), ((), ())),
                           preferred_element_type=F32)


def _matmul(a, b, *, ta=False, tb=False, out_dtype=F32, name):
    if ta:
        K, M = a.shape
    else:
        M, K = a.shape
    if tb:
        N, K2 = b.shape
    else:
        K2, N = b.shape
    assert K == K2, (a.shape, b.shape, ta, tb)
    tm = _pick(M, (512, 256, 128))
    tn = _pick(N, (512, 256, 128))
    tk = _pick(K, (1024, 512, 256, 128))
    nk = K // tk
    ca, cb = (0 if ta else 1), (1 if tb else 0)

    def body(a_ref, b_ref, o_ref, *scratch):
        p = _dot(a_ref[...], b_ref[...], ca, cb)
        if nk == 1:
            o_ref[...] = p.astype(out_dtype)
        else:
            acc_ref, = scratch
            k = pl.program_id(2)

            @pl.when(k == 0)
            def _():
                acc_ref[...] = p

            @pl.when(k > 0)
            def _():
                acc_ref[...] += p

            @pl.when(k == nk - 1)
            def _():
                o_ref[...] = acc_ref[...].astype(out_dtype)

    a_spec = pl.BlockSpec((tk, tm), lambda j, i, k: (k, i)) if ta else pl.BlockSpec((tm, tk), lambda j, i, k: (i, k))
    b_spec = pl.BlockSpec((tn, tk), lambda j, i, k: (j, k)) if tb else pl.BlockSpec((tk, tn), lambda j, i, k: (k, j))
    return pl.pallas_call(
        body, name=name,
        out_shape=jax.ShapeDtypeStruct((M, N), out_dtype),
        grid=(N // tn, M // tm, nk),
        in_specs=[a_spec, b_spec],
        out_specs=pl.BlockSpec((tm, tn), lambda j, i, k: (i, j)),
        scratch_shapes=[] if nk == 1 else [pltpu.VMEM((tm, tn), F32)],
        compiler_params=_cparams("parallel", "parallel", "arbitrary"),
    )(a, b)


def _rms(x, g):
    r = lax.rsqrt(jnp.mean(x * x, axis=-1, keepdims=True) + RMS_EPS)
    return x * r * g


def _rms_bwd(dy, x, g):
    r = lax.rsqrt(jnp.mean(x * x, axis=-1, keepdims=True) + RMS_EPS)
    xh = x * r
    dxh = dy * g
    dx = r * (dxh - xh * jnp.mean(dxh * xh, axis=-1, keepdims=True))
    dg = jnp.sum(dy * xh, axis=0, keepdims=True)
    return dx, dg


def _row_tile(rows):
    return _pick(rows, (512, 256, 128, 64, 32, 16, 8))


def _rms_fwd(x, g, *, name):
    R, D = x.shape
    tr = _row_tile(R)

    def body(x_ref, g_ref, h_ref):
        h_ref[...] = _rms(x_ref[...], g_ref[...]).astype(BF16)

    return pl.pallas_call(
        body, name=name, out_shape=jax.ShapeDtypeStruct((R, D), BF16), grid=(R // tr,),
        in_specs=[pl.BlockSpec((tr, D), lambda i: (i, 0)), pl.BlockSpec((1, D), lambda i: (0, 0))],
        out_specs=pl.BlockSpec((tr, D), lambda i: (i, 0)),
        compiler_params=_cparams("parallel"),
    )(x, g)


def _resnorm_norm(x, z, g_post, g_next, *, name):
    R, D = x.shape
    tr = _row_tile(R)

    def body(x_ref, z_ref, gp_ref, gn_ref, xn_ref, h_ref):
        xn = x_ref[...] + _rms(z_ref[...], gp_ref[...])
        xn_ref[...] = xn
        h_ref[...] = _rms(xn, gn_ref[...]).astype(BF16)

    row = pl.BlockSpec((tr, D), lambda i: (i, 0))
    vec = pl.BlockSpec((1, D), lambda i: (0, 0))
    return pl.pallas_call(
        body, name=name,
        out_shape=(jax.ShapeDtypeStruct((R, D), F32), jax.ShapeDtypeStruct((R, D), BF16)),
        grid=(R // tr,), in_specs=[row, row, vec, vec], out_specs=(row, row),
        compiler_params=_cparams("parallel"),
    )(x, z, g_post, g_next)


def _final_loss(x, z, g_post, target, *, name):
    R, D = x.shape
    tr = _row_tile(R)

    def body(x_ref, z_ref, gp_ref, t_ref, loss_ref, dy_ref, dz_ref, dg_ref):
        i = pl.program_id(0)
        z = z_ref[...]
        g = gp_ref[...]
        err = x_ref[...] + _rms(z, g) - t_ref[...]
        dy = err * (1.0 / D)
        dy_ref[...] = dy
        dz, dg = _rms_bwd(dy, z, g)
        dz_ref[...] = dz.astype(BF16)
        part = 0.5 * jnp.sum(jnp.sum(err * err, axis=-1, keepdims=True) * (1.0 / D), axis=0, keepdims=True)

        @pl.when(i == 0)
        def _():
            loss_ref[...] = part
            dg_ref[...] = dg

        @pl.when(i > 0)
        def _():
            loss_ref[...] += part
            dg_ref[...] += dg

    row = pl.BlockSpec((tr, D), lambda i: (i, 0))
    vec = pl.BlockSpec((1, D), lambda i: (0, 0))
    return pl.pallas_call(
        body, name=name,
        out_shape=(jax.ShapeDtypeStruct((1, 1), F32), jax.ShapeDtypeStruct((R, D), F32),
                   jax.ShapeDtypeStruct((R, D), BF16), jax.ShapeDtypeStruct((1, D), F32)),
        grid=(R // tr,), in_specs=[row, row, vec, row],
        out_specs=(pl.BlockSpec((1, 1), lambda i: (0, 0)), row, row, vec),
        compiler_params=_cparams("arbitrary"),
    )(x, z, g_post, target)


def _norm_bwd_pair(dres, dh, xk, g_pre, zprev, g_prev_post, *, name):
    R, D = xk.shape
    tr = _row_tile(R)

    def body(dres_ref, dh_ref, x_ref, gpre_ref, z_ref, gpost_ref, dx_ref, dz_ref, dgpre_ref, dgpost_ref):
        i = pl.program_id(0)
        d1, dgpre = _rms_bwd(dh_ref[...], x_ref[...], gpre_ref[...])
        dx = dres_ref[...] + d1
        dx_ref[...] = dx
        dz, dgpost = _rms_bwd(dx, z_ref[...], gpost_ref[...])
        dz_ref[...] = dz.astype(BF16)

        @pl.when(i == 0)
        def _():
            dgpre_ref[...] = dgpre
            dgpost_ref[...] = dgpost

        @pl.when(i > 0)
        def _():
            dgpre_ref[...] += dgpre
            dgpost_ref[...] += dgpost

    row = pl.BlockSpec((tr, D), lambda i: (i, 0))
    vec = pl.BlockSpec((1, D), lambda i: (0, 0))
    return pl.pallas_call(
        body, name=name,
        out_shape=(jax.ShapeDtypeStruct((R, D), F32), jax.ShapeDtypeStruct((R, D), BF16),
                   jax.ShapeDtypeStruct((1, D), F32), jax.ShapeDtypeStruct((1, D), F32)),
        grid=(R // tr,), in_specs=[row, row, row, vec, row, vec], out_specs=(row, row, vec, vec),
        compiler_params=_cparams("arbitrary"),
    )(dres, dh, xk, g_pre, zprev, g_prev_post)


def _norm_bwd_single(dres, dh, xk, g_pre, *, name):
    R, D = xk.shape
    tr = _row_tile(R)
    has_res = dres is not None

    def body(*refs):
        if has_res:
            dres_ref, dh_ref, x_ref, gpre_ref, dx_ref, dgpre_ref = refs
        else:
            dh_ref, x_ref, gpre_ref, dx_ref, dgpre_ref = refs
        i = pl.program_id(0)
        d1, dgpre = _rms_bwd(dh_ref[...], x_ref[...], gpre_ref[...])
        dx_ref[...] = dres_ref[...] + d1 if has_res else d1

        @pl.when(i == 0)
        def _():
            dgpre_ref[...] = dgpre

        @pl.when(i > 0)
        def _():
            dgpre_ref[...] += dgpre

    row = pl.BlockSpec((tr, D), lambda i: (i, 0))
    vec = pl.BlockSpec((1, D), lambda i: (0, 0))
    ins = ([dres] if has_res else []) + [dh, xk, g_pre]
    return pl.pallas_call(
        body, name=name,
        out_shape=(jax.ShapeDtypeStruct((R, D), F32), jax.ShapeDtypeStruct((1, D), F32)),
        grid=(R // tr,), in_specs=([row] if has_res else []) + [row, row, vec], out_specs=(row, vec),
        compiler_params=_cparams("arbitrary"),
    )(*ins)


SB_BLOCK = 256


def _sb_tri(kind):
    r = lax.broadcasted_iota(jnp.int32, (SB_BLOCK, SB_BLOCK), 0)
    c = lax.broadcasted_iota(jnp.int32, (SB_BLOCK, SB_BLOCK), 1)
    keep = {"after": r > c, "upto": r <= c, "before": r < c}[kind]
    return jnp.where(keep, 1.0, 0.0).astype(BF16)


def _running_sum(vals, tri):
    hi = vals.astype(BF16)
    lo = (vals - hi.astype(F32)).astype(BF16)
    return _dot(hi, tri, 1, 0) + _dot(lo, tri, 1, 0)


def _sb_scores(qm, k_blk, i, j):
    T = SB_BLOCK
    z = _dot(qm, k_blk, 1, 1)
    r = lax.broadcasted_iota(jnp.int32, (T, T), 0)
    c = lax.broadcasted_iota(jnp.int32, (T, T), 1)
    causal = (c - r) < (i - j) * T
    sp = jnp.maximum(z, 0.0) + jnp.log(1.0 + jnp.exp(-jnp.abs(z)))
    return z, sp, causal


def _head_masks():
    lane = lax.broadcasted_iota(jnp.int32, (1, LANES), 1)
    return [jnp.where(lane < SB_HEAD_DIM, 1.0, 0.0), jnp.where(lane >= SB_HEAD_DIM, 1.0, 0.0)]


def _sb_fwd(proj, *, name):
    S = proj.shape[0]
    T = SB_BLOCK
    nq = S // T
    npair = SB_WIDTH // LANES
    scale = SB_HEAD_DIM ** -0.5

    def body(q_ref, k_ref, v_ref, o_ref, tot_ref, acc_ref, run_ref):
        masks = _head_masks()
        tri = _sb_tri("after")

        def q_block(i, _):
            qrow = pl.ds(pl.multiple_of(i * T, T), T)
            q = q_ref[qrow, :] * scale
            qm = [(q * m).astype(BF16) for m in masks]
            acc_ref[...] = jnp.zeros_like(acc_ref)
            run_ref[...] = jnp.zeros_like(run_ref)

            def k_block(jj, _):
                j = i - jj
                krow = pl.ds(pl.multiple_of(j * T, T), T)
                k_blk = k_ref[krow, :].astype(BF16)
                v_blk = v_ref[krow, :].astype(BF16)
                for h in range(2):
                    z, sp, causal = _sb_scores(qm[h], k_blk, i, j)
                    lf = jnp.where(causal, -sp, 0.0)
                    ls = _running_sum(lf, tri)
                    w = jnp.where(causal, jnp.exp(z - sp + ls + run_ref[h]), 0.0)
                    acc_ref[h] += _dot(w, v_blk, 1, 0)
                    run_ref[h] += jnp.sum(lf, axis=1, keepdims=True)
                return 0

            lax.fori_loop(0, i + 1, k_block, 0)
            o_ref[qrow, :] = (acc_ref[0] * masks[0] + acc_ref[1] * masks[1]).astype(BF16)
            tot_ref[qrow, :] = run_ref[0] * masks[0] + run_ref[1] * masks[1]
            return 0

        lax.fori_loop(0, nq, q_block, 0)

    blk = lambda off: pl.BlockSpec((S, LANES), lambda p: (0, off + p))
    return pl.pallas_call(
        body, name=name,
        out_shape=(jax.ShapeDtypeStruct((S, SB_WIDTH), BF16), jax.ShapeDtypeStruct((S, SB_WIDTH), F32)),
        grid=(npair,),
        in_specs=[blk(0), blk(npair), blk(2 * npair)],
        out_specs=(blk(0), blk(0)),
        scratch_shapes=[pltpu.VMEM((2, T, LANES), F32), pltpu.VMEM((2, T, 1), F32)],
        compiler_params=_cparams("parallel"),
    )(proj, proj, proj)


def _sb_bwd(proj, tot, do_attn, *, name):
    S = proj.shape[0]
    T = SB_BLOCK
    nq = S // T
    npair = SB_WIDTH // LANES
    scale = SB_HEAD_DIM ** -0.5

    def body(q_ref, k_ref, v_ref, tot_ref, do_ref, dq_ref, dk_ref, dv_ref,
             dqacc_ref, dkacc_ref, dvacc_ref, run_ref, grun_ref):
        masks = _head_masks()
        tri_upto = _sb_tri("upto")
        tri_before = _sb_tri("before")
        dkacc_ref[...] = jnp.zeros_like(dkacc_ref)
        dvacc_ref[...] = jnp.zeros_like(dvacc_ref)

        def q_block(i, _):
            qrow = pl.ds(pl.multiple_of(i * T, T), T)
            q = q_ref[qrow, :] * scale
            do = do_ref[qrow, :].astype(F32)
            tot = tot_ref[qrow, :]
            qm = [(q * m).astype(BF16) for m in masks]
            dom = [(do * m).astype(BF16) for m in masks]
            ltot = [jnp.sum(tot * m, axis=1, keepdims=True) * (1.0 / SB_HEAD_DIM) for m in masks]
            dqacc_ref[...] = jnp.zeros_like(dqacc_ref)
            run_ref[...] = jnp.zeros_like(run_ref)
            grun_ref[...] = jnp.zeros_like(grun_ref)

            def k_block(j, _):
                krow = pl.ds(pl.multiple_of(j * T, T), T)
                k_blk = k_ref[krow, :].astype(BF16)
                v_blk = v_ref[krow, :].astype(BF16)
                for h in range(2):
                    z, sp, causal = _sb_scores(qm[h], k_blk, i, j)
                    lf = jnp.where(causal, -sp, 0.0)
                    later = ltot[h] - run_ref[h] - _running_sum(lf, tri_upto)
                    w = jnp.where(causal, jnp.exp(z - sp + later), 0.0)
                    dw = _dot(dom[h], v_blk, 1, 1)
                    g = dw * w
                    gbefore = grun_ref[h] + _running_sum(g, tri_before)
                    dz = jnp.where(causal, g * jnp.exp(-sp) - gbefore * jnp.exp(z - sp), 0.0).astype(BF16)
                    dqacc_ref[h] += _dot(dz, k_blk, 1, 0)
                    dkacc_ref[krow, :] += _dot(dz, qm[h], 0, 0)
                    dvacc_ref[krow, :] += _dot(w, dom[h], 0, 0)
                    run_ref[h] += jnp.sum(lf, axis=1, keepdims=True)
                    grun_ref[h] += jnp.sum(g, axis=1, keepdims=True)
                return 0

            lax.fori_loop(0, i + 1, k_block, 0)
            dq_ref[qrow, :] = ((dqacc_ref[0] * masks[0] + dqacc_ref[1] * masks[1]) * scale).astype(BF16)
            return 0

        lax.fori_loop(0, nq, q_block, 0)
        dk_ref[...] = dkacc_ref[...].astype(BF16)
        dv_ref[...] = dvacc_ref[...].astype(BF16)

    blk = lambda off: pl.BlockSpec((S, LANES), lambda p: (0, off + p))
    out = jax.ShapeDtypeStruct((S, SB_WIDTH), BF16)
    return pl.pallas_call(
        body, name=name, out_shape=(out, out, out), grid=(npair,),
        in_specs=[blk(0), blk(npair), blk(2 * npair), blk(0), blk(0)],
        out_specs=(blk(0), blk(0), blk(0)),
        scratch_shapes=[pltpu.VMEM((2, T, LANES), F32), pltpu.VMEM((S, LANES), F32), pltpu.VMEM((S, LANES), F32),
                        pltpu.VMEM((2, T, 1), F32), pltpu.VMEM((2, T, 1), F32)],
        compiler_params=_cparams("parallel"),
    )(proj, proj, proj, tot, do_attn)


SSM_HALVES = 2
SSM_HALF_CH = SSM_WIDTH // SSM_HALVES
SSM_HALF_ST = SSM_GROUPS * SSM_STATE // SSM_HALVES
SSM_CHUNK = 512


def _cmul(ar, ai, br, bi):
    return ar * br - ai * bi, ar * bi + ai * br


def _ssm_tables(lam_re, lam_im):
    lr = lam_re.reshape(-1)
    li = lam_im.reshape(-1)
    pows = [(jnp.ones_like(lr), jnp.zeros_like(li)), (lr, li)]
    for _ in range(2, SUBLANES + 1):
        pows.append(_cmul(pows[-1][0], pows[-1][1], lr, li))
    row = jnp.arange(SUBLANES)[:, None]

    def shift_tab(d, keep):
        return [jnp.where(keep, pows[d][0][None, :], 0.0), jnp.where(keep, pows[d][1][None, :], 0.0)]

    fwd, bwd = [], []
    for d in (1, 2, 4):
        fwd += shift_tab(d, row >= d)
        bwd += shift_tab(d, row + d < SUBLANES)
    fwd += [jnp.stack([pows[r + 1][0] for r in range(SUBLANES)]), jnp.stack([pows[r + 1][1] for r in range(SUBLANES)])]
    bwd += [jnp.stack([pows[SUBLANES - r][0] for r in range(SUBLANES)]),
            jnp.stack([pows[SUBLANES - r][1] for r in range(SUBLANES)])]

    def halves(tabs):
        t = jnp.stack(tabs)
        return t.reshape(8, SUBLANES, SSM_HALVES, SSM_HALF_ST).transpose(2, 0, 1, 3)

    return halves(fwd), halves(bwd)


def _ssm_fwd(proj, bd_re, bd_im, cd_re, cd_imneg, d_skip, tab, *, name):
    S = proj.shape[0]
    Tc = min(SSM_CHUNK, S)
    nc = S // Tc
    u_blk0 = (3 * SB_WIDTH) // SSM_HALF_CH

    def body(u_ref, bre_ref, bim_ref, cre_ref, cim_ref, d_ref, tab_ref, y_ref, xre_ref, xim_ref, cre_s, cim_s):
        c = pl.program_id(1)

        @pl.when(c == 0)
        def _():
            cre_s[...] = jnp.zeros_like(cre_s)
            cim_s[...] = jnp.zeros_like(cim_s)

        u = u_ref[...]
        ub = u.astype(BF16)
        xre_ref[...] = _dot(ub, bre_ref[0], 1, 0)
        xim_ref[...] = _dot(ub, bim_ref[0], 1, 0)

        def slab(k, carry):
            car_re, car_im = carry
            rows = pl.ds(pl.multiple_of(k * SUBLANES, SUBLANES), SUBLANES)
            sre = xre_ref[rows, :]
            sim = xim_ref[rows, :]
            for n, d in enumerate((1, 2, 4)):
                pre, pim = tab_ref[0, 2 * n], tab_ref[0, 2 * n + 1]
                rre = pltpu.roll(sre, d, 0)
                rim = pltpu.roll(sim, d, 0)
                sre, sim = sre + (pre * rre - pim * rim), sim + (pre * rim + pim * rre)
            pre, pim = tab_ref[0, 6], tab_ref[0, 7]
            sre, sim = sre + (pre * car_re - pim * car_im), sim + (pre * car_im + pim * car_re)
            xre_ref[rows, :] = sre
            xim_ref[rows, :] = sim
            last = (SUBLANES - 1, SUBLANES)
            return (jnp.broadcast_to(sre[last[0]:last[1], :], sre.shape),
                    jnp.broadcast_to(sim[last[0]:last[1], :], sim.shape))

        car = lax.fori_loop(0, Tc // SUBLANES, slab, (cre_s[...], cim_s[...]))
        cre_s[...] = car[0]
        cim_s[...] = car[1]
        y = _dot(xre_ref[...], cre_ref[0], 1, 0) + _dot(xim_ref[...], cim_ref[0], 1, 0)
        y_ref[...] = y + d_ref[...] * u

    return pl.pallas_call(
        body, name=name,
        out_shape=(jax.ShapeDtypeStruct((S, SSM_WIDTH), F32),
                   jax.ShapeDtypeStruct((S, SSM_HALVES * SSM_HALF_ST), F32),
                   jax.ShapeDtypeStruct((S, SSM_HALVES * SSM_HALF_ST), F32)),
        grid=(SSM_HALVES, nc),
        in_specs=[pl.BlockSpec((Tc, SSM_HALF_CH), lambda h, c: (c, u_blk0 + h)),
                  pl.BlockSpec((1, SSM_HALF_CH, SSM_HALF_ST), lambda h, c: (h, 0, 0)),
                  pl.BlockSpec((1, SSM_HALF_CH, SSM_HALF_ST), lambda h, c: (h, 0, 0)),
                  pl.BlockSpec((1, SSM_HALF_ST, SSM_HALF_CH), lambda h, c: (h, 0, 0)),
                  pl.BlockSpec((1, SSM_HALF_ST, SSM_HALF_CH), lambda h, c: (h, 0, 0)),
                  pl.BlockSpec((1, SSM_HALF_CH), lambda h, c: (0, h)),
                  pl.BlockSpec((1, 8, SUBLANES, SSM_HALF_ST), lambda h, c: (h, 0, 0, 0))],
        out_specs=(pl.BlockSpec((Tc, SSM_HALF_CH), lambda h, c: (c, h)),
                   pl.BlockSpec((Tc, SSM_HALF_ST), lambda h, c: (c, h)),
                   pl.BlockSpec((Tc, SSM_HALF_ST), lambda h, c: (c, h))),
        scratch_shapes=[pltpu.VMEM((SUBLANES, SSM_HALF_ST), F32), pltpu.VMEM((SUBLANES, SSM_HALF_ST), F32)],
        compiler_params=_cparams("parallel", "arbitrary"),
    )(proj, bd_re, bd_im, cd_re, cd_imneg, d_skip, tab)


def _ssm_bwd(dy, proj, x_re, x_im, bd_re, bd_im, cd_re, cd_imneg, d_skip, tab, *, name):
    S = proj.shape[0]
    Tc = min(SSM_CHUNK, S)
    nc = S // Tc
    u_blk0 = (3 * SB_WIDTH) // SSM_HALF_CH

    def body(dy_ref, u_ref, xre_ref, xim_ref, bre_ref, bim_ref, cre_ref, cim_ref, d_ref, tab_ref,
             du_ref, dbre_ref, dbim_ref, dcre_ref, dcim_ref, dd_ref, dlre_ref, dlim_ref,
             gre_s, gim_s, cre_s, cim_s):
        c = pl.program_id(1)

        @pl.when(c == 0)
        def _():
            cre_s[...] = jnp.zeros_like(cre_s)
            cim_s[...] = jnp.zeros_like(cim_s)
            dbre_ref[...] = jnp.zeros_like(dbre_ref)
            dbim_ref[...] = jnp.zeros_like(dbim_ref)
            dcre_ref[...] = jnp.zeros_like(dcre_ref)
            dcim_ref[...] = jnp.zeros_like(dcim_ref)
            dd_ref[...] = jnp.zeros_like(dd_ref)
            dlre_ref[...] = jnp.zeros_like(dlre_ref)
            dlim_ref[...] = jnp.zeros_like(dlim_ref)

        dy = dy_ref[...]
        dyb = dy.astype(BF16)
        u = u_ref[...]
        gre_s[...] = _dot(dyb, cre_ref[0], 1, 1)
        gim_s[...] = _dot(dyb, cim_ref[0], 1, 1)
        row = lax.broadcasted_iota(jnp.int32, (SUBLANES, SSM_HALF_ST), 0)
        nslab = Tc // SUBLANES

        def slab(kk, carry):
            car_re, car_im, acc_re, acc_im = carry
            k = nslab - 1 - kk
            rows = pl.ds(pl.multiple_of(k * SUBLANES, SUBLANES), SUBLANES)
            sre = gre_s[rows, :]
            sim = gim_s[rows, :]
            for n, d in enumerate((1, 2, 4)):
                pre, pim = tab_ref[0, 2 * n], tab_ref[0, 2 * n + 1]
                rre = pltpu.roll(sre, SUBLANES - d, 0)
                rim = pltpu.roll(sim, SUBLANES - d, 0)
                sre, sim = sre + (pre * rre + pim * rim), sim + (pre * rim - pim * rre)
            pre, pim = tab_ref[0, 6], tab_ref[0, 7]
            sre, sim = sre + (pre * car_re + pim * car_im), sim + (pre * car_im - pim * car_re)
            gre_s[rows, :] = sre
            gim_s[rows, :] = sim
            nre = jnp.where(row == SUBLANES - 1, car_re, pltpu.roll(sre, SUBLANES - 1, 0))
            nim = jnp.where(row == SUBLANES - 1, car_im, pltpu.roll(sim, SUBLANES - 1, 0))
            xr = xre_ref[rows, :]
            xi = xim_ref[rows, :]
            acc_re = acc_re + (nre * xr + nim * xi)
            acc_im = acc_im + (nim * xr - nre * xi)
            return (jnp.broadcast_to(sre[0:1, :], sre.shape), jnp.broadcast_to(sim[0:1, :], sim.shape), acc_re, acc_im)

        car = lax.fori_loop(0, nslab, slab, (cre_s[...], cim_s[...], dlre_ref[0], dlim_ref[0]))
        cre_s[...] = car[0]
        cim_s[...] = car[1]
        dlre_ref[0] = car[2]
        dlim_ref[0] = car[3]
        gre = gre_s[...].astype(BF16)
        gim = gim_s[...].astype(BF16)
        ub = u.astype(BF16)
        du = _dot(gre, bre_ref[0], 1, 1) + _dot(gim, bim_ref[0], 1, 1) + d_ref[...] * dy
        du_ref[...] = du.astype(BF16)
        dbre_ref[0] += _dot(ub, gre, 0, 0)
        dbim_ref[0] += _dot(ub, gim, 0, 0)
        dcre_ref[0] += _dot(xre_ref[...], dyb, 0, 0)
        dcim_ref[0] += _dot(xim_ref[...], dyb, 0, 0)
        dd_ref[...] += jnp.sum(dy * u, axis=0, keepdims=True)

    rev = lambda c: nc - 1 - c
    return pl.pallas_call(
        body, name=name,
        out_shape=(jax.ShapeDtypeStruct((S, SSM_WIDTH), BF16),
                   jax.ShapeDtypeStruct((SSM_HALVES, SSM_HALF_CH, SSM_HALF_ST), F32),
                   jax.ShapeDtypeStruct((SSM_HALVES, SSM_HALF_CH, SSM_HALF_ST), F32),
                   jax.ShapeDtypeStruct((SSM_HALVES, SSM_HALF_ST, SSM_HALF_CH), F32),
                   jax.ShapeDtypeStruct((SSM_HALVES, SSM_HALF_ST, SSM_HALF_CH), F32),
                   jax.ShapeDtypeStruct((1, SSM_WIDTH), F32),
                   jax.ShapeDtypeStruct((SSM_HALVES, SUBLANES, SSM_HALF_ST), F32),
                   jax.ShapeDtypeStruct((SSM_HALVES, SUBLANES, SSM_HALF_ST), F32)),
        grid=(SSM_HALVES, nc),
        in_specs=[pl.BlockSpec((Tc, SSM_HALF_CH), lambda h, c: (rev(c), h)),
                  pl.BlockSpec((Tc, SSM_HALF_CH), lambda h, c: (rev(c), u_blk0 + h)),
                  pl.BlockSpec((Tc, SSM_HALF_ST), lambda h, c: (rev(c), h)),
                  pl.BlockSpec((Tc, SSM_HALF_ST), lambda h, c: (rev(c), h)),
                  pl.BlockSpec((1, SSM_HALF_CH, SSM_HALF_ST), lambda h, c: (h, 0, 0)),
                  pl.BlockSpec((1, SSM_HALF_CH, SSM_HALF_ST), lambda h, c: (h, 0, 0)),
                  pl.BlockSpec((1, SSM_HALF_ST, SSM_HALF_CH), lambda h, c: (h, 0, 0)),
                  pl.BlockSpec((1, SSM_HALF_ST, SSM_HALF_CH), lambda h, c: (h, 0, 0)),
                  pl.BlockSpec((1, SSM_HALF_CH), lambda h, c: (0, h)),
                  pl.BlockSpec((1, 8, SUBLANES, SSM_HALF_ST), lambda h, c: (h, 0, 0, 0))],
        out_specs=(pl.BlockSpec((Tc, SSM_HALF_CH), lambda h, c: (rev(c), h)),
                   pl.BlockSpec((1, SSM_HALF_CH, SSM_HALF_ST), lambda h, c: (h, 0, 0)),
                   pl.BlockSpec((1, SSM_HALF_CH, SSM_HALF_ST), lambda h, c: (h, 0, 0)),
                   pl.BlockSpec((1, SSM_HALF_ST, SSM_HALF_CH), lambda h, c: (h, 0, 0)),
                   pl.BlockSpec((1, SSM_HALF_ST, SSM_HALF_CH), lambda h, c: (h, 0, 0)),
                   pl.BlockSpec((1, SSM_HALF_CH), lambda h, c: (0, h)),
                   pl.BlockSpec((1, SUBLANES, SSM_HALF_ST), lambda h, c: (h, 0, 0)),
                   pl.BlockSpec((1, SUBLANES, SSM_HALF_ST), lambda h, c: (h, 0, 0))),
        scratch_shapes=[pltpu.VMEM((Tc, SSM_HALF_ST), F32), pltpu.VMEM((Tc, SSM_HALF_ST), F32),
                        pltpu.VMEM((SUBLANES, SSM_HALF_ST), F32), pltpu.VMEM((SUBLANES, SSM_HALF_ST), F32)],
        compiler_params=_cparams("parallel", "arbitrary"),
    )(dy, proj, x_re, x_im, bd_re, bd_im, cd_re, cd_imneg, d_skip, tab)


def _ssm_prepare(a_re, a_im, log_dt, b_re, b_im):
    dt = jnp.exp(log_dt)[:, None]
    mag = jnp.exp(a_re * dt)
    lre = mag * jnp.cos(a_im * dt)
    lim = mag * jnp.sin(a_im * dt)
    den = a_re * a_re + a_im * a_im
    fre = ((lre - 1.0) * a_re + lim * a_im) / den
    fim = (lim * a_re - (lre - 1.0) * a_im) / den
    bbre = fre[:, :, None] * b_re - fim[:, :, None] * b_im
    bbim = fre[:, :, None] * b_im + fim[:, :, None] * b_re
    return lre, lim, bbre, bbim


def _group_eye():
    return jnp.eye(SSM_GROUPS // SSM_HALVES, dtype=F32)


def _bd_from_bbar(bbar):
    gh = SSM_GROUPS // SSM_HALVES
    b = bbar.reshape(SSM_HALVES, gh, SSM_STATE, SSM_GROUP).transpose(0, 1, 3, 2)
    out = b[:, :, :, None, :] * _group_eye()[None, :, None, :, None]
    return out.reshape(SSM_HALVES, SSM_HALF_CH, SSM_HALF_ST)


def _bbar_from_bd(dbd):
    gh = SSM_GROUPS // SSM_HALVES
    d = dbd.reshape(SSM_HALVES, gh, SSM_GROUP, gh, SSM_STATE)
    d = jnp.sum(d * _group_eye()[None, :, None, :, None], axis=3)
    return d.transpose(0, 1, 3, 2).reshape(SSM_GROUPS, SSM_STATE, SSM_GROUP)


def _cd_from_c(cmat):
    gh = SSM_GROUPS // SSM_HALVES
    c = cmat.reshape(SSM_HALVES, gh, SSM_GROUP, SSM_STATE).transpose(0, 1, 3, 2)
    out = c[:, :, :, None, :] * _group_eye()[None, :, None, :, None]
    return out.reshape(SSM_HALVES, SSM_HALF_ST, SSM_HALF_CH)


def _c_from_cd(dcd):
    gh = SSM_GROUPS // SSM_HALVES
    d = dcd.reshape(SSM_HALVES, gh, SSM_STATE, gh, SSM_GROUP)
    d = jnp.sum(d * _group_eye()[None, :, None, :, None], axis=3)
    return d.transpose(0, 1, 3, 2).reshape(SSM_GROUPS, SSM_GROUP, SSM_STATE)


def _glu_fwd(y_pre, w_glu, b_glu, *, name):
    S, W = y_pre.shape
    tr = _row_tile(S)

    def body(y_ref, w_ref, b_ref, o_ref):
        yg = _gelu(y_ref[...])
        gl = _dot(yg, w_ref[...], 1, 0) + b_ref[...]
        o_ref[...] = (yg * _sigmoid(gl)).astype(BF16)

    row = pl.BlockSpec((tr, W), lambda i: (i, 0))
    return pl.pallas_call(
        body, name=name, out_shape=jax.ShapeDtypeStruct((S, W), BF16), grid=(S // tr,),
        in_specs=[row, pl.BlockSpec((W, W), lambda i: (0, 0)), pl.BlockSpec((1, W), lambda i: (0, 0))],
        out_specs=row, compiler_params=_cparams("parallel"),
    )(y_pre, w_glu, b_glu)


def _glu_bwd(y_pre, do, w_glu, b_glu, *, name):
    S, W = y_pre.shape
    tr = _row_tile(S)

    def body(y_ref, do_ref, w_ref, b_ref, dy_ref, dw_ref, db_ref):
        i = pl.program_id(0)
        yg, dyg_dy = _gelu_and_grad(y_ref[...])
        ygb = yg.astype(BF16)
        sg = _sigmoid(_dot(ygb, w_ref[...], 1, 0) + b_ref[...])
        do = do_ref[...]
        dgl = do * yg * sg * (1.0 - sg)
        dglb = dgl.astype(BF16)
        dyg = do * sg + _dot(dglb, w_ref[...], 1, 1)
        dy_ref[...] = dyg * dyg_dy
        dw = _dot(ygb, dglb, 0, 0)
        db = jnp.sum(dgl, axis=0, keepdims=True)

        @pl.when(i == 0)
        def _():
            dw_ref[...] = dw
            db_ref[...] = db

        @pl.when(i > 0)
        def _():
            dw_ref[...] += dw
            db_ref[...] += db

    row = pl.BlockSpec((tr, W), lambda i: (i, 0))
    full = pl.BlockSpec((W, W), lambda i: (0, 0))
    vec = pl.BlockSpec((1, W), lambda i: (0, 0))
    return pl.pallas_call(
        body, name=name,
        out_shape=(jax.ShapeDtypeStruct((S, W), F32), jax.ShapeDtypeStruct((W, W), F32), jax.ShapeDtypeStruct((1, W), F32)),
        grid=(S // tr,), in_specs=[row, row, full, vec], out_specs=(row, full, vec),
        compiler_params=_cparams("arbitrary"),
    )(y_pre, do, w_glu, b_glu)


GATE_COL0 = 3 * SB_WIDTH + SSM_WIDTH


def _merge_fwd(proj, o_attn, o_ssm, w_ba, w_bs, b_gate, *, name):
    S = proj.shape[0]
    D = D_MODEL
    tr = _pick(S, (256, 128, 64, 32, 16, 8))
    gb = GATE_COL0 // D

    def body(ga_ref, gs_ref, oa_ref, os_ref, wa_ref, ws_ref, ba_ref, bs_ref, m_ref):
        pa = _dot(oa_ref[...], wa_ref[...], 1, 0)
        ps = _dot(os_ref[...], ws_ref[...], 1, 0)
        sa = _sigmoid(ga_ref[...] + ba_ref[...])
        ss = _sigmoid(gs_ref[...] + bs_ref[...])
        m_ref[...] = (sa * pa + ss * ps).astype(BF16)

    return pl.pallas_call(
        body, name=name, out_shape=jax.ShapeDtypeStruct((S, D), BF16), grid=(S // tr,),
        in_specs=[pl.BlockSpec((tr, D), lambda i: (i, gb)), pl.BlockSpec((tr, D), lambda i: (i, gb + 1)),
                  pl.BlockSpec((tr, SB_WIDTH), lambda i: (i, 0)), pl.BlockSpec((tr, SSM_WIDTH), lambda i: (i, 0)),
                  pl.BlockSpec((SB_WIDTH, D), lambda i: (0, 0)), pl.BlockSpec((SSM_WIDTH, D), lambda i: (0, 0)),
                  pl.BlockSpec((1, D), lambda i: (0, 0)), pl.BlockSpec((1, D), lambda i: (0, 1))],
        out_specs=pl.BlockSpec((tr, D), lambda i: (i, 0)),
        compiler_params=_cparams("parallel"),
    )(proj, proj, o_attn, o_ssm, w_ba, w_bs, b_gate, b_gate)


def _merge_bwd(dmerged, proj, o_attn, o_ssm, w_ba, w_bs, b_gate, *, name):
    S = proj.shape[0]
    D = D_MODEL
    tr = _pick(S, (256, 128, 64, 32, 16, 8))
    gb = GATE_COL0 // D

    def body(dm_ref, ga_ref, gs_ref, oa_ref, os_ref, wa_ref, ws_ref, ba_ref, bs_ref,
             doa_ref, dos_ref, dg_ref, db_ref, dwa_ref, dws_ref):
        i = pl.program_id(0)
        dm = dm_ref[...]
        oa = oa_ref[...]
        osm = os_ref[...]
        pa = _dot(oa, wa_ref[...], 1, 0)
        ps = _dot(osm, ws_ref[...], 1, 0)
        sa = _sigmoid(ga_ref[...] + ba_ref[...])
        ss = _sigmoid(gs_ref[...] + bs_ref[...])
        dpa = (dm * sa).astype(BF16)
        dps = (dm * ss).astype(BF16)
        dga = dm * pa * sa * (1.0 - sa)
        dgs = dm * ps * ss * (1.0 - ss)
        dg_ref[:, :D] = dga.astype(BF16)
        dg_ref[:, D:] = dgs.astype(BF16)
        doa_ref[...] = _dot(dpa, wa_ref[...], 1, 1).astype(BF16)
        dos_ref[...] = _dot(dps, ws_ref[...], 1, 1)
        dwa = _dot(oa, dpa, 0, 0)
        dws = _dot(osm, dps, 0, 0)
        dba = jnp.sum(dga, axis=0, keepdims=True)
        dbs = jnp.sum(dgs, axis=0, keepdims=True)

        @pl.when(i == 0)
        def _():
            dwa_ref[...] = dwa
            dws_ref[...] = dws
            db_ref[:, :D] = dba
            db_ref[:, D:] = dbs

        @pl.when(i > 0)
        def _():
            dwa_ref[...] += dwa
            dws_ref[...] += dws
            db_ref[:, :D] += dba
            db_ref[:, D:] += dbs

    rowD = pl.BlockSpec((tr, D), lambda i: (i, 0))
    wspec = pl.BlockSpec((SB_WIDTH, D), lambda i: (0, 0))
    return pl.pallas_call(
        body, name=name,
        out_shape=(jax.ShapeDtypeStruct((S, SB_WIDTH), BF16), jax.ShapeDtypeStruct((S, SSM_WIDTH), F32),
                   jax.ShapeDtypeStruct((S, 2 * D), BF16), jax.ShapeDtypeStruct((1, 2 * D), F32),
                   jax.ShapeDtypeStruct((SB_WIDTH, D), F32), jax.ShapeDtypeStruct((SSM_WIDTH, D), F32)),
        grid=(S // tr,),
        in_specs=[rowD, pl.BlockSpec((tr, D), lambda i: (i, gb)), pl.BlockSpec((tr, D), lambda i: (i, gb + 1)),
                  pl.BlockSpec((tr, SB_WIDTH), lambda i: (i, 0)), pl.BlockSpec((tr, SSM_WIDTH), lambda i: (i, 0)),
                  wspec, wspec, pl.BlockSpec((1, D), lambda i: (0, 0)), pl.BlockSpec((1, D), lambda i: (0, 1))],
        out_specs=(pl.BlockSpec((tr, SB_WIDTH), lambda i: (i, 0)), pl.BlockSpec((tr, SSM_WIDTH), lambda i: (i, 0)),
                   pl.BlockSpec((tr, 2 * D), lambda i: (i, 0)), pl.BlockSpec((1, 2 * D), lambda i: (0, 0)),
                   wspec, wspec),
        compiler_params=_cparams("arbitrary"),
    )(dmerged, proj, proj, o_attn, o_ssm, w_ba, w_bs, b_gate, b_gate)


def _xattn_probs(q, k, h):
    cols = slice(h * XA_HEAD_DIM, (h + 1) * XA_HEAD_DIM)
    s = _dot(q[:, cols], k[:, cols], 1, 1) * (XA_HEAD_DIM ** -0.5)
    s = s - jnp.max(s, axis=-1, keepdims=True)
    e = jnp.exp(s)
    return e / jnp.sum(e, axis=-1, keepdims=True), cols


def _xattn_fwd(q2, k2, v2, *, name):
    S, D = q2.shape
    M = k2.shape[0]
    tr = _row_tile(S)

    def body(q_ref, k_ref, v_ref, o_ref):
        q = q_ref[...]
        k = k_ref[...]
        v = v_ref[...]
        for h in range(XA_HEADS):
            p, cols = _xattn_probs(q, k, h)
            o_ref[:, cols] = _dot(p, v[:, cols], 1, 0).astype(BF16)

    row = pl.BlockSpec((tr, D), lambda i: (i, 0))
    memb = pl.BlockSpec((M, D), lambda i: (0, 0))
    return pl.pallas_call(
        body, name=name, out_shape=jax.ShapeDtypeStruct((S, D), BF16), grid=(S // tr,),
        in_specs=[row, memb, memb], out_specs=row, compiler_params=_cparams("parallel"),
    )(q2, k2, v2)


def _xattn_bwd(q2, k2, v2, do2, *, name):
    S, D = q2.shape
    M = k2.shape[0]
    tr = _row_tile(S)
    scale = XA_HEAD_DIM ** -0.5

    def body(q_ref, k_ref, v_ref, do_ref, dq_ref, dk_ref, dv_ref):
        i = pl.program_id(0)

        @pl.when(i == 0)
        def _():
            dk_ref[...] = jnp.zeros_like(dk_ref)
            dv_ref[...] = jnp.zeros_like(dv_ref)

        q = q_ref[...]
        k = k_ref[...]
        v = v_ref[...]
        do = do_ref[...]
        for h in range(XA_HEADS):
            p, cols = _xattn_probs(q, k, h)
            dp = _dot(do[:, cols], v[:, cols], 1, 1)
            ds = (p * (dp - jnp.sum(dp * p, axis=-1, keepdims=True)) * scale).astype(BF16)
            dq_ref[:, cols] = _dot(ds, k[:, cols], 1, 0).astype(BF16)
            dk_ref[:, cols] += _dot(ds, q[:, cols], 0, 0)
            dv_ref[:, cols] += _dot(p, do[:, cols], 0, 0)

    row = pl.BlockSpec((tr, D), lambda i: (i, 0))
    memb = pl.BlockSpec((M, D), lambda i: (0, 0))
    return pl.pallas_call(
        body, name=name,
        out_shape=(jax.ShapeDtypeStruct((S, D), BF16), jax.ShapeDtypeStruct((M, D), F32), jax.ShapeDtypeStruct((M, D), F32)),
        grid=(S // tr,), in_specs=[row, memb, memb, row], out_specs=(row, memb, memb),
        compiler_params=_cparams("arbitrary"),
    )(q2, k2, v2, do2)


CONV_ROWS = 512


def _shift_down(ref, t0, rows, d):
    cur = ref[pl.ds(t0, rows), :]
    out = pltpu.roll(cur, d, 0)
    r = lax.broadcasted_iota(jnp.int32, cur.shape, 0)
    for e in range(d):
        src = t0 - d + e
        prev = ref[pl.ds(src, 1), :] if src >= 0 else jnp.zeros((1, cur.shape[1]), cur.dtype)
        out = jnp.where(r == e, prev, out)
    return out


def _shift_up(ref, t0, rows, d, total):
    cur = ref[pl.ds(t0, rows), :]
    out = pltpu.roll(cur, rows - d, 0)
    r = lax.broadcasted_iota(jnp.int32, cur.shape, 0)
    for e in range(d):
        src = t0 + rows + e
        nxt = ref[pl.ds(src, 1), :] if src < total else jnp.zeros((1, cur.shape[1]), cur.dtype)
        out = jnp.where(r == rows - d + e, nxt, out)
    return out


def _conv3(ref, w_ref, b_ref, t0, rows):
    return (w_ref[2:3, :] * ref[pl.ds(t0, rows), :] + w_ref[1:2, :] * _shift_down(ref, t0, rows, 1)
            + w_ref[0:1, :] * _shift_down(ref, t0, rows, 2) + b_ref[...])


def _convgate_fwd(up, conv_w, conv_b, *, name):
    S = up.shape[0]
    nb = D_FF // LANES
    R = min(CONV_ROWS, S)

    def body(g_ref, v_ref, wg_ref, wv_ref, bg_ref, bv_ref, a_ref):
        for t0 in range(0, S, R):
            cg = _conv3(g_ref, wg_ref, bg_ref, t0, R)
            cv = _conv3(v_ref, wv_ref, bv_ref, t0, R)
            a_ref[pl.ds(t0, R), :] = (_gelu(cg) * cv).astype(BF16)

    col = lambda off: pl.BlockSpec((S, LANES), lambda j: (0, off + j))
    wcol = lambda off: pl.BlockSpec((3, LANES), lambda j: (0, off + j))
    bcol = lambda off: pl.BlockSpec((1, LANES), lambda j: (0, off + j))
    return pl.pallas_call(
        body, name=name, out_shape=jax.ShapeDtypeStruct((S, D_FF), BF16), grid=(nb,),
        in_specs=[col(0), col(nb), wcol(0), wcol(nb), bcol(0), bcol(nb)],
        out_specs=col(0), compiler_params=_cparams("parallel"),
    )(up, up, conv_w, conv_w, conv_b, conv_b)


def _convgate_bwd(up, da, conv_w, conv_b, *, name):
    S = up.shape[0]
    nb = D_FF // LANES
    R = min(CONV_ROWS, S)

    def body(g_ref, v_ref, da_ref, wg_ref, wv_ref, bg_ref, bv_ref,
             dug_ref, duv_ref, dwg_ref, dwv_ref, dbg_ref, dbv_ref, dcg_s, dcv_s):
        zero3 = jnp.zeros((1, LANES), F32)
        acc = {"g": [zero3, zero3, zero3, zero3], "v": [zero3, zero3, zero3, zero3]}
        for t0 in range(0, S, R):
            cg = _conv3(g_ref, wg_ref, bg_ref, t0, R)
            cv = _conv3(v_ref, wv_ref, bv_ref, t0, R)
            da = da_ref[pl.ds(t0, R), :]
            gl, dgl = _gelu_and_grad(cg)
            dcg = da * cv * dgl
            dcv = da * gl
            dcg_s[pl.ds(t0, R), :] = dcg
            dcv_s[pl.ds(t0, R), :] = dcv
            for key, ref, dc in (("g", g_ref, dcg), ("v", v_ref, dcv)):
                a = acc[key]
                a[2] = a[2] + jnp.sum(dc * ref[pl.ds(t0, R), :], axis=0, keepdims=True)
                a[1] = a[1] + jnp.sum(dc * _shift_down(ref, t0, R, 1), axis=0, keepdims=True)
                a[0] = a[0] + jnp.sum(dc * _shift_down(ref, t0, R, 2), axis=0, keepdims=True)
                a[3] = a[3] + jnp.sum(dc, axis=0, keepdims=True)
        for key, dw_ref, db_ref in (("g", dwg_ref, dbg_ref), ("v", dwv_ref, dbv_ref)):
            a = acc[key]
            dw_ref[0:1, :] = a[0]
            dw_ref[1:2, :] = a[1]
            dw_ref[2:3, :] = a[2]
            db_ref[...] = a[3]
        for t0 in range(0, S, R):
            for dc_s, w_ref, du_ref in ((dcg_s, wg_ref, dug_ref), (dcv_s, wv_ref, duv_ref)):
                du = (w_ref[2:3, :] * dc_s[pl.ds(t0, R), :] + w_ref[1:2, :] * _shift_up(dc_s, t0, R, 1, S)
                      + w_ref[0:1, :] * _shift_up(dc_s, t0, R, 2, S))
                du_ref[pl.ds(t0, R), :] = du.astype(BF16)

    col = lambda off: pl.BlockSpec((S, LANES), lambda j: (0, off + j))
    wcol = lambda off: pl.BlockSpec((3, LANES), lambda j: (0, off + j))
    bcol = lambda off: pl.BlockSpec((1, LANES), lambda j: (0, off + j))
    dug, duv, dwg, dwv, dbg, dbv = pl.pallas_call(
        body, name=name,
        out_shape=(jax.ShapeDtypeStruct((S, D_FF), BF16), jax.ShapeDtypeStruct((S, D_FF), BF16),
                   jax.ShapeDtypeStruct((3, D_FF), F32), jax.ShapeDtypeStruct((3, D_FF), F32),
                   jax.ShapeDtypeStruct((1, D_FF), F32), jax.ShapeDtypeStruct((1, D_FF), F32)),
        grid=(nb,),
        in_specs=[col(0), col(nb), col(0), wcol(0), wcol(nb), bcol(0), bcol(nb)],
        out_specs=(col(0), col(0), wcol(0), wcol(0), bcol(0), bcol(0)),
        scratch_shapes=[pltpu.VMEM((S, LANES), F32), pltpu.VMEM((S, LANES), F32)],
        compiler_params=_cparams("parallel"),
    )(up, up, da, conv_w, conv_w, conv_b, conv_b)
    return dug, duv, dwg, dwv, dbg, dbv


def _local_step(x, mem, target, W, P):
    mm = _matmul
    h1 = _rms_fwd(x, P["norm_mix_pre"], name="rms_mix_pre")
    proj = mm(h1, W["w_in"], name="mm_in")
    o_attn, sb_tot = _sb_fwd(proj, name="sb_fwd")

    ssm_prep = lambda *a: _ssm_prepare(*a)
    (lam_re, lam_im, bb_re, bb_im), prep_vjp = jax.vjp(
        ssm_prep, P["ssm_a_re"], P["ssm_a_im"], P["ssm_log_dt"], P["ssm_b_re"], P["ssm_b_im"])
    tab_f, tab_b = _ssm_tables(lam_re, lam_im)
    bd_re = _bd_from_bbar(bb_re).astype(BF16)
    bd_im = _bd_from_bbar(bb_im).astype(BF16)
    cd_re = _cd_from_c(P["ssm_c_re"]).astype(BF16)
    cd_imneg = _cd_from_c(-P["ssm_c_im"]).astype(BF16)
    y_pre, x_re, x_im = _ssm_fwd(proj, bd_re, bd_im, cd_re, cd_imneg, P["ssm_d"], tab_f, name="ssm_fwd")
    o_ssm = _glu_fwd(y_pre, W["ssm_w_glu"], P["ssm_b_glu"], name="glu_fwd")

    merged = _merge_fwd(proj, o_attn, o_ssm, W["w_branch_attn"], W["w_branch_ssm"], P["b_gate"], name="merge_fwd")
    mo = mm(merged, W["w_out"], name="mm_out")
    x1, h2 = _resnorm_norm(x, mo, P["norm_mix_post"], P["norm_xa_pre"], name="resnorm_1")

    mem_n = _rms_fwd(mem, P["norm_mem"], name="rms_mem")
    q2 = mm(h2, W["xa_wq"], out_dtype=BF16, name="mm_xq")
    k2 = mm(mem_n, W["xa_wk"], out_dtype=BF16, name="mm_xk")
    v2 = mm(mem_n, W["xa_wv"], out_dtype=BF16, name="mm_xv")
    o2 = _xattn_fwd(q2, k2, v2, name="xattn_fwd")
    xa = mm(o2, W["xa_wo"], name="mm_xo")
    x2, h3 = _resnorm_norm(x1, xa, P["norm_xa_post"], P["norm_ffn_pre"], name="resnorm_2")

    up = mm(h3, W["ffn_w_up"], name="mm_up")
    act = _convgate_fwd(up, W["ffn_conv_w"], P["ffn_conv_b"], name="convgate_fwd")
    f = mm(act, W["ffn_w_down"], name="mm_down")
    loss, dy, df, dg_ffn_post = _final_loss(x2, f, P["norm_ffn_post"], target, name="final_loss")

    G = {"norm_ffn_post": dg_ffn_post}
    dact = mm(df, W["ffn_w_down"], tb=True, name="mm_down_dx")
    G["ffn_w_down"] = mm(act, df, ta=True, name="mm_down_dw")
    dug, duv, dwg, dwv, dbg, dbv = _convgate_bwd(up, dact, W["ffn_conv_w"], P["ffn_conv_b"], name="convgate_bwd")
    dup = jnp.concatenate([dug, duv], axis=1)
    G["ffn_conv_w"] = jnp.concatenate([dwg, dwv], axis=1)
    G["ffn_conv_b"] = jnp.concatenate([dbg, dbv], axis=1)
    dh3 = mm(dup, W["ffn_w_up"], tb=True, name="mm_up_dx")
    G["ffn_w_up"] = mm(h3, dup, ta=True, name="mm_up_dw")
    dx2, dxa, G["norm_ffn_pre"], G["norm_xa_post"] = _norm_bwd_pair(
        dy, dh3, x2, P["norm_ffn_pre"], xa, P["norm_xa_post"], name="norm_bwd_3")

    G["xa_wo"] = mm(o2, dxa, ta=True, name="mm_xo_dw")
    do2 = mm(dxa, W["xa_wo"], tb=True, out_dtype=BF16, name="mm_xo_dx")
    dq2, dk2, dv2 = _xattn_bwd(q2, k2, v2, do2, name="xattn_bwd")
    G["xa_wq"] = mm(h2, dq2, ta=True, name="mm_xq_dw")
    dh2 = mm(dq2, W["xa_wq"], tb=True, name="mm_xq_dx")
    G["xa_wk"] = mm(mem_n, dk2, ta=True, name="mm_xk_dw")
    G["xa_wv"] = mm(mem_n, dv2, ta=True, name="mm_xv_dw")
    dmem_n = jnp.concatenate([dk2, dv2], axis=1)
    wkv = jnp.concatenate([W["xa_wk"], W["xa_wv"]], axis=1)
    dmem = mm(dmem_n, wkv, tb=True, name="mm_xkv_dx")
    _, G["norm_mem"] = _norm_bwd_single(None, dmem, mem, P["norm_mem"], name="norm_bwd_mem")
    dx1, dmo, G["norm_xa_pre"], G["norm_mix_post"] = _norm_bwd_pair(
        dx2, dh2, x1, P["norm_xa_pre"], mo, P["norm_mix_post"], name="norm_bwd_2")

    G["w_out"] = mm(merged, dmo, ta=True, name="mm_out_dw")
    dmerged = mm(dmo, W["w_out"], tb=True, name="mm_out_dx")
    do_attn, do_ssm, dgate, G["b_gate"], G["w_branch_attn"], G["w_branch_ssm"] = _merge_bwd(
        dmerged, proj, o_attn, o_ssm, W["w_branch_attn"], W["w_branch_ssm"], P["b_gate"], name="merge_bwd")
    dy_pre, G["ssm_w_glu"], G["ssm_b_glu"] = _glu_bwd(y_pre, do_ssm, W["ssm_w_glu"], P["ssm_b_glu"], name="glu_bwd")
    du, dbd_re, dbd_im, dcd_re, dcd_imneg, G["ssm_d"], dl_re, dl_im = _ssm_bwd(
        dy_pre, proj, x_re, x_im, bd_re, bd_im, cd_re, cd_imneg, P["ssm_d"], tab_b, name="ssm_bwd")
    G["ssm_c_re"] = _c_from_cd(dcd_re)
    G["ssm_c_im"] = -_c_from_cd(dcd_imneg)
    dlam_re = jnp.sum(dl_re, axis=1).reshape(SSM_GROUPS, SSM_STATE)
    dlam_im = jnp.sum(dl_im, axis=1).reshape(SSM_GROUPS, SSM_STATE)
    (G["ssm_a_re"], G["ssm_a_im"], G["ssm_log_dt"], G["ssm_b_re"], G["ssm_b_im"]) = prep_vjp(
        (dlam_re, dlam_im, _bbar_from_bd(dbd_re), _bbar_from_bd(dbd_im)))
    dq, dk, dv = _sb_bwd(proj, sb_tot, do_attn, name="sb_bwd")
    dproj = jnp.concatenate([dq, dk, dv, du, dgate], axis=1)
    G["w_in"] = mm(h1, dproj, ta=True, name="mm_in_dw")
    dh1 = mm(dproj, W["w_in"], tb=True, name="mm_in_dx")
    grad_x, G["norm_mix_pre"] = _norm_bwd_single(dx1, dh1, x, P["norm_mix_pre"], name="norm_bwd_1")
    return loss, grad_x, G


MESH = pl.DeviceIdType.MESH
_HBM = pl.BlockSpec(memory_space=pl.ANY)
N_XY = 4
N_XY_PEERS = 3


def _xy_peers(x, y):
    return [(1 - x, y), (x, 1 - y), (1 - x, 1 - y)]


def _gather_xy(src, *, name):
    def body(src_ref, out_ref, send_sems, recv_sems, local_sem):
        x, y, c = lax.axis_index("x"), lax.axis_index("y"), lax.axis_index("c")
        mine = 2 * x + y
        local = pltpu.make_async_copy(src_ref, out_ref.at[mine], local_sem)
        local.start()
        peers = _xy_peers(x, y)
        sends = [pltpu.make_async_remote_copy(src_ref=src_ref, dst_ref=out_ref.at[mine], send_sem=send_sems.at[j],
                                              recv_sem=recv_sems.at[j], device_id=(px, py, c), device_id_type=MESH)
                 for j, (px, py) in enumerate(peers)]
        for cp in sends:
            cp.start()
        for j, (px, py) in enumerate(peers):
            pltpu.make_async_remote_copy(src_ref=src_ref, dst_ref=out_ref.at[2 * px + py], send_sem=send_sems.at[j],
                                         recv_sem=recv_sems.at[j], device_id=(px, py, c), device_id_type=MESH).wait_recv()
        for cp in sends:
            cp.wait_send()
        local.wait()

    return pl.pallas_call(
        body, name=name, out_shape=jax.ShapeDtypeStruct((N_XY,) + src.shape, src.dtype),
        in_specs=[_HBM], out_specs=_HBM,
        scratch_shapes=[pltpu.SemaphoreType.DMA((N_XY_PEERS,)), pltpu.SemaphoreType.DMA((N_XY_PEERS,)),
                        pltpu.SemaphoreType.DMA],
    )(src)


def _gather_c(src, *, name):
    def body(src_ref, out_ref, send_sem, recv_sem, local_sem):
        x, y, c = lax.axis_index("x"), lax.axis_index("y"), lax.axis_index("c")
        local = pltpu.make_async_copy(src_ref, out_ref.at[c], local_sem)
        local.start()
        send = pltpu.make_async_remote_copy(src_ref=src_ref, dst_ref=out_ref.at[c], send_sem=send_sem, recv_sem=recv_sem,
                                            device_id=(x, y, 1 - c), device_id_type=MESH)
        send.start()
        pltpu.make_async_remote_copy(src_ref=src_ref, dst_ref=out_ref.at[1 - c], send_sem=send_sem, recv_sem=recv_sem,
                                     device_id=(x, y, 1 - c), device_id_type=MESH).wait_recv()
        send.wait_send()
        local.wait()

    return pl.pallas_call(
        body, name=name, out_shape=jax.ShapeDtypeStruct((2,) + src.shape, src.dtype),
        in_specs=[_HBM], out_specs=_HBM,
        scratch_shapes=[pltpu.SemaphoreType.DMA, pltpu.SemaphoreType.DMA, pltpu.SemaphoreType.DMA],
    )(src)


def _send_c(src, *, name):
    def body(src_ref, out_ref, send_sem, recv_sem):
        x, y, c = lax.axis_index("x"), lax.axis_index("y"), lax.axis_index("c")
        send = pltpu.make_async_remote_copy(src_ref=src_ref.at[1 - c], dst_ref=out_ref, send_sem=send_sem,
                                            recv_sem=recv_sem, device_id=(x, y, 1 - c), device_id_type=MESH)
        send.start()
        send.wait_recv()
        send.wait_send()

    return pl.pallas_call(
        body, name=name, out_shape=jax.ShapeDtypeStruct(src.shape[1:], src.dtype),
        in_specs=[_HBM], out_specs=_HBM,
        scratch_shapes=[pltpu.SemaphoreType.DMA, pltpu.SemaphoreType.DMA],
    )(src)


def _scatter_xy(src, *, name):
    def body(src_ref, out_ref, send_sems, recv_sems):
        x, y, c = lax.axis_index("x"), lax.axis_index("y"), lax.axis_index("c")
        sends = [pltpu.make_async_remote_copy(src_ref=src_ref.at[2 * px + py], dst_ref=out_ref.at[j],
                                              send_sem=send_sems.at[j], recv_sem=recv_sems.at[j],
                                              device_id=(px, py, c), device_id_type=MESH)
                 for j, (px, py) in enumerate(_xy_peers(x, y))]
        for cp in sends:
            cp.start()
        for cp in sends:
            cp.wait_recv()
        for cp in sends:
            cp.wait_send()

    return pl.pallas_call(
        body, name=name, out_shape=jax.ShapeDtypeStruct((N_XY_PEERS,) + src.shape[1:], src.dtype),
        in_specs=[_HBM], out_specs=_HBM,
        scratch_shapes=[pltpu.SemaphoreType.DMA((N_XY_PEERS,)), pltpu.SemaphoreType.DMA((N_XY_PEERS,))],
    )(src)


PACK_COLS = 1024


def _pair_sum(g8, recv, core, *, name):
    _, n, R, C = g8.shape
    tr = _pick(R, (128, 64, 32, 16, 8))

    def body(core_ref, a_ref, b_ref, o_ref):
        o_ref[...] = a_ref[0] + b_ref[...]

    return pl.pallas_call(
        body, name=name, out_shape=jax.ShapeDtypeStruct((n, R, C), F32),
        grid_spec=pltpu.PrefetchScalarGridSpec(
            num_scalar_prefetch=1, grid=(n, R // tr),
            in_specs=[pl.BlockSpec((1, 1, tr, C), lambda s, i, core_ref: (core_ref[0], s, i, 0)),
                      pl.BlockSpec((1, tr, C), lambda s, i, core_ref: (s, i, 0))],
            out_specs=pl.BlockSpec((1, tr, C), lambda s, i, core_ref: (s, i, 0))),
        compiler_params=_cparams("parallel", "parallel"),
    )(core, g8, recv)


def _adamw_math(w, g, m, v):
    m = ADAM_B1 * m + (1.0 - ADAM_B1) * g
    v = ADAM_B2 * v + (1.0 - ADAM_B2) * (g * g)
    m_hat = m / (1.0 - ADAM_B1 ** ADAM_STEP)
    v_hat = v / (1.0 - ADAM_B2 ** ADAM_STEP)
    delta = -ADAM_LR * (m_hat / (jnp.sqrt(v_hat) + ADAM_EPS) + ADAM_WD * w)
    return delta, m, v


def _reduce_adamw(parts, w, m, v, *, own=None, own_slot=None, name):
    n, R, C = parts.shape
    tr = _pick(R, (128, 64, 32, 16, 8))
    has_own = own is not None

    def body(*refs):
        if has_own:
            _, own_ref, parts_ref, w_ref, m_ref, v_ref, g_ref, d_ref, nm_ref, nv_ref = refs
            g = own_ref[0]
            first = 0
        else:
            parts_ref, w_ref, m_ref, v_ref, g_ref, d_ref, nm_ref, nv_ref = refs
            g = parts_ref[0]
            first = 1
        for k in range(first, n):
            g = g + parts_ref[k]
        g_ref[...] = g
        d_ref[...], nm_ref[...], nv_ref[...] = _adamw_math(w_ref[...], g, m_ref[...], v_ref[...])

    out = jax.ShapeDtypeStruct((R, C), F32)
    if has_own:
        row = pl.BlockSpec((tr, C), lambda i, s: (i, 0))
        return pl.pallas_call(
            body, name=name, out_shape=(out, out, out, out),
            grid_spec=pltpu.PrefetchScalarGridSpec(
                num_scalar_prefetch=1, grid=(R // tr,),
                in_specs=[pl.BlockSpec((1, tr, C), lambda i, s: (s[0], i, 0)),
                          pl.BlockSpec((n, tr, C), lambda i, s: (0, i, 0)), row, row, row],
                out_specs=(row, row, row, row)),
            compiler_params=_cparams("parallel"),
        )(own_slot, own, parts, w, m, v)
    row = pl.BlockSpec((tr, C), lambda i: (i, 0))
    return pl.pallas_call(
        body, name=name, out_shape=(out, out, out, out), grid=(R // tr,),
        in_specs=[pl.BlockSpec((n, tr, C), lambda i: (0, i, 0)), row, row, row],
        out_specs=(row, row, row, row), compiler_params=_cparams("parallel"),
    )(parts, w, m, v)


SHARDED = (("w_in", (1024, 4096), 1), ("ssm_w_glu", (512, 512), 0), ("w_branch_attn", (512, 1024), 1),
           ("w_branch_ssm", (512, 1024), 1), ("w_out", (1024, 1024), 0), ("xa_wq", (1024, 1024), 0),
           ("xa_wk", (1024, 1024), 0), ("xa_wv", (1024, 1024), 0), ("xa_wo", (1024, 1024), 0),
           ("ffn_w_up", (1024, 5632), 1), ("ffn_conv_w", (3, 5632), 1), ("ffn_w_down", (2816, 1024), 0))
REPLICATED = (("norm_mix_pre", (1024,)), ("norm_mix_post", (1024,)), ("b_gate", (2048,)), ("ssm_a_re", (32, 64)),
              ("ssm_a_im", (32, 64)), ("ssm_log_dt", (32,)), ("ssm_b_re", (32, 64, 16)), ("ssm_b_im", (32, 64, 16)),
              ("ssm_c_re", (32, 16, 64)), ("ssm_c_im", (32, 16, 64)), ("ssm_d", (512,)), ("ssm_b_glu", (512,)),
              ("norm_xa_pre", (1024,)), ("norm_xa_post", (1024,)), ("norm_mem", (1024,)), ("norm_ffn_pre", (1024,)),
              ("norm_ffn_post", (1024,)), ("ffn_conv_b", (5632,)))
PARAM_ORDER = ("norm_mix_pre", "norm_mix_post", "w_in", "b_gate", "ssm_a_re", "ssm_a_im", "ssm_log_dt", "ssm_b_re",
               "ssm_b_im", "ssm_c_re", "ssm_c_im", "ssm_d", "ssm_w_glu", "ssm_b_glu", "w_branch_attn", "w_branch_ssm",
               "w_out", "norm_xa_pre", "norm_xa_post", "norm_mem", "xa_wq", "xa_wk", "xa_wv", "xa_wo", "norm_ffn_pre",
               "norm_ffn_post", "ffn_w_up", "ffn_conv_w", "ffn_conv_b", "ffn_w_down")
BIG_ROW_UNIT = 128
SMALL_ROW_UNIT = 32


def _local_shape(shape, axis):
    return tuple(s // N_DEV if a == axis else s for a, s in enumerate(shape))


def _pad_to(n, unit):
    return -(-n // unit) * unit


def _pack_rows(flat_parts, row_unit):
    padded = []
    rows = 0
    for p in flat_parts:
        n = _pad_to(p.shape[-1], PACK_COLS)
        padded.append(jnp.pad(p, [(0, 0)] * (p.ndim - 1) + [(0, n - p.shape[-1])]))
        rows += n // PACK_COLS
    total = _pad_to(rows, row_unit)
    if total > rows:
        lead = flat_parts[0].shape[:-1]
        padded.append(jnp.zeros(lead + ((total - rows) * PACK_COLS,), flat_parts[0].dtype))
    buf = jnp.concatenate(padded, axis=-1)
    return buf.reshape(buf.shape[:-1] + (total, PACK_COLS))


def _unpack_rows(buf, sizes):
    flat = buf.reshape(buf.shape[:-2] + (-1,))
    out, off = [], 0
    for n in sizes:
        out.append(flat[..., off:off + n])
        off += _pad_to(n, PACK_COLS)
    return out


def _to_owner_major(full, axis):
    r, c = full.shape
    if axis == 0:
        return full.reshape(N_DEV, (r // N_DEV) * c)
    return full.reshape(r, N_DEV, c // N_DEV).transpose(1, 0, 2).reshape(N_DEV, r * (c // N_DEV))


def _from_owner_major(rows8, shape, axis):
    r, c = shape
    if axis == 0:
        return rows8.reshape(r, c)
    return rows8.reshape(N_DEV, r, c // N_DEV).transpose(1, 0, 2).reshape(r, c)


def _gathered_to_owner_major(g):
    return jnp.swapaxes(g, 0, 1).reshape((N_DEV,) + g.shape[2:])


def kernel(x, mem, norm_mix_pre, norm_mix_post, w_in, b_gate, ssm_a_re, ssm_a_im, ssm_log_dt, ssm_b_re, ssm_b_im, ssm_c_re, ssm_c_im, ssm_d, ssm_w_glu, ssm_b_glu, w_branch_attn, w_branch_ssm, w_out, norm_xa_pre, norm_xa_post, norm_mem, xa_wq, xa_wk, xa_wv, xa_wo, norm_ffn_pre, norm_ffn_post, ffn_w_up, ffn_conv_w, ffn_conv_b, ffn_w_down, loss_target, m_norm_mix_pre, m_norm_mix_post, m_w_in, m_b_gate, m_ssm_a_re, m_ssm_a_im, m_ssm_log_dt, m_ssm_b_re, m_ssm_b_im, m_ssm_c_re, m_ssm_c_im, m_ssm_d, m_ssm_w_glu, m_ssm_b_glu, m_w_branch_attn, m_w_branch_ssm, m_w_out, m_norm_xa_pre, m_norm_xa_post, m_norm_mem, m_xa_wq, m_xa_wk, m_xa_wv, m_xa_wo, m_norm_ffn_pre, m_norm_ffn_post, m_ffn_w_up, m_ffn_conv_w, m_ffn_conv_b, m_ffn_w_down, v_norm_mix_pre, v_norm_mix_post, v_w_in, v_b_gate, v_ssm_a_re, v_ssm_a_im, v_ssm_log_dt, v_ssm_b_re, v_ssm_b_im, v_ssm_c_re, v_ssm_c_im, v_ssm_d, v_ssm_w_glu, v_ssm_b_glu, v_w_branch_attn, v_w_branch_ssm, v_w_out, v_norm_xa_pre, v_norm_xa_post, v_norm_mem, v_xa_wq, v_xa_wk, v_xa_wv, v_xa_wo, v_norm_ffn_pre, v_norm_ffn_post, v_ffn_w_up, v_ffn_conv_w, v_ffn_conv_b, v_ffn_w_down):
    args = dict(locals())
    w_loc = {n: args[n][0] for n in PARAM_ORDER}
    m_loc = {n: args["m_" + n][0] for n in PARAM_ORDER}
    v_loc = {n: args["v_" + n][0] for n in PARAM_ORDER}
    core = lax.axis_index("c").astype(jnp.int32).reshape(1)
    chip = (2 * lax.axis_index("x") + lax.axis_index("y")).astype(jnp.int32).reshape(1)

    def as_wire(name, a):
        if name == "ffn_conv_w":
            return lax.bitcast_convert_type(a, BF16).reshape(-1)
        return a.astype(BF16).reshape(-1)

    wire_sizes = [(2 if n == "ffn_conv_w" else 1) * math.prod(_local_shape(s, ax)) for n, s, ax in SHARDED]
    wire = _pack_rows([as_wire(n, w_loc[n]) for n, _, _ in SHARDED], BIG_ROW_UNIT)
    wire8 = _gathered_to_owner_major(_gather_c(_gather_xy(wire, name="gather_w_xy"), name="gather_w_c"))
    W = {}
    for (n, shape, ax), piece in zip(SHARDED, _unpack_rows(wire8, wire_sizes)):
        if n == "ffn_conv_w":
            piece = lax.bitcast_convert_type(piece.reshape(N_DEV, -1, 2), F32)
        W[n] = _from_owner_major(piece, shape, ax)

    P = {}
    for n, shape in REPLICATED:
        P[n] = w_loc[n] if len(shape) > 1 or n == "ssm_log_dt" else w_loc[n].reshape(1, -1)

    loss, grad_x, G = _local_step(x[0], mem[0], loss_target[0], W, P)
    loss = lax.psum(loss[0, 0], ("x", "y", "c"))

    big_sizes = [math.prod(_local_shape(s, ax)) for _, s, ax in SHARDED]
    g8 = _pack_rows([_to_owner_major(G[n], ax) for n, _, ax in SHARDED], BIG_ROW_UNIT)
    g8 = jnp.swapaxes(g8.reshape((N_XY, 2) + g8.shape[1:]), 0, 1)
    from_core = _send_c(g8, name="reduce_c")
    pair = _pair_sum(g8, from_core, core, name="pair_sum")
    from_chips = _scatter_xy(pair, name="reduce_xy")
    pk = lambda d: _pack_rows([d[n].reshape(-1) for n, _, _ in SHARDED], BIG_ROW_UNIT)
    big_out = _reduce_adamw(from_chips, pk(w_loc), pk(m_loc), pk(v_loc), own=pair, own_slot=chip, name="adamw_sharded")
    big_out = [dict(zip([n for n, _, _ in SHARDED], _unpack_rows(b, big_sizes))) for b in big_out]

    small_sizes = [math.prod(s) for _, s in REPLICATED]
    spk = lambda d: _pack_rows([d[n].reshape(-1) for n, _ in REPLICATED], SMALL_ROW_UNIT)
    parts = _gather_c(_gather_xy(spk(G), name="gather_g_xy"), name="gather_g_c")
    parts = parts.reshape((N_DEV,) + parts.shape[2:])
    small_out = _reduce_adamw(parts, spk(w_loc), spk(m_loc), spk(v_loc), name="adamw_replicated")
    small_out = [dict(zip([n for n, _ in REPLICATED], _unpack_rows(b, small_sizes))) for b in small_out]

    outs = [loss, grad_x[None]]
    for k in range(4):
        for n in PARAM_ORDER:
            src = big_out[k] if n in big_out[k] else small_out[k]
            outs.append(src[n].reshape(args[n].shape))
    return tuple(outs)
```

```python
import math

import jax
import jax.numpy as jnp
from jax import lax
from jax.experimental import pallas as pl
from jax.experimental.pallas import tpu as pltpu

F32 = jnp.float32
BF16 = jnp.bfloat16

D_MODEL = 1024
SB_HEADS = 8
SB_HEAD_DIM = 64
SB_WIDTH = 512
SSM_WIDTH = 512
SSM_GROUP = 16
SSM_GROUPS = 32
SSM_STATE = 64
XA_HEADS = 4
XA_HEAD_DIM = 256
D_FF = 2816
RMS_EPS = 1e-6
IN_WIDTH = 4096
N_DEV = 8

ADAM_LR = 0.001
ADAM_B1 = 0.9
ADAM_B2 = 0.999
ADAM_EPS = 1e-08
ADAM_WD = 0.01
ADAM_STEP = 10

LANES = 128
SUBLANES = 8
VMEM_LIMIT = 48 * 1024 * 1024

_GELU_C = math.sqrt(2.0 / math.pi)


def _cparams(*sem):
    return pltpu.CompilerParams(dimension_semantics=sem, vmem_limit_bytes=VMEM_LIMIT)


def _pick(n, cands):
    for c in cands:
        if n % c == 0:
            return c
    return n


def _gelu(x):
    return 0.5 * x * (1.0 + jnp.tanh(_GELU_C * (x + 0.044715 * x * x * x)))


def _gelu_and_grad(x):
    t = jnp.tanh(_GELU_C * (x + 0.044715 * x * x * x))
    g = 0.5 * x * (1.0 + t)
    dg = 0.5 * (1.0 + t) + 0.5 * x * (1.0 - t * t) * _GELU_C * (1.0 + 3.0 * 0.044715 * x * x)
    return g, dg


def _sigmoid(x):
    return 1.0 / (1.0 + jnp.exp(-x))


def _dot(a, b, ca, cb):
    return lax.dot_general(a.astype(BF16), b.astype(BF16), (((ca,), (cb,)), ((), ())),
                           preferred_element_type=F32)


def _matmul(a, b, *, ta=False, tb=False, out_dtype=F32, name, b_block0=0, n_blocks=None,
            out_cb=None, out_into=None, out_block0=0, acc_in=None):
    if ta:
        K, M = a.shape
    else:
        M, K = a.shape
    b_cb = None
    if b.ndim == 3:
        b_cb = b.shape[2]
        n_blocks = b.shape[0] - b_block0 if n_blocks is None else n_blocks
        N, K2 = (b.shape[1], n_blocks * b_cb) if tb else (n_blocks * b_cb, b.shape[1])
    elif tb:
        N, K2 = b.shape
    else:
        K2, N = b.shape
    assert K == K2, (a.shape, b.shape, ta, tb)
    if out_into is not None:
        out_cb = out_into.shape[2]
    tm = _pick(M, (512, 256, 128))
    n_unit = math.gcd(N, math.gcd(b_cb if (b_cb and not tb) else N, out_cb or N))
    tn = _pick(n_unit, (768, 512, 256, 128))
    k_unit = b_cb if (b_cb and tb) else K
    tk = _pick(k_unit, (1024, 768, 512, 256, 128))
    nk = K // tk
    ca, cb = (0 if ta else 1), (1 if tb else 0)
    has_acc = acc_in is not None
    has_into = out_into is not None

    def body(*refs):
        a_ref, b_ref = refs[0], refs[1]
        pos = 2
        c_ref = None
        if has_acc:
            c_ref = refs[pos]
            pos += 1
        if has_into:
            pos += 1
        o_ref = refs[pos]
        p = _dot(a_ref[...], b_ref[...], ca, cb)
        if nk == 1:
            o_ref[...] = ((p + c_ref[...]) if has_acc else p).astype(out_dtype)
        else:
            acc_ref = refs[pos + 1]
            k = pl.program_id(2)

            @pl.when(k == 0)
            def _():
                acc_ref[...] = (p + c_ref[...]) if has_acc else p

            @pl.when(k > 0)
            def _():
                acc_ref[...] += p

            @pl.when(k == nk - 1)
            def _():
                o_ref[...] = acc_ref[...].astype(out_dtype)

    a_spec = pl.BlockSpec((tk, tm), lambda j, i, k: (k, i)) if ta else pl.BlockSpec((tm, tk), lambda j, i, k: (i, k))
    if b_cb is None:
        b_spec = pl.BlockSpec((tn, tk), lambda j, i, k: (j, k)) if tb else pl.BlockSpec((tk, tn), lambda j, i, k: (k, j))
    elif tb:
        per = b_cb // tk
        b_spec = pl.BlockSpec((None, tn, tk), lambda j, i, k: (b_block0 + k // per, j, k % per))
    else:
        per = b_cb // tn
        b_spec = pl.BlockSpec((None, tk, tn), lambda j, i, k: (b_block0 + j // per, k, j % per))
    in_specs = [a_spec, b_spec]
    operands = [a, b]
    aliases = {}
    if has_acc:
        in_specs.append(pl.BlockSpec((tm, tn), lambda j, i, k: (i, j)))
        operands.append(acc_in)
    if has_into:
        aliases = {len(operands): 0}
        in_specs.append(pl.BlockSpec(memory_space=pl.ANY))
        operands.append(out_into)
    if out_cb is None:
        out_shape = jax.ShapeDtypeStruct((M, N), out_dtype)
        out_spec = pl.BlockSpec((tm, tn), lambda j, i, k: (i, j))
    else:
        per_o = out_cb // tn
        out_shape = (jax.ShapeDtypeStruct(out_into.shape, out_into.dtype) if has_into
                     else jax.ShapeDtypeStruct((N // out_cb, M, out_cb), out_dtype))
        out_spec = pl.BlockSpec((None, tm, tn), lambda j, i, k: (out_block0 + j // per_o, i, j % per_o))
    return pl.pallas_call(
        body, name=name, out_shape=out_shape,
        grid=(N // tn, M // tm, nk),
        in_specs=in_specs, out_specs=out_spec, input_output_aliases=aliases,
        scratch_shapes=[] if nk == 1 else [pltpu.VMEM((tm, tn), F32)],
        compiler_params=_cparams("parallel", "parallel", "arbitrary"),
    )(*operands)


def _rms(x, g):
    r = lax.rsqrt(jnp.mean(x * x, axis=-1, keepdims=True) + RMS_EPS)
    return x * r * g


def _rms_bwd(dy, x, g):
    r = lax.rsqrt(jnp.mean(x * x, axis=-1, keepdims=True) + RMS_EPS)
    xh = x * r
    dxh = dy * g
    dx = r * (dxh - xh * jnp.mean(dxh * xh, axis=-1, keepdims=True))
    dg = jnp.sum(dy * xh, axis=0, keepdims=True)
    return dx, dg


def _row_tile(rows):
    return _pick(rows, (512, 256, 128, 64, 32, 16, 8))


def _rms_fwd(x, g, *, name):
    R, D = x.shape
    tr = _row_tile(R)

    def body(x_ref, g_ref, h_ref):
        h_ref[...] = _rms(x_ref[...], g_ref[...]).astype(BF16)

    return pl.pallas_call(
        body, name=name, out_shape=jax.ShapeDtypeStruct((R, D), BF16), grid=(R // tr,),
        in_specs=[pl.BlockSpec((tr, D), lambda i: (i, 0)), pl.BlockSpec((1, D), lambda i: (0, 0))],
        out_specs=pl.BlockSpec((tr, D), lambda i: (i, 0)),
        compiler_params=_cparams("parallel"),
    )(x, g)


def _resnorm_norm(x, z, g_post, g_next, *, name):
    R, D = x.shape
    tr = _row_tile(R)

    def body(x_ref, z_ref, gp_ref, gn_ref, xn_ref, h_ref):
        xn = x_ref[...] + _rms(z_ref[...], gp_ref[...])
        xn_ref[...] = xn
        h_ref[...] = _rms(xn, gn_ref[...]).astype(BF16)

    row = pl.BlockSpec((tr, D), lambda i: (i, 0))
    vec = pl.BlockSpec((1, D), lambda i: (0, 0))
    return pl.pallas_call(
        body, name=name,
        out_shape=(jax.ShapeDtypeStruct((R, D), F32), jax.ShapeDtypeStruct((R, D), BF16)),
        grid=(R // tr,), in_specs=[row, row, vec, vec], out_specs=(row, row),
        compiler_params=_cparams("parallel"),
    )(x, z, g_post, g_next)


def _final_loss(x, z, g_post, target, *, name):
    R, D = x.shape
    tr = _row_tile(R)

    def body(x_ref, z_ref, gp_ref, t_ref, loss_ref, dy_ref, dz_ref, dg_ref):
        i = pl.program_id(0)
        z = z_ref[...]
        g = gp_ref[...]
        err = x_ref[...] + _rms(z, g) - t_ref[...]
        dy = err * (1.0 / D)
        dy_ref[...] = dy
        dz, dg = _rms_bwd(dy, z, g)
        dz_ref[...] = dz.astype(BF16)
        part = 0.5 * jnp.sum(jnp.sum(err * err, axis=-1, keepdims=True) * (1.0 / D), axis=0, keepdims=True)

        @pl.when(i == 0)
        def _():
            loss_ref[...] = part
            dg_ref[...] = dg

        @pl.when(i > 0)
        def _():
            loss_ref[...] += part
            dg_ref[...] += dg

    row = pl.BlockSpec((tr, D), lambda i: (i, 0))
    vec = pl.BlockSpec((1, D), lambda i: (0, 0))
    return pl.pallas_call(
        body, name=name,
        out_shape=(jax.ShapeDtypeStruct((1, 1), F32), jax.ShapeDtypeStruct((R, D), F32),
                   jax.ShapeDtypeStruct((R, D), BF16), jax.ShapeDtypeStruct((1, D), F32)),
        grid=(R // tr,), in_specs=[row, row, vec, row],
        out_specs=(pl.BlockSpec((1, 1), lambda i: (0, 0)), row, row, vec),
        compiler_params=_cparams("arbitrary"),
    )(x, z, g_post, target)


def _norm_bwd_pair(dres, dh, xk, g_pre, zprev, g_prev_post, *, name):
    R, D = xk.shape
    tr = _row_tile(R)

    def body(dres_ref, dh_ref, x_ref, gpre_ref, z_ref, gpost_ref, dx_ref, dz_ref, dgpre_ref, dgpost_ref):
        i = pl.program_id(0)
        d1, dgpre = _rms_bwd(dh_ref[...], x_ref[...], gpre_ref[...])
        dx = dres_ref[...] + d1
        dx_ref[...] = dx
        dz, dgpost = _rms_bwd(dx, z_ref[...], gpost_ref[...])
        dz_ref[...] = dz.astype(BF16)

        @pl.when(i == 0)
        def _():
            dgpre_ref[...] = dgpre
            dgpost_ref[...] = dgpost

        @pl.when(i > 0)
        def _():
            dgpre_ref[...] += dgpre
            dgpost_ref[...] += dgpost

    row = pl.BlockSpec((tr, D), lambda i: (i, 0))
    vec = pl.BlockSpec((1, D), lambda i: (0, 0))
    return pl.pallas_call(
        body, name=name,
        out_shape=(jax.ShapeDtypeStruct((R, D), F32), jax.ShapeDtypeStruct((R, D), BF16),
                   jax.ShapeDtypeStruct((1, D), F32), jax.ShapeDtypeStruct((1, D), F32)),
        grid=(R // tr,), in_specs=[row, row, row, vec, row, vec], out_specs=(row, row, vec, vec),
        compiler_params=_cparams("arbitrary"),
    )(dres, dh, xk, g_pre, zprev, g_prev_post)


def _norm_bwd_single(dres, dh, xk, g_pre, *, name):
    R, D = xk.shape
    tr = _row_tile(R)
    has_res = dres is not None

    def body(*refs):
        if has_res:
            dres_ref, dh_ref, x_ref, gpre_ref, dx_ref, dgpre_ref = refs
        else:
            dh_ref, x_ref, gpre_ref, dx_ref, dgpre_ref = refs
        i = pl.program_id(0)
        d1, dgpre = _rms_bwd(dh_ref[...], x_ref[...], gpre_ref[...])
        dx_ref[...] = dres_ref[...] + d1 if has_res else d1

        @pl.when(i == 0)
        def _():
            dgpre_ref[...] = dgpre

        @pl.when(i > 0)
        def _():
            dgpre_ref[...] += dgpre

    row = pl.BlockSpec((tr, D), lambda i: (i, 0))
    vec = pl.BlockSpec((1, D), lambda i: (0, 0))
    ins = ([dres] if has_res else []) + [dh, xk, g_pre]
    return pl.pallas_call(
        body, name=name,
        out_shape=(jax.ShapeDtypeStruct((R, D), F32), jax.ShapeDtypeStruct((1, D), F32)),
        grid=(R // tr,), in_specs=([row] if has_res else []) + [row, row, vec], out_specs=(row, vec),
        compiler_params=_cparams("arbitrary"),
    )(*ins)


SB_BLOCK = 256


def _sb_tri(kind):
    r = lax.broadcasted_iota(jnp.int32, (SB_BLOCK, SB_BLOCK), 0)
    c = lax.broadcasted_iota(jnp.int32, (SB_BLOCK, SB_BLOCK), 1)
    keep = {"after": r > c, "upto": r <= c, "before": r < c}[kind]
    return jnp.where(keep, 1.0, 0.0).astype(BF16)


def _running_sum(vals, tri):
    hi = vals.astype(BF16)
    lo = (vals - hi.astype(F32)).astype(BF16)
    return _dot(hi, tri, 1, 0) + _dot(lo, tri, 1, 0)


def _sb_scores(qm, k_blk, i, j):
    T = SB_BLOCK
    z = _dot(qm, k_blk, 1, 1)
    r = lax.broadcasted_iota(jnp.int32, (T, T), 0)
    c = lax.broadcasted_iota(jnp.int32, (T, T), 1)
    causal = (c - r) < (i - j) * T
    sp = jnp.maximum(z, 0.0) + jnp.log(1.0 + jnp.exp(-jnp.abs(z)))
    return z, sp, causal


def _head_masks():
    lane = lax.broadcasted_iota(jnp.int32, (1, LANES), 1)
    return [jnp.where(lane < SB_HEAD_DIM, 1.0, 0.0), jnp.where(lane >= SB_HEAD_DIM, 1.0, 0.0)]


def _sb_fwd(proj, *, name):
    S = proj.shape[0]
    T = SB_BLOCK
    nq = S // T
    npair = SB_WIDTH // LANES
    scale = SB_HEAD_DIM ** -0.5

    def body(q_ref, k_ref, v_ref, o_ref, tot_ref, acc_ref, run_ref):
        masks = _head_masks()
        tri = _sb_tri("after")

        def q_block(i, _):
            qrow = pl.ds(pl.multiple_of(i * T, T), T)
            q = q_ref[qrow, :] * scale
            qm = [(q * m).astype(BF16) for m in masks]
            acc_ref[...] = jnp.zeros_like(acc_ref)
            run_ref[...] = jnp.zeros_like(run_ref)

            def k_block(jj, _):
                j = i - jj
                krow = pl.ds(pl.multiple_of(j * T, T), T)
                k_blk = k_ref[krow, :].astype(BF16)
                v_blk = v_ref[krow, :].astype(BF16)
                for h in range(2):
                    z, sp, causal = _sb_scores(qm[h], k_blk, i, j)
                    lf = jnp.where(causal, -sp, 0.0)
                    ls = _running_sum(lf, tri)
                    w = jnp.where(causal, jnp.exp(z - sp + ls + run_ref[h]), 0.0)
                    acc_ref[h] += _dot(w, v_blk, 1, 0)
                    run_ref[h] += jnp.sum(lf, axis=1, keepdims=True)
                return 0

            lax.fori_loop(0, i + 1, k_block, 0)
            o_ref[qrow, :] = (acc_ref[0] * masks[0] + acc_ref[1] * masks[1]).astype(BF16)
            tot_ref[qrow, :] = run_ref[0] * masks[0] + run_ref[1] * masks[1]
            return 0

        lax.fori_loop(0, nq, q_block, 0)

    blk = lambda off: pl.BlockSpec((S, LANES), lambda p: (0, off + p))
    return pl.pallas_call(
        body, name=name,
        out_shape=(jax.ShapeDtypeStruct((S, SB_WIDTH), BF16), jax.ShapeDtypeStruct((S, SB_WIDTH), F32)),
        grid=(npair,),
        in_specs=[blk(0), blk(npair), blk(2 * npair)],
        out_specs=(blk(0), blk(0)),
        scratch_shapes=[pltpu.VMEM((2, T, LANES), F32), pltpu.VMEM((2, T, 1), F32)],
        compiler_params=_cparams("parallel"),
    )(proj, proj, proj)


def _sb_bwd(proj, tot, do_attn, *, name):
    S = proj.shape[0]
    T = SB_BLOCK
    nq = S // T
    npair = SB_WIDTH // LANES
    scale = SB_HEAD_DIM ** -0.5

    def body(q_ref, k_ref, v_ref, tot_ref, do_ref, dq_ref, dk_ref, dv_ref,
             dqacc_ref, dkacc_ref, dvacc_ref, run_ref, grun_ref):
        masks = _head_masks()
        tri_upto = _sb_tri("upto")
        tri_before = _sb_tri("before")
        dkacc_ref[...] = jnp.zeros_like(dkacc_ref)
        dvacc_ref[...] = jnp.zeros_like(dvacc_ref)

        def q_block(i, _):
            qrow = pl.ds(pl.multiple_of(i * T, T), T)
            q = q_ref[qrow, :] * scale
            do = do_ref[qrow, :].astype(F32)
            tot = tot_ref[qrow, :]
            qm = [(q * m).astype(BF16) for m in masks]
            dom = [(do * m).astype(BF16) for m in masks]
            ltot = [jnp.sum(tot * m, axis=1, keepdims=True) * (1.0 / SB_HEAD_DIM) for m in masks]
            dqacc_ref[...] = jnp.zeros_like(dqacc_ref)
            run_ref[...] = jnp.zeros_like(run_ref)
            grun_ref[...] = jnp.zeros_like(grun_ref)

            def k_block(j, _):
                krow = pl.ds(pl.multiple_of(j * T, T), T)
                k_blk = k_ref[krow, :].astype(BF16)
                v_blk = v_ref[krow, :].astype(BF16)
                for h in range(2):
                    z, sp, causal = _sb_scores(qm[h], k_blk, i, j)
                    lf = jnp.where(causal, -sp, 0.0)
                    later = ltot[h] - run_ref[h] - _running_sum(lf, tri_upto)
                    w = jnp.where(causal, jnp.exp(z - sp + later), 0.0)
                    dw = _dot(dom[h], v_blk, 1, 1)
                    g = dw * w
                    gbefore = grun_ref[h] + _running_sum(g, tri_before)
                    dz = jnp.where(causal, g * jnp.exp(-sp) - gbefore * jnp.exp(z - sp), 0.0).astype(BF16)
                    dqacc_ref[h] += _dot(dz, k_blk, 1, 0)
                    dkacc_ref[krow, :] += _dot(dz, qm[h], 0, 0)
                    dvacc_ref[krow, :] += _dot(w, dom[h], 0, 0)
                    run_ref[h] += jnp.sum(lf, axis=1, keepdims=True)
                    grun_ref[h] += jnp.sum(g, axis=1, keepdims=True)
                return 0

            lax.fori_loop(0, i + 1, k_block, 0)
            dq_ref[qrow, :] = ((dqacc_ref[0] * masks[0] + dqacc_ref[1] * masks[1]) * scale).astype(BF16)
            return 0

        lax.fori_loop(0, nq, q_block, 0)
        dk_ref[...] = dkacc_ref[...].astype(BF16)
        dv_ref[...] = dvacc_ref[...].astype(BF16)

    blk = lambda off: pl.BlockSpec((S, LANES), lambda p: (0, off + p))
    out = jax.ShapeDtypeStruct((S, SB_WIDTH), BF16)
    return pl.pallas_call(
        body, name=name, out_shape=(out, out, out), grid=(npair,),
        in_specs=[blk(0), blk(npair), blk(2 * npair), blk(0), blk(0)],
        out_specs=(blk(0), blk(0), blk(0)),
        scratch_shapes=[pltpu.VMEM((2, T, LANES), F32), pltpu.VMEM((S, LANES), F32), pltpu.VMEM((S, LANES), F32),
                        pltpu.VMEM((2, T, 1), F32), pltpu.VMEM((2, T, 1), F32)],
        compiler_params=_cparams("parallel"),
    )(proj, proj, proj, tot, do_attn)


SSM_HALVES = 2
SSM_HALF_CH = SSM_WIDTH // SSM_HALVES
SSM_HALF_ST = SSM_GROUPS * SSM_STATE // SSM_HALVES
SSM_CHUNK = 512


def _cmul(ar, ai, br, bi):
    return ar * br - ai * bi, ar * bi + ai * br


def _ssm_tables(lam_re, lam_im):
    lr = lam_re.reshape(-1)
    li = lam_im.reshape(-1)
    pows = [(jnp.ones_like(lr), jnp.zeros_like(li)), (lr, li)]
    for _ in range(2, SUBLANES + 1):
        pows.append(_cmul(pows[-1][0], pows[-1][1], lr, li))
    row = jnp.arange(SUBLANES)[:, None]

    def shift_tab(d, keep):
        return [jnp.where(keep, pows[d][0][None, :], 0.0), jnp.where(keep, pows[d][1][None, :], 0.0)]

    fwd, bwd = [], []
    for d in (1, 2, 4):
        fwd += shift_tab(d, row >= d)
        bwd += shift_tab(d, row + d < SUBLANES)
    fwd += [jnp.stack([pows[r + 1][0] for r in range(SUBLANES)]), jnp.stack([pows[r + 1][1] for r in range(SUBLANES)])]
    bwd += [jnp.stack([pows[SUBLANES - r][0] for r in range(SUBLANES)]),
            jnp.stack([pows[SUBLANES - r][1] for r in range(SUBLANES)])]

    def halves(tabs):
        t = jnp.stack(tabs)
        return t.reshape(8, SUBLANES, SSM_HALVES, SSM_HALF_ST).transpose(2, 0, 1, 3)

    return halves(fwd), halves(bwd)


def _ssm_fwd(proj, bd_re, bd_im, cd_re, cd_imneg, d_skip, tab, *, name):
    S = proj.shape[0]
    Tc = min(SSM_CHUNK, S)
    nc = S // Tc
    u_blk0 = (3 * SB_WIDTH) // SSM_HALF_CH

    def body(u_ref, bre_ref, bim_ref, cre_ref, cim_ref, d_ref, tab_ref, y_ref, xre_ref, xim_ref, cre_s, cim_s):
        c = pl.program_id(1)

        @pl.when(c == 0)
        def _():
            cre_s[...] = jnp.zeros_like(cre_s)
            cim_s[...] = jnp.zeros_like(cim_s)

        u = u_ref[...]
        ub = u.astype(BF16)
        xre_ref[...] = _dot(ub, bre_ref[0], 1, 0)
        xim_ref[...] = _dot(ub, bim_ref[0], 1, 0)

        def slab(k, carry):
            car_re, car_im = carry
            rows = pl.ds(pl.multiple_of(k * SUBLANES, SUBLANES), SUBLANES)
            sre = xre_ref[rows, :]
            sim = xim_ref[rows, :]
            for n, d in enumerate((1, 2, 4)):
                pre, pim = tab_ref[0, 2 * n], tab_ref[0, 2 * n + 1]
                rre = pltpu.roll(sre, d, 0)
                rim = pltpu.roll(sim, d, 0)
                sre, sim = sre + (pre * rre - pim * rim), sim + (pre * rim + pim * rre)
            pre, pim = tab_ref[0, 6], tab_ref[0, 7]
            sre, sim = sre + (pre * car_re - pim * car_im), sim + (pre * car_im + pim * car_re)
            xre_ref[rows, :] = sre
            xim_ref[rows, :] = sim
            last = (SUBLANES - 1, SUBLANES)
            return (jnp.broadcast_to(sre[last[0]:last[1], :], sre.shape),
                    jnp.broadcast_to(sim[last[0]:last[1], :], sim.shape))

        car = lax.fori_loop(0, Tc // SUBLANES, slab, (cre_s[...], cim_s[...]))
        cre_s[...] = car[0]
        cim_s[...] = car[1]
        y = _dot(xre_ref[...], cre_ref[0], 1, 0) + _dot(xim_ref[...], cim_ref[0], 1, 0)
        y_ref[...] = y + d_ref[...] * u

    return pl.pallas_call(
        body, name=name,
        out_shape=(jax.ShapeDtypeStruct((S, SSM_WIDTH), F32),
                   jax.ShapeDtypeStruct((S, SSM_HALVES * SSM_HALF_ST), F32),
                   jax.ShapeDtypeStruct((S, SSM_HALVES * SSM_HALF_ST), F32)),
        grid=(SSM_HALVES, nc),
        in_specs=[pl.BlockSpec((Tc, SSM_HALF_CH), lambda h, c: (c, u_blk0 + h)),
                  pl.BlockSpec((1, SSM_HALF_CH, SSM_HALF_ST), lambda h, c: (h, 0, 0)),
                  pl.BlockSpec((1, SSM_HALF_CH, SSM_HALF_ST), lambda h, c: (h, 0, 0)),
                  pl.BlockSpec((1, SSM_HALF_ST, SSM_HALF_CH), lambda h, c: (h, 0, 0)),
                  pl.BlockSpec((1, SSM_HALF_ST, SSM_HALF_CH), lambda h, c: (h, 0, 0)),
                  pl.BlockSpec((1, SSM_HALF_CH), lambda h, c: (0, h)),
                  pl.BlockSpec((1, 8, SUBLANES, SSM_HALF_ST), lambda h, c: (h, 0, 0, 0))],
        out_specs=(pl.BlockSpec((Tc, SSM_HALF_CH), lambda h, c: (c, h)),
                   pl.BlockSpec((Tc, SSM_HALF_ST), lambda h, c: (c, h)),
                   pl.BlockSpec((Tc, SSM_HALF_ST), lambda h, c: (c, h))),
        scratch_shapes=[pltpu.VMEM((SUBLANES, SSM_HALF_ST), F32), pltpu.VMEM((SUBLANES, SSM_HALF_ST), F32)],
        compiler_params=_cparams("parallel", "arbitrary"),
    )(proj, bd_re, bd_im, cd_re, cd_imneg, d_skip, tab)


def _ssm_bwd(dy, proj, x_re, x_im, bd_re, bd_im, cd_re, cd_imneg, d_skip, tab, *, name):
    S = proj.shape[0]
    Tc = min(SSM_CHUNK, S)
    nc = S // Tc
    u_blk0 = (3 * SB_WIDTH) // SSM_HALF_CH

    def body(dy_ref, u_ref, xre_ref, xim_ref, bre_ref, bim_ref, cre_ref, cim_ref, d_ref, tab_ref,
             du_ref, dbre_ref, dbim_ref, dcre_ref, dcim_ref, dd_ref, dlre_ref, dlim_ref,
             gre_s, gim_s, cre_s, cim_s):
        c = pl.program_id(1)

        @pl.when(c == 0)
        def _():
            cre_s[...] = jnp.zeros_like(cre_s)
            cim_s[...] = jnp.zeros_like(cim_s)
            dbre_ref[...] = jnp.zeros_like(dbre_ref)
            dbim_ref[...] = jnp.zeros_like(dbim_ref)
            dcre_ref[...] = jnp.zeros_like(dcre_ref)
            dcim_ref[...] = jnp.zeros_like(dcim_ref)
            dd_ref[...] = jnp.zeros_like(dd_ref)
            dlre_ref[...] = jnp.zeros_like(dlre_ref)
            dlim_ref[...] = jnp.zeros_like(dlim_ref)

        dy = dy_ref[...]
        dyb = dy.astype(BF16)
        u = u_ref[...]
        gre_s[...] = _dot(dyb, cre_ref[0], 1, 1)
        gim_s[...] = _dot(dyb, cim_ref[0], 1, 1)
        row = lax.broadcasted_iota(jnp.int32, (SUBLANES, SSM_HALF_ST), 0)
        nslab = Tc // SUBLANES

        def slab(kk, carry):
            car_re, car_im, acc_re, acc_im = carry
            k = nslab - 1 - kk
            rows = pl.ds(pl.multiple_of(k * SUBLANES, SUBLANES), SUBLANES)
            sre = gre_s[rows, :]
            sim = gim_s[rows, :]
            for n, d in enumerate((1, 2, 4)):
                pre, pim = tab_ref[0, 2 * n], tab_ref[0, 2 * n + 1]
                rre = pltpu.roll(sre, SUBLANES - d, 0)
                rim = pltpu.roll(sim, SUBLANES - d, 0)
                sre, sim = sre + (pre * rre + pim * rim), sim + (pre * rim - pim * rre)
            pre, pim = tab_ref[0, 6], tab_ref[0, 7]
            sre, sim = sre + (pre * car_re + pim * car_im), sim + (pre * car_im - pim * car_re)
            gre_s[rows, :] = sre
            gim_s[rows, :] = sim
            nre = jnp.where(row == SUBLANES - 1, car_re, pltpu.roll(sre, SUBLANES - 1, 0))
            nim = jnp.where(row == SUBLANES - 1, car_im, pltpu.roll(sim, SUBLANES - 1, 0))
            xr = xre_ref[rows, :]
            xi = xim_ref[rows, :]
            acc_re = acc_re + (nre * xr + nim * xi)
            acc_im = acc_im + (nim * xr - nre * xi)
            return (jnp.broadcast_to(sre[0:1, :], sre.shape), jnp.broadcast_to(sim[0:1, :], sim.shape), acc_re, acc_im)

        car = lax.fori_loop(0, nslab, slab, (cre_s[...], cim_s[...], dlre_ref[0], dlim_ref[0]))
        cre_s[...] = car[0]
        cim_s[...] = car[1]
        dlre_ref[0] = car[2]
        dlim_ref[0] = car[3]
        gre = gre_s[...].astype(BF16)
        gim = gim_s[...].astype(BF16)
        ub = u.astype(BF16)
        du = _dot(gre, bre_ref[0], 1, 1) + _dot(gim, bim_ref[0], 1, 1) + d_ref[...] * dy
        du_ref[...] = du.astype(BF16)
        dbre_ref[0] += _dot(ub, gre, 0, 0)
        dbim_ref[0] += _dot(ub, gim, 0, 0)
        dcre_ref[0] += _dot(xre_ref[...], dyb, 0, 0)
        dcim_ref[0] += _dot(xim_ref[...], dyb, 0, 0)
        dd_ref[...] += jnp.sum(dy * u, axis=0, keepdims=True)

    rev = lambda c: nc - 1 - c
    return pl.pallas_call(
        body, name=name,
        out_shape=(jax.ShapeDtypeStruct((S, SSM_WIDTH), BF16),
                   jax.ShapeDtypeStruct((SSM_HALVES, SSM_HALF_CH, SSM_HALF_ST), F32),
                   jax.ShapeDtypeStruct((SSM_HALVES, SSM_HALF_CH, SSM_HALF_ST), F32),
                   jax.ShapeDtypeStruct((SSM_HALVES, SSM_HALF_ST, SSM_HALF_CH), F32),
                   jax.ShapeDtypeStruct((SSM_HALVES, SSM_HALF_ST, SSM_HALF_CH), F32),
                   jax.ShapeDtypeStruct((1, SSM_WIDTH), F32),
                   jax.ShapeDtypeStruct((SSM_HALVES, SUBLANES, SSM_HALF_ST), F32),
                   jax.ShapeDtypeStruct((SSM_HALVES, SUBLANES, SSM_HALF_ST), F32)),
        grid=(SSM_HALVES, nc),
        in_specs=[pl.BlockSpec((Tc, SSM_HALF_CH), lambda h, c: (rev(c), h)),
                  pl.BlockSpec((Tc, SSM_HALF_CH), lambda h, c: (rev(c), u_blk0 + h)),
                  pl.BlockSpec((Tc, SSM_HALF_ST), lambda h, c: (rev(c), h)),
                  pl.BlockSpec((Tc, SSM_HALF_ST), lambda h, c: (rev(c), h)),
                  pl.BlockSpec((1, SSM_HALF_CH, SSM_HALF_ST), lambda h, c: (h, 0, 0)),
                  pl.BlockSpec((1, SSM_HALF_CH, SSM_HALF_ST), lambda h, c: (h, 0, 0)),
                  pl.BlockSpec((1, SSM_HALF_ST, SSM_HALF_CH), lambda h, c: (h, 0, 0)),
                  pl.BlockSpec((1, SSM_HALF_ST, SSM_HALF_CH), lambda h, c: (h, 0, 0)),
                  pl.BlockSpec((1, SSM_HALF_CH), lambda h, c: (0, h)),
                  pl.BlockSpec((1, 8, SUBLANES, SSM_HALF_ST), lambda h, c: (h, 0, 0, 0))],
        out_specs=(pl.BlockSpec((Tc, SSM_HALF_CH), lambda h, c: (rev(c), h)),
                   pl.BlockSpec((1, SSM_HALF_CH, SSM_HALF_ST), lambda h, c: (h, 0, 0)),
                   pl.BlockSpec((1, SSM_HALF_CH, SSM_HALF_ST), lambda h, c: (h, 0, 0)),
                   pl.BlockSpec((1, SSM_HALF_ST, SSM_HALF_CH), lambda h, c: (h, 0, 0)),
                   pl.BlockSpec((1, SSM_HALF_ST, SSM_HALF_CH), lambda h, c: (h, 0, 0)),
                   pl.BlockSpec((1, SSM_HALF_CH), lambda h, c: (0, h)),
                   pl.BlockSpec((1, SUBLANES, SSM_HALF_ST), lambda h, c: (h, 0, 0)),
                   pl.BlockSpec((1, SUBLANES, SSM_HALF_ST), lambda h, c: (h, 0, 0))),
        scratch_shapes=[pltpu.VMEM((Tc, SSM_HALF_ST), F32), pltpu.VMEM((Tc, SSM_HALF_ST), F32),
                        pltpu.VMEM((SUBLANES, SSM_HALF_ST), F32), pltpu.VMEM((SUBLANES, SSM_HALF_ST), F32)],
        compiler_params=_cparams("parallel", "arbitrary"),
    )(dy, proj, x_re, x_im, bd_re, bd_im, cd_re, cd_imneg, d_skip, tab)


def _ssm_prepare(a_re, a_im, log_dt, b_re, b_im):
    dt = jnp.exp(log_dt)[:, None]
    mag = jnp.exp(a_re * dt)
    lre = mag * jnp.cos(a_im * dt)
    lim = mag * jnp.sin(a_im * dt)
    den = a_re * a_re + a_im * a_im
    fre = ((lre - 1.0) * a_re + lim * a_im) / den
    fim = (lim * a_re - (lre - 1.0) * a_im) / den
    bbre = fre[:, :, None] * b_re - fim[:, :, None] * b_im
    bbim = fre[:, :, None] * b_im + fim[:, :, None] * b_re
    return lre, lim, bbre, bbim


def _group_eye():
    return jnp.eye(SSM_GROUPS // SSM_HALVES, dtype=F32)


def _bd_from_bbar(bbar):
    gh = SSM_GROUPS // SSM_HALVES
    b = bbar.reshape(SSM_HALVES, gh, SSM_STATE, SSM_GROUP).transpose(0, 1, 3, 2)
    out = b[:, :, :, None, :] * _group_eye()[None, :, None, :, None]
    return out.reshape(SSM_HALVES, SSM_HALF_CH, SSM_HALF_ST)


def _bbar_from_bd(dbd):
    gh = SSM_GROUPS // SSM_HALVES
    d = dbd.reshape(SSM_HALVES, gh, SSM_GROUP, gh, SSM_STATE)
    d = jnp.sum(d * _group_eye()[None, :, None, :, None], axis=3)
    return d.transpose(0, 1, 3, 2).reshape(SSM_GROUPS, SSM_STATE, SSM_GROUP)


def _cd_from_c(cmat):
    gh = SSM_GROUPS // SSM_HALVES
    c = cmat.reshape(SSM_HALVES, gh, SSM_GROUP, SSM_STATE).transpose(0, 1, 3, 2)
    out = c[:, :, :, None, :] * _group_eye()[None, :, None, :, None]
    return out.reshape(SSM_HALVES, SSM_HALF_ST, SSM_HALF_CH)


def _c_from_cd(dcd):
    gh = SSM_GROUPS // SSM_HALVES
    d = dcd.reshape(SSM_HALVES, gh, SSM_STATE, gh, SSM_GROUP)
    d = jnp.sum(d * _group_eye()[None, :, None, :, None], axis=3)
    return d.transpose(0, 1, 3, 2).reshape(SSM_GROUPS, SSM_GROUP, SSM_STATE)


def _glu_fwd(y_pre, w_glu, b_glu, *, name):
    S, W = y_pre.shape
    tr = _row_tile(S)

    def body(y_ref, w_ref, b_ref, o_ref):
        yg = _gelu(y_ref[...])
        gl = _dot(yg, w_ref[...], 1, 0) + b_ref[...]
        o_ref[...] = (yg * _sigmoid(gl)).astype(BF16)

    row = pl.BlockSpec((tr, W), lambda i: (i, 0))
    return pl.pallas_call(
        body, name=name, out_shape=jax.ShapeDtypeStruct((S, W), BF16), grid=(S // tr,),
        in_specs=[row, pl.BlockSpec((W, W), lambda i: (0, 0)), pl.BlockSpec((1, W), lambda i: (0, 0))],
        out_specs=row, compiler_params=_cparams("parallel"),
    )(y_pre, w_glu, b_glu)


def _glu_bwd(y_pre, do, w_glu, b_glu, *, name):
    S, W = y_pre.shape
    tr = _row_tile(S)

    def body(y_ref, do_ref, w_ref, b_ref, dy_ref, dw_ref, db_ref):
        i = pl.program_id(0)
        yg, dyg_dy = _gelu_and_grad(y_ref[...])
        ygb = yg.astype(BF16)
        sg = _sigmoid(_dot(ygb, w_ref[...], 1, 0) + b_ref[...])
        do = do_ref[...]
        dgl = do * yg * sg * (1.0 - sg)
        dglb = dgl.astype(BF16)
        dyg = do * sg + _dot(dglb, w_ref[...], 1, 1)
        dy_ref[...] = dyg * dyg_dy
        dw = _dot(ygb, dglb, 0, 0)
        db = jnp.sum(dgl, axis=0, keepdims=True)

        @pl.when(i == 0)
        def _():
            dw_ref[...] = dw
            db_ref[...] = db

        @pl.when(i > 0)
        def _():
            dw_ref[...] += dw
            db_ref[...] += db

    row = pl.BlockSpec((tr, W), lambda i: (i, 0))
    full = pl.BlockSpec((W, W), lambda i: (0, 0))
    vec = pl.BlockSpec((1, W), lambda i: (0, 0))
    return pl.pallas_call(
        body, name=name,
        out_shape=(jax.ShapeDtypeStruct((S, W), F32), jax.ShapeDtypeStruct((W, W), F32), jax.ShapeDtypeStruct((1, W), F32)),
        grid=(S // tr,), in_specs=[row, row, full, vec], out_specs=(row, full, vec),
        compiler_params=_cparams("arbitrary"),
    )(y_pre, do, w_glu, b_glu)


GATE_COL0 = 3 * SB_WIDTH + SSM_WIDTH


def _merge_fwd(proj, o_attn, o_ssm, w_ba, w_bs, b_gate, *, name):
    S = proj.shape[0]
    D = D_MODEL
    tr = _pick(S, (256, 128, 64, 32, 16, 8))
    gb = GATE_COL0 // D

    def body(ga_ref, gs_ref, oa_ref, os_ref, wa_ref, ws_ref, ba_ref, bs_ref, m_ref):
        pa = _dot(oa_ref[...], wa_ref[...], 1, 0)
        ps = _dot(os_ref[...], ws_ref[...], 1, 0)
        sa = _sigmoid(ga_ref[...] + ba_ref[...])
        ss = _sigmoid(gs_ref[...] + bs_ref[...])
        m_ref[...] = (sa * pa + ss * ps).astype(BF16)

    return pl.pallas_call(
        body, name=name, out_shape=jax.ShapeDtypeStruct((S, D), BF16), grid=(S // tr,),
        in_specs=[pl.BlockSpec((tr, D), lambda i: (i, gb)), pl.BlockSpec((tr, D), lambda i: (i, gb + 1)),
                  pl.BlockSpec((tr, SB_WIDTH), lambda i: (i, 0)), pl.BlockSpec((tr, SSM_WIDTH), lambda i: (i, 0)),
                  pl.BlockSpec((SB_WIDTH, D), lambda i: (0, 0)), pl.BlockSpec((SSM_WIDTH, D), lambda i: (0, 0)),
                  pl.BlockSpec((1, D), lambda i: (0, 0)), pl.BlockSpec((1, D), lambda i: (0, 1))],
        out_specs=pl.BlockSpec((tr, D), lambda i: (i, 0)),
        compiler_params=_cparams("parallel"),
    )(proj, proj, o_attn, o_ssm, w_ba, w_bs, b_gate, b_gate)


def _merge_bwd(dmerged, proj, o_attn, o_ssm, w_ba, w_bs, b_gate, *, name):
    S = proj.shape[0]
    D = D_MODEL
    tr = _pick(S, (256, 128, 64, 32, 16, 8))
    gb = GATE_COL0 // D

    def body(dm_ref, ga_ref, gs_ref, oa_ref, os_ref, wa_ref, ws_ref, ba_ref, bs_ref,
             doa_ref, dos_ref, dg_ref, db_ref, dwa_ref, dws_ref):
        i = pl.program_id(0)
        dm = dm_ref[...]
        oa = oa_ref[...]
        osm = os_ref[...]
        pa = _dot(oa, wa_ref[...], 1, 0)
        ps = _dot(osm, ws_ref[...], 1, 0)
        sa = _sigmoid(ga_ref[...] + ba_ref[...])
        ss = _sigmoid(gs_ref[...] + bs_ref[...])
        dpa = (dm * sa).astype(BF16)
        dps = (dm * ss).astype(BF16)
        dga = dm * pa * sa * (1.0 - sa)
        dgs = dm * ps * ss * (1.0 - ss)
        dg_ref[:, :D] = dga.astype(BF16)
        dg_ref[:, D:] = dgs.astype(BF16)
        doa_ref[...] = _dot(dpa, wa_ref[...], 1, 1).astype(BF16)
        dos_ref[...] = _dot(dps, ws_ref[...], 1, 1)
        dwa = _dot(oa, dpa, 0, 0)
        dws = _dot(osm, dps, 0, 0)
        dba = jnp.sum(dga, axis=0, keepdims=True)
        dbs = jnp.sum(dgs, axis=0, keepdims=True)

        @pl.when(i == 0)
        def _():
            dwa_ref[...] = dwa
            dws_ref[...] = dws
            db_ref[:, :D] = dba
            db_ref[:, D:] = dbs

        @pl.when(i > 0)
        def _():
            dwa_ref[...] += dwa
            dws_ref[...] += dws
            db_ref[:, :D] += dba
            db_ref[:, D:] += dbs

    rowD = pl.BlockSpec((tr, D), lambda i: (i, 0))
    wspec = pl.BlockSpec((SB_WIDTH, D), lambda i: (0, 0))
    return pl.pallas_call(
        body, name=name,
        out_shape=(jax.ShapeDtypeStruct((S, SB_WIDTH), BF16), jax.ShapeDtypeStruct((S, SSM_WIDTH), F32),
                   jax.ShapeDtypeStruct((S, 2 * D), BF16), jax.ShapeDtypeStruct((1, 2 * D), F32),
                   jax.ShapeDtypeStruct((SB_WIDTH, D), F32), jax.ShapeDtypeStruct((SSM_WIDTH, D), F32)),
        grid=(S // tr,),
        in_specs=[rowD, pl.BlockSpec((tr, D), lambda i: (i, gb)), pl.BlockSpec((tr, D), lambda i: (i, gb + 1)),
                  pl.BlockSpec((tr, SB_WIDTH), lambda i: (i, 0)), pl.BlockSpec((tr, SSM_WIDTH), lambda i: (i, 0)),
                  wspec, wspec, pl.BlockSpec((1, D), lambda i: (0, 0)), pl.BlockSpec((1, D), lambda i: (0, 1))],
        out_specs=(pl.BlockSpec((tr, SB_WIDTH), lambda i: (i, 0)), pl.BlockSpec((tr, SSM_WIDTH), lambda i: (i, 0)),
                   pl.BlockSpec((tr, 2 * D), lambda i: (i, 0)), pl.BlockSpec((1, 2 * D), lambda i: (0, 0)),
                   wspec, wspec),
        compiler_params=_cparams("arbitrary"),
    )(dmerged, proj, proj, o_attn, o_ssm, w_ba, w_bs, b_gate, b_gate)


def _xattn_probs(q, k, h):
    cols = slice(h * XA_HEAD_DIM, (h + 1) * XA_HEAD_DIM)
    s = _dot(q[:, cols], k[:, cols], 1, 1) * (XA_HEAD_DIM ** -0.5)
    s = s - jnp.max(s, axis=-1, keepdims=True)
    e = jnp.exp(s)
    return e / jnp.sum(e, axis=-1, keepdims=True), cols


def _xattn_fwd(q2, k2, v2, *, name):
    S, D = q2.shape
    M = k2.shape[0]
    tr = _row_tile(S)

    def body(q_ref, k_ref, v_ref, o_ref):
        q = q_ref[...]
        k = k_ref[...]
        v = v_ref[...]
        for h in range(XA_HEADS):
            p, cols = _xattn_probs(q, k, h)
            o_ref[:, cols] = _dot(p, v[:, cols], 1, 0).astype(BF16)

    row = pl.BlockSpec((tr, D), lambda i: (i, 0))
    memb = pl.BlockSpec((M, D), lambda i: (0, 0))
    return pl.pallas_call(
        body, name=name, out_shape=jax.ShapeDtypeStruct((S, D), BF16), grid=(S // tr,),
        in_specs=[row, memb, memb], out_specs=row, compiler_params=_cparams("parallel"),
    )(q2, k2, v2)


def _xattn_bwd(q2, k2, v2, do2, *, name):
    S, D = q2.shape
    M = k2.shape[0]
    tr = _row_tile(S)
    scale = XA_HEAD_DIM ** -0.5

    def body(q_ref, k_ref, v_ref, do_ref, dq_ref, dk_ref, dv_ref):
        i = pl.program_id(0)

        @pl.when(i == 0)
        def _():
            dk_ref[...] = jnp.zeros_like(dk_ref)
            dv_ref[...] = jnp.zeros_like(dv_ref)

        q = q_ref[...]
        k = k_ref[...]
        v = v_ref[...]
        do = do_ref[...]
        for h in range(XA_HEADS):
            p, cols = _xattn_probs(q, k, h)
            dp = _dot(do[:, cols], v[:, cols], 1, 1)
            ds = (p * (dp - jnp.sum(dp * p, axis=-1, keepdims=True)) * scale).astype(BF16)
            dq_ref[:, cols] = _dot(ds, k[:, cols], 1, 0).astype(BF16)
            dk_ref[:, cols] += _dot(ds, q[:, cols], 0, 0)
            dv_ref[:, cols] += _dot(p, do[:, cols], 0, 0)

    row = pl.BlockSpec((tr, D), lambda i: (i, 0))
    memb = pl.BlockSpec((M, D), lambda i: (0, 0))
    return pl.pallas_call(
        body, name=name,
        out_shape=(jax.ShapeDtypeStruct((S, D), BF16), jax.ShapeDtypeStruct((M, D), F32), jax.ShapeDtypeStruct((M, D), F32)),
        grid=(S // tr,), in_specs=[row, memb, memb, row], out_specs=(row, memb, memb),
        compiler_params=_cparams("arbitrary"),
    )(q2, k2, v2, do2)


CONV_ROWS = 512


def _shift_down(ref, t0, rows, d):
    cur = ref[pl.ds(t0, rows), :]
    out = pltpu.roll(cur, d, 0)
    r = lax.broadcasted_iota(jnp.int32, cur.shape, 0)
    for e in range(d):
        src = t0 - d + e
        prev = ref[pl.ds(src, 1), :] if src >= 0 else jnp.zeros((1, cur.shape[1]), cur.dtype)
        out = jnp.where(r == e, prev, out)
    return out


def _shift_up(ref, t0, rows, d, total):
    cur = ref[pl.ds(t0, rows), :]
    out = pltpu.roll(cur, rows - d, 0)
    r = lax.broadcasted_iota(jnp.int32, cur.shape, 0)
    for e in range(d):
        src = t0 + rows + e
        nxt = ref[pl.ds(src, 1), :] if src < total else jnp.zeros((1, cur.shape[1]), cur.dtype)
        out = jnp.where(r == rows - d + e, nxt, out)
    return out


def _conv3(ref, w_ref, b_ref, t0, rows):
    return (w_ref[2:3, :] * ref[pl.ds(t0, rows), :] + w_ref[1:2, :] * _shift_down(ref, t0, rows, 1)
            + w_ref[0:1, :] * _shift_down(ref, t0, rows, 2) + b_ref[...])


def _convgate_fwd(up_g, up_v, conv_w, conv_b, *, name):
    S, H = up_g.shape
    nb = H // LANES
    R = min(CONV_ROWS, S)

    def body(g_ref, v_ref, wg_ref, wv_ref, bg_ref, bv_ref, a_ref):
        for t0 in range(0, S, R):
            cg = _conv3(g_ref, wg_ref, bg_ref, t0, R)
            cv = _conv3(v_ref, wv_ref, bv_ref, t0, R)
            a_ref[pl.ds(t0, R), :] = (_gelu(cg) * cv).astype(BF16)

    col = lambda off: pl.BlockSpec((S, LANES), lambda j: (0, off + j))
    wcol = lambda off: pl.BlockSpec((3, LANES), lambda j: (0, off + j))
    bcol = lambda off: pl.BlockSpec((1, LANES), lambda j: (0, off + j))
    return pl.pallas_call(
        body, name=name, out_shape=jax.ShapeDtypeStruct((S, H), BF16), grid=(nb,),
        in_specs=[col(0), col(0), wcol(0), wcol(nb), bcol(0), bcol(nb)],
        out_specs=col(0), compiler_params=_cparams("parallel"),
    )(up_g, up_v, conv_w, conv_w, conv_b, conv_b)


def _convgate_bwd(up_g, up_v, da, conv_w, conv_b, *, name):
    S, H = up_g.shape
    nb = H // LANES
    R = min(CONV_ROWS, S)

    def body(g_ref, v_ref, da_ref, wg_ref, wv_ref, bg_ref, bv_ref,
             dug_ref, duv_ref, dwg_ref, dwv_ref, dbg_ref, dbv_ref, dcg_s, dcv_s):
        zero3 = jnp.zeros((1, LANES), F32)
        acc = {"g": [zero3, zero3, zero3, zero3], "v": [zero3, zero3, zero3, zero3]}
        for t0 in range(0, S, R):
            cg = _conv3(g_ref, wg_ref, bg_ref, t0, R)
            cv = _conv3(v_ref, wv_ref, bv_ref, t0, R)
            da = da_ref[pl.ds(t0, R), :]
            gl, dgl = _gelu_and_grad(cg)
            dcg = da * cv * dgl
            dcv = da * gl
            dcg_s[pl.ds(t0, R), :] = dcg
            dcv_s[pl.ds(t0, R), :] = dcv
            for key, ref, dc in (("g", g_ref, dcg), ("v", v_ref, dcv)):
                a = acc[key]
                a[2] = a[2] + jnp.sum(dc * ref[pl.ds(t0, R), :], axis=0, keepdims=True)
                a[1] = a[1] + jnp.sum(dc * _shift_down(ref, t0, R, 1), axis=0, keepdims=True)
                a[0] = a[0] + jnp.sum(dc * _shift_down(ref, t0, R, 2), axis=0, keepdims=True)
                a[3] = a[3] + jnp.sum(dc, axis=0, keepdims=True)
        for key, dw_ref, db_ref in (("g", dwg_ref, dbg_ref), ("v", dwv_ref, dbv_ref)):
            a = acc[key]
            dw_ref[0:1, :] = a[0]
            dw_ref[1:2, :] = a[1]
            dw_ref[2:3, :] = a[2]
            db_ref[...] = a[3]
        for t0 in range(0, S, R):
            for dc_s, w_ref, du_ref in ((dcg_s, wg_ref, dug_ref), (dcv_s, wv_ref, duv_ref)):
                du = (w_ref[2:3, :] * dc_s[pl.ds(t0, R), :] + w_ref[1:2, :] * _shift_up(dc_s, t0, R, 1, S)
                      + w_ref[0:1, :] * _shift_up(dc_s, t0, R, 2, S))
                du_ref[pl.ds(t0, R), :] = du.astype(BF16)

    col = lambda off: pl.BlockSpec((S, LANES), lambda j: (0, off + j))
    wcol = lambda off: pl.BlockSpec((3, LANES), lambda j: (0, off + j))
    bcol = lambda off: pl.BlockSpec((1, LANES), lambda j: (0, off + j))
    return pl.pallas_call(
        body, name=name,
        out_shape=(jax.ShapeDtypeStruct((S, H), BF16), jax.ShapeDtypeStruct((S, H), BF16),
                   jax.ShapeDtypeStruct((3, H), F32), jax.ShapeDtypeStruct((3, H), F32),
                   jax.ShapeDtypeStruct((1, H), F32), jax.ShapeDtypeStruct((1, H), F32)),
        grid=(nb,),
        in_specs=[col(0), col(0), col(0), wcol(0), wcol(nb), bcol(0), bcol(nb)],
        out_specs=(col(0), col(0), wcol(0), wcol(0), bcol(0), bcol(0)),
        scratch_shapes=[pltpu.VMEM((S, LANES), F32), pltpu.VMEM((S, LANES), F32)],
        compiler_params=_cparams("parallel"),
    )(up_g, up_v, da, conv_w, conv_w, conv_b, conv_b)


def _local_step(x, mem, target, W, P):
    mm = _matmul
    h1 = _rms_fwd(x, P["norm_mix_pre"], name="rms_mix_pre")
    proj = mm(h1, W["w_in"], name="mm_in")
    o_attn, sb_tot = _sb_fwd(proj, name="sb_fwd")

    ssm_prep = lambda *a: _ssm_prepare(*a)
    (lam_re, lam_im, bb_re, bb_im), prep_vjp = jax.vjp(
        ssm_prep, P["ssm_a_re"], P["ssm_a_im"], P["ssm_log_dt"], P["ssm_b_re"], P["ssm_b_im"])
    tab_f, tab_b = _ssm_tables(lam_re, lam_im)
    bd_re = _bd_from_bbar(bb_re).astype(BF16)
    bd_im = _bd_from_bbar(bb_im).astype(BF16)
    cd_re = _cd_from_c(P["ssm_c_re"]).astype(BF16)
    cd_imneg = _cd_from_c(-P["ssm_c_im"]).astype(BF16)
    y_pre, x_re, x_im = _ssm_fwd(proj, bd_re, bd_im, cd_re, cd_imneg, P["ssm_d"], tab_f, name="ssm_fwd")
    o_ssm = _glu_fwd(y_pre, W["ssm_w_glu"], P["ssm_b_glu"], name="glu_fwd")

    merged = _merge_fwd(proj, o_attn, o_ssm, W["w_branch_attn"], W["w_branch_ssm"], P["b_gate"], name="merge_fwd")
    mo = mm(merged, W["w_out"], name="mm_out")
    x1, h2 = _resnorm_norm(x, mo, P["norm_mix_post"], P["norm_xa_pre"], name="resnorm_1")

    mem_n = _rms_fwd(mem, P["norm_mem"], name="rms_mem")
    q2 = mm(h2, W["xa_wq"], out_dtype=BF16, name="mm_xq")
    k2 = mm(mem_n, W["xa_wk"], out_dtype=BF16, name="mm_xk")
    v2 = mm(mem_n, W["xa_wv"], out_dtype=BF16, name="mm_xv")
    o2 = _xattn_fwd(q2, k2, v2, name="xattn_fwd")
    xa = mm(o2, W["xa_wo"], name="mm_xo")
    x2, h3 = _resnorm_norm(x1, xa, P["norm_xa_post"], P["norm_ffn_pre"], name="resnorm_2")

    half = N_DEV // 2
    up_g = mm(h3, W["ffn_w_up"], n_blocks=half, name="mm_up_g")
    up_v = mm(h3, W["ffn_w_up"], b_block0=half, name="mm_up_v")
    act = _convgate_fwd(up_g, up_v, W["ffn_conv_w"], P["ffn_conv_b"], name="convgate_fwd")
    f = mm(act, W["ffn_w_down"], name="mm_down")
    loss, dy, df, dg_ffn_post = _final_loss(x2, f, P["norm_ffn_post"], target, name="final_loss")

    G = {"norm_ffn_post": dg_ffn_post}
    dact = mm(df, W["ffn_w_down"], tb=True, name="mm_down_dx")
    G["ffn_w_down"] = mm(act, df, ta=True, name="mm_down_dw")
    dug, duv, dwg, dwv, dbg, dbv = _convgate_bwd(up_g, up_v, dact, W["ffn_conv_w"], P["ffn_conv_b"], name="convgate_bwd")
    G["ffn_conv_w"] = jnp.concatenate([dwg, dwv], axis=1)
    G["ffn_conv_b"] = jnp.concatenate([dbg, dbv], axis=1)
    dh3 = mm(dug, W["ffn_w_up"], tb=True, n_blocks=half, name="mm_up_g_dx")
    dh3 = mm(duv, W["ffn_w_up"], tb=True, b_block0=half, acc_in=dh3, name="mm_up_v_dx")
    dw_up = mm(h3, dug, ta=True, out_into=lax.empty(W["ffn_w_up"].shape, F32), name="mm_up_g_dw")
    G["ffn_w_up"] = mm(h3, duv, ta=True, out_into=dw_up, out_block0=half, name="mm_up_v_dw")
    dx2, dxa, G["norm_ffn_pre"], G["norm_xa_post"] = _norm_bwd_pair(
        dy, dh3, x2, P["norm_ffn_pre"], xa, P["norm_xa_post"], name="norm_bwd_3")

    G["xa_wo"] = mm(o2, dxa, ta=True, name="mm_xo_dw")
    do2 = mm(dxa, W["xa_wo"], tb=True, out_dtype=BF16, name="mm_xo_dx")
    dq2, dk2, dv2 = _xattn_bwd(q2, k2, v2, do2, name="xattn_bwd")
    G["xa_wq"] = mm(h2, dq2, ta=True, name="mm_xq_dw")
    dh2 = mm(dq2, W["xa_wq"], tb=True, name="mm_xq_dx")
    G["xa_wk"] = mm(mem_n, dk2, ta=True, name="mm_xk_dw")
    G["xa_wv"] = mm(mem_n, dv2, ta=True, name="mm_xv_dw")
    dmem_n = jnp.concatenate([dk2, dv2], axis=1)
    wkv = jnp.concatenate([W["xa_wk"], W["xa_wv"]], axis=1)
    dmem = mm(dmem_n, wkv, tb=True, name="mm_xkv_dx")
    _, G["norm_mem"] = _norm_bwd_single(None, dmem, mem, P["norm_mem"], name="norm_bwd_mem")
    dx1, dmo, G["norm_xa_pre"], G["norm_mix_post"] = _norm_bwd_pair(
        dx2, dh2, x1, P["norm_xa_pre"], mo, P["norm_mix_post"], name="norm_bwd_2")

    G["w_out"] = mm(merged, dmo, ta=True, name="mm_out_dw")
    dmerged = mm(dmo, W["w_out"], tb=True, name="mm_out_dx")
    do_attn, do_ssm, dgate, G["b_gate"], G["w_branch_attn"], G["w_branch_ssm"] = _merge_bwd(
        dmerged, proj, o_attn, o_ssm, W["w_branch_attn"], W["w_branch_ssm"], P["b_gate"], name="merge_bwd")
    dy_pre, G["ssm_w_glu"], G["ssm_b_glu"] = _glu_bwd(y_pre, do_ssm, W["ssm_w_glu"], P["ssm_b_glu"], name="glu_bwd")
    du, dbd_re, dbd_im, dcd_re, dcd_imneg, G["ssm_d"], dl_re, dl_im = _ssm_bwd(
        dy_pre, proj, x_re, x_im, bd_re, bd_im, cd_re, cd_imneg, P["ssm_d"], tab_b, name="ssm_bwd")
    G["ssm_c_re"] = _c_from_cd(dcd_re)
    G["ssm_c_im"] = -_c_from_cd(dcd_imneg)
    dlam_re = jnp.sum(dl_re, axis=1).reshape(SSM_GROUPS, SSM_STATE)
    dlam_im = jnp.sum(dl_im, axis=1).reshape(SSM_GROUPS, SSM_STATE)
    (G["ssm_a_re"], G["ssm_a_im"], G["ssm_log_dt"], G["ssm_b_re"], G["ssm_b_im"]) = prep_vjp(
        (dlam_re, dlam_im, _bbar_from_bd(dbd_re), _bbar_from_bd(dbd_im)))
    dq, dk, dv = _sb_bwd(proj, sb_tot, do_attn, name="sb_bwd")
    dproj = jnp.concatenate([dq, dk, dv, du, dgate], axis=1)
    G["w_in"] = mm(h1, dproj, ta=True, out_cb=W["w_in"].shape[2], name="mm_in_dw")
    dh1 = mm(dproj, W["w_in"], tb=True, name="mm_in_dx")
    grad_x, G["norm_mix_pre"] = _norm_bwd_single(dx1, dh1, x, P["norm_mix_pre"], name="norm_bwd_1")
    return loss, grad_x, G


MESH = pl.DeviceIdType.MESH
_HBM = pl.BlockSpec(memory_space=pl.ANY)
N_XY = 4
N_XY_PEERS = 3


def _xy_peers(x, y):
    return [(1 - x, y), (x, 1 - y), (1 - x, 1 - y)]


def _exchange(arrays, out_shapes, plan, n_copies, *, alias, name):
    n = len(arrays)

    def body(*refs):
        ins, outs = refs[:n], refs[n:2 * n]
        send_sems, recv_sems = refs[2 * n], refs[2 * n + 1]
        x, y, c = lax.axis_index("x"), lax.axis_index("y"), lax.axis_index("c")
        sends, lands = [], []
        for k in range(n):
            for j, (src, dst, dev, land) in enumerate(plan(k, ins[k], outs[k], x, y, c)):
                sends.append(pltpu.make_async_remote_copy(
                    src_ref=src, dst_ref=dst, send_sem=send_sems.at[k, j], recv_sem=recv_sems.at[k, j],
                    device_id=dev, device_id_type=MESH))
                lands.append(pltpu.make_async_remote_copy(
                    src_ref=src, dst_ref=land, send_sem=send_sems.at[k, j], recv_sem=recv_sems.at[k, j],
                    device_id=dev, device_id_type=MESH))
        for cp in sends:
            cp.start()
        for cp in lands:
            cp.wait_recv()
        for cp in sends:
            cp.wait_send()

    return pl.pallas_call(
        body, name=name, out_shape=tuple(out_shapes),
        in_specs=[_HBM] * n, out_specs=tuple([_HBM] * n),
        input_output_aliases={k: k for k in range(n)} if alias else {},
        scratch_shapes=[pltpu.SemaphoreType.DMA((n, n_copies)), pltpu.SemaphoreType.DMA((n, n_copies))],
    )(*arrays)


def _same(arrays):
    return [jax.ShapeDtypeStruct(a.shape, a.dtype) for a in arrays]


def _fill_xy(bufs, *, name):
    def plan(k, src, dst, x, y, c):
        mine = 2 * x + y
        return [(src.at[mine, c], dst.at[mine, c], (px, py, c), dst.at[2 * px + py, c]) for px, py in _xy_peers(x, y)]

    return _exchange(bufs, _same(bufs), plan, N_XY_PEERS, alias=True, name=name)


def _fill_c(bufs, *, name):
    def plan(k, src, dst, x, y, c):
        return [(src.at[:, c], dst.at[:, c], (x, y, 1 - c), dst.at[:, 1 - c])]

    return _exchange(bufs, _same(bufs), plan, 1, alias=True, name=name)


def _send_c(srcs, *, name):
    def plan(k, src, dst, x, y, c):
        return [(src.at[:, 1 - c], dst, (x, y, 1 - c), dst)]

    outs = [jax.ShapeDtypeStruct(a.shape[:1] + a.shape[2:], a.dtype) for a in srcs]
    return _exchange(srcs, outs, plan, 1, alias=False, name=name)


def _scatter_xy(srcs, *, name):
    def plan(k, src, dst, x, y, c):
        return [(src.at[2 * px + py], dst.at[j], (px, py, c), dst.at[j]) for j, (px, py) in enumerate(_xy_peers(x, y))]

    outs = [jax.ShapeDtypeStruct((N_XY_PEERS,) + a.shape[1:], a.dtype) for a in srcs]
    return _exchange(srcs, outs, plan, N_XY_PEERS, alias=False, name=name)


PACK_COLS = 1024


def _pair_sum(g8, recv, core, *, name):
    n, _, R, C = g8.shape
    tr = _pick(R, (128, 64, 32, 16, 8))

    def body(core_ref, a_ref, b_ref, o_ref):
        o_ref[...] = a_ref[0] + b_ref[...]

    return pl.pallas_call(
        body, name=name, out_shape=jax.ShapeDtypeStruct((n, R, C), F32),
        grid_spec=pltpu.PrefetchScalarGridSpec(
            num_scalar_prefetch=1, grid=(n, R // tr),
            in_specs=[pl.BlockSpec((1, 1, tr, C), lambda s, i, core_ref: (s, core_ref[0], i, 0)),
                      pl.BlockSpec((1, tr, C), lambda s, i, core_ref: (s, i, 0))],
            out_specs=pl.BlockSpec((1, tr, C), lambda s, i, core_ref: (s, i, 0))),
        compiler_params=_cparams("parallel", "parallel"),
    )(core, g8, recv)


def _adamw_math(w, g, m, v):
    m = ADAM_B1 * m + (1.0 - ADAM_B1) * g
    v = ADAM_B2 * v + (1.0 - ADAM_B2) * (g * g)
    m_hat = m / (1.0 - ADAM_B1 ** ADAM_STEP)
    v_hat = v / (1.0 - ADAM_B2 ** ADAM_STEP)
    delta = -ADAM_LR * (m_hat / (jnp.sqrt(v_hat) + ADAM_EPS) + ADAM_WD * w)
    return delta, m, v


def _reduce_adamw(parts, w, m, v, *, own=None, own_slot=None, name):
    n, R, C = parts.shape
    tr = _pick(R, (128, 64, 32, 16, 8))
    has_own = own is not None

    def body(*refs):
        if has_own:
            _, own_ref, parts_ref, w_ref, m_ref, v_ref, g_ref, d_ref, nm_ref, nv_ref = refs
            g = own_ref[0]
            first = 0
        else:
            parts_ref, w_ref, m_ref, v_ref, g_ref, d_ref, nm_ref, nv_ref = refs
            g = parts_ref[0]
            first = 1
        for k in range(first, n):
            g = g + parts_ref[k]
        g_ref[...] = g
        d_ref[...], nm_ref[...], nv_ref[...] = _adamw_math(w_ref[...], g, m_ref[...], v_ref[...])

    out = jax.ShapeDtypeStruct((R, C), F32)
    if has_own:
        row = pl.BlockSpec((tr, C), lambda i, s: (i, 0))
        return pl.pallas_call(
            body, name=name, out_shape=(out, out, out, out),
            grid_spec=pltpu.PrefetchScalarGridSpec(
                num_scalar_prefetch=1, grid=(R // tr,),
                in_specs=[pl.BlockSpec((1, tr, C), lambda i, s: (s[0], i, 0)),
                          pl.BlockSpec((n, tr, C), lambda i, s: (0, i, 0)), row, row, row],
                out_specs=(row, row, row, row)),
            compiler_params=_cparams("parallel"),
        )(own_slot, own, parts, w, m, v)
    row = pl.BlockSpec((tr, C), lambda i: (i, 0))
    return pl.pallas_call(
        body, name=name, out_shape=(out, out, out, out), grid=(R // tr,),
        in_specs=[pl.BlockSpec((n, tr, C), lambda i: (0, i, 0)), row, row, row],
        out_specs=(row, row, row, row), compiler_params=_cparams("parallel"),
    )(parts, w, m, v)


SHARDED = (("w_in", (1024, 4096), 1), ("ssm_w_glu", (512, 512), 0), ("w_branch_attn", (512, 1024), 1),
           ("w_branch_ssm", (512, 1024), 1), ("w_out", (1024, 1024), 0), ("xa_wq", (1024, 1024), 0),
           ("xa_wk", (1024, 1024), 0), ("xa_wv", (1024, 1024), 0), ("xa_wo", (1024, 1024), 0),
           ("ffn_w_up", (1024, 5632), 1), ("ffn_conv_w", (3, 5632), 1), ("ffn_w_down", (2816, 1024), 0))
REPLICATED = (("norm_mix_pre", (1024,)), ("norm_mix_post", (1024,)), ("b_gate", (2048,)), ("ssm_a_re", (32, 64)),
              ("ssm_a_im", (32, 64)), ("ssm_log_dt", (32,)), ("ssm_b_re", (32, 64, 16)), ("ssm_b_im", (32, 64, 16)),
              ("ssm_c_re", (32, 16, 64)), ("ssm_c_im", (32, 16, 64)), ("ssm_d", (512,)), ("ssm_b_glu", (512,)),
              ("norm_xa_pre", (1024,)), ("norm_xa_post", (1024,)), ("norm_mem", (1024,)), ("norm_ffn_pre", (1024,)),
              ("norm_ffn_post", (1024,)), ("ffn_conv_b", (5632,)))
PARAM_ORDER = ("norm_mix_pre", "norm_mix_post", "w_in", "b_gate", "ssm_a_re", "ssm_a_im", "ssm_log_dt", "ssm_b_re",
               "ssm_b_im", "ssm_c_re", "ssm_c_im", "ssm_d", "ssm_w_glu", "ssm_b_glu", "w_branch_attn", "w_branch_ssm",
               "w_out", "norm_xa_pre", "norm_xa_post", "norm_mem", "xa_wq", "xa_wk", "xa_wv", "xa_wo", "norm_ffn_pre",
               "norm_ffn_post", "ffn_w_up", "ffn_conv_w", "ffn_conv_b", "ffn_w_down")
SMALL_ROWS = 160
FF_LOCAL = 2 * D_FF // N_DEV
FF_LOCAL_PAD = 768
FF_PAD = (N_DEV // 2) * FF_LOCAL_PAD


def _local_shape(shape, axis):
    return tuple(s // N_DEV if a == axis else s for a, s in enumerate(shape))


def _pad_cols(a, width):
    return jnp.pad(a, [(0, 0)] * (a.ndim - 1) + [(0, width - a.shape[-1])])


def _blocks_to_cols(a8):
    return a8.transpose(1, 0, 2).reshape(a8.shape[1], N_DEV * a8.shape[2])


def _cols_to_blocks(a, cb):
    return a.reshape(a.shape[0], N_DEV, cb).transpose(1, 0, 2)


def _pack_small(d):
    flat = jnp.concatenate([d[n].reshape(-1) for n, _ in REPLICATED])
    return _pad_cols(flat, SMALL_ROWS * PACK_COLS).reshape(SMALL_ROWS, PACK_COLS)


def _unpack_small(buf):
    flat = buf.reshape(-1)
    out, off = {}, 0
    for n, shape in REPLICATED:
        size = math.prod(shape)
        out[n] = flat[off:off + size]
        off += size
    return out


def kernel(x, mem, norm_mix_pre, norm_mix_post, w_in, b_gate, ssm_a_re, ssm_a_im, ssm_log_dt, ssm_b_re, ssm_b_im, ssm_c_re, ssm_c_im, ssm_d, ssm_w_glu, ssm_b_glu, w_branch_attn, w_branch_ssm, w_out, norm_xa_pre, norm_xa_post, norm_mem, xa_wq, xa_wk, xa_wv, xa_wo, norm_ffn_pre, norm_ffn_post, ffn_w_up, ffn_conv_w, ffn_conv_b, ffn_w_down, loss_target, m_norm_mix_pre, m_norm_mix_post, m_w_in, m_b_gate, m_ssm_a_re, m_ssm_a_im, m_ssm_log_dt, m_ssm_b_re, m_ssm_b_im, m_ssm_c_re, m_ssm_c_im, m_ssm_d, m_ssm_w_glu, m_ssm_b_glu, m_w_branch_attn, m_w_branch_ssm, m_w_out, m_norm_xa_pre, m_norm_xa_post, m_norm_mem, m_xa_wq, m_xa_wk, m_xa_wv, m_xa_wo, m_norm_ffn_pre, m_norm_ffn_post, m_ffn_w_up, m_ffn_conv_w, m_ffn_conv_b, m_ffn_w_down, v_norm_mix_pre, v_norm_mix_post, v_w_in, v_b_gate, v_ssm_a_re, v_ssm_a_im, v_ssm_log_dt, v_ssm_b_re, v_ssm_b_im, v_ssm_c_re, v_ssm_c_im, v_ssm_d, v_ssm_w_glu, v_ssm_b_glu, v_w_branch_attn, v_w_branch_ssm, v_w_out, v_norm_xa_pre, v_norm_xa_post, v_norm_mem, v_xa_wq, v_xa_wk, v_xa_wv, v_xa_wo, v_norm_ffn_pre, v_norm_ffn_post, v_ffn_w_up, v_ffn_conv_w, v_ffn_conv_b, v_ffn_w_down):
    args = dict(locals())
    w_loc = {n: args[n][0] for n in PARAM_ORDER}
    m_loc = {n: args["m_" + n][0] for n in PARAM_ORDER}
    v_loc = {n: args["v_" + n][0] for n in PARAM_ORDER}
    core_i = lax.axis_index("c")
    chip_i = 2 * lax.axis_index("x") + lax.axis_index("y")
    core = core_i.astype(jnp.int32).reshape(1)
    chip = chip_i.astype(jnp.int32).reshape(1)
    sharded = [n for n, _, _ in SHARDED]
    padded = ("ffn_w_up", "ffn_conv_w")

    def in_place(a):
        buf = lax.empty((N_XY, 2) + a.shape, a.dtype)
        return lax.dynamic_update_slice(buf, a[None, None], (chip_i, core_i) + (0,) * a.ndim)

    def as_local(n, a):
        return _pad_cols(a, FF_LOCAL_PAD) if n in padded else a

    wire = [in_place(as_local(n, w_loc[n]).astype(F32 if n == "ffn_conv_w" else BF16)) for n in sharded]
    wire = _fill_c(_fill_xy(wire, name="gather_w_xy"), name="gather_w_c")
    full = {n: b.reshape((N_DEV,) + b.shape[2:]) for n, b in zip(sharded, wire)}
    W = {}
    for n, shape, ax in SHARDED:
        W[n] = full[n].reshape(shape) if ax == 0 else full[n]
    for n in ("w_branch_attn", "w_branch_ssm", "ffn_conv_w"):
        W[n] = _blocks_to_cols(full[n])
    W["ffn_w_down"] = jnp.pad(W["ffn_w_down"].reshape(N_DEV // 2, FF_LOCAL, D_MODEL),
                              ((0, 0), (0, FF_LOCAL_PAD - FF_LOCAL), (0, 0))).reshape(FF_PAD, D_MODEL)

    P = {}
    for n, shape in REPLICATED:
        P[n] = w_loc[n] if len(shape) > 1 or n == "ssm_log_dt" else w_loc[n].reshape(1, -1)
    P["ffn_conv_b"] = _pad_cols(w_loc["ffn_conv_b"].reshape(N_DEV, FF_LOCAL), FF_LOCAL_PAD).reshape(1, 2 * FF_PAD)

    loss, grad_x, G = _local_step(x[0], mem[0], loss_target[0], W, P)
    loss = lax.psum(loss[0, 0], ("x", "y", "c"))

    G["w_branch_attn"] = _cols_to_blocks(G["w_branch_attn"], D_MODEL // N_DEV)
    G["w_branch_ssm"] = _cols_to_blocks(G["w_branch_ssm"], D_MODEL // N_DEV)
    G["ffn_conv_w"] = _cols_to_blocks(G["ffn_conv_w"], FF_LOCAL_PAD)
    G["ffn_w_down"] = G["ffn_w_down"].reshape(N_DEV // 2, FF_LOCAL_PAD, D_MODEL)[:, :FF_LOCAL]
    g8 = [G[n].reshape((N_XY, 2) + as_local(n, w_loc[n]).shape) for n in sharded]
    from_core = _send_c(g8, name="reduce_c")
    pair = [_pair_sum(g, r, core, name="pair_sum_" + n) for n, g, r in zip(sharded, g8, from_core)]
    from_chips = _scatter_xy(pair, name="reduce_xy")
    big_out = {}
    for n, own, parts in zip(sharded, pair, from_chips):
        res = _reduce_adamw(parts, as_local(n, w_loc[n]), as_local(n, m_loc[n]), as_local(n, v_loc[n]),
                            own=own, own_slot=chip, name="adamw_" + n)
        big_out[n] = [r[:, :FF_LOCAL] if n in padded else r for r in res]

    G["ffn_conv_b"] = G["ffn_conv_b"].reshape(N_DEV, FF_LOCAL_PAD)[:, :FF_LOCAL]
    parts, = _fill_c(_fill_xy([in_place(_pack_small(G))], name="gather_g_xy"), name="gather_g_c")
    parts = parts.reshape((N_DEV,) + parts.shape[2:])
    small_out = _reduce_adamw(parts, _pack_small(w_loc), _pack_small(m_loc), _pack_small(v_loc), name="adamw_replicated")
    small_out = [_unpack_small(b) for b in small_out]

    outs = [loss, grad_x[None]]
    for k in range(4):
        for n in PARAM_ORDER:
            src = big_out[n][k] if n in big_out else small_out[k][n]
            outs.append(src.reshape(args[n].shape))
    return tuple(outs)
```

```python
import math

import jax
import jax.numpy as jnp
from jax import lax
from jax.experimental import pallas as pl
from jax.experimental.pallas import tpu as pltpu

F32 = jnp.float32
BF16 = jnp.bfloat16

D_MODEL = 1024
SB_HEADS = 8
SB_HEAD_DIM = 64
SB_WIDTH = 512
SSM_WIDTH = 512
SSM_GROUP = 16
SSM_GROUPS = 32
SSM_STATE = 64
XA_HEADS = 4
XA_HEAD_DIM = 256
D_FF = 2816
RMS_EPS = 1e-6
IN_WIDTH = 4096
N_DEV = 8

ADAM_LR = 0.001
ADAM_B1 = 0.9
ADAM_B2 = 0.999
ADAM_EPS = 1e-08
ADAM_WD = 0.01
ADAM_STEP = 10

LANES = 128
SUBLANES = 8
VMEM_LIMIT = 48 * 1024 * 1024

_GELU_C = math.sqrt(2.0 / math.pi)


def _cparams(*sem):
    return pltpu.CompilerParams(dimension_semantics=sem, vmem_limit_bytes=VMEM_LIMIT)


def _pick(n, cands):
    for c in cands:
        if n % c == 0:
            return c
    return n


def _gelu(x):
    return 0.5 * x * (1.0 + jnp.tanh(_GELU_C * (x + 0.044715 * x * x * x)))


def _gelu_and_grad(x):
    t = jnp.tanh(_GELU_C * (x + 0.044715 * x * x * x))
    g = 0.5 * x * (1.0 + t)
    dg = 0.5 * (1.0 + t) + 0.5 * x * (1.0 - t * t) * _GELU_C * (1.0 + 3.0 * 0.044715 * x * x)
    return g, dg


def _sigmoid(x):
    return 1.0 / (1.0 + jnp.exp(-x))


def _dot(a, b, ca, cb):
    return lax.dot_general(a.astype(BF16), b.astype(BF16), (((ca,), (cb,)), ((), ())),
                           preferred_element_type=F32)


def _matmul(a, b, *, ta=False, tb=False, out_dtype=F32, name, b_block0=0, n_blocks=None,
            out_cb=None, out_into=None, out_block0=0, acc_in=None):
    if ta:
        K, M = a.shape
    else:
        M, K = a.shape
    b_cb = None
    if b.ndim == 3:
        b_cb = b.shape[2]
        n_blocks = b.shape[0] - b_block0 if n_blocks is None else n_blocks
        N, K2 = (b.shape[1], n_blocks * b_cb) if tb else (n_blocks * b_cb, b.shape[1])
    elif tb:
        N, K2 = b.shape
    else:
        K2, N = b.shape
    assert K == K2, (a.shape, b.shape, ta, tb)
    if out_into is not None:
        out_cb = out_into.shape[2]
    tm = _pick(M, (512, 256, 128))
    n_unit = math.gcd(N, math.gcd(b_cb if (b_cb and not tb) else N, out_cb or N))
    tn = _pick(n_unit, (768, 512, 256, 128))
    k_unit = b_cb if (b_cb and tb) else K
    tk = _pick(k_unit, (1024, 768, 512, 256, 128))
    nk = K // tk
    ca, cb = (0 if ta else 1), (1 if tb else 0)
    has_acc = acc_in is not None
    has_into = out_into is not None

    def body(*refs):
        a_ref, b_ref = refs[0], refs[1]
        pos = 2
        c_ref = None
        if has_acc:
            c_ref = refs[pos]
            pos += 1
        if has_into:
            pos += 1
        o_ref = refs[pos]
        p = _dot(a_ref[...], b_ref[...], ca, cb)
        if nk == 1:
            o_ref[...] = ((p + c_ref[...]) if has_acc else p).astype(out_dtype)
        else:
            acc_ref = refs[pos + 1]
            k = pl.program_id(2)

            @pl.when(k == 0)
            def _():
                acc_ref[...] = (p + c_ref[...]) if has_acc else p

            @pl.when(k > 0)
            def _():
                acc_ref[...] += p

            @pl.when(k == nk - 1)
            def _():
                o_ref[...] = acc_ref[...].astype(out_dtype)

    a_spec = pl.BlockSpec((tk, tm), lambda j, i, k: (k, i)) if ta else pl.BlockSpec((tm, tk), lambda j, i, k: (i, k))
    if b_cb is None:
        b_spec = pl.BlockSpec((tn, tk), lambda j, i, k: (j, k)) if tb else pl.BlockSpec((tk, tn), lambda j, i, k: (k, j))
    elif tb:
        per = b_cb // tk
        b_spec = pl.BlockSpec((None, tn, tk), lambda j, i, k: (b_block0 + k // per, j, k % per))
    else:
        per = b_cb // tn
        b_spec = pl.BlockSpec((None, tk, tn), lambda j, i, k: (b_block0 + j // per, k, j % per))
    in_specs = [a_spec, b_spec]
    operands = [a, b]
    aliases = {}
    if has_acc:
        in_specs.append(pl.BlockSpec((tm, tn), lambda j, i, k: (i, j)))
        operands.append(acc_in)
    if has_into:
        aliases = {len(operands): 0}
        in_specs.append(pl.BlockSpec(memory_space=pl.ANY))
        operands.append(out_into)
    if out_cb is None:
        out_shape = jax.ShapeDtypeStruct((M, N), out_dtype)
        out_spec = pl.BlockSpec((tm, tn), lambda j, i, k: (i, j))
    else:
        per_o = out_cb // tn
        out_shape = (jax.ShapeDtypeStruct(out_into.shape, out_into.dtype) if has_into
                     else jax.ShapeDtypeStruct((N // out_cb, M, out_cb), out_dtype))
        out_spec = pl.BlockSpec((None, tm, tn), lambda j, i, k: (out_block0 + j // per_o, i, j % per_o))
    return pl.pallas_call(
        body, name=name, out_shape=out_shape,
        grid=(N // tn, M // tm, nk),
        in_specs=in_specs, out_specs=out_spec, input_output_aliases=aliases,
        scratch_shapes=[] if nk == 1 else [pltpu.VMEM((tm, tn), F32)],
        compiler_params=_cparams("parallel", "parallel", "arbitrary"),
    )(*operands)


def _rms(x, g):
    r = lax.rsqrt(jnp.mean(x * x, axis=-1, keepdims=True) + RMS_EPS)
    return x * r * g


def _rms_bwd(dy, x, g):
    r = lax.rsqrt(jnp.mean(x * x, axis=-1, keepdims=True) + RMS_EPS)
    xh = x * r
    dxh = dy * g
    dx = r * (dxh - xh * jnp.mean(dxh * xh, axis=-1, keepdims=True))
    dg = jnp.sum(dy * xh, axis=0, keepdims=True)
    return dx, dg


def _row_tile(rows):
    return _pick(rows, (512, 256, 128, 64, 32, 16, 8))


def _rms_fwd(x, g, *, name):
    R, D = x.shape
    tr = _row_tile(R)

    def body(x_ref, g_ref, h_ref):
        h_ref[...] = _rms(x_ref[...], g_ref[...]).astype(BF16)

    return pl.pallas_call(
        body, name=name, out_shape=jax.ShapeDtypeStruct((R, D), BF16), grid=(R // tr,),
        in_specs=[pl.BlockSpec((tr, D), lambda i: (i, 0)), pl.BlockSpec((1, D), lambda i: (0, 0))],
        out_specs=pl.BlockSpec((tr, D), lambda i: (i, 0)),
        compiler_params=_cparams("parallel"),
    )(x, g)


def _resnorm_norm(x, z, g_post, g_next, *, name):
    R, D = x.shape
    tr = _row_tile(R)

    def body(x_ref, z_ref, gp_ref, gn_ref, xn_ref, h_ref):
        xn = x_ref[...] + _rms(z_ref[...], gp_ref[...])
        xn_ref[...] = xn
        h_ref[...] = _rms(xn, gn_ref[...]).astype(BF16)

    row = pl.BlockSpec((tr, D), lambda i: (i, 0))
    vec = pl.BlockSpec((1, D), lambda i: (0, 0))
    return pl.pallas_call(
        body, name=name,
        out_shape=(jax.ShapeDtypeStruct((R, D), F32), jax.ShapeDtypeStruct((R, D), BF16)),
        grid=(R // tr,), in_specs=[row, row, vec, vec], out_specs=(row, row),
        compiler_params=_cparams("parallel"),
    )(x, z, g_post, g_next)


def _final_loss(x, z, g_post, target, *, name):
    R, D = x.shape
    tr = _row_tile(R)

    def body(x_ref, z_ref, gp_ref, t_ref, loss_ref, dy_ref, dz_ref, dg_ref):
        i = pl.program_id(0)
        z = z_ref[...]
        g = gp_ref[...]
        err = x_ref[...] + _rms(z, g) - t_ref[...]
        dy = err * (1.0 / D)
        dy_ref[...] = dy
        dz, dg = _rms_bwd(dy, z, g)
        dz_ref[...] = dz.astype(BF16)
        part = 0.5 * jnp.sum(jnp.sum(err * err, axis=-1, keepdims=True) * (1.0 / D), axis=0, keepdims=True)

        @pl.when(i == 0)
        def _():
            loss_ref[...] = part
            dg_ref[...] = dg

        @pl.when(i > 0)
        def _():
            loss_ref[...] += part
            dg_ref[...] += dg

    row = pl.BlockSpec((tr, D), lambda i: (i, 0))
    vec = pl.BlockSpec((1, D), lambda i: (0, 0))
    return pl.pallas_call(
        body, name=name,
        out_shape=(jax.ShapeDtypeStruct((1, 1), F32), jax.ShapeDtypeStruct((R, D), F32),
                   jax.ShapeDtypeStruct((R, D), BF16), jax.ShapeDtypeStruct((1, D), F32)),
        grid=(R // tr,), in_specs=[row, row, vec, row],
        out_specs=(pl.BlockSpec((1, 1), lambda i: (0, 0)), row, row, vec),
        compiler_params=_cparams("arbitrary"),
    )(x, z, g_post, target)


def _norm_bwd_pair(dres, dh, xk, g_pre, zprev, g_prev_post, *, name):
    R, D = xk.shape
    tr = _row_tile(R)

    def body(dres_ref, dh_ref, x_ref, gpre_ref, z_ref, gpost_ref, dx_ref, dz_ref, dgpre_ref, dgpost_ref):
        i = pl.program_id(0)
        d1, dgpre = _rms_bwd(dh_ref[...], x_ref[...], gpre_ref[...])
        dx = dres_ref[...] + d1
        dx_ref[...] = dx
        dz, dgpost = _rms_bwd(dx, z_ref[...], gpost_ref[...])
        dz_ref[...] = dz.astype(BF16)

        @pl.when(i == 0)
        def _():
            dgpre_ref[...] = dgpre
            dgpost_ref[...] = dgpost

        @pl.when(i > 0)
        def _():
            dgpre_ref[...] += dgpre
            dgpost_ref[...] += dgpost

    row = pl.BlockSpec((tr, D), lambda i: (i, 0))
    vec = pl.BlockSpec((1, D), lambda i: (0, 0))
    return pl.pallas_call(
        body, name=name,
        out_shape=(jax.ShapeDtypeStruct((R, D), F32), jax.ShapeDtypeStruct((R, D), BF16),
                   jax.ShapeDtypeStruct((1, D), F32), jax.ShapeDtypeStruct((1, D), F32)),
        grid=(R // tr,), in_specs=[row, row, row, vec, row, vec], out_specs=(row, row, vec, vec),
        compiler_params=_cparams("arbitrary"),
    )(dres, dh, xk, g_pre, zprev, g_prev_post)


def _norm_bwd_single(dres, dh, xk, g_pre, *, name):
    R, D = xk.shape
    tr = _row_tile(R)
    has_res = dres is not None

    def body(*refs):
        if has_res:
            dres_ref, dh_ref, x_ref, gpre_ref, dx_ref, dgpre_ref = refs
        else:
            dh_ref, x_ref, gpre_ref, dx_ref, dgpre_ref = refs
        i = pl.program_id(0)
        d1, dgpre = _rms_bwd(dh_ref[...], x_ref[...], gpre_ref[...])
        dx_ref[...] = dres_ref[...] + d1 if has_res else d1

        @pl.when(i == 0)
        def _():
            dgpre_ref[...] = dgpre

        @pl.when(i > 0)
        def _():
            dgpre_ref[...] += dgpre

    row = pl.BlockSpec((tr, D), lambda i: (i, 0))
    vec = pl.BlockSpec((1, D), lambda i: (0, 0))
    ins = ([dres] if has_res else []) + [dh, xk, g_pre]
    return pl.pallas_call(
        body, name=name,
        out_shape=(jax.ShapeDtypeStruct((R, D), F32), jax.ShapeDtypeStruct((1, D), F32)),
        grid=(R // tr,), in_specs=([row] if has_res else []) + [row, row, vec], out_specs=(row, vec),
        compiler_params=_cparams("arbitrary"),
    )(*ins)


SB_BLOCK = 256
SB_QBLOCK = 1024


def _sb_tri(kind):
    r = lax.broadcasted_iota(jnp.int32, (SB_BLOCK, SB_BLOCK), 0)
    c = lax.broadcasted_iota(jnp.int32, (SB_BLOCK, SB_BLOCK), 1)
    keep = {"after": r > c, "upto": r <= c, "before": r < c}[kind]
    return jnp.where(keep, 1.0, 0.0).astype(BF16)


def _running_sum(vals, tri):
    hi = vals.astype(BF16)
    lo = (vals - hi.astype(F32)).astype(BF16)
    return _dot(hi, tri, 1, 0) + _dot(lo, tri, 1, 0)


def _sb_scores(qm, k_blk):
    z = _dot(qm, k_blk, 1, 1)
    sp = jnp.maximum(z, 0.0) + jnp.log(1.0 + jnp.exp(-jnp.abs(z)))
    return z, sp


def _sb_causal(rows):
    r = lax.broadcasted_iota(jnp.int32, (rows, SB_BLOCK), 0)
    c = lax.broadcasted_iota(jnp.int32, (rows, SB_BLOCK), 1)
    return c < r


def _head_masks():
    lane = lax.broadcasted_iota(jnp.int32, (1, LANES), 1)
    return [jnp.where(lane < SB_HEAD_DIM, 1.0, 0.0), jnp.where(lane >= SB_HEAD_DIM, 1.0, 0.0)]


def _sb_fwd(proj, *, name):
    S = proj.shape[0]
    T = SB_BLOCK
    TQ = min(SB_QBLOCK, S)
    span = TQ // T
    nq = S // TQ
    npair = SB_WIDTH // LANES
    scale = SB_HEAD_DIM ** -0.5

    def body(q_ref, k_ref, v_ref, o_ref, tot_ref, acc_ref, run_ref):
        masks = _head_masks()
        tri = _sb_tri("after")

        def q_block(i, _):
            qrow = pl.ds(pl.multiple_of(i * TQ, TQ), TQ)
            q = q_ref[qrow, :] * scale
            qm = [(q * m).astype(BF16) for m in masks]
            acc_ref[...] = jnp.zeros_like(acc_ref)
            run_ref[...] = jnp.zeros_like(run_ref)

            def k_block(j, own):
                krow = pl.ds(pl.multiple_of(j * T, T), T)
                k_blk = k_ref[krow, :].astype(BF16)
                v_blk = v_ref[krow, :].astype(BF16)
                r0 = 0 if own is None else own * T
                rows = pl.ds(r0, TQ - r0)
                for h in range(2):
                    z, sp = _sb_scores(qm[h][r0:], k_blk)
                    causal = None if own is None else _sb_causal(TQ - r0)
                    lf = -sp if causal is None else jnp.where(causal, -sp, 0.0)
                    e = jnp.exp(z - sp + _running_sum(lf, tri) + run_ref[h, rows])
                    w = e if causal is None else jnp.where(causal, e, 0.0)
                    acc_ref[h, rows] += _dot(w, v_blk, 1, 0)
                    run_ref[h, rows] += jnp.sum(lf, axis=1, keepdims=True)

            for d in reversed(range(span)):
                k_block(i * span + d, d)

            def below(jj, _):
                k_block(i * span - 1 - jj, None)
                return 0

            lax.fori_loop(0, i * span, below, 0)
            o_ref[qrow, :] = (acc_ref[0] * masks[0] + acc_ref[1] * masks[1]).astype(BF16)
            tot_ref[qrow, :] = run_ref[0] * masks[0] + run_ref[1] * masks[1]
            return 0

        lax.fori_loop(0, nq, q_block, 0)

    blk = lambda off: pl.BlockSpec((S, LANES), lambda p: (0, off + p))
    return pl.pallas_call(
        body, name=name,
        out_shape=(jax.ShapeDtypeStruct((S, SB_WIDTH), BF16), jax.ShapeDtypeStruct((S, SB_WIDTH), F32)),
        grid=(npair,),
        in_specs=[blk(0), blk(npair), blk(2 * npair)],
        out_specs=(blk(0), blk(0)),
        scratch_shapes=[pltpu.VMEM((2, TQ, LANES), F32), pltpu.VMEM((2, TQ, 1), F32)],
        compiler_params=_cparams("parallel"),
    )(proj, proj, proj)


def _sb_bwd(proj, tot, do_attn, *, name):
    S = proj.shape[0]
    T = SB_BLOCK
    TQ = min(SB_QBLOCK, S)
    span = TQ // T
    nq = S // TQ
    npair = SB_WIDTH // LANES
    scale = SB_HEAD_DIM ** -0.5

    def body(q_ref, k_ref, v_ref, tot_ref, do_ref, dq_ref, dk_ref, dv_ref,
             dqacc_ref, dkacc_ref, dvacc_ref, run_ref, grun_ref):
        masks = _head_masks()
        tri_upto = _sb_tri("upto")
        tri_before = _sb_tri("before")
        dkacc_ref[...] = jnp.zeros_like(dkacc_ref)
        dvacc_ref[...] = jnp.zeros_like(dvacc_ref)

        def q_block(i, _):
            qrow = pl.ds(pl.multiple_of(i * TQ, TQ), TQ)
            q = q_ref[qrow, :] * scale
            do = do_ref[qrow, :].astype(F32)
            tot = tot_ref[qrow, :]
            qm = [(q * m).astype(BF16) for m in masks]
            dom = [(do * m).astype(BF16) for m in masks]
            ltot = [jnp.sum(tot * m, axis=1, keepdims=True) * (1.0 / SB_HEAD_DIM) for m in masks]
            dqacc_ref[...] = jnp.zeros_like(dqacc_ref)
            run_ref[...] = jnp.zeros_like(run_ref)
            grun_ref[...] = jnp.zeros_like(grun_ref)

            def k_block(j, own):
                krow = pl.ds(pl.multiple_of(j * T, T), T)
                k_blk = k_ref[krow, :].astype(BF16)
                v_blk = v_ref[krow, :].astype(BF16)
                r0 = 0 if own is None else own * T
                rows = pl.ds(r0, TQ - r0)
                for h in range(2):
                    z, sp = _sb_scores(qm[h][r0:], k_blk)
                    causal = None if own is None else _sb_causal(TQ - r0)
                    lf = -sp if causal is None else jnp.where(causal, -sp, 0.0)
                    later = ltot[h][r0:] - run_ref[h, rows] - _running_sum(lf, tri_upto)
                    beta = jnp.exp(z - sp)
                    w = jnp.exp(z - sp + later)
                    if causal is not None:
                        w = jnp.where(causal, w, 0.0)
                    g = _dot(dom[h][r0:], v_blk, 1, 1) * w
                    gbefore = grun_ref[h, rows] + _dot(g, tri_before, 1, 0)
                    dz = g - beta * (g + gbefore)
                    if causal is not None:
                        dz = jnp.where(causal, dz, 0.0)
                    dz = dz.astype(BF16)
                    dqacc_ref[h, rows] += _dot(dz, k_blk, 1, 0)
                    dkacc_ref[krow, :] += _dot(dz, qm[h][r0:], 0, 0)
                    dvacc_ref[krow, :] += _dot(w, dom[h][r0:], 0, 0)
                    run_ref[h, rows] += jnp.sum(lf, axis=1, keepdims=True)
                    grun_ref[h, rows] += jnp.sum(g, axis=1, keepdims=True)

            def above(j, _):
                k_block(j, None)
                return 0

            lax.fori_loop(0, i * span, above, 0)
            for d in range(span):
                k_block(i * span + d, d)
            dq_ref[qrow, :] = ((dqacc_ref[0] * masks[0] + dqacc_ref[1] * masks[1]) * scale).astype(BF16)
            return 0

        lax.fori_loop(0, nq, q_block, 0)
        dk_ref[...] = dkacc_ref[...].astype(BF16)
        dv_ref[...] = dvacc_ref[...].astype(BF16)

    blk = lambda off: pl.BlockSpec((S, LANES), lambda p: (0, off + p))
    out = jax.ShapeDtypeStruct((S, SB_WIDTH), BF16)
    return pl.pallas_call(
        body, name=name, out_shape=(out, out, out), grid=(npair,),
        in_specs=[blk(0), blk(npair), blk(2 * npair), blk(0), blk(0)],
        out_specs=(blk(0), blk(0), blk(0)),
        scratch_shapes=[pltpu.VMEM((2, TQ, LANES), F32), pltpu.VMEM((S, LANES), F32), pltpu.VMEM((S, LANES), F32),
                        pltpu.VMEM((2, TQ, 1), F32), pltpu.VMEM((2, TQ, 1), F32)],
        compiler_params=_cparams("parallel"),
    )(proj, proj, proj, tot, do_attn)


SSM_HALVES = 2
SSM_HALF_CH = SSM_WIDTH // SSM_HALVES
SSM_HALF_ST = SSM_GROUPS * SSM_STATE // SSM_HALVES
SSM_CHUNK = 512


def _cmul(ar, ai, br, bi):
    return ar * br - ai * bi, ar * bi + ai * br


def _ssm_tables(lam_re, lam_im):
    lr = lam_re.reshape(-1)
    li = lam_im.reshape(-1)
    pows = [(jnp.ones_like(lr), jnp.zeros_like(li)), (lr, li)]
    for _ in range(2, SUBLANES + 1):
        pows.append(_cmul(pows[-1][0], pows[-1][1], lr, li))
    row = jnp.arange(SUBLANES)[:, None]

    def shift_tab(d, keep):
        return [jnp.where(keep, pows[d][0][None, :], 0.0), jnp.where(keep, pows[d][1][None, :], 0.0)]

    fwd, bwd = [], []
    for d in (1, 2, 4):
        fwd += shift_tab(d, row >= d)
        bwd += shift_tab(d, row + d < SUBLANES)
    fwd += [jnp.stack([pows[r + 1][0] for r in range(SUBLANES)]), jnp.stack([pows[r + 1][1] for r in range(SUBLANES)])]
    bwd += [jnp.stack([pows[SUBLANES - r][0] for r in range(SUBLANES)]),
            jnp.stack([pows[SUBLANES - r][1] for r in range(SUBLANES)])]

    def halves(tabs):
        t = jnp.stack(tabs)
        return t.reshape(8, SUBLANES, SSM_HALVES, SSM_HALF_ST).transpose(2, 0, 1, 3)

    return halves(fwd), halves(bwd)


def _ssm_fwd(proj, bd_re, bd_im, cd_re, cd_imneg, d_skip, tab, *, name):
    S = proj.shape[0]
    Tc = min(SSM_CHUNK, S)
    nc = S // Tc
    u_blk0 = (3 * SB_WIDTH) // SSM_HALF_CH

    def body(u_ref, bre_ref, bim_ref, cre_ref, cim_ref, d_ref, tab_ref, y_ref, xre_ref, xim_ref, cre_s, cim_s):
        c = pl.program_id(1)

        @pl.when(c == 0)
        def _():
            cre_s[...] = jnp.zeros_like(cre_s)
            cim_s[...] = jnp.zeros_like(cim_s)

        u = u_ref[...]
        ub = u.astype(BF16)
        xre_ref[...] = _dot(ub, bre_ref[0], 1, 0)
        xim_ref[...] = _dot(ub, bim_ref[0], 1, 0)

        def slab(k, carry):
            car_re, car_im = carry
            rows = pl.ds(pl.multiple_of(k * SUBLANES, SUBLANES), SUBLANES)
            sre = xre_ref[rows, :]
            sim = xim_ref[rows, :]
            for n, d in enumerate((1, 2, 4)):
                pre, pim = tab_ref[0, 2 * n], tab_ref[0, 2 * n + 1]
                rre = pltpu.roll(sre, d, 0)
                rim = pltpu.roll(sim, d, 0)
                sre, sim = sre + (pre * rre - pim * rim), sim + (pre * rim + pim * rre)
            pre, pim = tab_ref[0, 6], tab_ref[0, 7]
            sre, sim = sre + (pre * car_re - pim * car_im), sim + (pre * car_im + pim * car_re)
            xre_ref[rows, :] = sre
            xim_ref[rows, :] = sim
            last = (SUBLANES - 1, SUBLANES)
            return (jnp.broadcast_to(sre[last[0]:last[1], :], sre.shape),
                    jnp.broadcast_to(sim[last[0]:last[1], :], sim.shape))

        car = lax.fori_loop(0, Tc // SUBLANES, slab, (cre_s[...], cim_s[...]))
        cre_s[...] = car[0]
        cim_s[...] = car[1]
        y = _dot(xre_ref[...], cre_ref[0], 1, 0) + _dot(xim_ref[...], cim_ref[0], 1, 0)
        y_ref[...] = y + d_ref[...] * u

    return pl.pallas_call(
        body, name=name,
        out_shape=(jax.ShapeDtypeStruct((S, SSM_WIDTH), F32),
                   jax.ShapeDtypeStruct((S, SSM_HALVES * SSM_HALF_ST), F32),
                   jax.ShapeDtypeStruct((S, SSM_HALVES * SSM_HALF_ST), F32)),
        grid=(SSM_HALVES, nc),
        in_specs=[pl.BlockSpec((Tc, SSM_HALF_CH), lambda h, c: (c, u_blk0 + h)),
                  pl.BlockSpec((1, SSM_HALF_CH, SSM_HALF_ST), lambda h, c: (h, 0, 0)),
                  pl.BlockSpec((1, SSM_HALF_CH, SSM_HALF_ST), lambda h, c: (h, 0, 0)),
                  pl.BlockSpec((1, SSM_HALF_ST, SSM_HALF_CH), lambda h, c: (h, 0, 0)),
                  pl.BlockSpec((1, SSM_HALF_ST, SSM_HALF_CH), lambda h, c: (h, 0, 0)),
                  pl.BlockSpec((1, SSM_HALF_CH), lambda h, c: (0, h)),
                  pl.BlockSpec((1, 8, SUBLANES, SSM_HALF_ST), lambda h, c: (h, 0, 0, 0))],
        out_specs=(pl.BlockSpec((Tc, SSM_HALF_CH), lambda h, c: (c, h)),
                   pl.BlockSpec((Tc, SSM_HALF_ST), lambda h, c: (c, h)),
                   pl.BlockSpec((Tc, SSM_HALF_ST), lambda h, c: (c, h))),
        scratch_shapes=[pltpu.VMEM((SUBLANES, SSM_HALF_ST), F32), pltpu.VMEM((SUBLANES, SSM_HALF_ST), F32)],
        compiler_params=_cparams("parallel", "arbitrary"),
    )(proj, bd_re, bd_im, cd_re, cd_imneg, d_skip, tab)


def _ssm_bwd(dy, proj, x_re, x_im, bd_re, bd_im, cd_re, cd_imneg, d_skip, tab, *, name):
    S = proj.shape[0]
    Tc = min(SSM_CHUNK, S)
    nc = S // Tc
    u_blk0 = (3 * SB_WIDTH) // SSM_HALF_CH

    def body(dy_ref, u_ref, xre_ref, xim_ref, bre_ref, bim_ref, cre_ref, cim_ref, d_ref, tab_ref,
             du_ref, dbre_ref, dbim_ref, dcre_ref, dcim_ref, dd_ref, dlre_ref, dlim_ref,
             gre_s, gim_s, cre_s, cim_s):
        c = pl.program_id(1)

        @pl.when(c == 0)
        def _():
            cre_s[...] = jnp.zeros_like(cre_s)
            cim_s[...] = jnp.zeros_like(cim_s)
            dbre_ref[...] = jnp.zeros_like(dbre_ref)
            dbim_ref[...] = jnp.zeros_like(dbim_ref)
            dcre_ref[...] = jnp.zeros_like(dcre_ref)
            dcim_ref[...] = jnp.zeros_like(dcim_ref)
            dd_ref[...] = jnp.zeros_like(dd_ref)
            dlre_ref[...] = jnp.zeros_like(dlre_ref)
            dlim_ref[...] = jnp.zeros_like(dlim_ref)

        dy = dy_ref[...]
        dyb = dy.astype(BF16)
        u = u_ref[...]
        gre_s[...] = _dot(dyb, cre_ref[0], 1, 1)
        gim_s[...] = _dot(dyb, cim_ref[0], 1, 1)
        row = lax.broadcasted_iota(jnp.int32, (SUBLANES, SSM_HALF_ST), 0)
        nslab = Tc // SUBLANES

        def slab(kk, carry):
            car_re, car_im, acc_re, acc_im = carry
            k = nslab - 1 - kk
            rows = pl.ds(pl.multiple_of(k * SUBLANES, SUBLANES), SUBLANES)
            sre = gre_s[rows, :]
            sim = gim_s[rows, :]
            for n, d in enumerate((1, 2, 4)):
                pre, pim = tab_ref[0, 2 * n], tab_ref[0, 2 * n + 1]
                rre = pltpu.roll(sre, SUBLANES - d, 0)
                rim = pltpu.roll(sim, SUBLANES - d, 0)
                sre, sim = sre + (pre * rre + pim * rim), sim + (pre * rim - pim * rre)
            pre, pim = tab_ref[0, 6], tab_ref[0, 7]
            sre, sim = sre + (pre * car_re + pim * car_im), sim + (pre * car_im - pim * car_re)
            gre_s[rows, :] = sre
            gim_s[rows, :] = sim
            nre = jnp.where(row == SUBLANES - 1, car_re, pltpu.roll(sre, SUBLANES - 1, 0))
            nim = jnp.where(row == SUBLANES - 1, car_im, pltpu.roll(sim, SUBLANES - 1, 0))
            xr = xre_ref[rows, :]
            xi = xim_ref[rows, :]
            acc_re = acc_re + (nre * xr + nim * xi)
            acc_im = acc_im + (nim * xr - nre * xi)
            return (jnp.broadcast_to(sre[0:1, :], sre.shape), jnp.broadcast_to(sim[0:1, :], sim.shape), acc_re, acc_im)

        car = lax.fori_loop(0, nslab, slab, (cre_s[...], cim_s[...], dlre_ref[0], dlim_ref[0]))
        cre_s[...] = car[0]
        cim_s[...] = car[1]
        dlre_ref[0] = car[2]
        dlim_ref[0] = car[3]
        gre = gre_s[...].astype(BF16)
        gim = gim_s[...].astype(BF16)
        ub = u.astype(BF16)
        du = _dot(gre, bre_ref[0], 1, 1) + _dot(gim, bim_ref[0], 1, 1) + d_ref[...] * dy
        du_ref[...] = du.astype(BF16)
        dbre_ref[0] += _dot(ub, gre, 0, 0)
        dbim_ref[0] += _dot(ub, gim, 0, 0)
        dcre_ref[0] += _dot(xre_ref[...], dyb, 0, 0)
        dcim_ref[0] += _dot(xim_ref[...], dyb, 0, 0)
        dd_ref[...] += jnp.sum(dy * u, axis=0, keepdims=True)

    rev = lambda c: nc - 1 - c
    return pl.pallas_call(
        body, name=name,
        out_shape=(jax.ShapeDtypeStruct((S, SSM_WIDTH), BF16),
                   jax.ShapeDtypeStruct((SSM_HALVES, SSM_HALF_CH, SSM_HALF_ST), F32),
                   jax.ShapeDtypeStruct((SSM_HALVES, SSM_HALF_CH, SSM_HALF_ST), F32),
                   jax.ShapeDtypeStruct((SSM_HALVES, SSM_HALF_ST, SSM_HALF_CH), F32),
                   jax.ShapeDtypeStruct((SSM_HALVES, SSM_HALF_ST, SSM_HALF_CH), F32),
                   jax.ShapeDtypeStruct((1, SSM_WIDTH), F32),
                   jax.ShapeDtypeStruct((SSM_HALVES, SUBLANES, SSM_HALF_ST), F32),
                   jax.ShapeDtypeStruct((SSM_HALVES, SUBLANES, SSM_HALF_ST), F32)),
        grid=(SSM_HALVES, nc),
        in_specs=[pl.BlockSpec((Tc, SSM_HALF_CH), lambda h, c: (rev(c), h)),
                  pl.BlockSpec((Tc, SSM_HALF_CH), lambda h, c: (rev(c), u_blk0 + h)),
                  pl.BlockSpec((Tc, SSM_HALF_ST), lambda h, c: (rev(c), h)),
                  pl.BlockSpec((Tc, SSM_HALF_ST), lambda h, c: (rev(c), h)),
                  pl.BlockSpec((1, SSM_HALF_CH, SSM_HALF_ST), lambda h, c: (h, 0, 0)),
                  pl.BlockSpec((1, SSM_HALF_CH, SSM_HALF_ST), lambda h, c: (h, 0, 0)),
                  pl.BlockSpec((1, SSM_HALF_ST, SSM_HALF_CH), lambda h, c: (h, 0, 0)),
                  pl.BlockSpec((1, SSM_HALF_ST, SSM_HALF_CH), lambda h, c: (h, 0, 0)),
                  pl.BlockSpec((1, SSM_HALF_CH), lambda h, c: (0, h)),
                  pl.BlockSpec((1, 8, SUBLANES, SSM_HALF_ST), lambda h, c: (h, 0, 0, 0))],
        out_specs=(pl.BlockSpec((Tc, SSM_HALF_CH), lambda h, c: (rev(c), h)),
                   pl.BlockSpec((1, SSM_HALF_CH, SSM_HALF_ST), lambda h, c: (h, 0, 0)),
                   pl.BlockSpec((1, SSM_HALF_CH, SSM_HALF_ST), lambda h, c: (h, 0, 0)),
                   pl.BlockSpec((1, SSM_HALF_ST, SSM_HALF_CH), lambda h, c: (h, 0, 0)),
                   pl.BlockSpec((1, SSM_HALF_ST, SSM_HALF_CH), lambda h, c: (h, 0, 0)),
                   pl.BlockSpec((1, SSM_HALF_CH), lambda h, c: (0, h)),
                   pl.BlockSpec((1, SUBLANES, SSM_HALF_ST), lambda h, c: (h, 0, 0)),
                   pl.BlockSpec((1, SUBLANES, SSM_HALF_ST), lambda h, c: (h, 0, 0))),
        scratch_shapes=[pltpu.VMEM((Tc, SSM_HALF_ST), F32), pltpu.VMEM((Tc, SSM_HALF_ST), F32),
                        pltpu.VMEM((SUBLANES, SSM_HALF_ST), F32), pltpu.VMEM((SUBLANES, SSM_HALF_ST), F32)],
        compiler_params=_cparams("parallel", "arbitrary"),
    )(dy, proj, x_re, x_im, bd_re, bd_im, cd_re, cd_imneg, d_skip, tab)


def _ssm_prepare(a_re, a_im, log_dt, b_re, b_im):
    dt = jnp.exp(log_dt)[:, None]
    mag = jnp.exp(a_re * dt)
    lre = mag * jnp.cos(a_im * dt)
    lim = mag * jnp.sin(a_im * dt)
    den = a_re * a_re + a_im * a_im
    fre = ((lre - 1.0) * a_re + lim * a_im) / den
    fim = (lim * a_re - (lre - 1.0) * a_im) / den
    bbre = fre[:, :, None] * b_re - fim[:, :, None] * b_im
    bbim = fre[:, :, None] * b_im + fim[:, :, None] * b_re
    return lre, lim, bbre, bbim


def _group_eye():
    return jnp.eye(SSM_GROUPS // SSM_HALVES, dtype=F32)


def _bd_from_bbar(bbar):
    gh = SSM_GROUPS // SSM_HALVES
    b = bbar.reshape(SSM_HALVES, gh, SSM_STATE, SSM_GROUP).transpose(0, 1, 3, 2)
    out = b[:, :, :, None, :] * _group_eye()[None, :, None, :, None]
    return out.reshape(SSM_HALVES, SSM_HALF_CH, SSM_HALF_ST)


def _bbar_from_bd(dbd):
    gh = SSM_GROUPS // SSM_HALVES
    d = dbd.reshape(SSM_HALVES, gh, SSM_GROUP, gh, SSM_STATE)
    d = jnp.sum(d * _group_eye()[None, :, None, :, None], axis=3)
    return d.transpose(0, 1, 3, 2).reshape(SSM_GROUPS, SSM_STATE, SSM_GROUP)


def _cd_from_c(cmat):
    gh = SSM_GROUPS // SSM_HALVES
    c = cmat.reshape(SSM_HALVES, gh, SSM_GROUP, SSM_STATE).transpose(0, 1, 3, 2)
    out = c[:, :, :, None, :] * _group_eye()[None, :, None, :, None]
    return out.reshape(SSM_HALVES, SSM_HALF_ST, SSM_HALF_CH)


def _c_from_cd(dcd):
    gh = SSM_GROUPS // SSM_HALVES
    d = dcd.reshape(SSM_HALVES, gh, SSM_STATE, gh, SSM_GROUP)
    d = jnp.sum(d * _group_eye()[None, :, None, :, None], axis=3)
    return d.transpose(0, 1, 3, 2).reshape(SSM_GROUPS, SSM_GROUP, SSM_STATE)


def _glu_fwd(y_pre, w_glu, b_glu, *, name):
    S, W = y_pre.shape
    tr = _row_tile(S)

    def body(y_ref, w_ref, b_ref, o_ref):
        yg = _gelu(y_ref[...])
        gl = _dot(yg, w_ref[...], 1, 0) + b_ref[...]
        o_ref[...] = (yg * _sigmoid(gl)).astype(BF16)

    row = pl.BlockSpec((tr, W), lambda i: (i, 0))
    return pl.pallas_call(
        body, name=name, out_shape=jax.ShapeDtypeStruct((S, W), BF16), grid=(S // tr,),
        in_specs=[row, pl.BlockSpec((W, W), lambda i: (0, 0)), pl.BlockSpec((1, W), lambda i: (0, 0))],
        out_specs=row, compiler_params=_cparams("parallel"),
    )(y_pre, w_glu, b_glu)


def _glu_bwd(y_pre, do, w_glu, b_glu, *, name):
    S, W = y_pre.shape
    tr = _row_tile(S)

    def body(y_ref, do_ref, w_ref, b_ref, dy_ref, dw_ref, db_ref):
        i = pl.program_id(0)
        yg, dyg_dy = _gelu_and_grad(y_ref[...])
        ygb = yg.astype(BF16)
        sg = _sigmoid(_dot(ygb, w_ref[...], 1, 0) + b_ref[...])
        do = do_ref[...]
        dgl = do * yg * sg * (1.0 - sg)
        dglb = dgl.astype(BF16)
        dyg = do * sg + _dot(dglb, w_ref[...], 1, 1)
        dy_ref[...] = dyg * dyg_dy
        dw = _dot(ygb, dglb, 0, 0)
        db = jnp.sum(dgl, axis=0, keepdims=True)

        @pl.when(i == 0)
        def _():
            dw_ref[...] = dw
            db_ref[...] = db

        @pl.when(i > 0)
        def _():
            dw_ref[...] += dw
            db_ref[...] += db

    row = pl.BlockSpec((tr, W), lambda i: (i, 0))
    full = pl.BlockSpec((W, W), lambda i: (0, 0))
    vec = pl.BlockSpec((1, W), lambda i: (0, 0))
    return pl.pallas_call(
        body, name=name,
        out_shape=(jax.ShapeDtypeStruct((S, W), F32), jax.ShapeDtypeStruct((W, W), F32), jax.ShapeDtypeStruct((1, W), F32)),
        grid=(S // tr,), in_specs=[row, row, full, vec], out_specs=(row, full, vec),
        compiler_params=_cparams("arbitrary"),
    )(y_pre, do, w_glu, b_glu)


GATE_COL0 = 3 * SB_WIDTH + SSM_WIDTH


def _merge_fwd(proj, o_attn, o_ssm, w_ba, w_bs, b_gate, *, name):
    S = proj.shape[0]
    D = D_MODEL
    tr = _pick(S, (256, 128, 64, 32, 16, 8))
    gb = GATE_COL0 // D

    def body(ga_ref, gs_ref, oa_ref, os_ref, wa_ref, ws_ref, ba_ref, bs_ref, m_ref):
        pa = _dot(oa_ref[...], wa_ref[...], 1, 0)
        ps = _dot(os_ref[...], ws_ref[...], 1, 0)
        sa = _sigmoid(ga_ref[...] + ba_ref[...])
        ss = _sigmoid(gs_ref[...] + bs_ref[...])
        m_ref[...] = (sa * pa + ss * ps).astype(BF16)

    return pl.pallas_call(
        body, name=name, out_shape=jax.ShapeDtypeStruct((S, D), BF16), grid=(S // tr,),
        in_specs=[pl.BlockSpec((tr, D), lambda i: (i, gb)), pl.BlockSpec((tr, D), lambda i: (i, gb + 1)),
                  pl.BlockSpec((tr, SB_WIDTH), lambda i: (i, 0)), pl.BlockSpec((tr, SSM_WIDTH), lambda i: (i, 0)),
                  pl.BlockSpec((SB_WIDTH, D), lambda i: (0, 0)), pl.BlockSpec((SSM_WIDTH, D), lambda i: (0, 0)),
                  pl.BlockSpec((1, D), lambda i: (0, 0)), pl.BlockSpec((1, D), lambda i: (0, 1))],
        out_specs=pl.BlockSpec((tr, D), lambda i: (i, 0)),
        compiler_params=_cparams("parallel"),
    )(proj, proj, o_attn, o_ssm, w_ba, w_bs, b_gate, b_gate)


def _merge_bwd(dmerged, proj, o_attn, o_ssm, w_ba, w_bs, b_gate, *, name):
    S = proj.shape[0]
    D = D_MODEL
    tr = _pick(S, (256, 128, 64, 32, 16, 8))
    gb = GATE_COL0 // D

    def body(dm_ref, ga_ref, gs_ref, oa_ref, os_ref, wa_ref, ws_ref, ba_ref, bs_ref,
             doa_ref, dos_ref, dg_ref, db_ref, dwa_ref, dws_ref):
        i = pl.program_id(0)
        dm = dm_ref[...]
        oa = oa_ref[...]
        osm = os_ref[...]
        pa = _dot(oa, wa_ref[...], 1, 0)
        ps = _dot(osm, ws_ref[...], 1, 0)
        sa = _sigmoid(ga_ref[...] + ba_ref[...])
        ss = _sigmoid(gs_ref[...] + bs_ref[...])
        dpa = (dm * sa).astype(BF16)
        dps = (dm * ss).astype(BF16)
        dga = dm * pa * sa * (1.0 - sa)
        dgs = dm * ps * ss * (1.0 - ss)
        dg_ref[:, :D] = dga.astype(BF16)
        dg_ref[:, D:] = dgs.astype(BF16)
        doa_ref[...] = _dot(dpa, wa_ref[...], 1, 1).astype(BF16)
        dos_ref[...] = _dot(dps, ws_ref[...], 1, 1)
        dwa = _dot(oa, dpa, 0, 0)
        dws = _dot(osm, dps, 0, 0)
        dba = jnp.sum(dga, axis=0, keepdims=True)
        dbs = jnp.sum(dgs, axis=0, keepdims=True)

        @pl.when(i == 0)
        def _():
            dwa_ref[...] = dwa
            dws_ref[...] = dws
            db_ref[:, :D] = dba
            db_ref[:, D:] = dbs

        @pl.when(i > 0)
        def _():
            dwa_ref[...] += dwa
            dws_ref[...] += dws
            db_ref[:, :D] += dba
            db_ref[:, D:] += dbs

    rowD = pl.BlockSpec((tr, D), lambda i: (i, 0))
    wspec = pl.BlockSpec((SB_WIDTH, D), lambda i: (0, 0))
    return pl.pallas_call(
        body, name=name,
        out_shape=(jax.ShapeDtypeStruct((S, SB_WIDTH), BF16), jax.ShapeDtypeStruct((S, SSM_WIDTH), F32),
                   jax.ShapeDtypeStruct((S, 2 * D), BF16), jax.ShapeDtypeStruct((1, 2 * D), F32),
                   jax.ShapeDtypeStruct((SB_WIDTH, D), F32), jax.ShapeDtypeStruct((SSM_WIDTH, D), F32)),
        grid=(S // tr,),
        in_specs=[rowD, pl.BlockSpec((tr, D), lambda i: (i, gb)), pl.BlockSpec((tr, D), lambda i: (i, gb + 1)),
                  pl.BlockSpec((tr, SB_WIDTH), lambda i: (i, 0)), pl.BlockSpec((tr, SSM_WIDTH), lambda i: (i, 0)),
                  wspec, wspec, pl.BlockSpec((1, D), lambda i: (0, 0)), pl.BlockSpec((1, D), lambda i: (0, 1))],
        out_specs=(pl.BlockSpec((tr, SB_WIDTH), lambda i: (i, 0)), pl.BlockSpec((tr, SSM_WIDTH), lambda i: (i, 0)),
                   pl.BlockSpec((tr, 2 * D), lambda i: (i, 0)), pl.BlockSpec((1, 2 * D), lambda i: (0, 0)),
                   wspec, wspec),
        compiler_params=_cparams("arbitrary"),
    )(dmerged, proj, proj, o_attn, o_ssm, w_ba, w_bs, b_gate, b_gate)


def _xattn_probs(q, k, h):
    cols = slice(h * XA_HEAD_DIM, (h + 1) * XA_HEAD_DIM)
    s = _dot(q[:, cols], k[:, cols], 1, 1) * (XA_HEAD_DIM ** -0.5)
    s = s - jnp.max(s, axis=-1, keepdims=True)
    e = jnp.exp(s)
    return e / jnp.sum(e, axis=-1, keepdims=True), cols


def _xattn_fwd(q2, k2, v2, *, name):
    S, D = q2.shape
    M = k2.shape[0]
    tr = _row_tile(S)

    def body(q_ref, k_ref, v_ref, o_ref):
        q = q_ref[...]
        k = k_ref[...]
        v = v_ref[...]
        for h in range(XA_HEADS):
            p, cols = _xattn_probs(q, k, h)
            o_ref[:, cols] = _dot(p, v[:, cols], 1, 0).astype(BF16)

    row = pl.BlockSpec((tr, D), lambda i: (i, 0))
    memb = pl.BlockSpec((M, D), lambda i: (0, 0))
    return pl.pallas_call(
        body, name=name, out_shape=jax.ShapeDtypeStruct((S, D), BF16), grid=(S // tr,),
        in_specs=[row, memb, memb], out_specs=row, compiler_params=_cparams("parallel"),
    )(q2, k2, v2)


def _xattn_bwd(q2, k2, v2, do2, *, name):
    S, D = q2.shape
    M = k2.shape[0]
    tr = _row_tile(S)
    scale = XA_HEAD_DIM ** -0.5

    def body(q_ref, k_ref, v_ref, do_ref, dq_ref, dk_ref, dv_ref):
        i = pl.program_id(0)

        @pl.when(i == 0)
        def _():
            dk_ref[...] = jnp.zeros_like(dk_ref)
            dv_ref[...] = jnp.zeros_like(dv_ref)

        q = q_ref[...]
        k = k_ref[...]
        v = v_ref[...]
        do = do_ref[...]
        for h in range(XA_HEADS):
            p, cols = _xattn_probs(q, k, h)
            dp = _dot(do[:, cols], v[:, cols], 1, 1)
            ds = (p * (dp - jnp.sum(dp * p, axis=-1, keepdims=True)) * scale).astype(BF16)
            dq_ref[:, cols] = _dot(ds, k[:, cols], 1, 0).astype(BF16)
            dk_ref[:, cols] += _dot(ds, q[:, cols], 0, 0)
            dv_ref[:, cols] += _dot(p, do[:, cols], 0, 0)

    row = pl.BlockSpec((tr, D), lambda i: (i, 0))
    memb = pl.BlockSpec((M, D), lambda i: (0, 0))
    return pl.pallas_call(
        body, name=name,
        out_shape=(jax.ShapeDtypeStruct((S, D), BF16), jax.ShapeDtypeStruct((M, D), F32), jax.ShapeDtypeStruct((M, D), F32)),
        grid=(S // tr,), in_specs=[row, memb, memb, row], out_specs=(row, memb, memb),
        compiler_params=_cparams("arbitrary"),
    )(q2, k2, v2, do2)


CONV_ROWS = 512


def _shift_down(ref, t0, rows, d):
    cur = ref[pl.ds(t0, rows), :]
    out = pltpu.roll(cur, d, 0)
    r = lax.broadcasted_iota(jnp.int32, cur.shape, 0)
    for e in range(d):
        src = t0 - d + e
        prev = ref[pl.ds(src, 1), :] if src >= 0 else jnp.zeros((1, cur.shape[1]), cur.dtype)
        out = jnp.where(r == e, prev, out)
    return out


def _shift_up(ref, t0, rows, d, total):
    cur = ref[pl.ds(t0, rows), :]
    out = pltpu.roll(cur, rows - d, 0)
    r = lax.broadcasted_iota(jnp.int32, cur.shape, 0)
    for e in range(d):
        src = t0 + rows + e
        nxt = ref[pl.ds(src, 1), :] if src < total else jnp.zeros((1, cur.shape[1]), cur.dtype)
        out = jnp.where(r == rows - d + e, nxt, out)
    return out


def _conv3(ref, w_ref, b_ref, t0, rows):
    return (w_ref[2:3, :] * ref[pl.ds(t0, rows), :] + w_ref[1:2, :] * _shift_down(ref, t0, rows, 1)
            + w_ref[0:1, :] * _shift_down(ref, t0, rows, 2) + b_ref[...])


def _convgate_fwd(up_g, up_v, conv_w, conv_b, *, name):
    S, H = up_g.shape
    nb = H // LANES
    R = min(CONV_ROWS, S)

    def body(g_ref, v_ref, wg_ref, wv_ref, bg_ref, bv_ref, a_ref):
        for t0 in range(0, S, R):
            cg = _conv3(g_ref, wg_ref, bg_ref, t0, R)
            cv = _conv3(v_ref, wv_ref, bv_ref, t0, R)
            a_ref[pl.ds(t0, R), :] = (_gelu(cg) * cv).astype(BF16)

    col = lambda off: pl.BlockSpec((S, LANES), lambda j: (0, off + j))
    wcol = lambda off: pl.BlockSpec((3, LANES), lambda j: (0, off + j))
    bcol = lambda off: pl.BlockSpec((1, LANES), lambda j: (0, off + j))
    return pl.pallas_call(
        body, name=name, out_shape=jax.ShapeDtypeStruct((S, H), BF16), grid=(nb,),
        in_specs=[col(0), col(0), wcol(0), wcol(nb), bcol(0), bcol(nb)],
        out_specs=col(0), compiler_params=_cparams("parallel"),
    )(up_g, up_v, conv_w, conv_w, conv_b, conv_b)


def _convgate_bwd(up_g, up_v, da, conv_w, conv_b, *, name):
    S, H = up_g.shape
    nb = H // LANES
    R = min(CONV_ROWS, S)

    def body(g_ref, v_ref, da_ref, wg_ref, wv_ref, bg_ref, bv_ref,
             dug_ref, duv_ref, dwg_ref, dwv_ref, dbg_ref, dbv_ref, dcg_s, dcv_s):
        zero3 = jnp.zeros((1, LANES), F32)
        acc = {"g": [zero3, zero3, zero3, zero3], "v": [zero3, zero3, zero3, zero3]}
        for t0 in range(0, S, R):
            cg = _conv3(g_ref, wg_ref, bg_ref, t0, R)
            cv = _conv3(v_ref, wv_ref, bv_ref, t0, R)
            da = da_ref[pl.ds(t0, R), :]
            gl, dgl = _gelu_and_grad(cg)
            dcg = da * cv * dgl
            dcv = da * gl
            dcg_s[pl.ds(t0, R), :] = dcg
            dcv_s[pl.ds(t0, R), :] = dcv
            for key, ref, dc in (("g", g_ref, dcg), ("v", v_ref, dcv)):
                a = acc[key]
                a[2] = a[2] + jnp.sum(dc * ref[pl.ds(t0, R), :], axis=0, keepdims=True)
                a[1] = a[1] + jnp.sum(dc * _shift_down(ref, t0, R, 1), axis=0, keepdims=True)
                a[0] = a[0] + jnp.sum(dc * _shift_down(ref, t0, R, 2), axis=0, keepdims=True)
                a[3] = a[3] + jnp.sum(dc, axis=0, keepdims=True)
        for key, dw_ref, db_ref in (("g", dwg_ref, dbg_ref), ("v", dwv_ref, dbv_ref)):
            a = acc[key]
            dw_ref[0:1, :] = a[0]
            dw_ref[1:2, :] = a[1]
            dw_ref[2:3, :] = a[2]
            db_ref[...] = a[3]
        for t0 in range(0, S, R):
            for dc_s, w_ref, du_ref in ((dcg_s, wg_ref, dug_ref), (dcv_s, wv_ref, duv_ref)):
                du = (w_ref[2:3, :] * dc_s[pl.ds(t0, R), :] + w_ref[1:2, :] * _shift_up(dc_s, t0, R, 1, S)
                      + w_ref[0:1, :] * _shift_up(dc_s, t0, R, 2, S))
                du_ref[pl.ds(t0, R), :] = du.astype(BF16)

    col = lambda off: pl.BlockSpec((S, LANES), lambda j: (0, off + j))
    wcol = lambda off: pl.BlockSpec((3, LANES), lambda j: (0, off + j))
    bcol = lambda off: pl.BlockSpec((1, LANES), lambda j: (0, off + j))
    return pl.pallas_call(
        body, name=name,
        out_shape=(jax.ShapeDtypeStruct((S, H), BF16), jax.ShapeDtypeStruct((S, H), BF16),
                   jax.ShapeDtypeStruct((3, H), F32), jax.ShapeDtypeStruct((3, H), F32),
                   jax.ShapeDtypeStruct((1, H), F32), jax.ShapeDtypeStruct((1, H), F32)),
        grid=(nb,),
        in_specs=[col(0), col(0), col(0), wcol(0), wcol(nb), bcol(0), bcol(nb)],
        out_specs=(col(0), col(0), wcol(0), wcol(0), bcol(0), bcol(0)),
        scratch_shapes=[pltpu.VMEM((S, LANES), F32), pltpu.VMEM((S, LANES), F32)],
        compiler_params=_cparams("parallel"),
    )(up_g, up_v, da, conv_w, conv_w, conv_b, conv_b)


def _local_step(x, mem, target, W, P):
    mm = _matmul
    h1 = _rms_fwd(x, P["norm_mix_pre"], name="rms_mix_pre")
    proj = mm(h1, W["w_in"], name="mm_in")
    o_attn, sb_tot = _sb_fwd(proj, name="sb_fwd")

    ssm_prep = lambda *a: _ssm_prepare(*a)
    (lam_re, lam_im, bb_re, bb_im), prep_vjp = jax.vjp(
        ssm_prep, P["ssm_a_re"], P["ssm_a_im"], P["ssm_log_dt"], P["ssm_b_re"], P["ssm_b_im"])
    tab_f, tab_b = _ssm_tables(lam_re, lam_im)
    bd_re = _bd_from_bbar(bb_re).astype(BF16)
    bd_im = _bd_from_bbar(bb_im).astype(BF16)
    cd_re = _cd_from_c(P["ssm_c_re"]).astype(BF16)
    cd_imneg = _cd_from_c(-P["ssm_c_im"]).astype(BF16)
    y_pre, x_re, x_im = _ssm_fwd(proj, bd_re, bd_im, cd_re, cd_imneg, P["ssm_d"], tab_f, name="ssm_fwd")
    o_ssm = _glu_fwd(y_pre, W["ssm_w_glu"], P["ssm_b_glu"], name="glu_fwd")

    merged = _merge_fwd(proj, o_attn, o_ssm, W["w_branch_attn"], W["w_branch_ssm"], P["b_gate"], name="merge_fwd")
    mo = mm(merged, W["w_out"], name="mm_out")
    x1, h2 = _resnorm_norm(x, mo, P["norm_mix_post"], P["norm_xa_pre"], name="resnorm_1")

    mem_n = _rms_fwd(mem, P["norm_mem"], name="rms_mem")
    q2 = mm(h2, W["xa_wq"], out_dtype=BF16, name="mm_xq")
    k2 = mm(mem_n, W["xa_wk"], out_dtype=BF16, name="mm_xk")
    v2 = mm(mem_n, W["xa_wv"], out_dtype=BF16, name="mm_xv")
    o2 = _xattn_fwd(q2, k2, v2, name="xattn_fwd")
    xa = mm(o2, W["xa_wo"], name="mm_xo")
    x2, h3 = _resnorm_norm(x1, xa, P["norm_xa_post"], P["norm_ffn_pre"], name="resnorm_2")

    half = N_DEV // 2
    up_g = mm(h3, W["ffn_w_up"], n_blocks=half, name="mm_up_g")
    up_v = mm(h3, W["ffn_w_up"], b_block0=half, name="mm_up_v")
    act = _convgate_fwd(up_g, up_v, W["ffn_conv_w"], P["ffn_conv_b"], name="convgate_fwd")
    f = mm(act, W["ffn_w_down"], name="mm_down")
    loss, dy, df, dg_ffn_post = _final_loss(x2, f, P["norm_ffn_post"], target, name="final_loss")

    G = {"norm_ffn_post": dg_ffn_post}
    dact = mm(df, W["ffn_w_down"], tb=True, name="mm_down_dx")
    G["ffn_w_down"] = mm(act, df, ta=True, name="mm_down_dw")
    dug, duv, dwg, dwv, dbg, dbv = _convgate_bwd(up_g, up_v, dact, W["ffn_conv_w"], P["ffn_conv_b"], name="convgate_bwd")
    G["ffn_conv_w"] = jnp.concatenate([dwg, dwv], axis=1)
    G["ffn_conv_b"] = jnp.concatenate([dbg, dbv], axis=1)
    dh3 = mm(dug, W["ffn_w_up"], tb=True, n_blocks=half, name="mm_up_g_dx")
    dh3 = mm(duv, W["ffn_w_up"], tb=True, b_block0=half, acc_in=dh3, name="mm_up_v_dx")
    dw_up = mm(h3, dug, ta=True, out_into=lax.empty(W["ffn_w_up"].shape, F32), name="mm_up_g_dw")
    G["ffn_w_up"] = mm(h3, duv, ta=True, out_into=dw_up, out_block0=half, name="mm_up_v_dw")
    dx2, dxa, G["norm_ffn_pre"], G["norm_xa_post"] = _norm_bwd_pair(
        dy, dh3, x2, P["norm_ffn_pre"], xa, P["norm_xa_post"], name="norm_bwd_3")

    G["xa_wo"] = mm(o2, dxa, ta=True, name="mm_xo_dw")
    do2 = mm(dxa, W["xa_wo"], tb=True, out_dtype=BF16, name="mm_xo_dx")
    dq2, dk2, dv2 = _xattn_bwd(q2, k2, v2, do2, name="xattn_bwd")
    G["xa_wq"] = mm(h2, dq2, ta=True, name="mm_xq_dw")
    dh2 = mm(dq2, W["xa_wq"], tb=True, name="mm_xq_dx")
    G["xa_wk"] = mm(mem_n, dk2, ta=True, name="mm_xk_dw")
    G["xa_wv"] = mm(mem_n, dv2, ta=True, name="mm_xv_dw")
    dmem_n = jnp.concatenate([dk2, dv2], axis=1)
    wkv = jnp.concatenate([W["xa_wk"], W["xa_wv"]], axis=1)
    dmem = mm(dmem_n, wkv, tb=True, name="mm_xkv_dx")
    _, G["norm_mem"] = _norm_bwd_single(None, dmem, mem, P["norm_mem"], name="norm_bwd_mem")
    dx1, dmo, G["norm_xa_pre"], G["norm_mix_post"] = _norm_bwd_pair(
        dx2, dh2, x1, P["norm_xa_pre"], mo, P["norm_mix_post"], name="norm_bwd_2")

    G["w_out"] = mm(merged, dmo, ta=True, name="mm_out_dw")
    dmerged = mm(dmo, W["w_out"], tb=True, name="mm_out_dx")
    do_attn, do_ssm, dgate, G["b_gate"], G["w_branch_attn"], G["w_branch_ssm"] = _merge_bwd(
        dmerged, proj, o_attn, o_ssm, W["w_branch_attn"], W["w_branch_ssm"], P["b_gate"], name="merge_bwd")
    dy_pre, G["ssm_w_glu"], G["ssm_b_glu"] = _glu_bwd(y_pre, do_ssm, W["ssm_w_glu"], P["ssm_b_glu"], name="glu_bwd")
    du, dbd_re, dbd_im, dcd_re, dcd_imneg, G["ssm_d"], dl_re, dl_im = _ssm_bwd(
        dy_pre, proj, x_re, x_im, bd_re, bd_im, cd_re, cd_imneg, P["ssm_d"], tab_b, name="ssm_bwd")
    G["ssm_c_re"] = _c_from_cd(dcd_re)
    G["ssm_c_im"] = -_c_from_cd(dcd_imneg)
    dlam_re = jnp.sum(dl_re, axis=1).reshape(SSM_GROUPS, SSM_STATE)
    dlam_im = jnp.sum(dl_im, axis=1).reshape(SSM_GROUPS, SSM_STATE)
    (G["ssm_a_re"], G["ssm_a_im"], G["ssm_log_dt"], G["ssm_b_re"], G["ssm_b_im"]) = prep_vjp(
        (dlam_re, dlam_im, _bbar_from_bd(dbd_re), _bbar_from_bd(dbd_im)))
    dq, dk, dv = _sb_bwd(proj, sb_tot, do_attn, name="sb_bwd")
    dproj = jnp.concatenate([dq, dk, dv, du, dgate], axis=1)
    G["w_in"] = mm(h1, dproj, ta=True, out_cb=W["w_in"].shape[2], name="mm_in_dw")
    dh1 = mm(dproj, W["w_in"], tb=True, name="mm_in_dx")
    grad_x, G["norm_mix_pre"] = _norm_bwd_single(dx1, dh1, x, P["norm_mix_pre"], name="norm_bwd_1")
    return loss, grad_x, G


MESH = pl.DeviceIdType.MESH
_HBM = pl.BlockSpec(memory_space=pl.ANY)
N_XY = 4
N_XY_PEERS = 3


def _xy_peers(x, y):
    return [(1 - x, y), (x, 1 - y), (1 - x, 1 - y)]


def _exchange(arrays, out_shapes, plan, n_copies, *, alias, name):
    n = len(arrays)

    def body(*refs):
        ins, outs = refs[:n], refs[n:2 * n]
        send_sems, recv_sems = refs[2 * n], refs[2 * n + 1]
        x, y, c = lax.axis_index("x"), lax.axis_index("y"), lax.axis_index("c")
        sends, lands = [], []
        for k in range(n):
            for j, (src, dst, dev, land) in enumerate(plan(k, ins[k], outs[k], x, y, c)):
                sends.append(pltpu.make_async_remote_copy(
                    src_ref=src, dst_ref=dst, send_sem=send_sems.at[k, j], recv_sem=recv_sems.at[k, j],
                    device_id=dev, device_id_type=MESH))
                lands.append(pltpu.make_async_remote_copy(
                    src_ref=src, dst_ref=land, send_sem=send_sems.at[k, j], recv_sem=recv_sems.at[k, j],
                    device_id=dev, device_id_type=MESH))
        for cp in sends:
            cp.start()
        for cp in lands:
            cp.wait_recv()
        for cp in sends:
            cp.wait_send()

    return pl.pallas_call(
        body, name=name, out_shape=tuple(out_shapes),
        in_specs=[_HBM] * n, out_specs=tuple([_HBM] * n),
        input_output_aliases={k: k for k in range(n)} if alias else {},
        scratch_shapes=[pltpu.SemaphoreType.DMA((n, n_copies)), pltpu.SemaphoreType.DMA((n, n_copies))],
    )(*arrays)


def _same(arrays):
    return [jax.ShapeDtypeStruct(a.shape, a.dtype) for a in arrays]


def _fill_xy(bufs, *, name):
    def plan(k, src, dst, x, y, c):
        mine = 2 * x + y
        return [(src.at[mine, c], dst.at[mine, c], (px, py, c), dst.at[2 * px + py, c]) for px, py in _xy_peers(x, y)]

    return _exchange(bufs, _same(bufs), plan, N_XY_PEERS, alias=True, name=name)


def _fill_c(bufs, *, name):
    def plan(k, src, dst, x, y, c):
        return [(src.at[:, c], dst.at[:, c], (x, y, 1 - c), dst.at[:, 1 - c])]

    return _exchange(bufs, _same(bufs), plan, 1, alias=True, name=name)


def _send_c(srcs, *, name):
    def plan(k, src, dst, x, y, c):
        return [(src.at[:, 1 - c], dst, (x, y, 1 - c), dst)]

    outs = [jax.ShapeDtypeStruct(a.shape[:1] + a.shape[2:], a.dtype) for a in srcs]
    return _exchange(srcs, outs, plan, 1, alias=False, name=name)


def _scatter_xy(srcs, *, name):
    def plan(k, src, dst, x, y, c):
        return [(src.at[2 * px + py], dst.at[j], (px, py, c), dst.at[j]) for j, (px, py) in enumerate(_xy_peers(x, y))]

    outs = [jax.ShapeDtypeStruct((N_XY_PEERS,) + a.shape[1:], a.dtype) for a in srcs]
    return _exchange(srcs, outs, plan, N_XY_PEERS, alias=False, name=name)


PACK_COLS = 1024
WIRE_DTYPE = BF16


def _pair_sum(g8, recv, core, *, name):
    n, _, R, C = g8.shape
    tr = _pick(R, (128, 64, 32, 16, 8))

    def body(core_ref, a_ref, b_ref, o_ref):
        o_ref[...] = (a_ref[0] + b_ref[...]).astype(WIRE_DTYPE)

    return pl.pallas_call(
        body, name=name, out_shape=jax.ShapeDtypeStruct((n, R, C), WIRE_DTYPE),
        grid_spec=pltpu.PrefetchScalarGridSpec(
            num_scalar_prefetch=1, grid=(n, R // tr),
            in_specs=[pl.BlockSpec((1, 1, tr, C), lambda s, i, core_ref: (s, core_ref[0], i, 0)),
                      pl.BlockSpec((1, tr, C), lambda s, i, core_ref: (s, i, 0))],
            out_specs=pl.BlockSpec((1, tr, C), lambda s, i, core_ref: (s, i, 0))),
        compiler_params=_cparams("parallel", "parallel"),
    )(core, g8, recv)


def _adamw_math(w, g, m, v):
    m = ADAM_B1 * m + (1.0 - ADAM_B1) * g
    v = ADAM_B2 * v + (1.0 - ADAM_B2) * (g * g)
    m_hat = m / (1.0 - ADAM_B1 ** ADAM_STEP)
    v_hat = v / (1.0 - ADAM_B2 ** ADAM_STEP)
    delta = -ADAM_LR * (m_hat / (jnp.sqrt(v_hat) + ADAM_EPS) + ADAM_WD * w)
    return delta, m, v


def _reduce_adamw(parts, w, m, v, *, own=None, own_slot=None, name):
    n, R, C = parts.shape
    tr = _pick(R, (128, 64, 32, 16, 8))
    has_own = own is not None

    def body(*refs):
        if has_own:
            _, own_ref, parts_ref, w_ref, m_ref, v_ref, g_ref, d_ref, nm_ref, nv_ref = refs
            g = own_ref[0].astype(F32)
            first = 0
        else:
            parts_ref, w_ref, m_ref, v_ref, g_ref, d_ref, nm_ref, nv_ref = refs
            g = parts_ref[0]
            first = 1
        for k in range(first, n):
            g = g + parts_ref[k].astype(F32)
        g_ref[...] = g
        d_ref[...], nm_ref[...], nv_ref[...] = _adamw_math(w_ref[...], g, m_ref[...], v_ref[...])

    out = jax.ShapeDtypeStruct((R, C), F32)
    if has_own:
        row = pl.BlockSpec((tr, C), lambda i, s: (i, 0))
        return pl.pallas_call(
            body, name=name, out_shape=(out, out, out, out),
            grid_spec=pltpu.PrefetchScalarGridSpec(
                num_scalar_prefetch=1, grid=(R // tr,),
                in_specs=[pl.BlockSpec((1, tr, C), lambda i, s: (s[0], i, 0)),
                          pl.BlockSpec((n, tr, C), lambda i, s: (0, i, 0)), row, row, row],
                out_specs=(row, row, row, row)),
            compiler_params=_cparams("parallel"),
        )(own_slot, own, parts, w, m, v)
    row = pl.BlockSpec((tr, C), lambda i: (i, 0))
    return pl.pallas_call(
        body, name=name, out_shape=(out, out, out, out), grid=(R // tr,),
        in_specs=[pl.BlockSpec((n, tr, C), lambda i: (0, i, 0)), row, row, row],
        out_specs=(row, row, row, row), compiler_params=_cparams("parallel"),
    )(parts, w, m, v)


SHARDED = (("w_in", (1024, 4096), 1), ("ssm_w_glu", (512, 512), 0), ("w_branch_attn", (512, 1024), 1),
           ("w_branch_ssm", (512, 1024), 1), ("w_out", (1024, 1024), 0), ("xa_wq", (1024, 1024), 0),
           ("xa_wk", (1024, 1024), 0), ("xa_wv", (1024, 1024), 0), ("xa_wo", (1024, 1024), 0),
           ("ffn_w_up", (1024, 5632), 1), ("ffn_conv_w", (3, 5632), 1), ("ffn_w_down", (2816, 1024), 0))
REPLICATED = (("norm_mix_pre", (1024,)), ("norm_mix_post", (1024,)), ("b_gate", (2048,)), ("ssm_a_re", (32, 64)),
              ("ssm_a_im", (32, 64)), ("ssm_log_dt", (32,)), ("ssm_b_re", (32, 64, 16)), ("ssm_b_im", (32, 64, 16)),
              ("ssm_c_re", (32, 16, 64)), ("ssm_c_im", (32, 16, 64)), ("ssm_d", (512,)), ("ssm_b_glu", (512,)),
              ("norm_xa_pre", (1024,)), ("norm_xa_post", (1024,)), ("norm_mem", (1024,)), ("norm_ffn_pre", (1024,)),
              ("norm_ffn_post", (1024,)), ("ffn_conv_b", (5632,)))
PARAM_ORDER = ("norm_mix_pre", "norm_mix_post", "w_in", "b_gate", "ssm_a_re", "ssm_a_im", "ssm_log_dt", "ssm_b_re",
               "ssm_b_im", "ssm_c_re", "ssm_c_im", "ssm_d", "ssm_w_glu", "ssm_b_glu", "w_branch_attn", "w_branch_ssm",
               "w_out", "norm_xa_pre", "norm_xa_post", "norm_mem", "xa_wq", "xa_wk", "xa_wv", "xa_wo", "norm_ffn_pre",
               "norm_ffn_post", "ffn_w_up", "ffn_conv_w", "ffn_conv_b", "ffn_w_down")
SMALL_ROWS = 160
FF_LOCAL = 2 * D_FF // N_DEV
FF_LOCAL_PAD = 768
FF_PAD = (N_DEV // 2) * FF_LOCAL_PAD


def _local_shape(shape, axis):
    return tuple(s // N_DEV if a == axis else s for a, s in enumerate(shape))


def _pad_cols(a, width):
    return jnp.pad(a, [(0, 0)] * (a.ndim - 1) + [(0, width - a.shape[-1])])


def _blocks_to_cols(a8):
    return a8.transpose(1, 0, 2).reshape(a8.shape[1], N_DEV * a8.shape[2])


def _cols_to_blocks(a, cb):
    return a.reshape(a.shape[0], N_DEV, cb).transpose(1, 0, 2)


def _pack_small(d):
    flat = jnp.concatenate([d[n].reshape(-1) for n, _ in REPLICATED])
    return _pad_cols(flat, SMALL_ROWS * PACK_COLS).reshape(SMALL_ROWS, PACK_COLS)


def _unpack_small(buf):
    flat = buf.reshape(-1)
    out, off = {}, 0
    for n, shape in REPLICATED:
        size = math.prod(shape)
        out[n] = flat[off:off + size]
        off += size
    return out


def kernel(x, mem, norm_mix_pre, norm_mix_post, w_in, b_gate, ssm_a_re, ssm_a_im, ssm_log_dt, ssm_b_re, ssm_b_im, ssm_c_re, ssm_c_im, ssm_d, ssm_w_glu, ssm_b_glu, w_branch_attn, w_branch_ssm, w_out, norm_xa_pre, norm_xa_post, norm_mem, xa_wq, xa_wk, xa_wv, xa_wo, norm_ffn_pre, norm_ffn_post, ffn_w_up, ffn_conv_w, ffn_conv_b, ffn_w_down, loss_target, m_norm_mix_pre, m_norm_mix_post, m_w_in, m_b_gate, m_ssm_a_re, m_ssm_a_im, m_ssm_log_dt, m_ssm_b_re, m_ssm_b_im, m_ssm_c_re, m_ssm_c_im, m_ssm_d, m_ssm_w_glu, m_ssm_b_glu, m_w_branch_attn, m_w_branch_ssm, m_w_out, m_norm_xa_pre, m_norm_xa_post, m_norm_mem, m_xa_wq, m_xa_wk, m_xa_wv, m_xa_wo, m_norm_ffn_pre, m_norm_ffn_post, m_ffn_w_up, m_ffn_conv_w, m_ffn_conv_b, m_ffn_w_down, v_norm_mix_pre, v_norm_mix_post, v_w_in, v_b_gate, v_ssm_a_re, v_ssm_a_im, v_ssm_log_dt, v_ssm_b_re, v_ssm_b_im, v_ssm_c_re, v_ssm_c_im, v_ssm_d, v_ssm_w_glu, v_ssm_b_glu, v_w_branch_attn, v_w_branch_ssm, v_w_out, v_norm_xa_pre, v_norm_xa_post, v_norm_mem, v_xa_wq, v_xa_wk, v_xa_wv, v_xa_wo, v_norm_ffn_pre, v_norm_ffn_post, v_ffn_w_up, v_ffn_conv_w, v_ffn_conv_b, v_ffn_w_down):
    args = dict(locals())
    w_loc = {n: args[n][0] for n in PARAM_ORDER}
    m_loc = {n: args["m_" + n][0] for n in PARAM_ORDER}
    v_loc = {n: args["v_" + n][0] for n in PARAM_ORDER}
    core_i = lax.axis_index("c")
    chip_i = 2 * lax.axis_index("x") + lax.axis_index("y")
    core = core_i.astype(jnp.int32).reshape(1)
    chip = chip_i.astype(jnp.int32).reshape(1)
    sharded = [n for n, _, _ in SHARDED]
    padded = ("ffn_w_up", "ffn_conv_w")

    def in_place(a):
        buf = lax.empty((N_XY, 2) + a.shape, a.dtype)
        return lax.dynamic_update_slice(buf, a[None, None], (chip_i, core_i) + (0,) * a.ndim)

    def as_local(n, a):
        return _pad_cols(a, FF_LOCAL_PAD) if n in padded else a

    wire = [in_place(as_local(n, w_loc[n]).astype(F32 if n == "ffn_conv_w" else BF16)) for n in sharded]
    wire = _fill_c(_fill_xy(wire, name="gather_w_xy"), name="gather_w_c")
    full = {n: b.reshape((N_DEV,) + b.shape[2:]) for n, b in zip(sharded, wire)}
    W = {}
    for n, shape, ax in SHARDED:
        W[n] = full[n].reshape(shape) if ax == 0 else full[n]
    for n in ("w_branch_attn", "w_branch_ssm", "ffn_conv_w"):
        W[n] = _blocks_to_cols(full[n])
    W["ffn_w_down"] = jnp.pad(W["ffn_w_down"].reshape(N_DEV // 2, FF_LOCAL, D_MODEL),
                              ((0, 0), (0, FF_LOCAL_PAD - FF_LOCAL), (0, 0))).reshape(FF_PAD, D_MODEL)

    P = {}
    for n, shape in REPLICATED:
        P[n] = w_loc[n] if len(shape) > 1 or n == "ssm_log_dt" else w_loc[n].reshape(1, -1)
    P["ffn_conv_b"] = _pad_cols(w_loc["ffn_conv_b"].reshape(N_DEV, FF_LOCAL), FF_LOCAL_PAD).reshape(1, 2 * FF_PAD)

    loss, grad_x, G = _local_step(x[0], mem[0], loss_target[0], W, P)
    loss = lax.psum(loss[0, 0], ("x", "y", "c"))

    G["w_branch_attn"] = _cols_to_blocks(G["w_branch_attn"], D_MODEL // N_DEV)
    G["w_branch_ssm"] = _cols_to_blocks(G["w_branch_ssm"], D_MODEL // N_DEV)
    G["ffn_conv_w"] = _cols_to_blocks(G["ffn_conv_w"], FF_LOCAL_PAD)
    G["ffn_w_down"] = G["ffn_w_down"].reshape(N_DEV // 2, FF_LOCAL_PAD, D_MODEL)[:, :FF_LOCAL]
    g8 = [G[n].reshape((N_XY, 2) + as_local(n, w_loc[n]).shape) for n in sharded]
    from_core = _send_c(g8, name="reduce_c")
    pair = [_pair_sum(g, r, core, name="pair_sum_" + n) for n, g, r in zip(sharded, g8, from_core)]
    from_chips = _scatter_xy(pair, name="reduce_xy")
    big_out = {}
    for n, own, parts in zip(sharded, pair, from_chips):
        res = _reduce_adamw(parts, as_local(n, w_loc[n]), as_local(n, m_loc[n]), as_local(n, v_loc[n]),
                            own=own, own_slot=chip, name="adamw_" + n)
        big_out[n] = [r[:, :FF_LOCAL] if n in padded else r for r in res]

    G["ffn_conv_b"] = G["ffn_conv_b"].reshape(N_DEV, FF_LOCAL_PAD)[:, :FF_LOCAL]
    parts, = _fill_c(_fill_xy([in_place(_pack_small(G))], name="gather_g_xy"), name="gather_g_c")
    parts = parts.reshape((N_DEV,) + parts.shape[2:])
    small_out = _reduce_adamw(parts, _pack_small(w_loc), _pack_small(m_loc), _pack_small(v_loc), name="adamw_replicated")
    small_out = [_unpack_small(b) for b in small_out]

    outs = [loss, grad_x[None]]
    for k in range(4):
        for n in PARAM_ORDER:
            src = big_out[n][k] if n in big_out else small_out[k][n]
            outs.append(src.reshape(args[n].shape))
    return tuple(outs)
```

```python
import math

import jax
import jax.numpy as jnp
from jax import lax
from jax.experimental import pallas as pl
from jax.experimental.pallas import tpu as pltpu

F32 = jnp.float32
BF16 = jnp.bfloat16

D_MODEL = 1024
SB_HEADS = 8
SB_HEAD_DIM = 64
SB_WIDTH = 512
SSM_WIDTH = 512
SSM_GROUP = 16
SSM_GROUPS = 32
SSM_STATE = 64
XA_HEADS = 4
XA_HEAD_DIM = 256
D_FF = 2816
RMS_EPS = 1e-6
IN_WIDTH = 4096
N_DEV = 8

ADAM_LR = 0.001
ADAM_B1 = 0.9
ADAM_B2 = 0.999
ADAM_EPS = 1e-08
ADAM_WD = 0.01
ADAM_STEP = 10

LANES = 128
SUBLANES = 8
VMEM_LIMIT = 48 * 1024 * 1024

_GELU_C = math.sqrt(2.0 / math.pi)


def _cparams(*sem):
    return pltpu.CompilerParams(dimension_semantics=sem, vmem_limit_bytes=VMEM_LIMIT)


def _pick(n, cands):
    for c in cands:
        if n % c == 0:
            return c
    return n


def _gelu(x):
    return 0.5 * x * (1.0 + jnp.tanh(_GELU_C * (x + 0.044715 * x * x * x)))


def _gelu_and_grad(x):
    t = jnp.tanh(_GELU_C * (x + 0.044715 * x * x * x))
    g = 0.5 * x * (1.0 + t)
    dg = 0.5 * (1.0 + t) + 0.5 * x * (1.0 - t * t) * _GELU_C * (1.0 + 3.0 * 0.044715 * x * x)
    return g, dg


def _sigmoid(x):
    return 1.0 / (1.0 + jnp.exp(-x))


def _dot(a, b, ca, cb):
    return lax.dot_general(a.astype(BF16), b.astype(BF16), (((ca,), (cb,)), ((), ())),
                           preferred_element_type=F32)


MM_TILES = (1024, 768, 512, 256, 128)


def _matmul(a, b, *, ta=False, tb=False, out_dtype=F32, name, b_block0=0, n_blocks=None,
            out_cb=None, out_into=None, out_block0=0, acc_in=None):
    if ta:
        K, M = a.shape
    else:
        M, K = a.shape
    b_cb = None
    if b.ndim == 3:
        b_cb = b.shape[2]
        n_blocks = b.shape[0] - b_block0 if n_blocks is None else n_blocks
        N, K2 = (b.shape[1], n_blocks * b_cb) if tb else (n_blocks * b_cb, b.shape[1])
    elif tb:
        N, K2 = b.shape
    else:
        K2, N = b.shape
    assert K == K2, (a.shape, b.shape, ta, tb)
    if out_into is not None:
        out_cb = out_into.shape[2]
    tm = _pick(M, MM_TILES)
    n_unit = math.gcd(N, math.gcd(b_cb if (b_cb and not tb) else N, out_cb or N))
    tn = _pick(n_unit, MM_TILES)
    k_unit = b_cb if (b_cb and tb) else K
    tk = _pick(k_unit, MM_TILES)
    nk = K // tk
    ca, cb = (0 if ta else 1), (1 if tb else 0)
    has_acc = acc_in is not None
    has_into = out_into is not None

    def body(*refs):
        a_ref, b_ref = refs[0], refs[1]
        pos = 2
        c_ref = None
        if has_acc:
            c_ref = refs[pos]
            pos += 1
        if has_into:
            pos += 1
        o_ref = refs[pos]
        p = _dot(a_ref[...], b_ref[...], ca, cb)
        if nk == 1:
            o_ref[...] = ((p + c_ref[...]) if has_acc else p).astype(out_dtype)
        else:
            acc_ref = refs[pos + 1]
            k = pl.program_id(2)

            @pl.when(k == 0)
            def _():
                acc_ref[...] = (p + c_ref[...]) if has_acc else p

            @pl.when(k > 0)
            def _():
                acc_ref[...] += p

            @pl.when(k == nk - 1)
            def _():
                o_ref[...] = acc_ref[...].astype(out_dtype)

    a_spec = pl.BlockSpec((tk, tm), lambda j, i, k: (k, i)) if ta else pl.BlockSpec((tm, tk), lambda j, i, k: (i, k))
    if b_cb is None:
        b_spec = pl.BlockSpec((tn, tk), lambda j, i, k: (j, k)) if tb else pl.BlockSpec((tk, tn), lambda j, i, k: (k, j))
    elif tb:
        per = b_cb // tk
        b_spec = pl.BlockSpec((None, tn, tk), lambda j, i, k: (b_block0 + k // per, j, k % per))
    else:
        per = b_cb // tn
        b_spec = pl.BlockSpec((None, tk, tn), lambda j, i, k: (b_block0 + j // per, k, j % per))
    in_specs = [a_spec, b_spec]
    operands = [a, b]
    aliases = {}
    if has_acc:
        in_specs.append(pl.BlockSpec((tm, tn), lambda j, i, k: (i, j)))
        operands.append(acc_in)
    if has_into:
        aliases = {len(operands): 0}
        in_specs.append(pl.BlockSpec(memory_space=pl.ANY))
        operands.append(out_into)
    if out_cb is None:
        out_shape = jax.ShapeDtypeStruct((M, N), out_dtype)
        out_spec = pl.BlockSpec((tm, tn), lambda j, i, k: (i, j))
    else:
        per_o = out_cb // tn
        out_shape = (jax.ShapeDtypeStruct(out_into.shape, out_into.dtype) if has_into
                     else jax.ShapeDtypeStruct((N // out_cb, M, out_cb), out_dtype))
        out_spec = pl.BlockSpec((None, tm, tn), lambda j, i, k: (out_block0 + j // per_o, i, j % per_o))
    return pl.pallas_call(
        body, name=name, out_shape=out_shape,
        grid=(N // tn, M // tm, nk),
        in_specs=in_specs, out_specs=out_spec, input_output_aliases=aliases,
        scratch_shapes=[] if nk == 1 else [pltpu.VMEM((tm, tn), F32)],
        compiler_params=_cparams("parallel", "parallel", "arbitrary"),
    )(*operands)


def _rms(x, g):
    r = lax.rsqrt(jnp.mean(x * x, axis=-1, keepdims=True) + RMS_EPS)
    return x * r * g


def _rms_bwd(dy, x, g):
    r = lax.rsqrt(jnp.mean(x * x, axis=-1, keepdims=True) + RMS_EPS)
    xh = x * r
    dxh = dy * g
    dx = r * (dxh - xh * jnp.mean(dxh * xh, axis=-1, keepdims=True))
    dg = jnp.sum(dy * xh, axis=0, keepdims=True)
    return dx, dg


def _row_tile(rows):
    return _pick(rows, (512, 256, 128, 64, 32, 16, 8))


def _rms_fwd(x, g, *, name):
    R, D = x.shape
    tr = _row_tile(R)

    def body(x_ref, g_ref, h_ref):
        h_ref[...] = _rms(x_ref[...], g_ref[...]).astype(BF16)

    return pl.pallas_call(
        body, name=name, out_shape=jax.ShapeDtypeStruct((R, D), BF16), grid=(R // tr,),
        in_specs=[pl.BlockSpec((tr, D), lambda i: (i, 0)), pl.BlockSpec((1, D), lambda i: (0, 0))],
        out_specs=pl.BlockSpec((tr, D), lambda i: (i, 0)),
        compiler_params=_cparams("parallel"),
    )(x, g)


def _resnorm_norm(x, z, g_post, g_next, *, name):
    R, D = x.shape
    tr = _row_tile(R)

    def body(x_ref, z_ref, gp_ref, gn_ref, xn_ref, h_ref):
        xn = x_ref[...] + _rms(z_ref[...], gp_ref[...])
        xn_ref[...] = xn
        h_ref[...] = _rms(xn, gn_ref[...]).astype(BF16)

    row = pl.BlockSpec((tr, D), lambda i: (i, 0))
    vec = pl.BlockSpec((1, D), lambda i: (0, 0))
    return pl.pallas_call(
        body, name=name,
        out_shape=(jax.ShapeDtypeStruct((R, D), F32), jax.ShapeDtypeStruct((R, D), BF16)),
        grid=(R // tr,), in_specs=[row, row, vec, vec], out_specs=(row, row),
        compiler_params=_cparams("parallel"),
    )(x, z, g_post, g_next)


def _final_loss(x, z, g_post, target, *, name):
    R, D = x.shape
    tr = _row_tile(R)

    def body(x_ref, z_ref, gp_ref, t_ref, loss_ref, dy_ref, dz_ref, dg_ref):
        i = pl.program_id(0)
        z = z_ref[...]
        g = gp_ref[...]
        err = x_ref[...] + _rms(z, g) - t_ref[...]
        dy = err * (1.0 / D)
        dy_ref[...] = dy
        dz, dg = _rms_bwd(dy, z, g)
        dz_ref[...] = dz.astype(BF16)
        part = 0.5 * jnp.sum(jnp.sum(err * err, axis=-1, keepdims=True) * (1.0 / D), axis=0, keepdims=True)

        @pl.when(i == 0)
        def _():
            loss_ref[...] = part
            dg_ref[...] = dg

        @pl.when(i > 0)
        def _():
            loss_ref[...] += part
            dg_ref[...] += dg

    row = pl.BlockSpec((tr, D), lambda i: (i, 0))
    vec = pl.BlockSpec((1, D), lambda i: (0, 0))
    return pl.pallas_call(
        body, name=name,
        out_shape=(jax.ShapeDtypeStruct((1, 1), F32), jax.ShapeDtypeStruct((R, D), F32),
                   jax.ShapeDtypeStruct((R, D), BF16), jax.ShapeDtypeStruct((1, D), F32)),
        grid=(R // tr,), in_specs=[row, row, vec, row],
        out_specs=(pl.BlockSpec((1, 1), lambda i: (0, 0)), row, row, vec),
        compiler_params=_cparams("arbitrary"),
    )(x, z, g_post, target)


def _norm_bwd_pair(dres, dh, xk, g_pre, zprev, g_prev_post, *, name):
    R, D = xk.shape
    tr = _row_tile(R)

    def body(dres_ref, dh_ref, x_ref, gpre_ref, z_ref, gpost_ref, dx_ref, dz_ref, dgpre_ref, dgpost_ref):
        i = pl.program_id(0)
        d1, dgpre = _rms_bwd(dh_ref[...], x_ref[...], gpre_ref[...])
        dx = dres_ref[...] + d1
        dx_ref[...] = dx
        dz, dgpost = _rms_bwd(dx, z_ref[...], gpost_ref[...])
        dz_ref[...] = dz.astype(BF16)

        @pl.when(i == 0)
        def _():
            dgpre_ref[...] = dgpre
            dgpost_ref[...] = dgpost

        @pl.when(i > 0)
        def _():
            dgpre_ref[...] += dgpre
            dgpost_ref[...] += dgpost

    row = pl.BlockSpec((tr, D), lambda i: (i, 0))
    vec = pl.BlockSpec((1, D), lambda i: (0, 0))
    return pl.pallas_call(
        body, name=name,
        out_shape=(jax.ShapeDtypeStruct((R, D), F32), jax.ShapeDtypeStruct((R, D), BF16),
                   jax.ShapeDtypeStruct((1, D), F32), jax.ShapeDtypeStruct((1, D), F32)),
        grid=(R // tr,), in_specs=[row, row, row, vec, row, vec], out_specs=(row, row, vec, vec),
        compiler_params=_cparams("arbitrary"),
    )(dres, dh, xk, g_pre, zprev, g_prev_post)


def _norm_bwd_single(dres, dh, xk, g_pre, *, name):
    R, D = xk.shape
    tr = _row_tile(R)
    has_res = dres is not None

    def body(*refs):
        if has_res:
            dres_ref, dh_ref, x_ref, gpre_ref, dx_ref, dgpre_ref = refs
        else:
            dh_ref, x_ref, gpre_ref, dx_ref, dgpre_ref = refs
        i = pl.program_id(0)
        d1, dgpre = _rms_bwd(dh_ref[...], x_ref[...], gpre_ref[...])
        dx_ref[...] = dres_ref[...] + d1 if has_res else d1

        @pl.when(i == 0)
        def _():
            dgpre_ref[...] = dgpre

        @pl.when(i > 0)
        def _():
            dgpre_ref[...] += dgpre

    row = pl.BlockSpec((tr, D), lambda i: (i, 0))
    vec = pl.BlockSpec((1, D), lambda i: (0, 0))
    ins = ([dres] if has_res else []) + [dh, xk, g_pre]
    return pl.pallas_call(
        body, name=name,
        out_shape=(jax.ShapeDtypeStruct((R, D), F32), jax.ShapeDtypeStruct((1, D), F32)),
        grid=(R // tr,), in_specs=([row] if has_res else []) + [row, row, vec], out_specs=(row, vec),
        compiler_params=_cparams("arbitrary"),
    )(*ins)


SB_BLOCK = 256
SB_QBLOCK = 1024


def _sb_tri(kind):
    r = lax.broadcasted_iota(jnp.int32, (SB_BLOCK, SB_BLOCK), 0)
    c = lax.broadcasted_iota(jnp.int32, (SB_BLOCK, SB_BLOCK), 1)
    keep = {"after": r > c, "upto": r <= c, "before": r < c}[kind]
    return jnp.where(keep, 1.0, 0.0).astype(BF16)


def _running_sum(vals, tri):
    hi = vals.astype(BF16)
    lo = (vals - hi.astype(F32)).astype(BF16)
    return _dot(hi, tri, 1, 0) + _dot(lo, tri, 1, 0)


def _sb_scores(qm, k_blk):
    z = _dot(qm, k_blk, 1, 1)
    sp = jnp.maximum(z, 0.0) + jnp.log(1.0 + jnp.exp(-jnp.abs(z)))
    return z, sp


def _sb_causal(rows):
    r = lax.broadcasted_iota(jnp.int32, (rows, SB_BLOCK), 0)
    c = lax.broadcasted_iota(jnp.int32, (rows, SB_BLOCK), 1)
    return c < r


def _head_masks():
    lane = lax.broadcasted_iota(jnp.int32, (1, LANES), 1)
    return [jnp.where(lane < SB_HEAD_DIM, 1.0, 0.0), jnp.where(lane >= SB_HEAD_DIM, 1.0, 0.0)]


def _sb_fwd(proj, *, name):
    S = proj.shape[0]
    T = SB_BLOCK
    TQ = min(SB_QBLOCK, S)
    span = TQ // T
    nq = S // TQ
    npair = SB_WIDTH // LANES
    scale = SB_HEAD_DIM ** -0.5

    def body(q_ref, k_ref, v_ref, o_ref, tot_ref, acc_ref, run_ref):
        masks = _head_masks()
        tri = _sb_tri("after")

        def q_block(i, _):
            qrow = pl.ds(pl.multiple_of(i * TQ, TQ), TQ)
            q = q_ref[qrow, :] * scale
            qm = [(q * m).astype(BF16) for m in masks]
            acc_ref[...] = jnp.zeros_like(acc_ref)
            run_ref[...] = jnp.zeros_like(run_ref)

            def k_block(j, own):
                krow = pl.ds(pl.multiple_of(j * T, T), T)
                k_blk = k_ref[krow, :].astype(BF16)
                v_blk = v_ref[krow, :].astype(BF16)
                r0 = 0 if own is None else own * T
                rows = pl.ds(r0, TQ - r0)
                for h in range(2):
                    z, sp = _sb_scores(qm[h][r0:], k_blk)
                    causal = None if own is None else _sb_causal(TQ - r0)
                    lf = -sp if causal is None else jnp.where(causal, -sp, 0.0)
                    e = jnp.exp(z - sp + _running_sum(lf, tri) + run_ref[h, rows])
                    w = e if causal is None else jnp.where(causal, e, 0.0)
                    acc_ref[h, rows] += _dot(w, v_blk, 1, 0)
                    run_ref[h, rows] += jnp.sum(lf, axis=1, keepdims=True)

            for d in reversed(range(span)):
                k_block(i * span + d, d)

            def below(jj, _):
                k_block(i * span - 1 - jj, None)
                return 0

            lax.fori_loop(0, i * span, below, 0)
            o_ref[qrow, :] = (acc_ref[0] * masks[0] + acc_ref[1] * masks[1]).astype(BF16)
            tot_ref[qrow, :] = run_ref[0] * masks[0] + run_ref[1] * masks[1]
            return 0

        lax.fori_loop(0, nq, q_block, 0)

    blk = lambda off: pl.BlockSpec((S, LANES), lambda p: (0, off + p))
    return pl.pallas_call(
        body, name=name,
        out_shape=(jax.ShapeDtypeStruct((S, SB_WIDTH), BF16), jax.ShapeDtypeStruct((S, SB_WIDTH), F32)),
        grid=(npair,),
        in_specs=[blk(0), blk(npair), blk(2 * npair)],
        out_specs=(blk(0), blk(0)),
        scratch_shapes=[pltpu.VMEM((2, TQ, LANES), F32), pltpu.VMEM((2, TQ, 1), F32)],
        compiler_params=_cparams("parallel"),
    )(proj, proj, proj)


def _sb_bwd(proj, tot, do_attn, *, name):
    S = proj.shape[0]
    T = SB_BLOCK
    TQ = min(SB_QBLOCK, S)
    span = TQ // T
    nq = S // TQ
    npair = SB_WIDTH // LANES
    scale = SB_HEAD_DIM ** -0.5

    def body(q_ref, k_ref, v_ref, tot_ref, do_ref, dq_ref, dk_ref, dv_ref,
             dqacc_ref, dkacc_ref, dvacc_ref, run_ref, grun_ref):
        masks = _head_masks()
        tri_upto = _sb_tri("upto")
        tri_before = _sb_tri("before")
        dkacc_ref[...] = jnp.zeros_like(dkacc_ref)
        dvacc_ref[...] = jnp.zeros_like(dvacc_ref)

        def q_block(i, _):
            qrow = pl.ds(pl.multiple_of(i * TQ, TQ), TQ)
            q = q_ref[qrow, :] * scale
            do = do_ref[qrow, :].astype(F32)
            tot = tot_ref[qrow, :]
            qm = [(q * m).astype(BF16) for m in masks]
            dom = [(do * m).astype(BF16) for m in masks]
            ltot = [jnp.sum(tot * m, axis=1, keepdims=True) * (1.0 / SB_HEAD_DIM) for m in masks]
            dqacc_ref[...] = jnp.zeros_like(dqacc_ref)
            run_ref[...] = jnp.zeros_like(run_ref)
            grun_ref[...] = jnp.zeros_like(grun_ref)

            def k_block(j, own):
                krow = pl.ds(pl.multiple_of(j * T, T), T)
                k_blk = k_ref[krow, :].astype(BF16)
                v_blk = v_ref[krow, :].astype(BF16)
                r0 = 0 if own is None else own * T
                rows = pl.ds(r0, TQ - r0)
                for h in range(2):
                    z, sp = _sb_scores(qm[h][r0:], k_blk)
                    causal = None if own is None else _sb_causal(TQ - r0)
                    lf = -sp if causal is None else jnp.where(causal, -sp, 0.0)
                    later = ltot[h][r0:] - run_ref[h, rows] - _running_sum(lf, tri_upto)
                    beta = jnp.exp(z - sp)
                    w = jnp.exp(z - sp + later)
                    if causal is not None:
                        w = jnp.where(causal, w, 0.0)
                    g = _dot(dom[h][r0:], v_blk, 1, 1) * w
                    gbefore = grun_ref[h, rows] + _dot(g, tri_before, 1, 0)
                    dz = g - beta * (g + gbefore)
                    if causal is not None:
                        dz = jnp.where(causal, dz, 0.0)
                    dz = dz.astype(BF16)
                    dqacc_ref[h, rows] += _dot(dz, k_blk, 1, 0)
                    dkacc_ref[krow, :] += _dot(dz, qm[h][r0:], 0, 0)
                    dvacc_ref[krow, :] += _dot(w, dom[h][r0:], 0, 0)
                    run_ref[h, rows] += jnp.sum(lf, axis=1, keepdims=True)
                    grun_ref[h, rows] += jnp.sum(g, axis=1, keepdims=True)

            def above(j, _):
                k_block(j, None)
                return 0

            lax.fori_loop(0, i * span, above, 0)
            for d in range(span):
                k_block(i * span + d, d)
            dq_ref[qrow, :] = ((dqacc_ref[0] * masks[0] + dqacc_ref[1] * masks[1]) * scale).astype(BF16)
            return 0

        lax.fori_loop(0, nq, q_block, 0)
        dk_ref[...] = dkacc_ref[...].astype(BF16)
        dv_ref[...] = dvacc_ref[...].astype(BF16)

    blk = lambda off: pl.BlockSpec((S, LANES), lambda p: (0, off + p))
    out = jax.ShapeDtypeStruct((S, SB_WIDTH), BF16)
    return pl.pallas_call(
        body, name=name, out_shape=(out, out, out), grid=(npair,),
        in_specs=[blk(0), blk(npair), blk(2 * npair), blk(0), blk(0)],
        out_specs=(blk(0), blk(0), blk(0)),
        scratch_shapes=[pltpu.VMEM((2, TQ, LANES), F32), pltpu.VMEM((S, LANES), F32), pltpu.VMEM((S, LANES), F32),
                        pltpu.VMEM((2, TQ, 1), F32), pltpu.VMEM((2, TQ, 1), F32)],
        compiler_params=_cparams("parallel"),
    )(proj, proj, proj, tot, do_attn)


SSM_HALVES = 2
SSM_HALF_CH = SSM_WIDTH // SSM_HALVES
SSM_HALF_ST = SSM_GROUPS * SSM_STATE // SSM_HALVES
SSM_CHUNK = 512


def _cmul(ar, ai, br, bi):
    return ar * br - ai * bi, ar * bi + ai * br


def _ssm_tables(lam_re, lam_im):
    lr = lam_re.reshape(-1)
    li = lam_im.reshape(-1)
    pows = [(jnp.ones_like(lr), jnp.zeros_like(li)), (lr, li)]
    for _ in range(2, SUBLANES + 1):
        pows.append(_cmul(pows[-1][0], pows[-1][1], lr, li))
    row = jnp.arange(SUBLANES)[:, None]

    def shift_tab(d, keep):
        return [jnp.where(keep, pows[d][0][None, :], 0.0), jnp.where(keep, pows[d][1][None, :], 0.0)]

    fwd, bwd = [], []
    for d in (1, 2, 4):
        fwd += shift_tab(d, row >= d)
        bwd += shift_tab(d, row + d < SUBLANES)
    fwd += [jnp.stack([pows[r + 1][0] for r in range(SUBLANES)]), jnp.stack([pows[r + 1][1] for r in range(SUBLANES)])]
    bwd += [jnp.stack([pows[SUBLANES - r][0] for r in range(SUBLANES)]),
            jnp.stack([pows[SUBLANES - r][1] for r in range(SUBLANES)])]

    def halves(tabs):
        t = jnp.stack(tabs)
        return t.reshape(8, SUBLANES, SSM_HALVES, SSM_HALF_ST).transpose(2, 0, 1, 3)

    return halves(fwd), halves(bwd)


def _ssm_fwd(proj, bd_re, bd_im, cd_re, cd_imneg, d_skip, tab, *, name):
    S = proj.shape[0]
    Tc = min(SSM_CHUNK, S)
    nc = S // Tc
    u_blk0 = (3 * SB_WIDTH) // SSM_HALF_CH

    def body(u_ref, bre_ref, bim_ref, cre_ref, cim_ref, d_ref, tab_ref, y_ref, xre_ref, xim_ref, cre_s, cim_s):
        c = pl.program_id(1)

        @pl.when(c == 0)
        def _():
            cre_s[...] = jnp.zeros_like(cre_s)
            cim_s[...] = jnp.zeros_like(cim_s)

        u = u_ref[...]
        ub = u.astype(BF16)
        xre_ref[...] = _dot(ub, bre_ref[0], 1, 0)
        xim_ref[...] = _dot(ub, bim_ref[0], 1, 0)

        def slab(k, carry):
            car_re, car_im = carry
            rows = pl.ds(pl.multiple_of(k * SUBLANES, SUBLANES), SUBLANES)
            sre = xre_ref[rows, :]
            sim = xim_ref[rows, :]
            for n, d in enumerate((1, 2, 4)):
                pre, pim = tab_ref[0, 2 * n], tab_ref[0, 2 * n + 1]
                rre = pltpu.roll(sre, d, 0)
                rim = pltpu.roll(sim, d, 0)
                sre, sim = sre + (pre * rre - pim * rim), sim + (pre * rim + pim * rre)
            pre, pim = tab_ref[0, 6], tab_ref[0, 7]
            sre, sim = sre + (pre * car_re - pim * car_im), sim + (pre * car_im + pim * car_re)
            xre_ref[rows, :] = sre
            xim_ref[rows, :] = sim
            last = (SUBLANES - 1, SUBLANES)
            return (jnp.broadcast_to(sre[last[0]:last[1], :], sre.shape),
                    jnp.broadcast_to(sim[last[0]:last[1], :], sim.shape))

        car = lax.fori_loop(0, Tc // SUBLANES, slab, (cre_s[...], cim_s[...]))
        cre_s[...] = car[0]
        cim_s[...] = car[1]
        y = _dot(xre_ref[...], cre_ref[0], 1, 0) + _dot(xim_ref[...], cim_ref[0], 1, 0)
        y_ref[...] = y + d_ref[...] * u

    return pl.pallas_call(
        body, name=name,
        out_shape=(jax.ShapeDtypeStruct((S, SSM_WIDTH), F32),
                   jax.ShapeDtypeStruct((S, SSM_HALVES * SSM_HALF_ST), F32),
                   jax.ShapeDtypeStruct((S, SSM_HALVES * SSM_HALF_ST), F32)),
        grid=(SSM_HALVES, nc),
        in_specs=[pl.BlockSpec((Tc, SSM_HALF_CH), lambda h, c: (c, u_blk0 + h)),
                  pl.BlockSpec((1, SSM_HALF_CH, SSM_HALF_ST), lambda h, c: (h, 0, 0)),
                  pl.BlockSpec((1, SSM_HALF_CH, SSM_HALF_ST), lambda h, c: (h, 0, 0)),
                  pl.BlockSpec((1, SSM_HALF_ST, SSM_HALF_CH), lambda h, c: (h, 0, 0)),
                  pl.BlockSpec((1, SSM_HALF_ST, SSM_HALF_CH), lambda h, c: (h, 0, 0)),
                  pl.BlockSpec((1, SSM_HALF_CH), lambda h, c: (0, h)),
                  pl.BlockSpec((1, 8, SUBLANES, SSM_HALF_ST), lambda h, c: (h, 0, 0, 0))],
        out_specs=(pl.BlockSpec((Tc, SSM_HALF_CH), lambda h, c: (c, h)),
                   pl.BlockSpec((Tc, SSM_HALF_ST), lambda h, c: (c, h)),
                   pl.BlockSpec((Tc, SSM_HALF_ST), lambda h, c: (c, h))),
        scratch_shapes=[pltpu.VMEM((SUBLANES, SSM_HALF_ST), F32), pltpu.VMEM((SUBLANES, SSM_HALF_ST), F32)],
        compiler_params=_cparams("parallel", "arbitrary"),
    )(proj, bd_re, bd_im, cd_re, cd_imneg, d_skip, tab)


def _ssm_bwd(dy, proj, x_re, x_im, bd_re, bd_im, cd_re, cd_imneg, d_skip, tab, *, name):
    S = proj.shape[0]
    Tc = min(SSM_CHUNK, S)
    nc = S // Tc
    u_blk0 = (3 * SB_WIDTH) // SSM_HALF_CH

    def body(dy_ref, u_ref, xre_ref, xim_ref, bre_ref, bim_ref, cre_ref, cim_ref, d_ref, tab_ref,
             du_ref, dbre_ref, dbim_ref, dcre_ref, dcim_ref, dd_ref, dlre_ref, dlim_ref,
             gre_s, gim_s, cre_s, cim_s):
        c = pl.program_id(1)

        @pl.when(c == 0)
        def _():
            cre_s[...] = jnp.zeros_like(cre_s)
            cim_s[...] = jnp.zeros_like(cim_s)
            dbre_ref[...] = jnp.zeros_like(dbre_ref)
            dbim_ref[...] = jnp.zeros_like(dbim_ref)
            dcre_ref[...] = jnp.zeros_like(dcre_ref)
            dcim_ref[...] = jnp.zeros_like(dcim_ref)
            dd_ref[...] = jnp.zeros_like(dd_ref)
            dlre_ref[...] = jnp.zeros_like(dlre_ref)
            dlim_ref[...] = jnp.zeros_like(dlim_ref)

        dy = dy_ref[...]
        dyb = dy.astype(BF16)
        u = u_ref[...]
        gre_s[...] = _dot(dyb, cre_ref[0], 1, 1)
        gim_s[...] = _dot(dyb, cim_ref[0], 1, 1)
        row = lax.broadcasted_iota(jnp.int32, (SUBLANES, SSM_HALF_ST), 0)
        nslab = Tc // SUBLANES

        def slab(kk, carry):
            car_re, car_im, acc_re, acc_im = carry
            k = nslab - 1 - kk
            rows = pl.ds(pl.multiple_of(k * SUBLANES, SUBLANES), SUBLANES)
            sre = gre_s[rows, :]
            sim = gim_s[rows, :]
            for n, d in enumerate((1, 2, 4)):
                pre, pim = tab_ref[0, 2 * n], tab_ref[0, 2 * n + 1]
                rre = pltpu.roll(sre, SUBLANES - d, 0)
                rim = pltpu.roll(sim, SUBLANES - d, 0)
                sre, sim = sre + (pre * rre + pim * rim), sim + (pre * rim - pim * rre)
            pre, pim = tab_ref[0, 6], tab_ref[0, 7]
            sre, sim = sre + (pre * car_re + pim * car_im), sim + (pre * car_im - pim * car_re)
            gre_s[rows, :] = sre
            gim_s[rows, :] = sim
            nre = jnp.where(row == SUBLANES - 1, car_re, pltpu.roll(sre, SUBLANES - 1, 0))
            nim = jnp.where(row == SUBLANES - 1, car_im, pltpu.roll(sim, SUBLANES - 1, 0))
            xr = xre_ref[rows, :]
            xi = xim_ref[rows, :]
            acc_re = acc_re + (nre * xr + nim * xi)
            acc_im = acc_im + (nim * xr - nre * xi)
            return (jnp.broadcast_to(sre[0:1, :], sre.shape), jnp.broadcast_to(sim[0:1, :], sim.shape), acc_re, acc_im)

        car = lax.fori_loop(0, nslab, slab, (cre_s[...], cim_s[...], dlre_ref[0], dlim_ref[0]))
        cre_s[...] = car[0]
        cim_s[...] = car[1]
        dlre_ref[0] = car[2]
        dlim_ref[0] = car[3]
        gre = gre_s[...].astype(BF16)
        gim = gim_s[...].astype(BF16)
        ub = u.astype(BF16)
        du = _dot(gre, bre_ref[0], 1, 1) + _dot(gim, bim_ref[0], 1, 1) + d_ref[...] * dy
        du_ref[...] = du.astype(BF16)
        dbre_ref[0] += _dot(ub, gre, 0, 0)
        dbim_ref[0] += _dot(ub, gim, 0, 0)
        dcre_ref[0] += _dot(xre_ref[...], dyb, 0, 0)
        dcim_ref[0] += _dot(xim_ref[...], dyb, 0, 0)
        dd_ref[...] += jnp.sum(dy * u, axis=0, keepdims=True)

    rev = lambda c: nc - 1 - c
    return pl.pallas_call(
        body, name=name,
        out_shape=(jax.ShapeDtypeStruct((S, SSM_WIDTH), BF16),
                   jax.ShapeDtypeStruct((SSM_HALVES, SSM_HALF_CH, SSM_HALF_ST), F32),
                   jax.ShapeDtypeStruct((SSM_HALVES, SSM_HALF_CH, SSM_HALF_ST), F32),
                   jax.ShapeDtypeStruct((SSM_HALVES, SSM_HALF_ST, SSM_HALF_CH), F32),
                   jax.ShapeDtypeStruct((SSM_HALVES, SSM_HALF_ST, SSM_HALF_CH), F32),
                   jax.ShapeDtypeStruct((1, SSM_WIDTH), F32),
                   jax.ShapeDtypeStruct((SSM_HALVES, SUBLANES, SSM_HALF_ST), F32),
                   jax.ShapeDtypeStruct((SSM_HALVES, SUBLANES, SSM_HALF_ST), F32)),
        grid=(SSM_HALVES, nc),
        in_specs=[pl.BlockSpec((Tc, SSM_HALF_CH), lambda h, c: (rev(c), h)),
                  pl.BlockSpec((Tc, SSM_HALF_CH), lambda h, c: (rev(c), u_blk0 + h)),
                  pl.BlockSpec((Tc, SSM_HALF_ST), lambda h, c: (rev(c), h)),
                  pl.BlockSpec((Tc, SSM_HALF_ST), lambda h, c: (rev(c), h)),
                  pl.BlockSpec((1, SSM_HALF_CH, SSM_HALF_ST), lambda h, c: (h, 0, 0)),
                  pl.BlockSpec((1, SSM_HALF_CH, SSM_HALF_ST), lambda h, c: (h, 0, 0)),
                  pl.BlockSpec((1, SSM_HALF_ST, SSM_HALF_CH), lambda h, c: (h, 0, 0)),
                  pl.BlockSpec((1, SSM_HALF_ST, SSM_HALF_CH), lambda h, c: (h, 0, 0)),
                  pl.BlockSpec((1, SSM_HALF_CH), lambda h, c: (0, h)),
                  pl.BlockSpec((1, 8, SUBLANES, SSM_HALF_ST), lambda h, c: (h, 0, 0, 0))],
        out_specs=(pl.BlockSpec((Tc, SSM_HALF_CH), lambda h, c: (rev(c), h)),
                   pl.BlockSpec((1, SSM_HALF_CH, SSM_HALF_ST), lambda h, c: (h, 0, 0)),
                   pl.BlockSpec((1, SSM_HALF_CH, SSM_HALF_ST), lambda h, c: (h, 0, 0)),
                   pl.BlockSpec((1, SSM_HALF_ST, SSM_HALF_CH), lambda h, c: (h, 0, 0)),
                   pl.BlockSpec((1, SSM_HALF_ST, SSM_HALF_CH), lambda h, c: (h, 0, 0)),
                   pl.BlockSpec((1, SSM_HALF_CH), lambda h, c: (0, h)),
                   pl.BlockSpec((1, SUBLANES, SSM_HALF_ST), lambda h, c: (h, 0, 0)),
                   pl.BlockSpec((1, SUBLANES, SSM_HALF_ST), lambda h, c: (h, 0, 0))),
        scratch_shapes=[pltpu.VMEM((Tc, SSM_HALF_ST), F32), pltpu.VMEM((Tc, SSM_HALF_ST), F32),
                        pltpu.VMEM((SUBLANES, SSM_HALF_ST), F32), pltpu.VMEM((SUBLANES, SSM_HALF_ST), F32)],
        compiler_params=_cparams("parallel", "arbitrary"),
    )(dy, proj, x_re, x_im, bd_re, bd_im, cd_re, cd_imneg, d_skip, tab)


def _ssm_prepare(a_re, a_im, log_dt, b_re, b_im):
    dt = jnp.exp(log_dt)[:, None]
    mag = jnp.exp(a_re * dt)
    lre = mag * jnp.cos(a_im * dt)
    lim = mag * jnp.sin(a_im * dt)
    den = a_re * a_re + a_im * a_im
    fre = ((lre - 1.0) * a_re + lim * a_im) / den
    fim = (lim * a_re - (lre - 1.0) * a_im) / den
    bbre = fre[:, :, None] * b_re - fim[:, :, None] * b_im
    bbim = fre[:, :, None] * b_im + fim[:, :, None] * b_re
    return lre, lim, bbre, bbim


def _group_eye():
    return jnp.eye(SSM_GROUPS // SSM_HALVES, dtype=F32)


def _bd_from_bbar(bbar):
    gh = SSM_GROUPS // SSM_HALVES
    b = bbar.reshape(SSM_HALVES, gh, SSM_STATE, SSM_GROUP).transpose(0, 1, 3, 2)
    out = b[:, :, :, None, :] * _group_eye()[None, :, None, :, None]
    return out.reshape(SSM_HALVES, SSM_HALF_CH, SSM_HALF_ST)


def _bbar_from_bd(dbd):
    gh = SSM_GROUPS // SSM_HALVES
    d = dbd.reshape(SSM_HALVES, gh, SSM_GROUP, gh, SSM_STATE)
    d = jnp.sum(d * _group_eye()[None, :, None, :, None], axis=3)
    return d.transpose(0, 1, 3, 2).reshape(SSM_GROUPS, SSM_STATE, SSM_GROUP)


def _cd_from_c(cmat):
    gh = SSM_GROUPS // SSM_HALVES
    c = cmat.reshape(SSM_HALVES, gh, SSM_GROUP, SSM_STATE).transpose(0, 1, 3, 2)
    out = c[:, :, :, None, :] * _group_eye()[None, :, None, :, None]
    return out.reshape(SSM_HALVES, SSM_HALF_ST, SSM_HALF_CH)


def _c_from_cd(dcd):
    gh = SSM_GROUPS // SSM_HALVES
    d = dcd.reshape(SSM_HALVES, gh, SSM_STATE, gh, SSM_GROUP)
    d = jnp.sum(d * _group_eye()[None, :, None, :, None], axis=3)
    return d.transpose(0, 1, 3, 2).reshape(SSM_GROUPS, SSM_GROUP, SSM_STATE)


def _glu_fwd(y_pre, w_glu, b_glu, *, name):
    S, W = y_pre.shape
    tr = _row_tile(S)

    def body(y_ref, w_ref, b_ref, o_ref):
        yg = _gelu(y_ref[...])
        gl = _dot(yg, w_ref[...], 1, 0) + b_ref[...]
        o_ref[...] = (yg * _sigmoid(gl)).astype(BF16)

    row = pl.BlockSpec((tr, W), lambda i: (i, 0))
    return pl.pallas_call(
        body, name=name, out_shape=jax.ShapeDtypeStruct((S, W), BF16), grid=(S // tr,),
        in_specs=[row, pl.BlockSpec((W, W), lambda i: (0, 0)), pl.BlockSpec((1, W), lambda i: (0, 0))],
        out_specs=row, compiler_params=_cparams("parallel"),
    )(y_pre, w_glu, b_glu)


def _glu_bwd(y_pre, do, w_glu, b_glu, *, name):
    S, W = y_pre.shape
    tr = _row_tile(S)

    def body(y_ref, do_ref, w_ref, b_ref, dy_ref, dw_ref, db_ref):
        i = pl.program_id(0)
        yg, dyg_dy = _gelu_and_grad(y_ref[...])
        ygb = yg.astype(BF16)
        sg = _sigmoid(_dot(ygb, w_ref[...], 1, 0) + b_ref[...])
        do = do_ref[...]
        dgl = do * yg * sg * (1.0 - sg)
        dglb = dgl.astype(BF16)
        dyg = do * sg + _dot(dglb, w_ref[...], 1, 1)
        dy_ref[...] = dyg * dyg_dy
        dw = _dot(ygb, dglb, 0, 0)
        db = jnp.sum(dgl, axis=0, keepdims=True)

        @pl.when(i == 0)
        def _():
            dw_ref[...] = dw
            db_ref[...] = db

        @pl.when(i > 0)
        def _():
            dw_ref[...] += dw
            db_ref[...] += db

    row = pl.BlockSpec((tr, W), lambda i: (i, 0))
    full = pl.BlockSpec((W, W), lambda i: (0, 0))
    vec = pl.BlockSpec((1, W), lambda i: (0, 0))
    return pl.pallas_call(
        body, name=name,
        out_shape=(jax.ShapeDtypeStruct((S, W), F32), jax.ShapeDtypeStruct((W, W), F32), jax.ShapeDtypeStruct((1, W), F32)),
        grid=(S // tr,), in_specs=[row, row, full, vec], out_specs=(row, full, vec),
        compiler_params=_cparams("arbitrary"),
    )(y_pre, do, w_glu, b_glu)


GATE_COL0 = 3 * SB_WIDTH + SSM_WIDTH


def _merge_fwd(proj, o_attn, o_ssm, w_ba, w_bs, b_gate, *, name):
    S = proj.shape[0]
    D = D_MODEL
    tr = _pick(S, (256, 128, 64, 32, 16, 8))
    gb = GATE_COL0 // D

    def body(ga_ref, gs_ref, oa_ref, os_ref, wa_ref, ws_ref, ba_ref, bs_ref, m_ref):
        pa = _dot(oa_ref[...], wa_ref[...], 1, 0)
        ps = _dot(os_ref[...], ws_ref[...], 1, 0)
        sa = _sigmoid(ga_ref[...] + ba_ref[...])
        ss = _sigmoid(gs_ref[...] + bs_ref[...])
        m_ref[...] = (sa * pa + ss * ps).astype(BF16)

    return pl.pallas_call(
        body, name=name, out_shape=jax.ShapeDtypeStruct((S, D), BF16), grid=(S // tr,),
        in_specs=[pl.BlockSpec((tr, D), lambda i: (i, gb)), pl.BlockSpec((tr, D), lambda i: (i, gb + 1)),
                  pl.BlockSpec((tr, SB_WIDTH), lambda i: (i, 0)), pl.BlockSpec((tr, SSM_WIDTH), lambda i: (i, 0)),
                  pl.BlockSpec((SB_WIDTH, D), lambda i: (0, 0)), pl.BlockSpec((SSM_WIDTH, D), lambda i: (0, 0)),
                  pl.BlockSpec((1, D), lambda i: (0, 0)), pl.BlockSpec((1, D), lambda i: (0, 1))],
        out_specs=pl.BlockSpec((tr, D), lambda i: (i, 0)),
        compiler_params=_cparams("parallel"),
    )(proj, proj, o_attn, o_ssm, w_ba, w_bs, b_gate, b_gate)


def _merge_bwd(dmerged, proj, o_attn, o_ssm, w_ba, w_bs, b_gate, *, name):
    S = proj.shape[0]
    D = D_MODEL
    tr = _pick(S, (256, 128, 64, 32, 16, 8))
    gb = GATE_COL0 // D

    def body(dm_ref, ga_ref, gs_ref, oa_ref, os_ref, wa_ref, ws_ref, ba_ref, bs_ref,
             doa_ref, dos_ref, dg_ref, db_ref, dwa_ref, dws_ref):
        i = pl.program_id(0)
        dm = dm_ref[...]
        oa = oa_ref[...]
        osm = os_ref[...]
        pa = _dot(oa, wa_ref[...], 1, 0)
        ps = _dot(osm, ws_ref[...], 1, 0)
        sa = _sigmoid(ga_ref[...] + ba_ref[...])
        ss = _sigmoid(gs_ref[...] + bs_ref[...])
        dpa = (dm * sa).astype(BF16)
        dps = (dm * ss).astype(BF16)
        dga = dm * pa * sa * (1.0 - sa)
        dgs = dm * ps * ss * (1.0 - ss)
        dg_ref[:, :D] = dga.astype(BF16)
        dg_ref[:, D:] = dgs.astype(BF16)
        doa_ref[...] = _dot(dpa, wa_ref[...], 1, 1).astype(BF16)
        dos_ref[...] = _dot(dps, ws_ref[...], 1, 1)
        dwa = _dot(oa, dpa, 0, 0)
        dws = _dot(osm, dps, 0, 0)
        dba = jnp.sum(dga, axis=0, keepdims=True)
        dbs = jnp.sum(dgs, axis=0, keepdims=True)

        @pl.when(i == 0)
        def _():
            dwa_ref[...] = dwa
            dws_ref[...] = dws
            db_ref[:, :D] = dba
            db_ref[:, D:] = dbs

        @pl.when(i > 0)
        def _():
            dwa_ref[...] += dwa
            dws_ref[...] += dws
            db_ref[:, :D] += dba
            db_ref[:, D:] += dbs

    rowD = pl.BlockSpec((tr, D), lambda i: (i, 0))
    wspec = pl.BlockSpec((SB_WIDTH, D), lambda i: (0, 0))
    return pl.pallas_call(
        body, name=name,
        out_shape=(jax.ShapeDtypeStruct((S, SB_WIDTH), BF16), jax.ShapeDtypeStruct((S, SSM_WIDTH), F32),
                   jax.ShapeDtypeStruct((S, 2 * D), BF16), jax.ShapeDtypeStruct((1, 2 * D), F32),
                   jax.ShapeDtypeStruct((SB_WIDTH, D), F32), jax.ShapeDtypeStruct((SSM_WIDTH, D), F32)),
        grid=(S // tr,),
        in_specs=[rowD, pl.BlockSpec((tr, D), lambda i: (i, gb)), pl.BlockSpec((tr, D), lambda i: (i, gb + 1)),
                  pl.BlockSpec((tr, SB_WIDTH), lambda i: (i, 0)), pl.BlockSpec((tr, SSM_WIDTH), lambda i: (i, 0)),
                  wspec, wspec, pl.BlockSpec((1, D), lambda i: (0, 0)), pl.BlockSpec((1, D), lambda i: (0, 1))],
        out_specs=(pl.BlockSpec((tr, SB_WIDTH), lambda i: (i, 0)), pl.BlockSpec((tr, SSM_WIDTH), lambda i: (i, 0)),
                   pl.BlockSpec((tr, 2 * D), lambda i: (i, 0)), pl.BlockSpec((1, 2 * D), lambda i: (0, 0)),
                   wspec, wspec),
        compiler_params=_cparams("arbitrary"),
    )(dmerged, proj, proj, o_attn, o_ssm, w_ba, w_bs, b_gate, b_gate)


def _xattn_probs(q, k, h):
    cols = slice(h * XA_HEAD_DIM, (h + 1) * XA_HEAD_DIM)
    s = _dot(q[:, cols], k[:, cols], 1, 1) * (XA_HEAD_DIM ** -0.5)
    s = s - jnp.max(s, axis=-1, keepdims=True)
    e = jnp.exp(s)
    return e / jnp.sum(e, axis=-1, keepdims=True), cols


def _xattn_fwd(q2, k2, v2, *, name):
    S, D = q2.shape
    M = k2.shape[0]
    tr = _row_tile(S)

    def body(q_ref, k_ref, v_ref, o_ref):
        q = q_ref[...]
        k = k_ref[...]
        v = v_ref[...]
        for h in range(XA_HEADS):
            p, cols = _xattn_probs(q, k, h)
            o_ref[:, cols] = _dot(p, v[:, cols], 1, 0).astype(BF16)

    row = pl.BlockSpec((tr, D), lambda i: (i, 0))
    memb = pl.BlockSpec((M, D), lambda i: (0, 0))
    return pl.pallas_call(
        body, name=name, out_shape=jax.ShapeDtypeStruct((S, D), BF16), grid=(S // tr,),
        in_specs=[row, memb, memb], out_specs=row, compiler_params=_cparams("parallel"),
    )(q2, k2, v2)


def _xattn_bwd(q2, k2, v2, do2, *, name):
    S, D = q2.shape
    M = k2.shape[0]
    tr = _row_tile(S)
    scale = XA_HEAD_DIM ** -0.5

    def body(q_ref, k_ref, v_ref, do_ref, dq_ref, dk_ref, dv_ref):
        i = pl.program_id(0)

        @pl.when(i == 0)
        def _():
            dk_ref[...] = jnp.zeros_like(dk_ref)
            dv_ref[...] = jnp.zeros_like(dv_ref)

        q = q_ref[...]
        k = k_ref[...]
        v = v_ref[...]
        do = do_ref[...]
        for h in range(XA_HEADS):
            p, cols = _xattn_probs(q, k, h)
            dp = _dot(do[:, cols], v[:, cols], 1, 1)
            ds = (p * (dp - jnp.sum(dp * p, axis=-1, keepdims=True)) * scale).astype(BF16)
            dq_ref[:, cols] = _dot(ds, k[:, cols], 1, 0).astype(BF16)
            dk_ref[:, cols] += _dot(ds, q[:, cols], 0, 0)
            dv_ref[:, cols] += _dot(p, do[:, cols], 0, 0)

    row = pl.BlockSpec((tr, D), lambda i: (i, 0))
    memb = pl.BlockSpec((M, D), lambda i: (0, 0))
    return pl.pallas_call(
        body, name=name,
        out_shape=(jax.ShapeDtypeStruct((S, D), BF16), jax.ShapeDtypeStruct((M, D), F32), jax.ShapeDtypeStruct((M, D), F32)),
        grid=(S // tr,), in_specs=[row, memb, memb, row], out_specs=(row, memb, memb),
        compiler_params=_cparams("arbitrary"),
    )(q2, k2, v2, do2)


CONV_ROWS = 512


def _shift_down(ref, t0, rows, d):
    cur = ref[pl.ds(t0, rows), :]
    out = pltpu.roll(cur, d, 0)
    r = lax.broadcasted_iota(jnp.int32, cur.shape, 0)
    for e in range(d):
        src = t0 - d + e
        prev = ref[pl.ds(src, 1), :] if src >= 0 else jnp.zeros((1, cur.shape[1]), cur.dtype)
        out = jnp.where(r == e, prev, out)
    return out


def _shift_up(ref, t0, rows, d, total):
    cur = ref[pl.ds(t0, rows), :]
    out = pltpu.roll(cur, rows - d, 0)
    r = lax.broadcasted_iota(jnp.int32, cur.shape, 0)
    for e in range(d):
        src = t0 + rows + e
        nxt = ref[pl.ds(src, 1), :] if src < total else jnp.zeros((1, cur.shape[1]), cur.dtype)
        out = jnp.where(r == rows - d + e, nxt, out)
    return out


def _conv3(ref, w_ref, b_ref, t0, rows):
    return (w_ref[2:3, :] * ref[pl.ds(t0, rows), :] + w_ref[1:2, :] * _shift_down(ref, t0, rows, 1)
            + w_ref[0:1, :] * _shift_down(ref, t0, rows, 2) + b_ref[...])


def _convgate_fwd(up_g, up_v, conv_w, conv_b, *, name):
    S, H = up_g.shape
    nb = H // LANES
    R = min(CONV_ROWS, S)

    def body(g_ref, v_ref, wg_ref, wv_ref, bg_ref, bv_ref, a_ref):
        for t0 in range(0, S, R):
            cg = _conv3(g_ref, wg_ref, bg_ref, t0, R)
            cv = _conv3(v_ref, wv_ref, bv_ref, t0, R)
            a_ref[pl.ds(t0, R), :] = (_gelu(cg) * cv).astype(BF16)

    col = lambda off: pl.BlockSpec((S, LANES), lambda j: (0, off + j))
    wcol = lambda off: pl.BlockSpec((3, LANES), lambda j: (0, off + j))
    bcol = lambda off: pl.BlockSpec((1, LANES), lambda j: (0, off + j))
    return pl.pallas_call(
        body, name=name, out_shape=jax.ShapeDtypeStruct((S, H), BF16), grid=(nb,),
        in_specs=[col(0), col(0), wcol(0), wcol(nb), bcol(0), bcol(nb)],
        out_specs=col(0), compiler_params=_cparams("parallel"),
    )(up_g, up_v, conv_w, conv_w, conv_b, conv_b)


def _convgate_bwd(up_g, up_v, da, conv_w, conv_b, *, name):
    S, H = up_g.shape
    nb = H // LANES
    R = min(CONV_ROWS, S)

    def body(g_ref, v_ref, da_ref, wg_ref, wv_ref, bg_ref, bv_ref,
             dug_ref, duv_ref, dwg_ref, dwv_ref, dbg_ref, dbv_ref, dcg_s, dcv_s):
        zero3 = jnp.zeros((1, LANES), F32)
        acc = {"g": [zero3, zero3, zero3, zero3], "v": [zero3, zero3, zero3, zero3]}
        for t0 in range(0, S, R):
            cg = _conv3(g_ref, wg_ref, bg_ref, t0, R)
            cv = _conv3(v_ref, wv_ref, bv_ref, t0, R)
            da = da_ref[pl.ds(t0, R), :]
            gl, dgl = _gelu_and_grad(cg)
            dcg = da * cv * dgl
            dcv = da * gl
            dcg_s[pl.ds(t0, R), :] = dcg
            dcv_s[pl.ds(t0, R), :] = dcv
            for key, ref, dc in (("g", g_ref, dcg), ("v", v_ref, dcv)):
                a = acc[key]
                a[2] = a[2] + jnp.sum(dc * ref[pl.ds(t0, R), :], axis=0, keepdims=True)
                a[1] = a[1] + jnp.sum(dc * _shift_down(ref, t0, R, 1), axis=0, keepdims=True)
                a[0] = a[0] + jnp.sum(dc * _shift_down(ref, t0, R, 2), axis=0, keepdims=True)
                a[3] = a[3] + jnp.sum(dc, axis=0, keepdims=True)
        for key, dw_ref, db_ref in (("g", dwg_ref, dbg_ref), ("v", dwv_ref, dbv_ref)):
            a = acc[key]
            dw_ref[0:1, :] = a[0]
            dw_ref[1:2, :] = a[1]
            dw_ref[2:3, :] = a[2]
            db_ref[...] = a[3]
        for t0 in range(0, S, R):
            for dc_s, w_ref, du_ref in ((dcg_s, wg_ref, dug_ref), (dcv_s, wv_ref, duv_ref)):
                du = (w_ref[2:3, :] * dc_s[pl.ds(t0, R), :] + w_ref[1:2, :] * _shift_up(dc_s, t0, R, 1, S)
                      + w_ref[0:1, :] * _shift_up(dc_s, t0, R, 2, S))
                du_ref[pl.ds(t0, R), :] = du.astype(BF16)

    col = lambda off: pl.BlockSpec((S, LANES), lambda j: (0, off + j))
    wcol = lambda off: pl.BlockSpec((3, LANES), lambda j: (0, off + j))
    bcol = lambda off: pl.BlockSpec((1, LANES), lambda j: (0, off + j))
    return pl.pallas_call(
        body, name=name,
        out_shape=(jax.ShapeDtypeStruct((S, H), BF16), jax.ShapeDtypeStruct((S, H), BF16),
                   jax.ShapeDtypeStruct((3, H), F32), jax.ShapeDtypeStruct((3, H), F32),
                   jax.ShapeDtypeStruct((1, H), F32), jax.ShapeDtypeStruct((1, H), F32)),
        grid=(nb,),
        in_specs=[col(0), col(0), col(0), wcol(0), wcol(nb), bcol(0), bcol(nb)],
        out_specs=(col(0), col(0), wcol(0), wcol(0), bcol(0), bcol(0)),
        scratch_shapes=[pltpu.VMEM((S, LANES), F32), pltpu.VMEM((S, LANES), F32)],
        compiler_params=_cparams("parallel"),
    )(up_g, up_v, da, conv_w, conv_w, conv_b, conv_b)


def _local_step(x, mem, target, W, P):
    mm = _matmul
    h1 = _rms_fwd(x, P["norm_mix_pre"], name="rms_mix_pre")
    proj = mm(h1, W["w_in"], name="mm_in")
    o_attn, sb_tot = _sb_fwd(proj, name="sb_fwd")

    ssm_prep = lambda *a: _ssm_prepare(*a)
    (lam_re, lam_im, bb_re, bb_im), prep_vjp = jax.vjp(
        ssm_prep, P["ssm_a_re"], P["ssm_a_im"], P["ssm_log_dt"], P["ssm_b_re"], P["ssm_b_im"])
    tab_f, tab_b = _ssm_tables(lam_re, lam_im)
    bd_re = _bd_from_bbar(bb_re).astype(BF16)
    bd_im = _bd_from_bbar(bb_im).astype(BF16)
    cd_re = _cd_from_c(P["ssm_c_re"]).astype(BF16)
    cd_imneg = _cd_from_c(-P["ssm_c_im"]).astype(BF16)
    y_pre, x_re, x_im = _ssm_fwd(proj, bd_re, bd_im, cd_re, cd_imneg, P["ssm_d"], tab_f, name="ssm_fwd")
    o_ssm = _glu_fwd(y_pre, W["ssm_w_glu"], P["ssm_b_glu"], name="glu_fwd")

    merged = _merge_fwd(proj, o_attn, o_ssm, W["w_branch_attn"], W["w_branch_ssm"], P["b_gate"], name="merge_fwd")
    mo = mm(merged, W["w_out"], name="mm_out")
    x1, h2 = _resnorm_norm(x, mo, P["norm_mix_post"], P["norm_xa_pre"], name="resnorm_1")

    mem_n = _rms_fwd(mem, P["norm_mem"], name="rms_mem")
    q2 = mm(h2, W["xa_wq"], out_dtype=BF16, name="mm_xq")
    k2 = mm(mem_n, W["xa_wk"], out_dtype=BF16, name="mm_xk")
    v2 = mm(mem_n, W["xa_wv"], out_dtype=BF16, name="mm_xv")
    o2 = _xattn_fwd(q2, k2, v2, name="xattn_fwd")
    xa = mm(o2, W["xa_wo"], name="mm_xo")
    x2, h3 = _resnorm_norm(x1, xa, P["norm_xa_post"], P["norm_ffn_pre"], name="resnorm_2")

    half = N_DEV // 2
    up_g = mm(h3, W["ffn_w_up"], n_blocks=half, name="mm_up_g")
    up_v = mm(h3, W["ffn_w_up"], b_block0=half, name="mm_up_v")
    act = _convgate_fwd(up_g, up_v, W["ffn_conv_w"], P["ffn_conv_b"], name="convgate_fwd")
    f = mm(act, W["ffn_w_down"], name="mm_down")
    loss, dy, df, dg_ffn_post = _final_loss(x2, f, P["norm_ffn_post"], target, name="final_loss")

    G = {"norm_ffn_post": dg_ffn_post}
    dact = mm(df, W["ffn_w_down"], tb=True, name="mm_down_dx")
    G["ffn_w_down"] = mm(act, df, ta=True, name="mm_down_dw")
    dug, duv, dwg, dwv, dbg, dbv = _convgate_bwd(up_g, up_v, dact, W["ffn_conv_w"], P["ffn_conv_b"], name="convgate_bwd")
    G["ffn_conv_w"] = jnp.concatenate([dwg, dwv], axis=1)
    G["ffn_conv_b"] = jnp.concatenate([dbg, dbv], axis=1)
    dh3 = mm(dug, W["ffn_w_up"], tb=True, n_blocks=half, name="mm_up_g_dx")
    dh3 = mm(duv, W["ffn_w_up"], tb=True, b_block0=half, acc_in=dh3, name="mm_up_v_dx")
    dw_up = mm(h3, dug, ta=True, out_into=lax.empty(W["ffn_w_up"].shape, F32), name="mm_up_g_dw")
    G["ffn_w_up"] = mm(h3, duv, ta=True, out_into=dw_up, out_block0=half, name="mm_up_v_dw")
    dx2, dxa, G["norm_ffn_pre"], G["norm_xa_post"] = _norm_bwd_pair(
        dy, dh3, x2, P["norm_ffn_pre"], xa, P["norm_xa_post"], name="norm_bwd_3")

    G["xa_wo"] = mm(o2, dxa, ta=True, name="mm_xo_dw")
    do2 = mm(dxa, W["xa_wo"], tb=True, out_dtype=BF16, name="mm_xo_dx")
    dq2, dk2, dv2 = _xattn_bwd(q2, k2, v2, do2, name="xattn_bwd")
    G["xa_wq"] = mm(h2, dq2, ta=True, name="mm_xq_dw")
    dh2 = mm(dq2, W["xa_wq"], tb=True, name="mm_xq_dx")
    G["xa_wk"] = mm(mem_n, dk2, ta=True, name="mm_xk_dw")
    G["xa_wv"] = mm(mem_n, dv2, ta=True, name="mm_xv_dw")
    dmem_n = jnp.concatenate([dk2, dv2], axis=1)
    wkv = jnp.concatenate([W["xa_wk"], W["xa_wv"]], axis=1)
    dmem = mm(dmem_n, wkv, tb=True, name="mm_xkv_dx")
    _, G["norm_mem"] = _norm_bwd_single(None, dmem, mem, P["norm_mem"], name="norm_bwd_mem")
    dx1, dmo, G["norm_xa_pre"], G["norm_mix_post"] = _norm_bwd_pair(
        dx2, dh2, x1, P["norm_xa_pre"], mo, P["norm_mix_post"], name="norm_bwd_2")

    G["w_out"] = mm(merged, dmo, ta=True, name="mm_out_dw")
    dmerged = mm(dmo, W["w_out"], tb=True, name="mm_out_dx")
    do_attn, do_ssm, dgate, G["b_gate"], G["w_branch_attn"], G["w_branch_ssm"] = _merge_bwd(
        dmerged, proj, o_attn, o_ssm, W["w_branch_attn"], W["w_branch_ssm"], P["b_gate"], name="merge_bwd")
    dy_pre, G["ssm_w_glu"], G["ssm_b_glu"] = _glu_bwd(y_pre, do_ssm, W["ssm_w_glu"], P["ssm_b_glu"], name="glu_bwd")
    du, dbd_re, dbd_im, dcd_re, dcd_imneg, G["ssm_d"], dl_re, dl_im = _ssm_bwd(
        dy_pre, proj, x_re, x_im, bd_re, bd_im, cd_re, cd_imneg, P["ssm_d"], tab_b, name="ssm_bwd")
    G["ssm_c_re"] = _c_from_cd(dcd_re)
    G["ssm_c_im"] = -_c_from_cd(dcd_imneg)
    dlam_re = jnp.sum(dl_re, axis=1).reshape(SSM_GROUPS, SSM_STATE)
    dlam_im = jnp.sum(dl_im, axis=1).reshape(SSM_GROUPS, SSM_STATE)
    (G["ssm_a_re"], G["ssm_a_im"], G["ssm_log_dt"], G["ssm_b_re"], G["ssm_b_im"]) = prep_vjp(
        (dlam_re, dlam_im, _bbar_from_bd(dbd_re), _bbar_from_bd(dbd_im)))
    dq, dk, dv = _sb_bwd(proj, sb_tot, do_attn, name="sb_bwd")
    dproj = jnp.concatenate([dq, dk, dv, du, dgate], axis=1)
    G["w_in"] = mm(h1, dproj, ta=True, out_cb=W["w_in"].shape[2], name="mm_in_dw")
    dh1 = mm(dproj, W["w_in"], tb=True, name="mm_in_dx")
    grad_x, G["norm_mix_pre"] = _norm_bwd_single(dx1, dh1, x, P["norm_mix_pre"], name="norm_bwd_1")
    return loss, grad_x, G


MESH = pl.DeviceIdType.MESH
_HBM = pl.BlockSpec(memory_space=pl.ANY)
N_XY = 4
N_XY_PEERS = 3


def _xy_peers(x, y):
    return [(1 - x, y), (x, 1 - y), (1 - x, 1 - y)]


def _exchange(arrays, out_shapes, plan, n_copies, *, alias, name):
    n = len(arrays)

    def body(*refs):
        ins, outs = refs[:n], refs[n:2 * n]
        send_sems, recv_sems = refs[2 * n], refs[2 * n + 1]
        x, y, c = lax.axis_index("x"), lax.axis_index("y"), lax.axis_index("c")
        sends, lands = [], []
        for k in range(n):
            for j, (src, dst, dev, land) in enumerate(plan(k, ins[k], outs[k], x, y, c)):
                sends.append(pltpu.make_async_remote_copy(
                    src_ref=src, dst_ref=dst, send_sem=send_sems.at[k, j], recv_sem=recv_sems.at[k, j],
                    device_id=dev, device_id_type=MESH))
                lands.append(pltpu.make_async_remote_copy(
                    src_ref=src, dst_ref=land, send_sem=send_sems.at[k, j], recv_sem=recv_sems.at[k, j],
                    device_id=dev, device_id_type=MESH))
        for cp in sends:
            cp.start()
        for cp in lands:
            cp.wait_recv()
        for cp in sends:
            cp.wait_send()

    return pl.pallas_call(
        body, name=name, out_shape=tuple(out_shapes),
        in_specs=[_HBM] * n, out_specs=tuple([_HBM] * n),
        input_output_aliases={k: k for k in range(n)} if alias else {},
        scratch_shapes=[pltpu.SemaphoreType.DMA((n, n_copies)), pltpu.SemaphoreType.DMA((n, n_copies))],
    )(*arrays)


def _same(arrays):
    return [jax.ShapeDtypeStruct(a.shape, a.dtype) for a in arrays]


def _fill_xy(bufs, *, name):
    def plan(k, src, dst, x, y, c):
        mine = 2 * x + y
        return [(src.at[mine, c], dst.at[mine, c], (px, py, c), dst.at[2 * px + py, c]) for px, py in _xy_peers(x, y)]

    return _exchange(bufs, _same(bufs), plan, N_XY_PEERS, alias=True, name=name)


def _fill_c(bufs, *, name):
    def plan(k, src, dst, x, y, c):
        return [(src.at[:, c], dst.at[:, c], (x, y, 1 - c), dst.at[:, 1 - c])]

    return _exchange(bufs, _same(bufs), plan, 1, alias=True, name=name)


def _send_c(srcs, *, name):
    def plan(k, src, dst, x, y, c):
        return [(src.at[:, 1 - c], dst, (x, y, 1 - c), dst)]

    outs = [jax.ShapeDtypeStruct(a.shape[:1] + a.shape[2:], a.dtype) for a in srcs]
    return _exchange(srcs, outs, plan, 1, alias=False, name=name)


def _scatter_xy(srcs, *, name):
    def plan(k, src, dst, x, y, c):
        return [(src.at[2 * px + py], dst.at[j], (px, py, c), dst.at[j]) for j, (px, py) in enumerate(_xy_peers(x, y))]

    outs = [jax.ShapeDtypeStruct((N_XY_PEERS,) + a.shape[1:], a.dtype) for a in srcs]
    return _exchange(srcs, outs, plan, N_XY_PEERS, alias=False, name=name)


PACK_COLS = 1024
WIRE_DTYPE = BF16


def _pair_sum(g8, recv, core, *, name):
    n, _, R, C = g8.shape
    tr = _pick(R, (128, 64, 32, 16, 8))

    def body(core_ref, a_ref, b_ref, o_ref):
        o_ref[...] = (a_ref[0] + b_ref[...]).astype(WIRE_DTYPE)

    return pl.pallas_call(
        body, name=name, out_shape=jax.ShapeDtypeStruct((n, R, C), WIRE_DTYPE),
        grid_spec=pltpu.PrefetchScalarGridSpec(
            num_scalar_prefetch=1, grid=(n, R // tr),
            in_specs=[pl.BlockSpec((1, 1, tr, C), lambda s, i, core_ref: (s, core_ref[0], i, 0)),
                      pl.BlockSpec((1, tr, C), lambda s, i, core_ref: (s, i, 0))],
            out_specs=pl.BlockSpec((1, tr, C), lambda s, i, core_ref: (s, i, 0))),
        compiler_params=_cparams("parallel", "parallel"),
    )(core, g8, recv)


def _adamw_math(w, g, m, v):
    m = ADAM_B1 * m + (1.0 - ADAM_B1) * g
    v = ADAM_B2 * v + (1.0 - ADAM_B2) * (g * g)
    m_hat = m / (1.0 - ADAM_B1 ** ADAM_STEP)
    v_hat = v / (1.0 - ADAM_B2 ** ADAM_STEP)
    delta = -ADAM_LR * (m_hat / (jnp.sqrt(v_hat) + ADAM_EPS) + ADAM_WD * w)
    return delta, m, v


def _reduce_adamw(parts, w, m, v, *, own=None, own_slot=None, name):
    n, R, C = parts.shape
    tr = _pick(R, (128, 64, 32, 16, 8))
    has_own = own is not None

    def body(*refs):
        if has_own:
            _, own_ref, parts_ref, w_ref, m_ref, v_ref, g_ref, d_ref, nm_ref, nv_ref = refs
            g = own_ref[0].astype(F32)
            first = 0
        else:
            parts_ref, w_ref, m_ref, v_ref, g_ref, d_ref, nm_ref, nv_ref = refs
            g = parts_ref[0]
            first = 1
        for k in range(first, n):
            g = g + parts_ref[k].astype(F32)
        g_ref[...] = g
        d_ref[...], nm_ref[...], nv_ref[...] = _adamw_math(w_ref[...], g, m_ref[...], v_ref[...])

    out = jax.ShapeDtypeStruct((R, C), F32)
    if has_own:
        row = pl.BlockSpec((tr, C), lambda i, s: (i, 0))
        return pl.pallas_call(
            body, name=name, out_shape=(out, out, out, out),
            grid_spec=pltpu.PrefetchScalarGridSpec(
                num_scalar_prefetch=1, grid=(R // tr,),
                in_specs=[pl.BlockSpec((1, tr, C), lambda i, s: (s[0], i, 0)),
                          pl.BlockSpec((n, tr, C), lambda i, s: (0, i, 0)), row, row, row],
                out_specs=(row, row, row, row)),
            compiler_params=_cparams("parallel"),
        )(own_slot, own, parts, w, m, v)
    row = pl.BlockSpec((tr, C), lambda i: (i, 0))
    return pl.pallas_call(
        body, name=name, out_shape=(out, out, out, out), grid=(R // tr,),
        in_specs=[pl.BlockSpec((n, tr, C), lambda i: (0, i, 0)), row, row, row],
        out_specs=(row, row, row, row), compiler_params=_cparams("parallel"),
    )(parts, w, m, v)


SHARDED = (("w_in", (1024, 4096), 1), ("ssm_w_glu", (512, 512), 0), ("w_branch_attn", (512, 1024), 1),
           ("w_branch_ssm", (512, 1024), 1), ("w_out", (1024, 1024), 0), ("xa_wq", (1024, 1024), 0),
           ("xa_wk", (1024, 1024), 0), ("xa_wv", (1024, 1024), 0), ("xa_wo", (1024, 1024), 0),
           ("ffn_w_up", (1024, 5632), 1), ("ffn_conv_w", (3, 5632), 1), ("ffn_w_down", (2816, 1024), 0))
REPLICATED = (("norm_mix_pre", (1024,)), ("norm_mix_post", (1024,)), ("b_gate", (2048,)), ("ssm_a_re", (32, 64)),
              ("ssm_a_im", (32, 64)), ("ssm_log_dt", (32,)), ("ssm_b_re", (32, 64, 16)), ("ssm_b_im", (32, 64, 16)),
              ("ssm_c_re", (32, 16, 64)), ("ssm_c_im", (32, 16, 64)), ("ssm_d", (512,)), ("ssm_b_glu", (512,)),
              ("norm_xa_pre", (1024,)), ("norm_xa_post", (1024,)), ("norm_mem", (1024,)), ("norm_ffn_pre", (1024,)),
              ("norm_ffn_post", (1024,)), ("ffn_conv_b", (5632,)))
PARAM_ORDER = ("norm_mix_pre", "norm_mix_post", "w_in", "b_gate", "ssm_a_re", "ssm_a_im", "ssm_log_dt", "ssm_b_re",
               "ssm_b_im", "ssm_c_re", "ssm_c_im", "ssm_d", "ssm_w_glu", "ssm_b_glu", "w_branch_attn", "w_branch_ssm",
               "w_out", "norm_xa_pre", "norm_xa_post", "norm_mem", "xa_wq", "xa_wk", "xa_wv", "xa_wo", "norm_ffn_pre",
               "norm_ffn_post", "ffn_w_up", "ffn_conv_w", "ffn_conv_b", "ffn_w_down")
SMALL_ROWS = 160
FF_LOCAL = 2 * D_FF // N_DEV
FF_LOCAL_PAD = 768
FF_PAD = (N_DEV // 2) * FF_LOCAL_PAD


def _local_shape(shape, axis):
    return tuple(s // N_DEV if a == axis else s for a, s in enumerate(shape))


def _pad_cols(a, width):
    return jnp.pad(a, [(0, 0)] * (a.ndim - 1) + [(0, width - a.shape[-1])])


def _blocks_to_cols(a8):
    return a8.transpose(1, 0, 2).reshape(a8.shape[1], N_DEV * a8.shape[2])


def _cols_to_blocks(a, cb):
    return a.reshape(a.shape[0], N_DEV, cb).transpose(1, 0, 2)


def _pack_small(d):
    flat = jnp.concatenate([d[n].reshape(-1) for n, _ in REPLICATED])
    return _pad_cols(flat, SMALL_ROWS * PACK_COLS).reshape(SMALL_ROWS, PACK_COLS)


def _unpack_small(buf):
    flat = buf.reshape(-1)
    out, off = {}, 0
    for n, shape in REPLICATED:
        size = math.prod(shape)
        out[n] = flat[off:off + size]
        off += size
    return out


def kernel(x, mem, norm_mix_pre, norm_mix_post, w_in, b_gate, ssm_a_re, ssm_a_im, ssm_log_dt, ssm_b_re, ssm_b_im, ssm_c_re, ssm_c_im, ssm_d, ssm_w_glu, ssm_b_glu, w_branch_attn, w_branch_ssm, w_out, norm_xa_pre, norm_xa_post, norm_mem, xa_wq, xa_wk, xa_wv, xa_wo, norm_ffn_pre, norm_ffn_post, ffn_w_up, ffn_conv_w, ffn_conv_b, ffn_w_down, loss_target, m_norm_mix_pre, m_norm_mix_post, m_w_in, m_b_gate, m_ssm_a_re, m_ssm_a_im, m_ssm_log_dt, m_ssm_b_re, m_ssm_b_im, m_ssm_c_re, m_ssm_c_im, m_ssm_d, m_ssm_w_glu, m_ssm_b_glu, m_w_branch_attn, m_w_branch_ssm, m_w_out, m_norm_xa_pre, m_norm_xa_post, m_norm_mem, m_xa_wq, m_xa_wk, m_xa_wv, m_xa_wo, m_norm_ffn_pre, m_norm_ffn_post, m_ffn_w_up, m_ffn_conv_w, m_ffn_conv_b, m_ffn_w_down, v_norm_mix_pre, v_norm_mix_post, v_w_in, v_b_gate, v_ssm_a_re, v_ssm_a_im, v_ssm_log_dt, v_ssm_b_re, v_ssm_b_im, v_ssm_c_re, v_ssm_c_im, v_ssm_d, v_ssm_w_glu, v_ssm_b_glu, v_w_branch_attn, v_w_branch_ssm, v_w_out, v_norm_xa_pre, v_norm_xa_post, v_norm_mem, v_xa_wq, v_xa_wk, v_xa_wv, v_xa_wo, v_norm_ffn_pre, v_norm_ffn_post, v_ffn_w_up, v_ffn_conv_w, v_ffn_conv_b, v_ffn_w_down):
    args = dict(locals())
    w_loc = {n: args[n][0] for n in PARAM_ORDER}
    m_loc = {n: args["m_" + n][0] for n in PARAM_ORDER}
    v_loc = {n: args["v_" + n][0] for n in PARAM_ORDER}
    core_i = lax.axis_index("c")
    chip_i = 2 * lax.axis_index("x") + lax.axis_index("y")
    core = core_i.astype(jnp.int32).reshape(1)
    chip = chip_i.astype(jnp.int32).reshape(1)
    sharded = [n for n, _, _ in SHARDED]
    padded = ("ffn_w_up", "ffn_conv_w")

    def in_place(a):
        buf = lax.empty((N_XY, 2) + a.shape, a.dtype)
        return lax.dynamic_update_slice(buf, a[None, None], (chip_i, core_i) + (0,) * a.ndim)

    def as_local(n, a):
        return _pad_cols(a, FF_LOCAL_PAD) if n in padded else a

    wire = [in_place(as_local(n, w_loc[n]).astype(F32 if n == "ffn_conv_w" else BF16)) for n in sharded]
    wire = _fill_c(_fill_xy(wire, name="gather_w_xy"), name="gather_w_c")
    full = {n: b.reshape((N_DEV,) + b.shape[2:]) for n, b in zip(sharded, wire)}
    W = {}
    for n, shape, ax in SHARDED:
        W[n] = full[n].reshape(shape) if ax == 0 else full[n]
    for n in ("w_branch_attn", "w_branch_ssm", "ffn_conv_w"):
        W[n] = _blocks_to_cols(full[n])
    W["ffn_w_down"] = jnp.pad(W["ffn_w_down"].reshape(N_DEV // 2, FF_LOCAL, D_MODEL),
                              ((0, 0), (0, FF_LOCAL_PAD - FF_LOCAL), (0, 0))).reshape(FF_PAD, D_MODEL)

    P = {}
    for n, shape in REPLICATED:
        P[n] = w_loc[n] if len(shape) > 1 or n == "ssm_log_dt" else w_loc[n].reshape(1, -1)
    P["ffn_conv_b"] = _pad_cols(w_loc["ffn_conv_b"].reshape(N_DEV, FF_LOCAL), FF_LOCAL_PAD).reshape(1, 2 * FF_PAD)

    loss, grad_x, G = _local_step(x[0], mem[0], loss_target[0], W, P)
    loss = lax.psum(loss[0, 0], ("x", "y", "c"))

    G["w_branch_attn"] = _cols_to_blocks(G["w_branch_attn"], D_MODEL // N_DEV)
    G["w_branch_ssm"] = _cols_to_blocks(G["w_branch_ssm"], D_MODEL // N_DEV)
    G["ffn_conv_w"] = _cols_to_blocks(G["ffn_conv_w"], FF_LOCAL_PAD)
    G["ffn_w_down"] = G["ffn_w_down"].reshape(N_DEV // 2, FF_LOCAL_PAD, D_MODEL)[:, :FF_LOCAL]
    g8 = [G[n].reshape((N_XY, 2) + as_local(n, w_loc[n]).shape) for n in sharded]
    from_core = _send_c(g8, name="reduce_c")
    pair = [_pair_sum(g, r, core, name="pair_sum_" + n) for n, g, r in zip(sharded, g8, from_core)]
    from_chips = _scatter_xy(pair, name="reduce_xy")
    big_out = {}
    for n, own, parts in zip(sharded, pair, from_chips):
        res = _reduce_adamw(parts, as_local(n, w_loc[n]), as_local(n, m_loc[n]), as_local(n, v_loc[n]),
                            own=own, own_slot=chip, name="adamw_" + n)
        big_out[n] = [r[:, :FF_LOCAL] if n in padded else r for r in res]

    G["ffn_conv_b"] = G["ffn_conv_b"].reshape(N_DEV, FF_LOCAL_PAD)[:, :FF_LOCAL]
    parts, = _fill_c(_fill_xy([in_place(_pack_small(G))], name="gather_g_xy"), name="gather_g_c")
    parts = parts.reshape((N_DEV,) + parts.shape[2:])
    small_out = _reduce_adamw(parts, _pack_small(w_loc), _pack_small(m_loc), _pack_small(v_loc), name="adamw_replicated")
    small_out = [_unpack_small(b) for b in small_out]

    outs = [loss, grad_x[None]]
    for k in range(4):
        for n in PARAM_ORDER:
            src = big_out[n][k] if n in big_out else small_out[k][n]
            outs.append(src.reshape(args[n].shape))
    return tuple(outs)
```

```python
import functools
import math

import jax
import jax.numpy as jnp
from jax import lax
from jax.experimental import pallas as pl
from jax.experimental.pallas import tpu as pltpu

F32 = jnp.float32
BF16 = jnp.bfloat16

D_MODEL = 1024
SB_HEADS = 8
SB_HEAD_DIM = 64
SB_WIDTH = 512
SSM_WIDTH = 512
SSM_GROUP = 16
SSM_GROUPS = 32
SSM_STATE = 64
XA_HEADS = 4
XA_HEAD_DIM = 256
D_FF = 2816
RMS_EPS = 1e-6
IN_WIDTH = 4096
N_DEV = 8

ADAM_LR = 0.001
ADAM_B1 = 0.9
ADAM_B2 = 0.999
ADAM_EPS = 1e-08
ADAM_WD = 0.01
ADAM_STEP = 10

LANES = 128
SUBLANES = 8
VMEM_LIMIT = 48 * 1024 * 1024

_GELU_C = math.sqrt(2.0 / math.pi)


def _cparams(*sem):
    return pltpu.CompilerParams(dimension_semantics=sem, vmem_limit_bytes=VMEM_LIMIT)


def _pick(n, cands):
    for c in cands:
        if n % c == 0:
            return c
    return n


def _gelu(x):
    return 0.5 * x * (1.0 + jnp.tanh(_GELU_C * (x + 0.044715 * x * x * x)))


def _gelu_and_grad(x):
    t = jnp.tanh(_GELU_C * (x + 0.044715 * x * x * x))
    g = 0.5 * x * (1.0 + t)
    dg = 0.5 * (1.0 + t) + 0.5 * x * (1.0 - t * t) * _GELU_C * (1.0 + 3.0 * 0.044715 * x * x)
    return g, dg


def _sigmoid(x):
    return 1.0 / (1.0 + jnp.exp(-x))


def _dot(a, b, ca, cb):
    return lax.dot_general(a.astype(BF16), b.astype(BF16), (((ca,), (cb,)), ((), ())),
                           preferred_element_type=F32)


MM_TILES = (1024, 768, 512, 256, 128)


def _matmul(a, b, *, ta=False, tb=False, out_dtype=F32, name, b_block0=0, n_blocks=None,
            out_cb=None, out_into=None, out_block0=0, acc_in=None):
    if ta:
        K, M = a.shape
    else:
        M, K = a.shape
    b_cb = None
    if b.ndim == 3:
        b_cb = b.shape[2]
        n_blocks = b.shape[0] - b_block0 if n_blocks is None else n_blocks
        N, K2 = (b.shape[1], n_blocks * b_cb) if tb else (n_blocks * b_cb, b.shape[1])
    elif tb:
        N, K2 = b.shape
    else:
        K2, N = b.shape
    assert K == K2, (a.shape, b.shape, ta, tb)
    if out_into is not None:
        out_cb = out_into.shape[2]
    tm = _pick(M, MM_TILES)
    n_unit = math.gcd(N, math.gcd(b_cb if (b_cb and not tb) else N, out_cb or N))
    tn = _pick(n_unit, MM_TILES)
    k_unit = b_cb if (b_cb and tb) else K
    tk = _pick(k_unit, MM_TILES)
    nk = K // tk
    ca, cb = (0 if ta else 1), (1 if tb else 0)
    has_acc = acc_in is not None
    has_into = out_into is not None

    def body(*refs):
        a_ref, b_ref = refs[0], refs[1]
        pos = 2
        c_ref = None
        if has_acc:
            c_ref = refs[pos]
            pos += 1
        if has_into:
            pos += 1
        o_ref = refs[pos]
        p = _dot(a_ref[...], b_ref[...], ca, cb)
        if nk == 1:
            o_ref[...] = ((p + c_ref[...]) if has_acc else p).astype(out_dtype)
        else:
            acc_ref = refs[pos + 1]
            k = pl.program_id(2)

            @pl.when(k == 0)
            def _():
                acc_ref[...] = (p + c_ref[...]) if has_acc else p

            @pl.when(k > 0)
            def _():
                acc_ref[...] += p

            @pl.when(k == nk - 1)
            def _():
                o_ref[...] = acc_ref[...].astype(out_dtype)

    a_spec = pl.BlockSpec((tk, tm), lambda j, i, k: (k, i)) if ta else pl.BlockSpec((tm, tk), lambda j, i, k: (i, k))
    if b_cb is None:
        b_spec = pl.BlockSpec((tn, tk), lambda j, i, k: (j, k)) if tb else pl.BlockSpec((tk, tn), lambda j, i, k: (k, j))
    elif tb:
        per = b_cb // tk
        b_spec = pl.BlockSpec((None, tn, tk), lambda j, i, k: (b_block0 + k // per, j, k % per))
    else:
        per = b_cb // tn
        b_spec = pl.BlockSpec((None, tk, tn), lambda j, i, k: (b_block0 + j // per, k, j % per))
    in_specs = [a_spec, b_spec]
    operands = [a, b]
    aliases = {}
    if has_acc:
        in_specs.append(pl.BlockSpec((tm, tn), lambda j, i, k: (i, j)))
        operands.append(acc_in)
    if has_into:
        aliases = {len(operands): 0}
        in_specs.append(pl.BlockSpec(memory_space=pl.ANY))
        operands.append(out_into)
    if out_cb is None:
        out_shape = jax.ShapeDtypeStruct((M, N), out_dtype)
        out_spec = pl.BlockSpec((tm, tn), lambda j, i, k: (i, j))
    else:
        per_o = out_cb // tn
        out_shape = (jax.ShapeDtypeStruct(out_into.shape, out_into.dtype) if has_into
                     else jax.ShapeDtypeStruct((N // out_cb, M, out_cb), out_dtype))
        out_spec = pl.BlockSpec((None, tm, tn), lambda j, i, k: (out_block0 + j // per_o, i, j % per_o))
    return pl.pallas_call(
        body, name=name, out_shape=out_shape,
        grid=(N // tn, M // tm, nk),
        in_specs=in_specs, out_specs=out_spec, input_output_aliases=aliases,
        scratch_shapes=[] if nk == 1 else [pltpu.VMEM((tm, tn), F32)],
        compiler_params=_cparams("parallel", "parallel", "arbitrary"),
    )(*operands)


def _rms(x, g):
    r = lax.rsqrt(jnp.mean(x * x, axis=-1, keepdims=True) + RMS_EPS)
    return x * r * g


def _rms_bwd(dy, x, g):
    r = lax.rsqrt(jnp.mean(x * x, axis=-1, keepdims=True) + RMS_EPS)
    xh = x * r
    dxh = dy * g
    dx = r * (dxh - xh * jnp.mean(dxh * xh, axis=-1, keepdims=True))
    dg = jnp.sum(dy * xh, axis=0, keepdims=True)
    return dx, dg


def _row_tile(rows):
    return _pick(rows, (512, 256, 128, 64, 32, 16, 8))


def _rms_fwd(x, g, *, name):
    R, D = x.shape
    tr = _row_tile(R)

    def body(x_ref, g_ref, h_ref):
        h_ref[...] = _rms(x_ref[...], g_ref[...]).astype(BF16)

    return pl.pallas_call(
        body, name=name, out_shape=jax.ShapeDtypeStruct((R, D), BF16), grid=(R // tr,),
        in_specs=[pl.BlockSpec((tr, D), lambda i: (i, 0)), pl.BlockSpec((1, D), lambda i: (0, 0))],
        out_specs=pl.BlockSpec((tr, D), lambda i: (i, 0)),
        compiler_params=_cparams("parallel"),
    )(x, g)


def _resnorm_norm(x, z, g_post, g_next, *, name):
    R, D = x.shape
    tr = _row_tile(R)

    def body(x_ref, z_ref, gp_ref, gn_ref, xn_ref, h_ref):
        xn = x_ref[...] + _rms(z_ref[...], gp_ref[...])
        xn_ref[...] = xn
        h_ref[...] = _rms(xn, gn_ref[...]).astype(BF16)

    row = pl.BlockSpec((tr, D), lambda i: (i, 0))
    vec = pl.BlockSpec((1, D), lambda i: (0, 0))
    return pl.pallas_call(
        body, name=name,
        out_shape=(jax.ShapeDtypeStruct((R, D), F32), jax.ShapeDtypeStruct((R, D), BF16)),
        grid=(R // tr,), in_specs=[row, row, vec, vec], out_specs=(row, row),
        compiler_params=_cparams("parallel"),
    )(x, z, g_post, g_next)


def _final_loss(x, z, g_post, target, *, name):
    R, D = x.shape
    tr = _row_tile(R)

    def body(x_ref, z_ref, gp_ref, t_ref, loss_ref, dy_ref, dz_ref, dg_ref):
        i = pl.program_id(0)
        z = z_ref[...]
        g = gp_ref[...]
        err = x_ref[...] + _rms(z, g) - t_ref[...]
        dy = err * (1.0 / D)
        dy_ref[...] = dy
        dz, dg = _rms_bwd(dy, z, g)
        dz_ref[...] = dz.astype(BF16)
        part = 0.5 * jnp.sum(jnp.sum(err * err, axis=-1, keepdims=True) * (1.0 / D), axis=0, keepdims=True)

        @pl.when(i == 0)
        def _():
            loss_ref[...] = part
            dg_ref[...] = dg

        @pl.when(i > 0)
        def _():
            loss_ref[...] += part
            dg_ref[...] += dg

    row = pl.BlockSpec((tr, D), lambda i: (i, 0))
    vec = pl.BlockSpec((1, D), lambda i: (0, 0))
    return pl.pallas_call(
        body, name=name,
        out_shape=(jax.ShapeDtypeStruct((1, 1), F32), jax.ShapeDtypeStruct((R, D), F32),
                   jax.ShapeDtypeStruct((R, D), BF16), jax.ShapeDtypeStruct((1, D), F32)),
        grid=(R // tr,), in_specs=[row, row, vec, row],
        out_specs=(pl.BlockSpec((1, 1), lambda i: (0, 0)), row, row, vec),
        compiler_params=_cparams("arbitrary"),
    )(x, z, g_post, target)


def _norm_bwd_pair(dres, dh, xk, g_pre, zprev, g_prev_post, *, name, rider=None):
    R, D = xk.shape
    tr = _row_tile(R)

    def body(dres_ref, dh_ref, x_ref, gpre_ref, z_ref, gpost_ref, dx_ref, dz_ref, dgpre_ref, dgpost_ref):
        i = pl.program_id(0)
        d1, dgpre = _rms_bwd(dh_ref[...], x_ref[...], gpre_ref[...])
        dx = dres_ref[...] + d1
        dx_ref[...] = dx
        dz, dgpost = _rms_bwd(dx, z_ref[...], gpost_ref[...])
        dz_ref[...] = dz.astype(BF16)

        @pl.when(i == 0)
        def _():
            dgpre_ref[...] = dgpre
            dgpost_ref[...] = dgpost

        @pl.when(i > 0)
        def _():
            dgpre_ref[...] += dgpre
            dgpost_ref[...] += dgpost

    row = pl.BlockSpec((tr, D), lambda i: (i, 0))
    vec = pl.BlockSpec((1, D), lambda i: (0, 0))
    return _call(
        body, name=name, rider=rider,
        out_shape=(jax.ShapeDtypeStruct((R, D), F32), jax.ShapeDtypeStruct((R, D), BF16),
                   jax.ShapeDtypeStruct((1, D), F32), jax.ShapeDtypeStruct((1, D), F32)),
        grid=(R // tr,), in_specs=[row, row, row, vec, row, vec], out_specs=(row, row, vec, vec),
        scratch_shapes=[], operands=(dres, dh, xk, g_pre, zprev, g_prev_post))


def _norm_bwd_single(dres, dh, xk, g_pre, *, name):
    R, D = xk.shape
    tr = _row_tile(R)
    has_res = dres is not None

    def body(*refs):
        if has_res:
            dres_ref, dh_ref, x_ref, gpre_ref, dx_ref, dgpre_ref = refs
        else:
            dh_ref, x_ref, gpre_ref, dx_ref, dgpre_ref = refs
        i = pl.program_id(0)
        d1, dgpre = _rms_bwd(dh_ref[...], x_ref[...], gpre_ref[...])
        dx_ref[...] = dres_ref[...] + d1 if has_res else d1

        @pl.when(i == 0)
        def _():
            dgpre_ref[...] = dgpre

        @pl.when(i > 0)
        def _():
            dgpre_ref[...] += dgpre

    row = pl.BlockSpec((tr, D), lambda i: (i, 0))
    vec = pl.BlockSpec((1, D), lambda i: (0, 0))
    ins = ([dres] if has_res else []) + [dh, xk, g_pre]
    return pl.pallas_call(
        body, name=name,
        out_shape=(jax.ShapeDtypeStruct((R, D), F32), jax.ShapeDtypeStruct((1, D), F32)),
        grid=(R // tr,), in_specs=([row] if has_res else []) + [row, row, vec], out_specs=(row, vec),
        compiler_params=_cparams("arbitrary"),
    )(*ins)


SB_BLOCK = 256
SB_QBLOCK = 1024


def _sb_tri(kind):
    r = lax.broadcasted_iota(jnp.int32, (SB_BLOCK, SB_BLOCK), 0)
    c = lax.broadcasted_iota(jnp.int32, (SB_BLOCK, SB_BLOCK), 1)
    keep = {"after": r > c, "upto": r <= c, "before": r < c}[kind]
    return jnp.where(keep, 1.0, 0.0).astype(BF16)


def _running_sum(vals, tri):
    hi = vals.astype(BF16)
    lo = (vals - hi.astype(F32)).astype(BF16)
    return _dot(hi, tri, 1, 0) + _dot(lo, tri, 1, 0)


def _sb_scores(qm, k_blk):
    z = _dot(qm, k_blk, 1, 1)
    sp = jnp.maximum(z, 0.0) + jnp.log(1.0 + jnp.exp(-jnp.abs(z)))
    return z, sp


def _sb_causal(rows):
    r = lax.broadcasted_iota(jnp.int32, (rows, SB_BLOCK), 0)
    c = lax.broadcasted_iota(jnp.int32, (rows, SB_BLOCK), 1)
    return c < r


def _head_masks():
    lane = lax.broadcasted_iota(jnp.int32, (1, LANES), 1)
    return [jnp.where(lane < SB_HEAD_DIM, 1.0, 0.0), jnp.where(lane >= SB_HEAD_DIM, 1.0, 0.0)]


def _sb_fwd(proj, *, name, rider=None):
    S = proj.shape[0]
    T = SB_BLOCK
    TQ = min(SB_QBLOCK, S)
    span = TQ // T
    nq = S // TQ
    npair = SB_WIDTH // LANES
    scale = SB_HEAD_DIM ** -0.5

    def body(q_ref, k_ref, v_ref, o_ref, tot_ref, acc_ref, run_ref):
        masks = _head_masks()
        tri = _sb_tri("after")

        def q_block(i, _):
            qrow = pl.ds(pl.multiple_of(i * TQ, TQ), TQ)
            q = q_ref[qrow, :] * scale
            qm = [(q * m).astype(BF16) for m in masks]
            acc_ref[...] = jnp.zeros_like(acc_ref)
            run_ref[...] = jnp.zeros_like(run_ref)

            def k_block(j, own):
                krow = pl.ds(pl.multiple_of(j * T, T), T)
                k_blk = k_ref[krow, :].astype(BF16)
                v_blk = v_ref[krow, :].astype(BF16)
                r0 = 0 if own is None else own * T
                rows = pl.ds(r0, TQ - r0)
                for h in range(2):
                    z, sp = _sb_scores(qm[h][r0:], k_blk)
                    causal = None if own is None else _sb_causal(TQ - r0)
                    lf = -sp if causal is None else jnp.where(causal, -sp, 0.0)
                    e = jnp.exp(z - sp + _running_sum(lf, tri) + run_ref[h, rows])
                    w = e if causal is None else jnp.where(causal, e, 0.0)
                    acc_ref[h, rows] += _dot(w, v_blk, 1, 0)
                    run_ref[h, rows] += jnp.sum(lf, axis=1, keepdims=True)

            for d in reversed(range(span)):
                k_block(i * span + d, d)

            def below(jj, _):
                k_block(i * span - 1 - jj, None)
                return 0

            lax.fori_loop(0, i * span, below, 0)
            o_ref[qrow, :] = (acc_ref[0] * masks[0] + acc_ref[1] * masks[1]).astype(BF16)
            tot_ref[qrow, :] = run_ref[0] * masks[0] + run_ref[1] * masks[1]
            return 0

        lax.fori_loop(0, nq, q_block, 0)

    blk = lambda off: pl.BlockSpec((S, LANES), lambda p: (0, off + p))
    return _call(
        body, name=name, rider=rider,
        out_shape=(jax.ShapeDtypeStruct((S, SB_WIDTH), BF16), jax.ShapeDtypeStruct((S, SB_WIDTH), F32)),
        grid=(npair,),
        in_specs=[blk(0), blk(npair), blk(2 * npair)],
        out_specs=(blk(0), blk(0)),
        scratch_shapes=[pltpu.VMEM((2, TQ, LANES), F32), pltpu.VMEM((2, TQ, 1), F32)],
        operands=(proj, proj, proj))


def _sb_bwd(proj, tot, do_attn, *, name, rider=None):
    S = proj.shape[0]
    T = SB_BLOCK
    TQ = min(SB_QBLOCK, S)
    span = TQ // T
    nq = S // TQ
    npair = SB_WIDTH // LANES
    scale = SB_HEAD_DIM ** -0.5

    def body(q_ref, k_ref, v_ref, tot_ref, do_ref, dq_ref, dk_ref, dv_ref,
             dqacc_ref, dkacc_ref, dvacc_ref, run_ref, grun_ref):
        masks = _head_masks()
        tri_upto = _sb_tri("upto")
        tri_before = _sb_tri("before")
        dkacc_ref[...] = jnp.zeros_like(dkacc_ref)
        dvacc_ref[...] = jnp.zeros_like(dvacc_ref)

        def q_block(i, _):
            qrow = pl.ds(pl.multiple_of(i * TQ, TQ), TQ)
            q = q_ref[qrow, :] * scale
            do = do_ref[qrow, :].astype(F32)
            tot = tot_ref[qrow, :]
            qm = [(q * m).astype(BF16) for m in masks]
            dom = [(do * m).astype(BF16) for m in masks]
            ltot = [jnp.sum(tot * m, axis=1, keepdims=True) * (1.0 / SB_HEAD_DIM) for m in masks]
            dqacc_ref[...] = jnp.zeros_like(dqacc_ref)
            run_ref[...] = jnp.zeros_like(run_ref)
            grun_ref[...] = jnp.zeros_like(grun_ref)

            def k_block(j, own):
                krow = pl.ds(pl.multiple_of(j * T, T), T)
                k_blk = k_ref[krow, :].astype(BF16)
                v_blk = v_ref[krow, :].astype(BF16)
                r0 = 0 if own is None else own * T
                rows = pl.ds(r0, TQ - r0)
                for h in range(2):
                    z, sp = _sb_scores(qm[h][r0:], k_blk)
                    causal = None if own is None else _sb_causal(TQ - r0)
                    lf = -sp if causal is None else jnp.where(causal, -sp, 0.0)
                    later = ltot[h][r0:] - run_ref[h, rows] - _running_sum(lf, tri_upto)
                    beta = jnp.exp(z - sp)
                    w = jnp.exp(z - sp + later)
                    if causal is not None:
                        w = jnp.where(causal, w, 0.0)
                    g = _dot(dom[h][r0:], v_blk, 1, 1) * w
                    gbefore = grun_ref[h, rows] + _dot(g, tri_before, 1, 0)
                    dz = g - beta * (g + gbefore)
                    if causal is not None:
                        dz = jnp.where(causal, dz, 0.0)
                    dz = dz.astype(BF16)
                    dqacc_ref[h, rows] += _dot(dz, k_blk, 1, 0)
                    dkacc_ref[krow, :] += _dot(dz, qm[h][r0:], 0, 0)
                    dvacc_ref[krow, :] += _dot(w, dom[h][r0:], 0, 0)
                    run_ref[h, rows] += jnp.sum(lf, axis=1, keepdims=True)
                    grun_ref[h, rows] += jnp.sum(g, axis=1, keepdims=True)

            def above(j, _):
                k_block(j, None)
                return 0

            lax.fori_loop(0, i * span, above, 0)
            for d in range(span):
                k_block(i * span + d, d)
            dq_ref[qrow, :] = ((dqacc_ref[0] * masks[0] + dqacc_ref[1] * masks[1]) * scale).astype(BF16)
            return 0

        lax.fori_loop(0, nq, q_block, 0)
        dk_ref[...] = dkacc_ref[...].astype(BF16)
        dv_ref[...] = dvacc_ref[...].astype(BF16)

    blk = lambda off: pl.BlockSpec((S, LANES), lambda p: (0, off + p))
    out = jax.ShapeDtypeStruct((S, SB_WIDTH), BF16)
    return _call(
        body, name=name, rider=rider, out_shape=(out, out, out), grid=(npair,),
        in_specs=[blk(0), blk(npair), blk(2 * npair), blk(0), blk(0)],
        out_specs=(blk(0), blk(0), blk(0)),
        scratch_shapes=[pltpu.VMEM((2, TQ, LANES), F32), pltpu.VMEM((S, LANES), F32), pltpu.VMEM((S, LANES), F32),
                        pltpu.VMEM((2, TQ, 1), F32), pltpu.VMEM((2, TQ, 1), F32)],
        operands=(proj, proj, proj, tot, do_attn))


SSM_HALVES = 2
SSM_HALF_CH = SSM_WIDTH // SSM_HALVES
SSM_HALF_ST = SSM_GROUPS * SSM_STATE // SSM_HALVES
SSM_CHUNK = 512


def _cmul(ar, ai, br, bi):
    return ar * br - ai * bi, ar * bi + ai * br


def _ssm_tables(lam_re, lam_im):
    lr = lam_re.reshape(-1)
    li = lam_im.reshape(-1)
    pows = [(jnp.ones_like(lr), jnp.zeros_like(li)), (lr, li)]
    for _ in range(2, SUBLANES + 1):
        pows.append(_cmul(pows[-1][0], pows[-1][1], lr, li))
    row = jnp.arange(SUBLANES)[:, None]

    def shift_tab(d, keep):
        return [jnp.where(keep, pows[d][0][None, :], 0.0), jnp.where(keep, pows[d][1][None, :], 0.0)]

    fwd, bwd = [], []
    for d in (1, 2, 4):
        fwd += shift_tab(d, row >= d)
        bwd += shift_tab(d, row + d < SUBLANES)
    fwd += [jnp.stack([pows[r + 1][0] for r in range(SUBLANES)]), jnp.stack([pows[r + 1][1] for r in range(SUBLANES)])]
    bwd += [jnp.stack([pows[SUBLANES - r][0] for r in range(SUBLANES)]),
            jnp.stack([pows[SUBLANES - r][1] for r in range(SUBLANES)])]

    def halves(tabs):
        t = jnp.stack(tabs)
        return t.reshape(8, SUBLANES, SSM_HALVES, SSM_HALF_ST).transpose(2, 0, 1, 3)

    return halves(fwd), halves(bwd)


def _ssm_fwd(proj, bd_re, bd_im, cd_re, cd_imneg, d_skip, tab, *, name, rider=None):
    S = proj.shape[0]
    Tc = min(SSM_CHUNK, S)
    nc = S // Tc
    u_blk0 = (3 * SB_WIDTH) // SSM_HALF_CH

    def body(u_ref, bre_ref, bim_ref, cre_ref, cim_ref, d_ref, tab_ref, y_ref, xre_ref, xim_ref, cre_s, cim_s):
        c = pl.program_id(1)

        @pl.when(c == 0)
        def _():
            cre_s[...] = jnp.zeros_like(cre_s)
            cim_s[...] = jnp.zeros_like(cim_s)

        u = u_ref[...]
        ub = u.astype(BF16)
        xre_ref[...] = _dot(ub, bre_ref[0], 1, 0)
        xim_ref[...] = _dot(ub, bim_ref[0], 1, 0)

        def slab(k, carry):
            car_re, car_im = carry
            rows = pl.ds(pl.multiple_of(k * SUBLANES, SUBLANES), SUBLANES)
            sre = xre_ref[rows, :]
            sim = xim_ref[rows, :]
            for n, d in enumerate((1, 2, 4)):
                pre, pim = tab_ref[0, 2 * n], tab_ref[0, 2 * n + 1]
                rre = pltpu.roll(sre, d, 0)
                rim = pltpu.roll(sim, d, 0)
                sre, sim = sre + (pre * rre - pim * rim), sim + (pre * rim + pim * rre)
            pre, pim = tab_ref[0, 6], tab_ref[0, 7]
            sre, sim = sre + (pre * car_re - pim * car_im), sim + (pre * car_im + pim * car_re)
            xre_ref[rows, :] = sre
            xim_ref[rows, :] = sim
            last = (SUBLANES - 1, SUBLANES)
            return (jnp.broadcast_to(sre[last[0]:last[1], :], sre.shape),
                    jnp.broadcast_to(sim[last[0]:last[1], :], sim.shape))

        car = lax.fori_loop(0, Tc // SUBLANES, slab, (cre_s[...], cim_s[...]))
        cre_s[...] = car[0]
        cim_s[...] = car[1]
        y = _dot(xre_ref[...], cre_ref[0], 1, 0) + _dot(xim_ref[...], cim_ref[0], 1, 0)
        y_ref[...] = y + d_ref[...] * u

    return _call(
        body, name=name, rider=rider,
        out_shape=(jax.ShapeDtypeStruct((S, SSM_WIDTH), F32),
                   jax.ShapeDtypeStruct((S, SSM_HALVES * SSM_HALF_ST), F32),
                   jax.ShapeDtypeStruct((S, SSM_HALVES * SSM_HALF_ST), F32)),
        grid=(SSM_HALVES, nc),
        in_specs=[pl.BlockSpec((Tc, SSM_HALF_CH), lambda h, c: (c, u_blk0 + h)),
                  pl.BlockSpec((1, SSM_HALF_CH, SSM_HALF_ST), lambda h, c: (h, 0, 0)),
                  pl.BlockSpec((1, SSM_HALF_CH, SSM_HALF_ST), lambda h, c: (h, 0, 0)),
                  pl.BlockSpec((1, SSM_HALF_ST, SSM_HALF_CH), lambda h, c: (h, 0, 0)),
                  pl.BlockSpec((1, SSM_HALF_ST, SSM_HALF_CH), lambda h, c: (h, 0, 0)),
                  pl.BlockSpec((1, SSM_HALF_CH), lambda h, c: (0, h)),
                  pl.BlockSpec((1, 8, SUBLANES, SSM_HALF_ST), lambda h, c: (h, 0, 0, 0))],
        out_specs=(pl.BlockSpec((Tc, SSM_HALF_CH), lambda h, c: (c, h)),
                   pl.BlockSpec((Tc, SSM_HALF_ST), lambda h, c: (c, h)),
                   pl.BlockSpec((Tc, SSM_HALF_ST), lambda h, c: (c, h))),
        scratch_shapes=[pltpu.VMEM((SUBLANES, SSM_HALF_ST), F32), pltpu.VMEM((SUBLANES, SSM_HALF_ST), F32)],
        operands=(proj, bd_re, bd_im, cd_re, cd_imneg, d_skip, tab))


def _ssm_bwd(dy, proj, x_re, x_im, bd_re, bd_im, cd_re, cd_imneg, d_skip, tab, *, name, rider=None):
    S = proj.shape[0]
    Tc = min(SSM_CHUNK, S)
    nc = S // Tc
    u_blk0 = (3 * SB_WIDTH) // SSM_HALF_CH

    def body(dy_ref, u_ref, xre_ref, xim_ref, bre_ref, bim_ref, cre_ref, cim_ref, d_ref, tab_ref,
             du_ref, dbre_ref, dbim_ref, dcre_ref, dcim_ref, dd_ref, dlre_ref, dlim_ref,
             gre_s, gim_s, cre_s, cim_s):
        c = pl.program_id(1)

        @pl.when(c == 0)
        def _():
            cre_s[...] = jnp.zeros_like(cre_s)
            cim_s[...] = jnp.zeros_like(cim_s)
            dbre_ref[...] = jnp.zeros_like(dbre_ref)
            dbim_ref[...] = jnp.zeros_like(dbim_ref)
            dcre_ref[...] = jnp.zeros_like(dcre_ref)
            dcim_ref[...] = jnp.zeros_like(dcim_ref)
            dd_ref[...] = jnp.zeros_like(dd_ref)
            dlre_ref[...] = jnp.zeros_like(dlre_ref)
            dlim_ref[...] = jnp.zeros_like(dlim_ref)

        dy = dy_ref[...]
        dyb = dy.astype(BF16)
        u = u_ref[...]
        gre_s[...] = _dot(dyb, cre_ref[0], 1, 1)
        gim_s[...] = _dot(dyb, cim_ref[0], 1, 1)
        row = lax.broadcasted_iota(jnp.int32, (SUBLANES, SSM_HALF_ST), 0)
        nslab = Tc // SUBLANES

        def slab(kk, carry):
            car_re, car_im, acc_re, acc_im = carry
            k = nslab - 1 - kk
            rows = pl.ds(pl.multiple_of(k * SUBLANES, SUBLANES), SUBLANES)
            sre = gre_s[rows, :]
            sim = gim_s[rows, :]
            for n, d in enumerate((1, 2, 4)):
                pre, pim = tab_ref[0, 2 * n], tab_ref[0, 2 * n + 1]
                rre = pltpu.roll(sre, SUBLANES - d, 0)
                rim = pltpu.roll(sim, SUBLANES - d, 0)
                sre, sim = sre + (pre * rre + pim * rim), sim + (pre * rim - pim * rre)
            pre, pim = tab_ref[0, 6], tab_ref[0, 7]
            sre, sim = sre + (pre * car_re + pim * car_im), sim + (pre * car_im - pim * car_re)
            gre_s[rows, :] = sre
            gim_s[rows, :] = sim
            nre = jnp.where(row == SUBLANES - 1, car_re, pltpu.roll(sre, SUBLANES - 1, 0))
            nim = jnp.where(row == SUBLANES - 1, car_im, pltpu.roll(sim, SUBLANES - 1, 0))
            xr = xre_ref[rows, :]
            xi = xim_ref[rows, :]
            acc_re = acc_re + (nre * xr + nim * xi)
            acc_im = acc_im + (nim * xr - nre * xi)
            return (jnp.broadcast_to(sre[0:1, :], sre.shape), jnp.broadcast_to(sim[0:1, :], sim.shape), acc_re, acc_im)

        car = lax.fori_loop(0, nslab, slab, (cre_s[...], cim_s[...], dlre_ref[0], dlim_ref[0]))
        cre_s[...] = car[0]
        cim_s[...] = car[1]
        dlre_ref[0] = car[2]
        dlim_ref[0] = car[3]
        gre = gre_s[...].astype(BF16)
        gim = gim_s[...].astype(BF16)
        ub = u.astype(BF16)
        du = _dot(gre, bre_ref[0], 1, 1) + _dot(gim, bim_ref[0], 1, 1) + d_ref[...] * dy
        du_ref[...] = du.astype(BF16)
        dbre_ref[0] += _dot(ub, gre, 0, 0)
        dbim_ref[0] += _dot(ub, gim, 0, 0)
        dcre_ref[0] += _dot(xre_ref[...], dyb, 0, 0)
        dcim_ref[0] += _dot(xim_ref[...], dyb, 0, 0)
        dd_ref[...] += jnp.sum(dy * u, axis=0, keepdims=True)

    rev = lambda c: nc - 1 - c
    return _call(
        body, name=name, rider=rider,
        out_shape=(jax.ShapeDtypeStruct((S, SSM_WIDTH), BF16),
                   jax.ShapeDtypeStruct((SSM_HALVES, SSM_HALF_CH, SSM_HALF_ST), F32),
                   jax.ShapeDtypeStruct((SSM_HALVES, SSM_HALF_CH, SSM_HALF_ST), F32),
                   jax.ShapeDtypeStruct((SSM_HALVES, SSM_HALF_ST, SSM_HALF_CH), F32),
                   jax.ShapeDtypeStruct((SSM_HALVES, SSM_HALF_ST, SSM_HALF_CH), F32),
                   jax.ShapeDtypeStruct((1, SSM_WIDTH), F32),
                   jax.ShapeDtypeStruct((SSM_HALVES, SUBLANES, SSM_HALF_ST), F32),
                   jax.ShapeDtypeStruct((SSM_HALVES, SUBLANES, SSM_HALF_ST), F32)),
        grid=(SSM_HALVES, nc),
        in_specs=[pl.BlockSpec((Tc, SSM_HALF_CH), lambda h, c: (rev(c), h)),
                  pl.BlockSpec((Tc, SSM_HALF_CH), lambda h, c: (rev(c), u_blk0 + h)),
                  pl.BlockSpec((Tc, SSM_HALF_ST), lambda h, c: (rev(c), h)),
                  pl.BlockSpec((Tc, SSM_HALF_ST), lambda h, c: (rev(c), h)),
                  pl.BlockSpec((1, SSM_HALF_CH, SSM_HALF_ST), lambda h, c: (h, 0, 0)),
                  pl.BlockSpec((1, SSM_HALF_CH, SSM_HALF_ST), lambda h, c: (h, 0, 0)),
                  pl.BlockSpec((1, SSM_HALF_ST, SSM_HALF_CH), lambda h, c: (h, 0, 0)),
                  pl.BlockSpec((1, SSM_HALF_ST, SSM_HALF_CH), lambda h, c: (h, 0, 0)),
                  pl.BlockSpec((1, SSM_HALF_CH), lambda h, c: (0, h)),
                  pl.BlockSpec((1, 8, SUBLANES, SSM_HALF_ST), lambda h, c: (h, 0, 0, 0))],
        out_specs=(pl.BlockSpec((Tc, SSM_HALF_CH), lambda h, c: (rev(c), h)),
                   pl.BlockSpec((1, SSM_HALF_CH, SSM_HALF_ST), lambda h, c: (h, 0, 0)),
                   pl.BlockSpec((1, SSM_HALF_CH, SSM_HALF_ST), lambda h, c: (h, 0, 0)),
                   pl.BlockSpec((1, SSM_HALF_ST, SSM_HALF_CH), lambda h, c: (h, 0, 0)),
                   pl.BlockSpec((1, SSM_HALF_ST, SSM_HALF_CH), lambda h, c: (h, 0, 0)),
                   pl.BlockSpec((1, SSM_HALF_CH), lambda h, c: (0, h)),
                   pl.BlockSpec((1, SUBLANES, SSM_HALF_ST), lambda h, c: (h, 0, 0)),
                   pl.BlockSpec((1, SUBLANES, SSM_HALF_ST), lambda h, c: (h, 0, 0))),
        scratch_shapes=[pltpu.VMEM((Tc, SSM_HALF_ST), F32), pltpu.VMEM((Tc, SSM_HALF_ST), F32),
                        pltpu.VMEM((SUBLANES, SSM_HALF_ST), F32), pltpu.VMEM((SUBLANES, SSM_HALF_ST), F32)],
        operands=(dy, proj, x_re, x_im, bd_re, bd_im, cd_re, cd_imneg, d_skip, tab))


def _ssm_prepare(a_re, a_im, log_dt, b_re, b_im):
    dt = jnp.exp(log_dt)[:, None]
    mag = jnp.exp(a_re * dt)
    lre = mag * jnp.cos(a_im * dt)
    lim = mag * jnp.sin(a_im * dt)
    den = a_re * a_re + a_im * a_im
    fre = ((lre - 1.0) * a_re + lim * a_im) / den
    fim = (lim * a_re - (lre - 1.0) * a_im) / den
    bbre = fre[:, :, None] * b_re - fim[:, :, None] * b_im
    bbim = fre[:, :, None] * b_im + fim[:, :, None] * b_re
    return lre, lim, bbre, bbim


def _group_eye():
    return jnp.eye(SSM_GROUPS // SSM_HALVES, dtype=F32)


def _bd_from_bbar(bbar):
    gh = SSM_GROUPS // SSM_HALVES
    b = bbar.reshape(SSM_HALVES, gh, SSM_STATE, SSM_GROUP).transpose(0, 1, 3, 2)
    out = b[:, :, :, None, :] * _group_eye()[None, :, None, :, None]
    return out.reshape(SSM_HALVES, SSM_HALF_CH, SSM_HALF_ST)


def _bbar_from_bd(dbd):
    gh = SSM_GROUPS // SSM_HALVES
    d = dbd.reshape(SSM_HALVES, gh, SSM_GROUP, gh, SSM_STATE)
    d = jnp.sum(d * _group_eye()[None, :, None, :, None], axis=3)
    return d.transpose(0, 1, 3, 2).reshape(SSM_GROUPS, SSM_STATE, SSM_GROUP)


def _cd_from_c(cmat):
    gh = SSM_GROUPS // SSM_HALVES
    c = cmat.reshape(SSM_HALVES, gh, SSM_GROUP, SSM_STATE).transpose(0, 1, 3, 2)
    out = c[:, :, :, None, :] * _group_eye()[None, :, None, :, None]
    return out.reshape(SSM_HALVES, SSM_HALF_ST, SSM_HALF_CH)


def _c_from_cd(dcd):
    gh = SSM_GROUPS // SSM_HALVES
    d = dcd.reshape(SSM_HALVES, gh, SSM_STATE, gh, SSM_GROUP)
    d = jnp.sum(d * _group_eye()[None, :, None, :, None], axis=3)
    return d.transpose(0, 1, 3, 2).reshape(SSM_GROUPS, SSM_GROUP, SSM_STATE)


def _glu_fwd(y_pre, w_glu, b_glu, *, name):
    S, W = y_pre.shape
    tr = _row_tile(S)

    def body(y_ref, w_ref, b_ref, o_ref):
        yg = _gelu(y_ref[...])
        gl = _dot(yg, w_ref[...], 1, 0) + b_ref[...]
        o_ref[...] = (yg * _sigmoid(gl)).astype(BF16)

    row = pl.BlockSpec((tr, W), lambda i: (i, 0))
    return pl.pallas_call(
        body, name=name, out_shape=jax.ShapeDtypeStruct((S, W), BF16), grid=(S // tr,),
        in_specs=[row, pl.BlockSpec((W, W), lambda i: (0, 0)), pl.BlockSpec((1, W), lambda i: (0, 0))],
        out_specs=row, compiler_params=_cparams("parallel"),
    )(y_pre, w_glu, b_glu)


def _glu_bwd(y_pre, do, w_glu, b_glu, *, name):
    S, W = y_pre.shape
    tr = _row_tile(S)

    def body(y_ref, do_ref, w_ref, b_ref, dy_ref, dw_ref, db_ref):
        i = pl.program_id(0)
        yg, dyg_dy = _gelu_and_grad(y_ref[...])
        ygb = yg.astype(BF16)
        sg = _sigmoid(_dot(ygb, w_ref[...], 1, 0) + b_ref[...])
        do = do_ref[...]
        dgl = do * yg * sg * (1.0 - sg)
        dglb = dgl.astype(BF16)
        dyg = do * sg + _dot(dglb, w_ref[...], 1, 1)
        dy_ref[...] = dyg * dyg_dy
        dw = _dot(ygb, dglb, 0, 0)
        db = jnp.sum(dgl, axis=0, keepdims=True)

        @pl.when(i == 0)
        def _():
            dw_ref[...] = dw
            db_ref[...] = db

        @pl.when(i > 0)
        def _():
            dw_ref[...] += dw
            db_ref[...] += db

    row = pl.BlockSpec((tr, W), lambda i: (i, 0))
    full = pl.BlockSpec((W, W), lambda i: (0, 0))
    vec = pl.BlockSpec((1, W), lambda i: (0, 0))
    return pl.pallas_call(
        body, name=name,
        out_shape=(jax.ShapeDtypeStruct((S, W), F32), jax.ShapeDtypeStruct((W, W), F32), jax.ShapeDtypeStruct((1, W), F32)),
        grid=(S // tr,), in_specs=[row, row, full, vec], out_specs=(row, full, vec),
        compiler_params=_cparams("arbitrary"),
    )(y_pre, do, w_glu, b_glu)


GATE_COL0 = 3 * SB_WIDTH + SSM_WIDTH


def _merge_fwd(proj, o_attn, o_ssm, w_ba, w_bs, b_gate, *, name):
    S = proj.shape[0]
    D = D_MODEL
    tr = _pick(S, (256, 128, 64, 32, 16, 8))
    gb = GATE_COL0 // D

    def body(ga_ref, gs_ref, oa_ref, os_ref, wa_ref, ws_ref, ba_ref, bs_ref, m_ref):
        pa = _dot(oa_ref[...], wa_ref[...], 1, 0)
        ps = _dot(os_ref[...], ws_ref[...], 1, 0)
        sa = _sigmoid(ga_ref[...] + ba_ref[...])
        ss = _sigmoid(gs_ref[...] + bs_ref[...])
        m_ref[...] = (sa * pa + ss * ps).astype(BF16)

    return pl.pallas_call(
        body, name=name, out_shape=jax.ShapeDtypeStruct((S, D), BF16), grid=(S // tr,),
        in_specs=[pl.BlockSpec((tr, D), lambda i: (i, gb)), pl.BlockSpec((tr, D), lambda i: (i, gb + 1)),
                  pl.BlockSpec((tr, SB_WIDTH), lambda i: (i, 0)), pl.BlockSpec((tr, SSM_WIDTH), lambda i: (i, 0)),
                  pl.BlockSpec((SB_WIDTH, D), lambda i: (0, 0)), pl.BlockSpec((SSM_WIDTH, D), lambda i: (0, 0)),
                  pl.BlockSpec((1, D), lambda i: (0, 0)), pl.BlockSpec((1, D), lambda i: (0, 1))],
        out_specs=pl.BlockSpec((tr, D), lambda i: (i, 0)),
        compiler_params=_cparams("parallel"),
    )(proj, proj, o_attn, o_ssm, w_ba, w_bs, b_gate, b_gate)


def _merge_bwd(dmerged, proj, o_attn, o_ssm, w_ba, w_bs, b_gate, *, name):
    S = proj.shape[0]
    D = D_MODEL
    tr = _pick(S, (256, 128, 64, 32, 16, 8))
    gb = GATE_COL0 // D

    def body(dm_ref, ga_ref, gs_ref, oa_ref, os_ref, wa_ref, ws_ref, ba_ref, bs_ref,
             doa_ref, dos_ref, dg_ref, db_ref, dwa_ref, dws_ref):
        i = pl.program_id(0)
        dm = dm_ref[...]
        oa = oa_ref[...]
        osm = os_ref[...]
        pa = _dot(oa, wa_ref[...], 1, 0)
        ps = _dot(osm, ws_ref[...], 1, 0)
        sa = _sigmoid(ga_ref[...] + ba_ref[...])
        ss = _sigmoid(gs_ref[...] + bs_ref[...])
        dpa = (dm * sa).astype(BF16)
        dps = (dm * ss).astype(BF16)
        dga = dm * pa * sa * (1.0 - sa)
        dgs = dm * ps * ss * (1.0 - ss)
        dg_ref[:, :D] = dga.astype(BF16)
        dg_ref[:, D:] = dgs.astype(BF16)
        doa_ref[...] = _dot(dpa, wa_ref[...], 1, 1).astype(BF16)
        dos_ref[...] = _dot(dps, ws_ref[...], 1, 1)
        dwa = _dot(oa, dpa, 0, 0)
        dws = _dot(osm, dps, 0, 0)
        dba = jnp.sum(dga, axis=0, keepdims=True)
        dbs = jnp.sum(dgs, axis=0, keepdims=True)

        @pl.when(i == 0)
        def _():
            dwa_ref[...] = dwa
            dws_ref[...] = dws
            db_ref[:, :D] = dba
            db_ref[:, D:] = dbs

        @pl.when(i > 0)
        def _():
            dwa_ref[...] += dwa
            dws_ref[...] += dws
            db_ref[:, :D] += dba
            db_ref[:, D:] += dbs

    rowD = pl.BlockSpec((tr, D), lambda i: (i, 0))
    wspec = pl.BlockSpec((SB_WIDTH, D), lambda i: (0, 0))
    return pl.pallas_call(
        body, name=name,
        out_shape=(jax.ShapeDtypeStruct((S, SB_WIDTH), BF16), jax.ShapeDtypeStruct((S, SSM_WIDTH), F32),
                   jax.ShapeDtypeStruct((S, 2 * D), BF16), jax.ShapeDtypeStruct((1, 2 * D), F32),
                   jax.ShapeDtypeStruct((SB_WIDTH, D), F32), jax.ShapeDtypeStruct((SSM_WIDTH, D), F32)),
        grid=(S // tr,),
        in_specs=[rowD, pl.BlockSpec((tr, D), lambda i: (i, gb)), pl.BlockSpec((tr, D), lambda i: (i, gb + 1)),
                  pl.BlockSpec((tr, SB_WIDTH), lambda i: (i, 0)), pl.BlockSpec((tr, SSM_WIDTH), lambda i: (i, 0)),
                  wspec, wspec, pl.BlockSpec((1, D), lambda i: (0, 0)), pl.BlockSpec((1, D), lambda i: (0, 1))],
        out_specs=(pl.BlockSpec((tr, SB_WIDTH), lambda i: (i, 0)), pl.BlockSpec((tr, SSM_WIDTH), lambda i: (i, 0)),
                   pl.BlockSpec((tr, 2 * D), lambda i: (i, 0)), pl.BlockSpec((1, 2 * D), lambda i: (0, 0)),
                   wspec, wspec),
        compiler_params=_cparams("arbitrary"),
    )(dmerged, proj, proj, o_attn, o_ssm, w_ba, w_bs, b_gate, b_gate)


def _xattn_probs(q, k, h):
    cols = slice(h * XA_HEAD_DIM, (h + 1) * XA_HEAD_DIM)
    s = _dot(q[:, cols], k[:, cols], 1, 1) * (XA_HEAD_DIM ** -0.5)
    s = s - jnp.max(s, axis=-1, keepdims=True)
    e = jnp.exp(s)
    return e / jnp.sum(e, axis=-1, keepdims=True), cols


def _xattn_fwd(q2, k2, v2, *, name):
    S, D = q2.shape
    M = k2.shape[0]
    tr = _row_tile(S)

    def body(q_ref, k_ref, v_ref, o_ref):
        q = q_ref[...]
        k = k_ref[...]
        v = v_ref[...]
        for h in range(XA_HEADS):
            p, cols = _xattn_probs(q, k, h)
            o_ref[:, cols] = _dot(p, v[:, cols], 1, 0).astype(BF16)

    row = pl.BlockSpec((tr, D), lambda i: (i, 0))
    memb = pl.BlockSpec((M, D), lambda i: (0, 0))
    return pl.pallas_call(
        body, name=name, out_shape=jax.ShapeDtypeStruct((S, D), BF16), grid=(S // tr,),
        in_specs=[row, memb, memb], out_specs=row, compiler_params=_cparams("parallel"),
    )(q2, k2, v2)


def _xattn_bwd(q2, k2, v2, do2, *, name):
    S, D = q2.shape
    M = k2.shape[0]
    tr = _row_tile(S)
    scale = XA_HEAD_DIM ** -0.5

    def body(q_ref, k_ref, v_ref, do_ref, dq_ref, dk_ref, dv_ref):
        i = pl.program_id(0)

        @pl.when(i == 0)
        def _():
            dk_ref[...] = jnp.zeros_like(dk_ref)
            dv_ref[...] = jnp.zeros_like(dv_ref)

        q = q_ref[...]
        k = k_ref[...]
        v = v_ref[...]
        do = do_ref[...]
        for h in range(XA_HEADS):
            p, cols = _xattn_probs(q, k, h)
            dp = _dot(do[:, cols], v[:, cols], 1, 1)
            ds = (p * (dp - jnp.sum(dp * p, axis=-1, keepdims=True)) * scale).astype(BF16)
            dq_ref[:, cols] = _dot(ds, k[:, cols], 1, 0).astype(BF16)
            dk_ref[:, cols] += _dot(ds, q[:, cols], 0, 0)
            dv_ref[:, cols] += _dot(p, do[:, cols], 0, 0)

    row = pl.BlockSpec((tr, D), lambda i: (i, 0))
    memb = pl.BlockSpec((M, D), lambda i: (0, 0))
    return pl.pallas_call(
        body, name=name,
        out_shape=(jax.ShapeDtypeStruct((S, D), BF16), jax.ShapeDtypeStruct((M, D), F32), jax.ShapeDtypeStruct((M, D), F32)),
        grid=(S // tr,), in_specs=[row, memb, memb, row], out_specs=(row, memb, memb),
        compiler_params=_cparams("arbitrary"),
    )(q2, k2, v2, do2)


CONV_ROWS = 512


def _shift_down(ref, t0, rows, d):
    cur = ref[pl.ds(t0, rows), :]
    out = pltpu.roll(cur, d, 0)
    r = lax.broadcasted_iota(jnp.int32, cur.shape, 0)
    for e in range(d):
        src = t0 - d + e
        prev = ref[pl.ds(src, 1), :] if src >= 0 else jnp.zeros((1, cur.shape[1]), cur.dtype)
        out = jnp.where(r == e, prev, out)
    return out


def _shift_up(ref, t0, rows, d, total):
    cur = ref[pl.ds(t0, rows), :]
    out = pltpu.roll(cur, rows - d, 0)
    r = lax.broadcasted_iota(jnp.int32, cur.shape, 0)
    for e in range(d):
        src = t0 + rows + e
        nxt = ref[pl.ds(src, 1), :] if src < total else jnp.zeros((1, cur.shape[1]), cur.dtype)
        out = jnp.where(r == rows - d + e, nxt, out)
    return out


def _conv3(ref, w_ref, b_ref, t0, rows):
    return (w_ref[2:3, :] * ref[pl.ds(t0, rows), :] + w_ref[1:2, :] * _shift_down(ref, t0, rows, 1)
            + w_ref[0:1, :] * _shift_down(ref, t0, rows, 2) + b_ref[...])


def _convgate_fwd(up_g, up_v, conv_w, conv_b, *, name):
    S, H = up_g.shape
    nb = H // LANES
    R = min(CONV_ROWS, S)

    def body(g_ref, v_ref, wg_ref, wv_ref, bg_ref, bv_ref, a_ref):
        for t0 in range(0, S, R):
            cg = _conv3(g_ref, wg_ref, bg_ref, t0, R)
            cv = _conv3(v_ref, wv_ref, bv_ref, t0, R)
            a_ref[pl.ds(t0, R), :] = (_gelu(cg) * cv).astype(BF16)

    col = lambda off: pl.BlockSpec((S, LANES), lambda j: (0, off + j))
    wcol = lambda off: pl.BlockSpec((3, LANES), lambda j: (0, off + j))
    bcol = lambda off: pl.BlockSpec((1, LANES), lambda j: (0, off + j))
    return pl.pallas_call(
        body, name=name, out_shape=jax.ShapeDtypeStruct((S, H), BF16), grid=(nb,),
        in_specs=[col(0), col(0), wcol(0), wcol(nb), bcol(0), bcol(nb)],
        out_specs=col(0), compiler_params=_cparams("parallel"),
    )(up_g, up_v, conv_w, conv_w, conv_b, conv_b)


def _convgate_bwd(up_g, up_v, da, conv_w, conv_b, *, name):
    S, H = up_g.shape
    nb = H // LANES
    R = min(CONV_ROWS, S)

    def body(g_ref, v_ref, da_ref, wg_ref, wv_ref, bg_ref, bv_ref,
             dug_ref, duv_ref, dwg_ref, dwv_ref, dbg_ref, dbv_ref, dcg_s, dcv_s):
        zero3 = jnp.zeros((1, LANES), F32)
        acc = {"g": [zero3, zero3, zero3, zero3], "v": [zero3, zero3, zero3, zero3]}
        for t0 in range(0, S, R):
            cg = _conv3(g_ref, wg_ref, bg_ref, t0, R)
            cv = _conv3(v_ref, wv_ref, bv_ref, t0, R)
            da = da_ref[pl.ds(t0, R), :]
            gl, dgl = _gelu_and_grad(cg)
            dcg = da * cv * dgl
            dcv = da * gl
            dcg_s[pl.ds(t0, R), :] = dcg
            dcv_s[pl.ds(t0, R), :] = dcv
            for key, ref, dc in (("g", g_ref, dcg), ("v", v_ref, dcv)):
                a = acc[key]
                a[2] = a[2] + jnp.sum(dc * ref[pl.ds(t0, R), :], axis=0, keepdims=True)
                a[1] = a[1] + jnp.sum(dc * _shift_down(ref, t0, R, 1), axis=0, keepdims=True)
                a[0] = a[0] + jnp.sum(dc * _shift_down(ref, t0, R, 2), axis=0, keepdims=True)
                a[3] = a[3] + jnp.sum(dc, axis=0, keepdims=True)
        for key, dw_ref, db_ref in (("g", dwg_ref, dbg_ref), ("v", dwv_ref, dbv_ref)):
            a = acc[key]
            dw_ref[0:1, :] = a[0]
            dw_ref[1:2, :] = a[1]
            dw_ref[2:3, :] = a[2]
            db_ref[...] = a[3]
        for t0 in range(0, S, R):
            for dc_s, w_ref, du_ref in ((dcg_s, wg_ref, dug_ref), (dcv_s, wv_ref, duv_ref)):
                du = (w_ref[2:3, :] * dc_s[pl.ds(t0, R), :] + w_ref[1:2, :] * _shift_up(dc_s, t0, R, 1, S)
                      + w_ref[0:1, :] * _shift_up(dc_s, t0, R, 2, S))
                du_ref[pl.ds(t0, R), :] = du.astype(BF16)

    col = lambda off: pl.BlockSpec((S, LANES), lambda j: (0, off + j))
    wcol = lambda off: pl.BlockSpec((3, LANES), lambda j: (0, off + j))
    bcol = lambda off: pl.BlockSpec((1, LANES), lambda j: (0, off + j))
    return pl.pallas_call(
        body, name=name,
        out_shape=(jax.ShapeDtypeStruct((S, H), BF16), jax.ShapeDtypeStruct((S, H), BF16),
                   jax.ShapeDtypeStruct((3, H), F32), jax.ShapeDtypeStruct((3, H), F32),
                   jax.ShapeDtypeStruct((1, H), F32), jax.ShapeDtypeStruct((1, H), F32)),
        grid=(nb,),
        in_specs=[col(0), col(0), col(0), wcol(0), wcol(nb), bcol(0), bcol(nb)],
        out_specs=(col(0), col(0), wcol(0), wcol(0), bcol(0), bcol(0)),
        scratch_shapes=[pltpu.VMEM((S, LANES), F32), pltpu.VMEM((S, LANES), F32)],
        compiler_params=_cparams("parallel"),
    )(up_g, up_v, da, conv_w, conv_w, conv_b, conv_b)


def _local_step(x, mem, target, w_in, late_wire, P, core):
    mm = _matmul
    h1 = _rms_fwd(x, P["norm_mix_pre"], name="rms_mix_pre")
    proj = mm(h1, w_in, name="mm_in")
    (o_attn, sb_tot), late_wire = _sb_fwd(proj, name="sb_fwd", rider=_fill_xy(late_wire))

    ssm_prep = lambda *a: _ssm_prepare(*a)
    (lam_re, lam_im, bb_re, bb_im), prep_vjp = jax.vjp(
        ssm_prep, P["ssm_a_re"], P["ssm_a_im"], P["ssm_log_dt"], P["ssm_b_re"], P["ssm_b_im"])
    tab_f, tab_b = _ssm_tables(lam_re, lam_im)
    bd_re = _bd_from_bbar(bb_re).astype(BF16)
    bd_im = _bd_from_bbar(bb_im).astype(BF16)
    cd_re = _cd_from_c(P["ssm_c_re"]).astype(BF16)
    cd_imneg = _cd_from_c(-P["ssm_c_im"]).astype(BF16)
    (y_pre, x_re, x_im), late_wire = _ssm_fwd(proj, bd_re, bd_im, cd_re, cd_imneg, P["ssm_d"], tab_f,
                                              name="ssm_fwd", rider=_fill_c(late_wire))
    W = _weights_from_wire(dict(zip(LATE, late_wire)))
    W["w_in"] = w_in
    o_ssm = _glu_fwd(y_pre, W["ssm_w_glu"], P["ssm_b_glu"], name="glu_fwd")

    merged = _merge_fwd(proj, o_attn, o_ssm, W["w_branch_attn"], W["w_branch_ssm"], P["b_gate"], name="merge_fwd")
    mo = mm(merged, W["w_out"], name="mm_out")
    x1, h2 = _resnorm_norm(x, mo, P["norm_mix_post"], P["norm_xa_pre"], name="resnorm_1")

    mem_n = _rms_fwd(mem, P["norm_mem"], name="rms_mem")
    q2 = mm(h2, W["xa_wq"], out_dtype=BF16, name="mm_xq")
    k2 = mm(mem_n, W["xa_wk"], out_dtype=BF16, name="mm_xk")
    v2 = mm(mem_n, W["xa_wv"], out_dtype=BF16, name="mm_xv")
    o2 = _xattn_fwd(q2, k2, v2, name="xattn_fwd")
    xa = mm(o2, W["xa_wo"], name="mm_xo")
    x2, h3 = _resnorm_norm(x1, xa, P["norm_xa_post"], P["norm_ffn_pre"], name="resnorm_2")

    half = N_DEV // 2
    up_g = mm(h3, W["ffn_w_up"], n_blocks=half, name="mm_up_g")
    up_v = mm(h3, W["ffn_w_up"], b_block0=half, name="mm_up_v")
    act = _convgate_fwd(up_g, up_v, W["ffn_conv_w"], P["ffn_conv_b"], name="convgate_fwd")
    f = mm(act, W["ffn_w_down"], name="mm_down")
    loss, dy, df, dg_ffn_post = _final_loss(x2, f, P["norm_ffn_post"], target, name="final_loss")

    G = {"norm_ffn_post": dg_ffn_post}
    dact = mm(df, W["ffn_w_down"], tb=True, name="mm_down_dx")
    G["ffn_w_down"] = mm(act, df, ta=True, name="mm_down_dw")
    dug, duv, dwg, dwv, dbg, dbv = _convgate_bwd(up_g, up_v, dact, W["ffn_conv_w"], P["ffn_conv_b"], name="convgate_bwd")
    G["ffn_conv_w"] = jnp.concatenate([dwg, dwv], axis=1)
    G["ffn_conv_b"] = jnp.concatenate([dbg, dbv], axis=1)
    dh3 = mm(dug, W["ffn_w_up"], tb=True, n_blocks=half, name="mm_up_g_dx")
    dh3 = mm(duv, W["ffn_w_up"], tb=True, b_block0=half, acc_in=dh3, name="mm_up_v_dx")
    dw_up = mm(h3, dug, ta=True, out_into=lax.empty(W["ffn_w_up"].shape, F32), name="mm_up_g_dw")
    G["ffn_w_up"] = mm(h3, duv, ta=True, out_into=dw_up, out_block0=half, name="mm_up_v_dw")
    blocks = {n: _grad_blocks(n, G[n]) for n in REDUCE_FFN}
    (dx2, dxa, G["norm_ffn_pre"], G["norm_xa_post"]), from_core = _norm_bwd_pair(
        dy, dh3, x2, P["norm_ffn_pre"], xa, P["norm_xa_post"], name="norm_bwd_3",
        rider=_send_c([blocks[n] for n in REDUCE_FFN]))
    pair = {n: _pair_sum(blocks[n], r, core, name="pair_sum_" + n) for n, r in zip(REDUCE_FFN, from_core)}

    G["xa_wo"] = mm(o2, dxa, ta=True, name="mm_xo_dw")
    do2 = mm(dxa, W["xa_wo"], tb=True, out_dtype=BF16, name="mm_xo_dx")
    dq2, dk2, dv2 = _xattn_bwd(q2, k2, v2, do2, name="xattn_bwd")
    G["xa_wq"] = mm(h2, dq2, ta=True, name="mm_xq_dw")
    dh2 = mm(dq2, W["xa_wq"], tb=True, name="mm_xq_dx")
    G["xa_wk"] = mm(mem_n, dk2, ta=True, name="mm_xk_dw")
    G["xa_wv"] = mm(mem_n, dv2, ta=True, name="mm_xv_dw")
    dmem_n = jnp.concatenate([dk2, dv2], axis=1)
    wkv = jnp.concatenate([W["xa_wk"], W["xa_wv"]], axis=1)
    dmem = mm(dmem_n, wkv, tb=True, name="mm_xkv_dx")
    _, G["norm_mem"] = _norm_bwd_single(None, dmem, mem, P["norm_mem"], name="norm_bwd_mem")
    (dx1, dmo, G["norm_xa_pre"], G["norm_mix_post"]), _ = _norm_bwd_pair(
        dx2, dh2, x1, P["norm_xa_pre"], mo, P["norm_mix_post"], name="norm_bwd_2")

    G["w_out"] = mm(merged, dmo, ta=True, name="mm_out_dw")
    dmerged = mm(dmo, W["w_out"], tb=True, name="mm_out_dx")
    do_attn, do_ssm, dgate, G["b_gate"], G["w_branch_attn"], G["w_branch_ssm"] = _merge_bwd(
        dmerged, proj, o_attn, o_ssm, W["w_branch_attn"], W["w_branch_ssm"], P["b_gate"], name="merge_bwd")
    dy_pre, G["ssm_w_glu"], G["ssm_b_glu"] = _glu_bwd(y_pre, do_ssm, W["ssm_w_glu"], P["ssm_b_glu"], name="glu_bwd")
    blocks.update({n: _grad_blocks(n, G[n]) for n in REDUCE_MID})
    (du, dbd_re, dbd_im, dcd_re, dcd_imneg, G["ssm_d"], dl_re, dl_im), from_core = _ssm_bwd(
        dy_pre, proj, x_re, x_im, bd_re, bd_im, cd_re, cd_imneg, P["ssm_d"], tab_b, name="ssm_bwd",
        rider=_send_c([blocks[n] for n in REDUCE_MID]))
    pair.update({n: _pair_sum(blocks[n], r, core, name="pair_sum_" + n) for n, r in zip(REDUCE_MID, from_core)})
    G["ssm_c_re"] = _c_from_cd(dcd_re)
    G["ssm_c_im"] = -_c_from_cd(dcd_imneg)
    dlam_re = jnp.sum(dl_re, axis=1).reshape(SSM_GROUPS, SSM_STATE)
    dlam_im = jnp.sum(dl_im, axis=1).reshape(SSM_GROUPS, SSM_STATE)
    (G["ssm_a_re"], G["ssm_a_im"], G["ssm_log_dt"], G["ssm_b_re"], G["ssm_b_im"]) = prep_vjp(
        (dlam_re, dlam_im, _bbar_from_bd(dbd_re), _bbar_from_bd(dbd_im)))
    early = REDUCE_FFN + REDUCE_MID
    (dq, dk, dv), from_chips = _sb_bwd(proj, sb_tot, do_attn, name="sb_bwd",
                                       rider=_scatter_xy([pair[n] for n in early]))
    reduced = {n: (pair[n], parts) for n, parts in zip(early, from_chips)}
    dproj = jnp.concatenate([dq, dk, dv, du, dgate], axis=1)
    G["w_in"] = mm(h1, dproj, ta=True, out_cb=W["w_in"].shape[2], name="mm_in_dw")
    dh1 = mm(dproj, W["w_in"], tb=True, name="mm_in_dx")
    grad_x, G["norm_mix_pre"] = _norm_bwd_single(dx1, dh1, x, P["norm_mix_pre"], name="norm_bwd_1")
    g_in = _grad_blocks("w_in", G["w_in"])
    from_core, = _send_c([g_in]).run(name="reduce_in_c")
    pair_in = _pair_sum(g_in, from_core, core, name="pair_sum_w_in")
    from_chips, = _scatter_xy([pair_in]).run(name="reduce_in_xy")
    reduced["w_in"] = (pair_in, from_chips)
    return loss, grad_x, G, reduced


MESH = pl.DeviceIdType.MESH
_HBM = pl.BlockSpec(memory_space=pl.ANY)
N_XY = 4
N_XY_PEERS = 3


def _xy_peers(x, y):
    return [(1 - x, y), (x, 1 - y), (1 - x, 1 - y)]


class _Exchange:
    def __init__(self, arrays, out_shapes, plan, n_copies, alias):
        self.arrays = list(arrays)
        self.out_shapes = list(out_shapes)
        self.plan = plan
        self.n_copies = n_copies
        self.alias = alias

    @property
    def n(self):
        return len(self.arrays)

    def sems(self):
        shape = (self.n, self.n_copies)
        return [pltpu.SemaphoreType.DMA(shape), pltpu.SemaphoreType.DMA(shape)]

    def _copies(self, ins, outs, send_sems, recv_sems):
        x, y, c = lax.axis_index("x"), lax.axis_index("y"), lax.axis_index("c")
        sends, lands = [], []
        for k in range(self.n):
            for j, (src, dst, dev, land) in enumerate(self.plan(k, ins[k], outs[k], x, y, c)):
                sems = dict(send_sem=send_sems.at[k, j], recv_sem=recv_sems.at[k, j], device_id=dev, device_id_type=MESH)
                sends.append(pltpu.make_async_remote_copy(src_ref=src, dst_ref=dst, **sems))
                lands.append(pltpu.make_async_remote_copy(src_ref=src, dst_ref=land, **sems))
        return sends, lands

    def start(self, ins, outs, send_sems, recv_sems):
        for cp in self._copies(ins, outs, send_sems, recv_sems)[0]:
            cp.start()

    def finish(self, ins, outs, send_sems, recv_sems):
        sends, lands = self._copies(ins, outs, send_sems, recv_sems)
        for cp in lands:
            cp.wait_recv()
        for cp in sends:
            cp.wait_send()

    def run(self, *, name):
        n = self.n

        def body(*refs):
            parts = (refs[:n], refs[n:2 * n], refs[2 * n], refs[2 * n + 1])
            self.start(*parts)
            self.finish(*parts)

        return pl.pallas_call(
            body, name=name, out_shape=tuple(self.out_shapes),
            in_specs=[_HBM] * n, out_specs=tuple([_HBM] * n),
            input_output_aliases={k: k for k in range(n)} if self.alias else {},
            scratch_shapes=self.sems(),
        )(*self.arrays)


def _call(host_body, *, name, grid, in_specs, out_specs, out_shape, scratch_shapes, operands, rider=None):
    out_specs, out_shape = tuple(out_specs), tuple(out_shape)
    if rider is None:
        res = pl.pallas_call(
            host_body, name=name, grid=grid, in_specs=list(in_specs), out_specs=out_specs, out_shape=out_shape,
            scratch_shapes=list(scratch_shapes), compiler_params=_cparams(*["arbitrary"] * len(grid)),
        )(*operands)
        return tuple(res), None
    n, n_in, n_out, n_scr = rider.n, len(in_specs), len(out_specs), len(scratch_shapes)

    def body(*refs):
        pos = [0]

        def take(count):
            pos[0] += count
            return refs[pos[0] - count:pos[0]]

        h_in, r_in, h_out, r_out, h_scr = take(n_in), take(n), take(n_out), take(n), take(n_scr)
        send_sems, recv_sems = take(2)
        ids = [pl.program_id(a) for a in range(len(grid))]
        first = functools.reduce(jnp.logical_and, [i == 0 for i in ids])
        last = functools.reduce(jnp.logical_and, [i == g - 1 for i, g in zip(ids, grid)])

        @pl.when(first)
        def _():
            rider.start(r_in, r_out, send_sems, recv_sems)

        host_body(*h_in, *h_out, *h_scr)

        @pl.when(last)
        def _():
            rider.finish(r_in, r_out, send_sems, recv_sems)

    res = pl.pallas_call(
        body, name=name, grid=grid,
        in_specs=list(in_specs) + [_HBM] * n, out_specs=out_specs + tuple([_HBM] * n),
        out_shape=out_shape + tuple(rider.out_shapes),
        input_output_aliases={n_in + k: n_out + k for k in range(n)} if rider.alias else {},
        scratch_shapes=list(scratch_shapes) + rider.sems(),
        compiler_params=_cparams(*["arbitrary"] * len(grid)),
    )(*operands, *rider.arrays)
    return tuple(res[:n_out]), list(res[n_out:])


def _same(arrays):
    return [jax.ShapeDtypeStruct(a.shape, a.dtype) for a in arrays]


def _fill_xy(bufs):
    def plan(k, src, dst, x, y, c):
        mine = 2 * x + y
        return [(src.at[mine, c], dst.at[mine, c], (px, py, c), dst.at[2 * px + py, c]) for px, py in _xy_peers(x, y)]

    return _Exchange(bufs, _same(bufs), plan, N_XY_PEERS, alias=True)


def _fill_c(bufs):
    def plan(k, src, dst, x, y, c):
        return [(src.at[:, c], dst.at[:, c], (x, y, 1 - c), dst.at[:, 1 - c])]

    return _Exchange(bufs, _same(bufs), plan, 1, alias=True)


def _send_c(srcs):
    def plan(k, src, dst, x, y, c):
        return [(src.at[:, 1 - c], dst, (x, y, 1 - c), dst)]

    outs = [jax.ShapeDtypeStruct(a.shape[:1] + a.shape[2:], a.dtype) for a in srcs]
    return _Exchange(srcs, outs, plan, 1, alias=False)


def _scatter_xy(srcs):
    def plan(k, src, dst, x, y, c):
        return [(src.at[2 * px + py], dst.at[j], (px, py, c), dst.at[j]) for j, (px, py) in enumerate(_xy_peers(x, y))]

    outs = [jax.ShapeDtypeStruct((N_XY_PEERS,) + a.shape[1:], a.dtype) for a in srcs]
    return _Exchange(srcs, outs, plan, N_XY_PEERS, alias=False)


PACK_COLS = 1024
WIRE_DTYPE = BF16


def _pair_sum(g8, recv, core, *, name):
    n, _, R, C = g8.shape
    tr = _pick(R, (128, 64, 32, 16, 8))

    def body(core_ref, a_ref, b_ref, o_ref):
        o_ref[...] = (a_ref[0] + b_ref[...]).astype(WIRE_DTYPE)

    return pl.pallas_call(
        body, name=name, out_shape=jax.ShapeDtypeStruct((n, R, C), WIRE_DTYPE),
        grid_spec=pltpu.PrefetchScalarGridSpec(
            num_scalar_prefetch=1, grid=(n, R // tr),
            in_specs=[pl.BlockSpec((1, 1, tr, C), lambda s, i, core_ref: (s, core_ref[0], i, 0)),
                      pl.BlockSpec((1, tr, C), lambda s, i, core_ref: (s, i, 0))],
            out_specs=pl.BlockSpec((1, tr, C), lambda s, i, core_ref: (s, i, 0))),
        compiler_params=_cparams("parallel", "parallel"),
    )(core, g8, recv)


def _adamw_math(w, g, m, v):
    m = ADAM_B1 * m + (1.0 - ADAM_B1) * g
    v = ADAM_B2 * v + (1.0 - ADAM_B2) * (g * g)
    m_hat = m / (1.0 - ADAM_B1 ** ADAM_STEP)
    v_hat = v / (1.0 - ADAM_B2 ** ADAM_STEP)
    delta = -ADAM_LR * (m_hat / (jnp.sqrt(v_hat) + ADAM_EPS) + ADAM_WD * w)
    return delta, m, v


def _reduce_adamw(parts, w, m, v, *, own=None, own_slot=None, name):
    n, R, C = parts.shape
    tr = _pick(R, (128, 64, 32, 16, 8))
    has_own = own is not None

    def body(*refs):
        if has_own:
            _, own_ref, parts_ref, w_ref, m_ref, v_ref, g_ref, d_ref, nm_ref, nv_ref = refs
            g = own_ref[0].astype(F32)
            first = 0
        else:
            parts_ref, w_ref, m_ref, v_ref, g_ref, d_ref, nm_ref, nv_ref = refs
            g = parts_ref[0]
            first = 1
        for k in range(first, n):
            g = g + parts_ref[k].astype(F32)
        g_ref[...] = g
        d_ref[...], nm_ref[...], nv_ref[...] = _adamw_math(w_ref[...], g, m_ref[...], v_ref[...])

    out = jax.ShapeDtypeStruct((R, C), F32)
    if has_own:
        row = pl.BlockSpec((tr, C), lambda i, s: (i, 0))
        return pl.pallas_call(
            body, name=name, out_shape=(out, out, out, out),
            grid_spec=pltpu.PrefetchScalarGridSpec(
                num_scalar_prefetch=1, grid=(R // tr,),
                in_specs=[pl.BlockSpec((1, tr, C), lambda i, s: (s[0], i, 0)),
                          pl.BlockSpec((n, tr, C), lambda i, s: (0, i, 0)), row, row, row],
                out_specs=(row, row, row, row)),
            compiler_params=_cparams("parallel"),
        )(own_slot, own, parts, w, m, v)
    row = pl.BlockSpec((tr, C), lambda i: (i, 0))
    return pl.pallas_call(
        body, name=name, out_shape=(out, out, out, out), grid=(R // tr,),
        in_specs=[pl.BlockSpec((n, tr, C), lambda i: (0, i, 0)), row, row, row],
        out_specs=(row, row, row, row), compiler_params=_cparams("parallel"),
    )(parts, w, m, v)


SHARDED = (("w_in", (1024, 4096), 1), ("ssm_w_glu", (512, 512), 0), ("w_branch_attn", (512, 1024), 1),
           ("w_branch_ssm", (512, 1024), 1), ("w_out", (1024, 1024), 0), ("xa_wq", (1024, 1024), 0),
           ("xa_wk", (1024, 1024), 0), ("xa_wv", (1024, 1024), 0), ("xa_wo", (1024, 1024), 0),
           ("ffn_w_up", (1024, 5632), 1), ("ffn_conv_w", (3, 5632), 1), ("ffn_w_down", (2816, 1024), 0))
REPLICATED = (("norm_mix_pre", (1024,)), ("norm_mix_post", (1024,)), ("b_gate", (2048,)), ("ssm_a_re", (32, 64)),
              ("ssm_a_im", (32, 64)), ("ssm_log_dt", (32,)), ("ssm_b_re", (32, 64, 16)), ("ssm_b_im", (32, 64, 16)),
              ("ssm_c_re", (32, 16, 64)), ("ssm_c_im", (32, 16, 64)), ("ssm_d", (512,)), ("ssm_b_glu", (512,)),
              ("norm_xa_pre", (1024,)), ("norm_xa_post", (1024,)), ("norm_mem", (1024,)), ("norm_ffn_pre", (1024,)),
              ("norm_ffn_post", (1024,)), ("ffn_conv_b", (5632,)))
PARAM_ORDER = ("norm_mix_pre", "norm_mix_post", "w_in", "b_gate", "ssm_a_re", "ssm_a_im", "ssm_log_dt", "ssm_b_re",
               "ssm_b_im", "ssm_c_re", "ssm_c_im", "ssm_d", "ssm_w_glu", "ssm_b_glu", "w_branch_attn", "w_branch_ssm",
               "w_out", "norm_xa_pre", "norm_xa_post", "norm_mem", "xa_wq", "xa_wk", "xa_wv", "xa_wo", "norm_ffn_pre",
               "norm_ffn_post", "ffn_w_up", "ffn_conv_w", "ffn_conv_b", "ffn_w_down")
SMALL_ROWS = 160
FF_LOCAL = 2 * D_FF // N_DEV
FF_LOCAL_PAD = 768
FF_PAD = (N_DEV // 2) * FF_LOCAL_PAD


def _local_shape(shape, axis):
    return tuple(s // N_DEV if a == axis else s for a, s in enumerate(shape))


def _pad_cols(a, width):
    return jnp.pad(a, [(0, 0)] * (a.ndim - 1) + [(0, width - a.shape[-1])])


def _blocks_to_cols(a8):
    return a8.transpose(1, 0, 2).reshape(a8.shape[1], N_DEV * a8.shape[2])


def _cols_to_blocks(a, cb):
    return a.reshape(a.shape[0], N_DEV, cb).transpose(1, 0, 2)


FF_PADDED = ("ffn_w_up", "ffn_conv_w")
LATE = tuple(n for n, _, _ in SHARDED if n != "w_in")
REDUCE_FFN = ("ffn_w_up", "ffn_conv_w", "ffn_w_down")
REDUCE_MID = ("xa_wo", "xa_wq", "xa_wk", "xa_wv", "w_out", "w_branch_attn", "w_branch_ssm", "ssm_w_glu")
SHARD_AXIS = {n: ax for n, _, ax in SHARDED}
FULL_SHAPE = {n: s for n, s, _ in SHARDED}


def _as_local(n, a):
    return _pad_cols(a, FF_LOCAL_PAD) if n in FF_PADDED else a


def _weights_from_wire(wire):
    full = {n: b.reshape((N_DEV,) + b.shape[2:]) for n, b in wire.items()}
    W = {n: a.reshape(FULL_SHAPE[n]) if SHARD_AXIS[n] == 0 else a for n, a in full.items()}
    for n in ("w_branch_attn", "w_branch_ssm", "ffn_conv_w"):
        W[n] = _blocks_to_cols(full[n])
    W["ffn_w_down"] = jnp.pad(W["ffn_w_down"].reshape(N_DEV // 2, FF_LOCAL, D_MODEL),
                              ((0, 0), (0, FF_LOCAL_PAD - FF_LOCAL), (0, 0))).reshape(FF_PAD, D_MODEL)
    return W


def _grad_blocks(n, g):
    if n in ("w_branch_attn", "w_branch_ssm"):
        g = _cols_to_blocks(g, D_MODEL // N_DEV)
    elif n == "ffn_conv_w":
        g = _cols_to_blocks(g, FF_LOCAL_PAD)
    elif n == "ffn_w_down":
        g = g.reshape(N_DEV // 2, FF_LOCAL_PAD, D_MODEL)[:, :FF_LOCAL]
    local = _local_shape(FULL_SHAPE[n], SHARD_AXIS[n])
    if n in FF_PADDED:
        local = local[:-1] + (FF_LOCAL_PAD,)
    return g.reshape((N_XY, 2) + local)


def _pack_small(d):
    flat = jnp.concatenate([d[n].reshape(-1) for n, _ in REPLICATED])
    return _pad_cols(flat, SMALL_ROWS * PACK_COLS).reshape(SMALL_ROWS, PACK_COLS)


def _unpack_small(buf):
    flat = buf.reshape(-1)
    out, off = {}, 0
    for n, shape in REPLICATED:
        size = math.prod(shape)
        out[n] = flat[off:off + size]
        off += size
    return out


def kernel(x, mem, norm_mix_pre, norm_mix_post, w_in, b_gate, ssm_a_re, ssm_a_im, ssm_log_dt, ssm_b_re, ssm_b_im, ssm_c_re, ssm_c_im, ssm_d, ssm_w_glu, ssm_b_glu, w_branch_attn, w_branch_ssm, w_out, norm_xa_pre, norm_xa_post, norm_mem, xa_wq, xa_wk, xa_wv, xa_wo, norm_ffn_pre, norm_ffn_post, ffn_w_up, ffn_conv_w, ffn_conv_b, ffn_w_down, loss_target, m_norm_mix_pre, m_norm_mix_post, m_w_in, m_b_gate, m_ssm_a_re, m_ssm_a_im, m_ssm_log_dt, m_ssm_b_re, m_ssm_b_im, m_ssm_c_re, m_ssm_c_im, m_ssm_d, m_ssm_w_glu, m_ssm_b_glu, m_w_branch_attn, m_w_branch_ssm, m_w_out, m_norm_xa_pre, m_norm_xa_post, m_norm_mem, m_xa_wq, m_xa_wk, m_xa_wv, m_xa_wo, m_norm_ffn_pre, m_norm_ffn_post, m_ffn_w_up, m_ffn_conv_w, m_ffn_conv_b, m_ffn_w_down, v_norm_mix_pre, v_norm_mix_post, v_w_in, v_b_gate, v_ssm_a_re, v_ssm_a_im, v_ssm_log_dt, v_ssm_b_re, v_ssm_b_im, v_ssm_c_re, v_ssm_c_im, v_ssm_d, v_ssm_w_glu, v_ssm_b_glu, v_w_branch_attn, v_w_branch_ssm, v_w_out, v_norm_xa_pre, v_norm_xa_post, v_norm_mem, v_xa_wq, v_xa_wk, v_xa_wv, v_xa_wo, v_norm_ffn_pre, v_norm_ffn_post, v_ffn_w_up, v_ffn_conv_w, v_ffn_conv_b, v_ffn_w_down):
    args = dict(locals())
    w_loc = {n: args[n][0] for n in PARAM_ORDER}
    m_loc = {n: args["m_" + n][0] for n in PARAM_ORDER}
    v_loc = {n: args["v_" + n][0] for n in PARAM_ORDER}
    core_i = lax.axis_index("c")
    chip_i = 2 * lax.axis_index("x") + lax.axis_index("y")
    core = core_i.astype(jnp.int32).reshape(1)
    chip = chip_i.astype(jnp.int32).reshape(1)

    def in_place(a):
        buf = lax.empty((N_XY, 2) + a.shape, a.dtype)
        return lax.dynamic_update_slice(buf, a[None, None], (chip_i, core_i) + (0,) * a.ndim)

    as_wire = lambda n: in_place(_as_local(n, w_loc[n]).astype(F32 if n == "ffn_conv_w" else BF16))
    wire_in = _fill_c(_fill_xy([as_wire("w_in")]).run(name="gather_in_xy")).run(name="gather_in_c")[0]
    w_in_full = wire_in.reshape((N_DEV,) + wire_in.shape[2:])

    P = {}
    for n, shape in REPLICATED:
        P[n] = w_loc[n] if len(shape) > 1 or n == "ssm_log_dt" else w_loc[n].reshape(1, -1)
    P["ffn_conv_b"] = _pad_cols(w_loc["ffn_conv_b"].reshape(N_DEV, FF_LOCAL), FF_LOCAL_PAD).reshape(1, 2 * FF_PAD)

    loss, grad_x, G, reduced = _local_step(x[0], mem[0], loss_target[0], w_in_full, [as_wire(n) for n in LATE], P, core)
    loss = lax.psum(loss[0, 0], ("x", "y", "c"))

    big_out = {}
    for n, (own, parts) in reduced.items():
        res = _reduce_adamw(parts, _as_local(n, w_loc[n]), _as_local(n, m_loc[n]), _as_local(n, v_loc[n]),
                            own=own, own_slot=chip, name="adamw_" + n)
        big_out[n] = [r[:, :FF_LOCAL] if n in FF_PADDED else r for r in res]

    G["ffn_conv_b"] = G["ffn_conv_b"].reshape(N_DEV, FF_LOCAL_PAD)[:, :FF_LOCAL]
    parts, = _fill_c(_fill_xy([in_place(_pack_small(G))]).run(name="gather_g_xy")).run(name="gather_g_c")
    parts = parts.reshape((N_DEV,) + parts.shape[2:])
    small_out = _reduce_adamw(parts, _pack_small(w_loc), _pack_small(m_loc), _pack_small(v_loc), name="adamw_replicated")
    small_out = [_unpack_small(b) for b in small_out]

    outs = [loss, grad_x[None]]
    for k in range(4):
        for n in PARAM_ORDER:
            src = big_out[n][k] if n in big_out else small_out[k][n]
            outs.append(src.reshape(args[n].shape))
    return tuple(outs)
```

```python
import functools
import math

import jax
import jax.numpy as jnp
from jax import lax
from jax.experimental import pallas as pl
from jax.experimental.pallas import tpu as pltpu

F32 = jnp.float32
BF16 = jnp.bfloat16

D_MODEL = 1024
SB_HEADS = 8
SB_HEAD_DIM = 64
SB_WIDTH = 512
SSM_WIDTH = 512
SSM_GROUP = 16
SSM_GROUPS = 32
SSM_STATE = 64
XA_HEADS = 4
XA_HEAD_DIM = 256
D_FF = 2816
RMS_EPS = 1e-6
IN_WIDTH = 4096
N_DEV = 8

ADAM_LR = 0.001
ADAM_B1 = 0.9
ADAM_B2 = 0.999
ADAM_EPS = 1e-08
ADAM_WD = 0.01
ADAM_STEP = 10

LANES = 128
SUBLANES = 8
VMEM_LIMIT = 48 * 1024 * 1024

_GELU_C = math.sqrt(2.0 / math.pi)


def _cparams(*sem):
    return pltpu.CompilerParams(dimension_semantics=sem, vmem_limit_bytes=VMEM_LIMIT)


def _pick(n, cands):
    for c in cands:
        if n % c == 0:
            return c
    return n


def _gelu(x):
    return 0.5 * x * (1.0 + jnp.tanh(_GELU_C * (x + 0.044715 * x * x * x)))


def _gelu_and_grad(x):
    t = jnp.tanh(_GELU_C * (x + 0.044715 * x * x * x))
    g = 0.5 * x * (1.0 + t)
    dg = 0.5 * (1.0 + t) + 0.5 * x * (1.0 - t * t) * _GELU_C * (1.0 + 3.0 * 0.044715 * x * x)
    return g, dg


def _sigmoid(x):
    return 1.0 / (1.0 + jnp.exp(-x))


def _dot(a, b, ca, cb):
    return lax.dot_general(a.astype(BF16), b.astype(BF16), (((ca,), (cb,)), ((), ())),
                           preferred_element_type=F32)


MM_TILES = (1024, 768, 512, 256, 128)


def _matmul(a, b, *, ta=False, tb=False, out_dtype=F32, name, b_block0=0, n_blocks=None,
            out_cb=None, out_into=None, out_block0=0, acc_in=None):
    if ta:
        K, M = a.shape
    else:
        M, K = a.shape
    b_cb = None
    if b.ndim == 3:
        b_cb = b.shape[2]
        n_blocks = b.shape[0] - b_block0 if n_blocks is None else n_blocks
        N, K2 = (b.shape[1], n_blocks * b_cb) if tb else (n_blocks * b_cb, b.shape[1])
    elif tb:
        N, K2 = b.shape
    else:
        K2, N = b.shape
    assert K == K2, (a.shape, b.shape, ta, tb)
    if out_into is not None:
        out_cb = out_into.shape[2]
    tm = _pick(M, MM_TILES)
    n_unit = math.gcd(N, math.gcd(b_cb if (b_cb and not tb) else N, out_cb or N))
    tn = _pick(n_unit, MM_TILES)
    k_unit = b_cb if (b_cb and tb) else K
    tk = _pick(k_unit, MM_TILES)
    nk = K // tk
    ca, cb = (0 if ta else 1), (1 if tb else 0)
    has_acc = acc_in is not None
    has_into = out_into is not None

    def body(*refs):
        a_ref, b_ref = refs[0], refs[1]
        pos = 2
        c_ref = None
        if has_acc:
            c_ref = refs[pos]
            pos += 1
        if has_into:
            pos += 1
        o_ref = refs[pos]
        p = _dot(a_ref[...], b_ref[...], ca, cb)
        if nk == 1:
            o_ref[...] = ((p + c_ref[...]) if has_acc else p).astype(out_dtype)
        else:
            acc_ref = refs[pos + 1]
            k = pl.program_id(2)

            @pl.when(k == 0)
            def _():
                acc_ref[...] = (p + c_ref[...]) if has_acc else p

            @pl.when(k > 0)
            def _():
                acc_ref[...] += p

            @pl.when(k == nk - 1)
            def _():
                o_ref[...] = acc_ref[...].astype(out_dtype)

    a_spec = pl.BlockSpec((tk, tm), lambda j, i, k: (k, i)) if ta else pl.BlockSpec((tm, tk), lambda j, i, k: (i, k))
    if b_cb is None:
        b_spec = pl.BlockSpec((tn, tk), lambda j, i, k: (j, k)) if tb else pl.BlockSpec((tk, tn), lambda j, i, k: (k, j))
    elif tb:
        per = b_cb // tk
        b_spec = pl.BlockSpec((None, tn, tk), lambda j, i, k: (b_block0 + k // per, j, k % per))
    else:
        per = b_cb // tn
        b_spec = pl.BlockSpec((None, tk, tn), lambda j, i, k: (b_block0 + j // per, k, j % per))
    in_specs = [a_spec, b_spec]
    operands = [a, b]
    aliases = {}
    if has_acc:
        in_specs.append(pl.BlockSpec((tm, tn), lambda j, i, k: (i, j)))
        operands.append(acc_in)
    if has_into:
        aliases = {len(operands): 0}
        in_specs.append(pl.BlockSpec(memory_space=pl.ANY))
        operands.append(out_into)
    if out_cb is None:
        out_shape = jax.ShapeDtypeStruct((M, N), out_dtype)
        out_spec = pl.BlockSpec((tm, tn), lambda j, i, k: (i, j))
    else:
        per_o = out_cb // tn
        out_shape = (jax.ShapeDtypeStruct(out_into.shape, out_into.dtype) if has_into
                     else jax.ShapeDtypeStruct((N // out_cb, M, out_cb), out_dtype))
        out_spec = pl.BlockSpec((None, tm, tn), lambda j, i, k: (out_block0 + j // per_o, i, j % per_o))
    return pl.pallas_call(
        body, name=name, out_shape=out_shape,
        grid=(N // tn, M // tm, nk),
        in_specs=in_specs, out_specs=out_spec, input_output_aliases=aliases,
        scratch_shapes=[] if nk == 1 else [pltpu.VMEM((tm, tn), F32)],
        compiler_params=_cparams("parallel", "parallel", "arbitrary"),
    )(*operands)


def _rms(x, g):
    r = lax.rsqrt(jnp.mean(x * x, axis=-1, keepdims=True) + RMS_EPS)
    return x * r * g


def _rms_bwd(dy, x, g):
    r = lax.rsqrt(jnp.mean(x * x, axis=-1, keepdims=True) + RMS_EPS)
    xh = x * r
    dxh = dy * g
    dx = r * (dxh - xh * jnp.mean(dxh * xh, axis=-1, keepdims=True))
    dg = jnp.sum(dy * xh, axis=0, keepdims=True)
    return dx, dg


def _row_tile(rows):
    return _pick(rows, (512, 256, 128, 64, 32, 16, 8))


def _rms_fwd(x, g, *, name):
    R, D = x.shape
    tr = _row_tile(R)

    def body(x_ref, g_ref, h_ref):
        h_ref[...] = _rms(x_ref[...], g_ref[...]).astype(BF16)

    return pl.pallas_call(
        body, name=name, out_shape=jax.ShapeDtypeStruct((R, D), BF16), grid=(R // tr,),
        in_specs=[pl.BlockSpec((tr, D), lambda i: (i, 0)), pl.BlockSpec((1, D), lambda i: (0, 0))],
        out_specs=pl.BlockSpec((tr, D), lambda i: (i, 0)),
        compiler_params=_cparams("parallel"),
    )(x, g)


def _resnorm_norm(x, z, g_post, g_next, *, name):
    R, D = x.shape
    tr = _row_tile(R)

    def body(x_ref, z_ref, gp_ref, gn_ref, xn_ref, h_ref):
        xn = x_ref[...] + _rms(z_ref[...], gp_ref[...])
        xn_ref[...] = xn
        h_ref[...] = _rms(xn, gn_ref[...]).astype(BF16)

    row = pl.BlockSpec((tr, D), lambda i: (i, 0))
    vec = pl.BlockSpec((1, D), lambda i: (0, 0))
    return pl.pallas_call(
        body, name=name,
        out_shape=(jax.ShapeDtypeStruct((R, D), F32), jax.ShapeDtypeStruct((R, D), BF16)),
        grid=(R // tr,), in_specs=[row, row, vec, vec], out_specs=(row, row),
        compiler_params=_cparams("parallel"),
    )(x, z, g_post, g_next)


def _final_loss(x, z, g_post, target, *, name):
    R, D = x.shape
    tr = _row_tile(R)

    def body(x_ref, z_ref, gp_ref, t_ref, loss_ref, dy_ref, dz_ref, dg_ref):
        i = pl.program_id(0)
        z = z_ref[...]
        g = gp_ref[...]
        err = x_ref[...] + _rms(z, g) - t_ref[...]
        dy = err * (1.0 / D)
        dy_ref[...] = dy
        dz, dg = _rms_bwd(dy, z, g)
        dz_ref[...] = dz.astype(BF16)
        part = 0.5 * jnp.sum(jnp.sum(err * err, axis=-1, keepdims=True) * (1.0 / D), axis=0, keepdims=True)

        @pl.when(i == 0)
        def _():
            loss_ref[...] = part
            dg_ref[...] = dg

        @pl.when(i > 0)
        def _():
            loss_ref[...] += part
            dg_ref[...] += dg

    row = pl.BlockSpec((tr, D), lambda i: (i, 0))
    vec = pl.BlockSpec((1, D), lambda i: (0, 0))
    return pl.pallas_call(
        body, name=name,
        out_shape=(jax.ShapeDtypeStruct((1, 1), F32), jax.ShapeDtypeStruct((R, D), F32),
                   jax.ShapeDtypeStruct((R, D), BF16), jax.ShapeDtypeStruct((1, D), F32)),
        grid=(R // tr,), in_specs=[row, row, vec, row],
        out_specs=(pl.BlockSpec((1, 1), lambda i: (0, 0)), row, row, vec),
        compiler_params=_cparams("arbitrary"),
    )(x, z, g_post, target)


def _norm_bwd_pair(dres, dh, xk, g_pre, zprev, g_prev_post, *, name, rider=None):
    R, D = xk.shape
    tr = _row_tile(R)

    def body(dres_ref, dh_ref, x_ref, gpre_ref, z_ref, gpost_ref, dx_ref, dz_ref, dgpre_ref, dgpost_ref):
        i = pl.program_id(0)
        d1, dgpre = _rms_bwd(dh_ref[...], x_ref[...], gpre_ref[...])
        dx = dres_ref[...] + d1
        dx_ref[...] = dx
        dz, dgpost = _rms_bwd(dx, z_ref[...], gpost_ref[...])
        dz_ref[...] = dz.astype(BF16)

        @pl.when(i == 0)
        def _():
            dgpre_ref[...] = dgpre
            dgpost_ref[...] = dgpost

        @pl.when(i > 0)
        def _():
            dgpre_ref[...] += dgpre
            dgpost_ref[...] += dgpost

    row = pl.BlockSpec((tr, D), lambda i: (i, 0))
    vec = pl.BlockSpec((1, D), lambda i: (0, 0))
    return _call(
        body, name=name, rider=rider,
        out_shape=(jax.ShapeDtypeStruct((R, D), F32), jax.ShapeDtypeStruct((R, D), BF16),
                   jax.ShapeDtypeStruct((1, D), F32), jax.ShapeDtypeStruct((1, D), F32)),
        grid=(R // tr,), in_specs=[row, row, row, vec, row, vec], out_specs=(row, row, vec, vec),
        scratch_shapes=[], operands=(dres, dh, xk, g_pre, zprev, g_prev_post))


def _norm_bwd_single(dres, dh, xk, g_pre, *, name):
    R, D = xk.shape
    tr = _row_tile(R)
    has_res = dres is not None

    def body(*refs):
        if has_res:
            dres_ref, dh_ref, x_ref, gpre_ref, dx_ref, dgpre_ref = refs
        else:
            dh_ref, x_ref, gpre_ref, dx_ref, dgpre_ref = refs
        i = pl.program_id(0)
        d1, dgpre = _rms_bwd(dh_ref[...], x_ref[...], gpre_ref[...])
        dx_ref[...] = dres_ref[...] + d1 if has_res else d1

        @pl.when(i == 0)
        def _():
            dgpre_ref[...] = dgpre

        @pl.when(i > 0)
        def _():
            dgpre_ref[...] += dgpre

    row = pl.BlockSpec((tr, D), lambda i: (i, 0))
    vec = pl.BlockSpec((1, D), lambda i: (0, 0))
    ins = ([dres] if has_res else []) + [dh, xk, g_pre]
    return pl.pallas_call(
        body, name=name,
        out_shape=(jax.ShapeDtypeStruct((R, D), F32), jax.ShapeDtypeStruct((1, D), F32)),
        grid=(R // tr,), in_specs=([row] if has_res else []) + [row, row, vec], out_specs=(row, vec),
        compiler_params=_cparams("arbitrary"),
    )(*ins)


SB_BLOCK = 256
SB_QBLOCK = 1024
SB_DEAD = -104.0


def _sb_tri(kind):
    r = lax.broadcasted_iota(jnp.int32, (SB_BLOCK, SB_BLOCK), 0)
    c = lax.broadcasted_iota(jnp.int32, (SB_BLOCK, SB_BLOCK), 1)
    keep = {"after": r > c, "upto": r <= c, "before": r < c}[kind]
    return jnp.where(keep, 1.0, 0.0).astype(BF16)


def _running_sum(vals, tri):
    hi = vals.astype(BF16)
    lo = (vals - hi.astype(F32)).astype(BF16)
    return _dot(hi, tri, 1, 0) + _dot(lo, tri, 1, 0)


def _sb_scores(qm, k_blk):
    z = _dot(qm, k_blk, 1, 1)
    sp = jnp.maximum(z, 0.0) + jnp.log(1.0 + jnp.exp(-jnp.abs(z)))
    return z, sp


def _sb_causal(rows):
    r = lax.broadcasted_iota(jnp.int32, (rows, SB_BLOCK), 0)
    c = lax.broadcasted_iota(jnp.int32, (rows, SB_BLOCK), 1)
    return c < r


def _head_masks():
    lane = lax.broadcasted_iota(jnp.int32, (1, LANES), 1)
    return [jnp.where(lane < SB_HEAD_DIM, 1.0, 0.0), jnp.where(lane >= SB_HEAD_DIM, 1.0, 0.0)]


def _sb_fwd(proj, *, name, rider=None):
    S = proj.shape[0]
    T = SB_BLOCK
    TQ = min(SB_QBLOCK, S)
    span = TQ // T
    nq = S // TQ
    npair = SB_WIDTH // LANES
    scale = SB_HEAD_DIM ** -0.5

    def body(q_ref, k_ref, v_ref, o_ref, tot_ref, first_ref, acc_ref, run_ref):
        masks = _head_masks()
        tri = _sb_tri("after")
        first_ref[...] = jnp.zeros_like(first_ref)
        slot = lax.broadcasted_iota(jnp.int32, first_ref.shape, 1)

        def alive():
            reach = jnp.maximum(jnp.max(run_ref[0]), jnp.max(run_ref[1]))
            return (reach > SB_DEAD).astype(jnp.int32)

        def q_block(i, _):
            qrow = pl.ds(pl.multiple_of(i * TQ, TQ), TQ)
            q = q_ref[qrow, :] * scale
            qm = [(q * m).astype(BF16) for m in masks]
            acc_ref[...] = jnp.zeros_like(acc_ref)
            run_ref[...] = jnp.zeros_like(run_ref)

            def k_block(j, own):
                krow = pl.ds(pl.multiple_of(j * T, T), T)
                k_blk = k_ref[krow, :].astype(BF16)
                v_blk = v_ref[krow, :].astype(BF16)
                r0 = 0 if own is None else own * T
                rows = pl.ds(r0, TQ - r0)
                for h in range(2):
                    z, sp = _sb_scores(qm[h][r0:], k_blk)
                    causal = None if own is None else _sb_causal(TQ - r0)
                    lf = -sp if causal is None else jnp.where(causal, -sp, 0.0)
                    e = jnp.exp(z - sp + _running_sum(lf, tri) + run_ref[h, rows])
                    w = e if causal is None else jnp.where(causal, e, 0.0)
                    acc_ref[h, rows] += _dot(w, v_blk, 1, 0)
                    run_ref[h, rows] += jnp.sum(lf, axis=1, keepdims=True)

            for d in reversed(range(span)):
                k_block(i * span + d, d)

            def below(carry):
                jj, _ = carry
                k_block(i * span - 1 - jj, None)
                return jj + 1, alive()

            done, _ = lax.while_loop(lambda c: jnp.logical_and(c[0] < i * span, c[1] > 0), below, (jnp.int32(0), alive()))
            o_ref[qrow, :] = (acc_ref[0] * masks[0] + acc_ref[1] * masks[1]).astype(BF16)
            tot_ref[qrow, :] = run_ref[0] * masks[0] + run_ref[1] * masks[1]
            first_ref[...] = jnp.where(slot == i, (i * span - done).astype(F32), first_ref[...])
            return 0

        lax.fori_loop(0, nq, q_block, 0)

    blk = lambda off: pl.BlockSpec((S, LANES), lambda p: (0, off + p))
    return _call(
        body, name=name, rider=rider,
        out_shape=(jax.ShapeDtypeStruct((S, SB_WIDTH), BF16), jax.ShapeDtypeStruct((S, SB_WIDTH), F32),
                   jax.ShapeDtypeStruct((npair, SUBLANES, LANES), F32)),
        grid=(npair,),
        in_specs=[blk(0), blk(npair), blk(2 * npair)],
        out_specs=(blk(0), blk(0), pl.BlockSpec((1, SUBLANES, LANES), lambda p: (p, 0, 0))),
        scratch_shapes=[pltpu.VMEM((2, TQ, LANES), F32), pltpu.VMEM((2, TQ, 1), F32)],
        operands=(proj, proj, proj))


def _sb_bwd(proj, tot, first, do_attn, *, name, rider=None):
    S = proj.shape[0]
    T = SB_BLOCK
    TQ = min(SB_QBLOCK, S)
    span = TQ // T
    nq = S // TQ
    npair = SB_WIDTH // LANES
    scale = SB_HEAD_DIM ** -0.5

    def body(q_ref, k_ref, v_ref, tot_ref, first_ref, do_ref, dq_ref, dk_ref, dv_ref,
             dqacc_ref, dkacc_ref, dvacc_ref, run_ref, grun_ref):
        masks = _head_masks()
        tri_upto = _sb_tri("upto")
        tri_before = _sb_tri("before")
        dkacc_ref[...] = jnp.zeros_like(dkacc_ref)
        dvacc_ref[...] = jnp.zeros_like(dvacc_ref)
        slot = lax.broadcasted_iota(jnp.int32, first_ref.shape, 1)

        def q_block(i, _):
            qrow = pl.ds(pl.multiple_of(i * TQ, TQ), TQ)
            q = q_ref[qrow, :] * scale
            do = do_ref[qrow, :].astype(F32)
            tot = tot_ref[qrow, :]
            qm = [(q * m).astype(BF16) for m in masks]
            dom = [(do * m).astype(BF16) for m in masks]
            ltot = [jnp.sum(tot * m, axis=1, keepdims=True) * (1.0 / SB_HEAD_DIM) for m in masks]
            dqacc_ref[...] = jnp.zeros_like(dqacc_ref)
            run_ref[...] = jnp.zeros_like(run_ref)
            grun_ref[...] = jnp.zeros_like(grun_ref)

            def k_block(j, own):
                krow = pl.ds(pl.multiple_of(j * T, T), T)
                k_blk = k_ref[krow, :].astype(BF16)
                v_blk = v_ref[krow, :].astype(BF16)
                r0 = 0 if own is None else own * T
                rows = pl.ds(r0, TQ - r0)
                for h in range(2):
                    z, sp = _sb_scores(qm[h][r0:], k_blk)
                    causal = None if own is None else _sb_causal(TQ - r0)
                    lf = -sp if causal is None else jnp.where(causal, -sp, 0.0)
                    later = ltot[h][r0:] - run_ref[h, rows] - _running_sum(lf, tri_upto)
                    beta = jnp.exp(z - sp)
                    w = jnp.exp(z - sp + later)
                    if causal is not None:
                        w = jnp.where(causal, w, 0.0)
                    g = _dot(dom[h][r0:], v_blk, 1, 1) * w
                    gbefore = grun_ref[h, rows] + _dot(g, tri_before, 1, 0)
                    dz = g - beta * (g + gbefore)
                    if causal is not None:
                        dz = jnp.where(causal, dz, 0.0)
                    dz = dz.astype(BF16)
                    dqacc_ref[h, rows] += _dot(dz, k_blk, 1, 0)
                    dkacc_ref[krow, :] += _dot(dz, qm[h][r0:], 0, 0)
                    dvacc_ref[krow, :] += _dot(w, dom[h][r0:], 0, 0)
                    run_ref[h, rows] += jnp.sum(lf, axis=1, keepdims=True)
                    grun_ref[h, rows] += jnp.sum(g, axis=1, keepdims=True)

            def above(j, _):
                k_block(j, None)
                return 0

            first = jnp.max(jnp.where(slot == i, first_ref[...], 0.0)).astype(jnp.int32)
            lax.fori_loop(jnp.clip(first, 0, i * span), i * span, above, 0)
            for d in range(span):
                k_block(i * span + d, d)
            dq_ref[qrow, :] = ((dqacc_ref[0] * masks[0] + dqacc_ref[1] * masks[1]) * scale).astype(BF16)
            return 0

        lax.fori_loop(0, nq, q_block, 0)
        dk_ref[...] = dkacc_ref[...].astype(BF16)
        dv_ref[...] = dvacc_ref[...].astype(BF16)

    blk = lambda off: pl.BlockSpec((S, LANES), lambda p: (0, off + p))
    out = jax.ShapeDtypeStruct((S, SB_WIDTH), BF16)
    return _call(
        body, name=name, rider=rider, out_shape=(out, out, out), grid=(npair,),
        in_specs=[blk(0), blk(npair), blk(2 * npair), blk(0), pl.BlockSpec((1, SUBLANES, LANES), lambda p: (p, 0, 0)),
                  blk(0)],
        out_specs=(blk(0), blk(0), blk(0)),
        scratch_shapes=[pltpu.VMEM((2, TQ, LANES), F32), pltpu.VMEM((S, LANES), F32), pltpu.VMEM((S, LANES), F32),
                        pltpu.VMEM((2, TQ, 1), F32), pltpu.VMEM((2, TQ, 1), F32)],
        operands=(proj, proj, proj, tot, first, do_attn))


SSM_HALVES = 2
SSM_HALF_CH = SSM_WIDTH // SSM_HALVES
SSM_HALF_ST = SSM_GROUPS * SSM_STATE // SSM_HALVES
SSM_CHUNK = 512


def _cmul(ar, ai, br, bi):
    return ar * br - ai * bi, ar * bi + ai * br


def _ssm_tables(lam_re, lam_im):
    lr = lam_re.reshape(-1)
    li = lam_im.reshape(-1)
    pows = [(jnp.ones_like(lr), jnp.zeros_like(li)), (lr, li)]
    for _ in range(2, SUBLANES + 1):
        pows.append(_cmul(pows[-1][0], pows[-1][1], lr, li))
    row = jnp.arange(SUBLANES)[:, None]

    def shift_tab(d, keep):
        return [jnp.where(keep, pows[d][0][None, :], 0.0), jnp.where(keep, pows[d][1][None, :], 0.0)]

    fwd, bwd = [], []
    for d in (1, 2, 4):
        fwd += shift_tab(d, row >= d)
        bwd += shift_tab(d, row + d < SUBLANES)
    fwd += [jnp.stack([pows[r + 1][0] for r in range(SUBLANES)]), jnp.stack([pows[r + 1][1] for r in range(SUBLANES)])]
    bwd += [jnp.stack([pows[SUBLANES - r][0] for r in range(SUBLANES)]),
            jnp.stack([pows[SUBLANES - r][1] for r in range(SUBLANES)])]

    def halves(tabs):
        t = jnp.stack(tabs)
        return t.reshape(8, SUBLANES, SSM_HALVES, SSM_HALF_ST).transpose(2, 0, 1, 3)

    return halves(fwd), halves(bwd)


def _ssm_fwd(proj, bd_re, bd_im, cd_re, cd_imneg, d_skip, tab, *, name, rider=None):
    S = proj.shape[0]
    Tc = min(SSM_CHUNK, S)
    nc = S // Tc
    u_blk0 = (3 * SB_WIDTH) // SSM_HALF_CH

    def body(u_ref, bre_ref, bim_ref, cre_ref, cim_ref, d_ref, tab_ref, y_ref, xre_ref, xim_ref, cre_s, cim_s):
        c = pl.program_id(1)

        @pl.when(c == 0)
        def _():
            cre_s[...] = jnp.zeros_like(cre_s)
            cim_s[...] = jnp.zeros_like(cim_s)

        u = u_ref[...]
        ub = u.astype(BF16)
        xre_ref[...] = _dot(ub, bre_ref[0], 1, 0)
        xim_ref[...] = _dot(ub, bim_ref[0], 1, 0)

        def slab(k, carry):
            car_re, car_im = carry
            rows = pl.ds(pl.multiple_of(k * SUBLANES, SUBLANES), SUBLANES)
            sre = xre_ref[rows, :]
            sim = xim_ref[rows, :]
            for n, d in enumerate((1, 2, 4)):
                pre, pim = tab_ref[0, 2 * n], tab_ref[0, 2 * n + 1]
                rre = pltpu.roll(sre, d, 0)
                rim = pltpu.roll(sim, d, 0)
                sre, sim = sre + (pre * rre - pim * rim), sim + (pre * rim + pim * rre)
            pre, pim = tab_ref[0, 6], tab_ref[0, 7]
            sre, sim = sre + (pre * car_re - pim * car_im), sim + (pre * car_im + pim * car_re)
            xre_ref[rows, :] = sre
            xim_ref[rows, :] = sim
            last = (SUBLANES - 1, SUBLANES)
            return (jnp.broadcast_to(sre[last[0]:last[1], :], sre.shape),
                    jnp.broadcast_to(sim[last[0]:last[1], :], sim.shape))

        car = lax.fori_loop(0, Tc // SUBLANES, slab, (cre_s[...], cim_s[...]))
        cre_s[...] = car[0]
        cim_s[...] = car[1]
        y = _dot(xre_ref[...], cre_ref[0], 1, 0) + _dot(xim_ref[...], cim_ref[0], 1, 0)
        y_ref[...] = y + d_ref[...] * u

    return _call(
        body, name=name, rider=rider,
        out_shape=(jax.ShapeDtypeStruct((S, SSM_WIDTH), F32),
                   jax.ShapeDtypeStruct((S, SSM_HALVES * SSM_HALF_ST), F32),
                   jax.ShapeDtypeStruct((S, SSM_HALVES * SSM_HALF_ST), F32)),
        grid=(SSM_HALVES, nc),
        in_specs=[pl.BlockSpec((Tc, SSM_HALF_CH), lambda h, c: (c, u_blk0 + h)),
                  pl.BlockSpec((1, SSM_HALF_CH, SSM_HALF_ST), lambda h, c: (h, 0, 0)),
                  pl.BlockSpec((1, SSM_HALF_CH, SSM_HALF_ST), lambda h, c: (h, 0, 0)),
                  pl.BlockSpec((1, SSM_HALF_ST, SSM_HALF_CH), lambda h, c: (h, 0, 0)),
                  pl.BlockSpec((1, SSM_HALF_ST, SSM_HALF_CH), lambda h, c: (h, 0, 0)),
                  pl.BlockSpec((1, SSM_HALF_CH), lambda h, c: (0, h)),
                  pl.BlockSpec((1, 8, SUBLANES, SSM_HALF_ST), lambda h, c: (h, 0, 0, 0))],
        out_specs=(pl.BlockSpec((Tc, SSM_HALF_CH), lambda h, c: (c, h)),
                   pl.BlockSpec((Tc, SSM_HALF_ST), lambda h, c: (c, h)),
                   pl.BlockSpec((Tc, SSM_HALF_ST), lambda h, c: (c, h))),
        scratch_shapes=[pltpu.VMEM((SUBLANES, SSM_HALF_ST), F32), pltpu.VMEM((SUBLANES, SSM_HALF_ST), F32)],
        operands=(proj, bd_re, bd_im, cd_re, cd_imneg, d_skip, tab))


def _ssm_bwd(dy, proj, x_re, x_im, bd_re, bd_im, cd_re, cd_imneg, d_skip, tab, *, name, rider=None):
    S = proj.shape[0]
    Tc = min(SSM_CHUNK, S)
    nc = S // Tc
    u_blk0 = (3 * SB_WIDTH) // SSM_HALF_CH

    def body(dy_ref, u_ref, xre_ref, xim_ref, bre_ref, bim_ref, cre_ref, cim_ref, d_ref, tab_ref,
             du_ref, dbre_ref, dbim_ref, dcre_ref, dcim_ref, dd_ref, dlre_ref, dlim_ref,
             gre_s, gim_s, cre_s, cim_s):
        c = pl.program_id(1)

        @pl.when(c == 0)
        def _():
            cre_s[...] = jnp.zeros_like(cre_s)
            cim_s[...] = jnp.zeros_like(cim_s)
            dbre_ref[...] = jnp.zeros_like(dbre_ref)
            dbim_ref[...] = jnp.zeros_like(dbim_ref)
            dcre_ref[...] = jnp.zeros_like(dcre_ref)
            dcim_ref[...] = jnp.zeros_like(dcim_ref)
            dd_ref[...] = jnp.zeros_like(dd_ref)
            dlre_ref[...] = jnp.zeros_like(dlre_ref)
            dlim_ref[...] = jnp.zeros_like(dlim_ref)

        dy = dy_ref[...]
        dyb = dy.astype(BF16)
        u = u_ref[...]
        gre_s[...] = _dot(dyb, cre_ref[0], 1, 1)
        gim_s[...] = _dot(dyb, cim_ref[0], 1, 1)
        row = lax.broadcasted_iota(jnp.int32, (SUBLANES, SSM_HALF_ST), 0)
        nslab = Tc // SUBLANES

        def slab(kk, carry):
            car_re, car_im, acc_re, acc_im = carry
            k = nslab - 1 - kk
            rows = pl.ds(pl.multiple_of(k * SUBLANES, SUBLANES), SUBLANES)
            sre = gre_s[rows, :]
            sim = gim_s[rows, :]
            for n, d in enumerate((1, 2, 4)):
                pre, pim = tab_ref[0, 2 * n], tab_ref[0, 2 * n + 1]
                rre = pltpu.roll(sre, SUBLANES - d, 0)
                rim = pltpu.roll(sim, SUBLANES - d, 0)
                sre, sim = sre + (pre * rre + pim * rim), sim + (pre * rim - pim * rre)
            pre, pim = tab_ref[0, 6], tab_ref[0, 7]
            sre, sim = sre + (pre * car_re + pim * car_im), sim + (pre * car_im - pim * car_re)
            gre_s[rows, :] = sre
            gim_s[rows, :] = sim
            nre = jnp.where(row == SUBLANES - 1, car_re, pltpu.roll(sre, SUBLANES - 1, 0))
            nim = jnp.where(row == SUBLANES - 1, car_im, pltpu.roll(sim, SUBLANES - 1, 0))
            xr = xre_ref[rows, :]
            xi = xim_ref[rows, :]
            acc_re = acc_re + (nre * xr + nim * xi)
            acc_im = acc_im + (nim * xr - nre * xi)
            return (jnp.broadcast_to(sre[0:1, :], sre.shape), jnp.broadcast_to(sim[0:1, :], sim.shape), acc_re, acc_im)

        car = lax.fori_loop(0, nslab, slab, (cre_s[...], cim_s[...], dlre_ref[0], dlim_ref[0]))
        cre_s[...] = car[0]
        cim_s[...] = car[1]
        dlre_ref[0] = car[2]
        dlim_ref[0] = car[3]
        gre = gre_s[...].astype(BF16)
        gim = gim_s[...].astype(BF16)
        ub = u.astype(BF16)
        du = _dot(gre, bre_ref[0], 1, 1) + _dot(gim, bim_ref[0], 1, 1) + d_ref[...] * dy
        du_ref[...] = du.astype(BF16)
        dbre_ref[0] += _dot(ub, gre, 0, 0)
        dbim_ref[0] += _dot(ub, gim, 0, 0)
        dcre_ref[0] += _dot(xre_ref[...], dyb, 0, 0)
        dcim_ref[0] += _dot(xim_ref[...], dyb, 0, 0)
        dd_ref[...] += jnp.sum(dy * u, axis=0, keepdims=True)

    rev = lambda c: nc - 1 - c
    return _call(
        body, name=name, rider=rider,
        out_shape=(jax.ShapeDtypeStruct((S, SSM_WIDTH), BF16),
                   jax.ShapeDtypeStruct((SSM_HALVES, SSM_HALF_CH, SSM_HALF_ST), F32),
                   jax.ShapeDtypeStruct((SSM_HALVES, SSM_HALF_CH, SSM_HALF_ST), F32),
                   jax.ShapeDtypeStruct((SSM_HALVES, SSM_HALF_ST, SSM_HALF_CH), F32),
                   jax.ShapeDtypeStruct((SSM_HALVES, SSM_HALF_ST, SSM_HALF_CH), F32),
                   jax.ShapeDtypeStruct((1, SSM_WIDTH), F32),
                   jax.ShapeDtypeStruct((SSM_HALVES, SUBLANES, SSM_HALF_ST), F32),
                   jax.ShapeDtypeStruct((SSM_HALVES, SUBLANES, SSM_HALF_ST), F32)),
        grid=(SSM_HALVES, nc),
        in_specs=[pl.BlockSpec((Tc, SSM_HALF_CH), lambda h, c: (rev(c), h)),
                  pl.BlockSpec((Tc, SSM_HALF_CH), lambda h, c: (rev(c), u_blk0 + h)),
                  pl.BlockSpec((Tc, SSM_HALF_ST), lambda h, c: (rev(c), h)),
                  pl.BlockSpec((Tc, SSM_HALF_ST), lambda h, c: (rev(c), h)),
                  pl.BlockSpec((1, SSM_HALF_CH, SSM_HALF_ST), lambda h, c: (h, 0, 0)),
                  pl.BlockSpec((1, SSM_HALF_CH, SSM_HALF_ST), lambda h, c: (h, 0, 0)),
                  pl.BlockSpec((1, SSM_HALF_ST, SSM_HALF_CH), lambda h, c: (h, 0, 0)),
                  pl.BlockSpec((1, SSM_HALF_ST, SSM_HALF_CH), lambda h, c: (h, 0, 0)),
                  pl.BlockSpec((1, SSM_HALF_CH), lambda h, c: (0, h)),
                  pl.BlockSpec((1, 8, SUBLANES, SSM_HALF_ST), lambda h, c: (h, 0, 0, 0))],
        out_specs=(pl.BlockSpec((Tc, SSM_HALF_CH), lambda h, c: (rev(c), h)),
                   pl.BlockSpec((1, SSM_HALF_CH, SSM_HALF_ST), lambda h, c: (h, 0, 0)),
                   pl.BlockSpec((1, SSM_HALF_CH, SSM_HALF_ST), lambda h, c: (h, 0, 0)),
                   pl.BlockSpec((1, SSM_HALF_ST, SSM_HALF_CH), lambda h, c: (h, 0, 0)),
                   pl.BlockSpec((1, SSM_HALF_ST, SSM_HALF_CH), lambda h, c: (h, 0, 0)),
                   pl.BlockSpec((1, SSM_HALF_CH), lambda h, c: (0, h)),
                   pl.BlockSpec((1, SUBLANES, SSM_HALF_ST), lambda h, c: (h, 0, 0)),
                   pl.BlockSpec((1, SUBLANES, SSM_HALF_ST), lambda h, c: (h, 0, 0))),
        scratch_shapes=[pltpu.VMEM((Tc, SSM_HALF_ST), F32), pltpu.VMEM((Tc, SSM_HALF_ST), F32),
                        pltpu.VMEM((SUBLANES, SSM_HALF_ST), F32), pltpu.VMEM((SUBLANES, SSM_HALF_ST), F32)],
        operands=(dy, proj, x_re, x_im, bd_re, bd_im, cd_re, cd_imneg, d_skip, tab))


def _ssm_prepare(a_re, a_im, log_dt, b_re, b_im):
    dt = jnp.exp(log_dt)[:, None]
    mag = jnp.exp(a_re * dt)
    lre = mag * jnp.cos(a_im * dt)
    lim = mag * jnp.sin(a_im * dt)
    den = a_re * a_re + a_im * a_im
    fre = ((lre - 1.0) * a_re + lim * a_im) / den
    fim = (lim * a_re - (lre - 1.0) * a_im) / den
    bbre = fre[:, :, None] * b_re - fim[:, :, None] * b_im
    bbim = fre[:, :, None] * b_im + fim[:, :, None] * b_re
    return lre, lim, bbre, bbim


def _group_eye():
    return jnp.eye(SSM_GROUPS // SSM_HALVES, dtype=F32)


def _bd_from_bbar(bbar):
    gh = SSM_GROUPS // SSM_HALVES
    b = bbar.reshape(SSM_HALVES, gh, SSM_STATE, SSM_GROUP).transpose(0, 1, 3, 2)
    out = b[:, :, :, None, :] * _group_eye()[None, :, None, :, None]
    return out.reshape(SSM_HALVES, SSM_HALF_CH, SSM_HALF_ST)


def _bbar_from_bd(dbd):
    gh = SSM_GROUPS // SSM_HALVES
    d = dbd.reshape(SSM_HALVES, gh, SSM_GROUP, gh, SSM_STATE)
    d = jnp.sum(d * _group_eye()[None, :, None, :, None], axis=3)
    return d.transpose(0, 1, 3, 2).reshape(SSM_GROUPS, SSM_STATE, SSM_GROUP)


def _cd_from_c(cmat):
    gh = SSM_GROUPS // SSM_HALVES
    c = cmat.reshape(SSM_HALVES, gh, SSM_GROUP, SSM_STATE).transpose(0, 1, 3, 2)
    out = c[:, :, :, None, :] * _group_eye()[None, :, None, :, None]
    return out.reshape(SSM_HALVES, SSM_HALF_ST, SSM_HALF_CH)


def _c_from_cd(dcd):
    gh = SSM_GROUPS // SSM_HALVES
    d = dcd.reshape(SSM_HALVES, gh, SSM_STATE, gh, SSM_GROUP)
    d = jnp.sum(d * _group_eye()[None, :, None, :, None], axis=3)
    return d.transpose(0, 1, 3, 2).reshape(SSM_GROUPS, SSM_GROUP, SSM_STATE)


def _glu_fwd(y_pre, w_glu, b_glu, *, name):
    S, W = y_pre.shape
    tr = _row_tile(S)

    def body(y_ref, w_ref, b_ref, o_ref):
        yg = _gelu(y_ref[...])
        gl = _dot(yg, w_ref[...], 1, 0) + b_ref[...]
        o_ref[...] = (yg * _sigmoid(gl)).astype(BF16)

    row = pl.BlockSpec((tr, W), lambda i: (i, 0))
    return pl.pallas_call(
        body, name=name, out_shape=jax.ShapeDtypeStruct((S, W), BF16), grid=(S // tr,),
        in_specs=[row, pl.BlockSpec((W, W), lambda i: (0, 0)), pl.BlockSpec((1, W), lambda i: (0, 0))],
        out_specs=row, compiler_params=_cparams("parallel"),
    )(y_pre, w_glu, b_glu)


def _glu_bwd(y_pre, do, w_glu, b_glu, *, name):
    S, W = y_pre.shape
    tr = _row_tile(S)

    def body(y_ref, do_ref, w_ref, b_ref, dy_ref, dw_ref, db_ref):
        i = pl.program_id(0)
        yg, dyg_dy = _gelu_and_grad(y_ref[...])
        ygb = yg.astype(BF16)
        sg = _sigmoid(_dot(ygb, w_ref[...], 1, 0) + b_ref[...])
        do = do_ref[...]
        dgl = do * yg * sg * (1.0 - sg)
        dglb = dgl.astype(BF16)
        dyg = do * sg + _dot(dglb, w_ref[...], 1, 1)
        dy_ref[...] = dyg * dyg_dy
        dw = _dot(ygb, dglb, 0, 0)
        db = jnp.sum(dgl, axis=0, keepdims=True)

        @pl.when(i == 0)
        def _():
            dw_ref[...] = dw
            db_ref[...] = db

        @pl.when(i > 0)
        def _():
            dw_ref[...] += dw
            db_ref[...] += db

    row = pl.BlockSpec((tr, W), lambda i: (i, 0))
    full = pl.BlockSpec((W, W), lambda i: (0, 0))
    vec = pl.BlockSpec((1, W), lambda i: (0, 0))
    return pl.pallas_call(
        body, name=name,
        out_shape=(jax.ShapeDtypeStruct((S, W), F32), jax.ShapeDtypeStruct((W, W), F32), jax.ShapeDtypeStruct((1, W), F32)),
        grid=(S // tr,), in_specs=[row, row, full, vec], out_specs=(row, full, vec),
        compiler_params=_cparams("arbitrary"),
    )(y_pre, do, w_glu, b_glu)


GATE_COL0 = 3 * SB_WIDTH + SSM_WIDTH


def _merge_fwd(proj, o_attn, o_ssm, w_ba, w_bs, b_gate, *, name):
    S = proj.shape[0]
    D = D_MODEL
    tr = _pick(S, (256, 128, 64, 32, 16, 8))
    gb = GATE_COL0 // D

    def body(ga_ref, gs_ref, oa_ref, os_ref, wa_ref, ws_ref, ba_ref, bs_ref, m_ref):
        pa = _dot(oa_ref[...], wa_ref[...], 1, 0)
        ps = _dot(os_ref[...], ws_ref[...], 1, 0)
        sa = _sigmoid(ga_ref[...] + ba_ref[...])
        ss = _sigmoid(gs_ref[...] + bs_ref[...])
        m_ref[...] = (sa * pa + ss * ps).astype(BF16)

    return pl.pallas_call(
        body, name=name, out_shape=jax.ShapeDtypeStruct((S, D), BF16), grid=(S // tr,),
        in_specs=[pl.BlockSpec((tr, D), lambda i: (i, gb)), pl.BlockSpec((tr, D), lambda i: (i, gb + 1)),
                  pl.BlockSpec((tr, SB_WIDTH), lambda i: (i, 0)), pl.BlockSpec((tr, SSM_WIDTH), lambda i: (i, 0)),
                  pl.BlockSpec((SB_WIDTH, D), lambda i: (0, 0)), pl.BlockSpec((SSM_WIDTH, D), lambda i: (0, 0)),
                  pl.BlockSpec((1, D), lambda i: (0, 0)), pl.BlockSpec((1, D), lambda i: (0, 1))],
        out_specs=pl.BlockSpec((tr, D), lambda i: (i, 0)),
        compiler_params=_cparams("parallel"),
    )(proj, proj, o_attn, o_ssm, w_ba, w_bs, b_gate, b_gate)


def _merge_bwd(dmerged, proj, o_attn, o_ssm, w_ba, w_bs, b_gate, *, name):
    S = proj.shape[0]
    D = D_MODEL
    tr = _pick(S, (256, 128, 64, 32, 16, 8))
    gb = GATE_COL0 // D

    def body(dm_ref, ga_ref, gs_ref, oa_ref, os_ref, wa_ref, ws_ref, ba_ref, bs_ref,
             doa_ref, dos_ref, dg_ref, db_ref, dwa_ref, dws_ref):
        i = pl.program_id(0)
        dm = dm_ref[...]
        oa = oa_ref[...]
        osm = os_ref[...]
        pa = _dot(oa, wa_ref[...], 1, 0)
        ps = _dot(osm, ws_ref[...], 1, 0)
        sa = _sigmoid(ga_ref[...] + ba_ref[...])
        ss = _sigmoid(gs_ref[...] + bs_ref[...])
        dpa = (dm * sa).astype(BF16)
        dps = (dm * ss).astype(BF16)
        dga = dm * pa * sa * (1.0 - sa)
        dgs = dm * ps * ss * (1.0 - ss)
        dg_ref[:, :D] = dga.astype(BF16)
        dg_ref[:, D:] = dgs.astype(BF16)
        doa_ref[...] = _dot(dpa, wa_ref[...], 1, 1).astype(BF16)
        dos_ref[...] = _dot(dps, ws_ref[...], 1, 1)
        dwa = _dot(oa, dpa, 0, 0)
        dws = _dot(osm, dps, 0, 0)
        dba = jnp.sum(dga, axis=0, keepdims=True)
        dbs = jnp.sum(dgs, axis=0, keepdims=True)

        @pl.when(i == 0)
        def _():
            dwa_ref[...] = dwa
            dws_ref[...] = dws
            db_ref[:, :D] = dba
            db_ref[:, D:] = dbs

        @pl.when(i > 0)
        def _():
            dwa_ref[...] += dwa
            dws_ref[...] += dws
            db_ref[:, :D] += dba
            db_ref[:, D:] += dbs

    rowD = pl.BlockSpec((tr, D), lambda i: (i, 0))
    wspec = pl.BlockSpec((SB_WIDTH, D), lambda i: (0, 0))
    return pl.pallas_call(
        body, name=name,
        out_shape=(jax.ShapeDtypeStruct((S, SB_WIDTH), BF16), jax.ShapeDtypeStruct((S, SSM_WIDTH), F32),
                   jax.ShapeDtypeStruct((S, 2 * D), BF16), jax.ShapeDtypeStruct((1, 2 * D), F32),
                   jax.ShapeDtypeStruct((SB_WIDTH, D), F32), jax.ShapeDtypeStruct((SSM_WIDTH, D), F32)),
        grid=(S // tr,),
        in_specs=[rowD, pl.BlockSpec((tr, D), lambda i: (i, gb)), pl.BlockSpec((tr, D), lambda i: (i, gb + 1)),
                  pl.BlockSpec((tr, SB_WIDTH), lambda i: (i, 0)), pl.BlockSpec((tr, SSM_WIDTH), lambda i: (i, 0)),
                  wspec, wspec, pl.BlockSpec((1, D), lambda i: (0, 0)), pl.BlockSpec((1, D), lambda i: (0, 1))],
        out_specs=(pl.BlockSpec((tr, SB_WIDTH), lambda i: (i, 0)), pl.BlockSpec((tr, SSM_WIDTH), lambda i: (i, 0)),
                   pl.BlockSpec((tr, 2 * D), lambda i: (i, 0)), pl.BlockSpec((1, 2 * D), lambda i: (0, 0)),
                   wspec, wspec),
        compiler_params=_cparams("arbitrary"),
    )(dmerged, proj, proj, o_attn, o_ssm, w_ba, w_bs, b_gate, b_gate)


def _xattn_probs(q, k, h):
    cols = slice(h * XA_HEAD_DIM, (h + 1) * XA_HEAD_DIM)
    s = _dot(q[:, cols], k[:, cols], 1, 1) * (XA_HEAD_DIM ** -0.5)
    s = s - jnp.max(s, axis=-1, keepdims=True)
    e = jnp.exp(s)
    return e / jnp.sum(e, axis=-1, keepdims=True), cols


def _xattn_fwd(q2, k2, v2, *, name):
    S, D = q2.shape
    M = k2.shape[0]
    tr = _row_tile(S)

    def body(q_ref, k_ref, v_ref, o_ref):
        q = q_ref[...]
        k = k_ref[...]
        v = v_ref[...]
        for h in range(XA_HEADS):
            p, cols = _xattn_probs(q, k, h)
            o_ref[:, cols] = _dot(p, v[:, cols], 1, 0).astype(BF16)

    row = pl.BlockSpec((tr, D), lambda i: (i, 0))
    memb = pl.BlockSpec((M, D), lambda i: (0, 0))
    return pl.pallas_call(
        body, name=name, out_shape=jax.ShapeDtypeStruct((S, D), BF16), grid=(S // tr,),
        in_specs=[row, memb, memb], out_specs=row, compiler_params=_cparams("parallel"),
    )(q2, k2, v2)


def _xattn_bwd(q2, k2, v2, do2, *, name):
    S, D = q2.shape
    M = k2.shape[0]
    tr = _row_tile(S)
    scale = XA_HEAD_DIM ** -0.5

    def body(q_ref, k_ref, v_ref, do_ref, dq_ref, dk_ref, dv_ref):
        i = pl.program_id(0)

        @pl.when(i == 0)
        def _():
            dk_ref[...] = jnp.zeros_like(dk_ref)
            dv_ref[...] = jnp.zeros_like(dv_ref)

        q = q_ref[...]
        k = k_ref[...]
        v = v_ref[...]
        do = do_ref[...]
        for h in range(XA_HEADS):
            p, cols = _xattn_probs(q, k, h)
            dp = _dot(do[:, cols], v[:, cols], 1, 1)
            ds = (p * (dp - jnp.sum(dp * p, axis=-1, keepdims=True)) * scale).astype(BF16)
            dq_ref[:, cols] = _dot(ds, k[:, cols], 1, 0).astype(BF16)
            dk_ref[:, cols] += _dot(ds, q[:, cols], 0, 0)
            dv_ref[:, cols] += _dot(p, do[:, cols], 0, 0)

    row = pl.BlockSpec((tr, D), lambda i: (i, 0))
    memb = pl.BlockSpec((M, D), lambda i: (0, 0))
    return pl.pallas_call(
        body, name=name,
        out_shape=(jax.ShapeDtypeStruct((S, D), BF16), jax.ShapeDtypeStruct((M, D), F32), jax.ShapeDtypeStruct((M, D), F32)),
        grid=(S // tr,), in_specs=[row, memb, memb, row], out_specs=(row, memb, memb),
        compiler_params=_cparams("arbitrary"),
    )(q2, k2, v2, do2)


CONV_ROWS = 512


def _shift_down(ref, t0, rows, d):
    cur = ref[pl.ds(t0, rows), :]
    out = pltpu.roll(cur, d, 0)
    r = lax.broadcasted_iota(jnp.int32, cur.shape, 0)
    for e in range(d):
        src = t0 - d + e
        prev = ref[pl.ds(src, 1), :] if src >= 0 else jnp.zeros((1, cur.shape[1]), cur.dtype)
        out = jnp.where(r == e, prev, out)
    return out


def _shift_up(ref, t0, rows, d, total):
    cur = ref[pl.ds(t0, rows), :]
    out = pltpu.roll(cur, rows - d, 0)
    r = lax.broadcasted_iota(jnp.int32, cur.shape, 0)
    for e in range(d):
        src = t0 + rows + e
        nxt = ref[pl.ds(src, 1), :] if src < total else jnp.zeros((1, cur.shape[1]), cur.dtype)
        out = jnp.where(r == rows - d + e, nxt, out)
    return out


def _conv3(ref, w_ref, b_ref, t0, rows):
    return (w_ref[2:3, :] * ref[pl.ds(t0, rows), :] + w_ref[1:2, :] * _shift_down(ref, t0, rows, 1)
            + w_ref[0:1, :] * _shift_down(ref, t0, rows, 2) + b_ref[...])


def _convgate_fwd(up_g, up_v, conv_w, conv_b, *, name):
    S, H = up_g.shape
    nb = H // LANES
    R = min(CONV_ROWS, S)

    def body(g_ref, v_ref, wg_ref, wv_ref, bg_ref, bv_ref, a_ref):
        for t0 in range(0, S, R):
            cg = _conv3(g_ref, wg_ref, bg_ref, t0, R)
            cv = _conv3(v_ref, wv_ref, bv_ref, t0, R)
            a_ref[pl.ds(t0, R), :] = (_gelu(cg) * cv).astype(BF16)

    col = lambda off: pl.BlockSpec((S, LANES), lambda j: (0, off + j))
    wcol = lambda off: pl.BlockSpec((3, LANES), lambda j: (0, off + j))
    bcol = lambda off: pl.BlockSpec((1, LANES), lambda j: (0, off + j))
    return pl.pallas_call(
        body, name=name, out_shape=jax.ShapeDtypeStruct((S, H), BF16), grid=(nb,),
        in_specs=[col(0), col(0), wcol(0), wcol(nb), bcol(0), bcol(nb)],
        out_specs=col(0), compiler_params=_cparams("parallel"),
    )(up_g, up_v, conv_w, conv_w, conv_b, conv_b)


def _convgate_bwd(up_g, up_v, da, conv_w, conv_b, *, name):
    S, H = up_g.shape
    nb = H // LANES
    R = min(CONV_ROWS, S)

    def body(g_ref, v_ref, da_ref, wg_ref, wv_ref, bg_ref, bv_ref,
             dug_ref, duv_ref, dwg_ref, dwv_ref, dbg_ref, dbv_ref, dcg_s, dcv_s):
        zero3 = jnp.zeros((1, LANES), F32)
        acc = {"g": [zero3, zero3, zero3, zero3], "v": [zero3, zero3, zero3, zero3]}
        for t0 in range(0, S, R):
            cg = _conv3(g_ref, wg_ref, bg_ref, t0, R)
            cv = _conv3(v_ref, wv_ref, bv_ref, t0, R)
            da = da_ref[pl.ds(t0, R), :]
            gl, dgl = _gelu_and_grad(cg)
            dcg = da * cv * dgl
            dcv = da * gl
            dcg_s[pl.ds(t0, R), :] = dcg
            dcv_s[pl.ds(t0, R), :] = dcv
            for key, ref, dc in (("g", g_ref, dcg), ("v", v_ref, dcv)):
                a = acc[key]
                a[2] = a[2] + jnp.sum(dc * ref[pl.ds(t0, R), :], axis=0, keepdims=True)
                a[1] = a[1] + jnp.sum(dc * _shift_down(ref, t0, R, 1), axis=0, keepdims=True)
                a[0] = a[0] + jnp.sum(dc * _shift_down(ref, t0, R, 2), axis=0, keepdims=True)
                a[3] = a[3] + jnp.sum(dc, axis=0, keepdims=True)
        for key, dw_ref, db_ref in (("g", dwg_ref, dbg_ref), ("v", dwv_ref, dbv_ref)):
            a = acc[key]
            dw_ref[0:1, :] = a[0]
            dw_ref[1:2, :] = a[1]
            dw_ref[2:3, :] = a[2]
            db_ref[...] = a[3]
        for t0 in range(0, S, R):
            for dc_s, w_ref, du_ref in ((dcg_s, wg_ref, dug_ref), (dcv_s, wv_ref, duv_ref)):
                du = (w_ref[2:3, :] * dc_s[pl.ds(t0, R), :] + w_ref[1:2, :] * _shift_up(dc_s, t0, R, 1, S)
                      + w_ref[0:1, :] * _shift_up(dc_s, t0, R, 2, S))
                du_ref[pl.ds(t0, R), :] = du.astype(BF16)

    col = lambda off: pl.BlockSpec((S, LANES), lambda j: (0, off + j))
    wcol = lambda off: pl.BlockSpec((3, LANES), lambda j: (0, off + j))
    bcol = lambda off: pl.BlockSpec((1, LANES), lambda j: (0, off + j))
    return pl.pallas_call(
        body, name=name,
        out_shape=(jax.ShapeDtypeStruct((S, H), BF16), jax.ShapeDtypeStruct((S, H), BF16),
                   jax.ShapeDtypeStruct((3, H), F32), jax.ShapeDtypeStruct((3, H), F32),
                   jax.ShapeDtypeStruct((1, H), F32), jax.ShapeDtypeStruct((1, H), F32)),
        grid=(nb,),
        in_specs=[col(0), col(0), col(0), wcol(0), wcol(nb), bcol(0), bcol(nb)],
        out_specs=(col(0), col(0), wcol(0), wcol(0), bcol(0), bcol(0)),
        scratch_shapes=[pltpu.VMEM((S, LANES), F32), pltpu.VMEM((S, LANES), F32)],
        compiler_params=_cparams("parallel"),
    )(up_g, up_v, da, conv_w, conv_w, conv_b, conv_b)


def _local_step(x, mem, target, w_in, late_wire, P, core):
    mm = _matmul
    h1 = _rms_fwd(x, P["norm_mix_pre"], name="rms_mix_pre")
    proj = mm(h1, w_in, name="mm_in")
    (o_attn, sb_tot, sb_first), late_wire = _sb_fwd(proj, name="sb_fwd", rider=_fill_xy(late_wire))

    ssm_prep = lambda *a: _ssm_prepare(*a)
    (lam_re, lam_im, bb_re, bb_im), prep_vjp = jax.vjp(
        ssm_prep, P["ssm_a_re"], P["ssm_a_im"], P["ssm_log_dt"], P["ssm_b_re"], P["ssm_b_im"])
    tab_f, tab_b = _ssm_tables(lam_re, lam_im)
    bd_re = _bd_from_bbar(bb_re).astype(BF16)
    bd_im = _bd_from_bbar(bb_im).astype(BF16)
    cd_re = _cd_from_c(P["ssm_c_re"]).astype(BF16)
    cd_imneg = _cd_from_c(-P["ssm_c_im"]).astype(BF16)
    (y_pre, x_re, x_im), late_wire = _ssm_fwd(proj, bd_re, bd_im, cd_re, cd_imneg, P["ssm_d"], tab_f,
                                              name="ssm_fwd", rider=_fill_c(late_wire))
    W = _weights_from_wire(dict(zip(LATE, late_wire)))
    W["w_in"] = w_in
    o_ssm = _glu_fwd(y_pre, W["ssm_w_glu"], P["ssm_b_glu"], name="glu_fwd")

    merged = _merge_fwd(proj, o_attn, o_ssm, W["w_branch_attn"], W["w_branch_ssm"], P["b_gate"], name="merge_fwd")
    mo = mm(merged, W["w_out"], name="mm_out")
    x1, h2 = _resnorm_norm(x, mo, P["norm_mix_post"], P["norm_xa_pre"], name="resnorm_1")

    mem_n = _rms_fwd(mem, P["norm_mem"], name="rms_mem")
    q2 = mm(h2, W["xa_wq"], out_dtype=BF16, name="mm_xq")
    k2 = mm(mem_n, W["xa_wk"], out_dtype=BF16, name="mm_xk")
    v2 = mm(mem_n, W["xa_wv"], out_dtype=BF16, name="mm_xv")
    o2 = _xattn_fwd(q2, k2, v2, name="xattn_fwd")
    xa = mm(o2, W["xa_wo"], name="mm_xo")
    x2, h3 = _resnorm_norm(x1, xa, P["norm_xa_post"], P["norm_ffn_pre"], name="resnorm_2")

    half = N_DEV // 2
    up_g = mm(h3, W["ffn_w_up"], n_blocks=half, name="mm_up_g")
    up_v = mm(h3, W["ffn_w_up"], b_block0=half, name="mm_up_v")
    act = _convgate_fwd(up_g, up_v, W["ffn_conv_w"], P["ffn_conv_b"], name="convgate_fwd")
    f = mm(act, W["ffn_w_down"], name="mm_down")
    loss, dy, df, dg_ffn_post = _final_loss(x2, f, P["norm_ffn_post"], target, name="final_loss")

    G = {"norm_ffn_post": dg_ffn_post}
    dact = mm(df, W["ffn_w_down"], tb=True, name="mm_down_dx")
    G["ffn_w_down"] = mm(act, df, ta=True, name="mm_down_dw")
    dug, duv, dwg, dwv, dbg, dbv = _convgate_bwd(up_g, up_v, dact, W["ffn_conv_w"], P["ffn_conv_b"], name="convgate_bwd")
    G["ffn_conv_w"] = jnp.concatenate([dwg, dwv], axis=1)
    G["ffn_conv_b"] = jnp.concatenate([dbg, dbv], axis=1)
    dh3 = mm(dug, W["ffn_w_up"], tb=True, n_blocks=half, name="mm_up_g_dx")
    dh3 = mm(duv, W["ffn_w_up"], tb=True, b_block0=half, acc_in=dh3, name="mm_up_v_dx")
    dw_up = mm(h3, dug, ta=True, out_into=lax.empty(W["ffn_w_up"].shape, F32), name="mm_up_g_dw")
    G["ffn_w_up"] = mm(h3, duv, ta=True, out_into=dw_up, out_block0=half, name="mm_up_v_dw")
    blocks = {n: _grad_blocks(n, G[n]) for n in REDUCE_FFN}
    (dx2, dxa, G["norm_ffn_pre"], G["norm_xa_post"]), from_core = _norm_bwd_pair(
        dy, dh3, x2, P["norm_ffn_pre"], xa, P["norm_xa_post"], name="norm_bwd_3",
        rider=_send_c([blocks[n] for n in REDUCE_FFN]))
    pair = {n: _pair_sum(blocks[n], r, core, name="pair_sum_" + n) for n, r in zip(REDUCE_FFN, from_core)}

    G["xa_wo"] = mm(o2, dxa, ta=True, name="mm_xo_dw")
    do2 = mm(dxa, W["xa_wo"], tb=True, out_dtype=BF16, name="mm_xo_dx")
    dq2, dk2, dv2 = _xattn_bwd(q2, k2, v2, do2, name="xattn_bwd")
    G["xa_wq"] = mm(h2, dq2, ta=True, name="mm_xq_dw")
    dh2 = mm(dq2, W["xa_wq"], tb=True, name="mm_xq_dx")
    G["xa_wk"] = mm(mem_n, dk2, ta=True, name="mm_xk_dw")
    G["xa_wv"] = mm(mem_n, dv2, ta=True, name="mm_xv_dw")
    dmem_n = jnp.concatenate([dk2, dv2], axis=1)
    wkv = jnp.concatenate([W["xa_wk"], W["xa_wv"]], axis=1)
    dmem = mm(dmem_n, wkv, tb=True, name="mm_xkv_dx")
    _, G["norm_mem"] = _norm_bwd_single(None, dmem, mem, P["norm_mem"], name="norm_bwd_mem")
    (dx1, dmo, G["norm_xa_pre"], G["norm_mix_post"]), _ = _norm_bwd_pair(
        dx2, dh2, x1, P["norm_xa_pre"], mo, P["norm_mix_post"], name="norm_bwd_2")

    G["w_out"] = mm(merged, dmo, ta=True, name="mm_out_dw")
    dmerged = mm(dmo, W["w_out"], tb=True, name="mm_out_dx")
    do_attn, do_ssm, dgate, G["b_gate"], G["w_branch_attn"], G["w_branch_ssm"] = _merge_bwd(
        dmerged, proj, o_attn, o_ssm, W["w_branch_attn"], W["w_branch_ssm"], P["b_gate"], name="merge_bwd")
    dy_pre, G["ssm_w_glu"], G["ssm_b_glu"] = _glu_bwd(y_pre, do_ssm, W["ssm_w_glu"], P["ssm_b_glu"], name="glu_bwd")
    blocks.update({n: _grad_blocks(n, G[n]) for n in REDUCE_MID})
    (du, dbd_re, dbd_im, dcd_re, dcd_imneg, G["ssm_d"], dl_re, dl_im), from_core = _ssm_bwd(
        dy_pre, proj, x_re, x_im, bd_re, bd_im, cd_re, cd_imneg, P["ssm_d"], tab_b, name="ssm_bwd",
        rider=_send_c([blocks[n] for n in REDUCE_MID]))
    pair.update({n: _pair_sum(blocks[n], r, core, name="pair_sum_" + n) for n, r in zip(REDUCE_MID, from_core)})
    G["ssm_c_re"] = _c_from_cd(dcd_re)
    G["ssm_c_im"] = -_c_from_cd(dcd_imneg)
    dlam_re = jnp.sum(dl_re, axis=1).reshape(SSM_GROUPS, SSM_STATE)
    dlam_im = jnp.sum(dl_im, axis=1).reshape(SSM_GROUPS, SSM_STATE)
    (G["ssm_a_re"], G["ssm_a_im"], G["ssm_log_dt"], G["ssm_b_re"], G["ssm_b_im"]) = prep_vjp(
        (dlam_re, dlam_im, _bbar_from_bd(dbd_re), _bbar_from_bd(dbd_im)))
    early = REDUCE_FFN + REDUCE_MID
    (dq, dk, dv), from_chips = _sb_bwd(proj, sb_tot, sb_first, do_attn, name="sb_bwd",
                                       rider=_scatter_xy([pair[n] for n in early]))
    reduced = {n: (pair[n], parts) for n, parts in zip(early, from_chips)}
    dproj = jnp.concatenate([dq, dk, dv, du, dgate], axis=1)
    G["w_in"] = mm(h1, dproj, ta=True, out_cb=W["w_in"].shape[2], name="mm_in_dw")
    dh1 = mm(dproj, W["w_in"], tb=True, name="mm_in_dx")
    grad_x, G["norm_mix_pre"] = _norm_bwd_single(dx1, dh1, x, P["norm_mix_pre"], name="norm_bwd_1")
    g_in = _grad_blocks("w_in", G["w_in"])
    from_core, = _send_c([g_in]).run(name="reduce_in_c")
    pair_in = _pair_sum(g_in, from_core, core, name="pair_sum_w_in")
    from_chips, = _scatter_xy([pair_in]).run(name="reduce_in_xy")
    reduced["w_in"] = (pair_in, from_chips)
    return loss, grad_x, G, reduced


MESH = pl.DeviceIdType.MESH
_HBM = pl.BlockSpec(memory_space=pl.ANY)
N_XY = 4
N_XY_PEERS = 3


def _xy_peers(x, y):
    return [(1 - x, y), (x, 1 - y), (1 - x, 1 - y)]


class _Exchange:
    def __init__(self, arrays, out_shapes, plan, n_copies, alias):
        self.arrays = list(arrays)
        self.out_shapes = list(out_shapes)
        self.plan = plan
        self.n_copies = n_copies
        self.alias = alias

    @property
    def n(self):
        return len(self.arrays)

    def sems(self):
        shape = (self.n, self.n_copies)
        return [pltpu.SemaphoreType.DMA(shape), pltpu.SemaphoreType.DMA(shape)]

    def _copies(self, ins, outs, send_sems, recv_sems):
        x, y, c = lax.axis_index("x"), lax.axis_index("y"), lax.axis_index("c")
        sends, lands = [], []
        for k in range(self.n):
            for j, (src, dst, dev, land) in enumerate(self.plan(k, ins[k], outs[k], x, y, c)):
                sems = dict(send_sem=send_sems.at[k, j], recv_sem=recv_sems.at[k, j], device_id=dev, device_id_type=MESH)
                sends.append(pltpu.make_async_remote_copy(src_ref=src, dst_ref=dst, **sems))
                lands.append(pltpu.make_async_remote_copy(src_ref=src, dst_ref=land, **sems))
        return sends, lands

    def start(self, ins, outs, send_sems, recv_sems):
        for cp in self._copies(ins, outs, send_sems, recv_sems)[0]:
            cp.start()

    def finish(self, ins, outs, send_sems, recv_sems):
        sends, lands = self._copies(ins, outs, send_sems, recv_sems)
        for cp in lands:
            cp.wait_recv()
        for cp in sends:
            cp.wait_send()

    def run(self, *, name):
        n = self.n

        def body(*refs):
            parts = (refs[:n], refs[n:2 * n], refs[2 * n], refs[2 * n + 1])
            self.start(*parts)
            self.finish(*parts)

        return pl.pallas_call(
            body, name=name, out_shape=tuple(self.out_shapes),
            in_specs=[_HBM] * n, out_specs=tuple([_HBM] * n),
            input_output_aliases={k: k for k in range(n)} if self.alias else {},
            scratch_shapes=self.sems(),
        )(*self.arrays)


def _call(host_body, *, name, grid, in_specs, out_specs, out_shape, scratch_shapes, operands, rider=None):
    out_specs, out_shape = tuple(out_specs), tuple(out_shape)
    if rider is None:
        res = pl.pallas_call(
            host_body, name=name, grid=grid, in_specs=list(in_specs), out_specs=out_specs, out_shape=out_shape,
            scratch_shapes=list(scratch_shapes), compiler_params=_cparams(*["arbitrary"] * len(grid)),
        )(*operands)
        return tuple(res), None
    n, n_in, n_out, n_scr = rider.n, len(in_specs), len(out_specs), len(scratch_shapes)

    def body(*refs):
        pos = [0]

        def take(count):
            pos[0] += count
            return refs[pos[0] - count:pos[0]]

        h_in, r_in, h_out, r_out, h_scr = take(n_in), take(n), take(n_out), take(n), take(n_scr)
        send_sems, recv_sems = take(2)
        ids = [pl.program_id(a) for a in range(len(grid))]
        first = functools.reduce(jnp.logical_and, [i == 0 for i in ids])
        last = functools.reduce(jnp.logical_and, [i == g - 1 for i, g in zip(ids, grid)])

        @pl.when(first)
        def _():
            rider.start(r_in, r_out, send_sems, recv_sems)

        host_body(*h_in, *h_out, *h_scr)

        @pl.when(last)
        def _():
            rider.finish(r_in, r_out, send_sems, recv_sems)

    res = pl.pallas_call(
        body, name=name, grid=grid,
        in_specs=list(in_specs) + [_HBM] * n, out_specs=out_specs + tuple([_HBM] * n),
        out_shape=out_shape + tuple(rider.out_shapes),
        input_output_aliases={n_in + k: n_out + k for k in range(n)} if rider.alias else {},
        scratch_shapes=list(scratch_shapes) + rider.sems(),
        compiler_params=_cparams(*["arbitrary"] * len(grid)),
    )(*operands, *rider.arrays)
    return tuple(res[:n_out]), list(res[n_out:])


def _same(arrays):
    return [jax.ShapeDtypeStruct(a.shape, a.dtype) for a in arrays]


def _fill_xy(bufs):
    def plan(k, src, dst, x, y, c):
        mine = 2 * x + y
        return [(src.at[mine, c], dst.at[mine, c], (px, py, c), dst.at[2 * px + py, c]) for px, py in _xy_peers(x, y)]

    return _Exchange(bufs, _same(bufs), plan, N_XY_PEERS, alias=True)


def _fill_c(bufs):
    def plan(k, src, dst, x, y, c):
        return [(src.at[:, c], dst.at[:, c], (x, y, 1 - c), dst.at[:, 1 - c])]

    return _Exchange(bufs, _same(bufs), plan, 1, alias=True)


def _send_c(srcs):
    def plan(k, src, dst, x, y, c):
        return [(src.at[:, 1 - c], dst, (x, y, 1 - c), dst)]

    outs = [jax.ShapeDtypeStruct(a.shape[:1] + a.shape[2:], a.dtype) for a in srcs]
    return _Exchange(srcs, outs, plan, 1, alias=False)


def _scatter_xy(srcs):
    def plan(k, src, dst, x, y, c):
        return [(src.at[2 * px + py], dst.at[j], (px, py, c), dst.at[j]) for j, (px, py) in enumerate(_xy_peers(x, y))]

    outs = [jax.ShapeDtypeStruct((N_XY_PEERS,) + a.shape[1:], a.dtype) for a in srcs]
    return _Exchange(srcs, outs, plan, N_XY_PEERS, alias=False)


PACK_COLS = 1024
WIRE_DTYPE = BF16


def _pair_sum(g8, recv, core, *, name):
    n, _, R, C = g8.shape
    tr = _pick(R, (128, 64, 32, 16, 8))

    def body(core_ref, a_ref, b_ref, o_ref):
        o_ref[...] = (a_ref[0] + b_ref[...]).astype(WIRE_DTYPE)

    return pl.pallas_call(
        body, name=name, out_shape=jax.ShapeDtypeStruct((n, R, C), WIRE_DTYPE),
        grid_spec=pltpu.PrefetchScalarGridSpec(
            num_scalar_prefetch=1, grid=(n, R // tr),
            in_specs=[pl.BlockSpec((1, 1, tr, C), lambda s, i, core_ref: (s, core_ref[0], i, 0)),
                      pl.BlockSpec((1, tr, C), lambda s, i, core_ref: (s, i, 0))],
            out_specs=pl.BlockSpec((1, tr, C), lambda s, i, core_ref: (s, i, 0))),
        compiler_params=_cparams("parallel", "parallel"),
    )(core, g8, recv)


def _adamw_math(w, g, m, v):
    m = ADAM_B1 * m + (1.0 - ADAM_B1) * g
    v = ADAM_B2 * v + (1.0 - ADAM_B2) * (g * g)
    m_hat = m / (1.0 - ADAM_B1 ** ADAM_STEP)
    v_hat = v / (1.0 - ADAM_B2 ** ADAM_STEP)
    delta = -ADAM_LR * (m_hat / (jnp.sqrt(v_hat) + ADAM_EPS) + ADAM_WD * w)
    return delta, m, v


def _reduce_adamw(parts, w, m, v, *, own=None, own_slot=None, name):
    n, R, C = parts.shape
    tr = _pick(R, (128, 64, 32, 16, 8))
    has_own = own is not None

    def body(*refs):
        if has_own:
            _, own_ref, parts_ref, w_ref, m_ref, v_ref, g_ref, d_ref, nm_ref, nv_ref = refs
            g = own_ref[0].astype(F32)
            first = 0
        else:
            parts_ref, w_ref, m_ref, v_ref, g_ref, d_ref, nm_ref, nv_ref = refs
            g = parts_ref[0]
            first = 1
        for k in range(first, n):
            g = g + parts_ref[k].astype(F32)
        g_ref[...] = g
        d_ref[...], nm_ref[...], nv_ref[...] = _adamw_math(w_ref[...], g, m_ref[...], v_ref[...])

    out = jax.ShapeDtypeStruct((R, C), F32)
    if has_own:
        row = pl.BlockSpec((tr, C), lambda i, s: (i, 0))
        return pl.pallas_call(
            body, name=name, out_shape=(out, out, out, out),
            grid_spec=pltpu.PrefetchScalarGridSpec(
                num_scalar_prefetch=1, grid=(R // tr,),
                in_specs=[pl.BlockSpec((1, tr, C), lambda i, s: (s[0], i, 0)),
                          pl.BlockSpec((n, tr, C), lambda i, s: (0, i, 0)), row, row, row],
                out_specs=(row, row, row, row)),
            compiler_params=_cparams("parallel"),
        )(own_slot, own, parts, w, m, v)
    row = pl.BlockSpec((tr, C), lambda i: (i, 0))
    return pl.pallas_call(
        body, name=name, out_shape=(out, out, out, out), grid=(R // tr,),
        in_specs=[pl.BlockSpec((n, tr, C), lambda i: (0, i, 0)), row, row, row],
        out_specs=(row, row, row, row), compiler_params=_cparams("parallel"),
    )(parts, w, m, v)


SHARDED = (("w_in", (1024, 4096), 1), ("ssm_w_glu", (512, 512), 0), ("w_branch_attn", (512, 1024), 1),
           ("w_branch_ssm", (512, 1024), 1), ("w_out", (1024, 1024), 0), ("xa_wq", (1024, 1024), 0),
           ("xa_wk", (1024, 1024), 0), ("xa_wv", (1024, 1024), 0), ("xa_wo", (1024, 1024), 0),
           ("ffn_w_up", (1024, 5632), 1), ("ffn_conv_w", (3, 5632), 1), ("ffn_w_down", (2816, 1024), 0))
REPLICATED = (("norm_mix_pre", (1024,)), ("norm_mix_post", (1024,)), ("b_gate", (2048,)), ("ssm_a_re", (32, 64)),
              ("ssm_a_im", (32, 64)), ("ssm_log_dt", (32,)), ("ssm_b_re", (32, 64, 16)), ("ssm_b_im", (32, 64, 16)),
              ("ssm_c_re", (32, 16, 64)), ("ssm_c_im", (32, 16, 64)), ("ssm_d", (512,)), ("ssm_b_glu", (512,)),
              ("norm_xa_pre", (1024,)), ("norm_xa_post", (1024,)), ("norm_mem", (1024,)), ("norm_ffn_pre", (1024,)),
              ("norm_ffn_post", (1024,)), ("ffn_conv_b", (5632,)))
PARAM_ORDER = ("norm_mix_pre", "norm_mix_post", "w_in", "b_gate", "ssm_a_re", "ssm_a_im", "ssm_log_dt", "ssm_b_re",
               "ssm_b_im", "ssm_c_re", "ssm_c_im", "ssm_d", "ssm_w_glu", "ssm_b_glu", "w_branch_attn", "w_branch_ssm",
               "w_out", "norm_xa_pre", "norm_xa_post", "norm_mem", "xa_wq", "xa_wk", "xa_wv", "xa_wo", "norm_ffn_pre",
               "norm_ffn_post", "ffn_w_up", "ffn_conv_w", "ffn_conv_b", "ffn_w_down")
SMALL_ROWS = 160
FF_LOCAL = 2 * D_FF // N_DEV
FF_LOCAL_PAD = 768
FF_PAD = (N_DEV // 2) * FF_LOCAL_PAD


def _local_shape(shape, axis):
    return tuple(s // N_DEV if a == axis else s for a, s in enumerate(shape))


def _pad_cols(a, width):
    return jnp.pad(a, [(0, 0)] * (a.ndim - 1) + [(0, width - a.shape[-1])])


def _blocks_to_cols(a8):
    return a8.transpose(1, 0, 2).reshape(a8.shape[1], N_DEV * a8.shape[2])


def _cols_to_blocks(a, cb):
    return a.reshape(a.shape[0], N_DEV, cb).transpose(1, 0, 2)


FF_PADDED = ("ffn_w_up", "ffn_conv_w")
LATE = tuple(n for n, _, _ in SHARDED if n != "w_in")
REDUCE_FFN = ("ffn_w_up", "ffn_conv_w", "ffn_w_down")
REDUCE_MID = ("xa_wo", "xa_wq", "xa_wk", "xa_wv", "w_out", "w_branch_attn", "w_branch_ssm", "ssm_w_glu")
SHARD_AXIS = {n: ax for n, _, ax in SHARDED}
FULL_SHAPE = {n: s for n, s, _ in SHARDED}


def _as_local(n, a):
    return _pad_cols(a, FF_LOCAL_PAD) if n in FF_PADDED else a


def _weights_from_wire(wire):
    full = {n: b.reshape((N_DEV,) + b.shape[2:]) for n, b in wire.items()}
    W = {n: a.reshape(FULL_SHAPE[n]) if SHARD_AXIS[n] == 0 else a for n, a in full.items()}
    for n in ("w_branch_attn", "w_branch_ssm", "ffn_conv_w"):
        W[n] = _blocks_to_cols(full[n])
    W["ffn_w_down"] = jnp.pad(W["ffn_w_down"].reshape(N_DEV // 2, FF_LOCAL, D_MODEL),
                              ((0, 0), (0, FF_LOCAL_PAD - FF_LOCAL), (0, 0))).reshape(FF_PAD, D_MODEL)
    return W


def _grad_blocks(n, g):
    if n in ("w_branch_attn", "w_branch_ssm"):
        g = _cols_to_blocks(g, D_MODEL // N_DEV)
    elif n == "ffn_conv_w":
        g = _cols_to_blocks(g, FF_LOCAL_PAD)
    elif n == "ffn_w_down":
        g = g.reshape(N_DEV // 2, FF_LOCAL_PAD, D_MODEL)[:, :FF_LOCAL]
    local = _local_shape(FULL_SHAPE[n], SHARD_AXIS[n])
    if n in FF_PADDED:
        local = local[:-1] + (FF_LOCAL_PAD,)
    return g.reshape((N_XY, 2) + local)


def _pack_small(d):
    flat = jnp.concatenate([d[n].reshape(-1) for n, _ in REPLICATED])
    return _pad_cols(flat, SMALL_ROWS * PACK_COLS).reshape(SMALL_ROWS, PACK_COLS)


def _unpack_small(buf):
    flat = buf.reshape(-1)
    out, off = {}, 0
    for n, shape in REPLICATED:
        size = math.prod(shape)
        out[n] = flat[off:off + size]
        off += size
    return out


def kernel(x, mem, norm_mix_pre, norm_mix_post, w_in, b_gate, ssm_a_re, ssm_a_im, ssm_log_dt, ssm_b_re, ssm_b_im, ssm_c_re, ssm_c_im, ssm_d, ssm_w_glu, ssm_b_glu, w_branch_attn, w_branch_ssm, w_out, norm_xa_pre, norm_xa_post, norm_mem, xa_wq, xa_wk, xa_wv, xa_wo, norm_ffn_pre, norm_ffn_post, ffn_w_up, ffn_conv_w, ffn_conv_b, ffn_w_down, loss_target, m_norm_mix_pre, m_norm_mix_post, m_w_in, m_b_gate, m_ssm_a_re, m_ssm_a_im, m_ssm_log_dt, m_ssm_b_re, m_ssm_b_im, m_ssm_c_re, m_ssm_c_im, m_ssm_d, m_ssm_w_glu, m_ssm_b_glu, m_w_branch_attn, m_w_branch_ssm, m_w_out, m_norm_xa_pre, m_norm_xa_post, m_norm_mem, m_xa_wq, m_xa_wk, m_xa_wv, m_xa_wo, m_norm_ffn_pre, m_norm_ffn_post, m_ffn_w_up, m_ffn_conv_w, m_ffn_conv_b, m_ffn_w_down, v_norm_mix_pre, v_norm_mix_post, v_w_in, v_b_gate, v_ssm_a_re, v_ssm_a_im, v_ssm_log_dt, v_ssm_b_re, v_ssm_b_im, v_ssm_c_re, v_ssm_c_im, v_ssm_d, v_ssm_w_glu, v_ssm_b_glu, v_w_branch_attn, v_w_branch_ssm, v_w_out, v_norm_xa_pre, v_norm_xa_post, v_norm_mem, v_xa_wq, v_xa_wk, v_xa_wv, v_xa_wo, v_norm_ffn_pre, v_norm_ffn_post, v_ffn_w_up, v_ffn_conv_w, v_ffn_conv_b, v_ffn_w_down):
    args = dict(locals())
    w_loc = {n: args[n][0] for n in PARAM_ORDER}
    m_loc = {n: args["m_" + n][0] for n in PARAM_ORDER}
    v_loc = {n: args["v_" + n][0] for n in PARAM_ORDER}
    core_i = lax.axis_index("c")
    chip_i = 2 * lax.axis_index("x") + lax.axis_index("y")
    core = core_i.astype(jnp.int32).reshape(1)
    chip = chip_i.astype(jnp.int32).reshape(1)

    def in_place(a):
        buf = lax.empty((N_XY, 2) + a.shape, a.dtype)
        return lax.dynamic_update_slice(buf, a[None, None], (chip_i, core_i) + (0,) * a.ndim)

    as_wire = lambda n: in_place(_as_local(n, w_loc[n]).astype(F32 if n == "ffn_conv_w" else BF16))
    wire_in = _fill_c(_fill_xy([as_wire("w_in")]).run(name="gather_in_xy")).run(name="gather_in_c")[0]
    w_in_full = wire_in.reshape((N_DEV,) + wire_in.shape[2:])

    P = {}
    for n, shape in REPLICATED:
        P[n] = w_loc[n] if len(shape) > 1 or n == "ssm_log_dt" else w_loc[n].reshape(1, -1)
    P["ffn_conv_b"] = _pad_cols(w_loc["ffn_conv_b"].reshape(N_DEV, FF_LOCAL), FF_LOCAL_PAD).reshape(1, 2 * FF_PAD)

    loss, grad_x, G, reduced = _local_step(x[0], mem[0], loss_target[0], w_in_full, [as_wire(n) for n in LATE], P, core)
    loss = lax.psum(loss[0, 0], ("x", "y", "c"))

    big_out = {}
    for n, (own, parts) in reduced.items():
        res = _reduce_adamw(parts, _as_local(n, w_loc[n]), _as_local(n, m_loc[n]), _as_local(n, v_loc[n]),
                            own=own, own_slot=chip, name="adamw_" + n)
        big_out[n] = [r[:, :FF_LOCAL] if n in FF_PADDED else r for r in res]

    G["ffn_conv_b"] = G["ffn_conv_b"].reshape(N_DEV, FF_LOCAL_PAD)[:, :FF_LOCAL]
    parts, = _fill_c(_fill_xy([in_place(_pack_small(G))]).run(name="gather_g_xy")).run(name="gather_g_c")
    parts = parts.reshape((N_DEV,) + parts.shape[2:])
    small_out = _reduce_adamw(parts, _pack_small(w_loc), _pack_small(m_loc), _pack_small(v_loc), name="adamw_replicated")
    small_out = [_unpack_small(b) for b in small_out]

    outs = [loss, grad_x[None]]
    for k in range(4):
        for n in PARAM_ORDER:
            src = big_out[n][k] if n in big_out else small_out[k][n]
            outs.append(src.reshape(args[n].shape))
    return tuple(outs)
```

```python
import functools
import math

import jax
import jax.numpy as jnp
from jax import lax
from jax.experimental import pallas as pl
from jax.experimental.pallas import tpu as pltpu

F32 = jnp.float32
BF16 = jnp.bfloat16

D_MODEL = 1024
SB_HEADS = 8
SB_HEAD_DIM = 64
SB_WIDTH = 512
SSM_WIDTH = 512
SSM_GROUP = 16
SSM_GROUPS = 32
SSM_STATE = 64
XA_HEADS = 4
XA_HEAD_DIM = 256
D_FF = 2816
RMS_EPS = 1e-6
IN_WIDTH = 4096
N_DEV = 8

ADAM_LR = 0.001
ADAM_B1 = 0.9
ADAM_B2 = 0.999
ADAM_EPS = 1e-08
ADAM_WD = 0.01
ADAM_STEP = 10

LANES = 128
SUBLANES = 8
VMEM_LIMIT = 48 * 1024 * 1024

_GELU_C = math.sqrt(2.0 / math.pi)


def _cparams(*sem):
    return pltpu.CompilerParams(dimension_semantics=sem, vmem_limit_bytes=VMEM_LIMIT)


def _pick(n, cands):
    for c in cands:
        if n % c == 0:
            return c
    return n


def _gelu(x):
    return 0.5 * x * (1.0 + jnp.tanh(_GELU_C * (x + 0.044715 * x * x * x)))


def _gelu_and_grad(x):
    t = jnp.tanh(_GELU_C * (x + 0.044715 * x * x * x))
    g = 0.5 * x * (1.0 + t)
    dg = 0.5 * (1.0 + t) + 0.5 * x * (1.0 - t * t) * _GELU_C * (1.0 + 3.0 * 0.044715 * x * x)
    return g, dg


def _sigmoid(x):
    return 1.0 / (1.0 + jnp.exp(-x))


def _dot(a, b, ca, cb):
    return lax.dot_general(a.astype(BF16), b.astype(BF16), (((ca,), (cb,)), ((), ())),
                           preferred_element_type=F32)


MM_TILES = (1024, 768, 512, 256, 128)


def _matmul(a, b, *, ta=False, tb=False, out_dtype=F32, name, b_block0=0, n_blocks=None,
            out_cb=None, out_into=None, out_block0=0, acc_in=None):
    if ta:
        K, M = a.shape
    else:
        M, K = a.shape
    b_cb = None
    if b.ndim == 3:
        b_cb = b.shape[2]
        n_blocks = b.shape[0] - b_block0 if n_blocks is None else n_blocks
        N, K2 = (b.shape[1], n_blocks * b_cb) if tb else (n_blocks * b_cb, b.shape[1])
    elif tb:
        N, K2 = b.shape
    else:
        K2, N = b.shape
    assert K == K2, (a.shape, b.shape, ta, tb)
    if out_into is not None:
        out_cb = out_into.shape[2]
    tm = _pick(M, MM_TILES)
    n_unit = math.gcd(N, math.gcd(b_cb if (b_cb and not tb) else N, out_cb or N))
    tn = _pick(n_unit, MM_TILES)
    k_unit = b_cb if (b_cb and tb) else K
    tk = _pick(k_unit, MM_TILES)
    nk = K // tk
    ca, cb = (0 if ta else 1), (1 if tb else 0)
    has_acc = acc_in is not None
    has_into = out_into is not None

    def body(*refs):
        a_ref, b_ref = refs[0], refs[1]
        pos = 2
        c_ref = None
        if has_acc:
            c_ref = refs[pos]
            pos += 1
        if has_into:
            pos += 1
        o_ref = refs[pos]
        p = _dot(a_ref[...], b_ref[...], ca, cb)
        if nk == 1:
            o_ref[...] = ((p + c_ref[...]) if has_acc else p).astype(out_dtype)
        else:
            acc_ref = refs[pos + 1]
            k = pl.program_id(2)

            @pl.when(k == 0)
            def _():
                acc_ref[...] = (p + c_ref[...]) if has_acc else p

            @pl.when(k > 0)
            def _():
                acc_ref[...] += p

            @pl.when(k == nk - 1)
            def _():
                o_ref[...] = acc_ref[...].astype(out_dtype)

    a_spec = pl.BlockSpec((tk, tm), lambda j, i, k: (k, i)) if ta else pl.BlockSpec((tm, tk), lambda j, i, k: (i, k))
    if b_cb is None:
        b_spec = pl.BlockSpec((tn, tk), lambda j, i, k: (j, k)) if tb else pl.BlockSpec((tk, tn), lambda j, i, k: (k, j))
    elif tb:
        per = b_cb // tk
        b_spec = pl.BlockSpec((None, tn, tk), lambda j, i, k: (b_block0 + k // per, j, k % per))
    else:
        per = b_cb // tn
        b_spec = pl.BlockSpec((None, tk, tn), lambda j, i, k: (b_block0 + j // per, k, j % per))
    in_specs = [a_spec, b_spec]
    operands = [a, b]
    aliases = {}
    if has_acc:
        in_specs.append(pl.BlockSpec((tm, tn), lambda j, i, k: (i, j)))
        operands.append(acc_in)
    if has_into:
        aliases = {len(operands): 0}
        in_specs.append(pl.BlockSpec(memory_space=pl.ANY))
        operands.append(out_into)
    if out_cb is None:
        out_shape = jax.ShapeDtypeStruct((M, N), out_dtype)
        out_spec = pl.BlockSpec((tm, tn), lambda j, i, k: (i, j))
    else:
        per_o = out_cb // tn
        out_shape = (jax.ShapeDtypeStruct(out_into.shape, out_into.dtype) if has_into
                     else jax.ShapeDtypeStruct((N // out_cb, M, out_cb), out_dtype))
        out_spec = pl.BlockSpec((None, tm, tn), lambda j, i, k: (out_block0 + j // per_o, i, j % per_o))
    return pl.pallas_call(
        body, name=name, out_shape=out_shape,
        grid=(N // tn, M // tm, nk),
        in_specs=in_specs, out_specs=out_spec, input_output_aliases=aliases,
        scratch_shapes=[] if nk == 1 else [pltpu.VMEM((tm, tn), F32)],
        compiler_params=_cparams("parallel", "parallel", "arbitrary"),
    )(*operands)


def _rms(x, g):
    r = lax.rsqrt(jnp.mean(x * x, axis=-1, keepdims=True) + RMS_EPS)
    return x * r * g


def _rms_bwd(dy, x, g):
    r = lax.rsqrt(jnp.mean(x * x, axis=-1, keepdims=True) + RMS_EPS)
    xh = x * r
    dxh = dy * g
    dx = r * (dxh - xh * jnp.mean(dxh * xh, axis=-1, keepdims=True))
    dg = jnp.sum(dy * xh, axis=0, keepdims=True)
    return dx, dg


def _row_tile(rows):
    return _pick(rows, (512, 256, 128, 64, 32, 16, 8))


def _rms_fwd(x, g, *, name):
    R, D = x.shape
    tr = _row_tile(R)

    def body(x_ref, g_ref, h_ref):
        h_ref[...] = _rms(x_ref[...], g_ref[...]).astype(BF16)

    return pl.pallas_call(
        body, name=name, out_shape=jax.ShapeDtypeStruct((R, D), BF16), grid=(R // tr,),
        in_specs=[pl.BlockSpec((tr, D), lambda i: (i, 0)), pl.BlockSpec((1, D), lambda i: (0, 0))],
        out_specs=pl.BlockSpec((tr, D), lambda i: (i, 0)),
        compiler_params=_cparams("parallel"),
    )(x, g)


def _resnorm_norm(x, z, g_post, g_next, *, name):
    R, D = x.shape
    tr = _row_tile(R)

    def body(x_ref, z_ref, gp_ref, gn_ref, xn_ref, h_ref):
        xn = x_ref[...] + _rms(z_ref[...], gp_ref[...])
        xn_ref[...] = xn
        h_ref[...] = _rms(xn, gn_ref[...]).astype(BF16)

    row = pl.BlockSpec((tr, D), lambda i: (i, 0))
    vec = pl.BlockSpec((1, D), lambda i: (0, 0))
    return pl.pallas_call(
        body, name=name,
        out_shape=(jax.ShapeDtypeStruct((R, D), F32), jax.ShapeDtypeStruct((R, D), BF16)),
        grid=(R // tr,), in_specs=[row, row, vec, vec], out_specs=(row, row),
        compiler_params=_cparams("parallel"),
    )(x, z, g_post, g_next)


def _final_loss(x, z, g_post, target, *, name):
    R, D = x.shape
    tr = _row_tile(R)

    def body(x_ref, z_ref, gp_ref, t_ref, loss_ref, dy_ref, dz_ref, dg_ref):
        i = pl.program_id(0)
        z = z_ref[...]
        g = gp_ref[...]
        err = x_ref[...] + _rms(z, g) - t_ref[...]
        dy = err * (1.0 / D)
        dy_ref[...] = dy
        dz, dg = _rms_bwd(dy, z, g)
        dz_ref[...] = dz.astype(BF16)
        part = 0.5 * jnp.sum(jnp.sum(err * err, axis=-1, keepdims=True) * (1.0 / D), axis=0, keepdims=True)

        @pl.when(i == 0)
        def _():
            loss_ref[...] = part
            dg_ref[...] = dg

        @pl.when(i > 0)
        def _():
            loss_ref[...] += part
            dg_ref[...] += dg

    row = pl.BlockSpec((tr, D), lambda i: (i, 0))
    vec = pl.BlockSpec((1, D), lambda i: (0, 0))
    return pl.pallas_call(
        body, name=name,
        out_shape=(jax.ShapeDtypeStruct((1, 1), F32), jax.ShapeDtypeStruct((R, D), F32),
                   jax.ShapeDtypeStruct((R, D), BF16), jax.ShapeDtypeStruct((1, D), F32)),
        grid=(R // tr,), in_specs=[row, row, vec, row],
        out_specs=(pl.BlockSpec((1, 1), lambda i: (0, 0)), row, row, vec),
        compiler_params=_cparams("arbitrary"),
    )(x, z, g_post, target)


def _norm_bwd_pair(dres, dh, xk, g_pre, zprev, g_prev_post, *, name, rider=None):
    R, D = xk.shape
    tr = _row_tile(R)

    def body(dres_ref, dh_ref, x_ref, gpre_ref, z_ref, gpost_ref, dx_ref, dz_ref, dgpre_ref, dgpost_ref):
        i = pl.program_id(0)
        d1, dgpre = _rms_bwd(dh_ref[...], x_ref[...], gpre_ref[...])
        dx = dres_ref[...] + d1
        dx_ref[...] = dx
        dz, dgpost = _rms_bwd(dx, z_ref[...], gpost_ref[...])
        dz_ref[...] = dz.astype(BF16)

        @pl.when(i == 0)
        def _():
            dgpre_ref[...] = dgpre
            dgpost_ref[...] = dgpost

        @pl.when(i > 0)
        def _():
            dgpre_ref[...] += dgpre
            dgpost_ref[...] += dgpost

    row = pl.BlockSpec((tr, D), lambda i: (i, 0))
    vec = pl.BlockSpec((1, D), lambda i: (0, 0))
    return _call(
        body, name=name, rider=rider,
        out_shape=(jax.ShapeDtypeStruct((R, D), F32), jax.ShapeDtypeStruct((R, D), BF16),
                   jax.ShapeDtypeStruct((1, D), F32), jax.ShapeDtypeStruct((1, D), F32)),
        grid=(R // tr,), in_specs=[row, row, row, vec, row, vec], out_specs=(row, row, vec, vec),
        scratch_shapes=[], operands=(dres, dh, xk, g_pre, zprev, g_prev_post))


def _norm_bwd_single(dres, dh, xk, g_pre, *, name):
    R, D = xk.shape
    tr = _row_tile(R)
    has_res = dres is not None

    def body(*refs):
        if has_res:
            dres_ref, dh_ref, x_ref, gpre_ref, dx_ref, dgpre_ref = refs
        else:
            dh_ref, x_ref, gpre_ref, dx_ref, dgpre_ref = refs
        i = pl.program_id(0)
        d1, dgpre = _rms_bwd(dh_ref[...], x_ref[...], gpre_ref[...])
        dx_ref[...] = dres_ref[...] + d1 if has_res else d1

        @pl.when(i == 0)
        def _():
            dgpre_ref[...] = dgpre

        @pl.when(i > 0)
        def _():
            dgpre_ref[...] += dgpre

    row = pl.BlockSpec((tr, D), lambda i: (i, 0))
    vec = pl.BlockSpec((1, D), lambda i: (0, 0))
    ins = ([dres] if has_res else []) + [dh, xk, g_pre]
    return pl.pallas_call(
        body, name=name,
        out_shape=(jax.ShapeDtypeStruct((R, D), F32), jax.ShapeDtypeStruct((1, D), F32)),
        grid=(R // tr,), in_specs=([row] if has_res else []) + [row, row, vec], out_specs=(row, vec),
        compiler_params=_cparams("arbitrary"),
    )(*ins)


SB_BLOCK = 256
SB_QBLOCK = 512
SB_DEAD = -104.0


def _sb_tri(kind):
    r = lax.broadcasted_iota(jnp.int32, (SB_BLOCK, SB_BLOCK), 0)
    c = lax.broadcasted_iota(jnp.int32, (SB_BLOCK, SB_BLOCK), 1)
    keep = {"after": r > c, "upto": r <= c, "before": r < c}[kind]
    return jnp.where(keep, 1.0, 0.0).astype(BF16)


def _running_sum(vals, tri):
    hi = vals.astype(BF16)
    lo = (vals - hi.astype(F32)).astype(BF16)
    return _dot(hi, tri, 1, 0) + _dot(lo, tri, 1, 0)


def _sb_scores(qm, k_blk):
    z = _dot(qm, k_blk, 1, 1)
    sp = jnp.maximum(z, 0.0) + jnp.log(1.0 + jnp.exp(-jnp.abs(z)))
    return z, sp


def _sb_causal(rows):
    r = lax.broadcasted_iota(jnp.int32, (rows, SB_BLOCK), 0)
    c = lax.broadcasted_iota(jnp.int32, (rows, SB_BLOCK), 1)
    return c < r


def _head_masks():
    lane = lax.broadcasted_iota(jnp.int32, (1, LANES), 1)
    return [jnp.where(lane < SB_HEAD_DIM, 1.0, 0.0), jnp.where(lane >= SB_HEAD_DIM, 1.0, 0.0)]


def _sb_fwd(proj, *, name, rider=None):
    S = proj.shape[0]
    T = SB_BLOCK
    TQ = min(SB_QBLOCK, S)
    span = TQ // T
    nq = S // TQ
    npair = SB_WIDTH // LANES
    scale = SB_HEAD_DIM ** -0.5

    def body(q_ref, k_ref, v_ref, o_ref, tot_ref, first_ref, acc_ref, run_ref):
        masks = _head_masks()
        tri = _sb_tri("after")
        first_ref[...] = jnp.zeros_like(first_ref)
        slot = lax.broadcasted_iota(jnp.int32, first_ref.shape, 1)

        def alive():
            reach = jnp.maximum(jnp.max(run_ref[0]), jnp.max(run_ref[1]))
            return (reach > SB_DEAD).astype(jnp.int32)

        def q_block(i, _):
            qrow = pl.ds(pl.multiple_of(i * TQ, TQ), TQ)
            q = q_ref[qrow, :] * scale
            qm = [(q * m).astype(BF16) for m in masks]
            acc_ref[...] = jnp.zeros_like(acc_ref)
            run_ref[...] = jnp.zeros_like(run_ref)

            def k_block(j, own):
                krow = pl.ds(pl.multiple_of(j * T, T), T)
                k_blk = k_ref[krow, :].astype(BF16)
                v_blk = v_ref[krow, :].astype(BF16)
                r0 = 0 if own is None else own * T
                rows = pl.ds(r0, TQ - r0)
                for h in range(2):
                    z, sp = _sb_scores(qm[h][r0:], k_blk)
                    causal = None if own is None else _sb_causal(TQ - r0)
                    lf = -sp if causal is None else jnp.where(causal, -sp, 0.0)
                    e = jnp.exp(z - sp + _running_sum(lf, tri) + run_ref[h, rows])
                    w = e if causal is None else jnp.where(causal, e, 0.0)
                    acc_ref[h, rows] += _dot(w, v_blk, 1, 0)
                    run_ref[h, rows] += jnp.sum(lf, axis=1, keepdims=True)

            for d in reversed(range(span)):
                k_block(i * span + d, d)

            def below(carry):
                jj, _ = carry
                k_block(i * span - 1 - jj, None)
                return jj + 1, alive()

            done, _ = lax.while_loop(lambda c: jnp.logical_and(c[0] < i * span, c[1] > 0), below, (jnp.int32(0), alive()))
            o_ref[qrow, :] = (acc_ref[0] * masks[0] + acc_ref[1] * masks[1]).astype(BF16)
            tot_ref[qrow, :] = run_ref[0] * masks[0] + run_ref[1] * masks[1]
            first_ref[...] = jnp.where(slot == i, (i * span - done).astype(F32), first_ref[...])
            return 0

        lax.fori_loop(0, nq, q_block, 0)

    blk = lambda off: pl.BlockSpec((S, LANES), lambda p: (0, off + p))
    return _call(
        body, name=name, rider=rider,
        out_shape=(jax.ShapeDtypeStruct((S, SB_WIDTH), BF16), jax.ShapeDtypeStruct((S, SB_WIDTH), F32),
                   jax.ShapeDtypeStruct((npair, SUBLANES, LANES), F32)),
        grid=(npair,),
        in_specs=[blk(0), blk(npair), blk(2 * npair)],
        out_specs=(blk(0), blk(0), pl.BlockSpec((1, SUBLANES, LANES), lambda p: (p, 0, 0))),
        scratch_shapes=[pltpu.VMEM((2, TQ, LANES), F32), pltpu.VMEM((2, TQ, 1), F32)],
        operands=(proj, proj, proj))


def _sb_bwd(proj, tot, first, do_attn, *, name, rider=None):
    S = proj.shape[0]
    T = SB_BLOCK
    TQ = min(SB_QBLOCK, S)
    span = TQ // T
    nq = S // TQ
    npair = SB_WIDTH // LANES
    scale = SB_HEAD_DIM ** -0.5

    def body(q_ref, k_ref, v_ref, tot_ref, first_ref, do_ref, dq_ref, dk_ref, dv_ref,
             dqacc_ref, dkacc_ref, dvacc_ref, run_ref, grun_ref):
        masks = _head_masks()
        tri_upto = _sb_tri("upto")
        tri_before = _sb_tri("before")
        dkacc_ref[...] = jnp.zeros_like(dkacc_ref)
        dvacc_ref[...] = jnp.zeros_like(dvacc_ref)
        slot = lax.broadcasted_iota(jnp.int32, first_ref.shape, 1)

        def q_block(i, _):
            qrow = pl.ds(pl.multiple_of(i * TQ, TQ), TQ)
            q = q_ref[qrow, :] * scale
            do = do_ref[qrow, :].astype(F32)
            tot = tot_ref[qrow, :]
            qm = [(q * m).astype(BF16) for m in masks]
            dom = [(do * m).astype(BF16) for m in masks]
            ltot = [jnp.sum(tot * m, axis=1, keepdims=True) * (1.0 / SB_HEAD_DIM) for m in masks]
            dqacc_ref[...] = jnp.zeros_like(dqacc_ref)
            run_ref[...] = jnp.zeros_like(run_ref)
            grun_ref[...] = jnp.zeros_like(grun_ref)

            def k_block(j, own):
                krow = pl.ds(pl.multiple_of(j * T, T), T)
                k_blk = k_ref[krow, :].astype(BF16)
                v_blk = v_ref[krow, :].astype(BF16)
                r0 = 0 if own is None else own * T
                rows = pl.ds(r0, TQ - r0)
                for h in range(2):
                    z, sp = _sb_scores(qm[h][r0:], k_blk)
                    causal = None if own is None else _sb_causal(TQ - r0)
                    lf = -sp if causal is None else jnp.where(causal, -sp, 0.0)
                    later = ltot[h][r0:] - run_ref[h, rows] - _running_sum(lf, tri_upto)
                    beta = jnp.exp(z - sp)
                    w = jnp.exp(z - sp + later)
                    if causal is not None:
                        w = jnp.where(causal, w, 0.0)
                    g = _dot(dom[h][r0:], v_blk, 1, 1) * w
                    gbefore = grun_ref[h, rows] + _dot(g, tri_before, 1, 0)
                    dz = g - beta * (g + gbefore)
                    if causal is not None:
                        dz = jnp.where(causal, dz, 0.0)
                    dz = dz.astype(BF16)
                    dqacc_ref[h, rows] += _dot(dz, k_blk, 1, 0)
                    dkacc_ref[krow, :] += _dot(dz, qm[h][r0:], 0, 0)
                    dvacc_ref[krow, :] += _dot(w, dom[h][r0:], 0, 0)
                    run_ref[h, rows] += jnp.sum(lf, axis=1, keepdims=True)
                    grun_ref[h, rows] += jnp.sum(g, axis=1, keepdims=True)

            def above(j, _):
                k_block(j, None)
                return 0

            first = jnp.max(jnp.where(slot == i, first_ref[...], 0.0)).astype(jnp.int32)
            lax.fori_loop(jnp.clip(first, 0, i * span), i * span, above, 0)
            for d in range(span):
                k_block(i * span + d, d)
            dq_ref[qrow, :] = ((dqacc_ref[0] * masks[0] + dqacc_ref[1] * masks[1]) * scale).astype(BF16)
            return 0

        lax.fori_loop(0, nq, q_block, 0)
        dk_ref[...] = dkacc_ref[...].astype(BF16)
        dv_ref[...] = dvacc_ref[...].astype(BF16)

    blk = lambda off: pl.BlockSpec((S, LANES), lambda p: (0, off + p))
    out = jax.ShapeDtypeStruct((S, SB_WIDTH), BF16)
    return _call(
        body, name=name, rider=rider, out_shape=(out, out, out), grid=(npair,),
        in_specs=[blk(0), blk(npair), blk(2 * npair), blk(0), pl.BlockSpec((1, SUBLANES, LANES), lambda p: (p, 0, 0)),
                  blk(0)],
        out_specs=(blk(0), blk(0), blk(0)),
        scratch_shapes=[pltpu.VMEM((2, TQ, LANES), F32), pltpu.VMEM((S, LANES), F32), pltpu.VMEM((S, LANES), F32),
                        pltpu.VMEM((2, TQ, 1), F32), pltpu.VMEM((2, TQ, 1), F32)],
        operands=(proj, proj, proj, tot, first, do_attn))


SSM_HALVES = 2
SSM_HALF_CH = SSM_WIDTH // SSM_HALVES
SSM_HALF_ST = SSM_GROUPS * SSM_STATE // SSM_HALVES
SSM_CHUNK = 512


def _cmul(ar, ai, br, bi):
    return ar * br - ai * bi, ar * bi + ai * br


def _ssm_tables(lam_re, lam_im):
    lr = lam_re.reshape(-1)
    li = lam_im.reshape(-1)
    pows = [(jnp.ones_like(lr), jnp.zeros_like(li)), (lr, li)]
    for _ in range(2, SUBLANES + 1):
        pows.append(_cmul(pows[-1][0], pows[-1][1], lr, li))
    row = jnp.arange(SUBLANES)[:, None]

    def shift_tab(d, keep):
        return [jnp.where(keep, pows[d][0][None, :], 0.0), jnp.where(keep, pows[d][1][None, :], 0.0)]

    fwd, bwd = [], []
    for d in (1, 2, 4):
        fwd += shift_tab(d, row >= d)
        bwd += shift_tab(d, row + d < SUBLANES)
    fwd += [jnp.stack([pows[r + 1][0] for r in range(SUBLANES)]), jnp.stack([pows[r + 1][1] for r in range(SUBLANES)])]
    bwd += [jnp.stack([pows[SUBLANES - r][0] for r in range(SUBLANES)]),
            jnp.stack([pows[SUBLANES - r][1] for r in range(SUBLANES)])]

    def halves(tabs):
        t = jnp.stack(tabs)
        return t.reshape(8, SUBLANES, SSM_HALVES, SSM_HALF_ST).transpose(2, 0, 1, 3)

    return halves(fwd), halves(bwd)


def _ssm_fwd(proj, bd_re, bd_im, cd_re, cd_imneg, d_skip, tab, *, name, rider=None):
    S = proj.shape[0]
    Tc = min(SSM_CHUNK, S)
    nc = S // Tc
    u_blk0 = (3 * SB_WIDTH) // SSM_HALF_CH

    def body(u_ref, bre_ref, bim_ref, cre_ref, cim_ref, d_ref, tab_ref, y_ref, xre_ref, xim_ref, cre_s, cim_s):
        c = pl.program_id(1)

        @pl.when(c == 0)
        def _():
            cre_s[...] = jnp.zeros_like(cre_s)
            cim_s[...] = jnp.zeros_like(cim_s)

        u = u_ref[...]
        ub = u.astype(BF16)
        xre_ref[...] = _dot(ub, bre_ref[0], 1, 0)
        xim_ref[...] = _dot(ub, bim_ref[0], 1, 0)

        def slab(k, carry):
            car_re, car_im = carry
            rows = pl.ds(pl.multiple_of(k * SUBLANES, SUBLANES), SUBLANES)
            sre = xre_ref[rows, :]
            sim = xim_ref[rows, :]
            for n, d in enumerate((1, 2, 4)):
                pre, pim = tab_ref[0, 2 * n], tab_ref[0, 2 * n + 1]
                rre = pltpu.roll(sre, d, 0)
                rim = pltpu.roll(sim, d, 0)
                sre, sim = sre + (pre * rre - pim * rim), sim + (pre * rim + pim * rre)
            pre, pim = tab_ref[0, 6], tab_ref[0, 7]
            sre, sim = sre + (pre * car_re - pim * car_im), sim + (pre * car_im + pim * car_re)
            xre_ref[rows, :] = sre
            xim_ref[rows, :] = sim
            last = (SUBLANES - 1, SUBLANES)
            return (jnp.broadcast_to(sre[last[0]:last[1], :], sre.shape),
                    jnp.broadcast_to(sim[last[0]:last[1], :], sim.shape))

        car = lax.fori_loop(0, Tc // SUBLANES, slab, (cre_s[...], cim_s[...]))
        cre_s[...] = car[0]
        cim_s[...] = car[1]
        y = _dot(xre_ref[...], cre_ref[0], 1, 0) + _dot(xim_ref[...], cim_ref[0], 1, 0)
        y_ref[...] = y + d_ref[...] * u

    return _call(
        body, name=name, rider=rider,
        out_shape=(jax.ShapeDtypeStruct((S, SSM_WIDTH), F32),
                   jax.ShapeDtypeStruct((S, SSM_HALVES * SSM_HALF_ST), F32),
                   jax.ShapeDtypeStruct((S, SSM_HALVES * SSM_HALF_ST), F32)),
        grid=(SSM_HALVES, nc),
        in_specs=[pl.BlockSpec((Tc, SSM_HALF_CH), lambda h, c: (c, u_blk0 + h)),
                  pl.BlockSpec((1, SSM_HALF_CH, SSM_HALF_ST), lambda h, c: (h, 0, 0)),
                  pl.BlockSpec((1, SSM_HALF_CH, SSM_HALF_ST), lambda h, c: (h, 0, 0)),
                  pl.BlockSpec((1, SSM_HALF_ST, SSM_HALF_CH), lambda h, c: (h, 0, 0)),
                  pl.BlockSpec((1, SSM_HALF_ST, SSM_HALF_CH), lambda h, c: (h, 0, 0)),
                  pl.BlockSpec((1, SSM_HALF_CH), lambda h, c: (0, h)),
                  pl.BlockSpec((1, 8, SUBLANES, SSM_HALF_ST), lambda h, c: (h, 0, 0, 0))],
        out_specs=(pl.BlockSpec((Tc, SSM_HALF_CH), lambda h, c: (c, h)),
                   pl.BlockSpec((Tc, SSM_HALF_ST), lambda h, c: (c, h)),
                   pl.BlockSpec((Tc, SSM_HALF_ST), lambda h, c: (c, h))),
        scratch_shapes=[pltpu.VMEM((SUBLANES, SSM_HALF_ST), F32), pltpu.VMEM((SUBLANES, SSM_HALF_ST), F32)],
        operands=(proj, bd_re, bd_im, cd_re, cd_imneg, d_skip, tab))


def _ssm_bwd(dy, proj, x_re, x_im, bd_re, bd_im, cd_re, cd_imneg, d_skip, tab, *, name, rider=None):
    S = proj.shape[0]
    Tc = min(SSM_CHUNK, S)
    nc = S // Tc
    u_blk0 = (3 * SB_WIDTH) // SSM_HALF_CH

    def body(dy_ref, u_ref, xre_ref, xim_ref, bre_ref, bim_ref, cre_ref, cim_ref, d_ref, tab_ref,
             du_ref, dbre_ref, dbim_ref, dcre_ref, dcim_ref, dd_ref, dlre_ref, dlim_ref,
             gre_s, gim_s, cre_s, cim_s):
        c = pl.program_id(1)

        @pl.when(c == 0)
        def _():
            cre_s[...] = jnp.zeros_like(cre_s)
            cim_s[...] = jnp.zeros_like(cim_s)
            dbre_ref[...] = jnp.zeros_like(dbre_ref)
            dbim_ref[...] = jnp.zeros_like(dbim_ref)
            dcre_ref[...] = jnp.zeros_like(dcre_ref)
            dcim_ref[...] = jnp.zeros_like(dcim_ref)
            dd_ref[...] = jnp.zeros_like(dd_ref)
            dlre_ref[...] = jnp.zeros_like(dlre_ref)
            dlim_ref[...] = jnp.zeros_like(dlim_ref)

        dy = dy_ref[...]
        dyb = dy.astype(BF16)
        u = u_ref[...]
        gre_s[...] = _dot(dyb, cre_ref[0], 1, 1)
        gim_s[...] = _dot(dyb, cim_ref[0], 1, 1)
        row = lax.broadcasted_iota(jnp.int32, (SUBLANES, SSM_HALF_ST), 0)
        nslab = Tc // SUBLANES

        def slab(kk, carry):
            car_re, car_im, acc_re, acc_im = carry
            k = nslab - 1 - kk
            rows = pl.ds(pl.multiple_of(k * SUBLANES, SUBLANES), SUBLANES)
            sre = gre_s[rows, :]
            sim = gim_s[rows, :]
            for n, d in enumerate((1, 2, 4)):
                pre, pim = tab_ref[0, 2 * n], tab_ref[0, 2 * n + 1]
                rre = pltpu.roll(sre, SUBLANES - d, 0)
                rim = pltpu.roll(sim, SUBLANES - d, 0)
                sre, sim = sre + (pre * rre + pim * rim), sim + (pre * rim - pim * rre)
            pre, pim = tab_ref[0, 6], tab_ref[0, 7]
            sre, sim = sre + (pre * car_re + pim * car_im), sim + (pre * car_im - pim * car_re)
            gre_s[rows, :] = sre
            gim_s[rows, :] = sim
            nre = jnp.where(row == SUBLANES - 1, car_re, pltpu.roll(sre, SUBLANES - 1, 0))
            nim = jnp.where(row == SUBLANES - 1, car_im, pltpu.roll(sim, SUBLANES - 1, 0))
            xr = xre_ref[rows, :]
            xi = xim_ref[rows, :]
            acc_re = acc_re + (nre * xr + nim * xi)
            acc_im = acc_im + (nim * xr - nre * xi)
            return (jnp.broadcast_to(sre[0:1, :], sre.shape), jnp.broadcast_to(sim[0:1, :], sim.shape), acc_re, acc_im)

        car = lax.fori_loop(0, nslab, slab, (cre_s[...], cim_s[...], dlre_ref[0], dlim_ref[0]))
        cre_s[...] = car[0]
        cim_s[...] = car[1]
        dlre_ref[0] = car[2]
        dlim_ref[0] = car[3]
        gre = gre_s[...].astype(BF16)
        gim = gim_s[...].astype(BF16)
        ub = u.astype(BF16)
        du = _dot(gre, bre_ref[0], 1, 1) + _dot(gim, bim_ref[0], 1, 1) + d_ref[...] * dy
        du_ref[...] = du.astype(BF16)
        dbre_ref[0] += _dot(ub, gre, 0, 0)
        dbim_ref[0] += _dot(ub, gim, 0, 0)
        dcre_ref[0] += _dot(xre_ref[...], dyb, 0, 0)
        dcim_ref[0] += _dot(xim_ref[...], dyb, 0, 0)
        dd_ref[...] += jnp.sum(dy * u, axis=0, keepdims=True)

    rev = lambda c: nc - 1 - c
    return _call(
        body, name=name, rider=rider,
        out_shape=(jax.ShapeDtypeStruct((S, SSM_WIDTH), BF16),
                   jax.ShapeDtypeStruct((SSM_HALVES, SSM_HALF_CH, SSM_HALF_ST), F32),
                   jax.ShapeDtypeStruct((SSM_HALVES, SSM_HALF_CH, SSM_HALF_ST), F32),
                   jax.ShapeDtypeStruct((SSM_HALVES, SSM_HALF_ST, SSM_HALF_CH), F32),
                   jax.ShapeDtypeStruct((SSM_HALVES, SSM_HALF_ST, SSM_HALF_CH), F32),
                   jax.ShapeDtypeStruct((1, SSM_WIDTH), F32),
                   jax.ShapeDtypeStruct((SSM_HALVES, SUBLANES, SSM_HALF_ST), F32),
                   jax.ShapeDtypeStruct((SSM_HALVES, SUBLANES, SSM_HALF_ST), F32)),
        grid=(SSM_HALVES, nc),
        in_specs=[pl.BlockSpec((Tc, SSM_HALF_CH), lambda h, c: (rev(c), h)),
                  pl.BlockSpec((Tc, SSM_HALF_CH), lambda h, c: (rev(c), u_blk0 + h)),
                  pl.BlockSpec((Tc, SSM_HALF_ST), lambda h, c: (rev(c), h)),
                  pl.BlockSpec((Tc, SSM_HALF_ST), lambda h, c: (rev(c), h)),
                  pl.BlockSpec((1, SSM_HALF_CH, SSM_HALF_ST), lambda h, c: (h, 0, 0)),
                  pl.BlockSpec((1, SSM_HALF_CH, SSM_HALF_ST), lambda h, c: (h, 0, 0)),
                  pl.BlockSpec((1, SSM_HALF_ST, SSM_HALF_CH), lambda h, c: (h, 0, 0)),
                  pl.BlockSpec((1, SSM_HALF_ST, SSM_HALF_CH), lambda h, c: (h, 0, 0)),
                  pl.BlockSpec((1, SSM_HALF_CH), lambda h, c: (0, h)),
                  pl.BlockSpec((1, 8, SUBLANES, SSM_HALF_ST), lambda h, c: (h, 0, 0, 0))],
        out_specs=(pl.BlockSpec((Tc, SSM_HALF_CH), lambda h, c: (rev(c), h)),
                   pl.BlockSpec((1, SSM_HALF_CH, SSM_HALF_ST), lambda h, c: (h, 0, 0)),
                   pl.BlockSpec((1, SSM_HALF_CH, SSM_HALF_ST), lambda h, c: (h, 0, 0)),
                   pl.BlockSpec((1, SSM_HALF_ST, SSM_HALF_CH), lambda h, c: (h, 0, 0)),
                   pl.BlockSpec((1, SSM_HALF_ST, SSM_HALF_CH), lambda h, c: (h, 0, 0)),
                   pl.BlockSpec((1, SSM_HALF_CH), lambda h, c: (0, h)),
                   pl.BlockSpec((1, SUBLANES, SSM_HALF_ST), lambda h, c: (h, 0, 0)),
                   pl.BlockSpec((1, SUBLANES, SSM_HALF_ST), lambda h, c: (h, 0, 0))),
        scratch_shapes=[pltpu.VMEM((Tc, SSM_HALF_ST), F32), pltpu.VMEM((Tc, SSM_HALF_ST), F32),
                        pltpu.VMEM((SUBLANES, SSM_HALF_ST), F32), pltpu.VMEM((SUBLANES, SSM_HALF_ST), F32)],
        operands=(dy, proj, x_re, x_im, bd_re, bd_im, cd_re, cd_imneg, d_skip, tab))


def _ssm_prepare(a_re, a_im, log_dt, b_re, b_im):
    dt = jnp.exp(log_dt)[:, None]
    mag = jnp.exp(a_re * dt)
    lre = mag * jnp.cos(a_im * dt)
    lim = mag * jnp.sin(a_im * dt)
    den = a_re * a_re + a_im * a_im
    fre = ((lre - 1.0) * a_re + lim * a_im) / den
    fim = (lim * a_re - (lre - 1.0) * a_im) / den
    bbre = fre[:, :, None] * b_re - fim[:, :, None] * b_im
    bbim = fre[:, :, None] * b_im + fim[:, :, None] * b_re
    return lre, lim, bbre, bbim


def _group_eye():
    return jnp.eye(SSM_GROUPS // SSM_HALVES, dtype=F32)


def _bd_from_bbar(bbar):
    gh = SSM_GROUPS // SSM_HALVES
    b = bbar.reshape(SSM_HALVES, gh, SSM_STATE, SSM_GROUP).transpose(0, 1, 3, 2)
    out = b[:, :, :, None, :] * _group_eye()[None, :, None, :, None]
    return out.reshape(SSM_HALVES, SSM_HALF_CH, SSM_HALF_ST)


def _bbar_from_bd(dbd):
    gh = SSM_GROUPS // SSM_HALVES
    d = dbd.reshape(SSM_HALVES, gh, SSM_GROUP, gh, SSM_STATE)
    d = jnp.sum(d * _group_eye()[None, :, None, :, None], axis=3)
    return d.transpose(0, 1, 3, 2).reshape(SSM_GROUPS, SSM_STATE, SSM_GROUP)


def _cd_from_c(cmat):
    gh = SSM_GROUPS // SSM_HALVES
    c = cmat.reshape(SSM_HALVES, gh, SSM_GROUP, SSM_STATE).transpose(0, 1, 3, 2)
    out = c[:, :, :, None, :] * _group_eye()[None, :, None, :, None]
    return out.reshape(SSM_HALVES, SSM_HALF_ST, SSM_HALF_CH)


def _c_from_cd(dcd):
    gh = SSM_GROUPS // SSM_HALVES
    d = dcd.reshape(SSM_HALVES, gh, SSM_STATE, gh, SSM_GROUP)
    d = jnp.sum(d * _group_eye()[None, :, None, :, None], axis=3)
    return d.transpose(0, 1, 3, 2).reshape(SSM_GROUPS, SSM_GROUP, SSM_STATE)


def _glu_fwd(y_pre, w_glu, b_glu, *, name):
    S, W = y_pre.shape
    tr = _row_tile(S)

    def body(y_ref, w_ref, b_ref, o_ref):
        yg = _gelu(y_ref[...])
        gl = _dot(yg, w_ref[...], 1, 0) + b_ref[...]
        o_ref[...] = (yg * _sigmoid(gl)).astype(BF16)

    row = pl.BlockSpec((tr, W), lambda i: (i, 0))
    return pl.pallas_call(
        body, name=name, out_shape=jax.ShapeDtypeStruct((S, W), BF16), grid=(S // tr,),
        in_specs=[row, pl.BlockSpec((W, W), lambda i: (0, 0)), pl.BlockSpec((1, W), lambda i: (0, 0))],
        out_specs=row, compiler_params=_cparams("parallel"),
    )(y_pre, w_glu, b_glu)


def _glu_bwd(y_pre, do, w_glu, b_glu, *, name):
    S, W = y_pre.shape
    tr = _row_tile(S)

    def body(y_ref, do_ref, w_ref, b_ref, dy_ref, dw_ref, db_ref):
        i = pl.program_id(0)
        yg, dyg_dy = _gelu_and_grad(y_ref[...])
        ygb = yg.astype(BF16)
        sg = _sigmoid(_dot(ygb, w_ref[...], 1, 0) + b_ref[...])
        do = do_ref[...]
        dgl = do * yg * sg * (1.0 - sg)
        dglb = dgl.astype(BF16)
        dyg = do * sg + _dot(dglb, w_ref[...], 1, 1)
        dy_ref[...] = dyg * dyg_dy
        dw = _dot(ygb, dglb, 0, 0)
        db = jnp.sum(dgl, axis=0, keepdims=True)

        @pl.when(i == 0)
        def _():
            dw_ref[...] = dw
            db_ref[...] = db

        @pl.when(i > 0)
        def _():
            dw_ref[...] += dw
            db_ref[...] += db

    row = pl.BlockSpec((tr, W), lambda i: (i, 0))
    full = pl.BlockSpec((W, W), lambda i: (0, 0))
    vec = pl.BlockSpec((1, W), lambda i: (0, 0))
    return pl.pallas_call(
        body, name=name,
        out_shape=(jax.ShapeDtypeStruct((S, W), F32), jax.ShapeDtypeStruct((W, W), F32), jax.ShapeDtypeStruct((1, W), F32)),
        grid=(S // tr,), in_specs=[row, row, full, vec], out_specs=(row, full, vec),
        compiler_params=_cparams("arbitrary"),
    )(y_pre, do, w_glu, b_glu)


GATE_COL0 = 3 * SB_WIDTH + SSM_WIDTH


def _merge_fwd(proj, o_attn, o_ssm, w_ba, w_bs, b_gate, *, name):
    S = proj.shape[0]
    D = D_MODEL
    tr = _pick(S, (256, 128, 64, 32, 16, 8))
    gb = GATE_COL0 // D

    def body(ga_ref, gs_ref, oa_ref, os_ref, wa_ref, ws_ref, ba_ref, bs_ref, m_ref):
        pa = _dot(oa_ref[...], wa_ref[...], 1, 0)
        ps = _dot(os_ref[...], ws_ref[...], 1, 0)
        sa = _sigmoid(ga_ref[...] + ba_ref[...])
        ss = _sigmoid(gs_ref[...] + bs_ref[...])
        m_ref[...] = (sa * pa + ss * ps).astype(BF16)

    return pl.pallas_call(
        body, name=name, out_shape=jax.ShapeDtypeStruct((S, D), BF16), grid=(S // tr,),
        in_specs=[pl.BlockSpec((tr, D), lambda i: (i, gb)), pl.BlockSpec((tr, D), lambda i: (i, gb + 1)),
                  pl.BlockSpec((tr, SB_WIDTH), lambda i: (i, 0)), pl.BlockSpec((tr, SSM_WIDTH), lambda i: (i, 0)),
                  pl.BlockSpec((SB_WIDTH, D), lambda i: (0, 0)), pl.BlockSpec((SSM_WIDTH, D), lambda i: (0, 0)),
                  pl.BlockSpec((1, D), lambda i: (0, 0)), pl.BlockSpec((1, D), lambda i: (0, 1))],
        out_specs=pl.BlockSpec((tr, D), lambda i: (i, 0)),
        compiler_params=_cparams("parallel"),
    )(proj, proj, o_attn, o_ssm, w_ba, w_bs, b_gate, b_gate)


def _merge_bwd(dmerged, proj, o_attn, o_ssm, w_ba, w_bs, b_gate, *, name):
    S = proj.shape[0]
    D = D_MODEL
    tr = _pick(S, (256, 128, 64, 32, 16, 8))
    gb = GATE_COL0 // D

    def body(dm_ref, ga_ref, gs_ref, oa_ref, os_ref, wa_ref, ws_ref, ba_ref, bs_ref,
             doa_ref, dos_ref, dg_ref, db_ref, dwa_ref, dws_ref):
        i = pl.program_id(0)
        dm = dm_ref[...]
        oa = oa_ref[...]
        osm = os_ref[...]
        pa = _dot(oa, wa_ref[...], 1, 0)
        ps = _dot(osm, ws_ref[...], 1, 0)
        sa = _sigmoid(ga_ref[...] + ba_ref[...])
        ss = _sigmoid(gs_ref[...] + bs_ref[...])
        dpa = (dm * sa).astype(BF16)
        dps = (dm * ss).astype(BF16)
        dga = dm * pa * sa * (1.0 - sa)
        dgs = dm * ps * ss * (1.0 - ss)
        dg_ref[:, :D] = dga.astype(BF16)
        dg_ref[:, D:] = dgs.astype(BF16)
        doa_ref[...] = _dot(dpa, wa_ref[...], 1, 1).astype(BF16)
        dos_ref[...] = _dot(dps, ws_ref[...], 1, 1)
        dwa = _dot(oa, dpa, 0, 0)
        dws = _dot(osm, dps, 0, 0)
        dba = jnp.sum(dga, axis=0, keepdims=True)
        dbs = jnp.sum(dgs, axis=0, keepdims=True)

        @pl.when(i == 0)
        def _():
            dwa_ref[...] = dwa
            dws_ref[...] = dws
            db_ref[:, :D] = dba
            db_ref[:, D:] = dbs

        @pl.when(i > 0)
        def _():
            dwa_ref[...] += dwa
            dws_ref[...] += dws
            db_ref[:, :D] += dba
            db_ref[:, D:] += dbs

    rowD = pl.BlockSpec((tr, D), lambda i: (i, 0))
    wspec = pl.BlockSpec((SB_WIDTH, D), lambda i: (0, 0))
    return pl.pallas_call(
        body, name=name,
        out_shape=(jax.ShapeDtypeStruct((S, SB_WIDTH), BF16), jax.ShapeDtypeStruct((S, SSM_WIDTH), F32),
                   jax.ShapeDtypeStruct((S, 2 * D), BF16), jax.ShapeDtypeStruct((1, 2 * D), F32),
                   jax.ShapeDtypeStruct((SB_WIDTH, D), F32), jax.ShapeDtypeStruct((SSM_WIDTH, D), F32)),
        grid=(S // tr,),
        in_specs=[rowD, pl.BlockSpec((tr, D), lambda i: (i, gb)), pl.BlockSpec((tr, D), lambda i: (i, gb + 1)),
                  pl.BlockSpec((tr, SB_WIDTH), lambda i: (i, 0)), pl.BlockSpec((tr, SSM_WIDTH), lambda i: (i, 0)),
                  wspec, wspec, pl.BlockSpec((1, D), lambda i: (0, 0)), pl.BlockSpec((1, D), lambda i: (0, 1))],
        out_specs=(pl.BlockSpec((tr, SB_WIDTH), lambda i: (i, 0)), pl.BlockSpec((tr, SSM_WIDTH), lambda i: (i, 0)),
                   pl.BlockSpec((tr, 2 * D), lambda i: (i, 0)), pl.BlockSpec((1, 2 * D), lambda i: (0, 0)),
                   wspec, wspec),
        compiler_params=_cparams("arbitrary"),
    )(dmerged, proj, proj, o_attn, o_ssm, w_ba, w_bs, b_gate, b_gate)


def _xattn_probs(q, k, h):
    cols = slice(h * XA_HEAD_DIM, (h + 1) * XA_HEAD_DIM)
    s = _dot(q[:, cols], k[:, cols], 1, 1) * (XA_HEAD_DIM ** -0.5)
    s = s - jnp.max(s, axis=-1, keepdims=True)
    e = jnp.exp(s)
    return e / jnp.sum(e, axis=-1, keepdims=True), cols


def _xattn_fwd(q2, k2, v2, *, name):
    S, D = q2.shape
    M = k2.shape[0]
    tr = _row_tile(S)

    def body(q_ref, k_ref, v_ref, o_ref):
        q = q_ref[...]
        k = k_ref[...]
        v = v_ref[...]
        for h in range(XA_HEADS):
            p, cols = _xattn_probs(q, k, h)
            o_ref[:, cols] = _dot(p, v[:, cols], 1, 0).astype(BF16)

    row = pl.BlockSpec((tr, D), lambda i: (i, 0))
    memb = pl.BlockSpec((M, D), lambda i: (0, 0))
    return pl.pallas_call(
        body, name=name, out_shape=jax.ShapeDtypeStruct((S, D), BF16), grid=(S // tr,),
        in_specs=[row, memb, memb], out_specs=row, compiler_params=_cparams("parallel"),
    )(q2, k2, v2)


def _xattn_bwd(q2, k2, v2, do2, *, name):
    S, D = q2.shape
    M = k2.shape[0]
    tr = _row_tile(S)
    scale = XA_HEAD_DIM ** -0.5

    def body(q_ref, k_ref, v_ref, do_ref, dq_ref, dk_ref, dv_ref):
        i = pl.program_id(0)

        @pl.when(i == 0)
        def _():
            dk_ref[...] = jnp.zeros_like(dk_ref)
            dv_ref[...] = jnp.zeros_like(dv_ref)

        q = q_ref[...]
        k = k_ref[...]
        v = v_ref[...]
        do = do_ref[...]
        for h in range(XA_HEADS):
            p, cols = _xattn_probs(q, k, h)
            dp = _dot(do[:, cols], v[:, cols], 1, 1)
            ds = (p * (dp - jnp.sum(dp * p, axis=-1, keepdims=True)) * scale).astype(BF16)
            dq_ref[:, cols] = _dot(ds, k[:, cols], 1, 0).astype(BF16)
            dk_ref[:, cols] += _dot(ds, q[:, cols], 0, 0)
            dv_ref[:, cols] += _dot(p, do[:, cols], 0, 0)

    row = pl.BlockSpec((tr, D), lambda i: (i, 0))
    memb = pl.BlockSpec((M, D), lambda i: (0, 0))
    return pl.pallas_call(
        body, name=name,
        out_shape=(jax.ShapeDtypeStruct((S, D), BF16), jax.ShapeDtypeStruct((M, D), F32), jax.ShapeDtypeStruct((M, D), F32)),
        grid=(S // tr,), in_specs=[row, memb, memb, row], out_specs=(row, memb, memb),
        compiler_params=_cparams("arbitrary"),
    )(q2, k2, v2, do2)


CONV_ROWS = 64
CONV_ROWS_FWD = 256


def _chunk(ref, c, rows):
    return ref[pl.ds(pl.multiple_of(c * rows, rows), rows), :]


def _rows_before(ref, c, rows):
    t0 = pl.multiple_of(jnp.maximum(c * rows - SUBLANES, 0), SUBLANES)
    return jnp.where(c > 0, ref[pl.ds(t0, SUBLANES), :], 0.0)


def _rows_after(ref, c, rows, n_chunks):
    t0 = pl.multiple_of(jnp.minimum((c + 1) * rows, n_chunks * rows - SUBLANES), SUBLANES)
    return jnp.where(c < n_chunks - 1, ref[pl.ds(t0, SUBLANES), :], 0.0)


def _shift_down(cur, before, d):
    out = pltpu.roll(cur, d, 0)
    r = lax.broadcasted_iota(jnp.int32, cur.shape, 0)
    for e in range(d):
        out = jnp.where(r == e, before[SUBLANES - d + e:SUBLANES - d + e + 1, :], out)
    return out


def _shift_up(cur, after, d):
    rows = cur.shape[0]
    out = pltpu.roll(cur, rows - d, 0)
    r = lax.broadcasted_iota(jnp.int32, cur.shape, 0)
    for e in range(d):
        out = jnp.where(r == rows - d + e, after[e:e + 1, :], out)
    return out


def _conv3(cur, before, w_ref, b_ref):
    return (w_ref[2:3, :] * cur + w_ref[1:2, :] * _shift_down(cur, before, 1)
            + w_ref[0:1, :] * _shift_down(cur, before, 2) + b_ref[...])


def _convgate_fwd(up_g, up_v, conv_w, conv_b, *, name):
    S, H = up_g.shape
    nb = H // LANES
    R = min(CONV_ROWS_FWD, S)
    n_chunks = S // R

    def body(g_ref, v_ref, wg_ref, wv_ref, bg_ref, bv_ref, a_ref):
        def chunk(c, _):
            cg = _conv3(_chunk(g_ref, c, R), _rows_before(g_ref, c, R), wg_ref, bg_ref)
            cv = _conv3(_chunk(v_ref, c, R), _rows_before(v_ref, c, R), wv_ref, bv_ref)
            a_ref[pl.ds(pl.multiple_of(c * R, R), R), :] = (_gelu(cg) * cv).astype(BF16)
            return 0

        lax.fori_loop(0, n_chunks, chunk, 0)

    col = lambda off: pl.BlockSpec((S, LANES), lambda j: (0, off + j))
    wcol = lambda off: pl.BlockSpec((3, LANES), lambda j: (0, off + j))
    bcol = lambda off: pl.BlockSpec((1, LANES), lambda j: (0, off + j))
    return pl.pallas_call(
        body, name=name, out_shape=jax.ShapeDtypeStruct((S, H), BF16), grid=(nb,),
        in_specs=[col(0), col(0), wcol(0), wcol(nb), bcol(0), bcol(nb)],
        out_specs=col(0), compiler_params=_cparams("parallel"),
    )(up_g, up_v, conv_w, conv_w, conv_b, conv_b)


def _convgate_bwd(up_g, up_v, da, conv_w, conv_b, *, name):
    S, H = up_g.shape
    nb = H // LANES
    R = min(CONV_ROWS, S)
    n_chunks = S // R

    def fold(a):
        return sum(a[r:r + SUBLANES] for r in range(0, a.shape[0], SUBLANES))

    def body(g_ref, v_ref, da_ref, wg_ref, wv_ref, bg_ref, bv_ref,
             dug_ref, duv_ref, dwg_ref, dwv_ref, dbg_ref, dbv_ref, dcg_s, dcv_s):
        def first_pass(c, acc):
            rows = pl.ds(pl.multiple_of(c * R, R), R)
            ug, uv = _chunk(g_ref, c, R), _chunk(v_ref, c, R)
            bg, bv = _rows_before(g_ref, c, R), _rows_before(v_ref, c, R)
            cg = _conv3(ug, bg, wg_ref, bg_ref)
            cv = _conv3(uv, bv, wv_ref, bv_ref)
            da = da_ref[rows, :]
            gl, dgl = _gelu_and_grad(cg)
            dcg = da * cv * dgl
            dcv = da * gl
            dcg_s[rows, :] = dcg
            dcv_s[rows, :] = dcv
            new = []
            for dc, u, before in ((dcg, ug, bg), (dcv, uv, bv)):
                new += [fold(dc * _shift_down(u, before, 2)), fold(dc * _shift_down(u, before, 1)), fold(dc * u), fold(dc)]
            return tuple(a + n for a, n in zip(acc, new))

        zero = jnp.zeros((SUBLANES, LANES), F32)
        acc = lax.fori_loop(0, n_chunks, first_pass, (zero,) * 8)
        total = [jnp.sum(a, axis=0, keepdims=True) for a in acc]
        for k, (dw_ref, db_ref) in enumerate(((dwg_ref, dbg_ref), (dwv_ref, dbv_ref))):
            dw_ref[0:1, :] = total[4 * k]
            dw_ref[1:2, :] = total[4 * k + 1]
            dw_ref[2:3, :] = total[4 * k + 2]
            db_ref[...] = total[4 * k + 3]

        def second_pass(c, _):
            rows = pl.ds(pl.multiple_of(c * R, R), R)
            for dc_s, w_ref, du_ref in ((dcg_s, wg_ref, dug_ref), (dcv_s, wv_ref, duv_ref)):
                cur, after = _chunk(dc_s, c, R), _rows_after(dc_s, c, R, n_chunks)
                du = w_ref[2:3, :] * cur + w_ref[1:2, :] * _shift_up(cur, after, 1) + w_ref[0:1, :] * _shift_up(cur, after, 2)
                du_ref[rows, :] = du.astype(BF16)
            return 0

        lax.fori_loop(0, n_chunks, second_pass, 0)

    col = lambda off: pl.BlockSpec((S, LANES), lambda j: (0, off + j))
    wcol = lambda off: pl.BlockSpec((3, LANES), lambda j: (0, off + j))
    bcol = lambda off: pl.BlockSpec((1, LANES), lambda j: (0, off + j))
    return pl.pallas_call(
        body, name=name,
        out_shape=(jax.ShapeDtypeStruct((S, H), BF16), jax.ShapeDtypeStruct((S, H), BF16),
                   jax.ShapeDtypeStruct((3, H), F32), jax.ShapeDtypeStruct((3, H), F32),
                   jax.ShapeDtypeStruct((1, H), F32), jax.ShapeDtypeStruct((1, H), F32)),
        grid=(nb,),
        in_specs=[col(0), col(0), col(0), wcol(0), wcol(nb), bcol(0), bcol(nb)],
        out_specs=(col(0), col(0), wcol(0), wcol(0), bcol(0), bcol(0)),
        scratch_shapes=[pltpu.VMEM((S, LANES), F32), pltpu.VMEM((S, LANES), F32)],
        compiler_params=_cparams("parallel"),
    )(up_g, up_v, da, conv_w, conv_w, conv_b, conv_b)


def _local_step(x, mem, target, w_in, late_wire, P, core):
    mm = _matmul
    h1 = _rms_fwd(x, P["norm_mix_pre"], name="rms_mix_pre")
    proj = mm(h1, w_in, name="mm_in")
    (o_attn, sb_tot, sb_first), late_wire = _sb_fwd(proj, name="sb_fwd", rider=_fill_xy(late_wire))

    ssm_prep = lambda *a: _ssm_prepare(*a)
    (lam_re, lam_im, bb_re, bb_im), prep_vjp = jax.vjp(
        ssm_prep, P["ssm_a_re"], P["ssm_a_im"], P["ssm_log_dt"], P["ssm_b_re"], P["ssm_b_im"])
    tab_f, tab_b = _ssm_tables(lam_re, lam_im)
    bd_re = _bd_from_bbar(bb_re).astype(BF16)
    bd_im = _bd_from_bbar(bb_im).astype(BF16)
    cd_re = _cd_from_c(P["ssm_c_re"]).astype(BF16)
    cd_imneg = _cd_from_c(-P["ssm_c_im"]).astype(BF16)
    (y_pre, x_re, x_im), late_wire = _ssm_fwd(proj, bd_re, bd_im, cd_re, cd_imneg, P["ssm_d"], tab_f,
                                              name="ssm_fwd", rider=_fill_c(late_wire))
    W = _weights_from_wire(dict(zip(LATE, late_wire)))
    W["w_in"] = w_in
    o_ssm = _glu_fwd(y_pre, W["ssm_w_glu"], P["ssm_b_glu"], name="glu_fwd")

    merged = _merge_fwd(proj, o_attn, o_ssm, W["w_branch_attn"], W["w_branch_ssm"], P["b_gate"], name="merge_fwd")
    mo = mm(merged, W["w_out"], name="mm_out")
    x1, h2 = _resnorm_norm(x, mo, P["norm_mix_post"], P["norm_xa_pre"], name="resnorm_1")

    mem_n = _rms_fwd(mem, P["norm_mem"], name="rms_mem")
    q2 = mm(h2, W["xa_wq"], out_dtype=BF16, name="mm_xq")
    k2 = mm(mem_n, W["xa_wk"], out_dtype=BF16, name="mm_xk")
    v2 = mm(mem_n, W["xa_wv"], out_dtype=BF16, name="mm_xv")
    o2 = _xattn_fwd(q2, k2, v2, name="xattn_fwd")
    xa = mm(o2, W["xa_wo"], name="mm_xo")
    x2, h3 = _resnorm_norm(x1, xa, P["norm_xa_post"], P["norm_ffn_pre"], name="resnorm_2")

    half = N_DEV // 2
    up_g = mm(h3, W["ffn_w_up"], n_blocks=half, name="mm_up_g")
    up_v = mm(h3, W["ffn_w_up"], b_block0=half, name="mm_up_v")
    act = _convgate_fwd(up_g, up_v, W["ffn_conv_w"], P["ffn_conv_b"], name="convgate_fwd")
    f = mm(act, W["ffn_w_down"], name="mm_down")
    loss, dy, df, dg_ffn_post = _final_loss(x2, f, P["norm_ffn_post"], target, name="final_loss")

    G = {"norm_ffn_post": dg_ffn_post}
    dact = mm(df, W["ffn_w_down"], tb=True, name="mm_down_dx")
    G["ffn_w_down"] = mm(act, df, ta=True, name="mm_down_dw")
    dug, duv, dwg, dwv, dbg, dbv = _convgate_bwd(up_g, up_v, dact, W["ffn_conv_w"], P["ffn_conv_b"], name="convgate_bwd")
    G["ffn_conv_w"] = jnp.concatenate([dwg, dwv], axis=1)
    G["ffn_conv_b"] = jnp.concatenate([dbg, dbv], axis=1)
    dh3 = mm(dug, W["ffn_w_up"], tb=True, n_blocks=half, name="mm_up_g_dx")
    dh3 = mm(duv, W["ffn_w_up"], tb=True, b_block0=half, acc_in=dh3, name="mm_up_v_dx")
    dw_up = mm(h3, dug, ta=True, out_into=lax.empty(W["ffn_w_up"].shape, F32), name="mm_up_g_dw")
    G["ffn_w_up"] = mm(h3, duv, ta=True, out_into=dw_up, out_block0=half, name="mm_up_v_dw")
    blocks = {n: _grad_blocks(n, G[n]) for n in REDUCE_FFN}
    (dx2, dxa, G["norm_ffn_pre"], G["norm_xa_post"]), from_core = _norm_bwd_pair(
        dy, dh3, x2, P["norm_ffn_pre"], xa, P["norm_xa_post"], name="norm_bwd_3",
        rider=_send_c([blocks[n] for n in REDUCE_FFN]))
    pair = {n: _pair_sum(blocks[n], r, core, name="pair_sum_" + n) for n, r in zip(REDUCE_FFN, from_core)}

    G["xa_wo"] = mm(o2, dxa, ta=True, name="mm_xo_dw")
    do2 = mm(dxa, W["xa_wo"], tb=True, out_dtype=BF16, name="mm_xo_dx")
    dq2, dk2, dv2 = _xattn_bwd(q2, k2, v2, do2, name="xattn_bwd")
    G["xa_wq"] = mm(h2, dq2, ta=True, name="mm_xq_dw")
    dh2 = mm(dq2, W["xa_wq"], tb=True, name="mm_xq_dx")
    G["xa_wk"] = mm(mem_n, dk2, ta=True, name="mm_xk_dw")
    G["xa_wv"] = mm(mem_n, dv2, ta=True, name="mm_xv_dw")
    dmem_n = jnp.concatenate([dk2, dv2], axis=1)
    wkv = jnp.concatenate([W["xa_wk"], W["xa_wv"]], axis=1)
    dmem = mm(dmem_n, wkv, tb=True, name="mm_xkv_dx")
    _, G["norm_mem"] = _norm_bwd_single(None, dmem, mem, P["norm_mem"], name="norm_bwd_mem")
    (dx1, dmo, G["norm_xa_pre"], G["norm_mix_post"]), _ = _norm_bwd_pair(
        dx2, dh2, x1, P["norm_xa_pre"], mo, P["norm_mix_post"], name="norm_bwd_2")

    G["w_out"] = mm(merged, dmo, ta=True, name="mm_out_dw")
    dmerged = mm(dmo, W["w_out"], tb=True, name="mm_out_dx")
    do_attn, do_ssm, dgate, G["b_gate"], G["w_branch_attn"], G["w_branch_ssm"] = _merge_bwd(
        dmerged, proj, o_attn, o_ssm, W["w_branch_attn"], W["w_branch_ssm"], P["b_gate"], name="merge_bwd")
    dy_pre, G["ssm_w_glu"], G["ssm_b_glu"] = _glu_bwd(y_pre, do_ssm, W["ssm_w_glu"], P["ssm_b_glu"], name="glu_bwd")
    blocks.update({n: _grad_blocks(n, G[n]) for n in REDUCE_MID})
    (du, dbd_re, dbd_im, dcd_re, dcd_imneg, G["ssm_d"], dl_re, dl_im), from_core = _ssm_bwd(
        dy_pre, proj, x_re, x_im, bd_re, bd_im, cd_re, cd_imneg, P["ssm_d"], tab_b, name="ssm_bwd",
        rider=_send_c([blocks[n] for n in REDUCE_MID]))
    pair.update({n: _pair_sum(blocks[n], r, core, name="pair_sum_" + n) for n, r in zip(REDUCE_MID, from_core)})
    G["ssm_c_re"] = _c_from_cd(dcd_re)
    G["ssm_c_im"] = -_c_from_cd(dcd_imneg)
    dlam_re = jnp.sum(dl_re, axis=1).reshape(SSM_GROUPS, SSM_STATE)
    dlam_im = jnp.sum(dl_im, axis=1).reshape(SSM_GROUPS, SSM_STATE)
    (G["ssm_a_re"], G["ssm_a_im"], G["ssm_log_dt"], G["ssm_b_re"], G["ssm_b_im"]) = prep_vjp(
        (dlam_re, dlam_im, _bbar_from_bd(dbd_re), _bbar_from_bd(dbd_im)))
    early = REDUCE_FFN + REDUCE_MID
    (dq, dk, dv), from_chips = _sb_bwd(proj, sb_tot, sb_first, do_attn, name="sb_bwd",
                                       rider=_scatter_xy([pair[n] for n in early]))
    reduced = {n: (pair[n], parts) for n, parts in zip(early, from_chips)}
    dproj = jnp.concatenate([dq, dk, dv, du, dgate], axis=1)
    G["w_in"] = mm(h1, dproj, ta=True, out_cb=W["w_in"].shape[2], name="mm_in_dw")
    dh1 = mm(dproj, W["w_in"], tb=True, name="mm_in_dx")
    grad_x, G["norm_mix_pre"] = _norm_bwd_single(dx1, dh1, x, P["norm_mix_pre"], name="norm_bwd_1")
    g_in = _grad_blocks("w_in", G["w_in"])
    from_core, = _send_c([g_in]).run(name="reduce_in_c")
    pair_in = _pair_sum(g_in, from_core, core, name="pair_sum_w_in")
    from_chips, = _scatter_xy([pair_in]).run(name="reduce_in_xy")
    reduced["w_in"] = (pair_in, from_chips)
    return loss, grad_x, G, reduced


MESH = pl.DeviceIdType.MESH
_HBM = pl.BlockSpec(memory_space=pl.ANY)
N_XY = 4
N_XY_PEERS = 3


def _xy_peers(x, y):
    return [(1 - x, y), (x, 1 - y), (1 - x, 1 - y)]


class _Exchange:
    def __init__(self, arrays, out_shapes, plan, n_copies, alias):
        self.arrays = list(arrays)
        self.out_shapes = list(out_shapes)
        self.plan = plan
        self.n_copies = n_copies
        self.alias = alias

    @property
    def n(self):
        return len(self.arrays)

    def sems(self):
        shape = (self.n, self.n_copies)
        return [pltpu.SemaphoreType.DMA(shape), pltpu.SemaphoreType.DMA(shape)]

    def _copies(self, ins, outs, send_sems, recv_sems):
        x, y, c = lax.axis_index("x"), lax.axis_index("y"), lax.axis_index("c")
        sends, lands = [], []
        for k in range(self.n):
            for j, (src, dst, dev, land) in enumerate(self.plan(k, ins[k], outs[k], x, y, c)):
                sems = dict(send_sem=send_sems.at[k, j], recv_sem=recv_sems.at[k, j], device_id=dev, device_id_type=MESH)
                sends.append(pltpu.make_async_remote_copy(src_ref=src, dst_ref=dst, **sems))
                lands.append(pltpu.make_async_remote_copy(src_ref=src, dst_ref=land, **sems))
        return sends, lands

    def start(self, ins, outs, send_sems, recv_sems):
        for cp in self._copies(ins, outs, send_sems, recv_sems)[0]:
            cp.start()

    def finish(self, ins, outs, send_sems, recv_sems):
        sends, lands = self._copies(ins, outs, send_sems, recv_sems)
        for cp in lands:
            cp.wait_recv()
        for cp in sends:
            cp.wait_send()

    def run(self, *, name):
        n = self.n

        def body(*refs):
            parts = (refs[:n], refs[n:2 * n], refs[2 * n], refs[2 * n + 1])
            self.start(*parts)
            self.finish(*parts)

        return pl.pallas_call(
            body, name=name, out_shape=tuple(self.out_shapes),
            in_specs=[_HBM] * n, out_specs=tuple([_HBM] * n),
            input_output_aliases={k: k for k in range(n)} if self.alias else {},
            scratch_shapes=self.sems(),
        )(*self.arrays)


def _call(host_body, *, name, grid, in_specs, out_specs, out_shape, scratch_shapes, operands, rider=None):
    out_specs, out_shape = tuple(out_specs), tuple(out_shape)
    if rider is None:
        res = pl.pallas_call(
            host_body, name=name, grid=grid, in_specs=list(in_specs), out_specs=out_specs, out_shape=out_shape,
            scratch_shapes=list(scratch_shapes), compiler_params=_cparams(*["arbitrary"] * len(grid)),
        )(*operands)
        return tuple(res), None
    n, n_in, n_out, n_scr = rider.n, len(in_specs), len(out_specs), len(scratch_shapes)

    def body(*refs):
        pos = [0]

        def take(count):
            pos[0] += count
            return refs[pos[0] - count:pos[0]]

        h_in, r_in, h_out, r_out, h_scr = take(n_in), take(n), take(n_out), take(n), take(n_scr)
        send_sems, recv_sems = take(2)
        ids = [pl.program_id(a) for a in range(len(grid))]
        first = functools.reduce(jnp.logical_and, [i == 0 for i in ids])
        last = functools.reduce(jnp.logical_and, [i == g - 1 for i, g in zip(ids, grid)])

        @pl.when(first)
        def _():
            rider.start(r_in, r_out, send_sems, recv_sems)

        host_body(*h_in, *h_out, *h_scr)

        @pl.when(last)
        def _():
            rider.finish(r_in, r_out, send_sems, recv_sems)

    res = pl.pallas_call(
        body, name=name, grid=grid,
        in_specs=list(in_specs) + [_HBM] * n, out_specs=out_specs + tuple([_HBM] * n),
        out_shape=out_shape + tuple(rider.out_shapes),
        input_output_aliases={n_in + k: n_out + k for k in range(n)} if rider.alias else {},
        scratch_shapes=list(scratch_shapes) + rider.sems(),
        compiler_params=_cparams(*["arbitrary"] * len(grid)),
    )(*operands, *rider.arrays)
    return tuple(res[:n_out]), list(res[n_out:])


def _same(arrays):
    return [jax.ShapeDtypeStruct(a.shape, a.dtype) for a in arrays]


def _fill_xy(bufs):
    def plan(k, src, dst, x, y, c):
        mine = 2 * x + y
        return [(src.at[mine, c], dst.at[mine, c], (px, py, c), dst.at[2 * px + py, c]) for px, py in _xy_peers(x, y)]

    return _Exchange(bufs, _same(bufs), plan, N_XY_PEERS, alias=True)


def _fill_c(bufs):
    def plan(k, src, dst, x, y, c):
        return [(src.at[:, c], dst.at[:, c], (x, y, 1 - c), dst.at[:, 1 - c])]

    return _Exchange(bufs, _same(bufs), plan, 1, alias=True)


def _send_c(srcs):
    def plan(k, src, dst, x, y, c):
        return [(src.at[:, 1 - c], dst, (x, y, 1 - c), dst)]

    outs = [jax.ShapeDtypeStruct(a.shape[:1] + a.shape[2:], a.dtype) for a in srcs]
    return _Exchange(srcs, outs, plan, 1, alias=False)


def _scatter_xy(srcs):
    def plan(k, src, dst, x, y, c):
        return [(src.at[2 * px + py], dst.at[j], (px, py, c), dst.at[j]) for j, (px, py) in enumerate(_xy_peers(x, y))]

    outs = [jax.ShapeDtypeStruct((N_XY_PEERS,) + a.shape[1:], a.dtype) for a in srcs]
    return _Exchange(srcs, outs, plan, N_XY_PEERS, alias=False)


PACK_COLS = 1024
WIRE_DTYPE = BF16


def _pair_sum(g8, recv, core, *, name):
    n, _, R, C = g8.shape
    tr = _pick(R, (128, 64, 32, 16, 8))

    def body(core_ref, a_ref, b_ref, o_ref):
        o_ref[...] = (a_ref[0] + b_ref[...]).astype(WIRE_DTYPE)

    return pl.pallas_call(
        body, name=name, out_shape=jax.ShapeDtypeStruct((n, R, C), WIRE_DTYPE),
        grid_spec=pltpu.PrefetchScalarGridSpec(
            num_scalar_prefetch=1, grid=(n, R // tr),
            in_specs=[pl.BlockSpec((1, 1, tr, C), lambda s, i, core_ref: (s, core_ref[0], i, 0)),
                      pl.BlockSpec((1, tr, C), lambda s, i, core_ref: (s, i, 0))],
            out_specs=pl.BlockSpec((1, tr, C), lambda s, i, core_ref: (s, i, 0))),
        compiler_params=_cparams("parallel", "parallel"),
    )(core, g8, recv)


def _adamw_math(w, g, m, v):
    m = ADAM_B1 * m + (1.0 - ADAM_B1) * g
    v = ADAM_B2 * v + (1.0 - ADAM_B2) * (g * g)
    m_hat = m / (1.0 - ADAM_B1 ** ADAM_STEP)
    v_hat = v / (1.0 - ADAM_B2 ** ADAM_STEP)
    delta = -ADAM_LR * (m_hat / (jnp.sqrt(v_hat) + ADAM_EPS) + ADAM_WD * w)
    return delta, m, v


def _reduce_adamw(parts, w, m, v, *, own=None, own_slot=None, name):
    n, R, C = parts.shape
    tr = _pick(R, (128, 64, 32, 16, 8))
    has_own = own is not None

    def body(*refs):
        if has_own:
            _, own_ref, parts_ref, w_ref, m_ref, v_ref, g_ref, d_ref, nm_ref, nv_ref = refs
            g = own_ref[0].astype(F32)
            first = 0
        else:
            parts_ref, w_ref, m_ref, v_ref, g_ref, d_ref, nm_ref, nv_ref = refs
            g = parts_ref[0]
            first = 1
        for k in range(first, n):
            g = g + parts_ref[k].astype(F32)
        g_ref[...] = g
        d_ref[...], nm_ref[...], nv_ref[...] = _adamw_math(w_ref[...], g, m_ref[...], v_ref[...])

    out = jax.ShapeDtypeStruct((R, C), F32)
    if has_own:
        row = pl.BlockSpec((tr, C), lambda i, s: (i, 0))
        return pl.pallas_call(
            body, name=name, out_shape=(out, out, out, out),
            grid_spec=pltpu.PrefetchScalarGridSpec(
                num_scalar_prefetch=1, grid=(R // tr,),
                in_specs=[pl.BlockSpec((1, tr, C), lambda i, s: (s[0], i, 0)),
                          pl.BlockSpec((n, tr, C), lambda i, s: (0, i, 0)), row, row, row],
                out_specs=(row, row, row, row)),
            compiler_params=_cparams("parallel"),
        )(own_slot, own, parts, w, m, v)
    row = pl.BlockSpec((tr, C), lambda i: (i, 0))
    return pl.pallas_call(
        body, name=name, out_shape=(out, out, out, out), grid=(R // tr,),
        in_specs=[pl.BlockSpec((n, tr, C), lambda i: (0, i, 0)), row, row, row],
        out_specs=(row, row, row, row), compiler_params=_cparams("parallel"),
    )(parts, w, m, v)


SHARDED = (("w_in", (1024, 4096), 1), ("ssm_w_glu", (512, 512), 0), ("w_branch_attn", (512, 1024), 1),
           ("w_branch_ssm", (512, 1024), 1), ("w_out", (1024, 1024), 0), ("xa_wq", (1024, 1024), 0),
           ("xa_wk", (1024, 1024), 0), ("xa_wv", (1024, 1024), 0), ("xa_wo", (1024, 1024), 0),
           ("ffn_w_up", (1024, 5632), 1), ("ffn_conv_w", (3, 5632), 1), ("ffn_w_down", (2816, 1024), 0))
REPLICATED = (("norm_mix_pre", (1024,)), ("norm_mix_post", (1024,)), ("b_gate", (2048,)), ("ssm_a_re", (32, 64)),
              ("ssm_a_im", (32, 64)), ("ssm_log_dt", (32,)), ("ssm_b_re", (32, 64, 16)), ("ssm_b_im", (32, 64, 16)),
              ("ssm_c_re", (32, 16, 64)), ("ssm_c_im", (32, 16, 64)), ("ssm_d", (512,)), ("ssm_b_glu", (512,)),
              ("norm_xa_pre", (1024,)), ("norm_xa_post", (1024,)), ("norm_mem", (1024,)), ("norm_ffn_pre", (1024,)),
              ("norm_ffn_post", (1024,)), ("ffn_conv_b", (5632,)))
PARAM_ORDER = ("norm_mix_pre", "norm_mix_post", "w_in", "b_gate", "ssm_a_re", "ssm_a_im", "ssm_log_dt", "ssm_b_re",
               "ssm_b_im", "ssm_c_re", "ssm_c_im", "ssm_d", "ssm_w_glu", "ssm_b_glu", "w_branch_attn", "w_branch_ssm",
               "w_out", "norm_xa_pre", "norm_xa_post", "norm_mem", "xa_wq", "xa_wk", "xa_wv", "xa_wo", "norm_ffn_pre",
               "norm_ffn_post", "ffn_w_up", "ffn_conv_w", "ffn_conv_b", "ffn_w_down")
SMALL_ROWS = 160
FF_LOCAL = 2 * D_FF // N_DEV
FF_LOCAL_PAD = 768
FF_PAD = (N_DEV // 2) * FF_LOCAL_PAD


def _local_shape(shape, axis):
    return tuple(s // N_DEV if a == axis else s for a, s in enumerate(shape))


def _pad_cols(a, width):
    return jnp.pad(a, [(0, 0)] * (a.ndim - 1) + [(0, width - a.shape[-1])])


def _blocks_to_cols(a8):
    return a8.transpose(1, 0, 2).reshape(a8.shape[1], N_DEV * a8.shape[2])


def _cols_to_blocks(a, cb):
    return a.reshape(a.shape[0], N_DEV, cb).transpose(1, 0, 2)


FF_PADDED = ("ffn_w_up", "ffn_conv_w")
LATE = tuple(n for n, _, _ in SHARDED if n != "w_in")
REDUCE_FFN = ("ffn_w_up", "ffn_conv_w", "ffn_w_down")
REDUCE_MID = ("xa_wo", "xa_wq", "xa_wk", "xa_wv", "w_out", "w_branch_attn", "w_branch_ssm", "ssm_w_glu")
SHARD_AXIS = {n: ax for n, _, ax in SHARDED}
FULL_SHAPE = {n: s for n, s, _ in SHARDED}


def _as_local(n, a):
    return _pad_cols(a, FF_LOCAL_PAD) if n in FF_PADDED else a


def _weights_from_wire(wire):
    full = {n: b.reshape((N_DEV,) + b.shape[2:]) for n, b in wire.items()}
    W = {n: a.reshape(FULL_SHAPE[n]) if SHARD_AXIS[n] == 0 else a for n, a in full.items()}
    for n in ("w_branch_attn", "w_branch_ssm", "ffn_conv_w"):
        W[n] = _blocks_to_cols(full[n])
    W["ffn_w_down"] = jnp.pad(W["ffn_w_down"].reshape(N_DEV // 2, FF_LOCAL, D_MODEL),
                              ((0, 0), (0, FF_LOCAL_PAD - FF_LOCAL), (0, 0))).reshape(FF_PAD, D_MODEL)
    return W


def _grad_blocks(n, g):
    if n in ("w_branch_attn", "w_branch_ssm"):
        g = _cols_to_blocks(g, D_MODEL // N_DEV)
    elif n == "ffn_conv_w":
        g = _cols_to_blocks(g, FF_LOCAL_PAD)
    elif n == "ffn_w_down":
        g = g.reshape(N_DEV // 2, FF_LOCAL_PAD, D_MODEL)[:, :FF_LOCAL]
    local = _local_shape(FULL_SHAPE[n], SHARD_AXIS[n])
    if n in FF_PADDED:
        local = local[:-1] + (FF_LOCAL_PAD,)
    return g.reshape((N_XY, 2) + local)


def _pack_small(d):
    flat = jnp.concatenate([d[n].reshape(-1) for n, _ in REPLICATED])
    return _pad_cols(flat, SMALL_ROWS * PACK_COLS).reshape(SMALL_ROWS, PACK_COLS)


def _unpack_small(buf):
    flat = buf.reshape(-1)
    out, off = {}, 0
    for n, shape in REPLICATED:
        size = math.prod(shape)
        out[n] = flat[off:off + size]
        off += size
    return out


def kernel(x, mem, norm_mix_pre, norm_mix_post, w_in, b_gate, ssm_a_re, ssm_a_im, ssm_log_dt, ssm_b_re, ssm_b_im, ssm_c_re, ssm_c_im, ssm_d, ssm_w_glu, ssm_b_glu, w_branch_attn, w_branch_ssm, w_out, norm_xa_pre, norm_xa_post, norm_mem, xa_wq, xa_wk, xa_wv, xa_wo, norm_ffn_pre, norm_ffn_post, ffn_w_up, ffn_conv_w, ffn_conv_b, ffn_w_down, loss_target, m_norm_mix_pre, m_norm_mix_post, m_w_in, m_b_gate, m_ssm_a_re, m_ssm_a_im, m_ssm_log_dt, m_ssm_b_re, m_ssm_b_im, m_ssm_c_re, m_ssm_c_im, m_ssm_d, m_ssm_w_glu, m_ssm_b_glu, m_w_branch_attn, m_w_branch_ssm, m_w_out, m_norm_xa_pre, m_norm_xa_post, m_norm_mem, m_xa_wq, m_xa_wk, m_xa_wv, m_xa_wo, m_norm_ffn_pre, m_norm_ffn_post, m_ffn_w_up, m_ffn_conv_w, m_ffn_conv_b, m_ffn_w_down, v_norm_mix_pre, v_norm_mix_post, v_w_in, v_b_gate, v_ssm_a_re, v_ssm_a_im, v_ssm_log_dt, v_ssm_b_re, v_ssm_b_im, v_ssm_c_re, v_ssm_c_im, v_ssm_d, v_ssm_w_glu, v_ssm_b_glu, v_w_branch_attn, v_w_branch_ssm, v_w_out, v_norm_xa_pre, v_norm_xa_post, v_norm_mem, v_xa_wq, v_xa_wk, v_xa_wv, v_xa_wo, v_norm_ffn_pre, v_norm_ffn_post, v_ffn_w_up, v_ffn_conv_w, v_ffn_conv_b, v_ffn_w_down):
    args = dict(locals())
    w_loc = {n: args[n][0] for n in PARAM_ORDER}
    m_loc = {n: args["m_" + n][0] for n in PARAM_ORDER}
    v_loc = {n: args["v_" + n][0] for n in PARAM_ORDER}
    core_i = lax.axis_index("c")
    chip_i = 2 * lax.axis_index("x") + lax.axis_index("y")
    core = core_i.astype(jnp.int32).reshape(1)
    chip = chip_i.astype(jnp.int32).reshape(1)

    def in_place(a):
        buf = lax.empty((N_XY, 2) + a.shape, a.dtype)
        return lax.dynamic_update_slice(buf, a[None, None], (chip_i, core_i) + (0,) * a.ndim)

    as_wire = lambda n: in_place(_as_local(n, w_loc[n]).astype(F32 if n == "ffn_conv_w" else BF16))
    wire_in = _fill_c(_fill_xy([as_wire("w_in")]).run(name="gather_in_xy")).run(name="gather_in_c")[0]
    w_in_full = wire_in.reshape((N_DEV,) + wire_in.shape[2:])

    P = {}
    for n, shape in REPLICATED:
        P[n] = w_loc[n] if len(shape) > 1 or n == "ssm_log_dt" else w_loc[n].reshape(1, -1)
    P["ffn_conv_b"] = _pad_cols(w_loc["ffn_conv_b"].reshape(N_DEV, FF_LOCAL), FF_LOCAL_PAD).reshape(1, 2 * FF_PAD)

    loss, grad_x, G, reduced = _local_step(x[0], mem[0], loss_target[0], w_in_full, [as_wire(n) for n in LATE], P, core)
    loss = lax.psum(loss[0, 0], ("x", "y", "c"))

    big_out = {}
    for n, (own, parts) in reduced.items():
        res = _reduce_adamw(parts, _as_local(n, w_loc[n]), _as_local(n, m_loc[n]), _as_local(n, v_loc[n]),
                            own=own, own_slot=chip, name="adamw_" + n)
        big_out[n] = [r[:, :FF_LOCAL] if n in FF_PADDED else r for r in res]

    G["ffn_conv_b"] = G["ffn_conv_b"].reshape(N_DEV, FF_LOCAL_PAD)[:, :FF_LOCAL]
    parts, = _fill_c(_fill_xy([in_place(_pack_small(G))]).run(name="gather_g_xy")).run(name="gather_g_c")
    parts = parts.reshape((N_DEV,) + parts.shape[2:])
    small_out = _reduce_adamw(parts, _pack_small(w_loc), _pack_small(m_loc), _pack_small(v_loc), name="adamw_replicated")
    small_out = [_unpack_small(b) for b in small_out]

    outs = [loss, grad_x[None]]
    for k in range(4):
        for n in PARAM_ORDER:
            src = big_out[n][k] if n in big_out else small_out[k][n]
            outs.append(src.reshape(args[n].shape))
    return tuple(outs)
```

```python
import functools
import math

import jax
import jax.numpy as jnp
from jax import lax
from jax.experimental import pallas as pl
from jax.experimental.pallas import tpu as pltpu

F32 = jnp.float32
BF16 = jnp.bfloat16

D_MODEL = 1024
SB_HEADS = 8
SB_HEAD_DIM = 64
SB_WIDTH = 512
SSM_WIDTH = 512
SSM_GROUP = 16
SSM_GROUPS = 32
SSM_STATE = 64
XA_HEADS = 4
XA_HEAD_DIM = 256
D_FF = 2816
RMS_EPS = 1e-6
IN_WIDTH = 4096
N_DEV = 8

ADAM_LR = 0.001
ADAM_B1 = 0.9
ADAM_B2 = 0.999
ADAM_EPS = 1e-08
ADAM_WD = 0.01
ADAM_STEP = 10

LANES = 128
SUBLANES = 8
VMEM_LIMIT = 48 * 1024 * 1024

_GELU_C = math.sqrt(2.0 / math.pi)


def _cparams(*sem):
    return pltpu.CompilerParams(dimension_semantics=sem, vmem_limit_bytes=VMEM_LIMIT)


def _pick(n, cands):
    for c in cands:
        if n % c == 0:
            return c
    return n


def _gelu(x):
    return 0.5 * x * (1.0 + jnp.tanh(_GELU_C * (x + 0.044715 * x * x * x)))


def _gelu_and_grad(x):
    t = jnp.tanh(_GELU_C * (x + 0.044715 * x * x * x))
    g = 0.5 * x * (1.0 + t)
    dg = 0.5 * (1.0 + t) + 0.5 * x * (1.0 - t * t) * _GELU_C * (1.0 + 3.0 * 0.044715 * x * x)
    return g, dg


def _sigmoid(x):
    return 1.0 / (1.0 + jnp.exp(-x))


def _dot(a, b, ca, cb):
    return lax.dot_general(a.astype(BF16), b.astype(BF16), (((ca,), (cb,)), ((), ())),
                           preferred_element_type=F32)


MM_TILES = (1024, 768, 512, 256, 128)


def _matmul(a, b, *, ta=False, tb=False, out_dtype=F32, name, b_block0=0, n_blocks=None,
            out_cb=None, out_into=None, out_block0=0, acc_in=None, rider=None):
    if ta:
        K, M = a.shape
    else:
        M, K = a.shape
    b_cb = None
    if b.ndim == 3:
        b_cb = b.shape[2]
        n_blocks = b.shape[0] - b_block0 if n_blocks is None else n_blocks
        N, K2 = (b.shape[1], n_blocks * b_cb) if tb else (n_blocks * b_cb, b.shape[1])
    elif tb:
        N, K2 = b.shape
    else:
        K2, N = b.shape
    assert K == K2, (a.shape, b.shape, ta, tb)
    if out_into is not None:
        out_cb = out_into.shape[2]
    tm = _pick(M, MM_TILES)
    n_unit = math.gcd(N, math.gcd(b_cb if (b_cb and not tb) else N, out_cb or N))
    tn = _pick(n_unit, MM_TILES)
    k_unit = b_cb if (b_cb and tb) else K
    tk = _pick(k_unit, MM_TILES)
    nk = K // tk
    ca, cb = (0 if ta else 1), (1 if tb else 0)
    has_acc = acc_in is not None
    has_into = out_into is not None

    def body(*refs):
        a_ref, b_ref = refs[0], refs[1]
        pos = 2
        c_ref = None
        if has_acc:
            c_ref = refs[pos]
            pos += 1
        if has_into:
            pos += 1
        o_ref = refs[pos]
        p = _dot(a_ref[...], b_ref[...], ca, cb)
        if nk == 1:
            o_ref[...] = ((p + c_ref[...]) if has_acc else p).astype(out_dtype)
        else:
            acc_ref = refs[pos + 1]
            k = pl.program_id(2)

            @pl.when(k == 0)
            def _():
                acc_ref[...] = (p + c_ref[...]) if has_acc else p

            @pl.when(k > 0)
            def _():
                acc_ref[...] += p

            @pl.when(k == nk - 1)
            def _():
                o_ref[...] = acc_ref[...].astype(out_dtype)

    a_spec = pl.BlockSpec((tk, tm), lambda j, i, k: (k, i)) if ta else pl.BlockSpec((tm, tk), lambda j, i, k: (i, k))
    if b_cb is None:
        b_spec = pl.BlockSpec((tn, tk), lambda j, i, k: (j, k)) if tb else pl.BlockSpec((tk, tn), lambda j, i, k: (k, j))
    elif tb:
        per = b_cb // tk
        b_spec = pl.BlockSpec((None, tn, tk), lambda j, i, k: (b_block0 + k // per, j, k % per))
    else:
        per = b_cb // tn
        b_spec = pl.BlockSpec((None, tk, tn), lambda j, i, k: (b_block0 + j // per, k, j % per))
    in_specs = [a_spec, b_spec]
    operands = [a, b]
    aliases = {}
    if has_acc:
        in_specs.append(pl.BlockSpec((tm, tn), lambda j, i, k: (i, j)))
        operands.append(acc_in)
    if has_into:
        aliases = {len(operands): 0}
        in_specs.append(pl.BlockSpec(memory_space=pl.ANY))
        operands.append(out_into)
    if out_cb is None:
        out_shape = jax.ShapeDtypeStruct((M, N), out_dtype)
        out_spec = pl.BlockSpec((tm, tn), lambda j, i, k: (i, j))
    else:
        per_o = out_cb // tn
        out_shape = (jax.ShapeDtypeStruct(out_into.shape, out_into.dtype) if has_into
                     else jax.ShapeDtypeStruct((N // out_cb, M, out_cb), out_dtype))
        out_spec = pl.BlockSpec((None, tm, tn), lambda j, i, k: (out_block0 + j // per_o, i, j % per_o))
    if rider is not None:
        assert not has_into
        (out,), brought = _call(body, name=name, rider=rider, grid=(N // tn, M // tm, nk), in_specs=in_specs,
                                out_specs=(out_spec,), out_shape=(out_shape,), operands=operands,
                                scratch_shapes=[] if nk == 1 else [pltpu.VMEM((tm, tn), F32)])
        return out, brought
    return pl.pallas_call(
        body, name=name, out_shape=out_shape,
        grid=(N // tn, M // tm, nk),
        in_specs=in_specs, out_specs=out_spec, input_output_aliases=aliases,
        scratch_shapes=[] if nk == 1 else [pltpu.VMEM((tm, tn), F32)],
        compiler_params=_cparams("parallel", "parallel", "arbitrary"),
    )(*operands)


def _rms(x, g):
    r = lax.rsqrt(jnp.mean(x * x, axis=-1, keepdims=True) + RMS_EPS)
    return x * r * g


def _rms_bwd(dy, x, g):
    r = lax.rsqrt(jnp.mean(x * x, axis=-1, keepdims=True) + RMS_EPS)
    xh = x * r
    dxh = dy * g
    dx = r * (dxh - xh * jnp.mean(dxh * xh, axis=-1, keepdims=True))
    dg = jnp.sum(dy * xh, axis=0, keepdims=True)
    return dx, dg


def _row_tile(rows):
    return _pick(rows, (512, 256, 128, 64, 32, 16, 8))


def _rms_fwd(x, g, *, name):
    R, D = x.shape
    tr = _row_tile(R)

    def body(x_ref, g_ref, h_ref):
        h_ref[...] = _rms(x_ref[...], g_ref[...]).astype(BF16)

    return pl.pallas_call(
        body, name=name, out_shape=jax.ShapeDtypeStruct((R, D), BF16), grid=(R // tr,),
        in_specs=[pl.BlockSpec((tr, D), lambda i: (i, 0)), pl.BlockSpec((1, D), lambda i: (0, 0))],
        out_specs=pl.BlockSpec((tr, D), lambda i: (i, 0)),
        compiler_params=_cparams("parallel"),
    )(x, g)


def _resnorm_norm(x, z, g_post, g_next, *, name):
    R, D = x.shape
    tr = _row_tile(R)

    def body(x_ref, z_ref, gp_ref, gn_ref, xn_ref, h_ref):
        xn = x_ref[...] + _rms(z_ref[...], gp_ref[...])
        xn_ref[...] = xn
        h_ref[...] = _rms(xn, gn_ref[...]).astype(BF16)

    row = pl.BlockSpec((tr, D), lambda i: (i, 0))
    vec = pl.BlockSpec((1, D), lambda i: (0, 0))
    return pl.pallas_call(
        body, name=name,
        out_shape=(jax.ShapeDtypeStruct((R, D), F32), jax.ShapeDtypeStruct((R, D), BF16)),
        grid=(R // tr,), in_specs=[row, row, vec, vec], out_specs=(row, row),
        compiler_params=_cparams("parallel"),
    )(x, z, g_post, g_next)


def _final_loss(x, z, g_post, target, *, name):
    R, D = x.shape
    tr = _row_tile(R)

    def body(x_ref, z_ref, gp_ref, t_ref, loss_ref, dy_ref, dz_ref, dg_ref):
        i = pl.program_id(0)
        z = z_ref[...]
        g = gp_ref[...]
        err = x_ref[...] + _rms(z, g) - t_ref[...]
        dy = err * (1.0 / D)
        dy_ref[...] = dy
        dz, dg = _rms_bwd(dy, z, g)
        dz_ref[...] = dz.astype(BF16)
        part = 0.5 * jnp.sum(jnp.sum(err * err, axis=-1, keepdims=True) * (1.0 / D), axis=0, keepdims=True)

        @pl.when(i == 0)
        def _():
            loss_ref[...] = part
            dg_ref[...] = dg

        @pl.when(i > 0)
        def _():
            loss_ref[...] += part
            dg_ref[...] += dg

    row = pl.BlockSpec((tr, D), lambda i: (i, 0))
    vec = pl.BlockSpec((1, D), lambda i: (0, 0))
    return pl.pallas_call(
        body, name=name,
        out_shape=(jax.ShapeDtypeStruct((1, 1), F32), jax.ShapeDtypeStruct((R, D), F32),
                   jax.ShapeDtypeStruct((R, D), BF16), jax.ShapeDtypeStruct((1, D), F32)),
        grid=(R // tr,), in_specs=[row, row, vec, row],
        out_specs=(pl.BlockSpec((1, 1), lambda i: (0, 0)), row, row, vec),
        compiler_params=_cparams("arbitrary"),
    )(x, z, g_post, target)


def _norm_bwd_pair(dres, dh, xk, g_pre, zprev, g_prev_post, *, name, rider=None):
    R, D = xk.shape
    tr = _row_tile(R)

    def body(dres_ref, dh_ref, x_ref, gpre_ref, z_ref, gpost_ref, dx_ref, dz_ref, dgpre_ref, dgpost_ref):
        i = pl.program_id(0)
        d1, dgpre = _rms_bwd(dh_ref[...], x_ref[...], gpre_ref[...])
        dx = dres_ref[...] + d1
        dx_ref[...] = dx
        dz, dgpost = _rms_bwd(dx, z_ref[...], gpost_ref[...])
        dz_ref[...] = dz.astype(BF16)

        @pl.when(i == 0)
        def _():
            dgpre_ref[...] = dgpre
            dgpost_ref[...] = dgpost

        @pl.when(i > 0)
        def _():
            dgpre_ref[...] += dgpre
            dgpost_ref[...] += dgpost

    row = pl.BlockSpec((tr, D), lambda i: (i, 0))
    vec = pl.BlockSpec((1, D), lambda i: (0, 0))
    return _call(
        body, name=name, rider=rider,
        out_shape=(jax.ShapeDtypeStruct((R, D), F32), jax.ShapeDtypeStruct((R, D), BF16),
                   jax.ShapeDtypeStruct((1, D), F32), jax.ShapeDtypeStruct((1, D), F32)),
        grid=(R // tr,), in_specs=[row, row, row, vec, row, vec], out_specs=(row, row, vec, vec),
        scratch_shapes=[], operands=(dres, dh, xk, g_pre, zprev, g_prev_post))


def _norm_bwd_single(dres, dh, xk, g_pre, *, name):
    R, D = xk.shape
    tr = _row_tile(R)
    has_res = dres is not None

    def body(*refs):
        if has_res:
            dres_ref, dh_ref, x_ref, gpre_ref, dx_ref, dgpre_ref = refs
        else:
            dh_ref, x_ref, gpre_ref, dx_ref, dgpre_ref = refs
        i = pl.program_id(0)
        d1, dgpre = _rms_bwd(dh_ref[...], x_ref[...], gpre_ref[...])
        dx_ref[...] = dres_ref[...] + d1 if has_res else d1

        @pl.when(i == 0)
        def _():
            dgpre_ref[...] = dgpre

        @pl.when(i > 0)
        def _():
            dgpre_ref[...] += dgpre

    row = pl.BlockSpec((tr, D), lambda i: (i, 0))
    vec = pl.BlockSpec((1, D), lambda i: (0, 0))
    ins = ([dres] if has_res else []) + [dh, xk, g_pre]
    return pl.pallas_call(
        body, name=name,
        out_shape=(jax.ShapeDtypeStruct((R, D), F32), jax.ShapeDtypeStruct((1, D), F32)),
        grid=(R // tr,), in_specs=([row] if has_res else []) + [row, row, vec], out_specs=(row, vec),
        compiler_params=_cparams("arbitrary"),
    )(*ins)


SB_BLOCK = 256
SB_QBLOCK = 512
SB_DEAD = -104.0


def _sb_tri(kind):
    r = lax.broadcasted_iota(jnp.int32, (SB_BLOCK, SB_BLOCK), 0)
    c = lax.broadcasted_iota(jnp.int32, (SB_BLOCK, SB_BLOCK), 1)
    keep = {"after": r > c, "upto": r <= c, "before": r < c}[kind]
    return jnp.where(keep, 1.0, 0.0).astype(BF16)


def _running_sum(vals, tri):
    hi = vals.astype(BF16)
    lo = (vals - hi.astype(F32)).astype(BF16)
    return _dot(hi, tri, 1, 0) + _dot(lo, tri, 1, 0)


def _sb_scores(qm, k_blk):
    z = _dot(qm, k_blk, 1, 1)
    sp = jnp.maximum(z, 0.0) + jnp.log(1.0 + jnp.exp(-jnp.abs(z)))
    return z, sp


def _sb_causal(rows):
    r = lax.broadcasted_iota(jnp.int32, (rows, SB_BLOCK), 0)
    c = lax.broadcasted_iota(jnp.int32, (rows, SB_BLOCK), 1)
    return c < r


def _head_masks():
    lane = lax.broadcasted_iota(jnp.int32, (1, LANES), 1)
    return [jnp.where(lane < SB_HEAD_DIM, 1.0, 0.0), jnp.where(lane >= SB_HEAD_DIM, 1.0, 0.0)]


def _sb_fwd(proj, *, name, rider=None):
    S = proj.shape[0]
    T = SB_BLOCK
    TQ = min(SB_QBLOCK, S)
    span = TQ // T
    nq = S // TQ
    npair = SB_WIDTH // LANES
    scale = SB_HEAD_DIM ** -0.5

    def body(q_ref, k_ref, v_ref, o_ref, tot_ref, first_ref, acc_ref, run_ref):
        masks = _head_masks()
        tri = _sb_tri("after")
        first_ref[...] = jnp.zeros_like(first_ref)
        slot = lax.broadcasted_iota(jnp.int32, first_ref.shape, 1)

        def alive():
            reach = jnp.maximum(jnp.max(run_ref[0]), jnp.max(run_ref[1]))
            return (reach > SB_DEAD).astype(jnp.int32)

        def q_block(i, _):
            qrow = pl.ds(pl.multiple_of(i * TQ, TQ), TQ)
            q = q_ref[qrow, :] * scale
            qm = [(q * m).astype(BF16) for m in masks]
            acc_ref[...] = jnp.zeros_like(acc_ref)
            run_ref[...] = jnp.zeros_like(run_ref)

            def k_block(j, own):
                krow = pl.ds(pl.multiple_of(j * T, T), T)
                k_blk = k_ref[krow, :].astype(BF16)
                v_blk = v_ref[krow, :].astype(BF16)
                r0 = 0 if own is None else own * T
                rows = pl.ds(r0, TQ - r0)
                for h in range(2):
                    z, sp = _sb_scores(qm[h][r0:], k_blk)
                    causal = None if own is None else _sb_causal(TQ - r0)
                    lf = -sp if causal is None else jnp.where(causal, -sp, 0.0)
                    e = jnp.exp(z - sp + _running_sum(lf, tri) + run_ref[h, rows])
                    w = e if causal is None else jnp.where(causal, e, 0.0)
                    acc_ref[h, rows] += _dot(w, v_blk, 1, 0)
                    run_ref[h, rows] += jnp.sum(lf, axis=1, keepdims=True)

            for d in reversed(range(span)):
                k_block(i * span + d, d)

            def below(carry):
                jj, _ = carry
                k_block(i * span - 1 - jj, None)
                return jj + 1, alive()

            done, _ = lax.while_loop(lambda c: jnp.logical_and(c[0] < i * span, c[1] > 0), below, (jnp.int32(0), alive()))
            o_ref[qrow, :] = (acc_ref[0] * masks[0] + acc_ref[1] * masks[1]).astype(BF16)
            tot_ref[qrow, :] = run_ref[0] * masks[0] + run_ref[1] * masks[1]
            first_ref[...] = jnp.where(slot == i, (i * span - done).astype(F32), first_ref[...])
            return 0

        lax.fori_loop(0, nq, q_block, 0)

    blk = lambda off: pl.BlockSpec((S, LANES), lambda p: (0, off + p))
    return _call(
        body, name=name, rider=rider,
        out_shape=(jax.ShapeDtypeStruct((S, SB_WIDTH), BF16), jax.ShapeDtypeStruct((S, SB_WIDTH), F32),
                   jax.ShapeDtypeStruct((npair, SUBLANES, LANES), F32)),
        grid=(npair,),
        in_specs=[blk(0), blk(npair), blk(2 * npair)],
        out_specs=(blk(0), blk(0), pl.BlockSpec((1, SUBLANES, LANES), lambda p: (p, 0, 0))),
        scratch_shapes=[pltpu.VMEM((2, TQ, LANES), F32), pltpu.VMEM((2, TQ, 1), F32)],
        operands=(proj, proj, proj))


def _sb_bwd(proj, tot, first, do_attn, *, name, rider=None):
    S = proj.shape[0]
    T = SB_BLOCK
    TQ = min(SB_QBLOCK, S)
    span = TQ // T
    nq = S // TQ
    npair = SB_WIDTH // LANES
    scale = SB_HEAD_DIM ** -0.5

    def body(q_ref, k_ref, v_ref, tot_ref, first_ref, do_ref, dq_ref, dk_ref, dv_ref,
             dqacc_ref, dkacc_ref, dvacc_ref, run_ref, grun_ref):
        masks = _head_masks()
        tri_upto = _sb_tri("upto")
        tri_before = _sb_tri("before")
        dkacc_ref[...] = jnp.zeros_like(dkacc_ref)
        dvacc_ref[...] = jnp.zeros_like(dvacc_ref)
        slot = lax.broadcasted_iota(jnp.int32, first_ref.shape, 1)

        def q_block(i, _):
            qrow = pl.ds(pl.multiple_of(i * TQ, TQ), TQ)
            q = q_ref[qrow, :] * scale
            do = do_ref[qrow, :].astype(F32)
            tot = tot_ref[qrow, :]
            qm = [(q * m).astype(BF16) for m in masks]
            dom = [(do * m).astype(BF16) for m in masks]
            ltot = [jnp.sum(tot * m, axis=1, keepdims=True) * (1.0 / SB_HEAD_DIM) for m in masks]
            dqacc_ref[...] = jnp.zeros_like(dqacc_ref)
            run_ref[...] = jnp.zeros_like(run_ref)
            grun_ref[...] = jnp.zeros_like(grun_ref)

            def k_block(j, own):
                krow = pl.ds(pl.multiple_of(j * T, T), T)
                k_blk = k_ref[krow, :].astype(BF16)
                v_blk = v_ref[krow, :].astype(BF16)
                r0 = 0 if own is None else own * T
                rows = pl.ds(r0, TQ - r0)
                for h in range(2):
                    z, sp = _sb_scores(qm[h][r0:], k_blk)
                    causal = None if own is None else _sb_causal(TQ - r0)
                    lf = -sp if causal is None else jnp.where(causal, -sp, 0.0)
                    later = ltot[h][r0:] - run_ref[h, rows] - _running_sum(lf, tri_upto)
                    beta = jnp.exp(z - sp)
                    w = jnp.exp(z - sp + later)
                    if causal is not None:
                        w = jnp.where(causal, w, 0.0)
                    g = _dot(dom[h][r0:], v_blk, 1, 1) * w
                    gbefore = grun_ref[h, rows] + _dot(g, tri_before, 1, 0)
                    dz = g - beta * (g + gbefore)
                    if causal is not None:
                        dz = jnp.where(causal, dz, 0.0)
                    dz = dz.astype(BF16)
                    dqacc_ref[h, rows] += _dot(dz, k_blk, 1, 0)
                    dkacc_ref[krow, :] += _dot(dz, qm[h][r0:], 0, 0)
                    dvacc_ref[krow, :] += _dot(w, dom[h][r0:], 0, 0)
                    run_ref[h, rows] += jnp.sum(lf, axis=1, keepdims=True)
                    grun_ref[h, rows] += jnp.sum(g, axis=1, keepdims=True)

            def above(j, _):
                k_block(j, None)
                return 0

            first = jnp.max(jnp.where(slot == i, first_ref[...], 0.0)).astype(jnp.int32)
            lax.fori_loop(jnp.clip(first, 0, i * span), i * span, above, 0)
            for d in range(span):
                k_block(i * span + d, d)
            dq_ref[qrow, :] = ((dqacc_ref[0] * masks[0] + dqacc_ref[1] * masks[1]) * scale).astype(BF16)
            return 0

        lax.fori_loop(0, nq, q_block, 0)
        dk_ref[...] = dkacc_ref[...].astype(BF16)
        dv_ref[...] = dvacc_ref[...].astype(BF16)

    blk = lambda off: pl.BlockSpec((S, LANES), lambda p: (0, off + p))
    out = jax.ShapeDtypeStruct((S, SB_WIDTH), BF16)
    return _call(
        body, name=name, rider=rider, out_shape=(out, out, out), grid=(npair,),
        in_specs=[blk(0), blk(npair), blk(2 * npair), blk(0), pl.BlockSpec((1, SUBLANES, LANES), lambda p: (p, 0, 0)),
                  blk(0)],
        out_specs=(blk(0), blk(0), blk(0)),
        scratch_shapes=[pltpu.VMEM((2, TQ, LANES), F32), pltpu.VMEM((S, LANES), F32), pltpu.VMEM((S, LANES), F32),
                        pltpu.VMEM((2, TQ, 1), F32), pltpu.VMEM((2, TQ, 1), F32)],
        operands=(proj, proj, proj, tot, first, do_attn))


SSM_HALVES = 2
SSM_HALF_CH = SSM_WIDTH // SSM_HALVES
SSM_HALF_ST = SSM_GROUPS * SSM_STATE // SSM_HALVES
SSM_CHUNK = 512


def _cmul(ar, ai, br, bi):
    return ar * br - ai * bi, ar * bi + ai * br


def _ssm_tables(lam_re, lam_im):
    lr = lam_re.reshape(-1)
    li = lam_im.reshape(-1)
    pows = [(jnp.ones_like(lr), jnp.zeros_like(li)), (lr, li)]
    for _ in range(2, SUBLANES + 1):
        pows.append(_cmul(pows[-1][0], pows[-1][1], lr, li))
    row = jnp.arange(SUBLANES)[:, None]

    def shift_tab(d, keep):
        return [jnp.where(keep, pows[d][0][None, :], 0.0), jnp.where(keep, pows[d][1][None, :], 0.0)]

    fwd, bwd = [], []
    for d in (1, 2, 4):
        fwd += shift_tab(d, row >= d)
        bwd += shift_tab(d, row + d < SUBLANES)
    fwd += [jnp.stack([pows[r + 1][0] for r in range(SUBLANES)]), jnp.stack([pows[r + 1][1] for r in range(SUBLANES)])]
    bwd += [jnp.stack([pows[SUBLANES - r][0] for r in range(SUBLANES)]),
            jnp.stack([pows[SUBLANES - r][1] for r in range(SUBLANES)])]

    def halves(tabs):
        t = jnp.stack(tabs)
        return t.reshape(8, SUBLANES, SSM_HALVES, SSM_HALF_ST).transpose(2, 0, 1, 3)

    return halves(fwd), halves(bwd)


def _ssm_fwd(proj, bd_re, bd_im, cd_re, cd_imneg, d_skip, tab, *, name, rider=None):
    S = proj.shape[0]
    Tc = min(SSM_CHUNK, S)
    nc = S // Tc
    u_blk0 = (3 * SB_WIDTH) // SSM_HALF_CH

    def body(u_ref, bre_ref, bim_ref, cre_ref, cim_ref, d_ref, tab_ref, y_ref, xre_ref, xim_ref, cre_s, cim_s):
        c = pl.program_id(1)

        @pl.when(c == 0)
        def _():
            cre_s[...] = jnp.zeros_like(cre_s)
            cim_s[...] = jnp.zeros_like(cim_s)

        u = u_ref[...]
        ub = u.astype(BF16)
        xre_ref[...] = _dot(ub, bre_ref[0], 1, 0)
        xim_ref[...] = _dot(ub, bim_ref[0], 1, 0)

        def slab(k, carry):
            car_re, car_im = carry
            rows = pl.ds(pl.multiple_of(k * SUBLANES, SUBLANES), SUBLANES)
            sre = xre_ref[rows, :]
            sim = xim_ref[rows, :]
            for n, d in enumerate((1, 2, 4)):
                pre, pim = tab_ref[0, 2 * n], tab_ref[0, 2 * n + 1]
                rre = pltpu.roll(sre, d, 0)
                rim = pltpu.roll(sim, d, 0)
                sre, sim = sre + (pre * rre - pim * rim), sim + (pre * rim + pim * rre)
            pre, pim = tab_ref[0, 6], tab_ref[0, 7]
            sre, sim = sre + (pre * car_re - pim * car_im), sim + (pre * car_im + pim * car_re)
            xre_ref[rows, :] = sre
            xim_ref[rows, :] = sim
            last = (SUBLANES - 1, SUBLANES)
            return (jnp.broadcast_to(sre[last[0]:last[1], :], sre.shape),
                    jnp.broadcast_to(sim[last[0]:last[1], :], sim.shape))

        car = lax.fori_loop(0, Tc // SUBLANES, slab, (cre_s[...], cim_s[...]))
        cre_s[...] = car[0]
        cim_s[...] = car[1]
        y = _dot(xre_ref[...], cre_ref[0], 1, 0) + _dot(xim_ref[...], cim_ref[0], 1, 0)
        y_ref[...] = y + d_ref[...] * u

    return _call(
        body, name=name, rider=rider,
        out_shape=(jax.ShapeDtypeStruct((S, SSM_WIDTH), F32),
                   jax.ShapeDtypeStruct((S, SSM_HALVES * SSM_HALF_ST), F32),
                   jax.ShapeDtypeStruct((S, SSM_HALVES * SSM_HALF_ST), F32)),
        grid=(SSM_HALVES, nc),
        in_specs=[pl.BlockSpec((Tc, SSM_HALF_CH), lambda h, c: (c, u_blk0 + h)),
                  pl.BlockSpec((1, SSM_HALF_CH, SSM_HALF_ST), lambda h, c: (h, 0, 0)),
                  pl.BlockSpec((1, SSM_HALF_CH, SSM_HALF_ST), lambda h, c: (h, 0, 0)),
                  pl.BlockSpec((1, SSM_HALF_ST, SSM_HALF_CH), lambda h, c: (h, 0, 0)),
                  pl.BlockSpec((1, SSM_HALF_ST, SSM_HALF_CH), lambda h, c: (h, 0, 0)),
                  pl.BlockSpec((1, SSM_HALF_CH), lambda h, c: (0, h)),
                  pl.BlockSpec((1, 8, SUBLANES, SSM_HALF_ST), lambda h, c: (h, 0, 0, 0))],
        out_specs=(pl.BlockSpec((Tc, SSM_HALF_CH), lambda h, c: (c, h)),
                   pl.BlockSpec((Tc, SSM_HALF_ST), lambda h, c: (c, h)),
                   pl.BlockSpec((Tc, SSM_HALF_ST), lambda h, c: (c, h))),
        scratch_shapes=[pltpu.VMEM((SUBLANES, SSM_HALF_ST), F32), pltpu.VMEM((SUBLANES, SSM_HALF_ST), F32)],
        operands=(proj, bd_re, bd_im, cd_re, cd_imneg, d_skip, tab))


def _ssm_bwd(dy, proj, x_re, x_im, bd_re, bd_im, cd_re, cd_imneg, d_skip, tab, *, name, rider=None):
    S = proj.shape[0]
    Tc = min(SSM_CHUNK, S)
    nc = S // Tc
    u_blk0 = (3 * SB_WIDTH) // SSM_HALF_CH

    def body(dy_ref, u_ref, xre_ref, xim_ref, bre_ref, bim_ref, cre_ref, cim_ref, d_ref, tab_ref,
             du_ref, dbre_ref, dbim_ref, dcre_ref, dcim_ref, dd_ref, dlre_ref, dlim_ref,
             gre_s, gim_s, cre_s, cim_s):
        c = pl.program_id(1)

        @pl.when(c == 0)
        def _():
            cre_s[...] = jnp.zeros_like(cre_s)
            cim_s[...] = jnp.zeros_like(cim_s)
            dbre_ref[...] = jnp.zeros_like(dbre_ref)
            dbim_ref[...] = jnp.zeros_like(dbim_ref)
            dcre_ref[...] = jnp.zeros_like(dcre_ref)
            dcim_ref[...] = jnp.zeros_like(dcim_ref)
            dd_ref[...] = jnp.zeros_like(dd_ref)
            dlre_ref[...] = jnp.zeros_like(dlre_ref)
            dlim_ref[...] = jnp.zeros_like(dlim_ref)

        dy = dy_ref[...]
        dyb = dy.astype(BF16)
        u = u_ref[...]
        gre_s[...] = _dot(dyb, cre_ref[0], 1, 1)
        gim_s[...] = _dot(dyb, cim_ref[0], 1, 1)
        row = lax.broadcasted_iota(jnp.int32, (SUBLANES, SSM_HALF_ST), 0)
        nslab = Tc // SUBLANES

        def slab(kk, carry):
            car_re, car_im, acc_re, acc_im = carry
            k = nslab - 1 - kk
            rows = pl.ds(pl.multiple_of(k * SUBLANES, SUBLANES), SUBLANES)
            sre = gre_s[rows, :]
            sim = gim_s[rows, :]
            for n, d in enumerate((1, 2, 4)):
                pre, pim = tab_ref[0, 2 * n], tab_ref[0, 2 * n + 1]
                rre = pltpu.roll(sre, SUBLANES - d, 0)
                rim = pltpu.roll(sim, SUBLANES - d, 0)
                sre, sim = sre + (pre * rre + pim * rim), sim + (pre * rim - pim * rre)
            pre, pim = tab_ref[0, 6], tab_ref[0, 7]
            sre, sim = sre + (pre * car_re + pim * car_im), sim + (pre * car_im - pim * car_re)
            gre_s[rows, :] = sre
            gim_s[rows, :] = sim
            nre = jnp.where(row == SUBLANES - 1, car_re, pltpu.roll(sre, SUBLANES - 1, 0))
            nim = jnp.where(row == SUBLANES - 1, car_im, pltpu.roll(sim, SUBLANES - 1, 0))
            xr = xre_ref[rows, :]
            xi = xim_ref[rows, :]
            acc_re = acc_re + (nre * xr + nim * xi)
            acc_im = acc_im + (nim * xr - nre * xi)
            return (jnp.broadcast_to(sre[0:1, :], sre.shape), jnp.broadcast_to(sim[0:1, :], sim.shape), acc_re, acc_im)

        car = lax.fori_loop(0, nslab, slab, (cre_s[...], cim_s[...], dlre_ref[0], dlim_ref[0]))
        cre_s[...] = car[0]
        cim_s[...] = car[1]
        dlre_ref[0] = car[2]
        dlim_ref[0] = car[3]
        gre = gre_s[...].astype(BF16)
        gim = gim_s[...].astype(BF16)
        ub = u.astype(BF16)
        du = _dot(gre, bre_ref[0], 1, 1) + _dot(gim, bim_ref[0], 1, 1) + d_ref[...] * dy
        du_ref[...] = du.astype(BF16)
        dbre_ref[0] += _dot(ub, gre, 0, 0)
        dbim_ref[0] += _dot(ub, gim, 0, 0)
        dcre_ref[0] += _dot(xre_ref[...], dyb, 0, 0)
        dcim_ref[0] += _dot(xim_ref[...], dyb, 0, 0)
        dd_ref[...] += jnp.sum(dy * u, axis=0, keepdims=True)

    rev = lambda c: nc - 1 - c
    return _call(
        body, name=name, rider=rider,
        out_shape=(jax.ShapeDtypeStruct((S, SSM_WIDTH), BF16),
                   jax.ShapeDtypeStruct((SSM_HALVES, SSM_HALF_CH, SSM_HALF_ST), F32),
                   jax.ShapeDtypeStruct((SSM_HALVES, SSM_HALF_CH, SSM_HALF_ST), F32),
                   jax.ShapeDtypeStruct((SSM_HALVES, SSM_HALF_ST, SSM_HALF_CH), F32),
                   jax.ShapeDtypeStruct((SSM_HALVES, SSM_HALF_ST, SSM_HALF_CH), F32),
                   jax.ShapeDtypeStruct((1, SSM_WIDTH), F32),
                   jax.ShapeDtypeStruct((SSM_HALVES, SUBLANES, SSM_HALF_ST), F32),
                   jax.ShapeDtypeStruct((SSM_HALVES, SUBLANES, SSM_HALF_ST), F32)),
        grid=(SSM_HALVES, nc),
        in_specs=[pl.BlockSpec((Tc, SSM_HALF_CH), lambda h, c: (rev(c), h)),
                  pl.BlockSpec((Tc, SSM_HALF_CH), lambda h, c: (rev(c), u_blk0 + h)),
                  pl.BlockSpec((Tc, SSM_HALF_ST), lambda h, c: (rev(c), h)),
                  pl.BlockSpec((Tc, SSM_HALF_ST), lambda h, c: (rev(c), h)),
                  pl.BlockSpec((1, SSM_HALF_CH, SSM_HALF_ST), lambda h, c: (h, 0, 0)),
                  pl.BlockSpec((1, SSM_HALF_CH, SSM_HALF_ST), lambda h, c: (h, 0, 0)),
                  pl.BlockSpec((1, SSM_HALF_ST, SSM_HALF_CH), lambda h, c: (h, 0, 0)),
                  pl.BlockSpec((1, SSM_HALF_ST, SSM_HALF_CH), lambda h, c: (h, 0, 0)),
                  pl.BlockSpec((1, SSM_HALF_CH), lambda h, c: (0, h)),
                  pl.BlockSpec((1, 8, SUBLANES, SSM_HALF_ST), lambda h, c: (h, 0, 0, 0))],
        out_specs=(pl.BlockSpec((Tc, SSM_HALF_CH), lambda h, c: (rev(c), h)),
                   pl.BlockSpec((1, SSM_HALF_CH, SSM_HALF_ST), lambda h, c: (h, 0, 0)),
                   pl.BlockSpec((1, SSM_HALF_CH, SSM_HALF_ST), lambda h, c: (h, 0, 0)),
                   pl.BlockSpec((1, SSM_HALF_ST, SSM_HALF_CH), lambda h, c: (h, 0, 0)),
                   pl.BlockSpec((1, SSM_HALF_ST, SSM_HALF_CH), lambda h, c: (h, 0, 0)),
                   pl.BlockSpec((1, SSM_HALF_CH), lambda h, c: (0, h)),
                   pl.BlockSpec((1, SUBLANES, SSM_HALF_ST), lambda h, c: (h, 0, 0)),
                   pl.BlockSpec((1, SUBLANES, SSM_HALF_ST), lambda h, c: (h, 0, 0))),
        scratch_shapes=[pltpu.VMEM((Tc, SSM_HALF_ST), F32), pltpu.VMEM((Tc, SSM_HALF_ST), F32),
                        pltpu.VMEM((SUBLANES, SSM_HALF_ST), F32), pltpu.VMEM((SUBLANES, SSM_HALF_ST), F32)],
        operands=(dy, proj, x_re, x_im, bd_re, bd_im, cd_re, cd_imneg, d_skip, tab))


def _ssm_prepare(a_re, a_im, log_dt, b_re, b_im):
    dt = jnp.exp(log_dt)[:, None]
    mag = jnp.exp(a_re * dt)
    lre = mag * jnp.cos(a_im * dt)
    lim = mag * jnp.sin(a_im * dt)
    den = a_re * a_re + a_im * a_im
    fre = ((lre - 1.0) * a_re + lim * a_im) / den
    fim = (lim * a_re - (lre - 1.0) * a_im) / den
    bbre = fre[:, :, None] * b_re - fim[:, :, None] * b_im
    bbim = fre[:, :, None] * b_im + fim[:, :, None] * b_re
    return lre, lim, bbre, bbim


def _group_eye():
    return jnp.eye(SSM_GROUPS // SSM_HALVES, dtype=F32)


def _bd_from_bbar(bbar):
    gh = SSM_GROUPS // SSM_HALVES
    b = bbar.reshape(SSM_HALVES, gh, SSM_STATE, SSM_GROUP).transpose(0, 1, 3, 2)
    out = b[:, :, :, None, :] * _group_eye()[None, :, None, :, None]
    return out.reshape(SSM_HALVES, SSM_HALF_CH, SSM_HALF_ST)


def _bbar_from_bd(dbd):
    gh = SSM_GROUPS // SSM_HALVES
    d = dbd.reshape(SSM_HALVES, gh, SSM_GROUP, gh, SSM_STATE)
    d = jnp.sum(d * _group_eye()[None, :, None, :, None], axis=3)
    return d.transpose(0, 1, 3, 2).reshape(SSM_GROUPS, SSM_STATE, SSM_GROUP)


def _cd_from_c(cmat):
    gh = SSM_GROUPS // SSM_HALVES
    c = cmat.reshape(SSM_HALVES, gh, SSM_GROUP, SSM_STATE).transpose(0, 1, 3, 2)
    out = c[:, :, :, None, :] * _group_eye()[None, :, None, :, None]
    return out.reshape(SSM_HALVES, SSM_HALF_ST, SSM_HALF_CH)


def _c_from_cd(dcd):
    gh = SSM_GROUPS // SSM_HALVES
    d = dcd.reshape(SSM_HALVES, gh, SSM_STATE, gh, SSM_GROUP)
    d = jnp.sum(d * _group_eye()[None, :, None, :, None], axis=3)
    return d.transpose(0, 1, 3, 2).reshape(SSM_GROUPS, SSM_GROUP, SSM_STATE)


def _glu_fwd(y_pre, w_glu, b_glu, *, name):
    S, W = y_pre.shape
    tr = _row_tile(S)

    def body(y_ref, w_ref, b_ref, o_ref):
        yg = _gelu(y_ref[...])
        gl = _dot(yg, w_ref[...], 1, 0) + b_ref[...]
        o_ref[...] = (yg * _sigmoid(gl)).astype(BF16)

    row = pl.BlockSpec((tr, W), lambda i: (i, 0))
    return pl.pallas_call(
        body, name=name, out_shape=jax.ShapeDtypeStruct((S, W), BF16), grid=(S // tr,),
        in_specs=[row, pl.BlockSpec((W, W), lambda i: (0, 0)), pl.BlockSpec((1, W), lambda i: (0, 0))],
        out_specs=row, compiler_params=_cparams("parallel"),
    )(y_pre, w_glu, b_glu)


def _glu_bwd(y_pre, do, w_glu, b_glu, *, name):
    S, W = y_pre.shape
    tr = _row_tile(S)

    def body(y_ref, do_ref, w_ref, b_ref, dy_ref, dw_ref, db_ref):
        i = pl.program_id(0)
        yg, dyg_dy = _gelu_and_grad(y_ref[...])
        ygb = yg.astype(BF16)
        sg = _sigmoid(_dot(ygb, w_ref[...], 1, 0) + b_ref[...])
        do = do_ref[...]
        dgl = do * yg * sg * (1.0 - sg)
        dglb = dgl.astype(BF16)
        dyg = do * sg + _dot(dglb, w_ref[...], 1, 1)
        dy_ref[...] = dyg * dyg_dy
        dw = _dot(ygb, dglb, 0, 0)
        db = jnp.sum(dgl, axis=0, keepdims=True)

        @pl.when(i == 0)
        def _():
            dw_ref[...] = dw
            db_ref[...] = db

        @pl.when(i > 0)
        def _():
            dw_ref[...] += dw
            db_ref[...] += db

    row = pl.BlockSpec((tr, W), lambda i: (i, 0))
    full = pl.BlockSpec((W, W), lambda i: (0, 0))
    vec = pl.BlockSpec((1, W), lambda i: (0, 0))
    return pl.pallas_call(
        body, name=name,
        out_shape=(jax.ShapeDtypeStruct((S, W), F32), jax.ShapeDtypeStruct((W, W), F32), jax.ShapeDtypeStruct((1, W), F32)),
        grid=(S // tr,), in_specs=[row, row, full, vec], out_specs=(row, full, vec),
        compiler_params=_cparams("arbitrary"),
    )(y_pre, do, w_glu, b_glu)


GATE_COL0 = 3 * SB_WIDTH + SSM_WIDTH


def _merge_fwd(proj, o_attn, o_ssm, w_ba, w_bs, b_gate, *, name):
    S = proj.shape[0]
    D = D_MODEL
    tr = _pick(S, (256, 128, 64, 32, 16, 8))
    gb = GATE_COL0 // D

    def body(ga_ref, gs_ref, oa_ref, os_ref, wa_ref, ws_ref, ba_ref, bs_ref, m_ref):
        pa = _dot(oa_ref[...], wa_ref[...], 1, 0)
        ps = _dot(os_ref[...], ws_ref[...], 1, 0)
        sa = _sigmoid(ga_ref[...] + ba_ref[...])
        ss = _sigmoid(gs_ref[...] + bs_ref[...])
        m_ref[...] = (sa * pa + ss * ps).astype(BF16)

    return pl.pallas_call(
        body, name=name, out_shape=jax.ShapeDtypeStruct((S, D), BF16), grid=(S // tr,),
        in_specs=[pl.BlockSpec((tr, D), lambda i: (i, gb)), pl.BlockSpec((tr, D), lambda i: (i, gb + 1)),
                  pl.BlockSpec((tr, SB_WIDTH), lambda i: (i, 0)), pl.BlockSpec((tr, SSM_WIDTH), lambda i: (i, 0)),
                  pl.BlockSpec((SB_WIDTH, D), lambda i: (0, 0)), pl.BlockSpec((SSM_WIDTH, D), lambda i: (0, 0)),
                  pl.BlockSpec((1, D), lambda i: (0, 0)), pl.BlockSpec((1, D), lambda i: (0, 1))],
        out_specs=pl.BlockSpec((tr, D), lambda i: (i, 0)),
        compiler_params=_cparams("parallel"),
    )(proj, proj, o_attn, o_ssm, w_ba, w_bs, b_gate, b_gate)


def _merge_bwd(dmerged, proj, o_attn, o_ssm, w_ba, w_bs, b_gate, *, name):
    S = proj.shape[0]
    D = D_MODEL
    tr = _pick(S, (256, 128, 64, 32, 16, 8))
    gb = GATE_COL0 // D

    def body(dm_ref, ga_ref, gs_ref, oa_ref, os_ref, wa_ref, ws_ref, ba_ref, bs_ref,
             doa_ref, dos_ref, dg_ref, db_ref, dwa_ref, dws_ref):
        i = pl.program_id(0)
        dm = dm_ref[...]
        oa = oa_ref[...]
        osm = os_ref[...]
        pa = _dot(oa, wa_ref[...], 1, 0)
        ps = _dot(osm, ws_ref[...], 1, 0)
        sa = _sigmoid(ga_ref[...] + ba_ref[...])
        ss = _sigmoid(gs_ref[...] + bs_ref[...])
        dpa = (dm * sa).astype(BF16)
        dps = (dm * ss).astype(BF16)
        dga = dm * pa * sa * (1.0 - sa)
        dgs = dm * ps * ss * (1.0 - ss)
        dg_ref[:, :D] = dga.astype(BF16)
        dg_ref[:, D:] = dgs.astype(BF16)
        doa_ref[...] = _dot(dpa, wa_ref[...], 1, 1).astype(BF16)
        dos_ref[...] = _dot(dps, ws_ref[...], 1, 1)
        dwa = _dot(oa, dpa, 0, 0)
        dws = _dot(osm, dps, 0, 0)
        dba = jnp.sum(dga, axis=0, keepdims=True)
        dbs = jnp.sum(dgs, axis=0, keepdims=True)

        @pl.when(i == 0)
        def _():
            dwa_ref[...] = dwa
            dws_ref[...] = dws
            db_ref[:, :D] = dba
            db_ref[:, D:] = dbs

        @pl.when(i > 0)
        def _():
            dwa_ref[...] += dwa
            dws_ref[...] += dws
            db_ref[:, :D] += dba
            db_ref[:, D:] += dbs

    rowD = pl.BlockSpec((tr, D), lambda i: (i, 0))
    wspec = pl.BlockSpec((SB_WIDTH, D), lambda i: (0, 0))
    return pl.pallas_call(
        body, name=name,
        out_shape=(jax.ShapeDtypeStruct((S, SB_WIDTH), BF16), jax.ShapeDtypeStruct((S, SSM_WIDTH), F32),
                   jax.ShapeDtypeStruct((S, 2 * D), BF16), jax.ShapeDtypeStruct((1, 2 * D), F32),
                   jax.ShapeDtypeStruct((SB_WIDTH, D), F32), jax.ShapeDtypeStruct((SSM_WIDTH, D), F32)),
        grid=(S // tr,),
        in_specs=[rowD, pl.BlockSpec((tr, D), lambda i: (i, gb)), pl.BlockSpec((tr, D), lambda i: (i, gb + 1)),
                  pl.BlockSpec((tr, SB_WIDTH), lambda i: (i, 0)), pl.BlockSpec((tr, SSM_WIDTH), lambda i: (i, 0)),
                  wspec, wspec, pl.BlockSpec((1, D), lambda i: (0, 0)), pl.BlockSpec((1, D), lambda i: (0, 1))],
        out_specs=(pl.BlockSpec((tr, SB_WIDTH), lambda i: (i, 0)), pl.BlockSpec((tr, SSM_WIDTH), lambda i: (i, 0)),
                   pl.BlockSpec((tr, 2 * D), lambda i: (i, 0)), pl.BlockSpec((1, 2 * D), lambda i: (0, 0)),
                   wspec, wspec),
        compiler_params=_cparams("arbitrary"),
    )(dmerged, proj, proj, o_attn, o_ssm, w_ba, w_bs, b_gate, b_gate)


def _xattn_probs(q, k, h):
    cols = slice(h * XA_HEAD_DIM, (h + 1) * XA_HEAD_DIM)
    s = _dot(q[:, cols], k[:, cols], 1, 1) * (XA_HEAD_DIM ** -0.5)
    s = s - jnp.max(s, axis=-1, keepdims=True)
    e = jnp.exp(s)
    return e / jnp.sum(e, axis=-1, keepdims=True), cols


def _xattn_fwd(q2, k2, v2, *, name):
    S, D = q2.shape
    M = k2.shape[0]
    tr = _row_tile(S)

    def body(q_ref, k_ref, v_ref, o_ref):
        q = q_ref[...]
        k = k_ref[...]
        v = v_ref[...]
        for h in range(XA_HEADS):
            p, cols = _xattn_probs(q, k, h)
            o_ref[:, cols] = _dot(p, v[:, cols], 1, 0).astype(BF16)

    row = pl.BlockSpec((tr, D), lambda i: (i, 0))
    memb = pl.BlockSpec((M, D), lambda i: (0, 0))
    return pl.pallas_call(
        body, name=name, out_shape=jax.ShapeDtypeStruct((S, D), BF16), grid=(S // tr,),
        in_specs=[row, memb, memb], out_specs=row, compiler_params=_cparams("parallel"),
    )(q2, k2, v2)


def _xattn_bwd(q2, k2, v2, do2, *, name):
    S, D = q2.shape
    M = k2.shape[0]
    tr = _row_tile(S)
    scale = XA_HEAD_DIM ** -0.5

    def body(q_ref, k_ref, v_ref, do_ref, dq_ref, dk_ref, dv_ref):
        i = pl.program_id(0)

        @pl.when(i == 0)
        def _():
            dk_ref[...] = jnp.zeros_like(dk_ref)
            dv_ref[...] = jnp.zeros_like(dv_ref)

        q = q_ref[...]
        k = k_ref[...]
        v = v_ref[...]
        do = do_ref[...]
        for h in range(XA_HEADS):
            p, cols = _xattn_probs(q, k, h)
            dp = _dot(do[:, cols], v[:, cols], 1, 1)
            ds = (p * (dp - jnp.sum(dp * p, axis=-1, keepdims=True)) * scale).astype(BF16)
            dq_ref[:, cols] = _dot(ds, k[:, cols], 1, 0).astype(BF16)
            dk_ref[:, cols] += _dot(ds, q[:, cols], 0, 0)
            dv_ref[:, cols] += _dot(p, do[:, cols], 0, 0)

    row = pl.BlockSpec((tr, D), lambda i: (i, 0))
    memb = pl.BlockSpec((M, D), lambda i: (0, 0))
    return pl.pallas_call(
        body, name=name,
        out_shape=(jax.ShapeDtypeStruct((S, D), BF16), jax.ShapeDtypeStruct((M, D), F32), jax.ShapeDtypeStruct((M, D), F32)),
        grid=(S // tr,), in_specs=[row, memb, memb, row], out_specs=(row, memb, memb),
        compiler_params=_cparams("arbitrary"),
    )(q2, k2, v2, do2)


CONV_ROWS = 64
CONV_ROWS_FWD = 256


def _chunk(ref, c, rows):
    return ref[pl.ds(pl.multiple_of(c * rows, rows), rows), :]


def _rows_before(ref, c, rows):
    t0 = pl.multiple_of(jnp.maximum(c * rows - SUBLANES, 0), SUBLANES)
    return jnp.where(c > 0, ref[pl.ds(t0, SUBLANES), :], 0.0)


def _rows_after(ref, c, rows, n_chunks):
    t0 = pl.multiple_of(jnp.minimum((c + 1) * rows, n_chunks * rows - SUBLANES), SUBLANES)
    return jnp.where(c < n_chunks - 1, ref[pl.ds(t0, SUBLANES), :], 0.0)


def _shift_down(cur, before, d):
    out = pltpu.roll(cur, d, 0)
    r = lax.broadcasted_iota(jnp.int32, cur.shape, 0)
    for e in range(d):
        out = jnp.where(r == e, before[SUBLANES - d + e:SUBLANES - d + e + 1, :], out)
    return out


def _shift_up(cur, after, d):
    rows = cur.shape[0]
    out = pltpu.roll(cur, rows - d, 0)
    r = lax.broadcasted_iota(jnp.int32, cur.shape, 0)
    for e in range(d):
        out = jnp.where(r == rows - d + e, after[e:e + 1, :], out)
    return out


def _conv3(cur, before, w_ref, b_ref):
    return (w_ref[2:3, :] * cur + w_ref[1:2, :] * _shift_down(cur, before, 1)
            + w_ref[0:1, :] * _shift_down(cur, before, 2) + b_ref[...])


def _convgate_fwd(up_g, up_v, conv_w, conv_b, *, name):
    S, H = up_g.shape
    nb = H // LANES
    R = min(CONV_ROWS_FWD, S)
    n_chunks = S // R

    def body(g_ref, v_ref, wg_ref, wv_ref, bg_ref, bv_ref, a_ref):
        def chunk(c, _):
            cg = _conv3(_chunk(g_ref, c, R), _rows_before(g_ref, c, R), wg_ref, bg_ref)
            cv = _conv3(_chunk(v_ref, c, R), _rows_before(v_ref, c, R), wv_ref, bv_ref)
            a_ref[pl.ds(pl.multiple_of(c * R, R), R), :] = (_gelu(cg) * cv).astype(BF16)
            return 0

        lax.fori_loop(0, n_chunks, chunk, 0)

    col = lambda off: pl.BlockSpec((S, LANES), lambda j: (0, off + j))
    wcol = lambda off: pl.BlockSpec((3, LANES), lambda j: (0, off + j))
    bcol = lambda off: pl.BlockSpec((1, LANES), lambda j: (0, off + j))
    return pl.pallas_call(
        body, name=name, out_shape=jax.ShapeDtypeStruct((S, H), BF16), grid=(nb,),
        in_specs=[col(0), col(0), wcol(0), wcol(nb), bcol(0), bcol(nb)],
        out_specs=col(0), compiler_params=_cparams("parallel"),
    )(up_g, up_v, conv_w, conv_w, conv_b, conv_b)


def _convgate_bwd(up_g, up_v, da, conv_w, conv_b, *, name):
    S, H = up_g.shape
    nb = H // LANES
    R = min(CONV_ROWS, S)
    n_chunks = S // R

    def fold(a):
        return sum(a[r:r + SUBLANES] for r in range(0, a.shape[0], SUBLANES))

    def body(g_ref, v_ref, da_ref, wg_ref, wv_ref, bg_ref, bv_ref,
             dug_ref, duv_ref, dwg_ref, dwv_ref, dbg_ref, dbv_ref, dcg_s, dcv_s):
        def first_pass(c, acc):
            rows = pl.ds(pl.multiple_of(c * R, R), R)
            ug, uv = _chunk(g_ref, c, R), _chunk(v_ref, c, R)
            bg, bv = _rows_before(g_ref, c, R), _rows_before(v_ref, c, R)
            cg = _conv3(ug, bg, wg_ref, bg_ref)
            cv = _conv3(uv, bv, wv_ref, bv_ref)
            da = da_ref[rows, :]
            gl, dgl = _gelu_and_grad(cg)
            dcg = da * cv * dgl
            dcv = da * gl
            dcg_s[rows, :] = dcg
            dcv_s[rows, :] = dcv
            new = []
            for dc, u, before in ((dcg, ug, bg), (dcv, uv, bv)):
                new += [fold(dc * _shift_down(u, before, 2)), fold(dc * _shift_down(u, before, 1)), fold(dc * u), fold(dc)]
            return tuple(a + n for a, n in zip(acc, new))

        zero = jnp.zeros((SUBLANES, LANES), F32)
        acc = lax.fori_loop(0, n_chunks, first_pass, (zero,) * 8)
        total = [jnp.sum(a, axis=0, keepdims=True) for a in acc]
        for k, (dw_ref, db_ref) in enumerate(((dwg_ref, dbg_ref), (dwv_ref, dbv_ref))):
            dw_ref[0:1, :] = total[4 * k]
            dw_ref[1:2, :] = total[4 * k + 1]
            dw_ref[2:3, :] = total[4 * k + 2]
            db_ref[...] = total[4 * k + 3]

        def second_pass(c, _):
            rows = pl.ds(pl.multiple_of(c * R, R), R)
            for dc_s, w_ref, du_ref in ((dcg_s, wg_ref, dug_ref), (dcv_s, wv_ref, duv_ref)):
                cur, after = _chunk(dc_s, c, R), _rows_after(dc_s, c, R, n_chunks)
                du = w_ref[2:3, :] * cur + w_ref[1:2, :] * _shift_up(cur, after, 1) + w_ref[0:1, :] * _shift_up(cur, after, 2)
                du_ref[rows, :] = du.astype(BF16)
            return 0

        lax.fori_loop(0, n_chunks, second_pass, 0)

    col = lambda off: pl.BlockSpec((S, LANES), lambda j: (0, off + j))
    wcol = lambda off: pl.BlockSpec((3, LANES), lambda j: (0, off + j))
    bcol = lambda off: pl.BlockSpec((1, LANES), lambda j: (0, off + j))
    return pl.pallas_call(
        body, name=name,
        out_shape=(jax.ShapeDtypeStruct((S, H), BF16), jax.ShapeDtypeStruct((S, H), BF16),
                   jax.ShapeDtypeStruct((3, H), F32), jax.ShapeDtypeStruct((3, H), F32),
                   jax.ShapeDtypeStruct((1, H), F32), jax.ShapeDtypeStruct((1, H), F32)),
        grid=(nb,),
        in_specs=[col(0), col(0), col(0), wcol(0), wcol(nb), bcol(0), bcol(nb)],
        out_specs=(col(0), col(0), wcol(0), wcol(0), bcol(0), bcol(0)),
        scratch_shapes=[pltpu.VMEM((S, LANES), F32), pltpu.VMEM((S, LANES), F32)],
        compiler_params=_cparams("parallel"),
    )(up_g, up_v, da, conv_w, conv_w, conv_b, conv_b)


def _local_step(x, mem, target, w_in, late_wire, P, core):
    mm = _matmul
    h1 = _rms_fwd(x, P["norm_mix_pre"], name="rms_mix_pre")
    n_mid = len(LATE) - len(REDUCE_FFN)
    proj, wire_mid = mm(h1, w_in, name="mm_in", rider=_fill_xy(late_wire[:n_mid]))
    (o_attn, sb_tot, sb_first), wires = _sb_fwd(
        proj, name="sb_fwd", rider=_Exchange.join(_fill_c(wire_mid), _fill_xy(late_wire[n_mid:])))
    wire_mid, wire_ffn = wires[:n_mid], wires[n_mid:]

    ssm_prep = lambda *a: _ssm_prepare(*a)
    (lam_re, lam_im, bb_re, bb_im), prep_vjp = jax.vjp(
        ssm_prep, P["ssm_a_re"], P["ssm_a_im"], P["ssm_log_dt"], P["ssm_b_re"], P["ssm_b_im"])
    tab_f, tab_b = _ssm_tables(lam_re, lam_im)
    bd_re = _bd_from_bbar(bb_re).astype(BF16)
    bd_im = _bd_from_bbar(bb_im).astype(BF16)
    cd_re = _cd_from_c(P["ssm_c_re"]).astype(BF16)
    cd_imneg = _cd_from_c(-P["ssm_c_im"]).astype(BF16)
    (y_pre, x_re, x_im), wire_ffn = _ssm_fwd(proj, bd_re, bd_im, cd_re, cd_imneg, P["ssm_d"], tab_f,
                                             name="ssm_fwd", rider=_fill_c(wire_ffn))
    W = _weights_from_wire(dict(zip(LATE, list(wire_mid) + list(wire_ffn))))
    W["w_in"] = w_in
    o_ssm = _glu_fwd(y_pre, W["ssm_w_glu"], P["ssm_b_glu"], name="glu_fwd")

    merged = _merge_fwd(proj, o_attn, o_ssm, W["w_branch_attn"], W["w_branch_ssm"], P["b_gate"], name="merge_fwd")
    mo = mm(merged, W["w_out"], name="mm_out")
    x1, h2 = _resnorm_norm(x, mo, P["norm_mix_post"], P["norm_xa_pre"], name="resnorm_1")

    mem_n = _rms_fwd(mem, P["norm_mem"], name="rms_mem")
    q2 = mm(h2, W["xa_wq"], out_dtype=BF16, name="mm_xq")
    k2 = mm(mem_n, W["xa_wk"], out_dtype=BF16, name="mm_xk")
    v2 = mm(mem_n, W["xa_wv"], out_dtype=BF16, name="mm_xv")
    o2 = _xattn_fwd(q2, k2, v2, name="xattn_fwd")
    xa = mm(o2, W["xa_wo"], name="mm_xo")
    x2, h3 = _resnorm_norm(x1, xa, P["norm_xa_post"], P["norm_ffn_pre"], name="resnorm_2")

    half = N_DEV // 2
    up_g = mm(h3, W["ffn_w_up"], n_blocks=half, name="mm_up_g")
    up_v = mm(h3, W["ffn_w_up"], b_block0=half, name="mm_up_v")
    act = _convgate_fwd(up_g, up_v, W["ffn_conv_w"], P["ffn_conv_b"], name="convgate_fwd")
    f = mm(act, W["ffn_w_down"], name="mm_down")
    loss, dy, df, dg_ffn_post = _final_loss(x2, f, P["norm_ffn_post"], target, name="final_loss")

    G = {"norm_ffn_post": dg_ffn_post}
    dact = mm(df, W["ffn_w_down"], tb=True, name="mm_down_dx")
    G["ffn_w_down"] = mm(act, df, ta=True, name="mm_down_dw")
    dug, duv, dwg, dwv, dbg, dbv = _convgate_bwd(up_g, up_v, dact, W["ffn_conv_w"], P["ffn_conv_b"], name="convgate_bwd")
    G["ffn_conv_w"] = jnp.concatenate([dwg, dwv], axis=1)
    G["ffn_conv_b"] = jnp.concatenate([dbg, dbv], axis=1)
    dh3 = mm(dug, W["ffn_w_up"], tb=True, n_blocks=half, name="mm_up_g_dx")
    dh3 = mm(duv, W["ffn_w_up"], tb=True, b_block0=half, acc_in=dh3, name="mm_up_v_dx")
    dw_up = mm(h3, dug, ta=True, out_into=lax.empty(W["ffn_w_up"].shape, F32), name="mm_up_g_dw")
    G["ffn_w_up"] = mm(h3, duv, ta=True, out_into=dw_up, out_block0=half, name="mm_up_v_dw")
    blocks = {n: _grad_blocks(n, G[n]) for n in REDUCE_FFN}
    (dx2, dxa, G["norm_ffn_pre"], G["norm_xa_post"]), from_core = _norm_bwd_pair(
        dy, dh3, x2, P["norm_ffn_pre"], xa, P["norm_xa_post"], name="norm_bwd_3",
        rider=_send_c([blocks[n] for n in REDUCE_FFN]))
    pair = {n: _pair_sum(blocks[n], r, core, name="pair_sum_" + n) for n, r in zip(REDUCE_FFN, from_core)}

    G["xa_wo"] = mm(o2, dxa, ta=True, name="mm_xo_dw")
    do2 = mm(dxa, W["xa_wo"], tb=True, out_dtype=BF16, name="mm_xo_dx")
    dq2, dk2, dv2 = _xattn_bwd(q2, k2, v2, do2, name="xattn_bwd")
    G["xa_wq"] = mm(h2, dq2, ta=True, name="mm_xq_dw")
    dh2 = mm(dq2, W["xa_wq"], tb=True, name="mm_xq_dx")
    G["xa_wk"] = mm(mem_n, dk2, ta=True, name="mm_xk_dw")
    G["xa_wv"] = mm(mem_n, dv2, ta=True, name="mm_xv_dw")
    dmem_n = jnp.concatenate([dk2, dv2], axis=1)
    wkv = jnp.concatenate([W["xa_wk"], W["xa_wv"]], axis=1)
    dmem = mm(dmem_n, wkv, tb=True, name="mm_xkv_dx")
    _, G["norm_mem"] = _norm_bwd_single(None, dmem, mem, P["norm_mem"], name="norm_bwd_mem")
    (dx1, dmo, G["norm_xa_pre"], G["norm_mix_post"]), _ = _norm_bwd_pair(
        dx2, dh2, x1, P["norm_xa_pre"], mo, P["norm_mix_post"], name="norm_bwd_2")

    G["w_out"] = mm(merged, dmo, ta=True, name="mm_out_dw")
    dmerged = mm(dmo, W["w_out"], tb=True, name="mm_out_dx")
    do_attn, do_ssm, dgate, G["b_gate"], G["w_branch_attn"], G["w_branch_ssm"] = _merge_bwd(
        dmerged, proj, o_attn, o_ssm, W["w_branch_attn"], W["w_branch_ssm"], P["b_gate"], name="merge_bwd")
    dy_pre, G["ssm_w_glu"], G["ssm_b_glu"] = _glu_bwd(y_pre, do_ssm, W["ssm_w_glu"], P["ssm_b_glu"], name="glu_bwd")
    blocks.update({n: _grad_blocks(n, G[n]) for n in REDUCE_MID})
    (du, dbd_re, dbd_im, dcd_re, dcd_imneg, G["ssm_d"], dl_re, dl_im), brought = _ssm_bwd(
        dy_pre, proj, x_re, x_im, bd_re, bd_im, cd_re, cd_imneg, P["ssm_d"], tab_b, name="ssm_bwd",
        rider=_Exchange.join(_send_c([blocks[n] for n in REDUCE_MID]), _scatter_xy([pair[n] for n in REDUCE_FFN])))
    from_core, from_chips = brought[:len(REDUCE_MID)], brought[len(REDUCE_MID):]
    reduced = {n: (pair[n], parts) for n, parts in zip(REDUCE_FFN, from_chips)}
    pair.update({n: _pair_sum(blocks[n], r, core, name="pair_sum_" + n) for n, r in zip(REDUCE_MID, from_core)})
    G["ssm_c_re"] = _c_from_cd(dcd_re)
    G["ssm_c_im"] = -_c_from_cd(dcd_imneg)
    dlam_re = jnp.sum(dl_re, axis=1).reshape(SSM_GROUPS, SSM_STATE)
    dlam_im = jnp.sum(dl_im, axis=1).reshape(SSM_GROUPS, SSM_STATE)
    (G["ssm_a_re"], G["ssm_a_im"], G["ssm_log_dt"], G["ssm_b_re"], G["ssm_b_im"]) = prep_vjp(
        (dlam_re, dlam_im, _bbar_from_bd(dbd_re), _bbar_from_bd(dbd_im)))
    (dq, dk, dv), from_chips = _sb_bwd(proj, sb_tot, sb_first, do_attn, name="sb_bwd",
                                       rider=_scatter_xy([pair[n] for n in REDUCE_MID]))
    reduced.update({n: (pair[n], parts) for n, parts in zip(REDUCE_MID, from_chips)})
    dproj = jnp.concatenate([dq, dk, dv, du, dgate], axis=1)
    G["w_in"] = mm(h1, dproj, ta=True, out_cb=W["w_in"].shape[2], name="mm_in_dw")
    dh1 = mm(dproj, W["w_in"], tb=True, name="mm_in_dx")
    grad_x, G["norm_mix_pre"] = _norm_bwd_single(dx1, dh1, x, P["norm_mix_pre"], name="norm_bwd_1")
    g_in = _grad_blocks("w_in", G["w_in"])
    from_core, = _send_c([g_in]).run(name="reduce_in_c")
    pair_in = _pair_sum(g_in, from_core, core, name="pair_sum_w_in")
    from_chips, = _scatter_xy([pair_in]).run(name="reduce_in_xy")
    reduced["w_in"] = (pair_in, from_chips)
    return loss, grad_x, G, reduced


MESH = pl.DeviceIdType.MESH
_HBM = pl.BlockSpec(memory_space=pl.ANY)
N_XY = 4
N_XY_PEERS = 3


def _xy_peers(x, y):
    return [(1 - x, y), (x, 1 - y), (1 - x, 1 - y)]


class _Exchange:
    def __init__(self, arrays, out_shapes, plan, n_copies, alias):
        self.arrays = list(arrays)
        self.out_shapes = list(out_shapes)
        self.plan = plan
        self.n_copies = n_copies
        self.alias = list(alias) if isinstance(alias, (list, tuple)) else [alias] * len(self.arrays)

    @property
    def n(self):
        return len(self.arrays)

    def aliases(self, first_in, first_out):
        return {first_in + k: first_out + k for k in range(self.n) if self.alias[k]}

    @staticmethod
    def join(a, b):
        def plan(k, src, dst, x, y, c):
            return a.plan(k, src, dst, x, y, c) if k < a.n else b.plan(k - a.n, src, dst, x, y, c)

        return _Exchange(a.arrays + b.arrays, a.out_shapes + b.out_shapes, plan, max(a.n_copies, b.n_copies),
                         a.alias + b.alias)

    def sems(self):
        shape = (self.n, self.n_copies)
        return [pltpu.SemaphoreType.DMA(shape), pltpu.SemaphoreType.DMA(shape)]

    def _copies(self, ins, outs, send_sems, recv_sems):
        x, y, c = lax.axis_index("x"), lax.axis_index("y"), lax.axis_index("c")
        sends, lands = [], []
        for k in range(self.n):
            for j, (src, dst, dev, land) in enumerate(self.plan(k, ins[k], outs[k], x, y, c)):
                sems = dict(send_sem=send_sems.at[k, j], recv_sem=recv_sems.at[k, j], device_id=dev, device_id_type=MESH)
                sends.append(pltpu.make_async_remote_copy(src_ref=src, dst_ref=dst, **sems))
                lands.append(pltpu.make_async_remote_copy(src_ref=src, dst_ref=land, **sems))
        return sends, lands

    def start(self, ins, outs, send_sems, recv_sems):
        for cp in self._copies(ins, outs, send_sems, recv_sems)[0]:
            cp.start()

    def finish(self, ins, outs, send_sems, recv_sems):
        sends, lands = self._copies(ins, outs, send_sems, recv_sems)
        for cp in lands:
            cp.wait_recv()
        for cp in sends:
            cp.wait_send()

    def run(self, *, name):
        n = self.n

        def body(*refs):
            parts = (refs[:n], refs[n:2 * n], refs[2 * n], refs[2 * n + 1])
            self.start(*parts)
            self.finish(*parts)

        return pl.pallas_call(
            body, name=name, out_shape=tuple(self.out_shapes),
            in_specs=[_HBM] * n, out_specs=tuple([_HBM] * n),
            input_output_aliases=self.aliases(0, 0),
            scratch_shapes=self.sems(),
        )(*self.arrays)


def _call(host_body, *, name, grid, in_specs, out_specs, out_shape, scratch_shapes, operands, rider=None):
    out_specs, out_shape = tuple(out_specs), tuple(out_shape)
    if rider is None:
        res = pl.pallas_call(
            host_body, name=name, grid=grid, in_specs=list(in_specs), out_specs=out_specs, out_shape=out_shape,
            scratch_shapes=list(scratch_shapes), compiler_params=_cparams(*["arbitrary"] * len(grid)),
        )(*operands)
        return tuple(res), None
    n, n_in, n_out, n_scr = rider.n, len(in_specs), len(out_specs), len(scratch_shapes)

    def body(*refs):
        pos = [0]

        def take(count):
            pos[0] += count
            return refs[pos[0] - count:pos[0]]

        h_in, r_in, h_out, r_out, h_scr = take(n_in), take(n), take(n_out), take(n), take(n_scr)
        send_sems, recv_sems = take(2)
        ids = [pl.program_id(a) for a in range(len(grid))]
        first = functools.reduce(jnp.logical_and, [i == 0 for i in ids])
        last = functools.reduce(jnp.logical_and, [i == g - 1 for i, g in zip(ids, grid)])

        @pl.when(first)
        def _():
            rider.start(r_in, r_out, send_sems, recv_sems)

        host_body(*h_in, *h_out, *h_scr)

        @pl.when(last)
        def _():
            rider.finish(r_in, r_out, send_sems, recv_sems)

    res = pl.pallas_call(
        body, name=name, grid=grid,
        in_specs=list(in_specs) + [_HBM] * n, out_specs=out_specs + tuple([_HBM] * n),
        out_shape=out_shape + tuple(rider.out_shapes),
        input_output_aliases=rider.aliases(n_in, n_out),
        scratch_shapes=list(scratch_shapes) + rider.sems(),
        compiler_params=_cparams(*["arbitrary"] * len(grid)),
    )(*operands, *rider.arrays)
    return tuple(res[:n_out]), list(res[n_out:])


def _same(arrays):
    return [jax.ShapeDtypeStruct(a.shape, a.dtype) for a in arrays]


def _fill_xy(bufs):
    def plan(k, src, dst, x, y, c):
        mine = 2 * x + y
        return [(src.at[mine, c], dst.at[mine, c], (px, py, c), dst.at[2 * px + py, c]) for px, py in _xy_peers(x, y)]

    return _Exchange(bufs, _same(bufs), plan, N_XY_PEERS, alias=True)


def _fill_c(bufs):
    def plan(k, src, dst, x, y, c):
        return [(src.at[:, c], dst.at[:, c], (x, y, 1 - c), dst.at[:, 1 - c])]

    return _Exchange(bufs, _same(bufs), plan, 1, alias=True)


def _send_c(srcs):
    def plan(k, src, dst, x, y, c):
        return [(src.at[:, 1 - c], dst, (x, y, 1 - c), dst)]

    outs = [jax.ShapeDtypeStruct(a.shape[:1] + a.shape[2:], a.dtype) for a in srcs]
    return _Exchange(srcs, outs, plan, 1, alias=False)


def _scatter_xy(srcs):
    def plan(k, src, dst, x, y, c):
        return [(src.at[2 * px + py], dst.at[j], (px, py, c), dst.at[j]) for j, (px, py) in enumerate(_xy_peers(x, y))]

    outs = [jax.ShapeDtypeStruct((N_XY_PEERS,) + a.shape[1:], a.dtype) for a in srcs]
    return _Exchange(srcs, outs, plan, N_XY_PEERS, alias=False)


PACK_COLS = 1024
WIRE_DTYPE = BF16


def _pair_sum(g8, recv, core, *, name):
    n, _, R, C = g8.shape
    tr = _pick(R, (128, 64, 32, 16, 8))

    def body(core_ref, a_ref, b_ref, o_ref):
        o_ref[...] = (a_ref[0] + b_ref[...]).astype(WIRE_DTYPE)

    return pl.pallas_call(
        body, name=name, out_shape=jax.ShapeDtypeStruct((n, R, C), WIRE_DTYPE),
        grid_spec=pltpu.PrefetchScalarGridSpec(
            num_scalar_prefetch=1, grid=(n, R // tr),
            in_specs=[pl.BlockSpec((1, 1, tr, C), lambda s, i, core_ref: (s, core_ref[0], i, 0)),
                      pl.BlockSpec((1, tr, C), lambda s, i, core_ref: (s, i, 0))],
            out_specs=pl.BlockSpec((1, tr, C), lambda s, i, core_ref: (s, i, 0))),
        compiler_params=_cparams("parallel", "parallel"),
    )(core, g8, recv)


def _adamw_math(w, g, m, v):
    m = ADAM_B1 * m + (1.0 - ADAM_B1) * g
    v = ADAM_B2 * v + (1.0 - ADAM_B2) * (g * g)
    m_hat = m / (1.0 - ADAM_B1 ** ADAM_STEP)
    v_hat = v / (1.0 - ADAM_B2 ** ADAM_STEP)
    delta = -ADAM_LR * (m_hat / (jnp.sqrt(v_hat) + ADAM_EPS) + ADAM_WD * w)
    return delta, m, v


def _reduce_adamw(parts, w, m, v, *, own=None, own_slot=None, name):
    n, R, C = parts.shape
    tr = _pick(R, (128, 64, 32, 16, 8))
    has_own = own is not None

    def body(*refs):
        if has_own:
            _, own_ref, parts_ref, w_ref, m_ref, v_ref, g_ref, d_ref, nm_ref, nv_ref = refs
            g = own_ref[0].astype(F32)
            first = 0
        else:
            parts_ref, w_ref, m_ref, v_ref, g_ref, d_ref, nm_ref, nv_ref = refs
            g = parts_ref[0]
            first = 1
        for k in range(first, n):
            g = g + parts_ref[k].astype(F32)
        g_ref[...] = g
        d_ref[...], nm_ref[...], nv_ref[...] = _adamw_math(w_ref[...], g, m_ref[...], v_ref[...])

    out = jax.ShapeDtypeStruct((R, C), F32)
    if has_own:
        row = pl.BlockSpec((tr, C), lambda i, s: (i, 0))
        return pl.pallas_call(
            body, name=name, out_shape=(out, out, out, out),
            grid_spec=pltpu.PrefetchScalarGridSpec(
                num_scalar_prefetch=1, grid=(R // tr,),
                in_specs=[pl.BlockSpec((1, tr, C), lambda i, s: (s[0], i, 0)),
                          pl.BlockSpec((n, tr, C), lambda i, s: (0, i, 0)), row, row, row],
                out_specs=(row, row, row, row)),
            compiler_params=_cparams("parallel"),
        )(own_slot, own, parts, w, m, v)
    row = pl.BlockSpec((tr, C), lambda i: (i, 0))
    return pl.pallas_call(
        body, name=name, out_shape=(out, out, out, out), grid=(R // tr,),
        in_specs=[pl.BlockSpec((n, tr, C), lambda i: (0, i, 0)), row, row, row],
        out_specs=(row, row, row, row), compiler_params=_cparams("parallel"),
    )(parts, w, m, v)


SHARDED = (("w_in", (1024, 4096), 1), ("ssm_w_glu", (512, 512), 0), ("w_branch_attn", (512, 1024), 1),
           ("w_branch_ssm", (512, 1024), 1), ("w_out", (1024, 1024), 0), ("xa_wq", (1024, 1024), 0),
           ("xa_wk", (1024, 1024), 0), ("xa_wv", (1024, 1024), 0), ("xa_wo", (1024, 1024), 0),
           ("ffn_w_up", (1024, 5632), 1), ("ffn_conv_w", (3, 5632), 1), ("ffn_w_down", (2816, 1024), 0))
REPLICATED = (("norm_mix_pre", (1024,)), ("norm_mix_post", (1024,)), ("b_gate", (2048,)), ("ssm_a_re", (32, 64)),
              ("ssm_a_im", (32, 64)), ("ssm_log_dt", (32,)), ("ssm_b_re", (32, 64, 16)), ("ssm_b_im", (32, 64, 16)),
              ("ssm_c_re", (32, 16, 64)), ("ssm_c_im", (32, 16, 64)), ("ssm_d", (512,)), ("ssm_b_glu", (512,)),
              ("norm_xa_pre", (1024,)), ("norm_xa_post", (1024,)), ("norm_mem", (1024,)), ("norm_ffn_pre", (1024,)),
              ("norm_ffn_post", (1024,)), ("ffn_conv_b", (5632,)))
PARAM_ORDER = ("norm_mix_pre", "norm_mix_post", "w_in", "b_gate", "ssm_a_re", "ssm_a_im", "ssm_log_dt", "ssm_b_re",
               "ssm_b_im", "ssm_c_re", "ssm_c_im", "ssm_d", "ssm_w_glu", "ssm_b_glu", "w_branch_attn", "w_branch_ssm",
               "w_out", "norm_xa_pre", "norm_xa_post", "norm_mem", "xa_wq", "xa_wk", "xa_wv", "xa_wo", "norm_ffn_pre",
               "norm_ffn_post", "ffn_w_up", "ffn_conv_w", "ffn_conv_b", "ffn_w_down")
SMALL_ROWS = 160
FF_LOCAL = 2 * D_FF // N_DEV
FF_LOCAL_PAD = 768
FF_PAD = (N_DEV // 2) * FF_LOCAL_PAD


def _local_shape(shape, axis):
    return tuple(s // N_DEV if a == axis else s for a, s in enumerate(shape))


def _pad_cols(a, width):
    return jnp.pad(a, [(0, 0)] * (a.ndim - 1) + [(0, width - a.shape[-1])])


def _blocks_to_cols(a8):
    return a8.transpose(1, 0, 2).reshape(a8.shape[1], N_DEV * a8.shape[2])


def _cols_to_blocks(a, cb):
    return a.reshape(a.shape[0], N_DEV, cb).transpose(1, 0, 2)


FF_PADDED = ("ffn_w_up", "ffn_conv_w")
LATE = tuple(n for n, _, _ in SHARDED if n != "w_in")
REDUCE_FFN = ("ffn_w_up", "ffn_conv_w", "ffn_w_down")
REDUCE_MID = ("xa_wo", "xa_wq", "xa_wk", "xa_wv", "w_out", "w_branch_attn", "w_branch_ssm", "ssm_w_glu")
SHARD_AXIS = {n: ax for n, _, ax in SHARDED}
FULL_SHAPE = {n: s for n, s, _ in SHARDED}


def _as_local(n, a):
    return _pad_cols(a, FF_LOCAL_PAD) if n in FF_PADDED else a


def _weights_from_wire(wire):
    full = {n: b.reshape((N_DEV,) + b.shape[2:]) for n, b in wire.items()}
    W = {n: a.reshape(FULL_SHAPE[n]) if SHARD_AXIS[n] == 0 else a for n, a in full.items()}
    for n in ("w_branch_attn", "w_branch_ssm", "ffn_conv_w"):
        W[n] = _blocks_to_cols(full[n])
    W["ffn_w_down"] = jnp.pad(W["ffn_w_down"].reshape(N_DEV // 2, FF_LOCAL, D_MODEL),
                              ((0, 0), (0, FF_LOCAL_PAD - FF_LOCAL), (0, 0))).reshape(FF_PAD, D_MODEL)
    return W


def _grad_blocks(n, g):
    if n in ("w_branch_attn", "w_branch_ssm"):
        g = _cols_to_blocks(g, D_MODEL // N_DEV)
    elif n == "ffn_conv_w":
        g = _cols_to_blocks(g, FF_LOCAL_PAD)
    elif n == "ffn_w_down":
        g = g.reshape(N_DEV // 2, FF_LOCAL_PAD, D_MODEL)[:, :FF_LOCAL]
    local = _local_shape(FULL_SHAPE[n], SHARD_AXIS[n])
    if n in FF_PADDED:
        local = local[:-1] + (FF_LOCAL_PAD,)
    return g.reshape((N_XY, 2) + local)


def _pack_small(d):
    flat = jnp.concatenate([d[n].reshape(-1) for n, _ in REPLICATED])
    return _pad_cols(flat, SMALL_ROWS * PACK_COLS).reshape(SMALL_ROWS, PACK_COLS)


def _unpack_small(buf):
    flat = buf.reshape(-1)
    out, off = {}, 0
    for n, shape in REPLICATED:
        size = math.prod(shape)
        out[n] = flat[off:off + size]
        off += size
    return out


def kernel(x, mem, norm_mix_pre, norm_mix_post, w_in, b_gate, ssm_a_re, ssm_a_im, ssm_log_dt, ssm_b_re, ssm_b_im, ssm_c_re, ssm_c_im, ssm_d, ssm_w_glu, ssm_b_glu, w_branch_attn, w_branch_ssm, w_out, norm_xa_pre, norm_xa_post, norm_mem, xa_wq, xa_wk, xa_wv, xa_wo, norm_ffn_pre, norm_ffn_post, ffn_w_up, ffn_conv_w, ffn_conv_b, ffn_w_down, loss_target, m_norm_mix_pre, m_norm_mix_post, m_w_in, m_b_gate, m_ssm_a_re, m_ssm_a_im, m_ssm_log_dt, m_ssm_b_re, m_ssm_b_im, m_ssm_c_re, m_ssm_c_im, m_ssm_d, m_ssm_w_glu, m_ssm_b_glu, m_w_branch_attn, m_w_branch_ssm, m_w_out, m_norm_xa_pre, m_norm_xa_post, m_norm_mem, m_xa_wq, m_xa_wk, m_xa_wv, m_xa_wo, m_norm_ffn_pre, m_norm_ffn_post, m_ffn_w_up, m_ffn_conv_w, m_ffn_conv_b, m_ffn_w_down, v_norm_mix_pre, v_norm_mix_post, v_w_in, v_b_gate, v_ssm_a_re, v_ssm_a_im, v_ssm_log_dt, v_ssm_b_re, v_ssm_b_im, v_ssm_c_re, v_ssm_c_im, v_ssm_d, v_ssm_w_glu, v_ssm_b_glu, v_w_branch_attn, v_w_branch_ssm, v_w_out, v_norm_xa_pre, v_norm_xa_post, v_norm_mem, v_xa_wq, v_xa_wk, v_xa_wv, v_xa_wo, v_norm_ffn_pre, v_norm_ffn_post, v_ffn_w_up, v_ffn_conv_w, v_ffn_conv_b, v_ffn_w_down):
    args = dict(locals())
    w_loc = {n: args[n][0] for n in PARAM_ORDER}
    m_loc = {n: args["m_" + n][0] for n in PARAM_ORDER}
    v_loc = {n: args["v_" + n][0] for n in PARAM_ORDER}
    core_i = lax.axis_index("c")
    chip_i = 2 * lax.axis_index("x") + lax.axis_index("y")
    core = core_i.astype(jnp.int32).reshape(1)
    chip = chip_i.astype(jnp.int32).reshape(1)

    def in_place(a):
        buf = lax.empty((N_XY, 2) + a.shape, a.dtype)
        return lax.dynamic_update_slice(buf, a[None, None], (chip_i, core_i) + (0,) * a.ndim)

    as_wire = lambda n: in_place(_as_local(n, w_loc[n]).astype(F32 if n == "ffn_conv_w" else BF16))
    wire_in = _fill_c(_fill_xy([as_wire("w_in")]).run(name="gather_in_xy")).run(name="gather_in_c")[0]
    w_in_full = wire_in.reshape((N_DEV,) + wire_in.shape[2:])

    P = {}
    for n, shape in REPLICATED:
        P[n] = w_loc[n] if len(shape) > 1 or n == "ssm_log_dt" else w_loc[n].reshape(1, -1)
    P["ffn_conv_b"] = _pad_cols(w_loc["ffn_conv_b"].reshape(N_DEV, FF_LOCAL), FF_LOCAL_PAD).reshape(1, 2 * FF_PAD)

    loss, grad_x, G, reduced = _local_step(x[0], mem[0], loss_target[0], w_in_full, [as_wire(n) for n in LATE], P, core)
    loss = lax.psum(loss[0, 0], ("x", "y", "c"))

    big_out = {}
    for n, (own, parts) in reduced.items():
        res = _reduce_adamw(parts, _as_local(n, w_loc[n]), _as_local(n, m_loc[n]), _as_local(n, v_loc[n]),
                            own=own, own_slot=chip, name="adamw_" + n)
        big_out[n] = [r[:, :FF_LOCAL] if n in FF_PADDED else r for r in res]

    G["ffn_conv_b"] = G["ffn_conv_b"].reshape(N_DEV, FF_LOCAL_PAD)[:, :FF_LOCAL]
    parts, = _fill_c(_fill_xy([in_place(_pack_small(G))]).run(name="gather_g_xy")).run(name="gather_g_c")
    parts = parts.reshape((N_DEV,) + parts.shape[2:])
    small_out = _reduce_adamw(parts, _pack_small(w_loc), _pack_small(m_loc), _pack_small(v_loc), name="adamw_replicated")
    small_out = [_unpack_small(b) for b in small_out]

    outs = [loss, grad_x[None]]
    for k in range(4):
        for n in PARAM_ORDER:
            src = big_out[n][k] if n in big_out else small_out[k][n]
            outs.append(src.reshape(args[n].shape))
    return tuple(outs)
```

```python
import functools
import math

import jax
import jax.numpy as jnp
from jax import lax
from jax.experimental import pallas as pl
from jax.experimental.pallas import tpu as pltpu

F32 = jnp.float32
BF16 = jnp.bfloat16

D_MODEL = 1024
SB_HEADS = 8
SB_HEAD_DIM = 64
SB_WIDTH = 512
SSM_WIDTH = 512
SSM_GROUP = 16
SSM_GROUPS = 32
SSM_STATE = 64
XA_HEADS = 4
XA_HEAD_DIM = 256
D_FF = 2816
RMS_EPS = 1e-6
IN_WIDTH = 4096
N_DEV = 8

ADAM_LR = 0.001
ADAM_B1 = 0.9
ADAM_B2 = 0.999
ADAM_EPS = 1e-08
ADAM_WD = 0.01
ADAM_STEP = 10

LANES = 128
SUBLANES = 8
VMEM_LIMIT = 48 * 1024 * 1024

_GELU_C = math.sqrt(2.0 / math.pi)


def _cparams(*sem):
    return pltpu.CompilerParams(dimension_semantics=sem, vmem_limit_bytes=VMEM_LIMIT)


def _pick(n, cands):
    for c in cands:
        if n % c == 0:
            return c
    return n


def _gelu(x):
    return 0.5 * x * (1.0 + jnp.tanh(_GELU_C * (x + 0.044715 * x * x * x)))


def _gelu_and_grad(x):
    t = jnp.tanh(_GELU_C * (x + 0.044715 * x * x * x))
    g = 0.5 * x * (1.0 + t)
    dg = 0.5 * (1.0 + t) + 0.5 * x * (1.0 - t * t) * _GELU_C * (1.0 + 3.0 * 0.044715 * x * x)
    return g, dg


def _sigmoid(x):
    return 1.0 / (1.0 + jnp.exp(-x))


def _dot(a, b, ca, cb):
    return lax.dot_general(a.astype(BF16), b.astype(BF16), (((ca,), (cb,)), ((), ())),
                           preferred_element_type=F32)


MM_TILES = (1024, 768, 512, 256, 128)


def _matmul(a, b, *, ta=False, tb=False, out_dtype=F32, name, b_block0=0, n_blocks=None,
            out_cb=None, out_into=None, out_block0=0, acc_in=None, rider=None):
    if ta:
        K, M = a.shape
    else:
        M, K = a.shape
    b_cb = None
    if b.ndim == 3:
        b_cb = b.shape[2]
        n_blocks = b.shape[0] - b_block0 if n_blocks is None else n_blocks
        N, K2 = (b.shape[1], n_blocks * b_cb) if tb else (n_blocks * b_cb, b.shape[1])
    elif tb:
        N, K2 = b.shape
    else:
        K2, N = b.shape
    assert K == K2, (a.shape, b.shape, ta, tb)
    if out_into is not None:
        out_cb = out_into.shape[2]
    tm = _pick(M, MM_TILES)
    n_unit = math.gcd(N, math.gcd(b_cb if (b_cb and not tb) else N, out_cb or N))
    tn = _pick(n_unit, MM_TILES)
    k_unit = b_cb if (b_cb and tb) else K
    tk = _pick(k_unit, MM_TILES)
    nk = K // tk
    ca, cb = (0 if ta else 1), (1 if tb else 0)
    has_acc = acc_in is not None
    has_into = out_into is not None

    def body(*refs):
        a_ref, b_ref = refs[0], refs[1]
        pos = 2
        c_ref = None
        if has_acc:
            c_ref = refs[pos]
            pos += 1
        if has_into:
            pos += 1
        o_ref = refs[pos]
        p = _dot(a_ref[...], b_ref[...], ca, cb)
        if nk == 1:
            o_ref[...] = ((p + c_ref[...]) if has_acc else p).astype(out_dtype)
        else:
            acc_ref = refs[pos + 1]
            k = pl.program_id(2)

            @pl.when(k == 0)
            def _():
                acc_ref[...] = (p + c_ref[...]) if has_acc else p

            @pl.when(k > 0)
            def _():
                acc_ref[...] += p

            @pl.when(k == nk - 1)
            def _():
                o_ref[...] = acc_ref[...].astype(out_dtype)

    a_spec = pl.BlockSpec((tk, tm), lambda j, i, k: (k, i)) if ta else pl.BlockSpec((tm, tk), lambda j, i, k: (i, k))
    if b_cb is None:
        b_spec = pl.BlockSpec((tn, tk), lambda j, i, k: (j, k)) if tb else pl.BlockSpec((tk, tn), lambda j, i, k: (k, j))
    elif tb:
        per = b_cb // tk
        b_spec = pl.BlockSpec((None, tn, tk), lambda j, i, k: (b_block0 + k // per, j, k % per))
    else:
        per = b_cb // tn
        b_spec = pl.BlockSpec((None, tk, tn), lambda j, i, k: (b_block0 + j // per, k, j % per))
    in_specs = [a_spec, b_spec]
    operands = [a, b]
    aliases = {}
    if has_acc:
        in_specs.append(pl.BlockSpec((tm, tn), lambda j, i, k: (i, j)))
        operands.append(acc_in)
    if has_into:
        aliases = {len(operands): 0}
        in_specs.append(pl.BlockSpec(memory_space=pl.ANY))
        operands.append(out_into)
    if out_cb is None:
        out_shape = jax.ShapeDtypeStruct((M, N), out_dtype)
        out_spec = pl.BlockSpec((tm, tn), lambda j, i, k: (i, j))
    else:
        per_o = out_cb // tn
        out_shape = (jax.ShapeDtypeStruct(out_into.shape, out_into.dtype) if has_into
                     else jax.ShapeDtypeStruct((N // out_cb, M, out_cb), out_dtype))
        out_spec = pl.BlockSpec((None, tm, tn), lambda j, i, k: (out_block0 + j // per_o, i, j % per_o))
    if rider is not None:
        assert not has_into
        (out,), brought = _call(body, name=name, rider=rider, grid=(N // tn, M // tm, nk), in_specs=in_specs,
                                out_specs=(out_spec,), out_shape=(out_shape,), operands=operands,
                                scratch_shapes=[] if nk == 1 else [pltpu.VMEM((tm, tn), F32)])
        return out, brought
    return pl.pallas_call(
        body, name=name, out_shape=out_shape,
        grid=(N // tn, M // tm, nk),
        in_specs=in_specs, out_specs=out_spec, input_output_aliases=aliases,
        scratch_shapes=[] if nk == 1 else [pltpu.VMEM((tm, tn), F32)],
        compiler_params=_cparams("parallel", "parallel", "arbitrary"),
    )(*operands)


def _rms(x, g):
    r = lax.rsqrt(jnp.mean(x * x, axis=-1, keepdims=True) + RMS_EPS)
    return x * r * g


def _rms_bwd(dy, x, g):
    r = lax.rsqrt(jnp.mean(x * x, axis=-1, keepdims=True) + RMS_EPS)
    xh = x * r
    dxh = dy * g
    dx = r * (dxh - xh * jnp.mean(dxh * xh, axis=-1, keepdims=True))
    dg = jnp.sum(dy * xh, axis=0, keepdims=True)
    return dx, dg


def _row_tile(rows):
    return _pick(rows, (512, 256, 128, 64, 32, 16, 8))


def _rms_fwd(x, g, *, name, rider=None):
    R, D = x.shape
    tr = _row_tile(R)

    def body(x_ref, g_ref, h_ref):
        h_ref[...] = _rms(x_ref[...], g_ref[...]).astype(BF16)

    (h,), brought = _call(
        body, name=name, rider=rider, out_shape=(jax.ShapeDtypeStruct((R, D), BF16),), grid=(R // tr,),
        in_specs=[pl.BlockSpec((tr, D), lambda i: (i, 0)), pl.BlockSpec((1, D), lambda i: (0, 0))],
        out_specs=(pl.BlockSpec((tr, D), lambda i: (i, 0)),), scratch_shapes=[], operands=(x, g))
    return h if rider is None else (h, brought)


def _resnorm_norm(x, z, g_post, g_next, *, name):
    R, D = x.shape
    tr = _row_tile(R)

    def body(x_ref, z_ref, gp_ref, gn_ref, xn_ref, h_ref):
        xn = x_ref[...] + _rms(z_ref[...], gp_ref[...])
        xn_ref[...] = xn
        h_ref[...] = _rms(xn, gn_ref[...]).astype(BF16)

    row = pl.BlockSpec((tr, D), lambda i: (i, 0))
    vec = pl.BlockSpec((1, D), lambda i: (0, 0))
    return pl.pallas_call(
        body, name=name,
        out_shape=(jax.ShapeDtypeStruct((R, D), F32), jax.ShapeDtypeStruct((R, D), BF16)),
        grid=(R // tr,), in_specs=[row, row, vec, vec], out_specs=(row, row),
        compiler_params=_cparams("parallel"),
    )(x, z, g_post, g_next)


def _final_loss(x, z, g_post, target, *, name):
    R, D = x.shape
    tr = _row_tile(R)

    def body(x_ref, z_ref, gp_ref, t_ref, loss_ref, dy_ref, dz_ref, dg_ref):
        i = pl.program_id(0)
        z = z_ref[...]
        g = gp_ref[...]
        err = x_ref[...] + _rms(z, g) - t_ref[...]
        dy = err * (1.0 / D)
        dy_ref[...] = dy
        dz, dg = _rms_bwd(dy, z, g)
        dz_ref[...] = dz.astype(BF16)
        part = 0.5 * jnp.sum(jnp.sum(err * err, axis=-1, keepdims=True) * (1.0 / D), axis=0, keepdims=True)

        @pl.when(i == 0)
        def _():
            loss_ref[...] = part
            dg_ref[...] = dg

        @pl.when(i > 0)
        def _():
            loss_ref[...] += part
            dg_ref[...] += dg

    row = pl.BlockSpec((tr, D), lambda i: (i, 0))
    vec = pl.BlockSpec((1, D), lambda i: (0, 0))
    return pl.pallas_call(
        body, name=name,
        out_shape=(jax.ShapeDtypeStruct((1, 1), F32), jax.ShapeDtypeStruct((R, D), F32),
                   jax.ShapeDtypeStruct((R, D), BF16), jax.ShapeDtypeStruct((1, D), F32)),
        grid=(R // tr,), in_specs=[row, row, vec, row],
        out_specs=(pl.BlockSpec((1, 1), lambda i: (0, 0)), row, row, vec),
        compiler_params=_cparams("arbitrary"),
    )(x, z, g_post, target)


def _norm_bwd_pair(dres, dh, xk, g_pre, zprev, g_prev_post, *, name, rider=None):
    R, D = xk.shape
    tr = _row_tile(R)

    def body(dres_ref, dh_ref, x_ref, gpre_ref, z_ref, gpost_ref, dx_ref, dz_ref, dgpre_ref, dgpost_ref):
        i = pl.program_id(0)
        d1, dgpre = _rms_bwd(dh_ref[...], x_ref[...], gpre_ref[...])
        dx = dres_ref[...] + d1
        dx_ref[...] = dx
        dz, dgpost = _rms_bwd(dx, z_ref[...], gpost_ref[...])
        dz_ref[...] = dz.astype(BF16)

        @pl.when(i == 0)
        def _():
            dgpre_ref[...] = dgpre
            dgpost_ref[...] = dgpost

        @pl.when(i > 0)
        def _():
            dgpre_ref[...] += dgpre
            dgpost_ref[...] += dgpost

    row = pl.BlockSpec((tr, D), lambda i: (i, 0))
    vec = pl.BlockSpec((1, D), lambda i: (0, 0))
    return _call(
        body, name=name, rider=rider,
        out_shape=(jax.ShapeDtypeStruct((R, D), F32), jax.ShapeDtypeStruct((R, D), BF16),
                   jax.ShapeDtypeStruct((1, D), F32), jax.ShapeDtypeStruct((1, D), F32)),
        grid=(R // tr,), in_specs=[row, row, row, vec, row, vec], out_specs=(row, row, vec, vec),
        scratch_shapes=[], operands=(dres, dh, xk, g_pre, zprev, g_prev_post))


def _norm_bwd_single(dres, dh, xk, g_pre, *, name, rider=None):
    R, D = xk.shape
    tr = _row_tile(R)
    has_res = dres is not None

    def body(*refs):
        if has_res:
            dres_ref, dh_ref, x_ref, gpre_ref, dx_ref, dgpre_ref = refs
        else:
            dh_ref, x_ref, gpre_ref, dx_ref, dgpre_ref = refs
        i = pl.program_id(0)
        d1, dgpre = _rms_bwd(dh_ref[...], x_ref[...], gpre_ref[...])
        dx_ref[...] = dres_ref[...] + d1 if has_res else d1

        @pl.when(i == 0)
        def _():
            dgpre_ref[...] = dgpre

        @pl.when(i > 0)
        def _():
            dgpre_ref[...] += dgpre

    row = pl.BlockSpec((tr, D), lambda i: (i, 0))
    vec = pl.BlockSpec((1, D), lambda i: (0, 0))
    ins = ([dres] if has_res else []) + [dh, xk, g_pre]
    res, brought = _call(
        body, name=name, rider=rider,
        out_shape=(jax.ShapeDtypeStruct((R, D), F32), jax.ShapeDtypeStruct((1, D), F32)),
        grid=(R // tr,), in_specs=([row] if has_res else []) + [row, row, vec], out_specs=(row, vec),
        scratch_shapes=[], operands=ins)
    return res if rider is None else (res, brought)


SB_BLOCK = 256
SB_QBLOCK = 512
SB_DEAD = -104.0


def _sb_tri(kind):
    r = lax.broadcasted_iota(jnp.int32, (SB_BLOCK, SB_BLOCK), 0)
    c = lax.broadcasted_iota(jnp.int32, (SB_BLOCK, SB_BLOCK), 1)
    keep = {"after": r > c, "upto": r <= c, "before": r < c}[kind]
    return jnp.where(keep, 1.0, 0.0).astype(BF16)


def _running_sum(vals, tri):
    hi = vals.astype(BF16)
    lo = (vals - hi.astype(F32)).astype(BF16)
    return _dot(hi, tri, 1, 0) + _dot(lo, tri, 1, 0)


def _sb_scores(qm, k_blk):
    z = _dot(qm, k_blk, 1, 1)
    sp = jnp.maximum(z, 0.0) + jnp.log(1.0 + jnp.exp(-jnp.abs(z)))
    return z, sp


def _sb_causal(rows):
    r = lax.broadcasted_iota(jnp.int32, (rows, SB_BLOCK), 0)
    c = lax.broadcasted_iota(jnp.int32, (rows, SB_BLOCK), 1)
    return c < r


def _head_masks():
    lane = lax.broadcasted_iota(jnp.int32, (1, LANES), 1)
    return [jnp.where(lane < SB_HEAD_DIM, 1.0, 0.0), jnp.where(lane >= SB_HEAD_DIM, 1.0, 0.0)]


def _sb_fwd(proj, *, name, rider=None):
    S = proj.shape[0]
    T = SB_BLOCK
    TQ = min(SB_QBLOCK, S)
    span = TQ // T
    nq = S // TQ
    npair = SB_WIDTH // LANES
    scale = SB_HEAD_DIM ** -0.5

    def body(q_ref, k_ref, v_ref, o_ref, tot_ref, first_ref, acc_ref, run_ref):
        masks = _head_masks()
        tri = _sb_tri("after")
        first_ref[...] = jnp.zeros_like(first_ref)
        slot = lax.broadcasted_iota(jnp.int32, first_ref.shape, 1)

        def alive():
            reach = jnp.maximum(jnp.max(run_ref[0]), jnp.max(run_ref[1]))
            return (reach > SB_DEAD).astype(jnp.int32)

        def q_block(i, _):
            qrow = pl.ds(pl.multiple_of(i * TQ, TQ), TQ)
            q = q_ref[qrow, :] * scale
            qm = [(q * m).astype(BF16) for m in masks]
            acc_ref[...] = jnp.zeros_like(acc_ref)
            run_ref[...] = jnp.zeros_like(run_ref)

            def k_block(j, own):
                krow = pl.ds(pl.multiple_of(j * T, T), T)
                k_blk = k_ref[krow, :].astype(BF16)
                v_blk = v_ref[krow, :].astype(BF16)
                r0 = 0 if own is None else own * T
                rows = pl.ds(r0, TQ - r0)
                for h in range(2):
                    z, sp = _sb_scores(qm[h][r0:], k_blk)
                    causal = None if own is None else _sb_causal(TQ - r0)
                    lf = -sp if causal is None else jnp.where(causal, -sp, 0.0)
                    e = jnp.exp(z - sp + _running_sum(lf, tri) + run_ref[h, rows])
                    w = e if causal is None else jnp.where(causal, e, 0.0)
                    acc_ref[h, rows] += _dot(w, v_blk, 1, 0)
                    run_ref[h, rows] += jnp.sum(lf, axis=1, keepdims=True)

            for d in reversed(range(span)):
                k_block(i * span + d, d)

            def below(carry):
                jj, _ = carry
                k_block(i * span - 1 - jj, None)
                return jj + 1, alive()

            done, _ = lax.while_loop(lambda c: jnp.logical_and(c[0] < i * span, c[1] > 0), below, (jnp.int32(0), alive()))
            o_ref[qrow, :] = (acc_ref[0] * masks[0] + acc_ref[1] * masks[1]).astype(BF16)
            tot_ref[qrow, :] = run_ref[0] * masks[0] + run_ref[1] * masks[1]
            first_ref[...] = jnp.where(slot == i, (i * span - done).astype(F32), first_ref[...])
            return 0

        lax.fori_loop(0, nq, q_block, 0)

    blk = lambda off: pl.BlockSpec((S, LANES), lambda p: (0, off + p))
    return _call(
        body, name=name, rider=rider,
        out_shape=(jax.ShapeDtypeStruct((S, SB_WIDTH), BF16), jax.ShapeDtypeStruct((S, SB_WIDTH), F32),
                   jax.ShapeDtypeStruct((npair, SUBLANES, LANES), F32)),
        grid=(npair,),
        in_specs=[blk(0), blk(npair), blk(2 * npair)],
        out_specs=(blk(0), blk(0), pl.BlockSpec((1, SUBLANES, LANES), lambda p: (p, 0, 0))),
        scratch_shapes=[pltpu.VMEM((2, TQ, LANES), F32), pltpu.VMEM((2, TQ, 1), F32)],
        operands=(proj, proj, proj))


def _sb_bwd(proj, tot, first, do_attn, *, name, rider=None):
    S = proj.shape[0]
    T = SB_BLOCK
    TQ = min(SB_QBLOCK, S)
    span = TQ // T
    nq = S // TQ
    npair = SB_WIDTH // LANES
    scale = SB_HEAD_DIM ** -0.5

    def body(q_ref, k_ref, v_ref, tot_ref, first_ref, do_ref, dq_ref, dk_ref, dv_ref,
             dqacc_ref, dkacc_ref, dvacc_ref, run_ref, grun_ref):
        masks = _head_masks()
        tri_upto = _sb_tri("upto")
        tri_before = _sb_tri("before")
        dkacc_ref[...] = jnp.zeros_like(dkacc_ref)
        dvacc_ref[...] = jnp.zeros_like(dvacc_ref)
        slot = lax.broadcasted_iota(jnp.int32, first_ref.shape, 1)

        def q_block(i, _):
            qrow = pl.ds(pl.multiple_of(i * TQ, TQ), TQ)
            q = q_ref[qrow, :] * scale
            do = do_ref[qrow, :].astype(F32)
            tot = tot_ref[qrow, :]
            qm = [(q * m).astype(BF16) for m in masks]
            dom = [(do * m).astype(BF16) for m in masks]
            ltot = [jnp.sum(tot * m, axis=1, keepdims=True) * (1.0 / SB_HEAD_DIM) for m in masks]
            dqacc_ref[...] = jnp.zeros_like(dqacc_ref)
            run_ref[...] = jnp.zeros_like(run_ref)
            grun_ref[...] = jnp.zeros_like(grun_ref)

            def k_block(j, own):
                krow = pl.ds(pl.multiple_of(j * T, T), T)
                k_blk = k_ref[krow, :].astype(BF16)
                v_blk = v_ref[krow, :].astype(BF16)
                r0 = 0 if own is None else own * T
                rows = pl.ds(r0, TQ - r0)
                for h in range(2):
                    z, sp = _sb_scores(qm[h][r0:], k_blk)
                    causal = None if own is None else _sb_causal(TQ - r0)
                    lf = -sp if causal is None else jnp.where(causal, -sp, 0.0)
                    later = ltot[h][r0:] - run_ref[h, rows] - _running_sum(lf, tri_upto)
                    beta = jnp.exp(z - sp)
                    w = jnp.exp(z - sp + later)
                    if causal is not None:
                        w = jnp.where(causal, w, 0.0)
                    g = _dot(dom[h][r0:], v_blk, 1, 1) * w
                    gbefore = grun_ref[h, rows] + _dot(g, tri_before, 1, 0)
                    dz = g - beta * (g + gbefore)
                    if causal is not None:
                        dz = jnp.where(causal, dz, 0.0)
                    dz = dz.astype(BF16)
                    dqacc_ref[h, rows] += _dot(dz, k_blk, 1, 0)
                    dkacc_ref[krow, :] += _dot(dz, qm[h][r0:], 0, 0)
                    dvacc_ref[krow, :] += _dot(w, dom[h][r0:], 0, 0)
                    run_ref[h, rows] += jnp.sum(lf, axis=1, keepdims=True)
                    grun_ref[h, rows] += jnp.sum(g, axis=1, keepdims=True)

            def above(j, _):
                k_block(j, None)
                return 0

            first = jnp.max(jnp.where(slot == i, first_ref[...], 0.0)).astype(jnp.int32)
            lax.fori_loop(jnp.clip(first, 0, i * span), i * span, above, 0)
            for d in range(span):
                k_block(i * span + d, d)
            dq_ref[qrow, :] = ((dqacc_ref[0] * masks[0] + dqacc_ref[1] * masks[1]) * scale).astype(BF16)
            return 0

        lax.fori_loop(0, nq, q_block, 0)
        dk_ref[...] = dkacc_ref[...].astype(BF16)
        dv_ref[...] = dvacc_ref[...].astype(BF16)

    blk = lambda off: pl.BlockSpec((S, LANES), lambda p: (0, off + p))
    out = jax.ShapeDtypeStruct((S, SB_WIDTH), BF16)
    return _call(
        body, name=name, rider=rider, out_shape=(out, out, out), grid=(npair,),
        in_specs=[blk(0), blk(npair), blk(2 * npair), blk(0), pl.BlockSpec((1, SUBLANES, LANES), lambda p: (p, 0, 0)),
                  blk(0)],
        out_specs=(blk(0), blk(0), blk(0)),
        scratch_shapes=[pltpu.VMEM((2, TQ, LANES), F32), pltpu.VMEM((S, LANES), F32), pltpu.VMEM((S, LANES), F32),
                        pltpu.VMEM((2, TQ, 1), F32), pltpu.VMEM((2, TQ, 1), F32)],
        operands=(proj, proj, proj, tot, first, do_attn))


SSM_HALVES = 2
SSM_HALF_CH = SSM_WIDTH // SSM_HALVES
SSM_HALF_ST = SSM_GROUPS * SSM_STATE // SSM_HALVES
SSM_CHUNK = 512


def _cmul(ar, ai, br, bi):
    return ar * br - ai * bi, ar * bi + ai * br


def _ssm_tables(lam_re, lam_im):
    lr = lam_re.reshape(-1)
    li = lam_im.reshape(-1)
    pows = [(jnp.ones_like(lr), jnp.zeros_like(li)), (lr, li)]
    for _ in range(2, SUBLANES + 1):
        pows.append(_cmul(pows[-1][0], pows[-1][1], lr, li))
    row = jnp.arange(SUBLANES)[:, None]

    def shift_tab(d, keep):
        return [jnp.where(keep, pows[d][0][None, :], 0.0), jnp.where(keep, pows[d][1][None, :], 0.0)]

    fwd, bwd = [], []
    for d in (1, 2, 4):
        fwd += shift_tab(d, row >= d)
        bwd += shift_tab(d, row + d < SUBLANES)
    fwd += [jnp.stack([pows[r + 1][0] for r in range(SUBLANES)]), jnp.stack([pows[r + 1][1] for r in range(SUBLANES)])]
    bwd += [jnp.stack([pows[SUBLANES - r][0] for r in range(SUBLANES)]),
            jnp.stack([pows[SUBLANES - r][1] for r in range(SUBLANES)])]

    def halves(tabs):
        t = jnp.stack(tabs)
        return t.reshape(8, SUBLANES, SSM_HALVES, SSM_HALF_ST).transpose(2, 0, 1, 3)

    return halves(fwd), halves(bwd)


def _ssm_fwd(proj, bd_re, bd_im, cd_re, cd_imneg, d_skip, tab, *, name, rider=None):
    S = proj.shape[0]
    Tc = min(SSM_CHUNK, S)
    nc = S // Tc
    u_blk0 = (3 * SB_WIDTH) // SSM_HALF_CH

    def body(u_ref, bre_ref, bim_ref, cre_ref, cim_ref, d_ref, tab_ref, y_ref, xre_ref, xim_ref, cre_s, cim_s):
        c = pl.program_id(1)

        @pl.when(c == 0)
        def _():
            cre_s[...] = jnp.zeros_like(cre_s)
            cim_s[...] = jnp.zeros_like(cim_s)

        u = u_ref[...]
        ub = u.astype(BF16)
        xre_ref[...] = _dot(ub, bre_ref[0], 1, 0)
        xim_ref[...] = _dot(ub, bim_ref[0], 1, 0)

        def slab(k, carry):
            car_re, car_im = carry
            rows = pl.ds(pl.multiple_of(k * SUBLANES, SUBLANES), SUBLANES)
            sre = xre_ref[rows, :]
            sim = xim_ref[rows, :]
            for n, d in enumerate((1, 2, 4)):
                pre, pim = tab_ref[0, 2 * n], tab_ref[0, 2 * n + 1]
                rre = pltpu.roll(sre, d, 0)
                rim = pltpu.roll(sim, d, 0)
                sre, sim = sre + (pre * rre - pim * rim), sim + (pre * rim + pim * rre)
            pre, pim = tab_ref[0, 6], tab_ref[0, 7]
            sre, sim = sre + (pre * car_re - pim * car_im), sim + (pre * car_im + pim * car_re)
            xre_ref[rows, :] = sre
            xim_ref[rows, :] = sim
            last = (SUBLANES - 1, SUBLANES)
            return (jnp.broadcast_to(sre[last[0]:last[1], :], sre.shape),
                    jnp.broadcast_to(sim[last[0]:last[1], :], sim.shape))

        car = lax.fori_loop(0, Tc // SUBLANES, slab, (cre_s[...], cim_s[...]))
        cre_s[...] = car[0]
        cim_s[...] = car[1]
        y = _dot(xre_ref[...], cre_ref[0], 1, 0) + _dot(xim_ref[...], cim_ref[0], 1, 0)
        y_ref[...] = y + d_ref[...] * u

    return _call(
        body, name=name, rider=rider,
        out_shape=(jax.ShapeDtypeStruct((S, SSM_WIDTH), F32),
                   jax.ShapeDtypeStruct((S, SSM_HALVES * SSM_HALF_ST), F32),
                   jax.ShapeDtypeStruct((S, SSM_HALVES * SSM_HALF_ST), F32)),
        grid=(SSM_HALVES, nc),
        in_specs=[pl.BlockSpec((Tc, SSM_HALF_CH), lambda h, c: (c, u_blk0 + h)),
                  pl.BlockSpec((1, SSM_HALF_CH, SSM_HALF_ST), lambda h, c: (h, 0, 0)),
                  pl.BlockSpec((1, SSM_HALF_CH, SSM_HALF_ST), lambda h, c: (h, 0, 0)),
                  pl.BlockSpec((1, SSM_HALF_ST, SSM_HALF_CH), lambda h, c: (h, 0, 0)),
                  pl.BlockSpec((1, SSM_HALF_ST, SSM_HALF_CH), lambda h, c: (h, 0, 0)),
                  pl.BlockSpec((1, SSM_HALF_CH), lambda h, c: (0, h)),
                  pl.BlockSpec((1, 8, SUBLANES, SSM_HALF_ST), lambda h, c: (h, 0, 0, 0))],
        out_specs=(pl.BlockSpec((Tc, SSM_HALF_CH), lambda h, c: (c, h)),
                   pl.BlockSpec((Tc, SSM_HALF_ST), lambda h, c: (c, h)),
                   pl.BlockSpec((Tc, SSM_HALF_ST), lambda h, c: (c, h))),
        scratch_shapes=[pltpu.VMEM((SUBLANES, SSM_HALF_ST), F32), pltpu.VMEM((SUBLANES, SSM_HALF_ST), F32)],
        operands=(proj, bd_re, bd_im, cd_re, cd_imneg, d_skip, tab))


def _ssm_bwd(dy, proj, x_re, x_im, bd_re, bd_im, cd_re, cd_imneg, d_skip, tab, *, name, rider=None):
    S = proj.shape[0]
    Tc = min(SSM_CHUNK, S)
    nc = S // Tc
    u_blk0 = (3 * SB_WIDTH) // SSM_HALF_CH

    def body(dy_ref, u_ref, xre_ref, xim_ref, bre_ref, bim_ref, cre_ref, cim_ref, d_ref, tab_ref,
             du_ref, dbre_ref, dbim_ref, dcre_ref, dcim_ref, dd_ref, dlre_ref, dlim_ref,
             gre_s, gim_s, cre_s, cim_s):
        c = pl.program_id(1)

        @pl.when(c == 0)
        def _():
            cre_s[...] = jnp.zeros_like(cre_s)
            cim_s[...] = jnp.zeros_like(cim_s)
            dbre_ref[...] = jnp.zeros_like(dbre_ref)
            dbim_ref[...] = jnp.zeros_like(dbim_ref)
            dcre_ref[...] = jnp.zeros_like(dcre_ref)
            dcim_ref[...] = jnp.zeros_like(dcim_ref)
            dd_ref[...] = jnp.zeros_like(dd_ref)
            dlre_ref[...] = jnp.zeros_like(dlre_ref)
            dlim_ref[...] = jnp.zeros_like(dlim_ref)

        dy = dy_ref[...]
        dyb = dy.astype(BF16)
        u = u_ref[...]
        gre_s[...] = _dot(dyb, cre_ref[0], 1, 1)
        gim_s[...] = _dot(dyb, cim_ref[0], 1, 1)
        row = lax.broadcasted_iota(jnp.int32, (SUBLANES, SSM_HALF_ST), 0)
        nslab = Tc // SUBLANES

        def slab(kk, carry):
            car_re, car_im, acc_re, acc_im = carry
            k = nslab - 1 - kk
            rows = pl.ds(pl.multiple_of(k * SUBLANES, SUBLANES), SUBLANES)
            sre = gre_s[rows, :]
            sim = gim_s[rows, :]
            for n, d in enumerate((1, 2, 4)):
                pre, pim = tab_ref[0, 2 * n], tab_ref[0, 2 * n + 1]
                rre = pltpu.roll(sre, SUBLANES - d, 0)
                rim = pltpu.roll(sim, SUBLANES - d, 0)
                sre, sim = sre + (pre * rre + pim * rim), sim + (pre * rim - pim * rre)
            pre, pim = tab_ref[0, 6], tab_ref[0, 7]
            sre, sim = sre + (pre * car_re + pim * car_im), sim + (pre * car_im - pim * car_re)
            gre_s[rows, :] = sre
            gim_s[rows, :] = sim
            nre = jnp.where(row == SUBLANES - 1, car_re, pltpu.roll(sre, SUBLANES - 1, 0))
            nim = jnp.where(row == SUBLANES - 1, car_im, pltpu.roll(sim, SUBLANES - 1, 0))
            xr = xre_ref[rows, :]
            xi = xim_ref[rows, :]
            acc_re = acc_re + (nre * xr + nim * xi)
            acc_im = acc_im + (nim * xr - nre * xi)
            return (jnp.broadcast_to(sre[0:1, :], sre.shape), jnp.broadcast_to(sim[0:1, :], sim.shape), acc_re, acc_im)

        car = lax.fori_loop(0, nslab, slab, (cre_s[...], cim_s[...], dlre_ref[0], dlim_ref[0]))
        cre_s[...] = car[0]
        cim_s[...] = car[1]
        dlre_ref[0] = car[2]
        dlim_ref[0] = car[3]
        gre = gre_s[...].astype(BF16)
        gim = gim_s[...].astype(BF16)
        ub = u.astype(BF16)
        du = _dot(gre, bre_ref[0], 1, 1) + _dot(gim, bim_ref[0], 1, 1) + d_ref[...] * dy
        du_ref[...] = du.astype(BF16)
        dbre_ref[0] += _dot(ub, gre, 0, 0)
        dbim_ref[0] += _dot(ub, gim, 0, 0)
        dcre_ref[0] += _dot(xre_ref[...], dyb, 0, 0)
        dcim_ref[0] += _dot(xim_ref[...], dyb, 0, 0)
        dd_ref[...] += jnp.sum(dy * u, axis=0, keepdims=True)

    rev = lambda c: nc - 1 - c
    return _call(
        body, name=name, rider=rider,
        out_shape=(jax.ShapeDtypeStruct((S, SSM_WIDTH), BF16),
                   jax.ShapeDtypeStruct((SSM_HALVES, SSM_HALF_CH, SSM_HALF_ST), F32),
                   jax.ShapeDtypeStruct((SSM_HALVES, SSM_HALF_CH, SSM_HALF_ST), F32),
                   jax.ShapeDtypeStruct((SSM_HALVES, SSM_HALF_ST, SSM_HALF_CH), F32),
                   jax.ShapeDtypeStruct((SSM_HALVES, SSM_HALF_ST, SSM_HALF_CH), F32),
                   jax.ShapeDtypeStruct((1, SSM_WIDTH), F32),
                   jax.ShapeDtypeStruct((SSM_HALVES, SUBLANES, SSM_HALF_ST), F32),
                   jax.ShapeDtypeStruct((SSM_HALVES, SUBLANES, SSM_HALF_ST), F32)),
        grid=(SSM_HALVES, nc),
        in_specs=[pl.BlockSpec((Tc, SSM_HALF_CH), lambda h, c: (rev(c), h)),
                  pl.BlockSpec((Tc, SSM_HALF_CH), lambda h, c: (rev(c), u_blk0 + h)),
                  pl.BlockSpec((Tc, SSM_HALF_ST), lambda h, c: (rev(c), h)),
                  pl.BlockSpec((Tc, SSM_HALF_ST), lambda h, c: (rev(c), h)),
                  pl.BlockSpec((1, SSM_HALF_CH, SSM_HALF_ST), lambda h, c: (h, 0, 0)),
                  pl.BlockSpec((1, SSM_HALF_CH, SSM_HALF_ST), lambda h, c: (h, 0, 0)),
                  pl.BlockSpec((1, SSM_HALF_ST, SSM_HALF_CH), lambda h, c: (h, 0, 0)),
                  pl.BlockSpec((1, SSM_HALF_ST, SSM_HALF_CH), lambda h, c: (h, 0, 0)),
                  pl.BlockSpec((1, SSM_HALF_CH), lambda h, c: (0, h)),
                  pl.BlockSpec((1, 8, SUBLANES, SSM_HALF_ST), lambda h, c: (h, 0, 0, 0))],
        out_specs=(pl.BlockSpec((Tc, SSM_HALF_CH), lambda h, c: (rev(c), h)),
                   pl.BlockSpec((1, SSM_HALF_CH, SSM_HALF_ST), lambda h, c: (h, 0, 0)),
                   pl.BlockSpec((1, SSM_HALF_CH, SSM_HALF_ST), lambda h, c: (h, 0, 0)),
                   pl.BlockSpec((1, SSM_HALF_ST, SSM_HALF_CH), lambda h, c: (h, 0, 0)),
                   pl.BlockSpec((1, SSM_HALF_ST, SSM_HALF_CH), lambda h, c: (h, 0, 0)),
                   pl.BlockSpec((1, SSM_HALF_CH), lambda h, c: (0, h)),
                   pl.BlockSpec((1, SUBLANES, SSM_HALF_ST), lambda h, c: (h, 0, 0)),
                   pl.BlockSpec((1, SUBLANES, SSM_HALF_ST), lambda h, c: (h, 0, 0))),
        scratch_shapes=[pltpu.VMEM((Tc, SSM_HALF_ST), F32), pltpu.VMEM((Tc, SSM_HALF_ST), F32),
                        pltpu.VMEM((SUBLANES, SSM_HALF_ST), F32), pltpu.VMEM((SUBLANES, SSM_HALF_ST), F32)],
        operands=(dy, proj, x_re, x_im, bd_re, bd_im, cd_re, cd_imneg, d_skip, tab))


def _ssm_prepare(a_re, a_im, log_dt, b_re, b_im):
    dt = jnp.exp(log_dt)[:, None]
    mag = jnp.exp(a_re * dt)
    lre = mag * jnp.cos(a_im * dt)
    lim = mag * jnp.sin(a_im * dt)
    den = a_re * a_re + a_im * a_im
    fre = ((lre - 1.0) * a_re + lim * a_im) / den
    fim = (lim * a_re - (lre - 1.0) * a_im) / den
    bbre = fre[:, :, None] * b_re - fim[:, :, None] * b_im
    bbim = fre[:, :, None] * b_im + fim[:, :, None] * b_re
    return lre, lim, bbre, bbim


def _group_eye():
    return jnp.eye(SSM_GROUPS // SSM_HALVES, dtype=F32)


def _bd_from_bbar(bbar):
    gh = SSM_GROUPS // SSM_HALVES
    b = bbar.reshape(SSM_HALVES, gh, SSM_STATE, SSM_GROUP).transpose(0, 1, 3, 2)
    out = b[:, :, :, None, :] * _group_eye()[None, :, None, :, None]
    return out.reshape(SSM_HALVES, SSM_HALF_CH, SSM_HALF_ST)


def _bbar_from_bd(dbd):
    gh = SSM_GROUPS // SSM_HALVES
    d = dbd.reshape(SSM_HALVES, gh, SSM_GROUP, gh, SSM_STATE)
    d = jnp.sum(d * _group_eye()[None, :, None, :, None], axis=3)
    return d.transpose(0, 1, 3, 2).reshape(SSM_GROUPS, SSM_STATE, SSM_GROUP)


def _cd_from_c(cmat):
    gh = SSM_GROUPS // SSM_HALVES
    c = cmat.reshape(SSM_HALVES, gh, SSM_GROUP, SSM_STATE).transpose(0, 1, 3, 2)
    out = c[:, :, :, None, :] * _group_eye()[None, :, None, :, None]
    return out.reshape(SSM_HALVES, SSM_HALF_ST, SSM_HALF_CH)


def _c_from_cd(dcd):
    gh = SSM_GROUPS // SSM_HALVES
    d = dcd.reshape(SSM_HALVES, gh, SSM_STATE, gh, SSM_GROUP)
    d = jnp.sum(d * _group_eye()[None, :, None, :, None], axis=3)
    return d.transpose(0, 1, 3, 2).reshape(SSM_GROUPS, SSM_GROUP, SSM_STATE)


def _glu_fwd(y_pre, w_glu, b_glu, *, name):
    S, W = y_pre.shape
    tr = _row_tile(S)

    def body(y_ref, w_ref, b_ref, o_ref):
        yg = _gelu(y_ref[...])
        gl = _dot(yg, w_ref[...], 1, 0) + b_ref[...]
        o_ref[...] = (yg * _sigmoid(gl)).astype(BF16)

    row = pl.BlockSpec((tr, W), lambda i: (i, 0))
    return pl.pallas_call(
        body, name=name, out_shape=jax.ShapeDtypeStruct((S, W), BF16), grid=(S // tr,),
        in_specs=[row, pl.BlockSpec((W, W), lambda i: (0, 0)), pl.BlockSpec((1, W), lambda i: (0, 0))],
        out_specs=row, compiler_params=_cparams("parallel"),
    )(y_pre, w_glu, b_glu)


def _glu_bwd(y_pre, do, w_glu, b_glu, *, name):
    S, W = y_pre.shape
    tr = _row_tile(S)

    def body(y_ref, do_ref, w_ref, b_ref, dy_ref, dw_ref, db_ref):
        i = pl.program_id(0)
        yg, dyg_dy = _gelu_and_grad(y_ref[...])
        ygb = yg.astype(BF16)
        sg = _sigmoid(_dot(ygb, w_ref[...], 1, 0) + b_ref[...])
        do = do_ref[...]
        dgl = do * yg * sg * (1.0 - sg)
        dglb = dgl.astype(BF16)
        dyg = do * sg + _dot(dglb, w_ref[...], 1, 1)
        dy_ref[...] = dyg * dyg_dy
        dw = _dot(ygb, dglb, 0, 0)
        db = jnp.sum(dgl, axis=0, keepdims=True)

        @pl.when(i == 0)
        def _():
            dw_ref[...] = dw
            db_ref[...] = db

        @pl.when(i > 0)
        def _():
            dw_ref[...] += dw
            db_ref[...] += db

    row = pl.BlockSpec((tr, W), lambda i: (i, 0))
    full = pl.BlockSpec((W, W), lambda i: (0, 0))
    vec = pl.BlockSpec((1, W), lambda i: (0, 0))
    return pl.pallas_call(
        body, name=name,
        out_shape=(jax.ShapeDtypeStruct((S, W), F32), jax.ShapeDtypeStruct((W, W), F32), jax.ShapeDtypeStruct((1, W), F32)),
        grid=(S // tr,), in_specs=[row, row, full, vec], out_specs=(row, full, vec),
        compiler_params=_cparams("arbitrary"),
    )(y_pre, do, w_glu, b_glu)


GATE_COL0 = 3 * SB_WIDTH + SSM_WIDTH


def _merge_fwd(proj, o_attn, o_ssm, w_ba, w_bs, b_gate, *, name):
    S = proj.shape[0]
    D = D_MODEL
    tr = _pick(S, (256, 128, 64, 32, 16, 8))
    gb = GATE_COL0 // D

    def body(ga_ref, gs_ref, oa_ref, os_ref, wa_ref, ws_ref, ba_ref, bs_ref, m_ref):
        pa = _dot(oa_ref[...], wa_ref[...], 1, 0)
        ps = _dot(os_ref[...], ws_ref[...], 1, 0)
        sa = _sigmoid(ga_ref[...] + ba_ref[...])
        ss = _sigmoid(gs_ref[...] + bs_ref[...])
        m_ref[...] = (sa * pa + ss * ps).astype(BF16)

    return pl.pallas_call(
        body, name=name, out_shape=jax.ShapeDtypeStruct((S, D), BF16), grid=(S // tr,),
        in_specs=[pl.BlockSpec((tr, D), lambda i: (i, gb)), pl.BlockSpec((tr, D), lambda i: (i, gb + 1)),
                  pl.BlockSpec((tr, SB_WIDTH), lambda i: (i, 0)), pl.BlockSpec((tr, SSM_WIDTH), lambda i: (i, 0)),
                  pl.BlockSpec((SB_WIDTH, D), lambda i: (0, 0)), pl.BlockSpec((SSM_WIDTH, D), lambda i: (0, 0)),
                  pl.BlockSpec((1, D), lambda i: (0, 0)), pl.BlockSpec((1, D), lambda i: (0, 1))],
        out_specs=pl.BlockSpec((tr, D), lambda i: (i, 0)),
        compiler_params=_cparams("parallel"),
    )(proj, proj, o_attn, o_ssm, w_ba, w_bs, b_gate, b_gate)


def _merge_bwd(dmerged, proj, o_attn, o_ssm, w_ba, w_bs, b_gate, *, name):
    S = proj.shape[0]
    D = D_MODEL
    tr = _pick(S, (256, 128, 64, 32, 16, 8))
    gb = GATE_COL0 // D

    def body(dm_ref, ga_ref, gs_ref, oa_ref, os_ref, wa_ref, ws_ref, ba_ref, bs_ref,
             doa_ref, dos_ref, dg_ref, db_ref, dwa_ref, dws_ref):
        i = pl.program_id(0)
        dm = dm_ref[...]
        oa = oa_ref[...]
        osm = os_ref[...]
        pa = _dot(oa, wa_ref[...], 1, 0)
        ps = _dot(osm, ws_ref[...], 1, 0)
        sa = _sigmoid(ga_ref[...] + ba_ref[...])
        ss = _sigmoid(gs_ref[...] + bs_ref[...])
        dpa = (dm * sa).astype(BF16)
        dps = (dm * ss).astype(BF16)
        dga = dm * pa * sa * (1.0 - sa)
        dgs = dm * ps * ss * (1.0 - ss)
        dg_ref[:, :D] = dga.astype(BF16)
        dg_ref[:, D:] = dgs.astype(BF16)
        doa_ref[...] = _dot(dpa, wa_ref[...], 1, 1).astype(BF16)
        dos_ref[...] = _dot(dps, ws_ref[...], 1, 1)
        dwa = _dot(oa, dpa, 0, 0)
        dws = _dot(osm, dps, 0, 0)
        dba = jnp.sum(dga, axis=0, keepdims=True)
        dbs = jnp.sum(dgs, axis=0, keepdims=True)

        @pl.when(i == 0)
        def _():
            dwa_ref[...] = dwa
            dws_ref[...] = dws
            db_ref[:, :D] = dba
            db_ref[:, D:] = dbs

        @pl.when(i > 0)
        def _():
            dwa_ref[...] += dwa
            dws_ref[...] += dws
            db_ref[:, :D] += dba
            db_ref[:, D:] += dbs

    rowD = pl.BlockSpec((tr, D), lambda i: (i, 0))
    wspec = pl.BlockSpec((SB_WIDTH, D), lambda i: (0, 0))
    return pl.pallas_call(
        body, name=name,
        out_shape=(jax.ShapeDtypeStruct((S, SB_WIDTH), BF16), jax.ShapeDtypeStruct((S, SSM_WIDTH), F32),
                   jax.ShapeDtypeStruct((S, 2 * D), BF16), jax.ShapeDtypeStruct((1, 2 * D), F32),
                   jax.ShapeDtypeStruct((SB_WIDTH, D), F32), jax.ShapeDtypeStruct((SSM_WIDTH, D), F32)),
        grid=(S // tr,),
        in_specs=[rowD, pl.BlockSpec((tr, D), lambda i: (i, gb)), pl.BlockSpec((tr, D), lambda i: (i, gb + 1)),
                  pl.BlockSpec((tr, SB_WIDTH), lambda i: (i, 0)), pl.BlockSpec((tr, SSM_WIDTH), lambda i: (i, 0)),
                  wspec, wspec, pl.BlockSpec((1, D), lambda i: (0, 0)), pl.BlockSpec((1, D), lambda i: (0, 1))],
        out_specs=(pl.BlockSpec((tr, SB_WIDTH), lambda i: (i, 0)), pl.BlockSpec((tr, SSM_WIDTH), lambda i: (i, 0)),
                   pl.BlockSpec((tr, 2 * D), lambda i: (i, 0)), pl.BlockSpec((1, 2 * D), lambda i: (0, 0)),
                   wspec, wspec),
        compiler_params=_cparams("arbitrary"),
    )(dmerged, proj, proj, o_attn, o_ssm, w_ba, w_bs, b_gate, b_gate)


def _xattn_probs(q, k, h):
    cols = slice(h * XA_HEAD_DIM, (h + 1) * XA_HEAD_DIM)
    s = _dot(q[:, cols], k[:, cols], 1, 1) * (XA_HEAD_DIM ** -0.5)
    s = s - jnp.max(s, axis=-1, keepdims=True)
    e = jnp.exp(s)
    return e / jnp.sum(e, axis=-1, keepdims=True), cols


def _xattn_fwd(q2, k2, v2, *, name):
    S, D = q2.shape
    M = k2.shape[0]
    tr = _row_tile(S)

    def body(q_ref, k_ref, v_ref, o_ref):
        q = q_ref[...]
        k = k_ref[...]
        v = v_ref[...]
        for h in range(XA_HEADS):
            p, cols = _xattn_probs(q, k, h)
            o_ref[:, cols] = _dot(p, v[:, cols], 1, 0).astype(BF16)

    row = pl.BlockSpec((tr, D), lambda i: (i, 0))
    memb = pl.BlockSpec((M, D), lambda i: (0, 0))
    return pl.pallas_call(
        body, name=name, out_shape=jax.ShapeDtypeStruct((S, D), BF16), grid=(S // tr,),
        in_specs=[row, memb, memb], out_specs=row, compiler_params=_cparams("parallel"),
    )(q2, k2, v2)


def _xattn_bwd(q2, k2, v2, do2, *, name):
    S, D = q2.shape
    M = k2.shape[0]
    tr = _row_tile(S)
    scale = XA_HEAD_DIM ** -0.5

    def body(q_ref, k_ref, v_ref, do_ref, dq_ref, dk_ref, dv_ref):
        i = pl.program_id(0)

        @pl.when(i == 0)
        def _():
            dk_ref[...] = jnp.zeros_like(dk_ref)
            dv_ref[...] = jnp.zeros_like(dv_ref)

        q = q_ref[...]
        k = k_ref[...]
        v = v_ref[...]
        do = do_ref[...]
        for h in range(XA_HEADS):
            p, cols = _xattn_probs(q, k, h)
            dp = _dot(do[:, cols], v[:, cols], 1, 1)
            ds = (p * (dp - jnp.sum(dp * p, axis=-1, keepdims=True)) * scale).astype(BF16)
            dq_ref[:, cols] = _dot(ds, k[:, cols], 1, 0).astype(BF16)
            dk_ref[:, cols] += _dot(ds, q[:, cols], 0, 0)
            dv_ref[:, cols] += _dot(p, do[:, cols], 0, 0)

    row = pl.BlockSpec((tr, D), lambda i: (i, 0))
    memb = pl.BlockSpec((M, D), lambda i: (0, 0))
    return pl.pallas_call(
        body, name=name,
        out_shape=(jax.ShapeDtypeStruct((S, D), BF16), jax.ShapeDtypeStruct((M, D), F32), jax.ShapeDtypeStruct((M, D), F32)),
        grid=(S // tr,), in_specs=[row, memb, memb, row], out_specs=(row, memb, memb),
        compiler_params=_cparams("arbitrary"),
    )(q2, k2, v2, do2)


CONV_ROWS = 64
CONV_ROWS_FWD = 256


def _chunk(ref, c, rows):
    return ref[pl.ds(pl.multiple_of(c * rows, rows), rows), :]


def _rows_before(ref, c, rows):
    t0 = pl.multiple_of(jnp.maximum(c * rows - SUBLANES, 0), SUBLANES)
    return jnp.where(c > 0, ref[pl.ds(t0, SUBLANES), :], 0.0)


def _rows_after(ref, c, rows, n_chunks):
    t0 = pl.multiple_of(jnp.minimum((c + 1) * rows, n_chunks * rows - SUBLANES), SUBLANES)
    return jnp.where(c < n_chunks - 1, ref[pl.ds(t0, SUBLANES), :], 0.0)


def _shift_down(cur, before, d):
    out = pltpu.roll(cur, d, 0)
    r = lax.broadcasted_iota(jnp.int32, cur.shape, 0)
    for e in range(d):
        out = jnp.where(r == e, before[SUBLANES - d + e:SUBLANES - d + e + 1, :], out)
    return out


def _shift_up(cur, after, d):
    rows = cur.shape[0]
    out = pltpu.roll(cur, rows - d, 0)
    r = lax.broadcasted_iota(jnp.int32, cur.shape, 0)
    for e in range(d):
        out = jnp.where(r == rows - d + e, after[e:e + 1, :], out)
    return out


def _conv3(cur, before, w_ref, b_ref):
    return (w_ref[2:3, :] * cur + w_ref[1:2, :] * _shift_down(cur, before, 1)
            + w_ref[0:1, :] * _shift_down(cur, before, 2) + b_ref[...])


def _convgate_fwd(up_g, up_v, conv_w, conv_b, *, name):
    S, H = up_g.shape
    nb = H // LANES
    R = min(CONV_ROWS_FWD, S)
    n_chunks = S // R

    def body(g_ref, v_ref, wg_ref, wv_ref, bg_ref, bv_ref, a_ref):
        def chunk(c, _):
            cg = _conv3(_chunk(g_ref, c, R), _rows_before(g_ref, c, R), wg_ref, bg_ref)
            cv = _conv3(_chunk(v_ref, c, R), _rows_before(v_ref, c, R), wv_ref, bv_ref)
            a_ref[pl.ds(pl.multiple_of(c * R, R), R), :] = (_gelu(cg) * cv).astype(BF16)
            return 0

        lax.fori_loop(0, n_chunks, chunk, 0)

    col = lambda off: pl.BlockSpec((S, LANES), lambda j: (0, off + j))
    wcol = lambda off: pl.BlockSpec((3, LANES), lambda j: (0, off + j))
    bcol = lambda off: pl.BlockSpec((1, LANES), lambda j: (0, off + j))
    return pl.pallas_call(
        body, name=name, out_shape=jax.ShapeDtypeStruct((S, H), BF16), grid=(nb,),
        in_specs=[col(0), col(0), wcol(0), wcol(nb), bcol(0), bcol(nb)],
        out_specs=col(0), compiler_params=_cparams("parallel"),
    )(up_g, up_v, conv_w, conv_w, conv_b, conv_b)


def _convgate_bwd(up_g, up_v, da, conv_w, conv_b, *, name):
    S, H = up_g.shape
    nb = H // LANES
    R = min(CONV_ROWS, S)
    n_chunks = S // R

    def fold(a):
        return sum(a[r:r + SUBLANES] for r in range(0, a.shape[0], SUBLANES))

    def body(g_ref, v_ref, da_ref, wg_ref, wv_ref, bg_ref, bv_ref,
             dug_ref, duv_ref, dwg_ref, dwv_ref, dbg_ref, dbv_ref, dcg_s, dcv_s):
        def first_pass(c, acc):
            rows = pl.ds(pl.multiple_of(c * R, R), R)
            ug, uv = _chunk(g_ref, c, R), _chunk(v_ref, c, R)
            bg, bv = _rows_before(g_ref, c, R), _rows_before(v_ref, c, R)
            cg = _conv3(ug, bg, wg_ref, bg_ref)
            cv = _conv3(uv, bv, wv_ref, bv_ref)
            da = da_ref[rows, :]
            gl, dgl = _gelu_and_grad(cg)
            dcg = da * cv * dgl
            dcv = da * gl
            dcg_s[rows, :] = dcg
            dcv_s[rows, :] = dcv
            new = []
            for dc, u, before in ((dcg, ug, bg), (dcv, uv, bv)):
                new += [fold(dc * _shift_down(u, before, 2)), fold(dc * _shift_down(u, before, 1)), fold(dc * u), fold(dc)]
            return tuple(a + n for a, n in zip(acc, new))

        zero = jnp.zeros((SUBLANES, LANES), F32)
        acc = lax.fori_loop(0, n_chunks, first_pass, (zero,) * 8)
        total = [jnp.sum(a, axis=0, keepdims=True) for a in acc]
        for k, (dw_ref, db_ref) in enumerate(((dwg_ref, dbg_ref), (dwv_ref, dbv_ref))):
            dw_ref[0:1, :] = total[4 * k]
            dw_ref[1:2, :] = total[4 * k + 1]
            dw_ref[2:3, :] = total[4 * k + 2]
            db_ref[...] = total[4 * k + 3]

        def second_pass(c, _):
            rows = pl.ds(pl.multiple_of(c * R, R), R)
            for dc_s, w_ref, du_ref in ((dcg_s, wg_ref, dug_ref), (dcv_s, wv_ref, duv_ref)):
                cur, after = _chunk(dc_s, c, R), _rows_after(dc_s, c, R, n_chunks)
                du = w_ref[2:3, :] * cur + w_ref[1:2, :] * _shift_up(cur, after, 1) + w_ref[0:1, :] * _shift_up(cur, after, 2)
                du_ref[rows, :] = du.astype(BF16)
            return 0

        lax.fori_loop(0, n_chunks, second_pass, 0)

    col = lambda off: pl.BlockSpec((S, LANES), lambda j: (0, off + j))
    wcol = lambda off: pl.BlockSpec((3, LANES), lambda j: (0, off + j))
    bcol = lambda off: pl.BlockSpec((1, LANES), lambda j: (0, off + j))
    return pl.pallas_call(
        body, name=name,
        out_shape=(jax.ShapeDtypeStruct((S, H), BF16), jax.ShapeDtypeStruct((S, H), BF16),
                   jax.ShapeDtypeStruct((3, H), F32), jax.ShapeDtypeStruct((3, H), F32),
                   jax.ShapeDtypeStruct((1, H), F32), jax.ShapeDtypeStruct((1, H), F32)),
        grid=(nb,),
        in_specs=[col(0), col(0), col(0), wcol(0), wcol(nb), bcol(0), bcol(nb)],
        out_specs=(col(0), col(0), wcol(0), wcol(0), bcol(0), bcol(0)),
        scratch_shapes=[pltpu.VMEM((S, LANES), F32), pltpu.VMEM((S, LANES), F32)],
        compiler_params=_cparams("parallel"),
    )(up_g, up_v, da, conv_w, conv_w, conv_b, conv_b)


def _local_step(x, mem, target, w_in, late_wire, P, core):
    mm = _matmul
    h1, (w_in,) = _rms_fwd(x, P["norm_mix_pre"], name="rms_mix_pre", rider=_fill_xy([w_in]))
    w_in, = _fill_c([w_in]).run(name="gather_in_c")
    w_in = w_in.reshape((N_DEV,) + w_in.shape[2:])
    n_mid = len(LATE) - len(REDUCE_FFN)
    proj, wire_mid = mm(h1, w_in, name="mm_in", rider=_fill_xy(late_wire[:n_mid]))
    (o_attn, sb_tot, sb_first), wires = _sb_fwd(
        proj, name="sb_fwd", rider=_Exchange.join(_fill_c(wire_mid), _fill_xy(late_wire[n_mid:])))
    wire_mid, wire_ffn = wires[:n_mid], wires[n_mid:]

    ssm_prep = lambda *a: _ssm_prepare(*a)
    (lam_re, lam_im, bb_re, bb_im), prep_vjp = jax.vjp(
        ssm_prep, P["ssm_a_re"], P["ssm_a_im"], P["ssm_log_dt"], P["ssm_b_re"], P["ssm_b_im"])
    tab_f, tab_b = _ssm_tables(lam_re, lam_im)
    bd_re = _bd_from_bbar(bb_re).astype(BF16)
    bd_im = _bd_from_bbar(bb_im).astype(BF16)
    cd_re = _cd_from_c(P["ssm_c_re"]).astype(BF16)
    cd_imneg = _cd_from_c(-P["ssm_c_im"]).astype(BF16)
    (y_pre, x_re, x_im), wire_ffn = _ssm_fwd(proj, bd_re, bd_im, cd_re, cd_imneg, P["ssm_d"], tab_f,
                                             name="ssm_fwd", rider=_fill_c(wire_ffn))
    W = _weights_from_wire(dict(zip(LATE, list(wire_mid) + list(wire_ffn))))
    W["w_in"] = w_in
    o_ssm = _glu_fwd(y_pre, W["ssm_w_glu"], P["ssm_b_glu"], name="glu_fwd")

    merged = _merge_fwd(proj, o_attn, o_ssm, W["w_branch_attn"], W["w_branch_ssm"], P["b_gate"], name="merge_fwd")
    mo = mm(merged, W["w_out"], name="mm_out")
    x1, h2 = _resnorm_norm(x, mo, P["norm_mix_post"], P["norm_xa_pre"], name="resnorm_1")

    mem_n = _rms_fwd(mem, P["norm_mem"], name="rms_mem")
    q2 = mm(h2, W["xa_wq"], out_dtype=BF16, name="mm_xq")
    k2 = mm(mem_n, W["xa_wk"], out_dtype=BF16, name="mm_xk")
    v2 = mm(mem_n, W["xa_wv"], out_dtype=BF16, name="mm_xv")
    o2 = _xattn_fwd(q2, k2, v2, name="xattn_fwd")
    xa = mm(o2, W["xa_wo"], name="mm_xo")
    x2, h3 = _resnorm_norm(x1, xa, P["norm_xa_post"], P["norm_ffn_pre"], name="resnorm_2")

    half = N_DEV // 2
    up_g = mm(h3, W["ffn_w_up"], n_blocks=half, name="mm_up_g")
    up_v = mm(h3, W["ffn_w_up"], b_block0=half, name="mm_up_v")
    act = _convgate_fwd(up_g, up_v, W["ffn_conv_w"], P["ffn_conv_b"], name="convgate_fwd")
    f = mm(act, W["ffn_w_down"], name="mm_down")
    loss, dy, df, dg_ffn_post = _final_loss(x2, f, P["norm_ffn_post"], target, name="final_loss")

    G = {"norm_ffn_post": dg_ffn_post}
    dact = mm(df, W["ffn_w_down"], tb=True, name="mm_down_dx")
    G["ffn_w_down"] = mm(act, df, ta=True, name="mm_down_dw")
    dug, duv, dwg, dwv, dbg, dbv = _convgate_bwd(up_g, up_v, dact, W["ffn_conv_w"], P["ffn_conv_b"], name="convgate_bwd")
    G["ffn_conv_w"] = jnp.concatenate([dwg, dwv], axis=1)
    G["ffn_conv_b"] = jnp.concatenate([dbg, dbv], axis=1)
    dh3 = mm(dug, W["ffn_w_up"], tb=True, n_blocks=half, name="mm_up_g_dx")
    dh3 = mm(duv, W["ffn_w_up"], tb=True, b_block0=half, acc_in=dh3, name="mm_up_v_dx")
    dw_up = mm(h3, dug, ta=True, out_into=lax.empty(W["ffn_w_up"].shape, F32), name="mm_up_g_dw")
    G["ffn_w_up"] = mm(h3, duv, ta=True, out_into=dw_up, out_block0=half, name="mm_up_v_dw")
    blocks = {n: _grad_blocks(n, G[n]) for n in REDUCE_FFN}
    (dx2, dxa, G["norm_ffn_pre"], G["norm_xa_post"]), from_core = _norm_bwd_pair(
        dy, dh3, x2, P["norm_ffn_pre"], xa, P["norm_xa_post"], name="norm_bwd_3",
        rider=_send_c([blocks[n] for n in REDUCE_FFN]))
    pair = {n: _pair_sum(blocks[n], r, core, name="pair_sum_" + n) for n, r in zip(REDUCE_FFN, from_core)}

    G["xa_wo"] = mm(o2, dxa, ta=True, name="mm_xo_dw")
    do2 = mm(dxa, W["xa_wo"], tb=True, out_dtype=BF16, name="mm_xo_dx")
    dq2, dk2, dv2 = _xattn_bwd(q2, k2, v2, do2, name="xattn_bwd")
    G["xa_wq"] = mm(h2, dq2, ta=True, name="mm_xq_dw")
    dh2 = mm(dq2, W["xa_wq"], tb=True, name="mm_xq_dx")
    G["xa_wk"] = mm(mem_n, dk2, ta=True, name="mm_xk_dw")
    G["xa_wv"] = mm(mem_n, dv2, ta=True, name="mm_xv_dw")
    dmem_n = jnp.concatenate([dk2, dv2], axis=1)
    wkv = jnp.concatenate([W["xa_wk"], W["xa_wv"]], axis=1)
    dmem = mm(dmem_n, wkv, tb=True, name="mm_xkv_dx")
    _, G["norm_mem"] = _norm_bwd_single(None, dmem, mem, P["norm_mem"], name="norm_bwd_mem")
    (dx1, dmo, G["norm_xa_pre"], G["norm_mix_post"]), _ = _norm_bwd_pair(
        dx2, dh2, x1, P["norm_xa_pre"], mo, P["norm_mix_post"], name="norm_bwd_2")

    G["w_out"] = mm(merged, dmo, ta=True, name="mm_out_dw")
    dmerged = mm(dmo, W["w_out"], tb=True, name="mm_out_dx")
    do_attn, do_ssm, dgate, G["b_gate"], G["w_branch_attn"], G["w_branch_ssm"] = _merge_bwd(
        dmerged, proj, o_attn, o_ssm, W["w_branch_attn"], W["w_branch_ssm"], P["b_gate"], name="merge_bwd")
    dy_pre, G["ssm_w_glu"], G["ssm_b_glu"] = _glu_bwd(y_pre, do_ssm, W["ssm_w_glu"], P["ssm_b_glu"], name="glu_bwd")
    blocks.update({n: _grad_blocks(n, G[n]) for n in REDUCE_MID})
    (du, dbd_re, dbd_im, dcd_re, dcd_imneg, G["ssm_d"], dl_re, dl_im), brought = _ssm_bwd(
        dy_pre, proj, x_re, x_im, bd_re, bd_im, cd_re, cd_imneg, P["ssm_d"], tab_b, name="ssm_bwd",
        rider=_Exchange.join(_send_c([blocks[n] for n in REDUCE_MID]), _scatter_xy([pair[n] for n in REDUCE_FFN])))
    from_core, from_chips = brought[:len(REDUCE_MID)], brought[len(REDUCE_MID):]
    reduced = {n: (pair[n], parts) for n, parts in zip(REDUCE_FFN, from_chips)}
    pair.update({n: _pair_sum(blocks[n], r, core, name="pair_sum_" + n) for n, r in zip(REDUCE_MID, from_core)})
    G["ssm_c_re"] = _c_from_cd(dcd_re)
    G["ssm_c_im"] = -_c_from_cd(dcd_imneg)
    dlam_re = jnp.sum(dl_re, axis=1).reshape(SSM_GROUPS, SSM_STATE)
    dlam_im = jnp.sum(dl_im, axis=1).reshape(SSM_GROUPS, SSM_STATE)
    (G["ssm_a_re"], G["ssm_a_im"], G["ssm_log_dt"], G["ssm_b_re"], G["ssm_b_im"]) = prep_vjp(
        (dlam_re, dlam_im, _bbar_from_bd(dbd_re), _bbar_from_bd(dbd_im)))
    G["ffn_conv_b"] = G["ffn_conv_b"].reshape(N_DEV, FF_LOCAL_PAD)[:, :FF_LOCAL]
    small = [G[n].reshape(SMALL_SHAPE[n]) for n in SMALL_EARLY]
    (dq, dk, dv), brought = _sb_bwd(
        proj, sb_tot, sb_first, do_attn, name="sb_bwd",
        rider=_Exchange.join(_scatter_xy([pair[n] for n in REDUCE_MID]), _gather_xy_from(small)))
    from_chips, small = brought[:len(REDUCE_MID)], brought[len(REDUCE_MID):]
    reduced.update({n: (pair[n], parts) for n, parts in zip(REDUCE_MID, from_chips)})
    dproj = jnp.concatenate([dq, dk, dv, du, dgate], axis=1)
    G["w_in"], small = mm(h1, dproj, ta=True, out_cb=W["w_in"].shape[2], name="mm_in_dw", rider=_fill_c(small))
    g_in = _grad_blocks("w_in", G["w_in"])
    dh1, (from_core,) = mm(dproj, W["w_in"], tb=True, name="mm_in_dx", rider=_send_c([g_in]))
    pair_in = _pair_sum(g_in, from_core, core, name="pair_sum_w_in")
    (grad_x, dg_pre), (from_chips,) = _norm_bwd_single(dx1, dh1, x, P["norm_mix_pre"], name="norm_bwd_1",
                                                       rider=_scatter_xy([pair_in]))
    reduced["w_in"] = (pair_in, from_chips)
    last, = _gather_all([dg_pre]).run(name="gather_g_last")
    parts = dict(zip(SMALL_EARLY, small))
    parts["norm_mix_pre"] = last
    return loss, grad_x, parts, reduced


MESH = pl.DeviceIdType.MESH
_HBM = pl.BlockSpec(memory_space=pl.ANY)
N_XY = 4
N_XY_PEERS = 3


def _xy_peers(x, y):
    return [(1 - x, y), (x, 1 - y), (1 - x, 1 - y)]


class _Exchange:
    def __init__(self, arrays, out_shapes, plan, n_copies, alias):
        self.arrays = list(arrays)
        self.out_shapes = list(out_shapes)
        self.plan = plan
        self.n_copies = n_copies
        self.alias = list(alias) if isinstance(alias, (list, tuple)) else [alias] * len(self.arrays)

    @property
    def n(self):
        return len(self.arrays)

    def aliases(self, first_in, first_out):
        return {first_in + k: first_out + k for k in range(self.n) if self.alias[k]}

    @staticmethod
    def join(a, b):
        def plan(k, src, dst, x, y, c):
            return a.plan(k, src, dst, x, y, c) if k < a.n else b.plan(k - a.n, src, dst, x, y, c)

        return _Exchange(a.arrays + b.arrays, a.out_shapes + b.out_shapes, plan, max(a.n_copies, b.n_copies),
                         a.alias + b.alias)

    def sems(self):
        shape = (self.n, self.n_copies)
        return [pltpu.SemaphoreType.DMA(shape), pltpu.SemaphoreType.DMA(shape)]

    def _copies(self, ins, outs, send_sems, recv_sems):
        x, y, c = lax.axis_index("x"), lax.axis_index("y"), lax.axis_index("c")
        sends, lands, own = [], [], []
        for k in range(self.n):
            for j, (src, dst, dev, land) in enumerate(self.plan(k, ins[k], outs[k], x, y, c)):
                if dev is None:
                    own.append(pltpu.make_async_copy(src, dst, send_sems.at[k, j]))
                    continue
                sems = dict(send_sem=send_sems.at[k, j], recv_sem=recv_sems.at[k, j], device_id=dev, device_id_type=MESH)
                sends.append(pltpu.make_async_remote_copy(src_ref=src, dst_ref=dst, **sems))
                lands.append(pltpu.make_async_remote_copy(src_ref=src, dst_ref=land, **sems))
        return sends, lands, own

    def start(self, ins, outs, send_sems, recv_sems):
        sends, _, own = self._copies(ins, outs, send_sems, recv_sems)
        for cp in own + sends:
            cp.start()

    def finish(self, ins, outs, send_sems, recv_sems):
        sends, lands, own = self._copies(ins, outs, send_sems, recv_sems)
        for cp in lands:
            cp.wait_recv()
        for cp in sends:
            cp.wait_send()
        for cp in own:
            cp.wait()

    def run(self, *, name):
        n = self.n

        def body(*refs):
            parts = (refs[:n], refs[n:2 * n], refs[2 * n], refs[2 * n + 1])
            self.start(*parts)
            self.finish(*parts)

        return pl.pallas_call(
            body, name=name, out_shape=tuple(self.out_shapes),
            in_specs=[_HBM] * n, out_specs=tuple([_HBM] * n),
            input_output_aliases=self.aliases(0, 0),
            scratch_shapes=self.sems(),
        )(*self.arrays)


def _call(host_body, *, name, grid, in_specs, out_specs, out_shape, scratch_shapes, operands, rider=None):
    out_specs, out_shape = tuple(out_specs), tuple(out_shape)
    if rider is None:
        res = pl.pallas_call(
            host_body, name=name, grid=grid, in_specs=list(in_specs), out_specs=out_specs, out_shape=out_shape,
            scratch_shapes=list(scratch_shapes), compiler_params=_cparams(*["arbitrary"] * len(grid)),
        )(*operands)
        return tuple(res), None
    n, n_in, n_out, n_scr = rider.n, len(in_specs), len(out_specs), len(scratch_shapes)

    def body(*refs):
        pos = [0]

        def take(count):
            pos[0] += count
            return refs[pos[0] - count:pos[0]]

        h_in, r_in, h_out, r_out, h_scr = take(n_in), take(n), take(n_out), take(n), take(n_scr)
        send_sems, recv_sems = take(2)
        ids = [pl.program_id(a) for a in range(len(grid))]
        first = functools.reduce(jnp.logical_and, [i == 0 for i in ids])
        last = functools.reduce(jnp.logical_and, [i == g - 1 for i, g in zip(ids, grid)])

        @pl.when(first)
        def _():
            rider.start(r_in, r_out, send_sems, recv_sems)

        host_body(*h_in, *h_out, *h_scr)

        @pl.when(last)
        def _():
            rider.finish(r_in, r_out, send_sems, recv_sems)

    res = pl.pallas_call(
        body, name=name, grid=grid,
        in_specs=list(in_specs) + [_HBM] * n, out_specs=out_specs + tuple([_HBM] * n),
        out_shape=out_shape + tuple(rider.out_shapes),
        input_output_aliases=rider.aliases(n_in, n_out),
        scratch_shapes=list(scratch_shapes) + rider.sems(),
        compiler_params=_cparams(*["arbitrary"] * len(grid)),
    )(*operands, *rider.arrays)
    return tuple(res[:n_out]), list(res[n_out:])


def _same(arrays):
    return [jax.ShapeDtypeStruct(a.shape, a.dtype) for a in arrays]


def _fill_xy(bufs):
    def plan(k, src, dst, x, y, c):
        mine = 2 * x + y
        return [(src.at[mine, c], dst.at[mine, c], (px, py, c), dst.at[2 * px + py, c]) for px, py in _xy_peers(x, y)]

    return _Exchange(bufs, _same(bufs), plan, N_XY_PEERS, alias=True)


def _fill_c(bufs):
    def plan(k, src, dst, x, y, c):
        return [(src.at[:, c], dst.at[:, c], (x, y, 1 - c), dst.at[:, 1 - c])]

    return _Exchange(bufs, _same(bufs), plan, 1, alias=True)


def _slots(arrays):
    return [jax.ShapeDtypeStruct((N_XY, 2) + a.shape, a.dtype) for a in arrays]


def _gather_xy_from(srcs):
    def plan(k, src, dst, x, y, c):
        mine = 2 * x + y
        return ([(src, dst.at[mine, c], None, None)]
                + [(src, dst.at[mine, c], (px, py, c), dst.at[2 * px + py, c]) for px, py in _xy_peers(x, y)])

    return _Exchange(srcs, _slots(srcs), plan, 1 + N_XY_PEERS, alias=False)


def _gather_all(srcs):
    def plan(k, src, dst, x, y, c):
        mine = 2 * x + y
        out = [(src, dst.at[mine, c], None, None)]
        for fx, fy, fc in [(a, b, e) for a in (0, 1) for b in (0, 1) for e in (0, 1)][1:]:
            px, py, pc = (1 - x) if fx else x, (1 - y) if fy else y, (1 - c) if fc else c
            out.append((src, dst.at[mine, c], (px, py, pc), dst.at[2 * px + py, pc]))
        return out

    return _Exchange(srcs, _slots(srcs), plan, N_DEV, alias=False)


def _send_c(srcs):
    def plan(k, src, dst, x, y, c):
        return [(src.at[:, 1 - c], dst, (x, y, 1 - c), dst)]

    outs = [jax.ShapeDtypeStruct(a.shape[:1] + a.shape[2:], a.dtype) for a in srcs]
    return _Exchange(srcs, outs, plan, 1, alias=False)


def _scatter_xy(srcs):
    def plan(k, src, dst, x, y, c):
        return [(src.at[2 * px + py], dst.at[j], (px, py, c), dst.at[j]) for j, (px, py) in enumerate(_xy_peers(x, y))]

    outs = [jax.ShapeDtypeStruct((N_XY_PEERS,) + a.shape[1:], a.dtype) for a in srcs]
    return _Exchange(srcs, outs, plan, N_XY_PEERS, alias=False)


PACK_COLS = 1024
WIRE_DTYPE = BF16


def _pair_sum(g8, recv, core, *, name):
    n, _, R, C = g8.shape
    tr = _pick(R, (128, 64, 32, 16, 8))

    def body(core_ref, a_ref, b_ref, o_ref):
        o_ref[...] = (a_ref[0] + b_ref[...]).astype(WIRE_DTYPE)

    return pl.pallas_call(
        body, name=name, out_shape=jax.ShapeDtypeStruct((n, R, C), WIRE_DTYPE),
        grid_spec=pltpu.PrefetchScalarGridSpec(
            num_scalar_prefetch=1, grid=(n, R // tr),
            in_specs=[pl.BlockSpec((1, 1, tr, C), lambda s, i, core_ref: (s, core_ref[0], i, 0)),
                      pl.BlockSpec((1, tr, C), lambda s, i, core_ref: (s, i, 0))],
            out_specs=pl.BlockSpec((1, tr, C), lambda s, i, core_ref: (s, i, 0))),
        compiler_params=_cparams("parallel", "parallel"),
    )(core, g8, recv)


def _adamw_math(w, g, m, v):
    m = ADAM_B1 * m + (1.0 - ADAM_B1) * g
    v = ADAM_B2 * v + (1.0 - ADAM_B2) * (g * g)
    m_hat = m / (1.0 - ADAM_B1 ** ADAM_STEP)
    v_hat = v / (1.0 - ADAM_B2 ** ADAM_STEP)
    delta = -ADAM_LR * (m_hat / (jnp.sqrt(v_hat) + ADAM_EPS) + ADAM_WD * w)
    return delta, m, v


def _reduce_adamw(parts, w, m, v, *, own=None, own_slot=None, name):
    n, R, C = parts.shape
    tr = _pick(R, (128, 64, 32, 16, 8))
    has_own = own is not None

    def body(*refs):
        if has_own:
            _, own_ref, parts_ref, w_ref, m_ref, v_ref, g_ref, d_ref, nm_ref, nv_ref = refs
            g = own_ref[0].astype(F32)
            first = 0
        else:
            parts_ref, w_ref, m_ref, v_ref, g_ref, d_ref, nm_ref, nv_ref = refs
            g = parts_ref[0]
            first = 1
        for k in range(first, n):
            g = g + parts_ref[k].astype(F32)
        g_ref[...] = g
        d_ref[...], nm_ref[...], nv_ref[...] = _adamw_math(w_ref[...], g, m_ref[...], v_ref[...])

    out = jax.ShapeDtypeStruct((R, C), F32)
    if has_own:
        row = pl.BlockSpec((tr, C), lambda i, s: (i, 0))
        return pl.pallas_call(
            body, name=name, out_shape=(out, out, out, out),
            grid_spec=pltpu.PrefetchScalarGridSpec(
                num_scalar_prefetch=1, grid=(R // tr,),
                in_specs=[pl.BlockSpec((1, tr, C), lambda i, s: (s[0], i, 0)),
                          pl.BlockSpec((n, tr, C), lambda i, s: (0, i, 0)), row, row, row],
                out_specs=(row, row, row, row)),
            compiler_params=_cparams("parallel"),
        )(own_slot, own, parts, w, m, v)
    row = pl.BlockSpec((tr, C), lambda i: (i, 0))
    return pl.pallas_call(
        body, name=name, out_shape=(out, out, out, out), grid=(R // tr,),
        in_specs=[pl.BlockSpec((n, tr, C), lambda i: (0, i, 0)), row, row, row],
        out_specs=(row, row, row, row), compiler_params=_cparams("parallel"),
    )(parts, w, m, v)


SHARDED = (("w_in", (1024, 4096), 1), ("ssm_w_glu", (512, 512), 0), ("w_branch_attn", (512, 1024), 1),
           ("w_branch_ssm", (512, 1024), 1), ("w_out", (1024, 1024), 0), ("xa_wq", (1024, 1024), 0),
           ("xa_wk", (1024, 1024), 0), ("xa_wv", (1024, 1024), 0), ("xa_wo", (1024, 1024), 0),
           ("ffn_w_up", (1024, 5632), 1), ("ffn_conv_w", (3, 5632), 1), ("ffn_w_down", (2816, 1024), 0))
REPLICATED = (("norm_mix_pre", (1024,)), ("norm_mix_post", (1024,)), ("b_gate", (2048,)), ("ssm_a_re", (32, 64)),
              ("ssm_a_im", (32, 64)), ("ssm_log_dt", (32,)), ("ssm_b_re", (32, 64, 16)), ("ssm_b_im", (32, 64, 16)),
              ("ssm_c_re", (32, 16, 64)), ("ssm_c_im", (32, 16, 64)), ("ssm_d", (512,)), ("ssm_b_glu", (512,)),
              ("norm_xa_pre", (1024,)), ("norm_xa_post", (1024,)), ("norm_mem", (1024,)), ("norm_ffn_pre", (1024,)),
              ("norm_ffn_post", (1024,)), ("ffn_conv_b", (5632,)))
PARAM_ORDER = ("norm_mix_pre", "norm_mix_post", "w_in", "b_gate", "ssm_a_re", "ssm_a_im", "ssm_log_dt", "ssm_b_re",
               "ssm_b_im", "ssm_c_re", "ssm_c_im", "ssm_d", "ssm_w_glu", "ssm_b_glu", "w_branch_attn", "w_branch_ssm",
               "w_out", "norm_xa_pre", "norm_xa_post", "norm_mem", "xa_wq", "xa_wk", "xa_wv", "xa_wo", "norm_ffn_pre",
               "norm_ffn_post", "ffn_w_up", "ffn_conv_w", "ffn_conv_b", "ffn_w_down")
SMALL_ROWS = 160
FF_LOCAL = 2 * D_FF // N_DEV
FF_LOCAL_PAD = 768
FF_PAD = (N_DEV // 2) * FF_LOCAL_PAD


def _local_shape(shape, axis):
    return tuple(s // N_DEV if a == axis else s for a, s in enumerate(shape))


def _pad_cols(a, width):
    return jnp.pad(a, [(0, 0)] * (a.ndim - 1) + [(0, width - a.shape[-1])])


def _blocks_to_cols(a8):
    return a8.transpose(1, 0, 2).reshape(a8.shape[1], N_DEV * a8.shape[2])


def _cols_to_blocks(a, cb):
    return a.reshape(a.shape[0], N_DEV, cb).transpose(1, 0, 2)


FF_PADDED = ("ffn_w_up", "ffn_conv_w")
LATE = tuple(n for n, _, _ in SHARDED if n != "w_in")
REDUCE_FFN = ("ffn_w_up", "ffn_conv_w", "ffn_w_down")
REDUCE_MID = ("xa_wo", "xa_wq", "xa_wk", "xa_wv", "w_out", "w_branch_attn", "w_branch_ssm", "ssm_w_glu")
SHARD_AXIS = {n: ax for n, _, ax in SHARDED}
FULL_SHAPE = {n: s for n, s, _ in SHARDED}


def _as_local(n, a):
    return _pad_cols(a, FF_LOCAL_PAD) if n in FF_PADDED else a


def _weights_from_wire(wire):
    full = {n: b.reshape((N_DEV,) + b.shape[2:]) for n, b in wire.items()}
    W = {n: a.reshape(FULL_SHAPE[n]) if SHARD_AXIS[n] == 0 else a for n, a in full.items()}
    for n in ("w_branch_attn", "w_branch_ssm", "ffn_conv_w"):
        W[n] = _blocks_to_cols(full[n])
    W["ffn_w_down"] = jnp.pad(W["ffn_w_down"].reshape(N_DEV // 2, FF_LOCAL, D_MODEL),
                              ((0, 0), (0, FF_LOCAL_PAD - FF_LOCAL), (0, 0))).reshape(FF_PAD, D_MODEL)
    return W


def _grad_blocks(n, g):
    if n in ("w_branch_attn", "w_branch_ssm"):
        g = _cols_to_blocks(g, D_MODEL // N_DEV)
    elif n == "ffn_conv_w":
        g = _cols_to_blocks(g, FF_LOCAL_PAD)
    elif n == "ffn_w_down":
        g = g.reshape(N_DEV // 2, FF_LOCAL_PAD, D_MODEL)[:, :FF_LOCAL]
    local = _local_shape(FULL_SHAPE[n], SHARD_AXIS[n])
    if n in FF_PADDED:
        local = local[:-1] + (FF_LOCAL_PAD,)
    return g.reshape((N_XY, 2) + local)


SMALL_SHAPE = {n: (1, s[0]) if len(s) == 1 else (s[0], math.prod(s[1:])) for n, s in REPLICATED}
SMALL_SHAPE["ffn_conv_b"] = (N_DEV, FF_LOCAL)
SMALL_EARLY = tuple(n for n, _ in REPLICATED if n != "norm_mix_pre")


def _adamw_replicated(parts, w, m, v, *, name):
    n = len(parts)

    def body(*refs):
        p_refs, w_refs, m_refs, v_refs = (refs[i * n:(i + 1) * n] for i in range(4))
        outs = refs[4 * n:]
        for k in range(n):
            g = p_refs[k][0, 0]
            for s in range(1, N_DEV):
                g = g + p_refs[k][s // 2, s % 2]
            d, nm, nv = _adamw_math(w_refs[k][...], g, m_refs[k][...], v_refs[k][...])
            for slot, val in enumerate((g, d, nm, nv)):
                outs[slot * n + k][...] = val

    vmem = pl.BlockSpec(memory_space=pltpu.VMEM)
    shapes = [jax.ShapeDtypeStruct(a.shape, F32) for a in w] * 4
    res = pl.pallas_call(
        body, name=name, out_shape=tuple(shapes), in_specs=[vmem] * (4 * n), out_specs=tuple([vmem] * (4 * n)),
        compiler_params=pltpu.CompilerParams(vmem_limit_bytes=VMEM_LIMIT),
    )(*parts, *w, *m, *v)
    return [list(res[i * n:(i + 1) * n]) for i in range(4)]


def kernel(x, mem, norm_mix_pre, norm_mix_post, w_in, b_gate, ssm_a_re, ssm_a_im, ssm_log_dt, ssm_b_re, ssm_b_im, ssm_c_re, ssm_c_im, ssm_d, ssm_w_glu, ssm_b_glu, w_branch_attn, w_branch_ssm, w_out, norm_xa_pre, norm_xa_post, norm_mem, xa_wq, xa_wk, xa_wv, xa_wo, norm_ffn_pre, norm_ffn_post, ffn_w_up, ffn_conv_w, ffn_conv_b, ffn_w_down, loss_target, m_norm_mix_pre, m_norm_mix_post, m_w_in, m_b_gate, m_ssm_a_re, m_ssm_a_im, m_ssm_log_dt, m_ssm_b_re, m_ssm_b_im, m_ssm_c_re, m_ssm_c_im, m_ssm_d, m_ssm_w_glu, m_ssm_b_glu, m_w_branch_attn, m_w_branch_ssm, m_w_out, m_norm_xa_pre, m_norm_xa_post, m_norm_mem, m_xa_wq, m_xa_wk, m_xa_wv, m_xa_wo, m_norm_ffn_pre, m_norm_ffn_post, m_ffn_w_up, m_ffn_conv_w, m_ffn_conv_b, m_ffn_w_down, v_norm_mix_pre, v_norm_mix_post, v_w_in, v_b_gate, v_ssm_a_re, v_ssm_a_im, v_ssm_log_dt, v_ssm_b_re, v_ssm_b_im, v_ssm_c_re, v_ssm_c_im, v_ssm_d, v_ssm_w_glu, v_ssm_b_glu, v_w_branch_attn, v_w_branch_ssm, v_w_out, v_norm_xa_pre, v_norm_xa_post, v_norm_mem, v_xa_wq, v_xa_wk, v_xa_wv, v_xa_wo, v_norm_ffn_pre, v_norm_ffn_post, v_ffn_w_up, v_ffn_conv_w, v_ffn_conv_b, v_ffn_w_down):
    args = dict(locals())
    w_loc = {n: args[n][0] for n in PARAM_ORDER}
    m_loc = {n: args["m_" + n][0] for n in PARAM_ORDER}
    v_loc = {n: args["v_" + n][0] for n in PARAM_ORDER}
    core_i = lax.axis_index("c")
    chip_i = 2 * lax.axis_index("x") + lax.axis_index("y")
    core = core_i.astype(jnp.int32).reshape(1)
    chip = chip_i.astype(jnp.int32).reshape(1)

    def in_place(a):
        buf = lax.empty((N_XY, 2) + a.shape, a.dtype)
        return lax.dynamic_update_slice(buf, a[None, None], (chip_i, core_i) + (0,) * a.ndim)

    as_wire = lambda n: in_place(_as_local(n, w_loc[n]).astype(F32 if n == "ffn_conv_w" else BF16))

    P = {}
    for n, shape in REPLICATED:
        P[n] = w_loc[n] if len(shape) > 1 or n == "ssm_log_dt" else w_loc[n].reshape(1, -1)
    P["ffn_conv_b"] = _pad_cols(w_loc["ffn_conv_b"].reshape(N_DEV, FF_LOCAL), FF_LOCAL_PAD).reshape(1, 2 * FF_PAD)

    loss, grad_x, small_parts, reduced = _local_step(x[0], mem[0], loss_target[0], as_wire("w_in"),
                                                     [as_wire(n) for n in LATE], P, core)
    loss = lax.psum(loss[0, 0], ("x", "y", "c"))

    big_out = {}
    for n, (own, parts) in reduced.items():
        res = _reduce_adamw(parts, _as_local(n, w_loc[n]), _as_local(n, m_loc[n]), _as_local(n, v_loc[n]),
                            own=own, own_slot=chip, name="adamw_" + n)
        big_out[n] = [r[:, :FF_LOCAL] if n in FF_PADDED else r for r in res]

    names = [n for n, _ in REPLICATED]
    as_small = lambda d: [d[n].reshape(SMALL_SHAPE[n]) for n in names]
    small_out = _adamw_replicated([small_parts[n] for n in names], as_small(w_loc), as_small(m_loc), as_small(v_loc),
                                  name="adamw_replicated")
    small_out = [dict(zip(names, res)) for res in small_out]

    outs = [loss, grad_x[None]]
    for k in range(4):
        for n in PARAM_ORDER:
            src = big_out[n][k] if n in big_out else small_out[k][n]
            outs.append(src.reshape(args[n].shape))
    return tuple(outs)
```

```python
import functools
import math

import jax
import jax.numpy as jnp
from jax import lax
from jax.experimental import pallas as pl
from jax.experimental.pallas import tpu as pltpu

F32 = jnp.float32
BF16 = jnp.bfloat16

D_MODEL = 1024
SB_HEADS = 8
SB_HEAD_DIM = 64
SB_WIDTH = 512
SSM_WIDTH = 512
SSM_GROUP = 16
SSM_GROUPS = 32
SSM_STATE = 64
XA_HEADS = 4
XA_HEAD_DIM = 256
D_FF = 2816
RMS_EPS = 1e-6
IN_WIDTH = 4096
N_DEV = 8

ADAM_LR = 0.001
ADAM_B1 = 0.9
ADAM_B2 = 0.999
ADAM_EPS = 1e-08
ADAM_WD = 0.01
ADAM_STEP = 10

LANES = 128
SUBLANES = 8
VMEM_LIMIT = 48 * 1024 * 1024

_GELU_C = math.sqrt(2.0 / math.pi)


def _cparams(*sem):
    return pltpu.CompilerParams(dimension_semantics=sem, vmem_limit_bytes=VMEM_LIMIT)


def _pick(n, cands):
    for c in cands:
        if n % c == 0:
            return c
    return n


def _gelu(x):
    return 0.5 * x * (1.0 + jnp.tanh(_GELU_C * (x + 0.044715 * x * x * x)))


def _gelu_and_grad(x):
    t = jnp.tanh(_GELU_C * (x + 0.044715 * x * x * x))
    g = 0.5 * x * (1.0 + t)
    dg = 0.5 * (1.0 + t) + 0.5 * x * (1.0 - t * t) * _GELU_C * (1.0 + 3.0 * 0.044715 * x * x)
    return g, dg


def _sigmoid(x):
    return 1.0 / (1.0 + jnp.exp(-x))


def _dot(a, b, ca, cb):
    return lax.dot_general(a.astype(BF16), b.astype(BF16), (((ca,), (cb,)), ((), ())),
                           preferred_element_type=F32)


MM_TILES = (1024, 768, 512, 256, 128)
MM_K_TILES = (2048, 1536) + MM_TILES


def _matmul(a, b, *, ta=False, tb=False, out_dtype=F32, name, b_block0=0, n_blocks=None,
            out_cb=None, out_into=None, out_block0=0, acc_in=None, rider=None):
    if ta:
        K, M = a.shape
    else:
        M, K = a.shape
    b_cb = None
    if b.ndim == 3:
        b_cb = b.shape[2]
        n_blocks = b.shape[0] - b_block0 if n_blocks is None else n_blocks
        N, K2 = (b.shape[1], n_blocks * b_cb) if tb else (n_blocks * b_cb, b.shape[1])
    elif tb:
        N, K2 = b.shape
    else:
        K2, N = b.shape
    assert K == K2, (a.shape, b.shape, ta, tb)
    if out_into is not None:
        out_cb = out_into.shape[2]
    tm = _pick(M, MM_TILES)
    n_unit = math.gcd(N, math.gcd(b_cb if (b_cb and not tb) else N, out_cb or N))
    tn = _pick(n_unit, MM_TILES)
    k_unit = b_cb if (b_cb and tb) else K
    tk = _pick(k_unit, MM_K_TILES)
    nk = K // tk
    ca, cb = (0 if ta else 1), (1 if tb else 0)
    has_acc = acc_in is not None
    has_into = out_into is not None

    def body(*refs):
        a_ref, b_ref = refs[0], refs[1]
        pos = 2
        c_ref = None
        if has_acc:
            c_ref = refs[pos]
            pos += 1
        if has_into:
            pos += 1
        o_ref = refs[pos]
        p = _dot(a_ref[...], b_ref[...], ca, cb)
        if nk == 1:
            o_ref[...] = ((p + c_ref[...]) if has_acc else p).astype(out_dtype)
        else:
            acc_ref = refs[pos + 1]
            k = pl.program_id(2)

            @pl.when(k == 0)
            def _():
                acc_ref[...] = (p + c_ref[...]) if has_acc else p

            @pl.when(k > 0)
            def _():
                acc_ref[...] += p

            @pl.when(k == nk - 1)
            def _():
                o_ref[...] = acc_ref[...].astype(out_dtype)

    nj, ni = N // tn, M // tm
    a_bytes, b_bytes = a.size * a.dtype.itemsize, K * N * b.dtype.itemsize
    n_outer = a_bytes * nj + b_bytes * (1 if nk == 1 else ni) <= a_bytes * (1 if nk == 1 else nj) + b_bytes * ni
    grid = (nj, ni, nk) if n_outer else (ni, nj, nk)

    def spec(block, index):
        return pl.BlockSpec(block, (lambda g0, g1, k: index(g0, g1, k)) if n_outer else (lambda g0, g1, k: index(g1, g0, k)))

    a_spec = spec((tk, tm), lambda j, i, k: (k, i)) if ta else spec((tm, tk), lambda j, i, k: (i, k))
    if b_cb is None:
        b_spec = spec((tn, tk), lambda j, i, k: (j, k)) if tb else spec((tk, tn), lambda j, i, k: (k, j))
    elif tb:
        per = b_cb // tk
        b_spec = spec((None, tn, tk), lambda j, i, k: (b_block0 + k // per, j, k % per))
    else:
        per = b_cb // tn
        b_spec = spec((None, tk, tn), lambda j, i, k: (b_block0 + j // per, k, j % per))
    in_specs = [a_spec, b_spec]
    operands = [a, b]
    aliases = {}
    if has_acc:
        in_specs.append(spec((tm, tn), lambda j, i, k: (i, j)))
        operands.append(acc_in)
    if has_into:
        aliases = {len(operands): 0}
        in_specs.append(pl.BlockSpec(memory_space=pl.ANY))
        operands.append(out_into)
    if out_cb is None:
        out_shape = jax.ShapeDtypeStruct((M, N), out_dtype)
        out_spec = spec((tm, tn), lambda j, i, k: (i, j))
    else:
        per_o = out_cb // tn
        out_shape = (jax.ShapeDtypeStruct(out_into.shape, out_into.dtype) if has_into
                     else jax.ShapeDtypeStruct((N // out_cb, M, out_cb), out_dtype))
        out_spec = spec((None, tm, tn), lambda j, i, k: (out_block0 + j // per_o, i, j % per_o))
    if rider is not None:
        assert not has_into
        (out,), brought = _call(body, name=name, rider=rider, grid=grid, in_specs=in_specs,
                                out_specs=(out_spec,), out_shape=(out_shape,), operands=operands,
                                scratch_shapes=[] if nk == 1 else [pltpu.VMEM((tm, tn), F32)])
        return out, brought
    return pl.pallas_call(
        body, name=name, out_shape=out_shape,
        grid=grid,
        in_specs=in_specs, out_specs=out_spec, input_output_aliases=aliases,
        scratch_shapes=[] if nk == 1 else [pltpu.VMEM((tm, tn), F32)],
        compiler_params=_cparams("parallel", "parallel", "arbitrary"),
    )(*operands)


def _rms(x, g):
    r = lax.rsqrt(jnp.mean(x * x, axis=-1, keepdims=True) + RMS_EPS)
    return x * r * g


def _rms_bwd(dy, x, g):
    r = lax.rsqrt(jnp.mean(x * x, axis=-1, keepdims=True) + RMS_EPS)
    xh = x * r
    dxh = dy * g
    dx = r * (dxh - xh * jnp.mean(dxh * xh, axis=-1, keepdims=True))
    dg = jnp.sum(dy * xh, axis=0, keepdims=True)
    return dx, dg


def _row_tile(rows):
    return _pick(rows, (512, 256, 128, 64, 32, 16, 8))


def _rms_fwd(x, g, *, name, rider=None):
    R, D = x.shape
    tr = _row_tile(R)

    def body(x_ref, g_ref, h_ref):
        h_ref[...] = _rms(x_ref[...], g_ref[...]).astype(BF16)

    (h,), brought = _call(
        body, name=name, rider=rider, out_shape=(jax.ShapeDtypeStruct((R, D), BF16),), grid=(R // tr,),
        in_specs=[pl.BlockSpec((tr, D), lambda i: (i, 0)), pl.BlockSpec((1, D), lambda i: (0, 0))],
        out_specs=(pl.BlockSpec((tr, D), lambda i: (i, 0)),), scratch_shapes=[], operands=(x, g))
    return h if rider is None else (h, brought)


def _resnorm_norm(x, z, g_post, g_next, *, name):
    R, D = x.shape
    tr = _row_tile(R)

    def body(x_ref, z_ref, gp_ref, gn_ref, xn_ref, h_ref):
        xn = x_ref[...] + _rms(z_ref[...], gp_ref[...])
        xn_ref[...] = xn
        h_ref[...] = _rms(xn, gn_ref[...]).astype(BF16)

    row = pl.BlockSpec((tr, D), lambda i: (i, 0))
    vec = pl.BlockSpec((1, D), lambda i: (0, 0))
    return pl.pallas_call(
        body, name=name,
        out_shape=(jax.ShapeDtypeStruct((R, D), F32), jax.ShapeDtypeStruct((R, D), BF16)),
        grid=(R // tr,), in_specs=[row, row, vec, vec], out_specs=(row, row),
        compiler_params=_cparams("parallel"),
    )(x, z, g_post, g_next)


def _final_loss(x, z, g_post, target, *, name):
    R, D = x.shape
    tr = _row_tile(R)

    def body(x_ref, z_ref, gp_ref, t_ref, loss_ref, dy_ref, dz_ref, dg_ref):
        i = pl.program_id(0)
        z = z_ref[...]
        g = gp_ref[...]
        err = x_ref[...] + _rms(z, g) - t_ref[...]
        dy = err * (1.0 / D)
        dy_ref[...] = dy
        dz, dg = _rms_bwd(dy, z, g)
        dz_ref[...] = dz.astype(BF16)
        part = 0.5 * jnp.sum(jnp.sum(err * err, axis=-1, keepdims=True) * (1.0 / D), axis=0, keepdims=True)

        @pl.when(i == 0)
        def _():
            loss_ref[...] = part
            dg_ref[...] = dg

        @pl.when(i > 0)
        def _():
            loss_ref[...] += part
            dg_ref[...] += dg

    row = pl.BlockSpec((tr, D), lambda i: (i, 0))
    vec = pl.BlockSpec((1, D), lambda i: (0, 0))
    return pl.pallas_call(
        body, name=name,
        out_shape=(jax.ShapeDtypeStruct((1, 1), F32), jax.ShapeDtypeStruct((R, D), F32),
                   jax.ShapeDtypeStruct((R, D), BF16), jax.ShapeDtypeStruct((1, D), F32)),
        grid=(R // tr,), in_specs=[row, row, vec, row],
        out_specs=(pl.BlockSpec((1, 1), lambda i: (0, 0)), row, row, vec),
        compiler_params=_cparams("arbitrary"),
    )(x, z, g_post, target)


def _norm_bwd_pair(dres, dh, xk, g_pre, zprev, g_prev_post, *, name, rider=None):
    R, D = xk.shape
    tr = _row_tile(R)

    def body(dres_ref, dh_ref, x_ref, gpre_ref, z_ref, gpost_ref, dx_ref, dz_ref, dgpre_ref, dgpost_ref):
        i = pl.program_id(0)
        d1, dgpre = _rms_bwd(dh_ref[...], x_ref[...], gpre_ref[...])
        dx = dres_ref[...] + d1
        dx_ref[...] = dx
        dz, dgpost = _rms_bwd(dx, z_ref[...], gpost_ref[...])
        dz_ref[...] = dz.astype(BF16)

        @pl.when(i == 0)
        def _():
            dgpre_ref[...] = dgpre
            dgpost_ref[...] = dgpost

        @pl.when(i > 0)
        def _():
            dgpre_ref[...] += dgpre
            dgpost_ref[...] += dgpost

    row = pl.BlockSpec((tr, D), lambda i: (i, 0))
    vec = pl.BlockSpec((1, D), lambda i: (0, 0))
    return _call(
        body, name=name, rider=rider,
        out_shape=(jax.ShapeDtypeStruct((R, D), F32), jax.ShapeDtypeStruct((R, D), BF16),
                   jax.ShapeDtypeStruct((1, D), F32), jax.ShapeDtypeStruct((1, D), F32)),
        grid=(R // tr,), in_specs=[row, row, row, vec, row, vec], out_specs=(row, row, vec, vec),
        scratch_shapes=[], operands=(dres, dh, xk, g_pre, zprev, g_prev_post))


def _norm_bwd_single(dres, dh, xk, g_pre, *, name, rider=None):
    R, D = xk.shape
    tr = _row_tile(R)
    has_res = dres is not None

    def body(*refs):
        if has_res:
            dres_ref, dh_ref, x_ref, gpre_ref, dx_ref, dgpre_ref = refs
        else:
            dh_ref, x_ref, gpre_ref, dx_ref, dgpre_ref = refs
        i = pl.program_id(0)
        d1, dgpre = _rms_bwd(dh_ref[...], x_ref[...], gpre_ref[...])
        dx_ref[...] = dres_ref[...] + d1 if has_res else d1

        @pl.when(i == 0)
        def _():
            dgpre_ref[...] = dgpre

        @pl.when(i > 0)
        def _():
            dgpre_ref[...] += dgpre

    row = pl.BlockSpec((tr, D), lambda i: (i, 0))
    vec = pl.BlockSpec((1, D), lambda i: (0, 0))
    ins = ([dres] if has_res else []) + [dh, xk, g_pre]
    res, brought = _call(
        body, name=name, rider=rider,
        out_shape=(jax.ShapeDtypeStruct((R, D), F32), jax.ShapeDtypeStruct((1, D), F32)),
        grid=(R // tr,), in_specs=([row] if has_res else []) + [row, row, vec], out_specs=(row, vec),
        scratch_shapes=[], operands=ins)
    return res if rider is None else (res, brought)


SB_BLOCK = 256
SB_QBLOCK = 512
SB_DEAD = -104.0


def _sb_tri(kind):
    r = lax.broadcasted_iota(jnp.int32, (SB_BLOCK, SB_BLOCK), 0)
    c = lax.broadcasted_iota(jnp.int32, (SB_BLOCK, SB_BLOCK), 1)
    keep = {"after": r > c, "before": r < c}[kind]
    return jnp.where(keep, 1.0, 0.0).astype(BF16)


def _sb_scores(qm, k_blk):
    z = _dot(qm, k_blk, 1, 1)
    sp = jnp.maximum(z, 0.0) + jnp.log(1.0 + jnp.exp(-jnp.abs(z)))
    return z, sp


def _sb_causal(rows):
    r = lax.broadcasted_iota(jnp.int32, (rows, SB_BLOCK), 0)
    c = lax.broadcasted_iota(jnp.int32, (rows, SB_BLOCK), 1)
    return c < r


def _head_masks():
    lane = lax.broadcasted_iota(jnp.int32, (1, LANES), 1)
    return [jnp.where(lane < SB_HEAD_DIM, 1.0, 0.0), jnp.where(lane >= SB_HEAD_DIM, 1.0, 0.0)]


def _sb_fwd(proj, *, name, rider=None):
    S = proj.shape[0]
    T = SB_BLOCK
    TQ = min(SB_QBLOCK, S)
    span = TQ // T
    nq = S // TQ
    npair = SB_WIDTH // LANES
    scale = SB_HEAD_DIM ** -0.5

    def body(q_ref, k_ref, v_ref, o_ref, tot_ref, first_ref, acc_ref, run_ref):
        masks = _head_masks()
        tri = _sb_tri("after")
        first_ref[...] = jnp.zeros_like(first_ref)
        slot = lax.broadcasted_iota(jnp.int32, first_ref.shape, 1)

        def alive():
            reach = jnp.maximum(jnp.max(run_ref[0]), jnp.max(run_ref[1]))
            return (reach > SB_DEAD).astype(jnp.int32)

        def q_block(i, _):
            qrow = pl.ds(pl.multiple_of(i * TQ, TQ), TQ)
            q = q_ref[qrow, :] * scale
            qm = [(q * m).astype(BF16) for m in masks]
            acc_ref[...] = jnp.zeros_like(acc_ref)
            run_ref[...] = jnp.zeros_like(run_ref)

            def k_block(j, own):
                krow = pl.ds(pl.multiple_of(j * T, T), T)
                k_blk = k_ref[krow, :].astype(BF16)
                v_blk = v_ref[krow, :].astype(BF16)
                r0 = 0 if own is None else own * T
                rows = pl.ds(r0, TQ - r0)
                for h in range(2):
                    z, sp = _sb_scores(qm[h][r0:], k_blk)
                    causal = None if own is None else _sb_causal(TQ - r0)
                    lf = -sp if causal is None else jnp.where(causal, -sp, 0.0)
                    e = jnp.exp(z - sp + _dot(lf, tri, 1, 0) + run_ref[h, rows])
                    w = e if causal is None else jnp.where(causal, e, 0.0)
                    acc_ref[h, rows] += _dot(w, v_blk, 1, 0)
                    run_ref[h, rows] += jnp.sum(lf, axis=1, keepdims=True)

            for d in reversed(range(span)):
                k_block(i * span + d, d)

            def below(carry):
                jj, _ = carry
                k_block(i * span - 1 - jj, None)
                return jj + 1, alive()

            done, _ = lax.while_loop(lambda c: jnp.logical_and(c[0] < i * span, c[1] > 0), below, (jnp.int32(0), alive()))
            o_ref[qrow, :] = (acc_ref[0] * masks[0] + acc_ref[1] * masks[1]).astype(BF16)
            tot_ref[qrow, :] = run_ref[0] * masks[0] + run_ref[1] * masks[1]
            first_ref[...] = jnp.where(slot == i, (i * span - done).astype(F32), first_ref[...])
            return 0

        lax.fori_loop(0, nq, q_block, 0)

    blk = lambda off: pl.BlockSpec((S, LANES), lambda p: (0, off + p))
    return _call(
        body, name=name, rider=rider,
        out_shape=(jax.ShapeDtypeStruct((S, SB_WIDTH), BF16), jax.ShapeDtypeStruct((S, SB_WIDTH), F32),
                   jax.ShapeDtypeStruct((npair, SUBLANES, LANES), F32)),
        grid=(npair,),
        in_specs=[blk(0), blk(npair), blk(2 * npair)],
        out_specs=(blk(0), blk(0), pl.BlockSpec((1, SUBLANES, LANES), lambda p: (p, 0, 0))),
        scratch_shapes=[pltpu.VMEM((2, TQ, LANES), F32), pltpu.VMEM((2, TQ, 1), F32)],
        operands=(proj, proj, proj))


def _sb_bwd(proj, tot, first, do_attn, *, name, rider=None):
    S = proj.shape[0]
    T = SB_BLOCK
    TQ = min(SB_QBLOCK, S)
    span = TQ // T
    nq = S // TQ
    npair = SB_WIDTH // LANES
    scale = SB_HEAD_DIM ** -0.5

    def body(q_ref, k_ref, v_ref, tot_ref, first_ref, do_ref, dq_ref, dk_ref, dv_ref,
             dqacc_ref, dkacc_ref, dvacc_ref, run_ref, grun_ref):
        masks = _head_masks()
        tri_after = _sb_tri("after")
        tri_before = _sb_tri("before")
        dkacc_ref[...] = jnp.zeros_like(dkacc_ref)
        dvacc_ref[...] = jnp.zeros_like(dvacc_ref)
        slot = lax.broadcasted_iota(jnp.int32, first_ref.shape, 1)

        def q_block(i, _):
            qrow = pl.ds(pl.multiple_of(i * TQ, TQ), TQ)
            q = q_ref[qrow, :] * scale
            do = do_ref[qrow, :].astype(F32)
            tot = tot_ref[qrow, :]
            qm = [(q * m).astype(BF16) for m in masks]
            dom = [(do * m).astype(BF16) for m in masks]
            ltot = [jnp.sum(tot * m, axis=1, keepdims=True) * (1.0 / SB_HEAD_DIM) for m in masks]
            dqacc_ref[...] = jnp.zeros_like(dqacc_ref)
            run_ref[...] = jnp.zeros_like(run_ref)
            grun_ref[...] = jnp.zeros_like(grun_ref)

            def k_block(j, own):
                krow = pl.ds(pl.multiple_of(j * T, T), T)
                k_blk = k_ref[krow, :].astype(BF16)
                v_blk = v_ref[krow, :].astype(BF16)
                r0 = 0 if own is None else own * T
                rows = pl.ds(r0, TQ - r0)
                for h in range(2):
                    z, sp = _sb_scores(qm[h][r0:], k_blk)
                    causal = None if own is None else _sb_causal(TQ - r0)
                    lf = -sp if causal is None else jnp.where(causal, -sp, 0.0)
                    lsum = jnp.sum(lf, axis=1, keepdims=True)
                    later = (ltot[h][r0:] - run_ref[h, rows] - lsum) + _dot(lf, tri_after, 1, 0)
                    beta = jnp.exp(z - sp)
                    w = jnp.exp(z - sp + later)
                    if causal is not None:
                        w = jnp.where(causal, w, 0.0)
                    g = _dot(dom[h][r0:], v_blk, 1, 1) * w
                    gbefore = grun_ref[h, rows] + _dot(g, tri_before, 1, 0)
                    dz = g - beta * (g + gbefore)
                    if causal is not None:
                        dz = jnp.where(causal, dz, 0.0)
                    dz = dz.astype(BF16)
                    dqacc_ref[h, rows] += _dot(dz, k_blk, 1, 0)
                    dkacc_ref[krow, :] += _dot(dz, qm[h][r0:], 0, 0)
                    dvacc_ref[krow, :] += _dot(w, dom[h][r0:], 0, 0)
                    run_ref[h, rows] += lsum
                    grun_ref[h, rows] += jnp.sum(g, axis=1, keepdims=True)

            def above(j, _):
                k_block(j, None)
                return 0

            first = jnp.max(jnp.where(slot == i, first_ref[...], 0.0)).astype(jnp.int32)
            lax.fori_loop(jnp.clip(first, 0, i * span), i * span, above, 0)
            for d in range(span):
                k_block(i * span + d, d)
            dq_ref[qrow, :] = ((dqacc_ref[0] * masks[0] + dqacc_ref[1] * masks[1]) * scale).astype(BF16)
            return 0

        lax.fori_loop(0, nq, q_block, 0)
        dk_ref[...] = dkacc_ref[...].astype(BF16)
        dv_ref[...] = dvacc_ref[...].astype(BF16)

    blk = lambda off: pl.BlockSpec((S, LANES), lambda p: (0, off + p))
    out = jax.ShapeDtypeStruct((S, SB_WIDTH), BF16)
    return _call(
        body, name=name, rider=rider, out_shape=(out, out, out), grid=(npair,),
        in_specs=[blk(0), blk(npair), blk(2 * npair), blk(0), pl.BlockSpec((1, SUBLANES, LANES), lambda p: (p, 0, 0)),
                  blk(0)],
        out_specs=(blk(0), blk(0), blk(0)),
        scratch_shapes=[pltpu.VMEM((2, TQ, LANES), F32), pltpu.VMEM((S, LANES), F32), pltpu.VMEM((S, LANES), F32),
                        pltpu.VMEM((2, TQ, 1), F32), pltpu.VMEM((2, TQ, 1), F32)],
        operands=(proj, proj, proj, tot, first, do_attn))


SSM_HALVES = 2
SSM_HALF_CH = SSM_WIDTH // SSM_HALVES
SSM_HALF_ST = SSM_GROUPS * SSM_STATE // SSM_HALVES
SSM_CHUNK = 512


def _cmul(ar, ai, br, bi):
    return ar * br - ai * bi, ar * bi + ai * br


def _ssm_tables(lam_re, lam_im):
    lr = lam_re.reshape(-1)
    li = lam_im.reshape(-1)
    pows = [(jnp.ones_like(lr), jnp.zeros_like(li)), (lr, li)]
    for _ in range(2, SUBLANES + 1):
        pows.append(_cmul(pows[-1][0], pows[-1][1], lr, li))
    row = jnp.arange(SUBLANES)[:, None]

    def shift_tab(d, keep):
        return [jnp.where(keep, pows[d][0][None, :], 0.0), jnp.where(keep, pows[d][1][None, :], 0.0)]

    fwd, bwd = [], []
    for d in (1, 2, 4):
        fwd += shift_tab(d, row >= d)
        bwd += shift_tab(d, row + d < SUBLANES)
    fwd += [jnp.stack([pows[r + 1][0] for r in range(SUBLANES)]), jnp.stack([pows[r + 1][1] for r in range(SUBLANES)])]
    bwd += [jnp.stack([pows[SUBLANES - r][0] for r in range(SUBLANES)]),
            jnp.stack([pows[SUBLANES - r][1] for r in range(SUBLANES)])]

    def halves(tabs):
        t = jnp.stack(tabs)
        return t.reshape(8, SUBLANES, SSM_HALVES, SSM_HALF_ST).transpose(2, 0, 1, 3)

    return halves(fwd), halves(bwd)


def _ssm_fwd(proj, bd_re, bd_im, cd_re, cd_imneg, d_skip, tab, *, name, rider=None):
    S = proj.shape[0]
    Tc = min(SSM_CHUNK, S)
    nc = S // Tc
    u_blk0 = (3 * SB_WIDTH) // SSM_HALF_CH

    def body(u_ref, bre_ref, bim_ref, cre_ref, cim_ref, d_ref, tab_ref, y_ref, xre_ref, xim_ref, cre_s, cim_s):
        c = pl.program_id(1)

        @pl.when(c == 0)
        def _():
            cre_s[...] = jnp.zeros_like(cre_s)
            cim_s[...] = jnp.zeros_like(cim_s)

        u = u_ref[...]
        ub = u.astype(BF16)
        xre_ref[...] = _dot(ub, bre_ref[0], 1, 0)
        xim_ref[...] = _dot(ub, bim_ref[0], 1, 0)

        def slab(k, carry):
            car_re, car_im = carry
            rows = pl.ds(pl.multiple_of(k * SUBLANES, SUBLANES), SUBLANES)
            sre = xre_ref[rows, :]
            sim = xim_ref[rows, :]
            for n, d in enumerate((1, 2, 4)):
                pre, pim = tab_ref[0, 2 * n], tab_ref[0, 2 * n + 1]
                rre = pltpu.roll(sre, d, 0)
                rim = pltpu.roll(sim, d, 0)
                sre, sim = sre + (pre * rre - pim * rim), sim + (pre * rim + pim * rre)
            pre, pim = tab_ref[0, 6], tab_ref[0, 7]
            sre, sim = sre + (pre * car_re - pim * car_im), sim + (pre * car_im + pim * car_re)
            xre_ref[rows, :] = sre
            xim_ref[rows, :] = sim
            last = (SUBLANES - 1, SUBLANES)
            return (jnp.broadcast_to(sre[last[0]:last[1], :], sre.shape),
                    jnp.broadcast_to(sim[last[0]:last[1], :], sim.shape))

        car = lax.fori_loop(0, Tc // SUBLANES, slab, (cre_s[...], cim_s[...]))
        cre_s[...] = car[0]
        cim_s[...] = car[1]
        y = _dot(xre_ref[...], cre_ref[0], 1, 0) + _dot(xim_ref[...], cim_ref[0], 1, 0)
        y_ref[...] = y + d_ref[...] * u

    return _call(
        body, name=name, rider=rider,
        out_shape=(jax.ShapeDtypeStruct((S, SSM_WIDTH), F32),
                   jax.ShapeDtypeStruct((S, SSM_HALVES * SSM_HALF_ST), F32),
                   jax.ShapeDtypeStruct((S, SSM_HALVES * SSM_HALF_ST), F32)),
        grid=(SSM_HALVES, nc),
        in_specs=[pl.BlockSpec((Tc, SSM_HALF_CH), lambda h, c: (c, u_blk0 + h)),
                  pl.BlockSpec((1, SSM_HALF_CH, SSM_HALF_ST), lambda h, c: (h, 0, 0)),
                  pl.BlockSpec((1, SSM_HALF_CH, SSM_HALF_ST), lambda h, c: (h, 0, 0)),
                  pl.BlockSpec((1, SSM_HALF_ST, SSM_HALF_CH), lambda h, c: (h, 0, 0)),
                  pl.BlockSpec((1, SSM_HALF_ST, SSM_HALF_CH), lambda h, c: (h, 0, 0)),
                  pl.BlockSpec((1, SSM_HALF_CH), lambda h, c: (0, h)),
                  pl.BlockSpec((1, 8, SUBLANES, SSM_HALF_ST), lambda h, c: (h, 0, 0, 0))],
        out_specs=(pl.BlockSpec((Tc, SSM_HALF_CH), lambda h, c: (c, h)),
                   pl.BlockSpec((Tc, SSM_HALF_ST), lambda h, c: (c, h)),
                   pl.BlockSpec((Tc, SSM_HALF_ST), lambda h, c: (c, h))),
        scratch_shapes=[pltpu.VMEM((SUBLANES, SSM_HALF_ST), F32), pltpu.VMEM((SUBLANES, SSM_HALF_ST), F32)],
        operands=(proj, bd_re, bd_im, cd_re, cd_imneg, d_skip, tab))


def _ssm_bwd(dy, proj, x_re, x_im, bd_re, bd_im, cd_re, cd_imneg, d_skip, tab, *, name, rider=None):
    S = proj.shape[0]
    Tc = min(SSM_CHUNK, S)
    nc = S // Tc
    u_blk0 = (3 * SB_WIDTH) // SSM_HALF_CH

    def body(dy_ref, u_ref, xre_ref, xim_ref, bre_ref, bim_ref, cre_ref, cim_ref, d_ref, tab_ref,
             du_ref, dbre_ref, dbim_ref, dcre_ref, dcim_ref, dd_ref, dlre_ref, dlim_ref,
             gre_s, gim_s, cre_s, cim_s):
        c = pl.program_id(1)

        @pl.when(c == 0)
        def _():
            cre_s[...] = jnp.zeros_like(cre_s)
            cim_s[...] = jnp.zeros_like(cim_s)
            dbre_ref[...] = jnp.zeros_like(dbre_ref)
            dbim_ref[...] = jnp.zeros_like(dbim_ref)
            dcre_ref[...] = jnp.zeros_like(dcre_ref)
            dcim_ref[...] = jnp.zeros_like(dcim_ref)
            dd_ref[...] = jnp.zeros_like(dd_ref)
            dlre_ref[...] = jnp.zeros_like(dlre_ref)
            dlim_ref[...] = jnp.zeros_like(dlim_ref)

        dy = dy_ref[...]
        dyb = dy.astype(BF16)
        u = u_ref[...]
        gre_s[...] = _dot(dyb, cre_ref[0], 1, 1)
        gim_s[...] = _dot(dyb, cim_ref[0], 1, 1)
        row = lax.broadcasted_iota(jnp.int32, (SUBLANES, SSM_HALF_ST), 0)
        nslab = Tc // SUBLANES

        def slab(kk, carry):
            car_re, car_im, acc_re, acc_im = carry
            k = nslab - 1 - kk
            rows = pl.ds(pl.multiple_of(k * SUBLANES, SUBLANES), SUBLANES)
            sre = gre_s[rows, :]
            sim = gim_s[rows, :]
            for n, d in enumerate((1, 2, 4)):
                pre, pim = tab_ref[0, 2 * n], tab_ref[0, 2 * n + 1]
                rre = pltpu.roll(sre, SUBLANES - d, 0)
                rim = pltpu.roll(sim, SUBLANES - d, 0)
                sre, sim = sre + (pre * rre + pim * rim), sim + (pre * rim - pim * rre)
            pre, pim = tab_ref[0, 6], tab_ref[0, 7]
            sre, sim = sre + (pre * car_re + pim * car_im), sim + (pre * car_im - pim * car_re)
            gre_s[rows, :] = sre
            gim_s[rows, :] = sim
            nre = jnp.where(row == SUBLANES - 1, car_re, pltpu.roll(sre, SUBLANES - 1, 0))
            nim = jnp.where(row == SUBLANES - 1, car_im, pltpu.roll(sim, SUBLANES - 1, 0))
            xr = xre_ref[rows, :]
            xi = xim_ref[rows, :]
            acc_re = acc_re + (nre * xr + nim * xi)
            acc_im = acc_im + (nim * xr - nre * xi)
            return (jnp.broadcast_to(sre[0:1, :], sre.shape), jnp.broadcast_to(sim[0:1, :], sim.shape), acc_re, acc_im)

        car = lax.fori_loop(0, nslab, slab, (cre_s[...], cim_s[...], dlre_ref[0], dlim_ref[0]))
        cre_s[...] = car[0]
        cim_s[...] = car[1]
        dlre_ref[0] = car[2]
        dlim_ref[0] = car[3]
        gre = gre_s[...].astype(BF16)
        gim = gim_s[...].astype(BF16)
        ub = u.astype(BF16)
        du = _dot(gre, bre_ref[0], 1, 1) + _dot(gim, bim_ref[0], 1, 1) + d_ref[...] * dy
        du_ref[...] = du.astype(BF16)
        dbre_ref[0] += _dot(ub, gre, 0, 0)
        dbim_ref[0] += _dot(ub, gim, 0, 0)
        dcre_ref[0] += _dot(xre_ref[...], dyb, 0, 0)
        dcim_ref[0] += _dot(xim_ref[...], dyb, 0, 0)
        dd_ref[...] += jnp.sum(dy * u, axis=0, keepdims=True)

    rev = lambda c: nc - 1 - c
    return _call(
        body, name=name, rider=rider,
        out_shape=(jax.ShapeDtypeStruct((S, SSM_WIDTH), BF16),
                   jax.ShapeDtypeStruct((SSM_HALVES, SSM_HALF_CH, SSM_HALF_ST), F32),
                   jax.ShapeDtypeStruct((SSM_HALVES, SSM_HALF_CH, SSM_HALF_ST), F32),
                   jax.ShapeDtypeStruct((SSM_HALVES, SSM_HALF_ST, SSM_HALF_CH), F32),
                   jax.ShapeDtypeStruct((SSM_HALVES, SSM_HALF_ST, SSM_HALF_CH), F32),
                   jax.ShapeDtypeStruct((1, SSM_WIDTH), F32),
                   jax.ShapeDtypeStruct((SSM_HALVES, SUBLANES, SSM_HALF_ST), F32),
                   jax.ShapeDtypeStruct((SSM_HALVES, SUBLANES, SSM_HALF_ST), F32)),
        grid=(SSM_HALVES, nc),
        in_specs=[pl.BlockSpec((Tc, SSM_HALF_CH), lambda h, c: (rev(c), h)),
                  pl.BlockSpec((Tc, SSM_HALF_CH), lambda h, c: (rev(c), u_blk0 + h)),
                  pl.BlockSpec((Tc, SSM_HALF_ST), lambda h, c: (rev(c), h)),
                  pl.BlockSpec((Tc, SSM_HALF_ST), lambda h, c: (rev(c), h)),
                  pl.BlockSpec((1, SSM_HALF_CH, SSM_HALF_ST), lambda h, c: (h, 0, 0)),
                  pl.BlockSpec((1, SSM_HALF_CH, SSM_HALF_ST), lambda h, c: (h, 0, 0)),
                  pl.BlockSpec((1, SSM_HALF_ST, SSM_HALF_CH), lambda h, c: (h, 0, 0)),
                  pl.BlockSpec((1, SSM_HALF_ST, SSM_HALF_CH), lambda h, c: (h, 0, 0)),
                  pl.BlockSpec((1, SSM_HALF_CH), lambda h, c: (0, h)),
                  pl.BlockSpec((1, 8, SUBLANES, SSM_HALF_ST), lambda h, c: (h, 0, 0, 0))],
        out_specs=(pl.BlockSpec((Tc, SSM_HALF_CH), lambda h, c: (rev(c), h)),
                   pl.BlockSpec((1, SSM_HALF_CH, SSM_HALF_ST), lambda h, c: (h, 0, 0)),
                   pl.BlockSpec((1, SSM_HALF_CH, SSM_HALF_ST), lambda h, c: (h, 0, 0)),
                   pl.BlockSpec((1, SSM_HALF_ST, SSM_HALF_CH), lambda h, c: (h, 0, 0)),
                   pl.BlockSpec((1, SSM_HALF_ST, SSM_HALF_CH), lambda h, c: (h, 0, 0)),
                   pl.BlockSpec((1, SSM_HALF_CH), lambda h, c: (0, h)),
                   pl.BlockSpec((1, SUBLANES, SSM_HALF_ST), lambda h, c: (h, 0, 0)),
                   pl.BlockSpec((1, SUBLANES, SSM_HALF_ST), lambda h, c: (h, 0, 0))),
        scratch_shapes=[pltpu.VMEM((Tc, SSM_HALF_ST), F32), pltpu.VMEM((Tc, SSM_HALF_ST), F32),
                        pltpu.VMEM((SUBLANES, SSM_HALF_ST), F32), pltpu.VMEM((SUBLANES, SSM_HALF_ST), F32)],
        operands=(dy, proj, x_re, x_im, bd_re, bd_im, cd_re, cd_imneg, d_skip, tab))


def _ssm_prepare(a_re, a_im, log_dt, b_re, b_im):
    dt = jnp.exp(log_dt)[:, None]
    mag = jnp.exp(a_re * dt)
    lre = mag * jnp.cos(a_im * dt)
    lim = mag * jnp.sin(a_im * dt)
    den = a_re * a_re + a_im * a_im
    fre = ((lre - 1.0) * a_re + lim * a_im) / den
    fim = (lim * a_re - (lre - 1.0) * a_im) / den
    bbre = fre[:, :, None] * b_re - fim[:, :, None] * b_im
    bbim = fre[:, :, None] * b_im + fim[:, :, None] * b_re
    return lre, lim, bbre, bbim


def _group_eye():
    return jnp.eye(SSM_GROUPS // SSM_HALVES, dtype=F32)


def _bd_from_bbar(bbar):
    gh = SSM_GROUPS // SSM_HALVES
    b = bbar.reshape(SSM_HALVES, gh, SSM_STATE, SSM_GROUP).transpose(0, 1, 3, 2)
    out = b[:, :, :, None, :] * _group_eye()[None, :, None, :, None]
    return out.reshape(SSM_HALVES, SSM_HALF_CH, SSM_HALF_ST)


def _bbar_from_bd(dbd):
    gh = SSM_GROUPS // SSM_HALVES
    d = dbd.reshape(SSM_HALVES, gh, SSM_GROUP, gh, SSM_STATE)
    d = jnp.sum(d * _group_eye()[None, :, None, :, None], axis=3)
    return d.transpose(0, 1, 3, 2).reshape(SSM_GROUPS, SSM_STATE, SSM_GROUP)


def _cd_from_c(cmat):
    gh = SSM_GROUPS // SSM_HALVES
    c = cmat.reshape(SSM_HALVES, gh, SSM_GROUP, SSM_STATE).transpose(0, 1, 3, 2)
    out = c[:, :, :, None, :] * _group_eye()[None, :, None, :, None]
    return out.reshape(SSM_HALVES, SSM_HALF_ST, SSM_HALF_CH)


def _c_from_cd(dcd):
    gh = SSM_GROUPS // SSM_HALVES
    d = dcd.reshape(SSM_HALVES, gh, SSM_STATE, gh, SSM_GROUP)
    d = jnp.sum(d * _group_eye()[None, :, None, :, None], axis=3)
    return d.transpose(0, 1, 3, 2).reshape(SSM_GROUPS, SSM_GROUP, SSM_STATE)


def _glu_fwd(y_pre, w_glu, b_glu, *, name):
    S, W = y_pre.shape
    tr = _row_tile(S)

    def body(y_ref, w_ref, b_ref, o_ref):
        yg = _gelu(y_ref[...])
        gl = _dot(yg, w_ref[...], 1, 0) + b_ref[...]
        o_ref[...] = (yg * _sigmoid(gl)).astype(BF16)

    row = pl.BlockSpec((tr, W), lambda i: (i, 0))
    return pl.pallas_call(
        body, name=name, out_shape=jax.ShapeDtypeStruct((S, W), BF16), grid=(S // tr,),
        in_specs=[row, pl.BlockSpec((W, W), lambda i: (0, 0)), pl.BlockSpec((1, W), lambda i: (0, 0))],
        out_specs=row, compiler_params=_cparams("parallel"),
    )(y_pre, w_glu, b_glu)


def _glu_bwd(y_pre, do, w_glu, b_glu, *, name):
    S, W = y_pre.shape
    tr = _row_tile(S)

    def body(y_ref, do_ref, w_ref, b_ref, dy_ref, dw_ref, db_ref):
        i = pl.program_id(0)
        yg, dyg_dy = _gelu_and_grad(y_ref[...])
        ygb = yg.astype(BF16)
        sg = _sigmoid(_dot(ygb, w_ref[...], 1, 0) + b_ref[...])
        do = do_ref[...]
        dgl = do * yg * sg * (1.0 - sg)
        dglb = dgl.astype(BF16)
        dyg = do * sg + _dot(dglb, w_ref[...], 1, 1)
        dy_ref[...] = dyg * dyg_dy
        dw = _dot(ygb, dglb, 0, 0)
        db = jnp.sum(dgl, axis=0, keepdims=True)

        @pl.when(i == 0)
        def _():
            dw_ref[...] = dw
            db_ref[...] = db

        @pl.when(i > 0)
        def _():
            dw_ref[...] += dw
            db_ref[...] += db

    row = pl.BlockSpec((tr, W), lambda i: (i, 0))
    full = pl.BlockSpec((W, W), lambda i: (0, 0))
    vec = pl.BlockSpec((1, W), lambda i: (0, 0))
    return pl.pallas_call(
        body, name=name,
        out_shape=(jax.ShapeDtypeStruct((S, W), F32), jax.ShapeDtypeStruct((W, W), F32), jax.ShapeDtypeStruct((1, W), F32)),
        grid=(S // tr,), in_specs=[row, row, full, vec], out_specs=(row, full, vec),
        compiler_params=_cparams("arbitrary"),
    )(y_pre, do, w_glu, b_glu)


GATE_COL0 = 3 * SB_WIDTH + SSM_WIDTH


def _merge_fwd(proj, o_attn, o_ssm, w_ba, w_bs, b_gate, *, name):
    S = proj.shape[0]
    D = D_MODEL
    tr = _pick(S, (256, 128, 64, 32, 16, 8))
    gb = GATE_COL0 // D

    def body(ga_ref, gs_ref, oa_ref, os_ref, wa_ref, ws_ref, ba_ref, bs_ref, m_ref):
        pa = _dot(oa_ref[...], wa_ref[...], 1, 0)
        ps = _dot(os_ref[...], ws_ref[...], 1, 0)
        sa = _sigmoid(ga_ref[...] + ba_ref[...])
        ss = _sigmoid(gs_ref[...] + bs_ref[...])
        m_ref[...] = (sa * pa + ss * ps).astype(BF16)

    return pl.pallas_call(
        body, name=name, out_shape=jax.ShapeDtypeStruct((S, D), BF16), grid=(S // tr,),
        in_specs=[pl.BlockSpec((tr, D), lambda i: (i, gb)), pl.BlockSpec((tr, D), lambda i: (i, gb + 1)),
                  pl.BlockSpec((tr, SB_WIDTH), lambda i: (i, 0)), pl.BlockSpec((tr, SSM_WIDTH), lambda i: (i, 0)),
                  pl.BlockSpec((SB_WIDTH, D), lambda i: (0, 0)), pl.BlockSpec((SSM_WIDTH, D), lambda i: (0, 0)),
                  pl.BlockSpec((1, D), lambda i: (0, 0)), pl.BlockSpec((1, D), lambda i: (0, 1))],
        out_specs=pl.BlockSpec((tr, D), lambda i: (i, 0)),
        compiler_params=_cparams("parallel"),
    )(proj, proj, o_attn, o_ssm, w_ba, w_bs, b_gate, b_gate)


def _merge_bwd(dmerged, proj, o_attn, o_ssm, w_ba, w_bs, b_gate, *, name):
    S = proj.shape[0]
    D = D_MODEL
    tr = _pick(S, (256, 128, 64, 32, 16, 8))
    gb = GATE_COL0 // D

    def body(dm_ref, ga_ref, gs_ref, oa_ref, os_ref, wa_ref, ws_ref, ba_ref, bs_ref,
             doa_ref, dos_ref, dg_ref, db_ref, dwa_ref, dws_ref):
        i = pl.program_id(0)
        dm = dm_ref[...]
        oa = oa_ref[...]
        osm = os_ref[...]
        pa = _dot(oa, wa_ref[...], 1, 0)
        ps = _dot(osm, ws_ref[...], 1, 0)
        sa = _sigmoid(ga_ref[...] + ba_ref[...])
        ss = _sigmoid(gs_ref[...] + bs_ref[...])
        dpa = (dm * sa).astype(BF16)
        dps = (dm * ss).astype(BF16)
        dga = dm * pa * sa * (1.0 - sa)
        dgs = dm * ps * ss * (1.0 - ss)
        dg_ref[:, :D] = dga.astype(BF16)
        dg_ref[:, D:] = dgs.astype(BF16)
        doa_ref[...] = _dot(dpa, wa_ref[...], 1, 1).astype(BF16)
        dos_ref[...] = _dot(dps, ws_ref[...], 1, 1)
        dwa = _dot(oa, dpa, 0, 0)
        dws = _dot(osm, dps, 0, 0)
        dba = jnp.sum(dga, axis=0, keepdims=True)
        dbs = jnp.sum(dgs, axis=0, keepdims=True)

        @pl.when(i == 0)
        def _():
            dwa_ref[...] = dwa
            dws_ref[...] = dws
            db_ref[:, :D] = dba
            db_ref[:, D:] = dbs

        @pl.when(i > 0)
        def _():
            dwa_ref[...] += dwa
            dws_ref[...] += dws
            db_ref[:, :D] += dba
            db_ref[:, D:] += dbs

    rowD = pl.BlockSpec((tr, D), lambda i: (i, 0))
    wspec = pl.BlockSpec((SB_WIDTH, D), lambda i: (0, 0))
    return pl.pallas_call(
        body, name=name,
        out_shape=(jax.ShapeDtypeStruct((S, SB_WIDTH), BF16), jax.ShapeDtypeStruct((S, SSM_WIDTH), F32),
                   jax.ShapeDtypeStruct((S, 2 * D), BF16), jax.ShapeDtypeStruct((1, 2 * D), F32),
                   jax.ShapeDtypeStruct((SB_WIDTH, D), F32), jax.ShapeDtypeStruct((SSM_WIDTH, D), F32)),
        grid=(S // tr,),
        in_specs=[rowD, pl.BlockSpec((tr, D), lambda i: (i, gb)), pl.BlockSpec((tr, D), lambda i: (i, gb + 1)),
                  pl.BlockSpec((tr, SB_WIDTH), lambda i: (i, 0)), pl.BlockSpec((tr, SSM_WIDTH), lambda i: (i, 0)),
                  wspec, wspec, pl.BlockSpec((1, D), lambda i: (0, 0)), pl.BlockSpec((1, D), lambda i: (0, 1))],
        out_specs=(pl.BlockSpec((tr, SB_WIDTH), lambda i: (i, 0)), pl.BlockSpec((tr, SSM_WIDTH), lambda i: (i, 0)),
                   pl.BlockSpec((tr, 2 * D), lambda i: (i, 0)), pl.BlockSpec((1, 2 * D), lambda i: (0, 0)),
                   wspec, wspec),
        compiler_params=_cparams("arbitrary"),
    )(dmerged, proj, proj, o_attn, o_ssm, w_ba, w_bs, b_gate, b_gate)


def _xattn_probs(q, k, h):
    cols = slice(h * XA_HEAD_DIM, (h + 1) * XA_HEAD_DIM)
    s = _dot(q[:, cols], k[:, cols], 1, 1) * (XA_HEAD_DIM ** -0.5)
    s = s - jnp.max(s, axis=-1, keepdims=True)
    e = jnp.exp(s)
    return e / jnp.sum(e, axis=-1, keepdims=True), cols


def _xattn_fwd(q2, k2, v2, *, name):
    S, D = q2.shape
    M = k2.shape[0]
    tr = _row_tile(S)

    def body(q_ref, k_ref, v_ref, o_ref):
        q = q_ref[...]
        k = k_ref[...]
        v = v_ref[...]
        for h in range(XA_HEADS):
            p, cols = _xattn_probs(q, k, h)
            o_ref[:, cols] = _dot(p, v[:, cols], 1, 0).astype(BF16)

    row = pl.BlockSpec((tr, D), lambda i: (i, 0))
    memb = pl.BlockSpec((M, D), lambda i: (0, 0))
    return pl.pallas_call(
        body, name=name, out_shape=jax.ShapeDtypeStruct((S, D), BF16), grid=(S // tr,),
        in_specs=[row, memb, memb], out_specs=row, compiler_params=_cparams("parallel"),
    )(q2, k2, v2)


def _xattn_bwd(q2, k2, v2, do2, *, name):
    S, D = q2.shape
    M = k2.shape[0]
    tr = _row_tile(S)
    scale = XA_HEAD_DIM ** -0.5

    def body(q_ref, k_ref, v_ref, do_ref, dq_ref, dk_ref, dv_ref):
        i = pl.program_id(0)

        @pl.when(i == 0)
        def _():
            dk_ref[...] = jnp.zeros_like(dk_ref)
            dv_ref[...] = jnp.zeros_like(dv_ref)

        q = q_ref[...]
        k = k_ref[...]
        v = v_ref[...]
        do = do_ref[...]
        for h in range(XA_HEADS):
            p, cols = _xattn_probs(q, k, h)
            dp = _dot(do[:, cols], v[:, cols], 1, 1)
            ds = (p * (dp - jnp.sum(dp * p, axis=-1, keepdims=True)) * scale).astype(BF16)
            dq_ref[:, cols] = _dot(ds, k[:, cols], 1, 0).astype(BF16)
            dk_ref[:, cols] += _dot(ds, q[:, cols], 0, 0)
            dv_ref[:, cols] += _dot(p, do[:, cols], 0, 0)

    row = pl.BlockSpec((tr, D), lambda i: (i, 0))
    memb = pl.BlockSpec((M, D), lambda i: (0, 0))
    return pl.pallas_call(
        body, name=name,
        out_shape=(jax.ShapeDtypeStruct((S, D), BF16), jax.ShapeDtypeStruct((M, D), F32), jax.ShapeDtypeStruct((M, D), F32)),
        grid=(S // tr,), in_specs=[row, memb, memb, row], out_specs=(row, memb, memb),
        compiler_params=_cparams("arbitrary"),
    )(q2, k2, v2, do2)


CONV_ROWS = 64
CONV_ROWS_FWD = 256


def _chunk(ref, c, rows):
    return ref[pl.ds(pl.multiple_of(c * rows, rows), rows), :]


def _rows_before(ref, c, rows):
    t0 = pl.multiple_of(jnp.maximum(c * rows - SUBLANES, 0), SUBLANES)
    return jnp.where(c > 0, ref[pl.ds(t0, SUBLANES), :], 0.0)


def _rows_after(ref, c, rows, n_chunks):
    t0 = pl.multiple_of(jnp.minimum((c + 1) * rows, n_chunks * rows - SUBLANES), SUBLANES)
    return jnp.where(c < n_chunks - 1, ref[pl.ds(t0, SUBLANES), :], 0.0)


def _shift_down(cur, before, d):
    out = pltpu.roll(cur, d, 0)
    r = lax.broadcasted_iota(jnp.int32, cur.shape, 0)
    for e in range(d):
        out = jnp.where(r == e, before[SUBLANES - d + e:SUBLANES - d + e + 1, :], out)
    return out


def _shift_up(cur, after, d):
    rows = cur.shape[0]
    out = pltpu.roll(cur, rows - d, 0)
    r = lax.broadcasted_iota(jnp.int32, cur.shape, 0)
    for e in range(d):
        out = jnp.where(r == rows - d + e, after[e:e + 1, :], out)
    return out


def _conv3(cur, before, w_ref, b_ref):
    return (w_ref[2:3, :] * cur + w_ref[1:2, :] * _shift_down(cur, before, 1)
            + w_ref[0:1, :] * _shift_down(cur, before, 2) + b_ref[...])


def _convgate_fwd(up_g, up_v, conv_w, conv_b, *, name):
    S, H = up_g.shape
    nb = H // LANES
    R = min(CONV_ROWS_FWD, S)
    n_chunks = S // R

    def body(g_ref, v_ref, wg_ref, wv_ref, bg_ref, bv_ref, a_ref):
        def chunk(c, _):
            cg = _conv3(_chunk(g_ref, c, R), _rows_before(g_ref, c, R), wg_ref, bg_ref)
            cv = _conv3(_chunk(v_ref, c, R), _rows_before(v_ref, c, R), wv_ref, bv_ref)
            a_ref[pl.ds(pl.multiple_of(c * R, R), R), :] = (_gelu(cg) * cv).astype(BF16)
            return 0

        lax.fori_loop(0, n_chunks, chunk, 0)

    col = lambda off: pl.BlockSpec((S, LANES), lambda j: (0, off + j))
    wcol = lambda off: pl.BlockSpec((3, LANES), lambda j: (0, off + j))
    bcol = lambda off: pl.BlockSpec((1, LANES), lambda j: (0, off + j))
    return pl.pallas_call(
        body, name=name, out_shape=jax.ShapeDtypeStruct((S, H), BF16), grid=(nb,),
        in_specs=[col(0), col(0), wcol(0), wcol(nb), bcol(0), bcol(nb)],
        out_specs=col(0), compiler_params=_cparams("parallel"),
    )(up_g, up_v, conv_w, conv_w, conv_b, conv_b)


def _convgate_bwd(up_g, up_v, da, conv_w, conv_b, *, name):
    S, H = up_g.shape
    nb = H // LANES
    R = min(CONV_ROWS, S)
    n_chunks = S // R

    def fold(a):
        return sum(a[r:r + SUBLANES] for r in range(0, a.shape[0], SUBLANES))

    def body(g_ref, v_ref, da_ref, wg_ref, wv_ref, bg_ref, bv_ref,
             dug_ref, duv_ref, dwg_ref, dwv_ref, dbg_ref, dbv_ref, dcg_s, dcv_s):
        def first_pass(c, acc):
            rows = pl.ds(pl.multiple_of(c * R, R), R)
            ug, uv = _chunk(g_ref, c, R), _chunk(v_ref, c, R)
            bg, bv = _rows_before(g_ref, c, R), _rows_before(v_ref, c, R)
            cg = _conv3(ug, bg, wg_ref, bg_ref)
            cv = _conv3(uv, bv, wv_ref, bv_ref)
            da = da_ref[rows, :]
            gl, dgl = _gelu_and_grad(cg)
            dcg = da * cv * dgl
            dcv = da * gl
            dcg_s[rows, :] = dcg
            dcv_s[rows, :] = dcv
            new = []
            for dc, u, before in ((dcg, ug, bg), (dcv, uv, bv)):
                new += [fold(dc * _shift_down(u, before, 2)), fold(dc * _shift_down(u, before, 1)), fold(dc * u), fold(dc)]
            return tuple(a + n for a, n in zip(acc, new))

        zero = jnp.zeros((SUBLANES, LANES), F32)
        acc = lax.fori_loop(0, n_chunks, first_pass, (zero,) * 8)
        total = [jnp.sum(a, axis=0, keepdims=True) for a in acc]
        for k, (dw_ref, db_ref) in enumerate(((dwg_ref, dbg_ref), (dwv_ref, dbv_ref))):
            dw_ref[0:1, :] = total[4 * k]
            dw_ref[1:2, :] = total[4 * k + 1]
            dw_ref[2:3, :] = total[4 * k + 2]
            db_ref[...] = total[4 * k + 3]

        def second_pass(c, _):
            rows = pl.ds(pl.multiple_of(c * R, R), R)
            for dc_s, w_ref, du_ref in ((dcg_s, wg_ref, dug_ref), (dcv_s, wv_ref, duv_ref)):
                cur, after = _chunk(dc_s, c, R), _rows_after(dc_s, c, R, n_chunks)
                du = w_ref[2:3, :] * cur + w_ref[1:2, :] * _shift_up(cur, after, 1) + w_ref[0:1, :] * _shift_up(cur, after, 2)
                du_ref[rows, :] = du.astype(BF16)
            return 0

        lax.fori_loop(0, n_chunks, second_pass, 0)

    col = lambda off: pl.BlockSpec((S, LANES), lambda j: (0, off + j))
    wcol = lambda off: pl.BlockSpec((3, LANES), lambda j: (0, off + j))
    bcol = lambda off: pl.BlockSpec((1, LANES), lambda j: (0, off + j))
    return pl.pallas_call(
        body, name=name,
        out_shape=(jax.ShapeDtypeStruct((S, H), BF16), jax.ShapeDtypeStruct((S, H), BF16),
                   jax.ShapeDtypeStruct((3, H), F32), jax.ShapeDtypeStruct((3, H), F32),
                   jax.ShapeDtypeStruct((1, H), F32), jax.ShapeDtypeStruct((1, H), F32)),
        grid=(nb,),
        in_specs=[col(0), col(0), col(0), wcol(0), wcol(nb), bcol(0), bcol(nb)],
        out_specs=(col(0), col(0), wcol(0), wcol(0), bcol(0), bcol(0)),
        scratch_shapes=[pltpu.VMEM((S, LANES), F32), pltpu.VMEM((S, LANES), F32)],
        compiler_params=_cparams("parallel"),
    )(up_g, up_v, da, conv_w, conv_w, conv_b, conv_b)


def _local_step(x, mem, target, w_in, late_wire, P, core):
    mm = _matmul
    h1, (w_in,) = _rms_fwd(x, P["norm_mix_pre"], name="rms_mix_pre", rider=_fill_xy([w_in]))
    w_in, = _fill_c([w_in]).run(name="gather_in_c")
    w_in = w_in.reshape((N_DEV,) + w_in.shape[2:])
    n_mid = len(LATE) - len(REDUCE_FFN)
    proj, wire_mid = mm(h1, w_in, name="mm_in", rider=_fill_xy(late_wire[:n_mid]))
    (o_attn, sb_tot, sb_first), wires = _sb_fwd(
        proj, name="sb_fwd", rider=_Exchange.join(_fill_c(wire_mid), _fill_xy(late_wire[n_mid:])))
    wire_mid, wire_ffn = wires[:n_mid], wires[n_mid:]

    ssm_prep = lambda *a: _ssm_prepare(*a)
    (lam_re, lam_im, bb_re, bb_im), prep_vjp = jax.vjp(
        ssm_prep, P["ssm_a_re"], P["ssm_a_im"], P["ssm_log_dt"], P["ssm_b_re"], P["ssm_b_im"])
    tab_f, tab_b = _ssm_tables(lam_re, lam_im)
    bd_re = _bd_from_bbar(bb_re).astype(BF16)
    bd_im = _bd_from_bbar(bb_im).astype(BF16)
    cd_re = _cd_from_c(P["ssm_c_re"]).astype(BF16)
    cd_imneg = _cd_from_c(-P["ssm_c_im"]).astype(BF16)
    (y_pre, x_re, x_im), wire_ffn = _ssm_fwd(proj, bd_re, bd_im, cd_re, cd_imneg, P["ssm_d"], tab_f,
                                             name="ssm_fwd", rider=_fill_c(wire_ffn))
    W = _weights_from_wire(dict(zip(LATE, list(wire_mid) + list(wire_ffn))))
    W["w_in"] = w_in
    o_ssm = _glu_fwd(y_pre, W["ssm_w_glu"], P["ssm_b_glu"], name="glu_fwd")

    merged = _merge_fwd(proj, o_attn, o_ssm, W["w_branch_attn"], W["w_branch_ssm"], P["b_gate"], name="merge_fwd")
    mo = mm(merged, W["w_out"], name="mm_out")
    x1, h2 = _resnorm_norm(x, mo, P["norm_mix_post"], P["norm_xa_pre"], name="resnorm_1")

    mem_n = _rms_fwd(mem, P["norm_mem"], name="rms_mem")
    q2 = mm(h2, W["xa_wq"], out_dtype=BF16, name="mm_xq")
    k2 = mm(mem_n, W["xa_wk"], out_dtype=BF16, name="mm_xk")
    v2 = mm(mem_n, W["xa_wv"], out_dtype=BF16, name="mm_xv")
    o2 = _xattn_fwd(q2, k2, v2, name="xattn_fwd")
    xa = mm(o2, W["xa_wo"], name="mm_xo")
    x2, h3 = _resnorm_norm(x1, xa, P["norm_xa_post"], P["norm_ffn_pre"], name="resnorm_2")

    half = N_DEV // 2
    up_g = mm(h3, W["ffn_w_up"], n_blocks=half, name="mm_up_g")
    up_v = mm(h3, W["ffn_w_up"], b_block0=half, name="mm_up_v")
    act = _convgate_fwd(up_g, up_v, W["ffn_conv_w"], P["ffn_conv_b"], name="convgate_fwd")
    f = mm(act, W["ffn_w_down"], name="mm_down")
    loss, dy, df, dg_ffn_post = _final_loss(x2, f, P["norm_ffn_post"], target, name="final_loss")

    G = {"norm_ffn_post": dg_ffn_post}
    dact = mm(df, W["ffn_w_down"], tb=True, name="mm_down_dx")
    G["ffn_w_down"] = mm(act, df, ta=True, name="mm_down_dw")
    dug, duv, dwg, dwv, dbg, dbv = _convgate_bwd(up_g, up_v, dact, W["ffn_conv_w"], P["ffn_conv_b"], name="convgate_bwd")
    G["ffn_conv_w"] = jnp.concatenate([dwg, dwv], axis=1)
    G["ffn_conv_b"] = jnp.concatenate([dbg, dbv], axis=1)
    dh3 = mm(dug, W["ffn_w_up"], tb=True, n_blocks=half, name="mm_up_g_dx")
    dh3 = mm(duv, W["ffn_w_up"], tb=True, b_block0=half, acc_in=dh3, name="mm_up_v_dx")
    dw_up = mm(h3, dug, ta=True, out_into=lax.empty(W["ffn_w_up"].shape, F32), name="mm_up_g_dw")
    G["ffn_w_up"] = mm(h3, duv, ta=True, out_into=dw_up, out_block0=half, name="mm_up_v_dw")
    blocks = {n: _grad_blocks(n, G[n]) for n in REDUCE_FFN}
    (dx2, dxa, G["norm_ffn_pre"], G["norm_xa_post"]), from_core = _norm_bwd_pair(
        dy, dh3, x2, P["norm_ffn_pre"], xa, P["norm_xa_post"], name="norm_bwd_3",
        rider=_send_c([blocks[n] for n in REDUCE_FFN]))
    pair = {n: _pair_sum(blocks[n], r, core, name="pair_sum_" + n) for n, r in zip(REDUCE_FFN, from_core)}

    G["xa_wo"] = mm(o2, dxa, ta=True, name="mm_xo_dw")
    do2 = mm(dxa, W["xa_wo"], tb=True, out_dtype=BF16, name="mm_xo_dx")
    dq2, dk2, dv2 = _xattn_bwd(q2, k2, v2, do2, name="xattn_bwd")
    G["xa_wq"] = mm(h2, dq2, ta=True, name="mm_xq_dw")
    dh2 = mm(dq2, W["xa_wq"], tb=True, name="mm_xq_dx")
    G["xa_wk"] = mm(mem_n, dk2, ta=True, name="mm_xk_dw")
    G["xa_wv"] = mm(mem_n, dv2, ta=True, name="mm_xv_dw")
    dmem_n = jnp.concatenate([dk2, dv2], axis=1)
    wkv = jnp.concatenate([W["xa_wk"], W["xa_wv"]], axis=1)
    dmem = mm(dmem_n, wkv, tb=True, name="mm_xkv_dx")
    _, G["norm_mem"] = _norm_bwd_single(None, dmem, mem, P["norm_mem"], name="norm_bwd_mem")
    (dx1, dmo, G["norm_xa_pre"], G["norm_mix_post"]), _ = _norm_bwd_pair(
        dx2, dh2, x1, P["norm_xa_pre"], mo, P["norm_mix_post"], name="norm_bwd_2")

    G["w_out"] = mm(merged, dmo, ta=True, name="mm_out_dw")
    dmerged = mm(dmo, W["w_out"], tb=True, name="mm_out_dx")
    do_attn, do_ssm, dgate, G["b_gate"], G["w_branch_attn"], G["w_branch_ssm"] = _merge_bwd(
        dmerged, proj, o_attn, o_ssm, W["w_branch_attn"], W["w_branch_ssm"], P["b_gate"], name="merge_bwd")
    dy_pre, G["ssm_w_glu"], G["ssm_b_glu"] = _glu_bwd(y_pre, do_ssm, W["ssm_w_glu"], P["ssm_b_glu"], name="glu_bwd")
    blocks.update({n: _grad_blocks(n, G[n]) for n in REDUCE_MID})
    (du, dbd_re, dbd_im, dcd_re, dcd_imneg, G["ssm_d"], dl_re, dl_im), brought = _ssm_bwd(
        dy_pre, proj, x_re, x_im, bd_re, bd_im, cd_re, cd_imneg, P["ssm_d"], tab_b, name="ssm_bwd",
        rider=_Exchange.join(_send_c([blocks[n] for n in REDUCE_MID]), _scatter_xy([pair[n] for n in REDUCE_FFN])))
    from_core, from_chips = brought[:len(REDUCE_MID)], brought[len(REDUCE_MID):]
    reduced = {n: (pair[n], parts) for n, parts in zip(REDUCE_FFN, from_chips)}
    pair.update({n: _pair_sum(blocks[n], r, core, name="pair_sum_" + n) for n, r in zip(REDUCE_MID, from_core)})
    G["ssm_c_re"] = _c_from_cd(dcd_re)
    G["ssm_c_im"] = -_c_from_cd(dcd_imneg)
    dlam_re = jnp.sum(dl_re, axis=1).reshape(SSM_GROUPS, SSM_STATE)
    dlam_im = jnp.sum(dl_im, axis=1).reshape(SSM_GROUPS, SSM_STATE)
    (G["ssm_a_re"], G["ssm_a_im"], G["ssm_log_dt"], G["ssm_b_re"], G["ssm_b_im"]) = prep_vjp(
        (dlam_re, dlam_im, _bbar_from_bd(dbd_re), _bbar_from_bd(dbd_im)))
    G["ffn_conv_b"] = G["ffn_conv_b"].reshape(N_DEV, FF_LOCAL_PAD)[:, :FF_LOCAL]
    small = [G[n].reshape(SMALL_SHAPE[n]) for n in SMALL_EARLY]
    (dq, dk, dv), brought = _sb_bwd(
        proj, sb_tot, sb_first, do_attn, name="sb_bwd",
        rider=_Exchange.join(_scatter_xy([pair[n] for n in REDUCE_MID]), _gather_xy_from(small)))
    from_chips, small = brought[:len(REDUCE_MID)], brought[len(REDUCE_MID):]
    reduced.update({n: (pair[n], parts) for n, parts in zip(REDUCE_MID, from_chips)})
    dproj = jnp.concatenate([dq, dk, dv, du, dgate], axis=1)
    G["w_in"], small = mm(h1, dproj, ta=True, out_cb=W["w_in"].shape[2], name="mm_in_dw", rider=_fill_c(small))
    g_in = _grad_blocks("w_in", G["w_in"])
    dh1, (from_core,) = mm(dproj, W["w_in"], tb=True, name="mm_in_dx", rider=_send_c([g_in]))
    pair_in = _pair_sum(g_in, from_core, core, name="pair_sum_w_in")
    (grad_x, dg_pre), (from_chips,) = _norm_bwd_single(dx1, dh1, x, P["norm_mix_pre"], name="norm_bwd_1",
                                                       rider=_scatter_xy([pair_in]))
    reduced["w_in"] = (pair_in, from_chips)
    last, = _gather_all([dg_pre]).run(name="gather_g_last")
    parts = dict(zip(SMALL_EARLY, small))
    parts["norm_mix_pre"] = last
    return loss, grad_x, parts, reduced


MESH = pl.DeviceIdType.MESH
_HBM = pl.BlockSpec(memory_space=pl.ANY)
N_XY = 4
N_XY_PEERS = 3


def _xy_peers(x, y):
    return [(1 - x, y), (x, 1 - y), (1 - x, 1 - y)]


class _Exchange:
    def __init__(self, arrays, out_shapes, plan, n_copies, alias):
        self.arrays = list(arrays)
        self.out_shapes = list(out_shapes)
        self.plan = plan
        self.n_copies = n_copies
        self.alias = list(alias) if isinstance(alias, (list, tuple)) else [alias] * len(self.arrays)

    @property
    def n(self):
        return len(self.arrays)

    def aliases(self, first_in, first_out):
        return {first_in + k: first_out + k for k in range(self.n) if self.alias[k]}

    @staticmethod
    def join(a, b):
        def plan(k, src, dst, x, y, c):
            return a.plan(k, src, dst, x, y, c) if k < a.n else b.plan(k - a.n, src, dst, x, y, c)

        return _Exchange(a.arrays + b.arrays, a.out_shapes + b.out_shapes, plan, max(a.n_copies, b.n_copies),
                         a.alias + b.alias)

    def sems(self):
        shape = (self.n, self.n_copies)
        return [pltpu.SemaphoreType.DMA(shape), pltpu.SemaphoreType.DMA(shape)]

    def _copies(self, ins, outs, send_sems, recv_sems):
        x, y, c = lax.axis_index("x"), lax.axis_index("y"), lax.axis_index("c")
        sends, lands, own = [], [], []
        for k in range(self.n):
            for j, (src, dst, dev, land) in enumerate(self.plan(k, ins[k], outs[k], x, y, c)):
                if dev is None:
                    own.append(pltpu.make_async_copy(src, dst, send_sems.at[k, j]))
                    continue
                sems = dict(send_sem=send_sems.at[k, j], recv_sem=recv_sems.at[k, j], device_id=dev, device_id_type=MESH)
                sends.append(pltpu.make_async_remote_copy(src_ref=src, dst_ref=dst, **sems))
                lands.append(pltpu.make_async_remote_copy(src_ref=src, dst_ref=land, **sems))
        return sends, lands, own

    def start(self, ins, outs, send_sems, recv_sems):
        sends, _, own = self._copies(ins, outs, send_sems, recv_sems)
        for cp in own + sends:
            cp.start()

    def finish(self, ins, outs, send_sems, recv_sems):
        sends, lands, own = self._copies(ins, outs, send_sems, recv_sems)
        for cp in lands:
            cp.wait_recv()
        for cp in sends:
            cp.wait_send()
        for cp in own:
            cp.wait()

    def run(self, *, name):
        n = self.n

        def body(*refs):
            parts = (refs[:n], refs[n:2 * n], refs[2 * n], refs[2 * n + 1])
            self.start(*parts)
            self.finish(*parts)

        return pl.pallas_call(
            body, name=name, out_shape=tuple(self.out_shapes),
            in_specs=[_HBM] * n, out_specs=tuple([_HBM] * n),
            input_output_aliases=self.aliases(0, 0),
            scratch_shapes=self.sems(),
        )(*self.arrays)


def _call(host_body, *, name, grid, in_specs, out_specs, out_shape, scratch_shapes, operands, rider=None):
    out_specs, out_shape = tuple(out_specs), tuple(out_shape)
    if rider is None:
        res = pl.pallas_call(
            host_body, name=name, grid=grid, in_specs=list(in_specs), out_specs=out_specs, out_shape=out_shape,
            scratch_shapes=list(scratch_shapes), compiler_params=_cparams(*["arbitrary"] * len(grid)),
        )(*operands)
        return tuple(res), None
    n, n_in, n_out, n_scr = rider.n, len(in_specs), len(out_specs), len(scratch_shapes)

    def body(*refs):
        pos = [0]

        def take(count):
            pos[0] += count
            return refs[pos[0] - count:pos[0]]

        h_in, r_in, h_out, r_out, h_scr = take(n_in), take(n), take(n_out), take(n), take(n_scr)
        send_sems, recv_sems = take(2)
        ids = [pl.program_id(a) for a in range(len(grid))]
        first = functools.reduce(jnp.logical_and, [i == 0 for i in ids])
        last = functools.reduce(jnp.logical_and, [i == g - 1 for i, g in zip(ids, grid)])

        @pl.when(first)
        def _():
            rider.start(r_in, r_out, send_sems, recv_sems)

        host_body(*h_in, *h_out, *h_scr)

        @pl.when(last)
        def _():
            rider.finish(r_in, r_out, send_sems, recv_sems)

    res = pl.pallas_call(
        body, name=name, grid=grid,
        in_specs=list(in_specs) + [_HBM] * n, out_specs=out_specs + tuple([_HBM] * n),
        out_shape=out_shape + tuple(rider.out_shapes),
        input_output_aliases=rider.aliases(n_in, n_out),
        scratch_shapes=list(scratch_shapes) + rider.sems(),
        compiler_params=_cparams(*["arbitrary"] * len(grid)),
    )(*operands, *rider.arrays)
    return tuple(res[:n_out]), list(res[n_out:])


def _same(arrays):
    return [jax.ShapeDtypeStruct(a.shape, a.dtype) for a in arrays]


def _fill_xy(bufs):
    def plan(k, src, dst, x, y, c):
        mine = 2 * x + y
        return [(src.at[mine, c], dst.at[mine, c], (px, py, c), dst.at[2 * px + py, c]) for px, py in _xy_peers(x, y)]

    return _Exchange(bufs, _same(bufs), plan, N_XY_PEERS, alias=True)


def _fill_c(bufs):
    def plan(k, src, dst, x, y, c):
        return [(src.at[:, c], dst.at[:, c], (x, y, 1 - c), dst.at[:, 1 - c])]

    return _Exchange(bufs, _same(bufs), plan, 1, alias=True)


def _slots(arrays):
    return [jax.ShapeDtypeStruct((N_XY, 2) + a.shape, a.dtype) for a in arrays]


def _gather_xy_from(srcs):
    def plan(k, src, dst, x, y, c):
        mine = 2 * x + y
        return ([(src, dst.at[mine, c], None, None)]
                + [(src, dst.at[mine, c], (px, py, c), dst.at[2 * px + py, c]) for px, py in _xy_peers(x, y)])

    return _Exchange(srcs, _slots(srcs), plan, 1 + N_XY_PEERS, alias=False)


def _gather_all(srcs):
    def plan(k, src, dst, x, y, c):
        mine = 2 * x + y
        out = [(src, dst.at[mine, c], None, None)]
        for fx, fy, fc in [(a, b, e) for a in (0, 1) for b in (0, 1) for e in (0, 1)][1:]:
            px, py, pc = (1 - x) if fx else x, (1 - y) if fy else y, (1 - c) if fc else c
            out.append((src, dst.at[mine, c], (px, py, pc), dst.at[2 * px + py, pc]))
        return out

    return _Exchange(srcs, _slots(srcs), plan, N_DEV, alias=False)


def _send_c(srcs):
    def plan(k, src, dst, x, y, c):
        return [(src.at[:, 1 - c], dst, (x, y, 1 - c), dst)]

    outs = [jax.ShapeDtypeStruct(a.shape[:1] + a.shape[2:], a.dtype) for a in srcs]
    return _Exchange(srcs, outs, plan, 1, alias=False)


def _scatter_xy(srcs):
    def plan(k, src, dst, x, y, c):
        return [(src.at[2 * px + py], dst.at[j], (px, py, c), dst.at[j]) for j, (px, py) in enumerate(_xy_peers(x, y))]

    outs = [jax.ShapeDtypeStruct((N_XY_PEERS,) + a.shape[1:], a.dtype) for a in srcs]
    return _Exchange(srcs, outs, plan, N_XY_PEERS, alias=False)


WIRE_DTYPE = BF16


def _pair_sum(g8, recv, core, *, name):
    n, _, R, C = g8.shape
    tr = _pick(R, (128, 64, 32, 16, 8))

    def body(core_ref, a_ref, b_ref, o_ref):
        o_ref[...] = (a_ref[0] + b_ref[...]).astype(WIRE_DTYPE)

    return pl.pallas_call(
        body, name=name, out_shape=jax.ShapeDtypeStruct((n, R, C), WIRE_DTYPE),
        grid_spec=pltpu.PrefetchScalarGridSpec(
            num_scalar_prefetch=1, grid=(n, R // tr),
            in_specs=[pl.BlockSpec((1, 1, tr, C), lambda s, i, core_ref: (s, core_ref[0], i, 0)),
                      pl.BlockSpec((1, tr, C), lambda s, i, core_ref: (s, i, 0))],
            out_specs=pl.BlockSpec((1, tr, C), lambda s, i, core_ref: (s, i, 0))),
        compiler_params=_cparams("parallel", "parallel"),
    )(core, g8, recv)


def _adamw_math(w, g, m, v):
    m = ADAM_B1 * m + (1.0 - ADAM_B1) * g
    v = ADAM_B2 * v + (1.0 - ADAM_B2) * (g * g)
    m_hat = m / (1.0 - ADAM_B1 ** ADAM_STEP)
    v_hat = v / (1.0 - ADAM_B2 ** ADAM_STEP)
    delta = -ADAM_LR * (m_hat / (jnp.sqrt(v_hat) + ADAM_EPS) + ADAM_WD * w)
    return delta, m, v


def _reduce_adamw(parts, w, m, v, *, own, own_slot, name):
    n, R, C = parts.shape
    tr = _pick(R, (128, 64, 32, 16, 8))

    def body(_, own_ref, parts_ref, w_ref, m_ref, v_ref, g_ref, d_ref, nm_ref, nv_ref):
        g = own_ref[0].astype(F32)
        for k in range(n):
            g = g + parts_ref[k].astype(F32)
        g_ref[...] = g
        d_ref[...], nm_ref[...], nv_ref[...] = _adamw_math(w_ref[...], g, m_ref[...], v_ref[...])

    out = jax.ShapeDtypeStruct((R, C), F32)
    row = pl.BlockSpec((tr, C), lambda i, s: (i, 0))
    return pl.pallas_call(
        body, name=name, out_shape=(out, out, out, out),
        grid_spec=pltpu.PrefetchScalarGridSpec(
            num_scalar_prefetch=1, grid=(R // tr,),
            in_specs=[pl.BlockSpec((1, tr, C), lambda i, s: (s[0], i, 0)),
                      pl.BlockSpec((n, tr, C), lambda i, s: (0, i, 0)), row, row, row],
            out_specs=(row, row, row, row)),
        compiler_params=_cparams("parallel"),
    )(own_slot, own, parts, w, m, v)


SHARDED = (("w_in", (1024, 4096), 1), ("ssm_w_glu", (512, 512), 0), ("w_branch_attn", (512, 1024), 1),
           ("w_branch_ssm", (512, 1024), 1), ("w_out", (1024, 1024), 0), ("xa_wq", (1024, 1024), 0),
           ("xa_wk", (1024, 1024), 0), ("xa_wv", (1024, 1024), 0), ("xa_wo", (1024, 1024), 0),
           ("ffn_w_up", (1024, 5632), 1), ("ffn_conv_w", (3, 5632), 1), ("ffn_w_down", (2816, 1024), 0))
REPLICATED = (("norm_mix_pre", (1024,)), ("norm_mix_post", (1024,)), ("b_gate", (2048,)), ("ssm_a_re", (32, 64)),
              ("ssm_a_im", (32, 64)), ("ssm_log_dt", (32,)), ("ssm_b_re", (32, 64, 16)), ("ssm_b_im", (32, 64, 16)),
              ("ssm_c_re", (32, 16, 64)), ("ssm_c_im", (32, 16, 64)), ("ssm_d", (512,)), ("ssm_b_glu", (512,)),
              ("norm_xa_pre", (1024,)), ("norm_xa_post", (1024,)), ("norm_mem", (1024,)), ("norm_ffn_pre", (1024,)),
              ("norm_ffn_post", (1024,)), ("ffn_conv_b", (5632,)))
PARAM_ORDER = ("norm_mix_pre", "norm_mix_post", "w_in", "b_gate", "ssm_a_re", "ssm_a_im", "ssm_log_dt", "ssm_b_re",
               "ssm_b_im", "ssm_c_re", "ssm_c_im", "ssm_d", "ssm_w_glu", "ssm_b_glu", "w_branch_attn", "w_branch_ssm",
               "w_out", "norm_xa_pre", "norm_xa_post", "norm_mem", "xa_wq", "xa_wk", "xa_wv", "xa_wo", "norm_ffn_pre",
               "norm_ffn_post", "ffn_w_up", "ffn_conv_w", "ffn_conv_b", "ffn_w_down")
FF_LOCAL = 2 * D_FF // N_DEV
FF_LOCAL_PAD = 768
FF_PAD = (N_DEV // 2) * FF_LOCAL_PAD


def _local_shape(shape, axis):
    return tuple(s // N_DEV if a == axis else s for a, s in enumerate(shape))


def _pad_cols(a, width):
    return jnp.pad(a, [(0, 0)] * (a.ndim - 1) + [(0, width - a.shape[-1])])


def _blocks_to_cols(a8):
    return a8.transpose(1, 0, 2).reshape(a8.shape[1], N_DEV * a8.shape[2])


def _cols_to_blocks(a, cb):
    return a.reshape(a.shape[0], N_DEV, cb).transpose(1, 0, 2)


FF_PADDED = ("ffn_w_up", "ffn_conv_w")
LATE = tuple(n for n, _, _ in SHARDED if n != "w_in")
REDUCE_FFN = ("ffn_w_up", "ffn_conv_w", "ffn_w_down")
REDUCE_MID = ("xa_wo", "xa_wq", "xa_wk", "xa_wv", "w_out", "w_branch_attn", "w_branch_ssm", "ssm_w_glu")
SHARD_AXIS = {n: ax for n, _, ax in SHARDED}
FULL_SHAPE = {n: s for n, s, _ in SHARDED}


def _as_local(n, a):
    return _pad_cols(a, FF_LOCAL_PAD) if n in FF_PADDED else a


def _weights_from_wire(wire):
    full = {n: b.reshape((N_DEV,) + b.shape[2:]) for n, b in wire.items()}
    W = {n: a.reshape(FULL_SHAPE[n]) if SHARD_AXIS[n] == 0 else a for n, a in full.items()}
    for n in ("w_branch_attn", "w_branch_ssm", "ffn_conv_w"):
        W[n] = _blocks_to_cols(full[n])
    W["ffn_w_down"] = jnp.pad(W["ffn_w_down"].reshape(N_DEV // 2, FF_LOCAL, D_MODEL),
                              ((0, 0), (0, FF_LOCAL_PAD - FF_LOCAL), (0, 0))).reshape(FF_PAD, D_MODEL)
    return W


def _grad_blocks(n, g):
    if n in ("w_branch_attn", "w_branch_ssm"):
        g = _cols_to_blocks(g, D_MODEL // N_DEV)
    elif n == "ffn_conv_w":
        g = _cols_to_blocks(g, FF_LOCAL_PAD)
    elif n == "ffn_w_down":
        g = g.reshape(N_DEV // 2, FF_LOCAL_PAD, D_MODEL)[:, :FF_LOCAL]
    local = _local_shape(FULL_SHAPE[n], SHARD_AXIS[n])
    if n in FF_PADDED:
        local = local[:-1] + (FF_LOCAL_PAD,)
    return g.reshape((N_XY, 2) + local)


SMALL_SHAPE = {n: (1, s[0]) if len(s) == 1 else (s[0], math.prod(s[1:])) for n, s in REPLICATED}
SMALL_SHAPE["ffn_conv_b"] = (N_DEV, FF_LOCAL)
SMALL_EARLY = tuple(n for n, _ in REPLICATED if n != "norm_mix_pre")


def _adamw_replicated(parts, w, m, v, *, name):
    n = len(parts)

    def body(*refs):
        p_refs, w_refs, m_refs, v_refs = (refs[i * n:(i + 1) * n] for i in range(4))
        outs = refs[4 * n:]
        for k in range(n):
            g = p_refs[k][0, 0]
            for s in range(1, N_DEV):
                g = g + p_refs[k][s // 2, s % 2]
            d, nm, nv = _adamw_math(w_refs[k][...], g, m_refs[k][...], v_refs[k][...])
            for slot, val in enumerate((g, d, nm, nv)):
                outs[slot * n + k][...] = val

    vmem = pl.BlockSpec(memory_space=pltpu.VMEM)
    shapes = [jax.ShapeDtypeStruct(a.shape, F32) for a in w] * 4
    res = pl.pallas_call(
        body, name=name, out_shape=tuple(shapes), in_specs=[vmem] * (4 * n), out_specs=tuple([vmem] * (4 * n)),
        compiler_params=pltpu.CompilerParams(vmem_limit_bytes=VMEM_LIMIT),
    )(*parts, *w, *m, *v)
    return [list(res[i * n:(i + 1) * n]) for i in range(4)]


def kernel(x, mem, norm_mix_pre, norm_mix_post, w_in, b_gate, ssm_a_re, ssm_a_im, ssm_log_dt, ssm_b_re, ssm_b_im, ssm_c_re, ssm_c_im, ssm_d, ssm_w_glu, ssm_b_glu, w_branch_attn, w_branch_ssm, w_out, norm_xa_pre, norm_xa_post, norm_mem, xa_wq, xa_wk, xa_wv, xa_wo, norm_ffn_pre, norm_ffn_post, ffn_w_up, ffn_conv_w, ffn_conv_b, ffn_w_down, loss_target, m_norm_mix_pre, m_norm_mix_post, m_w_in, m_b_gate, m_ssm_a_re, m_ssm_a_im, m_ssm_log_dt, m_ssm_b_re, m_ssm_b_im, m_ssm_c_re, m_ssm_c_im, m_ssm_d, m_ssm_w_glu, m_ssm_b_glu, m_w_branch_attn, m_w_branch_ssm, m_w_out, m_norm_xa_pre, m_norm_xa_post, m_norm_mem, m_xa_wq, m_xa_wk, m_xa_wv, m_xa_wo, m_norm_ffn_pre, m_norm_ffn_post, m_ffn_w_up, m_ffn_conv_w, m_ffn_conv_b, m_ffn_w_down, v_norm_mix_pre, v_norm_mix_post, v_w_in, v_b_gate, v_ssm_a_re, v_ssm_a_im, v_ssm_log_dt, v_ssm_b_re, v_ssm_b_im, v_ssm_c_re, v_ssm_c_im, v_ssm_d, v_ssm_w_glu, v_ssm_b_glu, v_w_branch_attn, v_w_branch_ssm, v_w_out, v_norm_xa_pre, v_norm_xa_post, v_norm_mem, v_xa_wq, v_xa_wk, v_xa_wv, v_xa_wo, v_norm_ffn_pre, v_norm_ffn_post, v_ffn_w_up, v_ffn_conv_w, v_ffn_conv_b, v_ffn_w_down):
    args = dict(locals())
    w_loc = {n: args[n][0] for n in PARAM_ORDER}
    m_loc = {n: args["m_" + n][0] for n in PARAM_ORDER}
    v_loc = {n: args["v_" + n][0] for n in PARAM_ORDER}
    core_i = lax.axis_index("c")
    chip_i = 2 * lax.axis_index("x") + lax.axis_index("y")
    core = core_i.astype(jnp.int32).reshape(1)
    chip = chip_i.astype(jnp.int32).reshape(1)

    def in_place(a):
        buf = lax.empty((N_XY, 2) + a.shape, a.dtype)
        return lax.dynamic_update_slice(buf, a[None, None], (chip_i, core_i) + (0,) * a.ndim)

    as_wire = lambda n: in_place(_as_local(n, w_loc[n]).astype(F32 if n == "ffn_conv_w" else BF16))

    P = {}
    for n, shape in REPLICATED:
        P[n] = w_loc[n] if len(shape) > 1 or n == "ssm_log_dt" else w_loc[n].reshape(1, -1)
    P["ffn_conv_b"] = _pad_cols(w_loc["ffn_conv_b"].reshape(N_DEV, FF_LOCAL), FF_LOCAL_PAD).reshape(1, 2 * FF_PAD)

    loss, grad_x, small_parts, reduced = _local_step(x[0], mem[0], loss_target[0], as_wire("w_in"),
                                                     [as_wire(n) for n in LATE], P, core)
    loss = lax.psum(loss[0, 0], ("x", "y", "c"))

    big_out = {}
    for n, (own, parts) in reduced.items():
        res = _reduce_adamw(parts, _as_local(n, w_loc[n]), _as_local(n, m_loc[n]), _as_local(n, v_loc[n]),
                            own=own, own_slot=chip, name="adamw_" + n)
        big_out[n] = [r[:, :FF_LOCAL] if n in FF_PADDED else r for r in res]

    names = [n for n, _ in REPLICATED]
    as_small = lambda d: [d[n].reshape(SMALL_SHAPE[n]) for n in names]
    small_out = _adamw_replicated([small_parts[n] for n in names], as_small(w_loc), as_small(m_loc), as_small(v_loc),
                                  name="adamw_replicated")
    small_out = [dict(zip(names, res)) for res in small_out]

    outs = [loss, grad_x[None]]
    for k in range(4):
        for n in PARAM_ORDER:
            src = big_out[n][k] if n in big_out else small_out[k][n]
            outs.append(src.reshape(args[n].shape))
    return tuple(outs)
```

```python
import functools
import math

import jax
import jax.numpy as jnp
from jax import lax
from jax.experimental import pallas as pl
from jax.experimental.pallas import tpu as pltpu

F32 = jnp.float32
BF16 = jnp.bfloat16

D_MODEL = 1024
SB_HEADS = 8
SB_HEAD_DIM = 64
SB_WIDTH = 512
SSM_WIDTH = 512
SSM_GROUP = 16
SSM_GROUPS = 32
SSM_STATE = 64
XA_HEADS = 4
XA_HEAD_DIM = 256
D_FF = 2816
RMS_EPS = 1e-6
IN_WIDTH = 4096
N_DEV = 8

ADAM_LR = 0.001
ADAM_B1 = 0.9
ADAM_B2 = 0.999
ADAM_EPS = 1e-08
ADAM_WD = 0.01
ADAM_STEP = 10

LANES = 128
SUBLANES = 8
VMEM_LIMIT = 48 * 1024 * 1024

_GELU_C = math.sqrt(2.0 / math.pi)


def _cparams(*sem):
    return pltpu.CompilerParams(dimension_semantics=sem, vmem_limit_bytes=VMEM_LIMIT)


def _pick(n, cands):
    for c in cands:
        if n % c == 0:
            return c
    return n


def _gelu(x):
    return 0.5 * x * (1.0 + jnp.tanh(_GELU_C * (x + 0.044715 * x * x * x)))


def _gelu_and_grad(x):
    t = jnp.tanh(_GELU_C * (x + 0.044715 * x * x * x))
    g = 0.5 * x * (1.0 + t)
    dg = 0.5 * (1.0 + t) + 0.5 * x * (1.0 - t * t) * _GELU_C * (1.0 + 3.0 * 0.044715 * x * x)
    return g, dg


def _sigmoid(x):
    return 1.0 / (1.0 + jnp.exp(-x))


def _dot(a, b, ca, cb):
    return lax.dot_general(a.astype(BF16), b.astype(BF16), (((ca,), (cb,)), ((), ())),
                           preferred_element_type=F32)


MM_TILES = (1024, 768, 512, 256, 128)
MM_K_TILES = (2048, 1536) + MM_TILES


def _matmul(a, b, *, ta=False, tb=False, out_dtype=F32, name, b_block0=0, n_blocks=None,
            out_cb=None, out_into=None, out_block0=0, acc_in=None, rider=None):
    if ta:
        K, M = a.shape
    else:
        M, K = a.shape
    b_cb = None
    if b.ndim == 3:
        b_cb = b.shape[2]
        n_blocks = b.shape[0] - b_block0 if n_blocks is None else n_blocks
        N, K2 = (b.shape[1], n_blocks * b_cb) if tb else (n_blocks * b_cb, b.shape[1])
    elif tb:
        N, K2 = b.shape
    else:
        K2, N = b.shape
    assert K == K2, (a.shape, b.shape, ta, tb)
    if out_into is not None:
        out_cb = out_into.shape[2]
    tm = _pick(M, MM_TILES)
    n_unit = math.gcd(N, math.gcd(b_cb if (b_cb and not tb) else N, out_cb or N))
    tn = _pick(n_unit, MM_TILES)
    k_unit = b_cb if (b_cb and tb) else K
    tk = _pick(k_unit, MM_K_TILES)
    nk = K // tk
    ca, cb = (0 if ta else 1), (1 if tb else 0)
    has_acc = acc_in is not None
    has_into = out_into is not None

    def body(*refs):
        a_ref, b_ref = refs[0], refs[1]
        pos = 2
        c_ref = None
        if has_acc:
            c_ref = refs[pos]
            pos += 1
        if has_into:
            pos += 1
        o_ref = refs[pos]
        p = _dot(a_ref[...], b_ref[...], ca, cb)
        if nk == 1:
            o_ref[...] = ((p + c_ref[...]) if has_acc else p).astype(out_dtype)
        else:
            acc_ref = refs[pos + 1]
            k = pl.program_id(2)

            @pl.when(k == 0)
            def _():
                acc_ref[...] = (p + c_ref[...]) if has_acc else p

            @pl.when(k > 0)
            def _():
                acc_ref[...] += p

            @pl.when(k == nk - 1)
            def _():
                o_ref[...] = acc_ref[...].astype(out_dtype)

    nj, ni = N // tn, M // tm
    a_bytes, b_bytes = a.size * a.dtype.itemsize, K * N * b.dtype.itemsize
    n_outer = a_bytes * nj + b_bytes * (1 if nk == 1 else ni) <= a_bytes * (1 if nk == 1 else nj) + b_bytes * ni
    grid = (nj, ni, nk) if n_outer else (ni, nj, nk)

    def spec(block, index):
        return pl.BlockSpec(block, (lambda g0, g1, k: index(g0, g1, k)) if n_outer else (lambda g0, g1, k: index(g1, g0, k)))

    a_spec = spec((tk, tm), lambda j, i, k: (k, i)) if ta else spec((tm, tk), lambda j, i, k: (i, k))
    if b_cb is None:
        b_spec = spec((tn, tk), lambda j, i, k: (j, k)) if tb else spec((tk, tn), lambda j, i, k: (k, j))
    elif tb:
        per = b_cb // tk
        b_spec = spec((None, tn, tk), lambda j, i, k: (b_block0 + k // per, j, k % per))
    else:
        per = b_cb // tn
        b_spec = spec((None, tk, tn), lambda j, i, k: (b_block0 + j // per, k, j % per))
    in_specs = [a_spec, b_spec]
    operands = [a, b]
    aliases = {}
    if has_acc:
        in_specs.append(spec((tm, tn), lambda j, i, k: (i, j)))
        operands.append(acc_in)
    if has_into:
        aliases = {len(operands): 0}
        in_specs.append(pl.BlockSpec(memory_space=pl.ANY))
        operands.append(out_into)
    if out_cb is None:
        out_shape = jax.ShapeDtypeStruct((M, N), out_dtype)
        out_spec = spec((tm, tn), lambda j, i, k: (i, j))
    else:
        per_o = out_cb // tn
        out_shape = (jax.ShapeDtypeStruct(out_into.shape, out_into.dtype) if has_into
                     else jax.ShapeDtypeStruct((N // out_cb, M, out_cb), out_dtype))
        out_spec = spec((None, tm, tn), lambda j, i, k: (out_block0 + j // per_o, i, j % per_o))
    if rider is not None:
        assert not has_into
        (out,), brought = _call(body, name=name, rider=rider, grid=grid, in_specs=in_specs,
                                out_specs=(out_spec,), out_shape=(out_shape,), operands=operands,
                                scratch_shapes=[] if nk == 1 else [pltpu.VMEM((tm, tn), F32)])
        return out, brought
    return pl.pallas_call(
        body, name=name, out_shape=out_shape,
        grid=grid,
        in_specs=in_specs, out_specs=out_spec, input_output_aliases=aliases,
        scratch_shapes=[] if nk == 1 else [pltpu.VMEM((tm, tn), F32)],
        compiler_params=_cparams("parallel", "parallel", "arbitrary"),
    )(*operands)


def _rms(x, g):
    r = lax.rsqrt(jnp.mean(x * x, axis=-1, keepdims=True) + RMS_EPS)
    return x * r * g


def _rms_bwd(dy, x, g):
    r = lax.rsqrt(jnp.mean(x * x, axis=-1, keepdims=True) + RMS_EPS)
    xh = x * r
    dxh = dy * g
    dx = r * (dxh - xh * jnp.mean(dxh * xh, axis=-1, keepdims=True))
    dg = jnp.sum(dy * xh, axis=0, keepdims=True)
    return dx, dg


def _row_tile(rows):
    return _pick(rows, (512, 256, 128, 64, 32, 16, 8))


def _rms_fwd(x, g, *, name, rider=None):
    R, D = x.shape
    tr = _row_tile(R)

    def body(x_ref, g_ref, h_ref):
        h_ref[...] = _rms(x_ref[...], g_ref[...]).astype(BF16)

    (h,), brought = _call(
        body, name=name, rider=rider, out_shape=(jax.ShapeDtypeStruct((R, D), BF16),), grid=(R // tr,),
        in_specs=[pl.BlockSpec((tr, D), lambda i: (i, 0)), pl.BlockSpec((1, D), lambda i: (0, 0))],
        out_specs=(pl.BlockSpec((tr, D), lambda i: (i, 0)),), scratch_shapes=[], operands=(x, g))
    return h if rider is None else (h, brought)


def _resnorm_norm(x, z, g_post, g_next, *, name):
    R, D = x.shape
    tr = _row_tile(R)

    def body(x_ref, z_ref, gp_ref, gn_ref, xn_ref, h_ref):
        xn = x_ref[...] + _rms(z_ref[...], gp_ref[...])
        xn_ref[...] = xn
        h_ref[...] = _rms(xn, gn_ref[...]).astype(BF16)

    row = pl.BlockSpec((tr, D), lambda i: (i, 0))
    vec = pl.BlockSpec((1, D), lambda i: (0, 0))
    return pl.pallas_call(
        body, name=name,
        out_shape=(jax.ShapeDtypeStruct((R, D), F32), jax.ShapeDtypeStruct((R, D), BF16)),
        grid=(R // tr,), in_specs=[row, row, vec, vec], out_specs=(row, row),
        compiler_params=_cparams("parallel"),
    )(x, z, g_post, g_next)


def _final_loss(x, z, g_post, target, *, name):
    R, D = x.shape
    tr = _row_tile(R)

    def body(x_ref, z_ref, gp_ref, t_ref, loss_ref, dy_ref, dz_ref, dg_ref):
        i = pl.program_id(0)
        z = z_ref[...]
        g = gp_ref[...]
        err = x_ref[...] + _rms(z, g) - t_ref[...]
        dy = err * (1.0 / D)
        dy_ref[...] = dy
        dz, dg = _rms_bwd(dy, z, g)
        dz_ref[...] = dz.astype(BF16)
        part = 0.5 * jnp.sum(jnp.sum(err * err, axis=-1, keepdims=True) * (1.0 / D), axis=0, keepdims=True)

        @pl.when(i == 0)
        def _():
            loss_ref[...] = part
            dg_ref[...] = dg

        @pl.when(i > 0)
        def _():
            loss_ref[...] += part
            dg_ref[...] += dg

    row = pl.BlockSpec((tr, D), lambda i: (i, 0))
    vec = pl.BlockSpec((1, D), lambda i: (0, 0))
    return pl.pallas_call(
        body, name=name,
        out_shape=(jax.ShapeDtypeStruct((1, 1), F32), jax.ShapeDtypeStruct((R, D), F32),
                   jax.ShapeDtypeStruct((R, D), BF16), jax.ShapeDtypeStruct((1, D), F32)),
        grid=(R // tr,), in_specs=[row, row, vec, row],
        out_specs=(pl.BlockSpec((1, 1), lambda i: (0, 0)), row, row, vec),
        compiler_params=_cparams("arbitrary"),
    )(x, z, g_post, target)


def _norm_bwd_pair(dres, dh, xk, g_pre, zprev, g_prev_post, *, name, rider=None):
    R, D = xk.shape
    tr = _row_tile(R)

    def body(dres_ref, dh_ref, x_ref, gpre_ref, z_ref, gpost_ref, dx_ref, dz_ref, dgpre_ref, dgpost_ref):
        i = pl.program_id(0)
        d1, dgpre = _rms_bwd(dh_ref[...], x_ref[...], gpre_ref[...])
        dx = dres_ref[...] + d1
        dx_ref[...] = dx
        dz, dgpost = _rms_bwd(dx, z_ref[...], gpost_ref[...])
        dz_ref[...] = dz.astype(BF16)

        @pl.when(i == 0)
        def _():
            dgpre_ref[...] = dgpre
            dgpost_ref[...] = dgpost

        @pl.when(i > 0)
        def _():
            dgpre_ref[...] += dgpre
            dgpost_ref[...] += dgpost

    row = pl.BlockSpec((tr, D), lambda i: (i, 0))
    vec = pl.BlockSpec((1, D), lambda i: (0, 0))
    return _call(
        body, name=name, rider=rider,
        out_shape=(jax.ShapeDtypeStruct((R, D), F32), jax.ShapeDtypeStruct((R, D), BF16),
                   jax.ShapeDtypeStruct((1, D), F32), jax.ShapeDtypeStruct((1, D), F32)),
        grid=(R // tr,), in_specs=[row, row, row, vec, row, vec], out_specs=(row, row, vec, vec),
        scratch_shapes=[], operands=(dres, dh, xk, g_pre, zprev, g_prev_post))


def _norm_bwd_single(dres, dh, xk, g_pre, *, name, rider=None):
    R, D = xk.shape
    tr = _row_tile(R)
    has_res = dres is not None

    def body(*refs):
        if has_res:
            dres_ref, dh_ref, x_ref, gpre_ref, dx_ref, dgpre_ref = refs
        else:
            dh_ref, x_ref, gpre_ref, dx_ref, dgpre_ref = refs
        i = pl.program_id(0)
        d1, dgpre = _rms_bwd(dh_ref[...], x_ref[...], gpre_ref[...])
        dx_ref[...] = dres_ref[...] + d1 if has_res else d1

        @pl.when(i == 0)
        def _():
            dgpre_ref[...] = dgpre

        @pl.when(i > 0)
        def _():
            dgpre_ref[...] += dgpre

    row = pl.BlockSpec((tr, D), lambda i: (i, 0))
    vec = pl.BlockSpec((1, D), lambda i: (0, 0))
    ins = ([dres] if has_res else []) + [dh, xk, g_pre]
    res, brought = _call(
        body, name=name, rider=rider,
        out_shape=(jax.ShapeDtypeStruct((R, D), F32), jax.ShapeDtypeStruct((1, D), F32)),
        grid=(R // tr,), in_specs=([row] if has_res else []) + [row, row, vec], out_specs=(row, vec),
        scratch_shapes=[], operands=ins)
    return res if rider is None else (res, brought)


SB_BLOCK = 256
SB_QBLOCK = 512
SB_DEAD = -104.0


def _sb_tri(kind):
    r = lax.broadcasted_iota(jnp.int32, (SB_BLOCK, SB_BLOCK), 0)
    c = lax.broadcasted_iota(jnp.int32, (SB_BLOCK, SB_BLOCK), 1)
    keep = {"after": r > c, "before": r < c}[kind]
    return jnp.where(keep, 1.0, 0.0).astype(BF16)


def _sb_scores(qm, k_blk):
    z = _dot(qm, k_blk, 1, 1)
    sp = jnp.maximum(z, 0.0) + jnp.log(1.0 + jnp.exp(-jnp.abs(z)))
    return z, sp


def _sb_causal(rows):
    r = lax.broadcasted_iota(jnp.int32, (rows, SB_BLOCK), 0)
    c = lax.broadcasted_iota(jnp.int32, (rows, SB_BLOCK), 1)
    return c < r


def _head_masks():
    lane = lax.broadcasted_iota(jnp.int32, (1, LANES), 1)
    return [jnp.where(lane < SB_HEAD_DIM, 1.0, 0.0), jnp.where(lane >= SB_HEAD_DIM, 1.0, 0.0)]


def _sb_fwd(proj, *, name, rider=None):
    S = proj.shape[0]
    T = SB_BLOCK
    TQ = min(SB_QBLOCK, S)
    span = TQ // T
    nq = S // TQ
    npair = SB_WIDTH // LANES
    scale = SB_HEAD_DIM ** -0.5

    def body(q_ref, k_ref, v_ref, o_ref, tot_ref, first_ref, acc_ref, run_ref):
        masks = _head_masks()
        tri = _sb_tri("after")
        first_ref[...] = jnp.zeros_like(first_ref)
        slot = lax.broadcasted_iota(jnp.int32, first_ref.shape, 1)

        def alive():
            reach = jnp.maximum(jnp.max(run_ref[0]), jnp.max(run_ref[1]))
            return (reach > SB_DEAD).astype(jnp.int32)

        def q_block(i, _):
            qrow = pl.ds(pl.multiple_of(i * TQ, TQ), TQ)
            q = q_ref[qrow, :] * scale
            qm = [(q * m).astype(BF16) for m in masks]
            acc_ref[...] = jnp.zeros_like(acc_ref)
            run_ref[...] = jnp.zeros_like(run_ref)

            def k_block(j, own):
                krow = pl.ds(pl.multiple_of(j * T, T), T)
                k_blk = k_ref[krow, :].astype(BF16)
                v_blk = v_ref[krow, :].astype(BF16)
                r0 = 0 if own is None else own * T
                rows = pl.ds(r0, TQ - r0)
                for h in range(2):
                    z, sp = _sb_scores(qm[h][r0:], k_blk)
                    causal = None if own is None else _sb_causal(TQ - r0)
                    lf = -sp if causal is None else jnp.where(causal, -sp, 0.0)
                    e = jnp.exp(z - sp + _dot(lf, tri, 1, 0) + run_ref[h, rows])
                    w = e if causal is None else jnp.where(causal, e, 0.0)
                    acc_ref[h, rows] += _dot(w, v_blk, 1, 0)
                    run_ref[h, rows] += jnp.sum(lf, axis=1, keepdims=True)

            for d in reversed(range(span)):
                k_block(i * span + d, d)

            def below(carry):
                jj, _ = carry
                k_block(i * span - 1 - jj, None)
                return jj + 1, alive()

            done, _ = lax.while_loop(lambda c: jnp.logical_and(c[0] < i * span, c[1] > 0), below, (jnp.int32(0), alive()))
            o_ref[qrow, :] = (acc_ref[0] * masks[0] + acc_ref[1] * masks[1]).astype(BF16)
            tot_ref[qrow, :] = run_ref[0] * masks[0] + run_ref[1] * masks[1]
            first_ref[...] = jnp.where(slot == i, (i * span - done).astype(F32), first_ref[...])
            return 0

        lax.fori_loop(0, nq, q_block, 0)

    blk = lambda off: pl.BlockSpec((S, LANES), lambda p: (0, off + p))
    return _call(
        body, name=name, rider=rider,
        out_shape=(jax.ShapeDtypeStruct((S, SB_WIDTH), BF16), jax.ShapeDtypeStruct((S, SB_WIDTH), F32),
                   jax.ShapeDtypeStruct((npair, SUBLANES, LANES), F32)),
        grid=(npair,),
        in_specs=[blk(0), blk(npair), blk(2 * npair)],
        out_specs=(blk(0), blk(0), pl.BlockSpec((1, SUBLANES, LANES), lambda p: (p, 0, 0))),
        scratch_shapes=[pltpu.VMEM((2, TQ, LANES), F32), pltpu.VMEM((2, TQ, 1), F32)],
        operands=(proj, proj, proj))


def _sb_bwd(proj, tot, first, do_attn, *, name, rider=None):
    S = proj.shape[0]
    T = SB_BLOCK
    TQ = min(SB_QBLOCK, S)
    span = TQ // T
    nq = S // TQ
    npair = SB_WIDTH // LANES
    scale = SB_HEAD_DIM ** -0.5

    def body(q_ref, k_ref, v_ref, tot_ref, first_ref, do_ref, dq_ref, dk_ref, dv_ref,
             dqacc_ref, dkacc_ref, dvacc_ref, run_ref, grun_ref):
        masks = _head_masks()
        tri_after = _sb_tri("after")
        tri_before = _sb_tri("before")
        dkacc_ref[...] = jnp.zeros_like(dkacc_ref)
        dvacc_ref[...] = jnp.zeros_like(dvacc_ref)
        slot = lax.broadcasted_iota(jnp.int32, first_ref.shape, 1)

        def q_block(i, _):
            qrow = pl.ds(pl.multiple_of(i * TQ, TQ), TQ)
            q = q_ref[qrow, :] * scale
            do = do_ref[qrow, :].astype(F32)
            tot = tot_ref[qrow, :]
            qm = [(q * m).astype(BF16) for m in masks]
            dom = [(do * m).astype(BF16) for m in masks]
            ltot = [jnp.sum(tot * m, axis=1, keepdims=True) * (1.0 / SB_HEAD_DIM) for m in masks]
            dqacc_ref[...] = jnp.zeros_like(dqacc_ref)
            run_ref[...] = jnp.zeros_like(run_ref)
            grun_ref[...] = jnp.zeros_like(grun_ref)

            def k_block(j, own):
                krow = pl.ds(pl.multiple_of(j * T, T), T)
                k_blk = k_ref[krow, :].astype(BF16)
                v_blk = v_ref[krow, :].astype(BF16)
                r0 = 0 if own is None else own * T
                rows = pl.ds(r0, TQ - r0)
                for h in range(2):
                    z, sp = _sb_scores(qm[h][r0:], k_blk)
                    causal = None if own is None else _sb_causal(TQ - r0)
                    lf = -sp if causal is None else jnp.where(causal, -sp, 0.0)
                    lsum = jnp.sum(lf, axis=1, keepdims=True)
                    later = (ltot[h][r0:] - run_ref[h, rows] - lsum) + _dot(lf, tri_after, 1, 0)
                    beta = jnp.exp(z - sp)
                    w = jnp.exp(z - sp + later)
                    if causal is not None:
                        w = jnp.where(causal, w, 0.0)
                    g = _dot(dom[h][r0:], v_blk, 1, 1) * w
                    gbefore = grun_ref[h, rows] + _dot(g, tri_before, 1, 0)
                    dz = g - beta * (g + gbefore)
                    if causal is not None:
                        dz = jnp.where(causal, dz, 0.0)
                    dz = dz.astype(BF16)
                    dqacc_ref[h, rows] += _dot(dz, k_blk, 1, 0)
                    dkacc_ref[krow, :] += _dot(dz, qm[h][r0:], 0, 0)
                    dvacc_ref[krow, :] += _dot(w, dom[h][r0:], 0, 0)
                    run_ref[h, rows] += lsum
                    grun_ref[h, rows] += jnp.sum(g, axis=1, keepdims=True)

            def above(j, _):
                k_block(j, None)
                return 0

            first = jnp.max(jnp.where(slot == i, first_ref[...], 0.0)).astype(jnp.int32)
            lax.fori_loop(jnp.clip(first, 0, i * span), i * span, above, 0)
            for d in range(span):
                k_block(i * span + d, d)
            dq_ref[qrow, :] = ((dqacc_ref[0] * masks[0] + dqacc_ref[1] * masks[1]) * scale).astype(BF16)
            return 0

        lax.fori_loop(0, nq, q_block, 0)
        dk_ref[...] = dkacc_ref[...].astype(BF16)
        dv_ref[...] = dvacc_ref[...].astype(BF16)

    blk = lambda off: pl.BlockSpec((S, LANES), lambda p: (0, off + p))
    out = jax.ShapeDtypeStruct((S, SB_WIDTH), BF16)
    return _call(
        body, name=name, rider=rider, out_shape=(out, out, out), grid=(npair,),
        in_specs=[blk(0), blk(npair), blk(2 * npair), blk(0), pl.BlockSpec((1, SUBLANES, LANES), lambda p: (p, 0, 0)),
                  blk(0)],
        out_specs=(blk(0), blk(0), blk(0)),
        scratch_shapes=[pltpu.VMEM((2, TQ, LANES), F32), pltpu.VMEM((S, LANES), F32), pltpu.VMEM((S, LANES), F32),
                        pltpu.VMEM((2, TQ, 1), F32), pltpu.VMEM((2, TQ, 1), F32)],
        operands=(proj, proj, proj, tot, first, do_attn))


SSM_HALVES = 2
SSM_HALF_CH = SSM_WIDTH // SSM_HALVES
SSM_HALF_ST = SSM_GROUPS * SSM_STATE // SSM_HALVES
SSM_CHUNK = 512


def _cmul(ar, ai, br, bi):
    return ar * br - ai * bi, ar * bi + ai * br


def _ssm_tables(lam_re, lam_im):
    lr = lam_re.reshape(-1)
    li = lam_im.reshape(-1)
    pows = [(jnp.ones_like(lr), jnp.zeros_like(li)), (lr, li)]
    for _ in range(2, SUBLANES + 1):
        pows.append(_cmul(pows[-1][0], pows[-1][1], lr, li))
    row = jnp.arange(SUBLANES)[:, None]

    def shift_tab(d, keep):
        return [jnp.where(keep, pows[d][0][None, :], 0.0), jnp.where(keep, pows[d][1][None, :], 0.0)]

    fwd, bwd = [], []
    for d in (1, 2, 4):
        fwd += shift_tab(d, row >= d)
        bwd += shift_tab(d, row + d < SUBLANES)
    fwd += [jnp.stack([pows[r + 1][0] for r in range(SUBLANES)]), jnp.stack([pows[r + 1][1] for r in range(SUBLANES)])]
    bwd += [jnp.stack([pows[SUBLANES - r][0] for r in range(SUBLANES)]),
            jnp.stack([pows[SUBLANES - r][1] for r in range(SUBLANES)])]

    def halves(tabs):
        t = jnp.stack(tabs)
        return t.reshape(8, SUBLANES, SSM_HALVES, SSM_HALF_ST).transpose(2, 0, 1, 3)

    return halves(fwd), halves(bwd)


def _ssm_fwd(proj, bd_re, bd_im, cd_re, cd_imneg, d_skip, tab, *, name, rider=None):
    S = proj.shape[0]
    Tc = min(SSM_CHUNK, S)
    nc = S // Tc
    u_blk0 = (3 * SB_WIDTH) // SSM_HALF_CH

    def body(u_ref, bre_ref, bim_ref, cre_ref, cim_ref, d_ref, tab_ref, y_ref, xre_ref, xim_ref, cre_s, cim_s):
        c = pl.program_id(1)

        @pl.when(c == 0)
        def _():
            cre_s[...] = jnp.zeros_like(cre_s)
            cim_s[...] = jnp.zeros_like(cim_s)

        u = u_ref[...]
        ub = u.astype(BF16)
        xre_ref[...] = _dot(ub, bre_ref[0], 1, 0)
        xim_ref[...] = _dot(ub, bim_ref[0], 1, 0)

        def slab(k, carry):
            car_re, car_im = carry
            rows = pl.ds(pl.multiple_of(k * SUBLANES, SUBLANES), SUBLANES)
            sre = xre_ref[rows, :]
            sim = xim_ref[rows, :]
            for n, d in enumerate((1, 2, 4)):
                pre, pim = tab_ref[0, 2 * n], tab_ref[0, 2 * n + 1]
                rre = pltpu.roll(sre, d, 0)
                rim = pltpu.roll(sim, d, 0)
                sre, sim = sre + (pre * rre - pim * rim), sim + (pre * rim + pim * rre)
            pre, pim = tab_ref[0, 6], tab_ref[0, 7]
            sre, sim = sre + (pre * car_re - pim * car_im), sim + (pre * car_im + pim * car_re)
            xre_ref[rows, :] = sre
            xim_ref[rows, :] = sim
            last = (SUBLANES - 1, SUBLANES)
            return (jnp.broadcast_to(sre[last[0]:last[1], :], sre.shape),
                    jnp.broadcast_to(sim[last[0]:last[1], :], sim.shape))

        car = lax.fori_loop(0, Tc // SUBLANES, slab, (cre_s[...], cim_s[...]))
        cre_s[...] = car[0]
        cim_s[...] = car[1]
        y = _dot(xre_ref[...], cre_ref[0], 1, 0) + _dot(xim_ref[...], cim_ref[0], 1, 0)
        y_ref[...] = y + d_ref[...] * u

    return _call(
        body, name=name, rider=rider,
        out_shape=(jax.ShapeDtypeStruct((S, SSM_WIDTH), F32),
                   jax.ShapeDtypeStruct((S, SSM_HALVES * SSM_HALF_ST), F32),
                   jax.ShapeDtypeStruct((S, SSM_HALVES * SSM_HALF_ST), F32)),
        grid=(SSM_HALVES, nc),
        in_specs=[pl.BlockSpec((Tc, SSM_HALF_CH), lambda h, c: (c, u_blk0 + h)),
                  pl.BlockSpec((1, SSM_HALF_CH, SSM_HALF_ST), lambda h, c: (h, 0, 0)),
                  pl.BlockSpec((1, SSM_HALF_CH, SSM_HALF_ST), lambda h, c: (h, 0, 0)),
                  pl.BlockSpec((1, SSM_HALF_ST, SSM_HALF_CH), lambda h, c: (h, 0, 0)),
                  pl.BlockSpec((1, SSM_HALF_ST, SSM_HALF_CH), lambda h, c: (h, 0, 0)),
                  pl.BlockSpec((1, SSM_HALF_CH), lambda h, c: (0, h)),
                  pl.BlockSpec((1, 8, SUBLANES, SSM_HALF_ST), lambda h, c: (h, 0, 0, 0))],
        out_specs=(pl.BlockSpec((Tc, SSM_HALF_CH), lambda h, c: (c, h)),
                   pl.BlockSpec((Tc, SSM_HALF_ST), lambda h, c: (c, h)),
                   pl.BlockSpec((Tc, SSM_HALF_ST), lambda h, c: (c, h))),
        scratch_shapes=[pltpu.VMEM((SUBLANES, SSM_HALF_ST), F32), pltpu.VMEM((SUBLANES, SSM_HALF_ST), F32)],
        operands=(proj, bd_re, bd_im, cd_re, cd_imneg, d_skip, tab))


def _ssm_bwd(dy, proj, x_re, x_im, bd_re, bd_im, cd_re, cd_imneg, d_skip, tab, *, name, rider=None):
    S = proj.shape[0]
    Tc = min(SSM_CHUNK, S)
    nc = S // Tc
    u_blk0 = (3 * SB_WIDTH) // SSM_HALF_CH

    def body(dy_ref, u_ref, xre_ref, xim_ref, bre_ref, bim_ref, cre_ref, cim_ref, d_ref, tab_ref,
             du_ref, dbre_ref, dbim_ref, dcre_ref, dcim_ref, dd_ref, dlre_ref, dlim_ref,
             gre_s, gim_s, cre_s, cim_s):
        c = pl.program_id(1)

        @pl.when(c == 0)
        def _():
            cre_s[...] = jnp.zeros_like(cre_s)
            cim_s[...] = jnp.zeros_like(cim_s)
            dbre_ref[...] = jnp.zeros_like(dbre_ref)
            dbim_ref[...] = jnp.zeros_like(dbim_ref)
            dcre_ref[...] = jnp.zeros_like(dcre_ref)
            dcim_ref[...] = jnp.zeros_like(dcim_ref)
            dd_ref[...] = jnp.zeros_like(dd_ref)
            dlre_ref[...] = jnp.zeros_like(dlre_ref)
            dlim_ref[...] = jnp.zeros_like(dlim_ref)

        dy = dy_ref[...]
        dyb = dy.astype(BF16)
        u = u_ref[...]
        gre_s[...] = _dot(dyb, cre_ref[0], 1, 1)
        gim_s[...] = _dot(dyb, cim_ref[0], 1, 1)
        row = lax.broadcasted_iota(jnp.int32, (SUBLANES, SSM_HALF_ST), 0)
        nslab = Tc // SUBLANES

        def slab(kk, carry):
            car_re, car_im, acc_re, acc_im = carry
            k = nslab - 1 - kk
            rows = pl.ds(pl.multiple_of(k * SUBLANES, SUBLANES), SUBLANES)
            sre = gre_s[rows, :]
            sim = gim_s[rows, :]
            for n, d in enumerate((1, 2, 4)):
                pre, pim = tab_ref[0, 2 * n], tab_ref[0, 2 * n + 1]
                rre = pltpu.roll(sre, SUBLANES - d, 0)
                rim = pltpu.roll(sim, SUBLANES - d, 0)
                sre, sim = sre + (pre * rre + pim * rim), sim + (pre * rim - pim * rre)
            pre, pim = tab_ref[0, 6], tab_ref[0, 7]
            sre, sim = sre + (pre * car_re + pim * car_im), sim + (pre * car_im - pim * car_re)
            gre_s[rows, :] = sre
            gim_s[rows, :] = sim
            nre = jnp.where(row == SUBLANES - 1, car_re, pltpu.roll(sre, SUBLANES - 1, 0))
            nim = jnp.where(row == SUBLANES - 1, car_im, pltpu.roll(sim, SUBLANES - 1, 0))
            xr = xre_ref[rows, :]
            xi = xim_ref[rows, :]
            acc_re = acc_re + (nre * xr + nim * xi)
            acc_im = acc_im + (nim * xr - nre * xi)
            return (jnp.broadcast_to(sre[0:1, :], sre.shape), jnp.broadcast_to(sim[0:1, :], sim.shape), acc_re, acc_im)

        car = lax.fori_loop(0, nslab, slab, (cre_s[...], cim_s[...], dlre_ref[0], dlim_ref[0]))
        cre_s[...] = car[0]
        cim_s[...] = car[1]
        dlre_ref[0] = car[2]
        dlim_ref[0] = car[3]
        gre = gre_s[...].astype(BF16)
        gim = gim_s[...].astype(BF16)
        ub = u.astype(BF16)
        du = _dot(gre, bre_ref[0], 1, 1) + _dot(gim, bim_ref[0], 1, 1) + d_ref[...] * dy
        du_ref[...] = du.astype(BF16)
        dbre_ref[0] += _dot(ub, gre, 0, 0)
        dbim_ref[0] += _dot(ub, gim, 0, 0)
        dcre_ref[0] += _dot(xre_ref[...], dyb, 0, 0)
        dcim_ref[0] += _dot(xim_ref[...], dyb, 0, 0)
        dd_ref[...] += jnp.sum(dy * u, axis=0, keepdims=True)

    rev = lambda c: nc - 1 - c
    return _call(
        body, name=name, rider=rider,
        out_shape=(jax.ShapeDtypeStruct((S, SSM_WIDTH), BF16),
                   jax.ShapeDtypeStruct((SSM_HALVES, SSM_HALF_CH, SSM_HALF_ST), F32),
                   jax.ShapeDtypeStruct((SSM_HALVES, SSM_HALF_CH, SSM_HALF_ST), F32),
                   jax.ShapeDtypeStruct((SSM_HALVES, SSM_HALF_ST, SSM_HALF_CH), F32),
                   jax.ShapeDtypeStruct((SSM_HALVES, SSM_HALF_ST, SSM_HALF_CH), F32),
                   jax.ShapeDtypeStruct((1, SSM_WIDTH), F32),
                   jax.ShapeDtypeStruct((SSM_HALVES, SUBLANES, SSM_HALF_ST), F32),
                   jax.ShapeDtypeStruct((SSM_HALVES, SUBLANES, SSM_HALF_ST), F32)),
        grid=(SSM_HALVES, nc),
        in_specs=[pl.BlockSpec((Tc, SSM_HALF_CH), lambda h, c: (rev(c), h)),
                  pl.BlockSpec((Tc, SSM_HALF_CH), lambda h, c: (rev(c), u_blk0 + h)),
                  pl.BlockSpec((Tc, SSM_HALF_ST), lambda h, c: (rev(c), h)),
                  pl.BlockSpec((Tc, SSM_HALF_ST), lambda h, c: (rev(c), h)),
                  pl.BlockSpec((1, SSM_HALF_CH, SSM_HALF_ST), lambda h, c: (h, 0, 0)),
                  pl.BlockSpec((1, SSM_HALF_CH, SSM_HALF_ST), lambda h, c: (h, 0, 0)),
                  pl.BlockSpec((1, SSM_HALF_ST, SSM_HALF_CH), lambda h, c: (h, 0, 0)),
                  pl.BlockSpec((1, SSM_HALF_ST, SSM_HALF_CH), lambda h, c: (h, 0, 0)),
                  pl.BlockSpec((1, SSM_HALF_CH), lambda h, c: (0, h)),
                  pl.BlockSpec((1, 8, SUBLANES, SSM_HALF_ST), lambda h, c: (h, 0, 0, 0))],
        out_specs=(pl.BlockSpec((Tc, SSM_HALF_CH), lambda h, c: (rev(c), h)),
                   pl.BlockSpec((1, SSM_HALF_CH, SSM_HALF_ST), lambda h, c: (h, 0, 0)),
                   pl.BlockSpec((1, SSM_HALF_CH, SSM_HALF_ST), lambda h, c: (h, 0, 0)),
                   pl.BlockSpec((1, SSM_HALF_ST, SSM_HALF_CH), lambda h, c: (h, 0, 0)),
                   pl.BlockSpec((1, SSM_HALF_ST, SSM_HALF_CH), lambda h, c: (h, 0, 0)),
                   pl.BlockSpec((1, SSM_HALF_CH), lambda h, c: (0, h)),
                   pl.BlockSpec((1, SUBLANES, SSM_HALF_ST), lambda h, c: (h, 0, 0)),
                   pl.BlockSpec((1, SUBLANES, SSM_HALF_ST), lambda h, c: (h, 0, 0))),
        scratch_shapes=[pltpu.VMEM((Tc, SSM_HALF_ST), F32), pltpu.VMEM((Tc, SSM_HALF_ST), F32),
                        pltpu.VMEM((SUBLANES, SSM_HALF_ST), F32), pltpu.VMEM((SUBLANES, SSM_HALF_ST), F32)],
        operands=(dy, proj, x_re, x_im, bd_re, bd_im, cd_re, cd_imneg, d_skip, tab))


def _ssm_prepare(a_re, a_im, log_dt, b_re, b_im):
    dt = jnp.exp(log_dt)[:, None]
    mag = jnp.exp(a_re * dt)
    lre = mag * jnp.cos(a_im * dt)
    lim = mag * jnp.sin(a_im * dt)
    den = a_re * a_re + a_im * a_im
    fre = ((lre - 1.0) * a_re + lim * a_im) / den
    fim = (lim * a_re - (lre - 1.0) * a_im) / den
    bbre = fre[:, :, None] * b_re - fim[:, :, None] * b_im
    bbim = fre[:, :, None] * b_im + fim[:, :, None] * b_re
    return lre, lim, bbre, bbim


def _group_eye():
    return jnp.eye(SSM_GROUPS // SSM_HALVES, dtype=F32)


def _bd_from_bbar(bbar):
    gh = SSM_GROUPS // SSM_HALVES
    b = bbar.reshape(SSM_HALVES, gh, SSM_STATE, SSM_GROUP).transpose(0, 1, 3, 2)
    out = b[:, :, :, None, :] * _group_eye()[None, :, None, :, None]
    return out.reshape(SSM_HALVES, SSM_HALF_CH, SSM_HALF_ST)


def _bbar_from_bd(dbd):
    gh = SSM_GROUPS // SSM_HALVES
    d = dbd.reshape(SSM_HALVES, gh, SSM_GROUP, gh, SSM_STATE)
    d = jnp.sum(d * _group_eye()[None, :, None, :, None], axis=3)
    return d.transpose(0, 1, 3, 2).reshape(SSM_GROUPS, SSM_STATE, SSM_GROUP)


def _cd_from_c(cmat):
    gh = SSM_GROUPS // SSM_HALVES
    c = cmat.reshape(SSM_HALVES, gh, SSM_GROUP, SSM_STATE).transpose(0, 1, 3, 2)
    out = c[:, :, :, None, :] * _group_eye()[None, :, None, :, None]
    return out.reshape(SSM_HALVES, SSM_HALF_ST, SSM_HALF_CH)


def _c_from_cd(dcd):
    gh = SSM_GROUPS // SSM_HALVES
    d = dcd.reshape(SSM_HALVES, gh, SSM_STATE, gh, SSM_GROUP)
    d = jnp.sum(d * _group_eye()[None, :, None, :, None], axis=3)
    return d.transpose(0, 1, 3, 2).reshape(SSM_GROUPS, SSM_GROUP, SSM_STATE)


def _glu_fwd(y_pre, w_glu, b_glu, *, name):
    S, W = y_pre.shape
    tr = _row_tile(S)

    def body(y_ref, w_ref, b_ref, o_ref):
        yg = _gelu(y_ref[...])
        gl = _dot(yg, w_ref[...], 1, 0) + b_ref[...]
        o_ref[...] = (yg * _sigmoid(gl)).astype(BF16)

    row = pl.BlockSpec((tr, W), lambda i: (i, 0))
    return pl.pallas_call(
        body, name=name, out_shape=jax.ShapeDtypeStruct((S, W), BF16), grid=(S // tr,),
        in_specs=[row, pl.BlockSpec((W, W), lambda i: (0, 0)), pl.BlockSpec((1, W), lambda i: (0, 0))],
        out_specs=row, compiler_params=_cparams("parallel"),
    )(y_pre, w_glu, b_glu)


def _glu_bwd(y_pre, do, w_glu, b_glu, *, name):
    S, W = y_pre.shape
    tr = _row_tile(S)

    def body(y_ref, do_ref, w_ref, b_ref, dy_ref, dw_ref, db_ref):
        i = pl.program_id(0)
        yg, dyg_dy = _gelu_and_grad(y_ref[...])
        ygb = yg.astype(BF16)
        sg = _sigmoid(_dot(ygb, w_ref[...], 1, 0) + b_ref[...])
        do = do_ref[...]
        dgl = do * yg * sg * (1.0 - sg)
        dglb = dgl.astype(BF16)
        dyg = do * sg + _dot(dglb, w_ref[...], 1, 1)
        dy_ref[...] = dyg * dyg_dy
        dw = _dot(ygb, dglb, 0, 0)
        db = jnp.sum(dgl, axis=0, keepdims=True)

        @pl.when(i == 0)
        def _():
            dw_ref[...] = dw
            db_ref[...] = db

        @pl.when(i > 0)
        def _():
            dw_ref[...] += dw
            db_ref[...] += db

    row = pl.BlockSpec((tr, W), lambda i: (i, 0))
    full = pl.BlockSpec((W, W), lambda i: (0, 0))
    vec = pl.BlockSpec((1, W), lambda i: (0, 0))
    return pl.pallas_call(
        body, name=name,
        out_shape=(jax.ShapeDtypeStruct((S, W), F32), jax.ShapeDtypeStruct((W, W), F32), jax.ShapeDtypeStruct((1, W), F32)),
        grid=(S // tr,), in_specs=[row, row, full, vec], out_specs=(row, full, vec),
        compiler_params=_cparams("arbitrary"),
    )(y_pre, do, w_glu, b_glu)


GATE_COL0 = 3 * SB_WIDTH + SSM_WIDTH


def _merge_fwd(proj, o_attn, o_ssm, w_ba, w_bs, b_gate, *, name):
    S = proj.shape[0]
    D = D_MODEL
    tr = _pick(S, (256, 128, 64, 32, 16, 8))
    gb = GATE_COL0 // D

    def body(ga_ref, gs_ref, oa_ref, os_ref, wa_ref, ws_ref, ba_ref, bs_ref, m_ref):
        pa = _dot(oa_ref[...], wa_ref[...], 1, 0)
        ps = _dot(os_ref[...], ws_ref[...], 1, 0)
        sa = _sigmoid(ga_ref[...] + ba_ref[...])
        ss = _sigmoid(gs_ref[...] + bs_ref[...])
        m_ref[...] = (sa * pa + ss * ps).astype(BF16)

    return pl.pallas_call(
        body, name=name, out_shape=jax.ShapeDtypeStruct((S, D), BF16), grid=(S // tr,),
        in_specs=[pl.BlockSpec((tr, D), lambda i: (i, gb)), pl.BlockSpec((tr, D), lambda i: (i, gb + 1)),
                  pl.BlockSpec((tr, SB_WIDTH), lambda i: (i, 0)), pl.BlockSpec((tr, SSM_WIDTH), lambda i: (i, 0)),
                  pl.BlockSpec((SB_WIDTH, D), lambda i: (0, 0)), pl.BlockSpec((SSM_WIDTH, D), lambda i: (0, 0)),
                  pl.BlockSpec((1, D), lambda i: (0, 0)), pl.BlockSpec((1, D), lambda i: (0, 1))],
        out_specs=pl.BlockSpec((tr, D), lambda i: (i, 0)),
        compiler_params=_cparams("parallel"),
    )(proj, proj, o_attn, o_ssm, w_ba, w_bs, b_gate, b_gate)


def _merge_bwd(dmerged, proj, o_attn, o_ssm, w_ba, w_bs, b_gate, *, name):
    S = proj.shape[0]
    D = D_MODEL
    tr = _pick(S, (256, 128, 64, 32, 16, 8))
    gb = GATE_COL0 // D

    def body(dm_ref, ga_ref, gs_ref, oa_ref, os_ref, wa_ref, ws_ref, ba_ref, bs_ref,
             doa_ref, dos_ref, dg_ref, db_ref, dwa_ref, dws_ref):
        i = pl.program_id(0)
        dm = dm_ref[...]
        oa = oa_ref[...]
        osm = os_ref[...]
        pa = _dot(oa, wa_ref[...], 1, 0)
        ps = _dot(osm, ws_ref[...], 1, 0)
        sa = _sigmoid(ga_ref[...] + ba_ref[...])
        ss = _sigmoid(gs_ref[...] + bs_ref[...])
        dpa = (dm * sa).astype(BF16)
        dps = (dm * ss).astype(BF16)
        dga = dm * pa * sa * (1.0 - sa)
        dgs = dm * ps * ss * (1.0 - ss)
        dg_ref[:, :D] = dga.astype(BF16)
        dg_ref[:, D:] = dgs.astype(BF16)
        doa_ref[...] = _dot(dpa, wa_ref[...], 1, 1).astype(BF16)
        dos_ref[...] = _dot(dps, ws_ref[...], 1, 1)
        dwa = _dot(oa, dpa, 0, 0)
        dws = _dot(osm, dps, 0, 0)
        dba = jnp.sum(dga, axis=0, keepdims=True)
        dbs = jnp.sum(dgs, axis=0, keepdims=True)

        @pl.when(i == 0)
        def _():
            dwa_ref[...] = dwa
            dws_ref[...] = dws
            db_ref[:, :D] = dba
            db_ref[:, D:] = dbs

        @pl.when(i > 0)
        def _():
            dwa_ref[...] += dwa
            dws_ref[...] += dws
            db_ref[:, :D] += dba
            db_ref[:, D:] += dbs

    rowD = pl.BlockSpec((tr, D), lambda i: (i, 0))
    wspec = pl.BlockSpec((SB_WIDTH, D), lambda i: (0, 0))
    return pl.pallas_call(
        body, name=name,
        out_shape=(jax.ShapeDtypeStruct((S, SB_WIDTH), BF16), jax.ShapeDtypeStruct((S, SSM_WIDTH), F32),
                   jax.ShapeDtypeStruct((S, 2 * D), BF16), jax.ShapeDtypeStruct((1, 2 * D), F32),
                   jax.ShapeDtypeStruct((SB_WIDTH, D), F32), jax.ShapeDtypeStruct((SSM_WIDTH, D), F32)),
        grid=(S // tr,),
        in_specs=[rowD, pl.BlockSpec((tr, D), lambda i: (i, gb)), pl.BlockSpec((tr, D), lambda i: (i, gb + 1)),
                  pl.BlockSpec((tr, SB_WIDTH), lambda i: (i, 0)), pl.BlockSpec((tr, SSM_WIDTH), lambda i: (i, 0)),
                  wspec, wspec, pl.BlockSpec((1, D), lambda i: (0, 0)), pl.BlockSpec((1, D), lambda i: (0, 1))],
        out_specs=(pl.BlockSpec((tr, SB_WIDTH), lambda i: (i, 0)), pl.BlockSpec((tr, SSM_WIDTH), lambda i: (i, 0)),
                   pl.BlockSpec((tr, 2 * D), lambda i: (i, 0)), pl.BlockSpec((1, 2 * D), lambda i: (0, 0)),
                   wspec, wspec),
        compiler_params=_cparams("arbitrary"),
    )(dmerged, proj, proj, o_attn, o_ssm, w_ba, w_bs, b_gate, b_gate)


def _xattn_probs(q, k, h):
    cols = slice(h * XA_HEAD_DIM, (h + 1) * XA_HEAD_DIM)
    s = _dot(q[:, cols], k[:, cols], 1, 1) * (XA_HEAD_DIM ** -0.5)
    s = s - jnp.max(s, axis=-1, keepdims=True)
    e = jnp.exp(s)
    return e / jnp.sum(e, axis=-1, keepdims=True), cols


def _xattn_fwd(q2, k2, v2, *, name):
    S, D = q2.shape
    M = k2.shape[0]
    tr = _row_tile(S)

    def body(q_ref, k_ref, v_ref, o_ref):
        q = q_ref[...]
        k = k_ref[...]
        v = v_ref[...]
        for h in range(XA_HEADS):
            p, cols = _xattn_probs(q, k, h)
            o_ref[:, cols] = _dot(p, v[:, cols], 1, 0).astype(BF16)

    row = pl.BlockSpec((tr, D), lambda i: (i, 0))
    memb = pl.BlockSpec((M, D), lambda i: (0, 0))
    return pl.pallas_call(
        body, name=name, out_shape=jax.ShapeDtypeStruct((S, D), BF16), grid=(S // tr,),
        in_specs=[row, memb, memb], out_specs=row, compiler_params=_cparams("parallel"),
    )(q2, k2, v2)


def _xattn_bwd(q2, k2, v2, do2, *, name):
    S, D = q2.shape
    M = k2.shape[0]
    tr = _row_tile(S)
    scale = XA_HEAD_DIM ** -0.5

    def body(q_ref, k_ref, v_ref, do_ref, dq_ref, dk_ref, dv_ref):
        i = pl.program_id(0)

        @pl.when(i == 0)
        def _():
            dk_ref[...] = jnp.zeros_like(dk_ref)
            dv_ref[...] = jnp.zeros_like(dv_ref)

        q = q_ref[...]
        k = k_ref[...]
        v = v_ref[...]
        do = do_ref[...]
        for h in range(XA_HEADS):
            p, cols = _xattn_probs(q, k, h)
            dp = _dot(do[:, cols], v[:, cols], 1, 1)
            ds = (p * (dp - jnp.sum(dp * p, axis=-1, keepdims=True)) * scale).astype(BF16)
            dq_ref[:, cols] = _dot(ds, k[:, cols], 1, 0).astype(BF16)
            dk_ref[:, cols] += _dot(ds, q[:, cols], 0, 0)
            dv_ref[:, cols] += _dot(p, do[:, cols], 0, 0)

    row = pl.BlockSpec((tr, D), lambda i: (i, 0))
    memb = pl.BlockSpec((M, D), lambda i: (0, 0))
    return pl.pallas_call(
        body, name=name,
        out_shape=(jax.ShapeDtypeStruct((S, D), BF16), jax.ShapeDtypeStruct((M, D), F32), jax.ShapeDtypeStruct((M, D), F32)),
        grid=(S // tr,), in_specs=[row, memb, memb, row], out_specs=(row, memb, memb),
        compiler_params=_cparams("arbitrary"),
    )(q2, k2, v2, do2)


CONV_ROWS = 64
CONV_ROWS_FWD = 256


def _chunk(ref, c, rows):
    return ref[pl.ds(pl.multiple_of(c * rows, rows), rows), :]


def _rows_before(ref, c, rows):
    t0 = pl.multiple_of(jnp.maximum(c * rows - SUBLANES, 0), SUBLANES)
    return jnp.where(c > 0, ref[pl.ds(t0, SUBLANES), :], 0.0)


def _rows_after(ref, c, rows, n_chunks):
    t0 = pl.multiple_of(jnp.minimum((c + 1) * rows, n_chunks * rows - SUBLANES), SUBLANES)
    return jnp.where(c < n_chunks - 1, ref[pl.ds(t0, SUBLANES), :], 0.0)


def _shift_down(cur, before, d):
    out = pltpu.roll(cur, d, 0)
    r = lax.broadcasted_iota(jnp.int32, cur.shape, 0)
    for e in range(d):
        out = jnp.where(r == e, before[SUBLANES - d + e:SUBLANES - d + e + 1, :], out)
    return out


def _shift_up(cur, after, d):
    rows = cur.shape[0]
    out = pltpu.roll(cur, rows - d, 0)
    r = lax.broadcasted_iota(jnp.int32, cur.shape, 0)
    for e in range(d):
        out = jnp.where(r == rows - d + e, after[e:e + 1, :], out)
    return out


def _conv3(cur, before, w_ref, b_ref):
    return (w_ref[2:3, :] * cur + w_ref[1:2, :] * _shift_down(cur, before, 1)
            + w_ref[0:1, :] * _shift_down(cur, before, 2) + b_ref[...])


def _convgate_fwd(up_g, up_v, conv_w, conv_b, *, name):
    S, H = up_g.shape
    nb = H // LANES
    R = min(CONV_ROWS_FWD, S)
    n_chunks = S // R

    def body(g_ref, v_ref, wg_ref, wv_ref, bg_ref, bv_ref, a_ref):
        def chunk(c, _):
            cg = _conv3(_chunk(g_ref, c, R), _rows_before(g_ref, c, R), wg_ref, bg_ref)
            cv = _conv3(_chunk(v_ref, c, R), _rows_before(v_ref, c, R), wv_ref, bv_ref)
            a_ref[pl.ds(pl.multiple_of(c * R, R), R), :] = (_gelu(cg) * cv).astype(BF16)
            return 0

        lax.fori_loop(0, n_chunks, chunk, 0)

    col = lambda off: pl.BlockSpec((S, LANES), lambda j: (0, off + j))
    wcol = lambda off: pl.BlockSpec((3, LANES), lambda j: (0, off + j))
    bcol = lambda off: pl.BlockSpec((1, LANES), lambda j: (0, off + j))
    return pl.pallas_call(
        body, name=name, out_shape=jax.ShapeDtypeStruct((S, H), BF16), grid=(nb,),
        in_specs=[col(0), col(0), wcol(0), wcol(nb), bcol(0), bcol(nb)],
        out_specs=col(0), compiler_params=_cparams("parallel"),
    )(up_g, up_v, conv_w, conv_w, conv_b, conv_b)


def _convgate_bwd(up_g, up_v, da, conv_w, conv_b, *, name):
    S, H = up_g.shape
    nb = H // LANES
    R = min(CONV_ROWS, S)
    n_chunks = S // R

    def fold(a):
        return sum(a[r:r + SUBLANES] for r in range(0, a.shape[0], SUBLANES))

    def body(g_ref, v_ref, da_ref, wg_ref, wv_ref, bg_ref, bv_ref,
             dug_ref, duv_ref, dwg_ref, dwv_ref, dbg_ref, dbv_ref, dcg_s, dcv_s):
        def first_pass(c, acc):
            rows = pl.ds(pl.multiple_of(c * R, R), R)
            ug, uv = _chunk(g_ref, c, R), _chunk(v_ref, c, R)
            bg, bv = _rows_before(g_ref, c, R), _rows_before(v_ref, c, R)
            cg = _conv3(ug, bg, wg_ref, bg_ref)
            cv = _conv3(uv, bv, wv_ref, bv_ref)
            da = da_ref[rows, :]
            gl, dgl = _gelu_and_grad(cg)
            dcg = da * cv * dgl
            dcv = da * gl
            dcg_s[rows, :] = dcg
            dcv_s[rows, :] = dcv
            new = []
            for dc, u, before in ((dcg, ug, bg), (dcv, uv, bv)):
                new += [fold(dc * _shift_down(u, before, 2)), fold(dc * _shift_down(u, before, 1)), fold(dc * u), fold(dc)]
            return tuple(a + n for a, n in zip(acc, new))

        zero = jnp.zeros((SUBLANES, LANES), F32)
        acc = lax.fori_loop(0, n_chunks, first_pass, (zero,) * 8)
        total = [jnp.sum(a, axis=0, keepdims=True) for a in acc]
        for k, (dw_ref, db_ref) in enumerate(((dwg_ref, dbg_ref), (dwv_ref, dbv_ref))):
            dw_ref[0:1, :] = total[4 * k]
            dw_ref[1:2, :] = total[4 * k + 1]
            dw_ref[2:3, :] = total[4 * k + 2]
            db_ref[...] = total[4 * k + 3]

        def second_pass(c, _):
            rows = pl.ds(pl.multiple_of(c * R, R), R)
            for dc_s, w_ref, du_ref in ((dcg_s, wg_ref, dug_ref), (dcv_s, wv_ref, duv_ref)):
                cur, after = _chunk(dc_s, c, R), _rows_after(dc_s, c, R, n_chunks)
                du = w_ref[2:3, :] * cur + w_ref[1:2, :] * _shift_up(cur, after, 1) + w_ref[0:1, :] * _shift_up(cur, after, 2)
                du_ref[rows, :] = du.astype(BF16)
            return 0

        lax.fori_loop(0, n_chunks, second_pass, 0)

    col = lambda off: pl.BlockSpec((S, LANES), lambda j: (0, off + j))
    wcol = lambda off: pl.BlockSpec((3, LANES), lambda j: (0, off + j))
    bcol = lambda off: pl.BlockSpec((1, LANES), lambda j: (0, off + j))
    return pl.pallas_call(
        body, name=name,
        out_shape=(jax.ShapeDtypeStruct((S, H), BF16), jax.ShapeDtypeStruct((S, H), BF16),
                   jax.ShapeDtypeStruct((3, H), F32), jax.ShapeDtypeStruct((3, H), F32),
                   jax.ShapeDtypeStruct((1, H), F32), jax.ShapeDtypeStruct((1, H), F32)),
        grid=(nb,),
        in_specs=[col(0), col(0), col(0), wcol(0), wcol(nb), bcol(0), bcol(nb)],
        out_specs=(col(0), col(0), wcol(0), wcol(0), bcol(0), bcol(0)),
        scratch_shapes=[pltpu.VMEM((S, LANES), F32), pltpu.VMEM((S, LANES), F32)],
        compiler_params=_cparams("parallel"),
    )(up_g, up_v, da, conv_w, conv_w, conv_b, conv_b)


def _local_step(x, mem, target, w_in, late_wire, P, core):
    mm = _matmul
    h1, (w_in,) = _rms_fwd(x, P["norm_mix_pre"], name="rms_mix_pre", rider=_fill_xy([w_in]))
    w_in, = _fill_c([w_in]).run(name="gather_in_c")
    w_in = w_in.reshape((N_DEV,) + w_in.shape[2:])
    n_mid = len(LATE) - len(REDUCE_FFN)
    proj, wire_mid = mm(h1, w_in, name="mm_in", rider=_fill_xy(late_wire[:n_mid]))
    (o_attn, sb_tot, sb_first), wires = _sb_fwd(
        proj, name="sb_fwd", rider=_Exchange.join(_fill_c(wire_mid), _fill_xy(late_wire[n_mid:])))
    wire_mid, wire_ffn = wires[:n_mid], wires[n_mid:]

    ssm_prep = lambda *a: _ssm_prepare(*a)
    (lam_re, lam_im, bb_re, bb_im), prep_vjp = jax.vjp(
        ssm_prep, P["ssm_a_re"], P["ssm_a_im"], P["ssm_log_dt"], P["ssm_b_re"], P["ssm_b_im"])
    tab_f, tab_b = _ssm_tables(lam_re, lam_im)
    bd_re = _bd_from_bbar(bb_re).astype(BF16)
    bd_im = _bd_from_bbar(bb_im).astype(BF16)
    cd_re = _cd_from_c(P["ssm_c_re"]).astype(BF16)
    cd_imneg = _cd_from_c(-P["ssm_c_im"]).astype(BF16)
    (y_pre, x_re, x_im), wire_ffn = _ssm_fwd(proj, bd_re, bd_im, cd_re, cd_imneg, P["ssm_d"], tab_f,
                                             name="ssm_fwd", rider=_fill_c(wire_ffn))
    W = _weights_from_wire(dict(zip(LATE, list(wire_mid) + list(wire_ffn))))
    W["w_in"] = w_in
    o_ssm = _glu_fwd(y_pre, W["ssm_w_glu"], P["ssm_b_glu"], name="glu_fwd")

    merged = _merge_fwd(proj, o_attn, o_ssm, W["w_branch_attn"], W["w_branch_ssm"], P["b_gate"], name="merge_fwd")
    mo = mm(merged, W["w_out"], name="mm_out")
    x1, h2 = _resnorm_norm(x, mo, P["norm_mix_post"], P["norm_xa_pre"], name="resnorm_1")

    mem_n = _rms_fwd(mem, P["norm_mem"], name="rms_mem")
    q2 = mm(h2, W["xa_wq"], out_dtype=BF16, name="mm_xq")
    k2 = mm(mem_n, W["xa_wk"], out_dtype=BF16, name="mm_xk")
    v2 = mm(mem_n, W["xa_wv"], out_dtype=BF16, name="mm_xv")
    o2 = _xattn_fwd(q2, k2, v2, name="xattn_fwd")
    xa = mm(o2, W["xa_wo"], name="mm_xo")
    x2, h3 = _resnorm_norm(x1, xa, P["norm_xa_post"], P["norm_ffn_pre"], name="resnorm_2")

    half = N_DEV // 2
    up_g = mm(h3, W["ffn_w_up"], n_blocks=half, name="mm_up_g")
    up_v = mm(h3, W["ffn_w_up"], b_block0=half, name="mm_up_v")
    act = _convgate_fwd(up_g, up_v, W["ffn_conv_w"], P["ffn_conv_b"], name="convgate_fwd")
    f = mm(act, W["ffn_w_down"], name="mm_down")
    loss, dy, df, dg_ffn_post = _final_loss(x2, f, P["norm_ffn_post"], target, name="final_loss")

    G = {"norm_ffn_post": dg_ffn_post}
    dact = mm(df, W["ffn_w_down"], tb=True, out_dtype=BF16, name="mm_down_dx")
    G["ffn_w_down"] = mm(act, df, ta=True, name="mm_down_dw")
    dug, duv, dwg, dwv, dbg, dbv = _convgate_bwd(up_g, up_v, dact, W["ffn_conv_w"], P["ffn_conv_b"], name="convgate_bwd")
    G["ffn_conv_w"] = jnp.concatenate([dwg, dwv], axis=1)
    G["ffn_conv_b"] = jnp.concatenate([dbg, dbv], axis=1)
    dh3 = mm(dug, W["ffn_w_up"], tb=True, n_blocks=half, name="mm_up_g_dx")
    dh3 = mm(duv, W["ffn_w_up"], tb=True, b_block0=half, acc_in=dh3, out_dtype=BF16, name="mm_up_v_dx")
    dw_up = mm(h3, dug, ta=True, out_into=lax.empty(W["ffn_w_up"].shape, F32), name="mm_up_g_dw")
    G["ffn_w_up"] = mm(h3, duv, ta=True, out_into=dw_up, out_block0=half, name="mm_up_v_dw")
    blocks = {n: _grad_blocks(n, G[n]) for n in REDUCE_FFN}
    (dx2, dxa, G["norm_ffn_pre"], G["norm_xa_post"]), from_core = _norm_bwd_pair(
        dy, dh3, x2, P["norm_ffn_pre"], xa, P["norm_xa_post"], name="norm_bwd_3",
        rider=_send_c([blocks[n] for n in REDUCE_FFN]))
    pair = {n: _pair_sum(blocks[n], r, core, name="pair_sum_" + n) for n, r in zip(REDUCE_FFN, from_core)}

    G["xa_wo"] = mm(o2, dxa, ta=True, name="mm_xo_dw")
    do2 = mm(dxa, W["xa_wo"], tb=True, out_dtype=BF16, name="mm_xo_dx")
    dq2, dk2, dv2 = _xattn_bwd(q2, k2, v2, do2, name="xattn_bwd")
    G["xa_wq"] = mm(h2, dq2, ta=True, name="mm_xq_dw")
    dh2 = mm(dq2, W["xa_wq"], tb=True, out_dtype=BF16, name="mm_xq_dx")
    G["xa_wk"] = mm(mem_n, dk2, ta=True, name="mm_xk_dw")
    G["xa_wv"] = mm(mem_n, dv2, ta=True, name="mm_xv_dw")
    dmem_n = jnp.concatenate([dk2, dv2], axis=1)
    wkv = jnp.concatenate([W["xa_wk"], W["xa_wv"]], axis=1)
    dmem = mm(dmem_n, wkv, tb=True, name="mm_xkv_dx")
    _, G["norm_mem"] = _norm_bwd_single(None, dmem, mem, P["norm_mem"], name="norm_bwd_mem")
    (dx1, dmo, G["norm_xa_pre"], G["norm_mix_post"]), _ = _norm_bwd_pair(
        dx2, dh2, x1, P["norm_xa_pre"], mo, P["norm_mix_post"], name="norm_bwd_2")

    G["w_out"] = mm(merged, dmo, ta=True, name="mm_out_dw")
    dmerged = mm(dmo, W["w_out"], tb=True, out_dtype=BF16, name="mm_out_dx")
    do_attn, do_ssm, dgate, G["b_gate"], G["w_branch_attn"], G["w_branch_ssm"] = _merge_bwd(
        dmerged, proj, o_attn, o_ssm, W["w_branch_attn"], W["w_branch_ssm"], P["b_gate"], name="merge_bwd")
    dy_pre, G["ssm_w_glu"], G["ssm_b_glu"] = _glu_bwd(y_pre, do_ssm, W["ssm_w_glu"], P["ssm_b_glu"], name="glu_bwd")
    blocks.update({n: _grad_blocks(n, G[n]) for n in REDUCE_MID})
    (du, dbd_re, dbd_im, dcd_re, dcd_imneg, G["ssm_d"], dl_re, dl_im), brought = _ssm_bwd(
        dy_pre, proj, x_re, x_im, bd_re, bd_im, cd_re, cd_imneg, P["ssm_d"], tab_b, name="ssm_bwd",
        rider=_Exchange.join(_send_c([blocks[n] for n in REDUCE_MID]), _scatter_xy([pair[n] for n in REDUCE_FFN])))
    from_core, from_chips = brought[:len(REDUCE_MID)], brought[len(REDUCE_MID):]
    reduced = {n: (pair[n], parts) for n, parts in zip(REDUCE_FFN, from_chips)}
    pair.update({n: _pair_sum(blocks[n], r, core, name="pair_sum_" + n) for n, r in zip(REDUCE_MID, from_core)})
    G["ssm_c_re"] = _c_from_cd(dcd_re)
    G["ssm_c_im"] = -_c_from_cd(dcd_imneg)
    dlam_re = jnp.sum(dl_re, axis=1).reshape(SSM_GROUPS, SSM_STATE)
    dlam_im = jnp.sum(dl_im, axis=1).reshape(SSM_GROUPS, SSM_STATE)
    (G["ssm_a_re"], G["ssm_a_im"], G["ssm_log_dt"], G["ssm_b_re"], G["ssm_b_im"]) = prep_vjp(
        (dlam_re, dlam_im, _bbar_from_bd(dbd_re), _bbar_from_bd(dbd_im)))
    G["ffn_conv_b"] = G["ffn_conv_b"].reshape(N_DEV, FF_LOCAL_PAD)[:, :FF_LOCAL]
    small = [G[n].reshape(SMALL_SHAPE[n]) for n in SMALL_EARLY]
    (dq, dk, dv), brought = _sb_bwd(
        proj, sb_tot, sb_first, do_attn, name="sb_bwd",
        rider=_Exchange.join(_scatter_xy([pair[n] for n in REDUCE_MID]), _gather_xy_from(small)))
    from_chips, small = brought[:len(REDUCE_MID)], brought[len(REDUCE_MID):]
    reduced.update({n: (pair[n], parts) for n, parts in zip(REDUCE_MID, from_chips)})
    dproj = jnp.concatenate([dq, dk, dv, du, dgate], axis=1)
    G["w_in"], small = mm(h1, dproj, ta=True, out_cb=W["w_in"].shape[2], name="mm_in_dw", rider=_fill_c(small))
    g_in = _grad_blocks("w_in", G["w_in"])
    dh1, (from_core,) = mm(dproj, W["w_in"], tb=True, out_dtype=BF16, name="mm_in_dx", rider=_send_c([g_in]))
    pair_in = _pair_sum(g_in, from_core, core, name="pair_sum_w_in")
    (grad_x, dg_pre), (from_chips,) = _norm_bwd_single(dx1, dh1, x, P["norm_mix_pre"], name="norm_bwd_1",
                                                       rider=_scatter_xy([pair_in]))
    reduced["w_in"] = (pair_in, from_chips)
    last, = _gather_all([dg_pre]).run(name="gather_g_last")
    parts = dict(zip(SMALL_EARLY, small))
    parts["norm_mix_pre"] = last
    return loss, grad_x, parts, reduced


MESH = pl.DeviceIdType.MESH
_HBM = pl.BlockSpec(memory_space=pl.ANY)
N_XY = 4
N_XY_PEERS = 3


def _xy_peers(x, y):
    return [(1 - x, y), (x, 1 - y), (1 - x, 1 - y)]


class _Exchange:
    def __init__(self, arrays, out_shapes, plan, n_copies, alias):
        self.arrays = list(arrays)
        self.out_shapes = list(out_shapes)
        self.plan = plan
        self.n_copies = n_copies
        self.alias = list(alias) if isinstance(alias, (list, tuple)) else [alias] * len(self.arrays)

    @property
    def n(self):
        return len(self.arrays)

    def aliases(self, first_in, first_out):
        return {first_in + k: first_out + k for k in range(self.n) if self.alias[k]}

    @staticmethod
    def join(a, b):
        def plan(k, src, dst, x, y, c):
            return a.plan(k, src, dst, x, y, c) if k < a.n else b.plan(k - a.n, src, dst, x, y, c)

        return _Exchange(a.arrays + b.arrays, a.out_shapes + b.out_shapes, plan, max(a.n_copies, b.n_copies),
                         a.alias + b.alias)

    def sems(self):
        shape = (self.n, self.n_copies)
        return [pltpu.SemaphoreType.DMA(shape), pltpu.SemaphoreType.DMA(shape)]

    def _copies(self, ins, outs, send_sems, recv_sems):
        x, y, c = lax.axis_index("x"), lax.axis_index("y"), lax.axis_index("c")
        sends, lands, own = [], [], []
        for k in range(self.n):
            for j, (src, dst, dev, land) in enumerate(self.plan(k, ins[k], outs[k], x, y, c)):
                if dev is None:
                    own.append(pltpu.make_async_copy(src, dst, send_sems.at[k, j]))
                    continue
                sems = dict(send_sem=send_sems.at[k, j], recv_sem=recv_sems.at[k, j], device_id=dev, device_id_type=MESH)
                sends.append(pltpu.make_async_remote_copy(src_ref=src, dst_ref=dst, **sems))
                lands.append(pltpu.make_async_remote_copy(src_ref=src, dst_ref=land, **sems))
        return sends, lands, own

    def start(self, ins, outs, send_sems, recv_sems):
        sends, _, own = self._copies(ins, outs, send_sems, recv_sems)
        for cp in own + sends:
            cp.start()

    def finish(self, ins, outs, send_sems, recv_sems):
        sends, lands, own = self._copies(ins, outs, send_sems, recv_sems)
        for cp in lands:
            cp.wait_recv()
        for cp in sends:
            cp.wait_send()
        for cp in own:
            cp.wait()

    def run(self, *, name):
        n = self.n

        def body(*refs):
            parts = (refs[:n], refs[n:2 * n], refs[2 * n], refs[2 * n + 1])
            self.start(*parts)
            self.finish(*parts)

        return pl.pallas_call(
            body, name=name, out_shape=tuple(self.out_shapes),
            in_specs=[_HBM] * n, out_specs=tuple([_HBM] * n),
            input_output_aliases=self.aliases(0, 0),
            scratch_shapes=self.sems(),
        )(*self.arrays)


def _call(host_body, *, name, grid, in_specs, out_specs, out_shape, scratch_shapes, operands, rider=None):
    out_specs, out_shape = tuple(out_specs), tuple(out_shape)
    if rider is None:
        res = pl.pallas_call(
            host_body, name=name, grid=grid, in_specs=list(in_specs), out_specs=out_specs, out_shape=out_shape,
            scratch_shapes=list(scratch_shapes), compiler_params=_cparams(*["arbitrary"] * len(grid)),
        )(*operands)
        return tuple(res), None
    n, n_in, n_out, n_scr = rider.n, len(in_specs), len(out_specs), len(scratch_shapes)

    def body(*refs):
        pos = [0]

        def take(count):
            pos[0] += count
            return refs[pos[0] - count:pos[0]]

        h_in, r_in, h_out, r_out, h_scr = take(n_in), take(n), take(n_out), take(n), take(n_scr)
        send_sems, recv_sems = take(2)
        ids = [pl.program_id(a) for a in range(len(grid))]
        first = functools.reduce(jnp.logical_and, [i == 0 for i in ids])
        last = functools.reduce(jnp.logical_and, [i == g - 1 for i, g in zip(ids, grid)])

        @pl.when(first)
        def _():
            rider.start(r_in, r_out, send_sems, recv_sems)

        host_body(*h_in, *h_out, *h_scr)

        @pl.when(last)
        def _():
            rider.finish(r_in, r_out, send_sems, recv_sems)

    res = pl.pallas_call(
        body, name=name, grid=grid,
        in_specs=list(in_specs) + [_HBM] * n, out_specs=out_specs + tuple([_HBM] * n),
        out_shape=out_shape + tuple(rider.out_shapes),
        input_output_aliases=rider.aliases(n_in, n_out),
        scratch_shapes=list(scratch_shapes) + rider.sems(),
        compiler_params=_cparams(*["arbitrary"] * len(grid)),
    )(*operands, *rider.arrays)
    return tuple(res[:n_out]), list(res[n_out:])


def _same(arrays):
    return [jax.ShapeDtypeStruct(a.shape, a.dtype) for a in arrays]


def _fill_xy(bufs):
    def plan(k, src, dst, x, y, c):
        mine = 2 * x + y
        return [(src.at[mine, c], dst.at[mine, c], (px, py, c), dst.at[2 * px + py, c]) for px, py in _xy_peers(x, y)]

    return _Exchange(bufs, _same(bufs), plan, N_XY_PEERS, alias=True)


def _fill_c(bufs):
    def plan(k, src, dst, x, y, c):
        return [(src.at[:, c], dst.at[:, c], (x, y, 1 - c), dst.at[:, 1 - c])]

    return _Exchange(bufs, _same(bufs), plan, 1, alias=True)


def _slots(arrays):
    return [jax.ShapeDtypeStruct((N_XY, 2) + a.shape, a.dtype) for a in arrays]


def _gather_xy_from(srcs):
    def plan(k, src, dst, x, y, c):
        mine = 2 * x + y
        return ([(src, dst.at[mine, c], None, None)]
                + [(src, dst.at[mine, c], (px, py, c), dst.at[2 * px + py, c]) for px, py in _xy_peers(x, y)])

    return _Exchange(srcs, _slots(srcs), plan, 1 + N_XY_PEERS, alias=False)


def _gather_all(srcs):
    def plan(k, src, dst, x, y, c):
        mine = 2 * x + y
        out = [(src, dst.at[mine, c], None, None)]
        for fx, fy, fc in [(a, b, e) for a in (0, 1) for b in (0, 1) for e in (0, 1)][1:]:
            px, py, pc = (1 - x) if fx else x, (1 - y) if fy else y, (1 - c) if fc else c
            out.append((src, dst.at[mine, c], (px, py, pc), dst.at[2 * px + py, pc]))
        return out

    return _Exchange(srcs, _slots(srcs), plan, N_DEV, alias=False)


def _send_c(srcs):
    def plan(k, src, dst, x, y, c):
        return [(src.at[:, 1 - c], dst, (x, y, 1 - c), dst)]

    outs = [jax.ShapeDtypeStruct(a.shape[:1] + a.shape[2:], a.dtype) for a in srcs]
    return _Exchange(srcs, outs, plan, 1, alias=False)


def _scatter_xy(srcs):
    def plan(k, src, dst, x, y, c):
        return [(src.at[2 * px + py], dst.at[j], (px, py, c), dst.at[j]) for j, (px, py) in enumerate(_xy_peers(x, y))]

    outs = [jax.ShapeDtypeStruct((N_XY_PEERS,) + a.shape[1:], a.dtype) for a in srcs]
    return _Exchange(srcs, outs, plan, N_XY_PEERS, alias=False)


WIRE_DTYPE = BF16


def _pair_sum(g8, recv, core, *, name):
    n, _, R, C = g8.shape
    tr = _pick(R, (128, 64, 32, 16, 8))

    def body(core_ref, a_ref, b_ref, o_ref):
        o_ref[...] = (a_ref[0] + b_ref[...]).astype(WIRE_DTYPE)

    return pl.pallas_call(
        body, name=name, out_shape=jax.ShapeDtypeStruct((n, R, C), WIRE_DTYPE),
        grid_spec=pltpu.PrefetchScalarGridSpec(
            num_scalar_prefetch=1, grid=(n, R // tr),
            in_specs=[pl.BlockSpec((1, 1, tr, C), lambda s, i, core_ref: (s, core_ref[0], i, 0)),
                      pl.BlockSpec((1, tr, C), lambda s, i, core_ref: (s, i, 0))],
            out_specs=pl.BlockSpec((1, tr, C), lambda s, i, core_ref: (s, i, 0))),
        compiler_params=_cparams("parallel", "parallel"),
    )(core, g8, recv)


def _adamw_math(w, g, m, v):
    m = ADAM_B1 * m + (1.0 - ADAM_B1) * g
    v = ADAM_B2 * v + (1.0 - ADAM_B2) * (g * g)
    m_hat = m / (1.0 - ADAM_B1 ** ADAM_STEP)
    v_hat = v / (1.0 - ADAM_B2 ** ADAM_STEP)
    delta = -ADAM_LR * (m_hat / (jnp.sqrt(v_hat) + ADAM_EPS) + ADAM_WD * w)
    return delta, m, v


def _reduce_adamw(parts, w, m, v, *, own, own_slot, name):
    n, R, C = parts.shape
    tr = _pick(R, (128, 64, 32, 16, 8))

    def body(_, own_ref, parts_ref, w_ref, m_ref, v_ref, g_ref, d_ref, nm_ref, nv_ref):
        g = own_ref[0].astype(F32)
        for k in range(n):
            g = g + parts_ref[k].astype(F32)
        g_ref[...] = g
        d_ref[...], nm_ref[...], nv_ref[...] = _adamw_math(w_ref[...], g, m_ref[...], v_ref[...])

    out = jax.ShapeDtypeStruct((R, C), F32)
    row = pl.BlockSpec((tr, C), lambda i, s: (i, 0))
    return pl.pallas_call(
        body, name=name, out_shape=(out, out, out, out),
        grid_spec=pltpu.PrefetchScalarGridSpec(
            num_scalar_prefetch=1, grid=(R // tr,),
            in_specs=[pl.BlockSpec((1, tr, C), lambda i, s: (s[0], i, 0)),
                      pl.BlockSpec((n, tr, C), lambda i, s: (0, i, 0)), row, row, row],
            out_specs=(row, row, row, row)),
        compiler_params=_cparams("parallel"),
    )(own_slot, own, parts, w, m, v)


SHARDED = (("w_in", (1024, 4096), 1), ("ssm_w_glu", (512, 512), 0), ("w_branch_attn", (512, 1024), 1),
           ("w_branch_ssm", (512, 1024), 1), ("w_out", (1024, 1024), 0), ("xa_wq", (1024, 1024), 0),
           ("xa_wk", (1024, 1024), 0), ("xa_wv", (1024, 1024), 0), ("xa_wo", (1024, 1024), 0),
           ("ffn_w_up", (1024, 5632), 1), ("ffn_conv_w", (3, 5632), 1), ("ffn_w_down", (2816, 1024), 0))
REPLICATED = (("norm_mix_pre", (1024,)), ("norm_mix_post", (1024,)), ("b_gate", (2048,)), ("ssm_a_re", (32, 64)),
              ("ssm_a_im", (32, 64)), ("ssm_log_dt", (32,)), ("ssm_b_re", (32, 64, 16)), ("ssm_b_im", (32, 64, 16)),
              ("ssm_c_re", (32, 16, 64)), ("ssm_c_im", (32, 16, 64)), ("ssm_d", (512,)), ("ssm_b_glu", (512,)),
              ("norm_xa_pre", (1024,)), ("norm_xa_post", (1024,)), ("norm_mem", (1024,)), ("norm_ffn_pre", (1024,)),
              ("norm_ffn_post", (1024,)), ("ffn_conv_b", (5632,)))
PARAM_ORDER = ("norm_mix_pre", "norm_mix_post", "w_in", "b_gate", "ssm_a_re", "ssm_a_im", "ssm_log_dt", "ssm_b_re",
               "ssm_b_im", "ssm_c_re", "ssm_c_im", "ssm_d", "ssm_w_glu", "ssm_b_glu", "w_branch_attn", "w_branch_ssm",
               "w_out", "norm_xa_pre", "norm_xa_post", "norm_mem", "xa_wq", "xa_wk", "xa_wv", "xa_wo", "norm_ffn_pre",
               "norm_ffn_post", "ffn_w_up", "ffn_conv_w", "ffn_conv_b", "ffn_w_down")
FF_LOCAL = 2 * D_FF // N_DEV
FF_LOCAL_PAD = 768
FF_PAD = (N_DEV // 2) * FF_LOCAL_PAD


def _local_shape(shape, axis):
    return tuple(s // N_DEV if a == axis else s for a, s in enumerate(shape))


def _pad_cols(a, width):
    return jnp.pad(a, [(0, 0)] * (a.ndim - 1) + [(0, width - a.shape[-1])])


def _blocks_to_cols(a8):
    return a8.transpose(1, 0, 2).reshape(a8.shape[1], N_DEV * a8.shape[2])


def _cols_to_blocks(a, cb):
    return a.reshape(a.shape[0], N_DEV, cb).transpose(1, 0, 2)


FF_PADDED = ("ffn_w_up", "ffn_conv_w")
LATE = tuple(n for n, _, _ in SHARDED if n != "w_in")
REDUCE_FFN = ("ffn_w_up", "ffn_conv_w", "ffn_w_down")
REDUCE_MID = ("xa_wo", "xa_wq", "xa_wk", "xa_wv", "w_out", "w_branch_attn", "w_branch_ssm", "ssm_w_glu")
SHARD_AXIS = {n: ax for n, _, ax in SHARDED}
FULL_SHAPE = {n: s for n, s, _ in SHARDED}


def _as_local(n, a):
    return _pad_cols(a, FF_LOCAL_PAD) if n in FF_PADDED else a


def _weights_from_wire(wire):
    full = {n: b.reshape((N_DEV,) + b.shape[2:]) for n, b in wire.items()}
    W = {n: a.reshape(FULL_SHAPE[n]) if SHARD_AXIS[n] == 0 else a for n, a in full.items()}
    for n in ("w_branch_attn", "w_branch_ssm", "ffn_conv_w"):
        W[n] = _blocks_to_cols(full[n])
    W["ffn_w_down"] = jnp.pad(W["ffn_w_down"].reshape(N_DEV // 2, FF_LOCAL, D_MODEL),
                              ((0, 0), (0, FF_LOCAL_PAD - FF_LOCAL), (0, 0))).reshape(FF_PAD, D_MODEL)
    return W


def _grad_blocks(n, g):
    if n in ("w_branch_attn", "w_branch_ssm"):
        g = _cols_to_blocks(g, D_MODEL // N_DEV)
    elif n == "ffn_conv_w":
        g = _cols_to_blocks(g, FF_LOCAL_PAD)
    elif n == "ffn_w_down":
        g = g.reshape(N_DEV // 2, FF_LOCAL_PAD, D_MODEL)[:, :FF_LOCAL]
    local = _local_shape(FULL_SHAPE[n], SHARD_AXIS[n])
    if n in FF_PADDED:
        local = local[:-1] + (FF_LOCAL_PAD,)
    return g.reshape((N_XY, 2) + local)


SMALL_SHAPE = {n: (1, s[0]) if len(s) == 1 else (s[0], math.prod(s[1:])) for n, s in REPLICATED}
SMALL_SHAPE["ffn_conv_b"] = (N_DEV, FF_LOCAL)
SMALL_EARLY = tuple(n for n, _ in REPLICATED if n != "norm_mix_pre")


def _adamw_replicated(parts, w, m, v, *, name):
    n = len(parts)

    def body(*refs):
        p_refs, w_refs, m_refs, v_refs = (refs[i * n:(i + 1) * n] for i in range(4))
        outs = refs[4 * n:]
        for k in range(n):
            g = p_refs[k][0, 0]
            for s in range(1, N_DEV):
                g = g + p_refs[k][s // 2, s % 2]
            d, nm, nv = _adamw_math(w_refs[k][...], g, m_refs[k][...], v_refs[k][...])
            for slot, val in enumerate((g, d, nm, nv)):
                outs[slot * n + k][...] = val

    vmem = pl.BlockSpec(memory_space=pltpu.VMEM)
    shapes = [jax.ShapeDtypeStruct(a.shape, F32) for a in w] * 4
    res = pl.pallas_call(
        body, name=name, out_shape=tuple(shapes), in_specs=[vmem] * (4 * n), out_specs=tuple([vmem] * (4 * n)),
        compiler_params=pltpu.CompilerParams(vmem_limit_bytes=VMEM_LIMIT),
    )(*parts, *w, *m, *v)
    return [list(res[i * n:(i + 1) * n]) for i in range(4)]


def kernel(x, mem, norm_mix_pre, norm_mix_post, w_in, b_gate, ssm_a_re, ssm_a_im, ssm_log_dt, ssm_b_re, ssm_b_im, ssm_c_re, ssm_c_im, ssm_d, ssm_w_glu, ssm_b_glu, w_branch_attn, w_branch_ssm, w_out, norm_xa_pre, norm_xa_post, norm_mem, xa_wq, xa_wk, xa_wv, xa_wo, norm_ffn_pre, norm_ffn_post, ffn_w_up, ffn_conv_w, ffn_conv_b, ffn_w_down, loss_target, m_norm_mix_pre, m_norm_mix_post, m_w_in, m_b_gate, m_ssm_a_re, m_ssm_a_im, m_ssm_log_dt, m_ssm_b_re, m_ssm_b_im, m_ssm_c_re, m_ssm_c_im, m_ssm_d, m_ssm_w_glu, m_ssm_b_glu, m_w_branch_attn, m_w_branch_ssm, m_w_out, m_norm_xa_pre, m_norm_xa_post, m_norm_mem, m_xa_wq, m_xa_wk, m_xa_wv, m_xa_wo, m_norm_ffn_pre, m_norm_ffn_post, m_ffn_w_up, m_ffn_conv_w, m_ffn_conv_b, m_ffn_w_down, v_norm_mix_pre, v_norm_mix_post, v_w_in, v_b_gate, v_ssm_a_re, v_ssm_a_im, v_ssm_log_dt, v_ssm_b_re, v_ssm_b_im, v_ssm_c_re, v_ssm_c_im, v_ssm_d, v_ssm_w_glu, v_ssm_b_glu, v_w_branch_attn, v_w_branch_ssm, v_w_out, v_norm_xa_pre, v_norm_xa_post, v_norm_mem, v_xa_wq, v_xa_wk, v_xa_wv, v_xa_wo, v_norm_ffn_pre, v_norm_ffn_post, v_ffn_w_up, v_ffn_conv_w, v_ffn_conv_b, v_ffn_w_down):
    args = dict(locals())
    w_loc = {n: args[n][0] for n in PARAM_ORDER}
    m_loc = {n: args["m_" + n][0] for n in PARAM_ORDER}
    v_loc = {n: args["v_" + n][0] for n in PARAM_ORDER}
    core_i = lax.axis_index("c")
    chip_i = 2 * lax.axis_index("x") + lax.axis_index("y")
    core = core_i.astype(jnp.int32).reshape(1)
    chip = chip_i.astype(jnp.int32).reshape(1)

    def in_place(a):
        buf = lax.empty((N_XY, 2) + a.shape, a.dtype)
        return lax.dynamic_update_slice(buf, a[None, None], (chip_i, core_i) + (0,) * a.ndim)

    as_wire = lambda n: in_place(_as_local(n, w_loc[n]).astype(F32 if n == "ffn_conv_w" else BF16))

    P = {}
    for n, shape in REPLICATED:
        P[n] = w_loc[n] if len(shape) > 1 or n == "ssm_log_dt" else w_loc[n].reshape(1, -1)
    P["ffn_conv_b"] = _pad_cols(w_loc["ffn_conv_b"].reshape(N_DEV, FF_LOCAL), FF_LOCAL_PAD).reshape(1, 2 * FF_PAD)

    loss, grad_x, small_parts, reduced = _local_step(x[0], mem[0], loss_target[0], as_wire("w_in"),
                                                     [as_wire(n) for n in LATE], P, core)
    loss = lax.psum(loss[0, 0], ("x", "y", "c"))

    big_out = {}
    for n, (own, parts) in reduced.items():
        res = _reduce_adamw(parts, _as_local(n, w_loc[n]), _as_local(n, m_loc[n]), _as_local(n, v_loc[n]),
                            own=own, own_slot=chip, name="adamw_" + n)
        big_out[n] = [r[:, :FF_LOCAL] if n in FF_PADDED else r for r in res]

    names = [n for n, _ in REPLICATED]
    as_small = lambda d: [d[n].reshape(SMALL_SHAPE[n]) for n in names]
    small_out = _adamw_replicated([small_parts[n] for n in names], as_small(w_loc), as_small(m_loc), as_small(v_loc),
                                  name="adamw_replicated")
    small_out = [dict(zip(names, res)) for res in small_out]

    outs = [loss, grad_x[None]]
    for k in range(4):
        for n in PARAM_ORDER:
            src = big_out[n][k] if n in big_out else small_out[k][n]
            outs.append(src.reshape(args[n].shape))
    return tuple(outs)
```

```python
import functools
import math

import jax
import jax.numpy as jnp
from jax import lax
from jax.experimental import pallas as pl
from jax.experimental.pallas import tpu as pltpu

F32 = jnp.float32
BF16 = jnp.bfloat16

D_MODEL = 1024
SB_HEADS = 8
SB_HEAD_DIM = 64
SB_WIDTH = 512
SSM_WIDTH = 512
SSM_GROUP = 16
SSM_GROUPS = 32
SSM_STATE = 64
XA_HEADS = 4
XA_HEAD_DIM = 256
D_FF = 2816
RMS_EPS = 1e-6
IN_WIDTH = 4096
N_DEV = 8

ADAM_LR = 0.001
ADAM_B1 = 0.9
ADAM_B2 = 0.999
ADAM_EPS = 1e-08
ADAM_WD = 0.01
ADAM_STEP = 10

LANES = 128
SUBLANES = 8
VMEM_LIMIT = 48 * 1024 * 1024

_GELU_C = math.sqrt(2.0 / math.pi)


def _cparams(*sem):
    return pltpu.CompilerParams(dimension_semantics=sem, vmem_limit_bytes=VMEM_LIMIT)


def _pick(n, cands):
    for c in cands:
        if n % c == 0:
            return c
    return n


def _gelu(x):
    return 0.5 * x * (1.0 + jnp.tanh(_GELU_C * (x + 0.044715 * x * x * x)))


def _gelu_and_grad(x):
    t = jnp.tanh(_GELU_C * (x + 0.044715 * x * x * x))
    g = 0.5 * x * (1.0 + t)
    dg = 0.5 * (1.0 + t) + 0.5 * x * (1.0 - t * t) * _GELU_C * (1.0 + 3.0 * 0.044715 * x * x)
    return g, dg


def _sigmoid(x):
    return 1.0 / (1.0 + jnp.exp(-x))


def _dot(a, b, ca, cb):
    return lax.dot_general(a.astype(BF16), b.astype(BF16), (((ca,), (cb,)), ((), ())),
                           preferred_element_type=F32)


MM_TILES = (1024, 768, 512, 256, 128)
MM_K_TILES = (2048, 1536) + MM_TILES


def _matmul(a, b, *, ta=False, tb=False, out_dtype=F32, name, b_block0=0, n_blocks=None,
            out_cb=None, out_into=None, out_block0=0, acc_in=None, rider=None):
    if ta:
        K, M = a.shape
    else:
        M, K = a.shape
    b_cb = None
    if b.ndim == 3:
        b_cb = b.shape[2]
        n_blocks = b.shape[0] - b_block0 if n_blocks is None else n_blocks
        N, K2 = (b.shape[1], n_blocks * b_cb) if tb else (n_blocks * b_cb, b.shape[1])
    elif tb:
        N, K2 = b.shape
    else:
        K2, N = b.shape
    assert K == K2, (a.shape, b.shape, ta, tb)
    if out_into is not None:
        out_cb = out_into.shape[2]
    tm = _pick(M, MM_TILES)
    n_unit = math.gcd(N, math.gcd(b_cb if (b_cb and not tb) else N, out_cb or N))
    tn = _pick(n_unit, MM_TILES)
    k_unit = b_cb if (b_cb and tb) else K
    tk = _pick(k_unit, MM_K_TILES)
    nk = K // tk
    ca, cb = (0 if ta else 1), (1 if tb else 0)
    has_acc = acc_in is not None
    has_into = out_into is not None

    def body(*refs):
        a_ref, b_ref = refs[0], refs[1]
        pos = 2
        c_ref = None
        if has_acc:
            c_ref = refs[pos]
            pos += 1
        if has_into:
            pos += 1
        o_ref = refs[pos]
        p = _dot(a_ref[...], b_ref[...], ca, cb)
        if nk == 1:
            o_ref[...] = ((p + c_ref[...]) if has_acc else p).astype(out_dtype)
        else:
            acc_ref = refs[pos + 1]
            k = pl.program_id(2)

            @pl.when(k == 0)
            def _():
                acc_ref[...] = (p + c_ref[...]) if has_acc else p

            @pl.when(k > 0)
            def _():
                acc_ref[...] += p

            @pl.when(k == nk - 1)
            def _():
                o_ref[...] = acc_ref[...].astype(out_dtype)

    nj, ni = N // tn, M // tm
    a_bytes, b_bytes = a.size * a.dtype.itemsize, K * N * b.dtype.itemsize
    n_outer = a_bytes * nj + b_bytes * (1 if nk == 1 else ni) <= a_bytes * (1 if nk == 1 else nj) + b_bytes * ni
    grid = (nj, ni, nk) if n_outer else (ni, nj, nk)

    def spec(block, index):
        return pl.BlockSpec(block, (lambda g0, g1, k: index(g0, g1, k)) if n_outer else (lambda g0, g1, k: index(g1, g0, k)))

    a_spec = spec((tk, tm), lambda j, i, k: (k, i)) if ta else spec((tm, tk), lambda j, i, k: (i, k))
    if b_cb is None:
        b_spec = spec((tn, tk), lambda j, i, k: (j, k)) if tb else spec((tk, tn), lambda j, i, k: (k, j))
    elif tb:
        per = b_cb // tk
        b_spec = spec((None, tn, tk), lambda j, i, k: (b_block0 + k // per, j, k % per))
    else:
        per = b_cb // tn
        b_spec = spec((None, tk, tn), lambda j, i, k: (b_block0 + j // per, k, j % per))
    in_specs = [a_spec, b_spec]
    operands = [a, b]
    aliases = {}
    if has_acc:
        in_specs.append(spec((tm, tn), lambda j, i, k: (i, j)))
        operands.append(acc_in)
    if has_into:
        aliases = {len(operands): 0}
        in_specs.append(pl.BlockSpec(memory_space=pl.ANY))
        operands.append(out_into)
    if out_cb is None:
        out_shape = jax.ShapeDtypeStruct((M, N), out_dtype)
        out_spec = spec((tm, tn), lambda j, i, k: (i, j))
    else:
        per_o = out_cb // tn
        out_shape = (jax.ShapeDtypeStruct(out_into.shape, out_into.dtype) if has_into
                     else jax.ShapeDtypeStruct((N // out_cb, M, out_cb), out_dtype))
        out_spec = spec((None, tm, tn), lambda j, i, k: (out_block0 + j // per_o, i, j % per_o))
    if rider is not None:
        assert not has_into
        (out,), brought = _call(body, name=name, rider=rider, grid=grid, in_specs=in_specs,
                                out_specs=(out_spec,), out_shape=(out_shape,), operands=operands,
                                scratch_shapes=[] if nk == 1 else [pltpu.VMEM((tm, tn), F32)])
        return out, brought
    return pl.pallas_call(
        body, name=name, out_shape=out_shape,
        grid=grid,
        in_specs=in_specs, out_specs=out_spec, input_output_aliases=aliases,
        scratch_shapes=[] if nk == 1 else [pltpu.VMEM((tm, tn), F32)],
        compiler_params=_cparams("parallel", "parallel", "arbitrary"),
    )(*operands)


def _rms(x, g):
    r = lax.rsqrt(jnp.mean(x * x, axis=-1, keepdims=True) + RMS_EPS)
    return x * r * g


def _rms_bwd(dy, x, g):
    r = lax.rsqrt(jnp.mean(x * x, axis=-1, keepdims=True) + RMS_EPS)
    xh = x * r
    dxh = dy * g
    dx = r * (dxh - xh * jnp.mean(dxh * xh, axis=-1, keepdims=True))
    dg = jnp.sum(dy * xh, axis=0, keepdims=True)
    return dx, dg


def _row_tile(rows):
    return _pick(rows, (512, 256, 128, 64, 32, 16, 8))


def _rms_fwd(x, g, *, name, rider=None):
    R, D = x.shape
    tr = _row_tile(R)

    def body(x_ref, g_ref, h_ref):
        h_ref[...] = _rms(x_ref[...], g_ref[...]).astype(BF16)

    (h,), brought = _call(
        body, name=name, rider=rider, out_shape=(jax.ShapeDtypeStruct((R, D), BF16),), grid=(R // tr,),
        in_specs=[pl.BlockSpec((tr, D), lambda i: (i, 0)), pl.BlockSpec((1, D), lambda i: (0, 0))],
        out_specs=(pl.BlockSpec((tr, D), lambda i: (i, 0)),), scratch_shapes=[], operands=(x, g))
    return h if rider is None else (h, brought)


def _resnorm_norm(x, z, g_post, g_next, *, name):
    R, D = x.shape
    tr = _row_tile(R)

    def body(x_ref, z_ref, gp_ref, gn_ref, xn_ref, h_ref):
        xn = x_ref[...] + _rms(z_ref[...], gp_ref[...])
        xn_ref[...] = xn
        h_ref[...] = _rms(xn, gn_ref[...]).astype(BF16)

    row = pl.BlockSpec((tr, D), lambda i: (i, 0))
    vec = pl.BlockSpec((1, D), lambda i: (0, 0))
    return pl.pallas_call(
        body, name=name,
        out_shape=(jax.ShapeDtypeStruct((R, D), F32), jax.ShapeDtypeStruct((R, D), BF16)),
        grid=(R // tr,), in_specs=[row, row, vec, vec], out_specs=(row, row),
        compiler_params=_cparams("parallel"),
    )(x, z, g_post, g_next)


def _final_loss(x, z, g_post, target, *, name):
    R, D = x.shape
    tr = _row_tile(R)

    def body(x_ref, z_ref, gp_ref, t_ref, loss_ref, dy_ref, dz_ref, dg_ref):
        i = pl.program_id(0)
        z = z_ref[...]
        g = gp_ref[...]
        err = x_ref[...] + _rms(z, g) - t_ref[...]
        dy = err * (1.0 / D)
        dy_ref[...] = dy
        dz, dg = _rms_bwd(dy, z, g)
        dz_ref[...] = dz.astype(BF16)
        part = 0.5 * jnp.sum(jnp.sum(err * err, axis=-1, keepdims=True) * (1.0 / D), axis=0, keepdims=True)

        @pl.when(i == 0)
        def _():
            loss_ref[...] = part
            dg_ref[...] = dg

        @pl.when(i > 0)
        def _():
            loss_ref[...] += part
            dg_ref[...] += dg

    row = pl.BlockSpec((tr, D), lambda i: (i, 0))
    vec = pl.BlockSpec((1, D), lambda i: (0, 0))
    return pl.pallas_call(
        body, name=name,
        out_shape=(jax.ShapeDtypeStruct((1, 1), F32), jax.ShapeDtypeStruct((R, D), F32),
                   jax.ShapeDtypeStruct((R, D), BF16), jax.ShapeDtypeStruct((1, D), F32)),
        grid=(R // tr,), in_specs=[row, row, vec, row],
        out_specs=(pl.BlockSpec((1, 1), lambda i: (0, 0)), row, row, vec),
        compiler_params=_cparams("arbitrary"),
    )(x, z, g_post, target)


def _norm_bwd_pair(dres, dh, xk, g_pre, zprev, g_prev_post, *, name, rider=None):
    R, D = xk.shape
    tr = _row_tile(R)

    def body(dres_ref, dh_ref, x_ref, gpre_ref, z_ref, gpost_ref, dx_ref, dz_ref, dgpre_ref, dgpost_ref):
        i = pl.program_id(0)
        d1, dgpre = _rms_bwd(dh_ref[...], x_ref[...], gpre_ref[...])
        dx = dres_ref[...] + d1
        dx_ref[...] = dx
        dz, dgpost = _rms_bwd(dx, z_ref[...], gpost_ref[...])
        dz_ref[...] = dz.astype(BF16)

        @pl.when(i == 0)
        def _():
            dgpre_ref[...] = dgpre
            dgpost_ref[...] = dgpost

        @pl.when(i > 0)
        def _():
            dgpre_ref[...] += dgpre
            dgpost_ref[...] += dgpost

    row = pl.BlockSpec((tr, D), lambda i: (i, 0))
    vec = pl.BlockSpec((1, D), lambda i: (0, 0))
    return _call(
        body, name=name, rider=rider,
        out_shape=(jax.ShapeDtypeStruct((R, D), F32), jax.ShapeDtypeStruct((R, D), BF16),
                   jax.ShapeDtypeStruct((1, D), F32), jax.ShapeDtypeStruct((1, D), F32)),
        grid=(R // tr,), in_specs=[row, row, row, vec, row, vec], out_specs=(row, row, vec, vec),
        scratch_shapes=[], operands=(dres, dh, xk, g_pre, zprev, g_prev_post))


def _norm_bwd_single(dres, dh, xk, g_pre, *, name, rider=None):
    R, D = xk.shape
    tr = _row_tile(R)
    has_res = dres is not None

    def body(*refs):
        if has_res:
            dres_ref, dh_ref, x_ref, gpre_ref, dx_ref, dgpre_ref = refs
        else:
            dh_ref, x_ref, gpre_ref, dx_ref, dgpre_ref = refs
        i = pl.program_id(0)
        d1, dgpre = _rms_bwd(dh_ref[...], x_ref[...], gpre_ref[...])
        dx_ref[...] = dres_ref[...] + d1 if has_res else d1

        @pl.when(i == 0)
        def _():
            dgpre_ref[...] = dgpre

        @pl.when(i > 0)
        def _():
            dgpre_ref[...] += dgpre

    row = pl.BlockSpec((tr, D), lambda i: (i, 0))
    vec = pl.BlockSpec((1, D), lambda i: (0, 0))
    ins = ([dres] if has_res else []) + [dh, xk, g_pre]
    res, brought = _call(
        body, name=name, rider=rider,
        out_shape=(jax.ShapeDtypeStruct((R, D), F32), jax.ShapeDtypeStruct((1, D), F32)),
        grid=(R // tr,), in_specs=([row] if has_res else []) + [row, row, vec], out_specs=(row, vec),
        scratch_shapes=[], operands=ins)
    return res if rider is None else (res, brought)


SB_BLOCK = 256
SB_QBLOCK = 512
SB_DEAD = -104.0


def _sb_tri(kind):
    r = lax.broadcasted_iota(jnp.int32, (SB_BLOCK, SB_BLOCK), 0)
    c = lax.broadcasted_iota(jnp.int32, (SB_BLOCK, SB_BLOCK), 1)
    keep = {"after": r > c, "before": r < c}[kind]
    return jnp.where(keep, 1.0, 0.0).astype(BF16)


def _sb_scores(qm, k_blk):
    z = _dot(qm, k_blk, 1, 1)
    sp = jnp.maximum(z, 0.0) + jnp.log(1.0 + jnp.exp(-jnp.abs(z)))
    return z, sp


def _sb_causal(rows):
    r = lax.broadcasted_iota(jnp.int32, (rows, SB_BLOCK), 0)
    c = lax.broadcasted_iota(jnp.int32, (rows, SB_BLOCK), 1)
    return c < r


def _head_masks():
    lane = lax.broadcasted_iota(jnp.int32, (1, LANES), 1)
    return [jnp.where(lane < SB_HEAD_DIM, 1.0, 0.0), jnp.where(lane >= SB_HEAD_DIM, 1.0, 0.0)]


def _sb_fwd(proj, *, name, rider=None):
    S = proj.shape[0]
    T = SB_BLOCK
    TQ = min(SB_QBLOCK, S)
    span = TQ // T
    nq = S // TQ
    npair = SB_WIDTH // LANES
    scale = SB_HEAD_DIM ** -0.5

    def body(q_ref, k_ref, v_ref, o_ref, tot_ref, first_ref, acc_ref, run_ref):
        masks = _head_masks()
        tri = _sb_tri("after")
        first_ref[...] = jnp.zeros_like(first_ref)
        slot = lax.broadcasted_iota(jnp.int32, first_ref.shape, 1)

        def alive():
            reach = jnp.maximum(jnp.max(run_ref[0]), jnp.max(run_ref[1]))
            return (reach > SB_DEAD).astype(jnp.int32)

        def q_block(i, _):
            qrow = pl.ds(pl.multiple_of(i * TQ, TQ), TQ)
            q = q_ref[qrow, :] * scale
            qm = [(q * m).astype(BF16) for m in masks]
            acc_ref[...] = jnp.zeros_like(acc_ref)
            run_ref[...] = jnp.zeros_like(run_ref)

            def k_block(j, own):
                krow = pl.ds(pl.multiple_of(j * T, T), T)
                k_blk = k_ref[krow, :].astype(BF16)
                v_blk = v_ref[krow, :].astype(BF16)
                r0 = 0 if own is None else own * T
                rows = pl.ds(r0, TQ - r0)
                for h in range(2):
                    z, sp = _sb_scores(qm[h][r0:], k_blk)
                    causal = None if own is None else _sb_causal(TQ - r0)
                    lf = -sp if causal is None else jnp.where(causal, -sp, 0.0)
                    e = jnp.exp(z - sp + _dot(lf, tri, 1, 0) + run_ref[h, rows])
                    w = e if causal is None else jnp.where(causal, e, 0.0)
                    acc_ref[h, rows] += _dot(w, v_blk, 1, 0)
                    run_ref[h, rows] += jnp.sum(lf, axis=1, keepdims=True)

            for d in reversed(range(span)):
                k_block(i * span + d, d)

            def below(carry):
                jj, _ = carry
                k_block(i * span - 1 - jj, None)
                return jj + 1, alive()

            done, _ = lax.fori_loop(0, i * span, lambda _, c: below(c), (jnp.int32(0), alive()))
            o_ref[qrow, :] = (acc_ref[0] * masks[0] + acc_ref[1] * masks[1]).astype(BF16)
            tot_ref[qrow, :] = run_ref[0] * masks[0] + run_ref[1] * masks[1]
            first_ref[...] = jnp.where(slot == i, (i * span - done).astype(F32), first_ref[...])
            return 0

        lax.fori_loop(0, nq, q_block, 0)

    blk = lambda off: pl.BlockSpec((S, LANES), lambda p: (0, off + p))
    return _call(
        body, name=name, rider=rider,
        out_shape=(jax.ShapeDtypeStruct((S, SB_WIDTH), BF16), jax.ShapeDtypeStruct((S, SB_WIDTH), F32),
                   jax.ShapeDtypeStruct((npair, SUBLANES, LANES), F32)),
        grid=(npair,),
        in_specs=[blk(0), blk(npair), blk(2 * npair)],
        out_specs=(blk(0), blk(0), pl.BlockSpec((1, SUBLANES, LANES), lambda p: (p, 0, 0))),
        scratch_shapes=[pltpu.VMEM((2, TQ, LANES), F32), pltpu.VMEM((2, TQ, 1), F32)],
        operands=(proj, proj, proj))


def _sb_bwd(proj, tot, first, do_attn, *, name, rider=None):
    S = proj.shape[0]
    T = SB_BLOCK
    TQ = min(SB_QBLOCK, S)
    span = TQ // T
    nq = S // TQ
    npair = SB_WIDTH // LANES
    scale = SB_HEAD_DIM ** -0.5

    def body(q_ref, k_ref, v_ref, tot_ref, first_ref, do_ref, dq_ref, dk_ref, dv_ref,
             dqacc_ref, dkacc_ref, dvacc_ref, run_ref, grun_ref):
        masks = _head_masks()
        tri_after = _sb_tri("after")
        tri_before = _sb_tri("before")
        dkacc_ref[...] = jnp.zeros_like(dkacc_ref)
        dvacc_ref[...] = jnp.zeros_like(dvacc_ref)
        slot = lax.broadcasted_iota(jnp.int32, first_ref.shape, 1)

        def q_block(i, _):
            qrow = pl.ds(pl.multiple_of(i * TQ, TQ), TQ)
            q = q_ref[qrow, :] * scale
            do = do_ref[qrow, :].astype(F32)
            tot = tot_ref[qrow, :]
            qm = [(q * m).astype(BF16) for m in masks]
            dom = [(do * m).astype(BF16) for m in masks]
            ltot = [jnp.sum(tot * m, axis=1, keepdims=True) * (1.0 / SB_HEAD_DIM) for m in masks]
            dqacc_ref[...] = jnp.zeros_like(dqacc_ref)
            run_ref[...] = jnp.zeros_like(run_ref)
            grun_ref[...] = jnp.zeros_like(grun_ref)

            def k_block(j, own):
                krow = pl.ds(pl.multiple_of(j * T, T), T)
                k_blk = k_ref[krow, :].astype(BF16)
                v_blk = v_ref[krow, :].astype(BF16)
                r0 = 0 if own is None else own * T
                rows = pl.ds(r0, TQ - r0)
                for h in range(2):
                    z, sp = _sb_scores(qm[h][r0:], k_blk)
                    causal = None if own is None else _sb_causal(TQ - r0)
                    lf = -sp if causal is None else jnp.where(causal, -sp, 0.0)
                    lsum = jnp.sum(lf, axis=1, keepdims=True)
                    later = (ltot[h][r0:] - run_ref[h, rows] - lsum) + _dot(lf, tri_after, 1, 0)
                    beta = jnp.exp(z - sp)
                    w = jnp.exp(z - sp + later)
                    if causal is not None:
                        w = jnp.where(causal, w, 0.0)
                    g = _dot(dom[h][r0:], v_blk, 1, 1) * w
                    gbefore = grun_ref[h, rows] + _dot(g, tri_before, 1, 0)
                    dz = g - beta * (g + gbefore)
                    if causal is not None:
                        dz = jnp.where(causal, dz, 0.0)
                    dz = dz.astype(BF16)
                    dqacc_ref[h, rows] += _dot(dz, k_blk, 1, 0)
                    dkacc_ref[krow, :] += _dot(dz, qm[h][r0:], 0, 0)
                    dvacc_ref[krow, :] += _dot(w, dom[h][r0:], 0, 0)
                    run_ref[h, rows] += lsum
                    grun_ref[h, rows] += jnp.sum(g, axis=1, keepdims=True)

            def above(j, _):
                k_block(j, None)
                return 0

            lax.fori_loop(0, i * span, above, 0)
            for d in range(span):
                k_block(i * span + d, d)
            dq_ref[qrow, :] = ((dqacc_ref[0] * masks[0] + dqacc_ref[1] * masks[1]) * scale).astype(BF16)
            return 0

        lax.fori_loop(0, nq, q_block, 0)
        dk_ref[...] = dkacc_ref[...].astype(BF16)
        dv_ref[...] = dvacc_ref[...].astype(BF16)

    blk = lambda off: pl.BlockSpec((S, LANES), lambda p: (0, off + p))
    out = jax.ShapeDtypeStruct((S, SB_WIDTH), BF16)
    return _call(
        body, name=name, rider=rider, out_shape=(out, out, out), grid=(npair,),
        in_specs=[blk(0), blk(npair), blk(2 * npair), blk(0), pl.BlockSpec((1, SUBLANES, LANES), lambda p: (p, 0, 0)),
                  blk(0)],
        out_specs=(blk(0), blk(0), blk(0)),
        scratch_shapes=[pltpu.VMEM((2, TQ, LANES), F32), pltpu.VMEM((S, LANES), F32), pltpu.VMEM((S, LANES), F32),
                        pltpu.VMEM((2, TQ, 1), F32), pltpu.VMEM((2, TQ, 1), F32)],
        operands=(proj, proj, proj, tot, first, do_attn))


SSM_HALVES = 2
SSM_HALF_CH = SSM_WIDTH // SSM_HALVES
SSM_HALF_ST = SSM_GROUPS * SSM_STATE // SSM_HALVES
SSM_CHUNK = 512


def _cmul(ar, ai, br, bi):
    return ar * br - ai * bi, ar * bi + ai * br


def _ssm_tables(lam_re, lam_im):
    lr = lam_re.reshape(-1)
    li = lam_im.reshape(-1)
    pows = [(jnp.ones_like(lr), jnp.zeros_like(li)), (lr, li)]
    for _ in range(2, SUBLANES + 1):
        pows.append(_cmul(pows[-1][0], pows[-1][1], lr, li))
    row = jnp.arange(SUBLANES)[:, None]

    def shift_tab(d, keep):
        return [jnp.where(keep, pows[d][0][None, :], 0.0), jnp.where(keep, pows[d][1][None, :], 0.0)]

    fwd, bwd = [], []
    for d in (1, 2, 4):
        fwd += shift_tab(d, row >= d)
        bwd += shift_tab(d, row + d < SUBLANES)
    fwd += [jnp.stack([pows[r + 1][0] for r in range(SUBLANES)]), jnp.stack([pows[r + 1][1] for r in range(SUBLANES)])]
    bwd += [jnp.stack([pows[SUBLANES - r][0] for r in range(SUBLANES)]),
            jnp.stack([pows[SUBLANES - r][1] for r in range(SUBLANES)])]

    def halves(tabs):
        t = jnp.stack(tabs)
        return t.reshape(8, SUBLANES, SSM_HALVES, SSM_HALF_ST).transpose(2, 0, 1, 3)

    return halves(fwd), halves(bwd)


def _ssm_fwd(proj, bd_re, bd_im, cd_re, cd_imneg, d_skip, tab, *, name, rider=None):
    S = proj.shape[0]
    Tc = min(SSM_CHUNK, S)
    nc = S // Tc
    u_blk0 = (3 * SB_WIDTH) // SSM_HALF_CH

    def body(u_ref, bre_ref, bim_ref, cre_ref, cim_ref, d_ref, tab_ref, y_ref, xre_ref, xim_ref, cre_s, cim_s):
        c = pl.program_id(1)

        @pl.when(c == 0)
        def _():
            cre_s[...] = jnp.zeros_like(cre_s)
            cim_s[...] = jnp.zeros_like(cim_s)

        u = u_ref[...]
        ub = u.astype(BF16)
        xre_ref[...] = _dot(ub, bre_ref[0], 1, 0)
        xim_ref[...] = _dot(ub, bim_ref[0], 1, 0)

        def slab(k, carry):
            car_re, car_im = carry
            rows = pl.ds(pl.multiple_of(k * SUBLANES, SUBLANES), SUBLANES)
            sre = xre_ref[rows, :]
            sim = xim_ref[rows, :]
            for n, d in enumerate((1, 2, 4)):
                pre, pim = tab_ref[0, 2 * n], tab_ref[0, 2 * n + 1]
                rre = pltpu.roll(sre, d, 0)
                rim = pltpu.roll(sim, d, 0)
                sre, sim = sre + (pre * rre - pim * rim), sim + (pre * rim + pim * rre)
            pre, pim = tab_ref[0, 6], tab_ref[0, 7]
            sre, sim = sre + (pre * car_re - pim * car_im), sim + (pre * car_im + pim * car_re)
            xre_ref[rows, :] = sre
            xim_ref[rows, :] = sim
            last = (SUBLANES - 1, SUBLANES)
            return (jnp.broadcast_to(sre[last[0]:last[1], :], sre.shape),
                    jnp.broadcast_to(sim[last[0]:last[1], :], sim.shape))

        car = lax.fori_loop(0, Tc // SUBLANES, slab, (cre_s[...], cim_s[...]))
        cre_s[...] = car[0]
        cim_s[...] = car[1]
        y = _dot(xre_ref[...], cre_ref[0], 1, 0) + _dot(xim_ref[...], cim_ref[0], 1, 0)
        y_ref[...] = y + d_ref[...] * u

    return _call(
        body, name=name, rider=rider,
        out_shape=(jax.ShapeDtypeStruct((S, SSM_WIDTH), F32),
                   jax.ShapeDtypeStruct((S, SSM_HALVES * SSM_HALF_ST), F32),
                   jax.ShapeDtypeStruct((S, SSM_HALVES * SSM_HALF_ST), F32)),
        grid=(SSM_HALVES, nc),
        in_specs=[pl.BlockSpec((Tc, SSM_HALF_CH), lambda h, c: (c, u_blk0 + h)),
                  pl.BlockSpec((1, SSM_HALF_CH, SSM_HALF_ST), lambda h, c: (h, 0, 0)),
                  pl.BlockSpec((1, SSM_HALF_CH, SSM_HALF_ST), lambda h, c: (h, 0, 0)),
                  pl.BlockSpec((1, SSM_HALF_ST, SSM_HALF_CH), lambda h, c: (h, 0, 0)),
                  pl.BlockSpec((1, SSM_HALF_ST, SSM_HALF_CH), lambda h, c: (h, 0, 0)),
                  pl.BlockSpec((1, SSM_HALF_CH), lambda h, c: (0, h)),
                  pl.BlockSpec((1, 8, SUBLANES, SSM_HALF_ST), lambda h, c: (h, 0, 0, 0))],
        out_specs=(pl.BlockSpec((Tc, SSM_HALF_CH), lambda h, c: (c, h)),
                   pl.BlockSpec((Tc, SSM_HALF_ST), lambda h, c: (c, h)),
                   pl.BlockSpec((Tc, SSM_HALF_ST), lambda h, c: (c, h))),
        scratch_shapes=[pltpu.VMEM((SUBLANES, SSM_HALF_ST), F32), pltpu.VMEM((SUBLANES, SSM_HALF_ST), F32)],
        operands=(proj, bd_re, bd_im, cd_re, cd_imneg, d_skip, tab))


def _ssm_bwd(dy, proj, x_re, x_im, bd_re, bd_im, cd_re, cd_imneg, d_skip, tab, *, name, rider=None):
    S = proj.shape[0]
    Tc = min(SSM_CHUNK, S)
    nc = S // Tc
    u_blk0 = (3 * SB_WIDTH) // SSM_HALF_CH

    def body(dy_ref, u_ref, xre_ref, xim_ref, bre_ref, bim_ref, cre_ref, cim_ref, d_ref, tab_ref,
             du_ref, dbre_ref, dbim_ref, dcre_ref, dcim_ref, dd_ref, dlre_ref, dlim_ref,
             gre_s, gim_s, cre_s, cim_s):
        c = pl.program_id(1)

        @pl.when(c == 0)
        def _():
            cre_s[...] = jnp.zeros_like(cre_s)
            cim_s[...] = jnp.zeros_like(cim_s)
            dbre_ref[...] = jnp.zeros_like(dbre_ref)
            dbim_ref[...] = jnp.zeros_like(dbim_ref)
            dcre_ref[...] = jnp.zeros_like(dcre_ref)
            dcim_ref[...] = jnp.zeros_like(dcim_ref)
            dd_ref[...] = jnp.zeros_like(dd_ref)
            dlre_ref[...] = jnp.zeros_like(dlre_ref)
            dlim_ref[...] = jnp.zeros_like(dlim_ref)

        dy = dy_ref[...]
        dyb = dy.astype(BF16)
        u = u_ref[...]
        gre_s[...] = _dot(dyb, cre_ref[0], 1, 1)
        gim_s[...] = _dot(dyb, cim_ref[0], 1, 1)
        row = lax.broadcasted_iota(jnp.int32, (SUBLANES, SSM_HALF_ST), 0)
        nslab = Tc // SUBLANES

        def slab(kk, carry):
            car_re, car_im, acc_re, acc_im = carry
            k = nslab - 1 - kk
            rows = pl.ds(pl.multiple_of(k * SUBLANES, SUBLANES), SUBLANES)
            sre = gre_s[rows, :]
            sim = gim_s[rows, :]
            for n, d in enumerate((1, 2, 4)):
                pre, pim = tab_ref[0, 2 * n], tab_ref[0, 2 * n + 1]
                rre = pltpu.roll(sre, SUBLANES - d, 0)
                rim = pltpu.roll(sim, SUBLANES - d, 0)
                sre, sim = sre + (pre * rre + pim * rim), sim + (pre * rim - pim * rre)
            pre, pim = tab_ref[0, 6], tab_ref[0, 7]
            sre, sim = sre + (pre * car_re + pim * car_im), sim + (pre * car_im - pim * car_re)
            gre_s[rows, :] = sre
            gim_s[rows, :] = sim
            nre = jnp.where(row == SUBLANES - 1, car_re, pltpu.roll(sre, SUBLANES - 1, 0))
            nim = jnp.where(row == SUBLANES - 1, car_im, pltpu.roll(sim, SUBLANES - 1, 0))
            xr = xre_ref[rows, :]
            xi = xim_ref[rows, :]
            acc_re = acc_re + (nre * xr + nim * xi)
            acc_im = acc_im + (nim * xr - nre * xi)
            return (jnp.broadcast_to(sre[0:1, :], sre.shape), jnp.broadcast_to(sim[0:1, :], sim.shape), acc_re, acc_im)

        car = lax.fori_loop(0, nslab, slab, (cre_s[...], cim_s[...], dlre_ref[0], dlim_ref[0]))
        cre_s[...] = car[0]
        cim_s[...] = car[1]
        dlre_ref[0] = car[2]
        dlim_ref[0] = car[3]
        gre = gre_s[...].astype(BF16)
        gim = gim_s[...].astype(BF16)
        ub = u.astype(BF16)
        du = _dot(gre, bre_ref[0], 1, 1) + _dot(gim, bim_ref[0], 1, 1) + d_ref[...] * dy
        du_ref[...] = du.astype(BF16)
        dbre_ref[0] += _dot(ub, gre, 0, 0)
        dbim_ref[0] += _dot(ub, gim, 0, 0)
        dcre_ref[0] += _dot(xre_ref[...], dyb, 0, 0)
        dcim_ref[0] += _dot(xim_ref[...], dyb, 0, 0)
        dd_ref[...] += jnp.sum(dy * u, axis=0, keepdims=True)

    rev = lambda c: nc - 1 - c
    return _call(
        body, name=name, rider=rider,
        out_shape=(jax.ShapeDtypeStruct((S, SSM_WIDTH), BF16),
                   jax.ShapeDtypeStruct((SSM_HALVES, SSM_HALF_CH, SSM_HALF_ST), F32),
                   jax.ShapeDtypeStruct((SSM_HALVES, SSM_HALF_CH, SSM_HALF_ST), F32),
                   jax.ShapeDtypeStruct((SSM_HALVES, SSM_HALF_ST, SSM_HALF_CH), F32),
                   jax.ShapeDtypeStruct((SSM_HALVES, SSM_HALF_ST, SSM_HALF_CH), F32),
                   jax.ShapeDtypeStruct((1, SSM_WIDTH), F32),
                   jax.ShapeDtypeStruct((SSM_HALVES, SUBLANES, SSM_HALF_ST), F32),
                   jax.ShapeDtypeStruct((SSM_HALVES, SUBLANES, SSM_HALF_ST), F32)),
        grid=(SSM_HALVES, nc),
        in_specs=[pl.BlockSpec((Tc, SSM_HALF_CH), lambda h, c: (rev(c), h)),
                  pl.BlockSpec((Tc, SSM_HALF_CH), lambda h, c: (rev(c), u_blk0 + h)),
                  pl.BlockSpec((Tc, SSM_HALF_ST), lambda h, c: (rev(c), h)),
                  pl.BlockSpec((Tc, SSM_HALF_ST), lambda h, c: (rev(c), h)),
                  pl.BlockSpec((1, SSM_HALF_CH, SSM_HALF_ST), lambda h, c: (h, 0, 0)),
                  pl.BlockSpec((1, SSM_HALF_CH, SSM_HALF_ST), lambda h, c: (h, 0, 0)),
                  pl.BlockSpec((1, SSM_HALF_ST, SSM_HALF_CH), lambda h, c: (h, 0, 0)),
                  pl.BlockSpec((1, SSM_HALF_ST, SSM_HALF_CH), lambda h, c: (h, 0, 0)),
                  pl.BlockSpec((1, SSM_HALF_CH), lambda h, c: (0, h)),
                  pl.BlockSpec((1, 8, SUBLANES, SSM_HALF_ST), lambda h, c: (h, 0, 0, 0))],
        out_specs=(pl.BlockSpec((Tc, SSM_HALF_CH), lambda h, c: (rev(c), h)),
                   pl.BlockSpec((1, SSM_HALF_CH, SSM_HALF_ST), lambda h, c: (h, 0, 0)),
                   pl.BlockSpec((1, SSM_HALF_CH, SSM_HALF_ST), lambda h, c: (h, 0, 0)),
                   pl.BlockSpec((1, SSM_HALF_ST, SSM_HALF_CH), lambda h, c: (h, 0, 0)),
                   pl.BlockSpec((1, SSM_HALF_ST, SSM_HALF_CH), lambda h, c: (h, 0, 0)),
                   pl.BlockSpec((1, SSM_HALF_CH), lambda h, c: (0, h)),
                   pl.BlockSpec((1, SUBLANES, SSM_HALF_ST), lambda h, c: (h, 0, 0)),
                   pl.BlockSpec((1, SUBLANES, SSM_HALF_ST), lambda h, c: (h, 0, 0))),
        scratch_shapes=[pltpu.VMEM((Tc, SSM_HALF_ST), F32), pltpu.VMEM((Tc, SSM_HALF_ST), F32),
                        pltpu.VMEM((SUBLANES, SSM_HALF_ST), F32), pltpu.VMEM((SUBLANES, SSM_HALF_ST), F32)],
        operands=(dy, proj, x_re, x_im, bd_re, bd_im, cd_re, cd_imneg, d_skip, tab))


def _ssm_prepare(a_re, a_im, log_dt, b_re, b_im):
    dt = jnp.exp(log_dt)[:, None]
    mag = jnp.exp(a_re * dt)
    lre = mag * jnp.cos(a_im * dt)
    lim = mag * jnp.sin(a_im * dt)
    den = a_re * a_re + a_im * a_im
    fre = ((lre - 1.0) * a_re + lim * a_im) / den
    fim = (lim * a_re - (lre - 1.0) * a_im) / den
    bbre = fre[:, :, None] * b_re - fim[:, :, None] * b_im
    bbim = fre[:, :, None] * b_im + fim[:, :, None] * b_re
    return lre, lim, bbre, bbim


def _group_eye():
    return jnp.eye(SSM_GROUPS // SSM_HALVES, dtype=F32)


def _bd_from_bbar(bbar):
    gh = SSM_GROUPS // SSM_HALVES
    b = bbar.reshape(SSM_HALVES, gh, SSM_STATE, SSM_GROUP).transpose(0, 1, 3, 2)
    out = b[:, :, :, None, :] * _group_eye()[None, :, None, :, None]
    return out.reshape(SSM_HALVES, SSM_HALF_CH, SSM_HALF_ST)


def _bbar_from_bd(dbd):
    gh = SSM_GROUPS // SSM_HALVES
    d = dbd.reshape(SSM_HALVES, gh, SSM_GROUP, gh, SSM_STATE)
    d = jnp.sum(d * _group_eye()[None, :, None, :, None], axis=3)
    return d.transpose(0, 1, 3, 2).reshape(SSM_GROUPS, SSM_STATE, SSM_GROUP)


def _cd_from_c(cmat):
    gh = SSM_GROUPS // SSM_HALVES
    c = cmat.reshape(SSM_HALVES, gh, SSM_GROUP, SSM_STATE).transpose(0, 1, 3, 2)
    out = c[:, :, :, None, :] * _group_eye()[None, :, None, :, None]
    return out.reshape(SSM_HALVES, SSM_HALF_ST, SSM_HALF_CH)


def _c_from_cd(dcd):
    gh = SSM_GROUPS // SSM_HALVES
    d = dcd.reshape(SSM_HALVES, gh, SSM_STATE, gh, SSM_GROUP)
    d = jnp.sum(d * _group_eye()[None, :, None, :, None], axis=3)
    return d.transpose(0, 1, 3, 2).reshape(SSM_GROUPS, SSM_GROUP, SSM_STATE)


def _glu_fwd(y_pre, w_glu, b_glu, *, name):
    S, W = y_pre.shape
    tr = _row_tile(S)

    def body(y_ref, w_ref, b_ref, o_ref):
        yg = _gelu(y_ref[...])
        gl = _dot(yg, w_ref[...], 1, 0) + b_ref[...]
        o_ref[...] = (yg * _sigmoid(gl)).astype(BF16)

    row = pl.BlockSpec((tr, W), lambda i: (i, 0))
    return pl.pallas_call(
        body, name=name, out_shape=jax.ShapeDtypeStruct((S, W), BF16), grid=(S // tr,),
        in_specs=[row, pl.BlockSpec((W, W), lambda i: (0, 0)), pl.BlockSpec((1, W), lambda i: (0, 0))],
        out_specs=row, compiler_params=_cparams("parallel"),
    )(y_pre, w_glu, b_glu)


def _glu_bwd(y_pre, do, w_glu, b_glu, *, name):
    S, W = y_pre.shape
    tr = _row_tile(S)

    def body(y_ref, do_ref, w_ref, b_ref, dy_ref, dw_ref, db_ref):
        i = pl.program_id(0)
        yg, dyg_dy = _gelu_and_grad(y_ref[...])
        ygb = yg.astype(BF16)
        sg = _sigmoid(_dot(ygb, w_ref[...], 1, 0) + b_ref[...])
        do = do_ref[...]
        dgl = do * yg * sg * (1.0 - sg)
        dglb = dgl.astype(BF16)
        dyg = do * sg + _dot(dglb, w_ref[...], 1, 1)
        dy_ref[...] = dyg * dyg_dy
        dw = _dot(ygb, dglb, 0, 0)
        db = jnp.sum(dgl, axis=0, keepdims=True)

        @pl.when(i == 0)
        def _():
            dw_ref[...] = dw
            db_ref[...] = db

        @pl.when(i > 0)
        def _():
            dw_ref[...] += dw
            db_ref[...] += db

    row = pl.BlockSpec((tr, W), lambda i: (i, 0))
    full = pl.BlockSpec((W, W), lambda i: (0, 0))
    vec = pl.BlockSpec((1, W), lambda i: (0, 0))
    return pl.pallas_call(
        body, name=name,
        out_shape=(jax.ShapeDtypeStruct((S, W), F32), jax.ShapeDtypeStruct((W, W), F32), jax.ShapeDtypeStruct((1, W), F32)),
        grid=(S // tr,), in_specs=[row, row, full, vec], out_specs=(row, full, vec),
        compiler_params=_cparams("arbitrary"),
    )(y_pre, do, w_glu, b_glu)


GATE_COL0 = 3 * SB_WIDTH + SSM_WIDTH


def _merge_fwd(proj, o_attn, o_ssm, w_ba, w_bs, b_gate, *, name):
    S = proj.shape[0]
    D = D_MODEL
    tr = _pick(S, (256, 128, 64, 32, 16, 8))
    gb = GATE_COL0 // D

    def body(ga_ref, gs_ref, oa_ref, os_ref, wa_ref, ws_ref, ba_ref, bs_ref, m_ref):
        pa = _dot(oa_ref[...], wa_ref[...], 1, 0)
        ps = _dot(os_ref[...], ws_ref[...], 1, 0)
        sa = _sigmoid(ga_ref[...] + ba_ref[...])
        ss = _sigmoid(gs_ref[...] + bs_ref[...])
        m_ref[...] = (sa * pa + ss * ps).astype(BF16)

    return pl.pallas_call(
        body, name=name, out_shape=jax.ShapeDtypeStruct((S, D), BF16), grid=(S // tr,),
        in_specs=[pl.BlockSpec((tr, D), lambda i: (i, gb)), pl.BlockSpec((tr, D), lambda i: (i, gb + 1)),
                  pl.BlockSpec((tr, SB_WIDTH), lambda i: (i, 0)), pl.BlockSpec((tr, SSM_WIDTH), lambda i: (i, 0)),
                  pl.BlockSpec((SB_WIDTH, D), lambda i: (0, 0)), pl.BlockSpec((SSM_WIDTH, D), lambda i: (0, 0)),
                  pl.BlockSpec((1, D), lambda i: (0, 0)), pl.BlockSpec((1, D), lambda i: (0, 1))],
        out_specs=pl.BlockSpec((tr, D), lambda i: (i, 0)),
        compiler_params=_cparams("parallel"),
    )(proj, proj, o_attn, o_ssm, w_ba, w_bs, b_gate, b_gate)


def _merge_bwd(dmerged, proj, o_attn, o_ssm, w_ba, w_bs, b_gate, *, name):
    S = proj.shape[0]
    D = D_MODEL
    tr = _pick(S, (256, 128, 64, 32, 16, 8))
    gb = GATE_COL0 // D

    def body(dm_ref, ga_ref, gs_ref, oa_ref, os_ref, wa_ref, ws_ref, ba_ref, bs_ref,
             doa_ref, dos_ref, dg_ref, db_ref, dwa_ref, dws_ref):
        i = pl.program_id(0)
        dm = dm_ref[...]
        oa = oa_ref[...]
        osm = os_ref[...]
        pa = _dot(oa, wa_ref[...], 1, 0)
        ps = _dot(osm, ws_ref[...], 1, 0)
        sa = _sigmoid(ga_ref[...] + ba_ref[...])
        ss = _sigmoid(gs_ref[...] + bs_ref[...])
        dpa = (dm * sa).astype(BF16)
        dps = (dm * ss).astype(BF16)
        dga = dm * pa * sa * (1.0 - sa)
        dgs = dm * ps * ss * (1.0 - ss)
        dg_ref[:, :D] = dga.astype(BF16)
        dg_ref[:, D:] = dgs.astype(BF16)
        doa_ref[...] = _dot(dpa, wa_ref[...], 1, 1).astype(BF16)
        dos_ref[...] = _dot(dps, ws_ref[...], 1, 1)
        dwa = _dot(oa, dpa, 0, 0)
        dws = _dot(osm, dps, 0, 0)
        dba = jnp.sum(dga, axis=0, keepdims=True)
        dbs = jnp.sum(dgs, axis=0, keepdims=True)

        @pl.when(i == 0)
        def _():
            dwa_ref[...] = dwa
            dws_ref[...] = dws
            db_ref[:, :D] = dba
            db_ref[:, D:] = dbs

        @pl.when(i > 0)
        def _():
            dwa_ref[...] += dwa
            dws_ref[...] += dws
            db_ref[:, :D] += dba
            db_ref[:, D:] += dbs

    rowD = pl.BlockSpec((tr, D), lambda i: (i, 0))
    wspec = pl.BlockSpec((SB_WIDTH, D), lambda i: (0, 0))
    return pl.pallas_call(
        body, name=name,
        out_shape=(jax.ShapeDtypeStruct((S, SB_WIDTH), BF16), jax.ShapeDtypeStruct((S, SSM_WIDTH), F32),
                   jax.ShapeDtypeStruct((S, 2 * D), BF16), jax.ShapeDtypeStruct((1, 2 * D), F32),
                   jax.ShapeDtypeStruct((SB_WIDTH, D), F32), jax.ShapeDtypeStruct((SSM_WIDTH, D), F32)),
        grid=(S // tr,),
        in_specs=[rowD, pl.BlockSpec((tr, D), lambda i: (i, gb)), pl.BlockSpec((tr, D), lambda i: (i, gb + 1)),
                  pl.BlockSpec((tr, SB_WIDTH), lambda i: (i, 0)), pl.BlockSpec((tr, SSM_WIDTH), lambda i: (i, 0)),
                  wspec, wspec, pl.BlockSpec((1, D), lambda i: (0, 0)), pl.BlockSpec((1, D), lambda i: (0, 1))],
        out_specs=(pl.BlockSpec((tr, SB_WIDTH), lambda i: (i, 0)), pl.BlockSpec((tr, SSM_WIDTH), lambda i: (i, 0)),
                   pl.BlockSpec((tr, 2 * D), lambda i: (i, 0)), pl.BlockSpec((1, 2 * D), lambda i: (0, 0)),
                   wspec, wspec),
        compiler_params=_cparams("arbitrary"),
    )(dmerged, proj, proj, o_attn, o_ssm, w_ba, w_bs, b_gate, b_gate)


def _xattn_probs(q, k, h):
    cols = slice(h * XA_HEAD_DIM, (h + 1) * XA_HEAD_DIM)
    s = _dot(q[:, cols], k[:, cols], 1, 1) * (XA_HEAD_DIM ** -0.5)
    s = s - jnp.max(s, axis=-1, keepdims=True)
    e = jnp.exp(s)
    return e / jnp.sum(e, axis=-1, keepdims=True), cols


def _xattn_fwd(q2, k2, v2, *, name):
    S, D = q2.shape
    M = k2.shape[0]
    tr = _row_tile(S)

    def body(q_ref, k_ref, v_ref, o_ref):
        q = q_ref[...]
        k = k_ref[...]
        v = v_ref[...]
        for h in range(XA_HEADS):
            p, cols = _xattn_probs(q, k, h)
            o_ref[:, cols] = _dot(p, v[:, cols], 1, 0).astype(BF16)

    row = pl.BlockSpec((tr, D), lambda i: (i, 0))
    memb = pl.BlockSpec((M, D), lambda i: (0, 0))
    return pl.pallas_call(
        body, name=name, out_shape=jax.ShapeDtypeStruct((S, D), BF16), grid=(S // tr,),
        in_specs=[row, memb, memb], out_specs=row, compiler_params=_cparams("parallel"),
    )(q2, k2, v2)


def _xattn_bwd(q2, k2, v2, do2, *, name):
    S, D = q2.shape
    M = k2.shape[0]
    tr = _row_tile(S)
    scale = XA_HEAD_DIM ** -0.5

    def body(q_ref, k_ref, v_ref, do_ref, dq_ref, dk_ref, dv_ref):
        i = pl.program_id(0)

        @pl.when(i == 0)
        def _():
            dk_ref[...] = jnp.zeros_like(dk_ref)
            dv_ref[...] = jnp.zeros_like(dv_ref)

        q = q_ref[...]
        k = k_ref[...]
        v = v_ref[...]
        do = do_ref[...]
        for h in range(XA_HEADS):
            p, cols = _xattn_probs(q, k, h)
            dp = _dot(do[:, cols], v[:, cols], 1, 1)
            ds = (p * (dp - jnp.sum(dp * p, axis=-1, keepdims=True)) * scale).astype(BF16)
            dq_ref[:, cols] = _dot(ds, k[:, cols], 1, 0).astype(BF16)
            dk_ref[:, cols] += _dot(ds, q[:, cols], 0, 0)
            dv_ref[:, cols] += _dot(p, do[:, cols], 0, 0)

    row = pl.BlockSpec((tr, D), lambda i: (i, 0))
    memb = pl.BlockSpec((M, D), lambda i: (0, 0))
    return pl.pallas_call(
        body, name=name,
        out_shape=(jax.ShapeDtypeStruct((S, D), BF16), jax.ShapeDtypeStruct((M, D), F32), jax.ShapeDtypeStruct((M, D), F32)),
        grid=(S // tr,), in_specs=[row, memb, memb, row], out_specs=(row, memb, memb),
        compiler_params=_cparams("arbitrary"),
    )(q2, k2, v2, do2)


CONV_ROWS = 64
CONV_ROWS_FWD = 256


def _chunk(ref, c, rows):
    return ref[pl.ds(pl.multiple_of(c * rows, rows), rows), :]


def _rows_before(ref, c, rows):
    t0 = pl.multiple_of(jnp.maximum(c * rows - SUBLANES, 0), SUBLANES)
    return jnp.where(c > 0, ref[pl.ds(t0, SUBLANES), :], 0.0)


def _rows_after(ref, c, rows, n_chunks):
    t0 = pl.multiple_of(jnp.minimum((c + 1) * rows, n_chunks * rows - SUBLANES), SUBLANES)
    return jnp.where(c < n_chunks - 1, ref[pl.ds(t0, SUBLANES), :], 0.0)


def _shift_down(cur, before, d):
    out = pltpu.roll(cur, d, 0)
    r = lax.broadcasted_iota(jnp.int32, cur.shape, 0)
    for e in range(d):
        out = jnp.where(r == e, before[SUBLANES - d + e:SUBLANES - d + e + 1, :], out)
    return out


def _shift_up(cur, after, d):
    rows = cur.shape[0]
    out = pltpu.roll(cur, rows - d, 0)
    r = lax.broadcasted_iota(jnp.int32, cur.shape, 0)
    for e in range(d):
        out = jnp.where(r == rows - d + e, after[e:e + 1, :], out)
    return out


def _conv3(cur, before, w_ref, b_ref):
    return (w_ref[2:3, :] * cur + w_ref[1:2, :] * _shift_down(cur, before, 1)
            + w_ref[0:1, :] * _shift_down(cur, before, 2) + b_ref[...])


def _convgate_fwd(up_g, up_v, conv_w, conv_b, *, name):
    S, H = up_g.shape
    nb = H // LANES
    R = min(CONV_ROWS_FWD, S)
    n_chunks = S // R

    def body(g_ref, v_ref, wg_ref, wv_ref, bg_ref, bv_ref, a_ref):
        def chunk(c, _):
            cg = _conv3(_chunk(g_ref, c, R), _rows_before(g_ref, c, R), wg_ref, bg_ref)
            cv = _conv3(_chunk(v_ref, c, R), _rows_before(v_ref, c, R), wv_ref, bv_ref)
            a_ref[pl.ds(pl.multiple_of(c * R, R), R), :] = (_gelu(cg) * cv).astype(BF16)
            return 0

        lax.fori_loop(0, n_chunks, chunk, 0)

    col = lambda off: pl.BlockSpec((S, LANES), lambda j: (0, off + j))
    wcol = lambda off: pl.BlockSpec((3, LANES), lambda j: (0, off + j))
    bcol = lambda off: pl.BlockSpec((1, LANES), lambda j: (0, off + j))
    return pl.pallas_call(
        body, name=name, out_shape=jax.ShapeDtypeStruct((S, H), BF16), grid=(nb,),
        in_specs=[col(0), col(0), wcol(0), wcol(nb), bcol(0), bcol(nb)],
        out_specs=col(0), compiler_params=_cparams("parallel"),
    )(up_g, up_v, conv_w, conv_w, conv_b, conv_b)


def _convgate_bwd(up_g, up_v, da, conv_w, conv_b, *, name):
    S, H = up_g.shape
    nb = H // LANES
    R = min(CONV_ROWS, S)
    n_chunks = S // R

    def fold(a):
        return sum(a[r:r + SUBLANES] for r in range(0, a.shape[0], SUBLANES))

    def body(g_ref, v_ref, da_ref, wg_ref, wv_ref, bg_ref, bv_ref,
             dug_ref, duv_ref, dwg_ref, dwv_ref, dbg_ref, dbv_ref, dcg_s, dcv_s):
        def first_pass(c, acc):
            rows = pl.ds(pl.multiple_of(c * R, R), R)
            ug, uv = _chunk(g_ref, c, R), _chunk(v_ref, c, R)
            bg, bv = _rows_before(g_ref, c, R), _rows_before(v_ref, c, R)
            cg = _conv3(ug, bg, wg_ref, bg_ref)
            cv = _conv3(uv, bv, wv_ref, bv_ref)
            da = da_ref[rows, :]
            gl, dgl = _gelu_and_grad(cg)
            dcg = da * cv * dgl
            dcv = da * gl
            dcg_s[rows, :] = dcg
            dcv_s[rows, :] = dcv
            new = []
            for dc, u, before in ((dcg, ug, bg), (dcv, uv, bv)):
                new += [fold(dc * _shift_down(u, before, 2)), fold(dc * _shift_down(u, before, 1)), fold(dc * u), fold(dc)]
            return tuple(a + n for a, n in zip(acc, new))

        zero = jnp.zeros((SUBLANES, LANES), F32)
        acc = lax.fori_loop(0, n_chunks, first_pass, (zero,) * 8)
        total = [jnp.sum(a, axis=0, keepdims=True) for a in acc]
        for k, (dw_ref, db_ref) in enumerate(((dwg_ref, dbg_ref), (dwv_ref, dbv_ref))):
            dw_ref[0:1, :] = total[4 * k]
            dw_ref[1:2, :] = total[4 * k + 1]
            dw_ref[2:3, :] = total[4 * k + 2]
            db_ref[...] = total[4 * k + 3]

        def second_pass(c, _):
            rows = pl.ds(pl.multiple_of(c * R, R), R)
            for dc_s, w_ref, du_ref in ((dcg_s, wg_ref, dug_ref), (dcv_s, wv_ref, duv_ref)):
                cur, after = _chunk(dc_s, c, R), _rows_after(dc_s, c, R, n_chunks)
                du = w_ref[2:3, :] * cur + w_ref[1:2, :] * _shift_up(cur, after, 1) + w_ref[0:1, :] * _shift_up(cur, after, 2)
                du_ref[rows, :] = du.astype(BF16)
            return 0

        lax.fori_loop(0, n_chunks, second_pass, 0)

    col = lambda off: pl.BlockSpec((S, LANES), lambda j: (0, off + j))
    wcol = lambda off: pl.BlockSpec((3, LANES), lambda j: (0, off + j))
    bcol = lambda off: pl.BlockSpec((1, LANES), lambda j: (0, off + j))
    return pl.pallas_call(
        body, name=name,
        out_shape=(jax.ShapeDtypeStruct((S, H), BF16), jax.ShapeDtypeStruct((S, H), BF16),
                   jax.ShapeDtypeStruct((3, H), F32), jax.ShapeDtypeStruct((3, H), F32),
                   jax.ShapeDtypeStruct((1, H), F32), jax.ShapeDtypeStruct((1, H), F32)),
        grid=(nb,),
        in_specs=[col(0), col(0), col(0), wcol(0), wcol(nb), bcol(0), bcol(nb)],
        out_specs=(col(0), col(0), wcol(0), wcol(0), bcol(0), bcol(0)),
        scratch_shapes=[pltpu.VMEM((S, LANES), F32), pltpu.VMEM((S, LANES), F32)],
        compiler_params=_cparams("parallel"),
    )(up_g, up_v, da, conv_w, conv_w, conv_b, conv_b)


def _local_step(x, mem, target, w_in, late_wire, P, core):
    mm = _matmul
    h1, (w_in,) = _rms_fwd(x, P["norm_mix_pre"], name="rms_mix_pre", rider=_fill_xy([w_in]))
    w_in, = _fill_c([w_in]).run(name="gather_in_c")
    w_in = w_in.reshape((N_DEV,) + w_in.shape[2:])
    n_mid = len(LATE) - len(REDUCE_FFN)
    proj, wire_mid = mm(h1, w_in, name="mm_in", rider=_fill_xy(late_wire[:n_mid]))
    (o_attn, sb_tot, sb_first), wires = _sb_fwd(
        proj, name="sb_fwd", rider=_Exchange.join(_fill_c(wire_mid), _fill_xy(late_wire[n_mid:])))
    wire_mid, wire_ffn = wires[:n_mid], wires[n_mid:]

    ssm_prep = lambda *a: _ssm_prepare(*a)
    (lam_re, lam_im, bb_re, bb_im), prep_vjp = jax.vjp(
        ssm_prep, P["ssm_a_re"], P["ssm_a_im"], P["ssm_log_dt"], P["ssm_b_re"], P["ssm_b_im"])
    tab_f, tab_b = _ssm_tables(lam_re, lam_im)
    bd_re = _bd_from_bbar(bb_re).astype(BF16)
    bd_im = _bd_from_bbar(bb_im).astype(BF16)
    cd_re = _cd_from_c(P["ssm_c_re"]).astype(BF16)
    cd_imneg = _cd_from_c(-P["ssm_c_im"]).astype(BF16)
    (y_pre, x_re, x_im), wire_ffn = _ssm_fwd(proj, bd_re, bd_im, cd_re, cd_imneg, P["ssm_d"], tab_f,
                                             name="ssm_fwd", rider=_fill_c(wire_ffn))
    W = _weights_from_wire(dict(zip(LATE, list(wire_mid) + list(wire_ffn))))
    W["w_in"] = w_in
    o_ssm = _glu_fwd(y_pre, W["ssm_w_glu"], P["ssm_b_glu"], name="glu_fwd")

    merged = _merge_fwd(proj, o_attn, o_ssm, W["w_branch_attn"], W["w_branch_ssm"], P["b_gate"], name="merge_fwd")
    mo = mm(merged, W["w_out"], name="mm_out")
    x1, h2 = _resnorm_norm(x, mo, P["norm_mix_post"], P["norm_xa_pre"], name="resnorm_1")

    mem_n = _rms_fwd(mem, P["norm_mem"], name="rms_mem")
    q2 = mm(h2, W["xa_wq"], out_dtype=BF16, name="mm_xq")
    k2 = mm(mem_n, W["xa_wk"], out_dtype=BF16, name="mm_xk")
    v2 = mm(mem_n, W["xa_wv"], out_dtype=BF16, name="mm_xv")
    o2 = _xattn_fwd(q2, k2, v2, name="xattn_fwd")
    xa = mm(o2, W["xa_wo"], name="mm_xo")
    x2, h3 = _resnorm_norm(x1, xa, P["norm_xa_post"], P["norm_ffn_pre"], name="resnorm_2")

    half = N_DEV // 2
    up_g = mm(h3, W["ffn_w_up"], n_blocks=half, name="mm_up_g")
    up_v = mm(h3, W["ffn_w_up"], b_block0=half, name="mm_up_v")
    act = _convgate_fwd(up_g, up_v, W["ffn_conv_w"], P["ffn_conv_b"], name="convgate_fwd")
    f = mm(act, W["ffn_w_down"], name="mm_down")
    loss, dy, df, dg_ffn_post = _final_loss(x2, f, P["norm_ffn_post"], target, name="final_loss")

    G = {"norm_ffn_post": dg_ffn_post}
    dact = mm(df, W["ffn_w_down"], tb=True, name="mm_down_dx")
    G["ffn_w_down"] = mm(act, df, ta=True, name="mm_down_dw")
    dug, duv, dwg, dwv, dbg, dbv = _convgate_bwd(up_g, up_v, dact, W["ffn_conv_w"], P["ffn_conv_b"], name="convgate_bwd")
    G["ffn_conv_w"] = jnp.concatenate([dwg, dwv], axis=1)
    G["ffn_conv_b"] = jnp.concatenate([dbg, dbv], axis=1)
    dh3 = mm(dug, W["ffn_w_up"], tb=True, n_blocks=half, name="mm_up_g_dx")
    dh3 = mm(duv, W["ffn_w_up"], tb=True, b_block0=half, acc_in=dh3, name="mm_up_v_dx")
    dw_up = mm(h3, dug, ta=True, out_into=lax.empty(W["ffn_w_up"].shape, F32), name="mm_up_g_dw")
    G["ffn_w_up"] = mm(h3, duv, ta=True, out_into=dw_up, out_block0=half, name="mm_up_v_dw")
    blocks = {n: _grad_blocks(n, G[n]) for n in REDUCE_FFN}
    (dx2, dxa, G["norm_ffn_pre"], G["norm_xa_post"]), from_core = _norm_bwd_pair(
        dy, dh3, x2, P["norm_ffn_pre"], xa, P["norm_xa_post"], name="norm_bwd_3",
        rider=_send_c([blocks[n] for n in REDUCE_FFN]))
    pair = {n: _pair_sum(blocks[n], r, core, name="pair_sum_" + n) for n, r in zip(REDUCE_FFN, from_core)}

    G["xa_wo"] = mm(o2, dxa, ta=True, name="mm_xo_dw")
    do2 = mm(dxa, W["xa_wo"], tb=True, out_dtype=BF16, name="mm_xo_dx")
    dq2, dk2, dv2 = _xattn_bwd(q2, k2, v2, do2, name="xattn_bwd")
    G["xa_wq"] = mm(h2, dq2, ta=True, name="mm_xq_dw")
    dh2 = mm(dq2, W["xa_wq"], tb=True, name="mm_xq_dx")
    G["xa_wk"] = mm(mem_n, dk2, ta=True, name="mm_xk_dw")
    G["xa_wv"] = mm(mem_n, dv2, ta=True, name="mm_xv_dw")
    dmem_n = jnp.concatenate([dk2, dv2], axis=1)
    wkv = jnp.concatenate([W["xa_wk"], W["xa_wv"]], axis=1)
    dmem = mm(dmem_n, wkv, tb=True, name="mm_xkv_dx")
    _, G["norm_mem"] = _norm_bwd_single(None, dmem, mem, P["norm_mem"], name="norm_bwd_mem")
    (dx1, dmo, G["norm_xa_pre"], G["norm_mix_post"]), _ = _norm_bwd_pair(
        dx2, dh2, x1, P["norm_xa_pre"], mo, P["norm_mix_post"], name="norm_bwd_2")

    G["w_out"] = mm(merged, dmo, ta=True, name="mm_out_dw")
    dmerged = mm(dmo, W["w_out"], tb=True, name="mm_out_dx")
    do_attn, do_ssm, dgate, G["b_gate"], G["w_branch_attn"], G["w_branch_ssm"] = _merge_bwd(
        dmerged, proj, o_attn, o_ssm, W["w_branch_attn"], W["w_branch_ssm"], P["b_gate"], name="merge_bwd")
    dy_pre, G["ssm_w_glu"], G["ssm_b_glu"] = _glu_bwd(y_pre, do_ssm, W["ssm_w_glu"], P["ssm_b_glu"], name="glu_bwd")
    blocks.update({n: _grad_blocks(n, G[n]) for n in REDUCE_MID})
    (du, dbd_re, dbd_im, dcd_re, dcd_imneg, G["ssm_d"], dl_re, dl_im), brought = _ssm_bwd(
        dy_pre, proj, x_re, x_im, bd_re, bd_im, cd_re, cd_imneg, P["ssm_d"], tab_b, name="ssm_bwd",
        rider=_Exchange.join(_send_c([blocks[n] for n in REDUCE_MID]), _scatter_xy([pair[n] for n in REDUCE_FFN])))
    from_core, from_chips = brought[:len(REDUCE_MID)], brought[len(REDUCE_MID):]
    reduced = {n: (pair[n], parts) for n, parts in zip(REDUCE_FFN, from_chips)}
    pair.update({n: _pair_sum(blocks[n], r, core, name="pair_sum_" + n) for n, r in zip(REDUCE_MID, from_core)})
    G["ssm_c_re"] = _c_from_cd(dcd_re)
    G["ssm_c_im"] = -_c_from_cd(dcd_imneg)
    dlam_re = jnp.sum(dl_re, axis=1).reshape(SSM_GROUPS, SSM_STATE)
    dlam_im = jnp.sum(dl_im, axis=1).reshape(SSM_GROUPS, SSM_STATE)
    (G["ssm_a_re"], G["ssm_a_im"], G["ssm_log_dt"], G["ssm_b_re"], G["ssm_b_im"]) = prep_vjp(
        (dlam_re, dlam_im, _bbar_from_bd(dbd_re), _bbar_from_bd(dbd_im)))
    G["ffn_conv_b"] = G["ffn_conv_b"].reshape(N_DEV, FF_LOCAL_PAD)[:, :FF_LOCAL]
    small = [G[n].reshape(SMALL_SHAPE[n]) for n in SMALL_EARLY]
    (dq, dk, dv), brought = _sb_bwd(
        proj, sb_tot, sb_first, do_attn, name="sb_bwd",
        rider=_Exchange.join(_scatter_xy([pair[n] for n in REDUCE_MID]), _gather_xy_from(small)))
    from_chips, small = brought[:len(REDUCE_MID)], brought[len(REDUCE_MID):]
    reduced.update({n: (pair[n], parts) for n, parts in zip(REDUCE_MID, from_chips)})
    dproj = jnp.concatenate([dq, dk, dv, du, dgate], axis=1)
    G["w_in"], small = mm(h1, dproj, ta=True, out_cb=W["w_in"].shape[2], name="mm_in_dw", rider=_fill_c(small))
    g_in = _grad_blocks("w_in", G["w_in"])
    dh1, (from_core,) = mm(dproj, W["w_in"], tb=True, name="mm_in_dx", rider=_send_c([g_in]))
    pair_in = _pair_sum(g_in, from_core, core, name="pair_sum_w_in")
    (grad_x, dg_pre), (from_chips,) = _norm_bwd_single(dx1, dh1, x, P["norm_mix_pre"], name="norm_bwd_1",
                                                       rider=_scatter_xy([pair_in]))
    reduced["w_in"] = (pair_in, from_chips)
    last, = _gather_all([dg_pre]).run(name="gather_g_last")
    parts = dict(zip(SMALL_EARLY, small))
    parts["norm_mix_pre"] = last
    return loss, grad_x, parts, reduced


MESH = pl.DeviceIdType.MESH
_HBM = pl.BlockSpec(memory_space=pl.ANY)
N_XY = 4
N_XY_PEERS = 3


def _xy_peers(x, y):
    return [(1 - x, y), (x, 1 - y), (1 - x, 1 - y)]


class _Exchange:
    def __init__(self, arrays, out_shapes, plan, n_copies, alias):
        self.arrays = list(arrays)
        self.out_shapes = list(out_shapes)
        self.plan = plan
        self.n_copies = n_copies
        self.alias = list(alias) if isinstance(alias, (list, tuple)) else [alias] * len(self.arrays)

    @property
    def n(self):
        return len(self.arrays)

    def aliases(self, first_in, first_out):
        return {first_in + k: first_out + k for k in range(self.n) if self.alias[k]}

    @staticmethod
    def join(a, b):
        def plan(k, src, dst, x, y, c):
            return a.plan(k, src, dst, x, y, c) if k < a.n else b.plan(k - a.n, src, dst, x, y, c)

        return _Exchange(a.arrays + b.arrays, a.out_shapes + b.out_shapes, plan, max(a.n_copies, b.n_copies),
                         a.alias + b.alias)

    def sems(self):
        shape = (self.n, self.n_copies)
        return [pltpu.SemaphoreType.DMA(shape), pltpu.SemaphoreType.DMA(shape)]

    def _copies(self, ins, outs, send_sems, recv_sems):
        x, y, c = lax.axis_index("x"), lax.axis_index("y"), lax.axis_index("c")
        sends, lands, own = [], [], []
        for k in range(self.n):
            for j, (src, dst, dev, land) in enumerate(self.plan(k, ins[k], outs[k], x, y, c)):
                if dev is None:
                    own.append(pltpu.make_async_copy(src, dst, send_sems.at[k, j]))
                    continue
                sems = dict(send_sem=send_sems.at[k, j], recv_sem=recv_sems.at[k, j], device_id=dev, device_id_type=MESH)
                sends.append(pltpu.make_async_remote_copy(src_ref=src, dst_ref=dst, **sems))
                lands.append(pltpu.make_async_remote_copy(src_ref=src, dst_ref=land, **sems))
        return sends, lands, own

    def start(self, ins, outs, send_sems, recv_sems):
        sends, _, own = self._copies(ins, outs, send_sems, recv_sems)
        for cp in own + sends:
            cp.start()

    def finish(self, ins, outs, send_sems, recv_sems):
        sends, lands, own = self._copies(ins, outs, send_sems, recv_sems)
        for cp in lands:
            cp.wait_recv()
        for cp in sends:
            cp.wait_send()
        for cp in own:
            cp.wait()

    def run(self, *, name):
        n = self.n

        def body(*refs):
            parts = (refs[:n], refs[n:2 * n], refs[2 * n], refs[2 * n + 1])
            self.start(*parts)
            self.finish(*parts)

        return pl.pallas_call(
            body, name=name, out_shape=tuple(self.out_shapes),
            in_specs=[_HBM] * n, out_specs=tuple([_HBM] * n),
            input_output_aliases=self.aliases(0, 0),
            scratch_shapes=self.sems(),
        )(*self.arrays)


def _call(host_body, *, name, grid, in_specs, out_specs, out_shape, scratch_shapes, operands, rider=None):
    out_specs, out_shape = tuple(out_specs), tuple(out_shape)
    if rider is None:
        res = pl.pallas_call(
            host_body, name=name, grid=grid, in_specs=list(in_specs), out_specs=out_specs, out_shape=out_shape,
            scratch_shapes=list(scratch_shapes), compiler_params=_cparams(*["arbitrary"] * len(grid)),
        )(*operands)
        return tuple(res), None
    n, n_in, n_out, n_scr = rider.n, len(in_specs), len(out_specs), len(scratch_shapes)

    def body(*refs):
        pos = [0]

        def take(count):
            pos[0] += count
            return refs[pos[0] - count:pos[0]]

        h_in, r_in, h_out, r_out, h_scr = take(n_in), take(n), take(n_out), take(n), take(n_scr)
        send_sems, recv_sems = take(2)
        ids = [pl.program_id(a) for a in range(len(grid))]
        first = functools.reduce(jnp.logical_and, [i == 0 for i in ids])
        last = functools.reduce(jnp.logical_and, [i == g - 1 for i, g in zip(ids, grid)])

        @pl.when(first)
        def _():
            rider.start(r_in, r_out, send_sems, recv_sems)

        host_body(*h_in, *h_out, *h_scr)

        @pl.when(last)
        def _():
            rider.finish(r_in, r_out, send_sems, recv_sems)

    res = pl.pallas_call(
        body, name=name, grid=grid,
        in_specs=list(in_specs) + [_HBM] * n, out_specs=out_specs + tuple([_HBM] * n),
        out_shape=out_shape + tuple(rider.out_shapes),
        input_output_aliases=rider.aliases(n_in, n_out),
        scratch_shapes=list(scratch_shapes) + rider.sems(),
        compiler_params=_cparams(*["arbitrary"] * len(grid)),
    )(*operands, *rider.arrays)
    return tuple(res[:n_out]), list(res[n_out:])


def _same(arrays):
    return [jax.ShapeDtypeStruct(a.shape, a.dtype) for a in arrays]


def _fill_xy(bufs):
    def plan(k, src, dst, x, y, c):
        mine = 2 * x + y
        return [(src.at[mine, c], dst.at[mine, c], (px, py, c), dst.at[2 * px + py, c]) for px, py in _xy_peers(x, y)]

    return _Exchange(bufs, _same(bufs), plan, N_XY_PEERS, alias=True)


def _fill_c(bufs):
    def plan(k, src, dst, x, y, c):
        return [(src.at[:, c], dst.at[:, c], (x, y, 1 - c), dst.at[:, 1 - c])]

    return _Exchange(bufs, _same(bufs), plan, 1, alias=True)


def _slots(arrays):
    return [jax.ShapeDtypeStruct((N_XY, 2) + a.shape, a.dtype) for a in arrays]


def _gather_xy_from(srcs):
    def plan(k, src, dst, x, y, c):
        mine = 2 * x + y
        return ([(src, dst.at[mine, c], None, None)]
                + [(src, dst.at[mine, c], (px, py, c), dst.at[2 * px + py, c]) for px, py in _xy_peers(x, y)])

    return _Exchange(srcs, _slots(srcs), plan, 1 + N_XY_PEERS, alias=False)


def _gather_all(srcs):
    def plan(k, src, dst, x, y, c):
        mine = 2 * x + y
        out = [(src, dst.at[mine, c], None, None)]
        for fx, fy, fc in [(a, b, e) for a in (0, 1) for b in (0, 1) for e in (0, 1)][1:]:
            px, py, pc = (1 - x) if fx else x, (1 - y) if fy else y, (1 - c) if fc else c
            out.append((src, dst.at[mine, c], (px, py, pc), dst.at[2 * px + py, pc]))
        return out

    return _Exchange(srcs, _slots(srcs), plan, N_DEV, alias=False)


def _send_c(srcs):
    def plan(k, src, dst, x, y, c):
        return [(src.at[:, 1 - c], dst, (x, y, 1 - c), dst)]

    outs = [jax.ShapeDtypeStruct(a.shape[:1] + a.shape[2:], a.dtype) for a in srcs]
    return _Exchange(srcs, outs, plan, 1, alias=False)


def _scatter_xy(srcs):
    def plan(k, src, dst, x, y, c):
        return [(src.at[2 * px + py], dst.at[j], (px, py, c), dst.at[j]) for j, (px, py) in enumerate(_xy_peers(x, y))]

    outs = [jax.ShapeDtypeStruct((N_XY_PEERS,) + a.shape[1:], a.dtype) for a in srcs]
    return _Exchange(srcs, outs, plan, N_XY_PEERS, alias=False)


WIRE_DTYPE = BF16


def _pair_sum(g8, recv, core, *, name):
    n, _, R, C = g8.shape
    tr = _pick(R, (128, 64, 32, 16, 8))

    def body(core_ref, a_ref, b_ref, o_ref):
        o_ref[...] = (a_ref[0] + b_ref[...]).astype(WIRE_DTYPE)

    return pl.pallas_call(
        body, name=name, out_shape=jax.ShapeDtypeStruct((n, R, C), WIRE_DTYPE),
        grid_spec=pltpu.PrefetchScalarGridSpec(
            num_scalar_prefetch=1, grid=(n, R // tr),
            in_specs=[pl.BlockSpec((1, 1, tr, C), lambda s, i, core_ref: (s, core_ref[0], i, 0)),
                      pl.BlockSpec((1, tr, C), lambda s, i, core_ref: (s, i, 0))],
            out_specs=pl.BlockSpec((1, tr, C), lambda s, i, core_ref: (s, i, 0))),
        compiler_params=_cparams("parallel", "parallel"),
    )(core, g8, recv)


def _adamw_math(w, g, m, v):
    m = ADAM_B1 * m + (1.0 - ADAM_B1) * g
    v = ADAM_B2 * v + (1.0 - ADAM_B2) * (g * g)
    m_hat = m / (1.0 - ADAM_B1 ** ADAM_STEP)
    v_hat = v / (1.0 - ADAM_B2 ** ADAM_STEP)
    delta = -ADAM_LR * (m_hat / (jnp.sqrt(v_hat) + ADAM_EPS) + ADAM_WD * w)
    return delta, m, v


def _reduce_adamw(parts, w, m, v, *, own, own_slot, name):
    n, R, C = parts.shape
    tr = _pick(R, (128, 64, 32, 16, 8))

    def body(_, own_ref, parts_ref, w_ref, m_ref, v_ref, g_ref, d_ref, nm_ref, nv_ref):
        g = own_ref[0].astype(F32)
        for k in range(n):
            g = g + parts_ref[k].astype(F32)
        g_ref[...] = g
        d_ref[...], nm_ref[...], nv_ref[...] = _adamw_math(w_ref[...], g, m_ref[...], v_ref[...])

    out = jax.ShapeDtypeStruct((R, C), F32)
    row = pl.BlockSpec((tr, C), lambda i, s: (i, 0))
    return pl.pallas_call(
        body, name=name, out_shape=(out, out, out, out),
        grid_spec=pltpu.PrefetchScalarGridSpec(
            num_scalar_prefetch=1, grid=(R // tr,),
            in_specs=[pl.BlockSpec((1, tr, C), lambda i, s: (s[0], i, 0)),
                      pl.BlockSpec((n, tr, C), lambda i, s: (0, i, 0)), row, row, row],
            out_specs=(row, row, row, row)),
        compiler_params=_cparams("parallel"),
    )(own_slot, own, parts, w, m, v)


SHARDED = (("w_in", (1024, 4096), 1), ("ssm_w_glu", (512, 512), 0), ("w_branch_attn", (512, 1024), 1),
           ("w_branch_ssm", (512, 1024), 1), ("w_out", (1024, 1024), 0), ("xa_wq", (1024, 1024), 0),
           ("xa_wk", (1024, 1024), 0), ("xa_wv", (1024, 1024), 0), ("xa_wo", (1024, 1024), 0),
           ("ffn_w_up", (1024, 5632), 1), ("ffn_conv_w", (3, 5632), 1), ("ffn_w_down", (2816, 1024), 0))
REPLICATED = (("norm_mix_pre", (1024,)), ("norm_mix_post", (1024,)), ("b_gate", (2048,)), ("ssm_a_re", (32, 64)),
              ("ssm_a_im", (32, 64)), ("ssm_log_dt", (32,)), ("ssm_b_re", (32, 64, 16)), ("ssm_b_im", (32, 64, 16)),
              ("ssm_c_re", (32, 16, 64)), ("ssm_c_im", (32, 16, 64)), ("ssm_d", (512,)), ("ssm_b_glu", (512,)),
              ("norm_xa_pre", (1024,)), ("norm_xa_post", (1024,)), ("norm_mem", (1024,)), ("norm_ffn_pre", (1024,)),
              ("norm_ffn_post", (1024,)), ("ffn_conv_b", (5632,)))
PARAM_ORDER = ("norm_mix_pre", "norm_mix_post", "w_in", "b_gate", "ssm_a_re", "ssm_a_im", "ssm_log_dt", "ssm_b_re",
               "ssm_b_im", "ssm_c_re", "ssm_c_im", "ssm_d", "ssm_w_glu", "ssm_b_glu", "w_branch_attn", "w_branch_ssm",
               "w_out", "norm_xa_pre", "norm_xa_post", "norm_mem", "xa_wq", "xa_wk", "xa_wv", "xa_wo", "norm_ffn_pre",
               "norm_ffn_post", "ffn_w_up", "ffn_conv_w", "ffn_conv_b", "ffn_w_down")
FF_LOCAL = 2 * D_FF // N_DEV
FF_LOCAL_PAD = 768
FF_PAD = (N_DEV // 2) * FF_LOCAL_PAD


def _local_shape(shape, axis):
    return tuple(s // N_DEV if a == axis else s for a, s in enumerate(shape))


def _pad_cols(a, width):
    return jnp.pad(a, [(0, 0)] * (a.ndim - 1) + [(0, width - a.shape[-1])])


def _blocks_to_cols(a8):
    return a8.transpose(1, 0, 2).reshape(a8.shape[1], N_DEV * a8.shape[2])


def _cols_to_blocks(a, cb):
    return a.reshape(a.shape[0], N_DEV, cb).transpose(1, 0, 2)


FF_PADDED = ("ffn_w_up", "ffn_conv_w")
LATE = tuple(n for n, _, _ in SHARDED if n != "w_in")
REDUCE_FFN = ("ffn_w_up", "ffn_conv_w", "ffn_w_down")
REDUCE_MID = ("xa_wo", "xa_wq", "xa_wk", "xa_wv", "w_out", "w_branch_attn", "w_branch_ssm", "ssm_w_glu")
SHARD_AXIS = {n: ax for n, _, ax in SHARDED}
FULL_SHAPE = {n: s for n, s, _ in SHARDED}


def _as_local(n, a):
    return _pad_cols(a, FF_LOCAL_PAD) if n in FF_PADDED else a


def _weights_from_wire(wire):
    full = {n: b.reshape((N_DEV,) + b.shape[2:]) for n, b in wire.items()}
    W = {n: a.reshape(FULL_SHAPE[n]) if SHARD_AXIS[n] == 0 else a for n, a in full.items()}
    for n in ("w_branch_attn", "w_branch_ssm", "ffn_conv_w"):
        W[n] = _blocks_to_cols(full[n])
    W["ffn_w_down"] = jnp.pad(W["ffn_w_down"].reshape(N_DEV // 2, FF_LOCAL, D_MODEL),
                              ((0, 0), (0, FF_LOCAL_PAD - FF_LOCAL), (0, 0))).reshape(FF_PAD, D_MODEL)
    return W


def _grad_blocks(n, g):
    if n in ("w_branch_attn", "w_branch_ssm"):
        g = _cols_to_blocks(g, D_MODEL // N_DEV)
    elif n == "ffn_conv_w":
        g = _cols_to_blocks(g, FF_LOCAL_PAD)
    elif n == "ffn_w_down":
        g = g.reshape(N_DEV // 2, FF_LOCAL_PAD, D_MODEL)[:, :FF_LOCAL]
    local = _local_shape(FULL_SHAPE[n], SHARD_AXIS[n])
    if n in FF_PADDED:
        local = local[:-1] + (FF_LOCAL_PAD,)
    return g.reshape((N_XY, 2) + local)


SMALL_SHAPE = {n: (1, s[0]) if len(s) == 1 else (s[0], math.prod(s[1:])) for n, s in REPLICATED}
SMALL_SHAPE["ffn_conv_b"] = (N_DEV, FF_LOCAL)
SMALL_EARLY = tuple(n for n, _ in REPLICATED if n != "norm_mix_pre")


def _adamw_replicated(parts, w, m, v, *, name):
    n = len(parts)

    def body(*refs):
        p_refs, w_refs, m_refs, v_refs = (refs[i * n:(i + 1) * n] for i in range(4))
        outs = refs[4 * n:]
        for k in range(n):
            g = p_refs[k][0, 0]
            for s in range(1, N_DEV):
                g = g + p_refs[k][s // 2, s % 2]
            d, nm, nv = _adamw_math(w_refs[k][...], g, m_refs[k][...], v_refs[k][...])
            for slot, val in enumerate((g, d, nm, nv)):
                outs[slot * n + k][...] = val

    vmem = pl.BlockSpec(memory_space=pltpu.VMEM)
    shapes = [jax.ShapeDtypeStruct(a.shape, F32) for a in w] * 4
    res = pl.pallas_call(
        body, name=name, out_shape=tuple(shapes), in_specs=[vmem] * (4 * n), out_specs=tuple([vmem] * (4 * n)),
        compiler_params=pltpu.CompilerParams(vmem_limit_bytes=VMEM_LIMIT),
    )(*parts, *w, *m, *v)
    return [list(res[i * n:(i + 1) * n]) for i in range(4)]


def kernel(x, mem, norm_mix_pre, norm_mix_post, w_in, b_gate, ssm_a_re, ssm_a_im, ssm_log_dt, ssm_b_re, ssm_b_im, ssm_c_re, ssm_c_im, ssm_d, ssm_w_glu, ssm_b_glu, w_branch_attn, w_branch_ssm, w_out, norm_xa_pre, norm_xa_post, norm_mem, xa_wq, xa_wk, xa_wv, xa_wo, norm_ffn_pre, norm_ffn_post, ffn_w_up, ffn_conv_w, ffn_conv_b, ffn_w_down, loss_target, m_norm_mix_pre, m_norm_mix_post, m_w_in, m_b_gate, m_ssm_a_re, m_ssm_a_im, m_ssm_log_dt, m_ssm_b_re, m_ssm_b_im, m_ssm_c_re, m_ssm_c_im, m_ssm_d, m_ssm_w_glu, m_ssm_b_glu, m_w_branch_attn, m_w_branch_ssm, m_w_out, m_norm_xa_pre, m_norm_xa_post, m_norm_mem, m_xa_wq, m_xa_wk, m_xa_wv, m_xa_wo, m_norm_ffn_pre, m_norm_ffn_post, m_ffn_w_up, m_ffn_conv_w, m_ffn_conv_b, m_ffn_w_down, v_norm_mix_pre, v_norm_mix_post, v_w_in, v_b_gate, v_ssm_a_re, v_ssm_a_im, v_ssm_log_dt, v_ssm_b_re, v_ssm_b_im, v_ssm_c_re, v_ssm_c_im, v_ssm_d, v_ssm_w_glu, v_ssm_b_glu, v_w_branch_attn, v_w_branch_ssm, v_w_out, v_norm_xa_pre, v_norm_xa_post, v_norm_mem, v_xa_wq, v_xa_wk, v_xa_wv, v_xa_wo, v_norm_ffn_pre, v_norm_ffn_post, v_ffn_w_up, v_ffn_conv_w, v_ffn_conv_b, v_ffn_w_down):
    args = dict(locals())
    w_loc = {n: args[n][0] for n in PARAM_ORDER}
    m_loc = {n: args["m_" + n][0] for n in PARAM_ORDER}
    v_loc = {n: args["v_" + n][0] for n in PARAM_ORDER}
    core_i = lax.axis_index("c")
    chip_i = 2 * lax.axis_index("x") + lax.axis_index("y")
    core = core_i.astype(jnp.int32).reshape(1)
    chip = chip_i.astype(jnp.int32).reshape(1)

    def in_place(a):
        buf = lax.empty((N_XY, 2) + a.shape, a.dtype)
        return lax.dynamic_update_slice(buf, a[None, None], (chip_i, core_i) + (0,) * a.ndim)

    as_wire = lambda n: in_place(_as_local(n, w_loc[n]).astype(F32 if n == "ffn_conv_w" else BF16))

    P = {}
    for n, shape in REPLICATED:
        P[n] = w_loc[n] if len(shape) > 1 or n == "ssm_log_dt" else w_loc[n].reshape(1, -1)
    P["ffn_conv_b"] = _pad_cols(w_loc["ffn_conv_b"].reshape(N_DEV, FF_LOCAL), FF_LOCAL_PAD).reshape(1, 2 * FF_PAD)

    loss, grad_x, small_parts, reduced = _local_step(x[0], mem[0], loss_target[0], as_wire("w_in"),
                                                     [as_wire(n) for n in LATE], P, core)
    loss = lax.psum(loss[0, 0], ("x", "y", "c"))

    big_out = {}
    for n, (own, parts) in reduced.items():
        res = _reduce_adamw(parts, _as_local(n, w_loc[n]), _as_local(n, m_loc[n]), _as_local(n, v_loc[n]),
                            own=own, own_slot=chip, name="adamw_" + n)
        big_out[n] = [r[:, :FF_LOCAL] if n in FF_PADDED else r for r in res]

    names = [n for n, _ in REPLICATED]
    as_small = lambda d: [d[n].reshape(SMALL_SHAPE[n]) for n in names]
    small_out = _adamw_replicated([small_parts[n] for n in names], as_small(w_loc), as_small(m_loc), as_small(v_loc),
                                  name="adamw_replicated")
    small_out = [dict(zip(names, res)) for res in small_out]

    outs = [loss, grad_x[None]]
    for k in range(4):
        for n in PARAM_ORDER:
            src = big_out[n][k] if n in big_out else small_out[k][n]
            outs.append(src.reshape(args[n].shape))
    return tuple(outs)
```

```python
import functools
import math

import jax
import jax.numpy as jnp
from jax import lax
from jax.experimental import pallas as pl
from jax.experimental.pallas import tpu as pltpu

F32 = jnp.float32
BF16 = jnp.bfloat16

D_MODEL = 1024
SB_HEADS = 8
SB_HEAD_DIM = 64
SB_WIDTH = 512
SSM_WIDTH = 512
SSM_GROUP = 16
SSM_GROUPS = 32
SSM_STATE = 64
XA_HEADS = 4
XA_HEAD_DIM = 256
D_FF = 2816
RMS_EPS = 1e-6
IN_WIDTH = 4096
N_DEV = 8

ADAM_LR = 0.001
ADAM_B1 = 0.9
ADAM_B2 = 0.999
ADAM_EPS = 1e-08
ADAM_WD = 0.01
ADAM_STEP = 10

LANES = 128
SUBLANES = 8
VMEM_LIMIT = 48 * 1024 * 1024

_GELU_C = math.sqrt(2.0 / math.pi)


def _cparams(*sem):
    return pltpu.CompilerParams(dimension_semantics=sem, vmem_limit_bytes=VMEM_LIMIT)


def _pick(n, cands):
    for c in cands:
        if n % c == 0:
            return c
    return n


def _gelu(x):
    return 0.5 * x * (1.0 + jnp.tanh(_GELU_C * (x + 0.044715 * x * x * x)))


def _gelu_and_grad(x):
    t = jnp.tanh(_GELU_C * (x + 0.044715 * x * x * x))
    g = 0.5 * x * (1.0 + t)
    dg = 0.5 * (1.0 + t) + 0.5 * x * (1.0 - t * t) * _GELU_C * (1.0 + 3.0 * 0.044715 * x * x)
    return g, dg


def _sigmoid(x):
    return 1.0 / (1.0 + jnp.exp(-x))


def _dot(a, b, ca, cb):
    return lax.dot_general(a.astype(BF16), b.astype(BF16), (((ca,), (cb,)), ((), ())),
                           preferred_element_type=F32)


MM_TILES = (1024, 768, 512, 256, 128)
MM_K_TILES = (2048, 1536) + MM_TILES


def _matmul(a, b, *, ta=False, tb=False, out_dtype=F32, name, b_block0=0, n_blocks=None,
            out_cb=None, out_into=None, out_block0=0, acc_in=None, rider=None):
    if ta:
        K, M = a.shape
    else:
        M, K = a.shape
    b_cb = None
    if b.ndim == 3:
        b_cb = b.shape[2]
        n_blocks = b.shape[0] - b_block0 if n_blocks is None else n_blocks
        N, K2 = (b.shape[1], n_blocks * b_cb) if tb else (n_blocks * b_cb, b.shape[1])
    elif tb:
        N, K2 = b.shape
    else:
        K2, N = b.shape
    assert K == K2, (a.shape, b.shape, ta, tb)
    if out_into is not None:
        out_cb = out_into.shape[2]
    tm = _pick(M, MM_TILES)
    n_unit = math.gcd(N, math.gcd(b_cb if (b_cb and not tb) else N, out_cb or N))
    tn = _pick(n_unit, MM_TILES)
    k_unit = b_cb if (b_cb and tb) else K
    tk = _pick(k_unit, MM_K_TILES)
    nk = K // tk
    ca, cb = (0 if ta else 1), (1 if tb else 0)
    has_acc = acc_in is not None
    has_into = out_into is not None

    def body(*refs):
        a_ref, b_ref = refs[0], refs[1]
        pos = 2
        c_ref = None
        if has_acc:
            c_ref = refs[pos]
            pos += 1
        if has_into:
            pos += 1
        o_ref = refs[pos]
        p = _dot(a_ref[...], b_ref[...], ca, cb)
        if nk == 1:
            o_ref[...] = ((p + c_ref[...]) if has_acc else p).astype(out_dtype)
        else:
            acc_ref = refs[pos + 1]
            k = pl.program_id(2)

            @pl.when(k == 0)
            def _():
                acc_ref[...] = (p + c_ref[...]) if has_acc else p

            @pl.when(k > 0)
            def _():
                acc_ref[...] += p

            @pl.when(k == nk - 1)
            def _():
                o_ref[...] = acc_ref[...].astype(out_dtype)

    nj, ni = N // tn, M // tm
    a_bytes, b_bytes = a.size * a.dtype.itemsize, K * N * b.dtype.itemsize
    n_outer = a_bytes * nj + b_bytes * (1 if nk == 1 else ni) <= a_bytes * (1 if nk == 1 else nj) + b_bytes * ni
    grid = (nj, ni, nk) if n_outer else (ni, nj, nk)

    def spec(block, index):
        return pl.BlockSpec(block, (lambda g0, g1, k: index(g0, g1, k)) if n_outer else (lambda g0, g1, k: index(g1, g0, k)))

    a_spec = spec((tk, tm), lambda j, i, k: (k, i)) if ta else spec((tm, tk), lambda j, i, k: (i, k))
    if b_cb is None:
        b_spec = spec((tn, tk), lambda j, i, k: (j, k)) if tb else spec((tk, tn), lambda j, i, k: (k, j))
    elif tb:
        per = b_cb // tk
        b_spec = spec((None, tn, tk), lambda j, i, k: (b_block0 + k // per, j, k % per))
    else:
        per = b_cb // tn
        b_spec = spec((None, tk, tn), lambda j, i, k: (b_block0 + j // per, k, j % per))
    in_specs = [a_spec, b_spec]
    operands = [a, b]
    aliases = {}
    if has_acc:
        in_specs.append(spec((tm, tn), lambda j, i, k: (i, j)))
        operands.append(acc_in)
    if has_into:
        aliases = {len(operands): 0}
        in_specs.append(pl.BlockSpec(memory_space=pl.ANY))
        operands.append(out_into)
    if out_cb is None:
        out_shape = jax.ShapeDtypeStruct((M, N), out_dtype)
        out_spec = spec((tm, tn), lambda j, i, k: (i, j))
    else:
        per_o = out_cb // tn
        out_shape = (jax.ShapeDtypeStruct(out_into.shape, out_into.dtype) if has_into
                     else jax.ShapeDtypeStruct((N // out_cb, M, out_cb), out_dtype))
        out_spec = spec((None, tm, tn), lambda j, i, k: (out_block0 + j // per_o, i, j % per_o))
    if rider is not None:
        assert not has_into
        (out,), brought = _call(body, name=name, rider=rider, grid=grid, in_specs=in_specs,
                                out_specs=(out_spec,), out_shape=(out_shape,), operands=operands,
                                scratch_shapes=[] if nk == 1 else [pltpu.VMEM((tm, tn), F32)])
        return out, brought
    return pl.pallas_call(
        body, name=name, out_shape=out_shape,
        grid=grid,
        in_specs=in_specs, out_specs=out_spec, input_output_aliases=aliases,
        scratch_shapes=[] if nk == 1 else [pltpu.VMEM((tm, tn), F32)],
        compiler_params=_cparams("parallel", "parallel", "arbitrary"),
    )(*operands)


def _rms(x, g):
    r = lax.rsqrt(jnp.mean(x * x, axis=-1, keepdims=True) + RMS_EPS)
    return x * r * g


def _rms_bwd(dy, x, g):
    r = lax.rsqrt(jnp.mean(x * x, axis=-1, keepdims=True) + RMS_EPS)
    xh = x * r
    dxh = dy * g
    dx = r * (dxh - xh * jnp.mean(dxh * xh, axis=-1, keepdims=True))
    dg = jnp.sum(dy * xh, axis=0, keepdims=True)
    return dx, dg


def _row_tile(rows):
    return _pick(rows, (512, 256, 128, 64, 32, 16, 8))


def _rms_fwd(x, g, *, name, rider=None):
    R, D = x.shape
    tr = _row_tile(R)

    def body(x_ref, g_ref, h_ref):
        h_ref[...] = _rms(x_ref[...], g_ref[...]).astype(BF16)

    (h,), brought = _call(
        body, name=name, rider=rider, out_shape=(jax.ShapeDtypeStruct((R, D), BF16),), grid=(R // tr,),
        in_specs=[pl.BlockSpec((tr, D), lambda i: (i, 0)), pl.BlockSpec((1, D), lambda i: (0, 0))],
        out_specs=(pl.BlockSpec((tr, D), lambda i: (i, 0)),), scratch_shapes=[], operands=(x, g))
    return h if rider is None else (h, brought)


def _resnorm_norm(x, z, g_post, g_next, *, name):
    R, D = x.shape
    tr = _row_tile(R)

    def body(x_ref, z_ref, gp_ref, gn_ref, xn_ref, h_ref):
        xn = x_ref[...] + _rms(z_ref[...], gp_ref[...])
        xn_ref[...] = xn
        h_ref[...] = _rms(xn, gn_ref[...]).astype(BF16)

    row = pl.BlockSpec((tr, D), lambda i: (i, 0))
    vec = pl.BlockSpec((1, D), lambda i: (0, 0))
    return pl.pallas_call(
        body, name=name,
        out_shape=(jax.ShapeDtypeStruct((R, D), F32), jax.ShapeDtypeStruct((R, D), BF16)),
        grid=(R // tr,), in_specs=[row, row, vec, vec], out_specs=(row, row),
        compiler_params=_cparams("parallel"),
    )(x, z, g_post, g_next)


def _final_loss(x, z, g_post, target, *, name):
    R, D = x.shape
    tr = _row_tile(R)

    def body(x_ref, z_ref, gp_ref, t_ref, loss_ref, dy_ref, dz_ref, dg_ref):
        i = pl.program_id(0)
        z = z_ref[...]
        g = gp_ref[...]
        err = x_ref[...] + _rms(z, g) - t_ref[...]
        dy = err * (1.0 / D)
        dy_ref[...] = dy
        dz, dg = _rms_bwd(dy, z, g)
        dz_ref[...] = dz.astype(BF16)
        part = 0.5 * jnp.sum(jnp.sum(err * err, axis=-1, keepdims=True) * (1.0 / D), axis=0, keepdims=True)

        @pl.when(i == 0)
        def _():
            loss_ref[...] = part
            dg_ref[...] = dg

        @pl.when(i > 0)
        def _():
            loss_ref[...] += part
            dg_ref[...] += dg

    row = pl.BlockSpec((tr, D), lambda i: (i, 0))
    vec = pl.BlockSpec((1, D), lambda i: (0, 0))
    return pl.pallas_call(
        body, name=name,
        out_shape=(jax.ShapeDtypeStruct((1, 1), F32), jax.ShapeDtypeStruct((R, D), F32),
                   jax.ShapeDtypeStruct((R, D), BF16), jax.ShapeDtypeStruct((1, D), F32)),
        grid=(R // tr,), in_specs=[row, row, vec, row],
        out_specs=(pl.BlockSpec((1, 1), lambda i: (0, 0)), row, row, vec),
        compiler_params=_cparams("arbitrary"),
    )(x, z, g_post, target)


def _norm_bwd_pair(dres, dh, xk, g_pre, zprev, g_prev_post, *, name, rider=None):
    R, D = xk.shape
    tr = _row_tile(R)

    def body(dres_ref, dh_ref, x_ref, gpre_ref, z_ref, gpost_ref, dx_ref, dz_ref, dgpre_ref, dgpost_ref):
        i = pl.program_id(0)
        d1, dgpre = _rms_bwd(dh_ref[...], x_ref[...], gpre_ref[...])
        dx = dres_ref[...] + d1
        dx_ref[...] = dx
        dz, dgpost = _rms_bwd(dx, z_ref[...], gpost_ref[...])
        dz_ref[...] = dz.astype(BF16)

        @pl.when(i == 0)
        def _():
            dgpre_ref[...] = dgpre
            dgpost_ref[...] = dgpost

        @pl.when(i > 0)
        def _():
            dgpre_ref[...] += dgpre
            dgpost_ref[...] += dgpost

    row = pl.BlockSpec((tr, D), lambda i: (i, 0))
    vec = pl.BlockSpec((1, D), lambda i: (0, 0))
    return _call(
        body, name=name, rider=rider,
        out_shape=(jax.ShapeDtypeStruct((R, D), F32), jax.ShapeDtypeStruct((R, D), BF16),
                   jax.ShapeDtypeStruct((1, D), F32), jax.ShapeDtypeStruct((1, D), F32)),
        grid=(R // tr,), in_specs=[row, row, row, vec, row, vec], out_specs=(row, row, vec, vec),
        scratch_shapes=[], operands=(dres, dh, xk, g_pre, zprev, g_prev_post))


def _norm_bwd_single(dres, dh, xk, g_pre, *, name, rider=None):
    R, D = xk.shape
    tr = _row_tile(R)
    has_res = dres is not None

    def body(*refs):
        if has_res:
            dres_ref, dh_ref, x_ref, gpre_ref, dx_ref, dgpre_ref = refs
        else:
            dh_ref, x_ref, gpre_ref, dx_ref, dgpre_ref = refs
        i = pl.program_id(0)
        d1, dgpre = _rms_bwd(dh_ref[...], x_ref[...], gpre_ref[...])
        dx_ref[...] = dres_ref[...] + d1 if has_res else d1

        @pl.when(i == 0)
        def _():
            dgpre_ref[...] = dgpre

        @pl.when(i > 0)
        def _():
            dgpre_ref[...] += dgpre

    row = pl.BlockSpec((tr, D), lambda i: (i, 0))
    vec = pl.BlockSpec((1, D), lambda i: (0, 0))
    ins = ([dres] if has_res else []) + [dh, xk, g_pre]
    res, brought = _call(
        body, name=name, rider=rider,
        out_shape=(jax.ShapeDtypeStruct((R, D), F32), jax.ShapeDtypeStruct((1, D), F32)),
        grid=(R // tr,), in_specs=([row] if has_res else []) + [row, row, vec], out_specs=(row, vec),
        scratch_shapes=[], operands=ins)
    return res if rider is None else (res, brought)


SB_BLOCK = 256
SB_QBLOCK = 1024
SB_DEAD = -104.0


def _sb_tri(kind):
    r = lax.broadcasted_iota(jnp.int32, (SB_BLOCK, SB_BLOCK), 0)
    c = lax.broadcasted_iota(jnp.int32, (SB_BLOCK, SB_BLOCK), 1)
    keep = {"after": r > c, "before": r < c}[kind]
    return jnp.where(keep, 1.0, 0.0).astype(BF16)


def _sb_scores(qm, k_blk):
    z = _dot(qm, k_blk, 1, 1)
    sp = jnp.maximum(z, 0.0) + jnp.log(1.0 + jnp.exp(-jnp.abs(z)))
    return z, sp


def _sb_causal(rows):
    r = lax.broadcasted_iota(jnp.int32, (rows, SB_BLOCK), 0)
    c = lax.broadcasted_iota(jnp.int32, (rows, SB_BLOCK), 1)
    return c < r


def _head_masks():
    lane = lax.broadcasted_iota(jnp.int32, (1, LANES), 1)
    return [jnp.where(lane < SB_HEAD_DIM, 1.0, 0.0), jnp.where(lane >= SB_HEAD_DIM, 1.0, 0.0)]


def _sb_fwd(proj, *, name, rider=None):
    S = proj.shape[0]
    T = SB_BLOCK
    TQ = min(SB_QBLOCK, S)
    span = TQ // T
    nq = S // TQ
    npair = SB_WIDTH // LANES
    scale = SB_HEAD_DIM ** -0.5

    def body(q_ref, k_ref, v_ref, o_ref, tot_ref, first_ref, acc_ref, run_ref):
        masks = _head_masks()
        tri = _sb_tri("after")
        first_ref[...] = jnp.zeros_like(first_ref)
        slot = lax.broadcasted_iota(jnp.int32, first_ref.shape, 1)

        def alive():
            reach = jnp.maximum(jnp.max(run_ref[0]), jnp.max(run_ref[1]))
            return (reach > SB_DEAD).astype(jnp.int32)

        def q_block(i, _):
            qrow = pl.ds(pl.multiple_of(i * TQ, TQ), TQ)
            q = q_ref[qrow, :] * scale
            qm = [(q * m).astype(BF16) for m in masks]
            acc_ref[...] = jnp.zeros_like(acc_ref)
            run_ref[...] = jnp.zeros_like(run_ref)

            def k_block(j, own):
                krow = pl.ds(pl.multiple_of(j * T, T), T)
                k_blk = k_ref[krow, :].astype(BF16)
                v_blk = v_ref[krow, :].astype(BF16)
                r0 = 0 if own is None else own * T
                rows = pl.ds(r0, TQ - r0)
                for h in range(2):
                    z, sp = _sb_scores(qm[h][r0:], k_blk)
                    causal = None if own is None else _sb_causal(TQ - r0)
                    lf = -sp if causal is None else jnp.where(causal, -sp, 0.0)
                    e = jnp.exp(z - sp + _dot(lf, tri, 1, 0) + run_ref[h, rows])
                    w = e if causal is None else jnp.where(causal, e, 0.0)
                    acc_ref[h, rows] += _dot(w, v_blk, 1, 0)
                    run_ref[h, rows] += jnp.sum(lf, axis=1, keepdims=True)

            for d in reversed(range(span)):
                k_block(i * span + d, d)

            def below(carry):
                jj, _ = carry
                k_block(i * span - 1 - jj, None)
                return jj + 1, alive()

            done, _ = lax.fori_loop(0, i * span, lambda _, c: below(c), (jnp.int32(0), alive()))
            o_ref[qrow, :] = (acc_ref[0] * masks[0] + acc_ref[1] * masks[1]).astype(BF16)
            tot_ref[qrow, :] = run_ref[0] * masks[0] + run_ref[1] * masks[1]
            first_ref[...] = jnp.where(slot == i, (i * span - done).astype(F32), first_ref[...])
            return 0

        lax.fori_loop(0, nq, q_block, 0)

    blk = lambda off: pl.BlockSpec((S, LANES), lambda p: (0, off + p))
    return _call(
        body, name=name, rider=rider,
        out_shape=(jax.ShapeDtypeStruct((S, SB_WIDTH), BF16), jax.ShapeDtypeStruct((S, SB_WIDTH), F32),
                   jax.ShapeDtypeStruct((npair, SUBLANES, LANES), F32)),
        grid=(npair,),
        in_specs=[blk(0), blk(npair), blk(2 * npair)],
        out_specs=(blk(0), blk(0), pl.BlockSpec((1, SUBLANES, LANES), lambda p: (p, 0, 0))),
        scratch_shapes=[pltpu.VMEM((2, TQ, LANES), F32), pltpu.VMEM((2, TQ, 1), F32)],
        operands=(proj, proj, proj))


def _sb_bwd(proj, tot, first, do_attn, *, name, rider=None):
    S = proj.shape[0]
    T = SB_BLOCK
    TQ = min(SB_QBLOCK, S)
    span = TQ // T
    nq = S // TQ
    npair = SB_WIDTH // LANES
    scale = SB_HEAD_DIM ** -0.5

    def body(q_ref, k_ref, v_ref, tot_ref, first_ref, do_ref, dq_ref, dk_ref, dv_ref,
             dqacc_ref, dkacc_ref, dvacc_ref, run_ref, grun_ref):
        masks = _head_masks()
        tri_after = _sb_tri("after")
        tri_before = _sb_tri("before")
        dkacc_ref[...] = jnp.zeros_like(dkacc_ref)
        dvacc_ref[...] = jnp.zeros_like(dvacc_ref)
        slot = lax.broadcasted_iota(jnp.int32, first_ref.shape, 1)

        def q_block(i, _):
            qrow = pl.ds(pl.multiple_of(i * TQ, TQ), TQ)
            q = q_ref[qrow, :] * scale
            do = do_ref[qrow, :].astype(F32)
            tot = tot_ref[qrow, :]
            qm = [(q * m).astype(BF16) for m in masks]
            dom = [(do * m).astype(BF16) for m in masks]
            ltot = [jnp.sum(tot * m, axis=1, keepdims=True) * (1.0 / SB_HEAD_DIM) for m in masks]
            dqacc_ref[...] = jnp.zeros_like(dqacc_ref)
            run_ref[...] = jnp.zeros_like(run_ref)
            grun_ref[...] = jnp.zeros_like(grun_ref)

            def k_block(j, own):
                krow = pl.ds(pl.multiple_of(j * T, T), T)
                k_blk = k_ref[krow, :].astype(BF16)
                v_blk = v_ref[krow, :].astype(BF16)
                r0 = 0 if own is None else own * T
                rows = pl.ds(r0, TQ - r0)
                for h in range(2):
                    z, sp = _sb_scores(qm[h][r0:], k_blk)
                    causal = None if own is None else _sb_causal(TQ - r0)
                    lf = -sp if causal is None else jnp.where(causal, -sp, 0.0)
                    lsum = jnp.sum(lf, axis=1, keepdims=True)
                    later = (ltot[h][r0:] - run_ref[h, rows] - lsum) + _dot(lf, tri_after, 1, 0)
                    beta = jnp.exp(z - sp)
                    w = jnp.exp(z - sp + later)
                    if causal is not None:
                        w = jnp.where(causal, w, 0.0)
                    g = _dot(dom[h][r0:], v_blk, 1, 1) * w
                    gbefore = grun_ref[h, rows] + _dot(g, tri_before, 1, 0)
                    dz = g - beta * (g + gbefore)
                    if causal is not None:
                        dz = jnp.where(causal, dz, 0.0)
                    dz = dz.astype(BF16)
                    dqacc_ref[h, rows] += _dot(dz, k_blk, 1, 0)
                    dkacc_ref[krow, :] += _dot(dz, qm[h][r0:], 0, 0)
                    dvacc_ref[krow, :] += _dot(w, dom[h][r0:], 0, 0)
                    run_ref[h, rows] += lsum
                    grun_ref[h, rows] += jnp.sum(g, axis=1, keepdims=True)

            def above(j, _):
                k_block(j, None)
                return 0

            lax.fori_loop(0, i * span, above, 0)
            for d in range(span):
                k_block(i * span + d, d)
            dq_ref[qrow, :] = ((dqacc_ref[0] * masks[0] + dqacc_ref[1] * masks[1]) * scale).astype(BF16)
            return 0

        lax.fori_loop(0, nq, q_block, 0)
        dk_ref[...] = dkacc_ref[...].astype(BF16)
        dv_ref[...] = dvacc_ref[...].astype(BF16)

    blk = lambda off: pl.BlockSpec((S, LANES), lambda p: (0, off + p))
    out = jax.ShapeDtypeStruct((S, SB_WIDTH), BF16)
    return _call(
        body, name=name, rider=rider, out_shape=(out, out, out), grid=(npair,),
        in_specs=[blk(0), blk(npair), blk(2 * npair), blk(0), pl.BlockSpec((1, SUBLANES, LANES), lambda p: (p, 0, 0)),
                  blk(0)],
        out_specs=(blk(0), blk(0), blk(0)),
        scratch_shapes=[pltpu.VMEM((2, TQ, LANES), F32), pltpu.VMEM((S, LANES), F32), pltpu.VMEM((S, LANES), F32),
                        pltpu.VMEM((2, TQ, 1), F32), pltpu.VMEM((2, TQ, 1), F32)],
        operands=(proj, proj, proj, tot, first, do_attn))


SSM_HALVES = 2
SSM_HALF_CH = SSM_WIDTH // SSM_HALVES
SSM_HALF_ST = SSM_GROUPS * SSM_STATE // SSM_HALVES
SSM_CHUNK = 512


def _cmul(ar, ai, br, bi):
    return ar * br - ai * bi, ar * bi + ai * br


def _ssm_tables(lam_re, lam_im):
    lr = lam_re.reshape(-1)
    li = lam_im.reshape(-1)
    pows = [(jnp.ones_like(lr), jnp.zeros_like(li)), (lr, li)]
    for _ in range(2, SUBLANES + 1):
        pows.append(_cmul(pows[-1][0], pows[-1][1], lr, li))
    row = jnp.arange(SUBLANES)[:, None]

    def shift_tab(d, keep):
        return [jnp.where(keep, pows[d][0][None, :], 0.0), jnp.where(keep, pows[d][1][None, :], 0.0)]

    fwd, bwd = [], []
    for d in (1, 2, 4):
        fwd += shift_tab(d, row >= d)
        bwd += shift_tab(d, row + d < SUBLANES)
    fwd += [jnp.stack([pows[r + 1][0] for r in range(SUBLANES)]), jnp.stack([pows[r + 1][1] for r in range(SUBLANES)])]
    bwd += [jnp.stack([pows[SUBLANES - r][0] for r in range(SUBLANES)]),
            jnp.stack([pows[SUBLANES - r][1] for r in range(SUBLANES)])]

    def halves(tabs):
        t = jnp.stack(tabs)
        return t.reshape(8, SUBLANES, SSM_HALVES, SSM_HALF_ST).transpose(2, 0, 1, 3)

    return halves(fwd), halves(bwd)


def _ssm_fwd(proj, bd_re, bd_im, cd_re, cd_imneg, d_skip, tab, *, name, rider=None):
    S = proj.shape[0]
    Tc = min(SSM_CHUNK, S)
    nc = S // Tc
    u_blk0 = (3 * SB_WIDTH) // SSM_HALF_CH

    def body(u_ref, bre_ref, bim_ref, cre_ref, cim_ref, d_ref, tab_ref, y_ref, xre_ref, xim_ref, cre_s, cim_s):
        c = pl.program_id(1)

        @pl.when(c == 0)
        def _():
            cre_s[...] = jnp.zeros_like(cre_s)
            cim_s[...] = jnp.zeros_like(cim_s)

        u = u_ref[...]
        ub = u.astype(BF16)
        xre_ref[...] = _dot(ub, bre_ref[0], 1, 0)
        xim_ref[...] = _dot(ub, bim_ref[0], 1, 0)

        def slab(k, carry):
            car_re, car_im = carry
            rows = pl.ds(pl.multiple_of(k * SUBLANES, SUBLANES), SUBLANES)
            sre = xre_ref[rows, :]
            sim = xim_ref[rows, :]
            for n, d in enumerate((1, 2, 4)):
                pre, pim = tab_ref[0, 2 * n], tab_ref[0, 2 * n + 1]
                rre = pltpu.roll(sre, d, 0)
                rim = pltpu.roll(sim, d, 0)
                sre, sim = sre + (pre * rre - pim * rim), sim + (pre * rim + pim * rre)
            pre, pim = tab_ref[0, 6], tab_ref[0, 7]
            sre, sim = sre + (pre * car_re - pim * car_im), sim + (pre * car_im + pim * car_re)
            xre_ref[rows, :] = sre
            xim_ref[rows, :] = sim
            last = (SUBLANES - 1, SUBLANES)
            return (jnp.broadcast_to(sre[last[0]:last[1], :], sre.shape),
                    jnp.broadcast_to(sim[last[0]:last[1], :], sim.shape))

        car = lax.fori_loop(0, Tc // SUBLANES, slab, (cre_s[...], cim_s[...]))
        cre_s[...] = car[0]
        cim_s[...] = car[1]
        y = _dot(xre_ref[...], cre_ref[0], 1, 0) + _dot(xim_ref[...], cim_ref[0], 1, 0)
        y_ref[...] = y + d_ref[...] * u

    return _call(
        body, name=name, rider=rider,
        out_shape=(jax.ShapeDtypeStruct((S, SSM_WIDTH), F32),
                   jax.ShapeDtypeStruct((S, SSM_HALVES * SSM_HALF_ST), F32),
                   jax.ShapeDtypeStruct((S, SSM_HALVES * SSM_HALF_ST), F32)),
        grid=(SSM_HALVES, nc),
        in_specs=[pl.BlockSpec((Tc, SSM_HALF_CH), lambda h, c: (c, u_blk0 + h)),
                  pl.BlockSpec((1, SSM_HALF_CH, SSM_HALF_ST), lambda h, c: (h, 0, 0)),
                  pl.BlockSpec((1, SSM_HALF_CH, SSM_HALF_ST), lambda h, c: (h, 0, 0)),
                  pl.BlockSpec((1, SSM_HALF_ST, SSM_HALF_CH), lambda h, c: (h, 0, 0)),
                  pl.BlockSpec((1, SSM_HALF_ST, SSM_HALF_CH), lambda h, c: (h, 0, 0)),
                  pl.BlockSpec((1, SSM_HALF_CH), lambda h, c: (0, h)),
                  pl.BlockSpec((1, 8, SUBLANES, SSM_HALF_ST), lambda h, c: (h, 0, 0, 0))],
        out_specs=(pl.BlockSpec((Tc, SSM_HALF_CH), lambda h, c: (c, h)),
                   pl.BlockSpec((Tc, SSM_HALF_ST), lambda h, c: (c, h)),
                   pl.BlockSpec((Tc, SSM_HALF_ST), lambda h, c: (c, h))),
        scratch_shapes=[pltpu.VMEM((SUBLANES, SSM_HALF_ST), F32), pltpu.VMEM((SUBLANES, SSM_HALF_ST), F32)],
        operands=(proj, bd_re, bd_im, cd_re, cd_imneg, d_skip, tab))


def _ssm_bwd(dy, proj, x_re, x_im, bd_re, bd_im, cd_re, cd_imneg, d_skip, tab, *, name, rider=None):
    S = proj.shape[0]
    Tc = min(SSM_CHUNK, S)
    nc = S // Tc
    u_blk0 = (3 * SB_WIDTH) // SSM_HALF_CH

    def body(dy_ref, u_ref, xre_ref, xim_ref, bre_ref, bim_ref, cre_ref, cim_ref, d_ref, tab_ref,
             du_ref, dbre_ref, dbim_ref, dcre_ref, dcim_ref, dd_ref, dlre_ref, dlim_ref,
             gre_s, gim_s, cre_s, cim_s):
        c = pl.program_id(1)

        @pl.when(c == 0)
        def _():
            cre_s[...] = jnp.zeros_like(cre_s)
            cim_s[...] = jnp.zeros_like(cim_s)
            dbre_ref[...] = jnp.zeros_like(dbre_ref)
            dbim_ref[...] = jnp.zeros_like(dbim_ref)
            dcre_ref[...] = jnp.zeros_like(dcre_ref)
            dcim_ref[...] = jnp.zeros_like(dcim_ref)
            dd_ref[...] = jnp.zeros_like(dd_ref)
            dlre_ref[...] = jnp.zeros_like(dlre_ref)
            dlim_ref[...] = jnp.zeros_like(dlim_ref)

        dy = dy_ref[...]
        dyb = dy.astype(BF16)
        u = u_ref[...]
        gre_s[...] = _dot(dyb, cre_ref[0], 1, 1)
        gim_s[...] = _dot(dyb, cim_ref[0], 1, 1)
        row = lax.broadcasted_iota(jnp.int32, (SUBLANES, SSM_HALF_ST), 0)
        nslab = Tc // SUBLANES

        def slab(kk, carry):
            car_re, car_im, acc_re, acc_im = carry
            k = nslab - 1 - kk
            rows = pl.ds(pl.multiple_of(k * SUBLANES, SUBLANES), SUBLANES)
            sre = gre_s[rows, :]
            sim = gim_s[rows, :]
            for n, d in enumerate((1, 2, 4)):
                pre, pim = tab_ref[0, 2 * n], tab_ref[0, 2 * n + 1]
                rre = pltpu.roll(sre, SUBLANES - d, 0)
                rim = pltpu.roll(sim, SUBLANES - d, 0)
                sre, sim = sre + (pre * rre + pim * rim), sim + (pre * rim - pim * rre)
            pre, pim = tab_ref[0, 6], tab_ref[0, 7]
            sre, sim = sre + (pre * car_re + pim * car_im), sim + (pre * car_im - pim * car_re)
            gre_s[rows, :] = sre
            gim_s[rows, :] = sim
            nre = jnp.where(row == SUBLANES - 1, car_re, pltpu.roll(sre, SUBLANES - 1, 0))
            nim = jnp.where(row == SUBLANES - 1, car_im, pltpu.roll(sim, SUBLANES - 1, 0))
            xr = xre_ref[rows, :]
            xi = xim_ref[rows, :]
            acc_re = acc_re + (nre * xr + nim * xi)
            acc_im = acc_im + (nim * xr - nre * xi)
            return (jnp.broadcast_to(sre[0:1, :], sre.shape), jnp.broadcast_to(sim[0:1, :], sim.shape), acc_re, acc_im)

        car = lax.fori_loop(0, nslab, slab, (cre_s[...], cim_s[...], dlre_ref[0], dlim_ref[0]))
        cre_s[...] = car[0]
        cim_s[...] = car[1]
        dlre_ref[0] = car[2]
        dlim_ref[0] = car[3]
        gre = gre_s[...].astype(BF16)
        gim = gim_s[...].astype(BF16)
        ub = u.astype(BF16)
        du = _dot(gre, bre_ref[0], 1, 1) + _dot(gim, bim_ref[0], 1, 1) + d_ref[...] * dy
        du_ref[...] = du.astype(BF16)
        dbre_ref[0] += _dot(ub, gre, 0, 0)
        dbim_ref[0] += _dot(ub, gim, 0, 0)
        dcre_ref[0] += _dot(xre_ref[...], dyb, 0, 0)
        dcim_ref[0] += _dot(xim_ref[...], dyb, 0, 0)
        dd_ref[...] += jnp.sum(dy * u, axis=0, keepdims=True)

    rev = lambda c: nc - 1 - c
    return _call(
        body, name=name, rider=rider,
        out_shape=(jax.ShapeDtypeStruct((S, SSM_WIDTH), BF16),
                   jax.ShapeDtypeStruct((SSM_HALVES, SSM_HALF_CH, SSM_HALF_ST), F32),
                   jax.ShapeDtypeStruct((SSM_HALVES, SSM_HALF_CH, SSM_HALF_ST), F32),
                   jax.ShapeDtypeStruct((SSM_HALVES, SSM_HALF_ST, SSM_HALF_CH), F32),
                   jax.ShapeDtypeStruct((SSM_HALVES, SSM_HALF_ST, SSM_HALF_CH), F32),
                   jax.ShapeDtypeStruct((1, SSM_WIDTH), F32),
                   jax.ShapeDtypeStruct((SSM_HALVES, SUBLANES, SSM_HALF_ST), F32),
                   jax.ShapeDtypeStruct((SSM_HALVES, SUBLANES, SSM_HALF_ST), F32)),
        grid=(SSM_HALVES, nc),
        in_specs=[pl.BlockSpec((Tc, SSM_HALF_CH), lambda h, c: (rev(c), h)),
                  pl.BlockSpec((Tc, SSM_HALF_CH), lambda h, c: (rev(c), u_blk0 + h)),
                  pl.BlockSpec((Tc, SSM_HALF_ST), lambda h, c: (rev(c), h)),
                  pl.BlockSpec((Tc, SSM_HALF_ST), lambda h, c: (rev(c), h)),
                  pl.BlockSpec((1, SSM_HALF_CH, SSM_HALF_ST), lambda h, c: (h, 0, 0)),
                  pl.BlockSpec((1, SSM_HALF_CH, SSM_HALF_ST), lambda h, c: (h, 0, 0)),
                  pl.BlockSpec((1, SSM_HALF_ST, SSM_HALF_CH), lambda h, c: (h, 0, 0)),
                  pl.BlockSpec((1, SSM_HALF_ST, SSM_HALF_CH), lambda h, c: (h, 0, 0)),
                  pl.BlockSpec((1, SSM_HALF_CH), lambda h, c: (0, h)),
                  pl.BlockSpec((1, 8, SUBLANES, SSM_HALF_ST), lambda h, c: (h, 0, 0, 0))],
        out_specs=(pl.BlockSpec((Tc, SSM_HALF_CH), lambda h, c: (rev(c), h)),
                   pl.BlockSpec((1, SSM_HALF_CH, SSM_HALF_ST), lambda h, c: (h, 0, 0)),
                   pl.BlockSpec((1, SSM_HALF_CH, SSM_HALF_ST), lambda h, c: (h, 0, 0)),
                   pl.BlockSpec((1, SSM_HALF_ST, SSM_HALF_CH), lambda h, c: (h, 0, 0)),
                   pl.BlockSpec((1, SSM_HALF_ST, SSM_HALF_CH), lambda h, c: (h, 0, 0)),
                   pl.BlockSpec((1, SSM_HALF_CH), lambda h, c: (0, h)),
                   pl.BlockSpec((1, SUBLANES, SSM_HALF_ST), lambda h, c: (h, 0, 0)),
                   pl.BlockSpec((1, SUBLANES, SSM_HALF_ST), lambda h, c: (h, 0, 0))),
        scratch_shapes=[pltpu.VMEM((Tc, SSM_HALF_ST), F32), pltpu.VMEM((Tc, SSM_HALF_ST), F32),
                        pltpu.VMEM((SUBLANES, SSM_HALF_ST), F32), pltpu.VMEM((SUBLANES, SSM_HALF_ST), F32)],
        operands=(dy, proj, x_re, x_im, bd_re, bd_im, cd_re, cd_imneg, d_skip, tab))


def _ssm_prepare(a_re, a_im, log_dt, b_re, b_im):
    dt = jnp.exp(log_dt)[:, None]
    mag = jnp.exp(a_re * dt)
    lre = mag * jnp.cos(a_im * dt)
    lim = mag * jnp.sin(a_im * dt)
    den = a_re * a_re + a_im * a_im
    fre = ((lre - 1.0) * a_re + lim * a_im) / den
    fim = (lim * a_re - (lre - 1.0) * a_im) / den
    bbre = fre[:, :, None] * b_re - fim[:, :, None] * b_im
    bbim = fre[:, :, None] * b_im + fim[:, :, None] * b_re
    return lre, lim, bbre, bbim


def _group_eye():
    return jnp.eye(SSM_GROUPS // SSM_HALVES, dtype=F32)


def _bd_from_bbar(bbar):
    gh = SSM_GROUPS // SSM_HALVES
    b = bbar.reshape(SSM_HALVES, gh, SSM_STATE, SSM_GROUP).transpose(0, 1, 3, 2)
    out = b[:, :, :, None, :] * _group_eye()[None, :, None, :, None]
    return out.reshape(SSM_HALVES, SSM_HALF_CH, SSM_HALF_ST)


def _bbar_from_bd(dbd):
    gh = SSM_GROUPS // SSM_HALVES
    d = dbd.reshape(SSM_HALVES, gh, SSM_GROUP, gh, SSM_STATE)
    d = jnp.sum(d * _group_eye()[None, :, None, :, None], axis=3)
    return d.transpose(0, 1, 3, 2).reshape(SSM_GROUPS, SSM_STATE, SSM_GROUP)


def _cd_from_c(cmat):
    gh = SSM_GROUPS // SSM_HALVES
    c = cmat.reshape(SSM_HALVES, gh, SSM_GROUP, SSM_STATE).transpose(0, 1, 3, 2)
    out = c[:, :, :, None, :] * _group_eye()[None, :, None, :, None]
    return out.reshape(SSM_HALVES, SSM_HALF_ST, SSM_HALF_CH)


def _c_from_cd(dcd):
    gh = SSM_GROUPS // SSM_HALVES
    d = dcd.reshape(SSM_HALVES, gh, SSM_STATE, gh, SSM_GROUP)
    d = jnp.sum(d * _group_eye()[None, :, None, :, None], axis=3)
    return d.transpose(0, 1, 3, 2).reshape(SSM_GROUPS, SSM_GROUP, SSM_STATE)


def _glu_fwd(y_pre, w_glu, b_glu, *, name):
    S, W = y_pre.shape
    tr = _row_tile(S)

    def body(y_ref, w_ref, b_ref, o_ref):
        yg = _gelu(y_ref[...])
        gl = _dot(yg, w_ref[...], 1, 0) + b_ref[...]
        o_ref[...] = (yg * _sigmoid(gl)).astype(BF16)

    row = pl.BlockSpec((tr, W), lambda i: (i, 0))
    return pl.pallas_call(
        body, name=name, out_shape=jax.ShapeDtypeStruct((S, W), BF16), grid=(S // tr,),
        in_specs=[row, pl.BlockSpec((W, W), lambda i: (0, 0)), pl.BlockSpec((1, W), lambda i: (0, 0))],
        out_specs=row, compiler_params=_cparams("parallel"),
    )(y_pre, w_glu, b_glu)


def _glu_bwd(y_pre, do, w_glu, b_glu, *, name):
    S, W = y_pre.shape
    tr = _row_tile(S)

    def body(y_ref, do_ref, w_ref, b_ref, dy_ref, dw_ref, db_ref):
        i = pl.program_id(0)
        yg, dyg_dy = _gelu_and_grad(y_ref[...])
        ygb = yg.astype(BF16)
        sg = _sigmoid(_dot(ygb, w_ref[...], 1, 0) + b_ref[...])
        do = do_ref[...]
        dgl = do * yg * sg * (1.0 - sg)
        dglb = dgl.astype(BF16)
        dyg = do * sg + _dot(dglb, w_ref[...], 1, 1)
        dy_ref[...] = dyg * dyg_dy
        dw = _dot(ygb, dglb, 0, 0)
        db = jnp.sum(dgl, axis=0, keepdims=True)

        @pl.when(i == 0)
        def _():
            dw_ref[...] = dw
            db_ref[...] = db

        @pl.when(i > 0)
        def _():
            dw_ref[...] += dw
            db_ref[...] += db

    row = pl.BlockSpec((tr, W), lambda i: (i, 0))
    full = pl.BlockSpec((W, W), lambda i: (0, 0))
    vec = pl.BlockSpec((1, W), lambda i: (0, 0))
    return pl.pallas_call(
        body, name=name,
        out_shape=(jax.ShapeDtypeStruct((S, W), F32), jax.ShapeDtypeStruct((W, W), F32), jax.ShapeDtypeStruct((1, W), F32)),
        grid=(S // tr,), in_specs=[row, row, full, vec], out_specs=(row, full, vec),
        compiler_params=_cparams("arbitrary"),
    )(y_pre, do, w_glu, b_glu)


GATE_COL0 = 3 * SB_WIDTH + SSM_WIDTH


def _merge_fwd(proj, o_attn, o_ssm, w_ba, w_bs, b_gate, *, name):
    S = proj.shape[0]
    D = D_MODEL
    tr = _pick(S, (256, 128, 64, 32, 16, 8))
    gb = GATE_COL0 // D

    def body(ga_ref, gs_ref, oa_ref, os_ref, wa_ref, ws_ref, ba_ref, bs_ref, m_ref):
        pa = _dot(oa_ref[...], wa_ref[...], 1, 0)
        ps = _dot(os_ref[...], ws_ref[...], 1, 0)
        sa = _sigmoid(ga_ref[...] + ba_ref[...])
        ss = _sigmoid(gs_ref[...] + bs_ref[...])
        m_ref[...] = (sa * pa + ss * ps).astype(BF16)

    return pl.pallas_call(
        body, name=name, out_shape=jax.ShapeDtypeStruct((S, D), BF16), grid=(S // tr,),
        in_specs=[pl.BlockSpec((tr, D), lambda i: (i, gb)), pl.BlockSpec((tr, D), lambda i: (i, gb + 1)),
                  pl.BlockSpec((tr, SB_WIDTH), lambda i: (i, 0)), pl.BlockSpec((tr, SSM_WIDTH), lambda i: (i, 0)),
                  pl.BlockSpec((SB_WIDTH, D), lambda i: (0, 0)), pl.BlockSpec((SSM_WIDTH, D), lambda i: (0, 0)),
                  pl.BlockSpec((1, D), lambda i: (0, 0)), pl.BlockSpec((1, D), lambda i: (0, 1))],
        out_specs=pl.BlockSpec((tr, D), lambda i: (i, 0)),
        compiler_params=_cparams("parallel"),
    )(proj, proj, o_attn, o_ssm, w_ba, w_bs, b_gate, b_gate)


def _merge_bwd(dmerged, proj, o_attn, o_ssm, w_ba, w_bs, b_gate, *, name):
    S = proj.shape[0]
    D = D_MODEL
    tr = _pick(S, (256, 128, 64, 32, 16, 8))
    gb = GATE_COL0 // D

    def body(dm_ref, ga_ref, gs_ref, oa_ref, os_ref, wa_ref, ws_ref, ba_ref, bs_ref,
             doa_ref, dos_ref, dg_ref, db_ref, dwa_ref, dws_ref):
        i = pl.program_id(0)
        dm = dm_ref[...]
        oa = oa_ref[...]
        osm = os_ref[...]
        pa = _dot(oa, wa_ref[...], 1, 0)
        ps = _dot(osm, ws_ref[...], 1, 0)
        sa = _sigmoid(ga_ref[...] + ba_ref[...])
        ss = _sigmoid(gs_ref[...] + bs_ref[...])
        dpa = (dm * sa).astype(BF16)
        dps = (dm * ss).astype(BF16)
        dga = dm * pa * sa * (1.0 - sa)
        dgs = dm * ps * ss * (1.0 - ss)
        dg_ref[:, :D] = dga.astype(BF16)
        dg_ref[:, D:] = dgs.astype(BF16)
        doa_ref[...] = _dot(dpa, wa_ref[...], 1, 1).astype(BF16)
        dos_ref[...] = _dot(dps, ws_ref[...], 1, 1)
        dwa = _dot(oa, dpa, 0, 0)
        dws = _dot(osm, dps, 0, 0)
        dba = jnp.sum(dga, axis=0, keepdims=True)
        dbs = jnp.sum(dgs, axis=0, keepdims=True)

        @pl.when(i == 0)
        def _():
            dwa_ref[...] = dwa
            dws_ref[...] = dws
            db_ref[:, :D] = dba
            db_ref[:, D:] = dbs

        @pl.when(i > 0)
        def _():
            dwa_ref[...] += dwa
            dws_ref[...] += dws
            db_ref[:, :D] += dba
            db_ref[:, D:] += dbs

    rowD = pl.BlockSpec((tr, D), lambda i: (i, 0))
    wspec = pl.BlockSpec((SB_WIDTH, D), lambda i: (0, 0))
    return pl.pallas_call(
        body, name=name,
        out_shape=(jax.ShapeDtypeStruct((S, SB_WIDTH), BF16), jax.ShapeDtypeStruct((S, SSM_WIDTH), F32),
                   jax.ShapeDtypeStruct((S, 2 * D), BF16), jax.ShapeDtypeStruct((1, 2 * D), F32),
                   jax.ShapeDtypeStruct((SB_WIDTH, D), F32), jax.ShapeDtypeStruct((SSM_WIDTH, D), F32)),
        grid=(S // tr,),
        in_specs=[rowD, pl.BlockSpec((tr, D), lambda i: (i, gb)), pl.BlockSpec((tr, D), lambda i: (i, gb + 1)),
                  pl.BlockSpec((tr, SB_WIDTH), lambda i: (i, 0)), pl.BlockSpec((tr, SSM_WIDTH), lambda i: (i, 0)),
                  wspec, wspec, pl.BlockSpec((1, D), lambda i: (0, 0)), pl.BlockSpec((1, D), lambda i: (0, 1))],
        out_specs=(pl.BlockSpec((tr, SB_WIDTH), lambda i: (i, 0)), pl.BlockSpec((tr, SSM_WIDTH), lambda i: (i, 0)),
                   pl.BlockSpec((tr, 2 * D), lambda i: (i, 0)), pl.BlockSpec((1, 2 * D), lambda i: (0, 0)),
                   wspec, wspec),
        compiler_params=_cparams("arbitrary"),
    )(dmerged, proj, proj, o_attn, o_ssm, w_ba, w_bs, b_gate, b_gate)


def _xattn_probs(q, k, h):
    cols = slice(h * XA_HEAD_DIM, (h + 1) * XA_HEAD_DIM)
    s = _dot(q[:, cols], k[:, cols], 1, 1) * (XA_HEAD_DIM ** -0.5)
    s = s - jnp.max(s, axis=-1, keepdims=True)
    e = jnp.exp(s)
    return e / jnp.sum(e, axis=-1, keepdims=True), cols


def _xattn_fwd(q2, k2, v2, *, name):
    S, D = q2.shape
    M = k2.shape[0]
    tr = _row_tile(S)

    def body(q_ref, k_ref, v_ref, o_ref):
        q = q_ref[...]
        k = k_ref[...]
        v = v_ref[...]
        for h in range(XA_HEADS):
            p, cols = _xattn_probs(q, k, h)
            o_ref[:, cols] = _dot(p, v[:, cols], 1, 0).astype(BF16)

    row = pl.BlockSpec((tr, D), lambda i: (i, 0))
    memb = pl.BlockSpec((M, D), lambda i: (0, 0))
    return pl.pallas_call(
        body, name=name, out_shape=jax.ShapeDtypeStruct((S, D), BF16), grid=(S // tr,),
        in_specs=[row, memb, memb], out_specs=row, compiler_params=_cparams("parallel"),
    )(q2, k2, v2)


def _xattn_bwd(q2, k2, v2, do2, *, name):
    S, D = q2.shape
    M = k2.shape[0]
    tr = _row_tile(S)
    scale = XA_HEAD_DIM ** -0.5

    def body(q_ref, k_ref, v_ref, do_ref, dq_ref, dk_ref, dv_ref):
        i = pl.program_id(0)

        @pl.when(i == 0)
        def _():
            dk_ref[...] = jnp.zeros_like(dk_ref)
            dv_ref[...] = jnp.zeros_like(dv_ref)

        q = q_ref[...]
        k = k_ref[...]
        v = v_ref[...]
        do = do_ref[...]
        for h in range(XA_HEADS):
            p, cols = _xattn_probs(q, k, h)
            dp = _dot(do[:, cols], v[:, cols], 1, 1)
            ds = (p * (dp - jnp.sum(dp * p, axis=-1, keepdims=True)) * scale).astype(BF16)
            dq_ref[:, cols] = _dot(ds, k[:, cols], 1, 0).astype(BF16)
            dk_ref[:, cols] += _dot(ds, q[:, cols], 0, 0)
            dv_ref[:, cols] += _dot(p, do[:, cols], 0, 0)

    row = pl.BlockSpec((tr, D), lambda i: (i, 0))
    memb = pl.BlockSpec((M, D), lambda i: (0, 0))
    return pl.pallas_call(
        body, name=name,
        out_shape=(jax.ShapeDtypeStruct((S, D), BF16), jax.ShapeDtypeStruct((M, D), F32), jax.ShapeDtypeStruct((M, D), F32)),
        grid=(S // tr,), in_specs=[row, memb, memb, row], out_specs=(row, memb, memb),
        compiler_params=_cparams("arbitrary"),
    )(q2, k2, v2, do2)


CONV_ROWS = 64
CONV_ROWS_FWD = 256


def _chunk(ref, c, rows):
    return ref[pl.ds(pl.multiple_of(c * rows, rows), rows), :]


def _rows_before(ref, c, rows):
    t0 = pl.multiple_of(jnp.maximum(c * rows - SUBLANES, 0), SUBLANES)
    return jnp.where(c > 0, ref[pl.ds(t0, SUBLANES), :], 0.0)


def _rows_after(ref, c, rows, n_chunks):
    t0 = pl.multiple_of(jnp.minimum((c + 1) * rows, n_chunks * rows - SUBLANES), SUBLANES)
    return jnp.where(c < n_chunks - 1, ref[pl.ds(t0, SUBLANES), :], 0.0)


def _shift_down(cur, before, d):
    out = pltpu.roll(cur, d, 0)
    r = lax.broadcasted_iota(jnp.int32, cur.shape, 0)
    for e in range(d):
        out = jnp.where(r == e, before[SUBLANES - d + e:SUBLANES - d + e + 1, :], out)
    return out


def _shift_up(cur, after, d):
    rows = cur.shape[0]
    out = pltpu.roll(cur, rows - d, 0)
    r = lax.broadcasted_iota(jnp.int32, cur.shape, 0)
    for e in range(d):
        out = jnp.where(r == rows - d + e, after[e:e + 1, :], out)
    return out


def _conv3(cur, before, w_ref, b_ref):
    return (w_ref[2:3, :] * cur + w_ref[1:2, :] * _shift_down(cur, before, 1)
            + w_ref[0:1, :] * _shift_down(cur, before, 2) + b_ref[...])


def _convgate_fwd(up_g, up_v, conv_w, conv_b, *, name):
    S, H = up_g.shape
    nb = H // LANES
    R = min(CONV_ROWS_FWD, S)
    n_chunks = S // R

    def body(g_ref, v_ref, wg_ref, wv_ref, bg_ref, bv_ref, a_ref):
        def chunk(c, _):
            cg = _conv3(_chunk(g_ref, c, R), _rows_before(g_ref, c, R), wg_ref, bg_ref)
            cv = _conv3(_chunk(v_ref, c, R), _rows_before(v_ref, c, R), wv_ref, bv_ref)
            a_ref[pl.ds(pl.multiple_of(c * R, R), R), :] = (_gelu(cg) * cv).astype(BF16)
            return 0

        lax.fori_loop(0, n_chunks, chunk, 0)

    col = lambda off: pl.BlockSpec((S, LANES), lambda j: (0, off + j))
    wcol = lambda off: pl.BlockSpec((3, LANES), lambda j: (0, off + j))
    bcol = lambda off: pl.BlockSpec((1, LANES), lambda j: (0, off + j))
    return pl.pallas_call(
        body, name=name, out_shape=jax.ShapeDtypeStruct((S, H), BF16), grid=(nb,),
        in_specs=[col(0), col(0), wcol(0), wcol(nb), bcol(0), bcol(nb)],
        out_specs=col(0), compiler_params=_cparams("parallel"),
    )(up_g, up_v, conv_w, conv_w, conv_b, conv_b)


def _convgate_bwd(up_g, up_v, da, conv_w, conv_b, *, name):
    S, H = up_g.shape
    nb = H // LANES
    R = min(CONV_ROWS, S)
    n_chunks = S // R

    def fold(a):
        return sum(a[r:r + SUBLANES] for r in range(0, a.shape[0], SUBLANES))

    def body(g_ref, v_ref, da_ref, wg_ref, wv_ref, bg_ref, bv_ref,
             dug_ref, duv_ref, dwg_ref, dwv_ref, dbg_ref, dbv_ref, dcg_s, dcv_s):
        def first_pass(c, acc):
            rows = pl.ds(pl.multiple_of(c * R, R), R)
            ug, uv = _chunk(g_ref, c, R), _chunk(v_ref, c, R)
            bg, bv = _rows_before(g_ref, c, R), _rows_before(v_ref, c, R)
            cg = _conv3(ug, bg, wg_ref, bg_ref)
            cv = _conv3(uv, bv, wv_ref, bv_ref)
            da = da_ref[rows, :]
            gl, dgl = _gelu_and_grad(cg)
            dcg = da * cv * dgl
            dcv = da * gl
            dcg_s[rows, :] = dcg
            dcv_s[rows, :] = dcv
            new = []
            for dc, u, before in ((dcg, ug, bg), (dcv, uv, bv)):
                new += [fold(dc * _shift_down(u, before, 2)), fold(dc * _shift_down(u, before, 1)), fold(dc * u), fold(dc)]
            return tuple(a + n for a, n in zip(acc, new))

        zero = jnp.zeros((SUBLANES, LANES), F32)
        acc = lax.fori_loop(0, n_chunks, first_pass, (zero,) * 8)
        total = [jnp.sum(a, axis=0, keepdims=True) for a in acc]
        for k, (dw_ref, db_ref) in enumerate(((dwg_ref, dbg_ref), (dwv_ref, dbv_ref))):
            dw_ref[0:1, :] = total[4 * k]
            dw_ref[1:2, :] = total[4 * k + 1]
            dw_ref[2:3, :] = total[4 * k + 2]
            db_ref[...] = total[4 * k + 3]

        def second_pass(c, _):
            rows = pl.ds(pl.multiple_of(c * R, R), R)
            for dc_s, w_ref, du_ref in ((dcg_s, wg_ref, dug_ref), (dcv_s, wv_ref, duv_ref)):
                cur, after = _chunk(dc_s, c, R), _rows_after(dc_s, c, R, n_chunks)
                du = w_ref[2:3, :] * cur + w_ref[1:2, :] * _shift_up(cur, after, 1) + w_ref[0:1, :] * _shift_up(cur, after, 2)
                du_ref[rows, :] = du.astype(BF16)
            return 0

        lax.fori_loop(0, n_chunks, second_pass, 0)

    col = lambda off: pl.BlockSpec((S, LANES), lambda j: (0, off + j))
    wcol = lambda off: pl.BlockSpec((3, LANES), lambda j: (0, off + j))
    bcol = lambda off: pl.BlockSpec((1, LANES), lambda j: (0, off + j))
    return pl.pallas_call(
        body, name=name,
        out_shape=(jax.ShapeDtypeStruct((S, H), BF16), jax.ShapeDtypeStruct((S, H), BF16),
                   jax.ShapeDtypeStruct((3, H), F32), jax.ShapeDtypeStruct((3, H), F32),
                   jax.ShapeDtypeStruct((1, H), F32), jax.ShapeDtypeStruct((1, H), F32)),
        grid=(nb,),
        in_specs=[col(0), col(0), col(0), wcol(0), wcol(nb), bcol(0), bcol(nb)],
        out_specs=(col(0), col(0), wcol(0), wcol(0), bcol(0), bcol(0)),
        scratch_shapes=[pltpu.VMEM((S, LANES), F32), pltpu.VMEM((S, LANES), F32)],
        compiler_params=_cparams("parallel"),
    )(up_g, up_v, da, conv_w, conv_w, conv_b, conv_b)


def _local_step(x, mem, target, w_in, late_wire, P, core):
    mm = _matmul
    h1, (w_in,) = _rms_fwd(x, P["norm_mix_pre"], name="rms_mix_pre", rider=_fill_xy([w_in]))
    w_in, = _fill_c([w_in]).run(name="gather_in_c")
    w_in = w_in.reshape((N_DEV,) + w_in.shape[2:])
    n_mid = len(LATE) - len(REDUCE_FFN)
    proj, wire_mid = mm(h1, w_in, name="mm_in", rider=_fill_xy(late_wire[:n_mid]))
    (o_attn, sb_tot, sb_first), wires = _sb_fwd(
        proj, name="sb_fwd", rider=_Exchange.join(_fill_c(wire_mid), _fill_xy(late_wire[n_mid:])))
    wire_mid, wire_ffn = wires[:n_mid], wires[n_mid:]

    ssm_prep = lambda *a: _ssm_prepare(*a)
    (lam_re, lam_im, bb_re, bb_im), prep_vjp = jax.vjp(
        ssm_prep, P["ssm_a_re"], P["ssm_a_im"], P["ssm_log_dt"], P["ssm_b_re"], P["ssm_b_im"])
    tab_f, tab_b = _ssm_tables(lam_re, lam_im)
    bd_re = _bd_from_bbar(bb_re).astype(BF16)
    bd_im = _bd_from_bbar(bb_im).astype(BF16)
    cd_re = _cd_from_c(P["ssm_c_re"]).astype(BF16)
    cd_imneg = _cd_from_c(-P["ssm_c_im"]).astype(BF16)
    (y_pre, x_re, x_im), wire_ffn = _ssm_fwd(proj, bd_re, bd_im, cd_re, cd_imneg, P["ssm_d"], tab_f,
                                             name="ssm_fwd", rider=_fill_c(wire_ffn))
    W = _weights_from_wire(dict(zip(LATE, list(wire_mid) + list(wire_ffn))))
    W["w_in"] = w_in
    o_ssm = _glu_fwd(y_pre, W["ssm_w_glu"], P["ssm_b_glu"], name="glu_fwd")

    merged = _merge_fwd(proj, o_attn, o_ssm, W["w_branch_attn"], W["w_branch_ssm"], P["b_gate"], name="merge_fwd")
    mo = mm(merged, W["w_out"], name="mm_out")
    x1, h2 = _resnorm_norm(x, mo, P["norm_mix_post"], P["norm_xa_pre"], name="resnorm_1")

    mem_n = _rms_fwd(mem, P["norm_mem"], name="rms_mem")
    q2 = mm(h2, W["xa_wq"], out_dtype=BF16, name="mm_xq")
    k2 = mm(mem_n, W["xa_wk"], out_dtype=BF16, name="mm_xk")
    v2 = mm(mem_n, W["xa_wv"], out_dtype=BF16, name="mm_xv")
    o2 = _xattn_fwd(q2, k2, v2, name="xattn_fwd")
    xa = mm(o2, W["xa_wo"], name="mm_xo")
    x2, h3 = _resnorm_norm(x1, xa, P["norm_xa_post"], P["norm_ffn_pre"], name="resnorm_2")

    half = N_DEV // 2
    up_g = mm(h3, W["ffn_w_up"], n_blocks=half, name="mm_up_g")
    up_v = mm(h3, W["ffn_w_up"], b_block0=half, name="mm_up_v")
    act = _convgate_fwd(up_g, up_v, W["ffn_conv_w"], P["ffn_conv_b"], name="convgate_fwd")
    f = mm(act, W["ffn_w_down"], name="mm_down")
    loss, dy, df, dg_ffn_post = _final_loss(x2, f, P["norm_ffn_post"], target, name="final_loss")

    G = {"norm_ffn_post": dg_ffn_post}
    dact = mm(df, W["ffn_w_down"], tb=True, name="mm_down_dx")
    G["ffn_w_down"] = mm(act, df, ta=True, name="mm_down_dw")
    dug, duv, dwg, dwv, dbg, dbv = _convgate_bwd(up_g, up_v, dact, W["ffn_conv_w"], P["ffn_conv_b"], name="convgate_bwd")
    G["ffn_conv_w"] = jnp.concatenate([dwg, dwv], axis=1)
    G["ffn_conv_b"] = jnp.concatenate([dbg, dbv], axis=1)
    dh3 = mm(dug, W["ffn_w_up"], tb=True, n_blocks=half, name="mm_up_g_dx")
    dh3 = mm(duv, W["ffn_w_up"], tb=True, b_block0=half, acc_in=dh3, name="mm_up_v_dx")
    dw_up = mm(h3, dug, ta=True, out_into=lax.empty(W["ffn_w_up"].shape, F32), name="mm_up_g_dw")
    G["ffn_w_up"] = mm(h3, duv, ta=True, out_into=dw_up, out_block0=half, name="mm_up_v_dw")
    blocks = {n: _grad_blocks(n, G[n]) for n in REDUCE_FFN}
    (dx2, dxa, G["norm_ffn_pre"], G["norm_xa_post"]), from_core = _norm_bwd_pair(
        dy, dh3, x2, P["norm_ffn_pre"], xa, P["norm_xa_post"], name="norm_bwd_3",
        rider=_send_c([blocks[n] for n in REDUCE_FFN]))
    pair = {n: _pair_sum(blocks[n], r, core, name="pair_sum_" + n) for n, r in zip(REDUCE_FFN, from_core)}

    G["xa_wo"] = mm(o2, dxa, ta=True, name="mm_xo_dw")
    do2 = mm(dxa, W["xa_wo"], tb=True, out_dtype=BF16, name="mm_xo_dx")
    dq2, dk2, dv2 = _xattn_bwd(q2, k2, v2, do2, name="xattn_bwd")
    G["xa_wq"] = mm(h2, dq2, ta=True, name="mm_xq_dw")
    dh2 = mm(dq2, W["xa_wq"], tb=True, name="mm_xq_dx")
    G["xa_wk"] = mm(mem_n, dk2, ta=True, name="mm_xk_dw")
    G["xa_wv"] = mm(mem_n, dv2, ta=True, name="mm_xv_dw")
    dmem_n = jnp.concatenate([dk2, dv2], axis=1)
    wkv = jnp.concatenate([W["xa_wk"], W["xa_wv"]], axis=1)
    dmem = mm(dmem_n, wkv, tb=True, name="mm_xkv_dx")
    _, G["norm_mem"] = _norm_bwd_single(None, dmem, mem, P["norm_mem"], name="norm_bwd_mem")
    (dx1, dmo, G["norm_xa_pre"], G["norm_mix_post"]), _ = _norm_bwd_pair(
        dx2, dh2, x1, P["norm_xa_pre"], mo, P["norm_mix_post"], name="norm_bwd_2")

    G["w_out"] = mm(merged, dmo, ta=True, name="mm_out_dw")
    dmerged = mm(dmo, W["w_out"], tb=True, name="mm_out_dx")
    do_attn, do_ssm, dgate, G["b_gate"], G["w_branch_attn"], G["w_branch_ssm"] = _merge_bwd(
        dmerged, proj, o_attn, o_ssm, W["w_branch_attn"], W["w_branch_ssm"], P["b_gate"], name="merge_bwd")
    dy_pre, G["ssm_w_glu"], G["ssm_b_glu"] = _glu_bwd(y_pre, do_ssm, W["ssm_w_glu"], P["ssm_b_glu"], name="glu_bwd")
    blocks.update({n: _grad_blocks(n, G[n]) for n in REDUCE_MID})
    (du, dbd_re, dbd_im, dcd_re, dcd_imneg, G["ssm_d"], dl_re, dl_im), brought = _ssm_bwd(
        dy_pre, proj, x_re, x_im, bd_re, bd_im, cd_re, cd_imneg, P["ssm_d"], tab_b, name="ssm_bwd",
        rider=_Exchange.join(_send_c([blocks[n] for n in REDUCE_MID]), _scatter_xy([pair[n] for n in REDUCE_FFN])))
    from_core, from_chips = brought[:len(REDUCE_MID)], brought[len(REDUCE_MID):]
    reduced = {n: (pair[n], parts) for n, parts in zip(REDUCE_FFN, from_chips)}
    pair.update({n: _pair_sum(blocks[n], r, core, name="pair_sum_" + n) for n, r in zip(REDUCE_MID, from_core)})
    G["ssm_c_re"] = _c_from_cd(dcd_re)
    G["ssm_c_im"] = -_c_from_cd(dcd_imneg)
    dlam_re = jnp.sum(dl_re, axis=1).reshape(SSM_GROUPS, SSM_STATE)
    dlam_im = jnp.sum(dl_im, axis=1).reshape(SSM_GROUPS, SSM_STATE)
    (G["ssm_a_re"], G["ssm_a_im"], G["ssm_log_dt"], G["ssm_b_re"], G["ssm_b_im"]) = prep_vjp(
        (dlam_re, dlam_im, _bbar_from_bd(dbd_re), _bbar_from_bd(dbd_im)))
    G["ffn_conv_b"] = G["ffn_conv_b"].reshape(N_DEV, FF_LOCAL_PAD)[:, :FF_LOCAL]
    small = [G[n].reshape(SMALL_SHAPE[n]) for n in SMALL_EARLY]
    (dq, dk, dv), brought = _sb_bwd(
        proj, sb_tot, sb_first, do_attn, name="sb_bwd",
        rider=_Exchange.join(_scatter_xy([pair[n] for n in REDUCE_MID]), _gather_xy_from(small)))
    from_chips, small = brought[:len(REDUCE_MID)], brought[len(REDUCE_MID):]
    reduced.update({n: (pair[n], parts) for n, parts in zip(REDUCE_MID, from_chips)})
    dproj = jnp.concatenate([dq, dk, dv, du, dgate], axis=1)
    G["w_in"], small = mm(h1, dproj, ta=True, out_cb=W["w_in"].shape[2], name="mm_in_dw", rider=_fill_c(small))
    g_in = _grad_blocks("w_in", G["w_in"])
    dh1, (from_core,) = mm(dproj, W["w_in"], tb=True, name="mm_in_dx", rider=_send_c([g_in]))
    pair_in = _pair_sum(g_in, from_core, core, name="pair_sum_w_in")
    (grad_x, dg_pre), (from_chips,) = _norm_bwd_single(dx1, dh1, x, P["norm_mix_pre"], name="norm_bwd_1",
                                                       rider=_scatter_xy([pair_in]))
    reduced["w_in"] = (pair_in, from_chips)
    last, = _gather_all([dg_pre]).run(name="gather_g_last")
    parts = dict(zip(SMALL_EARLY, small))
    parts["norm_mix_pre"] = last
    return loss, grad_x, parts, reduced


MESH = pl.DeviceIdType.MESH
_HBM = pl.BlockSpec(memory_space=pl.ANY)
N_XY = 4
N_XY_PEERS = 3


def _xy_peers(x, y):
    return [(1 - x, y), (x, 1 - y), (1 - x, 1 - y)]


class _Exchange:
    def __init__(self, arrays, out_shapes, plan, n_copies, alias):
        self.arrays = list(arrays)
        self.out_shapes = list(out_shapes)
        self.plan = plan
        self.n_copies = n_copies
        self.alias = list(alias) if isinstance(alias, (list, tuple)) else [alias] * len(self.arrays)

    @property
    def n(self):
        return len(self.arrays)

    def aliases(self, first_in, first_out):
        return {first_in + k: first_out + k for k in range(self.n) if self.alias[k]}

    @staticmethod
    def join(a, b):
        def plan(k, src, dst, x, y, c):
            return a.plan(k, src, dst, x, y, c) if k < a.n else b.plan(k - a.n, src, dst, x, y, c)

        return _Exchange(a.arrays + b.arrays, a.out_shapes + b.out_shapes, plan, max(a.n_copies, b.n_copies),
                         a.alias + b.alias)

    def sems(self):
        shape = (self.n, self.n_copies)
        return [pltpu.SemaphoreType.DMA(shape), pltpu.SemaphoreType.DMA(shape)]

    def _copies(self, ins, outs, send_sems, recv_sems):
        x, y, c = lax.axis_index("x"), lax.axis_index("y"), lax.axis_index("c")
        sends, lands, own = [], [], []
        for k in range(self.n):
            for j, (src, dst, dev, land) in enumerate(self.plan(k, ins[k], outs[k], x, y, c)):
                if dev is None:
                    own.append(pltpu.make_async_copy(src, dst, send_sems.at[k, j]))
                    continue
                sems = dict(send_sem=send_sems.at[k, j], recv_sem=recv_sems.at[k, j], device_id=dev, device_id_type=MESH)
                sends.append(pltpu.make_async_remote_copy(src_ref=src, dst_ref=dst, **sems))
                lands.append(pltpu.make_async_remote_copy(src_ref=src, dst_ref=land, **sems))
        return sends, lands, own

    def start(self, ins, outs, send_sems, recv_sems):
        sends, _, own = self._copies(ins, outs, send_sems, recv_sems)
        for cp in own + sends:
            cp.start()

    def finish(self, ins, outs, send_sems, recv_sems):
        sends, lands, own = self._copies(ins, outs, send_sems, recv_sems)
        for cp in lands:
            cp.wait_recv()
        for cp in sends:
            cp.wait_send()
        for cp in own:
            cp.wait()

    def run(self, *, name):
        n = self.n

        def body(*refs):
            parts = (refs[:n], refs[n:2 * n], refs[2 * n], refs[2 * n + 1])
            self.start(*parts)
            self.finish(*parts)

        return pl.pallas_call(
            body, name=name, out_shape=tuple(self.out_shapes),
            in_specs=[_HBM] * n, out_specs=tuple([_HBM] * n),
            input_output_aliases=self.aliases(0, 0),
            scratch_shapes=self.sems(),
        )(*self.arrays)


def _call(host_body, *, name, grid, in_specs, out_specs, out_shape, scratch_shapes, operands, rider=None):
    out_specs, out_shape = tuple(out_specs), tuple(out_shape)
    if rider is None:
        res = pl.pallas_call(
            host_body, name=name, grid=grid, in_specs=list(in_specs), out_specs=out_specs, out_shape=out_shape,
            scratch_shapes=list(scratch_shapes), compiler_params=_cparams(*["arbitrary"] * len(grid)),
        )(*operands)
        return tuple(res), None
    n, n_in, n_out, n_scr = rider.n, len(in_specs), len(out_specs), len(scratch_shapes)

    def body(*refs):
        pos = [0]

        def take(count):
            pos[0] += count
            return refs[pos[0] - count:pos[0]]

        h_in, r_in, h_out, r_out, h_scr = take(n_in), take(n), take(n_out), take(n), take(n_scr)
        send_sems, recv_sems = take(2)
        ids = [pl.program_id(a) for a in range(len(grid))]
        first = functools.reduce(jnp.logical_and, [i == 0 for i in ids])
        last = functools.reduce(jnp.logical_and, [i == g - 1 for i, g in zip(ids, grid)])

        @pl.when(first)
        def _():
            rider.start(r_in, r_out, send_sems, recv_sems)

        host_body(*h_in, *h_out, *h_scr)

        @pl.when(last)
        def _():
            rider.finish(r_in, r_out, send_sems, recv_sems)

    res = pl.pallas_call(
        body, name=name, grid=grid,
        in_specs=list(in_specs) + [_HBM] * n, out_specs=out_specs + tuple([_HBM] * n),
        out_shape=out_shape + tuple(rider.out_shapes),
        input_output_aliases=rider.aliases(n_in, n_out),
        scratch_shapes=list(scratch_shapes) + rider.sems(),
        compiler_params=_cparams(*["arbitrary"] * len(grid)),
    )(*operands, *rider.arrays)
    return tuple(res[:n_out]), list(res[n_out:])


def _same(arrays):
    return [jax.ShapeDtypeStruct(a.shape, a.dtype) for a in arrays]


def _fill_xy(bufs):
    def plan(k, src, dst, x, y, c):
        mine = 2 * x + y
        return [(src.at[mine, c], dst.at[mine, c], (px, py, c), dst.at[2 * px + py, c]) for px, py in _xy_peers(x, y)]

    return _Exchange(bufs, _same(bufs), plan, N_XY_PEERS, alias=True)


def _fill_c(bufs):
    def plan(k, src, dst, x, y, c):
        return [(src.at[:, c], dst.at[:, c], (x, y, 1 - c), dst.at[:, 1 - c])]

    return _Exchange(bufs, _same(bufs), plan, 1, alias=True)


def _slots(arrays):
    return [jax.ShapeDtypeStruct((N_XY, 2) + a.shape, a.dtype) for a in arrays]


def _gather_xy_from(srcs):
    def plan(k, src, dst, x, y, c):
        mine = 2 * x + y
        return ([(src, dst.at[mine, c], None, None)]
                + [(src, dst.at[mine, c], (px, py, c), dst.at[2 * px + py, c]) for px, py in _xy_peers(x, y)])

    return _Exchange(srcs, _slots(srcs), plan, 1 + N_XY_PEERS, alias=False)


def _gather_all(srcs):
    def plan(k, src, dst, x, y, c):
        mine = 2 * x + y
        out = [(src, dst.at[mine, c], None, None)]
        for fx, fy, fc in [(a, b, e) for a in (0, 1) for b in (0, 1) for e in (0, 1)][1:]:
            px, py, pc = (1 - x) if fx else x, (1 - y) if fy else y, (1 - c) if fc else c
            out.append((src, dst.at[mine, c], (px, py, pc), dst.at[2 * px + py, pc]))
        return out

    return _Exchange(srcs, _slots(srcs), plan, N_DEV, alias=False)


def _send_c(srcs):
    def plan(k, src, dst, x, y, c):
        return [(src.at[:, 1 - c], dst, (x, y, 1 - c), dst)]

    outs = [jax.ShapeDtypeStruct(a.shape[:1] + a.shape[2:], a.dtype) for a in srcs]
    return _Exchange(srcs, outs, plan, 1, alias=False)


def _scatter_xy(srcs):
    def plan(k, src, dst, x, y, c):
        return [(src.at[2 * px + py], dst.at[j], (px, py, c), dst.at[j]) for j, (px, py) in enumerate(_xy_peers(x, y))]

    outs = [jax.ShapeDtypeStruct((N_XY_PEERS,) + a.shape[1:], a.dtype) for a in srcs]
    return _Exchange(srcs, outs, plan, N_XY_PEERS, alias=False)


WIRE_DTYPE = BF16


def _pair_sum(g8, recv, core, *, name):
    n, _, R, C = g8.shape
    tr = _pick(R, (128, 64, 32, 16, 8))

    def body(core_ref, a_ref, b_ref, o_ref):
        o_ref[...] = (a_ref[0] + b_ref[...]).astype(WIRE_DTYPE)

    return pl.pallas_call(
        body, name=name, out_shape=jax.ShapeDtypeStruct((n, R, C), WIRE_DTYPE),
        grid_spec=pltpu.PrefetchScalarGridSpec(
            num_scalar_prefetch=1, grid=(n, R // tr),
            in_specs=[pl.BlockSpec((1, 1, tr, C), lambda s, i, core_ref: (s, core_ref[0], i, 0)),
                      pl.BlockSpec((1, tr, C), lambda s, i, core_ref: (s, i, 0))],
            out_specs=pl.BlockSpec((1, tr, C), lambda s, i, core_ref: (s, i, 0))),
        compiler_params=_cparams("parallel", "parallel"),
    )(core, g8, recv)


def _adamw_math(w, g, m, v):
    m = ADAM_B1 * m + (1.0 - ADAM_B1) * g
    v = ADAM_B2 * v + (1.0 - ADAM_B2) * (g * g)
    m_hat = m / (1.0 - ADAM_B1 ** ADAM_STEP)
    v_hat = v / (1.0 - ADAM_B2 ** ADAM_STEP)
    delta = -ADAM_LR * (m_hat / (jnp.sqrt(v_hat) + ADAM_EPS) + ADAM_WD * w)
    return delta, m, v


def _reduce_adamw(parts, w, m, v, *, own, own_slot, name):
    n, R, C = parts.shape
    tr = _pick(R, (128, 64, 32, 16, 8))

    def body(_, own_ref, parts_ref, w_ref, m_ref, v_ref, g_ref, d_ref, nm_ref, nv_ref):
        g = own_ref[0].astype(F32)
        for k in range(n):
            g = g + parts_ref[k].astype(F32)
        g_ref[...] = g
        d_ref[...], nm_ref[...], nv_ref[...] = _adamw_math(w_ref[...], g, m_ref[...], v_ref[...])

    out = jax.ShapeDtypeStruct((R, C), F32)
    row = pl.BlockSpec((tr, C), lambda i, s: (i, 0))
    return pl.pallas_call(
        body, name=name, out_shape=(out, out, out, out),
        grid_spec=pltpu.PrefetchScalarGridSpec(
            num_scalar_prefetch=1, grid=(R // tr,),
            in_specs=[pl.BlockSpec((1, tr, C), lambda i, s: (s[0], i, 0)),
                      pl.BlockSpec((n, tr, C), lambda i, s: (0, i, 0)), row, row, row],
            out_specs=(row, row, row, row)),
        compiler_params=_cparams("parallel"),
    )(own_slot, own, parts, w, m, v)


SHARDED = (("w_in", (1024, 4096), 1), ("ssm_w_glu", (512, 512), 0), ("w_branch_attn", (512, 1024), 1),
           ("w_branch_ssm", (512, 1024), 1), ("w_out", (1024, 1024), 0), ("xa_wq", (1024, 1024), 0),
           ("xa_wk", (1024, 1024), 0), ("xa_wv", (1024, 1024), 0), ("xa_wo", (1024, 1024), 0),
           ("ffn_w_up", (1024, 5632), 1), ("ffn_conv_w", (3, 5632), 1), ("ffn_w_down", (2816, 1024), 0))
REPLICATED = (("norm_mix_pre", (1024,)), ("norm_mix_post", (1024,)), ("b_gate", (2048,)), ("ssm_a_re", (32, 64)),
              ("ssm_a_im", (32, 64)), ("ssm_log_dt", (32,)), ("ssm_b_re", (32, 64, 16)), ("ssm_b_im", (32, 64, 16)),
              ("ssm_c_re", (32, 16, 64)), ("ssm_c_im", (32, 16, 64)), ("ssm_d", (512,)), ("ssm_b_glu", (512,)),
              ("norm_xa_pre", (1024,)), ("norm_xa_post", (1024,)), ("norm_mem", (1024,)), ("norm_ffn_pre", (1024,)),
              ("norm_ffn_post", (1024,)), ("ffn_conv_b", (5632,)))
PARAM_ORDER = ("norm_mix_pre", "norm_mix_post", "w_in", "b_gate", "ssm_a_re", "ssm_a_im", "ssm_log_dt", "ssm_b_re",
               "ssm_b_im", "ssm_c_re", "ssm_c_im", "ssm_d", "ssm_w_glu", "ssm_b_glu", "w_branch_attn", "w_branch_ssm",
               "w_out", "norm_xa_pre", "norm_xa_post", "norm_mem", "xa_wq", "xa_wk", "xa_wv", "xa_wo", "norm_ffn_pre",
               "norm_ffn_post", "ffn_w_up", "ffn_conv_w", "ffn_conv_b", "ffn_w_down")
FF_LOCAL = 2 * D_FF // N_DEV
FF_LOCAL_PAD = 768
FF_PAD = (N_DEV // 2) * FF_LOCAL_PAD


def _local_shape(shape, axis):
    return tuple(s // N_DEV if a == axis else s for a, s in enumerate(shape))


def _pad_cols(a, width):
    return jnp.pad(a, [(0, 0)] * (a.ndim - 1) + [(0, width - a.shape[-1])])


def _blocks_to_cols(a8):
    return a8.transpose(1, 0, 2).reshape(a8.shape[1], N_DEV * a8.shape[2])


def _cols_to_blocks(a, cb):
    return a.reshape(a.shape[0], N_DEV, cb).transpose(1, 0, 2)


FF_PADDED = ("ffn_w_up", "ffn_conv_w")
LATE = tuple(n for n, _, _ in SHARDED if n != "w_in")
REDUCE_FFN = ("ffn_w_up", "ffn_conv_w", "ffn_w_down")
REDUCE_MID = ("xa_wo", "xa_wq", "xa_wk", "xa_wv", "w_out", "w_branch_attn", "w_branch_ssm", "ssm_w_glu")
SHARD_AXIS = {n: ax for n, _, ax in SHARDED}
FULL_SHAPE = {n: s for n, s, _ in SHARDED}


def _as_local(n, a):
    return _pad_cols(a, FF_LOCAL_PAD) if n in FF_PADDED else a


def _weights_from_wire(wire):
    full = {n: b.reshape((N_DEV,) + b.shape[2:]) for n, b in wire.items()}
    W = {n: a.reshape(FULL_SHAPE[n]) if SHARD_AXIS[n] == 0 else a for n, a in full.items()}
    for n in ("w_branch_attn", "w_branch_ssm", "ffn_conv_w"):
        W[n] = _blocks_to_cols(full[n])
    W["ffn_w_down"] = jnp.pad(W["ffn_w_down"].reshape(N_DEV // 2, FF_LOCAL, D_MODEL),
                              ((0, 0), (0, FF_LOCAL_PAD - FF_LOCAL), (0, 0))).reshape(FF_PAD, D_MODEL)
    return W


def _grad_blocks(n, g):
    if n in ("w_branch_attn", "w_branch_ssm"):
        g = _cols_to_blocks(g, D_MODEL // N_DEV)
    elif n == "ffn_conv_w":
        g = _cols_to_blocks(g, FF_LOCAL_PAD)
    elif n == "ffn_w_down":
        g = g.reshape(N_DEV // 2, FF_LOCAL_PAD, D_MODEL)[:, :FF_LOCAL]
    local = _local_shape(FULL_SHAPE[n], SHARD_AXIS[n])
    if n in FF_PADDED:
        local = local[:-1] + (FF_LOCAL_PAD,)
    return g.reshape((N_XY, 2) + local)


SMALL_SHAPE = {n: (1, s[0]) if len(s) == 1 else (s[0], math.prod(s[1:])) for n, s in REPLICATED}
SMALL_SHAPE["ffn_conv_b"] = (N_DEV, FF_LOCAL)
SMALL_EARLY = tuple(n for n, _ in REPLICATED if n != "norm_mix_pre")


def _adamw_replicated(parts, w, m, v, *, name):
    n = len(parts)

    def body(*refs):
        p_refs, w_refs, m_refs, v_refs = (refs[i * n:(i + 1) * n] for i in range(4))
        outs = refs[4 * n:]
        for k in range(n):
            g = p_refs[k][0, 0]
            for s in range(1, N_DEV):
                g = g + p_refs[k][s // 2, s % 2]
            d, nm, nv = _adamw_math(w_refs[k][...], g, m_refs[k][...], v_refs[k][...])
            for slot, val in enumerate((g, d, nm, nv)):
                outs[slot * n + k][...] = val

    vmem = pl.BlockSpec(memory_space=pltpu.VMEM)
    shapes = [jax.ShapeDtypeStruct(a.shape, F32) for a in w] * 4
    res = pl.pallas_call(
        body, name=name, out_shape=tuple(shapes), in_specs=[vmem] * (4 * n), out_specs=tuple([vmem] * (4 * n)),
        compiler_params=pltpu.CompilerParams(vmem_limit_bytes=VMEM_LIMIT),
    )(*parts, *w, *m, *v)
    return [list(res[i * n:(i + 1) * n]) for i in range(4)]


def kernel(x, mem, norm_mix_pre, norm_mix_post, w_in, b_gate, ssm_a_re, ssm_a_im, ssm_log_dt, ssm_b_re, ssm_b_im, ssm_c_re, ssm_c_im, ssm_d, ssm_w_glu, ssm_b_glu, w_branch_attn, w_branch_ssm, w_out, norm_xa_pre, norm_xa_post, norm_mem, xa_wq, xa_wk, xa_wv, xa_wo, norm_ffn_pre, norm_ffn_post, ffn_w_up, ffn_conv_w, ffn_conv_b, ffn_w_down, loss_target, m_norm_mix_pre, m_norm_mix_post, m_w_in, m_b_gate, m_ssm_a_re, m_ssm_a_im, m_ssm_log_dt, m_ssm_b_re, m_ssm_b_im, m_ssm_c_re, m_ssm_c_im, m_ssm_d, m_ssm_w_glu, m_ssm_b_glu, m_w_branch_attn, m_w_branch_ssm, m_w_out, m_norm_xa_pre, m_norm_xa_post, m_norm_mem, m_xa_wq, m_xa_wk, m_xa_wv, m_xa_wo, m_norm_ffn_pre, m_norm_ffn_post, m_ffn_w_up, m_ffn_conv_w, m_ffn_conv_b, m_ffn_w_down, v_norm_mix_pre, v_norm_mix_post, v_w_in, v_b_gate, v_ssm_a_re, v_ssm_a_im, v_ssm_log_dt, v_ssm_b_re, v_ssm_b_im, v_ssm_c_re, v_ssm_c_im, v_ssm_d, v_ssm_w_glu, v_ssm_b_glu, v_w_branch_attn, v_w_branch_ssm, v_w_out, v_norm_xa_pre, v_norm_xa_post, v_norm_mem, v_xa_wq, v_xa_wk, v_xa_wv, v_xa_wo, v_norm_ffn_pre, v_norm_ffn_post, v_ffn_w_up, v_ffn_conv_w, v_ffn_conv_b, v_ffn_w_down):
    args = dict(locals())
    w_loc = {n: args[n][0] for n in PARAM_ORDER}
    m_loc = {n: args["m_" + n][0] for n in PARAM_ORDER}
    v_loc = {n: args["v_" + n][0] for n in PARAM_ORDER}
    core_i = lax.axis_index("c")
    chip_i = 2 * lax.axis_index("x") + lax.axis_index("y")
    core = core_i.astype(jnp.int32).reshape(1)
    chip = chip_i.astype(jnp.int32).reshape(1)

    def in_place(a):
        buf = lax.empty((N_XY, 2) + a.shape, a.dtype)
        return lax.dynamic_update_slice(buf, a[None, None], (chip_i, core_i) + (0,) * a.ndim)

    as_wire = lambda n: in_place(_as_local(n, w_loc[n]).astype(F32 if n == "ffn_conv_w" else BF16))

    P = {}
    for n, shape in REPLICATED:
        P[n] = w_loc[n] if len(shape) > 1 or n == "ssm_log_dt" else w_loc[n].reshape(1, -1)
    P["ffn_conv_b"] = _pad_cols(w_loc["ffn_conv_b"].reshape(N_DEV, FF_LOCAL), FF_LOCAL_PAD).reshape(1, 2 * FF_PAD)

    loss, grad_x, small_parts, reduced = _local_step(x[0], mem[0], loss_target[0], as_wire("w_in"),
                                                     [as_wire(n) for n in LATE], P, core)
    loss = lax.psum(loss[0, 0], ("x", "y", "c"))

    big_out = {}
    for n, (own, parts) in reduced.items():
        res = _reduce_adamw(parts, _as_local(n, w_loc[n]), _as_local(n, m_loc[n]), _as_local(n, v_loc[n]),
                            own=own, own_slot=chip, name="adamw_" + n)
        big_out[n] = [r[:, :FF_LOCAL] if n in FF_PADDED else r for r in res]

    names = [n for n, _ in REPLICATED]
    as_small = lambda d: [d[n].reshape(SMALL_SHAPE[n]) for n in names]
    small_out = _adamw_replicated([small_parts[n] for n in names], as_small(w_loc), as_small(m_loc), as_small(v_loc),
                                  name="adamw_replicated")
    small_out = [dict(zip(names, res)) for res in small_out]

    outs = [loss, grad_x[None]]
    for k in range(4):
        for n in PARAM_ORDER:
            src = big_out[n][k] if n in big_out else small_out[k][n]
            outs.append(src.reshape(args[n].shape))
    return tuple(outs)
```

```python
import functools
import math

import jax
import jax.numpy as jnp
from jax import lax
from jax.experimental import pallas as pl
from jax.experimental.pallas import tpu as pltpu

F32 = jnp.float32
BF16 = jnp.bfloat16

D_MODEL = 1024
SB_HEADS = 8
SB_HEAD_DIM = 64
SB_WIDTH = 512
SSM_WIDTH = 512
SSM_GROUP = 16
SSM_GROUPS = 32
SSM_STATE = 64
XA_HEADS = 4
XA_HEAD_DIM = 256
D_FF = 2816
RMS_EPS = 1e-6
IN_WIDTH = 4096
N_DEV = 8

ADAM_LR = 0.001
ADAM_B1 = 0.9
ADAM_B2 = 0.999
ADAM_EPS = 1e-08
ADAM_WD = 0.01
ADAM_STEP = 10

LANES = 128
SUBLANES = 8
VMEM_LIMIT = 48 * 1024 * 1024

_GELU_C = math.sqrt(2.0 / math.pi)


def _cparams(*sem):
    return pltpu.CompilerParams(dimension_semantics=sem, vmem_limit_bytes=VMEM_LIMIT)


def _pick(n, cands):
    for c in cands:
        if n % c == 0:
            return c
    return n


def _gelu(x):
    return 0.5 * x * (1.0 + jnp.tanh(_GELU_C * (x + 0.044715 * x * x * x)))


def _gelu_and_grad(x):
    t = jnp.tanh(_GELU_C * (x + 0.044715 * x * x * x))
    g = 0.5 * x * (1.0 + t)
    dg = 0.5 * (1.0 + t) + 0.5 * x * (1.0 - t * t) * _GELU_C * (1.0 + 3.0 * 0.044715 * x * x)
    return g, dg


def _sigmoid(x):
    return 1.0 / (1.0 + jnp.exp(-x))


def _dot(a, b, ca, cb):
    return lax.dot_general(a.astype(BF16), b.astype(BF16), (((ca,), (cb,)), ((), ())),
                           preferred_element_type=F32)


MM_TILES = (1024, 768, 512, 256, 128)
MM_K_TILES = (2048, 1536) + MM_TILES
MM_PAIR = 2
MM_WIDE = 1536


def _matmul(a, b, *, ta=False, tb=False, out_dtype=F32, name, b_block0=0, n_blocks=None,
            out_cb=None, out_into=None, out_block0=0, acc_in=None, rider=None):
    if ta:
        K, M = a.shape
    else:
        M, K = a.shape
    b_cb = None
    if b.ndim == 3:
        b_cb = b.shape[2]
        n_blocks = b.shape[0] - b_block0 if n_blocks is None else n_blocks
        N, K2 = (b.shape[1], n_blocks * b_cb) if tb else (n_blocks * b_cb, b.shape[1])
    elif tb:
        N, K2 = b.shape
    else:
        K2, N = b.shape
    assert K == K2, (a.shape, b.shape, ta, tb)
    if out_into is not None:
        out_cb = out_into.shape[2]
    tm = _pick(M, MM_TILES)
    pair = lambda cb_, count: MM_PAIR if (cb_ * MM_PAIR <= MM_WIDE and count % MM_PAIR == 0) else 1
    b_pair = pair(b_cb, n_blocks) if b_cb else 1
    o_pair = pair(out_cb, N // out_cb) if out_cb else 1
    if b_cb and not tb:
        tn = b_cb * b_pair
    elif out_cb:
        tn = out_cb * o_pair
    else:
        tn = _pick(N, MM_TILES)
    if b_cb and tb:
        tk = b_cb * b_pair
    else:
        tk = _pick(K, MM_TILES if tn > MM_TILES[0] else MM_K_TILES)
    nk = K // tk
    ca, cb = (0 if ta else 1), (1 if tb else 0)
    has_acc = acc_in is not None
    has_into = out_into is not None

    def body(*refs):
        a_ref, b_ref = refs[0], refs[1]
        pos = 2
        c_ref = None
        if has_acc:
            c_ref = refs[pos]
            pos += 1
        if has_into:
            pos += 1
        o_ref = refs[pos]
        b_tile = b_ref[...] if b_cb is None else jnp.concatenate([b_ref[t] for t in range(b_pair)], axis=1)
        p = _dot(a_ref[...], b_tile, ca, cb)

        def write(val):
            val = val.astype(out_dtype)
            if out_cb is None:
                o_ref[...] = val
            else:
                for t in range(o_pair):
                    o_ref[t] = val[:, t * out_cb:(t + 1) * out_cb]

        if nk == 1:
            write((p + c_ref[...]) if has_acc else p)
        else:
            acc_ref = refs[pos + 1]
            k = pl.program_id(2)

            @pl.when(k == 0)
            def _():
                acc_ref[...] = (p + c_ref[...]) if has_acc else p

            @pl.when(k > 0)
            def _():
                acc_ref[...] += p

            @pl.when(k == nk - 1)
            def _():
                write(acc_ref[...])

    nj, ni = N // tn, M // tm
    a_bytes, b_bytes = a.size * a.dtype.itemsize, K * N * b.dtype.itemsize
    n_outer = a_bytes * nj + b_bytes * (1 if nk == 1 else ni) <= a_bytes * (1 if nk == 1 else nj) + b_bytes * ni
    grid = (nj, ni, nk) if n_outer else (ni, nj, nk)

    def spec(block, index):
        return pl.BlockSpec(block, (lambda g0, g1, k: index(g0, g1, k)) if n_outer else (lambda g0, g1, k: index(g1, g0, k)))

    a_spec = spec((tk, tm), lambda j, i, k: (k, i)) if ta else spec((tm, tk), lambda j, i, k: (i, k))
    if b_cb is None:
        b_spec = spec((tn, tk), lambda j, i, k: (j, k)) if tb else spec((tk, tn), lambda j, i, k: (k, j))
    elif tb:
        b_spec = spec((b_pair, tn, b_cb), lambda j, i, k: (b_block0 // b_pair + k, j, 0))
    else:
        b_spec = spec((b_pair, tk, b_cb), lambda j, i, k: (b_block0 // b_pair + j, k, 0))
    in_specs = [a_spec, b_spec]
    operands = [a, b]
    aliases = {}
    if has_acc:
        in_specs.append(spec((tm, tn), lambda j, i, k: (i, j)))
        operands.append(acc_in)
    if has_into:
        aliases = {len(operands): 0}
        in_specs.append(pl.BlockSpec(memory_space=pl.ANY))
        operands.append(out_into)
    if out_cb is None:
        out_shape = jax.ShapeDtypeStruct((M, N), out_dtype)
        out_spec = spec((tm, tn), lambda j, i, k: (i, j))
    else:
        out_shape = (jax.ShapeDtypeStruct(out_into.shape, out_into.dtype) if has_into
                     else jax.ShapeDtypeStruct((N // out_cb, M, out_cb), out_dtype))
        out_spec = spec((o_pair, tm, out_cb), lambda j, i, k: (out_block0 // o_pair + j, i, 0))
    if rider is not None:
        assert not has_into
        (out,), brought = _call(body, name=name, rider=rider, grid=grid, in_specs=in_specs,
                                out_specs=(out_spec,), out_shape=(out_shape,), operands=operands,
                                scratch_shapes=[] if nk == 1 else [pltpu.VMEM((tm, tn), F32)])
        return out, brought
    return pl.pallas_call(
        body, name=name, out_shape=out_shape,
        grid=grid,
        in_specs=in_specs, out_specs=out_spec, input_output_aliases=aliases,
        scratch_shapes=[] if nk == 1 else [pltpu.VMEM((tm, tn), F32)],
        compiler_params=_cparams("parallel", "parallel", "arbitrary"),
    )(*operands)


def _rms(x, g):
    r = lax.rsqrt(jnp.mean(x * x, axis=-1, keepdims=True) + RMS_EPS)
    return x * r * g


def _rms_bwd(dy, x, g):
    r = lax.rsqrt(jnp.mean(x * x, axis=-1, keepdims=True) + RMS_EPS)
    xh = x * r
    dxh = dy * g
    dx = r * (dxh - xh * jnp.mean(dxh * xh, axis=-1, keepdims=True))
    dg = jnp.sum(dy * xh, axis=0, keepdims=True)
    return dx, dg


def _row_tile(rows):
    return _pick(rows, (512, 256, 128, 64, 32, 16, 8))


def _rms_fwd(x, g, *, name, rider=None):
    R, D = x.shape
    tr = _row_tile(R)

    def body(x_ref, g_ref, h_ref):
        h_ref[...] = _rms(x_ref[...], g_ref[...]).astype(BF16)

    (h,), brought = _call(
        body, name=name, rider=rider, out_shape=(jax.ShapeDtypeStruct((R, D), BF16),), grid=(R // tr,),
        in_specs=[pl.BlockSpec((tr, D), lambda i: (i, 0)), pl.BlockSpec((1, D), lambda i: (0, 0))],
        out_specs=(pl.BlockSpec((tr, D), lambda i: (i, 0)),), scratch_shapes=[], operands=(x, g))
    return h if rider is None else (h, brought)


def _resnorm_norm(x, z, g_post, g_next, *, name):
    R, D = x.shape
    tr = _row_tile(R)

    def body(x_ref, z_ref, gp_ref, gn_ref, xn_ref, h_ref):
        xn = x_ref[...] + _rms(z_ref[...], gp_ref[...])
        xn_ref[...] = xn
        h_ref[...] = _rms(xn, gn_ref[...]).astype(BF16)

    row = pl.BlockSpec((tr, D), lambda i: (i, 0))
    vec = pl.BlockSpec((1, D), lambda i: (0, 0))
    return pl.pallas_call(
        body, name=name,
        out_shape=(jax.ShapeDtypeStruct((R, D), F32), jax.ShapeDtypeStruct((R, D), BF16)),
        grid=(R // tr,), in_specs=[row, row, vec, vec], out_specs=(row, row),
        compiler_params=_cparams("parallel"),
    )(x, z, g_post, g_next)


def _final_loss(x, z, g_post, target, *, name):
    R, D = x.shape
    tr = _row_tile(R)

    def body(x_ref, z_ref, gp_ref, t_ref, loss_ref, dy_ref, dz_ref, dg_ref):
        i = pl.program_id(0)
        z = z_ref[...]
        g = gp_ref[...]
        err = x_ref[...] + _rms(z, g) - t_ref[...]
        dy = err * (1.0 / D)
        dy_ref[...] = dy
        dz, dg = _rms_bwd(dy, z, g)
        dz_ref[...] = dz.astype(BF16)
        part = 0.5 * jnp.sum(jnp.sum(err * err, axis=-1, keepdims=True) * (1.0 / D), axis=0, keepdims=True)

        @pl.when(i == 0)
        def _():
            loss_ref[...] = part
            dg_ref[...] = dg

        @pl.when(i > 0)
        def _():
            loss_ref[...] += part
            dg_ref[...] += dg

    row = pl.BlockSpec((tr, D), lambda i: (i, 0))
    vec = pl.BlockSpec((1, D), lambda i: (0, 0))
    return pl.pallas_call(
        body, name=name,
        out_shape=(jax.ShapeDtypeStruct((1, 1), F32), jax.ShapeDtypeStruct((R, D), F32),
                   jax.ShapeDtypeStruct((R, D), BF16), jax.ShapeDtypeStruct((1, D), F32)),
        grid=(R // tr,), in_specs=[row, row, vec, row],
        out_specs=(pl.BlockSpec((1, 1), lambda i: (0, 0)), row, row, vec),
        compiler_params=_cparams("arbitrary"),
    )(x, z, g_post, target)


def _norm_bwd_pair(dres, dh, xk, g_pre, zprev, g_prev_post, *, name, rider=None):
    R, D = xk.shape
    tr = _row_tile(R)

    def body(dres_ref, dh_ref, x_ref, gpre_ref, z_ref, gpost_ref, dx_ref, dz_ref, dgpre_ref, dgpost_ref):
        i = pl.program_id(0)
        d1, dgpre = _rms_bwd(dh_ref[...], x_ref[...], gpre_ref[...])
        dx = dres_ref[...] + d1
        dx_ref[...] = dx
        dz, dgpost = _rms_bwd(dx, z_ref[...], gpost_ref[...])
        dz_ref[...] = dz.astype(BF16)

        @pl.when(i == 0)
        def _():
            dgpre_ref[...] = dgpre
            dgpost_ref[...] = dgpost

        @pl.when(i > 0)
        def _():
            dgpre_ref[...] += dgpre
            dgpost_ref[...] += dgpost

    row = pl.BlockSpec((tr, D), lambda i: (i, 0))
    vec = pl.BlockSpec((1, D), lambda i: (0, 0))
    return _call(
        body, name=name, rider=rider,
        out_shape=(jax.ShapeDtypeStruct((R, D), F32), jax.ShapeDtypeStruct((R, D), BF16),
                   jax.ShapeDtypeStruct((1, D), F32), jax.ShapeDtypeStruct((1, D), F32)),
        grid=(R // tr,), in_specs=[row, row, row, vec, row, vec], out_specs=(row, row, vec, vec),
        scratch_shapes=[], operands=(dres, dh, xk, g_pre, zprev, g_prev_post))


def _norm_bwd_single(dres, dh, xk, g_pre, *, name, rider=None):
    R, D = xk.shape
    tr = _row_tile(R)
    has_res = dres is not None

    def body(*refs):
        if has_res:
            dres_ref, dh_ref, x_ref, gpre_ref, dx_ref, dgpre_ref = refs
        else:
            dh_ref, x_ref, gpre_ref, dx_ref, dgpre_ref = refs
        i = pl.program_id(0)
        d1, dgpre = _rms_bwd(dh_ref[...], x_ref[...], gpre_ref[...])
        dx_ref[...] = dres_ref[...] + d1 if has_res else d1

        @pl.when(i == 0)
        def _():
            dgpre_ref[...] = dgpre

        @pl.when(i > 0)
        def _():
            dgpre_ref[...] += dgpre

    row = pl.BlockSpec((tr, D), lambda i: (i, 0))
    vec = pl.BlockSpec((1, D), lambda i: (0, 0))
    ins = ([dres] if has_res else []) + [dh, xk, g_pre]
    res, brought = _call(
        body, name=name, rider=rider,
        out_shape=(jax.ShapeDtypeStruct((R, D), F32), jax.ShapeDtypeStruct((1, D), F32)),
        grid=(R // tr,), in_specs=([row] if has_res else []) + [row, row, vec], out_specs=(row, vec),
        scratch_shapes=[], operands=ins)
    return res if rider is None else (res, brought)


SB_BLOCK = 256
SB_QBLOCK = 512
SB_DEAD = -104.0


def _sb_tri(kind):
    r = lax.broadcasted_iota(jnp.int32, (SB_BLOCK, SB_BLOCK), 0)
    c = lax.broadcasted_iota(jnp.int32, (SB_BLOCK, SB_BLOCK), 1)
    keep = {"after": r > c, "before": r < c}[kind]
    return jnp.where(keep, 1.0, 0.0).astype(BF16)


def _sb_scores(qm, k_blk):
    z = _dot(qm, k_blk, 1, 1)
    sp = jnp.maximum(z, 0.0) + jnp.log(1.0 + jnp.exp(-jnp.abs(z)))
    return z, sp


def _sb_causal(rows):
    r = lax.broadcasted_iota(jnp.int32, (rows, SB_BLOCK), 0)
    c = lax.broadcasted_iota(jnp.int32, (rows, SB_BLOCK), 1)
    return c < r


def _head_masks():
    lane = lax.broadcasted_iota(jnp.int32, (1, LANES), 1)
    return [jnp.where(lane < SB_HEAD_DIM, 1.0, 0.0), jnp.where(lane >= SB_HEAD_DIM, 1.0, 0.0)]


def _sb_fwd(proj, *, name, rider=None):
    S = proj.shape[0]
    T = SB_BLOCK
    TQ = min(SB_QBLOCK, S)
    span = TQ // T
    nq = S // TQ
    npair = SB_WIDTH // LANES
    scale = SB_HEAD_DIM ** -0.5

    def body(q_ref, k_ref, v_ref, o_ref, tot_ref, first_ref, acc_ref, run_ref):
        masks = _head_masks()
        tri = _sb_tri("after")
        first_ref[...] = jnp.zeros_like(first_ref)
        slot = lax.broadcasted_iota(jnp.int32, first_ref.shape, 1)

        def alive():
            reach = jnp.maximum(jnp.max(run_ref[0]), jnp.max(run_ref[1]))
            return (reach > SB_DEAD).astype(jnp.int32)

        def q_block(i, _):
            qrow = pl.ds(pl.multiple_of(i * TQ, TQ), TQ)
            q = q_ref[qrow, :] * scale
            qm = [(q * m).astype(BF16) for m in masks]
            acc_ref[...] = jnp.zeros_like(acc_ref)
            run_ref[...] = jnp.zeros_like(run_ref)

            def k_block(j, own):
                krow = pl.ds(pl.multiple_of(j * T, T), T)
                k_blk = k_ref[krow, :].astype(BF16)
                v_blk = v_ref[krow, :].astype(BF16)
                r0 = 0 if own is None else own * T
                rows = pl.ds(r0, TQ - r0)
                for h in range(2):
                    z, sp = _sb_scores(qm[h][r0:], k_blk)
                    causal = None if own is None else _sb_causal(TQ - r0)
                    lf = -sp if causal is None else jnp.where(causal, -sp, 0.0)
                    e = jnp.exp(z - sp + _dot(lf, tri, 1, 0) + run_ref[h, rows])
                    w = e if causal is None else jnp.where(causal, e, 0.0)
                    acc_ref[h, rows] += _dot(w, v_blk, 1, 0)
                    run_ref[h, rows] += jnp.sum(lf, axis=1, keepdims=True)

            for d in reversed(range(span)):
                k_block(i * span + d, d)

            def below(carry):
                jj, _ = carry
                k_block(i * span - 1 - jj, None)
                return jj + 1, alive()

            done, _ = lax.while_loop(lambda c: jnp.logical_and(c[0] < i * span, c[1] > 0), below, (jnp.int32(0), alive()))
            o_ref[qrow, :] = (acc_ref[0] * masks[0] + acc_ref[1] * masks[1]).astype(BF16)
            tot_ref[qrow, :] = run_ref[0] * masks[0] + run_ref[1] * masks[1]
            first_ref[...] = jnp.where(slot == i, (i * span - done).astype(F32), first_ref[...])
            return 0

        lax.fori_loop(0, nq, q_block, 0)

    blk = lambda off: pl.BlockSpec((S, LANES), lambda p: (0, off + p))
    return _call(
        body, name=name, rider=rider,
        out_shape=(jax.ShapeDtypeStruct((S, SB_WIDTH), BF16), jax.ShapeDtypeStruct((S, SB_WIDTH), F32),
                   jax.ShapeDtypeStruct((npair, SUBLANES, LANES), F32)),
        grid=(npair,),
        in_specs=[blk(0), blk(npair), blk(2 * npair)],
        out_specs=(blk(0), blk(0), pl.BlockSpec((1, SUBLANES, LANES), lambda p: (p, 0, 0))),
        scratch_shapes=[pltpu.VMEM((2, TQ, LANES), F32), pltpu.VMEM((2, TQ, 1), F32)],
        operands=(proj, proj, proj))


def _sb_bwd(proj, tot, first, do_attn, *, name, rider=None):
    S = proj.shape[0]
    T = SB_BLOCK
    TQ = min(SB_QBLOCK, S)
    span = TQ // T
    nq = S // TQ
    npair = SB_WIDTH // LANES
    scale = SB_HEAD_DIM ** -0.5

    def body(q_ref, k_ref, v_ref, tot_ref, first_ref, do_ref, dq_ref, dk_ref, dv_ref,
             dqacc_ref, dkacc_ref, dvacc_ref, run_ref, grun_ref):
        masks = _head_masks()
        tri_after = _sb_tri("after")
        tri_before = _sb_tri("before")
        dkacc_ref[...] = jnp.zeros_like(dkacc_ref)
        dvacc_ref[...] = jnp.zeros_like(dvacc_ref)
        slot = lax.broadcasted_iota(jnp.int32, first_ref.shape, 1)

        def q_block(i, _):
            qrow = pl.ds(pl.multiple_of(i * TQ, TQ), TQ)
            q = q_ref[qrow, :] * scale
            do = do_ref[qrow, :].astype(F32)
            tot = tot_ref[qrow, :]
            qm = [(q * m).astype(BF16) for m in masks]
            dom = [(do * m).astype(BF16) for m in masks]
            ltot = [jnp.sum(tot * m, axis=1, keepdims=True) * (1.0 / SB_HEAD_DIM) for m in masks]
            dqacc_ref[...] = jnp.zeros_like(dqacc_ref)
            run_ref[...] = jnp.zeros_like(run_ref)
            grun_ref[...] = jnp.zeros_like(grun_ref)

            def k_block(j, own):
                krow = pl.ds(pl.multiple_of(j * T, T), T)
                k_blk = k_ref[krow, :].astype(BF16)
                v_blk = v_ref[krow, :].astype(BF16)
                r0 = 0 if own is None else own * T
                rows = pl.ds(r0, TQ - r0)
                for h in range(2):
                    z, sp = _sb_scores(qm[h][r0:], k_blk)
                    causal = None if own is None else _sb_causal(TQ - r0)
                    lf = -sp if causal is None else jnp.where(causal, -sp, 0.0)
                    lsum = jnp.sum(lf, axis=1, keepdims=True)
                    later = (ltot[h][r0:] - run_ref[h, rows] - lsum) + _dot(lf, tri_after, 1, 0)
                    beta = jnp.exp(z - sp)
                    w = jnp.exp(z - sp + later)
                    if causal is not None:
                        w = jnp.where(causal, w, 0.0)
                    g = _dot(dom[h][r0:], v_blk, 1, 1) * w
                    gbefore = grun_ref[h, rows] + _dot(g, tri_before, 1, 0)
                    dz = g - beta * (g + gbefore)
                    if causal is not None:
                        dz = jnp.where(causal, dz, 0.0)
                    dz = dz.astype(BF16)
                    dqacc_ref[h, rows] += _dot(dz, k_blk, 1, 0)
                    dkacc_ref[krow, :] += _dot(dz, qm[h][r0:], 0, 0)
                    dvacc_ref[krow, :] += _dot(w, dom[h][r0:], 0, 0)
                    run_ref[h, rows] += lsum
                    grun_ref[h, rows] += jnp.sum(g, axis=1, keepdims=True)

            def above(j, _):
                k_block(j, None)
                return 0

            first = jnp.max(jnp.where(slot == i, first_ref[...], 0.0)).astype(jnp.int32)
            lax.fori_loop(jnp.clip(first, 0, i * span), i * span, above, 0)
            for d in range(span):
                k_block(i * span + d, d)
            dq_ref[qrow, :] = ((dqacc_ref[0] * masks[0] + dqacc_ref[1] * masks[1]) * scale).astype(BF16)
            return 0

        lax.fori_loop(0, nq, q_block, 0)
        dk_ref[...] = dkacc_ref[...].astype(BF16)
        dv_ref[...] = dvacc_ref[...].astype(BF16)

    blk = lambda off: pl.BlockSpec((S, LANES), lambda p: (0, off + p))
    out = jax.ShapeDtypeStruct((S, SB_WIDTH), BF16)
    return _call(
        body, name=name, rider=rider, out_shape=(out, out, out), grid=(npair,),
        in_specs=[blk(0), blk(npair), blk(2 * npair), blk(0), pl.BlockSpec((1, SUBLANES, LANES), lambda p: (p, 0, 0)),
                  blk(0)],
        out_specs=(blk(0), blk(0), blk(0)),
        scratch_shapes=[pltpu.VMEM((2, TQ, LANES), F32), pltpu.VMEM((S, LANES), F32), pltpu.VMEM((S, LANES), F32),
                        pltpu.VMEM((2, TQ, 1), F32), pltpu.VMEM((2, TQ, 1), F32)],
        operands=(proj, proj, proj, tot, first, do_attn))


SSM_HALVES = 2
SSM_HALF_CH = SSM_WIDTH // SSM_HALVES
SSM_HALF_ST = SSM_GROUPS * SSM_STATE // SSM_HALVES
SSM_CHUNK = 512


def _cmul(ar, ai, br, bi):
    return ar * br - ai * bi, ar * bi + ai * br


def _ssm_tables(lam_re, lam_im):
    lr = lam_re.reshape(-1)
    li = lam_im.reshape(-1)
    pows = [(jnp.ones_like(lr), jnp.zeros_like(li)), (lr, li)]
    for _ in range(2, SUBLANES + 1):
        pows.append(_cmul(pows[-1][0], pows[-1][1], lr, li))
    row = jnp.arange(SUBLANES)[:, None]

    def shift_tab(d, keep):
        return [jnp.where(keep, pows[d][0][None, :], 0.0), jnp.where(keep, pows[d][1][None, :], 0.0)]

    fwd, bwd = [], []
    for d in (1, 2, 4):
        fwd += shift_tab(d, row >= d)
        bwd += shift_tab(d, row + d < SUBLANES)
    fwd += [jnp.stack([pows[r + 1][0] for r in range(SUBLANES)]), jnp.stack([pows[r + 1][1] for r in range(SUBLANES)])]
    bwd += [jnp.stack([pows[SUBLANES - r][0] for r in range(SUBLANES)]),
            jnp.stack([pows[SUBLANES - r][1] for r in range(SUBLANES)])]

    def halves(tabs):
        t = jnp.stack(tabs)
        return t.reshape(8, SUBLANES, SSM_HALVES, SSM_HALF_ST).transpose(2, 0, 1, 3)

    return halves(fwd), halves(bwd)


def _ssm_fwd(proj, bd_re, bd_im, cd_re, cd_imneg, d_skip, tab, *, name, rider=None):
    S = proj.shape[0]
    Tc = min(SSM_CHUNK, S)
    nc = S // Tc
    u_blk0 = (3 * SB_WIDTH) // SSM_HALF_CH

    def body(u_ref, bre_ref, bim_ref, cre_ref, cim_ref, d_ref, tab_ref, y_ref, xre_ref, xim_ref, cre_s, cim_s):
        c = pl.program_id(1)

        @pl.when(c == 0)
        def _():
            cre_s[...] = jnp.zeros_like(cre_s)
            cim_s[...] = jnp.zeros_like(cim_s)

        u = u_ref[...]
        ub = u.astype(BF16)
        xre_ref[...] = _dot(ub, bre_ref[0], 1, 0)
        xim_ref[...] = _dot(ub, bim_ref[0], 1, 0)

        def slab(k, carry):
            car_re, car_im = carry
            rows = pl.ds(pl.multiple_of(k * SUBLANES, SUBLANES), SUBLANES)
            sre = xre_ref[rows, :]
            sim = xim_ref[rows, :]
            for n, d in enumerate((1, 2, 4)):
                pre, pim = tab_ref[0, 2 * n], tab_ref[0, 2 * n + 1]
                rre = pltpu.roll(sre, d, 0)
                rim = pltpu.roll(sim, d, 0)
                sre, sim = sre + (pre * rre - pim * rim), sim + (pre * rim + pim * rre)
            pre, pim = tab_ref[0, 6], tab_ref[0, 7]
            sre, sim = sre + (pre * car_re - pim * car_im), sim + (pre * car_im + pim * car_re)
            xre_ref[rows, :] = sre
            xim_ref[rows, :] = sim
            last = (SUBLANES - 1, SUBLANES)
            return (jnp.broadcast_to(sre[last[0]:last[1], :], sre.shape),
                    jnp.broadcast_to(sim[last[0]:last[1], :], sim.shape))

        car = lax.fori_loop(0, Tc // SUBLANES, slab, (cre_s[...], cim_s[...]))
        cre_s[...] = car[0]
        cim_s[...] = car[1]
        y = _dot(xre_ref[...], cre_ref[0], 1, 0) + _dot(xim_ref[...], cim_ref[0], 1, 0)
        y_ref[...] = y + d_ref[...] * u

    return _call(
        body, name=name, rider=rider,
        out_shape=(jax.ShapeDtypeStruct((S, SSM_WIDTH), F32),
                   jax.ShapeDtypeStruct((S, SSM_HALVES * SSM_HALF_ST), F32),
                   jax.ShapeDtypeStruct((S, SSM_HALVES * SSM_HALF_ST), F32)),
        grid=(SSM_HALVES, nc),
        in_specs=[pl.BlockSpec((Tc, SSM_HALF_CH), lambda h, c: (c, u_blk0 + h)),
                  pl.BlockSpec((1, SSM_HALF_CH, SSM_HALF_ST), lambda h, c: (h, 0, 0)),
                  pl.BlockSpec((1, SSM_HALF_CH, SSM_HALF_ST), lambda h, c: (h, 0, 0)),
                  pl.BlockSpec((1, SSM_HALF_ST, SSM_HALF_CH), lambda h, c: (h, 0, 0)),
                  pl.BlockSpec((1, SSM_HALF_ST, SSM_HALF_CH), lambda h, c: (h, 0, 0)),
                  pl.BlockSpec((1, SSM_HALF_CH), lambda h, c: (0, h)),
                  pl.BlockSpec((1, 8, SUBLANES, SSM_HALF_ST), lambda h, c: (h, 0, 0, 0))],
        out_specs=(pl.BlockSpec((Tc, SSM_HALF_CH), lambda h, c: (c, h)),
                   pl.BlockSpec((Tc, SSM_HALF_ST), lambda h, c: (c, h)),
                   pl.BlockSpec((Tc, SSM_HALF_ST), lambda h, c: (c, h))),
        scratch_shapes=[pltpu.VMEM((SUBLANES, SSM_HALF_ST), F32), pltpu.VMEM((SUBLANES, SSM_HALF_ST), F32)],
        operands=(proj, bd_re, bd_im, cd_re, cd_imneg, d_skip, tab))


def _ssm_bwd(dy, proj, x_re, x_im, bd_re, bd_im, cd_re, cd_imneg, d_skip, tab, *, name, rider=None):
    S = proj.shape[0]
    Tc = min(SSM_CHUNK, S)
    nc = S // Tc
    u_blk0 = (3 * SB_WIDTH) // SSM_HALF_CH

    def body(dy_ref, u_ref, xre_ref, xim_ref, bre_ref, bim_ref, cre_ref, cim_ref, d_ref, tab_ref,
             du_ref, dbre_ref, dbim_ref, dcre_ref, dcim_ref, dd_ref, dlre_ref, dlim_ref,
             gre_s, gim_s, cre_s, cim_s):
        c = pl.program_id(1)

        @pl.when(c == 0)
        def _():
            cre_s[...] = jnp.zeros_like(cre_s)
            cim_s[...] = jnp.zeros_like(cim_s)
            dbre_ref[...] = jnp.zeros_like(dbre_ref)
            dbim_ref[...] = jnp.zeros_like(dbim_ref)
            dcre_ref[...] = jnp.zeros_like(dcre_ref)
            dcim_ref[...] = jnp.zeros_like(dcim_ref)
            dd_ref[...] = jnp.zeros_like(dd_ref)
            dlre_ref[...] = jnp.zeros_like(dlre_ref)
            dlim_ref[...] = jnp.zeros_like(dlim_ref)

        dy = dy_ref[...]
        dyb = dy.astype(BF16)
        u = u_ref[...]
        gre_s[...] = _dot(dyb, cre_ref[0], 1, 1)
        gim_s[...] = _dot(dyb, cim_ref[0], 1, 1)
        row = lax.broadcasted_iota(jnp.int32, (SUBLANES, SSM_HALF_ST), 0)
        nslab = Tc // SUBLANES

        def slab(kk, carry):
            car_re, car_im, acc_re, acc_im = carry
            k = nslab - 1 - kk
            rows = pl.ds(pl.multiple_of(k * SUBLANES, SUBLANES), SUBLANES)
            sre = gre_s[rows, :]
            sim = gim_s[rows, :]
            for n, d in enumerate((1, 2, 4)):
                pre, pim = tab_ref[0, 2 * n], tab_ref[0, 2 * n + 1]
                rre = pltpu.roll(sre, SUBLANES - d, 0)
                rim = pltpu.roll(sim, SUBLANES - d, 0)
                sre, sim = sre + (pre * rre + pim * rim), sim + (pre * rim - pim * rre)
            pre, pim = tab_ref[0, 6], tab_ref[0, 7]
            sre, sim = sre + (pre * car_re + pim * car_im), sim + (pre * car_im - pim * car_re)
            gre_s[rows, :] = sre
            gim_s[rows, :] = sim
            nre = jnp.where(row == SUBLANES - 1, car_re, pltpu.roll(sre, SUBLANES - 1, 0))
            nim = jnp.where(row == SUBLANES - 1, car_im, pltpu.roll(sim, SUBLANES - 1, 0))
            xr = xre_ref[rows, :]
            xi = xim_ref[rows, :]
            acc_re = acc_re + (nre * xr + nim * xi)
            acc_im = acc_im + (nim * xr - nre * xi)
            return (jnp.broadcast_to(sre[0:1, :], sre.shape), jnp.broadcast_to(sim[0:1, :], sim.shape), acc_re, acc_im)

        car = lax.fori_loop(0, nslab, slab, (cre_s[...], cim_s[...], dlre_ref[0], dlim_ref[0]))
        cre_s[...] = car[0]
        cim_s[...] = car[1]
        dlre_ref[0] = car[2]
        dlim_ref[0] = car[3]
        gre = gre_s[...].astype(BF16)
        gim = gim_s[...].astype(BF16)
        ub = u.astype(BF16)
        du = _dot(gre, bre_ref[0], 1, 1) + _dot(gim, bim_ref[0], 1, 1) + d_ref[...] * dy
        du_ref[...] = du.astype(BF16)
        dbre_ref[0] += _dot(ub, gre, 0, 0)
        dbim_ref[0] += _dot(ub, gim, 0, 0)
        dcre_ref[0] += _dot(xre_ref[...], dyb, 0, 0)
        dcim_ref[0] += _dot(xim_ref[...], dyb, 0, 0)
        dd_ref[...] += jnp.sum(dy * u, axis=0, keepdims=True)

    rev = lambda c: nc - 1 - c
    return _call(
        body, name=name, rider=rider,
        out_shape=(jax.ShapeDtypeStruct((S, SSM_WIDTH), BF16),
                   jax.ShapeDtypeStruct((SSM_HALVES, SSM_HALF_CH, SSM_HALF_ST), F32),
                   jax.ShapeDtypeStruct((SSM_HALVES, SSM_HALF_CH, SSM_HALF_ST), F32),
                   jax.ShapeDtypeStruct((SSM_HALVES, SSM_HALF_ST, SSM_HALF_CH), F32),
                   jax.ShapeDtypeStruct((SSM_HALVES, SSM_HALF_ST, SSM_HALF_CH), F32),
                   jax.ShapeDtypeStruct((1, SSM_WIDTH), F32),
                   jax.ShapeDtypeStruct((SSM_HALVES, SUBLANES, SSM_HALF_ST), F32),
                   jax.ShapeDtypeStruct((SSM_HALVES, SUBLANES, SSM_HALF_ST), F32)),
        grid=(SSM_HALVES, nc),
        in_specs=[pl.BlockSpec((Tc, SSM_HALF_CH), lambda h, c: (rev(c), h)),
                  pl.BlockSpec((Tc, SSM_HALF_CH), lambda h, c: (rev(c), u_blk0 + h)),
                  pl.BlockSpec((Tc, SSM_HALF_ST), lambda h, c: (rev(c), h)),
                  pl.BlockSpec((Tc, SSM_HALF_ST), lambda h, c: (rev(c), h)),
                  pl.BlockSpec((1, SSM_HALF_CH, SSM_HALF_ST), lambda h, c: (h, 0, 0)),
                  pl.BlockSpec((1, SSM_HALF_CH, SSM_HALF_ST), lambda h, c: (h, 0, 0)),
                  pl.BlockSpec((1, SSM_HALF_ST, SSM_HALF_CH), lambda h, c: (h, 0, 0)),
                  pl.BlockSpec((1, SSM_HALF_ST, SSM_HALF_CH), lambda h, c: (h, 0, 0)),
                  pl.BlockSpec((1, SSM_HALF_CH), lambda h, c: (0, h)),
                  pl.BlockSpec((1, 8, SUBLANES, SSM_HALF_ST), lambda h, c: (h, 0, 0, 0))],
        out_specs=(pl.BlockSpec((Tc, SSM_HALF_CH), lambda h, c: (rev(c), h)),
                   pl.BlockSpec((1, SSM_HALF_CH, SSM_HALF_ST), lambda h, c: (h, 0, 0)),
                   pl.BlockSpec((1, SSM_HALF_CH, SSM_HALF_ST), lambda h, c: (h, 0, 0)),
                   pl.BlockSpec((1, SSM_HALF_ST, SSM_HALF_CH), lambda h, c: (h, 0, 0)),
                   pl.BlockSpec((1, SSM_HALF_ST, SSM_HALF_CH), lambda h, c: (h, 0, 0)),
                   pl.BlockSpec((1, SSM_HALF_CH), lambda h, c: (0, h)),
                   pl.BlockSpec((1, SUBLANES, SSM_HALF_ST), lambda h, c: (h, 0, 0)),
                   pl.BlockSpec((1, SUBLANES, SSM_HALF_ST), lambda h, c: (h, 0, 0))),
        scratch_shapes=[pltpu.VMEM((Tc, SSM_HALF_ST), F32), pltpu.VMEM((Tc, SSM_HALF_ST), F32),
                        pltpu.VMEM((SUBLANES, SSM_HALF_ST), F32), pltpu.VMEM((SUBLANES, SSM_HALF_ST), F32)],
        operands=(dy, proj, x_re, x_im, bd_re, bd_im, cd_re, cd_imneg, d_skip, tab))


def _ssm_prepare(a_re, a_im, log_dt, b_re, b_im):
    dt = jnp.exp(log_dt)[:, None]
    mag = jnp.exp(a_re * dt)
    lre = mag * jnp.cos(a_im * dt)
    lim = mag * jnp.sin(a_im * dt)
    den = a_re * a_re + a_im * a_im
    fre = ((lre - 1.0) * a_re + lim * a_im) / den
    fim = (lim * a_re - (lre - 1.0) * a_im) / den
    bbre = fre[:, :, None] * b_re - fim[:, :, None] * b_im
    bbim = fre[:, :, None] * b_im + fim[:, :, None] * b_re
    return lre, lim, bbre, bbim


def _group_eye():
    return jnp.eye(SSM_GROUPS // SSM_HALVES, dtype=F32)


def _bd_from_bbar(bbar):
    gh = SSM_GROUPS // SSM_HALVES
    b = bbar.reshape(SSM_HALVES, gh, SSM_STATE, SSM_GROUP).transpose(0, 1, 3, 2)
    out = b[:, :, :, None, :] * _group_eye()[None, :, None, :, None]
    return out.reshape(SSM_HALVES, SSM_HALF_CH, SSM_HALF_ST)


def _bbar_from_bd(dbd):
    gh = SSM_GROUPS // SSM_HALVES
    d = dbd.reshape(SSM_HALVES, gh, SSM_GROUP, gh, SSM_STATE)
    d = jnp.sum(d * _group_eye()[None, :, None, :, None], axis=3)
    return d.transpose(0, 1, 3, 2).reshape(SSM_GROUPS, SSM_STATE, SSM_GROUP)


def _cd_from_c(cmat):
    gh = SSM_GROUPS // SSM_HALVES
    c = cmat.reshape(SSM_HALVES, gh, SSM_GROUP, SSM_STATE).transpose(0, 1, 3, 2)
    out = c[:, :, :, None, :] * _group_eye()[None, :, None, :, None]
    return out.reshape(SSM_HALVES, SSM_HALF_ST, SSM_HALF_CH)


def _c_from_cd(dcd):
    gh = SSM_GROUPS // SSM_HALVES
    d = dcd.reshape(SSM_HALVES, gh, SSM_STATE, gh, SSM_GROUP)
    d = jnp.sum(d * _group_eye()[None, :, None, :, None], axis=3)
    return d.transpose(0, 1, 3, 2).reshape(SSM_GROUPS, SSM_GROUP, SSM_STATE)


def _glu_fwd(y_pre, w_glu, b_glu, *, name):
    S, W = y_pre.shape
    tr = _row_tile(S)

    def body(y_ref, w_ref, b_ref, o_ref):
        yg = _gelu(y_ref[...])
        gl = _dot(yg, w_ref[...], 1, 0) + b_ref[...]
        o_ref[...] = (yg * _sigmoid(gl)).astype(BF16)

    row = pl.BlockSpec((tr, W), lambda i: (i, 0))
    return pl.pallas_call(
        body, name=name, out_shape=jax.ShapeDtypeStruct((S, W), BF16), grid=(S // tr,),
        in_specs=[row, pl.BlockSpec((W, W), lambda i: (0, 0)), pl.BlockSpec((1, W), lambda i: (0, 0))],
        out_specs=row, compiler_params=_cparams("parallel"),
    )(y_pre, w_glu, b_glu)


def _glu_bwd(y_pre, do, w_glu, b_glu, *, name):
    S, W = y_pre.shape
    tr = _row_tile(S)

    def body(y_ref, do_ref, w_ref, b_ref, dy_ref, dw_ref, db_ref):
        i = pl.program_id(0)
        yg, dyg_dy = _gelu_and_grad(y_ref[...])
        ygb = yg.astype(BF16)
        sg = _sigmoid(_dot(ygb, w_ref[...], 1, 0) + b_ref[...])
        do = do_ref[...]
        dgl = do * yg * sg * (1.0 - sg)
        dglb = dgl.astype(BF16)
        dyg = do * sg + _dot(dglb, w_ref[...], 1, 1)
        dy_ref[...] = dyg * dyg_dy
        dw = _dot(ygb, dglb, 0, 0)
        db = jnp.sum(dgl, axis=0, keepdims=True)

        @pl.when(i == 0)
        def _():
            dw_ref[...] = dw
            db_ref[...] = db

        @pl.when(i > 0)
        def _():
            dw_ref[...] += dw
            db_ref[...] += db

    row = pl.BlockSpec((tr, W), lambda i: (i, 0))
    full = pl.BlockSpec((W, W), lambda i: (0, 0))
    vec = pl.BlockSpec((1, W), lambda i: (0, 0))
    return pl.pallas_call(
        body, name=name,
        out_shape=(jax.ShapeDtypeStruct((S, W), F32), jax.ShapeDtypeStruct((W, W), F32), jax.ShapeDtypeStruct((1, W), F32)),
        grid=(S // tr,), in_specs=[row, row, full, vec], out_specs=(row, full, vec),
        compiler_params=_cparams("arbitrary"),
    )(y_pre, do, w_glu, b_glu)


GATE_COL0 = 3 * SB_WIDTH + SSM_WIDTH


def _merge_fwd(proj, o_attn, o_ssm, w_ba, w_bs, b_gate, *, name):
    S = proj.shape[0]
    D = D_MODEL
    tr = _pick(S, (256, 128, 64, 32, 16, 8))
    gb = GATE_COL0 // D

    def body(ga_ref, gs_ref, oa_ref, os_ref, wa_ref, ws_ref, ba_ref, bs_ref, m_ref):
        pa = _dot(oa_ref[...], wa_ref[...], 1, 0)
        ps = _dot(os_ref[...], ws_ref[...], 1, 0)
        sa = _sigmoid(ga_ref[...] + ba_ref[...])
        ss = _sigmoid(gs_ref[...] + bs_ref[...])
        m_ref[...] = (sa * pa + ss * ps).astype(BF16)

    return pl.pallas_call(
        body, name=name, out_shape=jax.ShapeDtypeStruct((S, D), BF16), grid=(S // tr,),
        in_specs=[pl.BlockSpec((tr, D), lambda i: (i, gb)), pl.BlockSpec((tr, D), lambda i: (i, gb + 1)),
                  pl.BlockSpec((tr, SB_WIDTH), lambda i: (i, 0)), pl.BlockSpec((tr, SSM_WIDTH), lambda i: (i, 0)),
                  pl.BlockSpec((SB_WIDTH, D), lambda i: (0, 0)), pl.BlockSpec((SSM_WIDTH, D), lambda i: (0, 0)),
                  pl.BlockSpec((1, D), lambda i: (0, 0)), pl.BlockSpec((1, D), lambda i: (0, 1))],
        out_specs=pl.BlockSpec((tr, D), lambda i: (i, 0)),
        compiler_params=_cparams("parallel"),
    )(proj, proj, o_attn, o_ssm, w_ba, w_bs, b_gate, b_gate)


def _merge_bwd(dmerged, proj, o_attn, o_ssm, w_ba, w_bs, b_gate, *, name):
    S = proj.shape[0]
    D = D_MODEL
    tr = _pick(S, (256, 128, 64, 32, 16, 8))
    gb = GATE_COL0 // D

    def body(dm_ref, ga_ref, gs_ref, oa_ref, os_ref, wa_ref, ws_ref, ba_ref, bs_ref,
             doa_ref, dos_ref, dg_ref, db_ref, dwa_ref, dws_ref):
        i = pl.program_id(0)
        dm = dm_ref[...]
        oa = oa_ref[...]
        osm = os_ref[...]
        pa = _dot(oa, wa_ref[...], 1, 0)
        ps = _dot(osm, ws_ref[...], 1, 0)
        sa = _sigmoid(ga_ref[...] + ba_ref[...])
        ss = _sigmoid(gs_ref[...] + bs_ref[...])
        dpa = (dm * sa).astype(BF16)
        dps = (dm * ss).astype(BF16)
        dga = dm * pa * sa * (1.0 - sa)
        dgs = dm * ps * ss * (1.0 - ss)
        dg_ref[:, :D] = dga.astype(BF16)
        dg_ref[:, D:] = dgs.astype(BF16)
        doa_ref[...] = _dot(dpa, wa_ref[...], 1, 1).astype(BF16)
        dos_ref[...] = _dot(dps, ws_ref[...], 1, 1)
        dwa = _dot(oa, dpa, 0, 0)
        dws = _dot(osm, dps, 0, 0)
        dba = jnp.sum(dga, axis=0, keepdims=True)
        dbs = jnp.sum(dgs, axis=0, keepdims=True)

        @pl.when(i == 0)
        def _():
            dwa_ref[...] = dwa
            dws_ref[...] = dws
            db_ref[:, :D] = dba
            db_ref[:, D:] = dbs

        @pl.when(i > 0)
        def _():
            dwa_ref[...] += dwa
            dws_ref[...] += dws
            db_ref[:, :D] += dba
            db_ref[:, D:] += dbs

    rowD = pl.BlockSpec((tr, D), lambda i: (i, 0))
    wspec = pl.BlockSpec((SB_WIDTH, D), lambda i: (0, 0))
    return pl.pallas_call(
        body, name=name,
        out_shape=(jax.ShapeDtypeStruct((S, SB_WIDTH), BF16), jax.ShapeDtypeStruct((S, SSM_WIDTH), F32),
                   jax.ShapeDtypeStruct((S, 2 * D), BF16), jax.ShapeDtypeStruct((1, 2 * D), F32),
                   jax.ShapeDtypeStruct((SB_WIDTH, D), F32), jax.ShapeDtypeStruct((SSM_WIDTH, D), F32)),
        grid=(S // tr,),
        in_specs=[rowD, pl.BlockSpec((tr, D), lambda i: (i, gb)), pl.BlockSpec((tr, D), lambda i: (i, gb + 1)),
                  pl.BlockSpec((tr, SB_WIDTH), lambda i: (i, 0)), pl.BlockSpec((tr, SSM_WIDTH), lambda i: (i, 0)),
                  wspec, wspec, pl.BlockSpec((1, D), lambda i: (0, 0)), pl.BlockSpec((1, D), lambda i: (0, 1))],
        out_specs=(pl.BlockSpec((tr, SB_WIDTH), lambda i: (i, 0)), pl.BlockSpec((tr, SSM_WIDTH), lambda i: (i, 0)),
                   pl.BlockSpec((tr, 2 * D), lambda i: (i, 0)), pl.BlockSpec((1, 2 * D), lambda i: (0, 0)),
                   wspec, wspec),
        compiler_params=_cparams("arbitrary"),
    )(dmerged, proj, proj, o_attn, o_ssm, w_ba, w_bs, b_gate, b_gate)


def _xattn_probs(q, k, h):
    cols = slice(h * XA_HEAD_DIM, (h + 1) * XA_HEAD_DIM)
    s = _dot(q[:, cols], k[:, cols], 1, 1) * (XA_HEAD_DIM ** -0.5)
    s = s - jnp.max(s, axis=-1, keepdims=True)
    e = jnp.exp(s)
    return e / jnp.sum(e, axis=-1, keepdims=True), cols


def _xattn_fwd(q2, k2, v2, *, name):
    S, D = q2.shape
    M = k2.shape[0]
    tr = _row_tile(S)

    def body(q_ref, k_ref, v_ref, o_ref):
        q = q_ref[...]
        k = k_ref[...]
        v = v_ref[...]
        for h in range(XA_HEADS):
            p, cols = _xattn_probs(q, k, h)
            o_ref[:, cols] = _dot(p, v[:, cols], 1, 0).astype(BF16)

    row = pl.BlockSpec((tr, D), lambda i: (i, 0))
    memb = pl.BlockSpec((M, D), lambda i: (0, 0))
    return pl.pallas_call(
        body, name=name, out_shape=jax.ShapeDtypeStruct((S, D), BF16), grid=(S // tr,),
        in_specs=[row, memb, memb], out_specs=row, compiler_params=_cparams("parallel"),
    )(q2, k2, v2)


def _xattn_bwd(q2, k2, v2, do2, *, name):
    S, D = q2.shape
    M = k2.shape[0]
    tr = _row_tile(S)
    scale = XA_HEAD_DIM ** -0.5

    def body(q_ref, k_ref, v_ref, do_ref, dq_ref, dk_ref, dv_ref):
        i = pl.program_id(0)

        @pl.when(i == 0)
        def _():
            dk_ref[...] = jnp.zeros_like(dk_ref)
            dv_ref[...] = jnp.zeros_like(dv_ref)

        q = q_ref[...]
        k = k_ref[...]
        v = v_ref[...]
        do = do_ref[...]
        for h in range(XA_HEADS):
            p, cols = _xattn_probs(q, k, h)
            dp = _dot(do[:, cols], v[:, cols], 1, 1)
            ds = (p * (dp - jnp.sum(dp * p, axis=-1, keepdims=True)) * scale).astype(BF16)
            dq_ref[:, cols] = _dot(ds, k[:, cols], 1, 0).astype(BF16)
            dk_ref[:, cols] += _dot(ds, q[:, cols], 0, 0)
            dv_ref[:, cols] += _dot(p, do[:, cols], 0, 0)

    row = pl.BlockSpec((tr, D), lambda i: (i, 0))
    memb = pl.BlockSpec((M, D), lambda i: (0, 0))
    return pl.pallas_call(
        body, name=name,
        out_shape=(jax.ShapeDtypeStruct((S, D), BF16), jax.ShapeDtypeStruct((M, D), F32), jax.ShapeDtypeStruct((M, D), F32)),
        grid=(S // tr,), in_specs=[row, memb, memb, row], out_specs=(row, memb, memb),
        compiler_params=_cparams("arbitrary"),
    )(q2, k2, v2, do2)


CONV_ROWS = 64
CONV_ROWS_FWD = 256


def _chunk(ref, c, rows):
    return ref[pl.ds(pl.multiple_of(c * rows, rows), rows), :]


def _rows_before(ref, c, rows):
    t0 = pl.multiple_of(jnp.maximum(c * rows - SUBLANES, 0), SUBLANES)
    return jnp.where(c > 0, ref[pl.ds(t0, SUBLANES), :], 0.0)


def _rows_after(ref, c, rows, n_chunks):
    t0 = pl.multiple_of(jnp.minimum((c + 1) * rows, n_chunks * rows - SUBLANES), SUBLANES)
    return jnp.where(c < n_chunks - 1, ref[pl.ds(t0, SUBLANES), :], 0.0)


def _shift_down(cur, before, d):
    out = pltpu.roll(cur, d, 0)
    r = lax.broadcasted_iota(jnp.int32, cur.shape, 0)
    for e in range(d):
        out = jnp.where(r == e, before[SUBLANES - d + e:SUBLANES - d + e + 1, :], out)
    return out


def _shift_up(cur, after, d):
    rows = cur.shape[0]
    out = pltpu.roll(cur, rows - d, 0)
    r = lax.broadcasted_iota(jnp.int32, cur.shape, 0)
    for e in range(d):
        out = jnp.where(r == rows - d + e, after[e:e + 1, :], out)
    return out


def _conv3(cur, before, w_ref, b_ref):
    return (w_ref[2:3, :] * cur + w_ref[1:2, :] * _shift_down(cur, before, 1)
            + w_ref[0:1, :] * _shift_down(cur, before, 2) + b_ref[...])


def _convgate_fwd(up_g, up_v, conv_w, conv_b, *, name):
    S, H = up_g.shape
    nb = H // LANES
    R = min(CONV_ROWS_FWD, S)
    n_chunks = S // R

    def body(g_ref, v_ref, wg_ref, wv_ref, bg_ref, bv_ref, a_ref):
        def chunk(c, _):
            cg = _conv3(_chunk(g_ref, c, R), _rows_before(g_ref, c, R), wg_ref, bg_ref)
            cv = _conv3(_chunk(v_ref, c, R), _rows_before(v_ref, c, R), wv_ref, bv_ref)
            a_ref[pl.ds(pl.multiple_of(c * R, R), R), :] = (_gelu(cg) * cv).astype(BF16)
            return 0

        lax.fori_loop(0, n_chunks, chunk, 0)

    col = lambda off: pl.BlockSpec((S, LANES), lambda j: (0, off + j))
    wcol = lambda off: pl.BlockSpec((3, LANES), lambda j: (0, off + j))
    bcol = lambda off: pl.BlockSpec((1, LANES), lambda j: (0, off + j))
    return pl.pallas_call(
        body, name=name, out_shape=jax.ShapeDtypeStruct((S, H), BF16), grid=(nb,),
        in_specs=[col(0), col(0), wcol(0), wcol(nb), bcol(0), bcol(nb)],
        out_specs=col(0), compiler_params=_cparams("parallel"),
    )(up_g, up_v, conv_w, conv_w, conv_b, conv_b)


def _convgate_bwd(up_g, up_v, da, conv_w, conv_b, *, name):
    S, H = up_g.shape
    nb = H // LANES
    R = min(CONV_ROWS, S)
    n_chunks = S // R

    def fold(a):
        return sum(a[r:r + SUBLANES] for r in range(0, a.shape[0], SUBLANES))

    def body(g_ref, v_ref, da_ref, wg_ref, wv_ref, bg_ref, bv_ref,
             dug_ref, duv_ref, dwg_ref, dwv_ref, dbg_ref, dbv_ref, dcg_s, dcv_s):
        def first_pass(c, acc):
            rows = pl.ds(pl.multiple_of(c * R, R), R)
            ug, uv = _chunk(g_ref, c, R), _chunk(v_ref, c, R)
            bg, bv = _rows_before(g_ref, c, R), _rows_before(v_ref, c, R)
            cg = _conv3(ug, bg, wg_ref, bg_ref)
            cv = _conv3(uv, bv, wv_ref, bv_ref)
            da = da_ref[rows, :]
            gl, dgl = _gelu_and_grad(cg)
            dcg = da * cv * dgl
            dcv = da * gl
            dcg_s[rows, :] = dcg
            dcv_s[rows, :] = dcv
            new = []
            for dc, u, before in ((dcg, ug, bg), (dcv, uv, bv)):
                new += [fold(dc * _shift_down(u, before, 2)), fold(dc * _shift_down(u, before, 1)), fold(dc * u), fold(dc)]
            return tuple(a + n for a, n in zip(acc, new))

        zero = jnp.zeros((SUBLANES, LANES), F32)
        acc = lax.fori_loop(0, n_chunks, first_pass, (zero,) * 8)
        total = [jnp.sum(a, axis=0, keepdims=True) for a in acc]
        for k, (dw_ref, db_ref) in enumerate(((dwg_ref, dbg_ref), (dwv_ref, dbv_ref))):
            dw_ref[0:1, :] = total[4 * k]
            dw_ref[1:2, :] = total[4 * k + 1]
            dw_ref[2:3, :] = total[4 * k + 2]
            db_ref[...] = total[4 * k + 3]

        def second_pass(c, _):
            rows = pl.ds(pl.multiple_of(c * R, R), R)
            for dc_s, w_ref, du_ref in ((dcg_s, wg_ref, dug_ref), (dcv_s, wv_ref, duv_ref)):
                cur, after = _chunk(dc_s, c, R), _rows_after(dc_s, c, R, n_chunks)
                du = w_ref[2:3, :] * cur + w_ref[1:2, :] * _shift_up(cur, after, 1) + w_ref[0:1, :] * _shift_up(cur, after, 2)
                du_ref[rows, :] = du.astype(BF16)
            return 0

        lax.fori_loop(0, n_chunks, second_pass, 0)

    col = lambda off: pl.BlockSpec((S, LANES), lambda j: (0, off + j))
    wcol = lambda off: pl.BlockSpec((3, LANES), lambda j: (0, off + j))
    bcol = lambda off: pl.BlockSpec((1, LANES), lambda j: (0, off + j))
    return pl.pallas_call(
        body, name=name,
        out_shape=(jax.ShapeDtypeStruct((S, H), BF16), jax.ShapeDtypeStruct((S, H), BF16),
                   jax.ShapeDtypeStruct((3, H), F32), jax.ShapeDtypeStruct((3, H), F32),
                   jax.ShapeDtypeStruct((1, H), F32), jax.ShapeDtypeStruct((1, H), F32)),
        grid=(nb,),
        in_specs=[col(0), col(0), col(0), wcol(0), wcol(nb), bcol(0), bcol(nb)],
        out_specs=(col(0), col(0), wcol(0), wcol(0), bcol(0), bcol(0)),
        scratch_shapes=[pltpu.VMEM((S, LANES), F32), pltpu.VMEM((S, LANES), F32)],
        compiler_params=_cparams("parallel"),
    )(up_g, up_v, da, conv_w, conv_w, conv_b, conv_b)


def _local_step(x, mem, target, w_in, late_wire, P, core):
    mm = _matmul
    h1, (w_in,) = _rms_fwd(x, P["norm_mix_pre"], name="rms_mix_pre", rider=_fill_xy([w_in]))
    w_in, = _fill_c([w_in]).run(name="gather_in_c")
    w_in = w_in.reshape((N_DEV,) + w_in.shape[2:])
    n_mid = len(LATE) - len(REDUCE_FFN)
    proj, wire_mid = mm(h1, w_in, name="mm_in", rider=_fill_xy(late_wire[:n_mid]))
    (o_attn, sb_tot, sb_first), wires = _sb_fwd(
        proj, name="sb_fwd", rider=_Exchange.join(_fill_c(wire_mid), _fill_xy(late_wire[n_mid:])))
    wire_mid, wire_ffn = wires[:n_mid], wires[n_mid:]

    ssm_prep = lambda *a: _ssm_prepare(*a)
    (lam_re, lam_im, bb_re, bb_im), prep_vjp = jax.vjp(
        ssm_prep, P["ssm_a_re"], P["ssm_a_im"], P["ssm_log_dt"], P["ssm_b_re"], P["ssm_b_im"])
    tab_f, tab_b = _ssm_tables(lam_re, lam_im)
    bd_re = _bd_from_bbar(bb_re).astype(BF16)
    bd_im = _bd_from_bbar(bb_im).astype(BF16)
    cd_re = _cd_from_c(P["ssm_c_re"]).astype(BF16)
    cd_imneg = _cd_from_c(-P["ssm_c_im"]).astype(BF16)
    (y_pre, x_re, x_im), wire_ffn = _ssm_fwd(proj, bd_re, bd_im, cd_re, cd_imneg, P["ssm_d"], tab_f,
                                             name="ssm_fwd", rider=_fill_c(wire_ffn))
    W = _weights_from_wire(dict(zip(LATE, list(wire_mid) + list(wire_ffn))))
    W["w_in"] = w_in
    o_ssm = _glu_fwd(y_pre, W["ssm_w_glu"], P["ssm_b_glu"], name="glu_fwd")

    merged = _merge_fwd(proj, o_attn, o_ssm, W["w_branch_attn"], W["w_branch_ssm"], P["b_gate"], name="merge_fwd")
    mo = mm(merged, W["w_out"], name="mm_out")
    x1, h2 = _resnorm_norm(x, mo, P["norm_mix_post"], P["norm_xa_pre"], name="resnorm_1")

    mem_n = _rms_fwd(mem, P["norm_mem"], name="rms_mem")
    q2 = mm(h2, W["xa_wq"], out_dtype=BF16, name="mm_xq")
    k2 = mm(mem_n, W["xa_wk"], out_dtype=BF16, name="mm_xk")
    v2 = mm(mem_n, W["xa_wv"], out_dtype=BF16, name="mm_xv")
    o2 = _xattn_fwd(q2, k2, v2, name="xattn_fwd")
    xa = mm(o2, W["xa_wo"], name="mm_xo")
    x2, h3 = _resnorm_norm(x1, xa, P["norm_xa_post"], P["norm_ffn_pre"], name="resnorm_2")

    half = N_DEV // 2
    up_g = mm(h3, W["ffn_w_up"], n_blocks=half, name="mm_up_g")
    up_v = mm(h3, W["ffn_w_up"], b_block0=half, name="mm_up_v")
    act = _convgate_fwd(up_g, up_v, W["ffn_conv_w"], P["ffn_conv_b"], name="convgate_fwd")
    f = mm(act, W["ffn_w_down"], name="mm_down")
    loss, dy, df, dg_ffn_post = _final_loss(x2, f, P["norm_ffn_post"], target, name="final_loss")

    G = {"norm_ffn_post": dg_ffn_post}
    dact = mm(df, W["ffn_w_down"], tb=True, name="mm_down_dx")
    G["ffn_w_down"] = mm(act, df, ta=True, name="mm_down_dw")
    dug, duv, dwg, dwv, dbg, dbv = _convgate_bwd(up_g, up_v, dact, W["ffn_conv_w"], P["ffn_conv_b"], name="convgate_bwd")
    G["ffn_conv_w"] = jnp.concatenate([dwg, dwv], axis=1)
    G["ffn_conv_b"] = jnp.concatenate([dbg, dbv], axis=1)
    dh3 = mm(dug, W["ffn_w_up"], tb=True, n_blocks=half, name="mm_up_g_dx")
    dh3 = mm(duv, W["ffn_w_up"], tb=True, b_block0=half, acc_in=dh3, name="mm_up_v_dx")
    dw_up = mm(h3, dug, ta=True, out_into=lax.empty(W["ffn_w_up"].shape, F32), name="mm_up_g_dw")
    G["ffn_w_up"] = mm(h3, duv, ta=True, out_into=dw_up, out_block0=half, name="mm_up_v_dw")
    blocks = {n: _grad_blocks(n, G[n]) for n in REDUCE_FFN}
    (dx2, dxa, G["norm_ffn_pre"], G["norm_xa_post"]), from_core = _norm_bwd_pair(
        dy, dh3, x2, P["norm_ffn_pre"], xa, P["norm_xa_post"], name="norm_bwd_3",
        rider=_send_c([blocks[n] for n in REDUCE_FFN]))
    pair = {n: _pair_sum(blocks[n], r, core, name="pair_sum_" + n) for n, r in zip(REDUCE_FFN, from_core)}

    G["xa_wo"] = mm(o2, dxa, ta=True, name="mm_xo_dw")
    do2 = mm(dxa, W["xa_wo"], tb=True, out_dtype=BF16, name="mm_xo_dx")
    dq2, dk2, dv2 = _xattn_bwd(q2, k2, v2, do2, name="xattn_bwd")
    G["xa_wq"] = mm(h2, dq2, ta=True, name="mm_xq_dw")
    dh2 = mm(dq2, W["xa_wq"], tb=True, name="mm_xq_dx")
    G["xa_wk"] = mm(mem_n, dk2, ta=True, name="mm_xk_dw")
    G["xa_wv"] = mm(mem_n, dv2, ta=True, name="mm_xv_dw")
    dmem_n = jnp.concatenate([dk2, dv2], axis=1)
    wkv = jnp.concatenate([W["xa_wk"], W["xa_wv"]], axis=1)
    dmem = mm(dmem_n, wkv, tb=True, name="mm_xkv_dx")
    _, G["norm_mem"] = _norm_bwd_single(None, dmem, mem, P["norm_mem"], name="norm_bwd_mem")
    (dx1, dmo, G["norm_xa_pre"], G["norm_mix_post"]), _ = _norm_bwd_pair(
        dx2, dh2, x1, P["norm_xa_pre"], mo, P["norm_mix_post"], name="norm_bwd_2")

    G["w_out"] = mm(merged, dmo, ta=True, name="mm_out_dw")
    dmerged = mm(dmo, W["w_out"], tb=True, name="mm_out_dx")
    do_attn, do_ssm, dgate, G["b_gate"], G["w_branch_attn"], G["w_branch_ssm"] = _merge_bwd(
        dmerged, proj, o_attn, o_ssm, W["w_branch_attn"], W["w_branch_ssm"], P["b_gate"], name="merge_bwd")
    dy_pre, G["ssm_w_glu"], G["ssm_b_glu"] = _glu_bwd(y_pre, do_ssm, W["ssm_w_glu"], P["ssm_b_glu"], name="glu_bwd")
    blocks.update({n: _grad_blocks(n, G[n]) for n in REDUCE_MID})
    (du, dbd_re, dbd_im, dcd_re, dcd_imneg, G["ssm_d"], dl_re, dl_im), brought = _ssm_bwd(
        dy_pre, proj, x_re, x_im, bd_re, bd_im, cd_re, cd_imneg, P["ssm_d"], tab_b, name="ssm_bwd",
        rider=_Exchange.join(_send_c([blocks[n] for n in REDUCE_MID]), _scatter_xy([pair[n] for n in REDUCE_FFN])))
    from_core, from_chips = brought[:len(REDUCE_MID)], brought[len(REDUCE_MID):]
    reduced = {n: (pair[n], parts) for n, parts in zip(REDUCE_FFN, from_chips)}
    pair.update({n: _pair_sum(blocks[n], r, core, name="pair_sum_" + n) for n, r in zip(REDUCE_MID, from_core)})
    G["ssm_c_re"] = _c_from_cd(dcd_re)
    G["ssm_c_im"] = -_c_from_cd(dcd_imneg)
    dlam_re = jnp.sum(dl_re, axis=1).reshape(SSM_GROUPS, SSM_STATE)
    dlam_im = jnp.sum(dl_im, axis=1).reshape(SSM_GROUPS, SSM_STATE)
    (G["ssm_a_re"], G["ssm_a_im"], G["ssm_log_dt"], G["ssm_b_re"], G["ssm_b_im"]) = prep_vjp(
        (dlam_re, dlam_im, _bbar_from_bd(dbd_re), _bbar_from_bd(dbd_im)))
    G["ffn_conv_b"] = G["ffn_conv_b"].reshape(N_DEV, FF_LOCAL_PAD)[:, :FF_LOCAL]
    small = [G[n].reshape(SMALL_SHAPE[n]) for n in SMALL_EARLY]
    (dq, dk, dv), brought = _sb_bwd(
        proj, sb_tot, sb_first, do_attn, name="sb_bwd",
        rider=_Exchange.join(_scatter_xy([pair[n] for n in REDUCE_MID]), _gather_xy_from(small)))
    from_chips, small = brought[:len(REDUCE_MID)], brought[len(REDUCE_MID):]
    reduced.update({n: (pair[n], parts) for n, parts in zip(REDUCE_MID, from_chips)})
    dproj = jnp.concatenate([dq, dk, dv, du, dgate], axis=1)
    G["w_in"], small = mm(h1, dproj, ta=True, out_cb=W["w_in"].shape[2], name="mm_in_dw", rider=_fill_c(small))
    g_in = _grad_blocks("w_in", G["w_in"])
    dh1, (from_core,) = mm(dproj, W["w_in"], tb=True, name="mm_in_dx", rider=_send_c([g_in]))
    pair_in = _pair_sum(g_in, from_core, core, name="pair_sum_w_in")
    (grad_x, dg_pre), (from_chips,) = _norm_bwd_single(dx1, dh1, x, P["norm_mix_pre"], name="norm_bwd_1",
                                                       rider=_scatter_xy([pair_in]))
    reduced["w_in"] = (pair_in, from_chips)
    last, = _gather_all([dg_pre]).run(name="gather_g_last")
    parts = dict(zip(SMALL_EARLY, small))
    parts["norm_mix_pre"] = last
    return loss, grad_x, parts, reduced


MESH = pl.DeviceIdType.MESH
_HBM = pl.BlockSpec(memory_space=pl.ANY)
N_XY = 4
N_XY_PEERS = 3


def _xy_peers(x, y):
    return [(1 - x, y), (x, 1 - y), (1 - x, 1 - y)]


class _Exchange:
    def __init__(self, arrays, out_shapes, plan, n_copies, alias):
        self.arrays = list(arrays)
        self.out_shapes = list(out_shapes)
        self.plan = plan
        self.n_copies = n_copies
        self.alias = list(alias) if isinstance(alias, (list, tuple)) else [alias] * len(self.arrays)

    @property
    def n(self):
        return len(self.arrays)

    def aliases(self, first_in, first_out):
        return {first_in + k: first_out + k for k in range(self.n) if self.alias[k]}

    @staticmethod
    def join(a, b):
        def plan(k, src, dst, x, y, c):
            return a.plan(k, src, dst, x, y, c) if k < a.n else b.plan(k - a.n, src, dst, x, y, c)

        return _Exchange(a.arrays + b.arrays, a.out_shapes + b.out_shapes, plan, max(a.n_copies, b.n_copies),
                         a.alias + b.alias)

    def sems(self):
        shape = (self.n, self.n_copies)
        return [pltpu.SemaphoreType.DMA(shape), pltpu.SemaphoreType.DMA(shape)]

    def _copies(self, ins, outs, send_sems, recv_sems):
        x, y, c = lax.axis_index("x"), lax.axis_index("y"), lax.axis_index("c")
        sends, lands, own = [], [], []
        for k in range(self.n):
            for j, (src, dst, dev, land) in enumerate(self.plan(k, ins[k], outs[k], x, y, c)):
                if dev is None:
                    own.append(pltpu.make_async_copy(src, dst, send_sems.at[k, j]))
                    continue
                sems = dict(send_sem=send_sems.at[k, j], recv_sem=recv_sems.at[k, j], device_id=dev, device_id_type=MESH)
                sends.append(pltpu.make_async_remote_copy(src_ref=src, dst_ref=dst, **sems))
                lands.append(pltpu.make_async_remote_copy(src_ref=src, dst_ref=land, **sems))
        return sends, lands, own

    def start(self, ins, outs, send_sems, recv_sems):
        sends, _, own = self._copies(ins, outs, send_sems, recv_sems)
        for cp in own + sends:
            cp.start()

    def finish(self, ins, outs, send_sems, recv_sems):
        sends, lands, own = self._copies(ins, outs, send_sems, recv_sems)
        for cp in lands:
            cp.wait_recv()
        for cp in sends:
            cp.wait_send()
        for cp in own:
            cp.wait()

    def run(self, *, name):
        n = self.n

        def body(*refs):
            parts = (refs[:n], refs[n:2 * n], refs[2 * n], refs[2 * n + 1])
            self.start(*parts)
            self.finish(*parts)

        return pl.pallas_call(
            body, name=name, out_shape=tuple(self.out_shapes),
            in_specs=[_HBM] * n, out_specs=tuple([_HBM] * n),
            input_output_aliases=self.aliases(0, 0),
            scratch_shapes=self.sems(),
        )(*self.arrays)


def _call(host_body, *, name, grid, in_specs, out_specs, out_shape, scratch_shapes, operands, rider=None):
    out_specs, out_shape = tuple(out_specs), tuple(out_shape)
    if rider is None:
        res = pl.pallas_call(
            host_body, name=name, grid=grid, in_specs=list(in_specs), out_specs=out_specs, out_shape=out_shape,
            scratch_shapes=list(scratch_shapes), compiler_params=_cparams(*["arbitrary"] * len(grid)),
        )(*operands)
        return tuple(res), None
    n, n_in, n_out, n_scr = rider.n, len(in_specs), len(out_specs), len(scratch_shapes)

    def body(*refs):
        pos = [0]

        def take(count):
            pos[0] += count
            return refs[pos[0] - count:pos[0]]

        h_in, r_in, h_out, r_out, h_scr = take(n_in), take(n), take(n_out), take(n), take(n_scr)
        send_sems, recv_sems = take(2)
        ids = [pl.program_id(a) for a in range(len(grid))]
        first = functools.reduce(jnp.logical_and, [i == 0 for i in ids])
        last = functools.reduce(jnp.logical_and, [i == g - 1 for i, g in zip(ids, grid)])

        @pl.when(first)
        def _():
            rider.start(r_in, r_out, send_sems, recv_sems)

        host_body(*h_in, *h_out, *h_scr)

        @pl.when(last)
        def _():
            rider.finish(r_in, r_out, send_sems, recv_sems)

    res = pl.pallas_call(
        body, name=name, grid=grid,
        in_specs=list(in_specs) + [_HBM] * n, out_specs=out_specs + tuple([_HBM] * n),
        out_shape=out_shape + tuple(rider.out_shapes),
        input_output_aliases=rider.aliases(n_in, n_out),
        scratch_shapes=list(scratch_shapes) + rider.sems(),
        compiler_params=_cparams(*["arbitrary"] * len(grid)),
    )(*operands, *rider.arrays)
    return tuple(res[:n_out]), list(res[n_out:])


def _same(arrays):
    return [jax.ShapeDtypeStruct(a.shape, a.dtype) for a in arrays]


def _fill_xy(bufs):
    def plan(k, src, dst, x, y, c):
        mine = 2 * x + y
        return [(src.at[mine, c], dst.at[mine, c], (px, py, c), dst.at[2 * px + py, c]) for px, py in _xy_peers(x, y)]

    return _Exchange(bufs, _same(bufs), plan, N_XY_PEERS, alias=True)


def _fill_c(bufs):
    def plan(k, src, dst, x, y, c):
        return [(src.at[:, c], dst.at[:, c], (x, y, 1 - c), dst.at[:, 1 - c])]

    return _Exchange(bufs, _same(bufs), plan, 1, alias=True)


def _slots(arrays):
    return [jax.ShapeDtypeStruct((N_XY, 2) + a.shape, a.dtype) for a in arrays]


def _gather_xy_from(srcs):
    def plan(k, src, dst, x, y, c):
        mine = 2 * x + y
        return ([(src, dst.at[mine, c], None, None)]
                + [(src, dst.at[mine, c], (px, py, c), dst.at[2 * px + py, c]) for px, py in _xy_peers(x, y)])

    return _Exchange(srcs, _slots(srcs), plan, 1 + N_XY_PEERS, alias=False)


def _gather_all(srcs):
    def plan(k, src, dst, x, y, c):
        mine = 2 * x + y
        out = [(src, dst.at[mine, c], None, None)]
        for fx, fy, fc in [(a, b, e) for a in (0, 1) for b in (0, 1) for e in (0, 1)][1:]:
            px, py, pc = (1 - x) if fx else x, (1 - y) if fy else y, (1 - c) if fc else c
            out.append((src, dst.at[mine, c], (px, py, pc), dst.at[2 * px + py, pc]))
        return out

    return _Exchange(srcs, _slots(srcs), plan, N_DEV, alias=False)


def _send_c(srcs):
    def plan(k, src, dst, x, y, c):
        return [(src.at[:, 1 - c], dst, (x, y, 1 - c), dst)]

    outs = [jax.ShapeDtypeStruct(a.shape[:1] + a.shape[2:], a.dtype) for a in srcs]
    return _Exchange(srcs, outs, plan, 1, alias=False)


def _scatter_xy(srcs):
    def plan(k, src, dst, x, y, c):
        return [(src.at[2 * px + py], dst.at[j], (px, py, c), dst.at[j]) for j, (px, py) in enumerate(_xy_peers(x, y))]

    outs = [jax.ShapeDtypeStruct((N_XY_PEERS,) + a.shape[1:], a.dtype) for a in srcs]
    return _Exchange(srcs, outs, plan, N_XY_PEERS, alias=False)


WIRE_DTYPE = BF16


def _pair_sum(g8, recv, core, *, name):
    n, _, R, C = g8.shape
    tr = _pick(R, (128, 64, 32, 16, 8))

    def body(core_ref, a_ref, b_ref, o_ref):
        o_ref[...] = (a_ref[0] + b_ref[...]).astype(WIRE_DTYPE)

    return pl.pallas_call(
        body, name=name, out_shape=jax.ShapeDtypeStruct((n, R, C), WIRE_DTYPE),
        grid_spec=pltpu.PrefetchScalarGridSpec(
            num_scalar_prefetch=1, grid=(n, R // tr),
            in_specs=[pl.BlockSpec((1, 1, tr, C), lambda s, i, core_ref: (s, core_ref[0], i, 0)),
                      pl.BlockSpec((1, tr, C), lambda s, i, core_ref: (s, i, 0))],
            out_specs=pl.BlockSpec((1, tr, C), lambda s, i, core_ref: (s, i, 0))),
        compiler_params=_cparams("parallel", "parallel"),
    )(core, g8, recv)


def _adamw_math(w, g, m, v):
    m = ADAM_B1 * m + (1.0 - ADAM_B1) * g
    v = ADAM_B2 * v + (1.0 - ADAM_B2) * (g * g)
    m_hat = m / (1.0 - ADAM_B1 ** ADAM_STEP)
    v_hat = v / (1.0 - ADAM_B2 ** ADAM_STEP)
    delta = -ADAM_LR * (m_hat / (jnp.sqrt(v_hat) + ADAM_EPS) + ADAM_WD * w)
    return delta, m, v


def _reduce_adamw(parts, w, m, v, *, own, own_slot, name):
    n, R, C = parts.shape
    tr = _pick(R, (128, 64, 32, 16, 8))

    def body(_, own_ref, parts_ref, w_ref, m_ref, v_ref, g_ref, d_ref, nm_ref, nv_ref):
        g = own_ref[0].astype(F32)
        for k in range(n):
            g = g + parts_ref[k].astype(F32)
        g_ref[...] = g
        d_ref[...], nm_ref[...], nv_ref[...] = _adamw_math(w_ref[...], g, m_ref[...], v_ref[...])

    out = jax.ShapeDtypeStruct((R, C), F32)
    row = pl.BlockSpec((tr, C), lambda i, s: (i, 0))
    return pl.pallas_call(
        body, name=name, out_shape=(out, out, out, out),
        grid_spec=pltpu.PrefetchScalarGridSpec(
            num_scalar_prefetch=1, grid=(R // tr,),
            in_specs=[pl.BlockSpec((1, tr, C), lambda i, s: (s[0], i, 0)),
                      pl.BlockSpec((n, tr, C), lambda i, s: (0, i, 0)), row, row, row],
            out_specs=(row, row, row, row)),
        compiler_params=_cparams("parallel"),
    )(own_slot, own, parts, w, m, v)


SHARDED = (("w_in", (1024, 4096), 1), ("ssm_w_glu", (512, 512), 0), ("w_branch_attn", (512, 1024), 1),
           ("w_branch_ssm", (512, 1024), 1), ("w_out", (1024, 1024), 0), ("xa_wq", (1024, 1024), 0),
           ("xa_wk", (1024, 1024), 0), ("xa_wv", (1024, 1024), 0), ("xa_wo", (1024, 1024), 0),
           ("ffn_w_up", (1024, 5632), 1), ("ffn_conv_w", (3, 5632), 1), ("ffn_w_down", (2816, 1024), 0))
REPLICATED = (("norm_mix_pre", (1024,)), ("norm_mix_post", (1024,)), ("b_gate", (2048,)), ("ssm_a_re", (32, 64)),
              ("ssm_a_im", (32, 64)), ("ssm_log_dt", (32,)), ("ssm_b_re", (32, 64, 16)), ("ssm_b_im", (32, 64, 16)),
              ("ssm_c_re", (32, 16, 64)), ("ssm_c_im", (32, 16, 64)), ("ssm_d", (512,)), ("ssm_b_glu", (512,)),
              ("norm_xa_pre", (1024,)), ("norm_xa_post", (1024,)), ("norm_mem", (1024,)), ("norm_ffn_pre", (1024,)),
              ("norm_ffn_post", (1024,)), ("ffn_conv_b", (5632,)))
PARAM_ORDER = ("norm_mix_pre", "norm_mix_post", "w_in", "b_gate", "ssm_a_re", "ssm_a_im", "ssm_log_dt", "ssm_b_re",
               "ssm_b_im", "ssm_c_re", "ssm_c_im", "ssm_d", "ssm_w_glu", "ssm_b_glu", "w_branch_attn", "w_branch_ssm",
               "w_out", "norm_xa_pre", "norm_xa_post", "norm_mem", "xa_wq", "xa_wk", "xa_wv", "xa_wo", "norm_ffn_pre",
               "norm_ffn_post", "ffn_w_up", "ffn_conv_w", "ffn_conv_b", "ffn_w_down")
FF_LOCAL = 2 * D_FF // N_DEV
FF_LOCAL_PAD = 768
FF_PAD = (N_DEV // 2) * FF_LOCAL_PAD


def _local_shape(shape, axis):
    return tuple(s // N_DEV if a == axis else s for a, s in enumerate(shape))


def _pad_cols(a, width):
    return jnp.pad(a, [(0, 0)] * (a.ndim - 1) + [(0, width - a.shape[-1])])


def _blocks_to_cols(a8):
    return a8.transpose(1, 0, 2).reshape(a8.shape[1], N_DEV * a8.shape[2])


def _cols_to_blocks(a, cb):
    return a.reshape(a.shape[0], N_DEV, cb).transpose(1, 0, 2)


FF_PADDED = ("ffn_w_up", "ffn_conv_w")
LATE = tuple(n for n, _, _ in SHARDED if n != "w_in")
REDUCE_FFN = ("ffn_w_up", "ffn_conv_w", "ffn_w_down")
REDUCE_MID = ("xa_wo", "xa_wq", "xa_wk", "xa_wv", "w_out", "w_branch_attn", "w_branch_ssm", "ssm_w_glu")
SHARD_AXIS = {n: ax for n, _, ax in SHARDED}
FULL_SHAPE = {n: s for n, s, _ in SHARDED}


def _as_local(n, a):
    return _pad_cols(a, FF_LOCAL_PAD) if n in FF_PADDED else a


def _weights_from_wire(wire):
    full = {n: b.reshape((N_DEV,) + b.shape[2:]) for n, b in wire.items()}
    W = {n: a.reshape(FULL_SHAPE[n]) if SHARD_AXIS[n] == 0 else a for n, a in full.items()}
    for n in ("w_branch_attn", "w_branch_ssm", "ffn_conv_w"):
        W[n] = _blocks_to_cols(full[n])
    W["ffn_w_down"] = jnp.pad(W["ffn_w_down"].reshape(N_DEV // 2, FF_LOCAL, D_MODEL),
                              ((0, 0), (0, FF_LOCAL_PAD - FF_LOCAL), (0, 0))).reshape(FF_PAD, D_MODEL)
    return W


def _grad_blocks(n, g):
    if n in ("w_branch_attn", "w_branch_ssm"):
        g = _cols_to_blocks(g, D_MODEL // N_DEV)
    elif n == "ffn_conv_w":
        g = _cols_to_blocks(g, FF_LOCAL_PAD)
    elif n == "ffn_w_down":
        g = g.reshape(N_DEV // 2, FF_LOCAL_PAD, D_MODEL)[:, :FF_LOCAL]
    local = _local_shape(FULL_SHAPE[n], SHARD_AXIS[n])
    if n in FF_PADDED:
        local = local[:-1] + (FF_LOCAL_PAD,)
    return g.reshape((N_XY, 2) + local)


SMALL_SHAPE = {n: (1, s[0]) if len(s) == 1 else (s[0], math.prod(s[1:])) for n, s in REPLICATED}
SMALL_SHAPE["ffn_conv_b"] = (N_DEV, FF_LOCAL)
SMALL_EARLY = tuple(n for n, _ in REPLICATED if n != "norm_mix_pre")


def _adamw_replicated(parts, w, m, v, *, name):
    n = len(parts)

    def body(*refs):
        p_refs, w_refs, m_refs, v_refs = (refs[i * n:(i + 1) * n] for i in range(4))
        outs = refs[4 * n:]
        for k in range(n):
            g = p_refs[k][0, 0]
            for s in range(1, N_DEV):
                g = g + p_refs[k][s // 2, s % 2]
            d, nm, nv = _adamw_math(w_refs[k][...], g, m_refs[k][...], v_refs[k][...])
            for slot, val in enumerate((g, d, nm, nv)):
                outs[slot * n + k][...] = val

    vmem = pl.BlockSpec(memory_space=pltpu.VMEM)
    shapes = [jax.ShapeDtypeStruct(a.shape, F32) for a in w] * 4
    res = pl.pallas_call(
        body, name=name, out_shape=tuple(shapes), in_specs=[vmem] * (4 * n), out_specs=tuple([vmem] * (4 * n)),
        compiler_params=pltpu.CompilerParams(vmem_limit_bytes=VMEM_LIMIT),
    )(*parts, *w, *m, *v)
    return [list(res[i * n:(i + 1) * n]) for i in range(4)]


def kernel(x, mem, norm_mix_pre, norm_mix_post, w_in, b_gate, ssm_a_re, ssm_a_im, ssm_log_dt, ssm_b_re, ssm_b_im, ssm_c_re, ssm_c_im, ssm_d, ssm_w_glu, ssm_b_glu, w_branch_attn, w_branch_ssm, w_out, norm_xa_pre, norm_xa_post, norm_mem, xa_wq, xa_wk, xa_wv, xa_wo, norm_ffn_pre, norm_ffn_post, ffn_w_up, ffn_conv_w, ffn_conv_b, ffn_w_down, loss_target, m_norm_mix_pre, m_norm_mix_post, m_w_in, m_b_gate, m_ssm_a_re, m_ssm_a_im, m_ssm_log_dt, m_ssm_b_re, m_ssm_b_im, m_ssm_c_re, m_ssm_c_im, m_ssm_d, m_ssm_w_glu, m_ssm_b_glu, m_w_branch_attn, m_w_branch_ssm, m_w_out, m_norm_xa_pre, m_norm_xa_post, m_norm_mem, m_xa_wq, m_xa_wk, m_xa_wv, m_xa_wo, m_norm_ffn_pre, m_norm_ffn_post, m_ffn_w_up, m_ffn_conv_w, m_ffn_conv_b, m_ffn_w_down, v_norm_mix_pre, v_norm_mix_post, v_w_in, v_b_gate, v_ssm_a_re, v_ssm_a_im, v_ssm_log_dt, v_ssm_b_re, v_ssm_b_im, v_ssm_c_re, v_ssm_c_im, v_ssm_d, v_ssm_w_glu, v_ssm_b_glu, v_w_branch_attn, v_w_branch_ssm, v_w_out, v_norm_xa_pre, v_norm_xa_post, v_norm_mem, v_xa_wq, v_xa_wk, v_xa_wv, v_xa_wo, v_norm_ffn_pre, v_norm_ffn_post, v_ffn_w_up, v_ffn_conv_w, v_ffn_conv_b, v_ffn_w_down):
    args = dict(locals())
    w_loc = {n: args[n][0] for n in PARAM_ORDER}
    m_loc = {n: args["m_" + n][0] for n in PARAM_ORDER}
    v_loc = {n: args["v_" + n][0] for n in PARAM_ORDER}
    core_i = lax.axis_index("c")
    chip_i = 2 * lax.axis_index("x") + lax.axis_index("y")
    core = core_i.astype(jnp.int32).reshape(1)
    chip = chip_i.astype(jnp.int32).reshape(1)

    def in_place(a):
        buf = lax.empty((N_XY, 2) + a.shape, a.dtype)
        return lax.dynamic_update_slice(buf, a[None, None], (chip_i, core_i) + (0,) * a.ndim)

    as_wire = lambda n: in_place(_as_local(n, w_loc[n]).astype(F32 if n == "ffn_conv_w" else BF16))

    P = {}
    for n, shape in REPLICATED:
        P[n] = w_loc[n] if len(shape) > 1 or n == "ssm_log_dt" else w_loc[n].reshape(1, -1)
    P["ffn_conv_b"] = _pad_cols(w_loc["ffn_conv_b"].reshape(N_DEV, FF_LOCAL), FF_LOCAL_PAD).reshape(1, 2 * FF_PAD)

    loss, grad_x, small_parts, reduced = _local_step(x[0], mem[0], loss_target[0], as_wire("w_in"),
                                                     [as_wire(n) for n in LATE], P, core)
    loss = lax.psum(loss[0, 0], ("x", "y", "c"))

    big_out = {}
    for n, (own, parts) in reduced.items():
        res = _reduce_adamw(parts, _as_local(n, w_loc[n]), _as_local(n, m_loc[n]), _as_local(n, v_loc[n]),
                            own=own, own_slot=chip, name="adamw_" + n)
        big_out[n] = [r[:, :FF_LOCAL] if n in FF_PADDED else r for r in res]

    names = [n for n, _ in REPLICATED]
    as_small = lambda d: [d[n].reshape(SMALL_SHAPE[n]) for n in names]
    small_out = _adamw_replicated([small_parts[n] for n in names], as_small(w_loc), as_small(m_loc), as_small(v_loc),
                                  name="adamw_replicated")
    small_out = [dict(zip(names, res)) for res in small_out]

    outs = [loss, grad_x[None]]
    for k in range(4):
        for n in PARAM_ORDER:
            src = big_out[n][k] if n in big_out else small_out[k][n]
            outs.append(src.reshape(args[n].shape))
    return tuple(outs)
```

```python
import functools
import math

import jax
import jax.numpy as jnp
from jax import lax
from jax.experimental import pallas as pl
from jax.experimental.pallas import tpu as pltpu

F32 = jnp.float32
BF16 = jnp.bfloat16

D_MODEL = 1024
SB_HEADS = 8
SB_HEAD_DIM = 64
SB_WIDTH = 512
SSM_WIDTH = 512
SSM_GROUP = 16
SSM_GROUPS = 32
SSM_STATE = 64
XA_HEADS = 4
XA_HEAD_DIM = 256
D_FF = 2816
RMS_EPS = 1e-6
IN_WIDTH = 4096
N_DEV = 8

ADAM_LR = 0.001
ADAM_B1 = 0.9
ADAM_B2 = 0.999
ADAM_EPS = 1e-08
ADAM_WD = 0.01
ADAM_STEP = 10

LANES = 128
SUBLANES = 8
VMEM_LIMIT = 48 * 1024 * 1024

_GELU_C = math.sqrt(2.0 / math.pi)


def _cparams(*sem):
    return pltpu.CompilerParams(dimension_semantics=sem, vmem_limit_bytes=VMEM_LIMIT)


def _pick(n, cands):
    for c in cands:
        if n % c == 0:
            return c
    return n


def _gelu(x):
    return 0.5 * x * (1.0 + jnp.tanh(_GELU_C * (x + 0.044715 * x * x * x)))


def _gelu_and_grad(x):
    t = jnp.tanh(_GELU_C * (x + 0.044715 * x * x * x))
    g = 0.5 * x * (1.0 + t)
    dg = 0.5 * (1.0 + t) + 0.5 * x * (1.0 - t * t) * _GELU_C * (1.0 + 3.0 * 0.044715 * x * x)
    return g, dg


def _sigmoid(x):
    return 1.0 / (1.0 + jnp.exp(-x))


def _dot(a, b, ca, cb):
    return lax.dot_general(a.astype(BF16), b.astype(BF16), (((ca,), (cb,)), ((), ())),
                           preferred_element_type=F32)


MM_TILES = (1024, 768, 512, 256, 128)
MM_K_TILES = (2048, 1536) + MM_TILES
MM_PAIR = 2
MM_WIDE = 1536


def _matmul(a, b, *, ta=False, tb=False, out_dtype=F32, name, b_block0=0, n_blocks=None,
            out_cb=None, out_into=None, out_block0=0, acc_in=None, rider=None):
    if ta:
        K, M = a.shape
    else:
        M, K = a.shape
    b_cb = None
    if b.ndim == 3:
        b_cb = b.shape[2]
        n_blocks = b.shape[0] - b_block0 if n_blocks is None else n_blocks
        N, K2 = (b.shape[1], n_blocks * b_cb) if tb else (n_blocks * b_cb, b.shape[1])
    elif tb:
        N, K2 = b.shape
    else:
        K2, N = b.shape
    assert K == K2, (a.shape, b.shape, ta, tb)
    if out_into is not None:
        out_cb = out_into.shape[2]
    tm = _pick(M, MM_TILES)
    pair = lambda cb_, count: MM_PAIR if (cb_ * MM_PAIR <= MM_WIDE and count % MM_PAIR == 0) else 1
    b_pair = pair(b_cb, n_blocks) if b_cb else 1
    o_pair = pair(out_cb, N // out_cb) if out_cb else 1
    if b_cb and not tb:
        tn = b_cb * b_pair
    elif out_cb:
        tn = out_cb * o_pair
    else:
        tn = _pick(N, MM_TILES)
    if b_cb and tb:
        tk = b_cb * b_pair
    else:
        tk = _pick(K, MM_TILES if tn > MM_TILES[0] else MM_K_TILES)
    nk = K // tk
    ca, cb = (0 if ta else 1), (1 if tb else 0)
    has_acc = acc_in is not None
    has_into = out_into is not None

    def body(*refs):
        a_ref, b_ref = refs[0], refs[1]
        pos = 2
        c_ref = None
        if has_acc:
            c_ref = refs[pos]
            pos += 1
        if has_into:
            pos += 1
        o_ref = refs[pos]
        b_tile = b_ref[...] if b_cb is None else jnp.concatenate([b_ref[t] for t in range(b_pair)], axis=1)
        p = _dot(a_ref[...], b_tile, ca, cb)

        def write(val):
            val = val.astype(out_dtype)
            if out_cb is None:
                o_ref[...] = val
            else:
                for t in range(o_pair):
                    o_ref[t] = val[:, t * out_cb:(t + 1) * out_cb]

        if nk == 1:
            write((p + c_ref[...]) if has_acc else p)
        else:
            acc_ref = refs[pos + 1]
            k = pl.program_id(2)

            @pl.when(k == 0)
            def _():
                acc_ref[...] = (p + c_ref[...]) if has_acc else p

            @pl.when(k > 0)
            def _():
                acc_ref[...] += p

            @pl.when(k == nk - 1)
            def _():
                write(acc_ref[...])

    nj, ni = N // tn, M // tm
    a_bytes, b_bytes = a.size * a.dtype.itemsize, K * N * b.dtype.itemsize
    n_outer = a_bytes * nj + b_bytes * (1 if nk == 1 else ni) <= a_bytes * (1 if nk == 1 else nj) + b_bytes * ni
    grid = (nj, ni, nk) if n_outer else (ni, nj, nk)

    def spec(block, index):
        return pl.BlockSpec(block, (lambda g0, g1, k: index(g0, g1, k)) if n_outer else (lambda g0, g1, k: index(g1, g0, k)))

    a_spec = spec((tk, tm), lambda j, i, k: (k, i)) if ta else spec((tm, tk), lambda j, i, k: (i, k))
    if b_cb is None:
        b_spec = spec((tn, tk), lambda j, i, k: (j, k)) if tb else spec((tk, tn), lambda j, i, k: (k, j))
    elif tb:
        b_spec = spec((b_pair, tn, b_cb), lambda j, i, k: (b_block0 // b_pair + k, j, 0))
    else:
        b_spec = spec((b_pair, tk, b_cb), lambda j, i, k: (b_block0 // b_pair + j, k, 0))
    in_specs = [a_spec, b_spec]
    operands = [a, b]
    aliases = {}
    if has_acc:
        in_specs.append(spec((tm, tn), lambda j, i, k: (i, j)))
        operands.append(acc_in)
    if has_into:
        aliases = {len(operands): 0}
        in_specs.append(pl.BlockSpec(memory_space=pl.ANY))
        operands.append(out_into)
    if out_cb is None:
        out_shape = jax.ShapeDtypeStruct((M, N), out_dtype)
        out_spec = spec((tm, tn), lambda j, i, k: (i, j))
    else:
        out_shape = (jax.ShapeDtypeStruct(out_into.shape, out_into.dtype) if has_into
                     else jax.ShapeDtypeStruct((N // out_cb, M, out_cb), out_dtype))
        out_spec = spec((o_pair, tm, out_cb), lambda j, i, k: (out_block0 // o_pair + j, i, 0))
    if rider is not None:
        assert not has_into
        (out,), brought = _call(body, name=name, rider=rider, grid=grid, in_specs=in_specs,
                                out_specs=(out_spec,), out_shape=(out_shape,), operands=operands,
                                scratch_shapes=[] if nk == 1 else [pltpu.VMEM((tm, tn), F32)])
        return out, brought
    return pl.pallas_call(
        body, name=name, out_shape=out_shape,
        grid=grid,
        in_specs=in_specs, out_specs=out_spec, input_output_aliases=aliases,
        scratch_shapes=[] if nk == 1 else [pltpu.VMEM((tm, tn), F32)],
        compiler_params=_cparams("parallel", "parallel", "arbitrary"),
    )(*operands)


def _rms(x, g):
    r = lax.rsqrt(jnp.mean(x * x, axis=-1, keepdims=True) + RMS_EPS)
    return x * r * g


def _rms_bwd(dy, x, g):
    r = lax.rsqrt(jnp.mean(x * x, axis=-1, keepdims=True) + RMS_EPS)
    xh = x * r
    dxh = dy * g
    dx = r * (dxh - xh * jnp.mean(dxh * xh, axis=-1, keepdims=True))
    dg = jnp.sum(dy * xh, axis=0, keepdims=True)
    return dx, dg


def _row_tile(rows):
    return _pick(rows, (512, 256, 128, 64, 32, 16, 8))


def _rms_fwd(x, g, *, name, rider=None):
    R, D = x.shape
    tr = _row_tile(R)

    def body(x_ref, g_ref, h_ref):
        h_ref[...] = _rms(x_ref[...], g_ref[...]).astype(BF16)

    (h,), brought = _call(
        body, name=name, rider=rider, out_shape=(jax.ShapeDtypeStruct((R, D), BF16),), grid=(R // tr,),
        in_specs=[pl.BlockSpec((tr, D), lambda i: (i, 0)), pl.BlockSpec((1, D), lambda i: (0, 0))],
        out_specs=(pl.BlockSpec((tr, D), lambda i: (i, 0)),), scratch_shapes=[], operands=(x, g))
    return h if rider is None else (h, brought)


def _resnorm_norm(x, z, g_post, g_next, *, name):
    R, D = x.shape
    tr = _row_tile(R)

    def body(x_ref, z_ref, gp_ref, gn_ref, xn_ref, h_ref):
        xn = x_ref[...] + _rms(z_ref[...], gp_ref[...])
        xn_ref[...] = xn
        h_ref[...] = _rms(xn, gn_ref[...]).astype(BF16)

    row = pl.BlockSpec((tr, D), lambda i: (i, 0))
    vec = pl.BlockSpec((1, D), lambda i: (0, 0))
    return pl.pallas_call(
        body, name=name,
        out_shape=(jax.ShapeDtypeStruct((R, D), F32), jax.ShapeDtypeStruct((R, D), BF16)),
        grid=(R // tr,), in_specs=[row, row, vec, vec], out_specs=(row, row),
        compiler_params=_cparams("parallel"),
    )(x, z, g_post, g_next)


def _final_loss(x, z, g_post, target, *, name):
    R, D = x.shape
    tr = _row_tile(R)

    def body(x_ref, z_ref, gp_ref, t_ref, loss_ref, dy_ref, dz_ref, dg_ref):
        i = pl.program_id(0)
        z = z_ref[...]
        g = gp_ref[...]
        err = x_ref[...] + _rms(z, g) - t_ref[...]
        dy = err * (1.0 / D)
        dy_ref[...] = dy
        dz, dg = _rms_bwd(dy, z, g)
        dz_ref[...] = dz.astype(BF16)
        part = 0.5 * jnp.sum(jnp.sum(err * err, axis=-1, keepdims=True) * (1.0 / D), axis=0, keepdims=True)

        @pl.when(i == 0)
        def _():
            loss_ref[...] = part
            dg_ref[...] = dg

        @pl.when(i > 0)
        def _():
            loss_ref[...] += part
            dg_ref[...] += dg

    row = pl.BlockSpec((tr, D), lambda i: (i, 0))
    vec = pl.BlockSpec((1, D), lambda i: (0, 0))
    return pl.pallas_call(
        body, name=name,
        out_shape=(jax.ShapeDtypeStruct((1, 1), F32), jax.ShapeDtypeStruct((R, D), F32),
                   jax.ShapeDtypeStruct((R, D), BF16), jax.ShapeDtypeStruct((1, D), F32)),
        grid=(R // tr,), in_specs=[row, row, vec, row],
        out_specs=(pl.BlockSpec((1, 1), lambda i: (0, 0)), row, row, vec),
        compiler_params=_cparams("arbitrary"),
    )(x, z, g_post, target)


def _norm_bwd_pair(dres, dh, xk, g_pre, zprev, g_prev_post, *, name, rider=None):
    R, D = xk.shape
    tr = _row_tile(R)

    def body(dres_ref, dh_ref, x_ref, gpre_ref, z_ref, gpost_ref, dx_ref, dz_ref, dgpre_ref, dgpost_ref):
        i = pl.program_id(0)
        d1, dgpre = _rms_bwd(dh_ref[...], x_ref[...], gpre_ref[...])
        dx = dres_ref[...] + d1
        dx_ref[...] = dx
        dz, dgpost = _rms_bwd(dx, z_ref[...], gpost_ref[...])
        dz_ref[...] = dz.astype(BF16)

        @pl.when(i == 0)
        def _():
            dgpre_ref[...] = dgpre
            dgpost_ref[...] = dgpost

        @pl.when(i > 0)
        def _():
            dgpre_ref[...] += dgpre
            dgpost_ref[...] += dgpost

    row = pl.BlockSpec((tr, D), lambda i: (i, 0))
    vec = pl.BlockSpec((1, D), lambda i: (0, 0))
    return _call(
        body, name=name, rider=rider,
        out_shape=(jax.ShapeDtypeStruct((R, D), F32), jax.ShapeDtypeStruct((R, D), BF16),
                   jax.ShapeDtypeStruct((1, D), F32), jax.ShapeDtypeStruct((1, D), F32)),
        grid=(R // tr,), in_specs=[row, row, row, vec, row, vec], out_specs=(row, row, vec, vec),
        scratch_shapes=[], operands=(dres, dh, xk, g_pre, zprev, g_prev_post))


def _norm_bwd_single(dres, dh, xk, g_pre, *, name, rider=None):
    R, D = xk.shape
    tr = _row_tile(R)
    has_res = dres is not None

    def body(*refs):
        if has_res:
            dres_ref, dh_ref, x_ref, gpre_ref, dx_ref, dgpre_ref = refs
        else:
            dh_ref, x_ref, gpre_ref, dx_ref, dgpre_ref = refs
        i = pl.program_id(0)
        d1, dgpre = _rms_bwd(dh_ref[...], x_ref[...], gpre_ref[...])
        dx_ref[...] = dres_ref[...] + d1 if has_res else d1

        @pl.when(i == 0)
        def _():
            dgpre_ref[...] = dgpre

        @pl.when(i > 0)
        def _():
            dgpre_ref[...] += dgpre

    row = pl.BlockSpec((tr, D), lambda i: (i, 0))
    vec = pl.BlockSpec((1, D), lambda i: (0, 0))
    ins = ([dres] if has_res else []) + [dh, xk, g_pre]
    res, brought = _call(
        body, name=name, rider=rider,
        out_shape=(jax.ShapeDtypeStruct((R, D), F32), jax.ShapeDtypeStruct((1, D), F32)),
        grid=(R // tr,), in_specs=([row] if has_res else []) + [row, row, vec], out_specs=(row, vec),
        scratch_shapes=[], operands=ins)
    return res if rider is None else (res, brought)


SB_BLOCK = 256
SB_QBLOCK = 512
SB_DEAD = -104.0


def _sb_tri(kind):
    r = lax.broadcasted_iota(jnp.int32, (SB_BLOCK, SB_BLOCK), 0)
    c = lax.broadcasted_iota(jnp.int32, (SB_BLOCK, SB_BLOCK), 1)
    keep = {"after": r > c, "before": r < c}[kind]
    return jnp.where(keep, 1.0, 0.0).astype(BF16)


def _sb_scores(qm, k_blk):
    z = _dot(qm, k_blk, 1, 1)
    sp = jnp.maximum(z, 0.0) + jnp.log(1.0 + jnp.exp(-jnp.abs(z)))
    return z, sp


def _sb_causal(rows):
    r = lax.broadcasted_iota(jnp.int32, (rows, SB_BLOCK), 0)
    c = lax.broadcasted_iota(jnp.int32, (rows, SB_BLOCK), 1)
    return c < r


def _head_masks():
    lane = lax.broadcasted_iota(jnp.int32, (1, LANES), 1)
    return [jnp.where(lane < SB_HEAD_DIM, 1.0, 0.0), jnp.where(lane >= SB_HEAD_DIM, 1.0, 0.0)]


def _sb_fwd(proj, *, name, rider=None):
    S = proj.shape[0]
    T = SB_BLOCK
    TQ = min(SB_QBLOCK, S)
    span = TQ // T
    nq = S // TQ
    npair = SB_WIDTH // LANES
    scale = SB_HEAD_DIM ** -0.5

    def body(q_ref, k_ref, v_ref, o_ref, tot_ref, first_ref, acc_ref, run_ref):
        masks = _head_masks()
        tri = _sb_tri("after")
        first_ref[...] = jnp.zeros_like(first_ref)
        slot = lax.broadcasted_iota(jnp.int32, first_ref.shape, 1)

        def alive():
            reach = jnp.maximum(jnp.max(run_ref[0]), jnp.max(run_ref[1]))
            return (reach > SB_DEAD).astype(jnp.int32)

        def q_block(i, _):
            qrow = pl.ds(pl.multiple_of(i * TQ, TQ), TQ)
            q = q_ref[qrow, :] * scale
            qm = [(q * m).astype(BF16) for m in masks]
            acc_ref[...] = jnp.zeros_like(acc_ref)
            run_ref[...] = jnp.zeros_like(run_ref)

            def k_block(j, own):
                krow = pl.ds(pl.multiple_of(j * T, T), T)
                k_blk = k_ref[krow, :].astype(BF16)
                v_blk = v_ref[krow, :].astype(BF16)
                r0 = 0 if own is None else own * T
                rows = pl.ds(r0, TQ - r0)
                for h in range(2):
                    z, sp = _sb_scores(qm[h][r0:], k_blk)
                    causal = None if own is None else _sb_causal(TQ - r0)
                    lf = -sp if causal is None else jnp.where(causal, -sp, 0.0)
                    e = jnp.exp(z - sp + _dot(lf, tri, 1, 0) + run_ref[h, rows])
                    w = e if causal is None else jnp.where(causal, e, 0.0)
                    acc_ref[h, rows] += _dot(w, v_blk, 1, 0)
                    run_ref[h, rows] += jnp.sum(lf, axis=1, keepdims=True)

            for d in reversed(range(span)):
                k_block(i * span + d, d)

            def below(carry):
                jj, _ = carry
                k_block(i * span - 1 - jj, None)
                return jj + 1, alive()

            done, _ = lax.while_loop(lambda c: jnp.logical_and(c[0] < i * span, c[1] > 0), below, (jnp.int32(0), alive()))
            o_ref[qrow, :] = (acc_ref[0] * masks[0] + acc_ref[1] * masks[1]).astype(BF16)
            tot_ref[qrow, :] = run_ref[0] * masks[0] + run_ref[1] * masks[1]
            first_ref[...] = jnp.where(slot == i, (i * span - done).astype(F32), first_ref[...])
            return 0

        lax.fori_loop(0, nq, q_block, 0)

    blk = lambda off: pl.BlockSpec((S, LANES), lambda p: (0, off + p))
    return _call(
        body, name=name, rider=rider,
        out_shape=(jax.ShapeDtypeStruct((S, SB_WIDTH), BF16), jax.ShapeDtypeStruct((S, SB_WIDTH), F32),
                   jax.ShapeDtypeStruct((npair, SUBLANES, LANES), F32)),
        grid=(npair,),
        in_specs=[blk(0), blk(npair), blk(2 * npair)],
        out_specs=(blk(0), blk(0), pl.BlockSpec((1, SUBLANES, LANES), lambda p: (p, 0, 0))),
        scratch_shapes=[pltpu.VMEM((2, TQ, LANES), F32), pltpu.VMEM((2, TQ, 1), F32)],
        operands=(proj, proj, proj))


def _sb_bwd(proj, tot, first, do_attn, *, name, rider=None):
    S = proj.shape[0]
    T = SB_BLOCK
    TQ = min(SB_QBLOCK, S)
    span = TQ // T
    nq = S // TQ
    npair = SB_WIDTH // LANES
    scale = SB_HEAD_DIM ** -0.5

    def body(q_ref, k_ref, v_ref, tot_ref, first_ref, do_ref, dq_ref, dk_ref, dv_ref,
             dqacc_ref, dkacc_ref, dvacc_ref, run_ref, grun_ref):
        masks = _head_masks()
        tri_after = _sb_tri("after")
        tri_before = _sb_tri("before")
        dkacc_ref[...] = jnp.zeros_like(dkacc_ref)
        dvacc_ref[...] = jnp.zeros_like(dvacc_ref)
        slot = lax.broadcasted_iota(jnp.int32, first_ref.shape, 1)

        def q_block(i, _):
            qrow = pl.ds(pl.multiple_of(i * TQ, TQ), TQ)
            q = q_ref[qrow, :] * scale
            do = do_ref[qrow, :].astype(F32)
            tot = tot_ref[qrow, :]
            qm = [(q * m).astype(BF16) for m in masks]
            dom = [(do * m).astype(BF16) for m in masks]
            ltot = [jnp.sum(tot * m, axis=1, keepdims=True) * (1.0 / SB_HEAD_DIM) for m in masks]
            dqacc_ref[...] = jnp.zeros_like(dqacc_ref)
            run_ref[...] = jnp.zeros_like(run_ref)
            grun_ref[...] = jnp.zeros_like(grun_ref)

            def k_block(j, own):
                krow = pl.ds(pl.multiple_of(j * T, T), T)
                k_blk = k_ref[krow, :].astype(BF16)
                v_blk = v_ref[krow, :].astype(BF16)
                r0 = 0 if own is None else own * T
                rows = pl.ds(r0, TQ - r0)
                for h in range(2):
                    z, sp = _sb_scores(qm[h][r0:], k_blk)
                    causal = None if own is None else _sb_causal(TQ - r0)
                    lf = -sp if causal is None else jnp.where(causal, -sp, 0.0)
                    lsum = jnp.sum(lf, axis=1, keepdims=True)
                    later = (ltot[h][r0:] - run_ref[h, rows] - lsum) + _dot(lf, tri_after, 1, 0)
                    beta = jnp.exp(z - sp)
                    w = jnp.exp(z - sp + later)
                    if causal is not None:
                        w = jnp.where(causal, w, 0.0)
                    g = _dot(dom[h][r0:], v_blk, 1, 1) * w
                    gbefore = grun_ref[h, rows] + _dot(g, tri_before, 1, 0)
                    dz = g - beta * (g + gbefore)
                    if causal is not None:
                        dz = jnp.where(causal, dz, 0.0)
                    dz = dz.astype(BF16)
                    dqacc_ref[h, rows] += _dot(dz, k_blk, 1, 0)
                    dkacc_ref[krow, :] += _dot(dz, qm[h][r0:], 0, 0)
                    dvacc_ref[krow, :] += _dot(w, dom[h][r0:], 0, 0)
                    run_ref[h, rows] += lsum
                    grun_ref[h, rows] += jnp.sum(g, axis=1, keepdims=True)

            def above(j, _):
                k_block(j, None)
                return 0

            first = jnp.max(jnp.where(slot == i, first_ref[...], 0.0)).astype(jnp.int32)
            lax.fori_loop(jnp.clip(first, 0, i * span), i * span, above, 0)
            for d in range(span):
                k_block(i * span + d, d)
            dq_ref[qrow, :] = ((dqacc_ref[0] * masks[0] + dqacc_ref[1] * masks[1]) * scale).astype(BF16)
            return 0

        lax.fori_loop(0, nq, q_block, 0)
        dk_ref[...] = dkacc_ref[...].astype(BF16)
        dv_ref[...] = dvacc_ref[...].astype(BF16)

    blk = lambda off: pl.BlockSpec((S, LANES), lambda p: (0, off + p))
    out = jax.ShapeDtypeStruct((S, SB_WIDTH), BF16)
    return _call(
        body, name=name, rider=rider, out_shape=(out, out, out), grid=(npair,),
        in_specs=[blk(0), blk(npair), blk(2 * npair), blk(0), pl.BlockSpec((1, SUBLANES, LANES), lambda p: (p, 0, 0)),
                  blk(0)],
        out_specs=(blk(0), blk(0), blk(0)),
        scratch_shapes=[pltpu.VMEM((2, TQ, LANES), F32), pltpu.VMEM((S, LANES), F32), pltpu.VMEM((S, LANES), F32),
                        pltpu.VMEM((2, TQ, 1), F32), pltpu.VMEM((2, TQ, 1), F32)],
        operands=(proj, proj, proj, tot, first, do_attn))


SSM_HALVES = 2
SSM_HALF_CH = SSM_WIDTH // SSM_HALVES
SSM_HALF_ST = SSM_GROUPS * SSM_STATE // SSM_HALVES
SSM_CHUNK = 512


def _cmul(ar, ai, br, bi):
    return ar * br - ai * bi, ar * bi + ai * br


def _ssm_tables(lam_re, lam_im):
    lr = lam_re.reshape(-1)
    li = lam_im.reshape(-1)
    pows = [(jnp.ones_like(lr), jnp.zeros_like(li)), (lr, li)]
    for _ in range(2, SUBLANES + 1):
        pows.append(_cmul(pows[-1][0], pows[-1][1], lr, li))
    row = jnp.arange(SUBLANES)[:, None]

    def shift_tab(d, keep):
        return [jnp.where(keep, pows[d][0][None, :], 0.0), jnp.where(keep, pows[d][1][None, :], 0.0)]

    fwd, bwd = [], []
    for d in (1, 2, 4):
        fwd += shift_tab(d, row >= d)
        bwd += shift_tab(d, row + d < SUBLANES)
    fwd += [jnp.stack([pows[r + 1][0] for r in range(SUBLANES)]), jnp.stack([pows[r + 1][1] for r in range(SUBLANES)])]
    bwd += [jnp.stack([pows[SUBLANES - r][0] for r in range(SUBLANES)]),
            jnp.stack([pows[SUBLANES - r][1] for r in range(SUBLANES)])]

    def halves(tabs):
        t = jnp.stack(tabs)
        return t.reshape(8, SUBLANES, SSM_HALVES, SSM_HALF_ST).transpose(2, 0, 1, 3)

    return halves(fwd), halves(bwd)


def _ssm_fwd(proj, bd_re, bd_im, cd_re, cd_imneg, d_skip, tab, *, name, rider=None):
    S = proj.shape[0]
    Tc = min(SSM_CHUNK, S)
    nc = S // Tc
    u_blk0 = (3 * SB_WIDTH) // SSM_HALF_CH

    def body(u_ref, bre_ref, bim_ref, cre_ref, cim_ref, d_ref, tab_ref, y_ref, xre_ref, xim_ref, cre_s, cim_s):
        c = pl.program_id(1)

        @pl.when(c == 0)
        def _():
            cre_s[...] = jnp.zeros_like(cre_s)
            cim_s[...] = jnp.zeros_like(cim_s)

        u = u_ref[...]
        ub = u.astype(BF16)
        xre_ref[...] = _dot(ub, bre_ref[0], 1, 0)
        xim_ref[...] = _dot(ub, bim_ref[0], 1, 0)

        def slab(k, carry):
            car_re, car_im = carry
            rows = pl.ds(pl.multiple_of(k * SUBLANES, SUBLANES), SUBLANES)
            sre = xre_ref[rows, :]
            sim = xim_ref[rows, :]
            for n, d in enumerate((1, 2, 4)):
                pre, pim = tab_ref[0, 2 * n], tab_ref[0, 2 * n + 1]
                rre = pltpu.roll(sre, d, 0)
                rim = pltpu.roll(sim, d, 0)
                sre, sim = sre + (pre * rre - pim * rim), sim + (pre * rim + pim * rre)
            pre, pim = tab_ref[0, 6], tab_ref[0, 7]
            sre, sim = sre + (pre * car_re - pim * car_im), sim + (pre * car_im + pim * car_re)
            xre_ref[rows, :] = sre
            xim_ref[rows, :] = sim
            last = (SUBLANES - 1, SUBLANES)
            return (jnp.broadcast_to(sre[last[0]:last[1], :], sre.shape),
                    jnp.broadcast_to(sim[last[0]:last[1], :], sim.shape))

        car = lax.fori_loop(0, Tc // SUBLANES, slab, (cre_s[...], cim_s[...]))
        cre_s[...] = car[0]
        cim_s[...] = car[1]
        y = _dot(xre_ref[...], cre_ref[0], 1, 0) + _dot(xim_ref[...], cim_ref[0], 1, 0)
        y_ref[...] = y + d_ref[...] * u

    return _call(
        body, name=name, rider=rider,
        out_shape=(jax.ShapeDtypeStruct((S, SSM_WIDTH), F32),
                   jax.ShapeDtypeStruct((S, SSM_HALVES * SSM_HALF_ST), F32),
                   jax.ShapeDtypeStruct((S, SSM_HALVES * SSM_HALF_ST), F32)),
        grid=(SSM_HALVES, nc),
        in_specs=[pl.BlockSpec((Tc, SSM_HALF_CH), lambda h, c: (c, u_blk0 + h)),
                  pl.BlockSpec((1, SSM_HALF_CH, SSM_HALF_ST), lambda h, c: (h, 0, 0)),
                  pl.BlockSpec((1, SSM_HALF_CH, SSM_HALF_ST), lambda h, c: (h, 0, 0)),
                  pl.BlockSpec((1, SSM_HALF_ST, SSM_HALF_CH), lambda h, c: (h, 0, 0)),
                  pl.BlockSpec((1, SSM_HALF_ST, SSM_HALF_CH), lambda h, c: (h, 0, 0)),
                  pl.BlockSpec((1, SSM_HALF_CH), lambda h, c: (0, h)),
                  pl.BlockSpec((1, 8, SUBLANES, SSM_HALF_ST), lambda h, c: (h, 0, 0, 0))],
        out_specs=(pl.BlockSpec((Tc, SSM_HALF_CH), lambda h, c: (c, h)),
                   pl.BlockSpec((Tc, SSM_HALF_ST), lambda h, c: (c, h)),
                   pl.BlockSpec((Tc, SSM_HALF_ST), lambda h, c: (c, h))),
        scratch_shapes=[pltpu.VMEM((SUBLANES, SSM_HALF_ST), F32), pltpu.VMEM((SUBLANES, SSM_HALF_ST), F32)],
        operands=(proj, bd_re, bd_im, cd_re, cd_imneg, d_skip, tab))


def _ssm_bwd(dy, proj, x_re, x_im, bd_re, bd_im, cd_re, cd_imneg, d_skip, tab, *, name, rider=None):
    S = proj.shape[0]
    Tc = min(SSM_CHUNK, S)
    nc = S // Tc
    u_blk0 = (3 * SB_WIDTH) // SSM_HALF_CH

    def body(dy_ref, u_ref, xre_ref, xim_ref, bre_ref, bim_ref, cre_ref, cim_ref, d_ref, tab_ref,
             du_ref, dbre_ref, dbim_ref, dcre_ref, dcim_ref, dd_ref, dlre_ref, dlim_ref,
             gre_s, gim_s, cre_s, cim_s):
        c = pl.program_id(1)

        @pl.when(c == 0)
        def _():
            cre_s[...] = jnp.zeros_like(cre_s)
            cim_s[...] = jnp.zeros_like(cim_s)
            dbre_ref[...] = jnp.zeros_like(dbre_ref)
            dbim_ref[...] = jnp.zeros_like(dbim_ref)
            dcre_ref[...] = jnp.zeros_like(dcre_ref)
            dcim_ref[...] = jnp.zeros_like(dcim_ref)
            dd_ref[...] = jnp.zeros_like(dd_ref)
            dlre_ref[...] = jnp.zeros_like(dlre_ref)
            dlim_ref[...] = jnp.zeros_like(dlim_ref)

        dy = dy_ref[...]
        dyb = dy.astype(BF16)
        u = u_ref[...]
        gre_s[...] = _dot(dyb, cre_ref[0], 1, 1)
        gim_s[...] = _dot(dyb, cim_ref[0], 1, 1)
        row = lax.broadcasted_iota(jnp.int32, (SUBLANES, SSM_HALF_ST), 0)
        nslab = Tc // SUBLANES

        def slab(kk, carry):
            car_re, car_im, acc_re, acc_im = carry
            k = nslab - 1 - kk
            rows = pl.ds(pl.multiple_of(k * SUBLANES, SUBLANES), SUBLANES)
            sre = gre_s[rows, :]
            sim = gim_s[rows, :]
            for n, d in enumerate((1, 2, 4)):
                pre, pim = tab_ref[0, 2 * n], tab_ref[0, 2 * n + 1]
                rre = pltpu.roll(sre, SUBLANES - d, 0)
                rim = pltpu.roll(sim, SUBLANES - d, 0)
                sre, sim = sre + (pre * rre + pim * rim), sim + (pre * rim - pim * rre)
            pre, pim = tab_ref[0, 6], tab_ref[0, 7]
            sre, sim = sre + (pre * car_re + pim * car_im), sim + (pre * car_im - pim * car_re)
            gre_s[rows, :] = sre
            gim_s[rows, :] = sim
            nre = jnp.where(row == SUBLANES - 1, car_re, pltpu.roll(sre, SUBLANES - 1, 0))
            nim = jnp.where(row == SUBLANES - 1, car_im, pltpu.roll(sim, SUBLANES - 1, 0))
            xr = xre_ref[rows, :]
            xi = xim_ref[rows, :]
            acc_re = acc_re + (nre * xr + nim * xi)
            acc_im = acc_im + (nim * xr - nre * xi)
            return (jnp.broadcast_to(sre[0:1, :], sre.shape), jnp.broadcast_to(sim[0:1, :], sim.shape), acc_re, acc_im)

        car = lax.fori_loop(0, nslab, slab, (cre_s[...], cim_s[...], dlre_ref[0], dlim_ref[0]))
        cre_s[...] = car[0]
        cim_s[...] = car[1]
        dlre_ref[0] = car[2]
        dlim_ref[0] = car[3]
        gre = gre_s[...].astype(BF16)
        gim = gim_s[...].astype(BF16)
        ub = u.astype(BF16)
        du = _dot(gre, bre_ref[0], 1, 1) + _dot(gim, bim_ref[0], 1, 1) + d_ref[...] * dy
        du_ref[...] = du.astype(BF16)
        dbre_ref[0] += _dot(ub, gre, 0, 0)
        dbim_ref[0] += _dot(ub, gim, 0, 0)
        dcre_ref[0] += _dot(xre_ref[...], dyb, 0, 0)
        dcim_ref[0] += _dot(xim_ref[...], dyb, 0, 0)
        dd_ref[...] += jnp.sum(dy * u, axis=0, keepdims=True)

    rev = lambda c: nc - 1 - c
    return _call(
        body, name=name, rider=rider,
        out_shape=(jax.ShapeDtypeStruct((S, SSM_WIDTH), BF16),
                   jax.ShapeDtypeStruct((SSM_HALVES, SSM_HALF_CH, SSM_HALF_ST), F32),
                   jax.ShapeDtypeStruct((SSM_HALVES, SSM_HALF_CH, SSM_HALF_ST), F32),
                   jax.ShapeDtypeStruct((SSM_HALVES, SSM_HALF_ST, SSM_HALF_CH), F32),
                   jax.ShapeDtypeStruct((SSM_HALVES, SSM_HALF_ST, SSM_HALF_CH), F32),
                   jax.ShapeDtypeStruct((1, SSM_WIDTH), F32),
                   jax.ShapeDtypeStruct((SSM_HALVES, SUBLANES, SSM_HALF_ST), F32),
                   jax.ShapeDtypeStruct((SSM_HALVES, SUBLANES, SSM_HALF_ST), F32)),
        grid=(SSM_HALVES, nc),
        in_specs=[pl.BlockSpec((Tc, SSM_HALF_CH), lambda h, c: (rev(c), h)),
                  pl.BlockSpec((Tc, SSM_HALF_CH), lambda h, c: (rev(c), u_blk0 + h)),
                  pl.BlockSpec((Tc, SSM_HALF_ST), lambda h, c: (rev(c), h)),
                  pl.BlockSpec((Tc, SSM_HALF_ST), lambda h, c: (rev(c), h)),
                  pl.BlockSpec((1, SSM_HALF_CH, SSM_HALF_ST), lambda h, c: (h, 0, 0)),
                  pl.BlockSpec((1, SSM_HALF_CH, SSM_HALF_ST), lambda h, c: (h, 0, 0)),
                  pl.BlockSpec((1, SSM_HALF_ST, SSM_HALF_CH), lambda h, c: (h, 0, 0)),
                  pl.BlockSpec((1, SSM_HALF_ST, SSM_HALF_CH), lambda h, c: (h, 0, 0)),
                  pl.BlockSpec((1, SSM_HALF_CH), lambda h, c: (0, h)),
                  pl.BlockSpec((1, 8, SUBLANES, SSM_HALF_ST), lambda h, c: (h, 0, 0, 0))],
        out_specs=(pl.BlockSpec((Tc, SSM_HALF_CH), lambda h, c: (rev(c), h)),
                   pl.BlockSpec((1, SSM_HALF_CH, SSM_HALF_ST), lambda h, c: (h, 0, 0)),
                   pl.BlockSpec((1, SSM_HALF_CH, SSM_HALF_ST), lambda h, c: (h, 0, 0)),
                   pl.BlockSpec((1, SSM_HALF_ST, SSM_HALF_CH), lambda h, c: (h, 0, 0)),
                   pl.BlockSpec((1, SSM_HALF_ST, SSM_HALF_CH), lambda h, c: (h, 0, 0)),
                   pl.BlockSpec((1, SSM_HALF_CH), lambda h, c: (0, h)),
                   pl.BlockSpec((1, SUBLANES, SSM_HALF_ST), lambda h, c: (h, 0, 0)),
                   pl.BlockSpec((1, SUBLANES, SSM_HALF_ST), lambda h, c: (h, 0, 0))),
        scratch_shapes=[pltpu.VMEM((Tc, SSM_HALF_ST), F32), pltpu.VMEM((Tc, SSM_HALF_ST), F32),
                        pltpu.VMEM((SUBLANES, SSM_HALF_ST), F32), pltpu.VMEM((SUBLANES, SSM_HALF_ST), F32)],
        operands=(dy, proj, x_re, x_im, bd_re, bd_im, cd_re, cd_imneg, d_skip, tab))


def _ssm_prepare(a_re, a_im, log_dt, b_re, b_im):
    dt = jnp.exp(log_dt)[:, None]
    mag = jnp.exp(a_re * dt)
    lre = mag * jnp.cos(a_im * dt)
    lim = mag * jnp.sin(a_im * dt)
    den = a_re * a_re + a_im * a_im
    fre = ((lre - 1.0) * a_re + lim * a_im) / den
    fim = (lim * a_re - (lre - 1.0) * a_im) / den
    bbre = fre[:, :, None] * b_re - fim[:, :, None] * b_im
    bbim = fre[:, :, None] * b_im + fim[:, :, None] * b_re
    return lre, lim, bbre, bbim


def _group_eye():
    return jnp.eye(SSM_GROUPS // SSM_HALVES, dtype=F32)


def _bd_from_bbar(bbar):
    gh = SSM_GROUPS // SSM_HALVES
    b = bbar.reshape(SSM_HALVES, gh, SSM_STATE, SSM_GROUP).transpose(0, 1, 3, 2)
    out = b[:, :, :, None, :] * _group_eye()[None, :, None, :, None]
    return out.reshape(SSM_HALVES, SSM_HALF_CH, SSM_HALF_ST)


def _bbar_from_bd(dbd):
    gh = SSM_GROUPS // SSM_HALVES
    d = dbd.reshape(SSM_HALVES, gh, SSM_GROUP, gh, SSM_STATE)
    d = jnp.sum(d * _group_eye()[None, :, None, :, None], axis=3)
    return d.transpose(0, 1, 3, 2).reshape(SSM_GROUPS, SSM_STATE, SSM_GROUP)


def _cd_from_c(cmat):
    gh = SSM_GROUPS // SSM_HALVES
    c = cmat.reshape(SSM_HALVES, gh, SSM_GROUP, SSM_STATE).transpose(0, 1, 3, 2)
    out = c[:, :, :, None, :] * _group_eye()[None, :, None, :, None]
    return out.reshape(SSM_HALVES, SSM_HALF_ST, SSM_HALF_CH)


def _c_from_cd(dcd):
    gh = SSM_GROUPS // SSM_HALVES
    d = dcd.reshape(SSM_HALVES, gh, SSM_STATE, gh, SSM_GROUP)
    d = jnp.sum(d * _group_eye()[None, :, None, :, None], axis=3)
    return d.transpose(0, 1, 3, 2).reshape(SSM_GROUPS, SSM_GROUP, SSM_STATE)


def _glu_fwd(y_pre, w_glu, b_glu, *, name):
    S, W = y_pre.shape
    tr = _row_tile(S)

    def body(y_ref, w_ref, b_ref, o_ref):
        yg = _gelu(y_ref[...])
        gl = _dot(yg, w_ref[...], 1, 0) + b_ref[...]
        o_ref[...] = (yg * _sigmoid(gl)).astype(BF16)

    row = pl.BlockSpec((tr, W), lambda i: (i, 0))
    return pl.pallas_call(
        body, name=name, out_shape=jax.ShapeDtypeStruct((S, W), BF16), grid=(S // tr,),
        in_specs=[row, pl.BlockSpec((W, W), lambda i: (0, 0)), pl.BlockSpec((1, W), lambda i: (0, 0))],
        out_specs=row, compiler_params=_cparams("parallel"),
    )(y_pre, w_glu, b_glu)


def _glu_bwd(y_pre, do, w_glu, b_glu, *, name):
    S, W = y_pre.shape
    tr = _row_tile(S)

    def body(y_ref, do_ref, w_ref, b_ref, dy_ref, dw_ref, db_ref):
        i = pl.program_id(0)
        yg, dyg_dy = _gelu_and_grad(y_ref[...])
        ygb = yg.astype(BF16)
        sg = _sigmoid(_dot(ygb, w_ref[...], 1, 0) + b_ref[...])
        do = do_ref[...]
        dgl = do * yg * sg * (1.0 - sg)
        dglb = dgl.astype(BF16)
        dyg = do * sg + _dot(dglb, w_ref[...], 1, 1)
        dy_ref[...] = dyg * dyg_dy
        dw = _dot(ygb, dglb, 0, 0)
        db = jnp.sum(dgl, axis=0, keepdims=True)

        @pl.when(i == 0)
        def _():
            dw_ref[...] = dw
            db_ref[...] = db

        @pl.when(i > 0)
        def _():
            dw_ref[...] += dw
            db_ref[...] += db

    row = pl.BlockSpec((tr, W), lambda i: (i, 0))
    full = pl.BlockSpec((W, W), lambda i: (0, 0))
    vec = pl.BlockSpec((1, W), lambda i: (0, 0))
    return pl.pallas_call(
        body, name=name,
        out_shape=(jax.ShapeDtypeStruct((S, W), F32), jax.ShapeDtypeStruct((W, W), F32), jax.ShapeDtypeStruct((1, W), F32)),
        grid=(S // tr,), in_specs=[row, row, full, vec], out_specs=(row, full, vec),
        compiler_params=_cparams("arbitrary"),
    )(y_pre, do, w_glu, b_glu)


GATE_COL0 = 3 * SB_WIDTH + SSM_WIDTH


def _merge_fwd(proj, o_attn, o_ssm, w_ba, w_bs, b_gate, *, name, rider=None):
    S = proj.shape[0]
    D = D_MODEL
    tr = _pick(S, (256, 128, 64, 32, 16, 8))
    gb = GATE_COL0 // D

    def body(ga_ref, gs_ref, oa_ref, os_ref, wa_ref, ws_ref, ba_ref, bs_ref, m_ref):
        pa = _dot(oa_ref[...], wa_ref[...], 1, 0)
        ps = _dot(os_ref[...], ws_ref[...], 1, 0)
        sa = _sigmoid(ga_ref[...] + ba_ref[...])
        ss = _sigmoid(gs_ref[...] + bs_ref[...])
        m_ref[...] = (sa * pa + ss * ps).astype(BF16)

    (merged,), brought = _call(
        body, name=name, rider=rider, out_shape=(jax.ShapeDtypeStruct((S, D), BF16),), grid=(S // tr,),
        in_specs=[pl.BlockSpec((tr, D), lambda i: (i, gb)), pl.BlockSpec((tr, D), lambda i: (i, gb + 1)),
                  pl.BlockSpec((tr, SB_WIDTH), lambda i: (i, 0)), pl.BlockSpec((tr, SSM_WIDTH), lambda i: (i, 0)),
                  pl.BlockSpec((SB_WIDTH, D), lambda i: (0, 0)), pl.BlockSpec((SSM_WIDTH, D), lambda i: (0, 0)),
                  pl.BlockSpec((1, D), lambda i: (0, 0)), pl.BlockSpec((1, D), lambda i: (0, 1))],
        out_specs=(pl.BlockSpec((tr, D), lambda i: (i, 0)),), scratch_shapes=[],
        operands=(proj, proj, o_attn, o_ssm, w_ba, w_bs, b_gate, b_gate))
    return merged if rider is None else (merged, brought)


def _merge_bwd(dmerged, proj, o_attn, o_ssm, w_ba, w_bs, b_gate, *, name):
    S = proj.shape[0]
    D = D_MODEL
    tr = _pick(S, (256, 128, 64, 32, 16, 8))
    gb = GATE_COL0 // D

    def body(dm_ref, ga_ref, gs_ref, oa_ref, os_ref, wa_ref, ws_ref, ba_ref, bs_ref,
             doa_ref, dos_ref, dg_ref, db_ref, dwa_ref, dws_ref):
        i = pl.program_id(0)
        dm = dm_ref[...]
        oa = oa_ref[...]
        osm = os_ref[...]
        pa = _dot(oa, wa_ref[...], 1, 0)
        ps = _dot(osm, ws_ref[...], 1, 0)
        sa = _sigmoid(ga_ref[...] + ba_ref[...])
        ss = _sigmoid(gs_ref[...] + bs_ref[...])
        dpa = (dm * sa).astype(BF16)
        dps = (dm * ss).astype(BF16)
        dga = dm * pa * sa * (1.0 - sa)
        dgs = dm * ps * ss * (1.0 - ss)
        dg_ref[:, :D] = dga.astype(BF16)
        dg_ref[:, D:] = dgs.astype(BF16)
        doa_ref[...] = _dot(dpa, wa_ref[...], 1, 1).astype(BF16)
        dos_ref[...] = _dot(dps, ws_ref[...], 1, 1)
        dwa = _dot(oa, dpa, 0, 0)
        dws = _dot(osm, dps, 0, 0)
        dba = jnp.sum(dga, axis=0, keepdims=True)
        dbs = jnp.sum(dgs, axis=0, keepdims=True)

        @pl.when(i == 0)
        def _():
            dwa_ref[...] = dwa
            dws_ref[...] = dws
            db_ref[:, :D] = dba
            db_ref[:, D:] = dbs

        @pl.when(i > 0)
        def _():
            dwa_ref[...] += dwa
            dws_ref[...] += dws
            db_ref[:, :D] += dba
            db_ref[:, D:] += dbs

    rowD = pl.BlockSpec((tr, D), lambda i: (i, 0))
    wspec = pl.BlockSpec((SB_WIDTH, D), lambda i: (0, 0))
    return pl.pallas_call(
        body, name=name,
        out_shape=(jax.ShapeDtypeStruct((S, SB_WIDTH), BF16), jax.ShapeDtypeStruct((S, SSM_WIDTH), F32),
                   jax.ShapeDtypeStruct((S, 2 * D), BF16), jax.ShapeDtypeStruct((1, 2 * D), F32),
                   jax.ShapeDtypeStruct((SB_WIDTH, D), F32), jax.ShapeDtypeStruct((SSM_WIDTH, D), F32)),
        grid=(S // tr,),
        in_specs=[rowD, pl.BlockSpec((tr, D), lambda i: (i, gb)), pl.BlockSpec((tr, D), lambda i: (i, gb + 1)),
                  pl.BlockSpec((tr, SB_WIDTH), lambda i: (i, 0)), pl.BlockSpec((tr, SSM_WIDTH), lambda i: (i, 0)),
                  wspec, wspec, pl.BlockSpec((1, D), lambda i: (0, 0)), pl.BlockSpec((1, D), lambda i: (0, 1))],
        out_specs=(pl.BlockSpec((tr, SB_WIDTH), lambda i: (i, 0)), pl.BlockSpec((tr, SSM_WIDTH), lambda i: (i, 0)),
                   pl.BlockSpec((tr, 2 * D), lambda i: (i, 0)), pl.BlockSpec((1, 2 * D), lambda i: (0, 0)),
                   wspec, wspec),
        compiler_params=_cparams("arbitrary"),
    )(dmerged, proj, proj, o_attn, o_ssm, w_ba, w_bs, b_gate, b_gate)


def _xattn_probs(q, k, h):
    cols = slice(h * XA_HEAD_DIM, (h + 1) * XA_HEAD_DIM)
    s = _dot(q[:, cols], k[:, cols], 1, 1) * (XA_HEAD_DIM ** -0.5)
    s = s - jnp.max(s, axis=-1, keepdims=True)
    e = jnp.exp(s)
    return e / jnp.sum(e, axis=-1, keepdims=True), cols


def _xattn_fwd(q2, k2, v2, *, name):
    S, D = q2.shape
    M = k2.shape[0]
    tr = _row_tile(S)

    def body(q_ref, k_ref, v_ref, o_ref):
        q = q_ref[...]
        k = k_ref[...]
        v = v_ref[...]
        for h in range(XA_HEADS):
            p, cols = _xattn_probs(q, k, h)
            o_ref[:, cols] = _dot(p, v[:, cols], 1, 0).astype(BF16)

    row = pl.BlockSpec((tr, D), lambda i: (i, 0))
    memb = pl.BlockSpec((M, D), lambda i: (0, 0))
    return pl.pallas_call(
        body, name=name, out_shape=jax.ShapeDtypeStruct((S, D), BF16), grid=(S // tr,),
        in_specs=[row, memb, memb], out_specs=row, compiler_params=_cparams("parallel"),
    )(q2, k2, v2)


def _xattn_bwd(q2, k2, v2, do2, *, name):
    S, D = q2.shape
    M = k2.shape[0]
    tr = _row_tile(S)
    scale = XA_HEAD_DIM ** -0.5

    def body(q_ref, k_ref, v_ref, do_ref, dq_ref, dk_ref, dv_ref):
        i = pl.program_id(0)

        @pl.when(i == 0)
        def _():
            dk_ref[...] = jnp.zeros_like(dk_ref)
            dv_ref[...] = jnp.zeros_like(dv_ref)

        q = q_ref[...]
        k = k_ref[...]
        v = v_ref[...]
        do = do_ref[...]
        for h in range(XA_HEADS):
            p, cols = _xattn_probs(q, k, h)
            dp = _dot(do[:, cols], v[:, cols], 1, 1)
            ds = (p * (dp - jnp.sum(dp * p, axis=-1, keepdims=True)) * scale).astype(BF16)
            dq_ref[:, cols] = _dot(ds, k[:, cols], 1, 0).astype(BF16)
            dk_ref[:, cols] += _dot(ds, q[:, cols], 0, 0)
            dv_ref[:, cols] += _dot(p, do[:, cols], 0, 0)

    row = pl.BlockSpec((tr, D), lambda i: (i, 0))
    memb = pl.BlockSpec((M, D), lambda i: (0, 0))
    return pl.pallas_call(
        body, name=name,
        out_shape=(jax.ShapeDtypeStruct((S, D), BF16), jax.ShapeDtypeStruct((M, D), F32), jax.ShapeDtypeStruct((M, D), F32)),
        grid=(S // tr,), in_specs=[row, memb, memb, row], out_specs=(row, memb, memb),
        compiler_params=_cparams("arbitrary"),
    )(q2, k2, v2, do2)


CONV_ROWS = 64
CONV_ROWS_FWD = 256


def _chunk(ref, c, rows):
    return ref[pl.ds(pl.multiple_of(c * rows, rows), rows), :]


def _rows_before(ref, c, rows):
    t0 = pl.multiple_of(jnp.maximum(c * rows - SUBLANES, 0), SUBLANES)
    return jnp.where(c > 0, ref[pl.ds(t0, SUBLANES), :], 0.0)


def _rows_after(ref, c, rows, n_chunks):
    t0 = pl.multiple_of(jnp.minimum((c + 1) * rows, n_chunks * rows - SUBLANES), SUBLANES)
    return jnp.where(c < n_chunks - 1, ref[pl.ds(t0, SUBLANES), :], 0.0)


def _shift_down(cur, before, d):
    out = pltpu.roll(cur, d, 0)
    r = lax.broadcasted_iota(jnp.int32, cur.shape, 0)
    for e in range(d):
        out = jnp.where(r == e, before[SUBLANES - d + e:SUBLANES - d + e + 1, :], out)
    return out


def _shift_up(cur, after, d):
    rows = cur.shape[0]
    out = pltpu.roll(cur, rows - d, 0)
    r = lax.broadcasted_iota(jnp.int32, cur.shape, 0)
    for e in range(d):
        out = jnp.where(r == rows - d + e, after[e:e + 1, :], out)
    return out


def _conv3(cur, before, w_ref, b_ref):
    return (w_ref[2:3, :] * cur + w_ref[1:2, :] * _shift_down(cur, before, 1)
            + w_ref[0:1, :] * _shift_down(cur, before, 2) + b_ref[...])


def _convgate_fwd(up_g, up_v, conv_w, conv_b, *, name):
    S, H = up_g.shape
    nb = H // LANES
    R = min(CONV_ROWS_FWD, S)
    n_chunks = S // R

    def body(g_ref, v_ref, wg_ref, wv_ref, bg_ref, bv_ref, a_ref):
        def chunk(c, _):
            cg = _conv3(_chunk(g_ref, c, R), _rows_before(g_ref, c, R), wg_ref, bg_ref)
            cv = _conv3(_chunk(v_ref, c, R), _rows_before(v_ref, c, R), wv_ref, bv_ref)
            a_ref[pl.ds(pl.multiple_of(c * R, R), R), :] = (_gelu(cg) * cv).astype(BF16)
            return 0

        lax.fori_loop(0, n_chunks, chunk, 0)

    col = lambda off: pl.BlockSpec((S, LANES), lambda j: (0, off + j))
    wcol = lambda off: pl.BlockSpec((3, LANES), lambda j: (0, off + j))
    bcol = lambda off: pl.BlockSpec((1, LANES), lambda j: (0, off + j))
    return pl.pallas_call(
        body, name=name, out_shape=jax.ShapeDtypeStruct((S, H), BF16), grid=(nb,),
        in_specs=[col(0), col(0), wcol(0), wcol(nb), bcol(0), bcol(nb)],
        out_specs=col(0), compiler_params=_cparams("parallel"),
    )(up_g, up_v, conv_w, conv_w, conv_b, conv_b)


def _convgate_bwd(up_g, up_v, da, conv_w, conv_b, *, name):
    S, H = up_g.shape
    nb = H // LANES
    R = min(CONV_ROWS, S)
    n_chunks = S // R

    def fold(a):
        return sum(a[r:r + SUBLANES] for r in range(0, a.shape[0], SUBLANES))

    def body(g_ref, v_ref, da_ref, wg_ref, wv_ref, bg_ref, bv_ref,
             dug_ref, duv_ref, dwg_ref, dwv_ref, dbg_ref, dbv_ref, dcg_s, dcv_s):
        def first_pass(c, acc):
            rows = pl.ds(pl.multiple_of(c * R, R), R)
            ug, uv = _chunk(g_ref, c, R), _chunk(v_ref, c, R)
            bg, bv = _rows_before(g_ref, c, R), _rows_before(v_ref, c, R)
            cg = _conv3(ug, bg, wg_ref, bg_ref)
            cv = _conv3(uv, bv, wv_ref, bv_ref)
            da = da_ref[rows, :]
            gl, dgl = _gelu_and_grad(cg)
            dcg = da * cv * dgl
            dcv = da * gl
            dcg_s[rows, :] = dcg
            dcv_s[rows, :] = dcv
            new = []
            for dc, u, before in ((dcg, ug, bg), (dcv, uv, bv)):
                new += [fold(dc * _shift_down(u, before, 2)), fold(dc * _shift_down(u, before, 1)), fold(dc * u), fold(dc)]
            return tuple(a + n for a, n in zip(acc, new))

        zero = jnp.zeros((SUBLANES, LANES), F32)
        acc = lax.fori_loop(0, n_chunks, first_pass, (zero,) * 8)
        total = [jnp.sum(a, axis=0, keepdims=True) for a in acc]
        for k, (dw_ref, db_ref) in enumerate(((dwg_ref, dbg_ref), (dwv_ref, dbv_ref))):
            dw_ref[0:1, :] = total[4 * k]
            dw_ref[1:2, :] = total[4 * k + 1]
            dw_ref[2:3, :] = total[4 * k + 2]
            db_ref[...] = total[4 * k + 3]

        def second_pass(c, _):
            rows = pl.ds(pl.multiple_of(c * R, R), R)
            for dc_s, w_ref, du_ref in ((dcg_s, wg_ref, dug_ref), (dcv_s, wv_ref, duv_ref)):
                cur, after = _chunk(dc_s, c, R), _rows_after(dc_s, c, R, n_chunks)
                du = w_ref[2:3, :] * cur + w_ref[1:2, :] * _shift_up(cur, after, 1) + w_ref[0:1, :] * _shift_up(cur, after, 2)
                du_ref[rows, :] = du.astype(BF16)
            return 0

        lax.fori_loop(0, n_chunks, second_pass, 0)

    col = lambda off: pl.BlockSpec((S, LANES), lambda j: (0, off + j))
    wcol = lambda off: pl.BlockSpec((3, LANES), lambda j: (0, off + j))
    bcol = lambda off: pl.BlockSpec((1, LANES), lambda j: (0, off + j))
    return pl.pallas_call(
        body, name=name,
        out_shape=(jax.ShapeDtypeStruct((S, H), BF16), jax.ShapeDtypeStruct((S, H), BF16),
                   jax.ShapeDtypeStruct((3, H), F32), jax.ShapeDtypeStruct((3, H), F32),
                   jax.ShapeDtypeStruct((1, H), F32), jax.ShapeDtypeStruct((1, H), F32)),
        grid=(nb,),
        in_specs=[col(0), col(0), col(0), wcol(0), wcol(nb), bcol(0), bcol(nb)],
        out_specs=(col(0), col(0), wcol(0), wcol(0), bcol(0), bcol(0)),
        scratch_shapes=[pltpu.VMEM((S, LANES), F32), pltpu.VMEM((S, LANES), F32)],
        compiler_params=_cparams("parallel"),
    )(up_g, up_v, da, conv_w, conv_w, conv_b, conv_b)


def _local_step(x, mem, target, w_in, late_wire, P, core):
    mm = _matmul
    h1, (w_in,) = _rms_fwd(x, P["norm_mix_pre"], name="rms_mix_pre", rider=_fill_xy([w_in]))
    w_in, = _fill_c([w_in]).run(name="gather_in_c")
    w_in = w_in.reshape((N_DEV,) + w_in.shape[2:])
    wire = dict(zip(LATE, late_wire))
    mixer, xattn, ffn = [n for n in LATE if n not in GATHER_XATTN + REDUCE_FFN], list(GATHER_XATTN), list(REDUCE_FFN)
    proj, got = mm(h1, w_in, name="mm_in", rider=_fill_xy([wire[n] for n in mixer]))
    wire.update(zip(mixer, got))
    (o_attn, sb_tot, sb_first), got = _sb_fwd(
        proj, name="sb_fwd", rider=_Exchange.join(_fill_c([wire[n] for n in mixer]), _fill_xy([wire[n] for n in ffn])))
    wire.update(zip(mixer + ffn, got))

    ssm_prep = lambda *a: _ssm_prepare(*a)
    (lam_re, lam_im, bb_re, bb_im), prep_vjp = jax.vjp(
        ssm_prep, P["ssm_a_re"], P["ssm_a_im"], P["ssm_log_dt"], P["ssm_b_re"], P["ssm_b_im"])
    tab_f, tab_b = _ssm_tables(lam_re, lam_im)
    bd_re = _bd_from_bbar(bb_re).astype(BF16)
    bd_im = _bd_from_bbar(bb_im).astype(BF16)
    cd_re = _cd_from_c(P["ssm_c_re"]).astype(BF16)
    cd_imneg = _cd_from_c(-P["ssm_c_im"]).astype(BF16)
    (y_pre, x_re, x_im), got = _ssm_fwd(
        proj, bd_re, bd_im, cd_re, cd_imneg, P["ssm_d"], tab_f, name="ssm_fwd",
        rider=_Exchange.join(_fill_c([wire[n] for n in ffn]), _fill_xy([wire[n] for n in xattn])))
    wire.update(zip(ffn + xattn, got))
    W = _weights_from_wire({n: wire[n] for n in mixer + ffn})
    W["w_in"] = w_in
    o_ssm = _glu_fwd(y_pre, W["ssm_w_glu"], P["ssm_b_glu"], name="glu_fwd")

    merged, got = _merge_fwd(proj, o_attn, o_ssm, W["w_branch_attn"], W["w_branch_ssm"], P["b_gate"], name="merge_fwd",
                             rider=_fill_c([wire[n] for n in xattn]))
    W.update(_weights_from_wire(dict(zip(xattn, got))))
    mo = mm(merged, W["w_out"], name="mm_out")
    x1, h2 = _resnorm_norm(x, mo, P["norm_mix_post"], P["norm_xa_pre"], name="resnorm_1")

    mem_n = _rms_fwd(mem, P["norm_mem"], name="rms_mem")
    q2 = mm(h2, W["xa_wq"], out_dtype=BF16, name="mm_xq")
    k2 = mm(mem_n, W["xa_wk"], out_dtype=BF16, name="mm_xk")
    v2 = mm(mem_n, W["xa_wv"], out_dtype=BF16, name="mm_xv")
    o2 = _xattn_fwd(q2, k2, v2, name="xattn_fwd")
    xa = mm(o2, W["xa_wo"], name="mm_xo")
    x2, h3 = _resnorm_norm(x1, xa, P["norm_xa_post"], P["norm_ffn_pre"], name="resnorm_2")

    half = N_DEV // 2
    up_g = mm(h3, W["ffn_w_up"], n_blocks=half, name="mm_up_g")
    up_v = mm(h3, W["ffn_w_up"], b_block0=half, name="mm_up_v")
    act = _convgate_fwd(up_g, up_v, W["ffn_conv_w"], P["ffn_conv_b"], name="convgate_fwd")
    f = mm(act, W["ffn_w_down"], name="mm_down")
    loss, dy, df, dg_ffn_post = _final_loss(x2, f, P["norm_ffn_post"], target, name="final_loss")

    G = {"norm_ffn_post": dg_ffn_post}
    dact = mm(df, W["ffn_w_down"], tb=True, name="mm_down_dx")
    G["ffn_w_down"] = mm(act, df, ta=True, name="mm_down_dw")
    dug, duv, dwg, dwv, dbg, dbv = _convgate_bwd(up_g, up_v, dact, W["ffn_conv_w"], P["ffn_conv_b"], name="convgate_bwd")
    G["ffn_conv_w"] = jnp.concatenate([dwg, dwv], axis=1)
    G["ffn_conv_b"] = jnp.concatenate([dbg, dbv], axis=1)
    dh3 = mm(dug, W["ffn_w_up"], tb=True, n_blocks=half, name="mm_up_g_dx")
    dh3 = mm(duv, W["ffn_w_up"], tb=True, b_block0=half, acc_in=dh3, name="mm_up_v_dx")
    dw_up = mm(h3, dug, ta=True, out_into=lax.empty(W["ffn_w_up"].shape, F32), name="mm_up_g_dw")
    G["ffn_w_up"] = mm(h3, duv, ta=True, out_into=dw_up, out_block0=half, name="mm_up_v_dw")
    blocks = {n: _grad_blocks(n, G[n]) for n in REDUCE_FFN}
    (dx2, dxa, G["norm_ffn_pre"], G["norm_xa_post"]), from_core = _norm_bwd_pair(
        dy, dh3, x2, P["norm_ffn_pre"], xa, P["norm_xa_post"], name="norm_bwd_3",
        rider=_send_c([blocks[n] for n in REDUCE_FFN]))
    pair = {n: _pair_sum(blocks[n], r, core, name="pair_sum_" + n) for n, r in zip(REDUCE_FFN, from_core)}

    G["xa_wo"] = mm(o2, dxa, ta=True, name="mm_xo_dw")
    do2 = mm(dxa, W["xa_wo"], tb=True, out_dtype=BF16, name="mm_xo_dx")
    dq2, dk2, dv2 = _xattn_bwd(q2, k2, v2, do2, name="xattn_bwd")
    G["xa_wq"] = mm(h2, dq2, ta=True, name="mm_xq_dw")
    dh2 = mm(dq2, W["xa_wq"], tb=True, name="mm_xq_dx")
    G["xa_wk"] = mm(mem_n, dk2, ta=True, name="mm_xk_dw")
    G["xa_wv"] = mm(mem_n, dv2, ta=True, name="mm_xv_dw")
    dmem_n = jnp.concatenate([dk2, dv2], axis=1)
    wkv = jnp.concatenate([W["xa_wk"], W["xa_wv"]], axis=1)
    dmem = mm(dmem_n, wkv, tb=True, name="mm_xkv_dx")
    _, G["norm_mem"] = _norm_bwd_single(None, dmem, mem, P["norm_mem"], name="norm_bwd_mem")
    (dx1, dmo, G["norm_xa_pre"], G["norm_mix_post"]), _ = _norm_bwd_pair(
        dx2, dh2, x1, P["norm_xa_pre"], mo, P["norm_mix_post"], name="norm_bwd_2")

    G["w_out"] = mm(merged, dmo, ta=True, name="mm_out_dw")
    dmerged = mm(dmo, W["w_out"], tb=True, name="mm_out_dx")
    do_attn, do_ssm, dgate, G["b_gate"], G["w_branch_attn"], G["w_branch_ssm"] = _merge_bwd(
        dmerged, proj, o_attn, o_ssm, W["w_branch_attn"], W["w_branch_ssm"], P["b_gate"], name="merge_bwd")
    dy_pre, G["ssm_w_glu"], G["ssm_b_glu"] = _glu_bwd(y_pre, do_ssm, W["ssm_w_glu"], P["ssm_b_glu"], name="glu_bwd")
    blocks.update({n: _grad_blocks(n, G[n]) for n in REDUCE_MID})
    (du, dbd_re, dbd_im, dcd_re, dcd_imneg, G["ssm_d"], dl_re, dl_im), brought = _ssm_bwd(
        dy_pre, proj, x_re, x_im, bd_re, bd_im, cd_re, cd_imneg, P["ssm_d"], tab_b, name="ssm_bwd",
        rider=_Exchange.join(_send_c([blocks[n] for n in REDUCE_MID]), _scatter_xy([pair[n] for n in REDUCE_FFN])))
    from_core, from_chips = brought[:len(REDUCE_MID)], brought[len(REDUCE_MID):]
    reduced = {n: (pair[n], parts) for n, parts in zip(REDUCE_FFN, from_chips)}
    pair.update({n: _pair_sum(blocks[n], r, core, name="pair_sum_" + n) for n, r in zip(REDUCE_MID, from_core)})
    G["ssm_c_re"] = _c_from_cd(dcd_re)
    G["ssm_c_im"] = -_c_from_cd(dcd_imneg)
    dlam_re = jnp.sum(dl_re, axis=1).reshape(SSM_GROUPS, SSM_STATE)
    dlam_im = jnp.sum(dl_im, axis=1).reshape(SSM_GROUPS, SSM_STATE)
    (G["ssm_a_re"], G["ssm_a_im"], G["ssm_log_dt"], G["ssm_b_re"], G["ssm_b_im"]) = prep_vjp(
        (dlam_re, dlam_im, _bbar_from_bd(dbd_re), _bbar_from_bd(dbd_im)))
    G["ffn_conv_b"] = G["ffn_conv_b"].reshape(N_DEV, FF_LOCAL_PAD)[:, :FF_LOCAL]
    small = [G[n].reshape(SMALL_SHAPE[n]) for n in SMALL_EARLY]
    (dq, dk, dv), brought = _sb_bwd(
        proj, sb_tot, sb_first, do_attn, name="sb_bwd",
        rider=_Exchange.join(_scatter_xy([pair[n] for n in REDUCE_MID]), _gather_xy_from(small)))
    from_chips, small = brought[:len(REDUCE_MID)], brought[len(REDUCE_MID):]
    reduced.update({n: (pair[n], parts) for n, parts in zip(REDUCE_MID, from_chips)})
    dproj = jnp.concatenate([dq, dk, dv, du, dgate], axis=1)
    G["w_in"], small = mm(h1, dproj, ta=True, out_cb=W["w_in"].shape[2], name="mm_in_dw", rider=_fill_c(small))
    g_in = _grad_blocks("w_in", G["w_in"])
    dh1, (from_core,) = mm(dproj, W["w_in"], tb=True, name="mm_in_dx", rider=_send_c([g_in]))
    pair_in = _pair_sum(g_in, from_core, core, name="pair_sum_w_in")
    (grad_x, dg_pre), (from_chips,) = _norm_bwd_single(dx1, dh1, x, P["norm_mix_pre"], name="norm_bwd_1",
                                                       rider=_scatter_xy([pair_in]))
    reduced["w_in"] = (pair_in, from_chips)
    last, = _gather_all([dg_pre]).run(name="gather_g_last")
    parts = dict(zip(SMALL_EARLY, small))
    parts["norm_mix_pre"] = last
    return loss, grad_x, parts, reduced


MESH = pl.DeviceIdType.MESH
_HBM = pl.BlockSpec(memory_space=pl.ANY)
N_XY = 4
N_XY_PEERS = 3


def _xy_peers(x, y):
    return [(1 - x, y), (x, 1 - y), (1 - x, 1 - y)]


class _Exchange:
    def __init__(self, arrays, out_shapes, plan, n_copies, alias):
        self.arrays = list(arrays)
        self.out_shapes = list(out_shapes)
        self.plan = plan
        self.n_copies = n_copies
        self.alias = list(alias) if isinstance(alias, (list, tuple)) else [alias] * len(self.arrays)

    @property
    def n(self):
        return len(self.arrays)

    def aliases(self, first_in, first_out):
        return {first_in + k: first_out + k for k in range(self.n) if self.alias[k]}

    @staticmethod
    def join(a, b):
        def plan(k, src, dst, x, y, c):
            return a.plan(k, src, dst, x, y, c) if k < a.n else b.plan(k - a.n, src, dst, x, y, c)

        return _Exchange(a.arrays + b.arrays, a.out_shapes + b.out_shapes, plan, max(a.n_copies, b.n_copies),
                         a.alias + b.alias)

    def sems(self):
        shape = (self.n, self.n_copies)
        return [pltpu.SemaphoreType.DMA(shape), pltpu.SemaphoreType.DMA(shape)]

    def _copies(self, ins, outs, send_sems, recv_sems):
        x, y, c = lax.axis_index("x"), lax.axis_index("y"), lax.axis_index("c")
        sends, lands, own = [], [], []
        for k in range(self.n):
            for j, (src, dst, dev, land) in enumerate(self.plan(k, ins[k], outs[k], x, y, c)):
                if dev is None:
                    own.append(pltpu.make_async_copy(src, dst, send_sems.at[k, j]))
                    continue
                sems = dict(send_sem=send_sems.at[k, j], recv_sem=recv_sems.at[k, j], device_id=dev, device_id_type=MESH)
                sends.append(pltpu.make_async_remote_copy(src_ref=src, dst_ref=dst, **sems))
                lands.append(pltpu.make_async_remote_copy(src_ref=src, dst_ref=land, **sems))
        return sends, lands, own

    def start(self, ins, outs, send_sems, recv_sems):
        sends, _, own = self._copies(ins, outs, send_sems, recv_sems)
        for cp in own + sends:
            cp.start()

    def finish(self, ins, outs, send_sems, recv_sems):
        sends, lands, own = self._copies(ins, outs, send_sems, recv_sems)
        for cp in lands:
            cp.wait_recv()
        for cp in sends:
            cp.wait_send()
        for cp in own:
            cp.wait()

    def run(self, *, name):
        n = self.n

        def body(*refs):
            parts = (refs[:n], refs[n:2 * n], refs[2 * n], refs[2 * n + 1])
            self.start(*parts)
            self.finish(*parts)

        return pl.pallas_call(
            body, name=name, out_shape=tuple(self.out_shapes),
            in_specs=[_HBM] * n, out_specs=tuple([_HBM] * n),
            input_output_aliases=self.aliases(0, 0),
            scratch_shapes=self.sems(),
        )(*self.arrays)


def _call(host_body, *, name, grid, in_specs, out_specs, out_shape, scratch_shapes, operands, rider=None):
    out_specs, out_shape = tuple(out_specs), tuple(out_shape)
    if rider is None:
        res = pl.pallas_call(
            host_body, name=name, grid=grid, in_specs=list(in_specs), out_specs=out_specs, out_shape=out_shape,
            scratch_shapes=list(scratch_shapes), compiler_params=_cparams(*["arbitrary"] * len(grid)),
        )(*operands)
        return tuple(res), None
    n, n_in, n_out, n_scr = rider.n, len(in_specs), len(out_specs), len(scratch_shapes)

    def body(*refs):
        pos = [0]

        def take(count):
            pos[0] += count
            return refs[pos[0] - count:pos[0]]

        h_in, r_in, h_out, r_out, h_scr = take(n_in), take(n), take(n_out), take(n), take(n_scr)
        send_sems, recv_sems = take(2)
        ids = [pl.program_id(a) for a in range(len(grid))]
        first = functools.reduce(jnp.logical_and, [i == 0 for i in ids])
        last = functools.reduce(jnp.logical_and, [i == g - 1 for i, g in zip(ids, grid)])

        @pl.when(first)
        def _():
            rider.start(r_in, r_out, send_sems, recv_sems)

        host_body(*h_in, *h_out, *h_scr)

        @pl.when(last)
        def _():
            rider.finish(r_in, r_out, send_sems, recv_sems)

    res = pl.pallas_call(
        body, name=name, grid=grid,
        in_specs=list(in_specs) + [_HBM] * n, out_specs=out_specs + tuple([_HBM] * n),
        out_shape=out_shape + tuple(rider.out_shapes),
        input_output_aliases=rider.aliases(n_in, n_out),
        scratch_shapes=list(scratch_shapes) + rider.sems(),
        compiler_params=_cparams(*["arbitrary"] * len(grid)),
    )(*operands, *rider.arrays)
    return tuple(res[:n_out]), list(res[n_out:])


def _same(arrays):
    return [jax.ShapeDtypeStruct(a.shape, a.dtype) for a in arrays]


def _fill_xy(bufs):
    def plan(k, src, dst, x, y, c):
        mine = 2 * x + y
        return [(src.at[mine, c], dst.at[mine, c], (px, py, c), dst.at[2 * px + py, c]) for px, py in _xy_peers(x, y)]

    return _Exchange(bufs, _same(bufs), plan, N_XY_PEERS, alias=True)


def _fill_c(bufs):
    def plan(k, src, dst, x, y, c):
        return [(src.at[:, c], dst.at[:, c], (x, y, 1 - c), dst.at[:, 1 - c])]

    return _Exchange(bufs, _same(bufs), plan, 1, alias=True)


def _slots(arrays):
    return [jax.ShapeDtypeStruct((N_XY, 2) + a.shape, a.dtype) for a in arrays]


def _gather_xy_from(srcs):
    def plan(k, src, dst, x, y, c):
        mine = 2 * x + y
        return ([(src, dst.at[mine, c], None, None)]
                + [(src, dst.at[mine, c], (px, py, c), dst.at[2 * px + py, c]) for px, py in _xy_peers(x, y)])

    return _Exchange(srcs, _slots(srcs), plan, 1 + N_XY_PEERS, alias=False)


def _gather_all(srcs):
    def plan(k, src, dst, x, y, c):
        mine = 2 * x + y
        out = [(src, dst.at[mine, c], None, None)]
        for fx, fy, fc in [(a, b, e) for a in (0, 1) for b in (0, 1) for e in (0, 1)][1:]:
            px, py, pc = (1 - x) if fx else x, (1 - y) if fy else y, (1 - c) if fc else c
            out.append((src, dst.at[mine, c], (px, py, pc), dst.at[2 * px + py, pc]))
        return out

    return _Exchange(srcs, _slots(srcs), plan, N_DEV, alias=False)


def _send_c(srcs):
    def plan(k, src, dst, x, y, c):
        return [(src.at[:, 1 - c], dst, (x, y, 1 - c), dst)]

    outs = [jax.ShapeDtypeStruct(a.shape[:1] + a.shape[2:], a.dtype) for a in srcs]
    return _Exchange(srcs, outs, plan, 1, alias=False)


def _scatter_xy(srcs):
    def plan(k, src, dst, x, y, c):
        return [(src.at[2 * px + py], dst.at[j], (px, py, c), dst.at[j]) for j, (px, py) in enumerate(_xy_peers(x, y))]

    outs = [jax.ShapeDtypeStruct((N_XY_PEERS,) + a.shape[1:], a.dtype) for a in srcs]
    return _Exchange(srcs, outs, plan, N_XY_PEERS, alias=False)


WIRE_DTYPE = BF16


def _pair_sum(g8, recv, core, *, name):
    n, _, R, C = g8.shape
    tr = _pick(R, (128, 64, 32, 16, 8))

    def body(core_ref, a_ref, b_ref, o_ref):
        o_ref[...] = (a_ref[0] + b_ref[...]).astype(WIRE_DTYPE)

    return pl.pallas_call(
        body, name=name, out_shape=jax.ShapeDtypeStruct((n, R, C), WIRE_DTYPE),
        grid_spec=pltpu.PrefetchScalarGridSpec(
            num_scalar_prefetch=1, grid=(n, R // tr),
            in_specs=[pl.BlockSpec((1, 1, tr, C), lambda s, i, core_ref: (s, core_ref[0], i, 0)),
                      pl.BlockSpec((1, tr, C), lambda s, i, core_ref: (s, i, 0))],
            out_specs=pl.BlockSpec((1, tr, C), lambda s, i, core_ref: (s, i, 0))),
        compiler_params=_cparams("parallel", "parallel"),
    )(core, g8, recv)


def _adamw_math(w, g, m, v):
    m = ADAM_B1 * m + (1.0 - ADAM_B1) * g
    v = ADAM_B2 * v + (1.0 - ADAM_B2) * (g * g)
    m_hat = m / (1.0 - ADAM_B1 ** ADAM_STEP)
    v_hat = v / (1.0 - ADAM_B2 ** ADAM_STEP)
    delta = -ADAM_LR * (m_hat / (jnp.sqrt(v_hat) + ADAM_EPS) + ADAM_WD * w)
    return delta, m, v


def _reduce_adamw(parts, w, m, v, *, own, own_slot, name):
    n, R, C = parts.shape
    tr = _pick(R, (128, 64, 32, 16, 8))

    def body(_, own_ref, parts_ref, w_ref, m_ref, v_ref, g_ref, d_ref, nm_ref, nv_ref):
        g = own_ref[0].astype(F32)
        for k in range(n):
            g = g + parts_ref[k].astype(F32)
        g_ref[...] = g
        d_ref[...], nm_ref[...], nv_ref[...] = _adamw_math(w_ref[...], g, m_ref[...], v_ref[...])

    out = jax.ShapeDtypeStruct((R, C), F32)
    row = pl.BlockSpec((tr, C), lambda i, s: (i, 0))
    return pl.pallas_call(
        body, name=name, out_shape=(out, out, out, out),
        grid_spec=pltpu.PrefetchScalarGridSpec(
            num_scalar_prefetch=1, grid=(R // tr,),
            in_specs=[pl.BlockSpec((1, tr, C), lambda i, s: (s[0], i, 0)),
                      pl.BlockSpec((n, tr, C), lambda i, s: (0, i, 0)), row, row, row],
            out_specs=(row, row, row, row)),
        compiler_params=_cparams("parallel"),
    )(own_slot, own, parts, w, m, v)


SHARDED = (("w_in", (1024, 4096), 1), ("ssm_w_glu", (512, 512), 0), ("w_branch_attn", (512, 1024), 1),
           ("w_branch_ssm", (512, 1024), 1), ("w_out", (1024, 1024), 0), ("xa_wq", (1024, 1024), 0),
           ("xa_wk", (1024, 1024), 0), ("xa_wv", (1024, 1024), 0), ("xa_wo", (1024, 1024), 0),
           ("ffn_w_up", (1024, 5632), 1), ("ffn_conv_w", (3, 5632), 1), ("ffn_w_down", (2816, 1024), 0))
REPLICATED = (("norm_mix_pre", (1024,)), ("norm_mix_post", (1024,)), ("b_gate", (2048,)), ("ssm_a_re", (32, 64)),
              ("ssm_a_im", (32, 64)), ("ssm_log_dt", (32,)), ("ssm_b_re", (32, 64, 16)), ("ssm_b_im", (32, 64, 16)),
              ("ssm_c_re", (32, 16, 64)), ("ssm_c_im", (32, 16, 64)), ("ssm_d", (512,)), ("ssm_b_glu", (512,)),
              ("norm_xa_pre", (1024,)), ("norm_xa_post", (1024,)), ("norm_mem", (1024,)), ("norm_ffn_pre", (1024,)),
              ("norm_ffn_post", (1024,)), ("ffn_conv_b", (5632,)))
PARAM_ORDER = ("norm_mix_pre", "norm_mix_post", "w_in", "b_gate", "ssm_a_re", "ssm_a_im", "ssm_log_dt", "ssm_b_re",
               "ssm_b_im", "ssm_c_re", "ssm_c_im", "ssm_d", "ssm_w_glu", "ssm_b_glu", "w_branch_attn", "w_branch_ssm",
               "w_out", "norm_xa_pre", "norm_xa_post", "norm_mem", "xa_wq", "xa_wk", "xa_wv", "xa_wo", "norm_ffn_pre",
               "norm_ffn_post", "ffn_w_up", "ffn_conv_w", "ffn_conv_b", "ffn_w_down")
FF_LOCAL = 2 * D_FF // N_DEV
FF_LOCAL_PAD = 768
FF_PAD = (N_DEV // 2) * FF_LOCAL_PAD


def _local_shape(shape, axis):
    return tuple(s // N_DEV if a == axis else s for a, s in enumerate(shape))


def _pad_cols(a, width):
    return jnp.pad(a, [(0, 0)] * (a.ndim - 1) + [(0, width - a.shape[-1])])


def _blocks_to_cols(a8):
    return a8.transpose(1, 0, 2).reshape(a8.shape[1], N_DEV * a8.shape[2])


def _cols_to_blocks(a, cb):
    return a.reshape(a.shape[0], N_DEV, cb).transpose(1, 0, 2)


FF_PADDED = ("ffn_w_up", "ffn_conv_w")
LATE = tuple(n for n, _, _ in SHARDED if n != "w_in")
REDUCE_FFN = ("ffn_w_up", "ffn_conv_w", "ffn_w_down")
GATHER_XATTN = ("xa_wq", "xa_wk", "xa_wv", "xa_wo")
REDUCE_MID = ("xa_wo", "xa_wq", "xa_wk", "xa_wv", "w_out", "w_branch_attn", "w_branch_ssm", "ssm_w_glu")
SHARD_AXIS = {n: ax for n, _, ax in SHARDED}
FULL_SHAPE = {n: s for n, s, _ in SHARDED}


def _as_local(n, a):
    return _pad_cols(a, FF_LOCAL_PAD) if n in FF_PADDED else a


def _weights_from_wire(wire):
    full = {n: b.reshape((N_DEV,) + b.shape[2:]) for n, b in wire.items()}
    W = {n: a.reshape(FULL_SHAPE[n]) if SHARD_AXIS[n] == 0 else a for n, a in full.items()}
    for n in ("w_branch_attn", "w_branch_ssm", "ffn_conv_w"):
        if n in full:
            W[n] = _blocks_to_cols(full[n])
    if "ffn_w_down" in W:
        W["ffn_w_down"] = jnp.pad(W["ffn_w_down"].reshape(N_DEV // 2, FF_LOCAL, D_MODEL),
                                  ((0, 0), (0, FF_LOCAL_PAD - FF_LOCAL), (0, 0))).reshape(FF_PAD, D_MODEL)
    return W


def _grad_blocks(n, g):
    if n in ("w_branch_attn", "w_branch_ssm"):
        g = _cols_to_blocks(g, D_MODEL // N_DEV)
    elif n == "ffn_conv_w":
        g = _cols_to_blocks(g, FF_LOCAL_PAD)
    elif n == "ffn_w_down":
        g = g.reshape(N_DEV // 2, FF_LOCAL_PAD, D_MODEL)[:, :FF_LOCAL]
    local = _local_shape(FULL_SHAPE[n], SHARD_AXIS[n])
    if n in FF_PADDED:
        local = local[:-1] + (FF_LOCAL_PAD,)
    return g.reshape((N_XY, 2) + local)


SMALL_SHAPE = {n: (1, s[0]) if len(s) == 1 else (s[0], math.prod(s[1:])) for n, s in REPLICATED}
SMALL_SHAPE["ffn_conv_b"] = (N_DEV, FF_LOCAL)
SMALL_EARLY = tuple(n for n, _ in REPLICATED if n != "norm_mix_pre")


def _adamw_replicated(parts, w, m, v, *, name):
    n = len(parts)

    def body(*refs):
        p_refs, w_refs, m_refs, v_refs = (refs[i * n:(i + 1) * n] for i in range(4))
        outs = refs[4 * n:]
        for k in range(n):
            g = p_refs[k][0, 0]
            for s in range(1, N_DEV):
                g = g + p_refs[k][s // 2, s % 2]
            d, nm, nv = _adamw_math(w_refs[k][...], g, m_refs[k][...], v_refs[k][...])
            for slot, val in enumerate((g, d, nm, nv)):
                outs[slot * n + k][...] = val

    vmem = pl.BlockSpec(memory_space=pltpu.VMEM)
    shapes = [jax.ShapeDtypeStruct(a.shape, F32) for a in w] * 4
    res = pl.pallas_call(
        body, name=name, out_shape=tuple(shapes), in_specs=[vmem] * (4 * n), out_specs=tuple([vmem] * (4 * n)),
        compiler_params=pltpu.CompilerParams(vmem_limit_bytes=VMEM_LIMIT),
    )(*parts, *w, *m, *v)
    return [list(res[i * n:(i + 1) * n]) for i in range(4)]


def kernel(x, mem, norm_mix_pre, norm_mix_post, w_in, b_gate, ssm_a_re, ssm_a_im, ssm_log_dt, ssm_b_re, ssm_b_im, ssm_c_re, ssm_c_im, ssm_d, ssm_w_glu, ssm_b_glu, w_branch_attn, w_branch_ssm, w_out, norm_xa_pre, norm_xa_post, norm_mem, xa_wq, xa_wk, xa_wv, xa_wo, norm_ffn_pre, norm_ffn_post, ffn_w_up, ffn_conv_w, ffn_conv_b, ffn_w_down, loss_target, m_norm_mix_pre, m_norm_mix_post, m_w_in, m_b_gate, m_ssm_a_re, m_ssm_a_im, m_ssm_log_dt, m_ssm_b_re, m_ssm_b_im, m_ssm_c_re, m_ssm_c_im, m_ssm_d, m_ssm_w_glu, m_ssm_b_glu, m_w_branch_attn, m_w_branch_ssm, m_w_out, m_norm_xa_pre, m_norm_xa_post, m_norm_mem, m_xa_wq, m_xa_wk, m_xa_wv, m_xa_wo, m_norm_ffn_pre, m_norm_ffn_post, m_ffn_w_up, m_ffn_conv_w, m_ffn_conv_b, m_ffn_w_down, v_norm_mix_pre, v_norm_mix_post, v_w_in, v_b_gate, v_ssm_a_re, v_ssm_a_im, v_ssm_log_dt, v_ssm_b_re, v_ssm_b_im, v_ssm_c_re, v_ssm_c_im, v_ssm_d, v_ssm_w_glu, v_ssm_b_glu, v_w_branch_attn, v_w_branch_ssm, v_w_out, v_norm_xa_pre, v_norm_xa_post, v_norm_mem, v_xa_wq, v_xa_wk, v_xa_wv, v_xa_wo, v_norm_ffn_pre, v_norm_ffn_post, v_ffn_w_up, v_ffn_conv_w, v_ffn_conv_b, v_ffn_w_down):
    args = dict(locals())
    w_loc = {n: args[n][0] for n in PARAM_ORDER}
    m_loc = {n: args["m_" + n][0] for n in PARAM_ORDER}
    v_loc = {n: args["v_" + n][0] for n in PARAM_ORDER}
    core_i = lax.axis_index("c")
    chip_i = 2 * lax.axis_index("x") + lax.axis_index("y")
    core = core_i.astype(jnp.int32).reshape(1)
    chip = chip_i.astype(jnp.int32).reshape(1)

    def in_place(a):
        buf = lax.empty((N_XY, 2) + a.shape, a.dtype)
        return lax.dynamic_update_slice(buf, a[None, None], (chip_i, core_i) + (0,) * a.ndim)

    as_wire = lambda n: in_place(_as_local(n, w_loc[n]).astype(F32 if n == "ffn_conv_w" else BF16))

    P = {}
    for n, shape in REPLICATED:
        P[n] = w_loc[n] if len(shape) > 1 or n == "ssm_log_dt" else w_loc[n].reshape(1, -1)
    P["ffn_conv_b"] = _pad_cols(w_loc["ffn_conv_b"].reshape(N_DEV, FF_LOCAL), FF_LOCAL_PAD).reshape(1, 2 * FF_PAD)

    loss, grad_x, small_parts, reduced = _local_step(x[0], mem[0], loss_target[0], as_wire("w_in"),
                                                     [as_wire(n) for n in LATE], P, core)
    loss = lax.psum(loss[0, 0], ("x", "y", "c"))

    big_out = {}
    for n, (own, parts) in reduced.items():
        res = _reduce_adamw(parts, _as_local(n, w_loc[n]), _as_local(n, m_loc[n]), _as_local(n, v_loc[n]),
                            own=own, own_slot=chip, name="adamw_" + n)
        big_out[n] = [r[:, :FF_LOCAL] if n in FF_PADDED else r for r in res]

    names = [n for n, _ in REPLICATED]
    as_small = lambda d: [d[n].reshape(SMALL_SHAPE[n]) for n in names]
    small_out = _adamw_replicated([small_parts[n] for n in names], as_small(w_loc), as_small(m_loc), as_small(v_loc),
                                  name="adamw_replicated")
    small_out = [dict(zip(names, res)) for res in small_out]

    outs = [loss, grad_x[None]]
    for k in range(4):
        for n in PARAM_ORDER:
            src = big_out[n][k] if n in big_out else small_out[k][n]
            outs.append(src.reshape(args[n].shape))
    return tuple(outs)
```

```python
import functools
import math

import jax
import jax.numpy as jnp
from jax import lax
from jax.experimental import pallas as pl
from jax.experimental.pallas import tpu as pltpu

F32 = jnp.float32
BF16 = jnp.bfloat16

D_MODEL = 1024
SB_HEADS = 8
SB_HEAD_DIM = 64
SB_WIDTH = 512
SSM_WIDTH = 512
SSM_GROUP = 16
SSM_GROUPS = 32
SSM_STATE = 64
XA_HEADS = 4
XA_HEAD_DIM = 256
D_FF = 2816
RMS_EPS = 1e-6
IN_WIDTH = 4096
N_DEV = 8

ADAM_LR = 0.001
ADAM_B1 = 0.9
ADAM_B2 = 0.999
ADAM_EPS = 1e-08
ADAM_WD = 0.01
ADAM_STEP = 10

LANES = 128
SUBLANES = 8
VMEM_LIMIT = 48 * 1024 * 1024

_GELU_C = math.sqrt(2.0 / math.pi)


def _cparams(*sem):
    return pltpu.CompilerParams(dimension_semantics=sem, vmem_limit_bytes=VMEM_LIMIT)


def _pick(n, cands):
    for c in cands:
        if n % c == 0:
            return c
    return n


def _gelu(x):
    return 0.5 * x * (1.0 + jnp.tanh(_GELU_C * (x + 0.044715 * x * x * x)))


def _gelu_and_grad(x):
    t = jnp.tanh(_GELU_C * (x + 0.044715 * x * x * x))
    g = 0.5 * x * (1.0 + t)
    dg = 0.5 * (1.0 + t) + 0.5 * x * (1.0 - t * t) * _GELU_C * (1.0 + 3.0 * 0.044715 * x * x)
    return g, dg


def _sigmoid(x):
    return 1.0 / (1.0 + jnp.exp(-x))


def _dot(a, b, ca, cb):
    return lax.dot_general(a.astype(BF16), b.astype(BF16), (((ca,), (cb,)), ((), ())),
                           preferred_element_type=F32)


MM_TILES = (1024, 768, 512, 256, 128)
MM_K_TILES = (2048, 1536) + MM_TILES
MM_PAIR = 2
MM_WIDE = 1536


def _matmul(a, b, *, ta=False, tb=False, out_dtype=F32, name, b_block0=0, n_blocks=None,
            out_cb=None, out_into=None, out_block0=0, acc_in=None, rider=None):
    if ta:
        K, M = a.shape
    else:
        M, K = a.shape
    b_cb = None
    if b.ndim == 3:
        b_cb = b.shape[2]
        n_blocks = b.shape[0] - b_block0 if n_blocks is None else n_blocks
        N, K2 = (b.shape[1], n_blocks * b_cb) if tb else (n_blocks * b_cb, b.shape[1])
    elif tb:
        N, K2 = b.shape
    else:
        K2, N = b.shape
    assert K == K2, (a.shape, b.shape, ta, tb)
    if out_into is not None:
        out_cb = out_into.shape[2]
    tm = _pick(M, MM_TILES)
    pair = lambda cb_, count: MM_PAIR if (cb_ * MM_PAIR <= MM_WIDE and count % MM_PAIR == 0) else 1
    b_pair = pair(b_cb, n_blocks) if b_cb else 1
    o_pair = pair(out_cb, N // out_cb) if out_cb else 1
    if b_cb and not tb:
        tn = b_cb * b_pair
    elif out_cb:
        tn = out_cb * o_pair
    else:
        tn = _pick(N, MM_TILES)
    if b_cb and tb:
        tk = b_cb * b_pair
    else:
        tk = _pick(K, MM_TILES if tn > MM_TILES[0] else MM_K_TILES)
    nk = K // tk
    ca, cb = (0 if ta else 1), (1 if tb else 0)
    has_acc = acc_in is not None
    has_into = out_into is not None

    def body(*refs):
        a_ref, b_ref = refs[0], refs[1]
        pos = 2
        c_ref = None
        if has_acc:
            c_ref = refs[pos]
            pos += 1
        if has_into:
            pos += 1
        o_ref = refs[pos]
        b_tile = b_ref[...] if b_cb is None else jnp.concatenate([b_ref[t] for t in range(b_pair)], axis=1)
        p = _dot(a_ref[...], b_tile, ca, cb)

        def write(val):
            val = val.astype(out_dtype)
            if out_cb is None:
                o_ref[...] = val
            else:
                for t in range(o_pair):
                    o_ref[t] = val[:, t * out_cb:(t + 1) * out_cb]

        if nk == 1:
            write((p + c_ref[...]) if has_acc else p)
        else:
            acc_ref = refs[pos + 1]
            k = pl.program_id(2)

            @pl.when(k == 0)
            def _():
                acc_ref[...] = (p + c_ref[...]) if has_acc else p

            @pl.when(k > 0)
            def _():
                acc_ref[...] += p

            @pl.when(k == nk - 1)
            def _():
                write(acc_ref[...])

    nj, ni = N // tn, M // tm
    a_bytes, b_bytes = a.size * a.dtype.itemsize, K * N * b.dtype.itemsize
    n_outer = a_bytes * nj + b_bytes * (1 if nk == 1 else ni) <= a_bytes * (1 if nk == 1 else nj) + b_bytes * ni
    grid = (nj, ni, nk) if n_outer else (ni, nj, nk)

    def spec(block, index):
        return pl.BlockSpec(block, (lambda g0, g1, k: index(g0, g1, k)) if n_outer else (lambda g0, g1, k: index(g1, g0, k)))

    a_spec = spec((tk, tm), lambda j, i, k: (k, i)) if ta else spec((tm, tk), lambda j, i, k: (i, k))
    if b_cb is None:
        b_spec = spec((tn, tk), lambda j, i, k: (j, k)) if tb else spec((tk, tn), lambda j, i, k: (k, j))
    elif tb:
        b_spec = spec((b_pair, tn, b_cb), lambda j, i, k: (b_block0 // b_pair + k, j, 0))
    else:
        b_spec = spec((b_pair, tk, b_cb), lambda j, i, k: (b_block0 // b_pair + j, k, 0))
    in_specs = [a_spec, b_spec]
    operands = [a, b]
    aliases = {}
    if has_acc:
        in_specs.append(spec((tm, tn), lambda j, i, k: (i, j)))
        operands.append(acc_in)
    if has_into:
        aliases = {len(operands): 0}
        in_specs.append(pl.BlockSpec(memory_space=pl.ANY))
        operands.append(out_into)
    if out_cb is None:
        out_shape = jax.ShapeDtypeStruct((M, N), out_dtype)
        out_spec = spec((tm, tn), lambda j, i, k: (i, j))
    else:
        out_shape = (jax.ShapeDtypeStruct(out_into.shape, out_into.dtype) if has_into
                     else jax.ShapeDtypeStruct((N // out_cb, M, out_cb), out_dtype))
        out_spec = spec((o_pair, tm, out_cb), lambda j, i, k: (out_block0 // o_pair + j, i, 0))
    if rider is not None:
        assert not has_into
        (out,), brought = _call(body, name=name, rider=rider, grid=grid, in_specs=in_specs,
                                out_specs=(out_spec,), out_shape=(out_shape,), operands=operands,
                                scratch_shapes=[] if nk == 1 else [pltpu.VMEM((tm, tn), F32)])
        return out, brought
    return pl.pallas_call(
        body, name=name, out_shape=out_shape,
        grid=grid,
        in_specs=in_specs, out_specs=out_spec, input_output_aliases=aliases,
        scratch_shapes=[] if nk == 1 else [pltpu.VMEM((tm, tn), F32)],
        compiler_params=_cparams("parallel", "parallel", "arbitrary"),
    )(*operands)


def _rms(x, g):
    r = lax.rsqrt(jnp.mean(x * x, axis=-1, keepdims=True) + RMS_EPS)
    return x * r * g


def _rms_bwd(dy, x, g):
    r = lax.rsqrt(jnp.mean(x * x, axis=-1, keepdims=True) + RMS_EPS)
    xh = x * r
    dxh = dy * g
    dx = r * (dxh - xh * jnp.mean(dxh * xh, axis=-1, keepdims=True))
    dg = jnp.sum(dy * xh, axis=0, keepdims=True)
    return dx, dg


def _row_tile(rows):
    return _pick(rows, (512, 256, 128, 64, 32, 16, 8))


def _rms_fwd(x, g, *, name, rider=None):
    R, D = x.shape
    tr = _row_tile(R)

    def body(x_ref, g_ref, h_ref):
        h_ref[...] = _rms(x_ref[...], g_ref[...]).astype(BF16)

    (h,), brought = _call(
        body, name=name, rider=rider, out_shape=(jax.ShapeDtypeStruct((R, D), BF16),), grid=(R // tr,),
        in_specs=[pl.BlockSpec((tr, D), lambda i: (i, 0)), pl.BlockSpec((1, D), lambda i: (0, 0))],
        out_specs=(pl.BlockSpec((tr, D), lambda i: (i, 0)),), scratch_shapes=[], operands=(x, g))
    return h if rider is None else (h, brought)


def _resnorm_norm(x, z, g_post, g_next, *, name):
    R, D = x.shape
    tr = _row_tile(R)

    def body(x_ref, z_ref, gp_ref, gn_ref, xn_ref, h_ref):
        xn = x_ref[...] + _rms(z_ref[...], gp_ref[...])
        xn_ref[...] = xn
        h_ref[...] = _rms(xn, gn_ref[...]).astype(BF16)

    row = pl.BlockSpec((tr, D), lambda i: (i, 0))
    vec = pl.BlockSpec((1, D), lambda i: (0, 0))
    return pl.pallas_call(
        body, name=name,
        out_shape=(jax.ShapeDtypeStruct((R, D), F32), jax.ShapeDtypeStruct((R, D), BF16)),
        grid=(R // tr,), in_specs=[row, row, vec, vec], out_specs=(row, row),
        compiler_params=_cparams("parallel"),
    )(x, z, g_post, g_next)


def _final_loss(x, z, g_post, target, *, name):
    R, D = x.shape
    tr = _row_tile(R)

    def body(x_ref, z_ref, gp_ref, t_ref, loss_ref, dy_ref, dz_ref, dg_ref):
        i = pl.program_id(0)
        z = z_ref[...]
        g = gp_ref[...]
        err = x_ref[...] + _rms(z, g) - t_ref[...]
        dy = err * (1.0 / D)
        dy_ref[...] = dy
        dz, dg = _rms_bwd(dy, z, g)
        dz_ref[...] = dz.astype(BF16)
        part = 0.5 * jnp.sum(jnp.sum(err * err, axis=-1, keepdims=True) * (1.0 / D), axis=0, keepdims=True)

        @pl.when(i == 0)
        def _():
            loss_ref[...] = part
            dg_ref[...] = dg

        @pl.when(i > 0)
        def _():
            loss_ref[...] += part
            dg_ref[...] += dg

    row = pl.BlockSpec((tr, D), lambda i: (i, 0))
    vec = pl.BlockSpec((1, D), lambda i: (0, 0))
    return pl.pallas_call(
        body, name=name,
        out_shape=(jax.ShapeDtypeStruct((1, 1), F32), jax.ShapeDtypeStruct((R, D), F32),
                   jax.ShapeDtypeStruct((R, D), BF16), jax.ShapeDtypeStruct((1, D), F32)),
        grid=(R // tr,), in_specs=[row, row, vec, row],
        out_specs=(pl.BlockSpec((1, 1), lambda i: (0, 0)), row, row, vec),
        compiler_params=_cparams("arbitrary"),
    )(x, z, g_post, target)


def _norm_bwd_pair(dres, dh, xk, g_pre, zprev, g_prev_post, *, name, rider=None):
    R, D = xk.shape
    tr = _row_tile(R)

    def body(dres_ref, dh_ref, x_ref, gpre_ref, z_ref, gpost_ref, dx_ref, dz_ref, dgpre_ref, dgpost_ref):
        i = pl.program_id(0)
        d1, dgpre = _rms_bwd(dh_ref[...], x_ref[...], gpre_ref[...])
        dx = dres_ref[...] + d1
        dx_ref[...] = dx
        dz, dgpost = _rms_bwd(dx, z_ref[...], gpost_ref[...])
        dz_ref[...] = dz.astype(BF16)

        @pl.when(i == 0)
        def _():
            dgpre_ref[...] = dgpre
            dgpost_ref[...] = dgpost

        @pl.when(i > 0)
        def _():
            dgpre_ref[...] += dgpre
            dgpost_ref[...] += dgpost

    row = pl.BlockSpec((tr, D), lambda i: (i, 0))
    vec = pl.BlockSpec((1, D), lambda i: (0, 0))
    return _call(
        body, name=name, rider=rider,
        out_shape=(jax.ShapeDtypeStruct((R, D), F32), jax.ShapeDtypeStruct((R, D), BF16),
                   jax.ShapeDtypeStruct((1, D), F32), jax.ShapeDtypeStruct((1, D), F32)),
        grid=(R // tr,), in_specs=[row, row, row, vec, row, vec], out_specs=(row, row, vec, vec),
        scratch_shapes=[], operands=(dres, dh, xk, g_pre, zprev, g_prev_post))


def _norm_bwd_single(dres, dh, xk, g_pre, *, name, rider=None):
    R, D = xk.shape
    tr = _row_tile(R)
    has_res = dres is not None

    def body(*refs):
        if has_res:
            dres_ref, dh_ref, x_ref, gpre_ref, dx_ref, dgpre_ref = refs
        else:
            dh_ref, x_ref, gpre_ref, dx_ref, dgpre_ref = refs
        i = pl.program_id(0)
        d1, dgpre = _rms_bwd(dh_ref[...], x_ref[...], gpre_ref[...])
        dx_ref[...] = dres_ref[...] + d1 if has_res else d1

        @pl.when(i == 0)
        def _():
            dgpre_ref[...] = dgpre

        @pl.when(i > 0)
        def _():
            dgpre_ref[...] += dgpre

    row = pl.BlockSpec((tr, D), lambda i: (i, 0))
    vec = pl.BlockSpec((1, D), lambda i: (0, 0))
    ins = ([dres] if has_res else []) + [dh, xk, g_pre]
    res, brought = _call(
        body, name=name, rider=rider,
        out_shape=(jax.ShapeDtypeStruct((R, D), F32), jax.ShapeDtypeStruct((1, D), F32)),
        grid=(R // tr,), in_specs=([row] if has_res else []) + [row, row, vec], out_specs=(row, vec),
        scratch_shapes=[], operands=ins)
    return res if rider is None else (res, brought)


SB_BLOCK = 256
SB_QBLOCK = 512
SB_DEAD = -104.0


def _sb_tri(kind):
    r = lax.broadcasted_iota(jnp.int32, (SB_BLOCK, SB_BLOCK), 0)
    c = lax.broadcasted_iota(jnp.int32, (SB_BLOCK, SB_BLOCK), 1)
    keep = {"after": r > c, "before": r < c}[kind]
    return jnp.where(keep, 1.0, 0.0).astype(BF16)


def _sb_scores(qm, k_blk):
    z = _dot(qm, k_blk, 1, 1)
    sp = jnp.maximum(z, 0.0) + jnp.log(1.0 + jnp.exp(-jnp.abs(z)))
    return z, sp


def _sb_causal(rows):
    r = lax.broadcasted_iota(jnp.int32, (rows, SB_BLOCK), 0)
    c = lax.broadcasted_iota(jnp.int32, (rows, SB_BLOCK), 1)
    return c < r


def _head_masks():
    lane = lax.broadcasted_iota(jnp.int32, (1, LANES), 1)
    return [jnp.where(lane < SB_HEAD_DIM, 1.0, 0.0), jnp.where(lane >= SB_HEAD_DIM, 1.0, 0.0)]


def _sb_fwd(proj, *, name, rider=None):
    S = proj.shape[0]
    T = SB_BLOCK
    TQ = min(SB_QBLOCK, S)
    span = TQ // T
    nq = S // TQ
    npair = SB_WIDTH // LANES
    scale = SB_HEAD_DIM ** -0.5

    def body(q_ref, k_ref, v_ref, o_ref, tot_ref, first_ref, acc_ref, run_ref):
        masks = _head_masks()
        tri = _sb_tri("after")
        first_ref[...] = jnp.zeros_like(first_ref)
        slot = lax.broadcasted_iota(jnp.int32, first_ref.shape, 1)

        def alive():
            reach = jnp.maximum(jnp.max(run_ref[0]), jnp.max(run_ref[1]))
            return (reach > SB_DEAD).astype(jnp.int32)

        def q_block(i, _):
            qrow = pl.ds(pl.multiple_of(i * TQ, TQ), TQ)
            q = q_ref[qrow, :] * scale
            qm = [(q * m).astype(BF16) for m in masks]
            acc_ref[...] = jnp.zeros_like(acc_ref)
            run_ref[...] = jnp.zeros_like(run_ref)

            def k_block(j, own):
                krow = pl.ds(pl.multiple_of(j * T, T), T)
                k_blk = k_ref[krow, :].astype(BF16)
                v_blk = v_ref[krow, :].astype(BF16)
                r0 = 0 if own is None else own * T
                rows = pl.ds(r0, TQ - r0)
                for h in range(2):
                    z, sp = _sb_scores(qm[h][r0:], k_blk)
                    causal = None if own is None else _sb_causal(TQ - r0)
                    lf = -sp if causal is None else jnp.where(causal, -sp, 0.0)
                    e = jnp.exp(z - sp + _dot(lf, tri, 1, 0) + run_ref[h, rows])
                    w = e if causal is None else jnp.where(causal, e, 0.0)
                    acc_ref[h, rows] += _dot(w, v_blk, 1, 0)
                    run_ref[h, rows] += jnp.sum(lf, axis=1, keepdims=True)

            for d in reversed(range(span)):
                k_block(i * span + d, d)

            def below(carry):
                jj, _ = carry
                k_block(i * span - 1 - jj, None)
                return jj + 1, alive()

            done, _ = lax.while_loop(lambda c: jnp.logical_and(c[0] < i * span, c[1] > 0), below, (jnp.int32(0), alive()))
            o_ref[qrow, :] = (acc_ref[0] * masks[0] + acc_ref[1] * masks[1]).astype(BF16)
            tot_ref[qrow, :] = run_ref[0] * masks[0] + run_ref[1] * masks[1]
            first_ref[...] = jnp.where(slot == i, (i * span - done).astype(F32), first_ref[...])
            return 0

        lax.fori_loop(0, nq, q_block, 0)

    blk = lambda off: pl.BlockSpec((S, LANES), lambda p: (0, off + p))
    return _call(
        body, name=name, rider=rider,
        out_shape=(jax.ShapeDtypeStruct((S, SB_WIDTH), BF16), jax.ShapeDtypeStruct((S, SB_WIDTH), F32),
                   jax.ShapeDtypeStruct((npair, SUBLANES, LANES), F32)),
        grid=(npair,),
        in_specs=[blk(0), blk(npair), blk(2 * npair)],
        out_specs=(blk(0), blk(0), pl.BlockSpec((1, SUBLANES, LANES), lambda p: (p, 0, 0))),
        scratch_shapes=[pltpu.VMEM((2, TQ, LANES), F32), pltpu.VMEM((2, TQ, 1), F32)],
        operands=(proj, proj, proj))


def _sb_bwd(proj, tot, first, do_attn, *, name, rider=None):
    S = proj.shape[0]
    T = SB_BLOCK
    TQ = min(SB_QBLOCK, S)
    span = TQ // T
    nq = S // TQ
    npair = SB_WIDTH // LANES
    scale = SB_HEAD_DIM ** -0.5

    def body(q_ref, k_ref, v_ref, tot_ref, first_ref, do_ref, dq_ref, dk_ref, dv_ref,
             dqacc_ref, dkacc_ref, dvacc_ref, run_ref, grun_ref):
        masks = _head_masks()
        tri_after = _sb_tri("after")
        tri_before = _sb_tri("before")
        dkacc_ref[...] = jnp.zeros_like(dkacc_ref)
        dvacc_ref[...] = jnp.zeros_like(dvacc_ref)
        slot = lax.broadcasted_iota(jnp.int32, first_ref.shape, 1)

        def q_block(i, _):
            qrow = pl.ds(pl.multiple_of(i * TQ, TQ), TQ)
            q = q_ref[qrow, :] * scale
            do = do_ref[qrow, :].astype(F32)
            tot = tot_ref[qrow, :]
            qm = [(q * m).astype(BF16) for m in masks]
            dom = [(do * m).astype(BF16) for m in masks]
            ltot = [jnp.sum(tot * m, axis=1, keepdims=True) * (1.0 / SB_HEAD_DIM) for m in masks]
            dqacc_ref[...] = jnp.zeros_like(dqacc_ref)
            run_ref[...] = jnp.zeros_like(run_ref)
            grun_ref[...] = jnp.zeros_like(grun_ref)

            def k_block(j, own):
                krow = pl.ds(pl.multiple_of(j * T, T), T)
                k_blk = k_ref[krow, :].astype(BF16)
                v_blk = v_ref[krow, :].astype(BF16)
                r0 = 0 if own is None else own * T
                rows = pl.ds(r0, TQ - r0)
                for h in range(2):
                    z, sp = _sb_scores(qm[h][r0:], k_blk)
                    causal = None if own is None else _sb_causal(TQ - r0)
                    lf = -sp if causal is None else jnp.where(causal, -sp, 0.0)
                    lsum = jnp.sum(lf, axis=1, keepdims=True)
                    later = (ltot[h][r0:] - run_ref[h, rows] - lsum) + _dot(lf, tri_after, 1, 0)
                    beta = jnp.exp(z - sp)
                    w = jnp.exp(z - sp + later)
                    if causal is not None:
                        w = jnp.where(causal, w, 0.0)
                    g = _dot(dom[h][r0:], v_blk, 1, 1) * w
                    gbefore = grun_ref[h, rows] + _dot(g, tri_before, 1, 0)
                    dz = g - beta * (g + gbefore)
                    if causal is not None:
                        dz = jnp.where(causal, dz, 0.0)
                    dz = dz.astype(BF16)
                    dqacc_ref[h, rows] += _dot(dz, k_blk, 1, 0)
                    dkacc_ref[krow, :] += _dot(dz, qm[h][r0:], 0, 0)
                    dvacc_ref[krow, :] += _dot(w, dom[h][r0:], 0, 0)
                    run_ref[h, rows] += lsum
                    grun_ref[h, rows] += jnp.sum(g, axis=1, keepdims=True)

            def above(j, _):
                k_block(j, None)
                return 0

            first = jnp.max(jnp.where(slot == i, first_ref[...], 0.0)).astype(jnp.int32)
            lax.fori_loop(jnp.clip(first, 0, i * span), i * span, above, 0)
            for d in range(span):
                k_block(i * span + d, d)
            dq_ref[qrow, :] = ((dqacc_ref[0] * masks[0] + dqacc_ref[1] * masks[1]) * scale).astype(BF16)
            return 0

        lax.fori_loop(0, nq, q_block, 0)
        dk_ref[...] = dkacc_ref[...].astype(BF16)
        dv_ref[...] = dvacc_ref[...].astype(BF16)

    blk = lambda off: pl.BlockSpec((S, LANES), lambda p: (0, off + p))
    out = jax.ShapeDtypeStruct((S, SB_WIDTH), BF16)
    return _call(
        body, name=name, rider=rider, out_shape=(out, out, out), grid=(npair,),
        in_specs=[blk(0), blk(npair), blk(2 * npair), blk(0), pl.BlockSpec((1, SUBLANES, LANES), lambda p: (p, 0, 0)),
                  blk(0)],
        out_specs=(blk(0), blk(0), blk(0)),
        scratch_shapes=[pltpu.VMEM((2, TQ, LANES), F32), pltpu.VMEM((S, LANES), F32), pltpu.VMEM((S, LANES), F32),
                        pltpu.VMEM((2, TQ, 1), F32), pltpu.VMEM((2, TQ, 1), F32)],
        operands=(proj, proj, proj, tot, first, do_attn))


SSM_HALVES = 2
SSM_HALF_CH = SSM_WIDTH // SSM_HALVES
SSM_HALF_ST = SSM_GROUPS * SSM_STATE // SSM_HALVES
SSM_CHUNK = 512


def _cmul(ar, ai, br, bi):
    return ar * br - ai * bi, ar * bi + ai * br


def _ssm_tables(lam_re, lam_im):
    lr = lam_re.reshape(-1)
    li = lam_im.reshape(-1)
    pows = [(jnp.ones_like(lr), jnp.zeros_like(li)), (lr, li)]
    for _ in range(2, SUBLANES + 1):
        pows.append(_cmul(pows[-1][0], pows[-1][1], lr, li))
    row = jnp.arange(SUBLANES)[:, None]

    def shift_tab(d, keep):
        return [jnp.where(keep, pows[d][0][None, :], 0.0), jnp.where(keep, pows[d][1][None, :], 0.0)]

    fwd, bwd = [], []
    for d in (1, 2, 4):
        fwd += shift_tab(d, row >= d)
        bwd += shift_tab(d, row + d < SUBLANES)
    fwd += [jnp.stack([pows[r + 1][0] for r in range(SUBLANES)]), jnp.stack([pows[r + 1][1] for r in range(SUBLANES)])]
    bwd += [jnp.stack([pows[SUBLANES - r][0] for r in range(SUBLANES)]),
            jnp.stack([pows[SUBLANES - r][1] for r in range(SUBLANES)])]

    def halves(tabs):
        t = jnp.stack(tabs)
        return t.reshape(8, SUBLANES, SSM_HALVES, SSM_HALF_ST).transpose(2, 0, 1, 3)

    return halves(fwd), halves(bwd)


def _expand_groups(blocks, rows, cols):
    gh = SSM_GROUPS // SSM_HALVES
    R, C = gh * rows, gh * cols
    rep = (lax.broadcasted_iota(jnp.int32, (cols, C), 1) & (cols - 1)) == lax.broadcasted_iota(jnp.int32, (cols, C), 0)
    wide = _dot(blocks, jnp.where(rep, 1.0, 0.0), 1, 0)
    r = lax.broadcasted_iota(jnp.int32, (R, C), 0) >> (rows.bit_length() - 1)
    c = lax.broadcasted_iota(jnp.int32, (R, C), 1) >> (cols.bit_length() - 1)
    return jnp.where(r == c, wide, 0.0).astype(BF16)


def _ssm_expand(bre_ref, bim_ref, cre_ref, cim_ref, bd_re_s, bd_im_s, cd_re_s, cd_im_s):
    bd_re_s[...] = _expand_groups(bre_ref[0], SSM_GROUP, SSM_STATE)
    bd_im_s[...] = _expand_groups(bim_ref[0], SSM_GROUP, SSM_STATE)
    cd_re_s[...] = _expand_groups(cre_ref[0], SSM_STATE, SSM_GROUP)
    cd_im_s[...] = _expand_groups(cim_ref[0], SSM_STATE, SSM_GROUP)


def _ssm_matrix_specs():
    b = pl.BlockSpec((1, SSM_HALF_CH, SSM_STATE), lambda h, c: (h, 0, 0))
    c = pl.BlockSpec((1, SSM_HALF_ST, SSM_GROUP), lambda h, c: (h, 0, 0))
    return [b, b, c, c]


def _ssm_matrix_scratch():
    return [pltpu.VMEM((SSM_HALF_CH, SSM_HALF_ST), BF16), pltpu.VMEM((SSM_HALF_CH, SSM_HALF_ST), BF16),
            pltpu.VMEM((SSM_HALF_ST, SSM_HALF_CH), BF16), pltpu.VMEM((SSM_HALF_ST, SSM_HALF_CH), BF16)]


def _ssm_fwd(proj, b_re, b_im, c_re, c_imneg, d_skip, tab, *, name, rider=None):
    S = proj.shape[0]
    Tc = min(SSM_CHUNK, S)
    nc = S // Tc
    u_blk0 = (3 * SB_WIDTH) // SSM_HALF_CH

    def body(u_ref, bre_ref, bim_ref, cre_ref, cim_ref, d_ref, tab_ref, y_ref, xre_ref, xim_ref, cre_s, cim_s,
             bd_re_s, bd_im_s, cd_re_s, cd_im_s):
        c = pl.program_id(1)

        @pl.when(c == 0)
        def _():
            cre_s[...] = jnp.zeros_like(cre_s)
            cim_s[...] = jnp.zeros_like(cim_s)
            _ssm_expand(bre_ref, bim_ref, cre_ref, cim_ref, bd_re_s, bd_im_s, cd_re_s, cd_im_s)

        u = u_ref[...]
        ub = u.astype(BF16)
        xre_ref[...] = _dot(ub, bd_re_s[...], 1, 0)
        xim_ref[...] = _dot(ub, bd_im_s[...], 1, 0)

        def slab(k, carry):
            car_re, car_im = carry
            rows = pl.ds(pl.multiple_of(k * SUBLANES, SUBLANES), SUBLANES)
            sre = xre_ref[rows, :]
            sim = xim_ref[rows, :]
            for n, d in enumerate((1, 2, 4)):
                pre, pim = tab_ref[0, 2 * n], tab_ref[0, 2 * n + 1]
                rre = pltpu.roll(sre, d, 0)
                rim = pltpu.roll(sim, d, 0)
                sre, sim = sre + (pre * rre - pim * rim), sim + (pre * rim + pim * rre)
            pre, pim = tab_ref[0, 6], tab_ref[0, 7]
            sre, sim = sre + (pre * car_re - pim * car_im), sim + (pre * car_im + pim * car_re)
            xre_ref[rows, :] = sre
            xim_ref[rows, :] = sim
            last = (SUBLANES - 1, SUBLANES)
            return (jnp.broadcast_to(sre[last[0]:last[1], :], sre.shape),
                    jnp.broadcast_to(sim[last[0]:last[1], :], sim.shape))

        car = lax.fori_loop(0, Tc // SUBLANES, slab, (cre_s[...], cim_s[...]))
        cre_s[...] = car[0]
        cim_s[...] = car[1]
        y = _dot(xre_ref[...], cd_re_s[...], 1, 0) + _dot(xim_ref[...], cd_im_s[...], 1, 0)
        y_ref[...] = y + d_ref[...] * u

    return _call(
        body, name=name, rider=rider,
        out_shape=(jax.ShapeDtypeStruct((S, SSM_WIDTH), F32),
                   jax.ShapeDtypeStruct((S, SSM_HALVES * SSM_HALF_ST), F32),
                   jax.ShapeDtypeStruct((S, SSM_HALVES * SSM_HALF_ST), F32)),
        grid=(SSM_HALVES, nc),
        in_specs=[pl.BlockSpec((Tc, SSM_HALF_CH), lambda h, c: (c, u_blk0 + h))] + _ssm_matrix_specs()
                 + [pl.BlockSpec((1, SSM_HALF_CH), lambda h, c: (0, h)),
                    pl.BlockSpec((1, 8, SUBLANES, SSM_HALF_ST), lambda h, c: (h, 0, 0, 0))],
        out_specs=(pl.BlockSpec((Tc, SSM_HALF_CH), lambda h, c: (c, h)),
                   pl.BlockSpec((Tc, SSM_HALF_ST), lambda h, c: (c, h)),
                   pl.BlockSpec((Tc, SSM_HALF_ST), lambda h, c: (c, h))),
        scratch_shapes=[pltpu.VMEM((SUBLANES, SSM_HALF_ST), F32), pltpu.VMEM((SUBLANES, SSM_HALF_ST), F32)]
                       + _ssm_matrix_scratch(),
        operands=(proj, b_re, b_im, c_re, c_imneg, d_skip, tab))


def _ssm_bwd(dy, proj, x_re, x_im, b_re, b_im, c_re, c_imneg, d_skip, tab, *, name, rider=None):
    S = proj.shape[0]
    Tc = min(SSM_CHUNK, S)
    nc = S // Tc
    u_blk0 = (3 * SB_WIDTH) // SSM_HALF_CH

    def body(dy_ref, u_ref, xre_ref, xim_ref, bre_ref, bim_ref, cre_ref, cim_ref, d_ref, tab_ref,
             du_ref, dbre_ref, dbim_ref, dcre_ref, dcim_ref, dd_ref, dlre_ref, dlim_ref,
             gre_s, gim_s, cre_s, cim_s, bd_re_s, bd_im_s, cd_re_s, cd_im_s):
        c = pl.program_id(1)

        @pl.when(c == 0)
        def _():
            _ssm_expand(bre_ref, bim_ref, cre_ref, cim_ref, bd_re_s, bd_im_s, cd_re_s, cd_im_s)
            cre_s[...] = jnp.zeros_like(cre_s)
            cim_s[...] = jnp.zeros_like(cim_s)
            dbre_ref[...] = jnp.zeros_like(dbre_ref)
            dbim_ref[...] = jnp.zeros_like(dbim_ref)
            dcre_ref[...] = jnp.zeros_like(dcre_ref)
            dcim_ref[...] = jnp.zeros_like(dcim_ref)
            dd_ref[...] = jnp.zeros_like(dd_ref)
            dlre_ref[...] = jnp.zeros_like(dlre_ref)
            dlim_ref[...] = jnp.zeros_like(dlim_ref)

        dy = dy_ref[...]
        dyb = dy.astype(BF16)
        u = u_ref[...]
        gre_s[...] = _dot(dyb, cd_re_s[...], 1, 1)
        gim_s[...] = _dot(dyb, cd_im_s[...], 1, 1)
        row = lax.broadcasted_iota(jnp.int32, (SUBLANES, SSM_HALF_ST), 0)
        nslab = Tc // SUBLANES

        def slab(kk, carry):
            car_re, car_im, acc_re, acc_im = carry
            k = nslab - 1 - kk
            rows = pl.ds(pl.multiple_of(k * SUBLANES, SUBLANES), SUBLANES)
            sre = gre_s[rows, :]
            sim = gim_s[rows, :]
            for n, d in enumerate((1, 2, 4)):
                pre, pim = tab_ref[0, 2 * n], tab_ref[0, 2 * n + 1]
                rre = pltpu.roll(sre, SUBLANES - d, 0)
                rim = pltpu.roll(sim, SUBLANES - d, 0)
                sre, sim = sre + (pre * rre + pim * rim), sim + (pre * rim - pim * rre)
            pre, pim = tab_ref[0, 6], tab_ref[0, 7]
            sre, sim = sre + (pre * car_re + pim * car_im), sim + (pre * car_im - pim * car_re)
            gre_s[rows, :] = sre
            gim_s[rows, :] = sim
            nre = jnp.where(row == SUBLANES - 1, car_re, pltpu.roll(sre, SUBLANES - 1, 0))
            nim = jnp.where(row == SUBLANES - 1, car_im, pltpu.roll(sim, SUBLANES - 1, 0))
            xr = xre_ref[rows, :]
            xi = xim_ref[rows, :]
            acc_re = acc_re + (nre * xr + nim * xi)
            acc_im = acc_im + (nim * xr - nre * xi)
            return (jnp.broadcast_to(sre[0:1, :], sre.shape), jnp.broadcast_to(sim[0:1, :], sim.shape), acc_re, acc_im)

        car = lax.fori_loop(0, nslab, slab, (cre_s[...], cim_s[...], dlre_ref[0], dlim_ref[0]))
        cre_s[...] = car[0]
        cim_s[...] = car[1]
        dlre_ref[0] = car[2]
        dlim_ref[0] = car[3]
        gre = gre_s[...].astype(BF16)
        gim = gim_s[...].astype(BF16)
        ub = u.astype(BF16)
        du = _dot(gre, bd_re_s[...], 1, 1) + _dot(gim, bd_im_s[...], 1, 1) + d_ref[...] * dy
        du_ref[...] = du.astype(BF16)
        dbre_ref[0] += _dot(ub, gre, 0, 0)
        dbim_ref[0] += _dot(ub, gim, 0, 0)
        dcre_ref[0] += _dot(xre_ref[...], dyb, 0, 0)
        dcim_ref[0] += _dot(xim_ref[...], dyb, 0, 0)
        dd_ref[...] += jnp.sum(dy * u, axis=0, keepdims=True)

    rev = lambda c: nc - 1 - c
    return _call(
        body, name=name, rider=rider,
        out_shape=(jax.ShapeDtypeStruct((S, SSM_WIDTH), BF16),
                   jax.ShapeDtypeStruct((SSM_HALVES, SSM_HALF_CH, SSM_HALF_ST), F32),
                   jax.ShapeDtypeStruct((SSM_HALVES, SSM_HALF_CH, SSM_HALF_ST), F32),
                   jax.ShapeDtypeStruct((SSM_HALVES, SSM_HALF_ST, SSM_HALF_CH), F32),
                   jax.ShapeDtypeStruct((SSM_HALVES, SSM_HALF_ST, SSM_HALF_CH), F32),
                   jax.ShapeDtypeStruct((1, SSM_WIDTH), F32),
                   jax.ShapeDtypeStruct((SSM_HALVES, SUBLANES, SSM_HALF_ST), F32),
                   jax.ShapeDtypeStruct((SSM_HALVES, SUBLANES, SSM_HALF_ST), F32)),
        grid=(SSM_HALVES, nc),
        in_specs=[pl.BlockSpec((Tc, SSM_HALF_CH), lambda h, c: (rev(c), h)),
                  pl.BlockSpec((Tc, SSM_HALF_CH), lambda h, c: (rev(c), u_blk0 + h)),
                  pl.BlockSpec((Tc, SSM_HALF_ST), lambda h, c: (rev(c), h)),
                  pl.BlockSpec((Tc, SSM_HALF_ST), lambda h, c: (rev(c), h))] + _ssm_matrix_specs()
                 + [pl.BlockSpec((1, SSM_HALF_CH), lambda h, c: (0, h)),
                    pl.BlockSpec((1, 8, SUBLANES, SSM_HALF_ST), lambda h, c: (h, 0, 0, 0))],
        out_specs=(pl.BlockSpec((Tc, SSM_HALF_CH), lambda h, c: (rev(c), h)),
                   pl.BlockSpec((1, SSM_HALF_CH, SSM_HALF_ST), lambda h, c: (h, 0, 0)),
                   pl.BlockSpec((1, SSM_HALF_CH, SSM_HALF_ST), lambda h, c: (h, 0, 0)),
                   pl.BlockSpec((1, SSM_HALF_ST, SSM_HALF_CH), lambda h, c: (h, 0, 0)),
                   pl.BlockSpec((1, SSM_HALF_ST, SSM_HALF_CH), lambda h, c: (h, 0, 0)),
                   pl.BlockSpec((1, SSM_HALF_CH), lambda h, c: (0, h)),
                   pl.BlockSpec((1, SUBLANES, SSM_HALF_ST), lambda h, c: (h, 0, 0)),
                   pl.BlockSpec((1, SUBLANES, SSM_HALF_ST), lambda h, c: (h, 0, 0))),
        scratch_shapes=[pltpu.VMEM((Tc, SSM_HALF_ST), F32), pltpu.VMEM((Tc, SSM_HALF_ST), F32),
                        pltpu.VMEM((SUBLANES, SSM_HALF_ST), F32), pltpu.VMEM((SUBLANES, SSM_HALF_ST), F32)]
                       + _ssm_matrix_scratch(),
        operands=(dy, proj, x_re, x_im, b_re, b_im, c_re, c_imneg, d_skip, tab))


def _ssm_prepare(a_re, a_im, log_dt, b_re, b_im):
    dt = jnp.exp(log_dt)[:, None]
    mag = jnp.exp(a_re * dt)
    lre = mag * jnp.cos(a_im * dt)
    lim = mag * jnp.sin(a_im * dt)
    den = a_re * a_re + a_im * a_im
    fre = ((lre - 1.0) * a_re + lim * a_im) / den
    fim = (lim * a_re - (lre - 1.0) * a_im) / den
    bbre = fre[:, :, None] * b_re - fim[:, :, None] * b_im
    bbim = fre[:, :, None] * b_im + fim[:, :, None] * b_re
    return lre, lim, bbre, bbim


def _group_eye():
    return jnp.eye(SSM_GROUPS // SSM_HALVES, dtype=F32)


def _b_blocks(bbar):
    gh = SSM_GROUPS // SSM_HALVES
    b = bbar.reshape(SSM_HALVES, gh, SSM_STATE, SSM_GROUP).transpose(0, 1, 3, 2)
    return b.reshape(SSM_HALVES, SSM_HALF_CH, SSM_STATE)


def _bbar_from_bd(dbd):
    gh = SSM_GROUPS // SSM_HALVES
    d = dbd.reshape(SSM_HALVES, gh, SSM_GROUP, gh, SSM_STATE)
    d = jnp.sum(d * _group_eye()[None, :, None, :, None], axis=3)
    return d.transpose(0, 1, 3, 2).reshape(SSM_GROUPS, SSM_STATE, SSM_GROUP)


def _c_blocks(cmat):
    gh = SSM_GROUPS // SSM_HALVES
    c = cmat.reshape(SSM_HALVES, gh, SSM_GROUP, SSM_STATE).transpose(0, 1, 3, 2)
    return c.reshape(SSM_HALVES, SSM_HALF_ST, SSM_GROUP)


def _c_from_cd(dcd):
    gh = SSM_GROUPS // SSM_HALVES
    d = dcd.reshape(SSM_HALVES, gh, SSM_STATE, gh, SSM_GROUP)
    d = jnp.sum(d * _group_eye()[None, :, None, :, None], axis=3)
    return d.transpose(0, 1, 3, 2).reshape(SSM_GROUPS, SSM_GROUP, SSM_STATE)


def _glu_fwd(y_pre, w_glu, b_glu, *, name):
    S, W = y_pre.shape
    tr = _row_tile(S)

    def body(y_ref, w_ref, b_ref, o_ref):
        yg = _gelu(y_ref[...])
        gl = _dot(yg, w_ref[...], 1, 0) + b_ref[...]
        o_ref[...] = (yg * _sigmoid(gl)).astype(BF16)

    row = pl.BlockSpec((tr, W), lambda i: (i, 0))
    return pl.pallas_call(
        body, name=name, out_shape=jax.ShapeDtypeStruct((S, W), BF16), grid=(S // tr,),
        in_specs=[row, pl.BlockSpec((W, W), lambda i: (0, 0)), pl.BlockSpec((1, W), lambda i: (0, 0))],
        out_specs=row, compiler_params=_cparams("parallel"),
    )(y_pre, w_glu, b_glu)


def _glu_bwd(y_pre, do, w_glu, b_glu, *, name):
    S, W = y_pre.shape
    tr = _row_tile(S)

    def body(y_ref, do_ref, w_ref, b_ref, dy_ref, dw_ref, db_ref):
        i = pl.program_id(0)
        yg, dyg_dy = _gelu_and_grad(y_ref[...])
        ygb = yg.astype(BF16)
        sg = _sigmoid(_dot(ygb, w_ref[...], 1, 0) + b_ref[...])
        do = do_ref[...]
        dgl = do * yg * sg * (1.0 - sg)
        dglb = dgl.astype(BF16)
        dyg = do * sg + _dot(dglb, w_ref[...], 1, 1)
        dy_ref[...] = dyg * dyg_dy
        dw = _dot(ygb, dglb, 0, 0)
        db = jnp.sum(dgl, axis=0, keepdims=True)

        @pl.when(i == 0)
        def _():
            dw_ref[...] = dw
            db_ref[...] = db

        @pl.when(i > 0)
        def _():
            dw_ref[...] += dw
            db_ref[...] += db

    row = pl.BlockSpec((tr, W), lambda i: (i, 0))
    full = pl.BlockSpec((W, W), lambda i: (0, 0))
    vec = pl.BlockSpec((1, W), lambda i: (0, 0))
    return pl.pallas_call(
        body, name=name,
        out_shape=(jax.ShapeDtypeStruct((S, W), F32), jax.ShapeDtypeStruct((W, W), F32), jax.ShapeDtypeStruct((1, W), F32)),
        grid=(S // tr,), in_specs=[row, row, full, vec], out_specs=(row, full, vec),
        compiler_params=_cparams("arbitrary"),
    )(y_pre, do, w_glu, b_glu)


GATE_COL0 = 3 * SB_WIDTH + SSM_WIDTH


def _merge_fwd(proj, o_attn, o_ssm, w_ba, w_bs, b_gate, *, name):
    S = proj.shape[0]
    D = D_MODEL
    tr = _pick(S, (256, 128, 64, 32, 16, 8))
    gb = GATE_COL0 // D

    def body(ga_ref, gs_ref, oa_ref, os_ref, wa_ref, ws_ref, ba_ref, bs_ref, m_ref):
        pa = _dot(oa_ref[...], wa_ref[...], 1, 0)
        ps = _dot(os_ref[...], ws_ref[...], 1, 0)
        sa = _sigmoid(ga_ref[...] + ba_ref[...])
        ss = _sigmoid(gs_ref[...] + bs_ref[...])
        m_ref[...] = (sa * pa + ss * ps).astype(BF16)

    return pl.pallas_call(
        body, name=name, out_shape=jax.ShapeDtypeStruct((S, D), BF16), grid=(S // tr,),
        in_specs=[pl.BlockSpec((tr, D), lambda i: (i, gb)), pl.BlockSpec((tr, D), lambda i: (i, gb + 1)),
                  pl.BlockSpec((tr, SB_WIDTH), lambda i: (i, 0)), pl.BlockSpec((tr, SSM_WIDTH), lambda i: (i, 0)),
                  pl.BlockSpec((SB_WIDTH, D), lambda i: (0, 0)), pl.BlockSpec((SSM_WIDTH, D), lambda i: (0, 0)),
                  pl.BlockSpec((1, D), lambda i: (0, 0)), pl.BlockSpec((1, D), lambda i: (0, 1))],
        out_specs=pl.BlockSpec((tr, D), lambda i: (i, 0)),
        compiler_params=_cparams("parallel"),
    )(proj, proj, o_attn, o_ssm, w_ba, w_bs, b_gate, b_gate)


def _merge_bwd(dmerged, proj, o_attn, o_ssm, w_ba, w_bs, b_gate, *, name):
    S = proj.shape[0]
    D = D_MODEL
    tr = _pick(S, (256, 128, 64, 32, 16, 8))
    gb = GATE_COL0 // D

    def body(dm_ref, ga_ref, gs_ref, oa_ref, os_ref, wa_ref, ws_ref, ba_ref, bs_ref,
             doa_ref, dos_ref, dg_ref, db_ref, dwa_ref, dws_ref):
        i = pl.program_id(0)
        dm = dm_ref[...]
        oa = oa_ref[...]
        osm = os_ref[...]
        pa = _dot(oa, wa_ref[...], 1, 0)
        ps = _dot(osm, ws_ref[...], 1, 0)
        sa = _sigmoid(ga_ref[...] + ba_ref[...])
        ss = _sigmoid(gs_ref[...] + bs_ref[...])
        dpa = (dm * sa).astype(BF16)
        dps = (dm * ss).astype(BF16)
        dga = dm * pa * sa * (1.0 - sa)
        dgs = dm * ps * ss * (1.0 - ss)
        dg_ref[:, :D] = dga.astype(BF16)
        dg_ref[:, D:] = dgs.astype(BF16)
        doa_ref[...] = _dot(dpa, wa_ref[...], 1, 1).astype(BF16)
        dos_ref[...] = _dot(dps, ws_ref[...], 1, 1)
        dwa = _dot(oa, dpa, 0, 0)
        dws = _dot(osm, dps, 0, 0)
        dba = jnp.sum(dga, axis=0, keepdims=True)
        dbs = jnp.sum(dgs, axis=0, keepdims=True)

        @pl.when(i == 0)
        def _():
            dwa_ref[...] = dwa
            dws_ref[...] = dws
            db_ref[:, :D] = dba
            db_ref[:, D:] = dbs

        @pl.when(i > 0)
        def _():
            dwa_ref[...] += dwa
            dws_ref[...] += dws
            db_ref[:, :D] += dba
            db_ref[:, D:] += dbs

    rowD = pl.BlockSpec((tr, D), lambda i: (i, 0))
    wspec = pl.BlockSpec((SB_WIDTH, D), lambda i: (0, 0))
    return pl.pallas_call(
        body, name=name,
        out_shape=(jax.ShapeDtypeStruct((S, SB_WIDTH), BF16), jax.ShapeDtypeStruct((S, SSM_WIDTH), F32),
                   jax.ShapeDtypeStruct((S, 2 * D), BF16), jax.ShapeDtypeStruct((1, 2 * D), F32),
                   jax.ShapeDtypeStruct((SB_WIDTH, D), F32), jax.ShapeDtypeStruct((SSM_WIDTH, D), F32)),
        grid=(S // tr,),
        in_specs=[rowD, pl.BlockSpec((tr, D), lambda i: (i, gb)), pl.BlockSpec((tr, D), lambda i: (i, gb + 1)),
                  pl.BlockSpec((tr, SB_WIDTH), lambda i: (i, 0)), pl.BlockSpec((tr, SSM_WIDTH), lambda i: (i, 0)),
                  wspec, wspec, pl.BlockSpec((1, D), lambda i: (0, 0)), pl.BlockSpec((1, D), lambda i: (0, 1))],
        out_specs=(pl.BlockSpec((tr, SB_WIDTH), lambda i: (i, 0)), pl.BlockSpec((tr, SSM_WIDTH), lambda i: (i, 0)),
                   pl.BlockSpec((tr, 2 * D), lambda i: (i, 0)), pl.BlockSpec((1, 2 * D), lambda i: (0, 0)),
                   wspec, wspec),
        compiler_params=_cparams("arbitrary"),
    )(dmerged, proj, proj, o_attn, o_ssm, w_ba, w_bs, b_gate, b_gate)


def _xattn_probs(q, k, h):
    cols = slice(h * XA_HEAD_DIM, (h + 1) * XA_HEAD_DIM)
    s = _dot(q[:, cols], k[:, cols], 1, 1) * (XA_HEAD_DIM ** -0.5)
    s = s - jnp.max(s, axis=-1, keepdims=True)
    e = jnp.exp(s)
    return e / jnp.sum(e, axis=-1, keepdims=True), cols


def _xattn_fwd(q2, k2, v2, *, name):
    S, D = q2.shape
    M = k2.shape[0]
    tr = _row_tile(S)

    def body(q_ref, k_ref, v_ref, o_ref):
        q = q_ref[...]
        k = k_ref[...]
        v = v_ref[...]
        for h in range(XA_HEADS):
            p, cols = _xattn_probs(q, k, h)
            o_ref[:, cols] = _dot(p, v[:, cols], 1, 0).astype(BF16)

    row = pl.BlockSpec((tr, D), lambda i: (i, 0))
    memb = pl.BlockSpec((M, D), lambda i: (0, 0))
    return pl.pallas_call(
        body, name=name, out_shape=jax.ShapeDtypeStruct((S, D), BF16), grid=(S // tr,),
        in_specs=[row, memb, memb], out_specs=row, compiler_params=_cparams("parallel"),
    )(q2, k2, v2)


def _xattn_bwd(q2, k2, v2, do2, *, name):
    S, D = q2.shape
    M = k2.shape[0]
    tr = _row_tile(S)
    scale = XA_HEAD_DIM ** -0.5

    def body(q_ref, k_ref, v_ref, do_ref, dq_ref, dk_ref, dv_ref):
        i = pl.program_id(0)

        @pl.when(i == 0)
        def _():
            dk_ref[...] = jnp.zeros_like(dk_ref)
            dv_ref[...] = jnp.zeros_like(dv_ref)

        q = q_ref[...]
        k = k_ref[...]
        v = v_ref[...]
        do = do_ref[...]
        for h in range(XA_HEADS):
            p, cols = _xattn_probs(q, k, h)
            dp = _dot(do[:, cols], v[:, cols], 1, 1)
            ds = (p * (dp - jnp.sum(dp * p, axis=-1, keepdims=True)) * scale).astype(BF16)
            dq_ref[:, cols] = _dot(ds, k[:, cols], 1, 0).astype(BF16)
            dk_ref[:, cols] += _dot(ds, q[:, cols], 0, 0)
            dv_ref[:, cols] += _dot(p, do[:, cols], 0, 0)

    row = pl.BlockSpec((tr, D), lambda i: (i, 0))
    memb = pl.BlockSpec((M, D), lambda i: (0, 0))
    return pl.pallas_call(
        body, name=name,
        out_shape=(jax.ShapeDtypeStruct((S, D), BF16), jax.ShapeDtypeStruct((M, D), F32), jax.ShapeDtypeStruct((M, D), F32)),
        grid=(S // tr,), in_specs=[row, memb, memb, row], out_specs=(row, memb, memb),
        compiler_params=_cparams("arbitrary"),
    )(q2, k2, v2, do2)


CONV_ROWS = 64
CONV_ROWS_FWD = 256


def _chunk(ref, c, rows):
    return ref[pl.ds(pl.multiple_of(c * rows, rows), rows), :]


def _rows_before(ref, c, rows):
    t0 = pl.multiple_of(jnp.maximum(c * rows - SUBLANES, 0), SUBLANES)
    return jnp.where(c > 0, ref[pl.ds(t0, SUBLANES), :], 0.0)


def _rows_after(ref, c, rows, n_chunks):
    t0 = pl.multiple_of(jnp.minimum((c + 1) * rows, n_chunks * rows - SUBLANES), SUBLANES)
    return jnp.where(c < n_chunks - 1, ref[pl.ds(t0, SUBLANES), :], 0.0)


def _shift_down(cur, before, d):
    out = pltpu.roll(cur, d, 0)
    r = lax.broadcasted_iota(jnp.int32, cur.shape, 0)
    for e in range(d):
        out = jnp.where(r == e, before[SUBLANES - d + e:SUBLANES - d + e + 1, :], out)
    return out


def _shift_up(cur, after, d):
    rows = cur.shape[0]
    out = pltpu.roll(cur, rows - d, 0)
    r = lax.broadcasted_iota(jnp.int32, cur.shape, 0)
    for e in range(d):
        out = jnp.where(r == rows - d + e, after[e:e + 1, :], out)
    return out


def _conv3(cur, before, w_ref, b_ref):
    return (w_ref[2:3, :] * cur + w_ref[1:2, :] * _shift_down(cur, before, 1)
            + w_ref[0:1, :] * _shift_down(cur, before, 2) + b_ref[...])


def _convgate_fwd(up_g, up_v, conv_w, conv_b, *, name):
    S, H = up_g.shape
    nb = H // LANES
    R = min(CONV_ROWS_FWD, S)
    n_chunks = S // R

    def body(g_ref, v_ref, wg_ref, wv_ref, bg_ref, bv_ref, a_ref):
        def chunk(c, _):
            cg = _conv3(_chunk(g_ref, c, R), _rows_before(g_ref, c, R), wg_ref, bg_ref)
            cv = _conv3(_chunk(v_ref, c, R), _rows_before(v_ref, c, R), wv_ref, bv_ref)
            a_ref[pl.ds(pl.multiple_of(c * R, R), R), :] = (_gelu(cg) * cv).astype(BF16)
            return 0

        lax.fori_loop(0, n_chunks, chunk, 0)

    col = lambda off: pl.BlockSpec((S, LANES), lambda j: (0, off + j))
    wcol = lambda off: pl.BlockSpec((3, LANES), lambda j: (0, off + j))
    bcol = lambda off: pl.BlockSpec((1, LANES), lambda j: (0, off + j))
    return pl.pallas_call(
        body, name=name, out_shape=jax.ShapeDtypeStruct((S, H), BF16), grid=(nb,),
        in_specs=[col(0), col(0), wcol(0), wcol(nb), bcol(0), bcol(nb)],
        out_specs=col(0), compiler_params=_cparams("parallel"),
    )(up_g, up_v, conv_w, conv_w, conv_b, conv_b)


def _convgate_bwd(up_g, up_v, da, conv_w, conv_b, *, name):
    S, H = up_g.shape
    nb = H // LANES
    R = min(CONV_ROWS, S)
    n_chunks = S // R

    def fold(a):
        return sum(a[r:r + SUBLANES] for r in range(0, a.shape[0], SUBLANES))

    def body(g_ref, v_ref, da_ref, wg_ref, wv_ref, bg_ref, bv_ref,
             dug_ref, duv_ref, dwg_ref, dwv_ref, dbg_ref, dbv_ref, dcg_s, dcv_s):
        def first_pass(c, acc):
            rows = pl.ds(pl.multiple_of(c * R, R), R)
            ug, uv = _chunk(g_ref, c, R), _chunk(v_ref, c, R)
            bg, bv = _rows_before(g_ref, c, R), _rows_before(v_ref, c, R)
            cg = _conv3(ug, bg, wg_ref, bg_ref)
            cv = _conv3(uv, bv, wv_ref, bv_ref)
            da = da_ref[rows, :]
            gl, dgl = _gelu_and_grad(cg)
            dcg = da * cv * dgl
            dcv = da * gl
            dcg_s[rows, :] = dcg
            dcv_s[rows, :] = dcv
            new = []
            for dc, u, before in ((dcg, ug, bg), (dcv, uv, bv)):
                new += [fold(dc * _shift_down(u, before, 2)), fold(dc * _shift_down(u, before, 1)), fold(dc * u), fold(dc)]
            return tuple(a + n for a, n in zip(acc, new))

        zero = jnp.zeros((SUBLANES, LANES), F32)
        acc = lax.fori_loop(0, n_chunks, first_pass, (zero,) * 8)
        total = [jnp.sum(a, axis=0, keepdims=True) for a in acc]
        for k, (dw_ref, db_ref) in enumerate(((dwg_ref, dbg_ref), (dwv_ref, dbv_ref))):
            dw_ref[0:1, :] = total[4 * k]
            dw_ref[1:2, :] = total[4 * k + 1]
            dw_ref[2:3, :] = total[4 * k + 2]
            db_ref[...] = total[4 * k + 3]

        def second_pass(c, _):
            rows = pl.ds(pl.multiple_of(c * R, R), R)
            for dc_s, w_ref, du_ref in ((dcg_s, wg_ref, dug_ref), (dcv_s, wv_ref, duv_ref)):
                cur, after = _chunk(dc_s, c, R), _rows_after(dc_s, c, R, n_chunks)
                du = w_ref[2:3, :] * cur + w_ref[1:2, :] * _shift_up(cur, after, 1) + w_ref[0:1, :] * _shift_up(cur, after, 2)
                du_ref[rows, :] = du.astype(BF16)
            return 0

        lax.fori_loop(0, n_chunks, second_pass, 0)

    col = lambda off: pl.BlockSpec((S, LANES), lambda j: (0, off + j))
    wcol = lambda off: pl.BlockSpec((3, LANES), lambda j: (0, off + j))
    bcol = lambda off: pl.BlockSpec((1, LANES), lambda j: (0, off + j))
    return pl.pallas_call(
        body, name=name,
        out_shape=(jax.ShapeDtypeStruct((S, H), BF16), jax.ShapeDtypeStruct((S, H), BF16),
                   jax.ShapeDtypeStruct((3, H), F32), jax.ShapeDtypeStruct((3, H), F32),
                   jax.ShapeDtypeStruct((1, H), F32), jax.ShapeDtypeStruct((1, H), F32)),
        grid=(nb,),
        in_specs=[col(0), col(0), col(0), wcol(0), wcol(nb), bcol(0), bcol(nb)],
        out_specs=(col(0), col(0), wcol(0), wcol(0), bcol(0), bcol(0)),
        scratch_shapes=[pltpu.VMEM((S, LANES), F32), pltpu.VMEM((S, LANES), F32)],
        compiler_params=_cparams("parallel"),
    )(up_g, up_v, da, conv_w, conv_w, conv_b, conv_b)


def _local_step(x, mem, target, w_in, late_wire, P, core):
    mm = _matmul
    h1, (w_in,) = _rms_fwd(x, P["norm_mix_pre"], name="rms_mix_pre", rider=_fill_xy([w_in]))
    w_in, = _fill_c([w_in]).run(name="gather_in_c")
    w_in = w_in.reshape((N_DEV,) + w_in.shape[2:])
    n_mid = len(LATE) - len(REDUCE_FFN)
    proj, wire_mid = mm(h1, w_in, name="mm_in", rider=_fill_xy(late_wire[:n_mid]))
    (o_attn, sb_tot, sb_first), wires = _sb_fwd(
        proj, name="sb_fwd", rider=_Exchange.join(_fill_c(wire_mid), _fill_xy(late_wire[n_mid:])))
    wire_mid, wire_ffn = wires[:n_mid], wires[n_mid:]

    ssm_prep = lambda *a: _ssm_prepare(*a)
    (lam_re, lam_im, bb_re, bb_im), prep_vjp = jax.vjp(
        ssm_prep, P["ssm_a_re"], P["ssm_a_im"], P["ssm_log_dt"], P["ssm_b_re"], P["ssm_b_im"])
    tab_f, tab_b = _ssm_tables(lam_re, lam_im)
    bd_re, bd_im = _b_blocks(bb_re), _b_blocks(bb_im)
    cd_re, cd_imneg = _c_blocks(P["ssm_c_re"]), _c_blocks(-P["ssm_c_im"])
    (y_pre, x_re, x_im), wire_ffn = _ssm_fwd(proj, bd_re, bd_im, cd_re, cd_imneg, P["ssm_d"], tab_f,
                                             name="ssm_fwd", rider=_fill_c(wire_ffn))
    W = _weights_from_wire(dict(zip(LATE, list(wire_mid) + list(wire_ffn))))
    W["w_in"] = w_in
    o_ssm = _glu_fwd(y_pre, W["ssm_w_glu"], P["ssm_b_glu"], name="glu_fwd")

    merged = _merge_fwd(proj, o_attn, o_ssm, W["w_branch_attn"], W["w_branch_ssm"], P["b_gate"], name="merge_fwd")
    mo = mm(merged, W["w_out"], name="mm_out")
    x1, h2 = _resnorm_norm(x, mo, P["norm_mix_post"], P["norm_xa_pre"], name="resnorm_1")

    mem_n = _rms_fwd(mem, P["norm_mem"], name="rms_mem")
    q2 = mm(h2, W["xa_wq"], out_dtype=BF16, name="mm_xq")
    k2 = mm(mem_n, W["xa_wk"], out_dtype=BF16, name="mm_xk")
    v2 = mm(mem_n, W["xa_wv"], out_dtype=BF16, name="mm_xv")
    o2 = _xattn_fwd(q2, k2, v2, name="xattn_fwd")
    xa = mm(o2, W["xa_wo"], name="mm_xo")
    x2, h3 = _resnorm_norm(x1, xa, P["norm_xa_post"], P["norm_ffn_pre"], name="resnorm_2")

    half = N_DEV // 2
    up_g = mm(h3, W["ffn_w_up"], n_blocks=half, name="mm_up_g")
    up_v = mm(h3, W["ffn_w_up"], b_block0=half, name="mm_up_v")
    act = _convgate_fwd(up_g, up_v, W["ffn_conv_w"], P["ffn_conv_b"], name="convgate_fwd")
    f = mm(act, W["ffn_w_down"], name="mm_down")
    loss, dy, df, dg_ffn_post = _final_loss(x2, f, P["norm_ffn_post"], target, name="final_loss")

    G = {"norm_ffn_post": dg_ffn_post}
    dact = mm(df, W["ffn_w_down"], tb=True, name="mm_down_dx")
    G["ffn_w_down"] = mm(act, df, ta=True, name="mm_down_dw")
    dug, duv, dwg, dwv, dbg, dbv = _convgate_bwd(up_g, up_v, dact, W["ffn_conv_w"], P["ffn_conv_b"], name="convgate_bwd")
    G["ffn_conv_w"] = jnp.concatenate([dwg, dwv], axis=1)
    G["ffn_conv_b"] = jnp.concatenate([dbg, dbv], axis=1)
    dh3 = mm(dug, W["ffn_w_up"], tb=True, n_blocks=half, name="mm_up_g_dx")
    dh3 = mm(duv, W["ffn_w_up"], tb=True, b_block0=half, acc_in=dh3, name="mm_up_v_dx")
    dw_up = mm(h3, dug, ta=True, out_into=lax.empty(W["ffn_w_up"].shape, F32), name="mm_up_g_dw")
    G["ffn_w_up"] = mm(h3, duv, ta=True, out_into=dw_up, out_block0=half, name="mm_up_v_dw")
    blocks = {n: _grad_blocks(n, G[n]) for n in REDUCE_FFN}
    (dx2, dxa, G["norm_ffn_pre"], G["norm_xa_post"]), from_core = _norm_bwd_pair(
        dy, dh3, x2, P["norm_ffn_pre"], xa, P["norm_xa_post"], name="norm_bwd_3",
        rider=_send_c([blocks[n] for n in REDUCE_FFN]))
    pair = {n: _pair_sum(blocks[n], r, core, name="pair_sum_" + n) for n, r in zip(REDUCE_FFN, from_core)}

    G["xa_wo"] = mm(o2, dxa, ta=True, name="mm_xo_dw")
    do2 = mm(dxa, W["xa_wo"], tb=True, out_dtype=BF16, name="mm_xo_dx")
    dq2, dk2, dv2 = _xattn_bwd(q2, k2, v2, do2, name="xattn_bwd")
    G["xa_wq"] = mm(h2, dq2, ta=True, name="mm_xq_dw")
    dh2 = mm(dq2, W["xa_wq"], tb=True, name="mm_xq_dx")
    G["xa_wk"] = mm(mem_n, dk2, ta=True, name="mm_xk_dw")
    G["xa_wv"] = mm(mem_n, dv2, ta=True, name="mm_xv_dw")
    dmem_n = jnp.concatenate([dk2, dv2], axis=1)
    wkv = jnp.concatenate([W["xa_wk"], W["xa_wv"]], axis=1)
    dmem = mm(dmem_n, wkv, tb=True, name="mm_xkv_dx")
    _, G["norm_mem"] = _norm_bwd_single(None, dmem, mem, P["norm_mem"], name="norm_bwd_mem")
    (dx1, dmo, G["norm_xa_pre"], G["norm_mix_post"]), _ = _norm_bwd_pair(
        dx2, dh2, x1, P["norm_xa_pre"], mo, P["norm_mix_post"], name="norm_bwd_2")

    G["w_out"] = mm(merged, dmo, ta=True, name="mm_out_dw")
    dmerged = mm(dmo, W["w_out"], tb=True, name="mm_out_dx")
    do_attn, do_ssm, dgate, G["b_gate"], G["w_branch_attn"], G["w_branch_ssm"] = _merge_bwd(
        dmerged, proj, o_attn, o_ssm, W["w_branch_attn"], W["w_branch_ssm"], P["b_gate"], name="merge_bwd")
    dy_pre, G["ssm_w_glu"], G["ssm_b_glu"] = _glu_bwd(y_pre, do_ssm, W["ssm_w_glu"], P["ssm_b_glu"], name="glu_bwd")
    blocks.update({n: _grad_blocks(n, G[n]) for n in REDUCE_MID})
    (du, dbd_re, dbd_im, dcd_re, dcd_imneg, G["ssm_d"], dl_re, dl_im), brought = _ssm_bwd(
        dy_pre, proj, x_re, x_im, bd_re, bd_im, cd_re, cd_imneg, P["ssm_d"], tab_b, name="ssm_bwd",
        rider=_Exchange.join(_send_c([blocks[n] for n in REDUCE_MID]), _scatter_xy([pair[n] for n in REDUCE_FFN])))
    from_core, from_chips = brought[:len(REDUCE_MID)], brought[len(REDUCE_MID):]
    reduced = {n: (pair[n], parts) for n, parts in zip(REDUCE_FFN, from_chips)}
    pair.update({n: _pair_sum(blocks[n], r, core, name="pair_sum_" + n) for n, r in zip(REDUCE_MID, from_core)})
    G["ssm_c_re"] = _c_from_cd(dcd_re)
    G["ssm_c_im"] = -_c_from_cd(dcd_imneg)
    dlam_re = jnp.sum(dl_re, axis=1).reshape(SSM_GROUPS, SSM_STATE)
    dlam_im = jnp.sum(dl_im, axis=1).reshape(SSM_GROUPS, SSM_STATE)
    (G["ssm_a_re"], G["ssm_a_im"], G["ssm_log_dt"], G["ssm_b_re"], G["ssm_b_im"]) = prep_vjp(
        (dlam_re, dlam_im, _bbar_from_bd(dbd_re), _bbar_from_bd(dbd_im)))
    G["ffn_conv_b"] = G["ffn_conv_b"].reshape(N_DEV, FF_LOCAL_PAD)[:, :FF_LOCAL]
    small = [G[n].reshape(SMALL_SHAPE[n]) for n in SMALL_EARLY]
    (dq, dk, dv), brought = _sb_bwd(
        proj, sb_tot, sb_first, do_attn, name="sb_bwd",
        rider=_Exchange.join(_scatter_xy([pair[n] for n in REDUCE_MID]), _gather_xy_from(small)))
    from_chips, small = brought[:len(REDUCE_MID)], brought[len(REDUCE_MID):]
    reduced.update({n: (pair[n], parts) for n, parts in zip(REDUCE_MID, from_chips)})
    dproj = jnp.concatenate([dq, dk, dv, du, dgate], axis=1)
    G["w_in"], small = mm(h1, dproj, ta=True, out_cb=W["w_in"].shape[2], name="mm_in_dw", rider=_fill_c(small))
    g_in = _grad_blocks("w_in", G["w_in"])
    dh1, (from_core,) = mm(dproj, W["w_in"], tb=True, name="mm_in_dx", rider=_send_c([g_in]))
    pair_in = _pair_sum(g_in, from_core, core, name="pair_sum_w_in")
    (grad_x, dg_pre), (from_chips,) = _norm_bwd_single(dx1, dh1, x, P["norm_mix_pre"], name="norm_bwd_1",
                                                       rider=_scatter_xy([pair_in]))
    reduced["w_in"] = (pair_in, from_chips)
    last, = _gather_all([dg_pre]).run(name="gather_g_last")
    parts = dict(zip(SMALL_EARLY, small))
    parts["norm_mix_pre"] = last
    return loss, grad_x, parts, reduced


MESH = pl.DeviceIdType.MESH
_HBM = pl.BlockSpec(memory_space=pl.ANY)
N_XY = 4
N_XY_PEERS = 3


def _xy_peers(x, y):
    return [(1 - x, y), (x, 1 - y), (1 - x, 1 - y)]


class _Exchange:
    def __init__(self, arrays, out_shapes, plan, n_copies, alias):
        self.arrays = list(arrays)
        self.out_shapes = list(out_shapes)
        self.plan = plan
        self.n_copies = n_copies
        self.alias = list(alias) if isinstance(alias, (list, tuple)) else [alias] * len(self.arrays)

    @property
    def n(self):
        return len(self.arrays)

    def aliases(self, first_in, first_out):
        return {first_in + k: first_out + k for k in range(self.n) if self.alias[k]}

    @staticmethod
    def join(a, b):
        def plan(k, src, dst, x, y, c):
            return a.plan(k, src, dst, x, y, c) if k < a.n else b.plan(k - a.n, src, dst, x, y, c)

        return _Exchange(a.arrays + b.arrays, a.out_shapes + b.out_shapes, plan, max(a.n_copies, b.n_copies),
                         a.alias + b.alias)

    def sems(self):
        shape = (self.n, self.n_copies)
        return [pltpu.SemaphoreType.DMA(shape), pltpu.SemaphoreType.DMA(shape)]

    def _copies(self, ins, outs, send_sems, recv_sems):
        x, y, c = lax.axis_index("x"), lax.axis_index("y"), lax.axis_index("c")
        sends, lands, own = [], [], []
        for k in range(self.n):
            for j, (src, dst, dev, land) in enumerate(self.plan(k, ins[k], outs[k], x, y, c)):
                if dev is None:
                    own.append(pltpu.make_async_copy(src, dst, send_sems.at[k, j]))
                    continue
                sems = dict(send_sem=send_sems.at[k, j], recv_sem=recv_sems.at[k, j], device_id=dev, device_id_type=MESH)
                sends.append(pltpu.make_async_remote_copy(src_ref=src, dst_ref=dst, **sems))
                lands.append(pltpu.make_async_remote_copy(src_ref=src, dst_ref=land, **sems))
        return sends, lands, own

    def start(self, ins, outs, send_sems, recv_sems):
        sends, _, own = self._copies(ins, outs, send_sems, recv_sems)
        for cp in own + sends:
            cp.start()

    def finish(self, ins, outs, send_sems, recv_sems):
        sends, lands, own = self._copies(ins, outs, send_sems, recv_sems)
        for cp in lands:
            cp.wait_recv()
        for cp in sends:
            cp.wait_send()
        for cp in own:
            cp.wait()

    def run(self, *, name):
        n = self.n

        def body(*refs):
            parts = (refs[:n], refs[n:2 * n], refs[2 * n], refs[2 * n + 1])
            self.start(*parts)
            self.finish(*parts)

        return pl.pallas_call(
            body, name=name, out_shape=tuple(self.out_shapes),
            in_specs=[_HBM] * n, out_specs=tuple([_HBM] * n),
            input_output_aliases=self.aliases(0, 0),
            scratch_shapes=self.sems(),
        )(*self.arrays)


def _call(host_body, *, name, grid, in_specs, out_specs, out_shape, scratch_shapes, operands, rider=None):
    out_specs, out_shape = tuple(out_specs), tuple(out_shape)
    if rider is None:
        res = pl.pallas_call(
            host_body, name=name, grid=grid, in_specs=list(in_specs), out_specs=out_specs, out_shape=out_shape,
            scratch_shapes=list(scratch_shapes), compiler_params=_cparams(*["arbitrary"] * len(grid)),
        )(*operands)
        return tuple(res), None
    n, n_in, n_out, n_scr = rider.n, len(in_specs), len(out_specs), len(scratch_shapes)

    def body(*refs):
        pos = [0]

        def take(count):
            pos[0] += count
            return refs[pos[0] - count:pos[0]]

        h_in, r_in, h_out, r_out, h_scr = take(n_in), take(n), take(n_out), take(n), take(n_scr)
        send_sems, recv_sems = take(2)
        ids = [pl.program_id(a) for a in range(len(grid))]
        first = functools.reduce(jnp.logical_and, [i == 0 for i in ids])
        last = functools.reduce(jnp.logical_and, [i == g - 1 for i, g in zip(ids, grid)])

        @pl.when(first)
        def _():
            rider.start(r_in, r_out, send_sems, recv_sems)

        host_body(*h_in, *h_out, *h_scr)

        @pl.when(last)
        def _():
            rider.finish(r_in, r_out, send_sems, recv_sems)

    res = pl.pallas_call(
        body, name=name, grid=grid,
        in_specs=list(in_specs) + [_HBM] * n, out_specs=out_specs + tuple([_HBM] * n),
        out_shape=out_shape + tuple(rider.out_shapes),
        input_output_aliases=rider.aliases(n_in, n_out),
        scratch_shapes=list(scratch_shapes) + rider.sems(),
        compiler_params=_cparams(*["arbitrary"] * len(grid)),
    )(*operands, *rider.arrays)
    return tuple(res[:n_out]), list(res[n_out:])


def _same(arrays):
    return [jax.ShapeDtypeStruct(a.shape, a.dtype) for a in arrays]


def _fill_xy(bufs):
    def plan(k, src, dst, x, y, c):
        mine = 2 * x + y
        return [(src.at[mine, c], dst.at[mine, c], (px, py, c), dst.at[2 * px + py, c]) for px, py in _xy_peers(x, y)]

    return _Exchange(bufs, _same(bufs), plan, N_XY_PEERS, alias=True)


def _fill_c(bufs):
    def plan(k, src, dst, x, y, c):
        return [(src.at[:, c], dst.at[:, c], (x, y, 1 - c), dst.at[:, 1 - c])]

    return _Exchange(bufs, _same(bufs), plan, 1, alias=True)


def _slots(arrays):
    return [jax.ShapeDtypeStruct((N_XY, 2) + a.shape, a.dtype) for a in arrays]


def _gather_xy_from(srcs):
    def plan(k, src, dst, x, y, c):
        mine = 2 * x + y
        return ([(src, dst.at[mine, c], None, None)]
                + [(src, dst.at[mine, c], (px, py, c), dst.at[2 * px + py, c]) for px, py in _xy_peers(x, y)])

    return _Exchange(srcs, _slots(srcs), plan, 1 + N_XY_PEERS, alias=False)


def _gather_all(srcs):
    def plan(k, src, dst, x, y, c):
        mine = 2 * x + y
        out = [(src, dst.at[mine, c], None, None)]
        for fx, fy, fc in [(a, b, e) for a in (0, 1) for b in (0, 1) for e in (0, 1)][1:]:
            px, py, pc = (1 - x) if fx else x, (1 - y) if fy else y, (1 - c) if fc else c
            out.append((src, dst.at[mine, c], (px, py, pc), dst.at[2 * px + py, pc]))
        return out

    return _Exchange(srcs, _slots(srcs), plan, N_DEV, alias=False)


def _send_c(srcs):
    def plan(k, src, dst, x, y, c):
        return [(src.at[:, 1 - c], dst, (x, y, 1 - c), dst)]

    outs = [jax.ShapeDtypeStruct(a.shape[:1] + a.shape[2:], a.dtype) for a in srcs]
    return _Exchange(srcs, outs, plan, 1, alias=False)


def _scatter_xy(srcs):
    def plan(k, src, dst, x, y, c):
        return [(src.at[2 * px + py], dst.at[j], (px, py, c), dst.at[j]) for j, (px, py) in enumerate(_xy_peers(x, y))]

    outs = [jax.ShapeDtypeStruct((N_XY_PEERS,) + a.shape[1:], a.dtype) for a in srcs]
    return _Exchange(srcs, outs, plan, N_XY_PEERS, alias=False)


WIRE_DTYPE = BF16


def _pair_sum(g8, recv, core, *, name):
    n, _, R, C = g8.shape
    tr = _pick(R, (128, 64, 32, 16, 8))

    def body(core_ref, a_ref, b_ref, o_ref):
        o_ref[...] = (a_ref[0] + b_ref[...]).astype(WIRE_DTYPE)

    return pl.pallas_call(
        body, name=name, out_shape=jax.ShapeDtypeStruct((n, R, C), WIRE_DTYPE),
        grid_spec=pltpu.PrefetchScalarGridSpec(
            num_scalar_prefetch=1, grid=(n, R // tr),
            in_specs=[pl.BlockSpec((1, 1, tr, C), lambda s, i, core_ref: (s, core_ref[0], i, 0)),
                      pl.BlockSpec((1, tr, C), lambda s, i, core_ref: (s, i, 0))],
            out_specs=pl.BlockSpec((1, tr, C), lambda s, i, core_ref: (s, i, 0))),
        compiler_params=_cparams("parallel", "parallel"),
    )(core, g8, recv)


def _adamw_math(w, g, m, v):
    m = ADAM_B1 * m + (1.0 - ADAM_B1) * g
    v = ADAM_B2 * v + (1.0 - ADAM_B2) * (g * g)
    m_hat = m / (1.0 - ADAM_B1 ** ADAM_STEP)
    v_hat = v / (1.0 - ADAM_B2 ** ADAM_STEP)
    delta = -ADAM_LR * (m_hat / (jnp.sqrt(v_hat) + ADAM_EPS) + ADAM_WD * w)
    return delta, m, v


def _reduce_adamw(parts, w, m, v, *, own, own_slot, name):
    n, R, C = parts.shape
    tr = _pick(R, (128, 64, 32, 16, 8))

    def body(_, own_ref, parts_ref, w_ref, m_ref, v_ref, g_ref, d_ref, nm_ref, nv_ref):
        g = own_ref[0].astype(F32)
        for k in range(n):
            g = g + parts_ref[k].astype(F32)
        g_ref[...] = g
        d_ref[...], nm_ref[...], nv_ref[...] = _adamw_math(w_ref[...], g, m_ref[...], v_ref[...])

    out = jax.ShapeDtypeStruct((R, C), F32)
    row = pl.BlockSpec((tr, C), lambda i, s: (i, 0))
    return pl.pallas_call(
        body, name=name, out_shape=(out, out, out, out),
        grid_spec=pltpu.PrefetchScalarGridSpec(
            num_scalar_prefetch=1, grid=(R // tr,),
            in_specs=[pl.BlockSpec((1, tr, C), lambda i, s: (s[0], i, 0)),
                      pl.BlockSpec((n, tr, C), lambda i, s: (0, i, 0)), row, row, row],
            out_specs=(row, row, row, row)),
        compiler_params=_cparams("parallel"),
    )(own_slot, own, parts, w, m, v)


SHARDED = (("w_in", (1024, 4096), 1), ("ssm_w_glu", (512, 512), 0), ("w_branch_attn", (512, 1024), 1),
           ("w_branch_ssm", (512, 1024), 1), ("w_out", (1024, 1024), 0), ("xa_wq", (1024, 1024), 0),
           ("xa_wk", (1024, 1024), 0), ("xa_wv", (1024, 1024), 0), ("xa_wo", (1024, 1024), 0),
           ("ffn_w_up", (1024, 5632), 1), ("ffn_conv_w", (3, 5632), 1), ("ffn_w_down", (2816, 1024), 0))
REPLICATED = (("norm_mix_pre", (1024,)), ("norm_mix_post", (1024,)), ("b_gate", (2048,)), ("ssm_a_re", (32, 64)),
              ("ssm_a_im", (32, 64)), ("ssm_log_dt", (32,)), ("ssm_b_re", (32, 64, 16)), ("ssm_b_im", (32, 64, 16)),
              ("ssm_c_re", (32, 16, 64)), ("ssm_c_im", (32, 16, 64)), ("ssm_d", (512,)), ("ssm_b_glu", (512,)),
              ("norm_xa_pre", (1024,)), ("norm_xa_post", (1024,)), ("norm_mem", (1024,)), ("norm_ffn_pre", (1024,)),
              ("norm_ffn_post", (1024,)), ("ffn_conv_b", (5632,)))
PARAM_ORDER = ("norm_mix_pre", "norm_mix_post", "w_in", "b_gate", "ssm_a_re", "ssm_a_im", "ssm_log_dt", "ssm_b_re",
               "ssm_b_im", "ssm_c_re", "ssm_c_im", "ssm_d", "ssm_w_glu", "ssm_b_glu", "w_branch_attn", "w_branch_ssm",
               "w_out", "norm_xa_pre", "norm_xa_post", "norm_mem", "xa_wq", "xa_wk", "xa_wv", "xa_wo", "norm_ffn_pre",
               "norm_ffn_post", "ffn_w_up", "ffn_conv_w", "ffn_conv_b", "ffn_w_down")
FF_LOCAL = 2 * D_FF // N_DEV
FF_LOCAL_PAD = 768
FF_PAD = (N_DEV // 2) * FF_LOCAL_PAD


def _local_shape(shape, axis):
    return tuple(s // N_DEV if a == axis else s for a, s in enumerate(shape))


def _pad_cols(a, width):
    return jnp.pad(a, [(0, 0)] * (a.ndim - 1) + [(0, width - a.shape[-1])])


def _blocks_to_cols(a8):
    return a8.transpose(1, 0, 2).reshape(a8.shape[1], N_DEV * a8.shape[2])


def _cols_to_blocks(a, cb):
    return a.reshape(a.shape[0], N_DEV, cb).transpose(1, 0, 2)


FF_PADDED = ("ffn_w_up", "ffn_conv_w")
LATE = tuple(n for n, _, _ in SHARDED if n != "w_in")
REDUCE_FFN = ("ffn_w_up", "ffn_conv_w", "ffn_w_down")
REDUCE_MID = ("xa_wo", "xa_wq", "xa_wk", "xa_wv", "w_out", "w_branch_attn", "w_branch_ssm", "ssm_w_glu")
SHARD_AXIS = {n: ax for n, _, ax in SHARDED}
FULL_SHAPE = {n: s for n, s, _ in SHARDED}


def _as_local(n, a):
    return _pad_cols(a, FF_LOCAL_PAD) if n in FF_PADDED else a


def _weights_from_wire(wire):
    full = {n: b.reshape((N_DEV,) + b.shape[2:]) for n, b in wire.items()}
    W = {n: a.reshape(FULL_SHAPE[n]) if SHARD_AXIS[n] == 0 else a for n, a in full.items()}
    for n in ("w_branch_attn", "w_branch_ssm", "ffn_conv_w"):
        W[n] = _blocks_to_cols(full[n])
    W["ffn_w_down"] = jnp.pad(W["ffn_w_down"].reshape(N_DEV // 2, FF_LOCAL, D_MODEL),
                              ((0, 0), (0, FF_LOCAL_PAD - FF_LOCAL), (0, 0))).reshape(FF_PAD, D_MODEL)
    return W


def _grad_blocks(n, g):
    if n in ("w_branch_attn", "w_branch_ssm"):
        g = _cols_to_blocks(g, D_MODEL // N_DEV)
    elif n == "ffn_conv_w":
        g = _cols_to_blocks(g, FF_LOCAL_PAD)
    elif n == "ffn_w_down":
        g = g.reshape(N_DEV // 2, FF_LOCAL_PAD, D_MODEL)[:, :FF_LOCAL]
    local = _local_shape(FULL_SHAPE[n], SHARD_AXIS[n])
    if n in FF_PADDED:
        local = local[:-1] + (FF_LOCAL_PAD,)
    return g.reshape((N_XY, 2) + local)


SMALL_SHAPE = {n: (1, s[0]) if len(s) == 1 else (s[0], math.prod(s[1:])) for n, s in REPLICATED}
SMALL_SHAPE["ffn_conv_b"] = (N_DEV, FF_LOCAL)
SMALL_EARLY = tuple(n for n, _ in REPLICATED if n != "norm_mix_pre")


def _adamw_replicated(parts, w, m, v, *, name):
    n = len(parts)

    def body(*refs):
        p_refs, w_refs, m_refs, v_refs = (refs[i * n:(i + 1) * n] for i in range(4))
        outs = refs[4 * n:]
        for k in range(n):
            g = p_refs[k][0, 0]
            for s in range(1, N_DEV):
                g = g + p_refs[k][s // 2, s % 2]
            d, nm, nv = _adamw_math(w_refs[k][...], g, m_refs[k][...], v_refs[k][...])
            for slot, val in enumerate((g, d, nm, nv)):
                outs[slot * n + k][...] = val

    vmem = pl.BlockSpec(memory_space=pltpu.VMEM)
    shapes = [jax.ShapeDtypeStruct(a.shape, F32) for a in w] * 4
    res = pl.pallas_call(
        body, name=name, out_shape=tuple(shapes), in_specs=[vmem] * (4 * n), out_specs=tuple([vmem] * (4 * n)),
        compiler_params=pltpu.CompilerParams(vmem_limit_bytes=VMEM_LIMIT),
    )(*parts, *w, *m, *v)
    return [list(res[i * n:(i + 1) * n]) for i in range(4)]


def kernel(x, mem, norm_mix_pre, norm_mix_post, w_in, b_gate, ssm_a_re, ssm_a_im, ssm_log_dt, ssm_b_re, ssm_b_im, ssm_c_re, ssm_c_im, ssm_d, ssm_w_glu, ssm_b_glu, w_branch_attn, w_branch_ssm, w_out, norm_xa_pre, norm_xa_post, norm_mem, xa_wq, xa_wk, xa_wv, xa_wo, norm_ffn_pre, norm_ffn_post, ffn_w_up, ffn_conv_w, ffn_conv_b, ffn_w_down, loss_target, m_norm_mix_pre, m_norm_mix_post, m_w_in, m_b_gate, m_ssm_a_re, m_ssm_a_im, m_ssm_log_dt, m_ssm_b_re, m_ssm_b_im, m_ssm_c_re, m_ssm_c_im, m_ssm_d, m_ssm_w_glu, m_ssm_b_glu, m_w_branch_attn, m_w_branch_ssm, m_w_out, m_norm_xa_pre, m_norm_xa_post, m_norm_mem, m_xa_wq, m_xa_wk, m_xa_wv, m_xa_wo, m_norm_ffn_pre, m_norm_ffn_post, m_ffn_w_up, m_ffn_conv_w, m_ffn_conv_b, m_ffn_w_down, v_norm_mix_pre, v_norm_mix_post, v_w_in, v_b_gate, v_ssm_a_re, v_ssm_a_im, v_ssm_log_dt, v_ssm_b_re, v_ssm_b_im, v_ssm_c_re, v_ssm_c_im, v_ssm_d, v_ssm_w_glu, v_ssm_b_glu, v_w_branch_attn, v_w_branch_ssm, v_w_out, v_norm_xa_pre, v_norm_xa_post, v_norm_mem, v_xa_wq, v_xa_wk, v_xa_wv, v_xa_wo, v_norm_ffn_pre, v_norm_ffn_post, v_ffn_w_up, v_ffn_conv_w, v_ffn_conv_b, v_ffn_w_down):
    args = dict(locals())
    w_loc = {n: args[n][0] for n in PARAM_ORDER}
    m_loc = {n: args["m_" + n][0] for n in PARAM_ORDER}
    v_loc = {n: args["v_" + n][0] for n in PARAM_ORDER}
    core_i = lax.axis_index("c")
    chip_i = 2 * lax.axis_index("x") + lax.axis_index("y")
    core = core_i.astype(jnp.int32).reshape(1)
    chip = chip_i.astype(jnp.int32).reshape(1)

    def in_place(a):
        buf = lax.empty((N_XY, 2) + a.shape, a.dtype)
        return lax.dynamic_update_slice(buf, a[None, None], (chip_i, core_i) + (0,) * a.ndim)

    as_wire = lambda n: in_place(_as_local(n, w_loc[n]).astype(F32 if n == "ffn_conv_w" else BF16))

    P = {}
    for n, shape in REPLICATED:
        P[n] = w_loc[n] if len(shape) > 1 or n == "ssm_log_dt" else w_loc[n].reshape(1, -1)
    P["ffn_conv_b"] = _pad_cols(w_loc["ffn_conv_b"].reshape(N_DEV, FF_LOCAL), FF_LOCAL_PAD).reshape(1, 2 * FF_PAD)

    loss, grad_x, small_parts, reduced = _local_step(x[0], mem[0], loss_target[0], as_wire("w_in"),
                                                     [as_wire(n) for n in LATE], P, core)
    loss = lax.psum(loss[0, 0], ("x", "y", "c"))

    big_out = {}
    for n, (own, parts) in reduced.items():
        res = _reduce_adamw(parts, _as_local(n, w_loc[n]), _as_local(n, m_loc[n]), _as_local(n, v_loc[n]),
                            own=own, own_slot=chip, name="adamw_" + n)
        big_out[n] = [r[:, :FF_LOCAL] if n in FF_PADDED else r for r in res]

    names = [n for n, _ in REPLICATED]
    as_small = lambda d: [d[n].reshape(SMALL_SHAPE[n]) for n in names]
    small_out = _adamw_replicated([small_parts[n] for n in names], as_small(w_loc), as_small(m_loc), as_small(v_loc),
                                  name="adamw_replicated")
    small_out = [dict(zip(names, res)) for res in small_out]

    outs = [loss, grad_x[None]]
    for k in range(4):
        for n in PARAM_ORDER:
            src = big_out[n][k] if n in big_out else small_out[k][n]
            outs.append(src.reshape(args[n].shape))
    return tuple(outs)
```

```python
import functools
import math

import jax
import jax.numpy as jnp
from jax import lax
from jax.experimental import pallas as pl
from jax.experimental.pallas import tpu as pltpu

F32 = jnp.float32
BF16 = jnp.bfloat16

D_MODEL = 1024
SB_HEADS = 8
SB_HEAD_DIM = 64
SB_WIDTH = 512
SSM_WIDTH = 512
SSM_GROUP = 16
SSM_GROUPS = 32
SSM_STATE = 64
XA_HEADS = 4
XA_HEAD_DIM = 256
D_FF = 2816
RMS_EPS = 1e-6
IN_WIDTH = 4096
N_DEV = 8

ADAM_LR = 0.001
ADAM_B1 = 0.9
ADAM_B2 = 0.999
ADAM_EPS = 1e-08
ADAM_WD = 0.01
ADAM_STEP = 10

LANES = 128
SUBLANES = 8
VMEM_LIMIT = 48 * 1024 * 1024

_GELU_C = math.sqrt(2.0 / math.pi)


def _cparams(*sem):
    return pltpu.CompilerParams(dimension_semantics=sem, vmem_limit_bytes=VMEM_LIMIT)


def _pick(n, cands):
    for c in cands:
        if n % c == 0:
            return c
    return n


def _gelu(x):
    return 0.5 * x * (1.0 + jnp.tanh(_GELU_C * (x + 0.044715 * x * x * x)))


def _gelu_and_grad(x):
    t = jnp.tanh(_GELU_C * (x + 0.044715 * x * x * x))
    g = 0.5 * x * (1.0 + t)
    dg = 0.5 * (1.0 + t) + 0.5 * x * (1.0 - t * t) * _GELU_C * (1.0 + 3.0 * 0.044715 * x * x)
    return g, dg


def _sigmoid(x):
    return 1.0 / (1.0 + jnp.exp(-x))


def _dot(a, b, ca, cb):
    return lax.dot_general(a.astype(BF16), b.astype(BF16), (((ca,), (cb,)), ((), ())),
                           preferred_element_type=F32)


MM_TILES = (1024, 768, 512, 256, 128)
MM_K_TILES = (2048, 1536) + MM_TILES
MM_PAIR = 2
MM_WIDE = 1536


def _matmul(a, b, *, ta=False, tb=False, out_dtype=F32, name, b_block0=0, n_blocks=None,
            out_cb=None, out_into=None, out_block0=0, acc_in=None, rider=None):
    if ta:
        K, M = a.shape
    else:
        M, K = a.shape
    b_cb = None
    if b.ndim == 3:
        b_cb = b.shape[2]
        n_blocks = b.shape[0] - b_block0 if n_blocks is None else n_blocks
        N, K2 = (b.shape[1], n_blocks * b_cb) if tb else (n_blocks * b_cb, b.shape[1])
    elif tb:
        N, K2 = b.shape
    else:
        K2, N = b.shape
    assert K == K2, (a.shape, b.shape, ta, tb)
    if out_into is not None:
        out_cb = out_into.shape[2]
    tm = _pick(M, MM_TILES)
    pair = lambda cb_, count: MM_PAIR if (cb_ * MM_PAIR <= MM_WIDE and count % MM_PAIR == 0) else 1
    b_pair = pair(b_cb, n_blocks) if b_cb else 1
    o_pair = pair(out_cb, N // out_cb) if out_cb else 1
    if b_cb and not tb:
        tn = b_cb * b_pair
    elif out_cb:
        tn = out_cb * o_pair
    else:
        tn = _pick(N, MM_TILES)
    if b_cb and tb:
        tk = b_cb * b_pair
    else:
        tk = _pick(K, MM_TILES if tn > MM_TILES[0] else MM_K_TILES)
    nk = K // tk
    ca, cb = (0 if ta else 1), (1 if tb else 0)
    has_acc = acc_in is not None
    has_into = out_into is not None

    def body(*refs):
        a_ref, b_ref = refs[0], refs[1]
        pos = 2
        c_ref = None
        if has_acc:
            c_ref = refs[pos]
            pos += 1
        if has_into:
            pos += 1
        o_ref = refs[pos]
        b_tile = b_ref[...] if b_cb is None else jnp.concatenate([b_ref[t] for t in range(b_pair)], axis=1)
        p = _dot(a_ref[...], b_tile, ca, cb)

        def write(val):
            val = val.astype(out_dtype)
            if out_cb is None:
                o_ref[...] = val
            else:
                for t in range(o_pair):
                    o_ref[t] = val[:, t * out_cb:(t + 1) * out_cb]

        if nk == 1:
            write((p + c_ref[...]) if has_acc else p)
        else:
            acc_ref = refs[pos + 1]
            k = pl.program_id(2)

            @pl.when(k == 0)
            def _():
                acc_ref[...] = (p + c_ref[...]) if has_acc else p

            @pl.when(k > 0)
            def _():
                acc_ref[...] += p

            @pl.when(k == nk - 1)
            def _():
                write(acc_ref[...])

    nj, ni = N // tn, M // tm
    a_bytes, b_bytes = a.size * a.dtype.itemsize, K * N * b.dtype.itemsize
    n_outer = a_bytes * nj + b_bytes * (1 if nk == 1 else ni) <= a_bytes * (1 if nk == 1 else nj) + b_bytes * ni
    grid = (nj, ni, nk) if n_outer else (ni, nj, nk)

    def spec(block, index):
        return pl.BlockSpec(block, (lambda g0, g1, k: index(g0, g1, k)) if n_outer else (lambda g0, g1, k: index(g1, g0, k)))

    a_spec = spec((tk, tm), lambda j, i, k: (k, i)) if ta else spec((tm, tk), lambda j, i, k: (i, k))
    if b_cb is None:
        b_spec = spec((tn, tk), lambda j, i, k: (j, k)) if tb else spec((tk, tn), lambda j, i, k: (k, j))
    elif tb:
        b_spec = spec((b_pair, tn, b_cb), lambda j, i, k: (b_block0 // b_pair + k, j, 0))
    else:
        b_spec = spec((b_pair, tk, b_cb), lambda j, i, k: (b_block0 // b_pair + j, k, 0))
    in_specs = [a_spec, b_spec]
    operands = [a, b]
    aliases = {}
    if has_acc:
        in_specs.append(spec((tm, tn), lambda j, i, k: (i, j)))
        operands.append(acc_in)
    if has_into:
        aliases = {len(operands): 0}
        in_specs.append(pl.BlockSpec(memory_space=pl.ANY))
        operands.append(out_into)
    if out_cb is None:
        out_shape = jax.ShapeDtypeStruct((M, N), out_dtype)
        out_spec = spec((tm, tn), lambda j, i, k: (i, j))
    else:
        out_shape = (jax.ShapeDtypeStruct(out_into.shape, out_into.dtype) if has_into
                     else jax.ShapeDtypeStruct((N // out_cb, M, out_cb), out_dtype))
        out_spec = spec((o_pair, tm, out_cb), lambda j, i, k: (out_block0 // o_pair + j, i, 0))
    if rider is not None:
        assert not has_into
        (out,), brought = _call(body, name=name, rider=rider, grid=grid, in_specs=in_specs,
                                out_specs=(out_spec,), out_shape=(out_shape,), operands=operands,
                                scratch_shapes=[] if nk == 1 else [pltpu.VMEM((tm, tn), F32)])
        return out, brought
    return pl.pallas_call(
        body, name=name, out_shape=out_shape,
        grid=grid,
        in_specs=in_specs, out_specs=out_spec, input_output_aliases=aliases,
        scratch_shapes=[] if nk == 1 else [pltpu.VMEM((tm, tn), F32)],
        compiler_params=_cparams("parallel", "parallel", "arbitrary"),
    )(*operands)


def _rms(x, g):
    r = lax.rsqrt(jnp.mean(x * x, axis=-1, keepdims=True) + RMS_EPS)
    return x * r * g


def _rms_bwd(dy, x, g):
    r = lax.rsqrt(jnp.mean(x * x, axis=-1, keepdims=True) + RMS_EPS)
    xh = x * r
    dxh = dy * g
    dx = r * (dxh - xh * jnp.mean(dxh * xh, axis=-1, keepdims=True))
    dg = jnp.sum(dy * xh, axis=0, keepdims=True)
    return dx, dg


def _row_tile(rows):
    return _pick(rows, (512, 256, 128, 64, 32, 16, 8))


def _rms_fwd(x, g, *, name, rider=None):
    R, D = x.shape
    tr = _row_tile(R)

    def body(x_ref, g_ref, h_ref):
        h_ref[...] = _rms(x_ref[...], g_ref[...]).astype(BF16)

    (h,), brought = _call(
        body, name=name, rider=rider, out_shape=(jax.ShapeDtypeStruct((R, D), BF16),), grid=(R // tr,),
        in_specs=[pl.BlockSpec((tr, D), lambda i: (i, 0)), pl.BlockSpec((1, D), lambda i: (0, 0))],
        out_specs=(pl.BlockSpec((tr, D), lambda i: (i, 0)),), scratch_shapes=[], operands=(x, g))
    return h if rider is None else (h, brought)


def _resnorm_norm(x, z, g_post, g_next, *, name):
    R, D = x.shape
    tr = _row_tile(R)

    def body(x_ref, z_ref, gp_ref, gn_ref, xn_ref, h_ref):
        xn = x_ref[...] + _rms(z_ref[...], gp_ref[...])
        xn_ref[...] = xn
        h_ref[...] = _rms(xn, gn_ref[...]).astype(BF16)

    row = pl.BlockSpec((tr, D), lambda i: (i, 0))
    vec = pl.BlockSpec((1, D), lambda i: (0, 0))
    return pl.pallas_call(
        body, name=name,
        out_shape=(jax.ShapeDtypeStruct((R, D), F32), jax.ShapeDtypeStruct((R, D), BF16)),
        grid=(R // tr,), in_specs=[row, row, vec, vec], out_specs=(row, row),
        compiler_params=_cparams("parallel"),
    )(x, z, g_post, g_next)


def _final_loss(x, z, g_post, target, *, name):
    R, D = x.shape
    tr = _row_tile(R)

    def body(x_ref, z_ref, gp_ref, t_ref, loss_ref, dy_ref, dz_ref, dg_ref):
        i = pl.program_id(0)
        z = z_ref[...]
        g = gp_ref[...]
        err = x_ref[...] + _rms(z, g) - t_ref[...]
        dy = err * (1.0 / D)
        dy_ref[...] = dy
        dz, dg = _rms_bwd(dy, z, g)
        dz_ref[...] = dz.astype(BF16)
        part = 0.5 * jnp.sum(jnp.sum(err * err, axis=-1, keepdims=True) * (1.0 / D), axis=0, keepdims=True)

        @pl.when(i == 0)
        def _():
            loss_ref[...] = part
            dg_ref[...] = dg

        @pl.when(i > 0)
        def _():
            loss_ref[...] += part
            dg_ref[...] += dg

    row = pl.BlockSpec((tr, D), lambda i: (i, 0))
    vec = pl.BlockSpec((1, D), lambda i: (0, 0))
    return pl.pallas_call(
        body, name=name,
        out_shape=(jax.ShapeDtypeStruct((1, 1), F32), jax.ShapeDtypeStruct((R, D), F32),
                   jax.ShapeDtypeStruct((R, D), BF16), jax.ShapeDtypeStruct((1, D), F32)),
        grid=(R // tr,), in_specs=[row, row, vec, row],
        out_specs=(pl.BlockSpec((1, 1), lambda i: (0, 0)), row, row, vec),
        compiler_params=_cparams("arbitrary"),
    )(x, z, g_post, target)


def _norm_bwd_pair(dres, dh, xk, g_pre, zprev, g_prev_post, *, name, rider=None):
    R, D = xk.shape
    tr = _row_tile(R)

    def body(dres_ref, dh_ref, x_ref, gpre_ref, z_ref, gpost_ref, dx_ref, dz_ref, dgpre_ref, dgpost_ref):
        i = pl.program_id(0)
        d1, dgpre = _rms_bwd(dh_ref[...], x_ref[...], gpre_ref[...])
        dx = dres_ref[...] + d1
        dx_ref[...] = dx
        dz, dgpost = _rms_bwd(dx, z_ref[...], gpost_ref[...])
        dz_ref[...] = dz.astype(BF16)

        @pl.when(i == 0)
        def _():
            dgpre_ref[...] = dgpre
            dgpost_ref[...] = dgpost

        @pl.when(i > 0)
        def _():
            dgpre_ref[...] += dgpre
            dgpost_ref[...] += dgpost

    row = pl.BlockSpec((tr, D), lambda i: (i, 0))
    vec = pl.BlockSpec((1, D), lambda i: (0, 0))
    return _call(
        body, name=name, rider=rider,
        out_shape=(jax.ShapeDtypeStruct((R, D), F32), jax.ShapeDtypeStruct((R, D), BF16),
                   jax.ShapeDtypeStruct((1, D), F32), jax.ShapeDtypeStruct((1, D), F32)),
        grid=(R // tr,), in_specs=[row, row, row, vec, row, vec], out_specs=(row, row, vec, vec),
        scratch_shapes=[], operands=(dres, dh, xk, g_pre, zprev, g_prev_post))


def _norm_bwd_single(dres, dh, xk, g_pre, *, name, rider=None):
    R, D = xk.shape
    tr = _row_tile(R)
    has_res = dres is not None

    def body(*refs):
        if has_res:
            dres_ref, dh_ref, x_ref, gpre_ref, dx_ref, dgpre_ref = refs
        else:
            dh_ref, x_ref, gpre_ref, dx_ref, dgpre_ref = refs
        i = pl.program_id(0)
        d1, dgpre = _rms_bwd(dh_ref[...], x_ref[...], gpre_ref[...])
        dx_ref[...] = dres_ref[...] + d1 if has_res else d1

        @pl.when(i == 0)
        def _():
            dgpre_ref[...] = dgpre

        @pl.when(i > 0)
        def _():
            dgpre_ref[...] += dgpre

    row = pl.BlockSpec((tr, D), lambda i: (i, 0))
    vec = pl.BlockSpec((1, D), lambda i: (0, 0))
    ins = ([dres] if has_res else []) + [dh, xk, g_pre]
    res, brought = _call(
        body, name=name, rider=rider,
        out_shape=(jax.ShapeDtypeStruct((R, D), F32), jax.ShapeDtypeStruct((1, D), F32)),
        grid=(R // tr,), in_specs=([row] if has_res else []) + [row, row, vec], out_specs=(row, vec),
        scratch_shapes=[], operands=ins)
    return res if rider is None else (res, brought)


SB_BLOCK = 256
SB_QBLOCK = 512
SB_DEAD = -104.0


def _sb_tri(kind):
    r = lax.broadcasted_iota(jnp.int32, (SB_BLOCK, SB_BLOCK), 0)
    c = lax.broadcasted_iota(jnp.int32, (SB_BLOCK, SB_BLOCK), 1)
    keep = {"after": r > c, "before": r < c}[kind]
    return jnp.where(keep, 1.0, 0.0).astype(BF16)


def _sb_scores(qm, k_blk):
    z = _dot(qm, k_blk, 1, 1)
    sp = jnp.maximum(z, 0.0) + jnp.log(1.0 + jnp.exp(-jnp.abs(z)))
    return z, sp


def _sb_causal(rows):
    r = lax.broadcasted_iota(jnp.int32, (rows, SB_BLOCK), 0)
    c = lax.broadcasted_iota(jnp.int32, (rows, SB_BLOCK), 1)
    return c < r


def _head_masks():
    lane = lax.broadcasted_iota(jnp.int32, (1, LANES), 1)
    return [jnp.where(lane < SB_HEAD_DIM, 1.0, 0.0), jnp.where(lane >= SB_HEAD_DIM, 1.0, 0.0)]


def _sb_fwd(proj, *, name, rider=None):
    S = proj.shape[0]
    T = SB_BLOCK
    TQ = min(SB_QBLOCK, S)
    span = TQ // T
    nq = S // TQ
    npair = SB_WIDTH // LANES
    scale = SB_HEAD_DIM ** -0.5

    def body(q_ref, k_ref, v_ref, o_ref, tot_ref, first_ref, acc_ref, run_ref):
        masks = _head_masks()
        tri = _sb_tri("after")
        first_ref[...] = jnp.zeros_like(first_ref)
        slot = lax.broadcasted_iota(jnp.int32, first_ref.shape, 1)

        def alive():
            reach = jnp.maximum(jnp.max(run_ref[0]), jnp.max(run_ref[1]))
            return (reach > SB_DEAD).astype(jnp.int32)

        def q_block(i, _):
            qrow = pl.ds(pl.multiple_of(i * TQ, TQ), TQ)
            q = q_ref[qrow, :] * scale
            qm = [(q * m).astype(BF16) for m in masks]
            acc_ref[...] = jnp.zeros_like(acc_ref)
            run_ref[...] = jnp.zeros_like(run_ref)

            def k_block(j, own):
                krow = pl.ds(pl.multiple_of(j * T, T), T)
                k_blk = k_ref[krow, :].astype(BF16)
                v_blk = v_ref[krow, :].astype(BF16)
                r0 = 0 if own is None else own * T
                rows = pl.ds(r0, TQ - r0)
                for h in range(2):
                    z, sp = _sb_scores(qm[h][r0:], k_blk)
                    causal = None if own is None else _sb_causal(TQ - r0)
                    lf = -sp if causal is None else jnp.where(causal, -sp, 0.0)
                    e = jnp.exp(z - sp + _dot(lf, tri, 1, 0) + run_ref[h, rows])
                    w = e if causal is None else jnp.where(causal, e, 0.0)
                    acc_ref[h, rows] += _dot(w, v_blk, 1, 0)
                    run_ref[h, rows] += jnp.sum(lf, axis=1, keepdims=True)

            for d in reversed(range(span)):
                k_block(i * span + d, d)

            def below(carry):
                jj, _ = carry
                k_block(i * span - 1 - jj, None)
                return jj + 1, alive()

            done, _ = lax.while_loop(lambda c: jnp.logical_and(c[0] < i * span, c[1] > 0), below, (jnp.int32(0), alive()))
            o_ref[qrow, :] = (acc_ref[0] * masks[0] + acc_ref[1] * masks[1]).astype(BF16)
            tot_ref[qrow, :] = run_ref[0] * masks[0] + run_ref[1] * masks[1]
            first_ref[...] = jnp.where(slot == i, (i * span - done).astype(F32), first_ref[...])
            return 0

        lax.fori_loop(0, nq, q_block, 0)

    blk = lambda off: pl.BlockSpec((S, LANES), lambda p: (0, off + p))
    return _call(
        body, name=name, rider=rider,
        out_shape=(jax.ShapeDtypeStruct((S, SB_WIDTH), BF16), jax.ShapeDtypeStruct((S, SB_WIDTH), F32),
                   jax.ShapeDtypeStruct((npair, SUBLANES, LANES), F32)),
        grid=(npair,),
        in_specs=[blk(0), blk(npair), blk(2 * npair)],
        out_specs=(blk(0), blk(0), pl.BlockSpec((1, SUBLANES, LANES), lambda p: (p, 0, 0))),
        scratch_shapes=[pltpu.VMEM((2, TQ, LANES), F32), pltpu.VMEM((2, TQ, 1), F32)],
        operands=(proj, proj, proj))


def _sb_bwd(proj, tot, first, do_attn, *, name, rider=None):
    S = proj.shape[0]
    T = SB_BLOCK
    TQ = min(SB_QBLOCK, S)
    span = TQ // T
    nq = S // TQ
    npair = SB_WIDTH // LANES
    scale = SB_HEAD_DIM ** -0.5

    def body(q_ref, k_ref, v_ref, tot_ref, first_ref, do_ref, dq_ref, dk_ref, dv_ref,
             dqacc_ref, dkacc_ref, dvacc_ref, run_ref, grun_ref):
        masks = _head_masks()
        tri_after = _sb_tri("after")
        tri_before = _sb_tri("before")
        dkacc_ref[...] = jnp.zeros_like(dkacc_ref)
        dvacc_ref[...] = jnp.zeros_like(dvacc_ref)
        slot = lax.broadcasted_iota(jnp.int32, first_ref.shape, 1)

        def q_block(i, _):
            qrow = pl.ds(pl.multiple_of(i * TQ, TQ), TQ)
            q = q_ref[qrow, :] * scale
            do = do_ref[qrow, :].astype(F32)
            tot = tot_ref[qrow, :]
            qm = [(q * m).astype(BF16) for m in masks]
            dom = [(do * m).astype(BF16) for m in masks]
            ltot = [jnp.sum(tot * m, axis=1, keepdims=True) * (1.0 / SB_HEAD_DIM) for m in masks]
            dqacc_ref[...] = jnp.zeros_like(dqacc_ref)
            run_ref[...] = jnp.zeros_like(run_ref)
            grun_ref[...] = jnp.zeros_like(grun_ref)

            def k_block(j, own):
                krow = pl.ds(pl.multiple_of(j * T, T), T)
                k_blk = k_ref[krow, :].astype(BF16)
                v_blk = v_ref[krow, :].astype(BF16)
                r0 = 0 if own is None else own * T
                rows = pl.ds(r0, TQ - r0)
                for h in range(2):
                    z, sp = _sb_scores(qm[h][r0:], k_blk)
                    causal = None if own is None else _sb_causal(TQ - r0)
                    lf = -sp if causal is None else jnp.where(causal, -sp, 0.0)
                    lsum = jnp.sum(lf, axis=1, keepdims=True)
                    later = (ltot[h][r0:] - run_ref[h, rows] - lsum) + _dot(lf, tri_after, 1, 0)
                    beta = jnp.exp(z - sp)
                    w = jnp.exp(z - sp + later)
                    if causal is not None:
                        w = jnp.where(causal, w, 0.0)
                    g = _dot(dom[h][r0:], v_blk, 1, 1) * w
                    gbefore = grun_ref[h, rows] + _dot(g, tri_before, 1, 0)
                    dz = g - beta * (g + gbefore)
                    if causal is not None:
                        dz = jnp.where(causal, dz, 0.0)
                    dz = dz.astype(BF16)
                    dqacc_ref[h, rows] += _dot(dz, k_blk, 1, 0)
                    dkacc_ref[krow, :] += _dot(dz, qm[h][r0:], 0, 0)
                    dvacc_ref[krow, :] += _dot(w, dom[h][r0:], 0, 0)
                    run_ref[h, rows] += lsum
                    grun_ref[h, rows] += jnp.sum(g, axis=1, keepdims=True)

            def above(j, _):
                k_block(j, None)
                return 0

            first = jnp.max(jnp.where(slot == i, first_ref[...], 0.0)).astype(jnp.int32)
            lax.fori_loop(jnp.clip(first, 0, i * span), i * span, above, 0)
            for d in range(span):
                k_block(i * span + d, d)
            dq_ref[qrow, :] = ((dqacc_ref[0] * masks[0] + dqacc_ref[1] * masks[1]) * scale).astype(BF16)
            return 0

        lax.fori_loop(0, nq, q_block, 0)
        dk_ref[...] = dkacc_ref[...].astype(BF16)
        dv_ref[...] = dvacc_ref[...].astype(BF16)

    blk = lambda off: pl.BlockSpec((S, LANES), lambda p: (0, off + p))
    out = jax.ShapeDtypeStruct((S, SB_WIDTH), BF16)
    return _call(
        body, name=name, rider=rider, out_shape=(out, out, out), grid=(npair,),
        in_specs=[blk(0), blk(npair), blk(2 * npair), blk(0), pl.BlockSpec((1, SUBLANES, LANES), lambda p: (p, 0, 0)),
                  blk(0)],
        out_specs=(blk(0), blk(0), blk(0)),
        scratch_shapes=[pltpu.VMEM((2, TQ, LANES), F32), pltpu.VMEM((S, LANES), F32), pltpu.VMEM((S, LANES), F32),
                        pltpu.VMEM((2, TQ, 1), F32), pltpu.VMEM((2, TQ, 1), F32)],
        operands=(proj, proj, proj, tot, first, do_attn))


SSM_HALVES = 2
SSM_HALF_CH = SSM_WIDTH // SSM_HALVES
SSM_HALF_ST = SSM_GROUPS * SSM_STATE // SSM_HALVES
SSM_CHUNK = 512


def _cmul(ar, ai, br, bi):
    return ar * br - ai * bi, ar * bi + ai * br


def _ssm_tables(lam_re, lam_im):
    lr = lam_re.reshape(-1)
    li = lam_im.reshape(-1)
    pows = [(jnp.ones_like(lr), jnp.zeros_like(li)), (lr, li)]
    for _ in range(2, SUBLANES + 1):
        pows.append(_cmul(pows[-1][0], pows[-1][1], lr, li))
    row = jnp.arange(SUBLANES)[:, None]

    def shift_tab(d, keep):
        return [jnp.where(keep, pows[d][0][None, :], 0.0), jnp.where(keep, pows[d][1][None, :], 0.0)]

    fwd, bwd = [], []
    for d in (1, 2, 4):
        fwd += shift_tab(d, row >= d)
        bwd += shift_tab(d, row + d < SUBLANES)
    fwd += [jnp.stack([pows[r + 1][0] for r in range(SUBLANES)]), jnp.stack([pows[r + 1][1] for r in range(SUBLANES)])]
    bwd += [jnp.stack([pows[SUBLANES - r][0] for r in range(SUBLANES)]),
            jnp.stack([pows[SUBLANES - r][1] for r in range(SUBLANES)])]

    def halves(tabs):
        t = jnp.stack(tabs)
        return t.reshape(8, SUBLANES, SSM_HALVES, SSM_HALF_ST).transpose(2, 0, 1, 3)

    return halves(fwd), halves(bwd)


def _expand_groups(blocks, rows, cols):
    gh = SSM_GROUPS // SSM_HALVES
    R, C = gh * rows, gh * cols
    rep = (lax.broadcasted_iota(jnp.int32, (cols, C), 1) & (cols - 1)) == lax.broadcasted_iota(jnp.int32, (cols, C), 0)
    wide = _dot(blocks, jnp.where(rep, 1.0, 0.0), 1, 0)
    r = lax.broadcasted_iota(jnp.int32, (R, C), 0) >> (rows.bit_length() - 1)
    c = lax.broadcasted_iota(jnp.int32, (R, C), 1) >> (cols.bit_length() - 1)
    return jnp.where(r == c, wide, 0.0).astype(BF16)


def _collect_groups(full, rows, cols):
    R, C = full.shape
    r = lax.broadcasted_iota(jnp.int32, (R, C), 0) >> (rows.bit_length() - 1)
    c = lax.broadcasted_iota(jnp.int32, (R, C), 1) >> (cols.bit_length() - 1)
    rest = jnp.where(r == c, full, 0.0)
    fold = (lax.broadcasted_iota(jnp.int32, (C, cols), 0) & (cols - 1)) == lax.broadcasted_iota(jnp.int32, (C, cols), 1)
    fold = jnp.where(fold, 1.0, 0.0)
    out = jnp.zeros((R, cols), F32)
    for _ in range(3):
        piece = rest.astype(BF16)
        out = out + _dot(piece, fold, 1, 0)
        rest = rest - piece.astype(F32)
    return out


def _ssm_expand(bre_ref, bim_ref, cre_ref, cim_ref, bd_re_s, bd_im_s, cd_re_s, cd_im_s):
    bd_re_s[...] = _expand_groups(bre_ref[0], SSM_GROUP, SSM_STATE)
    bd_im_s[...] = _expand_groups(bim_ref[0], SSM_GROUP, SSM_STATE)
    cd_re_s[...] = _expand_groups(cre_ref[0], SSM_STATE, SSM_GROUP)
    cd_im_s[...] = _expand_groups(cim_ref[0], SSM_STATE, SSM_GROUP)


def _ssm_matrix_specs():
    b = pl.BlockSpec((1, SSM_HALF_CH, SSM_STATE), lambda h, c: (h, 0, 0))
    c = pl.BlockSpec((1, SSM_HALF_ST, SSM_GROUP), lambda h, c: (h, 0, 0))
    return [b, b, c, c]


def _ssm_matrix_scratch():
    return [pltpu.VMEM((SSM_HALF_CH, SSM_HALF_ST), BF16), pltpu.VMEM((SSM_HALF_CH, SSM_HALF_ST), BF16),
            pltpu.VMEM((SSM_HALF_ST, SSM_HALF_CH), BF16), pltpu.VMEM((SSM_HALF_ST, SSM_HALF_CH), BF16)]


def _ssm_fwd(proj, b_re, b_im, c_re, c_imneg, d_skip, tab, *, name, rider=None):
    S = proj.shape[0]
    Tc = min(SSM_CHUNK, S)
    nc = S // Tc
    u_blk0 = (3 * SB_WIDTH) // SSM_HALF_CH

    def body(u_ref, bre_ref, bim_ref, cre_ref, cim_ref, d_ref, tab_ref, y_ref, xre_ref, xim_ref, cre_s, cim_s,
             bd_re_s, bd_im_s, cd_re_s, cd_im_s):
        c = pl.program_id(1)

        @pl.when(c == 0)
        def _():
            cre_s[...] = jnp.zeros_like(cre_s)
            cim_s[...] = jnp.zeros_like(cim_s)
            _ssm_expand(bre_ref, bim_ref, cre_ref, cim_ref, bd_re_s, bd_im_s, cd_re_s, cd_im_s)

        u = u_ref[...]
        ub = u.astype(BF16)
        xre_ref[...] = _dot(ub, bd_re_s[...], 1, 0)
        xim_ref[...] = _dot(ub, bd_im_s[...], 1, 0)

        def slab(k, carry):
            car_re, car_im = carry
            rows = pl.ds(pl.multiple_of(k * SUBLANES, SUBLANES), SUBLANES)
            sre = xre_ref[rows, :]
            sim = xim_ref[rows, :]
            for n, d in enumerate((1, 2, 4)):
                pre, pim = tab_ref[0, 2 * n], tab_ref[0, 2 * n + 1]
                rre = pltpu.roll(sre, d, 0)
                rim = pltpu.roll(sim, d, 0)
                sre, sim = sre + (pre * rre - pim * rim), sim + (pre * rim + pim * rre)
            pre, pim = tab_ref[0, 6], tab_ref[0, 7]
            sre, sim = sre + (pre * car_re - pim * car_im), sim + (pre * car_im + pim * car_re)
            xre_ref[rows, :] = sre
            xim_ref[rows, :] = sim
            last = (SUBLANES - 1, SUBLANES)
            return (jnp.broadcast_to(sre[last[0]:last[1], :], sre.shape),
                    jnp.broadcast_to(sim[last[0]:last[1], :], sim.shape))

        car = lax.fori_loop(0, Tc // SUBLANES, slab, (cre_s[...], cim_s[...]))
        cre_s[...] = car[0]
        cim_s[...] = car[1]
        y = _dot(xre_ref[...], cd_re_s[...], 1, 0) + _dot(xim_ref[...], cd_im_s[...], 1, 0)
        y_ref[...] = y + d_ref[...] * u

    return _call(
        body, name=name, rider=rider,
        out_shape=(jax.ShapeDtypeStruct((S, SSM_WIDTH), F32),
                   jax.ShapeDtypeStruct((S, SSM_HALVES * SSM_HALF_ST), F32),
                   jax.ShapeDtypeStruct((S, SSM_HALVES * SSM_HALF_ST), F32)),
        grid=(SSM_HALVES, nc),
        in_specs=[pl.BlockSpec((Tc, SSM_HALF_CH), lambda h, c: (c, u_blk0 + h))] + _ssm_matrix_specs()
                 + [pl.BlockSpec((1, SSM_HALF_CH), lambda h, c: (0, h)),
                    pl.BlockSpec((1, 8, SUBLANES, SSM_HALF_ST), lambda h, c: (h, 0, 0, 0))],
        out_specs=(pl.BlockSpec((Tc, SSM_HALF_CH), lambda h, c: (c, h)),
                   pl.BlockSpec((Tc, SSM_HALF_ST), lambda h, c: (c, h)),
                   pl.BlockSpec((Tc, SSM_HALF_ST), lambda h, c: (c, h))),
        scratch_shapes=[pltpu.VMEM((SUBLANES, SSM_HALF_ST), F32), pltpu.VMEM((SUBLANES, SSM_HALF_ST), F32)]
                       + _ssm_matrix_scratch(),
        operands=(proj, b_re, b_im, c_re, c_imneg, d_skip, tab))


def _ssm_bwd(dy, proj, x_re, x_im, b_re, b_im, c_re, c_imneg, d_skip, tab, *, name, rider=None):
    S = proj.shape[0]
    Tc = min(SSM_CHUNK, S)
    nc = S // Tc
    u_blk0 = (3 * SB_WIDTH) // SSM_HALF_CH

    def body(dy_ref, u_ref, xre_ref, xim_ref, bre_ref, bim_ref, cre_ref, cim_ref, d_ref, tab_ref,
             du_ref, dbre_ref, dbim_ref, dcre_ref, dcim_ref, dd_ref, dlre_ref, dlim_ref,
             gre_s, gim_s, cre_s, cim_s, bd_re_s, bd_im_s, cd_re_s, cd_im_s, dbre_s, dbim_s, dcre_s, dcim_s):
        c = pl.program_id(1)

        @pl.when(c == 0)
        def _():
            _ssm_expand(bre_ref, bim_ref, cre_ref, cim_ref, bd_re_s, bd_im_s, cd_re_s, cd_im_s)
            cre_s[...] = jnp.zeros_like(cre_s)
            cim_s[...] = jnp.zeros_like(cim_s)
            dbre_s[...] = jnp.zeros_like(dbre_s)
            dbim_s[...] = jnp.zeros_like(dbim_s)
            dcre_s[...] = jnp.zeros_like(dcre_s)
            dcim_s[...] = jnp.zeros_like(dcim_s)
            dd_ref[...] = jnp.zeros_like(dd_ref)
            dlre_ref[...] = jnp.zeros_like(dlre_ref)
            dlim_ref[...] = jnp.zeros_like(dlim_ref)

        dy = dy_ref[...]
        dyb = dy.astype(BF16)
        u = u_ref[...]
        gre_s[...] = _dot(dyb, cd_re_s[...], 1, 1)
        gim_s[...] = _dot(dyb, cd_im_s[...], 1, 1)
        row = lax.broadcasted_iota(jnp.int32, (SUBLANES, SSM_HALF_ST), 0)
        nslab = Tc // SUBLANES

        def slab(kk, carry):
            car_re, car_im, acc_re, acc_im = carry
            k = nslab - 1 - kk
            rows = pl.ds(pl.multiple_of(k * SUBLANES, SUBLANES), SUBLANES)
            sre = gre_s[rows, :]
            sim = gim_s[rows, :]
            for n, d in enumerate((1, 2, 4)):
                pre, pim = tab_ref[0, 2 * n], tab_ref[0, 2 * n + 1]
                rre = pltpu.roll(sre, SUBLANES - d, 0)
                rim = pltpu.roll(sim, SUBLANES - d, 0)
                sre, sim = sre + (pre * rre + pim * rim), sim + (pre * rim - pim * rre)
            pre, pim = tab_ref[0, 6], tab_ref[0, 7]
            sre, sim = sre + (pre * car_re + pim * car_im), sim + (pre * car_im - pim * car_re)
            gre_s[rows, :] = sre
            gim_s[rows, :] = sim
            nre = jnp.where(row == SUBLANES - 1, car_re, pltpu.roll(sre, SUBLANES - 1, 0))
            nim = jnp.where(row == SUBLANES - 1, car_im, pltpu.roll(sim, SUBLANES - 1, 0))
            xr = xre_ref[rows, :]
            xi = xim_ref[rows, :]
            acc_re = acc_re + (nre * xr + nim * xi)
            acc_im = acc_im + (nim * xr - nre * xi)
            return (jnp.broadcast_to(sre[0:1, :], sre.shape), jnp.broadcast_to(sim[0:1, :], sim.shape), acc_re, acc_im)

        car = lax.fori_loop(0, nslab, slab, (cre_s[...], cim_s[...], dlre_ref[0], dlim_ref[0]))
        cre_s[...] = car[0]
        cim_s[...] = car[1]
        dlre_ref[0] = car[2]
        dlim_ref[0] = car[3]
        gre = gre_s[...].astype(BF16)
        gim = gim_s[...].astype(BF16)
        ub = u.astype(BF16)
        du = _dot(gre, bd_re_s[...], 1, 1) + _dot(gim, bd_im_s[...], 1, 1) + d_ref[...] * dy
        du_ref[...] = du.astype(BF16)
        dbre_s[...] += _dot(ub, gre, 0, 0)
        dbim_s[...] += _dot(ub, gim, 0, 0)
        dcre_s[...] += _dot(xre_ref[...], dyb, 0, 0)
        dcim_s[...] += _dot(xim_ref[...], dyb, 0, 0)
        dd_ref[...] += jnp.sum(dy * u, axis=0, keepdims=True)

        @pl.when(c == nc - 1)
        def _():
            dbre_ref[0] = _collect_groups(dbre_s[...], SSM_GROUP, SSM_STATE)
            dbim_ref[0] = _collect_groups(dbim_s[...], SSM_GROUP, SSM_STATE)
            dcre_ref[0] = _collect_groups(dcre_s[...], SSM_STATE, SSM_GROUP)
            dcim_ref[0] = _collect_groups(dcim_s[...], SSM_STATE, SSM_GROUP)

    rev = lambda c: nc - 1 - c
    return _call(
        body, name=name, rider=rider,
        out_shape=(jax.ShapeDtypeStruct((S, SSM_WIDTH), BF16),
                   jax.ShapeDtypeStruct((SSM_HALVES, SSM_HALF_CH, SSM_STATE), F32),
                   jax.ShapeDtypeStruct((SSM_HALVES, SSM_HALF_CH, SSM_STATE), F32),
                   jax.ShapeDtypeStruct((SSM_HALVES, SSM_HALF_ST, SSM_GROUP), F32),
                   jax.ShapeDtypeStruct((SSM_HALVES, SSM_HALF_ST, SSM_GROUP), F32),
                   jax.ShapeDtypeStruct((1, SSM_WIDTH), F32),
                   jax.ShapeDtypeStruct((SSM_HALVES, SUBLANES, SSM_HALF_ST), F32),
                   jax.ShapeDtypeStruct((SSM_HALVES, SUBLANES, SSM_HALF_ST), F32)),
        grid=(SSM_HALVES, nc),
        in_specs=[pl.BlockSpec((Tc, SSM_HALF_CH), lambda h, c: (rev(c), h)),
                  pl.BlockSpec((Tc, SSM_HALF_CH), lambda h, c: (rev(c), u_blk0 + h)),
                  pl.BlockSpec((Tc, SSM_HALF_ST), lambda h, c: (rev(c), h)),
                  pl.BlockSpec((Tc, SSM_HALF_ST), lambda h, c: (rev(c), h))] + _ssm_matrix_specs()
                 + [pl.BlockSpec((1, SSM_HALF_CH), lambda h, c: (0, h)),
                    pl.BlockSpec((1, 8, SUBLANES, SSM_HALF_ST), lambda h, c: (h, 0, 0, 0))],
        out_specs=(pl.BlockSpec((Tc, SSM_HALF_CH), lambda h, c: (rev(c), h)), *_ssm_matrix_specs(),
                   pl.BlockSpec((1, SSM_HALF_CH), lambda h, c: (0, h)),
                   pl.BlockSpec((1, SUBLANES, SSM_HALF_ST), lambda h, c: (h, 0, 0)),
                   pl.BlockSpec((1, SUBLANES, SSM_HALF_ST), lambda h, c: (h, 0, 0))),
        scratch_shapes=[pltpu.VMEM((Tc, SSM_HALF_ST), F32), pltpu.VMEM((Tc, SSM_HALF_ST), F32),
                        pltpu.VMEM((SUBLANES, SSM_HALF_ST), F32), pltpu.VMEM((SUBLANES, SSM_HALF_ST), F32)]
                       + _ssm_matrix_scratch()
                       + [pltpu.VMEM((SSM_HALF_CH, SSM_HALF_ST), F32), pltpu.VMEM((SSM_HALF_CH, SSM_HALF_ST), F32),
                          pltpu.VMEM((SSM_HALF_ST, SSM_HALF_CH), F32), pltpu.VMEM((SSM_HALF_ST, SSM_HALF_CH), F32)],
        operands=(dy, proj, x_re, x_im, b_re, b_im, c_re, c_imneg, d_skip, tab))


def _ssm_prepare(a_re, a_im, log_dt, b_re, b_im):
    dt = jnp.exp(log_dt)[:, None]
    mag = jnp.exp(a_re * dt)
    lre = mag * jnp.cos(a_im * dt)
    lim = mag * jnp.sin(a_im * dt)
    den = a_re * a_re + a_im * a_im
    fre = ((lre - 1.0) * a_re + lim * a_im) / den
    fim = (lim * a_re - (lre - 1.0) * a_im) / den
    bbre = fre[:, :, None] * b_re - fim[:, :, None] * b_im
    bbim = fre[:, :, None] * b_im + fim[:, :, None] * b_re
    return lre, lim, bbre, bbim


def _b_blocks(bbar):
    gh = SSM_GROUPS // SSM_HALVES
    b = bbar.reshape(SSM_HALVES, gh, SSM_STATE, SSM_GROUP).transpose(0, 1, 3, 2)
    return b.reshape(SSM_HALVES, SSM_HALF_CH, SSM_STATE)


def _bbar_from_blocks(db):
    gh = SSM_GROUPS // SSM_HALVES
    return db.reshape(SSM_HALVES, gh, SSM_GROUP, SSM_STATE).transpose(0, 1, 3, 2).reshape(SSM_GROUPS, SSM_STATE, SSM_GROUP)


def _c_blocks(cmat):
    gh = SSM_GROUPS // SSM_HALVES
    c = cmat.reshape(SSM_HALVES, gh, SSM_GROUP, SSM_STATE).transpose(0, 1, 3, 2)
    return c.reshape(SSM_HALVES, SSM_HALF_ST, SSM_GROUP)


def _c_from_blocks(dc):
    gh = SSM_GROUPS // SSM_HALVES
    return dc.reshape(SSM_HALVES, gh, SSM_STATE, SSM_GROUP).transpose(0, 1, 3, 2).reshape(SSM_GROUPS, SSM_GROUP, SSM_STATE)


def _glu_fwd(y_pre, w_glu, b_glu, *, name):
    S, W = y_pre.shape
    tr = _row_tile(S)

    def body(y_ref, w_ref, b_ref, o_ref):
        yg = _gelu(y_ref[...])
        gl = _dot(yg, w_ref[...], 1, 0) + b_ref[...]
        o_ref[...] = (yg * _sigmoid(gl)).astype(BF16)

    row = pl.BlockSpec((tr, W), lambda i: (i, 0))
    return pl.pallas_call(
        body, name=name, out_shape=jax.ShapeDtypeStruct((S, W), BF16), grid=(S // tr,),
        in_specs=[row, pl.BlockSpec((W, W), lambda i: (0, 0)), pl.BlockSpec((1, W), lambda i: (0, 0))],
        out_specs=row, compiler_params=_cparams("parallel"),
    )(y_pre, w_glu, b_glu)


def _glu_bwd(y_pre, do, w_glu, b_glu, *, name):
    S, W = y_pre.shape
    tr = _row_tile(S)

    def body(y_ref, do_ref, w_ref, b_ref, dy_ref, dw_ref, db_ref):
        i = pl.program_id(0)
        yg, dyg_dy = _gelu_and_grad(y_ref[...])
        ygb = yg.astype(BF16)
        sg = _sigmoid(_dot(ygb, w_ref[...], 1, 0) + b_ref[...])
        do = do_ref[...]
        dgl = do * yg * sg * (1.0 - sg)
        dglb = dgl.astype(BF16)
        dyg = do * sg + _dot(dglb, w_ref[...], 1, 1)
        dy_ref[...] = dyg * dyg_dy
        dw = _dot(ygb, dglb, 0, 0)
        db = jnp.sum(dgl, axis=0, keepdims=True)

        @pl.when(i == 0)
        def _():
            dw_ref[...] = dw
            db_ref[...] = db

        @pl.when(i > 0)
        def _():
            dw_ref[...] += dw
            db_ref[...] += db

    row = pl.BlockSpec((tr, W), lambda i: (i, 0))
    full = pl.BlockSpec((W, W), lambda i: (0, 0))
    vec = pl.BlockSpec((1, W), lambda i: (0, 0))
    return pl.pallas_call(
        body, name=name,
        out_shape=(jax.ShapeDtypeStruct((S, W), F32), jax.ShapeDtypeStruct((W, W), F32), jax.ShapeDtypeStruct((1, W), F32)),
        grid=(S // tr,), in_specs=[row, row, full, vec], out_specs=(row, full, vec),
        compiler_params=_cparams("arbitrary"),
    )(y_pre, do, w_glu, b_glu)


GATE_COL0 = 3 * SB_WIDTH + SSM_WIDTH


def _merge_fwd(proj, o_attn, o_ssm, w_ba, w_bs, b_gate, *, name):
    S = proj.shape[0]
    D = D_MODEL
    tr = _pick(S, (256, 128, 64, 32, 16, 8))
    gb = GATE_COL0 // D

    def body(ga_ref, gs_ref, oa_ref, os_ref, wa_ref, ws_ref, ba_ref, bs_ref, m_ref):
        pa = _dot(oa_ref[...], wa_ref[...], 1, 0)
        ps = _dot(os_ref[...], ws_ref[...], 1, 0)
        sa = _sigmoid(ga_ref[...] + ba_ref[...])
        ss = _sigmoid(gs_ref[...] + bs_ref[...])
        m_ref[...] = (sa * pa + ss * ps).astype(BF16)

    return pl.pallas_call(
        body, name=name, out_shape=jax.ShapeDtypeStruct((S, D), BF16), grid=(S // tr,),
        in_specs=[pl.BlockSpec((tr, D), lambda i: (i, gb)), pl.BlockSpec((tr, D), lambda i: (i, gb + 1)),
                  pl.BlockSpec((tr, SB_WIDTH), lambda i: (i, 0)), pl.BlockSpec((tr, SSM_WIDTH), lambda i: (i, 0)),
                  pl.BlockSpec((SB_WIDTH, D), lambda i: (0, 0)), pl.BlockSpec((SSM_WIDTH, D), lambda i: (0, 0)),
                  pl.BlockSpec((1, D), lambda i: (0, 0)), pl.BlockSpec((1, D), lambda i: (0, 1))],
        out_specs=pl.BlockSpec((tr, D), lambda i: (i, 0)),
        compiler_params=_cparams("parallel"),
    )(proj, proj, o_attn, o_ssm, w_ba, w_bs, b_gate, b_gate)


def _merge_bwd(dmerged, proj, o_attn, o_ssm, w_ba, w_bs, b_gate, *, name):
    S = proj.shape[0]
    D = D_MODEL
    tr = _pick(S, (256, 128, 64, 32, 16, 8))
    gb = GATE_COL0 // D

    def body(dm_ref, ga_ref, gs_ref, oa_ref, os_ref, wa_ref, ws_ref, ba_ref, bs_ref,
             doa_ref, dos_ref, dg_ref, db_ref, dwa_ref, dws_ref):
        i = pl.program_id(0)
        dm = dm_ref[...]
        oa = oa_ref[...]
        osm = os_ref[...]
        pa = _dot(oa, wa_ref[...], 1, 0)
        ps = _dot(osm, ws_ref[...], 1, 0)
        sa = _sigmoid(ga_ref[...] + ba_ref[...])
        ss = _sigmoid(gs_ref[...] + bs_ref[...])
        dpa = (dm * sa).astype(BF16)
        dps = (dm * ss).astype(BF16)
        dga = dm * pa * sa * (1.0 - sa)
        dgs = dm * ps * ss * (1.0 - ss)
        dg_ref[:, :D] = dga.astype(BF16)
        dg_ref[:, D:] = dgs.astype(BF16)
        doa_ref[...] = _dot(dpa, wa_ref[...], 1, 1).astype(BF16)
        dos_ref[...] = _dot(dps, ws_ref[...], 1, 1)
        dwa = _dot(oa, dpa, 0, 0)
        dws = _dot(osm, dps, 0, 0)
        dba = jnp.sum(dga, axis=0, keepdims=True)
        dbs = jnp.sum(dgs, axis=0, keepdims=True)

        @pl.when(i == 0)
        def _():
            dwa_ref[...] = dwa
            dws_ref[...] = dws
            db_ref[:, :D] = dba
            db_ref[:, D:] = dbs

        @pl.when(i > 0)
        def _():
            dwa_ref[...] += dwa
            dws_ref[...] += dws
            db_ref[:, :D] += dba
            db_ref[:, D:] += dbs

    rowD = pl.BlockSpec((tr, D), lambda i: (i, 0))
    wspec = pl.BlockSpec((SB_WIDTH, D), lambda i: (0, 0))
    return pl.pallas_call(
        body, name=name,
        out_shape=(jax.ShapeDtypeStruct((S, SB_WIDTH), BF16), jax.ShapeDtypeStruct((S, SSM_WIDTH), F32),
                   jax.ShapeDtypeStruct((S, 2 * D), BF16), jax.ShapeDtypeStruct((1, 2 * D), F32),
                   jax.ShapeDtypeStruct((SB_WIDTH, D), F32), jax.ShapeDtypeStruct((SSM_WIDTH, D), F32)),
        grid=(S // tr,),
        in_specs=[rowD, pl.BlockSpec((tr, D), lambda i: (i, gb)), pl.BlockSpec((tr, D), lambda i: (i, gb + 1)),
                  pl.BlockSpec((tr, SB_WIDTH), lambda i: (i, 0)), pl.BlockSpec((tr, SSM_WIDTH), lambda i: (i, 0)),
                  wspec, wspec, pl.BlockSpec((1, D), lambda i: (0, 0)), pl.BlockSpec((1, D), lambda i: (0, 1))],
        out_specs=(pl.BlockSpec((tr, SB_WIDTH), lambda i: (i, 0)), pl.BlockSpec((tr, SSM_WIDTH), lambda i: (i, 0)),
                   pl.BlockSpec((tr, 2 * D), lambda i: (i, 0)), pl.BlockSpec((1, 2 * D), lambda i: (0, 0)),
                   wspec, wspec),
        compiler_params=_cparams("arbitrary"),
    )(dmerged, proj, proj, o_attn, o_ssm, w_ba, w_bs, b_gate, b_gate)


def _xattn_probs(q, k, h):
    cols = slice(h * XA_HEAD_DIM, (h + 1) * XA_HEAD_DIM)
    s = _dot(q[:, cols], k[:, cols], 1, 1) * (XA_HEAD_DIM ** -0.5)
    s = s - jnp.max(s, axis=-1, keepdims=True)
    e = jnp.exp(s)
    return e / jnp.sum(e, axis=-1, keepdims=True), cols


def _xattn_fwd(q2, k2, v2, *, name):
    S, D = q2.shape
    M = k2.shape[0]
    tr = _row_tile(S)

    def body(q_ref, k_ref, v_ref, o_ref):
        q = q_ref[...]
        k = k_ref[...]
        v = v_ref[...]
        for h in range(XA_HEADS):
            p, cols = _xattn_probs(q, k, h)
            o_ref[:, cols] = _dot(p, v[:, cols], 1, 0).astype(BF16)

    row = pl.BlockSpec((tr, D), lambda i: (i, 0))
    memb = pl.BlockSpec((M, D), lambda i: (0, 0))
    return pl.pallas_call(
        body, name=name, out_shape=jax.ShapeDtypeStruct((S, D), BF16), grid=(S // tr,),
        in_specs=[row, memb, memb], out_specs=row, compiler_params=_cparams("parallel"),
    )(q2, k2, v2)


def _xattn_bwd(q2, k2, v2, do2, *, name):
    S, D = q2.shape
    M = k2.shape[0]
    tr = _row_tile(S)
    scale = XA_HEAD_DIM ** -0.5

    def body(q_ref, k_ref, v_ref, do_ref, dq_ref, dk_ref, dv_ref):
        i = pl.program_id(0)

        @pl.when(i == 0)
        def _():
            dk_ref[...] = jnp.zeros_like(dk_ref)
            dv_ref[...] = jnp.zeros_like(dv_ref)

        q = q_ref[...]
        k = k_ref[...]
        v = v_ref[...]
        do = do_ref[...]
        for h in range(XA_HEADS):
            p, cols = _xattn_probs(q, k, h)
            dp = _dot(do[:, cols], v[:, cols], 1, 1)
            ds = (p * (dp - jnp.sum(dp * p, axis=-1, keepdims=True)) * scale).astype(BF16)
            dq_ref[:, cols] = _dot(ds, k[:, cols], 1, 0).astype(BF16)
            dk_ref[:, cols] += _dot(ds, q[:, cols], 0, 0)
            dv_ref[:, cols] += _dot(p, do[:, cols], 0, 0)

    row = pl.BlockSpec((tr, D), lambda i: (i, 0))
    memb = pl.BlockSpec((M, D), lambda i: (0, 0))
    return pl.pallas_call(
        body, name=name,
        out_shape=(jax.ShapeDtypeStruct((S, D), BF16), jax.ShapeDtypeStruct((M, D), F32), jax.ShapeDtypeStruct((M, D), F32)),
        grid=(S // tr,), in_specs=[row, memb, memb, row], out_specs=(row, memb, memb),
        compiler_params=_cparams("arbitrary"),
    )(q2, k2, v2, do2)


CONV_ROWS = 64
CONV_ROWS_FWD = 256


def _chunk(ref, c, rows):
    return ref[pl.ds(pl.multiple_of(c * rows, rows), rows), :]


def _rows_before(ref, c, rows):
    t0 = pl.multiple_of(jnp.maximum(c * rows - SUBLANES, 0), SUBLANES)
    return jnp.where(c > 0, ref[pl.ds(t0, SUBLANES), :], 0.0)


def _rows_after(ref, c, rows, n_chunks):
    t0 = pl.multiple_of(jnp.minimum((c + 1) * rows, n_chunks * rows - SUBLANES), SUBLANES)
    return jnp.where(c < n_chunks - 1, ref[pl.ds(t0, SUBLANES), :], 0.0)


def _shift_down(cur, before, d):
    out = pltpu.roll(cur, d, 0)
    r = lax.broadcasted_iota(jnp.int32, cur.shape, 0)
    for e in range(d):
        out = jnp.where(r == e, before[SUBLANES - d + e:SUBLANES - d + e + 1, :], out)
    return out


def _shift_up(cur, after, d):
    rows = cur.shape[0]
    out = pltpu.roll(cur, rows - d, 0)
    r = lax.broadcasted_iota(jnp.int32, cur.shape, 0)
    for e in range(d):
        out = jnp.where(r == rows - d + e, after[e:e + 1, :], out)
    return out


def _conv3(cur, before, w_ref, b_ref):
    return (w_ref[2:3, :] * cur + w_ref[1:2, :] * _shift_down(cur, before, 1)
            + w_ref[0:1, :] * _shift_down(cur, before, 2) + b_ref[...])


def _convgate_fwd(up_g, up_v, conv_w, conv_b, *, name):
    S, H = up_g.shape
    nb = H // LANES
    R = min(CONV_ROWS_FWD, S)
    n_chunks = S // R

    def body(g_ref, v_ref, wg_ref, wv_ref, bg_ref, bv_ref, a_ref):
        def chunk(c, _):
            cg = _conv3(_chunk(g_ref, c, R), _rows_before(g_ref, c, R), wg_ref, bg_ref)
            cv = _conv3(_chunk(v_ref, c, R), _rows_before(v_ref, c, R), wv_ref, bv_ref)
            a_ref[pl.ds(pl.multiple_of(c * R, R), R), :] = (_gelu(cg) * cv).astype(BF16)
            return 0

        lax.fori_loop(0, n_chunks, chunk, 0)

    col = lambda off: pl.BlockSpec((S, LANES), lambda j: (0, off + j))
    wcol = lambda off: pl.BlockSpec((3, LANES), lambda j: (0, off + j))
    bcol = lambda off: pl.BlockSpec((1, LANES), lambda j: (0, off + j))
    return pl.pallas_call(
        body, name=name, out_shape=jax.ShapeDtypeStruct((S, H), BF16), grid=(nb,),
        in_specs=[col(0), col(0), wcol(0), wcol(nb), bcol(0), bcol(nb)],
        out_specs=col(0), compiler_params=_cparams("parallel"),
    )(up_g, up_v, conv_w, conv_w, conv_b, conv_b)


def _convgate_bwd(up_g, up_v, da, conv_w, conv_b, *, name):
    S, H = up_g.shape
    nb = H // LANES
    R = min(CONV_ROWS, S)
    n_chunks = S // R

    def fold(a):
        return sum(a[r:r + SUBLANES] for r in range(0, a.shape[0], SUBLANES))

    def body(g_ref, v_ref, da_ref, wg_ref, wv_ref, bg_ref, bv_ref,
             dug_ref, duv_ref, dwg_ref, dwv_ref, dbg_ref, dbv_ref, dcg_s, dcv_s):
        def first_pass(c, acc):
            rows = pl.ds(pl.multiple_of(c * R, R), R)
            ug, uv = _chunk(g_ref, c, R), _chunk(v_ref, c, R)
            bg, bv = _rows_before(g_ref, c, R), _rows_before(v_ref, c, R)
            cg = _conv3(ug, bg, wg_ref, bg_ref)
            cv = _conv3(uv, bv, wv_ref, bv_ref)
            da = da_ref[rows, :]
            gl, dgl = _gelu_and_grad(cg)
            dcg = da * cv * dgl
            dcv = da * gl
            dcg_s[rows, :] = dcg
            dcv_s[rows, :] = dcv
            new = []
            for dc, u, before in ((dcg, ug, bg), (dcv, uv, bv)):
                new += [fold(dc * _shift_down(u, before, 2)), fold(dc * _shift_down(u, before, 1)), fold(dc * u), fold(dc)]
            return tuple(a + n for a, n in zip(acc, new))

        zero = jnp.zeros((SUBLANES, LANES), F32)
        acc = lax.fori_loop(0, n_chunks, first_pass, (zero,) * 8)
        total = [jnp.sum(a, axis=0, keepdims=True) for a in acc]
        for k, (dw_ref, db_ref) in enumerate(((dwg_ref, dbg_ref), (dwv_ref, dbv_ref))):
            dw_ref[0:1, :] = total[4 * k]
            dw_ref[1:2, :] = total[4 * k + 1]
            dw_ref[2:3, :] = total[4 * k + 2]
            db_ref[...] = total[4 * k + 3]

        def second_pass(c, _):
            rows = pl.ds(pl.multiple_of(c * R, R), R)
            for dc_s, w_ref, du_ref in ((dcg_s, wg_ref, dug_ref), (dcv_s, wv_ref, duv_ref)):
                cur, after = _chunk(dc_s, c, R), _rows_after(dc_s, c, R, n_chunks)
                du = w_ref[2:3, :] * cur + w_ref[1:2, :] * _shift_up(cur, after, 1) + w_ref[0:1, :] * _shift_up(cur, after, 2)
                du_ref[rows, :] = du.astype(BF16)
            return 0

        lax.fori_loop(0, n_chunks, second_pass, 0)

    col = lambda off: pl.BlockSpec((S, LANES), lambda j: (0, off + j))
    wcol = lambda off: pl.BlockSpec((3, LANES), lambda j: (0, off + j))
    bcol = lambda off: pl.BlockSpec((1, LANES), lambda j: (0, off + j))
    return pl.pallas_call(
        body, name=name,
        out_shape=(jax.ShapeDtypeStruct((S, H), BF16), jax.ShapeDtypeStruct((S, H), BF16),
                   jax.ShapeDtypeStruct((3, H), F32), jax.ShapeDtypeStruct((3, H), F32),
                   jax.ShapeDtypeStruct((1, H), F32), jax.ShapeDtypeStruct((1, H), F32)),
        grid=(nb,),
        in_specs=[col(0), col(0), col(0), wcol(0), wcol(nb), bcol(0), bcol(nb)],
        out_specs=(col(0), col(0), wcol(0), wcol(0), bcol(0), bcol(0)),
        scratch_shapes=[pltpu.VMEM((S, LANES), F32), pltpu.VMEM((S, LANES), F32)],
        compiler_params=_cparams("parallel"),
    )(up_g, up_v, da, conv_w, conv_w, conv_b, conv_b)


def _local_step(x, mem, target, w_in, late_wire, P, core):
    mm = _matmul
    h1, (w_in,) = _rms_fwd(x, P["norm_mix_pre"], name="rms_mix_pre", rider=_fill_xy([w_in]))
    w_in, = _fill_c([w_in]).run(name="gather_in_c")
    w_in = w_in.reshape((N_DEV,) + w_in.shape[2:])
    n_mid = len(LATE) - len(REDUCE_FFN)
    proj, wire_mid = mm(h1, w_in, name="mm_in", rider=_fill_xy(late_wire[:n_mid]))
    (o_attn, sb_tot, sb_first), wires = _sb_fwd(
        proj, name="sb_fwd", rider=_Exchange.join(_fill_c(wire_mid), _fill_xy(late_wire[n_mid:])))
    wire_mid, wire_ffn = wires[:n_mid], wires[n_mid:]

    ssm_prep = lambda *a: _ssm_prepare(*a)
    (lam_re, lam_im, bb_re, bb_im), prep_vjp = jax.vjp(
        ssm_prep, P["ssm_a_re"], P["ssm_a_im"], P["ssm_log_dt"], P["ssm_b_re"], P["ssm_b_im"])
    tab_f, tab_b = _ssm_tables(lam_re, lam_im)
    bd_re, bd_im = _b_blocks(bb_re), _b_blocks(bb_im)
    cd_re, cd_imneg = _c_blocks(P["ssm_c_re"]), _c_blocks(-P["ssm_c_im"])
    (y_pre, x_re, x_im), wire_ffn = _ssm_fwd(proj, bd_re, bd_im, cd_re, cd_imneg, P["ssm_d"], tab_f,
                                             name="ssm_fwd", rider=_fill_c(wire_ffn))
    W = _weights_from_wire(dict(zip(LATE, list(wire_mid) + list(wire_ffn))))
    W["w_in"] = w_in
    o_ssm = _glu_fwd(y_pre, W["ssm_w_glu"], P["ssm_b_glu"], name="glu_fwd")

    merged = _merge_fwd(proj, o_attn, o_ssm, W["w_branch_attn"], W["w_branch_ssm"], P["b_gate"], name="merge_fwd")
    mo = mm(merged, W["w_out"], name="mm_out")
    x1, h2 = _resnorm_norm(x, mo, P["norm_mix_post"], P["norm_xa_pre"], name="resnorm_1")

    mem_n = _rms_fwd(mem, P["norm_mem"], name="rms_mem")
    q2 = mm(h2, W["xa_wq"], out_dtype=BF16, name="mm_xq")
    k2 = mm(mem_n, W["xa_wk"], out_dtype=BF16, name="mm_xk")
    v2 = mm(mem_n, W["xa_wv"], out_dtype=BF16, name="mm_xv")
    o2 = _xattn_fwd(q2, k2, v2, name="xattn_fwd")
    xa = mm(o2, W["xa_wo"], name="mm_xo")
    x2, h3 = _resnorm_norm(x1, xa, P["norm_xa_post"], P["norm_ffn_pre"], name="resnorm_2")

    half = N_DEV // 2
    up_g = mm(h3, W["ffn_w_up"], n_blocks=half, name="mm_up_g")
    up_v = mm(h3, W["ffn_w_up"], b_block0=half, name="mm_up_v")
    act = _convgate_fwd(up_g, up_v, W["ffn_conv_w"], P["ffn_conv_b"], name="convgate_fwd")
    f = mm(act, W["ffn_w_down"], name="mm_down")
    loss, dy, df, dg_ffn_post = _final_loss(x2, f, P["norm_ffn_post"], target, name="final_loss")

    G = {"norm_ffn_post": dg_ffn_post}
    dact = mm(df, W["ffn_w_down"], tb=True, name="mm_down_dx")
    G["ffn_w_down"] = mm(act, df, ta=True, name="mm_down_dw")
    dug, duv, dwg, dwv, dbg, dbv = _convgate_bwd(up_g, up_v, dact, W["ffn_conv_w"], P["ffn_conv_b"], name="convgate_bwd")
    G["ffn_conv_w"] = jnp.concatenate([dwg, dwv], axis=1)
    G["ffn_conv_b"] = jnp.concatenate([dbg, dbv], axis=1)
    dh3 = mm(dug, W["ffn_w_up"], tb=True, n_blocks=half, name="mm_up_g_dx")
    dh3 = mm(duv, W["ffn_w_up"], tb=True, b_block0=half, acc_in=dh3, name="mm_up_v_dx")
    dw_up = mm(h3, dug, ta=True, out_into=lax.empty(W["ffn_w_up"].shape, F32), name="mm_up_g_dw")
    G["ffn_w_up"] = mm(h3, duv, ta=True, out_into=dw_up, out_block0=half, name="mm_up_v_dw")
    blocks = {n: _grad_blocks(n, G[n]) for n in REDUCE_FFN}
    (dx2, dxa, G["norm_ffn_pre"], G["norm_xa_post"]), from_core = _norm_bwd_pair(
        dy, dh3, x2, P["norm_ffn_pre"], xa, P["norm_xa_post"], name="norm_bwd_3",
        rider=_send_c([blocks[n] for n in REDUCE_FFN]))
    pair = {n: _pair_sum(blocks[n], r, core, name="pair_sum_" + n) for n, r in zip(REDUCE_FFN, from_core)}

    G["xa_wo"] = mm(o2, dxa, ta=True, name="mm_xo_dw")
    do2 = mm(dxa, W["xa_wo"], tb=True, out_dtype=BF16, name="mm_xo_dx")
    dq2, dk2, dv2 = _xattn_bwd(q2, k2, v2, do2, name="xattn_bwd")
    G["xa_wq"] = mm(h2, dq2, ta=True, name="mm_xq_dw")
    dh2 = mm(dq2, W["xa_wq"], tb=True, name="mm_xq_dx")
    G["xa_wk"] = mm(mem_n, dk2, ta=True, name="mm_xk_dw")
    G["xa_wv"] = mm(mem_n, dv2, ta=True, name="mm_xv_dw")
    dmem_n = jnp.concatenate([dk2, dv2], axis=1)
    wkv = jnp.concatenate([W["xa_wk"], W["xa_wv"]], axis=1)
    dmem = mm(dmem_n, wkv, tb=True, name="mm_xkv_dx")
    _, G["norm_mem"] = _norm_bwd_single(None, dmem, mem, P["norm_mem"], name="norm_bwd_mem")
    (dx1, dmo, G["norm_xa_pre"], G["norm_mix_post"]), _ = _norm_bwd_pair(
        dx2, dh2, x1, P["norm_xa_pre"], mo, P["norm_mix_post"], name="norm_bwd_2")

    G["w_out"] = mm(merged, dmo, ta=True, name="mm_out_dw")
    dmerged = mm(dmo, W["w_out"], tb=True, name="mm_out_dx")
    do_attn, do_ssm, dgate, G["b_gate"], G["w_branch_attn"], G["w_branch_ssm"] = _merge_bwd(
        dmerged, proj, o_attn, o_ssm, W["w_branch_attn"], W["w_branch_ssm"], P["b_gate"], name="merge_bwd")
    dy_pre, G["ssm_w_glu"], G["ssm_b_glu"] = _glu_bwd(y_pre, do_ssm, W["ssm_w_glu"], P["ssm_b_glu"], name="glu_bwd")
    blocks.update({n: _grad_blocks(n, G[n]) for n in REDUCE_MID})
    (du, dbd_re, dbd_im, dcd_re, dcd_imneg, G["ssm_d"], dl_re, dl_im), brought = _ssm_bwd(
        dy_pre, proj, x_re, x_im, bd_re, bd_im, cd_re, cd_imneg, P["ssm_d"], tab_b, name="ssm_bwd",
        rider=_Exchange.join(_send_c([blocks[n] for n in REDUCE_MID]), _scatter_xy([pair[n] for n in REDUCE_FFN])))
    from_core, from_chips = brought[:len(REDUCE_MID)], brought[len(REDUCE_MID):]
    reduced = {n: (pair[n], parts) for n, parts in zip(REDUCE_FFN, from_chips)}
    pair.update({n: _pair_sum(blocks[n], r, core, name="pair_sum_" + n) for n, r in zip(REDUCE_MID, from_core)})
    G["ssm_c_re"] = _c_from_blocks(dcd_re)
    G["ssm_c_im"] = -_c_from_blocks(dcd_imneg)
    dlam_re = jnp.sum(dl_re, axis=1).reshape(SSM_GROUPS, SSM_STATE)
    dlam_im = jnp.sum(dl_im, axis=1).reshape(SSM_GROUPS, SSM_STATE)
    (G["ssm_a_re"], G["ssm_a_im"], G["ssm_log_dt"], G["ssm_b_re"], G["ssm_b_im"]) = prep_vjp(
        (dlam_re, dlam_im, _bbar_from_blocks(dbd_re), _bbar_from_blocks(dbd_im)))
    G["ffn_conv_b"] = G["ffn_conv_b"].reshape(N_DEV, FF_LOCAL_PAD)[:, :FF_LOCAL]
    small = [G[n].reshape(SMALL_SHAPE[n]) for n in SMALL_EARLY]
    (dq, dk, dv), brought = _sb_bwd(
        proj, sb_tot, sb_first, do_attn, name="sb_bwd",
        rider=_Exchange.join(_scatter_xy([pair[n] for n in REDUCE_MID]), _gather_xy_from(small)))
    from_chips, small = brought[:len(REDUCE_MID)], brought[len(REDUCE_MID):]
    reduced.update({n: (pair[n], parts) for n, parts in zip(REDUCE_MID, from_chips)})
    dproj = jnp.concatenate([dq, dk, dv, du, dgate], axis=1)
    G["w_in"], small = mm(h1, dproj, ta=True, out_cb=W["w_in"].shape[2], name="mm_in_dw", rider=_fill_c(small))
    g_in = _grad_blocks("w_in", G["w_in"])
    dh1, (from_core,) = mm(dproj, W["w_in"], tb=True, name="mm_in_dx", rider=_send_c([g_in]))
    pair_in = _pair_sum(g_in, from_core, core, name="pair_sum_w_in")
    (grad_x, dg_pre), (from_chips,) = _norm_bwd_single(dx1, dh1, x, P["norm_mix_pre"], name="norm_bwd_1",
                                                       rider=_scatter_xy([pair_in]))
    reduced["w_in"] = (pair_in, from_chips)
    last, = _gather_all([dg_pre]).run(name="gather_g_last")
    parts = dict(zip(SMALL_EARLY, small))
    parts["norm_mix_pre"] = last
    return loss, grad_x, parts, reduced


MESH = pl.DeviceIdType.MESH
_HBM = pl.BlockSpec(memory_space=pl.ANY)
N_XY = 4
N_XY_PEERS = 3


def _xy_peers(x, y):
    return [(1 - x, y), (x, 1 - y), (1 - x, 1 - y)]


class _Exchange:
    def __init__(self, arrays, out_shapes, plan, n_copies, alias):
        self.arrays = list(arrays)
        self.out_shapes = list(out_shapes)
        self.plan = plan
        self.n_copies = n_copies
        self.alias = list(alias) if isinstance(alias, (list, tuple)) else [alias] * len(self.arrays)

    @property
    def n(self):
        return len(self.arrays)

    def aliases(self, first_in, first_out):
        return {first_in + k: first_out + k for k in range(self.n) if self.alias[k]}

    @staticmethod
    def join(a, b):
        def plan(k, src, dst, x, y, c):
            return a.plan(k, src, dst, x, y, c) if k < a.n else b.plan(k - a.n, src, dst, x, y, c)

        return _Exchange(a.arrays + b.arrays, a.out_shapes + b.out_shapes, plan, max(a.n_copies, b.n_copies),
                         a.alias + b.alias)

    def sems(self):
        shape = (self.n, self.n_copies)
        return [pltpu.SemaphoreType.DMA(shape), pltpu.SemaphoreType.DMA(shape)]

    def _copies(self, ins, outs, send_sems, recv_sems):
        x, y, c = lax.axis_index("x"), lax.axis_index("y"), lax.axis_index("c")
        sends, lands, own = [], [], []
        for k in range(self.n):
            for j, (src, dst, dev, land) in enumerate(self.plan(k, ins[k], outs[k], x, y, c)):
                if dev is None:
                    own.append(pltpu.make_async_copy(src, dst, send_sems.at[k, j]))
                    continue
                sems = dict(send_sem=send_sems.at[k, j], recv_sem=recv_sems.at[k, j], device_id=dev, device_id_type=MESH)
                sends.append(pltpu.make_async_remote_copy(src_ref=src, dst_ref=dst, **sems))
                lands.append(pltpu.make_async_remote_copy(src_ref=src, dst_ref=land, **sems))
        return sends, lands, own

    def start(self, ins, outs, send_sems, recv_sems):
        sends, _, own = self._copies(ins, outs, send_sems, recv_sems)
        for cp in own + sends:
            cp.start()

    def finish(self, ins, outs, send_sems, recv_sems):
        sends, lands, own = self._copies(ins, outs, send_sems, recv_sems)
        for cp in lands:
            cp.wait_recv()
        for cp in sends:
            cp.wait_send()
        for cp in own:
            cp.wait()

    def run(self, *, name):
        n = self.n

        def body(*refs):
            parts = (refs[:n], refs[n:2 * n], refs[2 * n], refs[2 * n + 1])
            self.start(*parts)
            self.finish(*parts)

        return pl.pallas_call(
            body, name=name, out_shape=tuple(self.out_shapes),
            in_specs=[_HBM] * n, out_specs=tuple([_HBM] * n),
            input_output_aliases=self.aliases(0, 0),
            scratch_shapes=self.sems(),
        )(*self.arrays)


def _call(host_body, *, name, grid, in_specs, out_specs, out_shape, scratch_shapes, operands, rider=None):
    out_specs, out_shape = tuple(out_specs), tuple(out_shape)
    if rider is None:
        res = pl.pallas_call(
            host_body, name=name, grid=grid, in_specs=list(in_specs), out_specs=out_specs, out_shape=out_shape,
            scratch_shapes=list(scratch_shapes), compiler_params=_cparams(*["arbitrary"] * len(grid)),
        )(*operands)
        return tuple(res), None
    n, n_in, n_out, n_scr = rider.n, len(in_specs), len(out_specs), len(scratch_shapes)

    def body(*refs):
        pos = [0]

        def take(count):
            pos[0] += count
            return refs[pos[0] - count:pos[0]]

        h_in, r_in, h_out, r_out, h_scr = take(n_in), take(n), take(n_out), take(n), take(n_scr)
        send_sems, recv_sems = take(2)
        ids = [pl.program_id(a) for a in range(len(grid))]
        first = functools.reduce(jnp.logical_and, [i == 0 for i in ids])
        last = functools.reduce(jnp.logical_and, [i == g - 1 for i, g in zip(ids, grid)])

        @pl.when(first)
        def _():
            rider.start(r_in, r_out, send_sems, recv_sems)

        host_body(*h_in, *h_out, *h_scr)

        @pl.when(last)
        def _():
            rider.finish(r_in, r_out, send_sems, recv_sems)

    res = pl.pallas_call(
        body, name=name, grid=grid,
        in_specs=list(in_specs) + [_HBM] * n, out_specs=out_specs + tuple([_HBM] * n),
        out_shape=out_shape + tuple(rider.out_shapes),
        input_output_aliases=rider.aliases(n_in, n_out),
        scratch_shapes=list(scratch_shapes) + rider.sems(),
        compiler_params=_cparams(*["arbitrary"] * len(grid)),
    )(*operands, *rider.arrays)
    return tuple(res[:n_out]), list(res[n_out:])


def _same(arrays):
    return [jax.ShapeDtypeStruct(a.shape, a.dtype) for a in arrays]


def _fill_xy(bufs):
    def plan(k, src, dst, x, y, c):
        mine = 2 * x + y
        return [(src.at[mine, c], dst.at[mine, c], (px, py, c), dst.at[2 * px + py, c]) for px, py in _xy_peers(x, y)]

    return _Exchange(bufs, _same(bufs), plan, N_XY_PEERS, alias=True)


def _fill_c(bufs):
    def plan(k, src, dst, x, y, c):
        return [(src.at[:, c], dst.at[:, c], (x, y, 1 - c), dst.at[:, 1 - c])]

    return _Exchange(bufs, _same(bufs), plan, 1, alias=True)


def _slots(arrays):
    return [jax.ShapeDtypeStruct((N_XY, 2) + a.shape, a.dtype) for a in arrays]


def _gather_xy_from(srcs):
    def plan(k, src, dst, x, y, c):
        mine = 2 * x + y
        return ([(src, dst.at[mine, c], None, None)]
                + [(src, dst.at[mine, c], (px, py, c), dst.at[2 * px + py, c]) for px, py in _xy_peers(x, y)])

    return _Exchange(srcs, _slots(srcs), plan, 1 + N_XY_PEERS, alias=False)


def _gather_all(srcs):
    def plan(k, src, dst, x, y, c):
        mine = 2 * x + y
        out = [(src, dst.at[mine, c], None, None)]
        for fx, fy, fc in [(a, b, e) for a in (0, 1) for b in (0, 1) for e in (0, 1)][1:]:
            px, py, pc = (1 - x) if fx else x, (1 - y) if fy else y, (1 - c) if fc else c
            out.append((src, dst.at[mine, c], (px, py, pc), dst.at[2 * px + py, pc]))
        return out

    return _Exchange(srcs, _slots(srcs), plan, N_DEV, alias=False)


def _send_c(srcs):
    def plan(k, src, dst, x, y, c):
        return [(src.at[:, 1 - c], dst, (x, y, 1 - c), dst)]

    outs = [jax.ShapeDtypeStruct(a.shape[:1] + a.shape[2:], a.dtype) for a in srcs]
    return _Exchange(srcs, outs, plan, 1, alias=False)


def _scatter_xy(srcs):
    def plan(k, src, dst, x, y, c):
        return [(src.at[2 * px + py], dst.at[j], (px, py, c), dst.at[j]) for j, (px, py) in enumerate(_xy_peers(x, y))]

    outs = [jax.ShapeDtypeStruct((N_XY_PEERS,) + a.shape[1:], a.dtype) for a in srcs]
    return _Exchange(srcs, outs, plan, N_XY_PEERS, alias=False)


WIRE_DTYPE = BF16


def _pair_sum(g8, recv, core, *, name):
    n, _, R, C = g8.shape
    tr = _pick(R, (128, 64, 32, 16, 8))

    def body(core_ref, a_ref, b_ref, o_ref):
        o_ref[...] = (a_ref[0] + b_ref[...]).astype(WIRE_DTYPE)

    return pl.pallas_call(
        body, name=name, out_shape=jax.ShapeDtypeStruct((n, R, C), WIRE_DTYPE),
        grid_spec=pltpu.PrefetchScalarGridSpec(
            num_scalar_prefetch=1, grid=(n, R // tr),
            in_specs=[pl.BlockSpec((1, 1, tr, C), lambda s, i, core_ref: (s, core_ref[0], i, 0)),
                      pl.BlockSpec((1, tr, C), lambda s, i, core_ref: (s, i, 0))],
            out_specs=pl.BlockSpec((1, tr, C), lambda s, i, core_ref: (s, i, 0))),
        compiler_params=_cparams("parallel", "parallel"),
    )(core, g8, recv)


def _adamw_math(w, g, m, v):
    m = ADAM_B1 * m + (1.0 - ADAM_B1) * g
    v = ADAM_B2 * v + (1.0 - ADAM_B2) * (g * g)
    m_hat = m / (1.0 - ADAM_B1 ** ADAM_STEP)
    v_hat = v / (1.0 - ADAM_B2 ** ADAM_STEP)
    delta = -ADAM_LR * (m_hat / (jnp.sqrt(v_hat) + ADAM_EPS) + ADAM_WD * w)
    return delta, m, v


def _reduce_adamw(parts, w, m, v, *, own, own_slot, name):
    n, R, C = parts.shape
    tr = _pick(R, (128, 64, 32, 16, 8))

    def body(_, own_ref, parts_ref, w_ref, m_ref, v_ref, g_ref, d_ref, nm_ref, nv_ref):
        g = own_ref[0].astype(F32)
        for k in range(n):
            g = g + parts_ref[k].astype(F32)
        g_ref[...] = g
        d_ref[...], nm_ref[...], nv_ref[...] = _adamw_math(w_ref[...], g, m_ref[...], v_ref[...])

    out = jax.ShapeDtypeStruct((R, C), F32)
    row = pl.BlockSpec((tr, C), lambda i, s: (i, 0))
    return pl.pallas_call(
        body, name=name, out_shape=(out, out, out, out),
        grid_spec=pltpu.PrefetchScalarGridSpec(
            num_scalar_prefetch=1, grid=(R // tr,),
            in_specs=[pl.BlockSpec((1, tr, C), lambda i, s: (s[0], i, 0)),
                      pl.BlockSpec((n, tr, C), lambda i, s: (0, i, 0)), row, row, row],
            out_specs=(row, row, row, row)),
        compiler_params=_cparams("parallel"),
    )(own_slot, own, parts, w, m, v)


SHARDED = (("w_in", (1024, 4096), 1), ("ssm_w_glu", (512, 512), 0), ("w_branch_attn", (512, 1024), 1),
           ("w_branch_ssm", (512, 1024), 1), ("w_out", (1024, 1024), 0), ("xa_wq", (1024, 1024), 0),
           ("xa_wk", (1024, 1024), 0), ("xa_wv", (1024, 1024), 0), ("xa_wo", (1024, 1024), 0),
           ("ffn_w_up", (1024, 5632), 1), ("ffn_conv_w", (3, 5632), 1), ("ffn_w_down", (2816, 1024), 0))
REPLICATED = (("norm_mix_pre", (1024,)), ("norm_mix_post", (1024,)), ("b_gate", (2048,)), ("ssm_a_re", (32, 64)),
              ("ssm_a_im", (32, 64)), ("ssm_log_dt", (32,)), ("ssm_b_re", (32, 64, 16)), ("ssm_b_im", (32, 64, 16)),
              ("ssm_c_re", (32, 16, 64)), ("ssm_c_im", (32, 16, 64)), ("ssm_d", (512,)), ("ssm_b_glu", (512,)),
              ("norm_xa_pre", (1024,)), ("norm_xa_post", (1024,)), ("norm_mem", (1024,)), ("norm_ffn_pre", (1024,)),
              ("norm_ffn_post", (1024,)), ("ffn_conv_b", (5632,)))
PARAM_ORDER = ("norm_mix_pre", "norm_mix_post", "w_in", "b_gate", "ssm_a_re", "ssm_a_im", "ssm_log_dt", "ssm_b_re",
               "ssm_b_im", "ssm_c_re", "ssm_c_im", "ssm_d", "ssm_w_glu", "ssm_b_glu", "w_branch_attn", "w_branch_ssm",
               "w_out", "norm_xa_pre", "norm_xa_post", "norm_mem", "xa_wq", "xa_wk", "xa_wv", "xa_wo", "norm_ffn_pre",
               "norm_ffn_post", "ffn_w_up", "ffn_conv_w", "ffn_conv_b", "ffn_w_down")
FF_LOCAL = 2 * D_FF // N_DEV
FF_LOCAL_PAD = 768
FF_PAD = (N_DEV // 2) * FF_LOCAL_PAD


def _local_shape(shape, axis):
    return tuple(s // N_DEV if a == axis else s for a, s in enumerate(shape))


def _pad_cols(a, width):
    return jnp.pad(a, [(0, 0)] * (a.ndim - 1) + [(0, width - a.shape[-1])])


def _blocks_to_cols(a8):
    return a8.transpose(1, 0, 2).reshape(a8.shape[1], N_DEV * a8.shape[2])


def _cols_to_blocks(a, cb):
    return a.reshape(a.shape[0], N_DEV, cb).transpose(1, 0, 2)


FF_PADDED = ("ffn_w_up", "ffn_conv_w")
LATE = tuple(n for n, _, _ in SHARDED if n != "w_in")
REDUCE_FFN = ("ffn_w_up", "ffn_conv_w", "ffn_w_down")
REDUCE_MID = ("xa_wo", "xa_wq", "xa_wk", "xa_wv", "w_out", "w_branch_attn", "w_branch_ssm", "ssm_w_glu")
SHARD_AXIS = {n: ax for n, _, ax in SHARDED}
FULL_SHAPE = {n: s for n, s, _ in SHARDED}


def _as_local(n, a):
    return _pad_cols(a, FF_LOCAL_PAD) if n in FF_PADDED else a


def _weights_from_wire(wire):
    full = {n: b.reshape((N_DEV,) + b.shape[2:]) for n, b in wire.items()}
    W = {n: a.reshape(FULL_SHAPE[n]) if SHARD_AXIS[n] == 0 else a for n, a in full.items()}
    for n in ("w_branch_attn", "w_branch_ssm", "ffn_conv_w"):
        W[n] = _blocks_to_cols(full[n])
    W["ffn_w_down"] = jnp.pad(W["ffn_w_down"].reshape(N_DEV // 2, FF_LOCAL, D_MODEL),
                              ((0, 0), (0, FF_LOCAL_PAD - FF_LOCAL), (0, 0))).reshape(FF_PAD, D_MODEL)
    return W


def _grad_blocks(n, g):
    if n in ("w_branch_attn", "w_branch_ssm"):
        g = _cols_to_blocks(g, D_MODEL // N_DEV)
    elif n == "ffn_conv_w":
        g = _cols_to_blocks(g, FF_LOCAL_PAD)
    elif n == "ffn_w_down":
        g = g.reshape(N_DEV // 2, FF_LOCAL_PAD, D_MODEL)[:, :FF_LOCAL]
    local = _local_shape(FULL_SHAPE[n], SHARD_AXIS[n])
    if n in FF_PADDED:
        local = local[:-1] + (FF_LOCAL_PAD,)
    return g.reshape((N_XY, 2) + local)


SMALL_SHAPE = {n: (1, s[0]) if len(s) == 1 else (s[0], math.prod(s[1:])) for n, s in REPLICATED}
SMALL_SHAPE["ffn_conv_b"] = (N_DEV, FF_LOCAL)
SMALL_EARLY = tuple(n for n, _ in REPLICATED if n != "norm_mix_pre")


def _adamw_replicated(parts, w, m, v, *, name):
    n = len(parts)

    def body(*refs):
        p_refs, w_refs, m_refs, v_refs = (refs[i * n:(i + 1) * n] for i in range(4))
        outs = refs[4 * n:]
        for k in range(n):
            g = p_refs[k][0, 0]
            for s in range(1, N_DEV):
                g = g + p_refs[k][s // 2, s % 2]
            d, nm, nv = _adamw_math(w_refs[k][...], g, m_refs[k][...], v_refs[k][...])
            for slot, val in enumerate((g, d, nm, nv)):
                outs[slot * n + k][...] = val

    vmem = pl.BlockSpec(memory_space=pltpu.VMEM)
    shapes = [jax.ShapeDtypeStruct(a.shape, F32) for a in w] * 4
    res = pl.pallas_call(
        body, name=name, out_shape=tuple(shapes), in_specs=[vmem] * (4 * n), out_specs=tuple([vmem] * (4 * n)),
        compiler_params=pltpu.CompilerParams(vmem_limit_bytes=VMEM_LIMIT),
    )(*parts, *w, *m, *v)
    return [list(res[i * n:(i + 1) * n]) for i in range(4)]


def kernel(x, mem, norm_mix_pre, norm_mix_post, w_in, b_gate, ssm_a_re, ssm_a_im, ssm_log_dt, ssm_b_re, ssm_b_im, ssm_c_re, ssm_c_im, ssm_d, ssm_w_glu, ssm_b_glu, w_branch_attn, w_branch_ssm, w_out, norm_xa_pre, norm_xa_post, norm_mem, xa_wq, xa_wk, xa_wv, xa_wo, norm_ffn_pre, norm_ffn_post, ffn_w_up, ffn_conv_w, ffn_conv_b, ffn_w_down, loss_target, m_norm_mix_pre, m_norm_mix_post, m_w_in, m_b_gate, m_ssm_a_re, m_ssm_a_im, m_ssm_log_dt, m_ssm_b_re, m_ssm_b_im, m_ssm_c_re, m_ssm_c_im, m_ssm_d, m_ssm_w_glu, m_ssm_b_glu, m_w_branch_attn, m_w_branch_ssm, m_w_out, m_norm_xa_pre, m_norm_xa_post, m_norm_mem, m_xa_wq, m_xa_wk, m_xa_wv, m_xa_wo, m_norm_ffn_pre, m_norm_ffn_post, m_ffn_w_up, m_ffn_conv_w, m_ffn_conv_b, m_ffn_w_down, v_norm_mix_pre, v_norm_mix_post, v_w_in, v_b_gate, v_ssm_a_re, v_ssm_a_im, v_ssm_log_dt, v_ssm_b_re, v_ssm_b_im, v_ssm_c_re, v_ssm_c_im, v_ssm_d, v_ssm_w_glu, v_ssm_b_glu, v_w_branch_attn, v_w_branch_ssm, v_w_out, v_norm_xa_pre, v_norm_xa_post, v_norm_mem, v_xa_wq, v_xa_wk, v_xa_wv, v_xa_wo, v_norm_ffn_pre, v_norm_ffn_post, v_ffn_w_up, v_ffn_conv_w, v_ffn_conv_b, v_ffn_w_down):
    args = dict(locals())
    w_loc = {n: args[n][0] for n in PARAM_ORDER}
    m_loc = {n: args["m_" + n][0] for n in PARAM_ORDER}
    v_loc = {n: args["v_" + n][0] for n in PARAM_ORDER}
    core_i = lax.axis_index("c")
    chip_i = 2 * lax.axis_index("x") + lax.axis_index("y")
    core = core_i.astype(jnp.int32).reshape(1)
    chip = chip_i.astype(jnp.int32).reshape(1)

    def in_place(a):
        buf = lax.empty((N_XY, 2) + a.shape, a.dtype)
        return lax.dynamic_update_slice(buf, a[None, None], (chip_i, core_i) + (0,) * a.ndim)

    as_wire = lambda n: in_place(_as_local(n, w_loc[n]).astype(F32 if n == "ffn_conv_w" else BF16))

    P = {}
    for n, shape in REPLICATED:
        P[n] = w_loc[n] if len(shape) > 1 or n == "ssm_log_dt" else w_loc[n].reshape(1, -1)
    P["ffn_conv_b"] = _pad_cols(w_loc["ffn_conv_b"].reshape(N_DEV, FF_LOCAL), FF_LOCAL_PAD).reshape(1, 2 * FF_PAD)

    loss, grad_x, small_parts, reduced = _local_step(x[0], mem[0], loss_target[0], as_wire("w_in"),
                                                     [as_wire(n) for n in LATE], P, core)
    loss = lax.psum(loss[0, 0], ("x", "y", "c"))

    big_out = {}
    for n, (own, parts) in reduced.items():
        res = _reduce_adamw(parts, _as_local(n, w_loc[n]), _as_local(n, m_loc[n]), _as_local(n, v_loc[n]),
                            own=own, own_slot=chip, name="adamw_" + n)
        big_out[n] = [r[:, :FF_LOCAL] if n in FF_PADDED else r for r in res]

    names = [n for n, _ in REPLICATED]
    as_small = lambda d: [d[n].reshape(SMALL_SHAPE[n]) for n in names]
    small_out = _adamw_replicated([small_parts[n] for n in names], as_small(w_loc), as_small(m_loc), as_small(v_loc),
                                  name="adamw_replicated")
    small_out = [dict(zip(names, res)) for res in small_out]

    outs = [loss, grad_x[None]]
    for k in range(4):
        for n in PARAM_ORDER:
            src = big_out[n][k] if n in big_out else small_out[k][n]
            outs.append(src.reshape(args[n].shape))
    return tuple(outs)
```

```python
import functools
import math

import jax
import jax.numpy as jnp
from jax import lax
from jax.experimental import pallas as pl
from jax.experimental.pallas import tpu as pltpu

F32 = jnp.float32
BF16 = jnp.bfloat16

D_MODEL = 1024
SB_HEADS = 8
SB_HEAD_DIM = 64
SB_WIDTH = 512
SSM_WIDTH = 512
SSM_GROUP = 16
SSM_GROUPS = 32
SSM_STATE = 64
XA_HEADS = 4
XA_HEAD_DIM = 256
D_FF = 2816
RMS_EPS = 1e-6
IN_WIDTH = 4096
N_DEV = 8

ADAM_LR = 0.001
ADAM_B1 = 0.9
ADAM_B2 = 0.999
ADAM_EPS = 1e-08
ADAM_WD = 0.01
ADAM_STEP = 10

LANES = 128
SUBLANES = 8
VMEM_LIMIT = 48 * 1024 * 1024

_GELU_C = math.sqrt(2.0 / math.pi)


def _cparams(*sem):
    return pltpu.CompilerParams(dimension_semantics=sem, vmem_limit_bytes=VMEM_LIMIT)


def _pick(n, cands):
    for c in cands:
        if n % c == 0:
            return c
    return n


def _gelu(x):
    return 0.5 * x * (1.0 + jnp.tanh(_GELU_C * (x + 0.044715 * x * x * x)))


def _gelu_and_grad(x):
    t = jnp.tanh(_GELU_C * (x + 0.044715 * x * x * x))
    g = 0.5 * x * (1.0 + t)
    dg = 0.5 * (1.0 + t) + 0.5 * x * (1.0 - t * t) * _GELU_C * (1.0 + 3.0 * 0.044715 * x * x)
    return g, dg


def _sigmoid(x):
    return 1.0 / (1.0 + jnp.exp(-x))


def _dot(a, b, ca, cb):
    return lax.dot_general(a.astype(BF16), b.astype(BF16), (((ca,), (cb,)), ((), ())),
                           preferred_element_type=F32)


MM_TILES = (1024, 768, 512, 256, 128)
MM_K_TILES = (2048, 1536) + MM_TILES
MM_PAIR = 2
MM_WIDE = 1536


def _matmul(a, b, *, ta=False, tb=False, out_dtype=F32, name, b_block0=0, n_blocks=None,
            out_cb=None, out_into=None, out_block0=0, acc_in=None, rider=None):
    if ta:
        K, M = a.shape
    else:
        M, K = a.shape
    b_cb = None
    if b.ndim == 3:
        b_cb = b.shape[2]
        n_blocks = b.shape[0] - b_block0 if n_blocks is None else n_blocks
        N, K2 = (b.shape[1], n_blocks * b_cb) if tb else (n_blocks * b_cb, b.shape[1])
    elif tb:
        N, K2 = b.shape
    else:
        K2, N = b.shape
    assert K == K2, (a.shape, b.shape, ta, tb)
    if out_into is not None:
        out_cb = out_into.shape[2]
    tm = _pick(M, MM_TILES)
    pair = lambda cb_, count: MM_PAIR if (cb_ * MM_PAIR <= MM_WIDE and count % MM_PAIR == 0) else 1
    b_pair = pair(b_cb, n_blocks) if b_cb else 1
    o_pair = pair(out_cb, N // out_cb) if out_cb else 1
    if b_cb and not tb:
        tn = b_cb * b_pair
    elif out_cb:
        tn = out_cb * o_pair
    else:
        tn = _pick(N, MM_TILES)
    if b_cb and tb:
        tk = b_cb * b_pair
    else:
        tk = _pick(K, MM_TILES if tn > MM_TILES[0] else MM_K_TILES)
    nk = K // tk
    ca, cb = (0 if ta else 1), (1 if tb else 0)
    has_acc = acc_in is not None
    has_into = out_into is not None

    def body(*refs):
        a_ref, b_ref = refs[0], refs[1]
        pos = 2
        c_ref = None
        if has_acc:
            c_ref = refs[pos]
            pos += 1
        if has_into:
            pos += 1
        o_ref = refs[pos]
        b_tile = b_ref[...] if b_cb is None else jnp.concatenate([b_ref[t] for t in range(b_pair)], axis=1)
        p = _dot(a_ref[...], b_tile, ca, cb)

        def write(val):
            val = val.astype(out_dtype)
            if out_cb is None:
                o_ref[...] = val
            else:
                for t in range(o_pair):
                    o_ref[t] = val[:, t * out_cb:(t + 1) * out_cb]

        if nk == 1:
            write((p + c_ref[...]) if has_acc else p)
        else:
            acc_ref = refs[pos + 1]
            k = pl.program_id(2)

            @pl.when(k == 0)
            def _():
                acc_ref[...] = (p + c_ref[...]) if has_acc else p

            @pl.when(k > 0)
            def _():
                acc_ref[...] += p

            @pl.when(k == nk - 1)
            def _():
                write(acc_ref[...])

    nj, ni = N // tn, M // tm
    a_bytes, b_bytes = a.size * a.dtype.itemsize, K * N * b.dtype.itemsize
    n_outer = a_bytes * nj + b_bytes * (1 if nk == 1 else ni) <= a_bytes * (1 if nk == 1 else nj) + b_bytes * ni
    grid = (nj, ni, nk) if n_outer else (ni, nj, nk)

    def spec(block, index):
        return pl.BlockSpec(block, (lambda g0, g1, k: index(g0, g1, k)) if n_outer else (lambda g0, g1, k: index(g1, g0, k)))

    a_spec = spec((tk, tm), lambda j, i, k: (k, i)) if ta else spec((tm, tk), lambda j, i, k: (i, k))
    if b_cb is None:
        b_spec = spec((tn, tk), lambda j, i, k: (j, k)) if tb else spec((tk, tn), lambda j, i, k: (k, j))
    elif tb:
        b_spec = spec((b_pair, tn, b_cb), lambda j, i, k: (b_block0 // b_pair + k, j, 0))
    else:
        b_spec = spec((b_pair, tk, b_cb), lambda j, i, k: (b_block0 // b_pair + j, k, 0))
    in_specs = [a_spec, b_spec]
    operands = [a, b]
    aliases = {}
    if has_acc:
        in_specs.append(spec((tm, tn), lambda j, i, k: (i, j)))
        operands.append(acc_in)
    if has_into:
        aliases = {len(operands): 0}
        in_specs.append(pl.BlockSpec(memory_space=pl.ANY))
        operands.append(out_into)
    if out_cb is None:
        out_shape = jax.ShapeDtypeStruct((M, N), out_dtype)
        out_spec = spec((tm, tn), lambda j, i, k: (i, j))
    else:
        out_shape = (jax.ShapeDtypeStruct(out_into.shape, out_into.dtype) if has_into
                     else jax.ShapeDtypeStruct((N // out_cb, M, out_cb), out_dtype))
        out_spec = spec((o_pair, tm, out_cb), lambda j, i, k: (out_block0 // o_pair + j, i, 0))
    if rider is not None:
        assert not has_into
        (out,), brought = _call(body, name=name, rider=rider, grid=grid, in_specs=in_specs,
                                out_specs=(out_spec,), out_shape=(out_shape,), operands=operands,
                                scratch_shapes=[] if nk == 1 else [pltpu.VMEM((tm, tn), F32)])
        return out, brought
    return pl.pallas_call(
        body, name=name, out_shape=out_shape,
        grid=grid,
        in_specs=in_specs, out_specs=out_spec, input_output_aliases=aliases,
        scratch_shapes=[] if nk == 1 else [pltpu.VMEM((tm, tn), F32)],
        compiler_params=_cparams("parallel", "parallel", "arbitrary"),
    )(*operands)


def _rms(x, g):
    r = lax.rsqrt(jnp.mean(x * x, axis=-1, keepdims=True) + RMS_EPS)
    return x * r * g


def _rms_bwd(dy, x, g):
    r = lax.rsqrt(jnp.mean(x * x, axis=-1, keepdims=True) + RMS_EPS)
    xh = x * r
    dxh = dy * g
    dx = r * (dxh - xh * jnp.mean(dxh * xh, axis=-1, keepdims=True))
    dg = jnp.sum(dy * xh, axis=0, keepdims=True)
    return dx, dg


def _row_tile(rows):
    return _pick(rows, (512, 256, 128, 64, 32, 16, 8))


def _rms_fwd(x, g, *, name, rider=None):
    R, D = x.shape
    tr = _row_tile(R)

    def body(x_ref, g_ref, h_ref):
        h_ref[...] = _rms(x_ref[...], g_ref[...]).astype(BF16)

    (h,), brought = _call(
        body, name=name, rider=rider, out_shape=(jax.ShapeDtypeStruct((R, D), BF16),), grid=(R // tr,),
        in_specs=[pl.BlockSpec((tr, D), lambda i: (i, 0)), pl.BlockSpec((1, D), lambda i: (0, 0))],
        out_specs=(pl.BlockSpec((tr, D), lambda i: (i, 0)),), scratch_shapes=[], operands=(x, g))
    return h if rider is None else (h, brought)


def _resnorm_norm(x, z, g_post, g_next, *, name):
    R, D = x.shape
    tr = _row_tile(R)

    def body(x_ref, z_ref, gp_ref, gn_ref, xn_ref, h_ref):
        xn = x_ref[...] + _rms(z_ref[...], gp_ref[...])
        xn_ref[...] = xn
        h_ref[...] = _rms(xn, gn_ref[...]).astype(BF16)

    row = pl.BlockSpec((tr, D), lambda i: (i, 0))
    vec = pl.BlockSpec((1, D), lambda i: (0, 0))
    return pl.pallas_call(
        body, name=name,
        out_shape=(jax.ShapeDtypeStruct((R, D), F32), jax.ShapeDtypeStruct((R, D), BF16)),
        grid=(R // tr,), in_specs=[row, row, vec, vec], out_specs=(row, row),
        compiler_params=_cparams("parallel"),
    )(x, z, g_post, g_next)


def _final_loss(x, z, g_post, target, *, name):
    R, D = x.shape
    tr = _row_tile(R)

    def body(x_ref, z_ref, gp_ref, t_ref, loss_ref, dy_ref, dz_ref, dg_ref):
        i = pl.program_id(0)
        z = z_ref[...]
        g = gp_ref[...]
        err = x_ref[...] + _rms(z, g) - t_ref[...]
        dy = err * (1.0 / D)
        dy_ref[...] = dy
        dz, dg = _rms_bwd(dy, z, g)
        dz_ref[...] = dz.astype(BF16)
        part = 0.5 * jnp.sum(jnp.sum(err * err, axis=-1, keepdims=True) * (1.0 / D), axis=0, keepdims=True)

        @pl.when(i == 0)
        def _():
            loss_ref[...] = part
            dg_ref[...] = dg

        @pl.when(i > 0)
        def _():
            loss_ref[...] += part
            dg_ref[...] += dg

    row = pl.BlockSpec((tr, D), lambda i: (i, 0))
    vec = pl.BlockSpec((1, D), lambda i: (0, 0))
    return pl.pallas_call(
        body, name=name,
        out_shape=(jax.ShapeDtypeStruct((1, 1), F32), jax.ShapeDtypeStruct((R, D), F32),
                   jax.ShapeDtypeStruct((R, D), BF16), jax.ShapeDtypeStruct((1, D), F32)),
        grid=(R // tr,), in_specs=[row, row, vec, row],
        out_specs=(pl.BlockSpec((1, 1), lambda i: (0, 0)), row, row, vec),
        compiler_params=_cparams("arbitrary"),
    )(x, z, g_post, target)


def _norm_bwd_pair(dres, dh, xk, g_pre, zprev, g_prev_post, *, name, rider=None):
    R, D = xk.shape
    tr = _row_tile(R)

    def body(dres_ref, dh_ref, x_ref, gpre_ref, z_ref, gpost_ref, dx_ref, dz_ref, dgpre_ref, dgpost_ref):
        i = pl.program_id(0)
        d1, dgpre = _rms_bwd(dh_ref[...], x_ref[...], gpre_ref[...])
        dx = dres_ref[...] + d1
        dx_ref[...] = dx
        dz, dgpost = _rms_bwd(dx, z_ref[...], gpost_ref[...])
        dz_ref[...] = dz.astype(BF16)

        @pl.when(i == 0)
        def _():
            dgpre_ref[...] = dgpre
            dgpost_ref[...] = dgpost

        @pl.when(i > 0)
        def _():
            dgpre_ref[...] += dgpre
            dgpost_ref[...] += dgpost

    row = pl.BlockSpec((tr, D), lambda i: (i, 0))
    vec = pl.BlockSpec((1, D), lambda i: (0, 0))
    return _call(
        body, name=name, rider=rider,
        out_shape=(jax.ShapeDtypeStruct((R, D), F32), jax.ShapeDtypeStruct((R, D), BF16),
                   jax.ShapeDtypeStruct((1, D), F32), jax.ShapeDtypeStruct((1, D), F32)),
        grid=(R // tr,), in_specs=[row, row, row, vec, row, vec], out_specs=(row, row, vec, vec),
        scratch_shapes=[], operands=(dres, dh, xk, g_pre, zprev, g_prev_post))


def _norm_bwd_single(dres, dh, xk, g_pre, *, name, rider=None):
    R, D = xk.shape
    tr = _row_tile(R)
    has_res = dres is not None

    def body(*refs):
        if has_res:
            dres_ref, dh_ref, x_ref, gpre_ref, dx_ref, dgpre_ref = refs
        else:
            dh_ref, x_ref, gpre_ref, dx_ref, dgpre_ref = refs
        i = pl.program_id(0)
        d1, dgpre = _rms_bwd(dh_ref[...], x_ref[...], gpre_ref[...])
        dx_ref[...] = dres_ref[...] + d1 if has_res else d1

        @pl.when(i == 0)
        def _():
            dgpre_ref[...] = dgpre

        @pl.when(i > 0)
        def _():
            dgpre_ref[...] += dgpre

    row = pl.BlockSpec((tr, D), lambda i: (i, 0))
    vec = pl.BlockSpec((1, D), lambda i: (0, 0))
    ins = ([dres] if has_res else []) + [dh, xk, g_pre]
    res, brought = _call(
        body, name=name, rider=rider,
        out_shape=(jax.ShapeDtypeStruct((R, D), F32), jax.ShapeDtypeStruct((1, D), F32)),
        grid=(R // tr,), in_specs=([row] if has_res else []) + [row, row, vec], out_specs=(row, vec),
        scratch_shapes=[], operands=ins)
    return res if rider is None else (res, brought)


SB_BLOCK = 256
SB_QBLOCK = 512
SB_DEAD = -104.0


def _sb_tri(kind):
    r = lax.broadcasted_iota(jnp.int32, (SB_BLOCK, SB_BLOCK), 0)
    c = lax.broadcasted_iota(jnp.int32, (SB_BLOCK, SB_BLOCK), 1)
    keep = {"after": r > c, "before": r < c}[kind]
    return jnp.where(keep, 1.0, 0.0).astype(BF16)


def _sb_scores(qm, k_blk):
    z = _dot(qm, k_blk, 1, 1)
    sp = jnp.maximum(z, 0.0) + jnp.log(1.0 + jnp.exp(-jnp.abs(z)))
    return z, sp


def _sb_causal(rows):
    r = lax.broadcasted_iota(jnp.int32, (rows, SB_BLOCK), 0)
    c = lax.broadcasted_iota(jnp.int32, (rows, SB_BLOCK), 1)
    return c < r


def _head_masks():
    lane = lax.broadcasted_iota(jnp.int32, (1, LANES), 1)
    return [jnp.where(lane < SB_HEAD_DIM, 1.0, 0.0), jnp.where(lane >= SB_HEAD_DIM, 1.0, 0.0)]


def _sb_fwd(proj, *, name, rider=None):
    S = proj.shape[0]
    T = SB_BLOCK
    TQ = min(SB_QBLOCK, S)
    span = TQ // T
    nq = S // TQ
    npair = SB_WIDTH // LANES
    scale = SB_HEAD_DIM ** -0.5

    def body(q_ref, k_ref, v_ref, o_ref, tot_ref, first_ref, acc_ref, run_ref):
        masks = _head_masks()
        tri = _sb_tri("after")
        first_ref[...] = jnp.zeros_like(first_ref)
        slot = lax.broadcasted_iota(jnp.int32, first_ref.shape, 1)

        def alive():
            reach = jnp.maximum(jnp.max(run_ref[0]), jnp.max(run_ref[1]))
            return (reach > SB_DEAD).astype(jnp.int32)

        def q_block(i, _):
            qrow = pl.ds(pl.multiple_of(i * TQ, TQ), TQ)
            q = q_ref[qrow, :] * scale
            qm = [(q * m).astype(BF16) for m in masks]
            acc_ref[...] = jnp.zeros_like(acc_ref)
            run_ref[...] = jnp.zeros_like(run_ref)

            def k_block(j, own):
                krow = pl.ds(pl.multiple_of(j * T, T), T)
                k_blk = k_ref[krow, :].astype(BF16)
                v_blk = v_ref[krow, :].astype(BF16)
                r0 = 0 if own is None else own * T
                rows = pl.ds(r0, TQ - r0)
                for h in range(2):
                    z, sp = _sb_scores(qm[h][r0:], k_blk)
                    causal = None if own is None else _sb_causal(TQ - r0)
                    lf = -sp if causal is None else jnp.where(causal, -sp, 0.0)
                    e = jnp.exp(z - sp + _dot(lf, tri, 1, 0) + run_ref[h, rows])
                    w = e if causal is None else jnp.where(causal, e, 0.0)
                    acc_ref[h, rows] += _dot(w, v_blk, 1, 0)
                    run_ref[h, rows] += jnp.sum(lf, axis=1, keepdims=True)

            for d in reversed(range(span)):
                k_block(i * span + d, d)

            def below(carry):
                jj, _ = carry
                k_block(i * span - 1 - jj, None)
                return jj + 1, alive()

            done, _ = lax.fori_loop(0, i * span, lambda _, c: below(c), (jnp.int32(0), alive()))
            o_ref[qrow, :] = (acc_ref[0] * masks[0] + acc_ref[1] * masks[1]).astype(BF16)
            tot_ref[qrow, :] = run_ref[0] * masks[0] + run_ref[1] * masks[1]
            first_ref[...] = jnp.where(slot == i, (i * span - done).astype(F32), first_ref[...])
            return 0

        lax.fori_loop(0, nq, q_block, 0)

    blk = lambda off: pl.BlockSpec((S, LANES), lambda p: (0, off + p))
    return _call(
        body, name=name, rider=rider,
        out_shape=(jax.ShapeDtypeStruct((S, SB_WIDTH), BF16), jax.ShapeDtypeStruct((S, SB_WIDTH), F32),
                   jax.ShapeDtypeStruct((npair, SUBLANES, LANES), F32)),
        grid=(npair,),
        in_specs=[blk(0), blk(npair), blk(2 * npair)],
        out_specs=(blk(0), blk(0), pl.BlockSpec((1, SUBLANES, LANES), lambda p: (p, 0, 0))),
        scratch_shapes=[pltpu.VMEM((2, TQ, LANES), F32), pltpu.VMEM((2, TQ, 1), F32)],
        operands=(proj, proj, proj))


def _sb_bwd(proj, tot, first, do_attn, *, name, rider=None):
    S = proj.shape[0]
    T = SB_BLOCK
    TQ = min(SB_QBLOCK, S)
    span = TQ // T
    nq = S // TQ
    npair = SB_WIDTH // LANES
    scale = SB_HEAD_DIM ** -0.5

    def body(q_ref, k_ref, v_ref, tot_ref, first_ref, do_ref, dq_ref, dk_ref, dv_ref,
             dqacc_ref, dkacc_ref, dvacc_ref, run_ref, grun_ref):
        masks = _head_masks()
        tri_after = _sb_tri("after")
        tri_before = _sb_tri("before")
        dkacc_ref[...] = jnp.zeros_like(dkacc_ref)
        dvacc_ref[...] = jnp.zeros_like(dvacc_ref)
        slot = lax.broadcasted_iota(jnp.int32, first_ref.shape, 1)

        def q_block(i, _):
            qrow = pl.ds(pl.multiple_of(i * TQ, TQ), TQ)
            q = q_ref[qrow, :] * scale
            do = do_ref[qrow, :].astype(F32)
            tot = tot_ref[qrow, :]
            qm = [(q * m).astype(BF16) for m in masks]
            dom = [(do * m).astype(BF16) for m in masks]
            ltot = [jnp.sum(tot * m, axis=1, keepdims=True) * (1.0 / SB_HEAD_DIM) for m in masks]
            dqacc_ref[...] = jnp.zeros_like(dqacc_ref)
            run_ref[...] = jnp.zeros_like(run_ref)
            grun_ref[...] = jnp.zeros_like(grun_ref)

            def k_block(j, own):
                krow = pl.ds(pl.multiple_of(j * T, T), T)
                k_blk = k_ref[krow, :].astype(BF16)
                v_blk = v_ref[krow, :].astype(BF16)
                r0 = 0 if own is None else own * T
                rows = pl.ds(r0, TQ - r0)
                for h in range(2):
                    z, sp = _sb_scores(qm[h][r0:], k_blk)
                    causal = None if own is None else _sb_causal(TQ - r0)
                    lf = -sp if causal is None else jnp.where(causal, -sp, 0.0)
                    lsum = jnp.sum(lf, axis=1, keepdims=True)
                    later = (ltot[h][r0:] - run_ref[h, rows] - lsum) + _dot(lf, tri_after, 1, 0)
                    beta = jnp.exp(z - sp)
                    w = jnp.exp(z - sp + later)
                    if causal is not None:
                        w = jnp.where(causal, w, 0.0)
                    g = _dot(dom[h][r0:], v_blk, 1, 1) * w
                    gbefore = grun_ref[h, rows] + _dot(g, tri_before, 1, 0)
                    dz = g - beta * (g + gbefore)
                    if causal is not None:
                        dz = jnp.where(causal, dz, 0.0)
                    dz = dz.astype(BF16)
                    dqacc_ref[h, rows] += _dot(dz, k_blk, 1, 0)
                    dkacc_ref[krow, :] += _dot(dz, qm[h][r0:], 0, 0)
                    dvacc_ref[krow, :] += _dot(w, dom[h][r0:], 0, 0)
                    run_ref[h, rows] += lsum
                    grun_ref[h, rows] += jnp.sum(g, axis=1, keepdims=True)

            def above(j, _):
                k_block(j, None)
                return 0

            lax.fori_loop(0, i * span, above, 0)
            for d in range(span):
                k_block(i * span + d, d)
            dq_ref[qrow, :] = ((dqacc_ref[0] * masks[0] + dqacc_ref[1] * masks[1]) * scale).astype(BF16)
            return 0

        lax.fori_loop(0, nq, q_block, 0)
        dk_ref[...] = dkacc_ref[...].astype(BF16)
        dv_ref[...] = dvacc_ref[...].astype(BF16)

    blk = lambda off: pl.BlockSpec((S, LANES), lambda p: (0, off + p))
    out = jax.ShapeDtypeStruct((S, SB_WIDTH), BF16)
    return _call(
        body, name=name, rider=rider, out_shape=(out, out, out), grid=(npair,),
        in_specs=[blk(0), blk(npair), blk(2 * npair), blk(0), pl.BlockSpec((1, SUBLANES, LANES), lambda p: (p, 0, 0)),
                  blk(0)],
        out_specs=(blk(0), blk(0), blk(0)),
        scratch_shapes=[pltpu.VMEM((2, TQ, LANES), F32), pltpu.VMEM((S, LANES), F32), pltpu.VMEM((S, LANES), F32),
                        pltpu.VMEM((2, TQ, 1), F32), pltpu.VMEM((2, TQ, 1), F32)],
        operands=(proj, proj, proj, tot, first, do_attn))


SSM_HALVES = 2
SSM_HALF_CH = SSM_WIDTH // SSM_HALVES
SSM_HALF_ST = SSM_GROUPS * SSM_STATE // SSM_HALVES
SSM_CHUNK = 512


def _cmul(ar, ai, br, bi):
    return ar * br - ai * bi, ar * bi + ai * br


def _ssm_tables(lam_re, lam_im):
    lr = lam_re.reshape(-1)
    li = lam_im.reshape(-1)
    pows = [(jnp.ones_like(lr), jnp.zeros_like(li)), (lr, li)]
    for _ in range(2, SUBLANES + 1):
        pows.append(_cmul(pows[-1][0], pows[-1][1], lr, li))
    row = jnp.arange(SUBLANES)[:, None]

    def shift_tab(d, keep):
        return [jnp.where(keep, pows[d][0][None, :], 0.0), jnp.where(keep, pows[d][1][None, :], 0.0)]

    fwd, bwd = [], []
    for d in (1, 2, 4):
        fwd += shift_tab(d, row >= d)
        bwd += shift_tab(d, row + d < SUBLANES)
    fwd += [jnp.stack([pows[r + 1][0] for r in range(SUBLANES)]), jnp.stack([pows[r + 1][1] for r in range(SUBLANES)])]
    bwd += [jnp.stack([pows[SUBLANES - r][0] for r in range(SUBLANES)]),
            jnp.stack([pows[SUBLANES - r][1] for r in range(SUBLANES)])]

    def halves(tabs):
        t = jnp.stack(tabs)
        return t.reshape(8, SUBLANES, SSM_HALVES, SSM_HALF_ST).transpose(2, 0, 1, 3)

    return halves(fwd), halves(bwd)


def _expand_groups(blocks, rows, cols):
    gh = SSM_GROUPS // SSM_HALVES
    R, C = gh * rows, gh * cols
    rep = (lax.broadcasted_iota(jnp.int32, (cols, C), 1) & (cols - 1)) == lax.broadcasted_iota(jnp.int32, (cols, C), 0)
    wide = _dot(blocks, jnp.where(rep, 1.0, 0.0), 1, 0)
    r = lax.broadcasted_iota(jnp.int32, (R, C), 0) >> (rows.bit_length() - 1)
    c = lax.broadcasted_iota(jnp.int32, (R, C), 1) >> (cols.bit_length() - 1)
    return jnp.where(r == c, wide, 0.0).astype(BF16)


def _collect_groups(full, rows, cols):
    R, C = full.shape
    r = lax.broadcasted_iota(jnp.int32, (R, C), 0) >> (rows.bit_length() - 1)
    c = lax.broadcasted_iota(jnp.int32, (R, C), 1) >> (cols.bit_length() - 1)
    rest = jnp.where(r == c, full, 0.0)
    fold = (lax.broadcasted_iota(jnp.int32, (C, cols), 0) & (cols - 1)) == lax.broadcasted_iota(jnp.int32, (C, cols), 1)
    fold = jnp.where(fold, 1.0, 0.0)
    out = jnp.zeros((R, cols), F32)
    for _ in range(3):
        piece = rest.astype(BF16)
        out = out + _dot(piece, fold, 1, 0)
        rest = rest - piece.astype(F32)
    return out


def _ssm_expand(bre_ref, bim_ref, cre_ref, cim_ref, bd_re_s, bd_im_s, cd_re_s, cd_im_s):
    bd_re_s[...] = _expand_groups(bre_ref[0], SSM_GROUP, SSM_STATE)
    bd_im_s[...] = _expand_groups(bim_ref[0], SSM_GROUP, SSM_STATE)
    cd_re_s[...] = _expand_groups(cre_ref[0], SSM_STATE, SSM_GROUP)
    cd_im_s[...] = _expand_groups(cim_ref[0], SSM_STATE, SSM_GROUP)


def _ssm_matrix_specs():
    b = pl.BlockSpec((1, SSM_HALF_CH, SSM_STATE), lambda h, c: (h, 0, 0))
    c = pl.BlockSpec((1, SSM_HALF_ST, SSM_GROUP), lambda h, c: (h, 0, 0))
    return [b, b, c, c]


def _ssm_matrix_scratch():
    return [pltpu.VMEM((SSM_HALF_CH, SSM_HALF_ST), BF16), pltpu.VMEM((SSM_HALF_CH, SSM_HALF_ST), BF16),
            pltpu.VMEM((SSM_HALF_ST, SSM_HALF_CH), BF16), pltpu.VMEM((SSM_HALF_ST, SSM_HALF_CH), BF16)]


def _ssm_fwd(proj, b_re, b_im, c_re, c_imneg, d_skip, tab, *, name, rider=None):
    S = proj.shape[0]
    Tc = min(SSM_CHUNK, S)
    nc = S // Tc
    u_blk0 = (3 * SB_WIDTH) // SSM_HALF_CH

    def body(u_ref, bre_ref, bim_ref, cre_ref, cim_ref, d_ref, tab_ref, y_ref, xre_ref, xim_ref, cre_s, cim_s,
             bd_re_s, bd_im_s, cd_re_s, cd_im_s):
        c = pl.program_id(1)

        @pl.when(c == 0)
        def _():
            cre_s[...] = jnp.zeros_like(cre_s)
            cim_s[...] = jnp.zeros_like(cim_s)
            _ssm_expand(bre_ref, bim_ref, cre_ref, cim_ref, bd_re_s, bd_im_s, cd_re_s, cd_im_s)

        u = u_ref[...]
        ub = u.astype(BF16)
        xre_ref[...] = _dot(ub, bd_re_s[...], 1, 0)
        xim_ref[...] = _dot(ub, bd_im_s[...], 1, 0)

        def slab(k, carry):
            car_re, car_im = carry
            rows = pl.ds(pl.multiple_of(k * SUBLANES, SUBLANES), SUBLANES)
            sre = xre_ref[rows, :]
            sim = xim_ref[rows, :]
            for n, d in enumerate((1, 2, 4)):
                pre, pim = tab_ref[0, 2 * n], tab_ref[0, 2 * n + 1]
                rre = pltpu.roll(sre, d, 0)
                rim = pltpu.roll(sim, d, 0)
                sre, sim = sre + (pre * rre - pim * rim), sim + (pre * rim + pim * rre)
            pre, pim = tab_ref[0, 6], tab_ref[0, 7]
            sre, sim = sre + (pre * car_re - pim * car_im), sim + (pre * car_im + pim * car_re)
            xre_ref[rows, :] = sre
            xim_ref[rows, :] = sim
            last = (SUBLANES - 1, SUBLANES)
            return (jnp.broadcast_to(sre[last[0]:last[1], :], sre.shape),
                    jnp.broadcast_to(sim[last[0]:last[1], :], sim.shape))

        car = lax.fori_loop(0, Tc // SUBLANES, slab, (cre_s[...], cim_s[...]))
        cre_s[...] = car[0]
        cim_s[...] = car[1]
        y = _dot(xre_ref[...], cd_re_s[...], 1, 0) + _dot(xim_ref[...], cd_im_s[...], 1, 0)
        y_ref[...] = y + d_ref[...] * u

    return _call(
        body, name=name, rider=rider,
        out_shape=(jax.ShapeDtypeStruct((S, SSM_WIDTH), F32),
                   jax.ShapeDtypeStruct((S, SSM_HALVES * SSM_HALF_ST), F32),
                   jax.ShapeDtypeStruct((S, SSM_HALVES * SSM_HALF_ST), F32)),
        grid=(SSM_HALVES, nc),
        in_specs=[pl.BlockSpec((Tc, SSM_HALF_CH), lambda h, c: (c, u_blk0 + h))] + _ssm_matrix_specs()
                 + [pl.BlockSpec((1, SSM_HALF_CH), lambda h, c: (0, h)),
                    pl.BlockSpec((1, 8, SUBLANES, SSM_HALF_ST), lambda h, c: (h, 0, 0, 0))],
        out_specs=(pl.BlockSpec((Tc, SSM_HALF_CH), lambda h, c: (c, h)),
                   pl.BlockSpec((Tc, SSM_HALF_ST), lambda h, c: (c, h)),
                   pl.BlockSpec((Tc, SSM_HALF_ST), lambda h, c: (c, h))),
        scratch_shapes=[pltpu.VMEM((SUBLANES, SSM_HALF_ST), F32), pltpu.VMEM((SUBLANES, SSM_HALF_ST), F32)]
                       + _ssm_matrix_scratch(),
        operands=(proj, b_re, b_im, c_re, c_imneg, d_skip, tab))


def _ssm_bwd(dy, proj, x_re, x_im, b_re, b_im, c_re, c_imneg, d_skip, tab, *, name, rider=None):
    S = proj.shape[0]
    Tc = min(SSM_CHUNK, S)
    nc = S // Tc
    u_blk0 = (3 * SB_WIDTH) // SSM_HALF_CH

    def body(dy_ref, u_ref, xre_ref, xim_ref, bre_ref, bim_ref, cre_ref, cim_ref, d_ref, tab_ref,
             du_ref, dbre_ref, dbim_ref, dcre_ref, dcim_ref, dd_ref, dlre_ref, dlim_ref,
             gre_s, gim_s, cre_s, cim_s, bd_re_s, bd_im_s, cd_re_s, cd_im_s, dbre_s, dbim_s, dcre_s, dcim_s):
        c = pl.program_id(1)

        @pl.when(c == 0)
        def _():
            _ssm_expand(bre_ref, bim_ref, cre_ref, cim_ref, bd_re_s, bd_im_s, cd_re_s, cd_im_s)
            cre_s[...] = jnp.zeros_like(cre_s)
            cim_s[...] = jnp.zeros_like(cim_s)
            dbre_s[...] = jnp.zeros_like(dbre_s)
            dbim_s[...] = jnp.zeros_like(dbim_s)
            dcre_s[...] = jnp.zeros_like(dcre_s)
            dcim_s[...] = jnp.zeros_like(dcim_s)
            dd_ref[...] = jnp.zeros_like(dd_ref)
            dlre_ref[...] = jnp.zeros_like(dlre_ref)
            dlim_ref[...] = jnp.zeros_like(dlim_ref)

        dy = dy_ref[...]
        dyb = dy.astype(BF16)
        u = u_ref[...]
        gre_s[...] = _dot(dyb, cd_re_s[...], 1, 1)
        gim_s[...] = _dot(dyb, cd_im_s[...], 1, 1)
        row = lax.broadcasted_iota(jnp.int32, (SUBLANES, SSM_HALF_ST), 0)
        nslab = Tc // SUBLANES

        def slab(kk, carry):
            car_re, car_im, acc_re, acc_im = carry
            k = nslab - 1 - kk
            rows = pl.ds(pl.multiple_of(k * SUBLANES, SUBLANES), SUBLANES)
            sre = gre_s[rows, :]
            sim = gim_s[rows, :]
            for n, d in enumerate((1, 2, 4)):
                pre, pim = tab_ref[0, 2 * n], tab_ref[0, 2 * n + 1]
                rre = pltpu.roll(sre, SUBLANES - d, 0)
                rim = pltpu.roll(sim, SUBLANES - d, 0)
                sre, sim = sre + (pre * rre + pim * rim), sim + (pre * rim - pim * rre)
            pre, pim = tab_ref[0, 6], tab_ref[0, 7]
            sre, sim = sre + (pre * car_re + pim * car_im), sim + (pre * car_im - pim * car_re)
            gre_s[rows, :] = sre
            gim_s[rows, :] = sim
            nre = jnp.where(row == SUBLANES - 1, car_re, pltpu.roll(sre, SUBLANES - 1, 0))
            nim = jnp.where(row == SUBLANES - 1, car_im, pltpu.roll(sim, SUBLANES - 1, 0))
            xr = xre_ref[rows, :]
            xi = xim_ref[rows, :]
            acc_re = acc_re + (nre * xr + nim * xi)
            acc_im = acc_im + (nim * xr - nre * xi)
            return (jnp.broadcast_to(sre[0:1, :], sre.shape), jnp.broadcast_to(sim[0:1, :], sim.shape), acc_re, acc_im)

        car = lax.fori_loop(0, nslab, slab, (cre_s[...], cim_s[...], dlre_ref[0], dlim_ref[0]))
        cre_s[...] = car[0]
        cim_s[...] = car[1]
        dlre_ref[0] = car[2]
        dlim_ref[0] = car[3]
        gre = gre_s[...].astype(BF16)
        gim = gim_s[...].astype(BF16)
        ub = u.astype(BF16)
        du = _dot(gre, bd_re_s[...], 1, 1) + _dot(gim, bd_im_s[...], 1, 1) + d_ref[...] * dy
        du_ref[...] = du.astype(BF16)
        dbre_s[...] += _dot(ub, gre, 0, 0)
        dbim_s[...] += _dot(ub, gim, 0, 0)
        dcre_s[...] += _dot(xre_ref[...], dyb, 0, 0)
        dcim_s[...] += _dot(xim_ref[...], dyb, 0, 0)
        dd_ref[...] += jnp.sum(dy * u, axis=0, keepdims=True)

        @pl.when(c == nc - 1)
        def _():
            dbre_ref[0] = _collect_groups(dbre_s[...], SSM_GROUP, SSM_STATE)
            dbim_ref[0] = _collect_groups(dbim_s[...], SSM_GROUP, SSM_STATE)
            dcre_ref[0] = _collect_groups(dcre_s[...], SSM_STATE, SSM_GROUP)
            dcim_ref[0] = _collect_groups(dcim_s[...], SSM_STATE, SSM_GROUP)

    rev = lambda c: nc - 1 - c
    return _call(
        body, name=name, rider=rider,
        out_shape=(jax.ShapeDtypeStruct((S, SSM_WIDTH), BF16),
                   jax.ShapeDtypeStruct((SSM_HALVES, SSM_HALF_CH, SSM_STATE), F32),
                   jax.ShapeDtypeStruct((SSM_HALVES, SSM_HALF_CH, SSM_STATE), F32),
                   jax.ShapeDtypeStruct((SSM_HALVES, SSM_HALF_ST, SSM_GROUP), F32),
                   jax.ShapeDtypeStruct((SSM_HALVES, SSM_HALF_ST, SSM_GROUP), F32),
                   jax.ShapeDtypeStruct((1, SSM_WIDTH), F32),
                   jax.ShapeDtypeStruct((SSM_HALVES, SUBLANES, SSM_HALF_ST), F32),
                   jax.ShapeDtypeStruct((SSM_HALVES, SUBLANES, SSM_HALF_ST), F32)),
        grid=(SSM_HALVES, nc),
        in_specs=[pl.BlockSpec((Tc, SSM_HALF_CH), lambda h, c: (rev(c), h)),
                  pl.BlockSpec((Tc, SSM_HALF_CH), lambda h, c: (rev(c), u_blk0 + h)),
                  pl.BlockSpec((Tc, SSM_HALF_ST), lambda h, c: (rev(c), h)),
                  pl.BlockSpec((Tc, SSM_HALF_ST), lambda h, c: (rev(c), h))] + _ssm_matrix_specs()
                 + [pl.BlockSpec((1, SSM_HALF_CH), lambda h, c: (0, h)),
                    pl.BlockSpec((1, 8, SUBLANES, SSM_HALF_ST), lambda h, c: (h, 0, 0, 0))],
        out_specs=(pl.BlockSpec((Tc, SSM_HALF_CH), lambda h, c: (rev(c), h)), *_ssm_matrix_specs(),
                   pl.BlockSpec((1, SSM_HALF_CH), lambda h, c: (0, h)),
                   pl.BlockSpec((1, SUBLANES, SSM_HALF_ST), lambda h, c: (h, 0, 0)),
                   pl.BlockSpec((1, SUBLANES, SSM_HALF_ST), lambda h, c: (h, 0, 0))),
        scratch_shapes=[pltpu.VMEM((Tc, SSM_HALF_ST), F32), pltpu.VMEM((Tc, SSM_HALF_ST), F32),
                        pltpu.VMEM((SUBLANES, SSM_HALF_ST), F32), pltpu.VMEM((SUBLANES, SSM_HALF_ST), F32)]
                       + _ssm_matrix_scratch()
                       + [pltpu.VMEM((SSM_HALF_CH, SSM_HALF_ST), F32), pltpu.VMEM((SSM_HALF_CH, SSM_HALF_ST), F32),
                          pltpu.VMEM((SSM_HALF_ST, SSM_HALF_CH), F32), pltpu.VMEM((SSM_HALF_ST, SSM_HALF_CH), F32)],
        operands=(dy, proj, x_re, x_im, b_re, b_im, c_re, c_imneg, d_skip, tab))


def _ssm_prepare(a_re, a_im, log_dt, b_re, b_im):
    dt = jnp.exp(log_dt)[:, None]
    mag = jnp.exp(a_re * dt)
    lre = mag * jnp.cos(a_im * dt)
    lim = mag * jnp.sin(a_im * dt)
    den = a_re * a_re + a_im * a_im
    fre = ((lre - 1.0) * a_re + lim * a_im) / den
    fim = (lim * a_re - (lre - 1.0) * a_im) / den
    bbre = fre[:, :, None] * b_re - fim[:, :, None] * b_im
    bbim = fre[:, :, None] * b_im + fim[:, :, None] * b_re
    return lre, lim, bbre, bbim


def _b_blocks(bbar):
    gh = SSM_GROUPS // SSM_HALVES
    b = bbar.reshape(SSM_HALVES, gh, SSM_STATE, SSM_GROUP).transpose(0, 1, 3, 2)
    return b.reshape(SSM_HALVES, SSM_HALF_CH, SSM_STATE)


def _bbar_from_blocks(db):
    gh = SSM_GROUPS // SSM_HALVES
    return db.reshape(SSM_HALVES, gh, SSM_GROUP, SSM_STATE).transpose(0, 1, 3, 2).reshape(SSM_GROUPS, SSM_STATE, SSM_GROUP)


def _c_blocks(cmat):
    gh = SSM_GROUPS // SSM_HALVES
    c = cmat.reshape(SSM_HALVES, gh, SSM_GROUP, SSM_STATE).transpose(0, 1, 3, 2)
    return c.reshape(SSM_HALVES, SSM_HALF_ST, SSM_GROUP)


def _c_from_blocks(dc):
    gh = SSM_GROUPS // SSM_HALVES
    return dc.reshape(SSM_HALVES, gh, SSM_STATE, SSM_GROUP).transpose(0, 1, 3, 2).reshape(SSM_GROUPS, SSM_GROUP, SSM_STATE)


def _glu_fwd(y_pre, w_glu, b_glu, *, name):
    S, W = y_pre.shape
    tr = _row_tile(S)

    def body(y_ref, w_ref, b_ref, o_ref):
        yg = _gelu(y_ref[...])
        gl = _dot(yg, w_ref[...], 1, 0) + b_ref[...]
        o_ref[...] = (yg * _sigmoid(gl)).astype(BF16)

    row = pl.BlockSpec((tr, W), lambda i: (i, 0))
    return pl.pallas_call(
        body, name=name, out_shape=jax.ShapeDtypeStruct((S, W), BF16), grid=(S // tr,),
        in_specs=[row, pl.BlockSpec((W, W), lambda i: (0, 0)), pl.BlockSpec((1, W), lambda i: (0, 0))],
        out_specs=row, compiler_params=_cparams("parallel"),
    )(y_pre, w_glu, b_glu)


def _glu_bwd(y_pre, do, w_glu, b_glu, *, name):
    S, W = y_pre.shape
    tr = _row_tile(S)

    def body(y_ref, do_ref, w_ref, b_ref, dy_ref, dw_ref, db_ref):
        i = pl.program_id(0)
        yg, dyg_dy = _gelu_and_grad(y_ref[...])
        ygb = yg.astype(BF16)
        sg = _sigmoid(_dot(ygb, w_ref[...], 1, 0) + b_ref[...])
        do = do_ref[...]
        dgl = do * yg * sg * (1.0 - sg)
        dglb = dgl.astype(BF16)
        dyg = do * sg + _dot(dglb, w_ref[...], 1, 1)
        dy_ref[...] = dyg * dyg_dy
        dw = _dot(ygb, dglb, 0, 0)
        db = jnp.sum(dgl, axis=0, keepdims=True)

        @pl.when(i == 0)
        def _():
            dw_ref[...] = dw
            db_ref[...] = db

        @pl.when(i > 0)
        def _():
            dw_ref[...] += dw
            db_ref[...] += db

    row = pl.BlockSpec((tr, W), lambda i: (i, 0))
    full = pl.BlockSpec((W, W), lambda i: (0, 0))
    vec = pl.BlockSpec((1, W), lambda i: (0, 0))
    return pl.pallas_call(
        body, name=name,
        out_shape=(jax.ShapeDtypeStruct((S, W), F32), jax.ShapeDtypeStruct((W, W), F32), jax.ShapeDtypeStruct((1, W), F32)),
        grid=(S // tr,), in_specs=[row, row, full, vec], out_specs=(row, full, vec),
        compiler_params=_cparams("arbitrary"),
    )(y_pre, do, w_glu, b_glu)


GATE_COL0 = 3 * SB_WIDTH + SSM_WIDTH


def _merge_fwd(proj, o_attn, o_ssm, w_ba, w_bs, b_gate, *, name):
    S = proj.shape[0]
    D = D_MODEL
    tr = _pick(S, (256, 128, 64, 32, 16, 8))
    gb = GATE_COL0 // D

    def body(ga_ref, gs_ref, oa_ref, os_ref, wa_ref, ws_ref, ba_ref, bs_ref, m_ref):
        pa = _dot(oa_ref[...], wa_ref[...], 1, 0)
        ps = _dot(os_ref[...], ws_ref[...], 1, 0)
        sa = _sigmoid(ga_ref[...] + ba_ref[...])
        ss = _sigmoid(gs_ref[...] + bs_ref[...])
        m_ref[...] = (sa * pa + ss * ps).astype(BF16)

    return pl.pallas_call(
        body, name=name, out_shape=jax.ShapeDtypeStruct((S, D), BF16), grid=(S // tr,),
        in_specs=[pl.BlockSpec((tr, D), lambda i: (i, gb)), pl.BlockSpec((tr, D), lambda i: (i, gb + 1)),
                  pl.BlockSpec((tr, SB_WIDTH), lambda i: (i, 0)), pl.BlockSpec((tr, SSM_WIDTH), lambda i: (i, 0)),
                  pl.BlockSpec((SB_WIDTH, D), lambda i: (0, 0)), pl.BlockSpec((SSM_WIDTH, D), lambda i: (0, 0)),
                  pl.BlockSpec((1, D), lambda i: (0, 0)), pl.BlockSpec((1, D), lambda i: (0, 1))],
        out_specs=pl.BlockSpec((tr, D), lambda i: (i, 0)),
        compiler_params=_cparams("parallel"),
    )(proj, proj, o_attn, o_ssm, w_ba, w_bs, b_gate, b_gate)


def _merge_bwd(dmerged, proj, o_attn, o_ssm, w_ba, w_bs, b_gate, *, name):
    S = proj.shape[0]
    D = D_MODEL
    tr = _pick(S, (256, 128, 64, 32, 16, 8))
    gb = GATE_COL0 // D

    def body(dm_ref, ga_ref, gs_ref, oa_ref, os_ref, wa_ref, ws_ref, ba_ref, bs_ref,
             doa_ref, dos_ref, dg_ref, db_ref, dwa_ref, dws_ref):
        i = pl.program_id(0)
        dm = dm_ref[...]
        oa = oa_ref[...]
        osm = os_ref[...]
        pa = _dot(oa, wa_ref[...], 1, 0)
        ps = _dot(osm, ws_ref[...], 1, 0)
        sa = _sigmoid(ga_ref[...] + ba_ref[...])
        ss = _sigmoid(gs_ref[...] + bs_ref[...])
        dpa = (dm * sa).astype(BF16)
        dps = (dm * ss).astype(BF16)
        dga = dm * pa * sa * (1.0 - sa)
        dgs = dm * ps * ss * (1.0 - ss)
        dg_ref[:, :D] = dga.astype(BF16)
        dg_ref[:, D:] = dgs.astype(BF16)
        doa_ref[...] = _dot(dpa, wa_ref[...], 1, 1).astype(BF16)
        dos_ref[...] = _dot(dps, ws_ref[...], 1, 1)
        dwa = _dot(oa, dpa, 0, 0)
        dws = _dot(osm, dps, 0, 0)
        dba = jnp.sum(dga, axis=0, keepdims=True)
        dbs = jnp.sum(dgs, axis=0, keepdims=True)

        @pl.when(i == 0)
        def _():
            dwa_ref[...] = dwa
            dws_ref[...] = dws
            db_ref[:, :D] = dba
            db_ref[:, D:] = dbs

        @pl.when(i > 0)
        def _():
            dwa_ref[...] += dwa
            dws_ref[...] += dws
            db_ref[:, :D] += dba
            db_ref[:, D:] += dbs

    rowD = pl.BlockSpec((tr, D), lambda i: (i, 0))
    wspec = pl.BlockSpec((SB_WIDTH, D), lambda i: (0, 0))
    return pl.pallas_call(
        body, name=name,
        out_shape=(jax.ShapeDtypeStruct((S, SB_WIDTH), BF16), jax.ShapeDtypeStruct((S, SSM_WIDTH), F32),
                   jax.ShapeDtypeStruct((S, 2 * D), BF16), jax.ShapeDtypeStruct((1, 2 * D), F32),
                   jax.ShapeDtypeStruct((SB_WIDTH, D), F32), jax.ShapeDtypeStruct((SSM_WIDTH, D), F32)),
        grid=(S // tr,),
        in_specs=[rowD, pl.BlockSpec((tr, D), lambda i: (i, gb)), pl.BlockSpec((tr, D), lambda i: (i, gb + 1)),
                  pl.BlockSpec((tr, SB_WIDTH), lambda i: (i, 0)), pl.BlockSpec((tr, SSM_WIDTH), lambda i: (i, 0)),
                  wspec, wspec, pl.BlockSpec((1, D), lambda i: (0, 0)), pl.BlockSpec((1, D), lambda i: (0, 1))],
        out_specs=(pl.BlockSpec((tr, SB_WIDTH), lambda i: (i, 0)), pl.BlockSpec((tr, SSM_WIDTH), lambda i: (i, 0)),
                   pl.BlockSpec((tr, 2 * D), lambda i: (i, 0)), pl.BlockSpec((1, 2 * D), lambda i: (0, 0)),
                   wspec, wspec),
        compiler_params=_cparams("arbitrary"),
    )(dmerged, proj, proj, o_attn, o_ssm, w_ba, w_bs, b_gate, b_gate)


def _xattn_probs(q, k, h):
    cols = slice(h * XA_HEAD_DIM, (h + 1) * XA_HEAD_DIM)
    s = _dot(q[:, cols], k[:, cols], 1, 1) * (XA_HEAD_DIM ** -0.5)
    s = s - jnp.max(s, axis=-1, keepdims=True)
    e = jnp.exp(s)
    return e / jnp.sum(e, axis=-1, keepdims=True), cols


def _xattn_fwd(q2, k2, v2, *, name):
    S, D = q2.shape
    M = k2.shape[0]
    tr = _row_tile(S)

    def body(q_ref, k_ref, v_ref, o_ref):
        q = q_ref[...]
        k = k_ref[...]
        v = v_ref[...]
        for h in range(XA_HEADS):
            p, cols = _xattn_probs(q, k, h)
            o_ref[:, cols] = _dot(p, v[:, cols], 1, 0).astype(BF16)

    row = pl.BlockSpec((tr, D), lambda i: (i, 0))
    memb = pl.BlockSpec((M, D), lambda i: (0, 0))
    return pl.pallas_call(
        body, name=name, out_shape=jax.ShapeDtypeStruct((S, D), BF16), grid=(S // tr,),
        in_specs=[row, memb, memb], out_specs=row, compiler_params=_cparams("parallel"),
    )(q2, k2, v2)


def _xattn_bwd(q2, k2, v2, do2, *, name):
    S, D = q2.shape
    M = k2.shape[0]
    tr = _row_tile(S)
    scale = XA_HEAD_DIM ** -0.5

    def body(q_ref, k_ref, v_ref, do_ref, dq_ref, dk_ref, dv_ref):
        i = pl.program_id(0)

        @pl.when(i == 0)
        def _():
            dk_ref[...] = jnp.zeros_like(dk_ref)
            dv_ref[...] = jnp.zeros_like(dv_ref)

        q = q_ref[...]
        k = k_ref[...]
        v = v_ref[...]
        do = do_ref[...]
        for h in range(XA_HEADS):
            p, cols = _xattn_probs(q, k, h)
            dp = _dot(do[:, cols], v[:, cols], 1, 1)
            ds = (p * (dp - jnp.sum(dp * p, axis=-1, keepdims=True)) * scale).astype(BF16)
            dq_ref[:, cols] = _dot(ds, k[:, cols], 1, 0).astype(BF16)
            dk_ref[:, cols] += _dot(ds, q[:, cols], 0, 0)
            dv_ref[:, cols] += _dot(p, do[:, cols], 0, 0)

    row = pl.BlockSpec((tr, D), lambda i: (i, 0))
    memb = pl.BlockSpec((M, D), lambda i: (0, 0))
    return pl.pallas_call(
        body, name=name,
        out_shape=(jax.ShapeDtypeStruct((S, D), BF16), jax.ShapeDtypeStruct((M, D), F32), jax.ShapeDtypeStruct((M, D), F32)),
        grid=(S // tr,), in_specs=[row, memb, memb, row], out_specs=(row, memb, memb),
        compiler_params=_cparams("arbitrary"),
    )(q2, k2, v2, do2)


CONV_ROWS = 64
CONV_ROWS_FWD = 256


def _chunk(ref, c, rows):
    return ref[pl.ds(pl.multiple_of(c * rows, rows), rows), :]


def _rows_before(ref, c, rows):
    t0 = pl.multiple_of(jnp.maximum(c * rows - SUBLANES, 0), SUBLANES)
    return jnp.where(c > 0, ref[pl.ds(t0, SUBLANES), :], 0.0)


def _rows_after(ref, c, rows, n_chunks):
    t0 = pl.multiple_of(jnp.minimum((c + 1) * rows, n_chunks * rows - SUBLANES), SUBLANES)
    return jnp.where(c < n_chunks - 1, ref[pl.ds(t0, SUBLANES), :], 0.0)


def _shift_down(cur, before, d):
    out = pltpu.roll(cur, d, 0)
    r = lax.broadcasted_iota(jnp.int32, cur.shape, 0)
    for e in range(d):
        out = jnp.where(r == e, before[SUBLANES - d + e:SUBLANES - d + e + 1, :], out)
    return out


def _shift_up(cur, after, d):
    rows = cur.shape[0]
    out = pltpu.roll(cur, rows - d, 0)
    r = lax.broadcasted_iota(jnp.int32, cur.shape, 0)
    for e in range(d):
        out = jnp.where(r == rows - d + e, after[e:e + 1, :], out)
    return out


def _conv3(cur, before, w_ref, b_ref):
    return (w_ref[2:3, :] * cur + w_ref[1:2, :] * _shift_down(cur, before, 1)
            + w_ref[0:1, :] * _shift_down(cur, before, 2) + b_ref[...])


def _convgate_fwd(up_g, up_v, conv_w, conv_b, *, name):
    S, H = up_g.shape
    nb = H // LANES
    R = min(CONV_ROWS_FWD, S)
    n_chunks = S // R

    def body(g_ref, v_ref, wg_ref, wv_ref, bg_ref, bv_ref, a_ref):
        def chunk(c, _):
            cg = _conv3(_chunk(g_ref, c, R), _rows_before(g_ref, c, R), wg_ref, bg_ref)
            cv = _conv3(_chunk(v_ref, c, R), _rows_before(v_ref, c, R), wv_ref, bv_ref)
            a_ref[pl.ds(pl.multiple_of(c * R, R), R), :] = (_gelu(cg) * cv).astype(BF16)
            return 0

        lax.fori_loop(0, n_chunks, chunk, 0)

    col = lambda off: pl.BlockSpec((S, LANES), lambda j: (0, off + j))
    wcol = lambda off: pl.BlockSpec((3, LANES), lambda j: (0, off + j))
    bcol = lambda off: pl.BlockSpec((1, LANES), lambda j: (0, off + j))
    return pl.pallas_call(
        body, name=name, out_shape=jax.ShapeDtypeStruct((S, H), BF16), grid=(nb,),
        in_specs=[col(0), col(0), wcol(0), wcol(nb), bcol(0), bcol(nb)],
        out_specs=col(0), compiler_params=_cparams("parallel"),
    )(up_g, up_v, conv_w, conv_w, conv_b, conv_b)


def _convgate_bwd(up_g, up_v, da, conv_w, conv_b, *, name):
    S, H = up_g.shape
    nb = H // LANES
    R = min(CONV_ROWS, S)
    n_chunks = S // R

    def fold(a):
        return sum(a[r:r + SUBLANES] for r in range(0, a.shape[0], SUBLANES))

    def body(g_ref, v_ref, da_ref, wg_ref, wv_ref, bg_ref, bv_ref,
             dug_ref, duv_ref, dwg_ref, dwv_ref, dbg_ref, dbv_ref, dcg_s, dcv_s):
        def first_pass(c, acc):
            rows = pl.ds(pl.multiple_of(c * R, R), R)
            ug, uv = _chunk(g_ref, c, R), _chunk(v_ref, c, R)
            bg, bv = _rows_before(g_ref, c, R), _rows_before(v_ref, c, R)
            cg = _conv3(ug, bg, wg_ref, bg_ref)
            cv = _conv3(uv, bv, wv_ref, bv_ref)
            da = da_ref[rows, :]
            gl, dgl = _gelu_and_grad(cg)
            dcg = da * cv * dgl
            dcv = da * gl
            dcg_s[rows, :] = dcg
            dcv_s[rows, :] = dcv
            new = []
            for dc, u, before in ((dcg, ug, bg), (dcv, uv, bv)):
                new += [fold(dc * _shift_down(u, before, 2)), fold(dc * _shift_down(u, before, 1)), fold(dc * u), fold(dc)]
            return tuple(a + n for a, n in zip(acc, new))

        zero = jnp.zeros((SUBLANES, LANES), F32)
        acc = lax.fori_loop(0, n_chunks, first_pass, (zero,) * 8)
        total = [jnp.sum(a, axis=0, keepdims=True) for a in acc]
        for k, (dw_ref, db_ref) in enumerate(((dwg_ref, dbg_ref), (dwv_ref, dbv_ref))):
            dw_ref[0:1, :] = total[4 * k]
            dw_ref[1:2, :] = total[4 * k + 1]
            dw_ref[2:3, :] = total[4 * k + 2]
            db_ref[...] = total[4 * k + 3]

        def second_pass(c, _):
            rows = pl.ds(pl.multiple_of(c * R, R), R)
            for dc_s, w_ref, du_ref in ((dcg_s, wg_ref, dug_ref), (dcv_s, wv_ref, duv_ref)):
                cur, after = _chunk(dc_s, c, R), _rows_after(dc_s, c, R, n_chunks)
                du = w_ref[2:3, :] * cur + w_ref[1:2, :] * _shift_up(cur, after, 1) + w_ref[0:1, :] * _shift_up(cur, after, 2)
                du_ref[rows, :] = du.astype(BF16)
            return 0

        lax.fori_loop(0, n_chunks, second_pass, 0)

    col = lambda off: pl.BlockSpec((S, LANES), lambda j: (0, off + j))
    wcol = lambda off: pl.BlockSpec((3, LANES), lambda j: (0, off + j))
    bcol = lambda off: pl.BlockSpec((1, LANES), lambda j: (0, off + j))
    return pl.pallas_call(
        body, name=name,
        out_shape=(jax.ShapeDtypeStruct((S, H), BF16), jax.ShapeDtypeStruct((S, H), BF16),
                   jax.ShapeDtypeStruct((3, H), F32), jax.ShapeDtypeStruct((3, H), F32),
                   jax.ShapeDtypeStruct((1, H), F32), jax.ShapeDtypeStruct((1, H), F32)),
        grid=(nb,),
        in_specs=[col(0), col(0), col(0), wcol(0), wcol(nb), bcol(0), bcol(nb)],
        out_specs=(col(0), col(0), wcol(0), wcol(0), bcol(0), bcol(0)),
        scratch_shapes=[pltpu.VMEM((S, LANES), F32), pltpu.VMEM((S, LANES), F32)],
        compiler_params=_cparams("parallel"),
    )(up_g, up_v, da, conv_w, conv_w, conv_b, conv_b)


def _local_step(x, mem, target, w_in, late_wire, P, core):
    mm = _matmul
    h1, (w_in,) = _rms_fwd(x, P["norm_mix_pre"], name="rms_mix_pre", rider=_fill_xy([w_in]))
    w_in, = _fill_c([w_in]).run(name="gather_in_c")
    w_in = w_in.reshape((N_DEV,) + w_in.shape[2:])
    n_mid = len(LATE) - len(REDUCE_FFN)
    proj, wire_mid = mm(h1, w_in, name="mm_in", rider=_fill_xy(late_wire[:n_mid]))
    (o_attn, sb_tot, sb_first), wires = _sb_fwd(
        proj, name="sb_fwd", rider=_Exchange.join(_fill_c(wire_mid), _fill_xy(late_wire[n_mid:])))
    wire_mid, wire_ffn = wires[:n_mid], wires[n_mid:]

    ssm_prep = lambda *a: _ssm_prepare(*a)
    (lam_re, lam_im, bb_re, bb_im), prep_vjp = jax.vjp(
        ssm_prep, P["ssm_a_re"], P["ssm_a_im"], P["ssm_log_dt"], P["ssm_b_re"], P["ssm_b_im"])
    tab_f, tab_b = _ssm_tables(lam_re, lam_im)
    bd_re, bd_im = _b_blocks(bb_re), _b_blocks(bb_im)
    cd_re, cd_imneg = _c_blocks(P["ssm_c_re"]), _c_blocks(-P["ssm_c_im"])
    (y_pre, x_re, x_im), wire_ffn = _ssm_fwd(proj, bd_re, bd_im, cd_re, cd_imneg, P["ssm_d"], tab_f,
                                             name="ssm_fwd", rider=_fill_c(wire_ffn))
    W = _weights_from_wire(dict(zip(LATE, list(wire_mid) + list(wire_ffn))))
    W["w_in"] = w_in
    o_ssm = _glu_fwd(y_pre, W["ssm_w_glu"], P["ssm_b_glu"], name="glu_fwd")

    merged = _merge_fwd(proj, o_attn, o_ssm, W["w_branch_attn"], W["w_branch_ssm"], P["b_gate"], name="merge_fwd")
    mo = mm(merged, W["w_out"], name="mm_out")
    x1, h2 = _resnorm_norm(x, mo, P["norm_mix_post"], P["norm_xa_pre"], name="resnorm_1")

    mem_n = _rms_fwd(mem, P["norm_mem"], name="rms_mem")
    q2 = mm(h2, W["xa_wq"], out_dtype=BF16, name="mm_xq")
    k2 = mm(mem_n, W["xa_wk"], out_dtype=BF16, name="mm_xk")
    v2 = mm(mem_n, W["xa_wv"], out_dtype=BF16, name="mm_xv")
    o2 = _xattn_fwd(q2, k2, v2, name="xattn_fwd")
    xa = mm(o2, W["xa_wo"], name="mm_xo")
    x2, h3 = _resnorm_norm(x1, xa, P["norm_xa_post"], P["norm_ffn_pre"], name="resnorm_2")

    half = N_DEV // 2
    up_g = mm(h3, W["ffn_w_up"], n_blocks=half, name="mm_up_g")
    up_v = mm(h3, W["ffn_w_up"], b_block0=half, name="mm_up_v")
    act = _convgate_fwd(up_g, up_v, W["ffn_conv_w"], P["ffn_conv_b"], name="convgate_fwd")
    f = mm(act, W["ffn_w_down"], name="mm_down")
    loss, dy, df, dg_ffn_post = _final_loss(x2, f, P["norm_ffn_post"], target, name="final_loss")

    G = {"norm_ffn_post": dg_ffn_post}
    dact = mm(df, W["ffn_w_down"], tb=True, name="mm_down_dx")
    G["ffn_w_down"] = mm(act, df, ta=True, name="mm_down_dw")
    dug, duv, dwg, dwv, dbg, dbv = _convgate_bwd(up_g, up_v, dact, W["ffn_conv_w"], P["ffn_conv_b"], name="convgate_bwd")
    G["ffn_conv_w"] = jnp.concatenate([dwg, dwv], axis=1)
    G["ffn_conv_b"] = jnp.concatenate([dbg, dbv], axis=1)
    dh3 = mm(dug, W["ffn_w_up"], tb=True, n_blocks=half, name="mm_up_g_dx")
    dh3 = mm(duv, W["ffn_w_up"], tb=True, b_block0=half, acc_in=dh3, name="mm_up_v_dx")
    dw_up = mm(h3, dug, ta=True, out_into=lax.empty(W["ffn_w_up"].shape, F32), name="mm_up_g_dw")
    G["ffn_w_up"] = mm(h3, duv, ta=True, out_into=dw_up, out_block0=half, name="mm_up_v_dw")
    blocks = {n: _grad_blocks(n, G[n]) for n in REDUCE_FFN}
    (dx2, dxa, G["norm_ffn_pre"], G["norm_xa_post"]), from_core = _norm_bwd_pair(
        dy, dh3, x2, P["norm_ffn_pre"], xa, P["norm_xa_post"], name="norm_bwd_3",
        rider=_send_c([blocks[n] for n in REDUCE_FFN]))
    pair = {n: _pair_sum(blocks[n], r, core, name="pair_sum_" + n) for n, r in zip(REDUCE_FFN, from_core)}

    G["xa_wo"] = mm(o2, dxa, ta=True, name="mm_xo_dw")
    do2 = mm(dxa, W["xa_wo"], tb=True, out_dtype=BF16, name="mm_xo_dx")
    dq2, dk2, dv2 = _xattn_bwd(q2, k2, v2, do2, name="xattn_bwd")
    G["xa_wq"] = mm(h2, dq2, ta=True, name="mm_xq_dw")
    dh2 = mm(dq2, W["xa_wq"], tb=True, name="mm_xq_dx")
    G["xa_wk"] = mm(mem_n, dk2, ta=True, name="mm_xk_dw")
    G["xa_wv"] = mm(mem_n, dv2, ta=True, name="mm_xv_dw")
    dmem_n = jnp.concatenate([dk2, dv2], axis=1)
    wkv = jnp.concatenate([W["xa_wk"], W["xa_wv"]], axis=1)
    dmem = mm(dmem_n, wkv, tb=True, name="mm_xkv_dx")
    _, G["norm_mem"] = _norm_bwd_single(None, dmem, mem, P["norm_mem"], name="norm_bwd_mem")
    (dx1, dmo, G["norm_xa_pre"], G["norm_mix_post"]), _ = _norm_bwd_pair(
        dx2, dh2, x1, P["norm_xa_pre"], mo, P["norm_mix_post"], name="norm_bwd_2")

    G["w_out"] = mm(merged, dmo, ta=True, name="mm_out_dw")
    dmerged = mm(dmo, W["w_out"], tb=True, name="mm_out_dx")
    do_attn, do_ssm, dgate, G["b_gate"], G["w_branch_attn"], G["w_branch_ssm"] = _merge_bwd(
        dmerged, proj, o_attn, o_ssm, W["w_branch_attn"], W["w_branch_ssm"], P["b_gate"], name="merge_bwd")
    dy_pre, G["ssm_w_glu"], G["ssm_b_glu"] = _glu_bwd(y_pre, do_ssm, W["ssm_w_glu"], P["ssm_b_glu"], name="glu_bwd")
    blocks.update({n: _grad_blocks(n, G[n]) for n in REDUCE_MID})
    (du, dbd_re, dbd_im, dcd_re, dcd_imneg, G["ssm_d"], dl_re, dl_im), brought = _ssm_bwd(
        dy_pre, proj, x_re, x_im, bd_re, bd_im, cd_re, cd_imneg, P["ssm_d"], tab_b, name="ssm_bwd",
        rider=_Exchange.join(_send_c([blocks[n] for n in REDUCE_MID]), _scatter_xy([pair[n] for n in REDUCE_FFN])))
    from_core, from_chips = brought[:len(REDUCE_MID)], brought[len(REDUCE_MID):]
    reduced = {n: (pair[n], parts) for n, parts in zip(REDUCE_FFN, from_chips)}
    pair.update({n: _pair_sum(blocks[n], r, core, name="pair_sum_" + n) for n, r in zip(REDUCE_MID, from_core)})
    G["ssm_c_re"] = _c_from_blocks(dcd_re)
    G["ssm_c_im"] = -_c_from_blocks(dcd_imneg)
    dlam_re = jnp.sum(dl_re, axis=1).reshape(SSM_GROUPS, SSM_STATE)
    dlam_im = jnp.sum(dl_im, axis=1).reshape(SSM_GROUPS, SSM_STATE)
    (G["ssm_a_re"], G["ssm_a_im"], G["ssm_log_dt"], G["ssm_b_re"], G["ssm_b_im"]) = prep_vjp(
        (dlam_re, dlam_im, _bbar_from_blocks(dbd_re), _bbar_from_blocks(dbd_im)))
    G["ffn_conv_b"] = G["ffn_conv_b"].reshape(N_DEV, FF_LOCAL_PAD)[:, :FF_LOCAL]
    small = [G[n].reshape(SMALL_SHAPE[n]) for n in SMALL_EARLY]
    (dq, dk, dv), brought = _sb_bwd(
        proj, sb_tot, sb_first, do_attn, name="sb_bwd",
        rider=_Exchange.join(_scatter_xy([pair[n] for n in REDUCE_MID]), _gather_xy_from(small)))
    from_chips, small = brought[:len(REDUCE_MID)], brought[len(REDUCE_MID):]
    reduced.update({n: (pair[n], parts) for n, parts in zip(REDUCE_MID, from_chips)})
    dproj = jnp.concatenate([dq, dk, dv, du, dgate], axis=1)
    G["w_in"], small = mm(h1, dproj, ta=True, out_cb=W["w_in"].shape[2], name="mm_in_dw", rider=_fill_c(small))
    g_in = _grad_blocks("w_in", G["w_in"])
    dh1, (from_core,) = mm(dproj, W["w_in"], tb=True, name="mm_in_dx", rider=_send_c([g_in]))
    pair_in = _pair_sum(g_in, from_core, core, name="pair_sum_w_in")
    (grad_x, dg_pre), (from_chips,) = _norm_bwd_single(dx1, dh1, x, P["norm_mix_pre"], name="norm_bwd_1",
                                                       rider=_scatter_xy([pair_in]))
    reduced["w_in"] = (pair_in, from_chips)
    last, = _gather_all([dg_pre]).run(name="gather_g_last")
    parts = dict(zip(SMALL_EARLY, small))
    parts["norm_mix_pre"] = last
    return loss, grad_x, parts, reduced


MESH = pl.DeviceIdType.MESH
_HBM = pl.BlockSpec(memory_space=pl.ANY)
N_XY = 4
N_XY_PEERS = 3


def _xy_peers(x, y):
    return [(1 - x, y), (x, 1 - y), (1 - x, 1 - y)]


class _Exchange:
    def __init__(self, arrays, out_shapes, plan, n_copies, alias):
        self.arrays = list(arrays)
        self.out_shapes = list(out_shapes)
        self.plan = plan
        self.n_copies = n_copies
        self.alias = list(alias) if isinstance(alias, (list, tuple)) else [alias] * len(self.arrays)

    @property
    def n(self):
        return len(self.arrays)

    def aliases(self, first_in, first_out):
        return {first_in + k: first_out + k for k in range(self.n) if self.alias[k]}

    @staticmethod
    def join(a, b):
        def plan(k, src, dst, x, y, c):
            return a.plan(k, src, dst, x, y, c) if k < a.n else b.plan(k - a.n, src, dst, x, y, c)

        return _Exchange(a.arrays + b.arrays, a.out_shapes + b.out_shapes, plan, max(a.n_copies, b.n_copies),
                         a.alias + b.alias)

    def sems(self):
        shape = (self.n, self.n_copies)
        return [pltpu.SemaphoreType.DMA(shape), pltpu.SemaphoreType.DMA(shape)]

    def _copies(self, ins, outs, send_sems, recv_sems):
        x, y, c = lax.axis_index("x"), lax.axis_index("y"), lax.axis_index("c")
        sends, lands, own = [], [], []
        for k in range(self.n):
            for j, (src, dst, dev, land) in enumerate(self.plan(k, ins[k], outs[k], x, y, c)):
                if dev is None:
                    own.append(pltpu.make_async_copy(src, dst, send_sems.at[k, j]))
                    continue
                sems = dict(send_sem=send_sems.at[k, j], recv_sem=recv_sems.at[k, j], device_id=dev, device_id_type=MESH)
                sends.append(pltpu.make_async_remote_copy(src_ref=src, dst_ref=dst, **sems))
                lands.append(pltpu.make_async_remote_copy(src_ref=src, dst_ref=land, **sems))
        return sends, lands, own

    def start(self, ins, outs, send_sems, recv_sems):
        sends, _, own = self._copies(ins, outs, send_sems, recv_sems)
        for cp in own + sends:
            cp.start()

    def finish(self, ins, outs, send_sems, recv_sems):
        sends, lands, own = self._copies(ins, outs, send_sems, recv_sems)
        for cp in lands:
            cp.wait_recv()
        for cp in sends:
            cp.wait_send()
        for cp in own:
            cp.wait()

    def run(self, *, name):
        n = self.n

        def body(*refs):
            parts = (refs[:n], refs[n:2 * n], refs[2 * n], refs[2 * n + 1])
            self.start(*parts)
            self.finish(*parts)

        return pl.pallas_call(
            body, name=name, out_shape=tuple(self.out_shapes),
            in_specs=[_HBM] * n, out_specs=tuple([_HBM] * n),
            input_output_aliases=self.aliases(0, 0),
            scratch_shapes=self.sems(),
        )(*self.arrays)


def _call(host_body, *, name, grid, in_specs, out_specs, out_shape, scratch_shapes, operands, rider=None):
    out_specs, out_shape = tuple(out_specs), tuple(out_shape)
    if rider is None:
        res = pl.pallas_call(
            host_body, name=name, grid=grid, in_specs=list(in_specs), out_specs=out_specs, out_shape=out_shape,
            scratch_shapes=list(scratch_shapes), compiler_params=_cparams(*["arbitrary"] * len(grid)),
        )(*operands)
        return tuple(res), None
    n, n_in, n_out, n_scr = rider.n, len(in_specs), len(out_specs), len(scratch_shapes)

    def body(*refs):
        pos = [0]

        def take(count):
            pos[0] += count
            return refs[pos[0] - count:pos[0]]

        h_in, r_in, h_out, r_out, h_scr = take(n_in), take(n), take(n_out), take(n), take(n_scr)
        send_sems, recv_sems = take(2)
        ids = [pl.program_id(a) for a in range(len(grid))]
        first = functools.reduce(jnp.logical_and, [i == 0 for i in ids])
        last = functools.reduce(jnp.logical_and, [i == g - 1 for i, g in zip(ids, grid)])

        @pl.when(first)
        def _():
            rider.start(r_in, r_out, send_sems, recv_sems)

        host_body(*h_in, *h_out, *h_scr)

        @pl.when(last)
        def _():
            rider.finish(r_in, r_out, send_sems, recv_sems)

    res = pl.pallas_call(
        body, name=name, grid=grid,
        in_specs=list(in_specs) + [_HBM] * n, out_specs=out_specs + tuple([_HBM] * n),
        out_shape=out_shape + tuple(rider.out_shapes),
        input_output_aliases=rider.aliases(n_in, n_out),
        scratch_shapes=list(scratch_shapes) + rider.sems(),
        compiler_params=_cparams(*["arbitrary"] * len(grid)),
    )(*operands, *rider.arrays)
    return tuple(res[:n_out]), list(res[n_out:])


def _same(arrays):
    return [jax.ShapeDtypeStruct(a.shape, a.dtype) for a in arrays]


def _fill_xy(bufs):
    def plan(k, src, dst, x, y, c):
        mine = 2 * x + y
        return [(src.at[mine, c], dst.at[mine, c], (px, py, c), dst.at[2 * px + py, c]) for px, py in _xy_peers(x, y)]

    return _Exchange(bufs, _same(bufs), plan, N_XY_PEERS, alias=True)


def _fill_c(bufs):
    def plan(k, src, dst, x, y, c):
        return [(src.at[:, c], dst.at[:, c], (x, y, 1 - c), dst.at[:, 1 - c])]

    return _Exchange(bufs, _same(bufs), plan, 1, alias=True)


def _slots(arrays):
    return [jax.ShapeDtypeStruct((N_XY, 2) + a.shape, a.dtype) for a in arrays]


def _gather_xy_from(srcs):
    def plan(k, src, dst, x, y, c):
        mine = 2 * x + y
        return ([(src, dst.at[mine, c], None, None)]
                + [(src, dst.at[mine, c], (px, py, c), dst.at[2 * px + py, c]) for px, py in _xy_peers(x, y)])

    return _Exchange(srcs, _slots(srcs), plan, 1 + N_XY_PEERS, alias=False)


def _gather_all(srcs):
    def plan(k, src, dst, x, y, c):
        mine = 2 * x + y
        out = [(src, dst.at[mine, c], None, None)]
        for fx, fy, fc in [(a, b, e) for a in (0, 1) for b in (0, 1) for e in (0, 1)][1:]:
            px, py, pc = (1 - x) if fx else x, (1 - y) if fy else y, (1 - c) if fc else c
            out.append((src, dst.at[mine, c], (px, py, pc), dst.at[2 * px + py, pc]))
        return out

    return _Exchange(srcs, _slots(srcs), plan, N_DEV, alias=False)


def _send_c(srcs):
    def plan(k, src, dst, x, y, c):
        return [(src.at[:, 1 - c], dst, (x, y, 1 - c), dst)]

    outs = [jax.ShapeDtypeStruct(a.shape[:1] + a.shape[2:], a.dtype) for a in srcs]
    return _Exchange(srcs, outs, plan, 1, alias=False)


def _scatter_xy(srcs):
    def plan(k, src, dst, x, y, c):
        return [(src.at[2 * px + py], dst.at[j], (px, py, c), dst.at[j]) for j, (px, py) in enumerate(_xy_peers(x, y))]

    outs = [jax.ShapeDtypeStruct((N_XY_PEERS,) + a.shape[1:], a.dtype) for a in srcs]
    return _Exchange(srcs, outs, plan, N_XY_PEERS, alias=False)


WIRE_DTYPE = BF16


def _pair_sum(g8, recv, core, *, name):
    n, _, R, C = g8.shape
    tr = _pick(R, (128, 64, 32, 16, 8))

    def body(core_ref, a_ref, b_ref, o_ref):
        o_ref[...] = (a_ref[0] + b_ref[...]).astype(WIRE_DTYPE)

    return pl.pallas_call(
        body, name=name, out_shape=jax.ShapeDtypeStruct((n, R, C), WIRE_DTYPE),
        grid_spec=pltpu.PrefetchScalarGridSpec(
            num_scalar_prefetch=1, grid=(n, R // tr),
            in_specs=[pl.BlockSpec((1, 1, tr, C), lambda s, i, core_ref: (s, core_ref[0], i, 0)),
                      pl.BlockSpec((1, tr, C), lambda s, i, core_ref: (s, i, 0))],
            out_specs=pl.BlockSpec((1, tr, C), lambda s, i, core_ref: (s, i, 0))),
        compiler_params=_cparams("parallel", "parallel"),
    )(core, g8, recv)


def _adamw_math(w, g, m, v):
    m = ADAM_B1 * m + (1.0 - ADAM_B1) * g
    v = ADAM_B2 * v + (1.0 - ADAM_B2) * (g * g)
    m_hat = m / (1.0 - ADAM_B1 ** ADAM_STEP)
    v_hat = v / (1.0 - ADAM_B2 ** ADAM_STEP)
    delta = -ADAM_LR * (m_hat / (jnp.sqrt(v_hat) + ADAM_EPS) + ADAM_WD * w)
    return delta, m, v


def _reduce_adamw(parts, w, m, v, *, own, own_slot, name):
    n, R, C = parts.shape
    tr = _pick(R, (128, 64, 32, 16, 8))

    def body(_, own_ref, parts_ref, w_ref, m_ref, v_ref, g_ref, d_ref, nm_ref, nv_ref):
        g = own_ref[0].astype(F32)
        for k in range(n):
            g = g + parts_ref[k].astype(F32)
        g_ref[...] = g
        d_ref[...], nm_ref[...], nv_ref[...] = _adamw_math(w_ref[...], g, m_ref[...], v_ref[...])

    out = jax.ShapeDtypeStruct((R, C), F32)
    row = pl.BlockSpec((tr, C), lambda i, s: (i, 0))
    return pl.pallas_call(
        body, name=name, out_shape=(out, out, out, out),
        grid_spec=pltpu.PrefetchScalarGridSpec(
            num_scalar_prefetch=1, grid=(R // tr,),
            in_specs=[pl.BlockSpec((1, tr, C), lambda i, s: (s[0], i, 0)),
                      pl.BlockSpec((n, tr, C), lambda i, s: (0, i, 0)), row, row, row],
            out_specs=(row, row, row, row)),
        compiler_params=_cparams("parallel"),
    )(own_slot, own, parts, w, m, v)


SHARDED = (("w_in", (1024, 4096), 1), ("ssm_w_glu", (512, 512), 0), ("w_branch_attn", (512, 1024), 1),
           ("w_branch_ssm", (512, 1024), 1), ("w_out", (1024, 1024), 0), ("xa_wq", (1024, 1024), 0),
           ("xa_wk", (1024, 1024), 0), ("xa_wv", (1024, 1024), 0), ("xa_wo", (1024, 1024), 0),
           ("ffn_w_up", (1024, 5632), 1), ("ffn_conv_w", (3, 5632), 1), ("ffn_w_down", (2816, 1024), 0))
REPLICATED = (("norm_mix_pre", (1024,)), ("norm_mix_post", (1024,)), ("b_gate", (2048,)), ("ssm_a_re", (32, 64)),
              ("ssm_a_im", (32, 64)), ("ssm_log_dt", (32,)), ("ssm_b_re", (32, 64, 16)), ("ssm_b_im", (32, 64, 16)),
              ("ssm_c_re", (32, 16, 64)), ("ssm_c_im", (32, 16, 64)), ("ssm_d", (512,)), ("ssm_b_glu", (512,)),
              ("norm_xa_pre", (1024,)), ("norm_xa_post", (1024,)), ("norm_mem", (1024,)), ("norm_ffn_pre", (1024,)),
              ("norm_ffn_post", (1024,)), ("ffn_conv_b", (5632,)))
PARAM_ORDER = ("norm_mix_pre", "norm_mix_post", "w_in", "b_gate", "ssm_a_re", "ssm_a_im", "ssm_log_dt", "ssm_b_re",
               "ssm_b_im", "ssm_c_re", "ssm_c_im", "ssm_d", "ssm_w_glu", "ssm_b_glu", "w_branch_attn", "w_branch_ssm",
               "w_out", "norm_xa_pre", "norm_xa_post", "norm_mem", "xa_wq", "xa_wk", "xa_wv", "xa_wo", "norm_ffn_pre",
               "norm_ffn_post", "ffn_w_up", "ffn_conv_w", "ffn_conv_b", "ffn_w_down")
FF_LOCAL = 2 * D_FF // N_DEV
FF_LOCAL_PAD = 768
FF_PAD = (N_DEV // 2) * FF_LOCAL_PAD


def _local_shape(shape, axis):
    return tuple(s // N_DEV if a == axis else s for a, s in enumerate(shape))


def _pad_cols(a, width):
    return jnp.pad(a, [(0, 0)] * (a.ndim - 1) + [(0, width - a.shape[-1])])


def _blocks_to_cols(a8):
    return a8.transpose(1, 0, 2).reshape(a8.shape[1], N_DEV * a8.shape[2])


def _cols_to_blocks(a, cb):
    return a.reshape(a.shape[0], N_DEV, cb).transpose(1, 0, 2)


FF_PADDED = ("ffn_w_up", "ffn_conv_w")
LATE = tuple(n for n, _, _ in SHARDED if n != "w_in")
REDUCE_FFN = ("ffn_w_up", "ffn_conv_w", "ffn_w_down")
REDUCE_MID = ("xa_wo", "xa_wq", "xa_wk", "xa_wv", "w_out", "w_branch_attn", "w_branch_ssm", "ssm_w_glu")
SHARD_AXIS = {n: ax for n, _, ax in SHARDED}
FULL_SHAPE = {n: s for n, s, _ in SHARDED}


def _as_local(n, a):
    return _pad_cols(a, FF_LOCAL_PAD) if n in FF_PADDED else a


def _weights_from_wire(wire):
    full = {n: b.reshape((N_DEV,) + b.shape[2:]) for n, b in wire.items()}
    W = {n: a.reshape(FULL_SHAPE[n]) if SHARD_AXIS[n] == 0 else a for n, a in full.items()}
    for n in ("w_branch_attn", "w_branch_ssm", "ffn_conv_w"):
        W[n] = _blocks_to_cols(full[n])
    W["ffn_w_down"] = jnp.pad(W["ffn_w_down"].reshape(N_DEV // 2, FF_LOCAL, D_MODEL),
                              ((0, 0), (0, FF_LOCAL_PAD - FF_LOCAL), (0, 0))).reshape(FF_PAD, D_MODEL)
    return W


def _grad_blocks(n, g):
    if n in ("w_branch_attn", "w_branch_ssm"):
        g = _cols_to_blocks(g, D_MODEL // N_DEV)
    elif n == "ffn_conv_w":
        g = _cols_to_blocks(g, FF_LOCAL_PAD)
    elif n == "ffn_w_down":
        g = g.reshape(N_DEV // 2, FF_LOCAL_PAD, D_MODEL)[:, :FF_LOCAL]
    local = _local_shape(FULL_SHAPE[n], SHARD_AXIS[n])
    if n in FF_PADDED:
        local = local[:-1] + (FF_LOCAL_PAD,)
    return g.reshape((N_XY, 2) + local)


SMALL_SHAPE = {n: (1, s[0]) if len(s) == 1 else (s[0], math.prod(s[1:])) for n, s in REPLICATED}
SMALL_SHAPE["ffn_conv_b"] = (N_DEV, FF_LOCAL)
SMALL_EARLY = tuple(n for n, _ in REPLICATED if n != "norm_mix_pre")


def _adamw_replicated(parts, w, m, v, *, name):
    n = len(parts)

    def body(*refs):
        p_refs, w_refs, m_refs, v_refs = (refs[i * n:(i + 1) * n] for i in range(4))
        outs = refs[4 * n:]
        for k in range(n):
            g = p_refs[k][0, 0]
            for s in range(1, N_DEV):
                g = g + p_refs[k][s // 2, s % 2]
            d, nm, nv = _adamw_math(w_refs[k][...], g, m_refs[k][...], v_refs[k][...])
            for slot, val in enumerate((g, d, nm, nv)):
                outs[slot * n + k][...] = val

    vmem = pl.BlockSpec(memory_space=pltpu.VMEM)
    shapes = [jax.ShapeDtypeStruct(a.shape, F32) for a in w] * 4
    res = pl.pallas_call(
        body, name=name, out_shape=tuple(shapes), in_specs=[vmem] * (4 * n), out_specs=tuple([vmem] * (4 * n)),
        compiler_params=pltpu.CompilerParams(vmem_limit_bytes=VMEM_LIMIT),
    )(*parts, *w, *m, *v)
    return [list(res[i * n:(i + 1) * n]) for i in range(4)]


def kernel(x, mem, norm_mix_pre, norm_mix_post, w_in, b_gate, ssm_a_re, ssm_a_im, ssm_log_dt, ssm_b_re, ssm_b_im, ssm_c_re, ssm_c_im, ssm_d, ssm_w_glu, ssm_b_glu, w_branch_attn, w_branch_ssm, w_out, norm_xa_pre, norm_xa_post, norm_mem, xa_wq, xa_wk, xa_wv, xa_wo, norm_ffn_pre, norm_ffn_post, ffn_w_up, ffn_conv_w, ffn_conv_b, ffn_w_down, loss_target, m_norm_mix_pre, m_norm_mix_post, m_w_in, m_b_gate, m_ssm_a_re, m_ssm_a_im, m_ssm_log_dt, m_ssm_b_re, m_ssm_b_im, m_ssm_c_re, m_ssm_c_im, m_ssm_d, m_ssm_w_glu, m_ssm_b_glu, m_w_branch_attn, m_w_branch_ssm, m_w_out, m_norm_xa_pre, m_norm_xa_post, m_norm_mem, m_xa_wq, m_xa_wk, m_xa_wv, m_xa_wo, m_norm_ffn_pre, m_norm_ffn_post, m_ffn_w_up, m_ffn_conv_w, m_ffn_conv_b, m_ffn_w_down, v_norm_mix_pre, v_norm_mix_post, v_w_in, v_b_gate, v_ssm_a_re, v_ssm_a_im, v_ssm_log_dt, v_ssm_b_re, v_ssm_b_im, v_ssm_c_re, v_ssm_c_im, v_ssm_d, v_ssm_w_glu, v_ssm_b_glu, v_w_branch_attn, v_w_branch_ssm, v_w_out, v_norm_xa_pre, v_norm_xa_post, v_norm_mem, v_xa_wq, v_xa_wk, v_xa_wv, v_xa_wo, v_norm_ffn_pre, v_norm_ffn_post, v_ffn_w_up, v_ffn_conv_w, v_ffn_conv_b, v_ffn_w_down):
    args = dict(locals())
    w_loc = {n: args[n][0] for n in PARAM_ORDER}
    m_loc = {n: args["m_" + n][0] for n in PARAM_ORDER}
    v_loc = {n: args["v_" + n][0] for n in PARAM_ORDER}
    core_i = lax.axis_index("c")
    chip_i = 2 * lax.axis_index("x") + lax.axis_index("y")
    core = core_i.astype(jnp.int32).reshape(1)
    chip = chip_i.astype(jnp.int32).reshape(1)

    def in_place(a):
        buf = lax.empty((N_XY, 2) + a.shape, a.dtype)
        return lax.dynamic_update_slice(buf, a[None, None], (chip_i, core_i) + (0,) * a.ndim)

    as_wire = lambda n: in_place(_as_local(n, w_loc[n]).astype(F32 if n == "ffn_conv_w" else BF16))

    P = {}
    for n, shape in REPLICATED:
        P[n] = w_loc[n] if len(shape) > 1 or n == "ssm_log_dt" else w_loc[n].reshape(1, -1)
    P["ffn_conv_b"] = _pad_cols(w_loc["ffn_conv_b"].reshape(N_DEV, FF_LOCAL), FF_LOCAL_PAD).reshape(1, 2 * FF_PAD)

    loss, grad_x, small_parts, reduced = _local_step(x[0], mem[0], loss_target[0], as_wire("w_in"),
                                                     [as_wire(n) for n in LATE], P, core)
    loss = lax.psum(loss[0, 0], ("x", "y", "c"))

    big_out = {}
    for n, (own, parts) in reduced.items():
        res = _reduce_adamw(parts, _as_local(n, w_loc[n]), _as_local(n, m_loc[n]), _as_local(n, v_loc[n]),
                            own=own, own_slot=chip, name="adamw_" + n)
        big_out[n] = [r[:, :FF_LOCAL] if n in FF_PADDED else r for r in res]

    names = [n for n, _ in REPLICATED]
    as_small = lambda d: [d[n].reshape(SMALL_SHAPE[n]) for n in names]
    small_out = _adamw_replicated([small_parts[n] for n in names], as_small(w_loc), as_small(m_loc), as_small(v_loc),
                                  name="adamw_replicated")
    small_out = [dict(zip(names, res)) for res in small_out]

    outs = [loss, grad_x[None]]
    for k in range(4):
        for n in PARAM_ORDER:
            src = big_out[n][k] if n in big_out else small_out[k][n]
            outs.append(src.reshape(args[n].shape))
    return tuple(outs)
```

```python
import functools
import math

import jax
import jax.numpy as jnp
from jax import lax
from jax.experimental import pallas as pl
from jax.experimental.pallas import tpu as pltpu

F32 = jnp.float32
BF16 = jnp.bfloat16

D_MODEL = 1024
SB_HEADS = 8
SB_HEAD_DIM = 64
SB_WIDTH = 512
SSM_WIDTH = 512
SSM_GROUP = 16
SSM_GROUPS = 32
SSM_STATE = 64
XA_HEADS = 4
XA_HEAD_DIM = 256
D_FF = 2816
RMS_EPS = 1e-6
IN_WIDTH = 4096
N_DEV = 8

ADAM_LR = 0.001
ADAM_B1 = 0.9
ADAM_B2 = 0.999
ADAM_EPS = 1e-08
ADAM_WD = 0.01
ADAM_STEP = 10

LANES = 128
SUBLANES = 8
VMEM_LIMIT = 48 * 1024 * 1024

_GELU_C = math.sqrt(2.0 / math.pi)


def _cparams(*sem):
    return pltpu.CompilerParams(dimension_semantics=sem, vmem_limit_bytes=VMEM_LIMIT)


def _pick(n, cands):
    for c in cands:
        if n % c == 0:
            return c
    return n


def _gelu(x):
    return 0.5 * x * (1.0 + jnp.tanh(_GELU_C * (x + 0.044715 * x * x * x)))


def _gelu_and_grad(x):
    t = jnp.tanh(_GELU_C * (x + 0.044715 * x * x * x))
    g = 0.5 * x * (1.0 + t)
    dg = 0.5 * (1.0 + t) + 0.5 * x * (1.0 - t * t) * _GELU_C * (1.0 + 3.0 * 0.044715 * x * x)
    return g, dg


def _sigmoid(x):
    return 1.0 / (1.0 + jnp.exp(-x))


def _dot(a, b, ca, cb):
    return lax.dot_general(a.astype(BF16), b.astype(BF16), (((ca,), (cb,)), ((), ())),
                           preferred_element_type=F32)


MM_TILES = (1024, 768, 512, 256, 128)
MM_K_TILES = (2048, 1536) + MM_TILES
MM_PAIR = 2
MM_WIDE = 1536


def _matmul(a, b, *, ta=False, tb=False, out_dtype=F32, name, b_block0=0, n_blocks=None,
            out_cb=None, out_into=None, out_block0=0, acc_in=None, rider=None):
    if ta:
        K, M = a.shape
    else:
        M, K = a.shape
    b_cb = None
    if b.ndim == 3:
        b_cb = b.shape[2]
        n_blocks = b.shape[0] - b_block0 if n_blocks is None else n_blocks
        N, K2 = (b.shape[1], n_blocks * b_cb) if tb else (n_blocks * b_cb, b.shape[1])
    elif tb:
        N, K2 = b.shape
    else:
        K2, N = b.shape
    assert K == K2, (a.shape, b.shape, ta, tb)
    if out_into is not None:
        out_cb = out_into.shape[2]
    tm = _pick(M, MM_TILES)
    pair = lambda cb_, count: MM_PAIR if (cb_ * MM_PAIR <= MM_WIDE and count % MM_PAIR == 0) else 1
    b_pair = pair(b_cb, n_blocks) if b_cb else 1
    o_pair = pair(out_cb, N // out_cb) if out_cb else 1
    if b_cb and not tb:
        tn = b_cb * b_pair
    elif out_cb:
        tn = out_cb * o_pair
    else:
        tn = _pick(N, MM_TILES)
    if b_cb and tb:
        tk = b_cb * b_pair
    else:
        tk = _pick(K, MM_TILES if tn > MM_TILES[0] else MM_K_TILES)
    nk = K // tk
    ca, cb = (0 if ta else 1), (1 if tb else 0)
    has_acc = acc_in is not None
    has_into = out_into is not None

    def body(*refs):
        a_ref, b_ref = refs[0], refs[1]
        pos = 2
        c_ref = None
        if has_acc:
            c_ref = refs[pos]
            pos += 1
        if has_into:
            pos += 1
        o_ref = refs[pos]
        b_tile = b_ref[...] if b_cb is None else jnp.concatenate([b_ref[t] for t in range(b_pair)], axis=1)
        p = _dot(a_ref[...], b_tile, ca, cb)

        def write(val):
            val = val.astype(out_dtype)
            if out_cb is None:
                o_ref[...] = val
            else:
                for t in range(o_pair):
                    o_ref[t] = val[:, t * out_cb:(t + 1) * out_cb]

        if nk == 1:
            write((p + c_ref[...]) if has_acc else p)
        else:
            acc_ref = refs[pos + 1]
            k = pl.program_id(2)

            @pl.when(k == 0)
            def _():
                acc_ref[...] = (p + c_ref[...]) if has_acc else p

            @pl.when(k > 0)
            def _():
                acc_ref[...] += p

            @pl.when(k == nk - 1)
            def _():
                write(acc_ref[...])

    nj, ni = N // tn, M // tm
    a_bytes, b_bytes = a.size * a.dtype.itemsize, K * N * b.dtype.itemsize
    n_outer = a_bytes * nj + b_bytes * (1 if nk == 1 else ni) <= a_bytes * (1 if nk == 1 else nj) + b_bytes * ni
    grid = (nj, ni, nk) if n_outer else (ni, nj, nk)

    def spec(block, index):
        return pl.BlockSpec(block, (lambda g0, g1, k: index(g0, g1, k)) if n_outer else (lambda g0, g1, k: index(g1, g0, k)))

    a_spec = spec((tk, tm), lambda j, i, k: (k, i)) if ta else spec((tm, tk), lambda j, i, k: (i, k))
    if b_cb is None:
        b_spec = spec((tn, tk), lambda j, i, k: (j, k)) if tb else spec((tk, tn), lambda j, i, k: (k, j))
    elif tb:
        b_spec = spec((b_pair, tn, b_cb), lambda j, i, k: (b_block0 // b_pair + k, j, 0))
    else:
        b_spec = spec((b_pair, tk, b_cb), lambda j, i, k: (b_block0 // b_pair + j, k, 0))
    in_specs = [a_spec, b_spec]
    operands = [a, b]
    aliases = {}
    if has_acc:
        in_specs.append(spec((tm, tn), lambda j, i, k: (i, j)))
        operands.append(acc_in)
    if has_into:
        aliases = {len(operands): 0}
        in_specs.append(pl.BlockSpec(memory_space=pl.ANY))
        operands.append(out_into)
    if out_cb is None:
        out_shape = jax.ShapeDtypeStruct((M, N), out_dtype)
        out_spec = spec((tm, tn), lambda j, i, k: (i, j))
    else:
        out_shape = (jax.ShapeDtypeStruct(out_into.shape, out_into.dtype) if has_into
                     else jax.ShapeDtypeStruct((N // out_cb, M, out_cb), out_dtype))
        out_spec = spec((o_pair, tm, out_cb), lambda j, i, k: (out_block0 // o_pair + j, i, 0))
    if rider is not None:
        assert not has_into
        (out,), brought = _call(body, name=name, rider=rider, grid=grid, in_specs=in_specs,
                                out_specs=(out_spec,), out_shape=(out_shape,), operands=operands,
                                scratch_shapes=[] if nk == 1 else [pltpu.VMEM((tm, tn), F32)])
        return out, brought
    return pl.pallas_call(
        body, name=name, out_shape=out_shape,
        grid=grid,
        in_specs=in_specs, out_specs=out_spec, input_output_aliases=aliases,
        scratch_shapes=[] if nk == 1 else [pltpu.VMEM((tm, tn), F32)],
        compiler_params=_cparams("parallel", "parallel", "arbitrary"),
    )(*operands)


def _rms(x, g):
    r = lax.rsqrt(jnp.mean(x * x, axis=-1, keepdims=True) + RMS_EPS)
    return x * r * g


def _rms_bwd(dy, x, g):
    r = lax.rsqrt(jnp.mean(x * x, axis=-1, keepdims=True) + RMS_EPS)
    xh = x * r
    dxh = dy * g
    dx = r * (dxh - xh * jnp.mean(dxh * xh, axis=-1, keepdims=True))
    dg = jnp.sum(dy * xh, axis=0, keepdims=True)
    return dx, dg


def _row_tile(rows):
    return _pick(rows, (512, 256, 128, 64, 32, 16, 8))


def _rms_fwd(x, g, *, name, rider=None):
    R, D = x.shape
    tr = _row_tile(R)

    def body(x_ref, g_ref, h_ref):
        h_ref[...] = _rms(x_ref[...], g_ref[...]).astype(BF16)

    (h,), brought = _call(
        body, name=name, rider=rider, out_shape=(jax.ShapeDtypeStruct((R, D), BF16),), grid=(R // tr,),
        in_specs=[pl.BlockSpec((tr, D), lambda i: (i, 0)), pl.BlockSpec((1, D), lambda i: (0, 0))],
        out_specs=(pl.BlockSpec((tr, D), lambda i: (i, 0)),), scratch_shapes=[], operands=(x, g))
    return h if rider is None else (h, brought)


def _resnorm_norm(x, z, g_post, g_next, *, name):
    R, D = x.shape
    tr = _row_tile(R)

    def body(x_ref, z_ref, gp_ref, gn_ref, xn_ref, h_ref):
        xn = x_ref[...] + _rms(z_ref[...], gp_ref[...])
        xn_ref[...] = xn
        h_ref[...] = _rms(xn, gn_ref[...]).astype(BF16)

    row = pl.BlockSpec((tr, D), lambda i: (i, 0))
    vec = pl.BlockSpec((1, D), lambda i: (0, 0))
    return pl.pallas_call(
        body, name=name,
        out_shape=(jax.ShapeDtypeStruct((R, D), F32), jax.ShapeDtypeStruct((R, D), BF16)),
        grid=(R // tr,), in_specs=[row, row, vec, vec], out_specs=(row, row),
        compiler_params=_cparams("parallel"),
    )(x, z, g_post, g_next)


def _final_loss(x, z, g_post, target, *, name):
    R, D = x.shape
    tr = _row_tile(R)

    def body(x_ref, z_ref, gp_ref, t_ref, loss_ref, dy_ref, dz_ref, dg_ref):
        i = pl.program_id(0)
        z = z_ref[...]
        g = gp_ref[...]
        err = x_ref[...] + _rms(z, g) - t_ref[...]
        dy = err * (1.0 / D)
        dy_ref[...] = dy
        dz, dg = _rms_bwd(dy, z, g)
        dz_ref[...] = dz.astype(BF16)
        part = 0.5 * jnp.sum(jnp.sum(err * err, axis=-1, keepdims=True) * (1.0 / D), axis=0, keepdims=True)

        @pl.when(i == 0)
        def _():
            loss_ref[...] = part
            dg_ref[...] = dg

        @pl.when(i > 0)
        def _():
            loss_ref[...] += part
            dg_ref[...] += dg

    row = pl.BlockSpec((tr, D), lambda i: (i, 0))
    vec = pl.BlockSpec((1, D), lambda i: (0, 0))
    return pl.pallas_call(
        body, name=name,
        out_shape=(jax.ShapeDtypeStruct((1, 1), F32), jax.ShapeDtypeStruct((R, D), F32),
                   jax.ShapeDtypeStruct((R, D), BF16), jax.ShapeDtypeStruct((1, D), F32)),
        grid=(R // tr,), in_specs=[row, row, vec, row],
        out_specs=(pl.BlockSpec((1, 1), lambda i: (0, 0)), row, row, vec),
        compiler_params=_cparams("arbitrary"),
    )(x, z, g_post, target)


def _norm_bwd_pair(dres, dh, xk, g_pre, zprev, g_prev_post, *, name, rider=None):
    R, D = xk.shape
    tr = _row_tile(R)

    def body(dres_ref, dh_ref, x_ref, gpre_ref, z_ref, gpost_ref, dx_ref, dz_ref, dgpre_ref, dgpost_ref):
        i = pl.program_id(0)
        d1, dgpre = _rms_bwd(dh_ref[...], x_ref[...], gpre_ref[...])
        dx = dres_ref[...] + d1
        dx_ref[...] = dx
        dz, dgpost = _rms_bwd(dx, z_ref[...], gpost_ref[...])
        dz_ref[...] = dz.astype(BF16)

        @pl.when(i == 0)
        def _():
            dgpre_ref[...] = dgpre
            dgpost_ref[...] = dgpost

        @pl.when(i > 0)
        def _():
            dgpre_ref[...] += dgpre
            dgpost_ref[...] += dgpost

    row = pl.BlockSpec((tr, D), lambda i: (i, 0))
    vec = pl.BlockSpec((1, D), lambda i: (0, 0))
    return _call(
        body, name=name, rider=rider,
        out_shape=(jax.ShapeDtypeStruct((R, D), F32), jax.ShapeDtypeStruct((R, D), BF16),
                   jax.ShapeDtypeStruct((1, D), F32), jax.ShapeDtypeStruct((1, D), F32)),
        grid=(R // tr,), in_specs=[row, row, row, vec, row, vec], out_specs=(row, row, vec, vec),
        scratch_shapes=[], operands=(dres, dh, xk, g_pre, zprev, g_prev_post))


def _norm_bwd_single(dres, dh, xk, g_pre, *, name, rider=None):
    R, D = xk.shape
    tr = _row_tile(R)
    has_res = dres is not None

    def body(*refs):
        if has_res:
            dres_ref, dh_ref, x_ref, gpre_ref, dx_ref, dgpre_ref = refs
        else:
            dh_ref, x_ref, gpre_ref, dx_ref, dgpre_ref = refs
        i = pl.program_id(0)
        d1, dgpre = _rms_bwd(dh_ref[...], x_ref[...], gpre_ref[...])
        dx_ref[...] = dres_ref[...] + d1 if has_res else d1

        @pl.when(i == 0)
        def _():
            dgpre_ref[...] = dgpre

        @pl.when(i > 0)
        def _():
            dgpre_ref[...] += dgpre

    row = pl.BlockSpec((tr, D), lambda i: (i, 0))
    vec = pl.BlockSpec((1, D), lambda i: (0, 0))
    ins = ([dres] if has_res else []) + [dh, xk, g_pre]
    res, brought = _call(
        body, name=name, rider=rider,
        out_shape=(jax.ShapeDtypeStruct((R, D), F32), jax.ShapeDtypeStruct((1, D), F32)),
        grid=(R // tr,), in_specs=([row] if has_res else []) + [row, row, vec], out_specs=(row, vec),
        scratch_shapes=[], operands=ins)
    return res if rider is None else (res, brought)


SB_BLOCK = 256
SB_QBLOCK = 512
SB_DEAD = -104.0


def _sb_tri(kind):
    r = lax.broadcasted_iota(jnp.int32, (SB_BLOCK, SB_BLOCK), 0)
    c = lax.broadcasted_iota(jnp.int32, (SB_BLOCK, SB_BLOCK), 1)
    keep = {"after": r > c, "before": r < c}[kind]
    return jnp.where(keep, 1.0, 0.0).astype(BF16)


def _sb_scores(qm, k_blk):
    z = _dot(qm, k_blk, 1, 1)
    sp = jnp.maximum(z, 0.0) + jnp.log(1.0 + jnp.exp(-jnp.abs(z)))
    return z, sp


def _sb_causal(rows):
    r = lax.broadcasted_iota(jnp.int32, (rows, SB_BLOCK), 0)
    c = lax.broadcasted_iota(jnp.int32, (rows, SB_BLOCK), 1)
    return c < r


def _head_masks():
    lane = lax.broadcasted_iota(jnp.int32, (1, LANES), 1)
    return [jnp.where(lane < SB_HEAD_DIM, 1.0, 0.0), jnp.where(lane >= SB_HEAD_DIM, 1.0, 0.0)]


def _sb_fwd(proj, *, name, rider=None):
    S = proj.shape[0]
    T = SB_BLOCK
    TQ = min(SB_QBLOCK, S)
    span = TQ // T
    nq = S // TQ
    npair = SB_WIDTH // LANES
    scale = SB_HEAD_DIM ** -0.5

    def body(q_ref, k_ref, v_ref, o_ref, tot_ref, first_ref, acc_ref, run_ref):
        masks = _head_masks()
        tri = _sb_tri("after")
        first_ref[...] = jnp.zeros_like(first_ref)
        slot = lax.broadcasted_iota(jnp.int32, first_ref.shape, 1)

        def alive():
            reach = jnp.maximum(jnp.max(run_ref[0]), jnp.max(run_ref[1]))
            return (reach > SB_DEAD).astype(jnp.int32)

        def q_block(i, _):
            qrow = pl.ds(pl.multiple_of(i * TQ, TQ), TQ)
            q = q_ref[qrow, :] * scale
            qm = [(q * m).astype(BF16) for m in masks]
            acc_ref[...] = jnp.zeros_like(acc_ref)
            run_ref[...] = jnp.zeros_like(run_ref)

            def k_block(j, own):
                krow = pl.ds(pl.multiple_of(j * T, T), T)
                k_blk = k_ref[krow, :].astype(BF16)
                v_blk = v_ref[krow, :].astype(BF16)
                r0 = 0 if own is None else own * T
                rows = pl.ds(r0, TQ - r0)
                for h in range(2):
                    z, sp = _sb_scores(qm[h][r0:], k_blk)
                    causal = None if own is None else _sb_causal(TQ - r0)
                    lf = -sp if causal is None else jnp.where(causal, -sp, 0.0)
                    e = jnp.exp(z - sp + _dot(lf, tri, 1, 0) + run_ref[h, rows])
                    w = e if causal is None else jnp.where(causal, e, 0.0)
                    acc_ref[h, rows] += _dot(w, v_blk, 1, 0)
                    run_ref[h, rows] += jnp.sum(lf, axis=1, keepdims=True)

            for d in reversed(range(span)):
                k_block(i * span + d, d)

            def below(carry):
                jj, _ = carry
                k_block(i * span - 1 - jj, None)
                return jj + 1, alive()

            done, _ = lax.while_loop(lambda c: jnp.logical_and(c[0] < i * span, c[1] > 0), below, (jnp.int32(0), alive()))
            o_ref[qrow, :] = (acc_ref[0] * masks[0] + acc_ref[1] * masks[1]).astype(BF16)
            tot_ref[qrow, :] = run_ref[0] * masks[0] + run_ref[1] * masks[1]
            first_ref[...] = jnp.where(slot == i, (i * span - done).astype(F32), first_ref[...])
            return 0

        lax.fori_loop(0, nq, q_block, 0)

    blk = lambda off: pl.BlockSpec((S, LANES), lambda p: (0, off + p))
    return _call(
        body, name=name, rider=rider,
        out_shape=(jax.ShapeDtypeStruct((S, SB_WIDTH), BF16), jax.ShapeDtypeStruct((S, SB_WIDTH), F32),
                   jax.ShapeDtypeStruct((npair, SUBLANES, LANES), F32)),
        grid=(npair,),
        in_specs=[blk(0), blk(npair), blk(2 * npair)],
        out_specs=(blk(0), blk(0), pl.BlockSpec((1, SUBLANES, LANES), lambda p: (p, 0, 0))),
        scratch_shapes=[pltpu.VMEM((2, TQ, LANES), F32), pltpu.VMEM((2, TQ, 1), F32)],
        operands=(proj, proj, proj))


def _sb_bwd(proj, tot, first, do_attn, *, name, rider=None):
    S = proj.shape[0]
    T = SB_BLOCK
    TQ = min(SB_QBLOCK, S)
    span = TQ // T
    nq = S // TQ
    npair = SB_WIDTH // LANES
    scale = SB_HEAD_DIM ** -0.5

    def body(q_ref, k_ref, v_ref, tot_ref, first_ref, do_ref, dq_ref, dk_ref, dv_ref,
             dqacc_ref, dkacc_ref, dvacc_ref, run_ref, grun_ref):
        masks = _head_masks()
        tri_after = _sb_tri("after")
        tri_before = _sb_tri("before")
        dkacc_ref[...] = jnp.zeros_like(dkacc_ref)
        dvacc_ref[...] = jnp.zeros_like(dvacc_ref)
        slot = lax.broadcasted_iota(jnp.int32, first_ref.shape, 1)

        def q_block(i, _):
            qrow = pl.ds(pl.multiple_of(i * TQ, TQ), TQ)
            q = q_ref[qrow, :] * scale
            do = do_ref[qrow, :].astype(F32)
            tot = tot_ref[qrow, :]
            qm = [(q * m).astype(BF16) for m in masks]
            dom = [(do * m).astype(BF16) for m in masks]
            ltot = [jnp.sum(tot * m, axis=1, keepdims=True) * (1.0 / SB_HEAD_DIM) for m in masks]
            dqacc_ref[...] = jnp.zeros_like(dqacc_ref)
            run_ref[...] = jnp.zeros_like(run_ref)
            grun_ref[...] = jnp.zeros_like(grun_ref)

            def k_block(j, own):
                krow = pl.ds(pl.multiple_of(j * T, T), T)
                k_blk = k_ref[krow, :].astype(BF16)
                v_blk = v_ref[krow, :].astype(BF16)
                r0 = 0 if own is None else own * T
                rows = pl.ds(r0, TQ - r0)
                for h in range(2):
                    z, sp = _sb_scores(qm[h][r0:], k_blk)
                    causal = None if own is None else _sb_causal(TQ - r0)
                    lf = -sp if causal is None else jnp.where(causal, -sp, 0.0)
                    lsum = jnp.sum(lf, axis=1, keepdims=True)
                    later = (ltot[h][r0:] - run_ref[h, rows] - lsum) + _dot(lf, tri_after, 1, 0)
                    beta = jnp.exp(z - sp)
                    w = jnp.exp(z - sp + later)
                    if causal is not None:
                        w = jnp.where(causal, w, 0.0)
                    g = _dot(dom[h][r0:], v_blk, 1, 1) * w
                    gbefore = grun_ref[h, rows] + _dot(g, tri_before, 1, 0)
                    dz = g - beta * (g + gbefore)
                    if causal is not None:
                        dz = jnp.where(causal, dz, 0.0)
                    dz = dz.astype(BF16)
                    dqacc_ref[h, rows] += _dot(dz, k_blk, 1, 0)
                    dkacc_ref[krow, :] += _dot(dz, qm[h][r0:], 0, 0)
                    dvacc_ref[krow, :] += _dot(w, dom[h][r0:], 0, 0)
                    run_ref[h, rows] += lsum
                    grun_ref[h, rows] += jnp.sum(g, axis=1, keepdims=True)

            def above(j, _):
                k_block(j, None)
                return 0

            first = jnp.max(jnp.where(slot == i, first_ref[...], 0.0)).astype(jnp.int32)
            lax.fori_loop(jnp.clip(first, 0, i * span), i * span, above, 0)
            for d in range(span):
                k_block(i * span + d, d)
            dq_ref[qrow, :] = ((dqacc_ref[0] * masks[0] + dqacc_ref[1] * masks[1]) * scale).astype(BF16)
            return 0

        lax.fori_loop(0, nq, q_block, 0)
        dk_ref[...] = dkacc_ref[...].astype(BF16)
        dv_ref[...] = dvacc_ref[...].astype(BF16)

    blk = lambda off: pl.BlockSpec((S, LANES), lambda p: (0, off + p))
    out = jax.ShapeDtypeStruct((S, SB_WIDTH), BF16)
    return _call(
        body, name=name, rider=rider, out_shape=(out, out, out), grid=(npair,),
        in_specs=[blk(0), blk(npair), blk(2 * npair), blk(0), pl.BlockSpec((1, SUBLANES, LANES), lambda p: (p, 0, 0)),
                  blk(0)],
        out_specs=(blk(0), blk(0), blk(0)),
        scratch_shapes=[pltpu.VMEM((2, TQ, LANES), F32), pltpu.VMEM((S, LANES), F32), pltpu.VMEM((S, LANES), F32),
                        pltpu.VMEM((2, TQ, 1), F32), pltpu.VMEM((2, TQ, 1), F32)],
        operands=(proj, proj, proj, tot, first, do_attn))


SSM_HALVES = 2
SSM_HALF_CH = SSM_WIDTH // SSM_HALVES
SSM_HALF_ST = SSM_GROUPS * SSM_STATE // SSM_HALVES
SSM_CHUNK = 1024


def _cmul(ar, ai, br, bi):
    return ar * br - ai * bi, ar * bi + ai * br


def _ssm_tables(lam_re, lam_im):
    lr = lam_re.reshape(-1)
    li = lam_im.reshape(-1)
    pows = [(jnp.ones_like(lr), jnp.zeros_like(li)), (lr, li)]
    for _ in range(2, SUBLANES + 1):
        pows.append(_cmul(pows[-1][0], pows[-1][1], lr, li))
    row = jnp.arange(SUBLANES)[:, None]

    def shift_tab(d, keep):
        return [jnp.where(keep, pows[d][0][None, :], 0.0), jnp.where(keep, pows[d][1][None, :], 0.0)]

    fwd, bwd = [], []
    for d in (1, 2, 4):
        fwd += shift_tab(d, row >= d)
        bwd += shift_tab(d, row + d < SUBLANES)
    fwd += [jnp.stack([pows[r + 1][0] for r in range(SUBLANES)]), jnp.stack([pows[r + 1][1] for r in range(SUBLANES)])]
    bwd += [jnp.stack([pows[SUBLANES - r][0] for r in range(SUBLANES)]),
            jnp.stack([pows[SUBLANES - r][1] for r in range(SUBLANES)])]

    def halves(tabs):
        t = jnp.stack(tabs)
        return t.reshape(8, SUBLANES, SSM_HALVES, SSM_HALF_ST).transpose(2, 0, 1, 3)

    return halves(fwd), halves(bwd)


def _expand_groups(blocks, rows, cols):
    gh = SSM_GROUPS // SSM_HALVES
    R, C = gh * rows, gh * cols
    rep = (lax.broadcasted_iota(jnp.int32, (cols, C), 1) & (cols - 1)) == lax.broadcasted_iota(jnp.int32, (cols, C), 0)
    wide = _dot(blocks, jnp.where(rep, 1.0, 0.0), 1, 0)
    r = lax.broadcasted_iota(jnp.int32, (R, C), 0) >> (rows.bit_length() - 1)
    c = lax.broadcasted_iota(jnp.int32, (R, C), 1) >> (cols.bit_length() - 1)
    return jnp.where(r == c, wide, 0.0).astype(BF16)


def _collect_groups(full, rows, cols):
    R, C = full.shape
    r = lax.broadcasted_iota(jnp.int32, (R, C), 0) >> (rows.bit_length() - 1)
    c = lax.broadcasted_iota(jnp.int32, (R, C), 1) >> (cols.bit_length() - 1)
    rest = jnp.where(r == c, full, 0.0)
    fold = (lax.broadcasted_iota(jnp.int32, (C, cols), 0) & (cols - 1)) == lax.broadcasted_iota(jnp.int32, (C, cols), 1)
    fold = jnp.where(fold, 1.0, 0.0)
    out = jnp.zeros((R, cols), F32)
    for _ in range(3):
        piece = rest.astype(BF16)
        out = out + _dot(piece, fold, 1, 0)
        rest = rest - piece.astype(F32)
    return out


def _ssm_expand(bre_ref, bim_ref, cre_ref, cim_ref, bd_re_s, bd_im_s, cd_re_s, cd_im_s):
    bd_re_s[...] = _expand_groups(bre_ref[0], SSM_GROUP, SSM_STATE)
    bd_im_s[...] = _expand_groups(bim_ref[0], SSM_GROUP, SSM_STATE)
    cd_re_s[...] = _expand_groups(cre_ref[0], SSM_STATE, SSM_GROUP)
    cd_im_s[...] = _expand_groups(cim_ref[0], SSM_STATE, SSM_GROUP)


def _ssm_matrix_specs():
    b = pl.BlockSpec((1, SSM_HALF_CH, SSM_STATE), lambda h, c: (h, 0, 0))
    c = pl.BlockSpec((1, SSM_HALF_ST, SSM_GROUP), lambda h, c: (h, 0, 0))
    return [b, b, c, c]


def _ssm_matrix_scratch():
    return [pltpu.VMEM((SSM_HALF_CH, SSM_HALF_ST), BF16), pltpu.VMEM((SSM_HALF_CH, SSM_HALF_ST), BF16),
            pltpu.VMEM((SSM_HALF_ST, SSM_HALF_CH), BF16), pltpu.VMEM((SSM_HALF_ST, SSM_HALF_CH), BF16)]


def _ssm_fwd(proj, b_re, b_im, c_re, c_imneg, d_skip, tab, *, name, rider=None):
    S = proj.shape[0]
    Tc = min(SSM_CHUNK, S)
    nc = S // Tc
    u_blk0 = (3 * SB_WIDTH) // SSM_HALF_CH

    def body(u_ref, bre_ref, bim_ref, cre_ref, cim_ref, d_ref, tab_ref, y_ref, xre_ref, xim_ref, cre_s, cim_s,
             bd_re_s, bd_im_s, cd_re_s, cd_im_s):
        c = pl.program_id(1)

        @pl.when(c == 0)
        def _():
            cre_s[...] = jnp.zeros_like(cre_s)
            cim_s[...] = jnp.zeros_like(cim_s)
            _ssm_expand(bre_ref, bim_ref, cre_ref, cim_ref, bd_re_s, bd_im_s, cd_re_s, cd_im_s)

        u = u_ref[...]
        ub = u.astype(BF16)
        xre_ref[...] = _dot(ub, bd_re_s[...], 1, 0)
        xim_ref[...] = _dot(ub, bd_im_s[...], 1, 0)

        def slab(k, carry):
            car_re, car_im = carry
            rows = pl.ds(pl.multiple_of(k * SUBLANES, SUBLANES), SUBLANES)
            sre = xre_ref[rows, :]
            sim = xim_ref[rows, :]
            for n, d in enumerate((1, 2, 4)):
                pre, pim = tab_ref[0, 2 * n], tab_ref[0, 2 * n + 1]
                rre = pltpu.roll(sre, d, 0)
                rim = pltpu.roll(sim, d, 0)
                sre, sim = sre + (pre * rre - pim * rim), sim + (pre * rim + pim * rre)
            pre, pim = tab_ref[0, 6], tab_ref[0, 7]
            sre, sim = sre + (pre * car_re - pim * car_im), sim + (pre * car_im + pim * car_re)
            xre_ref[rows, :] = sre
            xim_ref[rows, :] = sim
            last = (SUBLANES - 1, SUBLANES)
            return (jnp.broadcast_to(sre[last[0]:last[1], :], sre.shape),
                    jnp.broadcast_to(sim[last[0]:last[1], :], sim.shape))

        car = lax.fori_loop(0, Tc // SUBLANES, slab, (cre_s[...], cim_s[...]))
        cre_s[...] = car[0]
        cim_s[...] = car[1]
        y = _dot(xre_ref[...], cd_re_s[...], 1, 0) + _dot(xim_ref[...], cd_im_s[...], 1, 0)
        y_ref[...] = y + d_ref[...] * u

    return _call(
        body, name=name, rider=rider,
        out_shape=(jax.ShapeDtypeStruct((S, SSM_WIDTH), F32),
                   jax.ShapeDtypeStruct((S, SSM_HALVES * SSM_HALF_ST), F32),
                   jax.ShapeDtypeStruct((S, SSM_HALVES * SSM_HALF_ST), F32)),
        grid=(SSM_HALVES, nc),
        in_specs=[pl.BlockSpec((Tc, SSM_HALF_CH), lambda h, c: (c, u_blk0 + h))] + _ssm_matrix_specs()
                 + [pl.BlockSpec((1, SSM_HALF_CH), lambda h, c: (0, h)),
                    pl.BlockSpec((1, 8, SUBLANES, SSM_HALF_ST), lambda h, c: (h, 0, 0, 0))],
        out_specs=(pl.BlockSpec((Tc, SSM_HALF_CH), lambda h, c: (c, h)),
                   pl.BlockSpec((Tc, SSM_HALF_ST), lambda h, c: (c, h)),
                   pl.BlockSpec((Tc, SSM_HALF_ST), lambda h, c: (c, h))),
        scratch_shapes=[pltpu.VMEM((SUBLANES, SSM_HALF_ST), F32), pltpu.VMEM((SUBLANES, SSM_HALF_ST), F32)]
                       + _ssm_matrix_scratch(),
        operands=(proj, b_re, b_im, c_re, c_imneg, d_skip, tab))


def _ssm_bwd(dy, proj, x_re, x_im, b_re, b_im, c_re, c_imneg, d_skip, tab, *, name, rider=None):
    S = proj.shape[0]
    Tc = min(SSM_CHUNK, S)
    nc = S // Tc
    u_blk0 = (3 * SB_WIDTH) // SSM_HALF_CH

    def body(dy_ref, u_ref, xre_ref, xim_ref, bre_ref, bim_ref, cre_ref, cim_ref, d_ref, tab_ref,
             du_ref, dbre_ref, dbim_ref, dcre_ref, dcim_ref, dd_ref, dlre_ref, dlim_ref,
             gre_s, gim_s, cre_s, cim_s, bd_re_s, bd_im_s, cd_re_s, cd_im_s, dbre_s, dbim_s, dcre_s, dcim_s):
        c = pl.program_id(1)

        @pl.when(c == 0)
        def _():
            _ssm_expand(bre_ref, bim_ref, cre_ref, cim_ref, bd_re_s, bd_im_s, cd_re_s, cd_im_s)
            cre_s[...] = jnp.zeros_like(cre_s)
            cim_s[...] = jnp.zeros_like(cim_s)
            dbre_s[...] = jnp.zeros_like(dbre_s)
            dbim_s[...] = jnp.zeros_like(dbim_s)
            dcre_s[...] = jnp.zeros_like(dcre_s)
            dcim_s[...] = jnp.zeros_like(dcim_s)
            dd_ref[...] = jnp.zeros_like(dd_ref)
            dlre_ref[...] = jnp.zeros_like(dlre_ref)
            dlim_ref[...] = jnp.zeros_like(dlim_ref)

        dy = dy_ref[...]
        dyb = dy.astype(BF16)
        u = u_ref[...]
        gre_s[...] = _dot(dyb, cd_re_s[...], 1, 1)
        gim_s[...] = _dot(dyb, cd_im_s[...], 1, 1)
        row = lax.broadcasted_iota(jnp.int32, (SUBLANES, SSM_HALF_ST), 0)
        nslab = Tc // SUBLANES

        def slab(kk, carry):
            car_re, car_im, acc_re, acc_im = carry
            k = nslab - 1 - kk
            rows = pl.ds(pl.multiple_of(k * SUBLANES, SUBLANES), SUBLANES)
            sre = gre_s[rows, :]
            sim = gim_s[rows, :]
            for n, d in enumerate((1, 2, 4)):
                pre, pim = tab_ref[0, 2 * n], tab_ref[0, 2 * n + 1]
                rre = pltpu.roll(sre, SUBLANES - d, 0)
                rim = pltpu.roll(sim, SUBLANES - d, 0)
                sre, sim = sre + (pre * rre + pim * rim), sim + (pre * rim - pim * rre)
            pre, pim = tab_ref[0, 6], tab_ref[0, 7]
            sre, sim = sre + (pre * car_re + pim * car_im), sim + (pre * car_im - pim * car_re)
            gre_s[rows, :] = sre
            gim_s[rows, :] = sim
            nre = jnp.where(row == SUBLANES - 1, car_re, pltpu.roll(sre, SUBLANES - 1, 0))
            nim = jnp.where(row == SUBLANES - 1, car_im, pltpu.roll(sim, SUBLANES - 1, 0))
            xr = xre_ref[rows, :]
            xi = xim_ref[rows, :]
            acc_re = acc_re + (nre * xr + nim * xi)
            acc_im = acc_im + (nim * xr - nre * xi)
            return (jnp.broadcast_to(sre[0:1, :], sre.shape), jnp.broadcast_to(sim[0:1, :], sim.shape), acc_re, acc_im)

        car = lax.fori_loop(0, nslab, slab, (cre_s[...], cim_s[...], dlre_ref[0], dlim_ref[0]))
        cre_s[...] = car[0]
        cim_s[...] = car[1]
        dlre_ref[0] = car[2]
        dlim_ref[0] = car[3]
        gre = gre_s[...].astype(BF16)
        gim = gim_s[...].astype(BF16)
        ub = u.astype(BF16)
        du = _dot(gre, bd_re_s[...], 1, 1) + _dot(gim, bd_im_s[...], 1, 1) + d_ref[...] * dy
        du_ref[...] = du.astype(BF16)
        dbre_s[...] += _dot(ub, gre, 0, 0)
        dbim_s[...] += _dot(ub, gim, 0, 0)
        dcre_s[...] += _dot(xre_ref[...], dyb, 0, 0)
        dcim_s[...] += _dot(xim_ref[...], dyb, 0, 0)
        dd_ref[...] += jnp.sum(dy * u, axis=0, keepdims=True)

        @pl.when(c == nc - 1)
        def _():
            dbre_ref[0] = _collect_groups(dbre_s[...], SSM_GROUP, SSM_STATE)
            dbim_ref[0] = _collect_groups(dbim_s[...], SSM_GROUP, SSM_STATE)
            dcre_ref[0] = _collect_groups(dcre_s[...], SSM_STATE, SSM_GROUP)
            dcim_ref[0] = _collect_groups(dcim_s[...], SSM_STATE, SSM_GROUP)

    rev = lambda c: nc - 1 - c
    return _call(
        body, name=name, rider=rider,
        out_shape=(jax.ShapeDtypeStruct((S, SSM_WIDTH), BF16),
                   jax.ShapeDtypeStruct((SSM_HALVES, SSM_HALF_CH, SSM_STATE), F32),
                   jax.ShapeDtypeStruct((SSM_HALVES, SSM_HALF_CH, SSM_STATE), F32),
                   jax.ShapeDtypeStruct((SSM_HALVES, SSM_HALF_ST, SSM_GROUP), F32),
                   jax.ShapeDtypeStruct((SSM_HALVES, SSM_HALF_ST, SSM_GROUP), F32),
                   jax.ShapeDtypeStruct((1, SSM_WIDTH), F32),
                   jax.ShapeDtypeStruct((SSM_HALVES, SUBLANES, SSM_HALF_ST), F32),
                   jax.ShapeDtypeStruct((SSM_HALVES, SUBLANES, SSM_HALF_ST), F32)),
        grid=(SSM_HALVES, nc),
        in_specs=[pl.BlockSpec((Tc, SSM_HALF_CH), lambda h, c: (rev(c), h)),
                  pl.BlockSpec((Tc, SSM_HALF_CH), lambda h, c: (rev(c), u_blk0 + h)),
                  pl.BlockSpec((Tc, SSM_HALF_ST), lambda h, c: (rev(c), h)),
                  pl.BlockSpec((Tc, SSM_HALF_ST), lambda h, c: (rev(c), h))] + _ssm_matrix_specs()
                 + [pl.BlockSpec((1, SSM_HALF_CH), lambda h, c: (0, h)),
                    pl.BlockSpec((1, 8, SUBLANES, SSM_HALF_ST), lambda h, c: (h, 0, 0, 0))],
        out_specs=(pl.BlockSpec((Tc, SSM_HALF_CH), lambda h, c: (rev(c), h)), *_ssm_matrix_specs(),
                   pl.BlockSpec((1, SSM_HALF_CH), lambda h, c: (0, h)),
                   pl.BlockSpec((1, SUBLANES, SSM_HALF_ST), lambda h, c: (h, 0, 0)),
                   pl.BlockSpec((1, SUBLANES, SSM_HALF_ST), lambda h, c: (h, 0, 0))),
        scratch_shapes=[pltpu.VMEM((Tc, SSM_HALF_ST), F32), pltpu.VMEM((Tc, SSM_HALF_ST), F32),
                        pltpu.VMEM((SUBLANES, SSM_HALF_ST), F32), pltpu.VMEM((SUBLANES, SSM_HALF_ST), F32)]
                       + _ssm_matrix_scratch()
                       + [pltpu.VMEM((SSM_HALF_CH, SSM_HALF_ST), F32), pltpu.VMEM((SSM_HALF_CH, SSM_HALF_ST), F32),
                          pltpu.VMEM((SSM_HALF_ST, SSM_HALF_CH), F32), pltpu.VMEM((SSM_HALF_ST, SSM_HALF_CH), F32)],
        operands=(dy, proj, x_re, x_im, b_re, b_im, c_re, c_imneg, d_skip, tab))


def _ssm_prepare(a_re, a_im, log_dt, b_re, b_im):
    dt = jnp.exp(log_dt)[:, None]
    mag = jnp.exp(a_re * dt)
    lre = mag * jnp.cos(a_im * dt)
    lim = mag * jnp.sin(a_im * dt)
    den = a_re * a_re + a_im * a_im
    fre = ((lre - 1.0) * a_re + lim * a_im) / den
    fim = (lim * a_re - (lre - 1.0) * a_im) / den
    bbre = fre[:, :, None] * b_re - fim[:, :, None] * b_im
    bbim = fre[:, :, None] * b_im + fim[:, :, None] * b_re
    return lre, lim, bbre, bbim


def _b_blocks(bbar):
    gh = SSM_GROUPS // SSM_HALVES
    b = bbar.reshape(SSM_HALVES, gh, SSM_STATE, SSM_GROUP).transpose(0, 1, 3, 2)
    return b.reshape(SSM_HALVES, SSM_HALF_CH, SSM_STATE)


def _bbar_from_blocks(db):
    gh = SSM_GROUPS // SSM_HALVES
    return db.reshape(SSM_HALVES, gh, SSM_GROUP, SSM_STATE).transpose(0, 1, 3, 2).reshape(SSM_GROUPS, SSM_STATE, SSM_GROUP)


def _c_blocks(cmat):
    gh = SSM_GROUPS // SSM_HALVES
    c = cmat.reshape(SSM_HALVES, gh, SSM_GROUP, SSM_STATE).transpose(0, 1, 3, 2)
    return c.reshape(SSM_HALVES, SSM_HALF_ST, SSM_GROUP)


def _c_from_blocks(dc):
    gh = SSM_GROUPS // SSM_HALVES
    return dc.reshape(SSM_HALVES, gh, SSM_STATE, SSM_GROUP).transpose(0, 1, 3, 2).reshape(SSM_GROUPS, SSM_GROUP, SSM_STATE)


def _glu_fwd(y_pre, w_glu, b_glu, *, name):
    S, W = y_pre.shape
    tr = _row_tile(S)

    def body(y_ref, w_ref, b_ref, o_ref):
        yg = _gelu(y_ref[...])
        gl = _dot(yg, w_ref[...], 1, 0) + b_ref[...]
        o_ref[...] = (yg * _sigmoid(gl)).astype(BF16)

    row = pl.BlockSpec((tr, W), lambda i: (i, 0))
    return pl.pallas_call(
        body, name=name, out_shape=jax.ShapeDtypeStruct((S, W), BF16), grid=(S // tr,),
        in_specs=[row, pl.BlockSpec((W, W), lambda i: (0, 0)), pl.BlockSpec((1, W), lambda i: (0, 0))],
        out_specs=row, compiler_params=_cparams("parallel"),
    )(y_pre, w_glu, b_glu)


def _glu_bwd(y_pre, do, w_glu, b_glu, *, name):
    S, W = y_pre.shape
    tr = _row_tile(S)

    def body(y_ref, do_ref, w_ref, b_ref, dy_ref, dw_ref, db_ref):
        i = pl.program_id(0)
        yg, dyg_dy = _gelu_and_grad(y_ref[...])
        ygb = yg.astype(BF16)
        sg = _sigmoid(_dot(ygb, w_ref[...], 1, 0) + b_ref[...])
        do = do_ref[...]
        dgl = do * yg * sg * (1.0 - sg)
        dglb = dgl.astype(BF16)
        dyg = do * sg + _dot(dglb, w_ref[...], 1, 1)
        dy_ref[...] = dyg * dyg_dy
        dw = _dot(ygb, dglb, 0, 0)
        db = jnp.sum(dgl, axis=0, keepdims=True)

        @pl.when(i == 0)
        def _():
            dw_ref[...] = dw
            db_ref[...] = db

        @pl.when(i > 0)
        def _():
            dw_ref[...] += dw
            db_ref[...] += db

    row = pl.BlockSpec((tr, W), lambda i: (i, 0))
    full = pl.BlockSpec((W, W), lambda i: (0, 0))
    vec = pl.BlockSpec((1, W), lambda i: (0, 0))
    return pl.pallas_call(
        body, name=name,
        out_shape=(jax.ShapeDtypeStruct((S, W), F32), jax.ShapeDtypeStruct((W, W), F32), jax.ShapeDtypeStruct((1, W), F32)),
        grid=(S // tr,), in_specs=[row, row, full, vec], out_specs=(row, full, vec),
        compiler_params=_cparams("arbitrary"),
    )(y_pre, do, w_glu, b_glu)


GATE_COL0 = 3 * SB_WIDTH + SSM_WIDTH


def _merge_fwd(proj, o_attn, o_ssm, w_ba, w_bs, b_gate, *, name):
    S = proj.shape[0]
    D = D_MODEL
    tr = _pick(S, (256, 128, 64, 32, 16, 8))
    gb = GATE_COL0 // D

    def body(ga_ref, gs_ref, oa_ref, os_ref, wa_ref, ws_ref, ba_ref, bs_ref, m_ref):
        pa = _dot(oa_ref[...], wa_ref[...], 1, 0)
        ps = _dot(os_ref[...], ws_ref[...], 1, 0)
        sa = _sigmoid(ga_ref[...] + ba_ref[...])
        ss = _sigmoid(gs_ref[...] + bs_ref[...])
        m_ref[...] = (sa * pa + ss * ps).astype(BF16)

    return pl.pallas_call(
        body, name=name, out_shape=jax.ShapeDtypeStruct((S, D), BF16), grid=(S // tr,),
        in_specs=[pl.BlockSpec((tr, D), lambda i: (i, gb)), pl.BlockSpec((tr, D), lambda i: (i, gb + 1)),
                  pl.BlockSpec((tr, SB_WIDTH), lambda i: (i, 0)), pl.BlockSpec((tr, SSM_WIDTH), lambda i: (i, 0)),
                  pl.BlockSpec((SB_WIDTH, D), lambda i: (0, 0)), pl.BlockSpec((SSM_WIDTH, D), lambda i: (0, 0)),
                  pl.BlockSpec((1, D), lambda i: (0, 0)), pl.BlockSpec((1, D), lambda i: (0, 1))],
        out_specs=pl.BlockSpec((tr, D), lambda i: (i, 0)),
        compiler_params=_cparams("parallel"),
    )(proj, proj, o_attn, o_ssm, w_ba, w_bs, b_gate, b_gate)


def _merge_bwd(dmerged, proj, o_attn, o_ssm, w_ba, w_bs, b_gate, *, name):
    S = proj.shape[0]
    D = D_MODEL
    tr = _pick(S, (256, 128, 64, 32, 16, 8))
    gb = GATE_COL0 // D

    def body(dm_ref, ga_ref, gs_ref, oa_ref, os_ref, wa_ref, ws_ref, ba_ref, bs_ref,
             doa_ref, dos_ref, dg_ref, db_ref, dwa_ref, dws_ref):
        i = pl.program_id(0)
        dm = dm_ref[...]
        oa = oa_ref[...]
        osm = os_ref[...]
        pa = _dot(oa, wa_ref[...], 1, 0)
        ps = _dot(osm, ws_ref[...], 1, 0)
        sa = _sigmoid(ga_ref[...] + ba_ref[...])
        ss = _sigmoid(gs_ref[...] + bs_ref[...])
        dpa = (dm * sa).astype(BF16)
        dps = (dm * ss).astype(BF16)
        dga = dm * pa * sa * (1.0 - sa)
        dgs = dm * ps * ss * (1.0 - ss)
        dg_ref[:, :D] = dga.astype(BF16)
        dg_ref[:, D:] = dgs.astype(BF16)
        doa_ref[...] = _dot(dpa, wa_ref[...], 1, 1).astype(BF16)
        dos_ref[...] = _dot(dps, ws_ref[...], 1, 1)
        dwa = _dot(oa, dpa, 0, 0)
        dws = _dot(osm, dps, 0, 0)
        dba = jnp.sum(dga, axis=0, keepdims=True)
        dbs = jnp.sum(dgs, axis=0, keepdims=True)

        @pl.when(i == 0)
        def _():
            dwa_ref[...] = dwa
            dws_ref[...] = dws
            db_ref[:, :D] = dba
            db_ref[:, D:] = dbs

        @pl.when(i > 0)
        def _():
            dwa_ref[...] += dwa
            dws_ref[...] += dws
            db_ref[:, :D] += dba
            db_ref[:, D:] += dbs

    rowD = pl.BlockSpec((tr, D), lambda i: (i, 0))
    wspec = pl.BlockSpec((SB_WIDTH, D), lambda i: (0, 0))
    return pl.pallas_call(
        body, name=name,
        out_shape=(jax.ShapeDtypeStruct((S, SB_WIDTH), BF16), jax.ShapeDtypeStruct((S, SSM_WIDTH), F32),
                   jax.ShapeDtypeStruct((S, 2 * D), BF16), jax.ShapeDtypeStruct((1, 2 * D), F32),
                   jax.ShapeDtypeStruct((SB_WIDTH, D), F32), jax.ShapeDtypeStruct((SSM_WIDTH, D), F32)),
        grid=(S // tr,),
        in_specs=[rowD, pl.BlockSpec((tr, D), lambda i: (i, gb)), pl.BlockSpec((tr, D), lambda i: (i, gb + 1)),
                  pl.BlockSpec((tr, SB_WIDTH), lambda i: (i, 0)), pl.BlockSpec((tr, SSM_WIDTH), lambda i: (i, 0)),
                  wspec, wspec, pl.BlockSpec((1, D), lambda i: (0, 0)), pl.BlockSpec((1, D), lambda i: (0, 1))],
        out_specs=(pl.BlockSpec((tr, SB_WIDTH), lambda i: (i, 0)), pl.BlockSpec((tr, SSM_WIDTH), lambda i: (i, 0)),
                   pl.BlockSpec((tr, 2 * D), lambda i: (i, 0)), pl.BlockSpec((1, 2 * D), lambda i: (0, 0)),
                   wspec, wspec),
        compiler_params=_cparams("arbitrary"),
    )(dmerged, proj, proj, o_attn, o_ssm, w_ba, w_bs, b_gate, b_gate)


def _xattn_probs(q, k, h):
    cols = slice(h * XA_HEAD_DIM, (h + 1) * XA_HEAD_DIM)
    s = _dot(q[:, cols], k[:, cols], 1, 1) * (XA_HEAD_DIM ** -0.5)
    s = s - jnp.max(s, axis=-1, keepdims=True)
    e = jnp.exp(s)
    return e / jnp.sum(e, axis=-1, keepdims=True), cols


def _xattn_fwd(q2, k2, v2, *, name):
    S, D = q2.shape
    M = k2.shape[0]
    tr = _row_tile(S)

    def body(q_ref, k_ref, v_ref, o_ref):
        q = q_ref[...]
        k = k_ref[...]
        v = v_ref[...]
        for h in range(XA_HEADS):
            p, cols = _xattn_probs(q, k, h)
            o_ref[:, cols] = _dot(p, v[:, cols], 1, 0).astype(BF16)

    row = pl.BlockSpec((tr, D), lambda i: (i, 0))
    memb = pl.BlockSpec((M, D), lambda i: (0, 0))
    return pl.pallas_call(
        body, name=name, out_shape=jax.ShapeDtypeStruct((S, D), BF16), grid=(S // tr,),
        in_specs=[row, memb, memb], out_specs=row, compiler_params=_cparams("parallel"),
    )(q2, k2, v2)


def _xattn_bwd(q2, k2, v2, do2, *, name):
    S, D = q2.shape
    M = k2.shape[0]
    tr = _row_tile(S)
    scale = XA_HEAD_DIM ** -0.5

    def body(q_ref, k_ref, v_ref, do_ref, dq_ref, dk_ref, dv_ref):
        i = pl.program_id(0)

        @pl.when(i == 0)
        def _():
            dk_ref[...] = jnp.zeros_like(dk_ref)
            dv_ref[...] = jnp.zeros_like(dv_ref)

        q = q_ref[...]
        k = k_ref[...]
        v = v_ref[...]
        do = do_ref[...]
        for h in range(XA_HEADS):
            p, cols = _xattn_probs(q, k, h)
            dp = _dot(do[:, cols], v[:, cols], 1, 1)
            ds = (p * (dp - jnp.sum(dp * p, axis=-1, keepdims=True)) * scale).astype(BF16)
            dq_ref[:, cols] = _dot(ds, k[:, cols], 1, 0).astype(BF16)
            dk_ref[:, cols] += _dot(ds, q[:, cols], 0, 0)
            dv_ref[:, cols] += _dot(p, do[:, cols], 0, 0)

    row = pl.BlockSpec((tr, D), lambda i: (i, 0))
    memb = pl.BlockSpec((M, D), lambda i: (0, 0))
    return pl.pallas_call(
        body, name=name,
        out_shape=(jax.ShapeDtypeStruct((S, D), BF16), jax.ShapeDtypeStruct((M, D), F32), jax.ShapeDtypeStruct((M, D), F32)),
        grid=(S // tr,), in_specs=[row, memb, memb, row], out_specs=(row, memb, memb),
        compiler_params=_cparams("arbitrary"),
    )(q2, k2, v2, do2)


CONV_ROWS = 64
CONV_ROWS_FWD = 256


def _chunk(ref, c, rows):
    return ref[pl.ds(pl.multiple_of(c * rows, rows), rows), :]


def _rows_before(ref, c, rows):
    t0 = pl.multiple_of(jnp.maximum(c * rows - SUBLANES, 0), SUBLANES)
    return jnp.where(c > 0, ref[pl.ds(t0, SUBLANES), :], 0.0)


def _rows_after(ref, c, rows, n_chunks):
    t0 = pl.multiple_of(jnp.minimum((c + 1) * rows, n_chunks * rows - SUBLANES), SUBLANES)
    return jnp.where(c < n_chunks - 1, ref[pl.ds(t0, SUBLANES), :], 0.0)


def _shift_down(cur, before, d):
    out = pltpu.roll(cur, d, 0)
    r = lax.broadcasted_iota(jnp.int32, cur.shape, 0)
    for e in range(d):
        out = jnp.where(r == e, before[SUBLANES - d + e:SUBLANES - d + e + 1, :], out)
    return out


def _shift_up(cur, after, d):
    rows = cur.shape[0]
    out = pltpu.roll(cur, rows - d, 0)
    r = lax.broadcasted_iota(jnp.int32, cur.shape, 0)
    for e in range(d):
        out = jnp.where(r == rows - d + e, after[e:e + 1, :], out)
    return out


def _conv3(cur, before, w_ref, b_ref):
    return (w_ref[2:3, :] * cur + w_ref[1:2, :] * _shift_down(cur, before, 1)
            + w_ref[0:1, :] * _shift_down(cur, before, 2) + b_ref[...])


def _convgate_fwd(up_g, up_v, conv_w, conv_b, *, name):
    S, H = up_g.shape
    nb = H // LANES
    R = min(CONV_ROWS_FWD, S)
    n_chunks = S // R

    def body(g_ref, v_ref, wg_ref, wv_ref, bg_ref, bv_ref, a_ref):
        def chunk(c, _):
            cg = _conv3(_chunk(g_ref, c, R), _rows_before(g_ref, c, R), wg_ref, bg_ref)
            cv = _conv3(_chunk(v_ref, c, R), _rows_before(v_ref, c, R), wv_ref, bv_ref)
            a_ref[pl.ds(pl.multiple_of(c * R, R), R), :] = (_gelu(cg) * cv).astype(BF16)
            return 0

        lax.fori_loop(0, n_chunks, chunk, 0)

    col = lambda off: pl.BlockSpec((S, LANES), lambda j: (0, off + j))
    wcol = lambda off: pl.BlockSpec((3, LANES), lambda j: (0, off + j))
    bcol = lambda off: pl.BlockSpec((1, LANES), lambda j: (0, off + j))
    return pl.pallas_call(
        body, name=name, out_shape=jax.ShapeDtypeStruct((S, H), BF16), grid=(nb,),
        in_specs=[col(0), col(0), wcol(0), wcol(nb), bcol(0), bcol(nb)],
        out_specs=col(0), compiler_params=_cparams("parallel"),
    )(up_g, up_v, conv_w, conv_w, conv_b, conv_b)


def _convgate_bwd(up_g, up_v, da, conv_w, conv_b, *, name):
    S, H = up_g.shape
    nb = H // LANES
    R = min(CONV_ROWS, S)
    n_chunks = S // R

    def fold(a):
        return sum(a[r:r + SUBLANES] for r in range(0, a.shape[0], SUBLANES))

    def body(g_ref, v_ref, da_ref, wg_ref, wv_ref, bg_ref, bv_ref,
             dug_ref, duv_ref, dwg_ref, dwv_ref, dbg_ref, dbv_ref, dcg_s, dcv_s):
        def first_pass(c, acc):
            rows = pl.ds(pl.multiple_of(c * R, R), R)
            ug, uv = _chunk(g_ref, c, R), _chunk(v_ref, c, R)
            bg, bv = _rows_before(g_ref, c, R), _rows_before(v_ref, c, R)
            cg = _conv3(ug, bg, wg_ref, bg_ref)
            cv = _conv3(uv, bv, wv_ref, bv_ref)
            da = da_ref[rows, :]
            gl, dgl = _gelu_and_grad(cg)
            dcg = da * cv * dgl
            dcv = da * gl
            dcg_s[rows, :] = dcg
            dcv_s[rows, :] = dcv
            new = []
            for dc, u, before in ((dcg, ug, bg), (dcv, uv, bv)):
                new += [fold(dc * _shift_down(u, before, 2)), fold(dc * _shift_down(u, before, 1)), fold(dc * u), fold(dc)]
            return tuple(a + n for a, n in zip(acc, new))

        zero = jnp.zeros((SUBLANES, LANES), F32)
        acc = lax.fori_loop(0, n_chunks, first_pass, (zero,) * 8)
        total = [jnp.sum(a, axis=0, keepdims=True) for a in acc]
        for k, (dw_ref, db_ref) in enumerate(((dwg_ref, dbg_ref), (dwv_ref, dbv_ref))):
            dw_ref[0:1, :] = total[4 * k]
            dw_ref[1:2, :] = total[4 * k + 1]
            dw_ref[2:3, :] = total[4 * k + 2]
            db_ref[...] = total[4 * k + 3]

        def second_pass(c, _):
            rows = pl.ds(pl.multiple_of(c * R, R), R)
            for dc_s, w_ref, du_ref in ((dcg_s, wg_ref, dug_ref), (dcv_s, wv_ref, duv_ref)):
                cur, after = _chunk(dc_s, c, R), _rows_after(dc_s, c, R, n_chunks)
                du = w_ref[2:3, :] * cur + w_ref[1:2, :] * _shift_up(cur, after, 1) + w_ref[0:1, :] * _shift_up(cur, after, 2)
                du_ref[rows, :] = du.astype(BF16)
            return 0

        lax.fori_loop(0, n_chunks, second_pass, 0)

    col = lambda off: pl.BlockSpec((S, LANES), lambda j: (0, off + j))
    wcol = lambda off: pl.BlockSpec((3, LANES), lambda j: (0, off + j))
    bcol = lambda off: pl.BlockSpec((1, LANES), lambda j: (0, off + j))
    return pl.pallas_call(
        body, name=name,
        out_shape=(jax.ShapeDtypeStruct((S, H), BF16), jax.ShapeDtypeStruct((S, H), BF16),
                   jax.ShapeDtypeStruct((3, H), F32), jax.ShapeDtypeStruct((3, H), F32),
                   jax.ShapeDtypeStruct((1, H), F32), jax.ShapeDtypeStruct((1, H), F32)),
        grid=(nb,),
        in_specs=[col(0), col(0), col(0), wcol(0), wcol(nb), bcol(0), bcol(nb)],
        out_specs=(col(0), col(0), wcol(0), wcol(0), bcol(0), bcol(0)),
        scratch_shapes=[pltpu.VMEM((S, LANES), F32), pltpu.VMEM((S, LANES), F32)],
        compiler_params=_cparams("parallel"),
    )(up_g, up_v, da, conv_w, conv_w, conv_b, conv_b)


def _local_step(x, mem, target, w_in, late_wire, P, core):
    mm = _matmul
    h1, (w_in,) = _rms_fwd(x, P["norm_mix_pre"], name="rms_mix_pre", rider=_fill_xy([w_in]))
    w_in, = _fill_c([w_in]).run(name="gather_in_c")
    w_in = w_in.reshape((N_DEV,) + w_in.shape[2:])
    n_mid = len(LATE) - len(REDUCE_FFN)
    proj, wire_mid = mm(h1, w_in, name="mm_in", rider=_fill_xy(late_wire[:n_mid]))
    (o_attn, sb_tot, sb_first), wires = _sb_fwd(
        proj, name="sb_fwd", rider=_Exchange.join(_fill_c(wire_mid), _fill_xy(late_wire[n_mid:])))
    wire_mid, wire_ffn = wires[:n_mid], wires[n_mid:]

    ssm_prep = lambda *a: _ssm_prepare(*a)
    (lam_re, lam_im, bb_re, bb_im), prep_vjp = jax.vjp(
        ssm_prep, P["ssm_a_re"], P["ssm_a_im"], P["ssm_log_dt"], P["ssm_b_re"], P["ssm_b_im"])
    tab_f, tab_b = _ssm_tables(lam_re, lam_im)
    bd_re, bd_im = _b_blocks(bb_re), _b_blocks(bb_im)
    cd_re, cd_imneg = _c_blocks(P["ssm_c_re"]), _c_blocks(-P["ssm_c_im"])
    (y_pre, x_re, x_im), wire_ffn = _ssm_fwd(proj, bd_re, bd_im, cd_re, cd_imneg, P["ssm_d"], tab_f,
                                             name="ssm_fwd", rider=_fill_c(wire_ffn))
    W = _weights_from_wire(dict(zip(LATE, list(wire_mid) + list(wire_ffn))))
    W["w_in"] = w_in
    o_ssm = _glu_fwd(y_pre, W["ssm_w_glu"], P["ssm_b_glu"], name="glu_fwd")

    merged = _merge_fwd(proj, o_attn, o_ssm, W["w_branch_attn"], W["w_branch_ssm"], P["b_gate"], name="merge_fwd")
    mo = mm(merged, W["w_out"], name="mm_out")
    x1, h2 = _resnorm_norm(x, mo, P["norm_mix_post"], P["norm_xa_pre"], name="resnorm_1")

    mem_n = _rms_fwd(mem, P["norm_mem"], name="rms_mem")
    q2 = mm(h2, W["xa_wq"], out_dtype=BF16, name="mm_xq")
    k2 = mm(mem_n, W["xa_wk"], out_dtype=BF16, name="mm_xk")
    v2 = mm(mem_n, W["xa_wv"], out_dtype=BF16, name="mm_xv")
    o2 = _xattn_fwd(q2, k2, v2, name="xattn_fwd")
    xa = mm(o2, W["xa_wo"], name="mm_xo")
    x2, h3 = _resnorm_norm(x1, xa, P["norm_xa_post"], P["norm_ffn_pre"], name="resnorm_2")

    half = N_DEV // 2
    up_g = mm(h3, W["ffn_w_up"], n_blocks=half, name="mm_up_g")
    up_v = mm(h3, W["ffn_w_up"], b_block0=half, name="mm_up_v")
    act = _convgate_fwd(up_g, up_v, W["ffn_conv_w"], P["ffn_conv_b"], name="convgate_fwd")
    f = mm(act, W["ffn_w_down"], name="mm_down")
    loss, dy, df, dg_ffn_post = _final_loss(x2, f, P["norm_ffn_post"], target, name="final_loss")

    G = {"norm_ffn_post": dg_ffn_post}
    dact = mm(df, W["ffn_w_down"], tb=True, name="mm_down_dx")
    G["ffn_w_down"] = mm(act, df, ta=True, name="mm_down_dw")
    dug, duv, dwg, dwv, dbg, dbv = _convgate_bwd(up_g, up_v, dact, W["ffn_conv_w"], P["ffn_conv_b"], name="convgate_bwd")
    G["ffn_conv_w"] = jnp.concatenate([dwg, dwv], axis=1)
    G["ffn_conv_b"] = jnp.concatenate([dbg, dbv], axis=1)
    dh3 = mm(dug, W["ffn_w_up"], tb=True, n_blocks=half, name="mm_up_g_dx")
    dh3 = mm(duv, W["ffn_w_up"], tb=True, b_block0=half, acc_in=dh3, name="mm_up_v_dx")
    dw_up = mm(h3, dug, ta=True, out_into=lax.empty(W["ffn_w_up"].shape, F32), name="mm_up_g_dw")
    G["ffn_w_up"] = mm(h3, duv, ta=True, out_into=dw_up, out_block0=half, name="mm_up_v_dw")
    blocks = {n: _grad_blocks(n, G[n]) for n in REDUCE_FFN}
    (dx2, dxa, G["norm_ffn_pre"], G["norm_xa_post"]), from_core = _norm_bwd_pair(
        dy, dh3, x2, P["norm_ffn_pre"], xa, P["norm_xa_post"], name="norm_bwd_3",
        rider=_send_c([blocks[n] for n in REDUCE_FFN]))
    pair = {n: _pair_sum(blocks[n], r, core, name="pair_sum_" + n) for n, r in zip(REDUCE_FFN, from_core)}

    G["xa_wo"] = mm(o2, dxa, ta=True, name="mm_xo_dw")
    do2 = mm(dxa, W["xa_wo"], tb=True, out_dtype=BF16, name="mm_xo_dx")
    dq2, dk2, dv2 = _xattn_bwd(q2, k2, v2, do2, name="xattn_bwd")
    G["xa_wq"] = mm(h2, dq2, ta=True, name="mm_xq_dw")
    dh2 = mm(dq2, W["xa_wq"], tb=True, name="mm_xq_dx")
    G["xa_wk"] = mm(mem_n, dk2, ta=True, name="mm_xk_dw")
    G["xa_wv"] = mm(mem_n, dv2, ta=True, name="mm_xv_dw")
    dmem_n = jnp.concatenate([dk2, dv2], axis=1)
    wkv = jnp.concatenate([W["xa_wk"], W["xa_wv"]], axis=1)
    dmem = mm(dmem_n, wkv, tb=True, name="mm_xkv_dx")
    _, G["norm_mem"] = _norm_bwd_single(None, dmem, mem, P["norm_mem"], name="norm_bwd_mem")
    (dx1, dmo, G["norm_xa_pre"], G["norm_mix_post"]), _ = _norm_bwd_pair(
        dx2, dh2, x1, P["norm_xa_pre"], mo, P["norm_mix_post"], name="norm_bwd_2")

    G["w_out"] = mm(merged, dmo, ta=True, name="mm_out_dw")
    dmerged = mm(dmo, W["w_out"], tb=True, name="mm_out_dx")
    do_attn, do_ssm, dgate, G["b_gate"], G["w_branch_attn"], G["w_branch_ssm"] = _merge_bwd(
        dmerged, proj, o_attn, o_ssm, W["w_branch_attn"], W["w_branch_ssm"], P["b_gate"], name="merge_bwd")
    dy_pre, G["ssm_w_glu"], G["ssm_b_glu"] = _glu_bwd(y_pre, do_ssm, W["ssm_w_glu"], P["ssm_b_glu"], name="glu_bwd")
    blocks.update({n: _grad_blocks(n, G[n]) for n in REDUCE_MID})
    (du, dbd_re, dbd_im, dcd_re, dcd_imneg, G["ssm_d"], dl_re, dl_im), brought = _ssm_bwd(
        dy_pre, proj, x_re, x_im, bd_re, bd_im, cd_re, cd_imneg, P["ssm_d"], tab_b, name="ssm_bwd",
        rider=_Exchange.join(_send_c([blocks[n] for n in REDUCE_MID]), _scatter_xy([pair[n] for n in REDUCE_FFN])))
    from_core, from_chips = brought[:len(REDUCE_MID)], brought[len(REDUCE_MID):]
    reduced = {n: (pair[n], parts) for n, parts in zip(REDUCE_FFN, from_chips)}
    pair.update({n: _pair_sum(blocks[n], r, core, name="pair_sum_" + n) for n, r in zip(REDUCE_MID, from_core)})
    G["ssm_c_re"] = _c_from_blocks(dcd_re)
    G["ssm_c_im"] = -_c_from_blocks(dcd_imneg)
    dlam_re = jnp.sum(dl_re, axis=1).reshape(SSM_GROUPS, SSM_STATE)
    dlam_im = jnp.sum(dl_im, axis=1).reshape(SSM_GROUPS, SSM_STATE)
    (G["ssm_a_re"], G["ssm_a_im"], G["ssm_log_dt"], G["ssm_b_re"], G["ssm_b_im"]) = prep_vjp(
        (dlam_re, dlam_im, _bbar_from_blocks(dbd_re), _bbar_from_blocks(dbd_im)))
    G["ffn_conv_b"] = G["ffn_conv_b"].reshape(N_DEV, FF_LOCAL_PAD)[:, :FF_LOCAL]
    small = [G[n].reshape(SMALL_SHAPE[n]) for n in SMALL_EARLY]
    (dq, dk, dv), brought = _sb_bwd(
        proj, sb_tot, sb_first, do_attn, name="sb_bwd",
        rider=_Exchange.join(_scatter_xy([pair[n] for n in REDUCE_MID]), _gather_xy_from(small)))
    from_chips, small = brought[:len(REDUCE_MID)], brought[len(REDUCE_MID):]
    reduced.update({n: (pair[n], parts) for n, parts in zip(REDUCE_MID, from_chips)})
    dproj = jnp.concatenate([dq, dk, dv, du, dgate], axis=1)
    G["w_in"], small = mm(h1, dproj, ta=True, out_cb=W["w_in"].shape[2], name="mm_in_dw", rider=_fill_c(small))
    g_in = _grad_blocks("w_in", G["w_in"])
    dh1, (from_core,) = mm(dproj, W["w_in"], tb=True, name="mm_in_dx", rider=_send_c([g_in]))
    pair_in = _pair_sum(g_in, from_core, core, name="pair_sum_w_in")
    (grad_x, dg_pre), (from_chips,) = _norm_bwd_single(dx1, dh1, x, P["norm_mix_pre"], name="norm_bwd_1",
                                                       rider=_scatter_xy([pair_in]))
    reduced["w_in"] = (pair_in, from_chips)
    last, = _gather_all([dg_pre]).run(name="gather_g_last")
    parts = dict(zip(SMALL_EARLY, small))
    parts["norm_mix_pre"] = last
    return loss, grad_x, parts, reduced


MESH = pl.DeviceIdType.MESH
_HBM = pl.BlockSpec(memory_space=pl.ANY)
N_XY = 4
N_XY_PEERS = 3


def _xy_peers(x, y):
    return [(1 - x, y), (x, 1 - y), (1 - x, 1 - y)]


class _Exchange:
    def __init__(self, arrays, out_shapes, plan, n_copies, alias):
        self.arrays = list(arrays)
        self.out_shapes = list(out_shapes)
        self.plan = plan
        self.n_copies = n_copies
        self.alias = list(alias) if isinstance(alias, (list, tuple)) else [alias] * len(self.arrays)

    @property
    def n(self):
        return len(self.arrays)

    def aliases(self, first_in, first_out):
        return {first_in + k: first_out + k for k in range(self.n) if self.alias[k]}

    @staticmethod
    def join(a, b):
        def plan(k, src, dst, x, y, c):
            return a.plan(k, src, dst, x, y, c) if k < a.n else b.plan(k - a.n, src, dst, x, y, c)

        return _Exchange(a.arrays + b.arrays, a.out_shapes + b.out_shapes, plan, max(a.n_copies, b.n_copies),
                         a.alias + b.alias)

    def sems(self):
        shape = (self.n, self.n_copies)
        return [pltpu.SemaphoreType.DMA(shape), pltpu.SemaphoreType.DMA(shape)]

    def _copies(self, ins, outs, send_sems, recv_sems):
        x, y, c = lax.axis_index("x"), lax.axis_index("y"), lax.axis_index("c")
        sends, lands, own = [], [], []
        for k in range(self.n):
            for j, (src, dst, dev, land) in enumerate(self.plan(k, ins[k], outs[k], x, y, c)):
                if dev is None:
                    own.append(pltpu.make_async_copy(src, dst, send_sems.at[k, j]))
                    continue
                sems = dict(send_sem=send_sems.at[k, j], recv_sem=recv_sems.at[k, j], device_id=dev, device_id_type=MESH)
                sends.append(pltpu.make_async_remote_copy(src_ref=src, dst_ref=dst, **sems))
                lands.append(pltpu.make_async_remote_copy(src_ref=src, dst_ref=land, **sems))
        return sends, lands, own

    def start(self, ins, outs, send_sems, recv_sems):
        sends, _, own = self._copies(ins, outs, send_sems, recv_sems)
        for cp in own + sends:
            cp.start()

    def finish(self, ins, outs, send_sems, recv_sems):
        sends, lands, own = self._copies(ins, outs, send_sems, recv_sems)
        for cp in lands:
            cp.wait_recv()
        for cp in sends:
            cp.wait_send()
        for cp in own:
            cp.wait()

    def run(self, *, name):
        n = self.n

        def body(*refs):
            parts = (refs[:n], refs[n:2 * n], refs[2 * n], refs[2 * n + 1])
            self.start(*parts)
            self.finish(*parts)

        return pl.pallas_call(
            body, name=name, out_shape=tuple(self.out_shapes),
            in_specs=[_HBM] * n, out_specs=tuple([_HBM] * n),
            input_output_aliases=self.aliases(0, 0),
            scratch_shapes=self.sems(),
        )(*self.arrays)


def _call(host_body, *, name, grid, in_specs, out_specs, out_shape, scratch_shapes, operands, rider=None):
    out_specs, out_shape = tuple(out_specs), tuple(out_shape)
    if rider is None:
        res = pl.pallas_call(
            host_body, name=name, grid=grid, in_specs=list(in_specs), out_specs=out_specs, out_shape=out_shape,
            scratch_shapes=list(scratch_shapes), compiler_params=_cparams(*["arbitrary"] * len(grid)),
        )(*operands)
        return tuple(res), None
    n, n_in, n_out, n_scr = rider.n, len(in_specs), len(out_specs), len(scratch_shapes)

    def body(*refs):
        pos = [0]

        def take(count):
            pos[0] += count
            return refs[pos[0] - count:pos[0]]

        h_in, r_in, h_out, r_out, h_scr = take(n_in), take(n), take(n_out), take(n), take(n_scr)
        send_sems, recv_sems = take(2)
        ids = [pl.program_id(a) for a in range(len(grid))]
        first = functools.reduce(jnp.logical_and, [i == 0 for i in ids])
        last = functools.reduce(jnp.logical_and, [i == g - 1 for i, g in zip(ids, grid)])

        @pl.when(first)
        def _():
            rider.start(r_in, r_out, send_sems, recv_sems)

        host_body(*h_in, *h_out, *h_scr)

        @pl.when(last)
        def _():
            rider.finish(r_in, r_out, send_sems, recv_sems)

    res = pl.pallas_call(
        body, name=name, grid=grid,
        in_specs=list(in_specs) + [_HBM] * n, out_specs=out_specs + tuple([_HBM] * n),
        out_shape=out_shape + tuple(rider.out_shapes),
        input_output_aliases=rider.aliases(n_in, n_out),
        scratch_shapes=list(scratch_shapes) + rider.sems(),
        compiler_params=_cparams(*["arbitrary"] * len(grid)),
    )(*operands, *rider.arrays)
    return tuple(res[:n_out]), list(res[n_out:])


def _same(arrays):
    return [jax.ShapeDtypeStruct(a.shape, a.dtype) for a in arrays]


def _fill_xy(bufs):
    def plan(k, src, dst, x, y, c):
        mine = 2 * x + y
        return [(src.at[mine, c], dst.at[mine, c], (px, py, c), dst.at[2 * px + py, c]) for px, py in _xy_peers(x, y)]

    return _Exchange(bufs, _same(bufs), plan, N_XY_PEERS, alias=True)


def _fill_c(bufs):
    def plan(k, src, dst, x, y, c):
        return [(src.at[:, c], dst.at[:, c], (x, y, 1 - c), dst.at[:, 1 - c])]

    return _Exchange(bufs, _same(bufs), plan, 1, alias=True)


def _slots(arrays):
    return [jax.ShapeDtypeStruct((N_XY, 2) + a.shape, a.dtype) for a in arrays]


def _gather_xy_from(srcs):
    def plan(k, src, dst, x, y, c):
        mine = 2 * x + y
        return ([(src, dst.at[mine, c], None, None)]
                + [(src, dst.at[mine, c], (px, py, c), dst.at[2 * px + py, c]) for px, py in _xy_peers(x, y)])

    return _Exchange(srcs, _slots(srcs), plan, 1 + N_XY_PEERS, alias=False)


def _gather_all(srcs):
    def plan(k, src, dst, x, y, c):
        mine = 2 * x + y
        out = [(src, dst.at[mine, c], None, None)]
        for fx, fy, fc in [(a, b, e) for a in (0, 1) for b in (0, 1) for e in (0, 1)][1:]:
            px, py, pc = (1 - x) if fx else x, (1 - y) if fy else y, (1 - c) if fc else c
            out.append((src, dst.at[mine, c], (px, py, pc), dst.at[2 * px + py, pc]))
        return out

    return _Exchange(srcs, _slots(srcs), plan, N_DEV, alias=False)


def _send_c(srcs):
    def plan(k, src, dst, x, y, c):
        return [(src.at[:, 1 - c], dst, (x, y, 1 - c), dst)]

    outs = [jax.ShapeDtypeStruct(a.shape[:1] + a.shape[2:], a.dtype) for a in srcs]
    return _Exchange(srcs, outs, plan, 1, alias=False)


def _scatter_xy(srcs):
    def plan(k, src, dst, x, y, c):
        return [(src.at[2 * px + py], dst.at[j], (px, py, c), dst.at[j]) for j, (px, py) in enumerate(_xy_peers(x, y))]

    outs = [jax.ShapeDtypeStruct((N_XY_PEERS,) + a.shape[1:], a.dtype) for a in srcs]
    return _Exchange(srcs, outs, plan, N_XY_PEERS, alias=False)


WIRE_DTYPE = BF16


def _pair_sum(g8, recv, core, *, name):
    n, _, R, C = g8.shape
    tr = _pick(R, (128, 64, 32, 16, 8))

    def body(core_ref, a_ref, b_ref, o_ref):
        o_ref[...] = (a_ref[0] + b_ref[...]).astype(WIRE_DTYPE)

    return pl.pallas_call(
        body, name=name, out_shape=jax.ShapeDtypeStruct((n, R, C), WIRE_DTYPE),
        grid_spec=pltpu.PrefetchScalarGridSpec(
            num_scalar_prefetch=1, grid=(n, R // tr),
            in_specs=[pl.BlockSpec((1, 1, tr, C), lambda s, i, core_ref: (s, core_ref[0], i, 0)),
                      pl.BlockSpec((1, tr, C), lambda s, i, core_ref: (s, i, 0))],
            out_specs=pl.BlockSpec((1, tr, C), lambda s, i, core_ref: (s, i, 0))),
        compiler_params=_cparams("parallel", "parallel"),
    )(core, g8, recv)


def _adamw_math(w, g, m, v):
    m = ADAM_B1 * m + (1.0 - ADAM_B1) * g
    v = ADAM_B2 * v + (1.0 - ADAM_B2) * (g * g)
    m_hat = m / (1.0 - ADAM_B1 ** ADAM_STEP)
    v_hat = v / (1.0 - ADAM_B2 ** ADAM_STEP)
    delta = -ADAM_LR * (m_hat / (jnp.sqrt(v_hat) + ADAM_EPS) + ADAM_WD * w)
    return delta, m, v


def _reduce_adamw(parts, w, m, v, *, own, own_slot, name):
    n, R, C = parts.shape
    tr = _pick(R, (128, 64, 32, 16, 8))

    def body(_, own_ref, parts_ref, w_ref, m_ref, v_ref, g_ref, d_ref, nm_ref, nv_ref):
        g = own_ref[0].astype(F32)
        for k in range(n):
            g = g + parts_ref[k].astype(F32)
        g_ref[...] = g
        d_ref[...], nm_ref[...], nv_ref[...] = _adamw_math(w_ref[...], g, m_ref[...], v_ref[...])

    out = jax.ShapeDtypeStruct((R, C), F32)
    row = pl.BlockSpec((tr, C), lambda i, s: (i, 0))
    return pl.pallas_call(
        body, name=name, out_shape=(out, out, out, out),
        grid_spec=pltpu.PrefetchScalarGridSpec(
            num_scalar_prefetch=1, grid=(R // tr,),
            in_specs=[pl.BlockSpec((1, tr, C), lambda i, s: (s[0], i, 0)),
                      pl.BlockSpec((n, tr, C), lambda i, s: (0, i, 0)), row, row, row],
            out_specs=(row, row, row, row)),
        compiler_params=_cparams("parallel"),
    )(own_slot, own, parts, w, m, v)


SHARDED = (("w_in", (1024, 4096), 1), ("ssm_w_glu", (512, 512), 0), ("w_branch_attn", (512, 1024), 1),
           ("w_branch_ssm", (512, 1024), 1), ("w_out", (1024, 1024), 0), ("xa_wq", (1024, 1024), 0),
           ("xa_wk", (1024, 1024), 0), ("xa_wv", (1024, 1024), 0), ("xa_wo", (1024, 1024), 0),
           ("ffn_w_up", (1024, 5632), 1), ("ffn_conv_w", (3, 5632), 1), ("ffn_w_down", (2816, 1024), 0))
REPLICATED = (("norm_mix_pre", (1024,)), ("norm_mix_post", (1024,)), ("b_gate", (2048,)), ("ssm_a_re", (32, 64)),
              ("ssm_a_im", (32, 64)), ("ssm_log_dt", (32,)), ("ssm_b_re", (32, 64, 16)), ("ssm_b_im", (32, 64, 16)),
              ("ssm_c_re", (32, 16, 64)), ("ssm_c_im", (32, 16, 64)), ("ssm_d", (512,)), ("ssm_b_glu", (512,)),
              ("norm_xa_pre", (1024,)), ("norm_xa_post", (1024,)), ("norm_mem", (1024,)), ("norm_ffn_pre", (1024,)),
              ("norm_ffn_post", (1024,)), ("ffn_conv_b", (5632,)))
PARAM_ORDER = ("norm_mix_pre", "norm_mix_post", "w_in", "b_gate", "ssm_a_re", "ssm_a_im", "ssm_log_dt", "ssm_b_re",
               "ssm_b_im", "ssm_c_re", "ssm_c_im", "ssm_d", "ssm_w_glu", "ssm_b_glu", "w_branch_attn", "w_branch_ssm",
               "w_out", "norm_xa_pre", "norm_xa_post", "norm_mem", "xa_wq", "xa_wk", "xa_wv", "xa_wo", "norm_ffn_pre",
               "norm_ffn_post", "ffn_w_up", "ffn_conv_w", "ffn_conv_b", "ffn_w_down")
FF_LOCAL = 2 * D_FF // N_DEV
FF_LOCAL_PAD = 768
FF_PAD = (N_DEV // 2) * FF_LOCAL_PAD


def _local_shape(shape, axis):
    return tuple(s // N_DEV if a == axis else s for a, s in enumerate(shape))


def _pad_cols(a, width):
    return jnp.pad(a, [(0, 0)] * (a.ndim - 1) + [(0, width - a.shape[-1])])


def _blocks_to_cols(a8):
    return a8.transpose(1, 0, 2).reshape(a8.shape[1], N_DEV * a8.shape[2])


def _cols_to_blocks(a, cb):
    return a.reshape(a.shape[0], N_DEV, cb).transpose(1, 0, 2)


FF_PADDED = ("ffn_w_up", "ffn_conv_w")
LATE = tuple(n for n, _, _ in SHARDED if n != "w_in")
REDUCE_FFN = ("ffn_w_up", "ffn_conv_w", "ffn_w_down")
REDUCE_MID = ("xa_wo", "xa_wq", "xa_wk", "xa_wv", "w_out", "w_branch_attn", "w_branch_ssm", "ssm_w_glu")
SHARD_AXIS = {n: ax for n, _, ax in SHARDED}
FULL_SHAPE = {n: s for n, s, _ in SHARDED}


def _as_local(n, a):
    return _pad_cols(a, FF_LOCAL_PAD) if n in FF_PADDED else a


def _weights_from_wire(wire):
    full = {n: b.reshape((N_DEV,) + b.shape[2:]) for n, b in wire.items()}
    W = {n: a.reshape(FULL_SHAPE[n]) if SHARD_AXIS[n] == 0 else a for n, a in full.items()}
    for n in ("w_branch_attn", "w_branch_ssm", "ffn_conv_w"):
        W[n] = _blocks_to_cols(full[n])
    W["ffn_w_down"] = jnp.pad(W["ffn_w_down"].reshape(N_DEV // 2, FF_LOCAL, D_MODEL),
                              ((0, 0), (0, FF_LOCAL_PAD - FF_LOCAL), (0, 0))).reshape(FF_PAD, D_MODEL)
    return W


def _grad_blocks(n, g):
    if n in ("w_branch_attn", "w_branch_ssm"):
        g = _cols_to_blocks(g, D_MODEL // N_DEV)
    elif n == "ffn_conv_w":
        g = _cols_to_blocks(g, FF_LOCAL_PAD)
    elif n == "ffn_w_down":
        g = g.reshape(N_DEV // 2, FF_LOCAL_PAD, D_MODEL)[:, :FF_LOCAL]
    local = _local_shape(FULL_SHAPE[n], SHARD_AXIS[n])
    if n in FF_PADDED:
        local = local[:-1] + (FF_LOCAL_PAD,)
    return g.reshape((N_XY, 2) + local)


SMALL_SHAPE = {n: (1, s[0]) if len(s) == 1 else (s[0], math.prod(s[1:])) for n, s in REPLICATED}
SMALL_SHAPE["ffn_conv_b"] = (N_DEV, FF_LOCAL)
SMALL_EARLY = tuple(n for n, _ in REPLICATED if n != "norm_mix_pre")


def _adamw_replicated(parts, w, m, v, *, name):
    n = len(parts)

    def body(*refs):
        p_refs, w_refs, m_refs, v_refs = (refs[i * n:(i + 1) * n] for i in range(4))
        outs = refs[4 * n:]
        for k in range(n):
            g = p_refs[k][0, 0]
            for s in range(1, N_DEV):
                g = g + p_refs[k][s // 2, s % 2]
            d, nm, nv = _adamw_math(w_refs[k][...], g, m_refs[k][...], v_refs[k][...])
            for slot, val in enumerate((g, d, nm, nv)):
                outs[slot * n + k][...] = val

    vmem = pl.BlockSpec(memory_space=pltpu.VMEM)
    shapes = [jax.ShapeDtypeStruct(a.shape, F32) for a in w] * 4
    res = pl.pallas_call(
        body, name=name, out_shape=tuple(shapes), in_specs=[vmem] * (4 * n), out_specs=tuple([vmem] * (4 * n)),
        compiler_params=pltpu.CompilerParams(vmem_limit_bytes=VMEM_LIMIT),
    )(*parts, *w, *m, *v)
    return [list(res[i * n:(i + 1) * n]) for i in range(4)]


def kernel(x, mem, norm_mix_pre, norm_mix_post, w_in, b_gate, ssm_a_re, ssm_a_im, ssm_log_dt, ssm_b_re, ssm_b_im, ssm_c_re, ssm_c_im, ssm_d, ssm_w_glu, ssm_b_glu, w_branch_attn, w_branch_ssm, w_out, norm_xa_pre, norm_xa_post, norm_mem, xa_wq, xa_wk, xa_wv, xa_wo, norm_ffn_pre, norm_ffn_post, ffn_w_up, ffn_conv_w, ffn_conv_b, ffn_w_down, loss_target, m_norm_mix_pre, m_norm_mix_post, m_w_in, m_b_gate, m_ssm_a_re, m_ssm_a_im, m_ssm_log_dt, m_ssm_b_re, m_ssm_b_im, m_ssm_c_re, m_ssm_c_im, m_ssm_d, m_ssm_w_glu, m_ssm_b_glu, m_w_branch_attn, m_w_branch_ssm, m_w_out, m_norm_xa_pre, m_norm_xa_post, m_norm_mem, m_xa_wq, m_xa_wk, m_xa_wv, m_xa_wo, m_norm_ffn_pre, m_norm_ffn_post, m_ffn_w_up, m_ffn_conv_w, m_ffn_conv_b, m_ffn_w_down, v_norm_mix_pre, v_norm_mix_post, v_w_in, v_b_gate, v_ssm_a_re, v_ssm_a_im, v_ssm_log_dt, v_ssm_b_re, v_ssm_b_im, v_ssm_c_re, v_ssm_c_im, v_ssm_d, v_ssm_w_glu, v_ssm_b_glu, v_w_branch_attn, v_w_branch_ssm, v_w_out, v_norm_xa_pre, v_norm_xa_post, v_norm_mem, v_xa_wq, v_xa_wk, v_xa_wv, v_xa_wo, v_norm_ffn_pre, v_norm_ffn_post, v_ffn_w_up, v_ffn_conv_w, v_ffn_conv_b, v_ffn_w_down):
    args = dict(locals())
    w_loc = {n: args[n][0] for n in PARAM_ORDER}
    m_loc = {n: args["m_" + n][0] for n in PARAM_ORDER}
    v_loc = {n: args["v_" + n][0] for n in PARAM_ORDER}
    core_i = lax.axis_index("c")
    chip_i = 2 * lax.axis_index("x") + lax.axis_index("y")
    core = core_i.astype(jnp.int32).reshape(1)
    chip = chip_i.astype(jnp.int32).reshape(1)

    def in_place(a):
        buf = lax.empty((N_XY, 2) + a.shape, a.dtype)
        return lax.dynamic_update_slice(buf, a[None, None], (chip_i, core_i) + (0,) * a.ndim)

    as_wire = lambda n: in_place(_as_local(n, w_loc[n]).astype(F32 if n == "ffn_conv_w" else BF16))

    P = {}
    for n, shape in REPLICATED:
        P[n] = w_loc[n] if len(shape) > 1 or n == "ssm_log_dt" else w_loc[n].reshape(1, -1)
    P["ffn_conv_b"] = _pad_cols(w_loc["ffn_conv_b"].reshape(N_DEV, FF_LOCAL), FF_LOCAL_PAD).reshape(1, 2 * FF_PAD)

    loss, grad_x, small_parts, reduced = _local_step(x[0], mem[0], loss_target[0], as_wire("w_in"),
                                                     [as_wire(n) for n in LATE], P, core)
    loss = lax.psum(loss[0, 0], ("x", "y", "c"))

    big_out = {}
    for n, (own, parts) in reduced.items():
        res = _reduce_adamw(parts, _as_local(n, w_loc[n]), _as_local(n, m_loc[n]), _as_local(n, v_loc[n]),
                            own=own, own_slot=chip, name="adamw_" + n)
        big_out[n] = [r[:, :FF_LOCAL] if n in FF_PADDED else r for r in res]

    names = [n for n, _ in REPLICATED]
    as_small = lambda d: [d[n].reshape(SMALL_SHAPE[n]) for n in names]
    small_out = _adamw_replicated([small_parts[n] for n in names], as_small(w_loc), as_small(m_loc), as_small(v_loc),
                                  name="adamw_replicated")
    small_out = [dict(zip(names, res)) for res in small_out]

    outs = [loss, grad_x[None]]
    for k in range(4):
        for n in PARAM_ORDER:
            src = big_out[n][k] if n in big_out else small_out[k][n]
            outs.append(src.reshape(args[n].shape))
    return tuple(outs)
```
